```python
import math
import jax, jax.numpy as jnp
from jax import lax
import numpy as np

D_MODEL = 1024
BATCH = 16
SEQ = 2048
DEPTH = 1

ATTN_Q_HEADS = 8
ATTN_KV_HEADS = 2
ATTN_HEAD_DIM = 64
WINDOW = 128
REL_BUCKETS = 32
REL_MAX_DIST = 128
DN_HEADS = 4
DN_HEAD_DIM = 128
DN_CONV = 4
DN_CHUNK = 64
D_FF = 2816
FFN_CONV = 3
RMS_EPS = 1e-6
L2_EPS = 1e-6
N_MOD = 6
NEG_INF = -1e30

ATTN_Q_DIM = ATTN_Q_HEADS * ATTN_HEAD_DIM
ATTN_KV_DIM = ATTN_KV_HEADS * ATTN_HEAD_DIM
DN_DIM = DN_HEADS * DN_HEAD_DIM
IN_SPLIT_SIZES = (ATTN_Q_DIM, ATTN_KV_DIM, ATTN_KV_DIM, 3 * DN_DIM, DN_DIM, DN_HEADS, DN_HEADS, D_MODEL, D_MODEL)
IN_DIM = sum(IN_SPLIT_SIZES)

kernel_name = 'hybrid_swa_gdn_convffn_block'


def rms_norm(x, w):
    xf = x.astype(jnp.float32)
    y = xf * lax.rsqrt(jnp.mean(xf * xf, axis=-1, keepdims=True) + RMS_EPS)
    return (y * w.astype(jnp.float32)).astype(x.dtype)


def l2_normalize(x):
    return x * lax.rsqrt(jnp.sum(x * x, axis=-1, keepdims=True) + L2_EPS)


def causal_depthwise_conv(x, w):
    k, ch = w.shape
    return lax.conv_general_dilated(x, w[:, None, :].astype(x.dtype), window_strides=(1,), padding=[(k - 1, 0)], dimension_numbers=('NWC', 'WIO', 'NWC'), feature_group_count=ch)


def t5_causal_bucket(dist):
    dist = jnp.maximum(dist, 0)
    max_exact = REL_BUCKETS // 2
    scaled = jnp.log(jnp.maximum(dist, 1).astype(jnp.float32) / max_exact) / math.log(REL_MAX_DIST / max_exact)
    large = max_exact + (scaled * (REL_BUCKETS - max_exact)).astype(jnp.int32)
    large = jnp.minimum(large, REL_BUCKETS - 1)
    return jnp.where(dist < max_exact, dist, large)


def sliding_window_gqa(q, k, v, sinks, rel_bias):
    b, s, _, hd = q.shape
    grp = ATTN_Q_HEADS // ATTN_KV_HEADS
    nb = s // WINDOW
    qb = q.reshape(b, nb, WINDOW, ATTN_KV_HEADS, grp, hd)

    def band(t):
        tb = t.reshape(b, nb, WINDOW, ATTN_KV_HEADS, hd)
        prev = jnp.concatenate([jnp.zeros_like(tb[:, :1]), tb[:, :-1]], axis=1)
        return jnp.concatenate([prev, tb], axis=2)

    kb, vb = band(k), band(v)
    scores = jnp.einsum('bnqhgd,bnkhd->bnhgqk', qb, kb, preferred_element_type=jnp.float32) * (hd ** -0.5)
    qi = jnp.arange(WINDOW)[:, None]
    kj = jnp.arange(2 * WINDOW)[None, :]
    dist = WINDOW + qi - kj
    bias = rel_bias[t5_causal_bucket(dist)]
    bias = jnp.transpose(bias, (2, 0, 1)).reshape(ATTN_KV_HEADS, grp, WINDOW, 2 * WINDOW).astype(jnp.float32)
    in_band = (dist >= 0) & (dist < WINDOW)
    key_pos = jnp.arange(nb)[:, None, None] * WINDOW - WINDOW + kj[None]
    mask = in_band[None] & (key_pos >= 0)
    scores = jnp.where(mask[None, :, None, None], scores + bias, NEG_INF)
    sink = sinks.astype(jnp.float32).reshape(1, 1, ATTN_KV_HEADS, grp, 1, 1)
    m = jnp.maximum(jnp.max(scores, axis=-1, keepdims=True), sink)
    p = jnp.exp(scores - m)
    probs = p / (jnp.sum(p, axis=-1, keepdims=True) + jnp.exp(sink - m))
    out = jnp.einsum('bnhgqk,bnkhd->bnqhgd', probs.astype(v.dtype), vb)
    return out.reshape(b, s, ATTN_Q_DIM)


def chunk_gated_delta_rule(q, k, v, g, beta):
    b, s, nh, dk = q.shape
    dv = v.shape[-1]
    n = s // DN_CHUNK

    def chunks(t):
        return jnp.swapaxes(t, 1, 2).reshape(b, nh, n, DN_CHUNK, t.shape[-1])

    q = chunks(q) * (dk ** -0.5)
    k = chunks(k)
    v = chunks(v)
    gc = jnp.cumsum(jnp.swapaxes(g, 1, 2).reshape(b, nh, n, DN_CHUNK), axis=-1)
    bt = jnp.swapaxes(beta, 1, 2).reshape(b, nh, n, DN_CHUNK)[..., None]
    k_beta = k * bt
    v_beta = v * bt
    incl = jnp.tril(jnp.ones((DN_CHUNK, DN_CHUNK), dtype=bool))
    strict = jnp.tril(jnp.ones((DN_CHUNK, DN_CHUNK), dtype=bool), k=-1)
    diff = gc[..., :, None] - gc[..., None, :]
    decay = jnp.where(incl, jnp.exp(jnp.where(incl, diff, 0.0)), 0.0)
    eg = jnp.exp(gc)[..., None]
    lower = jnp.where(strict, jnp.einsum('bhncd,bhnmd->bhncm', k_beta, k) * decay, 0.0)
    rhs = jnp.concatenate([v_beta, k_beta * eg], axis=-1)
    eye = jnp.eye(DN_CHUNK, dtype=jnp.float32)
    sol = lax.linalg.triangular_solve(eye + lower, rhs, left_side=True, lower=True)
    u, w = sol[..., :dv], sol[..., dv:]
    intra = jnp.where(incl, jnp.einsum('bhncd,bhnmd->bhncm', q, k) * decay, 0.0)
    q_dec = q * eg
    k_tail = k * jnp.exp(gc[..., -1:] - gc)[..., None]
    g_last = jnp.exp(gc[..., -1])

    def step(state, inp):
        q_i, k_i, u_i, w_i, a_i, gl_i = inp
        v_new = u_i - jnp.einsum('bhcd,bhde->bhce', w_i, state)
        o_i = jnp.einsum('bhcd,bhde->bhce', q_i, state) + jnp.einsum('bhcm,bhme->bhce', a_i, v_new)
        state = state * gl_i[..., None, None] + jnp.einsum('bhcd,bhce->bhde', k_i, v_new)
        return state, o_i

    xs = tuple(jnp.moveaxis(t, 2, 0) for t in (q_dec, k_tail, u, w, intra, g_last))
    state0 = jnp.zeros((b, nh, dk, dv), jnp.float32)
    _, o = lax.scan(step, state0, xs)
    o = jnp.moveaxis(o, 0, 2).reshape(b, nh, s, dv)
    return jnp.swapaxes(o, 1, 2)


def gated_deltanet(qkv, beta_logits, a_logits, z, conv_w, a_log, dt_bias, norm_w):
    b, s, _ = qkv.shape
    qkv = jax.nn.silu(causal_depthwise_conv(qkv, conv_w)).astype(jnp.float32)
    q, k, v = jnp.split(qkv, 3, axis=-1)
    q = l2_normalize(q.reshape(b, s, DN_HEADS, DN_HEAD_DIM))
    k = l2_normalize(k.reshape(b, s, DN_HEADS, DN_HEAD_DIM))
    v = v.reshape(b, s, DN_HEADS, DN_HEAD_DIM)
    beta = jax.nn.sigmoid(beta_logits.astype(jnp.float32))
    g = -jnp.exp(a_log.astype(jnp.float32)) * jax.nn.softplus(a_logits.astype(jnp.float32) + dt_bias.astype(jnp.float32))
    o = chunk_gated_delta_rule(q, k, v, g, beta)
    zf = z.astype(jnp.float32).reshape(b, s, DN_HEADS, DN_HEAD_DIM)
    o = rms_norm(o, norm_w) * jax.nn.silu(zf)
    return o.reshape(b, s, DN_DIM).astype(z.dtype)


def hybrid_mixer(h, w_in, dn_conv_w, dn_a_log, dn_dt_bias, dn_norm_w, attn_sinks, rel_bias, w_attn_branch, w_dn_branch, w_out):
    b, s, _ = h.shape
    proj = h @ w_in
    splits = np.cumsum(IN_SPLIT_SIZES)[:-1].tolist()
    aq, ak, av, dqkv, dz, dbeta, da, gate_a, gate_d = jnp.split(proj, splits, axis=-1)
    y_attn = sliding_window_gqa(aq.reshape(b, s, ATTN_Q_HEADS, ATTN_HEAD_DIM), ak.reshape(b, s, ATTN_KV_HEADS, ATTN_HEAD_DIM), av.reshape(b, s, ATTN_KV_HEADS, ATTN_HEAD_DIM), attn_sinks, rel_bias)
    y_dn = gated_deltanet(dqkv, dbeta, da, dz, dn_conv_w, dn_a_log, dn_dt_bias, dn_norm_w)
    merged = jax.nn.sigmoid(gate_a) * (y_attn @ w_attn_branch) + jax.nn.sigmoid(gate_d) * (y_dn @ w_dn_branch)
    return merged @ w_out


def conv_ffn(h, w_up, conv_w, w_down):
    u = causal_depthwise_conv(h @ w_up, conv_w)
    gate, val = jnp.split(u, 2, axis=-1)
    return (jax.nn.gelu(gate, approximate=True) * val) @ w_down


def _fwd_setup_inputs(seed: int = 0) -> dict:
    key = jax.random.key(seed)
    ks = jax.random.split(key, 22)
    f32 = jnp.float32
    L = DEPTH

    def normal(k, shape, scale):
        return jax.random.normal(k, shape, f32) * scale

    def gain(k, shape):
        return 1.0 + 0.05 * jax.random.normal(k, shape, f32)

    dt = jnp.exp(jax.random.uniform(ks[11], (L, DN_HEADS), f32, math.log(1e-3), math.log(1e-1)))
    return {
        'x': normal(ks[0], (BATCH, SEQ, D_MODEL), 1.0),
        'c': normal(ks[1], (BATCH, D_MODEL), 1.0),
        'ada_w': normal(ks[2], (L, D_MODEL, N_MOD * D_MODEL), 0.5 * D_MODEL ** -0.5),
        'ada_b': normal(ks[3], (L, N_MOD * D_MODEL), 0.02),
        'norm_mix_pre': gain(ks[4], (L, D_MODEL)),
        'norm_mix_post': gain(ks[5], (L, D_MODEL)),
        'norm_ffn_pre': gain(ks[6], (L, D_MODEL)),
        'norm_ffn_post': gain(ks[7], (L, D_MODEL)),
        'w_in': normal(ks[8], (L, D_MODEL, IN_DIM), D_MODEL ** -0.5),
        'dn_conv_w': normal(ks[9], (L, DN_CONV, 3 * DN_DIM), DN_CONV ** -0.5),
        'dn_a_log': jnp.log(jax.random.uniform(ks[10], (L, DN_HEADS), f32, 1.0, 16.0)),
        'dn_dt_bias': dt + jnp.log(-jnp.expm1(-dt)),
        'dn_norm_w': gain(ks[12], (L, DN_HEAD_DIM)),
        'attn_sinks': normal(ks[13], (L, ATTN_Q_HEADS), 1.0),
        'rel_bias': normal(ks[14], (REL_BUCKETS, ATTN_Q_HEADS), 0.5),
        'w_attn_branch': normal(ks[15], (L, ATTN_Q_DIM, D_MODEL), ATTN_Q_DIM ** -0.5),
        'w_dn_branch': normal(ks[16], (L, DN_DIM, D_MODEL), DN_DIM ** -0.5),
        'w_out': normal(ks[17], (L, D_MODEL, D_MODEL), D_MODEL ** -0.5),
        'ffn_w_up': normal(ks[18], (L, D_MODEL, 2 * D_FF), D_MODEL ** -0.5),
        'ffn_conv_w': normal(ks[19], (L, FFN_CONV, 2 * D_FF), FFN_CONV ** -0.5),
        'ffn_w_down': normal(ks[20], (L, D_FF, D_MODEL), D_FF ** -0.5),
    }


def _fwd_reference(x, c, ada_w, ada_b, norm_mix_pre, norm_mix_post, norm_ffn_pre, norm_ffn_post, w_in, dn_conv_w, dn_a_log, dn_dt_bias, dn_norm_w, attn_sinks, rel_bias, w_attn_branch, w_dn_branch, w_out, ffn_w_up, ffn_conv_w, ffn_w_down):
    h = x
    c_act = jax.nn.silu(c)
    for l in range(DEPTH):
        mod = c_act @ ada_w[l] + ada_b[l]
        sh1, sc1, g1, sh2, sc2, g2 = [m[:, None, :] for m in jnp.split(mod, N_MOD, axis=-1)]
        u = rms_norm(h, norm_mix_pre[l]) * (1.0 + sc1) + sh1
        y = hybrid_mixer(u, w_in[l], dn_conv_w[l], dn_a_log[l], dn_dt_bias[l], dn_norm_w[l], attn_sinks[l], rel_bias, w_attn_branch[l], w_dn_branch[l], w_out[l])
        h = h + g1 * rms_norm(y, norm_mix_post[l])
        u = rms_norm(h, norm_ffn_pre[l]) * (1.0 + sc2) + sh2
        y = conv_ffn(u, ffn_w_up[l], ffn_conv_w[l], ffn_w_down[l])
        h = h + g2 * rms_norm(y, norm_ffn_post[l])
    return h


import jax as _jax
import jax.numpy as _jnp

TWIN_FORMAT = 'train_step'
FWD_PARAMS = ['x', 'c', 'ada_w', 'ada_b', 'norm_mix_pre', 'norm_mix_post', 'norm_ffn_pre', 'norm_ffn_post', 'w_in', 'dn_conv_w', 'dn_a_log', 'dn_dt_bias', 'dn_norm_w', 'attn_sinks', 'rel_bias', 'w_attn_branch', 'w_dn_branch', 'w_out', 'ffn_w_up', 'ffn_conv_w', 'ffn_w_down']
TWIN_WEIGHTS = ['ada_w', 'ada_b', 'norm_mix_pre', 'norm_mix_post', 'norm_ffn_pre', 'norm_ffn_post', 'w_in', 'dn_conv_w', 'dn_a_log', 'dn_dt_bias', 'dn_norm_w', 'attn_sinks', 'rel_bias', 'w_attn_branch', 'w_dn_branch', 'w_out', 'ffn_w_up', 'ffn_conv_w', 'ffn_w_down']
TWIN_DIFF_INPUT = 'x'
TWIN_INPUTS = ['x', 'c', 'ada_w', 'ada_b', 'norm_mix_pre', 'norm_mix_post', 'norm_ffn_pre', 'norm_ffn_post', 'w_in', 'dn_conv_w', 'dn_a_log', 'dn_dt_bias', 'dn_norm_w', 'attn_sinks', 'rel_bias', 'w_attn_branch', 'w_dn_branch', 'w_out', 'ffn_w_up', 'ffn_conv_w', 'ffn_w_down', 'loss_target', 'm_ada_w', 'm_ada_b', 'm_norm_mix_pre', 'm_norm_mix_post', 'm_norm_ffn_pre', 'm_norm_ffn_post', 'm_w_in', 'm_dn_conv_w', 'm_dn_a_log', 'm_dn_dt_bias', 'm_dn_norm_w', 'm_attn_sinks', 'm_rel_bias', 'm_w_attn_branch', 'm_w_dn_branch', 'm_w_out', 'm_ffn_w_up', 'm_ffn_conv_w', 'm_ffn_w_down', 'v_ada_w', 'v_ada_b', 'v_norm_mix_pre', 'v_norm_mix_post', 'v_norm_ffn_pre', 'v_norm_ffn_post', 'v_w_in', 'v_dn_conv_w', 'v_dn_a_log', 'v_dn_dt_bias', 'v_dn_norm_w', 'v_attn_sinks', 'v_rel_bias', 'v_w_attn_branch', 'v_w_dn_branch', 'v_w_out', 'v_ffn_w_up', 'v_ffn_conv_w', 'v_ffn_w_down']
TWIN_OUTPUTS = ['loss', 'grad_x', 'grad_ada_w', 'grad_ada_b', 'grad_norm_mix_pre', 'grad_norm_mix_post', 'grad_norm_ffn_pre', 'grad_norm_ffn_post', 'grad_w_in', 'grad_dn_conv_w', 'grad_dn_a_log', 'grad_dn_dt_bias', 'grad_dn_norm_w', 'grad_attn_sinks', 'grad_rel_bias', 'grad_w_attn_branch', 'grad_w_dn_branch', 'grad_w_out', 'grad_ffn_w_up', 'grad_ffn_conv_w', 'grad_ffn_w_down', 'delta_ada_w', 'delta_ada_b', 'delta_norm_mix_pre', 'delta_norm_mix_post', 'delta_norm_ffn_pre', 'delta_norm_ffn_post', 'delta_w_in', 'delta_dn_conv_w', 'delta_dn_a_log', 'delta_dn_dt_bias', 'delta_dn_norm_w', 'delta_attn_sinks', 'delta_rel_bias', 'delta_w_attn_branch', 'delta_w_dn_branch', 'delta_w_out', 'delta_ffn_w_up', 'delta_ffn_conv_w', 'delta_ffn_w_down', 'new_m_ada_w', 'new_m_ada_b', 'new_m_norm_mix_pre', 'new_m_norm_mix_post', 'new_m_norm_ffn_pre', 'new_m_norm_ffn_post', 'new_m_w_in', 'new_m_dn_conv_w', 'new_m_dn_a_log', 'new_m_dn_dt_bias', 'new_m_dn_norm_w', 'new_m_attn_sinks', 'new_m_rel_bias', 'new_m_w_attn_branch', 'new_m_w_dn_branch', 'new_m_w_out', 'new_m_ffn_w_up', 'new_m_ffn_conv_w', 'new_m_ffn_w_down', 'new_v_ada_w', 'new_v_ada_b', 'new_v_norm_mix_pre', 'new_v_norm_mix_post', 'new_v_norm_ffn_pre', 'new_v_norm_ffn_post', 'new_v_w_in', 'new_v_dn_conv_w', 'new_v_dn_a_log', 'new_v_dn_dt_bias', 'new_v_dn_norm_w', 'new_v_attn_sinks', 'new_v_rel_bias', 'new_v_w_attn_branch', 'new_v_w_dn_branch', 'new_v_w_out', 'new_v_ffn_w_up', 'new_v_ffn_conv_w', 'new_v_ffn_w_down']
TWIN_LEAF_KINDS = {'loss': 'loss', 'grad_x': 'grad_x', 'grad_ada_w': 'grad_w', 'grad_ada_b': 'grad_w', 'grad_norm_mix_pre': 'grad_w', 'grad_norm_mix_post': 'grad_w', 'grad_norm_ffn_pre': 'grad_w', 'grad_norm_ffn_post': 'grad_w', 'grad_w_in': 'grad_w', 'grad_dn_conv_w': 'grad_w', 'grad_dn_a_log': 'grad_w', 'grad_dn_dt_bias': 'grad_w', 'grad_dn_norm_w': 'grad_w', 'grad_attn_sinks': 'grad_w', 'grad_rel_bias': 'grad_w', 'grad_w_attn_branch': 'grad_w', 'grad_w_dn_branch': 'grad_w', 'grad_w_out': 'grad_w', 'grad_ffn_w_up': 'grad_w', 'grad_ffn_conv_w': 'grad_w', 'grad_ffn_w_down': 'grad_w', 'delta_ada_w': 'delta_w', 'delta_ada_b': 'delta_w', 'delta_norm_mix_pre': 'delta_w', 'delta_norm_mix_post': 'delta_w', 'delta_norm_ffn_pre': 'delta_w', 'delta_norm_ffn_post': 'delta_w', 'delta_w_in': 'delta_w', 'delta_dn_conv_w': 'delta_w', 'delta_dn_a_log': 'delta_w', 'delta_dn_dt_bias': 'delta_w', 'delta_dn_norm_w': 'delta_w', 'delta_attn_sinks': 'delta_w', 'delta_rel_bias': 'delta_w', 'delta_w_attn_branch': 'delta_w', 'delta_w_dn_branch': 'delta_w', 'delta_w_out': 'delta_w', 'delta_ffn_w_up': 'delta_w', 'delta_ffn_conv_w': 'delta_w', 'delta_ffn_w_down': 'delta_w', 'new_m_ada_w': 'new_m', 'new_m_ada_b': 'new_m', 'new_m_norm_mix_pre': 'new_m', 'new_m_norm_mix_post': 'new_m', 'new_m_norm_ffn_pre': 'new_m', 'new_m_norm_ffn_post': 'new_m', 'new_m_w_in': 'new_m', 'new_m_dn_conv_w': 'new_m', 'new_m_dn_a_log': 'new_m', 'new_m_dn_dt_bias': 'new_m', 'new_m_dn_norm_w': 'new_m', 'new_m_attn_sinks': 'new_m', 'new_m_rel_bias': 'new_m', 'new_m_w_attn_branch': 'new_m', 'new_m_w_dn_branch': 'new_m', 'new_m_w_out': 'new_m', 'new_m_ffn_w_up': 'new_m', 'new_m_ffn_conv_w': 'new_m', 'new_m_ffn_w_down': 'new_m', 'new_v_ada_w': 'new_v', 'new_v_ada_b': 'new_v', 'new_v_norm_mix_pre': 'new_v', 'new_v_norm_mix_post': 'new_v', 'new_v_norm_ffn_pre': 'new_v', 'new_v_norm_ffn_post': 'new_v', 'new_v_w_in': 'new_v', 'new_v_dn_conv_w': 'new_v', 'new_v_dn_a_log': 'new_v', 'new_v_dn_dt_bias': 'new_v', 'new_v_dn_norm_w': 'new_v', 'new_v_attn_sinks': 'new_v', 'new_v_rel_bias': 'new_v', 'new_v_w_attn_branch': 'new_v', 'new_v_w_dn_branch': 'new_v', 'new_v_w_out': 'new_v', 'new_v_ffn_w_up': 'new_v', 'new_v_ffn_conv_w': 'new_v', 'new_v_ffn_w_down': 'new_v'}


def _forward(args):
    return _fwd_reference(*[args[k] for k in FWD_PARAMS])


def _output_shape():
    out = _jax.eval_shape(lambda: _forward(_fwd_setup_inputs(0)))
    return out.shape, out.dtype

N_MICROBATCH = 1
ADAM_LR = 0.001
ADAM_B1 = 0.9
ADAM_B2 = 0.999
ADAM_EPS = 1e-08
ADAM_WD = 0.01
ADAM_STEP = 10
PER_EXAMPLE_BATCH_AXIS = {'x': 0, 'c': 0, 'loss_target': 0}
SHARED_INPUTS = []
_WEIGHT_DTYPES = {'ada_w': _jnp.float32, 'ada_b': _jnp.float32, 'norm_mix_pre': _jnp.float32, 'norm_mix_post': _jnp.float32, 'norm_ffn_pre': _jnp.float32, 'norm_ffn_post': _jnp.float32, 'w_in': _jnp.float32, 'dn_conv_w': _jnp.float32, 'dn_a_log': _jnp.float32, 'dn_dt_bias': _jnp.float32, 'dn_norm_w': _jnp.float32, 'attn_sinks': _jnp.float32, 'rel_bias': _jnp.float32, 'w_attn_branch': _jnp.float32, 'w_dn_branch': _jnp.float32, 'w_out': _jnp.float32, 'ffn_w_up': _jnp.float32, 'ffn_conv_w': _jnp.float32, 'ffn_w_down': _jnp.float32}
MOMENT_SCALE = {'ada_w': 1.783465e+00, 'ada_b': 3.387329e+00, 'norm_mix_pre': 1.708033e-01, 'norm_mix_post': 3.782729e+00, 'norm_ffn_pre': 1.249307e-01, 'norm_ffn_post': 3.767581e+00, 'w_in': 1.399369e-01, 'dn_conv_w': 1.313807e-01, 'dn_a_log': 6.601497e-01, 'dn_dt_bias': 5.373962e-01, 'dn_norm_w': 5.803096e-01, 'attn_sinks': 4.145423e-02, 'rel_bias': 4.706788e-02, 'w_attn_branch': 2.942577e-01, 'w_dn_branch': 1.892448e-01, 'w_out': 3.472454e-01, 'ffn_w_up': 6.157185e-02, 'ffn_conv_w': 6.694005e-02, 'ffn_w_down': 1.233290e-01}


def _to_microbatches(a, axis):
    t = _jnp.moveaxis(a, axis, 0)
    t = t.reshape((N_MICROBATCH, t.shape[0] // N_MICROBATCH) + t.shape[1:])
    return _jnp.moveaxis(t, 1, axis + 1)


def setup_inputs(seed: int = 0) -> dict:
    inp = _fwd_setup_inputs(seed)
    key = _jax.random.fold_in(_jax.random.key(seed), 7919)
    shape, _ = _output_shape()
    out = dict(inp)
    out["loss_target"] = _jax.random.normal(_jax.random.fold_in(key, 0), shape, _jnp.float32)
    for i, name in enumerate(TWIN_WEIGHTS):
        w = inp[name].astype(_jnp.float32)
        if MOMENT_SCALE is None:
            s = _jnp.sqrt(_jnp.mean(_jnp.square(w)) + 1e-30)
        else:
            s = MOMENT_SCALE[name]
        km, kv = _jax.random.split(_jax.random.fold_in(key, i + 1))
        out[name] = w
        out["m_" + name] = s * _jax.random.normal(km, w.shape, _jnp.float32)
        out["v_" + name] = (s * s) * _jax.random.uniform(kv, w.shape, _jnp.float32, 0.5, 1.5)
    if N_MICROBATCH > 1:
        for name, axis in PER_EXAMPLE_BATCH_AXIS.items():
            out[name] = _to_microbatches(out[name], axis)
    return {'x': out['x'], 'c': out['c'], 'ada_w': out['ada_w'], 'ada_b': out['ada_b'], 'norm_mix_pre': out['norm_mix_pre'], 'norm_mix_post': out['norm_mix_post'], 'norm_ffn_pre': out['norm_ffn_pre'], 'norm_ffn_post': out['norm_ffn_post'], 'w_in': out['w_in'], 'dn_conv_w': out['dn_conv_w'], 'dn_a_log': out['dn_a_log'], 'dn_dt_bias': out['dn_dt_bias'], 'dn_norm_w': out['dn_norm_w'], 'attn_sinks': out['attn_sinks'], 'rel_bias': out['rel_bias'], 'w_attn_branch': out['w_attn_branch'], 'w_dn_branch': out['w_dn_branch'], 'w_out': out['w_out'], 'ffn_w_up': out['ffn_w_up'], 'ffn_conv_w': out['ffn_conv_w'], 'ffn_w_down': out['ffn_w_down'], 'loss_target': out['loss_target'], 'm_ada_w': out['m_ada_w'], 'm_ada_b': out['m_ada_b'], 'm_norm_mix_pre': out['m_norm_mix_pre'], 'm_norm_mix_post': out['m_norm_mix_post'], 'm_norm_ffn_pre': out['m_norm_ffn_pre'], 'm_norm_ffn_post': out['m_norm_ffn_post'], 'm_w_in': out['m_w_in'], 'm_dn_conv_w': out['m_dn_conv_w'], 'm_dn_a_log': out['m_dn_a_log'], 'm_dn_dt_bias': out['m_dn_dt_bias'], 'm_dn_norm_w': out['m_dn_norm_w'], 'm_attn_sinks': out['m_attn_sinks'], 'm_rel_bias': out['m_rel_bias'], 'm_w_attn_branch': out['m_w_attn_branch'], 'm_w_dn_branch': out['m_w_dn_branch'], 'm_w_out': out['m_w_out'], 'm_ffn_w_up': out['m_ffn_w_up'], 'm_ffn_conv_w': out['m_ffn_conv_w'], 'm_ffn_w_down': out['m_ffn_w_down'], 'v_ada_w': out['v_ada_w'], 'v_ada_b': out['v_ada_b'], 'v_norm_mix_pre': out['v_norm_mix_pre'], 'v_norm_mix_post': out['v_norm_mix_post'], 'v_norm_ffn_pre': out['v_norm_ffn_pre'], 'v_norm_ffn_post': out['v_norm_ffn_post'], 'v_w_in': out['v_w_in'], 'v_dn_conv_w': out['v_dn_conv_w'], 'v_dn_a_log': out['v_dn_a_log'], 'v_dn_dt_bias': out['v_dn_dt_bias'], 'v_dn_norm_w': out['v_dn_norm_w'], 'v_attn_sinks': out['v_attn_sinks'], 'v_rel_bias': out['v_rel_bias'], 'v_w_attn_branch': out['v_w_attn_branch'], 'v_w_dn_branch': out['v_w_dn_branch'], 'v_w_out': out['v_w_out'], 'v_ffn_w_up': out['v_ffn_w_up'], 'v_ffn_conv_w': out['v_ffn_conv_w'], 'v_ffn_w_down': out['v_ffn_w_down']}


def _loss(weights, diff, rest, loss_target):
    with _jax.named_scope("forward"):
        args = {**rest, TWIN_DIFF_INPUT: diff, **{k: w.astype(_WEIGHT_DTYPES[k]) for k, w in weights.items()}}
        y = _forward(args)
    with _jax.named_scope("loss_head"):
        err = _jnp.square(y.astype(_jnp.float32) - loss_target)
        return 0.5 * _jnp.sum(_jnp.mean(err, axis=-1)) if err.ndim else 0.5 * err


def _adamw(w, g, m, v):
    m = ADAM_B1 * m + (1.0 - ADAM_B1) * g
    v = ADAM_B2 * v + (1.0 - ADAM_B2) * _jnp.square(g)
    m_hat = m / (1.0 - ADAM_B1 ** ADAM_STEP)
    v_hat = v / (1.0 - ADAM_B2 ** ADAM_STEP)
    delta = -ADAM_LR * (m_hat / (_jnp.sqrt(v_hat) + ADAM_EPS) + ADAM_WD * w)
    return delta, m, v


def reference(x, c, ada_w, ada_b, norm_mix_pre, norm_mix_post, norm_ffn_pre, norm_ffn_post, w_in, dn_conv_w, dn_a_log, dn_dt_bias, dn_norm_w, attn_sinks, rel_bias, w_attn_branch, w_dn_branch, w_out, ffn_w_up, ffn_conv_w, ffn_w_down, loss_target, m_ada_w, m_ada_b, m_norm_mix_pre, m_norm_mix_post, m_norm_ffn_pre, m_norm_ffn_post, m_w_in, m_dn_conv_w, m_dn_a_log, m_dn_dt_bias, m_dn_norm_w, m_attn_sinks, m_rel_bias, m_w_attn_branch, m_w_dn_branch, m_w_out, m_ffn_w_up, m_ffn_conv_w, m_ffn_w_down, v_ada_w, v_ada_b, v_norm_mix_pre, v_norm_mix_post, v_norm_ffn_pre, v_norm_ffn_post, v_w_in, v_dn_conv_w, v_dn_a_log, v_dn_dt_bias, v_dn_norm_w, v_attn_sinks, v_rel_bias, v_w_attn_branch, v_w_dn_branch, v_w_out, v_ffn_w_up, v_ffn_conv_w, v_ffn_w_down):
    given = dict(x=x, c=c, ada_w=ada_w, ada_b=ada_b, norm_mix_pre=norm_mix_pre, norm_mix_post=norm_mix_post, norm_ffn_pre=norm_ffn_pre, norm_ffn_post=norm_ffn_post, w_in=w_in, dn_conv_w=dn_conv_w, dn_a_log=dn_a_log, dn_dt_bias=dn_dt_bias, dn_norm_w=dn_norm_w, attn_sinks=attn_sinks, rel_bias=rel_bias, w_attn_branch=w_attn_branch, w_dn_branch=w_dn_branch, w_out=w_out, ffn_w_up=ffn_w_up, ffn_conv_w=ffn_conv_w, ffn_w_down=ffn_w_down, loss_target=loss_target, m_ada_w=m_ada_w, m_ada_b=m_ada_b, m_norm_mix_pre=m_norm_mix_pre, m_norm_mix_post=m_norm_mix_post, m_norm_ffn_pre=m_norm_ffn_pre, m_norm_ffn_post=m_norm_ffn_post, m_w_in=m_w_in, m_dn_conv_w=m_dn_conv_w, m_dn_a_log=m_dn_a_log, m_dn_dt_bias=m_dn_dt_bias, m_dn_norm_w=m_dn_norm_w, m_attn_sinks=m_attn_sinks, m_rel_bias=m_rel_bias, m_w_attn_branch=m_w_attn_branch, m_w_dn_branch=m_w_dn_branch, m_w_out=m_w_out, m_ffn_w_up=m_ffn_w_up, m_ffn_conv_w=m_ffn_conv_w, m_ffn_w_down=m_ffn_w_down, v_ada_w=v_ada_w, v_ada_b=v_ada_b, v_norm_mix_pre=v_norm_mix_pre, v_norm_mix_post=v_norm_mix_post, v_norm_ffn_pre=v_norm_ffn_pre, v_norm_ffn_post=v_norm_ffn_post, v_w_in=v_w_in, v_dn_conv_w=v_dn_conv_w, v_dn_a_log=v_dn_a_log, v_dn_dt_bias=v_dn_dt_bias, v_dn_norm_w=v_dn_norm_w, v_attn_sinks=v_attn_sinks, v_rel_bias=v_rel_bias, v_w_attn_branch=v_w_attn_branch, v_w_dn_branch=v_w_dn_branch, v_w_out=v_w_out, v_ffn_w_up=v_ffn_w_up, v_ffn_conv_w=v_ffn_conv_w, v_ffn_w_down=v_ffn_w_down)
    weights = {n: given[n] for n in TWIN_WEIGHTS}
    shared = {n: given[n] for n in SHARED_INPUTS}
    per_example = {n: given[n] for n in ['x', 'c']}
    grad_fn = _jax.value_and_grad(_loss, argnums=(0, 1))

    def one_microbatch(ex, loss_target):
        ex = dict(ex)
        diff = ex.pop(TWIN_DIFF_INPUT)
        return grad_fn(weights, diff, {**shared, **ex}, loss_target)

    if N_MICROBATCH == 1:
        loss, (grad_w, grad_x) = one_microbatch(per_example, given["loss_target"])
    else:
        def body(carry, xs):
            loss_sum, grad_sum = carry
            l_k, (gw_k, gx_k) = one_microbatch(xs[0], xs[1])
            with _jax.named_scope("update"):
                return (loss_sum + l_k, _jax.tree.map(_jnp.add, grad_sum, gw_k)), gx_k

        init = (_jnp.zeros((), _jnp.float32), _jax.tree.map(_jnp.zeros_like, weights))
        (loss, grad_w), grad_x = _jax.lax.scan(body, init, (per_example, given["loss_target"]))
    with _jax.named_scope("update"):
        delta_w, new_m, new_v = {}, {}, {}
        for n in TWIN_WEIGHTS:
            delta_w[n], new_m[n], new_v[n] = _adamw(weights[n], grad_w[n], given["m_" + n], given["v_" + n])
    return (loss, grad_x, *[grad_w[n] for n in TWIN_WEIGHTS], *[delta_w[n] for n in TWIN_WEIGHTS],
            *[new_m[n] for n in TWIN_WEIGHTS], *[new_v[n] for n in TWIN_WEIGHTS])
```

```python
import functools
import math

import numpy as np
import jax
import jax.numpy as jnp
from jax import lax
from jax.experimental import pallas as pl
from jax.experimental.pallas import tpu as pltpu

F32 = jnp.float32
BF16 = jnp.bfloat16
HI = lax.Precision.HIGHEST
MESH = pl.DeviceIdType.MESH

D_MODEL = 1024
N_MOD = 6
AQ_HEADS, AKV_HEADS, A_HD, WINDOW = 8, 2, 64, 128
REL_BUCKETS, REL_MAX_DIST = 32, 128
DN_HEADS, DN_HD, DN_CONV, DN_CHUNK = 4, 128, 4, 64
D_FF, FFN_CONV = 2816, 3
RMS_EPS, L2_EPS, NEG_INF = 1e-6, 1e-6, -1e30
AQ, AKV, DN = AQ_HEADS * A_HD, AKV_HEADS * A_HD, DN_HEADS * DN_HD
IN_DIM = AQ + 2 * AKV + 3 * DN + DN + 2 * DN_HEADS + 2 * D_MODEL
C_DQKV, C_Q, C_DZ, C_GA, C_GD, C_K, C_V, C_BD = 0, 1536, 2048, 2560, 3584, 4608, 4736, 4864
IN_PAD = 4992
LANES = 128
N_CHIPS = 4

ADAM_LR, ADAM_B1, ADAM_B2, ADAM_EPS, ADAM_WD, ADAM_STEP = 0.001, 0.9, 0.999, 1e-08, 0.01, 10


def _pick(n, cap):
    best = None
    for t in range(LANES, cap + 1, LANES):
        if n % t == 0:
            best = t
    return best if best is not None else n


def _vspec(shape, index_map):
    return pl.BlockSpec(shape, index_map)


def _mm(a, b, mode, out_dtype, name, tm_cap=512, tn_cap=512):
    if mode == "nn":
        (m, k), n = a.shape, b.shape[1]
        dims = (((1,), (0,)), ((), ()))
    elif mode == "nt":
        (m, k), n = a.shape, b.shape[0]
        dims = (((1,), (1,)), ((), ()))
    else:
        (k, m), n = a.shape, b.shape[1]
        dims = (((0,), (0,)), ((), ()))
    tm, tn = _pick(m, tm_cap), _pick(n, tn_cap)
    if mode == "tn":
        a_spec = _vspec((k, tm), lambda i, j: (0, i))
    else:
        a_spec = _vspec((tm, k), lambda i, j: (i, 0))
    if mode == "nt":
        b_spec = _vspec((tn, k), lambda i, j: (j, 0))
    else:
        b_spec = _vspec((k, tn), lambda i, j: (0, j))

    def body(a_ref, b_ref, o_ref):
        o_ref[...] = lax.dot_general(a_ref[...].astype(BF16), b_ref[...].astype(BF16), dims,
                                     preferred_element_type=F32).astype(out_dtype)

    return pl.pallas_call(
        body, name=name, grid=(m // tm, n // tn), in_specs=[a_spec, b_spec],
        out_specs=_vspec((tm, tn), lambda i, j: (i, j)), out_shape=jax.ShapeDtypeStruct((m, n), out_dtype),
    )(a, b)


def _rms(x, w):
    return (x * lax.rsqrt(jnp.mean(x * x, axis=-1, keepdims=True) + RMS_EPS)) * w


def _pre_f(x, w, sc, sh):
    return _rms(x, w) * (1.0 + sc) + sh


def _post_f(y, w, g):
    return g * _rms(y, w)


def _tok_grid(t, bsz, ts):
    nt = t // bsz // ts
    return nt, (bsz, nt)


def _pre_fwd(x, w, sc, sh, name, ts=512):
    t, d = x.shape
    bsz = sc.shape[0]
    nt, grid = _tok_grid(t, bsz, ts)
    row = _vspec((ts, d), lambda b, i: (b * nt + i, 0))
    vec = _vspec((1, d), lambda b, i: (0, 0))
    bvec = _vspec((1, 1, d), lambda b, i: (b, 0, 0))

    def body(x_ref, w_ref, sc_ref, sh_ref, u_ref):
        u_ref[...] = _pre_f(x_ref[...], w_ref[...], sc_ref[0], sh_ref[0]).astype(BF16)

    return pl.pallas_call(body, name=name, grid=grid, in_specs=[row, vec, bvec, bvec], out_specs=row,
                          out_shape=jax.ShapeDtypeStruct((t, d), BF16))(x, w, sc, sh)


def _pre_bwd(x, w, sc, sh, du, dres, name, ts=512):
    t, d = x.shape
    bsz = sc.shape[0]
    nt, grid = _tok_grid(t, bsz, ts)
    row = _vspec((ts, d), lambda b, i: (b * nt + i, 0))
    vec = _vspec((1, d), lambda b, i: (0, 0))
    bvec = _vspec((1, 1, d), lambda b, i: (b, 0, 0))

    def body(x_ref, w_ref, sc_ref, sh_ref, du_ref, dres_ref, dx_ref, dw_ref, dsc_ref, dsh_ref):
        b, i = pl.program_id(0), pl.program_id(1)
        _, vjp = jax.vjp(_pre_f, x_ref[...], w_ref[...], sc_ref[0], sh_ref[0])
        dx, dw, dsc, dsh = vjp(du_ref[...])
        dx_ref[...] = dres_ref[...] + dx

        @pl.when((b == 0) & (i == 0))
        def _():
            dw_ref[...] = jnp.zeros_like(dw_ref)

        @pl.when(i == 0)
        def _():
            dsc_ref[...] = jnp.zeros_like(dsc_ref)
            dsh_ref[...] = jnp.zeros_like(dsh_ref)

        dw_ref[...] += dw
        dsc_ref[0] += dsc
        dsh_ref[0] += dsh

    return pl.pallas_call(
        body, name=name, grid=grid, in_specs=[row, vec, bvec, bvec, row, row], out_specs=[row, vec, bvec, bvec],
        out_shape=[jax.ShapeDtypeStruct((t, d), F32), jax.ShapeDtypeStruct((1, d), F32),
                   jax.ShapeDtypeStruct((bsz, 1, d), F32), jax.ShapeDtypeStruct((bsz, 1, d), F32)],
    )(x, w, sc, sh, du, dres)


def _post_fwd(res, y, w, g, name, ts=512):
    t, d = y.shape
    bsz = g.shape[0]
    nt, grid = _tok_grid(t, bsz, ts)
    row = _vspec((ts, d), lambda b, i: (b * nt + i, 0))
    vec = _vspec((1, d), lambda b, i: (0, 0))
    bvec = _vspec((1, 1, d), lambda b, i: (b, 0, 0))

    def body(res_ref, y_ref, w_ref, g_ref, h_ref):
        h_ref[...] = res_ref[...] + _post_f(y_ref[...], w_ref[...], g_ref[0])

    return pl.pallas_call(body, name=name, grid=grid, in_specs=[row, row, vec, bvec], out_specs=row,
                          out_shape=jax.ShapeDtypeStruct((t, d), F32))(res, y, w, g)


def _post_loss(res, y, w, g, tgt, name, ts=512):
    t, d = y.shape
    bsz = g.shape[0]
    nt, grid = _tok_grid(t, bsz, ts)
    row = _vspec((ts, d), lambda b, i: (b * nt + i, 0))
    vec = _vspec((1, d), lambda b, i: (0, 0))
    bvec = _vspec((1, 1, d), lambda b, i: (b, 0, 0))
    acc = _vspec((1, d), lambda b, i: (0, 0))

    def body(res_ref, y_ref, w_ref, g_ref, tgt_ref, dh_ref, sq_ref):
        b, i = pl.program_id(0), pl.program_id(1)
        e = res_ref[...] + _post_f(y_ref[...], w_ref[...], g_ref[0]) - tgt_ref[...]
        dh_ref[...] = e * (1.0 / d)

        @pl.when((b == 0) & (i == 0))
        def _():
            sq_ref[...] = jnp.zeros_like(sq_ref)

        sq_ref[...] += jnp.sum(e * e, axis=0, keepdims=True) * (1.0 / d)

    return pl.pallas_call(
        body, name=name, grid=grid, in_specs=[row, row, vec, bvec, row], out_specs=[row, acc],
        out_shape=[jax.ShapeDtypeStruct((t, d), F32), jax.ShapeDtypeStruct((1, d), F32)],
    )(res, y, w, g, tgt)


def _post_bwd(dh, y, w, g, name, ts=512):
    t, d = y.shape
    bsz = g.shape[0]
    nt, grid = _tok_grid(t, bsz, ts)
    row = _vspec((ts, d), lambda b, i: (b * nt + i, 0))
    vec = _vspec((1, d), lambda b, i: (0, 0))
    bvec = _vspec((1, 1, d), lambda b, i: (b, 0, 0))

    def body(dh_ref, y_ref, w_ref, g_ref, dy_ref, dw_ref, dg_ref):
        b, i = pl.program_id(0), pl.program_id(1)
        _, vjp = jax.vjp(_post_f, y_ref[...], w_ref[...], g_ref[0])
        dy, dw, dg = vjp(dh_ref[...])
        dy_ref[...] = dy.astype(BF16)

        @pl.when((b == 0) & (i == 0))
        def _():
            dw_ref[...] = jnp.zeros_like(dw_ref)

        @pl.when(i == 0)
        def _():
            dg_ref[...] = jnp.zeros_like(dg_ref)

        dw_ref[...] += dw
        dg_ref[0] += dg

    return pl.pallas_call(
        body, name=name, grid=grid, in_specs=[row, row, vec, bvec], out_specs=[row, vec, bvec],
        out_shape=[jax.ShapeDtypeStruct((t, d), BF16), jax.ShapeDtypeStruct((1, d), F32),
                   jax.ShapeDtypeStruct((bsz, 1, d), F32)],
    )(dh, y, w, g)


def _merge_f(ga, gd, ya, yd):
    return jax.nn.sigmoid(ga) * ya + jax.nn.sigmoid(gd) * yd


_MW = 512


def _merge_fwd(proj, ya, yd, ts=512):
    t, d = ya.shape
    blk = _vspec((ts, _MW), lambda i, j: (i, j))
    ga = _vspec((ts, _MW), lambda i, j: (i, C_GA // _MW + j))
    gd = _vspec((ts, _MW), lambda i, j: (i, C_GD // _MW + j))

    def body(ga_ref, gd_ref, ya_ref, yd_ref, o_ref):
        o_ref[...] = _merge_f(ga_ref[...], gd_ref[...], ya_ref[...], yd_ref[...]).astype(BF16)

    return pl.pallas_call(body, name="merge_fwd", grid=(t // ts, d // _MW), in_specs=[ga, gd, blk, blk], out_specs=blk,
                          out_shape=jax.ShapeDtypeStruct((t, d), BF16))(proj, proj, ya, yd)


def _merge_bwd(proj, ya, yd, dm, ts=512):
    t, d = ya.shape
    blk = _vspec((ts, _MW), lambda i, j: (i, j))
    ga = _vspec((ts, _MW), lambda i, j: (i, C_GA // _MW + j))
    gd = _vspec((ts, _MW), lambda i, j: (i, C_GD // _MW + j))

    def body(ga_ref, gd_ref, ya_ref, yd_ref, dm_ref, dga_ref, dgd_ref, dya_ref, dyd_ref):
        _, vjp = jax.vjp(_merge_f, ga_ref[...], gd_ref[...], ya_ref[...], yd_ref[...])
        dga, dgd, dya, dyd = vjp(dm_ref[...])
        dga_ref[...] = dga.astype(BF16)
        dgd_ref[...] = dgd.astype(BF16)
        dya_ref[...] = dya.astype(BF16)
        dyd_ref[...] = dyd.astype(BF16)

    o = jax.ShapeDtypeStruct((t, d), BF16)
    return pl.pallas_call(body, name="merge_bwd", grid=(t // ts, d // _MW), in_specs=[ga, gd, blk, blk, blk],
                          out_specs=[blk] * 4, out_shape=[o] * 4)(proj, proj, ya, yd, dm)


def _shift_down(x, s):
    if s == 0:
        return x
    r = lax.broadcasted_iota(jnp.int32, x.shape, 0)
    return jnp.where(r >= s, pltpu.roll(x, s, 0), 0.0)


def _shift_up(x, s):
    if s == 0:
        return x
    n = x.shape[0]
    r = lax.broadcasted_iota(jnp.int32, x.shape, 0)
    return jnp.where(r < n - s, pltpu.roll(x, n - s, 0), 0.0)


def _conv_fwd(x, w, k):
    out = None
    for j in range(k):
        term = w[j:j + 1, :] * _shift_down(x, k - 1 - j)
        out = term if out is None else out + term
    return out


def _conv_bwd(x, w, dc, k):
    dx = None
    dws = []
    for j in range(k):
        s = k - 1 - j
        term = w[j:j + 1, :] * _shift_up(dc, s)
        dx = term if dx is None else dx + term
        dws.append(jnp.sum(dc * _shift_down(x, s), axis=0, keepdims=True))
    return dx, jnp.concatenate(dws, axis=0)


def _geglu_f(gate, val):
    return jax.nn.gelu(gate, approximate=True) * val


_FW = 256


def _ffn_act_fwd(up, conv_w, bsz):
    t = up.shape[0]
    s = t // bsz
    nj = D_FF // _FW
    xg = _vspec((s, _FW), lambda b, j: (b, j))
    xv = _vspec((s, _FW), lambda b, j: (b, nj + j))
    wg = _vspec((FFN_CONV, _FW), lambda b, j: (0, j))
    wv = _vspec((FFN_CONV, _FW), lambda b, j: (0, nj + j))

    def body(xg_ref, xv_ref, wg_ref, wv_ref, o_ref):
        gate = _conv_fwd(xg_ref[...], wg_ref[...], FFN_CONV)
        val = _conv_fwd(xv_ref[...], wv_ref[...], FFN_CONV)
        o_ref[...] = _geglu_f(gate, val).astype(BF16)

    return pl.pallas_call(body, name="ffn_act_fwd", grid=(bsz, nj), in_specs=[xg, xv, wg, wv],
                          out_specs=_vspec((s, _FW), lambda b, j: (b, j)),
                          out_shape=jax.ShapeDtypeStruct((t, D_FF), BF16))(up, up, conv_w, conv_w)


def _ffn_act_bwd(up, conv_w, dact, bsz):
    t = up.shape[0]
    s = t // bsz
    nj = D_FF // _FW
    xg = _vspec((s, _FW), lambda j, b: (b, j))
    xv = _vspec((s, _FW), lambda j, b: (b, nj + j))
    wg = _vspec((FFN_CONV, _FW), lambda j, b: (0, j))
    wv = _vspec((FFN_CONV, _FW), lambda j, b: (0, nj + j))
    da = _vspec((s, _FW), lambda j, b: (b, j))
    dwo = _vspec((FFN_CONV, _FW), lambda j, b: (0, j))

    def body(xg_ref, xv_ref, wg_ref, wv_ref, da_ref, dxg_ref, dxv_ref, dwg_ref, dwv_ref):
        b = pl.program_id(1)
        xg_, xv_, wg_, wv_ = xg_ref[...], xv_ref[...], wg_ref[...], wv_ref[...]
        gate = _conv_fwd(xg_, wg_, FFN_CONV)
        val = _conv_fwd(xv_, wv_, FFN_CONV)
        _, vjp = jax.vjp(_geglu_f, gate, val)
        dgate, dval = vjp(da_ref[...])
        dxg, dwg = _conv_bwd(xg_, wg_, dgate, FFN_CONV)
        dxv, dwv = _conv_bwd(xv_, wv_, dval, FFN_CONV)
        dxg_ref[...] = dxg.astype(BF16)
        dxv_ref[...] = dxv.astype(BF16)

        @pl.when(b == 0)
        def _():
            dwg_ref[...] = jnp.zeros_like(dwg_ref)
            dwv_ref[...] = jnp.zeros_like(dwv_ref)

        dwg_ref[...] += dwg
        dwv_ref[...] += dwv

    return pl.pallas_call(
        body, name="ffn_act_bwd", grid=(nj, bsz), in_specs=[xg, xv, wg, wv, da], out_specs=[da, da, dwo, dwo],
        out_shape=[jax.ShapeDtypeStruct((t, D_FF), BF16), jax.ShapeDtypeStruct((t, D_FF), BF16),
                   jax.ShapeDtypeStruct((FFN_CONV, D_FF), F32), jax.ShapeDtypeStruct((FFN_CONV, D_FF), F32)],
    )(up, up, conv_w, conv_w, dact)


def _bucket_table():
    qi = np.arange(WINDOW)[:, None]
    kj = np.arange(2 * WINDOW)[None, :]
    dist = WINDOW + qi - kj
    dc = np.maximum(dist, 0)
    max_exact = REL_BUCKETS // 2
    scaled = np.log(np.maximum(dc, 1).astype(np.float32) / np.float32(max_exact)) / np.float32(math.log(REL_MAX_DIST / max_exact))
    large = max_exact + (scaled.astype(np.float32) * np.float32(REL_BUCKETS - max_exact)).astype(np.int32)
    large = np.minimum(large, REL_BUCKETS - 1)
    bucket = np.where(dc < max_exact, dc, large).astype(np.int32)
    in_band = ((dist >= 0) & (dist < WINDOW)).astype(np.int32)
    return bucket, in_band


def _bias_build(rel_bias):
    bucket, _ = _bucket_table()

    def body(rb_ref, idx_ref, o_ref):
        h = pl.program_id(0)
        idx = idx_ref[...]
        acc = jnp.zeros(idx.shape, F32)
        for r in range(REL_BUCKETS):
            acc = jnp.where(idx == r, rb_ref[r, h], acc)
        o_ref[0] = acc

    return pl.pallas_call(
        body, name="bias_build", grid=(AQ_HEADS,),
        in_specs=[pl.BlockSpec(memory_space=pltpu.SMEM), _vspec((WINDOW, 2 * WINDOW), lambda h: (0, 0))],
        out_specs=_vspec((1, WINDOW, 2 * WINDOW), lambda h: (h, 0, 0)),
        out_shape=jax.ShapeDtypeStruct((AQ_HEADS, WINDOW, 2 * WINDOW), F32),
    )(rel_bias, jnp.asarray(bucket))


def _bias_grad(dbias):
    bucket, _ = _bucket_table()

    def body(db_ref, idx_ref, o_ref):
        idx = idx_ref[...]
        db = db_ref[0]
        lane = lax.broadcasted_iota(jnp.int32, (1, LANES), 1)
        acc = jnp.zeros((1, LANES), F32)
        for r in range(REL_BUCKETS):
            s = jnp.sum(jnp.sum(jnp.where(idx == r, db, 0.0), axis=1, keepdims=True), axis=0, keepdims=True)
            acc = jnp.where(lane == r, s, acc)
        o_ref[0] = acc

    return pl.pallas_call(
        body, name="bias_grad", grid=(AQ_HEADS,),
        in_specs=[_vspec((1, WINDOW, 2 * WINDOW), lambda h: (h, 0, 0)), _vspec((WINDOW, 2 * WINDOW), lambda h: (0, 0))],
        out_specs=_vspec((1, 1, LANES), lambda h: (h, 0, 0)),
        out_shape=jax.ShapeDtypeStruct((AQ_HEADS, 1, LANES), F32),
    )(dbias, jnp.asarray(bucket))


def _attn_mask(n):
    qi = lax.broadcasted_iota(jnp.int32, (WINDOW, 2 * WINDOW), 0)
    kj = lax.broadcasted_iota(jnp.int32, (WINDOW, 2 * WINDOW), 1)
    dist = WINDOW + qi - kj
    band = (dist >= 0) & (dist < WINDOW)
    return band & ((kj >= WINDOW) | (n > 0))


def _attn_probs(q, kb, bias, sink, mask):
    s = lax.dot_general(q, kb, (((1,), (1,)), ((), ())), preferred_element_type=F32) * (A_HD ** -0.5)
    s = jnp.where(mask, s + bias, NEG_INF)
    m = jnp.maximum(jnp.max(s, axis=-1, keepdims=True), sink)
    p = jnp.exp(s - m)
    es = jnp.exp(sink - m)
    inv = 1.0 / (jnp.sum(p, axis=-1, keepdims=True) + es)
    return p * inv, es * inv


def _attn_fwd(proj, bias, sinks, bsz):
    t = proj.shape[0]
    s = t // bsz
    nb = s // WINDOW
    grp = AQ_HEADS // AKV_HEADS

    def body(q_ref, k_ref, v_ref, bias_ref, sink_ref, y_ref, kp_ref, vp_ref):
        kp_ref[0:WINDOW, :] = jnp.zeros((WINDOW, LANES), BF16)
        vp_ref[0:WINDOW, :] = jnp.zeros((WINDOW, LANES), BF16)
        kp_ref[WINDOW:, :] = k_ref[...].astype(BF16)
        vp_ref[WINDOW:, :] = v_ref[...].astype(BF16)

        def blk(n, carry):
            r0 = pl.multiple_of(n * WINDOW, WINDOW)
            mask = _attn_mask(n)
            kband = kp_ref[pl.ds(r0, 2 * WINDOW), :]
            vband = vp_ref[pl.ds(r0, 2 * WINDOW), :]
            qb = q_ref[pl.ds(r0, WINDOW), :].astype(BF16)
            outs = []
            for hq in range(AQ_HEADS):
                kv = hq // grp
                kb = kband[:, kv * A_HD:(kv + 1) * A_HD]
                vb = vband[:, kv * A_HD:(kv + 1) * A_HD]
                probs, _ = _attn_probs(qb[:, hq * A_HD:(hq + 1) * A_HD], kb, bias_ref[hq], sink_ref[0, hq], mask)
                outs.append(jnp.dot(probs.astype(BF16), vb, preferred_element_type=F32))
            y_ref[pl.ds(r0, WINDOW), :] = jnp.concatenate(outs, axis=1).astype(BF16)
            return carry

        lax.fori_loop(0, nb, blk, 0)

    return pl.pallas_call(
        body, name="attn_fwd", grid=(bsz,),
        in_specs=[_vspec((s, AQ), lambda b: (b, C_Q // AQ)), _vspec((s, AKV), lambda b: (b, C_K // AKV)),
                  _vspec((s, AKV), lambda b: (b, C_V // AKV)),
                  _vspec((AQ_HEADS, WINDOW, 2 * WINDOW), lambda b: (0, 0, 0)), pl.BlockSpec(memory_space=pltpu.SMEM)],
        out_specs=_vspec((s, AQ), lambda b: (b, 0)), out_shape=jax.ShapeDtypeStruct((t, AQ), BF16),
        scratch_shapes=[pltpu.VMEM((s + WINDOW, LANES), BF16), pltpu.VMEM((s + WINDOW, LANES), BF16)],
    )(proj, proj, proj, bias, sinks)


def _attn_bwd(proj, bias, sinks, dy, bsz):
    t = proj.shape[0]
    s = t // bsz
    nb = s // WINDOW
    grp = AQ_HEADS // AKV_HEADS
    scale = A_HD ** -0.5

    def body(q_ref, k_ref, v_ref, bias_ref, sink_ref, dy_ref, dq_ref, dk_ref, dv_ref, dbias_ref, dsink_ref,
             kp_ref, vp_ref, dkp_ref, dvp_ref):
        b = pl.program_id(0)
        kp_ref[0:WINDOW, :] = jnp.zeros((WINDOW, LANES), BF16)
        vp_ref[0:WINDOW, :] = jnp.zeros((WINDOW, LANES), BF16)
        kp_ref[WINDOW:, :] = k_ref[...].astype(BF16)
        vp_ref[WINDOW:, :] = v_ref[...].astype(BF16)
        dkp_ref[...] = jnp.zeros_like(dkp_ref)
        dvp_ref[...] = jnp.zeros_like(dvp_ref)

        @pl.when(b == 0)
        def _():
            dbias_ref[...] = jnp.zeros_like(dbias_ref)
            dsink_ref[...] = jnp.zeros_like(dsink_ref)

        def blk(n, carry):
            r0 = pl.multiple_of(n * WINDOW, WINDOW)
            mask = _attn_mask(n)
            kband = kp_ref[pl.ds(r0, 2 * WINDOW), :]
            vband = vp_ref[pl.ds(r0, 2 * WINDOW), :]
            qb = q_ref[pl.ds(r0, WINDOW), :].astype(BF16)
            dyb = dy_ref[pl.ds(r0, WINDOW), :].astype(BF16)
            dqs = []
            dks = [None] * AKV_HEADS
            dvs = [None] * AKV_HEADS
            for hq in range(AQ_HEADS):
                kv = hq // grp
                kb = kband[:, kv * A_HD:(kv + 1) * A_HD]
                vb = vband[:, kv * A_HD:(kv + 1) * A_HD]
                qh = qb[:, hq * A_HD:(hq + 1) * A_HD]
                dyh = dyb[:, hq * A_HD:(hq + 1) * A_HD]
                probs, psink = _attn_probs(qh, kb, bias_ref[hq], sink_ref[0, hq], mask)
                dprobs = lax.dot_general(dyh, vb, (((1,), (1,)), ((), ())), preferred_element_type=F32)
                dvh = lax.dot_general(probs.astype(BF16), dyh, (((0,), (0,)), ((), ())), preferred_element_type=F32)
                rowdot = jnp.sum(probs * dprobs, axis=-1, keepdims=True)
                ds = probs * (dprobs - rowdot)
                dbias_ref[hq] += ds
                dsink_ref[hq] += jnp.sum(-psink * rowdot, axis=0, keepdims=True) + jnp.zeros((1, LANES), F32)
                dsb = ds.astype(BF16)
                dqs.append(jnp.dot(dsb, kb, preferred_element_type=F32) * scale)
                dkh = lax.dot_general(dsb, qh, (((0,), (0,)), ((), ())), preferred_element_type=F32) * scale
                dks[kv] = dkh if dks[kv] is None else dks[kv] + dkh
                dvs[kv] = dvh if dvs[kv] is None else dvs[kv] + dvh
            dq_ref[pl.ds(r0, WINDOW), :] = jnp.concatenate(dqs, axis=1).astype(BF16)
            dkp_ref[pl.ds(r0, 2 * WINDOW), :] += jnp.concatenate(dks, axis=1)
            dvp_ref[pl.ds(r0, 2 * WINDOW), :] += jnp.concatenate(dvs, axis=1)
            return carry

        lax.fori_loop(0, nb, blk, 0)
        dk_ref[...] = dkp_ref[WINDOW:, :].astype(BF16)
        dv_ref[...] = dvp_ref[WINDOW:, :].astype(BF16)

    kvs = jax.ShapeDtypeStruct((t, AKV), BF16)
    return pl.pallas_call(
        body, name="attn_bwd", grid=(bsz,),
        in_specs=[_vspec((s, AQ), lambda b: (b, C_Q // AQ)), _vspec((s, AKV), lambda b: (b, C_K // AKV)),
                  _vspec((s, AKV), lambda b: (b, C_V // AKV)),
                  _vspec((AQ_HEADS, WINDOW, 2 * WINDOW), lambda b: (0, 0, 0)), pl.BlockSpec(memory_space=pltpu.SMEM),
                  _vspec((s, AQ), lambda b: (b, 0))],
        out_specs=[_vspec((s, AQ), lambda b: (b, 0)), _vspec((s, AKV), lambda b: (b, 0)), _vspec((s, AKV), lambda b: (b, 0)),
                   _vspec((AQ_HEADS, WINDOW, 2 * WINDOW), lambda b: (0, 0, 0)), _vspec((AQ_HEADS, 1, LANES), lambda b: (0, 0, 0))],
        out_shape=[jax.ShapeDtypeStruct((t, AQ), BF16), kvs, kvs,
                   jax.ShapeDtypeStruct((AQ_HEADS, WINDOW, 2 * WINDOW), F32), jax.ShapeDtypeStruct((AQ_HEADS, 1, LANES), F32)],
        scratch_shapes=[pltpu.VMEM((s + WINDOW, LANES), BF16), pltpu.VMEM((s + WINDOW, LANES), BF16),
                        pltpu.VMEM((s + WINDOW, LANES), F32), pltpu.VMEM((s + WINDOW, LANES), F32)],
    )(proj, proj, proj, bias, sinks, dy)


def _dn_act_f(c, is_qk):
    a = jax.nn.silu(c)
    outs = []
    for h in range(DN_HEADS):
        ah = a[:, h * DN_HD:(h + 1) * DN_HD]
        nh = ah * lax.rsqrt(jnp.sum(ah * ah, axis=-1, keepdims=True) + L2_EPS)
        outs.append(jnp.where(is_qk, nh, ah))
    return jnp.concatenate(outs, axis=1)


def _dn_prep_fwd(proj, conv_w, bsz):
    t = proj.shape[0]
    s = t // bsz
    blk = _vspec((s, DN), lambda b, j: (b, j))
    wsp = _vspec((DN_CONV, DN), lambda b, j: (0, j))

    def body(x_ref, w_ref, o_ref):
        j = pl.program_id(1)
        o_ref[...] = _dn_act_f(_conv_fwd(x_ref[...], w_ref[...], DN_CONV), j < 2)

    return pl.pallas_call(body, name="dn_prep_fwd", grid=(bsz, 3), in_specs=[blk, wsp], out_specs=blk,
                          out_shape=jax.ShapeDtypeStruct((t, 3 * DN), F32))(proj, conv_w)


def _dn_prep_bwd(proj, conv_w, dqkvn, bsz):
    t = proj.shape[0]
    s = t // bsz
    blk = _vspec((s, DN), lambda j, b: (b, j))
    wsp = _vspec((DN_CONV, DN), lambda j, b: (0, j))

    def body(x_ref, w_ref, d_ref, dx_ref, dw_ref):
        j, b = pl.program_id(0), pl.program_id(1)
        x, w = x_ref[...], w_ref[...]
        c = _conv_fwd(x, w, DN_CONV)
        _, vjp = jax.vjp(lambda cc: _dn_act_f(cc, j < 2), c)
        (dc,) = vjp(d_ref[...])
        dx, dw = _conv_bwd(x, w, dc, DN_CONV)
        dx_ref[...] = dx.astype(BF16)

        @pl.when(b == 0)
        def _():
            dw_ref[...] = jnp.zeros_like(dw_ref)

        dw_ref[...] += dw

    return pl.pallas_call(
        body, name="dn_prep_bwd", grid=(3, bsz), in_specs=[blk, wsp, blk], out_specs=[blk, wsp],
        out_shape=[jax.ShapeDtypeStruct((t, 3 * DN), BF16), jax.ShapeDtypeStruct((DN_CONV, 3 * DN), F32)],
    )(proj, conv_w, dqkvn)


def _bg_f(x, alog, dt):
    lane = lax.broadcasted_iota(jnp.int32, x.shape, 1)
    beta = jax.nn.sigmoid(x)
    g = -jnp.exp(alog) * jax.nn.softplus(x + dt)
    return jnp.where(lane < DN_HEADS, beta, jnp.where(lane < 2 * DN_HEADS, g, 0.0))


def _bg_fwd(proj, alog, dt, bsz):
    t = proj.shape[0]
    s = t // bsz
    vec = _vspec((1, LANES), lambda b: (0, 0))

    def body(x_ref, a_ref, d_ref, o_ref):
        o_ref[...] = _bg_f(x_ref[...], a_ref[...], d_ref[...])

    return pl.pallas_call(body, name="bg_fwd", grid=(bsz,), in_specs=[_vspec((s, LANES), lambda b: (b, C_BD // LANES)), vec, vec],
                          out_specs=_vspec((s, LANES), lambda b: (b, 0)), out_shape=jax.ShapeDtypeStruct((t, LANES), F32))(proj, alog, dt)


def _bg_bwd(proj, alog, dt, dbg4, bsz):
    t = proj.shape[0]
    s = t // bsz
    vec = _vspec((1, LANES), lambda b: (0, 0))

    def body(x_ref, a_ref, d_ref, g4_ref, dx_ref, da_ref, dd_ref):
        b = pl.program_id(0)
        lane = lax.broadcasted_iota(jnp.int32, (s, LANES), 1)
        dbg = jnp.zeros((s, LANES), F32)
        for h in range(DN_HEADS):
            gh = g4_ref[:, h * DN_HD:(h + 1) * DN_HD]
            dbg = jnp.where(lane == h, gh[:, 0:1], dbg)
            dbg = jnp.where(lane == DN_HEADS + h, gh[:, 1:2], dbg)
        _, vjp = jax.vjp(_bg_f, x_ref[...], a_ref[...], d_ref[...])
        dx, da, dd = vjp(dbg)
        dx_ref[...] = dx.astype(BF16)

        @pl.when(b == 0)
        def _():
            da_ref[...] = jnp.zeros_like(da_ref)
            dd_ref[...] = jnp.zeros_like(dd_ref)

        da_ref[...] += da
        dd_ref[...] += dd

    return pl.pallas_call(
        body, name="bg_bwd", grid=(bsz,),
        in_specs=[_vspec((s, LANES), lambda b: (b, C_BD // LANES)), vec, vec, _vspec((s, DN), lambda b: (b, 0))],
        out_specs=[_vspec((s, LANES), lambda b: (b, 0)), vec, vec],
        out_shape=[jax.ShapeDtypeStruct((t, LANES), BF16), jax.ShapeDtypeStruct((1, LANES), F32), jax.ShapeDtypeStruct((1, LANES), F32)],
    )(proj, alog, dt, dbg4)


def _dn_out_f(o, z, w):
    outs = []
    for h in range(DN_HEADS):
        sl = slice(h * DN_HD, (h + 1) * DN_HD)
        outs.append(_rms(o[:, sl], w) * jax.nn.silu(z[:, sl]))
    return jnp.concatenate(outs, axis=1)


def _dn_out_fwd(o, proj, w, ts=512):
    t = o.shape[0]
    blk = _vspec((ts, DN), lambda i: (i, 0))
    zsp = _vspec((ts, DN), lambda i: (i, C_DZ // DN))
    vec = _vspec((1, DN_HD), lambda i: (0, 0))

    def body(o_ref, z_ref, w_ref, y_ref):
        y_ref[...] = _dn_out_f(o_ref[...], z_ref[...], w_ref[...]).astype(BF16)

    return pl.pallas_call(body, name="dn_out_fwd", grid=(t // ts,), in_specs=[blk, zsp, vec], out_specs=blk,
                          out_shape=jax.ShapeDtypeStruct((t, DN), BF16))(o, proj, w)


def _dn_out_bwd(o, proj, w, dy, ts=512):
    t = o.shape[0]
    blk = _vspec((ts, DN), lambda i: (i, 0))
    zsp = _vspec((ts, DN), lambda i: (i, C_DZ // DN))
    vec = _vspec((1, DN_HD), lambda i: (0, 0))

    def body(o_ref, z_ref, w_ref, dy_ref, do_ref, dz_ref, dw_ref):
        i = pl.program_id(0)
        _, vjp = jax.vjp(_dn_out_f, o_ref[...], z_ref[...], w_ref[...])
        do, dz, dw = vjp(dy_ref[...])
        do_ref[...] = do
        dz_ref[...] = dz.astype(BF16)

        @pl.when(i == 0)
        def _():
            dw_ref[...] = jnp.zeros_like(dw_ref)

        dw_ref[...] += dw

    return pl.pallas_call(
        body, name="dn_out_bwd", grid=(t // ts,), in_specs=[blk, zsp, vec, blk], out_specs=[blk, blk, vec],
        out_shape=[jax.ShapeDtypeStruct((t, DN), F32), jax.ShapeDtypeStruct((t, DN), BF16), jax.ShapeDtypeStruct((1, DN_HD), F32)],
    )(o, proj, w, dy)


_C = DN_CHUNK


def _dot(a, b, dims):
    return lax.dot_general(a.astype(BF16), b.astype(BF16), dims, preferred_element_type=F32)


_NN = (((1,), (0,)), ((), ()))
_NT = (((1,), (1,)), ((), ()))
_TN = (((0,), (0,)), ((), ()))


def _tri_inverse(lt, t_ref):
    ri = lax.broadcasted_iota(jnp.int32, (_C, _C), 0)
    ci = lax.broadcasted_iota(jnp.int32, (_C, _C), 1)
    t_ref[...] = jnp.where(ri == ci, 1.0, 0.0).astype(F32)
    for i in range(1, _C):
        hi = ((i + 7) // 8) * 8
        row = jnp.sum(lt[0:hi, i:i + 1] * t_ref[0:hi, :], axis=0, keepdims=True)
        t_ref[i:i + 1, :] = t_ref[i:i + 1, :] - row
    return t_ref[...]


def _chunk_common(q_ref, k_ref, v_ref, bg_ref, gr_ref, t_ref, c, h):
    r0 = pl.multiple_of(c * _C, _C)
    lane = lax.broadcasted_iota(jnp.int32, (_C, LANES), 1)
    ri = lax.broadcasted_iota(jnp.int32, (_C, _C), 0)
    ci = lax.broadcasted_iota(jnp.int32, (_C, _C), 1)
    q = q_ref[pl.ds(r0, _C), :] * (DN_HD ** -0.5)
    k = k_ref[pl.ds(r0, _C), :]
    v = v_ref[pl.ds(r0, _C), :]
    bgc = bg_ref[pl.ds(r0, _C), :]
    beta = jnp.sum(jnp.where(lane == h, bgc, 0.0), axis=1, keepdims=True)
    g_col = jnp.sum(jnp.where(lane == h + DN_HEADS, bgc, 0.0), axis=1, keepdims=True)
    g_row = gr_ref[0, 0, pl.ds(c, 1), :]
    gc_col = jnp.sum(jnp.where(ri >= ci, g_row, 0.0), axis=1, keepdims=True)
    gc_row = jnp.sum(jnp.where(ri <= ci, g_col, 0.0), axis=0, keepdims=True)
    gc_last = jnp.sum(g_col, axis=0, keepdims=True)
    diff = gc_col - gc_row
    decay = jnp.where(ri >= ci, jnp.exp(jnp.where(ri >= ci, diff, 0.0)), 0.0)
    diff_t = gc_row - gc_col
    decay_t = jnp.where(ri <= ci, jnp.exp(jnp.where(ri <= ci, diff_t, 0.0)), 0.0)
    eg = jnp.exp(gc_col)
    et = jnp.exp(gc_last - gc_col)
    gl = jnp.exp(gc_last)
    kb = k * beta
    vb = v * beta
    kk = _dot(kb, k, _NT)
    kk_t = _dot(k, kb, _NT)
    lt = jnp.where(ri < ci, kk_t * decay_t, 0.0)
    tm = _tri_inverse(lt, t_ref)
    rhs = jnp.concatenate([vb, kb * eg], axis=1)
    sol = jnp.dot(tm, rhs, precision=HI, preferred_element_type=F32)
    qk = _dot(q, k, _NT)
    intra = jnp.where(ri >= ci, qk * decay, 0.0)
    return dict(r0=r0, q=q, k=k, v=v, beta=beta, decay=decay, decay_t=decay_t, eg=eg, et=et, gl=gl, kb=kb, vb=vb,
                kk=kk, kk_t=kk_t, tm=tm, sol=sol, qk=qk, intra=intra, ri=ri, ci=ci, lane=lane)


def _dn_chunk_fwd(qkvn, bg, g_rows, bsz):
    t = qkvn.shape[0]
    s = t // bsz
    nc = s // _C

    def body(q_ref, k_ref, v_ref, bg_ref, gr_ref, o_ref, st_ref, s_ref, t_ref):
        h = pl.program_id(1)
        s_ref[...] = jnp.zeros_like(s_ref)

        def chunk(c, carry):
            m = _chunk_common(q_ref, k_ref, v_ref, bg_ref, gr_ref, t_ref, c, h)
            st = s_ref[...]
            st_ref[0, 0, c] = st
            u, w = m["sol"][:, :DN_HD], m["sol"][:, DN_HD:]
            v_new = u - _dot(w, st, _NN)
            o = _dot(m["q"] * m["eg"], st, _NN) + _dot(m["intra"], v_new, _NN)
            s_ref[...] = st * m["gl"] + _dot(m["k"] * m["et"], v_new, _TN)
            o_ref[pl.ds(m["r0"], _C), :] = o
            return carry

        lax.fori_loop(0, nc, chunk, 0)

    col = lambda off: _vspec((s, DN_HD), lambda b, h: (b, off + h))
    return pl.pallas_call(
        body, name="dn_chunk_fwd", grid=(bsz, DN_HEADS),
        in_specs=[col(0), col(DN_HEADS), col(2 * DN_HEADS), _vspec((s, LANES), lambda b, h: (b, 0)),
                  _vspec((1, 1, nc, _C), lambda b, h: (b, h, 0, 0))],
        out_specs=[col(0), _vspec((1, 1, nc, DN_HD, DN_HD), lambda b, h: (b, h, 0, 0, 0))],
        out_shape=[jax.ShapeDtypeStruct((t, DN), F32), jax.ShapeDtypeStruct((bsz, DN_HEADS, nc, DN_HD, DN_HD), F32)],
        scratch_shapes=[pltpu.VMEM((DN_HD, DN_HD), F32), pltpu.VMEM((_C, _C), F32)],
    )(qkvn, qkvn, qkvn, bg, g_rows)


def _dn_chunk_bwd(qkvn, bg, g_rows, states, do, bsz):
    t = qkvn.shape[0]
    s = t // bsz
    nc = s // _C

    def body(q_ref, k_ref, v_ref, bg_ref, gr_ref, st_ref, do_ref, dq_ref, dk_ref, dv_ref, dbg_ref, ds_ref, t_ref):
        h = pl.program_id(1)
        ds_ref[...] = jnp.zeros_like(ds_ref)

        def chunk(cc, carry):
            c = nc - 1 - cc
            m = _chunk_common(q_ref, k_ref, v_ref, bg_ref, gr_ref, t_ref, c, h)
            q, k, v, beta, decay, decay_t = m["q"], m["k"], m["v"], m["beta"], m["decay"], m["decay_t"]
            eg, et, gl, kb, tm, sol, ri, ci, lane = m["eg"], m["et"], m["gl"], m["kb"], m["tm"], m["sol"], m["ri"], m["ci"], m["lane"]
            st = st_ref[0, 0, c]
            u, w = sol[:, :DN_HD], sol[:, DN_HD:]
            v_new = u - _dot(w, st, _NN)
            q_dec = q * eg
            k_tail = k * et
            ds_out = ds_ref[...]
            do_c = do_ref[pl.ds(m["r0"], _C), :]
            ds_in = ds_out * gl
            dgl = jnp.sum(jnp.sum(ds_out * st, axis=1, keepdims=True), axis=0, keepdims=True)
            dk_tail = _dot(v_new, ds_out, _NT)
            dv_new = _dot(k_tail, ds_out, _NN) + _dot(m["intra"], do_c, _TN)
            dq_dec = _dot(do_c, st, _NT)
            ds_in = ds_in + _dot(q_dec, do_c, _TN)
            dintra = jnp.where(ri >= ci, _dot(do_c, v_new, _NT), 0.0)
            dintra_t = jnp.where(ri <= ci, _dot(v_new, do_c, _NT), 0.0)
            dw = -_dot(dv_new, st, _NT)
            ds_in = ds_in - _dot(w, dv_new, _TN)
            dsol = jnp.concatenate([dv_new, dw], axis=1)
            drhs = lax.dot_general(tm, dsol, _TN, precision=HI, preferred_element_type=F32)
            dl = jnp.where(ri > ci, -_dot(drhs, sol, _NT), 0.0)
            dl_t = jnp.where(ri < ci, -_dot(sol, drhs, _NT), 0.0)
            dvb, dkbeg = drhs[:, :DN_HD], drhs[:, DN_HD:]
            dkb = dkbeg * eg
            deg = jnp.sum(dkbeg * kb, axis=1, keepdims=True)
            qk_t = _dot(k, q, _NT)
            em = (dl * m["kk"] + dintra * m["qk"]) * decay
            em_t = (dl_t * m["kk_t"] + dintra_t * qk_t) * decay_t
            dgc = jnp.sum(em, axis=1, keepdims=True) - jnp.sum(em_t, axis=1, keepdims=True)
            dkb = dkb + _dot(dl * decay, k, _NN)
            dk = _dot(dl_t * decay_t, kb, _NN)
            dq = _dot(dintra * decay, k, _NN)
            dk = dk + _dot(dintra_t * decay_t, q, _NN)
            dq = dq + dq_dec * eg
            deg = deg + jnp.sum(dq_dec * q, axis=1, keepdims=True)
            dk = dk + dk_tail * et
            det = jnp.sum(dk_tail * k, axis=1, keepdims=True)
            dgc = dgc + deg * eg - det * et
            dgc_last = jnp.sum(det * et, axis=0, keepdims=True) + dgl * gl
            rcol = lax.broadcasted_iota(jnp.int32, (_C, 1), 0)
            dgc = dgc + jnp.where(rcol == _C - 1, dgc_last, 0.0)
            dk = dk + dkb * beta
            dbeta = jnp.sum(dkb * k, axis=1, keepdims=True) + jnp.sum(dvb * v, axis=1, keepdims=True)
            dv = dvb * beta
            dgc_b = dgc + jnp.zeros((_C, LANES), F32)
            dg_b = jnp.dot(jnp.where(ri <= ci, 1.0, 0.0).astype(F32), dgc_b, precision=HI, preferred_element_type=F32)
            dq_ref[pl.ds(m["r0"], _C), :] = dq * (DN_HD ** -0.5)
            dk_ref[pl.ds(m["r0"], _C), :] = dk
            dv_ref[pl.ds(m["r0"], _C), :] = dv
            dbg_ref[pl.ds(m["r0"], _C), :] = jnp.where(lane == 0, dbeta, jnp.where(lane == 1, dg_b, 0.0))
            ds_ref[...] = ds_in
            return carry

        lax.fori_loop(0, nc, chunk, 0)

    col = lambda off: _vspec((s, DN_HD), lambda b, h: (b, off + h))
    o = jax.ShapeDtypeStruct((t, DN), F32)
    return pl.pallas_call(
        body, name="dn_chunk_bwd", grid=(bsz, DN_HEADS),
        in_specs=[col(0), col(DN_HEADS), col(2 * DN_HEADS), _vspec((s, LANES), lambda b, h: (b, 0)),
                  _vspec((1, 1, nc, _C), lambda b, h: (b, h, 0, 0)),
                  _vspec((1, 1, nc, DN_HD, DN_HD), lambda b, h: (b, h, 0, 0, 0)), col(0)],
        out_specs=[col(0)] * 4, out_shape=[o] * 4,
        scratch_shapes=[pltpu.VMEM((DN_HD, DN_HD), F32), pltpu.VMEM((_C, _C), F32)],
    )(qkvn, qkvn, qkvn, bg, g_rows, states, do)


def _mod_fwd(c_all, ada_w_loc, ada_b_loc):
    n, cols = c_all.shape[0], ada_w_loc.shape[1]

    def body(c_ref, w_ref, b_ref, o_ref):
        o_ref[...] = _dot(jax.nn.silu(c_ref[...]), w_ref[...], _NN) + b_ref[...]

    return pl.pallas_call(body, name="mod_fwd", out_shape=jax.ShapeDtypeStruct((n, cols), F32))(c_all, ada_w_loc, ada_b_loc)


def _ada_grad(c_all, dmod_loc, dmod_all):
    d, cols = c_all.shape[1], dmod_loc.shape[1]

    def body(c_ref, dl_ref, da_ref, gw_ref, gb_ref):
        gw_ref[...] = _dot(jax.nn.silu(c_ref[...]), dl_ref[...], _TN)
        gb_ref[...] = jnp.sum(da_ref[...], axis=0, keepdims=True)

    return pl.pallas_call(body, name="ada_grad", out_shape=[jax.ShapeDtypeStruct((d, cols), F32),
                                                           jax.ShapeDtypeStruct((1, dmod_all.shape[1]), F32)])(c_all, dmod_loc, dmod_all)


def _row_tile(r):
    for tr in (256, 128, 64, 32, 16, 8):
        if r % tr == 0:
            return tr
    return r


def _adamw(w, m, v, grads, name):
    r, c = w.shape
    tr = _row_tile(r)
    blk = _vspec((tr, c), lambda i: (i, 0))
    n = len(grads)

    def body(*refs):
        w_ref, m_ref, v_ref = refs[:3]
        g_ref, d_ref, mo_ref, vo_ref = refs[3 + n:]
        g = refs[3][...]
        for p in refs[4:3 + n]:
            g = g + p[...]
        m_new = ADAM_B1 * m_ref[...] + (1.0 - ADAM_B1) * g
        v_new = ADAM_B2 * v_ref[...] + (1.0 - ADAM_B2) * jnp.square(g)
        m_hat = m_new / (1.0 - ADAM_B1 ** ADAM_STEP)
        v_hat = v_new / (1.0 - ADAM_B2 ** ADAM_STEP)
        g_ref[...] = g
        d_ref[...] = -ADAM_LR * (m_hat / (jnp.sqrt(v_hat) + ADAM_EPS) + ADAM_WD * w_ref[...])
        mo_ref[...] = m_new
        vo_ref[...] = v_new

    o = jax.ShapeDtypeStruct((r, c), F32)
    return pl.pallas_call(body, name=name, grid=(r // tr,), in_specs=[blk] * (3 + n), out_specs=[blk] * 4,
                          out_shape=[o] * 4)(w, m, v, *grads)


def _sum_lead(x, name):
    p, r, c = x.shape
    tr = _row_tile(r)

    def body(x_ref, o_ref):
        acc = x_ref[0].astype(F32)
        for i in range(1, p):
            acc = acc + x_ref[i].astype(F32)
        o_ref[...] = acc

    return pl.pallas_call(body, name=name, grid=(r // tr,), in_specs=[_vspec((p, tr, c), lambda i: (0, i, 0))],
                          out_specs=_vspec((tr, c), lambda i: (i, 0)), out_shape=jax.ShapeDtypeStruct((r, c), F32))(x)


def _allgather8(x_shard, name):
    m_per, n = x_shard.shape

    def body(x_ref, out_ref, send_sems, recv_sems, local_sem):
        x, y, c = lax.axis_index("x"), lax.axis_index("y"), lax.axis_index("c")
        me, sibling = (x, y, c), (x, y, 1 - c)
        chips = [(1 - x, y), (x, 1 - y), (1 - x, 1 - y)]

        def rows(px, py, pc):
            return out_ref.at[pl.ds((4 * px + 2 * py + pc) * m_per, m_per), :]

        def copy(k, block, to, src=None):
            return pltpu.make_async_remote_copy(
                src_ref=rows(*block) if src is None else src, dst_ref=rows(*block), send_sem=send_sems.at[k],
                recv_sem=recv_sems.at[k], device_id=to, device_id_type=MESH)

        mine = pltpu.make_async_copy(x_ref, rows(*me), local_sem)
        mine.start()
        first = [copy(0, me, sibling, src=x_ref)]
        first += [copy(1 + j, me, (*chip, c), src=x_ref) for j, chip in enumerate(chips)]
        for cp in first:
            cp.start()
        passed = [copy(4 + j, (*chip, c), sibling) for j, chip in enumerate(chips)]
        for j, chip in enumerate(chips):
            copy(1 + j, (*chip, c), me).wait_recv()
            passed[j].start()
        copy(0, sibling, me).wait_recv()
        for j, chip in enumerate(chips):
            copy(4 + j, (*chip, 1 - c), me).wait_recv()
        for cp in first + passed:
            cp.wait_send()
        mine.wait()

    return pl.pallas_call(
        body, name=name, out_shape=jax.ShapeDtypeStruct((8 * m_per, n), x_shard.dtype),
        in_specs=[pl.BlockSpec(memory_space=pltpu.VMEM)], out_specs=pl.BlockSpec(memory_space=pltpu.VMEM),
        scratch_shapes=[pltpu.SemaphoreType.DMA((7,)), pltpu.SemaphoreType.DMA((7,)), pltpu.SemaphoreType.DMA],
    )(x_shard)


_HBM = pl.BlockSpec(memory_space=pltpu.HBM)


def _chip_gather(shards, name):
    n = len(shards)

    def body(*refs):
        ins, outs = refs[:n], refs[n:2 * n]
        send_sems, recv_sems, local_sems = refs[2 * n:]
        x, y, c = lax.axis_index("x"), lax.axis_index("y"), lax.axis_index("c")
        me = 2 * x + y
        chips = [(1 - x, y), (x, 1 - y), (1 - x, 1 - y)]
        sends, locs = [], []
        for i in range(n):
            loc = pltpu.make_async_copy(ins[i], outs[i].at[me], local_sems.at[i])
            loc.start()
            locs.append(loc)
            for j, (px, py) in enumerate(chips):
                cp = pltpu.make_async_remote_copy(src_ref=ins[i], dst_ref=outs[i].at[me], send_sem=send_sems.at[3 * i + j],
                                                  recv_sem=recv_sems.at[3 * i + j], device_id=(px, py, c), device_id_type=MESH)
                cp.start()
                sends.append(cp)
        for i in range(n):
            for j, (px, py) in enumerate(chips):
                pltpu.make_async_remote_copy(src_ref=ins[i], dst_ref=outs[i].at[2 * px + py], send_sem=send_sems.at[3 * i + j],
                                             recv_sem=recv_sems.at[3 * i + j], device_id=(px, py, c), device_id_type=MESH).wait_recv()
        for cp in sends:
            cp.wait_send()
        for loc in locs:
            loc.wait()

    return pl.pallas_call(
        body, name=name, out_shape=[jax.ShapeDtypeStruct((N_CHIPS,) + a.shape, a.dtype) for a in shards],
        in_specs=[_HBM] * n, out_specs=[_HBM] * n,
        scratch_shapes=[pltpu.SemaphoreType.DMA((3 * n,)), pltpu.SemaphoreType.DMA((3 * n,)), pltpu.SemaphoreType.DMA((n,))],
    )(*shards)


def _piece_exchange(pieces, name):
    n = len(pieces)

    def body(*refs):
        ins, outs = refs[:n], refs[n:2 * n]
        send_sems, recv_sems, local_sems = refs[2 * n:]
        x, y, c = lax.axis_index("x"), lax.axis_index("y"), lax.axis_index("c")
        me = 2 * x + y
        chips = [(1 - x, y), (x, 1 - y), (1 - x, 1 - y)]
        sends, locs = [], []
        for i in range(n):
            loc = pltpu.make_async_copy(ins[i].at[me], outs[i].at[me], local_sems.at[i])
            loc.start()
            locs.append(loc)
            for j, (px, py) in enumerate(chips):
                cp = pltpu.make_async_remote_copy(src_ref=ins[i].at[2 * px + py], dst_ref=outs[i].at[me],
                                                  send_sem=send_sems.at[3 * i + j], recv_sem=recv_sems.at[3 * i + j],
                                                  device_id=(px, py, c), device_id_type=MESH)
                cp.start()
                sends.append(cp)
        for i in range(n):
            for j, (px, py) in enumerate(chips):
                pltpu.make_async_remote_copy(src_ref=ins[i].at[me], dst_ref=outs[i].at[2 * px + py], send_sem=send_sems.at[3 * i + j],
                                             recv_sem=recv_sems.at[3 * i + j], device_id=(px, py, c), device_id_type=MESH).wait_recv()
        for cp in sends:
            cp.wait_send()
        for loc in locs:
            loc.wait()

    return pl.pallas_call(
        body, name=name, out_shape=[jax.ShapeDtypeStruct(a.shape, a.dtype) for a in pieces],
        in_specs=[_HBM] * n, out_specs=[_HBM] * n,
        scratch_shapes=[pltpu.SemaphoreType.DMA((3 * n,)), pltpu.SemaphoreType.DMA((3 * n,)), pltpu.SemaphoreType.DMA((n,))],
    )(*pieces)


def _sibling_exchange(arrs, name):
    n = len(arrs)

    def body(*refs):
        ins, outs = refs[:n], refs[n:2 * n]
        send_sems, recv_sems = refs[2 * n:]
        sibling = (lax.axis_index("x"), lax.axis_index("y"), 1 - lax.axis_index("c"))
        cps = [pltpu.make_async_remote_copy(src_ref=ins[i], dst_ref=outs[i], send_sem=send_sems.at[i], recv_sem=recv_sems.at[i],
                                            device_id=sibling, device_id_type=MESH) for i in range(n)]
        for cp in cps:
            cp.start()
        for cp in cps:
            cp.wait()

    return pl.pallas_call(
        body, name=name, out_shape=[jax.ShapeDtypeStruct(a.shape, a.dtype) for a in arrs], in_specs=[_HBM] * n, out_specs=[_HBM] * n,
        scratch_shapes=[pltpu.SemaphoreType.DMA((n,)), pltpu.SemaphoreType.DMA((n,))],
    )(*arrs)


def _to_padded(w):
    z = jnp.zeros((w.shape[0], IN_PAD - IN_DIM), w.dtype)
    return jnp.concatenate([w[:, 768:2304], w[:, 0:512], w[:, 2304:2816], w[:, 2824:3848], w[:, 3848:4872],
                            w[:, 512:640], w[:, 640:768], w[:, 2816:2824], z], axis=1)


def _from_padded(g):
    return jnp.concatenate([g[:, C_Q:C_Q + AQ], g[:, C_K:C_K + AKV], g[:, C_V:C_V + AKV], g[:, C_DQKV:C_DQKV + 3 * DN],
                            g[:, C_DZ:C_DZ + DN], g[:, C_BD:C_BD + 2 * DN_HEADS], g[:, C_GA:C_GA + D_MODEL],
                            g[:, C_GD:C_GD + D_MODEL]], axis=1)


def _lane_vec(a):
    return jnp.zeros((1, LANES), F32).at[0, DN_HEADS:2 * DN_HEADS].set(a)


def _device_step(x2, tgt2, mod, p, bsz):
    d = D_MODEL
    sh1, sc1, g1, sh2, sc2, g2 = [mod[:, i * d:(i + 1) * d].reshape(bsz, 1, d) for i in range(N_MOD)]
    alog_v, dt_v = _lane_vec(p["dn_a_log"]), _lane_vec(p["dn_dt_bias"])
    sinks = p["attn_sinks"].reshape(1, AQ_HEADS)
    u1 = _pre_fwd(x2, p["norm_mix_pre"], sc1, sh1, "pre1_fwd")
    proj = _mm(u1, p["w_in"], "nn", F32, "mm_proj")
    bias = _bias_build(p["rel_bias"])
    y_attn = _attn_fwd(proj, bias, sinks, bsz)
    qkvn = _dn_prep_fwd(proj, p["dn_conv_w"], bsz)
    bg = _bg_fwd(proj, alog_v, dt_v, bsz)
    nc = x2.shape[0] // bsz // DN_CHUNK
    g_rows = jnp.transpose(bg[:, DN_HEADS:2 * DN_HEADS].reshape(bsz, nc, DN_CHUNK, DN_HEADS), (0, 3, 1, 2))
    o, states = _dn_chunk_fwd(qkvn, bg, g_rows, bsz)
    y_dn = _dn_out_fwd(o, proj, p["dn_norm_w"])
    ya = _mm(y_attn, p["w_attn_branch"], "nn", F32, "mm_ya")
    yd = _mm(y_dn, p["w_dn_branch"], "nn", F32, "mm_yd")
    merged = _merge_fwd(proj, ya, yd)
    y1 = _mm(merged, p["w_out"], "nn", F32, "mm_y1")
    h1 = _post_fwd(x2, y1, p["norm_mix_post"], g1, "post1_fwd")
    u2 = _pre_fwd(h1, p["norm_ffn_pre"], sc2, sh2, "pre2_fwd")
    up = _mm(u2, p["ffn_w_up"], "nn", F32, "mm_up")
    act = _ffn_act_fwd(up, p["ffn_conv_w"], bsz)
    y2 = _mm(act, p["ffn_w_down"], "nn", F32, "mm_y2")
    dh2, sq = _post_loss(h1, y2, p["norm_ffn_post"], g2, tgt2, "post2_loss")
    g = {}
    dy2, g["norm_ffn_post"], dg2 = _post_bwd(dh2, y2, p["norm_ffn_post"], g2, "post2_bwd")
    dact = _mm(dy2, p["ffn_w_down"], "nt", F32, "mm_dact")
    g["ffn_w_down"] = _mm(act, dy2, "tn", F32, "mm_dwdown")
    dupg, dupv, dcwg, dcwv = _ffn_act_bwd(up, p["ffn_conv_w"], dact, bsz)
    g["ffn_conv_w"] = jnp.concatenate([dcwg, dcwv], axis=1)
    dup = jnp.concatenate([dupg, dupv], axis=1)
    g["ffn_w_up"] = _mm(u2, dup, "tn", F32, "mm_dwup")
    du2 = _mm(dup, p["ffn_w_up"], "nt", F32, "mm_du2")
    dh1, g["norm_ffn_pre"], dsc2, dsh2 = _pre_bwd(h1, p["norm_ffn_pre"], sc2, sh2, du2, dh2, "pre2_bwd")
    dy1, g["norm_mix_post"], dg1 = _post_bwd(dh1, y1, p["norm_mix_post"], g1, "post1_bwd")
    dmerged = _mm(dy1, p["w_out"], "nt", F32, "mm_dmerged")
    g["w_out"] = _mm(merged, dy1, "tn", F32, "mm_dwout")
    dga, dgd, dya, dyd = _merge_bwd(proj, ya, yd, dmerged)
    dy_attn = _mm(dya, p["w_attn_branch"], "nt", BF16, "mm_dyattn")
    g["w_attn_branch"] = _mm(y_attn, dya, "tn", F32, "mm_dwab")
    dy_dn = _mm(dyd, p["w_dn_branch"], "nt", F32, "mm_dydn")
    g["w_dn_branch"] = _mm(y_dn, dyd, "tn", F32, "mm_dwdb")
    do, dz, g["dn_norm_w"] = _dn_out_bwd(o, proj, p["dn_norm_w"], dy_dn)
    dqn, dkn, dvv, dbg4 = _dn_chunk_bwd(qkvn, bg, g_rows, states, do, bsz)
    d_dqkv, g["dn_conv_w"] = _dn_prep_bwd(proj, p["dn_conv_w"], jnp.concatenate([dqn, dkn, dvv], axis=1), bsz)
    dbd, g["dn_a_log"], g["dn_dt_bias"] = _bg_bwd(proj, alog_v, dt_v, dbg4, bsz)
    dq, dk, dv, dbias, g["attn_sinks"] = _attn_bwd(proj, bias, sinks, dy_attn, bsz)
    g["rel_bias"] = _bias_grad(dbias)
    dproj = jnp.concatenate([d_dqkv, dq, dz, dga, dgd, dk, dv, dbd], axis=1)
    g["w_in"] = _mm(u1, dproj, "tn", F32, "mm_dwin")
    du1 = _mm(dproj, p["w_in"], "nt", F32, "mm_du1")
    dx, g["norm_mix_pre"], dsc1, dsh1 = _pre_bwd(x2, p["norm_mix_pre"], sc1, sh1, du1, dh1, "pre1_bwd")
    dmod = jnp.concatenate([dsh1, dsc1, dg1, dsh2, dsc2, dg2], axis=-1).reshape(bsz, N_MOD * d)
    return sq, dx, dmod, g


_SMALL = (("norm_mix_pre", D_MODEL), ("norm_mix_post", D_MODEL), ("norm_ffn_pre", D_MODEL), ("norm_ffn_post", D_MODEL),
          ("dn_norm_w", DN_HD), ("dn_a_log", LANES), ("dn_dt_bias", LANES), ("attn_sinks", AQ_HEADS * LANES),
          ("rel_bias", AQ_HEADS * LANES), ("dn_conv_w", DN_CONV * 3 * DN), ("ffn_conv_w", FFN_CONV * 2 * D_FF))


def _pack_rows(parts, rows):
    flat = jnp.concatenate([a.reshape(-1) for a in parts])
    return jnp.concatenate([flat, jnp.zeros((rows * LANES - flat.shape[0],), F32)]).reshape(rows, LANES)


def _pad128(a):
    flat = a.reshape(-1)
    n = -(-flat.shape[0] // LANES) * LANES
    return jnp.concatenate([flat, jnp.zeros((n - flat.shape[0],), F32)]) if n != flat.shape[0] else flat


_W_NAMES = ("ada_w", "ada_b", "norm_mix_pre", "norm_mix_post", "norm_ffn_pre", "norm_ffn_post", "w_in", "dn_conv_w", "dn_a_log",
            "dn_dt_bias", "dn_norm_w", "attn_sinks", "rel_bias", "w_attn_branch", "w_dn_branch", "w_out", "ffn_w_up", "ffn_conv_w",
            "ffn_w_down")
_BIG = ("w_in", "w_attn_branch", "w_dn_branch", "w_out", "ffn_w_up", "ffn_w_down")
_ROW_SHARDED = ("w_out", "ffn_w_down")


def kernel(x, c, *rest):
    nw = len(_W_NAMES)
    w = dict(zip(_W_NAMES, rest[:nw]))
    loss_target = rest[nw]
    m = dict(zip(_W_NAMES, rest[nw + 1:2 * nw + 1]))
    v = dict(zip(_W_NAMES, rest[2 * nw + 1:3 * nw + 1]))
    ix, iy, ic = lax.axis_index("x"), lax.axis_index("y"), lax.axis_index("c")
    chip, dev = 2 * ix + iy, 4 * ix + 2 * iy + ic
    bsz, s, d = x.shape
    t = bsz * s
    n_dev = 8

    front_rows = 64
    front = _pack_rows([c, w["dn_conv_w"], w["ffn_conv_w"]], front_rows)
    front_all = _allgather8(front, "ag_front").reshape(n_dev, front_rows * LANES)
    n_c, n_dc, n_fc = bsz * d, DN_CONV * 3 * DN // N_CHIPS, FFN_CONV * 2 * D_FF // N_CHIPS
    c_all = front_all[:, :n_c].reshape(n_dev * bsz, d)
    per_chip = front_all[0::2]
    dn_conv_full = jnp.transpose(per_chip[:, n_c:n_c + n_dc].reshape(N_CHIPS, DN_CONV, -1), (1, 0, 2)).reshape(DN_CONV, 3 * DN)
    ffn_conv_full = jnp.transpose(per_chip[:, n_c + n_dc:n_c + n_dc + n_fc].reshape(N_CHIPS, FFN_CONV, -1), (1, 0, 2)).reshape(FFN_CONV, 2 * D_FF)

    mod_cols = N_MOD * d // N_CHIPS
    ada_b_loc = lax.dynamic_slice(w["ada_b"], (0, chip * mod_cols), (1, mod_cols))
    mod_part = _mod_fwd(c_all, w["ada_w"][0], ada_b_loc)
    mod_all = _allgather8(mod_part, "ag_mod").reshape(n_dev, n_dev * bsz, mod_cols)[0::2]
    mod = jnp.transpose(lax.dynamic_slice(mod_all, (0, dev * bsz, 0), (N_CHIPS, bsz, mod_cols)), (1, 0, 2)).reshape(bsz, N_MOD * d)

    gathered = _chip_gather([w[k][0].astype(BF16) for k in _BIG], "gather_weights")
    p = {}
    for k, a in zip(_BIG, gathered):
        if k in _ROW_SHARDED:
            p[k] = a.reshape(-1, a.shape[2])
        else:
            p[k] = jnp.transpose(a, (1, 0, 2)).reshape(a.shape[1], -1)
    p["w_in"] = _to_padded(p["w_in"])
    for k in ("norm_mix_pre", "norm_mix_post", "norm_ffn_pre", "norm_ffn_post", "dn_norm_w", "attn_sinks"):
        p[k] = w[k]
    p["dn_a_log"], p["dn_dt_bias"], p["rel_bias"] = w["dn_a_log"][0], w["dn_dt_bias"][0], w["rel_bias"]
    p["dn_conv_w"], p["ffn_conv_w"] = dn_conv_full, ffn_conv_full

    sq, dx, dmod, g = _device_step(x.reshape(t, d), loss_target.reshape(t, d), mod, p, bsz)
    loss = lax.psum(0.5 * jnp.sum(sq), ("x", "y", "c"))

    g["dn_a_log"], g["dn_dt_bias"] = g["dn_a_log"].reshape(-1), g["dn_dt_bias"].reshape(-1)
    small_rows = 328
    small = _pack_rows([dmod] + [g[k] for k, _ in _SMALL], small_rows)
    small_all = _allgather8(small, "ag_small").reshape(n_dev, small_rows, LANES)
    n_dm = bsz * N_MOD * d
    dmod_all = small_all.reshape(n_dev, -1)[:, :n_dm].reshape(n_dev * bsz, N_MOD * d)
    tot = _sum_lead(small_all, "sum_small").reshape(-1)
    gs, off = {}, n_dm
    for k, n in _SMALL:
        gs[k] = tot[off:off + n]
        off += n
    grad = {}
    grad["ada_w"], grad["ada_b"] = _ada_grad(c_all, lax.dynamic_slice(dmod_all, (0, chip * mod_cols), (n_dev * bsz, mod_cols)), dmod_all)
    for k in ("norm_mix_pre", "norm_mix_post", "norm_ffn_pre", "norm_ffn_post", "dn_norm_w"):
        grad[k] = gs[k]
    grad["dn_a_log"] = gs["dn_a_log"][DN_HEADS:2 * DN_HEADS]
    grad["dn_dt_bias"] = gs["dn_dt_bias"][DN_HEADS:2 * DN_HEADS]
    grad["attn_sinks"] = gs["attn_sinks"].reshape(AQ_HEADS, LANES)[:, 0]
    grad["rel_bias"] = gs["rel_bias"].reshape(AQ_HEADS, LANES)[:, :REL_BUCKETS].T
    grad["dn_conv_w"] = lax.dynamic_slice(gs["dn_conv_w"].reshape(DN_CONV, 3 * DN), (0, chip * (3 * DN // N_CHIPS)), (DN_CONV, 3 * DN // N_CHIPS))
    grad["ffn_conv_w"] = lax.dynamic_slice(gs["ffn_conv_w"].reshape(FFN_CONV, 2 * D_FF), (0, chip * (2 * D_FF // N_CHIPS)), (FFN_CONV, 2 * D_FF // N_CHIPS))

    g["w_in"] = _from_padded(g["w_in"])
    pieces = []
    for k in _BIG:
        a = g[k]
        if k in _ROW_SHARDED:
            pieces.append(a.reshape(N_CHIPS, a.shape[0] // N_CHIPS, a.shape[1]).astype(BF16))
        else:
            pieces.append(jnp.transpose(a.reshape(a.shape[0], N_CHIPS, a.shape[1] // N_CHIPS), (1, 0, 2)).astype(BF16))
    received = _piece_exchange(pieces, "exchange_grads")
    mine = [_sum_lead(r, "sum_" + k) for k, r in zip(_BIG, received)]
    theirs = _sibling_exchange(mine, "exchange_cores")

    out = {}
    for k, a, b in zip(_BIG, mine, theirs):
        out[k] = _adamw(w[k][0], m[k][0], v[k][0], [a, b], "adamw_" + k)
    out["ada_w"] = _adamw(w["ada_w"][0], m["ada_w"][0], v["ada_w"][0], [grad["ada_w"]], "adamw_ada_w")
    small_names = [k for k in _W_NAMES if k not in _BIG and k != "ada_w"]
    offs, n_tot = {}, 0
    for k in small_names:
        offs[k] = n_tot
        n_tot += -(-w[k].size // LANES) * LANES
    pack = lambda dct: jnp.concatenate([_pad128(dct[k]) for k in small_names]).reshape(-1, LANES)
    rows_s = n_tot // LANES
    pad_rows = -(-rows_s // 8) * 8 - rows_s
    padr = lambda a: jnp.concatenate([a, jnp.zeros((pad_rows, LANES), F32)]) if pad_rows else a
    res_small = _adamw(padr(pack(w)), padr(pack(m)), padr(pack(v)), [padr(pack(grad))], "adamw_small")
    for k in small_names:
        out[k] = [r.reshape(-1)[offs[k]:offs[k] + w[k].size].reshape(w[k].shape) for r in res_small]
    for k in _BIG + ("ada_w",):
        out[k] = [r.reshape(w[k].shape) for r in out[k]]
    grads, deltas, new_m, new_v = ([out[k][i] for k in _W_NAMES] for i in range(4))
    return (loss, dx.reshape(bsz, s, d), *grads, *deltas, *new_m, *new_v)
```

```python
import functools
import math

import numpy as np
import jax
import jax.numpy as jnp
from jax import lax
from jax.experimental import pallas as pl
from jax.experimental.pallas import tpu as pltpu

F32 = jnp.float32
BF16 = jnp.bfloat16
HI = lax.Precision.HIGHEST
MESH = pl.DeviceIdType.MESH

D_MODEL = 1024
N_MOD = 6
AQ_HEADS, AKV_HEADS, A_HD, WINDOW = 8, 2, 64, 128
REL_BUCKETS, REL_MAX_DIST = 32, 128
DN_HEADS, DN_HD, DN_CONV, DN_CHUNK = 4, 128, 4, 64
D_FF, FFN_CONV = 2816, 3
RMS_EPS, L2_EPS, NEG_INF = 1e-6, 1e-6, -1e30
AQ, AKV, DN = AQ_HEADS * A_HD, AKV_HEADS * A_HD, DN_HEADS * DN_HD
IN_DIM = AQ + 2 * AKV + 3 * DN + DN + 2 * DN_HEADS + 2 * D_MODEL
C_DQKV, C_Q, C_DZ, C_GA, C_GD, C_K, C_V, C_BD = 0, 1536, 2048, 2560, 3584, 4608, 4736, 4864
IN_PAD = 4992
LANES = 128
N_CHIPS = 4

ADAM_LR, ADAM_B1, ADAM_B2, ADAM_EPS, ADAM_WD, ADAM_STEP = 0.001, 0.9, 0.999, 1e-08, 0.01, 10


def _pick(n, cap):
    best = None
    for t in range(LANES, cap + 1, LANES):
        if n % t == 0:
            best = t
    return best if best is not None else n


def _vspec(shape, index_map):
    return pl.BlockSpec(shape, index_map)


def _mm(a, b, mode, out_dtype, name, tm_cap=512, tn_cap=512):
    if mode == "nn":
        (m, k), n = a.shape, b.shape[1]
        dims = (((1,), (0,)), ((), ()))
    elif mode == "nt":
        (m, k), n = a.shape, b.shape[0]
        dims = (((1,), (1,)), ((), ()))
    else:
        (k, m), n = a.shape, b.shape[1]
        dims = (((0,), (0,)), ((), ()))
    tm, tn = _pick(m, tm_cap), _pick(n, tn_cap)
    if mode == "tn":
        a_spec = _vspec((k, tm), lambda i, j: (0, i))
    else:
        a_spec = _vspec((tm, k), lambda i, j: (i, 0))
    if mode == "nt":
        b_spec = _vspec((tn, k), lambda i, j: (j, 0))
    else:
        b_spec = _vspec((k, tn), lambda i, j: (0, j))

    def body(a_ref, b_ref, o_ref):
        o_ref[...] = lax.dot_general(a_ref[...].astype(BF16), b_ref[...].astype(BF16), dims,
                                     preferred_element_type=F32).astype(out_dtype)

    return pl.pallas_call(
        body, name=name, grid=(m // tm, n // tn), in_specs=[a_spec, b_spec],
        out_specs=_vspec((tm, tn), lambda i, j: (i, j)), out_shape=jax.ShapeDtypeStruct((m, n), out_dtype),
    )(a, b)


def _rms(x, w):
    return (x * lax.rsqrt(jnp.mean(x * x, axis=-1, keepdims=True) + RMS_EPS)) * w


def _pre_f(x, w, sc, sh):
    return _rms(x, w) * (1.0 + sc) + sh


def _post_f(y, w, g):
    return g * _rms(y, w)


def _tok_grid(t, bsz, ts):
    nt = t // bsz // ts
    return nt, (bsz, nt)


def _pre_fwd(x, w, sc, sh, name, ts=512):
    t, d = x.shape
    bsz = sc.shape[0]
    nt, grid = _tok_grid(t, bsz, ts)
    row = _vspec((ts, d), lambda b, i: (b * nt + i, 0))
    vec = _vspec((1, d), lambda b, i: (0, 0))
    bvec = _vspec((1, 1, d), lambda b, i: (b, 0, 0))

    def body(x_ref, w_ref, sc_ref, sh_ref, u_ref):
        u_ref[...] = _pre_f(x_ref[...], w_ref[...], sc_ref[0], sh_ref[0]).astype(BF16)

    return pl.pallas_call(body, name=name, grid=grid, in_specs=[row, vec, bvec, bvec], out_specs=row,
                          out_shape=jax.ShapeDtypeStruct((t, d), BF16))(x, w, sc, sh)


def _pre_bwd(x, w, sc, sh, du, dres, name, ts=512):
    t, d = x.shape
    bsz = sc.shape[0]
    nt, grid = _tok_grid(t, bsz, ts)
    row = _vspec((ts, d), lambda b, i: (b * nt + i, 0))
    vec = _vspec((1, d), lambda b, i: (0, 0))
    bvec = _vspec((1, 1, d), lambda b, i: (b, 0, 0))

    def body(x_ref, w_ref, sc_ref, sh_ref, du_ref, dres_ref, dx_ref, dw_ref, dsc_ref, dsh_ref):
        b, i = pl.program_id(0), pl.program_id(1)
        _, vjp = jax.vjp(_pre_f, x_ref[...], w_ref[...], sc_ref[0], sh_ref[0])
        dx, dw, dsc, dsh = vjp(du_ref[...])
        dx_ref[...] = dres_ref[...] + dx

        @pl.when((b == 0) & (i == 0))
        def _():
            dw_ref[...] = jnp.zeros_like(dw_ref)

        @pl.when(i == 0)
        def _():
            dsc_ref[...] = jnp.zeros_like(dsc_ref)
            dsh_ref[...] = jnp.zeros_like(dsh_ref)

        dw_ref[...] += dw
        dsc_ref[0] += dsc
        dsh_ref[0] += dsh

    return pl.pallas_call(
        body, name=name, grid=grid, in_specs=[row, vec, bvec, bvec, row, row], out_specs=[row, vec, bvec, bvec],
        out_shape=[jax.ShapeDtypeStruct((t, d), F32), jax.ShapeDtypeStruct((1, d), F32),
                   jax.ShapeDtypeStruct((bsz, 1, d), F32), jax.ShapeDtypeStruct((bsz, 1, d), F32)],
    )(x, w, sc, sh, du, dres)


def _post_fwd(res, y, w, g, name, ts=512):
    t, d = y.shape
    bsz = g.shape[0]
    nt, grid = _tok_grid(t, bsz, ts)
    row = _vspec((ts, d), lambda b, i: (b * nt + i, 0))
    vec = _vspec((1, d), lambda b, i: (0, 0))
    bvec = _vspec((1, 1, d), lambda b, i: (b, 0, 0))

    def body(res_ref, y_ref, w_ref, g_ref, h_ref):
        h_ref[...] = res_ref[...] + _post_f(y_ref[...], w_ref[...], g_ref[0])

    return pl.pallas_call(body, name=name, grid=grid, in_specs=[row, row, vec, bvec], out_specs=row,
                          out_shape=jax.ShapeDtypeStruct((t, d), F32))(res, y, w, g)


def _post_loss(res, y, w, g, tgt, name, ts=512):
    t, d = y.shape
    bsz = g.shape[0]
    nt, grid = _tok_grid(t, bsz, ts)
    row = _vspec((ts, d), lambda b, i: (b * nt + i, 0))
    vec = _vspec((1, d), lambda b, i: (0, 0))
    bvec = _vspec((1, 1, d), lambda b, i: (b, 0, 0))
    acc = _vspec((1, d), lambda b, i: (0, 0))

    def body(res_ref, y_ref, w_ref, g_ref, tgt_ref, dh_ref, sq_ref):
        b, i = pl.program_id(0), pl.program_id(1)
        e = res_ref[...] + _post_f(y_ref[...], w_ref[...], g_ref[0]) - tgt_ref[...]
        dh_ref[...] = e * (1.0 / d)

        @pl.when((b == 0) & (i == 0))
        def _():
            sq_ref[...] = jnp.zeros_like(sq_ref)

        sq_ref[...] += jnp.sum(e * e, axis=0, keepdims=True) * (1.0 / d)

    return pl.pallas_call(
        body, name=name, grid=grid, in_specs=[row, row, vec, bvec, row], out_specs=[row, acc],
        out_shape=[jax.ShapeDtypeStruct((t, d), F32), jax.ShapeDtypeStruct((1, d), F32)],
    )(res, y, w, g, tgt)


def _post_bwd(dh, y, w, g, name, ts=512):
    t, d = y.shape
    bsz = g.shape[0]
    nt, grid = _tok_grid(t, bsz, ts)
    row = _vspec((ts, d), lambda b, i: (b * nt + i, 0))
    vec = _vspec((1, d), lambda b, i: (0, 0))
    bvec = _vspec((1, 1, d), lambda b, i: (b, 0, 0))

    def body(dh_ref, y_ref, w_ref, g_ref, dy_ref, dw_ref, dg_ref):
        b, i = pl.program_id(0), pl.program_id(1)
        _, vjp = jax.vjp(_post_f, y_ref[...], w_ref[...], g_ref[0])
        dy, dw, dg = vjp(dh_ref[...])
        dy_ref[...] = dy.astype(BF16)

        @pl.when((b == 0) & (i == 0))
        def _():
            dw_ref[...] = jnp.zeros_like(dw_ref)

        @pl.when(i == 0)
        def _():
            dg_ref[...] = jnp.zeros_like(dg_ref)

        dw_ref[...] += dw
        dg_ref[0] += dg

    return pl.pallas_call(
        body, name=name, grid=grid, in_specs=[row, row, vec, bvec], out_specs=[row, vec, bvec],
        out_shape=[jax.ShapeDtypeStruct((t, d), BF16), jax.ShapeDtypeStruct((1, d), F32),
                   jax.ShapeDtypeStruct((bsz, 1, d), F32)],
    )(dh, y, w, g)


def _merge_f(ga, gd, ya, yd):
    return jax.nn.sigmoid(ga) * ya + jax.nn.sigmoid(gd) * yd


_MW = 512


def _merge_fwd(proj, ya, yd, ts=512):
    t, d = ya.shape
    blk = _vspec((ts, _MW), lambda i, j: (i, j))
    ga = _vspec((ts, _MW), lambda i, j: (i, C_GA // _MW + j))
    gd = _vspec((ts, _MW), lambda i, j: (i, C_GD // _MW + j))

    def body(ga_ref, gd_ref, ya_ref, yd_ref, o_ref):
        o_ref[...] = _merge_f(ga_ref[...], gd_ref[...], ya_ref[...], yd_ref[...]).astype(BF16)

    return pl.pallas_call(body, name="merge_fwd", grid=(t // ts, d // _MW), in_specs=[ga, gd, blk, blk], out_specs=blk,
                          out_shape=jax.ShapeDtypeStruct((t, d), BF16))(proj, proj, ya, yd)


def _merge_bwd(proj, ya, yd, dm, ts=512):
    t, d = ya.shape
    blk = _vspec((ts, _MW), lambda i, j: (i, j))
    ga = _vspec((ts, _MW), lambda i, j: (i, C_GA // _MW + j))
    gd = _vspec((ts, _MW), lambda i, j: (i, C_GD // _MW + j))

    def body(ga_ref, gd_ref, ya_ref, yd_ref, dm_ref, dga_ref, dgd_ref, dya_ref, dyd_ref):
        _, vjp = jax.vjp(_merge_f, ga_ref[...], gd_ref[...], ya_ref[...], yd_ref[...])
        dga, dgd, dya, dyd = vjp(dm_ref[...])
        dga_ref[...] = dga.astype(BF16)
        dgd_ref[...] = dgd.astype(BF16)
        dya_ref[...] = dya.astype(BF16)
        dyd_ref[...] = dyd.astype(BF16)

    o = jax.ShapeDtypeStruct((t, d), BF16)
    return pl.pallas_call(body, name="merge_bwd", grid=(t // ts, d // _MW), in_specs=[ga, gd, blk, blk, blk],
                          out_specs=[blk] * 4, out_shape=[o] * 4)(proj, proj, ya, yd, dm)


def _shift_down(x, s):
    if s == 0:
        return x
    r = lax.broadcasted_iota(jnp.int32, x.shape, 0)
    return jnp.where(r >= s, pltpu.roll(x, s, 0), 0.0)


def _shift_up(x, s):
    if s == 0:
        return x
    n = x.shape[0]
    r = lax.broadcasted_iota(jnp.int32, x.shape, 0)
    return jnp.where(r < n - s, pltpu.roll(x, n - s, 0), 0.0)


def _conv_fwd(x, w, k):
    out = None
    for j in range(k):
        term = w[j:j + 1, :] * _shift_down(x, k - 1 - j)
        out = term if out is None else out + term
    return out


def _conv_bwd(x, w, dc, k):
    dx = None
    dws = []
    for j in range(k):
        s = k - 1 - j
        term = w[j:j + 1, :] * _shift_up(dc, s)
        dx = term if dx is None else dx + term
        dws.append(jnp.sum(dc * _shift_down(x, s), axis=0, keepdims=True))
    return dx, jnp.concatenate(dws, axis=0)


def _geglu_f(gate, val):
    return jax.nn.gelu(gate, approximate=True) * val


_FW = 256


def _ffn_act_fwd(up, conv_w, bsz):
    t = up.shape[0]
    s = t // bsz
    nj = D_FF // _FW
    xg = _vspec((s, _FW), lambda b, j: (b, j))
    xv = _vspec((s, _FW), lambda b, j: (b, nj + j))
    wg = _vspec((FFN_CONV, _FW), lambda b, j: (0, j))
    wv = _vspec((FFN_CONV, _FW), lambda b, j: (0, nj + j))

    def body(xg_ref, xv_ref, wg_ref, wv_ref, o_ref):
        gate = _conv_fwd(xg_ref[...], wg_ref[...], FFN_CONV)
        val = _conv_fwd(xv_ref[...], wv_ref[...], FFN_CONV)
        o_ref[...] = _geglu_f(gate, val).astype(BF16)

    return pl.pallas_call(body, name="ffn_act_fwd", grid=(bsz, nj), in_specs=[xg, xv, wg, wv],
                          out_specs=_vspec((s, _FW), lambda b, j: (b, j)),
                          out_shape=jax.ShapeDtypeStruct((t, D_FF), BF16))(up, up, conv_w, conv_w)


def _ffn_act_bwd(up, conv_w, dact, bsz):
    t = up.shape[0]
    s = t // bsz
    nj = D_FF // _FW
    xg = _vspec((s, _FW), lambda j, b: (b, j))
    xv = _vspec((s, _FW), lambda j, b: (b, nj + j))
    wg = _vspec((FFN_CONV, _FW), lambda j, b: (0, j))
    wv = _vspec((FFN_CONV, _FW), lambda j, b: (0, nj + j))
    da = _vspec((s, _FW), lambda j, b: (b, j))
    dwo = _vspec((FFN_CONV, _FW), lambda j, b: (0, j))

    def body(xg_ref, xv_ref, wg_ref, wv_ref, da_ref, dxg_ref, dxv_ref, dwg_ref, dwv_ref):
        b = pl.program_id(1)
        xg_, xv_, wg_, wv_ = xg_ref[...], xv_ref[...], wg_ref[...], wv_ref[...]
        gate = _conv_fwd(xg_, wg_, FFN_CONV)
        val = _conv_fwd(xv_, wv_, FFN_CONV)
        _, vjp = jax.vjp(_geglu_f, gate, val)
        dgate, dval = vjp(da_ref[...])
        dxg, dwg = _conv_bwd(xg_, wg_, dgate, FFN_CONV)
        dxv, dwv = _conv_bwd(xv_, wv_, dval, FFN_CONV)
        dxg_ref[...] = dxg.astype(BF16)
        dxv_ref[...] = dxv.astype(BF16)

        @pl.when(b == 0)
        def _():
            dwg_ref[...] = jnp.zeros_like(dwg_ref)
            dwv_ref[...] = jnp.zeros_like(dwv_ref)

        dwg_ref[...] += dwg
        dwv_ref[...] += dwv

    return pl.pallas_call(
        body, name="ffn_act_bwd", grid=(nj, bsz), in_specs=[xg, xv, wg, wv, da], out_specs=[da, da, dwo, dwo],
        out_shape=[jax.ShapeDtypeStruct((t, D_FF), BF16), jax.ShapeDtypeStruct((t, D_FF), BF16),
                   jax.ShapeDtypeStruct((FFN_CONV, D_FF), F32), jax.ShapeDtypeStruct((FFN_CONV, D_FF), F32)],
    )(up, up, conv_w, conv_w, dact)


def _bucket_table():
    qi = np.arange(WINDOW)[:, None]
    kj = np.arange(2 * WINDOW)[None, :]
    dist = WINDOW + qi - kj
    dc = np.maximum(dist, 0)
    max_exact = REL_BUCKETS // 2
    scaled = np.log(np.maximum(dc, 1).astype(np.float32) / np.float32(max_exact)) / np.float32(math.log(REL_MAX_DIST / max_exact))
    large = max_exact + (scaled.astype(np.float32) * np.float32(REL_BUCKETS - max_exact)).astype(np.int32)
    large = np.minimum(large, REL_BUCKETS - 1)
    bucket = np.where(dc < max_exact, dc, large).astype(np.int32)
    in_band = ((dist >= 0) & (dist < WINDOW)).astype(np.int32)
    return bucket, in_band


def _bias_build(rel_bias):
    bucket, _ = _bucket_table()

    def body(rb_ref, idx_ref, o_ref):
        h = pl.program_id(0)
        idx = idx_ref[...]
        acc = jnp.zeros(idx.shape, F32)
        for r in range(REL_BUCKETS):
            acc = jnp.where(idx == r, rb_ref[r, h], acc)
        o_ref[0] = acc

    return pl.pallas_call(
        body, name="bias_build", grid=(AQ_HEADS,),
        in_specs=[pl.BlockSpec(memory_space=pltpu.SMEM), _vspec((WINDOW, 2 * WINDOW), lambda h: (0, 0))],
        out_specs=_vspec((1, WINDOW, 2 * WINDOW), lambda h: (h, 0, 0)),
        out_shape=jax.ShapeDtypeStruct((AQ_HEADS, WINDOW, 2 * WINDOW), F32),
    )(rel_bias, jnp.asarray(bucket))


def _bias_grad(dbias):
    bucket, _ = _bucket_table()

    def body(db_ref, idx_ref, o_ref):
        idx = idx_ref[...]
        db = db_ref[0]
        lane = lax.broadcasted_iota(jnp.int32, (1, LANES), 1)
        acc = jnp.zeros((1, LANES), F32)
        for r in range(REL_BUCKETS):
            s = jnp.sum(jnp.sum(jnp.where(idx == r, db, 0.0), axis=1, keepdims=True), axis=0, keepdims=True)
            acc = jnp.where(lane == r, s, acc)
        o_ref[0] = acc

    return pl.pallas_call(
        body, name="bias_grad", grid=(AQ_HEADS,),
        in_specs=[_vspec((1, WINDOW, 2 * WINDOW), lambda h: (h, 0, 0)), _vspec((WINDOW, 2 * WINDOW), lambda h: (0, 0))],
        out_specs=_vspec((1, 1, LANES), lambda h: (h, 0, 0)),
        out_shape=jax.ShapeDtypeStruct((AQ_HEADS, 1, LANES), F32),
    )(dbias, jnp.asarray(bucket))


def _attn_mask(n):
    qi = lax.broadcasted_iota(jnp.int32, (WINDOW, 2 * WINDOW), 0)
    kj = lax.broadcasted_iota(jnp.int32, (WINDOW, 2 * WINDOW), 1)
    dist = WINDOW + qi - kj
    band = (dist >= 0) & (dist < WINDOW)
    return band & ((kj >= WINDOW) | (n > 0))


def _attn_probs(q, kb, bias, sink, mask):
    s = lax.dot_general(q, kb, (((1,), (1,)), ((), ())), preferred_element_type=F32) * (A_HD ** -0.5)
    s = jnp.where(mask, s + bias, NEG_INF)
    m = jnp.maximum(jnp.max(s, axis=-1, keepdims=True), sink)
    p = jnp.exp(s - m)
    es = jnp.exp(sink - m)
    inv = 1.0 / (jnp.sum(p, axis=-1, keepdims=True) + es)
    return p * inv, es * inv


def _attn_fwd(proj, bias, sinks, bsz):
    t = proj.shape[0]
    s = t // bsz
    nb = s // WINDOW
    grp = AQ_HEADS // AKV_HEADS

    def body(q_ref, k_ref, v_ref, bias_ref, sink_ref, y_ref, kp_ref, vp_ref):
        kp_ref[0:WINDOW, :] = jnp.zeros((WINDOW, LANES), BF16)
        vp_ref[0:WINDOW, :] = jnp.zeros((WINDOW, LANES), BF16)
        kp_ref[WINDOW:, :] = k_ref[...].astype(BF16)
        vp_ref[WINDOW:, :] = v_ref[...].astype(BF16)

        def blk(n, carry):
            r0 = pl.multiple_of(n * WINDOW, WINDOW)
            mask = _attn_mask(n)
            kband = kp_ref[pl.ds(r0, 2 * WINDOW), :]
            vband = vp_ref[pl.ds(r0, 2 * WINDOW), :]
            qb = q_ref[pl.ds(r0, WINDOW), :].astype(BF16)
            outs = []
            for hq in range(AQ_HEADS):
                kv = hq // grp
                kb = kband[:, kv * A_HD:(kv + 1) * A_HD]
                vb = vband[:, kv * A_HD:(kv + 1) * A_HD]
                probs, _ = _attn_probs(qb[:, hq * A_HD:(hq + 1) * A_HD], kb, bias_ref[hq], sink_ref[0, hq], mask)
                outs.append(jnp.dot(probs.astype(BF16), vb, preferred_element_type=F32))
            y_ref[pl.ds(r0, WINDOW), :] = jnp.concatenate(outs, axis=1).astype(BF16)
            return carry

        lax.fori_loop(0, nb, blk, 0)

    return pl.pallas_call(
        body, name="attn_fwd", grid=(bsz,),
        in_specs=[_vspec((s, AQ), lambda b: (b, C_Q // AQ)), _vspec((s, AKV), lambda b: (b, C_K // AKV)),
                  _vspec((s, AKV), lambda b: (b, C_V // AKV)),
                  _vspec((AQ_HEADS, WINDOW, 2 * WINDOW), lambda b: (0, 0, 0)), pl.BlockSpec(memory_space=pltpu.SMEM)],
        out_specs=_vspec((s, AQ), lambda b: (b, 0)), out_shape=jax.ShapeDtypeStruct((t, AQ), BF16),
        scratch_shapes=[pltpu.VMEM((s + WINDOW, LANES), BF16), pltpu.VMEM((s + WINDOW, LANES), BF16)],
    )(proj, proj, proj, bias, sinks)


def _attn_bwd(proj, bias, sinks, dy, bsz):
    t = proj.shape[0]
    s = t // bsz
    nb = s // WINDOW
    grp = AQ_HEADS // AKV_HEADS
    scale = A_HD ** -0.5

    def body(q_ref, k_ref, v_ref, bias_ref, sink_ref, dy_ref, dq_ref, dk_ref, dv_ref, dbias_ref, dsink_ref,
             kp_ref, vp_ref, dkp_ref, dvp_ref):
        b = pl.program_id(0)
        kp_ref[0:WINDOW, :] = jnp.zeros((WINDOW, LANES), BF16)
        vp_ref[0:WINDOW, :] = jnp.zeros((WINDOW, LANES), BF16)
        kp_ref[WINDOW:, :] = k_ref[...].astype(BF16)
        vp_ref[WINDOW:, :] = v_ref[...].astype(BF16)
        dkp_ref[...] = jnp.zeros_like(dkp_ref)
        dvp_ref[...] = jnp.zeros_like(dvp_ref)

        @pl.when(b == 0)
        def _():
            dbias_ref[...] = jnp.zeros_like(dbias_ref)
            dsink_ref[...] = jnp.zeros_like(dsink_ref)

        def blk(n, carry):
            r0 = pl.multiple_of(n * WINDOW, WINDOW)
            mask = _attn_mask(n)
            kband = kp_ref[pl.ds(r0, 2 * WINDOW), :]
            vband = vp_ref[pl.ds(r0, 2 * WINDOW), :]
            qb = q_ref[pl.ds(r0, WINDOW), :].astype(BF16)
            dyb = dy_ref[pl.ds(r0, WINDOW), :].astype(BF16)
            dqs = []
            dks = [None] * AKV_HEADS
            dvs = [None] * AKV_HEADS
            for hq in range(AQ_HEADS):
                kv = hq // grp
                kb = kband[:, kv * A_HD:(kv + 1) * A_HD]
                vb = vband[:, kv * A_HD:(kv + 1) * A_HD]
                qh = qb[:, hq * A_HD:(hq + 1) * A_HD]
                dyh = dyb[:, hq * A_HD:(hq + 1) * A_HD]
                probs, psink = _attn_probs(qh, kb, bias_ref[hq], sink_ref[0, hq], mask)
                dprobs = lax.dot_general(dyh, vb, (((1,), (1,)), ((), ())), preferred_element_type=F32)
                dvh = lax.dot_general(probs.astype(BF16), dyh, (((0,), (0,)), ((), ())), preferred_element_type=F32)
                rowdot = jnp.sum(probs * dprobs, axis=-1, keepdims=True)
                ds = probs * (dprobs - rowdot)
                dbias_ref[hq] += ds
                dsink_ref[hq] += jnp.sum(-psink * rowdot, axis=0, keepdims=True) + jnp.zeros((1, LANES), F32)
                dsb = ds.astype(BF16)
                dqs.append(jnp.dot(dsb, kb, preferred_element_type=F32) * scale)
                dkh = lax.dot_general(dsb, qh, (((0,), (0,)), ((), ())), preferred_element_type=F32) * scale
                dks[kv] = dkh if dks[kv] is None else dks[kv] + dkh
                dvs[kv] = dvh if dvs[kv] is None else dvs[kv] + dvh
            dq_ref[pl.ds(r0, WINDOW), :] = jnp.concatenate(dqs, axis=1).astype(BF16)
            dkp_ref[pl.ds(r0, 2 * WINDOW), :] += jnp.concatenate(dks, axis=1)
            dvp_ref[pl.ds(r0, 2 * WINDOW), :] += jnp.concatenate(dvs, axis=1)
            return carry

        lax.fori_loop(0, nb, blk, 0)
        dk_ref[...] = dkp_ref[WINDOW:, :].astype(BF16)
        dv_ref[...] = dvp_ref[WINDOW:, :].astype(BF16)

    kvs = jax.ShapeDtypeStruct((t, AKV), BF16)
    return pl.pallas_call(
        body, name="attn_bwd", grid=(bsz,),
        in_specs=[_vspec((s, AQ), lambda b: (b, C_Q // AQ)), _vspec((s, AKV), lambda b: (b, C_K // AKV)),
                  _vspec((s, AKV), lambda b: (b, C_V // AKV)),
                  _vspec((AQ_HEADS, WINDOW, 2 * WINDOW), lambda b: (0, 0, 0)), pl.BlockSpec(memory_space=pltpu.SMEM),
                  _vspec((s, AQ), lambda b: (b, 0))],
        out_specs=[_vspec((s, AQ), lambda b: (b, 0)), _vspec((s, AKV), lambda b: (b, 0)), _vspec((s, AKV), lambda b: (b, 0)),
                   _vspec((AQ_HEADS, WINDOW, 2 * WINDOW), lambda b: (0, 0, 0)), _vspec((AQ_HEADS, 1, LANES), lambda b: (0, 0, 0))],
        out_shape=[jax.ShapeDtypeStruct((t, AQ), BF16), kvs, kvs,
                   jax.ShapeDtypeStruct((AQ_HEADS, WINDOW, 2 * WINDOW), F32), jax.ShapeDtypeStruct((AQ_HEADS, 1, LANES), F32)],
        scratch_shapes=[pltpu.VMEM((s + WINDOW, LANES), BF16), pltpu.VMEM((s + WINDOW, LANES), BF16),
                        pltpu.VMEM((s + WINDOW, LANES), F32), pltpu.VMEM((s + WINDOW, LANES), F32)],
    )(proj, proj, proj, bias, sinks, dy)


def _dn_act_f(c, is_qk):
    a = jax.nn.silu(c)
    outs = []
    for h in range(DN_HEADS):
        ah = a[:, h * DN_HD:(h + 1) * DN_HD]
        nh = ah * lax.rsqrt(jnp.sum(ah * ah, axis=-1, keepdims=True) + L2_EPS)
        outs.append(jnp.where(is_qk, nh, ah))
    return jnp.concatenate(outs, axis=1)


def _dn_prep_fwd(proj, conv_w, bsz):
    t = proj.shape[0]
    s = t // bsz
    blk = _vspec((s, DN), lambda b, j: (b, j))
    wsp = _vspec((DN_CONV, DN), lambda b, j: (0, j))

    def body(x_ref, w_ref, o_ref):
        j = pl.program_id(1)
        o_ref[...] = _dn_act_f(_conv_fwd(x_ref[...], w_ref[...], DN_CONV), j < 2)

    return pl.pallas_call(body, name="dn_prep_fwd", grid=(bsz, 3), in_specs=[blk, wsp], out_specs=blk,
                          out_shape=jax.ShapeDtypeStruct((t, 3 * DN), F32))(proj, conv_w)


def _dn_prep_bwd(proj, conv_w, dqkvn, bsz):
    t = proj.shape[0]
    s = t // bsz
    blk = _vspec((s, DN), lambda j, b: (b, j))
    wsp = _vspec((DN_CONV, DN), lambda j, b: (0, j))

    def body(x_ref, w_ref, d_ref, dx_ref, dw_ref):
        j, b = pl.program_id(0), pl.program_id(1)
        x, w = x_ref[...], w_ref[...]
        c = _conv_fwd(x, w, DN_CONV)
        _, vjp = jax.vjp(lambda cc: _dn_act_f(cc, j < 2), c)
        (dc,) = vjp(d_ref[...])
        dx, dw = _conv_bwd(x, w, dc, DN_CONV)
        dx_ref[...] = dx.astype(BF16)

        @pl.when(b == 0)
        def _():
            dw_ref[...] = jnp.zeros_like(dw_ref)

        dw_ref[...] += dw

    return pl.pallas_call(
        body, name="dn_prep_bwd", grid=(3, bsz), in_specs=[blk, wsp, blk], out_specs=[blk, wsp],
        out_shape=[jax.ShapeDtypeStruct((t, 3 * DN), BF16), jax.ShapeDtypeStruct((DN_CONV, 3 * DN), F32)],
    )(proj, conv_w, dqkvn)


def _bg_f(x, alog, dt):
    lane = lax.broadcasted_iota(jnp.int32, x.shape, 1)
    beta = jax.nn.sigmoid(x)
    g = -jnp.exp(alog) * jax.nn.softplus(x + dt)
    return jnp.where(lane < DN_HEADS, beta, jnp.where(lane < 2 * DN_HEADS, g, 0.0))


def _bg_fwd(proj, alog, dt, bsz):
    t = proj.shape[0]
    s = t // bsz
    vec = _vspec((1, LANES), lambda b: (0, 0))

    def body(x_ref, a_ref, d_ref, o_ref):
        o_ref[...] = _bg_f(x_ref[...], a_ref[...], d_ref[...])

    return pl.pallas_call(body, name="bg_fwd", grid=(bsz,), in_specs=[_vspec((s, LANES), lambda b: (b, C_BD // LANES)), vec, vec],
                          out_specs=_vspec((s, LANES), lambda b: (b, 0)), out_shape=jax.ShapeDtypeStruct((t, LANES), F32))(proj, alog, dt)


def _bg_bwd(proj, alog, dt, dbg4, bsz):
    t = proj.shape[0]
    s = t // bsz
    vec = _vspec((1, LANES), lambda b: (0, 0))

    def body(x_ref, a_ref, d_ref, g4_ref, dx_ref, da_ref, dd_ref):
        b = pl.program_id(0)
        lane = lax.broadcasted_iota(jnp.int32, (s, LANES), 1)
        dbg = jnp.zeros((s, LANES), F32)
        for h in range(DN_HEADS):
            gh = g4_ref[:, h * DN_HD:(h + 1) * DN_HD]
            dbg = jnp.where(lane == h, gh[:, 0:1], dbg)
            dbg = jnp.where(lane == DN_HEADS + h, gh[:, 1:2], dbg)
        _, vjp = jax.vjp(_bg_f, x_ref[...], a_ref[...], d_ref[...])
        dx, da, dd = vjp(dbg)
        dx_ref[...] = dx.astype(BF16)

        @pl.when(b == 0)
        def _():
            da_ref[...] = jnp.zeros_like(da_ref)
            dd_ref[...] = jnp.zeros_like(dd_ref)

        da_ref[...] += da
        dd_ref[...] += dd

    return pl.pallas_call(
        body, name="bg_bwd", grid=(bsz,),
        in_specs=[_vspec((s, LANES), lambda b: (b, C_BD // LANES)), vec, vec, _vspec((s, DN), lambda b: (b, 0))],
        out_specs=[_vspec((s, LANES), lambda b: (b, 0)), vec, vec],
        out_shape=[jax.ShapeDtypeStruct((t, LANES), BF16), jax.ShapeDtypeStruct((1, LANES), F32), jax.ShapeDtypeStruct((1, LANES), F32)],
    )(proj, alog, dt, dbg4)


def _dn_out_f(o, z, w):
    outs = []
    for h in range(DN_HEADS):
        sl = slice(h * DN_HD, (h + 1) * DN_HD)
        outs.append(_rms(o[:, sl], w) * jax.nn.silu(z[:, sl]))
    return jnp.concatenate(outs, axis=1)


def _dn_out_fwd(o, proj, w, ts=512):
    t = o.shape[0]
    blk = _vspec((ts, DN), lambda i: (i, 0))
    zsp = _vspec((ts, DN), lambda i: (i, C_DZ // DN))
    vec = _vspec((1, DN_HD), lambda i: (0, 0))

    def body(o_ref, z_ref, w_ref, y_ref):
        y_ref[...] = _dn_out_f(o_ref[...], z_ref[...], w_ref[...]).astype(BF16)

    return pl.pallas_call(body, name="dn_out_fwd", grid=(t // ts,), in_specs=[blk, zsp, vec], out_specs=blk,
                          out_shape=jax.ShapeDtypeStruct((t, DN), BF16))(o, proj, w)


def _dn_out_bwd(o, proj, w, dy, ts=512):
    t = o.shape[0]
    blk = _vspec((ts, DN), lambda i: (i, 0))
    zsp = _vspec((ts, DN), lambda i: (i, C_DZ // DN))
    vec = _vspec((1, DN_HD), lambda i: (0, 0))

    def body(o_ref, z_ref, w_ref, dy_ref, do_ref, dz_ref, dw_ref):
        i = pl.program_id(0)
        _, vjp = jax.vjp(_dn_out_f, o_ref[...], z_ref[...], w_ref[...])
        do, dz, dw = vjp(dy_ref[...])
        do_ref[...] = do
        dz_ref[...] = dz.astype(BF16)

        @pl.when(i == 0)
        def _():
            dw_ref[...] = jnp.zeros_like(dw_ref)

        dw_ref[...] += dw

    return pl.pallas_call(
        body, name="dn_out_bwd", grid=(t // ts,), in_specs=[blk, zsp, vec, blk], out_specs=[blk, blk, vec],
        out_shape=[jax.ShapeDtypeStruct((t, DN), F32), jax.ShapeDtypeStruct((t, DN), BF16), jax.ShapeDtypeStruct((1, DN_HD), F32)],
    )(o, proj, w, dy)


_C = DN_CHUNK


def _dot(a, b, dims):
    return lax.dot_general(a.astype(BF16), b.astype(BF16), dims, preferred_element_type=F32)


_NN = (((1,), (0,)), ((), ()))
_NT = (((1,), (1,)), ((), ()))
_TN = (((0,), (0,)), ((), ()))


_SUB = 8


def _tri_inverse(lt):
    ri = lax.broadcasted_iota(jnp.int32, (_SUB, _C), 0)
    ci = lax.broadcasted_iota(jnp.int32, (_SUB, _C), 1)
    blocks = [jnp.where(ci == ri + _SUB * b, 1.0, 0.0).astype(F32) for b in range(_C // _SUB)]
    for i in range(1, _C):
        acc = None
        for b in range((i + _SUB - 1) // _SUB):
            term = lt[_SUB * b:_SUB * (b + 1), i:i + 1] * blocks[b]
            acc = term if acc is None else acc + term
        row = jnp.sum(acc, axis=0, keepdims=True)
        bi, r = divmod(i, _SUB)
        blocks[bi] = jnp.where(ri == r, blocks[bi] - row, blocks[bi])
    return jnp.concatenate(blocks, axis=0)


_SEG = 512
_HEADS = tuple(range(DN_HEADS))


def _hsl(hh):
    return slice(hh * DN_HD, (hh + 1) * DN_HD)


def _chunk_specs(nseg, reverse):
    seg = (lambda i: nseg - 1 - i) if reverse else (lambda i: i)
    ncs = _SEG // _C
    col = lambda off: _vspec((_SEG, DN), lambda b, i: (b * nseg + seg(i), off))
    return (col, _vspec((_SEG, LANES), lambda b, i: (b * nseg + seg(i), 0)),
            _vspec((1, DN_HEADS, ncs, _C), lambda b, i: (b, 0, seg(i), 0)),
            _vspec((1, DN_HEADS, ncs, DN_HD, DN_HD), lambda b, i: (b, 0, seg(i), 0, 0)))


def _chunk_pre(q_ref, k_ref, v_ref, bg_ref, gr_ref, c, hh):
    r0 = pl.multiple_of(c * _C, _C)
    ri = lax.broadcasted_iota(jnp.int32, (_C, _C), 0)
    ci = lax.broadcasted_iota(jnp.int32, (_C, _C), 1)
    q = q_ref[pl.ds(r0, _C), _hsl(hh)] * (DN_HD ** -0.5)
    k = k_ref[pl.ds(r0, _C), _hsl(hh)]
    v = v_ref[pl.ds(r0, _C), _hsl(hh)]
    bgc = bg_ref[pl.ds(r0, _C), :]
    beta = bgc[:, hh:hh + 1]
    g_col = bgc[:, DN_HEADS + hh:DN_HEADS + hh + 1]
    g_row = gr_ref[0, hh, pl.ds(c, 1), :]
    gc_col = jnp.sum(jnp.where(ri >= ci, g_row, 0.0), axis=1, keepdims=True)
    gc_row = jnp.sum(jnp.where(ri <= ci, g_col, 0.0), axis=0, keepdims=True)
    gc_last = jnp.sum(g_col, axis=0, keepdims=True)
    diff = gc_col - gc_row
    decay = jnp.where(ri >= ci, jnp.exp(jnp.where(ri >= ci, diff, 0.0)), 0.0)
    diff_t = gc_row - gc_col
    decay_t = jnp.where(ri <= ci, jnp.exp(jnp.where(ri <= ci, diff_t, 0.0)), 0.0)
    eg = jnp.exp(gc_col)
    et = jnp.exp(gc_last - gc_col)
    gl = jnp.exp(gc_last)
    kb = k * beta
    vb = v * beta
    return dict(r0=r0, q=q, k=k, v=v, beta=beta, decay=decay, decay_t=decay_t, eg=eg, et=et, gl=gl, kb=kb, vb=vb, ri=ri, ci=ci)


def _chunk_solve(ms, with_transposes):
    for m in ms:
        m["kk_t"] = _dot(m["k"], m["kb"], _NT)
        m["qk"] = _dot(m["q"], m["k"], _NT)
        if with_transposes:
            m["kk"] = _dot(m["kb"], m["k"], _NT)
            m["qk_t"] = _dot(m["k"], m["q"], _NT)
    for m in ms:
        m["tm"] = _tri_inverse(jnp.where(m["ri"] < m["ci"], m["kk_t"] * m["decay_t"], 0.0))
    for m in ms:
        rhs = jnp.concatenate([m["vb"], m["kb"] * m["eg"]], axis=1)
        m["sol"] = jnp.dot(m["tm"], rhs, precision=HI, preferred_element_type=F32)
        m["intra"] = jnp.where(m["ri"] >= m["ci"], m["qk"] * m["decay"], 0.0)


def _dn_chunk_fwd(qkvn, bg, g_rows, bsz):
    t = qkvn.shape[0]
    s = t // bsz
    nc, nseg = s // _C, s // _SEG

    def body(q_ref, k_ref, v_ref, bg_ref, gr_ref, o_ref, st_ref, s_ref):
        @pl.when(pl.program_id(1) == 0)
        def _():
            s_ref[...] = jnp.zeros_like(s_ref)

        def chunk(c, carry):
            ms = [_chunk_pre(q_ref, k_ref, v_ref, bg_ref, gr_ref, c, hh) for hh in _HEADS]
            _chunk_solve(ms, False)
            sts = [s_ref[hh] for hh in _HEADS]
            for hh in _HEADS:
                st_ref[0, hh, c] = sts[hh]
            ws = [_dot(m["sol"][:, DN_HD:], st, _NN) for m, st in zip(ms, sts)]
            qs = [_dot(m["q"] * m["eg"], st, _NN) for m, st in zip(ms, sts)]
            v_new = [m["sol"][:, :DN_HD] - a for m, a in zip(ms, ws)]
            iv = [_dot(m["intra"], vn, _NN) for m, vn in zip(ms, v_new)]
            upd = [_dot(m["k"] * m["et"], vn, _TN) for m, vn in zip(ms, v_new)]
            for hh in _HEADS:
                s_ref[hh] = sts[hh] * ms[hh]["gl"] + upd[hh]
                o_ref[pl.ds(ms[hh]["r0"], _C), _hsl(hh)] = qs[hh] + iv[hh]
            return carry

        lax.fori_loop(0, _SEG // _C, chunk, 0)

    col, bgs, grs, sts_spec = _chunk_specs(nseg, False)
    return pl.pallas_call(
        body, name="dn_chunk_fwd", grid=(bsz, nseg),
        in_specs=[col(0), col(1), col(2), bgs, grs], out_specs=[col(0), sts_spec],
        out_shape=[jax.ShapeDtypeStruct((t, DN), F32), jax.ShapeDtypeStruct((bsz, DN_HEADS, nc, DN_HD, DN_HD), F32)],
        scratch_shapes=[pltpu.VMEM((DN_HEADS, DN_HD, DN_HD), F32)],
    )(qkvn, qkvn, qkvn, bg, g_rows)


def _dn_chunk_bwd(qkvn, bg, g_rows, states, do, bsz):
    t = qkvn.shape[0]
    s = t // bsz
    nc, nseg = s // _C, s // _SEG

    def body(q_ref, k_ref, v_ref, bg_ref, gr_ref, st_ref, do_ref, dq_ref, dk_ref, dv_ref, dbg_ref, ds_ref):
        @pl.when(pl.program_id(1) == 0)
        def _():
            ds_ref[...] = jnp.zeros_like(ds_ref)

        def chunk(cc, carry):
            c = _SEG // _C - 1 - cc
            ms = [_chunk_pre(q_ref, k_ref, v_ref, bg_ref, gr_ref, c, hh) for hh in _HEADS]
            _chunk_solve(ms, True)
            ri, ci = ms[0]["ri"], ms[0]["ci"]
            for hh, m in enumerate(ms):
                m["st"] = st_ref[0, hh, c]
                m["ds_out"] = ds_ref[hh]
                m["do"] = do_ref[pl.ds(m["r0"], _C), _hsl(hh)]
                m["w"] = m["sol"][:, DN_HD:]
            for m in ms:
                m["v_new"] = m["sol"][:, :DN_HD] - _dot(m["w"], m["st"], _NN)
            for m in ms:
                m["q_dec"], m["k_tail"] = m["q"] * m["eg"], m["k"] * m["et"]
                m["dk_tail"] = _dot(m["v_new"], m["ds_out"], _NT)
                m["dv_new"] = _dot(m["k_tail"], m["ds_out"], _NN) + _dot(m["intra"], m["do"], _TN)
                m["dq_dec"] = _dot(m["do"], m["st"], _NT)
                m["ds_in"] = m["ds_out"] * m["gl"] + _dot(m["q_dec"], m["do"], _TN)
                m["dintra"] = jnp.where(ri >= ci, _dot(m["do"], m["v_new"], _NT), 0.0)
                m["dintra_t"] = jnp.where(ri <= ci, _dot(m["v_new"], m["do"], _NT), 0.0)
            for m in ms:
                m["dw"] = -_dot(m["dv_new"], m["st"], _NT)
                m["ds_in"] = m["ds_in"] - _dot(m["w"], m["dv_new"], _TN)
            for m in ms:
                dsol = jnp.concatenate([m["dv_new"], m["dw"]], axis=1)
                m["drhs"] = lax.dot_general(m["tm"], dsol, _TN, precision=HI, preferred_element_type=F32)
            for m in ms:
                m["dl"] = jnp.where(ri > ci, -_dot(m["drhs"], m["sol"], _NT), 0.0)
                m["dl_t"] = jnp.where(ri < ci, -_dot(m["sol"], m["drhs"], _NT), 0.0)
            for m in ms:
                m["dkb2"] = _dot(m["dl"] * m["decay"], m["k"], _NN)
                m["dk"] = _dot(m["dl_t"] * m["decay_t"], m["kb"], _NN) + _dot(m["dintra_t"] * m["decay_t"], m["q"], _NN)
                m["dq"] = _dot(m["dintra"] * m["decay"], m["k"], _NN)
            for m in ms:
                _chunk_bwd_finish(m)
            for m in ms:
                m["dg_b"] = jnp.dot(jnp.where(ri <= ci, 1.0, 0.0).astype(F32), m["dgc"] + jnp.zeros((_C, LANES), F32),
                                    precision=HI, preferred_element_type=F32)
            lane = lax.broadcasted_iota(jnp.int32, (_C, LANES), 1)
            for hh, m in enumerate(ms):
                rows = pl.ds(m["r0"], _C)
                dq_ref[rows, _hsl(hh)] = m["dq"] * (DN_HD ** -0.5)
                dk_ref[rows, _hsl(hh)] = m["dk"]
                dv_ref[rows, _hsl(hh)] = m["dv"]
                dbg_ref[rows, _hsl(hh)] = jnp.where(lane == 0, m["dbeta"], jnp.where(lane == 1, m["dg_b"], 0.0))
                ds_ref[hh] = m["ds_in"]
            return carry

        lax.fori_loop(0, _SEG // _C, chunk, 0)

    col, bgs, grs, sts_spec = _chunk_specs(nseg, True)
    o = jax.ShapeDtypeStruct((t, DN), F32)
    return pl.pallas_call(
        body, name="dn_chunk_bwd", grid=(bsz, nseg),
        in_specs=[col(0), col(1), col(2), bgs, grs, sts_spec, col(0)], out_specs=[col(0)] * 4, out_shape=[o] * 4,
        scratch_shapes=[pltpu.VMEM((DN_HEADS, DN_HD, DN_HD), F32)],
    )(qkvn, qkvn, qkvn, bg, g_rows, states, do)


def _chunk_bwd_finish(m):
    q, k, v, beta, decay, decay_t = m["q"], m["k"], m["v"], m["beta"], m["decay"], m["decay_t"]
    eg, et, gl, kb, dl, dl_t, dintra, dintra_t = m["eg"], m["et"], m["gl"], m["kb"], m["dl"], m["dl_t"], m["dintra"], m["dintra_t"]
    dq_dec, dk_tail, dq, dk = m["dq_dec"], m["dk_tail"], m["dq"], m["dk"]
    dgl = jnp.sum(jnp.sum(m["ds_out"] * m["st"], axis=1, keepdims=True), axis=0, keepdims=True)
    dvb, dkbeg = m["drhs"][:, :DN_HD], m["drhs"][:, DN_HD:]
    dkb = dkbeg * eg + m["dkb2"]
    deg = jnp.sum(dkbeg * kb, axis=1, keepdims=True)
    em = (dl * m["kk"] + dintra * m["qk"]) * decay
    em_t = (dl_t * m["kk_t"] + dintra_t * m["qk_t"]) * decay_t
    dgc = jnp.sum(em, axis=1, keepdims=True) - jnp.sum(em_t, axis=1, keepdims=True)
    dq = dq + dq_dec * eg
    deg = deg + jnp.sum(dq_dec * q, axis=1, keepdims=True)
    dk = dk + dk_tail * et
    det = jnp.sum(dk_tail * k, axis=1, keepdims=True)
    dgc = dgc + deg * eg - det * et
    dgc_last = jnp.sum(det * et, axis=0, keepdims=True) + dgl * gl
    rcol = lax.broadcasted_iota(jnp.int32, (_C, 1), 0)
    m["dgc"] = dgc + jnp.where(rcol == _C - 1, dgc_last, 0.0)
    m["dq"] = dq
    m["dk"] = dk + dkb * beta
    m["dbeta"] = jnp.sum(dkb * k, axis=1, keepdims=True) + jnp.sum(dvb * v, axis=1, keepdims=True)
    m["dv"] = dvb * beta


def _mod_fwd(c_all, ada_w_loc, ada_b_loc):
    n, cols = c_all.shape[0], ada_w_loc.shape[1]

    def body(c_ref, w_ref, b_ref, o_ref):
        o_ref[...] = _dot(jax.nn.silu(c_ref[...]), w_ref[...], _NN) + b_ref[...]

    return pl.pallas_call(body, name="mod_fwd", out_shape=jax.ShapeDtypeStruct((n, cols), F32))(c_all, ada_w_loc, ada_b_loc)


def _ada_grad(c_all, dmod_loc, dmod_all):
    d, cols = c_all.shape[1], dmod_loc.shape[1]

    def body(c_ref, dl_ref, da_ref, gw_ref, gb_ref):
        gw_ref[...] = _dot(jax.nn.silu(c_ref[...]), dl_ref[...], _TN)
        gb_ref[...] = jnp.sum(da_ref[...], axis=0, keepdims=True)

    return pl.pallas_call(body, name="ada_grad", out_shape=[jax.ShapeDtypeStruct((d, cols), F32),
                                                           jax.ShapeDtypeStruct((1, dmod_all.shape[1]), F32)])(c_all, dmod_loc, dmod_all)


def _row_tile(r):
    for tr in (256, 128, 64, 32, 16, 8):
        if r % tr == 0:
            return tr
    return r


def _adamw(w, m, v, grads, name):
    r, c = w.shape
    tr = _row_tile(r)
    blk = _vspec((tr, c), lambda i: (i, 0))
    n = len(grads)

    def body(*refs):
        w_ref, m_ref, v_ref = refs[:3]
        g_ref, d_ref, mo_ref, vo_ref = refs[3 + n:]
        g = refs[3][...]
        for p in refs[4:3 + n]:
            g = g + p[...]
        m_new = ADAM_B1 * m_ref[...] + (1.0 - ADAM_B1) * g
        v_new = ADAM_B2 * v_ref[...] + (1.0 - ADAM_B2) * jnp.square(g)
        m_hat = m_new / (1.0 - ADAM_B1 ** ADAM_STEP)
        v_hat = v_new / (1.0 - ADAM_B2 ** ADAM_STEP)
        g_ref[...] = g
        d_ref[...] = -ADAM_LR * (m_hat / (jnp.sqrt(v_hat) + ADAM_EPS) + ADAM_WD * w_ref[...])
        mo_ref[...] = m_new
        vo_ref[...] = v_new

    o = jax.ShapeDtypeStruct((r, c), F32)
    return pl.pallas_call(body, name=name, grid=(r // tr,), in_specs=[blk] * (3 + n), out_specs=[blk] * 4,
                          out_shape=[o] * 4)(w, m, v, *grads)


def _sum_lead(x, name):
    p, r, c = x.shape
    tr = _row_tile(r)

    def body(x_ref, o_ref):
        acc = x_ref[0].astype(F32)
        for i in range(1, p):
            acc = acc + x_ref[i].astype(F32)
        o_ref[...] = acc

    return pl.pallas_call(body, name=name, grid=(r // tr,), in_specs=[_vspec((p, tr, c), lambda i: (0, i, 0))],
                          out_specs=_vspec((tr, c), lambda i: (i, 0)), out_shape=jax.ShapeDtypeStruct((r, c), F32))(x)


def _allgather8(x_shard, name):
    m_per, n = x_shard.shape

    def body(x_ref, out_ref, send_sems, recv_sems, local_sem):
        x, y, c = lax.axis_index("x"), lax.axis_index("y"), lax.axis_index("c")
        me, sibling = (x, y, c), (x, y, 1 - c)
        chips = [(1 - x, y), (x, 1 - y), (1 - x, 1 - y)]

        def rows(px, py, pc):
            return out_ref.at[pl.ds((4 * px + 2 * py + pc) * m_per, m_per), :]

        def copy(k, block, to, src=None):
            return pltpu.make_async_remote_copy(
                src_ref=rows(*block) if src is None else src, dst_ref=rows(*block), send_sem=send_sems.at[k],
                recv_sem=recv_sems.at[k], device_id=to, device_id_type=MESH)

        mine = pltpu.make_async_copy(x_ref, rows(*me), local_sem)
        mine.start()
        first = [copy(0, me, sibling, src=x_ref)]
        first += [copy(1 + j, me, (*chip, c), src=x_ref) for j, chip in enumerate(chips)]
        for cp in first:
            cp.start()
        passed = [copy(4 + j, (*chip, c), sibling) for j, chip in enumerate(chips)]
        for j, chip in enumerate(chips):
            copy(1 + j, (*chip, c), me).wait_recv()
            passed[j].start()
        copy(0, sibling, me).wait_recv()
        for j, chip in enumerate(chips):
            copy(4 + j, (*chip, 1 - c), me).wait_recv()
        for cp in first + passed:
            cp.wait_send()
        mine.wait()

    return pl.pallas_call(
        body, name=name, out_shape=jax.ShapeDtypeStruct((8 * m_per, n), x_shard.dtype),
        in_specs=[pl.BlockSpec(memory_space=pltpu.VMEM)], out_specs=pl.BlockSpec(memory_space=pltpu.VMEM),
        scratch_shapes=[pltpu.SemaphoreType.DMA((7,)), pltpu.SemaphoreType.DMA((7,)), pltpu.SemaphoreType.DMA],
    )(x_shard)


_HBM = pl.BlockSpec(memory_space=pltpu.HBM)


def _chip_gather(shards, name):
    n = len(shards)

    def body(*refs):
        ins, outs = refs[:n], refs[n:2 * n]
        send_sems, recv_sems, local_sems = refs[2 * n:]
        x, y, c = lax.axis_index("x"), lax.axis_index("y"), lax.axis_index("c")
        me = 2 * x + y
        chips = [(1 - x, y), (x, 1 - y), (1 - x, 1 - y)]
        sends, locs = [], []
        for i in range(n):
            loc = pltpu.make_async_copy(ins[i], outs[i].at[me], local_sems.at[i])
            loc.start()
            locs.append(loc)
            for j, (px, py) in enumerate(chips):
                cp = pltpu.make_async_remote_copy(src_ref=ins[i], dst_ref=outs[i].at[me], send_sem=send_sems.at[3 * i + j],
                                                  recv_sem=recv_sems.at[3 * i + j], device_id=(px, py, c), device_id_type=MESH)
                cp.start()
                sends.append(cp)
        for i in range(n):
            for j, (px, py) in enumerate(chips):
                pltpu.make_async_remote_copy(src_ref=ins[i], dst_ref=outs[i].at[2 * px + py], send_sem=send_sems.at[3 * i + j],
                                             recv_sem=recv_sems.at[3 * i + j], device_id=(px, py, c), device_id_type=MESH).wait_recv()
        for cp in sends:
            cp.wait_send()
        for loc in locs:
            loc.wait()

    return pl.pallas_call(
        body, name=name, out_shape=[jax.ShapeDtypeStruct((N_CHIPS,) + a.shape, a.dtype) for a in shards],
        in_specs=[_HBM] * n, out_specs=[_HBM] * n,
        scratch_shapes=[pltpu.SemaphoreType.DMA((3 * n,)), pltpu.SemaphoreType.DMA((3 * n,)), pltpu.SemaphoreType.DMA((n,))],
    )(*shards)


def _piece_exchange(pieces, name):
    n = len(pieces)

    def body(*refs):
        ins, outs = refs[:n], refs[n:2 * n]
        send_sems, recv_sems, local_sems = refs[2 * n:]
        x, y, c = lax.axis_index("x"), lax.axis_index("y"), lax.axis_index("c")
        me = 2 * x + y
        chips = [(1 - x, y), (x, 1 - y), (1 - x, 1 - y)]
        sends, locs = [], []
        for i in range(n):
            loc = pltpu.make_async_copy(ins[i].at[me], outs[i].at[me], local_sems.at[i])
            loc.start()
            locs.append(loc)
            for j, (px, py) in enumerate(chips):
                cp = pltpu.make_async_remote_copy(src_ref=ins[i].at[2 * px + py], dst_ref=outs[i].at[me],
                                                  send_sem=send_sems.at[3 * i + j], recv_sem=recv_sems.at[3 * i + j],
                                                  device_id=(px, py, c), device_id_type=MESH)
                cp.start()
                sends.append(cp)
        for i in range(n):
            for j, (px, py) in enumerate(chips):
                pltpu.make_async_remote_copy(src_ref=ins[i].at[me], dst_ref=outs[i].at[2 * px + py], send_sem=send_sems.at[3 * i + j],
                                             recv_sem=recv_sems.at[3 * i + j], device_id=(px, py, c), device_id_type=MESH).wait_recv()
        for cp in sends:
            cp.wait_send()
        for loc in locs:
            loc.wait()

    return pl.pallas_call(
        body, name=name, out_shape=[jax.ShapeDtypeStruct(a.shape, a.dtype) for a in pieces],
        in_specs=[_HBM] * n, out_specs=[_HBM] * n,
        scratch_shapes=[pltpu.SemaphoreType.DMA((3 * n,)), pltpu.SemaphoreType.DMA((3 * n,)), pltpu.SemaphoreType.DMA((n,))],
    )(*pieces)


def _sibling_exchange(arrs, name):
    n = len(arrs)

    def body(*refs):
        ins, outs = refs[:n], refs[n:2 * n]
        send_sems, recv_sems = refs[2 * n:]
        sibling = (lax.axis_index("x"), lax.axis_index("y"), 1 - lax.axis_index("c"))
        cps = [pltpu.make_async_remote_copy(src_ref=ins[i], dst_ref=outs[i], send_sem=send_sems.at[i], recv_sem=recv_sems.at[i],
                                            device_id=sibling, device_id_type=MESH) for i in range(n)]
        for cp in cps:
            cp.start()
        for cp in cps:
            cp.wait()

    return pl.pallas_call(
        body, name=name, out_shape=[jax.ShapeDtypeStruct(a.shape, a.dtype) for a in arrs], in_specs=[_HBM] * n, out_specs=[_HBM] * n,
        scratch_shapes=[pltpu.SemaphoreType.DMA((n,)), pltpu.SemaphoreType.DMA((n,))],
    )(*arrs)


def _to_padded(w):
    z = jnp.zeros((w.shape[0], IN_PAD - IN_DIM), w.dtype)
    return jnp.concatenate([w[:, 768:2304], w[:, 0:512], w[:, 2304:2816], w[:, 2824:3848], w[:, 3848:4872],
                            w[:, 512:640], w[:, 640:768], w[:, 2816:2824], z], axis=1)


def _from_padded(g):
    return jnp.concatenate([g[:, C_Q:C_Q + AQ], g[:, C_K:C_K + AKV], g[:, C_V:C_V + AKV], g[:, C_DQKV:C_DQKV + 3 * DN],
                            g[:, C_DZ:C_DZ + DN], g[:, C_BD:C_BD + 2 * DN_HEADS], g[:, C_GA:C_GA + D_MODEL],
                            g[:, C_GD:C_GD + D_MODEL]], axis=1)


def _lane_vec(a):
    return jnp.zeros((1, LANES), F32).at[0, DN_HEADS:2 * DN_HEADS].set(a)


def _device_step(x2, tgt2, mod, p, bsz):
    d = D_MODEL
    sh1, sc1, g1, sh2, sc2, g2 = [mod[:, i * d:(i + 1) * d].reshape(bsz, 1, d) for i in range(N_MOD)]
    alog_v, dt_v = _lane_vec(p["dn_a_log"]), _lane_vec(p["dn_dt_bias"])
    sinks = p["attn_sinks"].reshape(1, AQ_HEADS)
    u1 = _pre_fwd(x2, p["norm_mix_pre"], sc1, sh1, "pre1_fwd")
    proj = _mm(u1, p["w_in"], "nn", F32, "mm_proj")
    bias = _bias_build(p["rel_bias"])
    y_attn = _attn_fwd(proj, bias, sinks, bsz)
    qkvn = _dn_prep_fwd(proj, p["dn_conv_w"], bsz)
    bg = _bg_fwd(proj, alog_v, dt_v, bsz)
    nc = x2.shape[0] // bsz // DN_CHUNK
    g_rows = jnp.transpose(bg[:, DN_HEADS:2 * DN_HEADS].reshape(bsz, nc, DN_CHUNK, DN_HEADS), (0, 3, 1, 2))
    o, states = _dn_chunk_fwd(qkvn, bg, g_rows, bsz)
    y_dn = _dn_out_fwd(o, proj, p["dn_norm_w"])
    ya = _mm(y_attn, p["w_attn_branch"], "nn", F32, "mm_ya")
    yd = _mm(y_dn, p["w_dn_branch"], "nn", F32, "mm_yd")
    merged = _merge_fwd(proj, ya, yd)
    y1 = _mm(merged, p["w_out"], "nn", F32, "mm_y1")
    h1 = _post_fwd(x2, y1, p["norm_mix_post"], g1, "post1_fwd")
    u2 = _pre_fwd(h1, p["norm_ffn_pre"], sc2, sh2, "pre2_fwd")
    up = _mm(u2, p["ffn_w_up"], "nn", F32, "mm_up")
    act = _ffn_act_fwd(up, p["ffn_conv_w"], bsz)
    y2 = _mm(act, p["ffn_w_down"], "nn", F32, "mm_y2")
    dh2, sq = _post_loss(h1, y2, p["norm_ffn_post"], g2, tgt2, "post2_loss")
    g = {}
    dy2, g["norm_ffn_post"], dg2 = _post_bwd(dh2, y2, p["norm_ffn_post"], g2, "post2_bwd")
    dact = _mm(dy2, p["ffn_w_down"], "nt", F32, "mm_dact")
    g["ffn_w_down"] = _mm(act, dy2, "tn", F32, "mm_dwdown")
    dupg, dupv, dcwg, dcwv = _ffn_act_bwd(up, p["ffn_conv_w"], dact, bsz)
    g["ffn_conv_w"] = jnp.concatenate([dcwg, dcwv], axis=1)
    dup = jnp.concatenate([dupg, dupv], axis=1)
    g["ffn_w_up"] = _mm(u2, dup, "tn", F32, "mm_dwup")
    du2 = _mm(dup, p["ffn_w_up"], "nt", F32, "mm_du2")
    dh1, g["norm_ffn_pre"], dsc2, dsh2 = _pre_bwd(h1, p["norm_ffn_pre"], sc2, sh2, du2, dh2, "pre2_bwd")
    dy1, g["norm_mix_post"], dg1 = _post_bwd(dh1, y1, p["norm_mix_post"], g1, "post1_bwd")
    dmerged = _mm(dy1, p["w_out"], "nt", F32, "mm_dmerged")
    g["w_out"] = _mm(merged, dy1, "tn", F32, "mm_dwout")
    dga, dgd, dya, dyd = _merge_bwd(proj, ya, yd, dmerged)
    dy_attn = _mm(dya, p["w_attn_branch"], "nt", BF16, "mm_dyattn")
    g["w_attn_branch"] = _mm(y_attn, dya, "tn", F32, "mm_dwab")
    dy_dn = _mm(dyd, p["w_dn_branch"], "nt", F32, "mm_dydn")
    g["w_dn_branch"] = _mm(y_dn, dyd, "tn", F32, "mm_dwdb")
    do, dz, g["dn_norm_w"] = _dn_out_bwd(o, proj, p["dn_norm_w"], dy_dn)
    dqn, dkn, dvv, dbg4 = _dn_chunk_bwd(qkvn, bg, g_rows, states, do, bsz)
    d_dqkv, g["dn_conv_w"] = _dn_prep_bwd(proj, p["dn_conv_w"], jnp.concatenate([dqn, dkn, dvv], axis=1), bsz)
    dbd, g["dn_a_log"], g["dn_dt_bias"] = _bg_bwd(proj, alog_v, dt_v, dbg4, bsz)
    dq, dk, dv, dbias, g["attn_sinks"] = _attn_bwd(proj, bias, sinks, dy_attn, bsz)
    g["rel_bias"] = _bias_grad(dbias)
    dproj = jnp.concatenate([d_dqkv, dq, dz, dga, dgd, dk, dv, dbd], axis=1)
    g["w_in"] = _mm(u1, dproj, "tn", F32, "mm_dwin")
    du1 = _mm(dproj, p["w_in"], "nt", F32, "mm_du1")
    dx, g["norm_mix_pre"], dsc1, dsh1 = _pre_bwd(x2, p["norm_mix_pre"], sc1, sh1, du1, dh1, "pre1_bwd")
    dmod = jnp.concatenate([dsh1, dsc1, dg1, dsh2, dsc2, dg2], axis=-1).reshape(bsz, N_MOD * d)
    return sq, dx, dmod, g


_SMALL = (("norm_mix_pre", D_MODEL), ("norm_mix_post", D_MODEL), ("norm_ffn_pre", D_MODEL), ("norm_ffn_post", D_MODEL),
          ("dn_norm_w", DN_HD), ("dn_a_log", LANES), ("dn_dt_bias", LANES), ("attn_sinks", AQ_HEADS * LANES),
          ("rel_bias", AQ_HEADS * LANES), ("dn_conv_w", DN_CONV * 3 * DN), ("ffn_conv_w", FFN_CONV * 2 * D_FF))


def _pack_rows(parts, rows):
    flat = jnp.concatenate([a.reshape(-1) for a in parts])
    return jnp.concatenate([flat, jnp.zeros((rows * LANES - flat.shape[0],), F32)]).reshape(rows, LANES)


def _pad128(a):
    flat = a.reshape(-1)
    n = -(-flat.shape[0] // LANES) * LANES
    return jnp.concatenate([flat, jnp.zeros((n - flat.shape[0],), F32)]) if n != flat.shape[0] else flat


_W_NAMES = ("ada_w", "ada_b", "norm_mix_pre", "norm_mix_post", "norm_ffn_pre", "norm_ffn_post", "w_in", "dn_conv_w", "dn_a_log",
            "dn_dt_bias", "dn_norm_w", "attn_sinks", "rel_bias", "w_attn_branch", "w_dn_branch", "w_out", "ffn_w_up", "ffn_conv_w",
            "ffn_w_down")
_BIG = ("w_in", "w_attn_branch", "w_dn_branch", "w_out", "ffn_w_up", "ffn_w_down")
_ROW_SHARDED = ("w_out", "ffn_w_down")


def kernel(x, c, *rest):
    nw = len(_W_NAMES)
    w = dict(zip(_W_NAMES, rest[:nw]))
    loss_target = rest[nw]
    m = dict(zip(_W_NAMES, rest[nw + 1:2 * nw + 1]))
    v = dict(zip(_W_NAMES, rest[2 * nw + 1:3 * nw + 1]))
    ix, iy, ic = lax.axis_index("x"), lax.axis_index("y"), lax.axis_index("c")
    chip, dev = 2 * ix + iy, 4 * ix + 2 * iy + ic
    bsz, s, d = x.shape
    t = bsz * s
    n_dev = 8

    front_rows = 64
    front = _pack_rows([c, w["dn_conv_w"], w["ffn_conv_w"]], front_rows)
    front_all = _allgather8(front, "ag_front").reshape(n_dev, front_rows * LANES)
    n_c, n_dc, n_fc = bsz * d, DN_CONV * 3 * DN // N_CHIPS, FFN_CONV * 2 * D_FF // N_CHIPS
    c_all = front_all[:, :n_c].reshape(n_dev * bsz, d)
    per_chip = front_all[0::2]
    dn_conv_full = jnp.transpose(per_chip[:, n_c:n_c + n_dc].reshape(N_CHIPS, DN_CONV, -1), (1, 0, 2)).reshape(DN_CONV, 3 * DN)
    ffn_conv_full = jnp.transpose(per_chip[:, n_c + n_dc:n_c + n_dc + n_fc].reshape(N_CHIPS, FFN_CONV, -1), (1, 0, 2)).reshape(FFN_CONV, 2 * D_FF)

    mod_cols = N_MOD * d // N_CHIPS
    ada_b_loc = lax.dynamic_slice(w["ada_b"], (0, chip * mod_cols), (1, mod_cols))
    mod_part = _mod_fwd(c_all, w["ada_w"][0], ada_b_loc)
    mod_all = _allgather8(mod_part, "ag_mod").reshape(n_dev, n_dev * bsz, mod_cols)[0::2]
    mod = jnp.transpose(lax.dynamic_slice(mod_all, (0, dev * bsz, 0), (N_CHIPS, bsz, mod_cols)), (1, 0, 2)).reshape(bsz, N_MOD * d)

    gathered = _chip_gather([w[k][0].astype(BF16) for k in _BIG], "gather_weights")
    p = {}
    for k, a in zip(_BIG, gathered):
        if k in _ROW_SHARDED:
            p[k] = a.reshape(-1, a.shape[2])
        else:
            p[k] = jnp.transpose(a, (1, 0, 2)).reshape(a.shape[1], -1)
    p["w_in"] = _to_padded(p["w_in"])
    for k in ("norm_mix_pre", "norm_mix_post", "norm_ffn_pre", "norm_ffn_post", "dn_norm_w", "attn_sinks"):
        p[k] = w[k]
    p["dn_a_log"], p["dn_dt_bias"], p["rel_bias"] = w["dn_a_log"][0], w["dn_dt_bias"][0], w["rel_bias"]
    p["dn_conv_w"], p["ffn_conv_w"] = dn_conv_full, ffn_conv_full

    sq, dx, dmod, g = _device_step(x.reshape(t, d), loss_target.reshape(t, d), mod, p, bsz)
    loss = lax.psum(0.5 * jnp.sum(sq), ("x", "y", "c"))

    g["dn_a_log"], g["dn_dt_bias"] = g["dn_a_log"].reshape(-1), g["dn_dt_bias"].reshape(-1)
    small_rows = 328
    small = _pack_rows([dmod] + [g[k] for k, _ in _SMALL], small_rows)
    small_all = _allgather8(small, "ag_small").reshape(n_dev, small_rows, LANES)
    n_dm = bsz * N_MOD * d
    dmod_all = small_all.reshape(n_dev, -1)[:, :n_dm].reshape(n_dev * bsz, N_MOD * d)
    tot = _sum_lead(small_all, "sum_small").reshape(-1)
    gs, off = {}, n_dm
    for k, n in _SMALL:
        gs[k] = tot[off:off + n]
        off += n
    grad = {}
    grad["ada_w"], grad["ada_b"] = _ada_grad(c_all, lax.dynamic_slice(dmod_all, (0, chip * mod_cols), (n_dev * bsz, mod_cols)), dmod_all)
    for k in ("norm_mix_pre", "norm_mix_post", "norm_ffn_pre", "norm_ffn_post", "dn_norm_w"):
        grad[k] = gs[k]
    grad["dn_a_log"] = gs["dn_a_log"][DN_HEADS:2 * DN_HEADS]
    grad["dn_dt_bias"] = gs["dn_dt_bias"][DN_HEADS:2 * DN_HEADS]
    grad["attn_sinks"] = gs["attn_sinks"].reshape(AQ_HEADS, LANES)[:, 0]
    grad["rel_bias"] = gs["rel_bias"].reshape(AQ_HEADS, LANES)[:, :REL_BUCKETS].T
    grad["dn_conv_w"] = lax.dynamic_slice(gs["dn_conv_w"].reshape(DN_CONV, 3 * DN), (0, chip * (3 * DN // N_CHIPS)), (DN_CONV, 3 * DN // N_CHIPS))
    grad["ffn_conv_w"] = lax.dynamic_slice(gs["ffn_conv_w"].reshape(FFN_CONV, 2 * D_FF), (0, chip * (2 * D_FF // N_CHIPS)), (FFN_CONV, 2 * D_FF // N_CHIPS))

    g["w_in"] = _from_padded(g["w_in"])
    pieces = []
    for k in _BIG:
        a = g[k]
        if k in _ROW_SHARDED:
            pieces.append(a.reshape(N_CHIPS, a.shape[0] // N_CHIPS, a.shape[1]).astype(BF16))
        else:
            pieces.append(jnp.transpose(a.reshape(a.shape[0], N_CHIPS, a.shape[1] // N_CHIPS), (1, 0, 2)).astype(BF16))
    received = _piece_exchange(pieces, "exchange_grads")
    mine = [_sum_lead(r, "sum_" + k) for k, r in zip(_BIG, received)]
    theirs = _sibling_exchange(mine, "exchange_cores")

    out = {}
    for k, a, b in zip(_BIG, mine, theirs):
        out[k] = _adamw(w[k][0], m[k][0], v[k][0], [a, b], "adamw_" + k)
    out["ada_w"] = _adamw(w["ada_w"][0], m["ada_w"][0], v["ada_w"][0], [grad["ada_w"]], "adamw_ada_w")
    small_names = [k for k in _W_NAMES if k not in _BIG and k != "ada_w"]
    offs, n_tot = {}, 0
    for k in small_names:
        offs[k] = n_tot
        n_tot += -(-w[k].size // LANES) * LANES
    pack = lambda dct: jnp.concatenate([_pad128(dct[k]) for k in small_names]).reshape(-1, LANES)
    rows_s = n_tot // LANES
    pad_rows = -(-rows_s // 8) * 8 - rows_s
    padr = lambda a: jnp.concatenate([a, jnp.zeros((pad_rows, LANES), F32)]) if pad_rows else a
    res_small = _adamw(padr(pack(w)), padr(pack(m)), padr(pack(v)), [padr(pack(grad))], "adamw_small")
    for k in small_names:
        out[k] = [r.reshape(-1)[offs[k]:offs[k] + w[k].size].reshape(w[k].shape) for r in res_small]
    for k in _BIG + ("ada_w",):
        out[k] = [r.reshape(w[k].shape) for r in out[k]]
    grads, deltas, new_m, new_v = ([out[k][i] for k in _W_NAMES] for i in range(4))
    return (loss, dx.reshape(bsz, s, d), *grads, *deltas, *new_m, *new_v)
```

```python
import functools
import math

import numpy as np
import jax
import jax.numpy as jnp
from jax import lax
from jax.experimental import pallas as pl
from jax.experimental.pallas import tpu as pltpu

F32 = jnp.float32
BF16 = jnp.bfloat16
HI = lax.Precision.HIGHEST
MESH = pl.DeviceIdType.MESH

D_MODEL = 1024
N_MOD = 6
AQ_HEADS, AKV_HEADS, A_HD, WINDOW = 8, 2, 64, 128
REL_BUCKETS, REL_MAX_DIST = 32, 128
DN_HEADS, DN_HD, DN_CONV, DN_CHUNK = 4, 128, 4, 64
D_FF, FFN_CONV = 2816, 3
RMS_EPS, L2_EPS, NEG_INF = 1e-6, 1e-6, -1e30
AQ, AKV, DN = AQ_HEADS * A_HD, AKV_HEADS * A_HD, DN_HEADS * DN_HD
IN_DIM = AQ + 2 * AKV + 3 * DN + DN + 2 * DN_HEADS + 2 * D_MODEL
C_DQKV, C_Q, C_DZ, C_GA, C_GD, C_K, C_V, C_BD = 0, 1536, 2048, 2560, 3584, 4608, 4736, 4864
IN_PAD = 4992
LANES = 128
N_CHIPS = 4

ADAM_LR, ADAM_B1, ADAM_B2, ADAM_EPS, ADAM_WD, ADAM_STEP = 0.001, 0.9, 0.999, 1e-08, 0.01, 10


def _pick(n, cap):
    best = None
    for t in range(LANES, cap + 1, LANES):
        if n % t == 0:
            best = t
    return best if best is not None else n


def _vspec(shape, index_map):
    return pl.BlockSpec(shape, index_map)


MM_VMEM_BUDGET = 40 * 2 ** 20
GRID_STEP_S = 0.35e-6
HBM_BYTES_PER_S = 3.0e12


def _mm_tiles(m, n, k, mode, in_bytes, out_bytes):
    best = None
    for tm in [t for t in range(LANES, m + 1, LANES) if m % t == 0]:
        for tn in [t for t in range(LANES, n + 1, LANES) if n % t == 0]:
            a, b, o = k * tm * in_bytes, k * tn * in_bytes, tm * tn * out_bytes
            if 2 * (a + b + o) + (a if mode == "tn" else 0) > MM_VMEM_BUDGET:
                continue
            cost = (m // tm) * (n // tn) * GRID_STEP_S + (m * k * in_bytes + (m // tm) * n * k * in_bytes + m * n * out_bytes) / HBM_BYTES_PER_S
            if best is None or cost < best[0]:
                best = (cost, tm, tn)
    return best[1], best[2]


def _mm(a, b, mode, out_dtype, name):
    if mode == "nn":
        (m, k), n = a.shape, b.shape[1]
        dims = (((1,), (0,)), ((), ()))
    elif mode == "nt":
        (m, k), n = a.shape, b.shape[0]
        dims = (((1,), (1,)), ((), ()))
    else:
        (k, m), n = a.shape, b.shape[1]
        dims = (((0,), (0,)), ((), ()))
    tm, tn = _mm_tiles(m, n, k, mode, a.dtype.itemsize, jnp.dtype(out_dtype).itemsize)
    if mode == "tn":
        a_spec = _vspec((k, tm), lambda i, j: (0, i))
    else:
        a_spec = _vspec((tm, k), lambda i, j: (i, 0))
    if mode == "nt":
        b_spec = _vspec((tn, k), lambda i, j: (j, 0))
    else:
        b_spec = _vspec((k, tn), lambda i, j: (0, j))

    def body(a_ref, b_ref, o_ref):
        o_ref[...] = lax.dot_general(a_ref[...].astype(BF16), b_ref[...].astype(BF16), dims,
                                     preferred_element_type=F32).astype(out_dtype)

    return pl.pallas_call(
        body, name=name, grid=(m // tm, n // tn), in_specs=[a_spec, b_spec],
        out_specs=_vspec((tm, tn), lambda i, j: (i, j)), out_shape=jax.ShapeDtypeStruct((m, n), out_dtype),
    )(a, b)


def _rms(x, w):
    return (x * lax.rsqrt(jnp.mean(x * x, axis=-1, keepdims=True) + RMS_EPS)) * w


def _pre_f(x, w, sc, sh):
    return _rms(x, w) * (1.0 + sc) + sh


def _post_f(y, w, g):
    return g * _rms(y, w)


def _tok_grid(t, bsz, ts):
    nt = t // bsz // ts
    return nt, (bsz, nt)


def _pre_fwd(x, w, sc, sh, name, ts=512):
    t, d = x.shape
    bsz = sc.shape[0]
    nt, grid = _tok_grid(t, bsz, ts)
    row = _vspec((ts, d), lambda b, i: (b * nt + i, 0))
    vec = _vspec((1, d), lambda b, i: (0, 0))
    bvec = _vspec((1, 1, d), lambda b, i: (b, 0, 0))

    def body(x_ref, w_ref, sc_ref, sh_ref, u_ref):
        u_ref[...] = _pre_f(x_ref[...], w_ref[...], sc_ref[0], sh_ref[0]).astype(BF16)

    return pl.pallas_call(body, name=name, grid=grid, in_specs=[row, vec, bvec, bvec], out_specs=row,
                          out_shape=jax.ShapeDtypeStruct((t, d), BF16))(x, w, sc, sh)


def _pre_bwd(x, w, sc, sh, du, dres, name, ts=512):
    t, d = x.shape
    bsz = sc.shape[0]
    nt, grid = _tok_grid(t, bsz, ts)
    row = _vspec((ts, d), lambda b, i: (b * nt + i, 0))
    vec = _vspec((1, d), lambda b, i: (0, 0))
    bvec = _vspec((1, 1, d), lambda b, i: (b, 0, 0))

    def body(x_ref, w_ref, sc_ref, sh_ref, du_ref, dres_ref, dx_ref, dw_ref, dsc_ref, dsh_ref):
        b, i = pl.program_id(0), pl.program_id(1)
        _, vjp = jax.vjp(_pre_f, x_ref[...], w_ref[...], sc_ref[0], sh_ref[0])
        dx, dw, dsc, dsh = vjp(du_ref[...])
        dx_ref[...] = dres_ref[...] + dx

        @pl.when((b == 0) & (i == 0))
        def _():
            dw_ref[...] = jnp.zeros_like(dw_ref)

        @pl.when(i == 0)
        def _():
            dsc_ref[...] = jnp.zeros_like(dsc_ref)
            dsh_ref[...] = jnp.zeros_like(dsh_ref)

        dw_ref[...] += dw
        dsc_ref[0] += dsc
        dsh_ref[0] += dsh

    return pl.pallas_call(
        body, name=name, grid=grid, in_specs=[row, vec, bvec, bvec, row, row], out_specs=[row, vec, bvec, bvec],
        out_shape=[jax.ShapeDtypeStruct((t, d), F32), jax.ShapeDtypeStruct((1, d), F32),
                   jax.ShapeDtypeStruct((bsz, 1, d), F32), jax.ShapeDtypeStruct((bsz, 1, d), F32)],
    )(x, w, sc, sh, du, dres)


def _post_fwd(res, y, w, g, name, ts=512):
    t, d = y.shape
    bsz = g.shape[0]
    nt, grid = _tok_grid(t, bsz, ts)
    row = _vspec((ts, d), lambda b, i: (b * nt + i, 0))
    vec = _vspec((1, d), lambda b, i: (0, 0))
    bvec = _vspec((1, 1, d), lambda b, i: (b, 0, 0))

    def body(res_ref, y_ref, w_ref, g_ref, h_ref):
        h_ref[...] = res_ref[...] + _post_f(y_ref[...], w_ref[...], g_ref[0])

    return pl.pallas_call(body, name=name, grid=grid, in_specs=[row, row, vec, bvec], out_specs=row,
                          out_shape=jax.ShapeDtypeStruct((t, d), F32))(res, y, w, g)


def _post_loss(res, y, w, g, tgt, name, ts=512):
    t, d = y.shape
    bsz = g.shape[0]
    nt, grid = _tok_grid(t, bsz, ts)
    row = _vspec((ts, d), lambda b, i: (b * nt + i, 0))
    vec = _vspec((1, d), lambda b, i: (0, 0))
    bvec = _vspec((1, 1, d), lambda b, i: (b, 0, 0))
    acc = _vspec((1, d), lambda b, i: (0, 0))

    def body(res_ref, y_ref, w_ref, g_ref, tgt_ref, dh_ref, sq_ref):
        b, i = pl.program_id(0), pl.program_id(1)
        e = res_ref[...] + _post_f(y_ref[...], w_ref[...], g_ref[0]) - tgt_ref[...]
        dh_ref[...] = e * (1.0 / d)

        @pl.when((b == 0) & (i == 0))
        def _():
            sq_ref[...] = jnp.zeros_like(sq_ref)

        sq_ref[...] += jnp.sum(e * e, axis=0, keepdims=True) * (1.0 / d)

    return pl.pallas_call(
        body, name=name, grid=grid, in_specs=[row, row, vec, bvec, row], out_specs=[row, acc],
        out_shape=[jax.ShapeDtypeStruct((t, d), F32), jax.ShapeDtypeStruct((1, d), F32)],
    )(res, y, w, g, tgt)


def _post_bwd(dh, y, w, g, name, ts=512):
    t, d = y.shape
    bsz = g.shape[0]
    nt, grid = _tok_grid(t, bsz, ts)
    row = _vspec((ts, d), lambda b, i: (b * nt + i, 0))
    vec = _vspec((1, d), lambda b, i: (0, 0))
    bvec = _vspec((1, 1, d), lambda b, i: (b, 0, 0))

    def body(dh_ref, y_ref, w_ref, g_ref, dy_ref, dw_ref, dg_ref):
        b, i = pl.program_id(0), pl.program_id(1)
        _, vjp = jax.vjp(_post_f, y_ref[...], w_ref[...], g_ref[0])
        dy, dw, dg = vjp(dh_ref[...])
        dy_ref[...] = dy.astype(BF16)

        @pl.when((b == 0) & (i == 0))
        def _():
            dw_ref[...] = jnp.zeros_like(dw_ref)

        @pl.when(i == 0)
        def _():
            dg_ref[...] = jnp.zeros_like(dg_ref)

        dw_ref[...] += dw
        dg_ref[0] += dg

    return pl.pallas_call(
        body, name=name, grid=grid, in_specs=[row, row, vec, bvec], out_specs=[row, vec, bvec],
        out_shape=[jax.ShapeDtypeStruct((t, d), BF16), jax.ShapeDtypeStruct((1, d), F32),
                   jax.ShapeDtypeStruct((bsz, 1, d), F32)],
    )(dh, y, w, g)


def _merge_f(ga, gd, ya, yd):
    return jax.nn.sigmoid(ga) * ya + jax.nn.sigmoid(gd) * yd


_MW = 512


def _merge_fwd(proj, ya, yd, ts=512):
    t, d = ya.shape
    blk = _vspec((ts, _MW), lambda i, j: (i, j))
    ga = _vspec((ts, _MW), lambda i, j: (i, C_GA // _MW + j))
    gd = _vspec((ts, _MW), lambda i, j: (i, C_GD // _MW + j))

    def body(ga_ref, gd_ref, ya_ref, yd_ref, o_ref):
        o_ref[...] = _merge_f(ga_ref[...], gd_ref[...], ya_ref[...], yd_ref[...]).astype(BF16)

    return pl.pallas_call(body, name="merge_fwd", grid=(t // ts, d // _MW), in_specs=[ga, gd, blk, blk], out_specs=blk,
                          out_shape=jax.ShapeDtypeStruct((t, d), BF16))(proj, proj, ya, yd)


def _merge_bwd(proj, ya, yd, dm, ts=512):
    t, d = ya.shape
    blk = _vspec((ts, _MW), lambda i, j: (i, j))
    ga = _vspec((ts, _MW), lambda i, j: (i, C_GA // _MW + j))
    gd = _vspec((ts, _MW), lambda i, j: (i, C_GD // _MW + j))

    def body(ga_ref, gd_ref, ya_ref, yd_ref, dm_ref, dga_ref, dgd_ref, dya_ref, dyd_ref):
        _, vjp = jax.vjp(_merge_f, ga_ref[...], gd_ref[...], ya_ref[...], yd_ref[...])
        dga, dgd, dya, dyd = vjp(dm_ref[...])
        dga_ref[...] = dga.astype(BF16)
        dgd_ref[...] = dgd.astype(BF16)
        dya_ref[...] = dya.astype(BF16)
        dyd_ref[...] = dyd.astype(BF16)

    o = jax.ShapeDtypeStruct((t, d), BF16)
    return pl.pallas_call(body, name="merge_bwd", grid=(t // ts, d // _MW), in_specs=[ga, gd, blk, blk, blk],
                          out_specs=[blk] * 4, out_shape=[o] * 4)(proj, proj, ya, yd, dm)


def _shift_down(x, s):
    if s == 0:
        return x
    r = lax.broadcasted_iota(jnp.int32, x.shape, 0)
    return jnp.where(r >= s, pltpu.roll(x, s, 0), 0.0)


def _shift_up(x, s):
    if s == 0:
        return x
    n = x.shape[0]
    r = lax.broadcasted_iota(jnp.int32, x.shape, 0)
    return jnp.where(r < n - s, pltpu.roll(x, n - s, 0), 0.0)


def _conv_fwd(x, w, k):
    out = None
    for j in range(k):
        term = w[j:j + 1, :] * _shift_down(x, k - 1 - j)
        out = term if out is None else out + term
    return out


def _conv_bwd(x, w, dc, k):
    dx = None
    dws = []
    for j in range(k):
        s = k - 1 - j
        term = w[j:j + 1, :] * _shift_up(dc, s)
        dx = term if dx is None else dx + term
        dws.append(jnp.sum(dc * _shift_down(x, s), axis=0, keepdims=True))
    return dx, jnp.concatenate(dws, axis=0)


def _geglu_f(gate, val):
    return jax.nn.gelu(gate, approximate=True) * val


_FW = 256


def _ffn_act_fwd(up, conv_w, bsz):
    t = up.shape[0]
    s = t // bsz
    nj = D_FF // _FW
    xg = _vspec((s, _FW), lambda b, j: (b, j))
    xv = _vspec((s, _FW), lambda b, j: (b, nj + j))
    wg = _vspec((FFN_CONV, _FW), lambda b, j: (0, j))
    wv = _vspec((FFN_CONV, _FW), lambda b, j: (0, nj + j))

    def body(xg_ref, xv_ref, wg_ref, wv_ref, o_ref):
        gate = _conv_fwd(xg_ref[...], wg_ref[...], FFN_CONV)
        val = _conv_fwd(xv_ref[...], wv_ref[...], FFN_CONV)
        o_ref[...] = _geglu_f(gate, val).astype(BF16)

    return pl.pallas_call(body, name="ffn_act_fwd", grid=(bsz, nj), in_specs=[xg, xv, wg, wv],
                          out_specs=_vspec((s, _FW), lambda b, j: (b, j)),
                          out_shape=jax.ShapeDtypeStruct((t, D_FF), BF16))(up, up, conv_w, conv_w)


def _ffn_act_bwd(up, conv_w, dact, bsz):
    t = up.shape[0]
    s = t // bsz
    nj = D_FF // _FW
    xg = _vspec((s, _FW), lambda j, b: (b, j))
    xv = _vspec((s, _FW), lambda j, b: (b, nj + j))
    wg = _vspec((FFN_CONV, _FW), lambda j, b: (0, j))
    wv = _vspec((FFN_CONV, _FW), lambda j, b: (0, nj + j))
    da = _vspec((s, _FW), lambda j, b: (b, j))
    dwo = _vspec((FFN_CONV, _FW), lambda j, b: (0, j))

    def body(xg_ref, xv_ref, wg_ref, wv_ref, da_ref, dxg_ref, dxv_ref, dwg_ref, dwv_ref):
        b = pl.program_id(1)
        xg_, xv_, wg_, wv_ = xg_ref[...], xv_ref[...], wg_ref[...], wv_ref[...]
        gate = _conv_fwd(xg_, wg_, FFN_CONV)
        val = _conv_fwd(xv_, wv_, FFN_CONV)
        _, vjp = jax.vjp(_geglu_f, gate, val)
        dgate, dval = vjp(da_ref[...])
        dxg, dwg = _conv_bwd(xg_, wg_, dgate, FFN_CONV)
        dxv, dwv = _conv_bwd(xv_, wv_, dval, FFN_CONV)
        dxg_ref[...] = dxg.astype(BF16)
        dxv_ref[...] = dxv.astype(BF16)

        @pl.when(b == 0)
        def _():
            dwg_ref[...] = jnp.zeros_like(dwg_ref)
            dwv_ref[...] = jnp.zeros_like(dwv_ref)

        dwg_ref[...] += dwg
        dwv_ref[...] += dwv

    return pl.pallas_call(
        body, name="ffn_act_bwd", grid=(nj, bsz), in_specs=[xg, xv, wg, wv, da], out_specs=[da, da, dwo, dwo],
        out_shape=[jax.ShapeDtypeStruct((t, D_FF), BF16), jax.ShapeDtypeStruct((t, D_FF), BF16),
                   jax.ShapeDtypeStruct((FFN_CONV, D_FF), F32), jax.ShapeDtypeStruct((FFN_CONV, D_FF), F32)],
    )(up, up, conv_w, conv_w, dact)


def _bucket_table():
    qi = np.arange(WINDOW)[:, None]
    kj = np.arange(2 * WINDOW)[None, :]
    dist = WINDOW + qi - kj
    dc = np.maximum(dist, 0)
    max_exact = REL_BUCKETS // 2
    scaled = np.log(np.maximum(dc, 1).astype(np.float32) / np.float32(max_exact)) / np.float32(math.log(REL_MAX_DIST / max_exact))
    large = max_exact + (scaled.astype(np.float32) * np.float32(REL_BUCKETS - max_exact)).astype(np.int32)
    large = np.minimum(large, REL_BUCKETS - 1)
    bucket = np.where(dc < max_exact, dc, large).astype(np.int32)
    in_band = ((dist >= 0) & (dist < WINDOW)).astype(np.int32)
    return bucket, in_band


def _bias_build(rel_bias):
    bucket, _ = _bucket_table()

    def body(rb_ref, idx_ref, o_ref):
        h = pl.program_id(0)
        idx = idx_ref[...]
        acc = jnp.zeros(idx.shape, F32)
        for r in range(REL_BUCKETS):
            acc = jnp.where(idx == r, rb_ref[r, h], acc)
        o_ref[0] = acc

    return pl.pallas_call(
        body, name="bias_build", grid=(AQ_HEADS,),
        in_specs=[pl.BlockSpec(memory_space=pltpu.SMEM), _vspec((WINDOW, 2 * WINDOW), lambda h: (0, 0))],
        out_specs=_vspec((1, WINDOW, 2 * WINDOW), lambda h: (h, 0, 0)),
        out_shape=jax.ShapeDtypeStruct((AQ_HEADS, WINDOW, 2 * WINDOW), F32),
    )(rel_bias, jnp.asarray(bucket))


def _bias_grad(dbias):
    bucket, _ = _bucket_table()

    def body(db_ref, idx_ref, o_ref):
        idx = idx_ref[...]
        db = db_ref[0]
        lane = lax.broadcasted_iota(jnp.int32, (1, LANES), 1)
        acc = jnp.zeros((1, LANES), F32)
        for r in range(REL_BUCKETS):
            s = jnp.sum(jnp.sum(jnp.where(idx == r, db, 0.0), axis=1, keepdims=True), axis=0, keepdims=True)
            acc = jnp.where(lane == r, s, acc)
        o_ref[0] = acc

    return pl.pallas_call(
        body, name="bias_grad", grid=(AQ_HEADS,),
        in_specs=[_vspec((1, WINDOW, 2 * WINDOW), lambda h: (h, 0, 0)), _vspec((WINDOW, 2 * WINDOW), lambda h: (0, 0))],
        out_specs=_vspec((1, 1, LANES), lambda h: (h, 0, 0)),
        out_shape=jax.ShapeDtypeStruct((AQ_HEADS, 1, LANES), F32),
    )(dbias, jnp.asarray(bucket))


def _attn_mask(n):
    qi = lax.broadcasted_iota(jnp.int32, (WINDOW, 2 * WINDOW), 0)
    kj = lax.broadcasted_iota(jnp.int32, (WINDOW, 2 * WINDOW), 1)
    dist = WINDOW + qi - kj
    band = (dist >= 0) & (dist < WINDOW)
    return band & ((kj >= WINDOW) | (n > 0))


def _attn_probs(q, kb, bias, sink, mask):
    s = lax.dot_general(q, kb, (((1,), (1,)), ((), ())), preferred_element_type=F32) * (A_HD ** -0.5)
    s = jnp.where(mask, s + bias, NEG_INF)
    m = jnp.maximum(jnp.max(s, axis=-1, keepdims=True), sink)
    p = jnp.exp(s - m)
    es = jnp.exp(sink - m)
    inv = 1.0 / (jnp.sum(p, axis=-1, keepdims=True) + es)
    return p * inv, es * inv


def _attn_fwd(proj, bias, sinks, bsz):
    t = proj.shape[0]
    s = t // bsz
    nb = s // WINDOW
    grp = AQ_HEADS // AKV_HEADS

    def body(q_ref, k_ref, v_ref, bias_ref, sink_ref, y_ref, kp_ref, vp_ref):
        kp_ref[0:WINDOW, :] = jnp.zeros((WINDOW, LANES), BF16)
        vp_ref[0:WINDOW, :] = jnp.zeros((WINDOW, LANES), BF16)
        kp_ref[WINDOW:, :] = k_ref[...].astype(BF16)
        vp_ref[WINDOW:, :] = v_ref[...].astype(BF16)

        def blk(n, carry):
            r0 = pl.multiple_of(n * WINDOW, WINDOW)
            mask = _attn_mask(n)
            kband = kp_ref[pl.ds(r0, 2 * WINDOW), :]
            vband = vp_ref[pl.ds(r0, 2 * WINDOW), :]
            qb = q_ref[pl.ds(r0, WINDOW), :].astype(BF16)
            outs = []
            for hq in range(AQ_HEADS):
                kv = hq // grp
                kb = kband[:, kv * A_HD:(kv + 1) * A_HD]
                vb = vband[:, kv * A_HD:(kv + 1) * A_HD]
                probs, _ = _attn_probs(qb[:, hq * A_HD:(hq + 1) * A_HD], kb, bias_ref[hq], sink_ref[0, hq], mask)
                outs.append(jnp.dot(probs.astype(BF16), vb, preferred_element_type=F32))
            y_ref[pl.ds(r0, WINDOW), :] = jnp.concatenate(outs, axis=1).astype(BF16)
            return carry

        lax.fori_loop(0, nb, blk, 0)

    return pl.pallas_call(
        body, name="attn_fwd", grid=(bsz,),
        in_specs=[_vspec((s, AQ), lambda b: (b, C_Q // AQ)), _vspec((s, AKV), lambda b: (b, C_K // AKV)),
                  _vspec((s, AKV), lambda b: (b, C_V // AKV)),
                  _vspec((AQ_HEADS, WINDOW, 2 * WINDOW), lambda b: (0, 0, 0)), pl.BlockSpec(memory_space=pltpu.SMEM)],
        out_specs=_vspec((s, AQ), lambda b: (b, 0)), out_shape=jax.ShapeDtypeStruct((t, AQ), BF16),
        scratch_shapes=[pltpu.VMEM((s + WINDOW, LANES), BF16), pltpu.VMEM((s + WINDOW, LANES), BF16)],
    )(proj, proj, proj, bias, sinks)


def _attn_bwd(proj, bias, sinks, dy, bsz):
    t = proj.shape[0]
    s = t // bsz
    nb = s // WINDOW
    grp = AQ_HEADS // AKV_HEADS
    scale = A_HD ** -0.5

    def body(q_ref, k_ref, v_ref, bias_ref, sink_ref, dy_ref, dq_ref, dk_ref, dv_ref, dbias_ref, dsink_ref,
             kp_ref, vp_ref, dkp_ref, dvp_ref):
        b = pl.program_id(0)
        kp_ref[0:WINDOW, :] = jnp.zeros((WINDOW, LANES), BF16)
        vp_ref[0:WINDOW, :] = jnp.zeros((WINDOW, LANES), BF16)
        kp_ref[WINDOW:, :] = k_ref[...].astype(BF16)
        vp_ref[WINDOW:, :] = v_ref[...].astype(BF16)
        dkp_ref[...] = jnp.zeros_like(dkp_ref)
        dvp_ref[...] = jnp.zeros_like(dvp_ref)

        @pl.when(b == 0)
        def _():
            dbias_ref[...] = jnp.zeros_like(dbias_ref)
            dsink_ref[...] = jnp.zeros_like(dsink_ref)

        def blk(n, carry):
            r0 = pl.multiple_of(n * WINDOW, WINDOW)
            mask = _attn_mask(n)
            kband = kp_ref[pl.ds(r0, 2 * WINDOW), :]
            vband = vp_ref[pl.ds(r0, 2 * WINDOW), :]
            qb = q_ref[pl.ds(r0, WINDOW), :].astype(BF16)
            dyb = dy_ref[pl.ds(r0, WINDOW), :].astype(BF16)
            dqs = []
            dks = [None] * AKV_HEADS
            dvs = [None] * AKV_HEADS
            for hq in range(AQ_HEADS):
                kv = hq // grp
                kb = kband[:, kv * A_HD:(kv + 1) * A_HD]
                vb = vband[:, kv * A_HD:(kv + 1) * A_HD]
                qh = qb[:, hq * A_HD:(hq + 1) * A_HD]
                dyh = dyb[:, hq * A_HD:(hq + 1) * A_HD]
                probs, psink = _attn_probs(qh, kb, bias_ref[hq], sink_ref[0, hq], mask)
                dprobs = lax.dot_general(dyh, vb, (((1,), (1,)), ((), ())), preferred_element_type=F32)
                dvh = lax.dot_general(probs.astype(BF16), dyh, (((0,), (0,)), ((), ())), preferred_element_type=F32)
                rowdot = jnp.sum(probs * dprobs, axis=-1, keepdims=True)
                ds = probs * (dprobs - rowdot)
                dbias_ref[hq] += ds
                dsink_ref[hq] += jnp.sum(-psink * rowdot, axis=0, keepdims=True) + jnp.zeros((1, LANES), F32)
                dsb = ds.astype(BF16)
                dqs.append(jnp.dot(dsb, kb, preferred_element_type=F32) * scale)
                dkh = lax.dot_general(dsb, qh, (((0,), (0,)), ((), ())), preferred_element_type=F32) * scale
                dks[kv] = dkh if dks[kv] is None else dks[kv] + dkh
                dvs[kv] = dvh if dvs[kv] is None else dvs[kv] + dvh
            dq_ref[pl.ds(r0, WINDOW), :] = jnp.concatenate(dqs, axis=1).astype(BF16)
            dkp_ref[pl.ds(r0, 2 * WINDOW), :] += jnp.concatenate(dks, axis=1)
            dvp_ref[pl.ds(r0, 2 * WINDOW), :] += jnp.concatenate(dvs, axis=1)
            return carry

        lax.fori_loop(0, nb, blk, 0)
        dk_ref[...] = dkp_ref[WINDOW:, :].astype(BF16)
        dv_ref[...] = dvp_ref[WINDOW:, :].astype(BF16)

    kvs = jax.ShapeDtypeStruct((t, AKV), BF16)
    return pl.pallas_call(
        body, name="attn_bwd", grid=(bsz,),
        in_specs=[_vspec((s, AQ), lambda b: (b, C_Q // AQ)), _vspec((s, AKV), lambda b: (b, C_K // AKV)),
                  _vspec((s, AKV), lambda b: (b, C_V // AKV)),
                  _vspec((AQ_HEADS, WINDOW, 2 * WINDOW), lambda b: (0, 0, 0)), pl.BlockSpec(memory_space=pltpu.SMEM),
                  _vspec((s, AQ), lambda b: (b, 0))],
        out_specs=[_vspec((s, AQ), lambda b: (b, 0)), _vspec((s, AKV), lambda b: (b, 0)), _vspec((s, AKV), lambda b: (b, 0)),
                   _vspec((AQ_HEADS, WINDOW, 2 * WINDOW), lambda b: (0, 0, 0)), _vspec((AQ_HEADS, 1, LANES), lambda b: (0, 0, 0))],
        out_shape=[jax.ShapeDtypeStruct((t, AQ), BF16), kvs, kvs,
                   jax.ShapeDtypeStruct((AQ_HEADS, WINDOW, 2 * WINDOW), F32), jax.ShapeDtypeStruct((AQ_HEADS, 1, LANES), F32)],
        scratch_shapes=[pltpu.VMEM((s + WINDOW, LANES), BF16), pltpu.VMEM((s + WINDOW, LANES), BF16),
                        pltpu.VMEM((s + WINDOW, LANES), F32), pltpu.VMEM((s + WINDOW, LANES), F32)],
    )(proj, proj, proj, bias, sinks, dy)


def _dn_act_f(c, is_qk):
    a = jax.nn.silu(c)
    outs = []
    for h in range(DN_HEADS):
        ah = a[:, h * DN_HD:(h + 1) * DN_HD]
        nh = ah * lax.rsqrt(jnp.sum(ah * ah, axis=-1, keepdims=True) + L2_EPS)
        outs.append(jnp.where(is_qk, nh, ah))
    return jnp.concatenate(outs, axis=1)


def _dn_prep_fwd(proj, conv_w, bsz):
    t = proj.shape[0]
    s = t // bsz
    blk = _vspec((s, DN), lambda b, j: (b, j))
    wsp = _vspec((DN_CONV, DN), lambda b, j: (0, j))

    def body(x_ref, w_ref, o_ref):
        j = pl.program_id(1)
        o_ref[...] = _dn_act_f(_conv_fwd(x_ref[...], w_ref[...], DN_CONV), j < 2)

    return pl.pallas_call(body, name="dn_prep_fwd", grid=(bsz, 3), in_specs=[blk, wsp], out_specs=blk,
                          out_shape=jax.ShapeDtypeStruct((t, 3 * DN), F32))(proj, conv_w)


def _dn_prep_bwd(proj, conv_w, dqkvn, bsz):
    t = proj.shape[0]
    s = t // bsz
    blk = _vspec((s, DN), lambda j, b: (b, j))
    wsp = _vspec((DN_CONV, DN), lambda j, b: (0, j))

    def body(x_ref, w_ref, d_ref, dx_ref, dw_ref):
        j, b = pl.program_id(0), pl.program_id(1)
        x, w = x_ref[...], w_ref[...]
        c = _conv_fwd(x, w, DN_CONV)
        _, vjp = jax.vjp(lambda cc: _dn_act_f(cc, j < 2), c)
        (dc,) = vjp(d_ref[...])
        dx, dw = _conv_bwd(x, w, dc, DN_CONV)
        dx_ref[...] = dx.astype(BF16)

        @pl.when(b == 0)
        def _():
            dw_ref[...] = jnp.zeros_like(dw_ref)

        dw_ref[...] += dw

    return pl.pallas_call(
        body, name="dn_prep_bwd", grid=(3, bsz), in_specs=[blk, wsp, blk], out_specs=[blk, wsp],
        out_shape=[jax.ShapeDtypeStruct((t, 3 * DN), BF16), jax.ShapeDtypeStruct((DN_CONV, 3 * DN), F32)],
    )(proj, conv_w, dqkvn)


def _bg_f(x, alog, dt):
    lane = lax.broadcasted_iota(jnp.int32, x.shape, 1)
    beta = jax.nn.sigmoid(x)
    g = -jnp.exp(alog) * jax.nn.softplus(x + dt)
    return jnp.where(lane < DN_HEADS, beta, jnp.where(lane < 2 * DN_HEADS, g, 0.0))


def _bg_fwd(proj, alog, dt, bsz):
    t = proj.shape[0]
    s = t // bsz
    vec = _vspec((1, LANES), lambda b: (0, 0))

    def body(x_ref, a_ref, d_ref, o_ref):
        o_ref[...] = _bg_f(x_ref[...], a_ref[...], d_ref[...])

    return pl.pallas_call(body, name="bg_fwd", grid=(bsz,), in_specs=[_vspec((s, LANES), lambda b: (b, C_BD // LANES)), vec, vec],
                          out_specs=_vspec((s, LANES), lambda b: (b, 0)), out_shape=jax.ShapeDtypeStruct((t, LANES), F32))(proj, alog, dt)


def _bg_bwd(proj, alog, dt, dbg4, bsz):
    t = proj.shape[0]
    s = t // bsz
    vec = _vspec((1, LANES), lambda b: (0, 0))

    def body(x_ref, a_ref, d_ref, g4_ref, dx_ref, da_ref, dd_ref):
        b = pl.program_id(0)
        lane = lax.broadcasted_iota(jnp.int32, (s, LANES), 1)
        dbg = jnp.zeros((s, LANES), F32)
        for h in range(DN_HEADS):
            gh = g4_ref[:, h * DN_HD:(h + 1) * DN_HD]
            dbg = jnp.where(lane == h, gh[:, 0:1], dbg)
            dbg = jnp.where(lane == DN_HEADS + h, gh[:, 1:2], dbg)
        _, vjp = jax.vjp(_bg_f, x_ref[...], a_ref[...], d_ref[...])
        dx, da, dd = vjp(dbg)
        dx_ref[...] = dx.astype(BF16)

        @pl.when(b == 0)
        def _():
            da_ref[...] = jnp.zeros_like(da_ref)
            dd_ref[...] = jnp.zeros_like(dd_ref)

        da_ref[...] += da
        dd_ref[...] += dd

    return pl.pallas_call(
        body, name="bg_bwd", grid=(bsz,),
        in_specs=[_vspec((s, LANES), lambda b: (b, C_BD // LANES)), vec, vec, _vspec((s, DN), lambda b: (b, 0))],
        out_specs=[_vspec((s, LANES), lambda b: (b, 0)), vec, vec],
        out_shape=[jax.ShapeDtypeStruct((t, LANES), BF16), jax.ShapeDtypeStruct((1, LANES), F32), jax.ShapeDtypeStruct((1, LANES), F32)],
    )(proj, alog, dt, dbg4)


def _dn_out_f(o, z, w):
    outs = []
    for h in range(DN_HEADS):
        sl = slice(h * DN_HD, (h + 1) * DN_HD)
        outs.append(_rms(o[:, sl], w) * jax.nn.silu(z[:, sl]))
    return jnp.concatenate(outs, axis=1)


def _dn_out_fwd(o, proj, w, ts=512):
    t = o.shape[0]
    blk = _vspec((ts, DN), lambda i: (i, 0))
    zsp = _vspec((ts, DN), lambda i: (i, C_DZ // DN))
    vec = _vspec((1, DN_HD), lambda i: (0, 0))

    def body(o_ref, z_ref, w_ref, y_ref):
        y_ref[...] = _dn_out_f(o_ref[...], z_ref[...], w_ref[...]).astype(BF16)

    return pl.pallas_call(body, name="dn_out_fwd", grid=(t // ts,), in_specs=[blk, zsp, vec], out_specs=blk,
                          out_shape=jax.ShapeDtypeStruct((t, DN), BF16))(o, proj, w)


def _dn_out_bwd(o, proj, w, dy, ts=512):
    t = o.shape[0]
    blk = _vspec((ts, DN), lambda i: (i, 0))
    zsp = _vspec((ts, DN), lambda i: (i, C_DZ // DN))
    vec = _vspec((1, DN_HD), lambda i: (0, 0))

    def body(o_ref, z_ref, w_ref, dy_ref, do_ref, dz_ref, dw_ref):
        i = pl.program_id(0)
        _, vjp = jax.vjp(_dn_out_f, o_ref[...], z_ref[...], w_ref[...])
        do, dz, dw = vjp(dy_ref[...])
        do_ref[...] = do
        dz_ref[...] = dz.astype(BF16)

        @pl.when(i == 0)
        def _():
            dw_ref[...] = jnp.zeros_like(dw_ref)

        dw_ref[...] += dw

    return pl.pallas_call(
        body, name="dn_out_bwd", grid=(t // ts,), in_specs=[blk, zsp, vec, blk], out_specs=[blk, blk, vec],
        out_shape=[jax.ShapeDtypeStruct((t, DN), F32), jax.ShapeDtypeStruct((t, DN), BF16), jax.ShapeDtypeStruct((1, DN_HD), F32)],
    )(o, proj, w, dy)


_C = DN_CHUNK


def _dot(a, b, dims):
    return lax.dot_general(a.astype(BF16), b.astype(BF16), dims, preferred_element_type=F32)


_NN = (((1,), (0,)), ((), ()))
_NT = (((1,), (1,)), ((), ()))
_TN = (((0,), (0,)), ((), ()))


_SUB = 8


def _tri_inverse(lt):
    ri = lax.broadcasted_iota(jnp.int32, (_SUB, _C), 0)
    ci = lax.broadcasted_iota(jnp.int32, (_SUB, _C), 1)
    blocks = [jnp.where(ci == ri + _SUB * b, 1.0, 0.0).astype(F32) for b in range(_C // _SUB)]
    for i in range(1, _C):
        acc = None
        for b in range((i + _SUB - 1) // _SUB):
            term = lt[_SUB * b:_SUB * (b + 1), i:i + 1] * blocks[b]
            acc = term if acc is None else acc + term
        row = jnp.sum(acc, axis=0, keepdims=True)
        bi, r = divmod(i, _SUB)
        blocks[bi] = jnp.where(ri == r, blocks[bi] - row, blocks[bi])
    return jnp.concatenate(blocks, axis=0)


_SEG = 512
_HEADS = tuple(range(DN_HEADS))


def _hsl(hh):
    return slice(hh * DN_HD, (hh + 1) * DN_HD)


def _chunk_specs(nseg, reverse):
    seg = (lambda i: nseg - 1 - i) if reverse else (lambda i: i)
    ncs = _SEG // _C
    col = lambda off: _vspec((_SEG, DN), lambda b, i: (b * nseg + seg(i), off))
    return (col, _vspec((_SEG, LANES), lambda b, i: (b * nseg + seg(i), 0)),
            _vspec((1, DN_HEADS, ncs, _C), lambda b, i: (b, 0, seg(i), 0)),
            _vspec((1, DN_HEADS, ncs, DN_HD, DN_HD), lambda b, i: (b, 0, seg(i), 0, 0)))


def _chunk_pre(q_ref, k_ref, v_ref, bg_ref, gr_ref, c, hh):
    r0 = pl.multiple_of(c * _C, _C)
    ri = lax.broadcasted_iota(jnp.int32, (_C, _C), 0)
    ci = lax.broadcasted_iota(jnp.int32, (_C, _C), 1)
    q = q_ref[pl.ds(r0, _C), _hsl(hh)] * (DN_HD ** -0.5)
    k = k_ref[pl.ds(r0, _C), _hsl(hh)]
    v = v_ref[pl.ds(r0, _C), _hsl(hh)]
    bgc = bg_ref[pl.ds(r0, _C), :]
    beta = bgc[:, hh:hh + 1]
    g_col = bgc[:, DN_HEADS + hh:DN_HEADS + hh + 1]
    g_row = gr_ref[0, hh, pl.ds(c, 1), :]
    gc_col = jnp.sum(jnp.where(ri >= ci, g_row, 0.0), axis=1, keepdims=True)
    gc_row = jnp.sum(jnp.where(ri <= ci, g_col, 0.0), axis=0, keepdims=True)
    gc_last = jnp.sum(g_col, axis=0, keepdims=True)
    diff = gc_col - gc_row
    decay = jnp.where(ri >= ci, jnp.exp(jnp.where(ri >= ci, diff, 0.0)), 0.0)
    diff_t = gc_row - gc_col
    decay_t = jnp.where(ri <= ci, jnp.exp(jnp.where(ri <= ci, diff_t, 0.0)), 0.0)
    eg = jnp.exp(gc_col)
    et = jnp.exp(gc_last - gc_col)
    gl = jnp.exp(gc_last)
    kb = k * beta
    vb = v * beta
    return dict(r0=r0, q=q, k=k, v=v, beta=beta, decay=decay, decay_t=decay_t, eg=eg, et=et, gl=gl, kb=kb, vb=vb, ri=ri, ci=ci)


def _chunk_solve(ms, with_transposes):
    for m in ms:
        m["kk_t"] = _dot(m["k"], m["kb"], _NT)
        m["qk"] = _dot(m["q"], m["k"], _NT)
        if with_transposes:
            m["kk"] = _dot(m["kb"], m["k"], _NT)
            m["qk_t"] = _dot(m["k"], m["q"], _NT)
    for m in ms:
        m["tm"] = _tri_inverse(jnp.where(m["ri"] < m["ci"], m["kk_t"] * m["decay_t"], 0.0))
    for m in ms:
        rhs = jnp.concatenate([m["vb"], m["kb"] * m["eg"]], axis=1)
        m["sol"] = jnp.dot(m["tm"], rhs, precision=HI, preferred_element_type=F32)
        m["intra"] = jnp.where(m["ri"] >= m["ci"], m["qk"] * m["decay"], 0.0)


def _dn_chunk_fwd(qkvn, bg, g_rows, bsz):
    t = qkvn.shape[0]
    s = t // bsz
    nc, nseg = s // _C, s // _SEG

    def body(q_ref, k_ref, v_ref, bg_ref, gr_ref, o_ref, st_ref, s_ref):
        @pl.when(pl.program_id(1) == 0)
        def _():
            s_ref[...] = jnp.zeros_like(s_ref)

        def chunk(c, carry):
            ms = [_chunk_pre(q_ref, k_ref, v_ref, bg_ref, gr_ref, c, hh) for hh in _HEADS]
            _chunk_solve(ms, False)
            sts = [s_ref[hh] for hh in _HEADS]
            for hh in _HEADS:
                st_ref[0, hh, c] = sts[hh]
            ws = [_dot(m["sol"][:, DN_HD:], st, _NN) for m, st in zip(ms, sts)]
            qs = [_dot(m["q"] * m["eg"], st, _NN) for m, st in zip(ms, sts)]
            v_new = [m["sol"][:, :DN_HD] - a for m, a in zip(ms, ws)]
            iv = [_dot(m["intra"], vn, _NN) for m, vn in zip(ms, v_new)]
            upd = [_dot(m["k"] * m["et"], vn, _TN) for m, vn in zip(ms, v_new)]
            for hh in _HEADS:
                s_ref[hh] = sts[hh] * ms[hh]["gl"] + upd[hh]
                o_ref[pl.ds(ms[hh]["r0"], _C), _hsl(hh)] = qs[hh] + iv[hh]
            return carry

        lax.fori_loop(0, _SEG // _C, chunk, 0)

    col, bgs, grs, sts_spec = _chunk_specs(nseg, False)
    return pl.pallas_call(
        body, name="dn_chunk_fwd", grid=(bsz, nseg),
        in_specs=[col(0), col(1), col(2), bgs, grs], out_specs=[col(0), sts_spec],
        out_shape=[jax.ShapeDtypeStruct((t, DN), F32), jax.ShapeDtypeStruct((bsz, DN_HEADS, nc, DN_HD, DN_HD), F32)],
        scratch_shapes=[pltpu.VMEM((DN_HEADS, DN_HD, DN_HD), F32)],
    )(qkvn, qkvn, qkvn, bg, g_rows)


def _dn_chunk_bwd(qkvn, bg, g_rows, states, do, bsz):
    t = qkvn.shape[0]
    s = t // bsz
    nc, nseg = s // _C, s // _SEG

    def body(q_ref, k_ref, v_ref, bg_ref, gr_ref, st_ref, do_ref, dq_ref, dk_ref, dv_ref, dbg_ref, ds_ref):
        @pl.when(pl.program_id(1) == 0)
        def _():
            ds_ref[...] = jnp.zeros_like(ds_ref)

        def chunk(cc, carry):
            c = _SEG // _C - 1 - cc
            ms = [_chunk_pre(q_ref, k_ref, v_ref, bg_ref, gr_ref, c, hh) for hh in _HEADS]
            _chunk_solve(ms, True)
            ri, ci = ms[0]["ri"], ms[0]["ci"]
            for hh, m in enumerate(ms):
                m["st"] = st_ref[0, hh, c]
                m["ds_out"] = ds_ref[hh]
                m["do"] = do_ref[pl.ds(m["r0"], _C), _hsl(hh)]
                m["w"] = m["sol"][:, DN_HD:]
            for m in ms:
                m["v_new"] = m["sol"][:, :DN_HD] - _dot(m["w"], m["st"], _NN)
            for m in ms:
                m["q_dec"], m["k_tail"] = m["q"] * m["eg"], m["k"] * m["et"]
                m["dk_tail"] = _dot(m["v_new"], m["ds_out"], _NT)
                m["dv_new"] = _dot(m["k_tail"], m["ds_out"], _NN) + _dot(m["intra"], m["do"], _TN)
                m["dq_dec"] = _dot(m["do"], m["st"], _NT)
                m["ds_in"] = m["ds_out"] * m["gl"] + _dot(m["q_dec"], m["do"], _TN)
                m["dintra"] = jnp.where(ri >= ci, _dot(m["do"], m["v_new"], _NT), 0.0)
                m["dintra_t"] = jnp.where(ri <= ci, _dot(m["v_new"], m["do"], _NT), 0.0)
            for m in ms:
                m["dw"] = -_dot(m["dv_new"], m["st"], _NT)
                m["ds_in"] = m["ds_in"] - _dot(m["w"], m["dv_new"], _TN)
            for m in ms:
                dsol = jnp.concatenate([m["dv_new"], m["dw"]], axis=1)
                m["drhs"] = lax.dot_general(m["tm"], dsol, _TN, precision=HI, preferred_element_type=F32)
            for m in ms:
                m["dl"] = jnp.where(ri > ci, -_dot(m["drhs"], m["sol"], _NT), 0.0)
                m["dl_t"] = jnp.where(ri < ci, -_dot(m["sol"], m["drhs"], _NT), 0.0)
            for m in ms:
                m["dkb2"] = _dot(m["dl"] * m["decay"], m["k"], _NN)
                m["dk"] = _dot(m["dl_t"] * m["decay_t"], m["kb"], _NN) + _dot(m["dintra_t"] * m["decay_t"], m["q"], _NN)
                m["dq"] = _dot(m["dintra"] * m["decay"], m["k"], _NN)
            for m in ms:
                _chunk_bwd_finish(m)
            for m in ms:
                m["dg_b"] = jnp.dot(jnp.where(ri <= ci, 1.0, 0.0).astype(F32), m["dgc"] + jnp.zeros((_C, LANES), F32),
                                    precision=HI, preferred_element_type=F32)
            lane = lax.broadcasted_iota(jnp.int32, (_C, LANES), 1)
            for hh, m in enumerate(ms):
                rows = pl.ds(m["r0"], _C)
                dq_ref[rows, _hsl(hh)] = m["dq"] * (DN_HD ** -0.5)
                dk_ref[rows, _hsl(hh)] = m["dk"]
                dv_ref[rows, _hsl(hh)] = m["dv"]
                dbg_ref[rows, _hsl(hh)] = jnp.where(lane == 0, m["dbeta"], jnp.where(lane == 1, m["dg_b"], 0.0))
                ds_ref[hh] = m["ds_in"]
            return carry

        lax.fori_loop(0, _SEG // _C, chunk, 0)

    col, bgs, grs, sts_spec = _chunk_specs(nseg, True)
    o = jax.ShapeDtypeStruct((t, DN), F32)
    return pl.pallas_call(
        body, name="dn_chunk_bwd", grid=(bsz, nseg),
        in_specs=[col(0), col(1), col(2), bgs, grs, sts_spec, col(0)], out_specs=[col(0)] * 4, out_shape=[o] * 4,
        scratch_shapes=[pltpu.VMEM((DN_HEADS, DN_HD, DN_HD), F32)],
    )(qkvn, qkvn, qkvn, bg, g_rows, states, do)


def _chunk_bwd_finish(m):
    q, k, v, beta, decay, decay_t = m["q"], m["k"], m["v"], m["beta"], m["decay"], m["decay_t"]
    eg, et, gl, kb, dl, dl_t, dintra, dintra_t = m["eg"], m["et"], m["gl"], m["kb"], m["dl"], m["dl_t"], m["dintra"], m["dintra_t"]
    dq_dec, dk_tail, dq, dk = m["dq_dec"], m["dk_tail"], m["dq"], m["dk"]
    dgl = jnp.sum(jnp.sum(m["ds_out"] * m["st"], axis=1, keepdims=True), axis=0, keepdims=True)
    dvb, dkbeg = m["drhs"][:, :DN_HD], m["drhs"][:, DN_HD:]
    dkb = dkbeg * eg + m["dkb2"]
    deg = jnp.sum(dkbeg * kb, axis=1, keepdims=True)
    em = (dl * m["kk"] + dintra * m["qk"]) * decay
    em_t = (dl_t * m["kk_t"] + dintra_t * m["qk_t"]) * decay_t
    dgc = jnp.sum(em, axis=1, keepdims=True) - jnp.sum(em_t, axis=1, keepdims=True)
    dq = dq + dq_dec * eg
    deg = deg + jnp.sum(dq_dec * q, axis=1, keepdims=True)
    dk = dk + dk_tail * et
    det = jnp.sum(dk_tail * k, axis=1, keepdims=True)
    dgc = dgc + deg * eg - det * et
    dgc_last = jnp.sum(det * et, axis=0, keepdims=True) + dgl * gl
    rcol = lax.broadcasted_iota(jnp.int32, (_C, 1), 0)
    m["dgc"] = dgc + jnp.where(rcol == _C - 1, dgc_last, 0.0)
    m["dq"] = dq
    m["dk"] = dk + dkb * beta
    m["dbeta"] = jnp.sum(dkb * k, axis=1, keepdims=True) + jnp.sum(dvb * v, axis=1, keepdims=True)
    m["dv"] = dvb * beta


def _mod_fwd(c_all, ada_w_loc, ada_b_loc):
    n, cols = c_all.shape[0], ada_w_loc.shape[1]

    def body(c_ref, w_ref, b_ref, o_ref):
        o_ref[...] = _dot(jax.nn.silu(c_ref[...]), w_ref[...], _NN) + b_ref[...]

    return pl.pallas_call(body, name="mod_fwd", out_shape=jax.ShapeDtypeStruct((n, cols), F32))(c_all, ada_w_loc, ada_b_loc)


def _ada_grad(c_all, dmod_loc, dmod_all):
    d, cols = c_all.shape[1], dmod_loc.shape[1]

    def body(c_ref, dl_ref, da_ref, gw_ref, gb_ref):
        gw_ref[...] = _dot(jax.nn.silu(c_ref[...]), dl_ref[...], _TN)
        gb_ref[...] = jnp.sum(da_ref[...], axis=0, keepdims=True)

    return pl.pallas_call(body, name="ada_grad", out_shape=[jax.ShapeDtypeStruct((d, cols), F32),
                                                           jax.ShapeDtypeStruct((1, dmod_all.shape[1]), F32)])(c_all, dmod_loc, dmod_all)


def _row_tile(r):
    for tr in (256, 128, 64, 32, 16, 8):
        if r % tr == 0:
            return tr
    return r


def _adamw(w, m, v, grads, name):
    r, c = w.shape
    tr = _row_tile(r)
    blk = _vspec((tr, c), lambda i: (i, 0))
    n = len(grads)

    def body(*refs):
        w_ref, m_ref, v_ref = refs[:3]
        g_ref, d_ref, mo_ref, vo_ref = refs[3 + n:]
        g = refs[3][...]
        for p in refs[4:3 + n]:
            g = g + p[...]
        m_new = ADAM_B1 * m_ref[...] + (1.0 - ADAM_B1) * g
        v_new = ADAM_B2 * v_ref[...] + (1.0 - ADAM_B2) * jnp.square(g)
        m_hat = m_new / (1.0 - ADAM_B1 ** ADAM_STEP)
        v_hat = v_new / (1.0 - ADAM_B2 ** ADAM_STEP)
        g_ref[...] = g
        d_ref[...] = -ADAM_LR * (m_hat / (jnp.sqrt(v_hat) + ADAM_EPS) + ADAM_WD * w_ref[...])
        mo_ref[...] = m_new
        vo_ref[...] = v_new

    o = jax.ShapeDtypeStruct((r, c), F32)
    return pl.pallas_call(body, name=name, grid=(r // tr,), in_specs=[blk] * (3 + n), out_specs=[blk] * 4,
                          out_shape=[o] * 4)(w, m, v, *grads)


def _sum_lead(x, name):
    p, r, c = x.shape
    tr = _row_tile(r)

    def body(x_ref, o_ref):
        acc = x_ref[0].astype(F32)
        for i in range(1, p):
            acc = acc + x_ref[i].astype(F32)
        o_ref[...] = acc

    return pl.pallas_call(body, name=name, grid=(r // tr,), in_specs=[_vspec((p, tr, c), lambda i: (0, i, 0))],
                          out_specs=_vspec((tr, c), lambda i: (i, 0)), out_shape=jax.ShapeDtypeStruct((r, c), F32))(x)


def _allgather8(x_shard, name):
    m_per, n = x_shard.shape

    def body(x_ref, out_ref, send_sems, recv_sems, local_sem):
        x, y, c = lax.axis_index("x"), lax.axis_index("y"), lax.axis_index("c")
        me, sibling = (x, y, c), (x, y, 1 - c)
        chips = [(1 - x, y), (x, 1 - y), (1 - x, 1 - y)]

        def rows(px, py, pc):
            return out_ref.at[pl.ds((4 * px + 2 * py + pc) * m_per, m_per), :]

        def copy(k, block, to, src=None):
            return pltpu.make_async_remote_copy(
                src_ref=rows(*block) if src is None else src, dst_ref=rows(*block), send_sem=send_sems.at[k],
                recv_sem=recv_sems.at[k], device_id=to, device_id_type=MESH)

        mine = pltpu.make_async_copy(x_ref, rows(*me), local_sem)
        mine.start()
        first = [copy(0, me, sibling, src=x_ref)]
        first += [copy(1 + j, me, (*chip, c), src=x_ref) for j, chip in enumerate(chips)]
        for cp in first:
            cp.start()
        passed = [copy(4 + j, (*chip, c), sibling) for j, chip in enumerate(chips)]
        for j, chip in enumerate(chips):
            copy(1 + j, (*chip, c), me).wait_recv()
            passed[j].start()
        copy(0, sibling, me).wait_recv()
        for j, chip in enumerate(chips):
            copy(4 + j, (*chip, 1 - c), me).wait_recv()
        for cp in first + passed:
            cp.wait_send()
        mine.wait()

    return pl.pallas_call(
        body, name=name, out_shape=jax.ShapeDtypeStruct((8 * m_per, n), x_shard.dtype),
        in_specs=[pl.BlockSpec(memory_space=pltpu.VMEM)], out_specs=pl.BlockSpec(memory_space=pltpu.VMEM),
        scratch_shapes=[pltpu.SemaphoreType.DMA((7,)), pltpu.SemaphoreType.DMA((7,)), pltpu.SemaphoreType.DMA],
    )(x_shard)


_HBM = pl.BlockSpec(memory_space=pltpu.HBM)


def _chip_gather(shards, name):
    n = len(shards)

    def body(*refs):
        ins, outs = refs[:n], refs[n:2 * n]
        send_sems, recv_sems, local_sems = refs[2 * n:]
        x, y, c = lax.axis_index("x"), lax.axis_index("y"), lax.axis_index("c")
        me = 2 * x + y
        chips = [(1 - x, y), (x, 1 - y), (1 - x, 1 - y)]
        sends, locs = [], []
        for i in range(n):
            loc = pltpu.make_async_copy(ins[i], outs[i].at[me], local_sems.at[i])
            loc.start()
            locs.append(loc)
            for j, (px, py) in enumerate(chips):
                cp = pltpu.make_async_remote_copy(src_ref=ins[i], dst_ref=outs[i].at[me], send_sem=send_sems.at[3 * i + j],
                                                  recv_sem=recv_sems.at[3 * i + j], device_id=(px, py, c), device_id_type=MESH)
                cp.start()
                sends.append(cp)
        for i in range(n):
            for j, (px, py) in enumerate(chips):
                pltpu.make_async_remote_copy(src_ref=ins[i], dst_ref=outs[i].at[2 * px + py], send_sem=send_sems.at[3 * i + j],
                                             recv_sem=recv_sems.at[3 * i + j], device_id=(px, py, c), device_id_type=MESH).wait_recv()
        for cp in sends:
            cp.wait_send()
        for loc in locs:
            loc.wait()

    return pl.pallas_call(
        body, name=name, out_shape=[jax.ShapeDtypeStruct((N_CHIPS,) + a.shape, a.dtype) for a in shards],
        in_specs=[_HBM] * n, out_specs=[_HBM] * n,
        scratch_shapes=[pltpu.SemaphoreType.DMA((3 * n,)), pltpu.SemaphoreType.DMA((3 * n,)), pltpu.SemaphoreType.DMA((n,))],
    )(*shards)


def _piece_exchange(pieces, name):
    n = len(pieces)

    def body(*refs):
        ins, outs = refs[:n], refs[n:2 * n]
        send_sems, recv_sems, local_sems = refs[2 * n:]
        x, y, c = lax.axis_index("x"), lax.axis_index("y"), lax.axis_index("c")
        me = 2 * x + y
        chips = [(1 - x, y), (x, 1 - y), (1 - x, 1 - y)]
        sends, locs = [], []
        for i in range(n):
            loc = pltpu.make_async_copy(ins[i].at[me], outs[i].at[me], local_sems.at[i])
            loc.start()
            locs.append(loc)
            for j, (px, py) in enumerate(chips):
                cp = pltpu.make_async_remote_copy(src_ref=ins[i].at[2 * px + py], dst_ref=outs[i].at[me],
                                                  send_sem=send_sems.at[3 * i + j], recv_sem=recv_sems.at[3 * i + j],
                                                  device_id=(px, py, c), device_id_type=MESH)
                cp.start()
                sends.append(cp)
        for i in range(n):
            for j, (px, py) in enumerate(chips):
                pltpu.make_async_remote_copy(src_ref=ins[i].at[me], dst_ref=outs[i].at[2 * px + py], send_sem=send_sems.at[3 * i + j],
                                             recv_sem=recv_sems.at[3 * i + j], device_id=(px, py, c), device_id_type=MESH).wait_recv()
        for cp in sends:
            cp.wait_send()
        for loc in locs:
            loc.wait()

    return pl.pallas_call(
        body, name=name, out_shape=[jax.ShapeDtypeStruct(a.shape, a.dtype) for a in pieces],
        in_specs=[_HBM] * n, out_specs=[_HBM] * n,
        scratch_shapes=[pltpu.SemaphoreType.DMA((3 * n,)), pltpu.SemaphoreType.DMA((3 * n,)), pltpu.SemaphoreType.DMA((n,))],
    )(*pieces)


def _sibling_exchange(arrs, name):
    n = len(arrs)

    def body(*refs):
        ins, outs = refs[:n], refs[n:2 * n]
        send_sems, recv_sems = refs[2 * n:]
        sibling = (lax.axis_index("x"), lax.axis_index("y"), 1 - lax.axis_index("c"))
        cps = [pltpu.make_async_remote_copy(src_ref=ins[i], dst_ref=outs[i], send_sem=send_sems.at[i], recv_sem=recv_sems.at[i],
                                            device_id=sibling, device_id_type=MESH) for i in range(n)]
        for cp in cps:
            cp.start()
        for cp in cps:
            cp.wait()

    return pl.pallas_call(
        body, name=name, out_shape=[jax.ShapeDtypeStruct(a.shape, a.dtype) for a in arrs], in_specs=[_HBM] * n, out_specs=[_HBM] * n,
        scratch_shapes=[pltpu.SemaphoreType.DMA((n,)), pltpu.SemaphoreType.DMA((n,))],
    )(*arrs)


def _to_padded(w):
    z = jnp.zeros((w.shape[0], IN_PAD - IN_DIM), w.dtype)
    return jnp.concatenate([w[:, 768:2304], w[:, 0:512], w[:, 2304:2816], w[:, 2824:3848], w[:, 3848:4872],
                            w[:, 512:640], w[:, 640:768], w[:, 2816:2824], z], axis=1)


def _from_padded(g):
    return jnp.concatenate([g[:, C_Q:C_Q + AQ], g[:, C_K:C_K + AKV], g[:, C_V:C_V + AKV], g[:, C_DQKV:C_DQKV + 3 * DN],
                            g[:, C_DZ:C_DZ + DN], g[:, C_BD:C_BD + 2 * DN_HEADS], g[:, C_GA:C_GA + D_MODEL],
                            g[:, C_GD:C_GD + D_MODEL]], axis=1)


def _lane_vec(a):
    return jnp.zeros((1, LANES), F32).at[0, DN_HEADS:2 * DN_HEADS].set(a)


def _device_step(x2, tgt2, mod, p, bsz):
    d = D_MODEL
    sh1, sc1, g1, sh2, sc2, g2 = [mod[:, i * d:(i + 1) * d].reshape(bsz, 1, d) for i in range(N_MOD)]
    alog_v, dt_v = _lane_vec(p["dn_a_log"]), _lane_vec(p["dn_dt_bias"])
    sinks = p["attn_sinks"].reshape(1, AQ_HEADS)
    u1 = _pre_fwd(x2, p["norm_mix_pre"], sc1, sh1, "pre1_fwd")
    proj = _mm(u1, p["w_in"], "nn", F32, "mm_proj")
    bias = _bias_build(p["rel_bias"])
    y_attn = _attn_fwd(proj, bias, sinks, bsz)
    qkvn = _dn_prep_fwd(proj, p["dn_conv_w"], bsz)
    bg = _bg_fwd(proj, alog_v, dt_v, bsz)
    nc = x2.shape[0] // bsz // DN_CHUNK
    g_rows = jnp.transpose(bg[:, DN_HEADS:2 * DN_HEADS].reshape(bsz, nc, DN_CHUNK, DN_HEADS), (0, 3, 1, 2))
    o, states = _dn_chunk_fwd(qkvn, bg, g_rows, bsz)
    y_dn = _dn_out_fwd(o, proj, p["dn_norm_w"])
    ya = _mm(y_attn, p["w_attn_branch"], "nn", F32, "mm_ya")
    yd = _mm(y_dn, p["w_dn_branch"], "nn", F32, "mm_yd")
    merged = _merge_fwd(proj, ya, yd)
    y1 = _mm(merged, p["w_out"], "nn", F32, "mm_y1")
    h1 = _post_fwd(x2, y1, p["norm_mix_post"], g1, "post1_fwd")
    u2 = _pre_fwd(h1, p["norm_ffn_pre"], sc2, sh2, "pre2_fwd")
    up = _mm(u2, p["ffn_w_up"], "nn", F32, "mm_up")
    act = _ffn_act_fwd(up, p["ffn_conv_w"], bsz)
    y2 = _mm(act, p["ffn_w_down"], "nn", F32, "mm_y2")
    dh2, sq = _post_loss(h1, y2, p["norm_ffn_post"], g2, tgt2, "post2_loss")
    g = {}
    dy2, g["norm_ffn_post"], dg2 = _post_bwd(dh2, y2, p["norm_ffn_post"], g2, "post2_bwd")
    dact = _mm(dy2, p["ffn_w_down"], "nt", F32, "mm_dact")
    g["ffn_w_down"] = _mm(act, dy2, "tn", F32, "mm_dwdown")
    dupg, dupv, dcwg, dcwv = _ffn_act_bwd(up, p["ffn_conv_w"], dact, bsz)
    g["ffn_conv_w"] = jnp.concatenate([dcwg, dcwv], axis=1)
    dup = jnp.concatenate([dupg, dupv], axis=1)
    g["ffn_w_up"] = _mm(u2, dup, "tn", F32, "mm_dwup")
    du2 = _mm(dup, p["ffn_w_up"], "nt", F32, "mm_du2")
    dh1, g["norm_ffn_pre"], dsc2, dsh2 = _pre_bwd(h1, p["norm_ffn_pre"], sc2, sh2, du2, dh2, "pre2_bwd")
    dy1, g["norm_mix_post"], dg1 = _post_bwd(dh1, y1, p["norm_mix_post"], g1, "post1_bwd")
    dmerged = _mm(dy1, p["w_out"], "nt", F32, "mm_dmerged")
    g["w_out"] = _mm(merged, dy1, "tn", F32, "mm_dwout")
    dga, dgd, dya, dyd = _merge_bwd(proj, ya, yd, dmerged)
    dy_attn = _mm(dya, p["w_attn_branch"], "nt", BF16, "mm_dyattn")
    g["w_attn_branch"] = _mm(y_attn, dya, "tn", F32, "mm_dwab")
    dy_dn = _mm(dyd, p["w_dn_branch"], "nt", F32, "mm_dydn")
    g["w_dn_branch"] = _mm(y_dn, dyd, "tn", F32, "mm_dwdb")
    do, dz, g["dn_norm_w"] = _dn_out_bwd(o, proj, p["dn_norm_w"], dy_dn)
    dqn, dkn, dvv, dbg4 = _dn_chunk_bwd(qkvn, bg, g_rows, states, do, bsz)
    d_dqkv, g["dn_conv_w"] = _dn_prep_bwd(proj, p["dn_conv_w"], jnp.concatenate([dqn, dkn, dvv], axis=1), bsz)
    dbd, g["dn_a_log"], g["dn_dt_bias"] = _bg_bwd(proj, alog_v, dt_v, dbg4, bsz)
    dq, dk, dv, dbias, g["attn_sinks"] = _attn_bwd(proj, bias, sinks, dy_attn, bsz)
    g["rel_bias"] = _bias_grad(dbias)
    dproj = jnp.concatenate([d_dqkv, dq, dz, dga, dgd, dk, dv, dbd], axis=1)
    g["w_in"] = _mm(u1, dproj, "tn", F32, "mm_dwin")
    du1 = _mm(dproj, p["w_in"], "nt", F32, "mm_du1")
    dx, g["norm_mix_pre"], dsc1, dsh1 = _pre_bwd(x2, p["norm_mix_pre"], sc1, sh1, du1, dh1, "pre1_bwd")
    dmod = jnp.concatenate([dsh1, dsc1, dg1, dsh2, dsc2, dg2], axis=-1).reshape(bsz, N_MOD * d)
    return sq, dx, dmod, g


_SMALL = (("norm_mix_pre", D_MODEL), ("norm_mix_post", D_MODEL), ("norm_ffn_pre", D_MODEL), ("norm_ffn_post", D_MODEL),
          ("dn_norm_w", DN_HD), ("dn_a_log", LANES), ("dn_dt_bias", LANES), ("attn_sinks", AQ_HEADS * LANES),
          ("rel_bias", AQ_HEADS * LANES), ("dn_conv_w", DN_CONV * 3 * DN), ("ffn_conv_w", FFN_CONV * 2 * D_FF))


def _pack_rows(parts, rows):
    flat = jnp.concatenate([a.reshape(-1) for a in parts])
    return jnp.concatenate([flat, jnp.zeros((rows * LANES - flat.shape[0],), F32)]).reshape(rows, LANES)


def _pad128(a):
    flat = a.reshape(-1)
    n = -(-flat.shape[0] // LANES) * LANES
    return jnp.concatenate([flat, jnp.zeros((n - flat.shape[0],), F32)]) if n != flat.shape[0] else flat


_W_NAMES = ("ada_w", "ada_b", "norm_mix_pre", "norm_mix_post", "norm_ffn_pre", "norm_ffn_post", "w_in", "dn_conv_w", "dn_a_log",
            "dn_dt_bias", "dn_norm_w", "attn_sinks", "rel_bias", "w_attn_branch", "w_dn_branch", "w_out", "ffn_w_up", "ffn_conv_w",
            "ffn_w_down")
_BIG = ("w_in", "w_attn_branch", "w_dn_branch", "w_out", "ffn_w_up", "ffn_w_down")
_ROW_SHARDED = ("w_out", "ffn_w_down")


def kernel(x, c, *rest):
    nw = len(_W_NAMES)
    w = dict(zip(_W_NAMES, rest[:nw]))
    loss_target = rest[nw]
    m = dict(zip(_W_NAMES, rest[nw + 1:2 * nw + 1]))
    v = dict(zip(_W_NAMES, rest[2 * nw + 1:3 * nw + 1]))
    ix, iy, ic = lax.axis_index("x"), lax.axis_index("y"), lax.axis_index("c")
    chip, dev = 2 * ix + iy, 4 * ix + 2 * iy + ic
    bsz, s, d = x.shape
    t = bsz * s
    n_dev = 8

    front_rows = 64
    front = _pack_rows([c, w["dn_conv_w"], w["ffn_conv_w"]], front_rows)
    front_all = _allgather8(front, "ag_front").reshape(n_dev, front_rows * LANES)
    n_c, n_dc, n_fc = bsz * d, DN_CONV * 3 * DN // N_CHIPS, FFN_CONV * 2 * D_FF // N_CHIPS
    c_all = front_all[:, :n_c].reshape(n_dev * bsz, d)
    per_chip = front_all[0::2]
    dn_conv_full = jnp.transpose(per_chip[:, n_c:n_c + n_dc].reshape(N_CHIPS, DN_CONV, -1), (1, 0, 2)).reshape(DN_CONV, 3 * DN)
    ffn_conv_full = jnp.transpose(per_chip[:, n_c + n_dc:n_c + n_dc + n_fc].reshape(N_CHIPS, FFN_CONV, -1), (1, 0, 2)).reshape(FFN_CONV, 2 * D_FF)

    mod_cols = N_MOD * d // N_CHIPS
    ada_b_loc = lax.dynamic_slice(w["ada_b"], (0, chip * mod_cols), (1, mod_cols))
    mod_part = _mod_fwd(c_all, w["ada_w"][0], ada_b_loc)
    mod_all = _allgather8(mod_part, "ag_mod").reshape(n_dev, n_dev * bsz, mod_cols)[0::2]
    mod = jnp.transpose(lax.dynamic_slice(mod_all, (0, dev * bsz, 0), (N_CHIPS, bsz, mod_cols)), (1, 0, 2)).reshape(bsz, N_MOD * d)

    gathered = _chip_gather([w[k][0].astype(BF16) for k in _BIG], "gather_weights")
    p = {}
    for k, a in zip(_BIG, gathered):
        if k in _ROW_SHARDED:
            p[k] = a.reshape(-1, a.shape[2])
        else:
            p[k] = jnp.transpose(a, (1, 0, 2)).reshape(a.shape[1], -1)
    p["w_in"] = _to_padded(p["w_in"])
    for k in ("norm_mix_pre", "norm_mix_post", "norm_ffn_pre", "norm_ffn_post", "dn_norm_w", "attn_sinks"):
        p[k] = w[k]
    p["dn_a_log"], p["dn_dt_bias"], p["rel_bias"] = w["dn_a_log"][0], w["dn_dt_bias"][0], w["rel_bias"]
    p["dn_conv_w"], p["ffn_conv_w"] = dn_conv_full, ffn_conv_full

    sq, dx, dmod, g = _device_step(x.reshape(t, d), loss_target.reshape(t, d), mod, p, bsz)
    loss = lax.psum(0.5 * jnp.sum(sq), ("x", "y", "c"))

    g["dn_a_log"], g["dn_dt_bias"] = g["dn_a_log"].reshape(-1), g["dn_dt_bias"].reshape(-1)
    small_rows = 328
    small = _pack_rows([dmod] + [g[k] for k, _ in _SMALL], small_rows)
    small_all = _allgather8(small, "ag_small").reshape(n_dev, small_rows, LANES)
    n_dm = bsz * N_MOD * d
    dmod_all = small_all.reshape(n_dev, -1)[:, :n_dm].reshape(n_dev * bsz, N_MOD * d)
    tot = _sum_lead(small_all, "sum_small").reshape(-1)
    gs, off = {}, n_dm
    for k, n in _SMALL:
        gs[k] = tot[off:off + n]
        off += n
    grad = {}
    grad["ada_w"], grad["ada_b"] = _ada_grad(c_all, lax.dynamic_slice(dmod_all, (0, chip * mod_cols), (n_dev * bsz, mod_cols)), dmod_all)
    for k in ("norm_mix_pre", "norm_mix_post", "norm_ffn_pre", "norm_ffn_post", "dn_norm_w"):
        grad[k] = gs[k]
    grad["dn_a_log"] = gs["dn_a_log"][DN_HEADS:2 * DN_HEADS]
    grad["dn_dt_bias"] = gs["dn_dt_bias"][DN_HEADS:2 * DN_HEADS]
    grad["attn_sinks"] = gs["attn_sinks"].reshape(AQ_HEADS, LANES)[:, 0]
    grad["rel_bias"] = gs["rel_bias"].reshape(AQ_HEADS, LANES)[:, :REL_BUCKETS].T
    grad["dn_conv_w"] = lax.dynamic_slice(gs["dn_conv_w"].reshape(DN_CONV, 3 * DN), (0, chip * (3 * DN // N_CHIPS)), (DN_CONV, 3 * DN // N_CHIPS))
    grad["ffn_conv_w"] = lax.dynamic_slice(gs["ffn_conv_w"].reshape(FFN_CONV, 2 * D_FF), (0, chip * (2 * D_FF // N_CHIPS)), (FFN_CONV, 2 * D_FF // N_CHIPS))

    g["w_in"] = _from_padded(g["w_in"])
    pieces = []
    for k in _BIG:
        a = g[k]
        if k in _ROW_SHARDED:
            pieces.append(a.reshape(N_CHIPS, a.shape[0] // N_CHIPS, a.shape[1]).astype(BF16))
        else:
            pieces.append(jnp.transpose(a.reshape(a.shape[0], N_CHIPS, a.shape[1] // N_CHIPS), (1, 0, 2)).astype(BF16))
    received = _piece_exchange(pieces, "exchange_grads")
    mine = [_sum_lead(r, "sum_" + k) for k, r in zip(_BIG, received)]
    theirs = _sibling_exchange(mine, "exchange_cores")

    out = {}
    for k, a, b in zip(_BIG, mine, theirs):
        out[k] = _adamw(w[k][0], m[k][0], v[k][0], [a, b], "adamw_" + k)
    out["ada_w"] = _adamw(w["ada_w"][0], m["ada_w"][0], v["ada_w"][0], [grad["ada_w"]], "adamw_ada_w")
    small_names = [k for k in _W_NAMES if k not in _BIG and k != "ada_w"]
    offs, n_tot = {}, 0
    for k in small_names:
        offs[k] = n_tot
        n_tot += -(-w[k].size // LANES) * LANES
    pack = lambda dct: jnp.concatenate([_pad128(dct[k]) for k in small_names]).reshape(-1, LANES)
    rows_s = n_tot // LANES
    pad_rows = -(-rows_s // 8) * 8 - rows_s
    padr = lambda a: jnp.concatenate([a, jnp.zeros((pad_rows, LANES), F32)]) if pad_rows else a
    res_small = _adamw(padr(pack(w)), padr(pack(m)), padr(pack(v)), [padr(pack(grad))], "adamw_small")
    for k in small_names:
        out[k] = [r.reshape(-1)[offs[k]:offs[k] + w[k].size].reshape(w[k].shape) for r in res_small]
    for k in _BIG + ("ada_w",):
        out[k] = [r.reshape(w[k].shape) for r in out[k]]
    grads, deltas, new_m, new_v = ([out[k][i] for k in _W_NAMES] for i in range(4))
    return (loss, dx.reshape(bsz, s, d), *grads, *deltas, *new_m, *new_v)
```

```python
import functools
import math

import numpy as np
import jax
import jax.numpy as jnp
from jax import lax
from jax.experimental import pallas as pl
from jax.experimental.pallas import tpu as pltpu

F32 = jnp.float32
BF16 = jnp.bfloat16
HI = lax.Precision.HIGHEST
MESH = pl.DeviceIdType.MESH

D_MODEL = 1024
N_MOD = 6
AQ_HEADS, AKV_HEADS, A_HD, WINDOW = 8, 2, 64, 128
REL_BUCKETS, REL_MAX_DIST = 32, 128
DN_HEADS, DN_HD, DN_CONV, DN_CHUNK = 4, 128, 4, 64
D_FF, FFN_CONV = 2816, 3
RMS_EPS, L2_EPS, NEG_INF = 1e-6, 1e-6, -1e30
AQ, AKV, DN = AQ_HEADS * A_HD, AKV_HEADS * A_HD, DN_HEADS * DN_HD
IN_DIM = AQ + 2 * AKV + 3 * DN + DN + 2 * DN_HEADS + 2 * D_MODEL
C_DQKV, C_Q, C_DZ, C_GA, C_GD, C_K, C_V, C_BD = 0, 1536, 2048, 2560, 3584, 4608, 4736, 4864
IN_PAD = 4992
LANES = 128
N_CHIPS = 4

ADAM_LR, ADAM_B1, ADAM_B2, ADAM_EPS, ADAM_WD, ADAM_STEP = 0.001, 0.9, 0.999, 1e-08, 0.01, 10


def _pick(n, cap):
    best = None
    for t in range(LANES, cap + 1, LANES):
        if n % t == 0:
            best = t
    return best if best is not None else n


def _vspec(shape, index_map):
    return pl.BlockSpec(shape, index_map)


MM_VMEM_BUDGET = 40 * 2 ** 20
GRID_STEP_S = 0.35e-6
HBM_BYTES_PER_S = 3.0e12


def _mm_tiles(m, n, k, mode, in_bytes, out_bytes):
    best = None
    for tm in [t for t in range(LANES, m + 1, LANES) if m % t == 0]:
        for tn in [t for t in range(LANES, n + 1, LANES) if n % t == 0]:
            a, b, o = k * tm * in_bytes, k * tn * in_bytes, tm * tn * out_bytes
            if 2 * (a + b + o) + (a if mode == "tn" else 0) > MM_VMEM_BUDGET:
                continue
            cost = (m // tm) * (n // tn) * GRID_STEP_S + (m * k * in_bytes + (m // tm) * n * k * in_bytes + m * n * out_bytes) / HBM_BYTES_PER_S
            if best is None or cost < best[0]:
                best = (cost, tm, tn)
    return best[1], best[2]


def _mm(a, b, mode, out_dtype, name, plug=None):
    if mode == "nn":
        (m, k), n = a.shape, b.shape[1]
        dims = (((1,), (0,)), ((), ()))
    elif mode == "nt":
        (m, k), n = a.shape, b.shape[0]
        dims = (((1,), (1,)), ((), ()))
    else:
        (k, m), n = a.shape, b.shape[1]
        dims = (((0,), (0,)), ((), ()))
    tm, tn = _mm_tiles(m, n, k, mode, a.dtype.itemsize, jnp.dtype(out_dtype).itemsize)
    if mode == "tn":
        a_spec = _vspec((k, tm), lambda i, j: (0, i))
    else:
        a_spec = _vspec((tm, k), lambda i, j: (i, 0))
    if mode == "nt":
        b_spec = _vspec((tn, k), lambda i, j: (j, 0))
    else:
        b_spec = _vspec((k, tn), lambda i, j: (0, j))

    def body(a_ref, b_ref, o_ref):
        o_ref[...] = lax.dot_general(a_ref[...].astype(BF16), b_ref[...].astype(BF16), dims,
                                     preferred_element_type=F32).astype(out_dtype)

    grid = (m // tm, n // tn)
    (out,), extra = _plugged_call(body, plug, _grid_ends(grid), (a, b), name=name, grid=grid, in_specs=[a_spec, b_spec],
                                  out_specs=[_vspec((tm, tn), lambda i, j: (i, j))], out_shape=[jax.ShapeDtypeStruct((m, n), out_dtype)])
    return out if plug is None else (out, extra)


def _rms(x, w):
    return (x * lax.rsqrt(jnp.mean(x * x, axis=-1, keepdims=True) + RMS_EPS)) * w


def _pre_f(x, w, sc, sh):
    return _rms(x, w) * (1.0 + sc) + sh


def _post_f(y, w, g):
    return g * _rms(y, w)


def _tok_grid(t, bsz, ts):
    nt = t // bsz // ts
    return nt, (bsz, nt)


def _pre_fwd(x, w, sc, sh, name, ts=512):
    t, d = x.shape
    bsz = sc.shape[0]
    nt, grid = _tok_grid(t, bsz, ts)
    row = _vspec((ts, d), lambda b, i: (b * nt + i, 0))
    vec = _vspec((1, d), lambda b, i: (0, 0))
    bvec = _vspec((1, 1, d), lambda b, i: (b, 0, 0))

    def body(x_ref, w_ref, sc_ref, sh_ref, u_ref):
        u_ref[...] = _pre_f(x_ref[...], w_ref[...], sc_ref[0], sh_ref[0]).astype(BF16)

    return pl.pallas_call(body, name=name, grid=grid, in_specs=[row, vec, bvec, bvec], out_specs=row,
                          out_shape=jax.ShapeDtypeStruct((t, d), BF16))(x, w, sc, sh)


def _pre_bwd(x, w, sc, sh, du, dres, name, ts=512):
    t, d = x.shape
    bsz = sc.shape[0]
    nt, grid = _tok_grid(t, bsz, ts)
    row = _vspec((ts, d), lambda b, i: (b * nt + i, 0))
    vec = _vspec((1, d), lambda b, i: (0, 0))
    bvec = _vspec((1, 1, d), lambda b, i: (b, 0, 0))

    def body(x_ref, w_ref, sc_ref, sh_ref, du_ref, dres_ref, dx_ref, dw_ref, dsc_ref, dsh_ref):
        b, i = pl.program_id(0), pl.program_id(1)
        _, vjp = jax.vjp(_pre_f, x_ref[...], w_ref[...], sc_ref[0], sh_ref[0])
        dx, dw, dsc, dsh = vjp(du_ref[...])
        dx_ref[...] = dres_ref[...] + dx

        @pl.when((b == 0) & (i == 0))
        def _():
            dw_ref[...] = jnp.zeros_like(dw_ref)

        @pl.when(i == 0)
        def _():
            dsc_ref[...] = jnp.zeros_like(dsc_ref)
            dsh_ref[...] = jnp.zeros_like(dsh_ref)

        dw_ref[...] += dw
        dsc_ref[0] += dsc
        dsh_ref[0] += dsh

    return pl.pallas_call(
        body, name=name, grid=grid, in_specs=[row, vec, bvec, bvec, row, row], out_specs=[row, vec, bvec, bvec],
        out_shape=[jax.ShapeDtypeStruct((t, d), F32), jax.ShapeDtypeStruct((1, d), F32),
                   jax.ShapeDtypeStruct((bsz, 1, d), F32), jax.ShapeDtypeStruct((bsz, 1, d), F32)],
    )(x, w, sc, sh, du, dres)


def _post_fwd(res, y, w, g, name, ts=512):
    t, d = y.shape
    bsz = g.shape[0]
    nt, grid = _tok_grid(t, bsz, ts)
    row = _vspec((ts, d), lambda b, i: (b * nt + i, 0))
    vec = _vspec((1, d), lambda b, i: (0, 0))
    bvec = _vspec((1, 1, d), lambda b, i: (b, 0, 0))

    def body(res_ref, y_ref, w_ref, g_ref, h_ref):
        h_ref[...] = res_ref[...] + _post_f(y_ref[...], w_ref[...], g_ref[0])

    return pl.pallas_call(body, name=name, grid=grid, in_specs=[row, row, vec, bvec], out_specs=row,
                          out_shape=jax.ShapeDtypeStruct((t, d), F32))(res, y, w, g)


def _post_loss(res, y, w, g, tgt, name, ts=512):
    t, d = y.shape
    bsz = g.shape[0]
    nt, grid = _tok_grid(t, bsz, ts)
    row = _vspec((ts, d), lambda b, i: (b * nt + i, 0))
    vec = _vspec((1, d), lambda b, i: (0, 0))
    bvec = _vspec((1, 1, d), lambda b, i: (b, 0, 0))
    acc = _vspec((1, d), lambda b, i: (0, 0))

    def body(res_ref, y_ref, w_ref, g_ref, tgt_ref, dh_ref, sq_ref):
        b, i = pl.program_id(0), pl.program_id(1)
        e = res_ref[...] + _post_f(y_ref[...], w_ref[...], g_ref[0]) - tgt_ref[...]
        dh_ref[...] = e * (1.0 / d)

        @pl.when((b == 0) & (i == 0))
        def _():
            sq_ref[...] = jnp.zeros_like(sq_ref)

        sq_ref[...] += jnp.sum(e * e, axis=0, keepdims=True) * (1.0 / d)

    return pl.pallas_call(
        body, name=name, grid=grid, in_specs=[row, row, vec, bvec, row], out_specs=[row, acc],
        out_shape=[jax.ShapeDtypeStruct((t, d), F32), jax.ShapeDtypeStruct((1, d), F32)],
    )(res, y, w, g, tgt)


def _post_bwd(dh, y, w, g, name, ts=512):
    t, d = y.shape
    bsz = g.shape[0]
    nt, grid = _tok_grid(t, bsz, ts)
    row = _vspec((ts, d), lambda b, i: (b * nt + i, 0))
    vec = _vspec((1, d), lambda b, i: (0, 0))
    bvec = _vspec((1, 1, d), lambda b, i: (b, 0, 0))

    def body(dh_ref, y_ref, w_ref, g_ref, dy_ref, dw_ref, dg_ref):
        b, i = pl.program_id(0), pl.program_id(1)
        _, vjp = jax.vjp(_post_f, y_ref[...], w_ref[...], g_ref[0])
        dy, dw, dg = vjp(dh_ref[...])
        dy_ref[...] = dy.astype(BF16)

        @pl.when((b == 0) & (i == 0))
        def _():
            dw_ref[...] = jnp.zeros_like(dw_ref)

        @pl.when(i == 0)
        def _():
            dg_ref[...] = jnp.zeros_like(dg_ref)

        dw_ref[...] += dw
        dg_ref[0] += dg

    return pl.pallas_call(
        body, name=name, grid=grid, in_specs=[row, row, vec, bvec], out_specs=[row, vec, bvec],
        out_shape=[jax.ShapeDtypeStruct((t, d), BF16), jax.ShapeDtypeStruct((1, d), F32),
                   jax.ShapeDtypeStruct((bsz, 1, d), F32)],
    )(dh, y, w, g)


def _merge_f(ga, gd, ya, yd):
    return jax.nn.sigmoid(ga) * ya + jax.nn.sigmoid(gd) * yd


_MW = 512


def _merge_fwd(proj, ya, yd, ts=512):
    t, d = ya.shape
    blk = _vspec((ts, _MW), lambda i, j: (i, j))
    ga = _vspec((ts, _MW), lambda i, j: (i, C_GA // _MW + j))
    gd = _vspec((ts, _MW), lambda i, j: (i, C_GD // _MW + j))

    def body(ga_ref, gd_ref, ya_ref, yd_ref, o_ref):
        o_ref[...] = _merge_f(ga_ref[...], gd_ref[...], ya_ref[...], yd_ref[...]).astype(BF16)

    return pl.pallas_call(body, name="merge_fwd", grid=(t // ts, d // _MW), in_specs=[ga, gd, blk, blk], out_specs=blk,
                          out_shape=jax.ShapeDtypeStruct((t, d), BF16))(proj, proj, ya, yd)


def _merge_bwd(proj, ya, yd, dm, ts=512):
    t, d = ya.shape
    blk = _vspec((ts, _MW), lambda i, j: (i, j))
    ga = _vspec((ts, _MW), lambda i, j: (i, C_GA // _MW + j))
    gd = _vspec((ts, _MW), lambda i, j: (i, C_GD // _MW + j))

    def body(ga_ref, gd_ref, ya_ref, yd_ref, dm_ref, dga_ref, dgd_ref, dya_ref, dyd_ref):
        _, vjp = jax.vjp(_merge_f, ga_ref[...], gd_ref[...], ya_ref[...], yd_ref[...])
        dga, dgd, dya, dyd = vjp(dm_ref[...])
        dga_ref[...] = dga.astype(BF16)
        dgd_ref[...] = dgd.astype(BF16)
        dya_ref[...] = dya.astype(BF16)
        dyd_ref[...] = dyd.astype(BF16)

    o = jax.ShapeDtypeStruct((t, d), BF16)
    return pl.pallas_call(body, name="merge_bwd", grid=(t // ts, d // _MW), in_specs=[ga, gd, blk, blk, blk],
                          out_specs=[blk] * 4, out_shape=[o] * 4)(proj, proj, ya, yd, dm)


def _shift_down(x, s):
    if s == 0:
        return x
    r = lax.broadcasted_iota(jnp.int32, x.shape, 0)
    return jnp.where(r >= s, pltpu.roll(x, s, 0), 0.0)


def _shift_up(x, s):
    if s == 0:
        return x
    n = x.shape[0]
    r = lax.broadcasted_iota(jnp.int32, x.shape, 0)
    return jnp.where(r < n - s, pltpu.roll(x, n - s, 0), 0.0)


def _conv_fwd(x, w, k):
    out = None
    for j in range(k):
        term = w[j:j + 1, :] * _shift_down(x, k - 1 - j)
        out = term if out is None else out + term
    return out


def _conv_bwd(x, w, dc, k):
    dx = None
    dws = []
    for j in range(k):
        s = k - 1 - j
        term = w[j:j + 1, :] * _shift_up(dc, s)
        dx = term if dx is None else dx + term
        dws.append(jnp.sum(dc * _shift_down(x, s), axis=0, keepdims=True))
    return dx, jnp.concatenate(dws, axis=0)


def _geglu_f(gate, val):
    return jax.nn.gelu(gate, approximate=True) * val


_FW = 256


def _ffn_act_fwd(up, conv_w, bsz):
    t = up.shape[0]
    s = t // bsz
    nj = D_FF // _FW
    xg = _vspec((s, _FW), lambda b, j: (b, j))
    xv = _vspec((s, _FW), lambda b, j: (b, nj + j))
    wg = _vspec((FFN_CONV, _FW), lambda b, j: (0, j))
    wv = _vspec((FFN_CONV, _FW), lambda b, j: (0, nj + j))

    def body(xg_ref, xv_ref, wg_ref, wv_ref, o_ref):
        gate = _conv_fwd(xg_ref[...], wg_ref[...], FFN_CONV)
        val = _conv_fwd(xv_ref[...], wv_ref[...], FFN_CONV)
        o_ref[...] = _geglu_f(gate, val).astype(BF16)

    return pl.pallas_call(body, name="ffn_act_fwd", grid=(bsz, nj), in_specs=[xg, xv, wg, wv],
                          out_specs=_vspec((s, _FW), lambda b, j: (b, j)),
                          out_shape=jax.ShapeDtypeStruct((t, D_FF), BF16))(up, up, conv_w, conv_w)


def _ffn_act_bwd(up, conv_w, dact, bsz):
    t = up.shape[0]
    s = t // bsz
    nj = D_FF // _FW
    xg = _vspec((s, _FW), lambda j, b: (b, j))
    xv = _vspec((s, _FW), lambda j, b: (b, nj + j))
    wg = _vspec((FFN_CONV, _FW), lambda j, b: (0, j))
    wv = _vspec((FFN_CONV, _FW), lambda j, b: (0, nj + j))
    da = _vspec((s, _FW), lambda j, b: (b, j))
    dwo = _vspec((FFN_CONV, _FW), lambda j, b: (0, j))

    def body(xg_ref, xv_ref, wg_ref, wv_ref, da_ref, dxg_ref, dxv_ref, dwg_ref, dwv_ref):
        b = pl.program_id(1)
        xg_, xv_, wg_, wv_ = xg_ref[...], xv_ref[...], wg_ref[...], wv_ref[...]
        gate = _conv_fwd(xg_, wg_, FFN_CONV)
        val = _conv_fwd(xv_, wv_, FFN_CONV)
        _, vjp = jax.vjp(_geglu_f, gate, val)
        dgate, dval = vjp(da_ref[...])
        dxg, dwg = _conv_bwd(xg_, wg_, dgate, FFN_CONV)
        dxv, dwv = _conv_bwd(xv_, wv_, dval, FFN_CONV)
        dxg_ref[...] = dxg.astype(BF16)
        dxv_ref[...] = dxv.astype(BF16)

        @pl.when(b == 0)
        def _():
            dwg_ref[...] = jnp.zeros_like(dwg_ref)
            dwv_ref[...] = jnp.zeros_like(dwv_ref)

        dwg_ref[...] += dwg
        dwv_ref[...] += dwv

    return pl.pallas_call(
        body, name="ffn_act_bwd", grid=(nj, bsz), in_specs=[xg, xv, wg, wv, da], out_specs=[da, da, dwo, dwo],
        out_shape=[jax.ShapeDtypeStruct((t, D_FF), BF16), jax.ShapeDtypeStruct((t, D_FF), BF16),
                   jax.ShapeDtypeStruct((FFN_CONV, D_FF), F32), jax.ShapeDtypeStruct((FFN_CONV, D_FF), F32)],
    )(up, up, conv_w, conv_w, dact)


def _bucket_table():
    qi = np.arange(WINDOW)[:, None]
    kj = np.arange(2 * WINDOW)[None, :]
    dist = WINDOW + qi - kj
    dc = np.maximum(dist, 0)
    max_exact = REL_BUCKETS // 2
    scaled = np.log(np.maximum(dc, 1).astype(np.float32) / np.float32(max_exact)) / np.float32(math.log(REL_MAX_DIST / max_exact))
    large = max_exact + (scaled.astype(np.float32) * np.float32(REL_BUCKETS - max_exact)).astype(np.int32)
    large = np.minimum(large, REL_BUCKETS - 1)
    bucket = np.where(dc < max_exact, dc, large).astype(np.int32)
    in_band = ((dist >= 0) & (dist < WINDOW)).astype(np.int32)
    return bucket, in_band


def _bias_build(rel_bias):
    bucket, _ = _bucket_table()

    def body(rb_ref, idx_ref, o_ref):
        h = pl.program_id(0)
        idx = idx_ref[...]
        acc = jnp.zeros(idx.shape, F32)
        for r in range(REL_BUCKETS):
            acc = jnp.where(idx == r, rb_ref[r, h], acc)
        o_ref[0] = acc

    return pl.pallas_call(
        body, name="bias_build", grid=(AQ_HEADS,),
        in_specs=[pl.BlockSpec(memory_space=pltpu.SMEM), _vspec((WINDOW, 2 * WINDOW), lambda h: (0, 0))],
        out_specs=_vspec((1, WINDOW, 2 * WINDOW), lambda h: (h, 0, 0)),
        out_shape=jax.ShapeDtypeStruct((AQ_HEADS, WINDOW, 2 * WINDOW), F32),
    )(rel_bias, jnp.asarray(bucket))


def _bias_grad(dbias):
    bucket, _ = _bucket_table()

    def body(db_ref, idx_ref, o_ref):
        idx = idx_ref[...]
        db = db_ref[0]
        lane = lax.broadcasted_iota(jnp.int32, (1, LANES), 1)
        acc = jnp.zeros((1, LANES), F32)
        for r in range(REL_BUCKETS):
            s = jnp.sum(jnp.sum(jnp.where(idx == r, db, 0.0), axis=1, keepdims=True), axis=0, keepdims=True)
            acc = jnp.where(lane == r, s, acc)
        o_ref[0] = acc

    return pl.pallas_call(
        body, name="bias_grad", grid=(AQ_HEADS,),
        in_specs=[_vspec((1, WINDOW, 2 * WINDOW), lambda h: (h, 0, 0)), _vspec((WINDOW, 2 * WINDOW), lambda h: (0, 0))],
        out_specs=_vspec((1, 1, LANES), lambda h: (h, 0, 0)),
        out_shape=jax.ShapeDtypeStruct((AQ_HEADS, 1, LANES), F32),
    )(dbias, jnp.asarray(bucket))


def _attn_mask(n):
    qi = lax.broadcasted_iota(jnp.int32, (WINDOW, 2 * WINDOW), 0)
    kj = lax.broadcasted_iota(jnp.int32, (WINDOW, 2 * WINDOW), 1)
    dist = WINDOW + qi - kj
    band = (dist >= 0) & (dist < WINDOW)
    return band & ((kj >= WINDOW) | (n > 0))


def _attn_probs(q, kb, bias, sink, mask):
    s = lax.dot_general(q, kb, (((1,), (1,)), ((), ())), preferred_element_type=F32) * (A_HD ** -0.5)
    s = jnp.where(mask, s + bias, NEG_INF)
    m = jnp.maximum(jnp.max(s, axis=-1, keepdims=True), sink)
    p = jnp.exp(s - m)
    es = jnp.exp(sink - m)
    inv = 1.0 / (jnp.sum(p, axis=-1, keepdims=True) + es)
    return p * inv, es * inv


def _attn_fwd(proj, bias, sinks, bsz):
    t = proj.shape[0]
    s = t // bsz
    nb = s // WINDOW
    grp = AQ_HEADS // AKV_HEADS

    def body(q_ref, k_ref, v_ref, bias_ref, sink_ref, y_ref, kp_ref, vp_ref):
        kp_ref[0:WINDOW, :] = jnp.zeros((WINDOW, LANES), BF16)
        vp_ref[0:WINDOW, :] = jnp.zeros((WINDOW, LANES), BF16)
        kp_ref[WINDOW:, :] = k_ref[...].astype(BF16)
        vp_ref[WINDOW:, :] = v_ref[...].astype(BF16)

        def blk(n, carry):
            r0 = pl.multiple_of(n * WINDOW, WINDOW)
            mask = _attn_mask(n)
            kband = kp_ref[pl.ds(r0, 2 * WINDOW), :]
            vband = vp_ref[pl.ds(r0, 2 * WINDOW), :]
            qb = q_ref[pl.ds(r0, WINDOW), :].astype(BF16)
            outs = []
            for hq in range(AQ_HEADS):
                kv = hq // grp
                kb = kband[:, kv * A_HD:(kv + 1) * A_HD]
                vb = vband[:, kv * A_HD:(kv + 1) * A_HD]
                probs, _ = _attn_probs(qb[:, hq * A_HD:(hq + 1) * A_HD], kb, bias_ref[hq], sink_ref[0, hq], mask)
                outs.append(jnp.dot(probs.astype(BF16), vb, preferred_element_type=F32))
            y_ref[pl.ds(r0, WINDOW), :] = jnp.concatenate(outs, axis=1).astype(BF16)
            return carry

        lax.fori_loop(0, nb, blk, 0)

    return pl.pallas_call(
        body, name="attn_fwd", grid=(bsz,),
        in_specs=[_vspec((s, AQ), lambda b: (b, C_Q // AQ)), _vspec((s, AKV), lambda b: (b, C_K // AKV)),
                  _vspec((s, AKV), lambda b: (b, C_V // AKV)),
                  _vspec((AQ_HEADS, WINDOW, 2 * WINDOW), lambda b: (0, 0, 0)), pl.BlockSpec(memory_space=pltpu.SMEM)],
        out_specs=_vspec((s, AQ), lambda b: (b, 0)), out_shape=jax.ShapeDtypeStruct((t, AQ), BF16),
        scratch_shapes=[pltpu.VMEM((s + WINDOW, LANES), BF16), pltpu.VMEM((s + WINDOW, LANES), BF16)],
    )(proj, proj, proj, bias, sinks)


def _attn_bwd(proj, bias, sinks, dy, bsz, plug=None):
    t = proj.shape[0]
    s = t // bsz
    nb = s // WINDOW
    grp = AQ_HEADS // AKV_HEADS
    scale = A_HD ** -0.5

    def body(q_ref, k_ref, v_ref, bias_ref, sink_ref, dy_ref, dq_ref, dk_ref, dv_ref, dbias_ref, dsink_ref,
             kp_ref, vp_ref, dkp_ref, dvp_ref):
        b = pl.program_id(0)
        kp_ref[0:WINDOW, :] = jnp.zeros((WINDOW, LANES), BF16)
        vp_ref[0:WINDOW, :] = jnp.zeros((WINDOW, LANES), BF16)
        kp_ref[WINDOW:, :] = k_ref[...].astype(BF16)
        vp_ref[WINDOW:, :] = v_ref[...].astype(BF16)
        dkp_ref[...] = jnp.zeros_like(dkp_ref)
        dvp_ref[...] = jnp.zeros_like(dvp_ref)

        @pl.when(b == 0)
        def _():
            dbias_ref[...] = jnp.zeros_like(dbias_ref)
            dsink_ref[...] = jnp.zeros_like(dsink_ref)

        def blk(n, carry):
            r0 = pl.multiple_of(n * WINDOW, WINDOW)
            mask = _attn_mask(n)
            kband = kp_ref[pl.ds(r0, 2 * WINDOW), :]
            vband = vp_ref[pl.ds(r0, 2 * WINDOW), :]
            qb = q_ref[pl.ds(r0, WINDOW), :].astype(BF16)
            dyb = dy_ref[pl.ds(r0, WINDOW), :].astype(BF16)
            dqs = []
            dks = [None] * AKV_HEADS
            dvs = [None] * AKV_HEADS
            for hq in range(AQ_HEADS):
                kv = hq // grp
                kb = kband[:, kv * A_HD:(kv + 1) * A_HD]
                vb = vband[:, kv * A_HD:(kv + 1) * A_HD]
                qh = qb[:, hq * A_HD:(hq + 1) * A_HD]
                dyh = dyb[:, hq * A_HD:(hq + 1) * A_HD]
                probs, psink = _attn_probs(qh, kb, bias_ref[hq], sink_ref[0, hq], mask)
                dprobs = lax.dot_general(dyh, vb, (((1,), (1,)), ((), ())), preferred_element_type=F32)
                dvh = lax.dot_general(probs.astype(BF16), dyh, (((0,), (0,)), ((), ())), preferred_element_type=F32)
                rowdot = jnp.sum(probs * dprobs, axis=-1, keepdims=True)
                ds = probs * (dprobs - rowdot)
                dbias_ref[hq] += ds
                dsink_ref[hq] += jnp.sum(-psink * rowdot, axis=0, keepdims=True) + jnp.zeros((1, LANES), F32)
                dsb = ds.astype(BF16)
                dqs.append(jnp.dot(dsb, kb, preferred_element_type=F32) * scale)
                dkh = lax.dot_general(dsb, qh, (((0,), (0,)), ((), ())), preferred_element_type=F32) * scale
                dks[kv] = dkh if dks[kv] is None else dks[kv] + dkh
                dvs[kv] = dvh if dvs[kv] is None else dvs[kv] + dvh
            dq_ref[pl.ds(r0, WINDOW), :] = jnp.concatenate(dqs, axis=1).astype(BF16)
            dkp_ref[pl.ds(r0, 2 * WINDOW), :] += jnp.concatenate(dks, axis=1)
            dvp_ref[pl.ds(r0, 2 * WINDOW), :] += jnp.concatenate(dvs, axis=1)
            return carry

        lax.fori_loop(0, nb, blk, 0)
        dk_ref[...] = dkp_ref[WINDOW:, :].astype(BF16)
        dv_ref[...] = dvp_ref[WINDOW:, :].astype(BF16)

    kvs = jax.ShapeDtypeStruct((t, AKV), BF16)
    outs, extra = _plugged_call(
        body, plug, _grid_ends((bsz,)), (proj, proj, proj, bias, sinks, dy), name="attn_bwd", grid=(bsz,),
        in_specs=[_vspec((s, AQ), lambda b: (b, C_Q // AQ)), _vspec((s, AKV), lambda b: (b, C_K // AKV)),
                  _vspec((s, AKV), lambda b: (b, C_V // AKV)),
                  _vspec((AQ_HEADS, WINDOW, 2 * WINDOW), lambda b: (0, 0, 0)), pl.BlockSpec(memory_space=pltpu.SMEM),
                  _vspec((s, AQ), lambda b: (b, 0))],
        out_specs=[_vspec((s, AQ), lambda b: (b, 0)), _vspec((s, AKV), lambda b: (b, 0)), _vspec((s, AKV), lambda b: (b, 0)),
                   _vspec((AQ_HEADS, WINDOW, 2 * WINDOW), lambda b: (0, 0, 0)), _vspec((AQ_HEADS, 1, LANES), lambda b: (0, 0, 0))],
        out_shape=[jax.ShapeDtypeStruct((t, AQ), BF16), kvs, kvs,
                   jax.ShapeDtypeStruct((AQ_HEADS, WINDOW, 2 * WINDOW), F32), jax.ShapeDtypeStruct((AQ_HEADS, 1, LANES), F32)],
        scratch_shapes=[pltpu.VMEM((s + WINDOW, LANES), BF16), pltpu.VMEM((s + WINDOW, LANES), BF16),
                        pltpu.VMEM((s + WINDOW, LANES), F32), pltpu.VMEM((s + WINDOW, LANES), F32)])
    return (*outs, extra)


def _dn_act_f(c, is_qk):
    a = jax.nn.silu(c)
    outs = []
    for h in range(DN_HEADS):
        ah = a[:, h * DN_HD:(h + 1) * DN_HD]
        nh = ah * lax.rsqrt(jnp.sum(ah * ah, axis=-1, keepdims=True) + L2_EPS)
        outs.append(jnp.where(is_qk, nh, ah))
    return jnp.concatenate(outs, axis=1)


def _dn_prep_fwd(proj, conv_w, bsz):
    t = proj.shape[0]
    s = t // bsz
    blk = _vspec((s, DN), lambda b, j: (b, j))
    wsp = _vspec((DN_CONV, DN), lambda b, j: (0, j))

    def body(x_ref, w_ref, o_ref):
        j = pl.program_id(1)
        o_ref[...] = _dn_act_f(_conv_fwd(x_ref[...], w_ref[...], DN_CONV), j < 2)

    return pl.pallas_call(body, name="dn_prep_fwd", grid=(bsz, 3), in_specs=[blk, wsp], out_specs=blk,
                          out_shape=jax.ShapeDtypeStruct((t, 3 * DN), F32))(proj, conv_w)


def _dn_prep_bwd(proj, conv_w, dqkvn, bsz):
    t = proj.shape[0]
    s = t // bsz
    blk = _vspec((s, DN), lambda j, b: (b, j))
    wsp = _vspec((DN_CONV, DN), lambda j, b: (0, j))

    def body(x_ref, w_ref, d_ref, dx_ref, dw_ref):
        j, b = pl.program_id(0), pl.program_id(1)
        x, w = x_ref[...], w_ref[...]
        c = _conv_fwd(x, w, DN_CONV)
        _, vjp = jax.vjp(lambda cc: _dn_act_f(cc, j < 2), c)
        (dc,) = vjp(d_ref[...])
        dx, dw = _conv_bwd(x, w, dc, DN_CONV)
        dx_ref[...] = dx.astype(BF16)

        @pl.when(b == 0)
        def _():
            dw_ref[...] = jnp.zeros_like(dw_ref)

        dw_ref[...] += dw

    return pl.pallas_call(
        body, name="dn_prep_bwd", grid=(3, bsz), in_specs=[blk, wsp, blk], out_specs=[blk, wsp],
        out_shape=[jax.ShapeDtypeStruct((t, 3 * DN), BF16), jax.ShapeDtypeStruct((DN_CONV, 3 * DN), F32)],
    )(proj, conv_w, dqkvn)


def _bg_f(x, alog, dt):
    lane = lax.broadcasted_iota(jnp.int32, x.shape, 1)
    beta = jax.nn.sigmoid(x)
    g = -jnp.exp(alog) * jax.nn.softplus(x + dt)
    return jnp.where(lane < DN_HEADS, beta, jnp.where(lane < 2 * DN_HEADS, g, 0.0))


def _bg_fwd(proj, alog, dt, bsz):
    t = proj.shape[0]
    s = t // bsz
    vec = _vspec((1, LANES), lambda b: (0, 0))

    def body(x_ref, a_ref, d_ref, o_ref):
        o_ref[...] = _bg_f(x_ref[...], a_ref[...], d_ref[...])

    return pl.pallas_call(body, name="bg_fwd", grid=(bsz,), in_specs=[_vspec((s, LANES), lambda b: (b, C_BD // LANES)), vec, vec],
                          out_specs=_vspec((s, LANES), lambda b: (b, 0)), out_shape=jax.ShapeDtypeStruct((t, LANES), F32))(proj, alog, dt)


def _bg_bwd(proj, alog, dt, dbg4, bsz):
    t = proj.shape[0]
    s = t // bsz
    vec = _vspec((1, LANES), lambda b: (0, 0))

    def body(x_ref, a_ref, d_ref, g4_ref, dx_ref, da_ref, dd_ref):
        b = pl.program_id(0)
        lane = lax.broadcasted_iota(jnp.int32, (s, LANES), 1)
        dbg = jnp.zeros((s, LANES), F32)
        for h in range(DN_HEADS):
            gh = g4_ref[:, h * DN_HD:(h + 1) * DN_HD]
            dbg = jnp.where(lane == h, gh[:, 0:1], dbg)
            dbg = jnp.where(lane == DN_HEADS + h, gh[:, 1:2], dbg)
        _, vjp = jax.vjp(_bg_f, x_ref[...], a_ref[...], d_ref[...])
        dx, da, dd = vjp(dbg)
        dx_ref[...] = dx.astype(BF16)

        @pl.when(b == 0)
        def _():
            da_ref[...] = jnp.zeros_like(da_ref)
            dd_ref[...] = jnp.zeros_like(dd_ref)

        da_ref[...] += da
        dd_ref[...] += dd

    return pl.pallas_call(
        body, name="bg_bwd", grid=(bsz,),
        in_specs=[_vspec((s, LANES), lambda b: (b, C_BD // LANES)), vec, vec, _vspec((s, DN), lambda b: (b, 0))],
        out_specs=[_vspec((s, LANES), lambda b: (b, 0)), vec, vec],
        out_shape=[jax.ShapeDtypeStruct((t, LANES), BF16), jax.ShapeDtypeStruct((1, LANES), F32), jax.ShapeDtypeStruct((1, LANES), F32)],
    )(proj, alog, dt, dbg4)


def _dn_out_f(o, z, w):
    outs = []
    for h in range(DN_HEADS):
        sl = slice(h * DN_HD, (h + 1) * DN_HD)
        outs.append(_rms(o[:, sl], w) * jax.nn.silu(z[:, sl]))
    return jnp.concatenate(outs, axis=1)


def _dn_out_fwd(o, proj, w, ts=512):
    t = o.shape[0]
    blk = _vspec((ts, DN), lambda i: (i, 0))
    zsp = _vspec((ts, DN), lambda i: (i, C_DZ // DN))
    vec = _vspec((1, DN_HD), lambda i: (0, 0))

    def body(o_ref, z_ref, w_ref, y_ref):
        y_ref[...] = _dn_out_f(o_ref[...], z_ref[...], w_ref[...]).astype(BF16)

    return pl.pallas_call(body, name="dn_out_fwd", grid=(t // ts,), in_specs=[blk, zsp, vec], out_specs=blk,
                          out_shape=jax.ShapeDtypeStruct((t, DN), BF16))(o, proj, w)


def _dn_out_bwd(o, proj, w, dy, ts=512):
    t = o.shape[0]
    blk = _vspec((ts, DN), lambda i: (i, 0))
    zsp = _vspec((ts, DN), lambda i: (i, C_DZ // DN))
    vec = _vspec((1, DN_HD), lambda i: (0, 0))

    def body(o_ref, z_ref, w_ref, dy_ref, do_ref, dz_ref, dw_ref):
        i = pl.program_id(0)
        _, vjp = jax.vjp(_dn_out_f, o_ref[...], z_ref[...], w_ref[...])
        do, dz, dw = vjp(dy_ref[...])
        do_ref[...] = do
        dz_ref[...] = dz.astype(BF16)

        @pl.when(i == 0)
        def _():
            dw_ref[...] = jnp.zeros_like(dw_ref)

        dw_ref[...] += dw

    return pl.pallas_call(
        body, name="dn_out_bwd", grid=(t // ts,), in_specs=[blk, zsp, vec, blk], out_specs=[blk, blk, vec],
        out_shape=[jax.ShapeDtypeStruct((t, DN), F32), jax.ShapeDtypeStruct((t, DN), BF16), jax.ShapeDtypeStruct((1, DN_HD), F32)],
    )(o, proj, w, dy)


_C = DN_CHUNK


def _dot(a, b, dims):
    return lax.dot_general(a.astype(BF16), b.astype(BF16), dims, preferred_element_type=F32)


_NN = (((1,), (0,)), ((), ()))
_NT = (((1,), (1,)), ((), ()))
_TN = (((0,), (0,)), ((), ()))


_SUB = 8


def _tri_inverse(lt):
    ri = lax.broadcasted_iota(jnp.int32, (_SUB, _C), 0)
    ci = lax.broadcasted_iota(jnp.int32, (_SUB, _C), 1)
    blocks = [jnp.where(ci == ri + _SUB * b, 1.0, 0.0).astype(F32) for b in range(_C // _SUB)]
    for i in range(1, _C):
        acc = None
        for b in range((i + _SUB - 1) // _SUB):
            term = lt[_SUB * b:_SUB * (b + 1), i:i + 1] * blocks[b]
            acc = term if acc is None else acc + term
        row = jnp.sum(acc, axis=0, keepdims=True)
        bi, r = divmod(i, _SUB)
        blocks[bi] = jnp.where(ri == r, blocks[bi] - row, blocks[bi])
    return jnp.concatenate(blocks, axis=0)


_SEG = 512
_HEADS = tuple(range(DN_HEADS))


def _hsl(hh):
    return slice(hh * DN_HD, (hh + 1) * DN_HD)


def _chunk_specs(nseg, reverse):
    seg = (lambda i: nseg - 1 - i) if reverse else (lambda i: i)
    ncs = _SEG // _C
    col = lambda off: _vspec((_SEG, DN), lambda b, i: (b * nseg + seg(i), off))
    return (col, _vspec((_SEG, LANES), lambda b, i: (b * nseg + seg(i), 0)),
            _vspec((1, DN_HEADS, ncs, _C), lambda b, i: (b, 0, seg(i), 0)),
            _vspec((1, DN_HEADS, ncs, DN_HD, DN_HD), lambda b, i: (b, 0, seg(i), 0, 0)))


def _chunk_pre(q_ref, k_ref, v_ref, bg_ref, gr_ref, c, hh):
    r0 = pl.multiple_of(c * _C, _C)
    ri = lax.broadcasted_iota(jnp.int32, (_C, _C), 0)
    ci = lax.broadcasted_iota(jnp.int32, (_C, _C), 1)
    q = q_ref[pl.ds(r0, _C), _hsl(hh)] * (DN_HD ** -0.5)
    k = k_ref[pl.ds(r0, _C), _hsl(hh)]
    v = v_ref[pl.ds(r0, _C), _hsl(hh)]
    bgc = bg_ref[pl.ds(r0, _C), :]
    beta = bgc[:, hh:hh + 1]
    g_col = bgc[:, DN_HEADS + hh:DN_HEADS + hh + 1]
    g_row = gr_ref[0, hh, pl.ds(c, 1), :]
    gc_col = jnp.sum(jnp.where(ri >= ci, g_row, 0.0), axis=1, keepdims=True)
    gc_row = jnp.sum(jnp.where(ri <= ci, g_col, 0.0), axis=0, keepdims=True)
    gc_last = jnp.sum(g_col, axis=0, keepdims=True)
    diff = gc_col - gc_row
    decay = jnp.where(ri >= ci, jnp.exp(jnp.where(ri >= ci, diff, 0.0)), 0.0)
    diff_t = gc_row - gc_col
    decay_t = jnp.where(ri <= ci, jnp.exp(jnp.where(ri <= ci, diff_t, 0.0)), 0.0)
    eg = jnp.exp(gc_col)
    et = jnp.exp(gc_last - gc_col)
    gl = jnp.exp(gc_last)
    kb = k * beta
    vb = v * beta
    return dict(r0=r0, q=q, k=k, v=v, beta=beta, decay=decay, decay_t=decay_t, eg=eg, et=et, gl=gl, kb=kb, vb=vb, ri=ri, ci=ci)


def _chunk_solve(ms, with_transposes):
    for m in ms:
        m["kk_t"] = _dot(m["k"], m["kb"], _NT)
        m["qk"] = _dot(m["q"], m["k"], _NT)
        if with_transposes:
            m["kk"] = _dot(m["kb"], m["k"], _NT)
            m["qk_t"] = _dot(m["k"], m["q"], _NT)
    for m in ms:
        m["tm"] = _tri_inverse(jnp.where(m["ri"] < m["ci"], m["kk_t"] * m["decay_t"], 0.0))
    for m in ms:
        rhs = jnp.concatenate([m["vb"], m["kb"] * m["eg"]], axis=1)
        m["sol"] = jnp.dot(m["tm"], rhs, precision=HI, preferred_element_type=F32)
        m["intra"] = jnp.where(m["ri"] >= m["ci"], m["qk"] * m["decay"], 0.0)


def _dn_chunk_fwd(qkvn, bg, g_rows, bsz, plug=None):
    t = qkvn.shape[0]
    s = t // bsz
    nc, nseg = s // _C, s // _SEG

    def body(q_ref, k_ref, v_ref, bg_ref, gr_ref, o_ref, st_ref, s_ref):
        @pl.when(pl.program_id(1) == 0)
        def _():
            s_ref[...] = jnp.zeros_like(s_ref)

        def chunk(c, carry):
            ms = [_chunk_pre(q_ref, k_ref, v_ref, bg_ref, gr_ref, c, hh) for hh in _HEADS]
            _chunk_solve(ms, False)
            sts = [s_ref[hh] for hh in _HEADS]
            for hh in _HEADS:
                st_ref[0, hh, c] = sts[hh]
            ws = [_dot(m["sol"][:, DN_HD:], st, _NN) for m, st in zip(ms, sts)]
            qs = [_dot(m["q"] * m["eg"], st, _NN) for m, st in zip(ms, sts)]
            v_new = [m["sol"][:, :DN_HD] - a for m, a in zip(ms, ws)]
            iv = [_dot(m["intra"], vn, _NN) for m, vn in zip(ms, v_new)]
            upd = [_dot(m["k"] * m["et"], vn, _TN) for m, vn in zip(ms, v_new)]
            for hh in _HEADS:
                s_ref[hh] = sts[hh] * ms[hh]["gl"] + upd[hh]
                o_ref[pl.ds(ms[hh]["r0"], _C), _hsl(hh)] = qs[hh] + iv[hh]
            return carry

        lax.fori_loop(0, _SEG // _C, chunk, 0)

    col, bgs, grs, sts_spec = _chunk_specs(nseg, False)
    (o, states), extra = _plugged_call(
        body, plug, _grid_ends((bsz, nseg)), (qkvn, qkvn, qkvn, bg, g_rows), name="dn_chunk_fwd", grid=(bsz, nseg),
        in_specs=[col(0), col(1), col(2), bgs, grs], out_specs=[col(0), sts_spec],
        out_shape=[jax.ShapeDtypeStruct((t, DN), F32), jax.ShapeDtypeStruct((bsz, DN_HEADS, nc, DN_HD, DN_HD), F32)],
        scratch_shapes=[pltpu.VMEM((DN_HEADS, DN_HD, DN_HD), F32)])
    return o, states, extra


def _dn_chunk_bwd(qkvn, bg, g_rows, states, do, bsz, plug=None):
    t = qkvn.shape[0]
    s = t // bsz
    nc, nseg = s // _C, s // _SEG

    def body(q_ref, k_ref, v_ref, bg_ref, gr_ref, st_ref, do_ref, dq_ref, dk_ref, dv_ref, dbg_ref, ds_ref):
        @pl.when(pl.program_id(1) == 0)
        def _():
            ds_ref[...] = jnp.zeros_like(ds_ref)

        def chunk(cc, carry):
            c = _SEG // _C - 1 - cc
            ms = [_chunk_pre(q_ref, k_ref, v_ref, bg_ref, gr_ref, c, hh) for hh in _HEADS]
            _chunk_solve(ms, True)
            ri, ci = ms[0]["ri"], ms[0]["ci"]
            for hh, m in enumerate(ms):
                m["st"] = st_ref[0, hh, c]
                m["ds_out"] = ds_ref[hh]
                m["do"] = do_ref[pl.ds(m["r0"], _C), _hsl(hh)]
                m["w"] = m["sol"][:, DN_HD:]
            for m in ms:
                m["v_new"] = m["sol"][:, :DN_HD] - _dot(m["w"], m["st"], _NN)
            for m in ms:
                m["q_dec"], m["k_tail"] = m["q"] * m["eg"], m["k"] * m["et"]
                m["dk_tail"] = _dot(m["v_new"], m["ds_out"], _NT)
                m["dv_new"] = _dot(m["k_tail"], m["ds_out"], _NN) + _dot(m["intra"], m["do"], _TN)
                m["dq_dec"] = _dot(m["do"], m["st"], _NT)
                m["ds_in"] = m["ds_out"] * m["gl"] + _dot(m["q_dec"], m["do"], _TN)
                m["dintra"] = jnp.where(ri >= ci, _dot(m["do"], m["v_new"], _NT), 0.0)
                m["dintra_t"] = jnp.where(ri <= ci, _dot(m["v_new"], m["do"], _NT), 0.0)
            for m in ms:
                m["dw"] = -_dot(m["dv_new"], m["st"], _NT)
                m["ds_in"] = m["ds_in"] - _dot(m["w"], m["dv_new"], _TN)
            for m in ms:
                dsol = jnp.concatenate([m["dv_new"], m["dw"]], axis=1)
                m["drhs"] = lax.dot_general(m["tm"], dsol, _TN, precision=HI, preferred_element_type=F32)
            for m in ms:
                m["dl"] = jnp.where(ri > ci, -_dot(m["drhs"], m["sol"], _NT), 0.0)
                m["dl_t"] = jnp.where(ri < ci, -_dot(m["sol"], m["drhs"], _NT), 0.0)
            for m in ms:
                m["dkb2"] = _dot(m["dl"] * m["decay"], m["k"], _NN)
                m["dk"] = _dot(m["dl_t"] * m["decay_t"], m["kb"], _NN) + _dot(m["dintra_t"] * m["decay_t"], m["q"], _NN)
                m["dq"] = _dot(m["dintra"] * m["decay"], m["k"], _NN)
            for m in ms:
                _chunk_bwd_finish(m)
            for m in ms:
                m["dg_b"] = jnp.dot(jnp.where(ri <= ci, 1.0, 0.0).astype(F32), m["dgc"] + jnp.zeros((_C, LANES), F32),
                                    precision=HI, preferred_element_type=F32)
            lane = lax.broadcasted_iota(jnp.int32, (_C, LANES), 1)
            for hh, m in enumerate(ms):
                rows = pl.ds(m["r0"], _C)
                dq_ref[rows, _hsl(hh)] = m["dq"] * (DN_HD ** -0.5)
                dk_ref[rows, _hsl(hh)] = m["dk"]
                dv_ref[rows, _hsl(hh)] = m["dv"]
                dbg_ref[rows, _hsl(hh)] = jnp.where(lane == 0, m["dbeta"], jnp.where(lane == 1, m["dg_b"], 0.0))
                ds_ref[hh] = m["ds_in"]
            return carry

        lax.fori_loop(0, _SEG // _C, chunk, 0)

    col, bgs, grs, sts_spec = _chunk_specs(nseg, True)
    o = jax.ShapeDtypeStruct((t, DN), F32)
    outs, extra = _plugged_call(
        body, plug, _grid_ends((bsz, nseg)), (qkvn, qkvn, qkvn, bg, g_rows, states, do), name="dn_chunk_bwd", grid=(bsz, nseg),
        in_specs=[col(0), col(1), col(2), bgs, grs, sts_spec, col(0)], out_specs=[col(0)] * 4, out_shape=[o] * 4,
        scratch_shapes=[pltpu.VMEM((DN_HEADS, DN_HD, DN_HD), F32)])
    return (*outs, extra)


def _chunk_bwd_finish(m):
    q, k, v, beta, decay, decay_t = m["q"], m["k"], m["v"], m["beta"], m["decay"], m["decay_t"]
    eg, et, gl, kb, dl, dl_t, dintra, dintra_t = m["eg"], m["et"], m["gl"], m["kb"], m["dl"], m["dl_t"], m["dintra"], m["dintra_t"]
    dq_dec, dk_tail, dq, dk = m["dq_dec"], m["dk_tail"], m["dq"], m["dk"]
    dgl = jnp.sum(jnp.sum(m["ds_out"] * m["st"], axis=1, keepdims=True), axis=0, keepdims=True)
    dvb, dkbeg = m["drhs"][:, :DN_HD], m["drhs"][:, DN_HD:]
    dkb = dkbeg * eg + m["dkb2"]
    deg = jnp.sum(dkbeg * kb, axis=1, keepdims=True)
    em = (dl * m["kk"] + dintra * m["qk"]) * decay
    em_t = (dl_t * m["kk_t"] + dintra_t * m["qk_t"]) * decay_t
    dgc = jnp.sum(em, axis=1, keepdims=True) - jnp.sum(em_t, axis=1, keepdims=True)
    dq = dq + dq_dec * eg
    deg = deg + jnp.sum(dq_dec * q, axis=1, keepdims=True)
    dk = dk + dk_tail * et
    det = jnp.sum(dk_tail * k, axis=1, keepdims=True)
    dgc = dgc + deg * eg - det * et
    dgc_last = jnp.sum(det * et, axis=0, keepdims=True) + dgl * gl
    rcol = lax.broadcasted_iota(jnp.int32, (_C, 1), 0)
    m["dgc"] = dgc + jnp.where(rcol == _C - 1, dgc_last, 0.0)
    m["dq"] = dq
    m["dk"] = dk + dkb * beta
    m["dbeta"] = jnp.sum(dkb * k, axis=1, keepdims=True) + jnp.sum(dvb * v, axis=1, keepdims=True)
    m["dv"] = dvb * beta


def _mod_fwd(c_all, ada_w_loc, ada_b_loc):
    n, cols = c_all.shape[0], ada_w_loc.shape[1]

    def body(c_ref, w_ref, b_ref, o_ref):
        o_ref[...] = _dot(jax.nn.silu(c_ref[...]), w_ref[...], _NN) + b_ref[...]

    return pl.pallas_call(body, name="mod_fwd", out_shape=jax.ShapeDtypeStruct((n, cols), F32))(c_all, ada_w_loc, ada_b_loc)


def _ada_grad(c_all, dmod_loc, dmod_all):
    d, cols = c_all.shape[1], dmod_loc.shape[1]

    def body(c_ref, dl_ref, da_ref, gw_ref, gb_ref):
        gw_ref[...] = _dot(jax.nn.silu(c_ref[...]), dl_ref[...], _TN)
        gb_ref[...] = jnp.sum(da_ref[...], axis=0, keepdims=True)

    return pl.pallas_call(body, name="ada_grad", out_shape=[jax.ShapeDtypeStruct((d, cols), F32),
                                                           jax.ShapeDtypeStruct((1, dmod_all.shape[1]), F32)])(c_all, dmod_loc, dmod_all)


def _row_tile(r):
    for tr in (256, 128, 64, 32, 16, 8):
        if r % tr == 0:
            return tr
    return r


def _adamw(w, m, v, grads, name):
    r, c = w.shape
    tr = _row_tile(r)
    blk = _vspec((tr, c), lambda i: (i, 0))
    n = len(grads)

    def body(*refs):
        w_ref, m_ref, v_ref = refs[:3]
        g_ref, d_ref, mo_ref, vo_ref = refs[3 + n:]
        g = refs[3][...]
        for p in refs[4:3 + n]:
            g = g + p[...]
        m_new = ADAM_B1 * m_ref[...] + (1.0 - ADAM_B1) * g
        v_new = ADAM_B2 * v_ref[...] + (1.0 - ADAM_B2) * jnp.square(g)
        m_hat = m_new / (1.0 - ADAM_B1 ** ADAM_STEP)
        v_hat = v_new / (1.0 - ADAM_B2 ** ADAM_STEP)
        g_ref[...] = g
        d_ref[...] = -ADAM_LR * (m_hat / (jnp.sqrt(v_hat) + ADAM_EPS) + ADAM_WD * w_ref[...])
        mo_ref[...] = m_new
        vo_ref[...] = v_new

    o = jax.ShapeDtypeStruct((r, c), F32)
    return pl.pallas_call(body, name=name, grid=(r // tr,), in_specs=[blk] * (3 + n), out_specs=[blk] * 4,
                          out_shape=[o] * 4)(w, m, v, *grads)


def _sum_lead(x, name):
    p, r, c = x.shape
    tr = _row_tile(r)

    def body(x_ref, o_ref):
        acc = x_ref[0].astype(F32)
        for i in range(1, p):
            acc = acc + x_ref[i].astype(F32)
        o_ref[...] = acc

    return pl.pallas_call(body, name=name, grid=(r // tr,), in_specs=[_vspec((p, tr, c), lambda i: (0, i, 0))],
                          out_specs=_vspec((tr, c), lambda i: (i, 0)), out_shape=jax.ShapeDtypeStruct((r, c), F32))(x)


def _allgather8(x_shard, name):
    m_per, n = x_shard.shape

    def body(x_ref, out_ref, send_sems, recv_sems, local_sem):
        x, y, c = lax.axis_index("x"), lax.axis_index("y"), lax.axis_index("c")
        me, sibling = (x, y, c), (x, y, 1 - c)
        chips = [(1 - x, y), (x, 1 - y), (1 - x, 1 - y)]

        def rows(px, py, pc):
            return out_ref.at[pl.ds((4 * px + 2 * py + pc) * m_per, m_per), :]

        def copy(k, block, to, src=None):
            return pltpu.make_async_remote_copy(
                src_ref=rows(*block) if src is None else src, dst_ref=rows(*block), send_sem=send_sems.at[k],
                recv_sem=recv_sems.at[k], device_id=to, device_id_type=MESH)

        mine = pltpu.make_async_copy(x_ref, rows(*me), local_sem)
        mine.start()
        first = [copy(0, me, sibling, src=x_ref)]
        first += [copy(1 + j, me, (*chip, c), src=x_ref) for j, chip in enumerate(chips)]
        for cp in first:
            cp.start()
        passed = [copy(4 + j, (*chip, c), sibling) for j, chip in enumerate(chips)]
        for j, chip in enumerate(chips):
            copy(1 + j, (*chip, c), me).wait_recv()
            passed[j].start()
        copy(0, sibling, me).wait_recv()
        for j, chip in enumerate(chips):
            copy(4 + j, (*chip, 1 - c), me).wait_recv()
        for cp in first + passed:
            cp.wait_send()
        mine.wait()

    return pl.pallas_call(
        body, name=name, out_shape=jax.ShapeDtypeStruct((8 * m_per, n), x_shard.dtype),
        in_specs=[pl.BlockSpec(memory_space=pltpu.VMEM)], out_specs=pl.BlockSpec(memory_space=pltpu.VMEM),
        scratch_shapes=[pltpu.SemaphoreType.DMA((7,)), pltpu.SemaphoreType.DMA((7,)), pltpu.SemaphoreType.DMA],
    )(x_shard)


_HBM = pl.BlockSpec(memory_space=pltpu.HBM)


def _mesh_place():
    x, y, c = lax.axis_index("x"), lax.axis_index("y"), lax.axis_index("c")
    return x, y, c, 2 * x + y, [(1 - x, y), (x, 1 - y), (1 - x, 1 - y)]


def _gather_plug(shards):
    n = len(shards)

    def half(ref, c, lead=None):
        r = ref.shape[-2] // 2
        rows = pl.ds(pl.multiple_of(c * r, 16), r)
        return ref.at[rows, :] if lead is None else ref.at[lead, rows, :]

    def copies(ins, outs, send, recv):
        x, y, c, me, chips = _mesh_place()
        ici, fwd, fwd_in = [], [], []
        for i in range(n):
            for j, (px, py) in enumerate(chips):
                q = 2 * px + py
                ici.append((pltpu.make_async_remote_copy(
                    src_ref=half(ins[i], c), dst_ref=half(outs[i], c, me), send_sem=send.at[6 * i + j], recv_sem=recv.at[6 * i + j],
                    device_id=(px, py, c), device_id_type=MESH),
                    pltpu.make_async_remote_copy(
                    src_ref=half(ins[i], c), dst_ref=half(outs[i], c, q), send_sem=send.at[6 * i + j], recv_sem=recv.at[6 * i + j],
                    device_id=(px, py, c), device_id_type=MESH)))
                fwd.append(pltpu.make_async_remote_copy(
                    src_ref=half(outs[i], c, q), dst_ref=half(outs[i], c, q), send_sem=send.at[6 * i + 3 + j],
                    recv_sem=recv.at[6 * i + 3 + j], device_id=(x, y, 1 - c), device_id_type=MESH))
                fwd_in.append(pltpu.make_async_remote_copy(
                    src_ref=half(outs[i], 1 - c, q), dst_ref=half(outs[i], 1 - c, q), send_sem=send.at[6 * i + 3 + j],
                    recv_sem=recv.at[6 * i + 3 + j], device_id=(x, y, 1 - c), device_id_type=MESH))
        return ici, fwd, fwd_in, me

    def start(ins, outs, send, recv, loc):
        ici, _, _, me = copies(ins, outs, send, recv)
        for i in range(n):
            pltpu.make_async_copy(ins[i], outs[i].at[me], loc.at[i]).start()
        for out_cp, _ in ici:
            out_cp.start()

    def finish(ins, outs, send, recv, loc):
        ici, fwd, fwd_in, me = copies(ins, outs, send, recv)
        for (_, in_cp), f in zip(ici, fwd):
            in_cp.wait_recv()
            f.start()
        for f in fwd_in:
            f.wait_recv()
        for (out_cp, _), f in zip(ici, fwd):
            out_cp.wait_send()
            f.wait_send()
        for i in range(n):
            pltpu.make_async_copy(ins[i], outs[i].at[me], loc.at[i]).wait()

    return dict(ins=list(shards), out_shape=[jax.ShapeDtypeStruct((N_CHIPS,) + a.shape, a.dtype) for a in shards],
                scratch=[pltpu.SemaphoreType.DMA((6 * n,)), pltpu.SemaphoreType.DMA((6 * n,)), pltpu.SemaphoreType.DMA((n,))],
                start=start, finish=finish)


def _exchange_plug(pieces):
    n = len(pieces)

    def copies(ins, outs, send, recv):
        x, y, c, me, chips = _mesh_place()
        out_cps, in_cps = [], []
        for i in range(n):
            for j, (px, py) in enumerate(chips):
                q = 2 * px + py
                out_cps.append(pltpu.make_async_remote_copy(src_ref=ins[i].at[q], dst_ref=outs[i].at[me], send_sem=send.at[3 * i + j],
                                                            recv_sem=recv.at[3 * i + j], device_id=(px, py, c), device_id_type=MESH))
                in_cps.append(pltpu.make_async_remote_copy(src_ref=ins[i].at[me], dst_ref=outs[i].at[q], send_sem=send.at[3 * i + j],
                                                           recv_sem=recv.at[3 * i + j], device_id=(px, py, c), device_id_type=MESH))
        return out_cps, in_cps, me

    def start(ins, outs, send, recv, loc):
        out_cps, _, me = copies(ins, outs, send, recv)
        for i in range(n):
            pltpu.make_async_copy(ins[i].at[me], outs[i].at[me], loc.at[i]).start()
        for cp in out_cps:
            cp.start()

    def finish(ins, outs, send, recv, loc):
        out_cps, in_cps, me = copies(ins, outs, send, recv)
        for cp in in_cps:
            cp.wait_recv()
        for cp in out_cps:
            cp.wait_send()
        for i in range(n):
            pltpu.make_async_copy(ins[i].at[me], outs[i].at[me], loc.at[i]).wait()

    return dict(ins=list(pieces), out_shape=[jax.ShapeDtypeStruct(a.shape, a.dtype) for a in pieces],
                scratch=[pltpu.SemaphoreType.DMA((3 * n,)), pltpu.SemaphoreType.DMA((3 * n,)), pltpu.SemaphoreType.DMA((n,))],
                start=start, finish=finish)


def _comm_call(plug, name):
    n_in, n_out = len(plug["ins"]), len(plug["out_shape"])

    def body(*refs):
        ins, outs, sems = refs[:n_in], refs[n_in:n_in + n_out], refs[n_in + n_out:]
        plug["start"](ins, outs, *sems)
        plug["finish"](ins, outs, *sems)

    return pl.pallas_call(body, name=name, out_shape=plug["out_shape"], in_specs=[_HBM] * n_in, out_specs=[_HBM] * n_out,
                          scratch_shapes=plug["scratch"])(*plug["ins"])


def _plugged_call(body, plug, first_last, args, *, name, grid, in_specs, out_specs, out_shape, scratch_shapes=()):
    in_specs, out_specs, out_shape, scratch_shapes = list(in_specs), list(out_specs), list(out_shape), list(scratch_shapes)
    if plug is None:
        return pl.pallas_call(body, name=name, grid=grid, in_specs=in_specs, out_specs=out_specs, out_shape=out_shape,
                              scratch_shapes=scratch_shapes)(*args), []
    n_in, n_out, n_sc = len(in_specs), len(out_specs), len(scratch_shapes)
    p_in, p_out = len(plug["ins"]), len(plug["out_shape"])

    def full(*refs):
        ins, refs = refs[:n_in], refs[n_in:]
        pins, refs = refs[:p_in], refs[p_in:]
        outs, refs = refs[:n_out], refs[n_out:]
        pouts, refs = refs[:p_out], refs[p_out:]
        scr, psems = refs[:n_sc], refs[n_sc:]
        first, last = first_last()

        @pl.when(first)
        def _():
            plug["start"](pins, pouts, *psems)

        body(*ins, *outs, *scr)

        @pl.when(last)
        def _():
            plug["finish"](pins, pouts, *psems)

    res = pl.pallas_call(full, name=name, grid=grid, in_specs=in_specs + [_HBM] * p_in, out_specs=out_specs + [_HBM] * p_out,
                         out_shape=out_shape + plug["out_shape"], scratch_shapes=scratch_shapes + plug["scratch"])(*args, *plug["ins"])
    return res[:n_out], res[n_out:]


def _grid_ends(grid):
    def ends():
        first = last = None
        for ax, n in enumerate(grid):
            i = pl.program_id(ax)
            first = (i == 0) if first is None else first & (i == 0)
            last = (i == n - 1) if last is None else last & (i == n - 1)
        return first, last
    return ends


def _sibling_exchange(arrs, name):
    n = len(arrs)

    def body(*refs):
        ins, outs = refs[:n], refs[n:2 * n]
        send_sems, recv_sems = refs[2 * n:]
        sibling = (lax.axis_index("x"), lax.axis_index("y"), 1 - lax.axis_index("c"))
        cps = [pltpu.make_async_remote_copy(src_ref=ins[i], dst_ref=outs[i], send_sem=send_sems.at[i], recv_sem=recv_sems.at[i],
                                            device_id=sibling, device_id_type=MESH) for i in range(n)]
        for cp in cps:
            cp.start()
        for cp in cps:
            cp.wait()

    return pl.pallas_call(
        body, name=name, out_shape=[jax.ShapeDtypeStruct(a.shape, a.dtype) for a in arrs], in_specs=[_HBM] * n, out_specs=[_HBM] * n,
        scratch_shapes=[pltpu.SemaphoreType.DMA((n,)), pltpu.SemaphoreType.DMA((n,))],
    )(*arrs)


def _to_padded(w):
    z = jnp.zeros((w.shape[0], IN_PAD - IN_DIM), w.dtype)
    return jnp.concatenate([w[:, 768:2304], w[:, 0:512], w[:, 2304:2816], w[:, 2824:3848], w[:, 3848:4872],
                            w[:, 512:640], w[:, 640:768], w[:, 2816:2824], z], axis=1)


def _from_padded(g):
    return jnp.concatenate([g[:, C_Q:C_Q + AQ], g[:, C_K:C_K + AKV], g[:, C_V:C_V + AKV], g[:, C_DQKV:C_DQKV + 3 * DN],
                            g[:, C_DZ:C_DZ + DN], g[:, C_BD:C_BD + 2 * DN_HEADS], g[:, C_GA:C_GA + D_MODEL],
                            g[:, C_GD:C_GD + D_MODEL]], axis=1)


def _lane_vec(a):
    return jnp.zeros((1, LANES), F32).at[0, DN_HEADS:2 * DN_HEADS].set(a)


_ROW_SHARDED = ("w_out", "ffn_w_down")
_FFN = ("ffn_w_up", "ffn_w_down")
_MIXER = ("w_in", "w_attn_branch", "w_dn_branch", "w_out")


def _pieces(k, a):
    if k in _ROW_SHARDED:
        return a.reshape(N_CHIPS, a.shape[0] // N_CHIPS, a.shape[1]).astype(BF16)
    return jnp.transpose(a.reshape(a.shape[0], N_CHIPS, a.shape[1] // N_CHIPS), (1, 0, 2)).astype(BF16)


def _assemble(k, a):
    if k in _ROW_SHARDED:
        return a.reshape(-1, a.shape[2])
    return jnp.transpose(a, (1, 0, 2)).reshape(a.shape[1], -1)


def _device_step(x2, tgt2, mod, p, bsz, ffn_shards=None):
    d = D_MODEL
    on_mesh = ffn_shards is not None
    p = dict(p)
    sh1, sc1, g1, sh2, sc2, g2 = [mod[:, i * d:(i + 1) * d].reshape(bsz, 1, d) for i in range(N_MOD)]
    alog_v, dt_v = _lane_vec(p["dn_a_log"]), _lane_vec(p["dn_dt_bias"])
    sinks = p["attn_sinks"].reshape(1, AQ_HEADS)
    u1 = _pre_fwd(x2, p["norm_mix_pre"], sc1, sh1, "pre1_fwd")
    proj = _mm(u1, p["w_in"], "nn", F32, "mm_proj")
    bias = _bias_build(p["rel_bias"])
    y_attn = _attn_fwd(proj, bias, sinks, bsz)
    qkvn = _dn_prep_fwd(proj, p["dn_conv_w"], bsz)
    bg = _bg_fwd(proj, alog_v, dt_v, bsz)
    nc = x2.shape[0] // bsz // DN_CHUNK
    g_rows = jnp.transpose(bg[:, DN_HEADS:2 * DN_HEADS].reshape(bsz, nc, DN_CHUNK, DN_HEADS), (0, 3, 1, 2))
    o, states, got = _dn_chunk_fwd(qkvn, bg, g_rows, bsz, _gather_plug(ffn_shards) if on_mesh else None)
    for k, a in zip(_FFN, got):
        p[k] = _assemble(k, a)
    y_dn = _dn_out_fwd(o, proj, p["dn_norm_w"])
    ya = _mm(y_attn, p["w_attn_branch"], "nn", F32, "mm_ya")
    yd = _mm(y_dn, p["w_dn_branch"], "nn", F32, "mm_yd")
    merged = _merge_fwd(proj, ya, yd)
    y1 = _mm(merged, p["w_out"], "nn", F32, "mm_y1")
    h1 = _post_fwd(x2, y1, p["norm_mix_post"], g1, "post1_fwd")
    u2 = _pre_fwd(h1, p["norm_ffn_pre"], sc2, sh2, "pre2_fwd")
    up = _mm(u2, p["ffn_w_up"], "nn", F32, "mm_up")
    act = _ffn_act_fwd(up, p["ffn_conv_w"], bsz)
    y2 = _mm(act, p["ffn_w_down"], "nn", F32, "mm_y2")
    dh2, sq = _post_loss(h1, y2, p["norm_ffn_post"], g2, tgt2, "post2_loss")
    g = {}
    dy2, g["norm_ffn_post"], dg2 = _post_bwd(dh2, y2, p["norm_ffn_post"], g2, "post2_bwd")
    dact = _mm(dy2, p["ffn_w_down"], "nt", F32, "mm_dact")
    g["ffn_w_down"] = _mm(act, dy2, "tn", F32, "mm_dwdown")
    dupg, dupv, dcwg, dcwv = _ffn_act_bwd(up, p["ffn_conv_w"], dact, bsz)
    g["ffn_conv_w"] = jnp.concatenate([dcwg, dcwv], axis=1)
    dup = jnp.concatenate([dupg, dupv], axis=1)
    g["ffn_w_up"] = _mm(u2, dup, "tn", F32, "mm_dwup")
    du2 = _mm(dup, p["ffn_w_up"], "nt", F32, "mm_du2")
    dh1, g["norm_ffn_pre"], dsc2, dsh2 = _pre_bwd(h1, p["norm_ffn_pre"], sc2, sh2, du2, dh2, "pre2_bwd")
    dy1, g["norm_mix_post"], dg1 = _post_bwd(dh1, y1, p["norm_mix_post"], g1, "post1_bwd")
    dmerged = _mm(dy1, p["w_out"], "nt", F32, "mm_dmerged")
    g["w_out"] = _mm(merged, dy1, "tn", F32, "mm_dwout")
    dga, dgd, dya, dyd = _merge_bwd(proj, ya, yd, dmerged)
    dy_attn = _mm(dya, p["w_attn_branch"], "nt", BF16, "mm_dyattn")
    g["w_attn_branch"] = _mm(y_attn, dya, "tn", F32, "mm_dwab")
    dy_dn = _mm(dyd, p["w_dn_branch"], "nt", F32, "mm_dydn")
    g["w_dn_branch"] = _mm(y_dn, dyd, "tn", F32, "mm_dwdb")
    do, dz, g["dn_norm_w"] = _dn_out_bwd(o, proj, p["dn_norm_w"], dy_dn)
    def plug_for(names):
        return _exchange_plug([_pieces(k, g[k]) for k in names]) if on_mesh else None

    early = ("w_out", "w_attn_branch", "w_dn_branch")
    dqn, dkn, dvv, dbg4, got_ffn = _dn_chunk_bwd(qkvn, bg, g_rows, states, do, bsz, plug_for(_FFN))
    d_dqkv, g["dn_conv_w"] = _dn_prep_bwd(proj, p["dn_conv_w"], jnp.concatenate([dqn, dkn, dvv], axis=1), bsz)
    dbd, g["dn_a_log"], g["dn_dt_bias"] = _bg_bwd(proj, alog_v, dt_v, dbg4, bsz)
    dq, dk, dv, dbias, g["attn_sinks"], got_early = _attn_bwd(proj, bias, sinks, dy_attn, bsz, plug_for(early))
    g["rel_bias"] = _bias_grad(dbias)
    dproj = jnp.concatenate([d_dqkv, dq, dz, dga, dgd, dk, dv, dbd], axis=1)
    g["w_in"] = _mm(u1, dproj, "tn", F32, "mm_dwin")
    if on_mesh:
        g["w_in"] = _from_padded(g["w_in"])
        du1, got_in = _mm(dproj, p["w_in"], "nt", F32, "mm_du1", plug_for(("w_in",)))
        g.update(zip(_FFN + early + ("w_in",), list(got_ffn) + list(got_early) + list(got_in)))
    else:
        du1 = _mm(dproj, p["w_in"], "nt", F32, "mm_du1")
    dx, g["norm_mix_pre"], dsc1, dsh1 = _pre_bwd(x2, p["norm_mix_pre"], sc1, sh1, du1, dh1, "pre1_bwd")
    dmod = jnp.concatenate([dsh1, dsc1, dg1, dsh2, dsc2, dg2], axis=-1).reshape(bsz, N_MOD * d)
    return sq, dx, dmod, g


_SMALL = (("norm_mix_pre", D_MODEL), ("norm_mix_post", D_MODEL), ("norm_ffn_pre", D_MODEL), ("norm_ffn_post", D_MODEL),
          ("dn_norm_w", DN_HD), ("dn_a_log", LANES), ("dn_dt_bias", LANES), ("attn_sinks", AQ_HEADS * LANES),
          ("rel_bias", AQ_HEADS * LANES), ("dn_conv_w", DN_CONV * 3 * DN), ("ffn_conv_w", FFN_CONV * 2 * D_FF))


def _pack_rows(parts, rows):
    flat = jnp.concatenate([a.reshape(-1) for a in parts])
    return jnp.concatenate([flat, jnp.zeros((rows * LANES - flat.shape[0],), F32)]).reshape(rows, LANES)


def _pad128(a):
    flat = a.reshape(-1)
    n = -(-flat.shape[0] // LANES) * LANES
    return jnp.concatenate([flat, jnp.zeros((n - flat.shape[0],), F32)]) if n != flat.shape[0] else flat


_W_NAMES = ("ada_w", "ada_b", "norm_mix_pre", "norm_mix_post", "norm_ffn_pre", "norm_ffn_post", "w_in", "dn_conv_w", "dn_a_log",
            "dn_dt_bias", "dn_norm_w", "attn_sinks", "rel_bias", "w_attn_branch", "w_dn_branch", "w_out", "ffn_w_up", "ffn_conv_w",
            "ffn_w_down")
_BIG = ("w_in", "w_attn_branch", "w_dn_branch", "w_out", "ffn_w_up", "ffn_w_down")
_ROW_SHARDED = ("w_out", "ffn_w_down")


def kernel(x, c, *rest):
    nw = len(_W_NAMES)
    w = dict(zip(_W_NAMES, rest[:nw]))
    loss_target = rest[nw]
    m = dict(zip(_W_NAMES, rest[nw + 1:2 * nw + 1]))
    v = dict(zip(_W_NAMES, rest[2 * nw + 1:3 * nw + 1]))
    ix, iy, ic = lax.axis_index("x"), lax.axis_index("y"), lax.axis_index("c")
    chip, dev = 2 * ix + iy, 4 * ix + 2 * iy + ic
    bsz, s, d = x.shape
    t = bsz * s
    n_dev = 8

    front_rows = 64
    front = _pack_rows([c, w["dn_conv_w"], w["ffn_conv_w"]], front_rows)
    front_all = _allgather8(front, "ag_front").reshape(n_dev, front_rows * LANES)
    n_c, n_dc, n_fc = bsz * d, DN_CONV * 3 * DN // N_CHIPS, FFN_CONV * 2 * D_FF // N_CHIPS
    c_all = front_all[:, :n_c].reshape(n_dev * bsz, d)
    per_chip = front_all[0::2]
    dn_conv_full = jnp.transpose(per_chip[:, n_c:n_c + n_dc].reshape(N_CHIPS, DN_CONV, -1), (1, 0, 2)).reshape(DN_CONV, 3 * DN)
    ffn_conv_full = jnp.transpose(per_chip[:, n_c + n_dc:n_c + n_dc + n_fc].reshape(N_CHIPS, FFN_CONV, -1), (1, 0, 2)).reshape(FFN_CONV, 2 * D_FF)

    mod_cols = N_MOD * d // N_CHIPS
    ada_b_loc = lax.dynamic_slice(w["ada_b"], (0, chip * mod_cols), (1, mod_cols))
    mod_part = _mod_fwd(c_all, w["ada_w"][0], ada_b_loc)
    mod_all = _allgather8(mod_part, "ag_mod").reshape(n_dev, n_dev * bsz, mod_cols)[0::2]
    mod = jnp.transpose(lax.dynamic_slice(mod_all, (0, dev * bsz, 0), (N_CHIPS, bsz, mod_cols)), (1, 0, 2)).reshape(bsz, N_MOD * d)

    gathered = _comm_call(_gather_plug([w[k][0].astype(BF16) for k in _MIXER]), "gather_mixer_weights")
    p = {k: _assemble(k, a) for k, a in zip(_MIXER, gathered)}
    p["w_in"] = _to_padded(p["w_in"])
    for k in ("norm_mix_pre", "norm_mix_post", "norm_ffn_pre", "norm_ffn_post", "dn_norm_w", "attn_sinks"):
        p[k] = w[k]
    p["dn_a_log"], p["dn_dt_bias"], p["rel_bias"] = w["dn_a_log"][0], w["dn_dt_bias"][0], w["rel_bias"]
    p["dn_conv_w"], p["ffn_conv_w"] = dn_conv_full, ffn_conv_full

    sq, dx, dmod, g = _device_step(x.reshape(t, d), loss_target.reshape(t, d), mod, p, bsz, [w[k][0].astype(BF16) for k in _FFN])
    loss = lax.psum(0.5 * jnp.sum(sq), ("x", "y", "c"))

    g["dn_a_log"], g["dn_dt_bias"] = g["dn_a_log"].reshape(-1), g["dn_dt_bias"].reshape(-1)
    small_rows = 328
    small = _pack_rows([dmod] + [g[k] for k, _ in _SMALL], small_rows)
    small_all = _allgather8(small, "ag_small").reshape(n_dev, small_rows, LANES)
    n_dm = bsz * N_MOD * d
    dmod_all = small_all.reshape(n_dev, -1)[:, :n_dm].reshape(n_dev * bsz, N_MOD * d)
    tot = _sum_lead(small_all, "sum_small").reshape(-1)
    gs, off = {}, n_dm
    for k, n in _SMALL:
        gs[k] = tot[off:off + n]
        off += n
    grad = {}
    grad["ada_w"], grad["ada_b"] = _ada_grad(c_all, lax.dynamic_slice(dmod_all, (0, chip * mod_cols), (n_dev * bsz, mod_cols)), dmod_all)
    for k in ("norm_mix_pre", "norm_mix_post", "norm_ffn_pre", "norm_ffn_post", "dn_norm_w"):
        grad[k] = gs[k]
    grad["dn_a_log"] = gs["dn_a_log"][DN_HEADS:2 * DN_HEADS]
    grad["dn_dt_bias"] = gs["dn_dt_bias"][DN_HEADS:2 * DN_HEADS]
    grad["attn_sinks"] = gs["attn_sinks"].reshape(AQ_HEADS, LANES)[:, 0]
    grad["rel_bias"] = gs["rel_bias"].reshape(AQ_HEADS, LANES)[:, :REL_BUCKETS].T
    grad["dn_conv_w"] = lax.dynamic_slice(gs["dn_conv_w"].reshape(DN_CONV, 3 * DN), (0, chip * (3 * DN // N_CHIPS)), (DN_CONV, 3 * DN // N_CHIPS))
    grad["ffn_conv_w"] = lax.dynamic_slice(gs["ffn_conv_w"].reshape(FFN_CONV, 2 * D_FF), (0, chip * (2 * D_FF // N_CHIPS)), (FFN_CONV, 2 * D_FF // N_CHIPS))

    mine = [_sum_lead(g[k], "sum_" + k) for k in _BIG]
    theirs = _sibling_exchange(mine, "exchange_cores")

    out = {}
    for k, a, b in zip(_BIG, mine, theirs):
        out[k] = _adamw(w[k][0], m[k][0], v[k][0], [a, b], "adamw_" + k)
    out["ada_w"] = _adamw(w["ada_w"][0], m["ada_w"][0], v["ada_w"][0], [grad["ada_w"]], "adamw_ada_w")
    small_names = [k for k in _W_NAMES if k not in _BIG and k != "ada_w"]
    offs, n_tot = {}, 0
    for k in small_names:
        offs[k] = n_tot
        n_tot += -(-w[k].size // LANES) * LANES
    pack = lambda dct: jnp.concatenate([_pad128(dct[k]) for k in small_names]).reshape(-1, LANES)
    rows_s = n_tot // LANES
    pad_rows = -(-rows_s // 8) * 8 - rows_s
    padr = lambda a: jnp.concatenate([a, jnp.zeros((pad_rows, LANES), F32)]) if pad_rows else a
    res_small = _adamw(padr(pack(w)), padr(pack(m)), padr(pack(v)), [padr(pack(grad))], "adamw_small")
    for k in small_names:
        out[k] = [r.reshape(-1)[offs[k]:offs[k] + w[k].size].reshape(w[k].shape) for r in res_small]
    for k in _BIG + ("ada_w",):
        out[k] = [r.reshape(w[k].shape) for r in out[k]]
    grads, deltas, new_m, new_v = ([out[k][i] for k in _W_NAMES] for i in range(4))
    return (loss, dx.reshape(bsz, s, d), *grads, *deltas, *new_m, *new_v)
```

```python
import functools
import math

import numpy as np
import jax
import jax.numpy as jnp
from jax import lax
from jax.experimental import pallas as pl
from jax.experimental.pallas import tpu as pltpu

F32 = jnp.float32
BF16 = jnp.bfloat16
HI = lax.Precision.HIGHEST
MESH = pl.DeviceIdType.MESH

D_MODEL = 1024
N_MOD = 6
AQ_HEADS, AKV_HEADS, A_HD, WINDOW = 8, 2, 64, 128
REL_BUCKETS, REL_MAX_DIST = 32, 128
DN_HEADS, DN_HD, DN_CONV, DN_CHUNK = 4, 128, 4, 64
D_FF, FFN_CONV = 2816, 3
RMS_EPS, L2_EPS, NEG_INF = 1e-6, 1e-6, -1e30
AQ, AKV, DN = AQ_HEADS * A_HD, AKV_HEADS * A_HD, DN_HEADS * DN_HD
IN_DIM = AQ + 2 * AKV + 3 * DN + DN + 2 * DN_HEADS + 2 * D_MODEL
C_DQKV, C_Q, C_DZ, C_GA, C_GD, C_K, C_V, C_BD = 0, 1536, 2048, 2560, 3584, 4608, 4736, 4864
IN_PAD = 4992
LANES = 128
N_CHIPS = 4

ADAM_LR, ADAM_B1, ADAM_B2, ADAM_EPS, ADAM_WD, ADAM_STEP = 0.001, 0.9, 0.999, 1e-08, 0.01, 10


def _pick(n, cap):
    best = None
    for t in range(LANES, cap + 1, LANES):
        if n % t == 0:
            best = t
    return best if best is not None else n


def _vspec(shape, index_map):
    return pl.BlockSpec(shape, index_map)


MM_VMEM_BUDGET = 40 * 2 ** 20
GRID_STEP_S = 0.35e-6
HBM_BYTES_PER_S = 3.0e12


def _mm_tiles(m, n, k, mode, in_bytes, out_bytes, split=1):
    best = None
    for tm in [t for t in range(LANES, m + 1, LANES) if m % t == 0]:
        for tn in [t for t in range(LANES, n // split + 1, LANES) if (n // split) % t == 0]:
            a, b, o = k * tm * in_bytes, k * tn * in_bytes, tm * tn * out_bytes
            if 2 * (a + b + o) + (a if mode == "tn" else 0) > MM_VMEM_BUDGET:
                continue
            cost = (m // tm) * (n // tn) * GRID_STEP_S + (m * k * in_bytes + (m // tm) * n * k * in_bytes + m * n * out_bytes) / HBM_BYTES_PER_S
            if best is None or cost < best[0]:
                best = (cost, tm, tn)
    return best[1], best[2]


def _mm(a, b, mode, out_dtype, name, plug=None, split=1):
    if mode == "nn":
        (m, k), n = a.shape, b.shape[1]
        dims = (((1,), (0,)), ((), ()))
    elif mode == "nt":
        (m, k), n = a.shape, b.shape[0]
        dims = (((1,), (1,)), ((), ()))
    else:
        (k, m), n = a.shape, b.shape[1]
        dims = (((0,), (0,)), ((), ()))
    tm, tn = _mm_tiles(m, n, k, mode, a.dtype.itemsize, jnp.dtype(out_dtype).itemsize, split)
    if mode == "tn":
        a_spec = _vspec((k, tm), lambda i, j: (0, i))
    else:
        a_spec = _vspec((tm, k), lambda i, j: (i, 0))
    if mode == "nt":
        b_spec = _vspec((tn, k), lambda i, j: (j, 0))
    else:
        b_spec = _vspec((k, tn), lambda i, j: (0, j))

    def body(a_ref, b_ref, o_ref):
        o_ref[...] = lax.dot_general(a_ref[...].astype(BF16), b_ref[...].astype(BF16), dims,
                                     preferred_element_type=F32).astype(out_dtype).reshape(o_ref.shape)

    grid = (m // tm, n // tn)
    if split == 1:
        out_spec, out_shape = _vspec((tm, tn), lambda i, j: (i, j)), (m, n)
    else:
        per = n // split // tn
        out_spec, out_shape = _vspec((1, tm, tn), lambda i, j: (j // per, i, j % per)), (split, m, n // split)
    (out,), extra = _plugged_call(body, plug, _grid_ends(grid), (a, b), name=name, grid=grid, in_specs=[a_spec, b_spec],
                                  out_specs=[out_spec], out_shape=[jax.ShapeDtypeStruct(out_shape, out_dtype)])
    return out if plug is None else (out, extra)


def _rms(x, w):
    return (x * lax.rsqrt(jnp.mean(x * x, axis=-1, keepdims=True) + RMS_EPS)) * w


def _pre_f(x, w, sc, sh):
    return _rms(x, w) * (1.0 + sc) + sh


def _post_f(y, w, g):
    return g * _rms(y, w)


def _tok_grid(t, bsz, ts):
    nt = t // bsz // ts
    return nt, (bsz, nt)


def _pre_fwd(x, w, sc, sh, name, ts=512):
    t, d = x.shape
    bsz = sc.shape[0]
    nt, grid = _tok_grid(t, bsz, ts)
    row = _vspec((ts, d), lambda b, i: (b * nt + i, 0))
    vec = _vspec((1, d), lambda b, i: (0, 0))
    bvec = _vspec((1, 1, d), lambda b, i: (b, 0, 0))

    def body(x_ref, w_ref, sc_ref, sh_ref, u_ref):
        u_ref[...] = _pre_f(x_ref[...], w_ref[...], sc_ref[0], sh_ref[0]).astype(BF16)

    return pl.pallas_call(body, name=name, grid=grid, in_specs=[row, vec, bvec, bvec], out_specs=row,
                          out_shape=jax.ShapeDtypeStruct((t, d), BF16))(x, w, sc, sh)


def _pre_bwd(x, w, sc, sh, du, dres, name, ts=512):
    t, d = x.shape
    bsz = sc.shape[0]
    nt, grid = _tok_grid(t, bsz, ts)
    row = _vspec((ts, d), lambda b, i: (b * nt + i, 0))
    vec = _vspec((1, d), lambda b, i: (0, 0))
    bvec = _vspec((1, 1, d), lambda b, i: (b, 0, 0))

    def body(x_ref, w_ref, sc_ref, sh_ref, du_ref, dres_ref, dx_ref, dw_ref, dsc_ref, dsh_ref):
        b, i = pl.program_id(0), pl.program_id(1)
        _, vjp = jax.vjp(_pre_f, x_ref[...], w_ref[...], sc_ref[0], sh_ref[0])
        dx, dw, dsc, dsh = vjp(du_ref[...])
        dx_ref[...] = dres_ref[...] + dx

        @pl.when((b == 0) & (i == 0))
        def _():
            dw_ref[...] = jnp.zeros_like(dw_ref)

        @pl.when(i == 0)
        def _():
            dsc_ref[...] = jnp.zeros_like(dsc_ref)
            dsh_ref[...] = jnp.zeros_like(dsh_ref)

        dw_ref[...] += dw
        dsc_ref[0] += dsc
        dsh_ref[0] += dsh

    return pl.pallas_call(
        body, name=name, grid=grid, in_specs=[row, vec, bvec, bvec, row, row], out_specs=[row, vec, bvec, bvec],
        out_shape=[jax.ShapeDtypeStruct((t, d), F32), jax.ShapeDtypeStruct((1, d), F32),
                   jax.ShapeDtypeStruct((bsz, 1, d), F32), jax.ShapeDtypeStruct((bsz, 1, d), F32)],
    )(x, w, sc, sh, du, dres)


def _post_fwd(res, y, w, g, name, ts=512):
    t, d = y.shape
    bsz = g.shape[0]
    nt, grid = _tok_grid(t, bsz, ts)
    row = _vspec((ts, d), lambda b, i: (b * nt + i, 0))
    vec = _vspec((1, d), lambda b, i: (0, 0))
    bvec = _vspec((1, 1, d), lambda b, i: (b, 0, 0))

    def body(res_ref, y_ref, w_ref, g_ref, h_ref):
        h_ref[...] = res_ref[...] + _post_f(y_ref[...], w_ref[...], g_ref[0])

    return pl.pallas_call(body, name=name, grid=grid, in_specs=[row, row, vec, bvec], out_specs=row,
                          out_shape=jax.ShapeDtypeStruct((t, d), F32))(res, y, w, g)


def _post_loss(res, y, w, g, tgt, name, ts=512):
    t, d = y.shape
    bsz = g.shape[0]
    nt, grid = _tok_grid(t, bsz, ts)
    row = _vspec((ts, d), lambda b, i: (b * nt + i, 0))
    vec = _vspec((1, d), lambda b, i: (0, 0))
    bvec = _vspec((1, 1, d), lambda b, i: (b, 0, 0))
    acc = _vspec((1, d), lambda b, i: (0, 0))

    def body(res_ref, y_ref, w_ref, g_ref, tgt_ref, dh_ref, sq_ref):
        b, i = pl.program_id(0), pl.program_id(1)
        e = res_ref[...] + _post_f(y_ref[...], w_ref[...], g_ref[0]) - tgt_ref[...]
        dh_ref[...] = e * (1.0 / d)

        @pl.when((b == 0) & (i == 0))
        def _():
            sq_ref[...] = jnp.zeros_like(sq_ref)

        sq_ref[...] += jnp.sum(e * e, axis=0, keepdims=True) * (1.0 / d)

    return pl.pallas_call(
        body, name=name, grid=grid, in_specs=[row, row, vec, bvec, row], out_specs=[row, acc],
        out_shape=[jax.ShapeDtypeStruct((t, d), F32), jax.ShapeDtypeStruct((1, d), F32)],
    )(res, y, w, g, tgt)


def _post_bwd(dh, y, w, g, name, ts=512):
    t, d = y.shape
    bsz = g.shape[0]
    nt, grid = _tok_grid(t, bsz, ts)
    row = _vspec((ts, d), lambda b, i: (b * nt + i, 0))
    vec = _vspec((1, d), lambda b, i: (0, 0))
    bvec = _vspec((1, 1, d), lambda b, i: (b, 0, 0))

    def body(dh_ref, y_ref, w_ref, g_ref, dy_ref, dw_ref, dg_ref):
        b, i = pl.program_id(0), pl.program_id(1)
        _, vjp = jax.vjp(_post_f, y_ref[...], w_ref[...], g_ref[0])
        dy, dw, dg = vjp(dh_ref[...])
        dy_ref[...] = dy.astype(BF16)

        @pl.when((b == 0) & (i == 0))
        def _():
            dw_ref[...] = jnp.zeros_like(dw_ref)

        @pl.when(i == 0)
        def _():
            dg_ref[...] = jnp.zeros_like(dg_ref)

        dw_ref[...] += dw
        dg_ref[0] += dg

    return pl.pallas_call(
        body, name=name, grid=grid, in_specs=[row, row, vec, bvec], out_specs=[row, vec, bvec],
        out_shape=[jax.ShapeDtypeStruct((t, d), BF16), jax.ShapeDtypeStruct((1, d), F32),
                   jax.ShapeDtypeStruct((bsz, 1, d), F32)],
    )(dh, y, w, g)


def _merge_f(ga, gd, ya, yd):
    return jax.nn.sigmoid(ga) * ya + jax.nn.sigmoid(gd) * yd


_MW = 512


def _merge_fwd(proj, ya, yd, ts=512):
    t, d = ya.shape
    blk = _vspec((ts, _MW), lambda i, j: (i, j))
    ga = _vspec((ts, _MW), lambda i, j: (i, C_GA // _MW + j))
    gd = _vspec((ts, _MW), lambda i, j: (i, C_GD // _MW + j))

    def body(ga_ref, gd_ref, ya_ref, yd_ref, o_ref):
        o_ref[...] = _merge_f(ga_ref[...], gd_ref[...], ya_ref[...], yd_ref[...]).astype(BF16)

    return pl.pallas_call(body, name="merge_fwd", grid=(t // ts, d // _MW), in_specs=[ga, gd, blk, blk], out_specs=blk,
                          out_shape=jax.ShapeDtypeStruct((t, d), BF16))(proj, proj, ya, yd)


def _merge_bwd(proj, ya, yd, dm, ts=512):
    t, d = ya.shape
    blk = _vspec((ts, _MW), lambda i, j: (i, j))
    ga = _vspec((ts, _MW), lambda i, j: (i, C_GA // _MW + j))
    gd = _vspec((ts, _MW), lambda i, j: (i, C_GD // _MW + j))

    def body(ga_ref, gd_ref, ya_ref, yd_ref, dm_ref, dga_ref, dgd_ref, dya_ref, dyd_ref):
        _, vjp = jax.vjp(_merge_f, ga_ref[...], gd_ref[...], ya_ref[...], yd_ref[...])
        dga, dgd, dya, dyd = vjp(dm_ref[...])
        dga_ref[...] = dga.astype(BF16)
        dgd_ref[...] = dgd.astype(BF16)
        dya_ref[...] = dya.astype(BF16)
        dyd_ref[...] = dyd.astype(BF16)

    o = jax.ShapeDtypeStruct((t, d), BF16)
    return pl.pallas_call(body, name="merge_bwd", grid=(t // ts, d // _MW), in_specs=[ga, gd, blk, blk, blk],
                          out_specs=[blk] * 4, out_shape=[o] * 4)(proj, proj, ya, yd, dm)


def _shift_down(x, s):
    if s == 0:
        return x
    r = lax.broadcasted_iota(jnp.int32, x.shape, 0)
    return jnp.where(r >= s, pltpu.roll(x, s, 0), 0.0)


def _shift_up(x, s):
    if s == 0:
        return x
    n = x.shape[0]
    r = lax.broadcasted_iota(jnp.int32, x.shape, 0)
    return jnp.where(r < n - s, pltpu.roll(x, n - s, 0), 0.0)


def _conv_fwd(x, w, k):
    out = None
    for j in range(k):
        term = w[j:j + 1, :] * _shift_down(x, k - 1 - j)
        out = term if out is None else out + term
    return out


def _conv_bwd(x, w, dc, k):
    dx = None
    dws = []
    for j in range(k):
        s = k - 1 - j
        term = w[j:j + 1, :] * _shift_up(dc, s)
        dx = term if dx is None else dx + term
        dws.append(jnp.sum(dc * _shift_down(x, s), axis=0, keepdims=True))
    return dx, jnp.concatenate(dws, axis=0)


def _geglu_f(gate, val):
    return jax.nn.gelu(gate, approximate=True) * val


_FW = 256


def _ffn_act_fwd(up, conv_w, bsz):
    t = up.shape[0]
    s = t // bsz
    nj = D_FF // _FW
    xg = _vspec((s, _FW), lambda b, j: (b, j))
    xv = _vspec((s, _FW), lambda b, j: (b, nj + j))
    wg = _vspec((FFN_CONV, _FW), lambda b, j: (0, j))
    wv = _vspec((FFN_CONV, _FW), lambda b, j: (0, nj + j))

    def body(xg_ref, xv_ref, wg_ref, wv_ref, o_ref):
        gate = _conv_fwd(xg_ref[...], wg_ref[...], FFN_CONV)
        val = _conv_fwd(xv_ref[...], wv_ref[...], FFN_CONV)
        o_ref[...] = _geglu_f(gate, val).astype(BF16)

    return pl.pallas_call(body, name="ffn_act_fwd", grid=(bsz, nj), in_specs=[xg, xv, wg, wv],
                          out_specs=_vspec((s, _FW), lambda b, j: (b, j)),
                          out_shape=jax.ShapeDtypeStruct((t, D_FF), BF16))(up, up, conv_w, conv_w)


def _ffn_act_bwd(up, conv_w, dact, bsz):
    t = up.shape[0]
    s = t // bsz
    nj = D_FF // _FW
    xg = _vspec((s, _FW), lambda j, b: (b, j))
    xv = _vspec((s, _FW), lambda j, b: (b, nj + j))
    wg = _vspec((FFN_CONV, _FW), lambda j, b: (0, j))
    wv = _vspec((FFN_CONV, _FW), lambda j, b: (0, nj + j))
    da = _vspec((s, _FW), lambda j, b: (b, j))
    dwo = _vspec((FFN_CONV, _FW), lambda j, b: (0, j))

    def body(xg_ref, xv_ref, wg_ref, wv_ref, da_ref, dxg_ref, dxv_ref, dwg_ref, dwv_ref):
        b = pl.program_id(1)
        xg_, xv_, wg_, wv_ = xg_ref[...], xv_ref[...], wg_ref[...], wv_ref[...]
        gate = _conv_fwd(xg_, wg_, FFN_CONV)
        val = _conv_fwd(xv_, wv_, FFN_CONV)
        _, vjp = jax.vjp(_geglu_f, gate, val)
        dgate, dval = vjp(da_ref[...])
        dxg, dwg = _conv_bwd(xg_, wg_, dgate, FFN_CONV)
        dxv, dwv = _conv_bwd(xv_, wv_, dval, FFN_CONV)
        dxg_ref[...] = dxg.astype(BF16)
        dxv_ref[...] = dxv.astype(BF16)

        @pl.when(b == 0)
        def _():
            dwg_ref[...] = jnp.zeros_like(dwg_ref)
            dwv_ref[...] = jnp.zeros_like(dwv_ref)

        dwg_ref[...] += dwg
        dwv_ref[...] += dwv

    return pl.pallas_call(
        body, name="ffn_act_bwd", grid=(nj, bsz), in_specs=[xg, xv, wg, wv, da], out_specs=[da, da, dwo, dwo],
        out_shape=[jax.ShapeDtypeStruct((t, D_FF), BF16), jax.ShapeDtypeStruct((t, D_FF), BF16),
                   jax.ShapeDtypeStruct((FFN_CONV, D_FF), F32), jax.ShapeDtypeStruct((FFN_CONV, D_FF), F32)],
    )(up, up, conv_w, conv_w, dact)


def _bucket_table():
    qi = np.arange(WINDOW)[:, None]
    kj = np.arange(2 * WINDOW)[None, :]
    dist = WINDOW + qi - kj
    dc = np.maximum(dist, 0)
    max_exact = REL_BUCKETS // 2
    scaled = np.log(np.maximum(dc, 1).astype(np.float32) / np.float32(max_exact)) / np.float32(math.log(REL_MAX_DIST / max_exact))
    large = max_exact + (scaled.astype(np.float32) * np.float32(REL_BUCKETS - max_exact)).astype(np.int32)
    large = np.minimum(large, REL_BUCKETS - 1)
    bucket = np.where(dc < max_exact, dc, large).astype(np.int32)
    in_band = ((dist >= 0) & (dist < WINDOW)).astype(np.int32)
    return bucket, in_band


def _bias_build(rel_bias):
    bucket, _ = _bucket_table()

    def body(rb_ref, idx_ref, o_ref):
        h = pl.program_id(0)
        idx = idx_ref[...]
        acc = jnp.zeros(idx.shape, F32)
        for r in range(REL_BUCKETS):
            acc = jnp.where(idx == r, rb_ref[r, h], acc)
        o_ref[0] = acc

    return pl.pallas_call(
        body, name="bias_build", grid=(AQ_HEADS,),
        in_specs=[pl.BlockSpec(memory_space=pltpu.SMEM), _vspec((WINDOW, 2 * WINDOW), lambda h: (0, 0))],
        out_specs=_vspec((1, WINDOW, 2 * WINDOW), lambda h: (h, 0, 0)),
        out_shape=jax.ShapeDtypeStruct((AQ_HEADS, WINDOW, 2 * WINDOW), F32),
    )(rel_bias, jnp.asarray(bucket))


def _bias_grad(dbias):
    bucket, _ = _bucket_table()

    def body(db_ref, idx_ref, o_ref):
        idx = idx_ref[...]
        db = db_ref[0]
        lane = lax.broadcasted_iota(jnp.int32, (1, LANES), 1)
        acc = jnp.zeros((1, LANES), F32)
        for r in range(REL_BUCKETS):
            s = jnp.sum(jnp.sum(jnp.where(idx == r, db, 0.0), axis=1, keepdims=True), axis=0, keepdims=True)
            acc = jnp.where(lane == r, s, acc)
        o_ref[0] = acc

    return pl.pallas_call(
        body, name="bias_grad", grid=(AQ_HEADS,),
        in_specs=[_vspec((1, WINDOW, 2 * WINDOW), lambda h: (h, 0, 0)), _vspec((WINDOW, 2 * WINDOW), lambda h: (0, 0))],
        out_specs=_vspec((1, 1, LANES), lambda h: (h, 0, 0)),
        out_shape=jax.ShapeDtypeStruct((AQ_HEADS, 1, LANES), F32),
    )(dbias, jnp.asarray(bucket))


def _attn_mask(n):
    qi = lax.broadcasted_iota(jnp.int32, (WINDOW, 2 * WINDOW), 0)
    kj = lax.broadcasted_iota(jnp.int32, (WINDOW, 2 * WINDOW), 1)
    dist = WINDOW + qi - kj
    band = (dist >= 0) & (dist < WINDOW)
    return band & ((kj >= WINDOW) | (n > 0))


def _attn_probs(q, kb, bias, sink, mask):
    s = lax.dot_general(q, kb, (((1,), (1,)), ((), ())), preferred_element_type=F32) * (A_HD ** -0.5)
    s = jnp.where(mask, s + bias, NEG_INF)
    m = jnp.maximum(jnp.max(s, axis=-1, keepdims=True), sink)
    p = jnp.exp(s - m)
    es = jnp.exp(sink - m)
    inv = 1.0 / (jnp.sum(p, axis=-1, keepdims=True) + es)
    return p * inv, es * inv


def _attn_fwd(proj, bias, sinks, bsz):
    t = proj.shape[0]
    s = t // bsz
    nb = s // WINDOW
    grp = AQ_HEADS // AKV_HEADS

    def body(q_ref, k_ref, v_ref, bias_ref, sink_ref, y_ref, kp_ref, vp_ref):
        kp_ref[0:WINDOW, :] = jnp.zeros((WINDOW, LANES), BF16)
        vp_ref[0:WINDOW, :] = jnp.zeros((WINDOW, LANES), BF16)
        kp_ref[WINDOW:, :] = k_ref[...].astype(BF16)
        vp_ref[WINDOW:, :] = v_ref[...].astype(BF16)

        def blk(n, carry):
            r0 = pl.multiple_of(n * WINDOW, WINDOW)
            mask = _attn_mask(n)
            kband = kp_ref[pl.ds(r0, 2 * WINDOW), :]
            vband = vp_ref[pl.ds(r0, 2 * WINDOW), :]
            qb = q_ref[pl.ds(r0, WINDOW), :].astype(BF16)
            outs = []
            for hq in range(AQ_HEADS):
                kv = hq // grp
                kb = kband[:, kv * A_HD:(kv + 1) * A_HD]
                vb = vband[:, kv * A_HD:(kv + 1) * A_HD]
                probs, _ = _attn_probs(qb[:, hq * A_HD:(hq + 1) * A_HD], kb, bias_ref[hq], sink_ref[0, hq], mask)
                outs.append(jnp.dot(probs.astype(BF16), vb, preferred_element_type=F32))
            y_ref[pl.ds(r0, WINDOW), :] = jnp.concatenate(outs, axis=1).astype(BF16)
            return carry

        lax.fori_loop(0, nb, blk, 0)

    return pl.pallas_call(
        body, name="attn_fwd", grid=(bsz,),
        in_specs=[_vspec((s, AQ), lambda b: (b, C_Q // AQ)), _vspec((s, AKV), lambda b: (b, C_K // AKV)),
                  _vspec((s, AKV), lambda b: (b, C_V // AKV)),
                  _vspec((AQ_HEADS, WINDOW, 2 * WINDOW), lambda b: (0, 0, 0)), pl.BlockSpec(memory_space=pltpu.SMEM)],
        out_specs=_vspec((s, AQ), lambda b: (b, 0)), out_shape=jax.ShapeDtypeStruct((t, AQ), BF16),
        scratch_shapes=[pltpu.VMEM((s + WINDOW, LANES), BF16), pltpu.VMEM((s + WINDOW, LANES), BF16)],
    )(proj, proj, proj, bias, sinks)


def _attn_bwd(proj, bias, sinks, dy, bsz, plug=None):
    t = proj.shape[0]
    s = t // bsz
    nb = s // WINDOW
    grp = AQ_HEADS // AKV_HEADS
    scale = A_HD ** -0.5

    def body(q_ref, k_ref, v_ref, bias_ref, sink_ref, dy_ref, dq_ref, dk_ref, dv_ref, dbias_ref, dsink_ref,
             kp_ref, vp_ref, dkp_ref, dvp_ref):
        b = pl.program_id(0)
        kp_ref[0:WINDOW, :] = jnp.zeros((WINDOW, LANES), BF16)
        vp_ref[0:WINDOW, :] = jnp.zeros((WINDOW, LANES), BF16)
        kp_ref[WINDOW:, :] = k_ref[...].astype(BF16)
        vp_ref[WINDOW:, :] = v_ref[...].astype(BF16)
        dkp_ref[...] = jnp.zeros_like(dkp_ref)
        dvp_ref[...] = jnp.zeros_like(dvp_ref)

        @pl.when(b == 0)
        def _():
            dbias_ref[...] = jnp.zeros_like(dbias_ref)
            dsink_ref[...] = jnp.zeros_like(dsink_ref)

        def blk(n, carry):
            r0 = pl.multiple_of(n * WINDOW, WINDOW)
            mask = _attn_mask(n)
            kband = kp_ref[pl.ds(r0, 2 * WINDOW), :]
            vband = vp_ref[pl.ds(r0, 2 * WINDOW), :]
            qb = q_ref[pl.ds(r0, WINDOW), :].astype(BF16)
            dyb = dy_ref[pl.ds(r0, WINDOW), :].astype(BF16)
            dqs = []
            dks = [None] * AKV_HEADS
            dvs = [None] * AKV_HEADS
            for hq in range(AQ_HEADS):
                kv = hq // grp
                kb = kband[:, kv * A_HD:(kv + 1) * A_HD]
                vb = vband[:, kv * A_HD:(kv + 1) * A_HD]
                qh = qb[:, hq * A_HD:(hq + 1) * A_HD]
                dyh = dyb[:, hq * A_HD:(hq + 1) * A_HD]
                probs, psink = _attn_probs(qh, kb, bias_ref[hq], sink_ref[0, hq], mask)
                dprobs = lax.dot_general(dyh, vb, (((1,), (1,)), ((), ())), preferred_element_type=F32)
                dvh = lax.dot_general(probs.astype(BF16), dyh, (((0,), (0,)), ((), ())), preferred_element_type=F32)
                rowdot = jnp.sum(probs * dprobs, axis=-1, keepdims=True)
                ds = probs * (dprobs - rowdot)
                dbias_ref[hq] += ds
                dsink_ref[hq] += jnp.sum(-psink * rowdot, axis=0, keepdims=True) + jnp.zeros((1, LANES), F32)
                dsb = ds.astype(BF16)
                dqs.append(jnp.dot(dsb, kb, preferred_element_type=F32) * scale)
                dkh = lax.dot_general(dsb, qh, (((0,), (0,)), ((), ())), preferred_element_type=F32) * scale
                dks[kv] = dkh if dks[kv] is None else dks[kv] + dkh
                dvs[kv] = dvh if dvs[kv] is None else dvs[kv] + dvh
            dq_ref[pl.ds(r0, WINDOW), :] = jnp.concatenate(dqs, axis=1).astype(BF16)
            dkp_ref[pl.ds(r0, 2 * WINDOW), :] += jnp.concatenate(dks, axis=1)
            dvp_ref[pl.ds(r0, 2 * WINDOW), :] += jnp.concatenate(dvs, axis=1)
            return carry

        lax.fori_loop(0, nb, blk, 0)
        dk_ref[...] = dkp_ref[WINDOW:, :].astype(BF16)
        dv_ref[...] = dvp_ref[WINDOW:, :].astype(BF16)

    kvs = jax.ShapeDtypeStruct((t, AKV), BF16)
    outs, extra = _plugged_call(
        body, plug, _grid_ends((bsz,)), (proj, proj, proj, bias, sinks, dy), name="attn_bwd", grid=(bsz,),
        in_specs=[_vspec((s, AQ), lambda b: (b, C_Q // AQ)), _vspec((s, AKV), lambda b: (b, C_K // AKV)),
                  _vspec((s, AKV), lambda b: (b, C_V // AKV)),
                  _vspec((AQ_HEADS, WINDOW, 2 * WINDOW), lambda b: (0, 0, 0)), pl.BlockSpec(memory_space=pltpu.SMEM),
                  _vspec((s, AQ), lambda b: (b, 0))],
        out_specs=[_vspec((s, AQ), lambda b: (b, 0)), _vspec((s, AKV), lambda b: (b, 0)), _vspec((s, AKV), lambda b: (b, 0)),
                   _vspec((AQ_HEADS, WINDOW, 2 * WINDOW), lambda b: (0, 0, 0)), _vspec((AQ_HEADS, 1, LANES), lambda b: (0, 0, 0))],
        out_shape=[jax.ShapeDtypeStruct((t, AQ), BF16), kvs, kvs,
                   jax.ShapeDtypeStruct((AQ_HEADS, WINDOW, 2 * WINDOW), F32), jax.ShapeDtypeStruct((AQ_HEADS, 1, LANES), F32)],
        scratch_shapes=[pltpu.VMEM((s + WINDOW, LANES), BF16), pltpu.VMEM((s + WINDOW, LANES), BF16),
                        pltpu.VMEM((s + WINDOW, LANES), F32), pltpu.VMEM((s + WINDOW, LANES), F32)])
    return (*outs, extra)


def _dn_act_f(c, is_qk):
    a = jax.nn.silu(c)
    outs = []
    for h in range(DN_HEADS):
        ah = a[:, h * DN_HD:(h + 1) * DN_HD]
        nh = ah * lax.rsqrt(jnp.sum(ah * ah, axis=-1, keepdims=True) + L2_EPS)
        outs.append(jnp.where(is_qk, nh, ah))
    return jnp.concatenate(outs, axis=1)


def _dn_prep_fwd(proj, conv_w, bsz):
    t = proj.shape[0]
    s = t // bsz
    blk = _vspec((s, DN), lambda b, j: (b, j))
    wsp = _vspec((DN_CONV, DN), lambda b, j: (0, j))

    def body(x_ref, w_ref, o_ref):
        j = pl.program_id(1)
        o_ref[...] = _dn_act_f(_conv_fwd(x_ref[...], w_ref[...], DN_CONV), j < 2)

    return pl.pallas_call(body, name="dn_prep_fwd", grid=(bsz, 3), in_specs=[blk, wsp], out_specs=blk,
                          out_shape=jax.ShapeDtypeStruct((t, 3 * DN), F32))(proj, conv_w)


def _dn_prep_bwd(proj, conv_w, dqkvn, bsz):
    t = proj.shape[0]
    s = t // bsz
    blk = _vspec((s, DN), lambda j, b: (b, j))
    wsp = _vspec((DN_CONV, DN), lambda j, b: (0, j))

    def body(x_ref, w_ref, d_ref, dx_ref, dw_ref):
        j, b = pl.program_id(0), pl.program_id(1)
        x, w = x_ref[...], w_ref[...]
        c = _conv_fwd(x, w, DN_CONV)
        _, vjp = jax.vjp(lambda cc: _dn_act_f(cc, j < 2), c)
        (dc,) = vjp(d_ref[0])
        dx, dw = _conv_bwd(x, w, dc, DN_CONV)
        dx_ref[...] = dx.astype(BF16)

        @pl.when(b == 0)
        def _():
            dw_ref[...] = jnp.zeros_like(dw_ref)

        dw_ref[...] += dw

    return pl.pallas_call(
        body, name="dn_prep_bwd", grid=(3, bsz),
        in_specs=[blk, wsp, _vspec((1, s, DN), lambda j, b: (j, b, 0))], out_specs=[blk, wsp],
        out_shape=[jax.ShapeDtypeStruct((t, 3 * DN), BF16), jax.ShapeDtypeStruct((DN_CONV, 3 * DN), F32)],
    )(proj, conv_w, dqkvn)


def _bg_f(x, alog, dt):
    lane = lax.broadcasted_iota(jnp.int32, x.shape, 1)
    beta = jax.nn.sigmoid(x)
    g = -jnp.exp(alog) * jax.nn.softplus(x + dt)
    return jnp.where(lane < DN_HEADS, beta, jnp.where(lane < 2 * DN_HEADS, g, 0.0))


def _bg_fwd(proj, alog, dt, bsz):
    t = proj.shape[0]
    s = t // bsz
    vec = _vspec((1, LANES), lambda b: (0, 0))

    def body(x_ref, a_ref, d_ref, o_ref):
        o_ref[...] = _bg_f(x_ref[...], a_ref[...], d_ref[...])

    return pl.pallas_call(body, name="bg_fwd", grid=(bsz,), in_specs=[_vspec((s, LANES), lambda b: (b, C_BD // LANES)), vec, vec],
                          out_specs=_vspec((s, LANES), lambda b: (b, 0)), out_shape=jax.ShapeDtypeStruct((t, LANES), F32))(proj, alog, dt)


def _bg_bwd(proj, alog, dt, dbg4, bsz):
    t = proj.shape[0]
    s = t // bsz
    vec = _vspec((1, LANES), lambda b: (0, 0))

    def body(x_ref, a_ref, d_ref, g4_ref, dx_ref, da_ref, dd_ref):
        b = pl.program_id(0)
        lane = lax.broadcasted_iota(jnp.int32, (s, LANES), 1)
        dbg = jnp.zeros((s, LANES), F32)
        for h in range(DN_HEADS):
            gh = g4_ref[:, h * DN_HD:(h + 1) * DN_HD]
            dbg = jnp.where(lane == h, gh[:, 0:1], dbg)
            dbg = jnp.where(lane == DN_HEADS + h, gh[:, 1:2], dbg)
        _, vjp = jax.vjp(_bg_f, x_ref[...], a_ref[...], d_ref[...])
        dx, da, dd = vjp(dbg)
        dx_ref[...] = dx.astype(BF16)

        @pl.when(b == 0)
        def _():
            da_ref[...] = jnp.zeros_like(da_ref)
            dd_ref[...] = jnp.zeros_like(dd_ref)

        da_ref[...] += da
        dd_ref[...] += dd

    return pl.pallas_call(
        body, name="bg_bwd", grid=(bsz,),
        in_specs=[_vspec((s, LANES), lambda b: (b, C_BD // LANES)), vec, vec, _vspec((s, DN), lambda b: (b, 0))],
        out_specs=[_vspec((s, LANES), lambda b: (b, 0)), vec, vec],
        out_shape=[jax.ShapeDtypeStruct((t, LANES), BF16), jax.ShapeDtypeStruct((1, LANES), F32), jax.ShapeDtypeStruct((1, LANES), F32)],
    )(proj, alog, dt, dbg4)


def _dn_out_f(o, z, w):
    outs = []
    for h in range(DN_HEADS):
        sl = slice(h * DN_HD, (h + 1) * DN_HD)
        outs.append(_rms(o[:, sl], w) * jax.nn.silu(z[:, sl]))
    return jnp.concatenate(outs, axis=1)


def _dn_out_fwd(o, proj, w, ts=512):
    t = o.shape[0]
    blk = _vspec((ts, DN), lambda i: (i, 0))
    zsp = _vspec((ts, DN), lambda i: (i, C_DZ // DN))
    vec = _vspec((1, DN_HD), lambda i: (0, 0))

    def body(o_ref, z_ref, w_ref, y_ref):
        y_ref[...] = _dn_out_f(o_ref[...], z_ref[...], w_ref[...]).astype(BF16)

    return pl.pallas_call(body, name="dn_out_fwd", grid=(t // ts,), in_specs=[blk, zsp, vec], out_specs=blk,
                          out_shape=jax.ShapeDtypeStruct((t, DN), BF16))(o, proj, w)


def _dn_out_bwd(o, proj, w, dy, ts=512):
    t = o.shape[0]
    blk = _vspec((ts, DN), lambda i: (i, 0))
    zsp = _vspec((ts, DN), lambda i: (i, C_DZ // DN))
    vec = _vspec((1, DN_HD), lambda i: (0, 0))

    def body(o_ref, z_ref, w_ref, dy_ref, do_ref, dz_ref, dw_ref):
        i = pl.program_id(0)
        _, vjp = jax.vjp(_dn_out_f, o_ref[...], z_ref[...], w_ref[...])
        do, dz, dw = vjp(dy_ref[...])
        do_ref[...] = do
        dz_ref[...] = dz.astype(BF16)

        @pl.when(i == 0)
        def _():
            dw_ref[...] = jnp.zeros_like(dw_ref)

        dw_ref[...] += dw

    return pl.pallas_call(
        body, name="dn_out_bwd", grid=(t // ts,), in_specs=[blk, zsp, vec, blk], out_specs=[blk, blk, vec],
        out_shape=[jax.ShapeDtypeStruct((t, DN), F32), jax.ShapeDtypeStruct((t, DN), BF16), jax.ShapeDtypeStruct((1, DN_HD), F32)],
    )(o, proj, w, dy)


_C = DN_CHUNK


def _dot(a, b, dims):
    return lax.dot_general(a.astype(BF16), b.astype(BF16), dims, preferred_element_type=F32)


_NN = (((1,), (0,)), ((), ()))
_NT = (((1,), (1,)), ((), ()))
_TN = (((0,), (0,)), ((), ()))


_SUB = 8


def _tri_inverse(lt):
    ri = lax.broadcasted_iota(jnp.int32, (_SUB, _C), 0)
    ci = lax.broadcasted_iota(jnp.int32, (_SUB, _C), 1)
    blocks = [jnp.where(ci == ri + _SUB * b, 1.0, 0.0).astype(F32) for b in range(_C // _SUB)]
    for i in range(1, _C):
        acc = None
        for b in range((i + _SUB - 1) // _SUB):
            term = lt[_SUB * b:_SUB * (b + 1), i:i + 1] * blocks[b]
            acc = term if acc is None else acc + term
        row = jnp.sum(acc, axis=0, keepdims=True)
        bi, r = divmod(i, _SUB)
        blocks[bi] = jnp.where(ri == r, blocks[bi] - row, blocks[bi])
    return jnp.concatenate(blocks, axis=0)


_SEG = 512
_HEADS = tuple(range(DN_HEADS))


def _hsl(hh):
    return slice(hh * DN_HD, (hh + 1) * DN_HD)


def _chunk_specs(nseg, reverse):
    seg = (lambda i: nseg - 1 - i) if reverse else (lambda i: i)
    ncs = _SEG // _C
    col = lambda off: _vspec((_SEG, DN), lambda b, i: (b * nseg + seg(i), off))
    return (col, _vspec((_SEG, LANES), lambda b, i: (b * nseg + seg(i), 0)),
            _vspec((1, DN_HEADS, ncs, _C), lambda b, i: (b, 0, seg(i), 0)),
            _vspec((1, DN_HEADS, ncs, DN_HD, DN_HD), lambda b, i: (b, 0, seg(i), 0, 0)))


def _chunk_pre(q_ref, k_ref, v_ref, bg_ref, gr_ref, c, hh):
    r0 = pl.multiple_of(c * _C, _C)
    ri = lax.broadcasted_iota(jnp.int32, (_C, _C), 0)
    ci = lax.broadcasted_iota(jnp.int32, (_C, _C), 1)
    q = q_ref[pl.ds(r0, _C), _hsl(hh)] * (DN_HD ** -0.5)
    k = k_ref[pl.ds(r0, _C), _hsl(hh)]
    v = v_ref[pl.ds(r0, _C), _hsl(hh)]
    bgc = bg_ref[pl.ds(r0, _C), :]
    beta = bgc[:, hh:hh + 1]
    g_col = bgc[:, DN_HEADS + hh:DN_HEADS + hh + 1]
    g_row = gr_ref[0, hh, pl.ds(c, 1), :]
    gc_col = jnp.sum(jnp.where(ri >= ci, g_row, 0.0), axis=1, keepdims=True)
    gc_row = jnp.sum(jnp.where(ri <= ci, g_col, 0.0), axis=0, keepdims=True)
    gc_last = jnp.sum(g_col, axis=0, keepdims=True)
    diff = gc_col - gc_row
    decay = jnp.where(ri >= ci, jnp.exp(jnp.where(ri >= ci, diff, 0.0)), 0.0)
    diff_t = gc_row - gc_col
    decay_t = jnp.where(ri <= ci, jnp.exp(jnp.where(ri <= ci, diff_t, 0.0)), 0.0)
    eg = jnp.exp(gc_col)
    et = jnp.exp(gc_last - gc_col)
    gl = jnp.exp(gc_last)
    kb = k * beta
    vb = v * beta
    return dict(r0=r0, q=q, k=k, v=v, beta=beta, decay=decay, decay_t=decay_t, eg=eg, et=et, gl=gl, kb=kb, vb=vb, ri=ri, ci=ci)


def _chunk_solve(ms, with_transposes):
    for m in ms:
        m["kk_t"] = _dot(m["k"], m["kb"], _NT)
        m["qk"] = _dot(m["q"], m["k"], _NT)
        if with_transposes:
            m["kk"] = _dot(m["kb"], m["k"], _NT)
            m["qk_t"] = _dot(m["k"], m["q"], _NT)
    for m in ms:
        m["tm"] = _tri_inverse(jnp.where(m["ri"] < m["ci"], m["kk_t"] * m["decay_t"], 0.0))
    for m in ms:
        rhs = jnp.concatenate([m["vb"], m["kb"] * m["eg"]], axis=1)
        m["sol"] = jnp.dot(m["tm"], rhs, precision=HI, preferred_element_type=F32)
        m["intra"] = jnp.where(m["ri"] >= m["ci"], m["qk"] * m["decay"], 0.0)


def _dn_chunk_fwd(qkvn, bg, g_rows, bsz, plug=None):
    t = qkvn.shape[0]
    s = t // bsz
    nc, nseg = s // _C, s // _SEG

    def body(q_ref, k_ref, v_ref, bg_ref, gr_ref, o_ref, st_ref, s_ref):
        @pl.when(pl.program_id(1) == 0)
        def _():
            s_ref[...] = jnp.zeros_like(s_ref)

        def chunk(c, carry):
            ms = [_chunk_pre(q_ref, k_ref, v_ref, bg_ref, gr_ref, c, hh) for hh in _HEADS]
            _chunk_solve(ms, False)
            sts = [s_ref[hh] for hh in _HEADS]
            for hh in _HEADS:
                st_ref[0, hh, c] = sts[hh]
            ws = [_dot(m["sol"][:, DN_HD:], st, _NN) for m, st in zip(ms, sts)]
            qs = [_dot(m["q"] * m["eg"], st, _NN) for m, st in zip(ms, sts)]
            v_new = [m["sol"][:, :DN_HD] - a for m, a in zip(ms, ws)]
            iv = [_dot(m["intra"], vn, _NN) for m, vn in zip(ms, v_new)]
            upd = [_dot(m["k"] * m["et"], vn, _TN) for m, vn in zip(ms, v_new)]
            for hh in _HEADS:
                s_ref[hh] = sts[hh] * ms[hh]["gl"] + upd[hh]
                o_ref[pl.ds(ms[hh]["r0"], _C), _hsl(hh)] = qs[hh] + iv[hh]
            return carry

        lax.fori_loop(0, _SEG // _C, chunk, 0)

    col, bgs, grs, sts_spec = _chunk_specs(nseg, False)
    (o, states), extra = _plugged_call(
        body, plug, _grid_ends((bsz, nseg)), (qkvn, qkvn, qkvn, bg, g_rows), name="dn_chunk_fwd", grid=(bsz, nseg),
        in_specs=[col(0), col(1), col(2), bgs, grs], out_specs=[col(0), sts_spec],
        out_shape=[jax.ShapeDtypeStruct((t, DN), F32), jax.ShapeDtypeStruct((bsz, DN_HEADS, nc, DN_HD, DN_HD), F32)],
        scratch_shapes=[pltpu.VMEM((DN_HEADS, DN_HD, DN_HD), F32)])
    return o, states, extra


def _dn_chunk_bwd(qkvn, bg, g_rows, states, do, bsz, plug=None):
    t = qkvn.shape[0]
    s = t // bsz
    nc, nseg = s // _C, s // _SEG

    def body(q_ref, k_ref, v_ref, bg_ref, gr_ref, st_ref, do_ref, dqkv_ref, dbg_ref, ds_ref):
        @pl.when(pl.program_id(1) == 0)
        def _():
            ds_ref[...] = jnp.zeros_like(ds_ref)

        def chunk(cc, carry):
            c = _SEG // _C - 1 - cc
            ms = [_chunk_pre(q_ref, k_ref, v_ref, bg_ref, gr_ref, c, hh) for hh in _HEADS]
            _chunk_solve(ms, True)
            ri, ci = ms[0]["ri"], ms[0]["ci"]
            for hh, m in enumerate(ms):
                m["st"] = st_ref[0, hh, c]
                m["ds_out"] = ds_ref[hh]
                m["do"] = do_ref[pl.ds(m["r0"], _C), _hsl(hh)]
                m["w"] = m["sol"][:, DN_HD:]
            for m in ms:
                m["v_new"] = m["sol"][:, :DN_HD] - _dot(m["w"], m["st"], _NN)
            for m in ms:
                m["q_dec"], m["k_tail"] = m["q"] * m["eg"], m["k"] * m["et"]
                m["dk_tail"] = _dot(m["v_new"], m["ds_out"], _NT)
                m["dv_new"] = _dot(m["k_tail"], m["ds_out"], _NN) + _dot(m["intra"], m["do"], _TN)
                m["dq_dec"] = _dot(m["do"], m["st"], _NT)
                m["ds_in"] = m["ds_out"] * m["gl"] + _dot(m["q_dec"], m["do"], _TN)
                m["dintra"] = jnp.where(ri >= ci, _dot(m["do"], m["v_new"], _NT), 0.0)
                m["dintra_t"] = jnp.where(ri <= ci, _dot(m["v_new"], m["do"], _NT), 0.0)
            for m in ms:
                m["dw"] = -_dot(m["dv_new"], m["st"], _NT)
                m["ds_in"] = m["ds_in"] - _dot(m["w"], m["dv_new"], _TN)
            for m in ms:
                dsol = jnp.concatenate([m["dv_new"], m["dw"]], axis=1)
                m["drhs"] = lax.dot_general(m["tm"], dsol, _TN, precision=HI, preferred_element_type=F32)
            for m in ms:
                m["dl"] = jnp.where(ri > ci, -_dot(m["drhs"], m["sol"], _NT), 0.0)
                m["dl_t"] = jnp.where(ri < ci, -_dot(m["sol"], m["drhs"], _NT), 0.0)
            for m in ms:
                m["dkb2"] = _dot(m["dl"] * m["decay"], m["k"], _NN)
                m["dk"] = _dot(m["dl_t"] * m["decay_t"], m["kb"], _NN) + _dot(m["dintra_t"] * m["decay_t"], m["q"], _NN)
                m["dq"] = _dot(m["dintra"] * m["decay"], m["k"], _NN)
            for m in ms:
                _chunk_bwd_finish(m)
            for m in ms:
                m["dg_b"] = jnp.dot(jnp.where(ri <= ci, 1.0, 0.0).astype(F32), m["dgc"] + jnp.zeros((_C, LANES), F32),
                                    precision=HI, preferred_element_type=F32)
            lane = lax.broadcasted_iota(jnp.int32, (_C, LANES), 1)
            for hh, m in enumerate(ms):
                rows = pl.ds(m["r0"], _C)
                dqkv_ref[0, rows, _hsl(hh)] = m["dq"] * (DN_HD ** -0.5)
                dqkv_ref[1, rows, _hsl(hh)] = m["dk"]
                dqkv_ref[2, rows, _hsl(hh)] = m["dv"]
                dbg_ref[rows, _hsl(hh)] = jnp.where(lane == 0, m["dbeta"], jnp.where(lane == 1, m["dg_b"], 0.0))
                ds_ref[hh] = m["ds_in"]
            return carry

        lax.fori_loop(0, _SEG // _C, chunk, 0)

    col, bgs, grs, sts_spec = _chunk_specs(nseg, True)
    o = jax.ShapeDtypeStruct((t, DN), F32)
    outs, extra = _plugged_call(
        body, plug, _grid_ends((bsz, nseg)), (qkvn, qkvn, qkvn, bg, g_rows, states, do), name="dn_chunk_bwd", grid=(bsz, nseg),
        in_specs=[col(0), col(1), col(2), bgs, grs, sts_spec, col(0)],
        out_specs=[_vspec((3, _SEG, DN), lambda b, i: (0, b * nseg + nseg - 1 - i, 0)), col(0)],
        out_shape=[jax.ShapeDtypeStruct((3, t, DN), F32), o],
        scratch_shapes=[pltpu.VMEM((DN_HEADS, DN_HD, DN_HD), F32)])
    return (*outs, extra)


def _chunk_bwd_finish(m):
    q, k, v, beta, decay, decay_t = m["q"], m["k"], m["v"], m["beta"], m["decay"], m["decay_t"]
    eg, et, gl, kb, dl, dl_t, dintra, dintra_t = m["eg"], m["et"], m["gl"], m["kb"], m["dl"], m["dl_t"], m["dintra"], m["dintra_t"]
    dq_dec, dk_tail, dq, dk = m["dq_dec"], m["dk_tail"], m["dq"], m["dk"]
    dgl = jnp.sum(jnp.sum(m["ds_out"] * m["st"], axis=1, keepdims=True), axis=0, keepdims=True)
    dvb, dkbeg = m["drhs"][:, :DN_HD], m["drhs"][:, DN_HD:]
    dkb = dkbeg * eg + m["dkb2"]
    deg = jnp.sum(dkbeg * kb, axis=1, keepdims=True)
    em = (dl * m["kk"] + dintra * m["qk"]) * decay
    em_t = (dl_t * m["kk_t"] + dintra_t * m["qk_t"]) * decay_t
    dgc = jnp.sum(em, axis=1, keepdims=True) - jnp.sum(em_t, axis=1, keepdims=True)
    dq = dq + dq_dec * eg
    deg = deg + jnp.sum(dq_dec * q, axis=1, keepdims=True)
    dk = dk + dk_tail * et
    det = jnp.sum(dk_tail * k, axis=1, keepdims=True)
    dgc = dgc + deg * eg - det * et
    dgc_last = jnp.sum(det * et, axis=0, keepdims=True) + dgl * gl
    rcol = lax.broadcasted_iota(jnp.int32, (_C, 1), 0)
    m["dgc"] = dgc + jnp.where(rcol == _C - 1, dgc_last, 0.0)
    m["dq"] = dq
    m["dk"] = dk + dkb * beta
    m["dbeta"] = jnp.sum(dkb * k, axis=1, keepdims=True) + jnp.sum(dvb * v, axis=1, keepdims=True)
    m["dv"] = dvb * beta


def _mod_fwd(c_all, ada_w_loc, ada_b_loc):
    n, cols = c_all.shape[0], ada_w_loc.shape[1]

    def body(c_ref, w_ref, b_ref, o_ref):
        o_ref[...] = _dot(jax.nn.silu(c_ref[...]), w_ref[...], _NN) + b_ref[...]

    return pl.pallas_call(body, name="mod_fwd", out_shape=jax.ShapeDtypeStruct((n, cols), F32))(c_all, ada_w_loc, ada_b_loc)


def _ada_grad(c_all, dmod_loc, dmod_all):
    d, cols = c_all.shape[1], dmod_loc.shape[1]

    def body(c_ref, dl_ref, da_ref, gw_ref, gb_ref):
        gw_ref[...] = _dot(jax.nn.silu(c_ref[...]), dl_ref[...], _TN)
        gb_ref[...] = jnp.sum(da_ref[...], axis=0, keepdims=True)

    return pl.pallas_call(body, name="ada_grad", out_shape=[jax.ShapeDtypeStruct((d, cols), F32),
                                                           jax.ShapeDtypeStruct((1, dmod_all.shape[1]), F32)])(c_all, dmod_loc, dmod_all)


ELEMENTWISE_BLOCK_BYTES = 3 * 2 ** 19


def _row_tile(r, c=1024):
    best = None
    for tr in range(16, r + 1, 16):
        if r % tr == 0 and tr * c * 4 <= ELEMENTWISE_BLOCK_BYTES:
            best = tr
    return best if best is not None else r


def _adamw(w, m, v, grads, name):
    r, c = w.shape
    tr = _row_tile(r, c)
    blk = _vspec((tr, c), lambda i: (i, 0))
    n = len(grads)

    def body(*refs):
        w_ref, m_ref, v_ref = refs[:3]
        g_ref, d_ref, mo_ref, vo_ref = refs[3 + n:]
        g = refs[3][...]
        for p in refs[4:3 + n]:
            g = g + p[...]
        m_new = ADAM_B1 * m_ref[...] + (1.0 - ADAM_B1) * g
        v_new = ADAM_B2 * v_ref[...] + (1.0 - ADAM_B2) * jnp.square(g)
        m_hat = m_new / (1.0 - ADAM_B1 ** ADAM_STEP)
        v_hat = v_new / (1.0 - ADAM_B2 ** ADAM_STEP)
        g_ref[...] = g
        d_ref[...] = -ADAM_LR * (m_hat / (jnp.sqrt(v_hat) + ADAM_EPS) + ADAM_WD * w_ref[...])
        mo_ref[...] = m_new
        vo_ref[...] = v_new

    o = jax.ShapeDtypeStruct((r, c), F32)
    return pl.pallas_call(body, name=name, grid=(r // tr,), in_specs=[blk] * (3 + n), out_specs=[blk] * 4,
                          out_shape=[o] * 4)(w, m, v, *grads)


def _sum_lead(x, name):
    p, r, c = x.shape
    tr = _row_tile(r, c)

    def body(x_ref, o_ref):
        acc = x_ref[0].astype(F32)
        for i in range(1, p):
            acc = acc + x_ref[i].astype(F32)
        o_ref[...] = acc

    return pl.pallas_call(body, name=name, grid=(r // tr,), in_specs=[_vspec((p, tr, c), lambda i: (0, i, 0))],
                          out_specs=_vspec((tr, c), lambda i: (i, 0)), out_shape=jax.ShapeDtypeStruct((r, c), F32))(x)


def _allgather8(x_shard, name):
    m_per, n = x_shard.shape

    def body(x_ref, out_ref, send_sems, recv_sems, local_sem):
        x, y, c = lax.axis_index("x"), lax.axis_index("y"), lax.axis_index("c")
        me, sibling = (x, y, c), (x, y, 1 - c)
        chips = [(1 - x, y), (x, 1 - y), (1 - x, 1 - y)]

        def rows(px, py, pc):
            return out_ref.at[pl.ds((4 * px + 2 * py + pc) * m_per, m_per), :]

        def copy(k, block, to, src=None):
            return pltpu.make_async_remote_copy(
                src_ref=rows(*block) if src is None else src, dst_ref=rows(*block), send_sem=send_sems.at[k],
                recv_sem=recv_sems.at[k], device_id=to, device_id_type=MESH)

        mine = pltpu.make_async_copy(x_ref, rows(*me), local_sem)
        mine.start()
        first = [copy(0, me, sibling, src=x_ref)]
        first += [copy(1 + j, me, (*chip, c), src=x_ref) for j, chip in enumerate(chips)]
        for cp in first:
            cp.start()
        passed = [copy(4 + j, (*chip, c), sibling) for j, chip in enumerate(chips)]
        for j, chip in enumerate(chips):
            copy(1 + j, (*chip, c), me).wait_recv()
            passed[j].start()
        copy(0, sibling, me).wait_recv()
        for j, chip in enumerate(chips):
            copy(4 + j, (*chip, 1 - c), me).wait_recv()
        for cp in first + passed:
            cp.wait_send()
        mine.wait()

    return pl.pallas_call(
        body, name=name, out_shape=jax.ShapeDtypeStruct((8 * m_per, n), x_shard.dtype),
        in_specs=[pl.BlockSpec(memory_space=pltpu.VMEM)], out_specs=pl.BlockSpec(memory_space=pltpu.VMEM),
        scratch_shapes=[pltpu.SemaphoreType.DMA((7,)), pltpu.SemaphoreType.DMA((7,)), pltpu.SemaphoreType.DMA],
    )(x_shard)


_HBM = pl.BlockSpec(memory_space=pltpu.HBM)


def _mesh_place():
    x, y, c = lax.axis_index("x"), lax.axis_index("y"), lax.axis_index("c")
    return x, y, c, 2 * x + y, [(1 - x, y), (x, 1 - y), (1 - x, 1 - y)]


def _gather_plug(shards):
    n = len(shards)

    def half(ref, c, lead=None):
        r = ref.shape[-2] // 2
        rows = pl.ds(pl.multiple_of(c * r, 16), r)
        return ref.at[rows, :] if lead is None else ref.at[lead, rows, :]

    def copies(ins, outs, send, recv):
        x, y, c, me, chips = _mesh_place()
        ici, fwd, fwd_in = [], [], []
        for i in range(n):
            for j, (px, py) in enumerate(chips):
                q = 2 * px + py
                ici.append((pltpu.make_async_remote_copy(
                    src_ref=half(ins[i], c), dst_ref=half(outs[i], c, me), send_sem=send.at[6 * i + j], recv_sem=recv.at[6 * i + j],
                    device_id=(px, py, c), device_id_type=MESH),
                    pltpu.make_async_remote_copy(
                    src_ref=half(ins[i], c), dst_ref=half(outs[i], c, q), send_sem=send.at[6 * i + j], recv_sem=recv.at[6 * i + j],
                    device_id=(px, py, c), device_id_type=MESH)))
                fwd.append(pltpu.make_async_remote_copy(
                    src_ref=half(outs[i], c, q), dst_ref=half(outs[i], c, q), send_sem=send.at[6 * i + 3 + j],
                    recv_sem=recv.at[6 * i + 3 + j], device_id=(x, y, 1 - c), device_id_type=MESH))
                fwd_in.append(pltpu.make_async_remote_copy(
                    src_ref=half(outs[i], 1 - c, q), dst_ref=half(outs[i], 1 - c, q), send_sem=send.at[6 * i + 3 + j],
                    recv_sem=recv.at[6 * i + 3 + j], device_id=(x, y, 1 - c), device_id_type=MESH))
        return ici, fwd, fwd_in, me

    def start(ins, outs, send, recv, loc):
        ici, _, _, me = copies(ins, outs, send, recv)
        for i in range(n):
            pltpu.make_async_copy(ins[i], outs[i].at[me], loc.at[i]).start()
        for out_cp, _ in ici:
            out_cp.start()

    def finish(ins, outs, send, recv, loc):
        ici, fwd, fwd_in, me = copies(ins, outs, send, recv)
        for (_, in_cp), f in zip(ici, fwd):
            in_cp.wait_recv()
            f.start()
        for f in fwd_in:
            f.wait_recv()
        for (out_cp, _), f in zip(ici, fwd):
            out_cp.wait_send()
            f.wait_send()
        for i in range(n):
            pltpu.make_async_copy(ins[i], outs[i].at[me], loc.at[i]).wait()

    return dict(ins=list(shards), out_shape=[jax.ShapeDtypeStruct((N_CHIPS,) + a.shape, a.dtype) for a in shards],
                scratch=[pltpu.SemaphoreType.DMA((6 * n,)), pltpu.SemaphoreType.DMA((6 * n,)), pltpu.SemaphoreType.DMA((n,))],
                start=start, finish=finish)


def _exchange_plug(pieces):
    n = len(pieces)

    def copies(ins, outs, send, recv):
        x, y, c, me, chips = _mesh_place()
        out_cps, in_cps = [], []
        for i in range(n):
            for j, (px, py) in enumerate(chips):
                q = 2 * px + py
                out_cps.append(pltpu.make_async_remote_copy(src_ref=ins[i].at[q], dst_ref=outs[i].at[me], send_sem=send.at[3 * i + j],
                                                            recv_sem=recv.at[3 * i + j], device_id=(px, py, c), device_id_type=MESH))
                in_cps.append(pltpu.make_async_remote_copy(src_ref=ins[i].at[me], dst_ref=outs[i].at[q], send_sem=send.at[3 * i + j],
                                                           recv_sem=recv.at[3 * i + j], device_id=(px, py, c), device_id_type=MESH))
        return out_cps, in_cps, me

    def start(ins, outs, send, recv, loc):
        out_cps, _, me = copies(ins, outs, send, recv)
        for i in range(n):
            pltpu.make_async_copy(ins[i].at[me], outs[i].at[me], loc.at[i]).start()
        for cp in out_cps:
            cp.start()

    def finish(ins, outs, send, recv, loc):
        out_cps, in_cps, me = copies(ins, outs, send, recv)
        for cp in in_cps:
            cp.wait_recv()
        for cp in out_cps:
            cp.wait_send()
        for i in range(n):
            pltpu.make_async_copy(ins[i].at[me], outs[i].at[me], loc.at[i]).wait()

    return dict(ins=list(pieces), out_shape=[jax.ShapeDtypeStruct(a.shape, a.dtype) for a in pieces],
                scratch=[pltpu.SemaphoreType.DMA((3 * n,)), pltpu.SemaphoreType.DMA((3 * n,)), pltpu.SemaphoreType.DMA((n,))],
                start=start, finish=finish)


def _comm_call(plug, name):
    n_in, n_out = len(plug["ins"]), len(plug["out_shape"])

    def body(*refs):
        ins, outs, sems = refs[:n_in], refs[n_in:n_in + n_out], refs[n_in + n_out:]
        plug["start"](ins, outs, *sems)
        plug["finish"](ins, outs, *sems)

    return pl.pallas_call(body, name=name, out_shape=plug["out_shape"], in_specs=[_HBM] * n_in, out_specs=[_HBM] * n_out,
                          scratch_shapes=plug["scratch"])(*plug["ins"])


def _plugged_call(body, plug, first_last, args, *, name, grid, in_specs, out_specs, out_shape, scratch_shapes=()):
    in_specs, out_specs, out_shape, scratch_shapes = list(in_specs), list(out_specs), list(out_shape), list(scratch_shapes)
    if plug is None:
        return pl.pallas_call(body, name=name, grid=grid, in_specs=in_specs, out_specs=out_specs, out_shape=out_shape,
                              scratch_shapes=scratch_shapes)(*args), []
    n_in, n_out, n_sc = len(in_specs), len(out_specs), len(scratch_shapes)
    p_in, p_out = len(plug["ins"]), len(plug["out_shape"])

    def full(*refs):
        ins, refs = refs[:n_in], refs[n_in:]
        pins, refs = refs[:p_in], refs[p_in:]
        outs, refs = refs[:n_out], refs[n_out:]
        pouts, refs = refs[:p_out], refs[p_out:]
        scr, psems = refs[:n_sc], refs[n_sc:]
        first, last = first_last()

        @pl.when(first)
        def _():
            plug["start"](pins, pouts, *psems)

        body(*ins, *outs, *scr)

        @pl.when(last)
        def _():
            plug["finish"](pins, pouts, *psems)

    res = pl.pallas_call(full, name=name, grid=grid, in_specs=in_specs + [_HBM] * p_in, out_specs=out_specs + [_HBM] * p_out,
                         out_shape=out_shape + plug["out_shape"], scratch_shapes=scratch_shapes + plug["scratch"])(*args, *plug["ins"])
    return res[:n_out], res[n_out:]


def _grid_ends(grid):
    def ends():
        first = last = None
        for ax, n in enumerate(grid):
            i = pl.program_id(ax)
            first = (i == 0) if first is None else first & (i == 0)
            last = (i == n - 1) if last is None else last & (i == n - 1)
        return first, last
    return ends


def _sibling_exchange(arrs, name):
    n = len(arrs)

    def body(*refs):
        ins, outs = refs[:n], refs[n:2 * n]
        send_sems, recv_sems = refs[2 * n:]
        sibling = (lax.axis_index("x"), lax.axis_index("y"), 1 - lax.axis_index("c"))
        cps = [pltpu.make_async_remote_copy(src_ref=ins[i], dst_ref=outs[i], send_sem=send_sems.at[i], recv_sem=recv_sems.at[i],
                                            device_id=sibling, device_id_type=MESH) for i in range(n)]
        for cp in cps:
            cp.start()
        for cp in cps:
            cp.wait()

    return pl.pallas_call(
        body, name=name, out_shape=[jax.ShapeDtypeStruct(a.shape, a.dtype) for a in arrs], in_specs=[_HBM] * n, out_specs=[_HBM] * n,
        scratch_shapes=[pltpu.SemaphoreType.DMA((n,)), pltpu.SemaphoreType.DMA((n,))],
    )(*arrs)


def _to_padded(w):
    z = jnp.zeros((w.shape[0], IN_PAD - IN_DIM), w.dtype)
    return jnp.concatenate([w[:, 768:2304], w[:, 0:512], w[:, 2304:2816], w[:, 2824:3848], w[:, 3848:4872],
                            w[:, 512:640], w[:, 640:768], w[:, 2816:2824], z], axis=1)


def _from_padded(g):
    return jnp.concatenate([g[:, C_Q:C_Q + AQ], g[:, C_K:C_K + AKV], g[:, C_V:C_V + AKV], g[:, C_DQKV:C_DQKV + 3 * DN],
                            g[:, C_DZ:C_DZ + DN], g[:, C_BD:C_BD + 2 * DN_HEADS], g[:, C_GA:C_GA + D_MODEL],
                            g[:, C_GD:C_GD + D_MODEL]], axis=1)


def _lane_vec(a):
    return jnp.zeros((1, LANES), F32).at[0, DN_HEADS:2 * DN_HEADS].set(a)


_ROW_SHARDED = ("w_out", "ffn_w_down")
_FFN = ("ffn_w_up", "ffn_w_down")
_LATE_MIXER = ("w_attn_branch", "w_dn_branch", "w_out")


def _pieces(k, a):
    if a.ndim == 3:
        return a
    if k in _ROW_SHARDED:
        return a.reshape(N_CHIPS, a.shape[0] // N_CHIPS, a.shape[1]).astype(BF16)
    return jnp.transpose(a.reshape(a.shape[0], N_CHIPS, a.shape[1] // N_CHIPS), (1, 0, 2)).astype(BF16)


def _assemble(k, a):
    if k in _ROW_SHARDED:
        return a.reshape(-1, a.shape[2])
    return jnp.transpose(a, (1, 0, 2)).reshape(a.shape[1], -1)


def _device_step(x2, tgt2, mod, p, bsz, shards=None):
    d = D_MODEL
    on_mesh = shards is not None
    p = dict(p)
    sh1, sc1, g1, sh2, sc2, g2 = [mod[:, i * d:(i + 1) * d].reshape(bsz, 1, d) for i in range(N_MOD)]
    alog_v, dt_v = _lane_vec(p["dn_a_log"]), _lane_vec(p["dn_dt_bias"])
    sinks = p["attn_sinks"].reshape(1, AQ_HEADS)
    u1 = _pre_fwd(x2, p["norm_mix_pre"], sc1, sh1, "pre1_fwd")
    if on_mesh:
        proj, got = _mm(u1, p["w_in"], "nn", F32, "mm_proj", _gather_plug(shards["late_mixer"]))
        p.update({k: _assemble(k, a) for k, a in zip(_LATE_MIXER, got)})
    else:
        proj = _mm(u1, p["w_in"], "nn", F32, "mm_proj")
    bias = _bias_build(p["rel_bias"])
    y_attn = _attn_fwd(proj, bias, sinks, bsz)
    qkvn = _dn_prep_fwd(proj, p["dn_conv_w"], bsz)
    bg = _bg_fwd(proj, alog_v, dt_v, bsz)
    nc = x2.shape[0] // bsz // DN_CHUNK
    g_rows = jnp.transpose(bg[:, DN_HEADS:2 * DN_HEADS].reshape(bsz, nc, DN_CHUNK, DN_HEADS), (0, 3, 1, 2))
    o, states, got = _dn_chunk_fwd(qkvn, bg, g_rows, bsz, _gather_plug(shards["ffn"]) if on_mesh else None)
    for k, a in zip(_FFN, got):
        p[k] = _assemble(k, a)
    y_dn = _dn_out_fwd(o, proj, p["dn_norm_w"])
    ya = _mm(y_attn, p["w_attn_branch"], "nn", F32, "mm_ya")
    yd = _mm(y_dn, p["w_dn_branch"], "nn", F32, "mm_yd")
    merged = _merge_fwd(proj, ya, yd)
    y1 = _mm(merged, p["w_out"], "nn", F32, "mm_y1")
    h1 = _post_fwd(x2, y1, p["norm_mix_post"], g1, "post1_fwd")
    u2 = _pre_fwd(h1, p["norm_ffn_pre"], sc2, sh2, "pre2_fwd")
    up = _mm(u2, p["ffn_w_up"], "nn", F32, "mm_up")
    act = _ffn_act_fwd(up, p["ffn_conv_w"], bsz)
    y2 = _mm(act, p["ffn_w_down"], "nn", F32, "mm_y2")
    dh2, sq = _post_loss(h1, y2, p["norm_ffn_post"], g2, tgt2, "post2_loss")
    g = {}
    dy2, g["norm_ffn_post"], dg2 = _post_bwd(dh2, y2, p["norm_ffn_post"], g2, "post2_bwd")
    dact = _mm(dy2, p["ffn_w_down"], "nt", F32, "mm_dact")
    g["ffn_w_down"] = _mm(act, dy2, "tn", BF16, "mm_dwdown")
    dupg, dupv, dcwg, dcwv = _ffn_act_bwd(up, p["ffn_conv_w"], dact, bsz)
    g["ffn_conv_w"] = jnp.concatenate([dcwg, dcwv], axis=1)
    dup = jnp.concatenate([dupg, dupv], axis=1)
    g["ffn_w_up"] = _mm(u2, dup, "tn", BF16, "mm_dwup", split=N_CHIPS)
    du2 = _mm(dup, p["ffn_w_up"], "nt", F32, "mm_du2")
    dh1, g["norm_ffn_pre"], dsc2, dsh2 = _pre_bwd(h1, p["norm_ffn_pre"], sc2, sh2, du2, dh2, "pre2_bwd")
    dy1, g["norm_mix_post"], dg1 = _post_bwd(dh1, y1, p["norm_mix_post"], g1, "post1_bwd")
    dmerged = _mm(dy1, p["w_out"], "nt", F32, "mm_dmerged")
    g["w_out"] = _mm(merged, dy1, "tn", BF16, "mm_dwout")
    dga, dgd, dya, dyd = _merge_bwd(proj, ya, yd, dmerged)
    dy_attn = _mm(dya, p["w_attn_branch"], "nt", BF16, "mm_dyattn")
    g["w_attn_branch"] = _mm(y_attn, dya, "tn", BF16, "mm_dwab", split=N_CHIPS)
    dy_dn = _mm(dyd, p["w_dn_branch"], "nt", F32, "mm_dydn")
    g["w_dn_branch"] = _mm(y_dn, dyd, "tn", BF16, "mm_dwdb", split=N_CHIPS)
    do, dz, g["dn_norm_w"] = _dn_out_bwd(o, proj, p["dn_norm_w"], dy_dn)
    def plug_for(names):
        return _exchange_plug([_pieces(k, g[k]) for k in names]) if on_mesh else None

    early = ("w_out", "w_attn_branch", "w_dn_branch")
    dqkvn, dbg4, got_ffn = _dn_chunk_bwd(qkvn, bg, g_rows, states, do, bsz, plug_for(_FFN))
    d_dqkv, g["dn_conv_w"] = _dn_prep_bwd(proj, p["dn_conv_w"], dqkvn, bsz)
    dbd, g["dn_a_log"], g["dn_dt_bias"] = _bg_bwd(proj, alog_v, dt_v, dbg4, bsz)
    dq, dk, dv, dbias, g["attn_sinks"], got_early = _attn_bwd(proj, bias, sinks, dy_attn, bsz, plug_for(early))
    g["rel_bias"] = _bias_grad(dbias)
    dproj = jnp.concatenate([d_dqkv, dq, dz, dga, dgd, dk, dv, dbd], axis=1)
    g["w_in"] = _mm(u1, dproj, "tn", BF16, "mm_dwin")
    if on_mesh:
        g["w_in"] = _from_padded(g["w_in"])
        du1, got_in = _mm(dproj, p["w_in"], "nt", F32, "mm_du1", plug_for(("w_in",)))
        g.update(zip(_FFN + early + ("w_in",), list(got_ffn) + list(got_early) + list(got_in)))
    else:
        du1 = _mm(dproj, p["w_in"], "nt", F32, "mm_du1")
    dx, g["norm_mix_pre"], dsc1, dsh1 = _pre_bwd(x2, p["norm_mix_pre"], sc1, sh1, du1, dh1, "pre1_bwd")
    dmod = jnp.concatenate([dsh1, dsc1, dg1, dsh2, dsc2, dg2], axis=-1).reshape(bsz, N_MOD * d)
    return sq, dx, dmod, g


_SMALL = (("norm_mix_pre", D_MODEL), ("norm_mix_post", D_MODEL), ("norm_ffn_pre", D_MODEL), ("norm_ffn_post", D_MODEL),
          ("dn_norm_w", DN_HD), ("dn_a_log", LANES), ("dn_dt_bias", LANES), ("attn_sinks", AQ_HEADS * LANES),
          ("rel_bias", AQ_HEADS * LANES), ("dn_conv_w", DN_CONV * 3 * DN), ("ffn_conv_w", FFN_CONV * 2 * D_FF))


def _pack_rows(parts, rows):
    flat = jnp.concatenate([a.reshape(-1) for a in parts])
    return jnp.concatenate([flat, jnp.zeros((rows * LANES - flat.shape[0],), F32)]).reshape(rows, LANES)


def _pad128(a):
    flat = a.reshape(-1)
    n = -(-flat.shape[0] // LANES) * LANES
    return jnp.concatenate([flat, jnp.zeros((n - flat.shape[0],), F32)]) if n != flat.shape[0] else flat


_W_NAMES = ("ada_w", "ada_b", "norm_mix_pre", "norm_mix_post", "norm_ffn_pre", "norm_ffn_post", "w_in", "dn_conv_w", "dn_a_log",
            "dn_dt_bias", "dn_norm_w", "attn_sinks", "rel_bias", "w_attn_branch", "w_dn_branch", "w_out", "ffn_w_up", "ffn_conv_w",
            "ffn_w_down")
_BIG = ("w_in", "w_attn_branch", "w_dn_branch", "w_out", "ffn_w_up", "ffn_w_down")
_ROW_SHARDED = ("w_out", "ffn_w_down")


def kernel(x, c, *rest):
    nw = len(_W_NAMES)
    w = dict(zip(_W_NAMES, rest[:nw]))
    loss_target = rest[nw]
    m = dict(zip(_W_NAMES, rest[nw + 1:2 * nw + 1]))
    v = dict(zip(_W_NAMES, rest[2 * nw + 1:3 * nw + 1]))
    ix, iy, ic = lax.axis_index("x"), lax.axis_index("y"), lax.axis_index("c")
    chip, dev = 2 * ix + iy, 4 * ix + 2 * iy + ic
    bsz, s, d = x.shape
    t = bsz * s
    n_dev = 8

    front_rows = 64
    front = _pack_rows([c, w["dn_conv_w"], w["ffn_conv_w"]], front_rows)
    front_all = _allgather8(front, "ag_front").reshape(n_dev, front_rows * LANES)
    n_c, n_dc, n_fc = bsz * d, DN_CONV * 3 * DN // N_CHIPS, FFN_CONV * 2 * D_FF // N_CHIPS
    c_all = front_all[:, :n_c].reshape(n_dev * bsz, d)
    per_chip = front_all[0::2]
    dn_conv_full = jnp.transpose(per_chip[:, n_c:n_c + n_dc].reshape(N_CHIPS, DN_CONV, -1), (1, 0, 2)).reshape(DN_CONV, 3 * DN)
    ffn_conv_full = jnp.transpose(per_chip[:, n_c + n_dc:n_c + n_dc + n_fc].reshape(N_CHIPS, FFN_CONV, -1), (1, 0, 2)).reshape(FFN_CONV, 2 * D_FF)

    mod_cols = N_MOD * d // N_CHIPS
    ada_b_loc = lax.dynamic_slice(w["ada_b"], (0, chip * mod_cols), (1, mod_cols))
    mod_part = _mod_fwd(c_all, w["ada_w"][0], ada_b_loc)
    mod_all = _allgather8(mod_part, "ag_mod").reshape(n_dev, n_dev * bsz, mod_cols)[0::2]
    mod = jnp.transpose(lax.dynamic_slice(mod_all, (0, dev * bsz, 0), (N_CHIPS, bsz, mod_cols)), (1, 0, 2)).reshape(bsz, N_MOD * d)

    (w_in_all,) = _comm_call(_gather_plug([w["w_in"][0].astype(BF16)]), "gather_w_in")
    p = {"w_in": _to_padded(_assemble("w_in", w_in_all))}
    shards = {"late_mixer": [w[k][0].astype(BF16) for k in _LATE_MIXER], "ffn": [w[k][0].astype(BF16) for k in _FFN]}
    for k in ("norm_mix_pre", "norm_mix_post", "norm_ffn_pre", "norm_ffn_post", "dn_norm_w", "attn_sinks"):
        p[k] = w[k]
    p["dn_a_log"], p["dn_dt_bias"], p["rel_bias"] = w["dn_a_log"][0], w["dn_dt_bias"][0], w["rel_bias"]
    p["dn_conv_w"], p["ffn_conv_w"] = dn_conv_full, ffn_conv_full

    sq, dx, dmod, g = _device_step(x.reshape(t, d), loss_target.reshape(t, d), mod, p, bsz, shards)
    loss = lax.psum(0.5 * jnp.sum(sq), ("x", "y", "c"))

    g["dn_a_log"], g["dn_dt_bias"] = g["dn_a_log"].reshape(-1), g["dn_dt_bias"].reshape(-1)
    small_rows = 328
    small = _pack_rows([dmod] + [g[k] for k, _ in _SMALL], small_rows)
    small_all = _allgather8(small, "ag_small").reshape(n_dev, small_rows, LANES)
    n_dm = bsz * N_MOD * d
    dmod_all = small_all.reshape(n_dev, -1)[:, :n_dm].reshape(n_dev * bsz, N_MOD * d)
    tot = _sum_lead(small_all, "sum_small").reshape(-1)
    gs, off = {}, n_dm
    for k, n in _SMALL:
        gs[k] = tot[off:off + n]
        off += n
    grad = {}
    grad["ada_w"], grad["ada_b"] = _ada_grad(c_all, lax.dynamic_slice(dmod_all, (0, chip * mod_cols), (n_dev * bsz, mod_cols)), dmod_all)
    for k in ("norm_mix_pre", "norm_mix_post", "norm_ffn_pre", "norm_ffn_post", "dn_norm_w"):
        grad[k] = gs[k]
    grad["dn_a_log"] = gs["dn_a_log"][DN_HEADS:2 * DN_HEADS]
    grad["dn_dt_bias"] = gs["dn_dt_bias"][DN_HEADS:2 * DN_HEADS]
    grad["attn_sinks"] = gs["attn_sinks"].reshape(AQ_HEADS, LANES)[:, 0]
    grad["rel_bias"] = gs["rel_bias"].reshape(AQ_HEADS, LANES)[:, :REL_BUCKETS].T
    grad["dn_conv_w"] = lax.dynamic_slice(gs["dn_conv_w"].reshape(DN_CONV, 3 * DN), (0, chip * (3 * DN // N_CHIPS)), (DN_CONV, 3 * DN // N_CHIPS))
    grad["ffn_conv_w"] = lax.dynamic_slice(gs["ffn_conv_w"].reshape(FFN_CONV, 2 * D_FF), (0, chip * (2 * D_FF // N_CHIPS)), (FFN_CONV, 2 * D_FF // N_CHIPS))

    mine = [_sum_lead(g[k], "sum_" + k) for k in _BIG]
    theirs = _sibling_exchange(mine, "exchange_cores")

    out = {}
    for k, a, b in zip(_BIG, mine, theirs):
        if k == "w_in":
            lin = lambda z: jnp.transpose(z, (1, 0)).reshape(-1, LANES)
            res = _adamw(lin(w[k][0]), lin(m[k][0]), lin(v[k][0]), [lin(a), lin(b)], "adamw_" + k)
            out[k] = [jnp.transpose(r.reshape(a.shape[1], a.shape[0]), (1, 0)) for r in res]
        else:
            out[k] = _adamw(w[k][0], m[k][0], v[k][0], [a, b], "adamw_" + k)
    out["ada_w"] = _adamw(w["ada_w"][0], m["ada_w"][0], v["ada_w"][0], [grad["ada_w"]], "adamw_ada_w")
    small_names = [k for k in _W_NAMES if k not in _BIG and k != "ada_w"]
    offs, n_tot = {}, 0
    for k in small_names:
        offs[k] = n_tot
        n_tot += -(-w[k].size // LANES) * LANES
    pack = lambda dct: jnp.concatenate([_pad128(dct[k]) for k in small_names]).reshape(-1, LANES)
    rows_s = n_tot // LANES
    pad_rows = -(-rows_s // 8) * 8 - rows_s
    padr = lambda a: jnp.concatenate([a, jnp.zeros((pad_rows, LANES), F32)]) if pad_rows else a
    res_small = _adamw(padr(pack(w)), padr(pack(m)), padr(pack(v)), [padr(pack(grad))], "adamw_small")
    for k in small_names:
        out[k] = [r.reshape(-1)[offs[k]:offs[k] + w[k].size].reshape(w[k].shape) for r in res_small]
    for k in _BIG + ("ada_w",):
        out[k] = [r.reshape(w[k].shape) for r in out[k]]
    grads, deltas, new_m, new_v = ([out[k][i] for k in _W_NAMES] for i in range(4))
    return (loss, dx.reshape(bsz, s, d), *grads, *deltas, *new_m, *new_v)
```

```python
import functools
import math

import numpy as np
import jax
import jax.numpy as jnp
from jax import lax
from jax.experimental import pallas as pl
from jax.experimental.pallas import tpu as pltpu

F32 = jnp.float32
BF16 = jnp.bfloat16
HI = lax.Precision.HIGHEST
MESH = pl.DeviceIdType.MESH

D_MODEL = 1024
N_MOD = 6
AQ_HEADS, AKV_HEADS, A_HD, WINDOW = 8, 2, 64, 128
REL_BUCKETS, REL_MAX_DIST = 32, 128
DN_HEADS, DN_HD, DN_CONV, DN_CHUNK = 4, 128, 4, 64
D_FF, FFN_CONV = 2816, 3
RMS_EPS, L2_EPS, NEG_INF = 1e-6, 1e-6, -1e30
AQ, AKV, DN = AQ_HEADS * A_HD, AKV_HEADS * A_HD, DN_HEADS * DN_HD
IN_DIM = AQ + 2 * AKV + 3 * DN + DN + 2 * DN_HEADS + 2 * D_MODEL
C_DQKV, C_Q, C_DZ, C_GA, C_GD, C_K, C_V, C_BD = 0, 1536, 2048, 2560, 3584, 4608, 4736, 4864
IN_PAD = 4992
LANES = 128
N_CHIPS = 4

ADAM_LR, ADAM_B1, ADAM_B2, ADAM_EPS, ADAM_WD, ADAM_STEP = 0.001, 0.9, 0.999, 1e-08, 0.01, 10


def _pick(n, cap):
    best = None
    for t in range(LANES, cap + 1, LANES):
        if n % t == 0:
            best = t
    return best if best is not None else n


def _vspec(shape, index_map):
    return pl.BlockSpec(shape, index_map)


MM_VMEM_BUDGET = 40 * 2 ** 20
GRID_STEP_S = 0.35e-6
HBM_BYTES_PER_S = 3.0e12
MXU_FLOPS_PER_S = 9.0e14
MXU_DIM = 256


def _mm_tiles(m, n, k, mode, in_bytes, out_bytes, split=1):
    best = None
    for tm in [t for t in range(LANES, m + 1, LANES) if m % t == 0]:
        for tn in [t for t in range(LANES, n // split + 1, LANES) if (n // split) % t == 0]:
            a, b, o = k * tm * in_bytes, k * tn * in_bytes, tm * tn * out_bytes
            if 2 * (a + b + o) + (a if mode == "tn" else 0) > MM_VMEM_BUDGET:
                continue
            hbm_s = (m * k * in_bytes + (m // tm) * n * k * in_bytes + m * n * out_bytes) / HBM_BYTES_PER_S
            mxu_s = 2 * m * n * k / (MXU_FLOPS_PER_S * min(1.0, tm / MXU_DIM) * min(1.0, tn / MXU_DIM))
            cost = (m // tm) * (n // tn) * GRID_STEP_S + max(hbm_s, mxu_s)
            if best is None or cost < best[0]:
                best = (cost, tm, tn)
    return best[1], best[2]


def _mm(a, b, mode, out_dtype, name, plug=None, split=1):
    if mode == "nn":
        (m, k), n = a.shape, b.shape[1]
        dims = (((1,), (0,)), ((), ()))
    elif mode == "nt":
        (m, k), n = a.shape, b.shape[0]
        dims = (((1,), (1,)), ((), ()))
    else:
        (k, m), n = a.shape, b.shape[1]
        dims = (((0,), (0,)), ((), ()))
    tm, tn = _mm_tiles(m, n, k, mode, a.dtype.itemsize, jnp.dtype(out_dtype).itemsize, split)
    if mode == "tn":
        a_spec = _vspec((k, tm), lambda i, j: (0, i))
    else:
        a_spec = _vspec((tm, k), lambda i, j: (i, 0))
    if mode == "nt":
        b_spec = _vspec((tn, k), lambda i, j: (j, 0))
    else:
        b_spec = _vspec((k, tn), lambda i, j: (0, j))

    def body(a_ref, b_ref, o_ref):
        o_ref[...] = lax.dot_general(a_ref[...].astype(BF16), b_ref[...].astype(BF16), dims,
                                     preferred_element_type=F32).astype(out_dtype).reshape(o_ref.shape)

    grid = (m // tm, n // tn)
    if split == 1:
        out_spec, out_shape = _vspec((tm, tn), lambda i, j: (i, j)), (m, n)
    else:
        per = n // split // tn
        out_spec, out_shape = _vspec((1, tm, tn), lambda i, j: (j // per, i, j % per)), (split, m, n // split)
    (out,), extra = _plugged_call(body, plug, _grid_ends(grid), (a, b), name=name, grid=grid, in_specs=[a_spec, b_spec],
                                  out_specs=[out_spec], out_shape=[jax.ShapeDtypeStruct(out_shape, out_dtype)])
    return out if plug is None else (out, extra)


def _rms(x, w):
    return (x * lax.rsqrt(jnp.mean(x * x, axis=-1, keepdims=True) + RMS_EPS)) * w


def _pre_f(x, w, sc, sh):
    return _rms(x, w) * (1.0 + sc) + sh


def _post_f(y, w, g):
    return g * _rms(y, w)


def _tok_grid(t, bsz, ts):
    nt = t // bsz // ts
    return nt, (bsz, nt)


def _pre_fwd(x, w, sc, sh, name, ts=512):
    t, d = x.shape
    bsz = sc.shape[0]
    nt, grid = _tok_grid(t, bsz, ts)
    row = _vspec((ts, d), lambda b, i: (b * nt + i, 0))
    vec = _vspec((1, d), lambda b, i: (0, 0))
    bvec = _vspec((1, 1, d), lambda b, i: (b, 0, 0))

    def body(x_ref, w_ref, sc_ref, sh_ref, u_ref):
        u_ref[...] = _pre_f(x_ref[...], w_ref[...], sc_ref[0], sh_ref[0]).astype(BF16)

    return pl.pallas_call(body, name=name, grid=grid, in_specs=[row, vec, bvec, bvec], out_specs=row,
                          out_shape=jax.ShapeDtypeStruct((t, d), BF16))(x, w, sc, sh)


def _pre_bwd(x, w, sc, sh, du, dres, name, ts=512):
    t, d = x.shape
    bsz = sc.shape[0]
    nt, grid = _tok_grid(t, bsz, ts)
    row = _vspec((ts, d), lambda b, i: (b * nt + i, 0))
    vec = _vspec((1, d), lambda b, i: (0, 0))
    bvec = _vspec((1, 1, d), lambda b, i: (b, 0, 0))

    def body(x_ref, w_ref, sc_ref, sh_ref, du_ref, dres_ref, dx_ref, dw_ref, dsc_ref, dsh_ref):
        b, i = pl.program_id(0), pl.program_id(1)
        _, vjp = jax.vjp(_pre_f, x_ref[...], w_ref[...], sc_ref[0], sh_ref[0])
        dx, dw, dsc, dsh = vjp(du_ref[...])
        dx_ref[...] = dres_ref[...] + dx

        @pl.when((b == 0) & (i == 0))
        def _():
            dw_ref[...] = jnp.zeros_like(dw_ref)

        @pl.when(i == 0)
        def _():
            dsc_ref[...] = jnp.zeros_like(dsc_ref)
            dsh_ref[...] = jnp.zeros_like(dsh_ref)

        dw_ref[...] += dw
        dsc_ref[0] += dsc
        dsh_ref[0] += dsh

    return pl.pallas_call(
        body, name=name, grid=grid, in_specs=[row, vec, bvec, bvec, row, row], out_specs=[row, vec, bvec, bvec],
        out_shape=[jax.ShapeDtypeStruct((t, d), F32), jax.ShapeDtypeStruct((1, d), F32),
                   jax.ShapeDtypeStruct((bsz, 1, d), F32), jax.ShapeDtypeStruct((bsz, 1, d), F32)],
    )(x, w, sc, sh, du, dres)


def _post_fwd(res, y, w, g, name, ts=512):
    t, d = y.shape
    bsz = g.shape[0]
    nt, grid = _tok_grid(t, bsz, ts)
    row = _vspec((ts, d), lambda b, i: (b * nt + i, 0))
    vec = _vspec((1, d), lambda b, i: (0, 0))
    bvec = _vspec((1, 1, d), lambda b, i: (b, 0, 0))

    def body(res_ref, y_ref, w_ref, g_ref, h_ref):
        h_ref[...] = res_ref[...] + _post_f(y_ref[...], w_ref[...], g_ref[0])

    return pl.pallas_call(body, name=name, grid=grid, in_specs=[row, row, vec, bvec], out_specs=row,
                          out_shape=jax.ShapeDtypeStruct((t, d), F32))(res, y, w, g)


def _post_loss(res, y, w, g, tgt, name, ts=512):
    t, d = y.shape
    bsz = g.shape[0]
    nt, grid = _tok_grid(t, bsz, ts)
    row = _vspec((ts, d), lambda b, i: (b * nt + i, 0))
    vec = _vspec((1, d), lambda b, i: (0, 0))
    bvec = _vspec((1, 1, d), lambda b, i: (b, 0, 0))
    acc = _vspec((1, d), lambda b, i: (0, 0))

    def body(res_ref, y_ref, w_ref, g_ref, tgt_ref, dh_ref, sq_ref):
        b, i = pl.program_id(0), pl.program_id(1)
        e = res_ref[...] + _post_f(y_ref[...], w_ref[...], g_ref[0]) - tgt_ref[...]
        dh_ref[...] = e * (1.0 / d)

        @pl.when((b == 0) & (i == 0))
        def _():
            sq_ref[...] = jnp.zeros_like(sq_ref)

        sq_ref[...] += jnp.sum(e * e, axis=0, keepdims=True) * (1.0 / d)

    return pl.pallas_call(
        body, name=name, grid=grid, in_specs=[row, row, vec, bvec, row], out_specs=[row, acc],
        out_shape=[jax.ShapeDtypeStruct((t, d), F32), jax.ShapeDtypeStruct((1, d), F32)],
    )(res, y, w, g, tgt)


def _post_bwd(dh, y, w, g, name, ts=512):
    t, d = y.shape
    bsz = g.shape[0]
    nt, grid = _tok_grid(t, bsz, ts)
    row = _vspec((ts, d), lambda b, i: (b * nt + i, 0))
    vec = _vspec((1, d), lambda b, i: (0, 0))
    bvec = _vspec((1, 1, d), lambda b, i: (b, 0, 0))

    def body(dh_ref, y_ref, w_ref, g_ref, dy_ref, dw_ref, dg_ref):
        b, i = pl.program_id(0), pl.program_id(1)
        _, vjp = jax.vjp(_post_f, y_ref[...], w_ref[...], g_ref[0])
        dy, dw, dg = vjp(dh_ref[...])
        dy_ref[...] = dy.astype(BF16)

        @pl.when((b == 0) & (i == 0))
        def _():
            dw_ref[...] = jnp.zeros_like(dw_ref)

        @pl.when(i == 0)
        def _():
            dg_ref[...] = jnp.zeros_like(dg_ref)

        dw_ref[...] += dw
        dg_ref[0] += dg

    return pl.pallas_call(
        body, name=name, grid=grid, in_specs=[row, row, vec, bvec], out_specs=[row, vec, bvec],
        out_shape=[jax.ShapeDtypeStruct((t, d), BF16), jax.ShapeDtypeStruct((1, d), F32),
                   jax.ShapeDtypeStruct((bsz, 1, d), F32)],
    )(dh, y, w, g)


def _merge_f(ga, gd, ya, yd):
    return jax.nn.sigmoid(ga) * ya + jax.nn.sigmoid(gd) * yd


_MW = 512


def _merge_fwd(proj, ya, yd, ts=512):
    t, d = ya.shape
    blk = _vspec((ts, _MW), lambda i, j: (i, j))
    ga = _vspec((ts, _MW), lambda i, j: (i, C_GA // _MW + j))
    gd = _vspec((ts, _MW), lambda i, j: (i, C_GD // _MW + j))

    def body(ga_ref, gd_ref, ya_ref, yd_ref, o_ref):
        o_ref[...] = _merge_f(ga_ref[...], gd_ref[...], ya_ref[...], yd_ref[...]).astype(BF16)

    return pl.pallas_call(body, name="merge_fwd", grid=(t // ts, d // _MW), in_specs=[ga, gd, blk, blk], out_specs=blk,
                          out_shape=jax.ShapeDtypeStruct((t, d), BF16))(proj, proj, ya, yd)


def _merge_bwd(proj, ya, yd, dm, ts=512):
    t, d = ya.shape
    blk = _vspec((ts, _MW), lambda i, j: (i, j))
    ga = _vspec((ts, _MW), lambda i, j: (i, C_GA // _MW + j))
    gd = _vspec((ts, _MW), lambda i, j: (i, C_GD // _MW + j))

    def body(ga_ref, gd_ref, ya_ref, yd_ref, dm_ref, dga_ref, dgd_ref, dya_ref, dyd_ref):
        _, vjp = jax.vjp(_merge_f, ga_ref[...], gd_ref[...], ya_ref[...], yd_ref[...])
        dga, dgd, dya, dyd = vjp(dm_ref[...])
        dga_ref[...] = dga.astype(BF16)
        dgd_ref[...] = dgd.astype(BF16)
        dya_ref[...] = dya.astype(BF16)
        dyd_ref[...] = dyd.astype(BF16)

    o = jax.ShapeDtypeStruct((t, d), BF16)
    return pl.pallas_call(body, name="merge_bwd", grid=(t // ts, d // _MW), in_specs=[ga, gd, blk, blk, blk],
                          out_specs=[blk] * 4, out_shape=[o] * 4)(proj, proj, ya, yd, dm)


def _shift_down(x, s):
    if s == 0:
        return x
    r = lax.broadcasted_iota(jnp.int32, x.shape, 0)
    return jnp.where(r >= s, pltpu.roll(x, s, 0), 0.0)


def _shift_up(x, s):
    if s == 0:
        return x
    n = x.shape[0]
    r = lax.broadcasted_iota(jnp.int32, x.shape, 0)
    return jnp.where(r < n - s, pltpu.roll(x, n - s, 0), 0.0)


def _conv_fwd(x, w, k):
    out = None
    for j in range(k):
        term = w[j:j + 1, :] * _shift_down(x, k - 1 - j)
        out = term if out is None else out + term
    return out


def _conv_bwd(x, w, dc, k):
    dx = None
    dws = []
    for j in range(k):
        s = k - 1 - j
        term = w[j:j + 1, :] * _shift_up(dc, s)
        dx = term if dx is None else dx + term
        dws.append(jnp.sum(dc * _shift_down(x, s), axis=0, keepdims=True))
    return dx, jnp.concatenate(dws, axis=0)


def _geglu_f(gate, val):
    return jax.nn.gelu(gate, approximate=True) * val


_FW = 256


def _ffn_act_fwd(up, conv_w, bsz):
    t = up.shape[0]
    s = t // bsz
    nj = D_FF // _FW
    xg = _vspec((s, _FW), lambda b, j: (b, j))
    xv = _vspec((s, _FW), lambda b, j: (b, nj + j))
    wg = _vspec((FFN_CONV, _FW), lambda b, j: (0, j))
    wv = _vspec((FFN_CONV, _FW), lambda b, j: (0, nj + j))

    def body(xg_ref, xv_ref, wg_ref, wv_ref, o_ref):
        gate = _conv_fwd(xg_ref[...], wg_ref[...], FFN_CONV)
        val = _conv_fwd(xv_ref[...], wv_ref[...], FFN_CONV)
        o_ref[...] = _geglu_f(gate, val).astype(BF16)

    return pl.pallas_call(body, name="ffn_act_fwd", grid=(bsz, nj), in_specs=[xg, xv, wg, wv],
                          out_specs=_vspec((s, _FW), lambda b, j: (b, j)),
                          out_shape=jax.ShapeDtypeStruct((t, D_FF), BF16))(up, up, conv_w, conv_w)


def _ffn_act_bwd(up, conv_w, dact, bsz):
    t = up.shape[0]
    s = t // bsz
    nj = D_FF // _FW
    xg = _vspec((s, _FW), lambda j, b: (b, j))
    xv = _vspec((s, _FW), lambda j, b: (b, nj + j))
    wg = _vspec((FFN_CONV, _FW), lambda j, b: (0, j))
    wv = _vspec((FFN_CONV, _FW), lambda j, b: (0, nj + j))
    da = _vspec((s, _FW), lambda j, b: (b, j))
    dwo = _vspec((FFN_CONV, _FW), lambda j, b: (0, j))

    def body(xg_ref, xv_ref, wg_ref, wv_ref, da_ref, dxg_ref, dxv_ref, dwg_ref, dwv_ref):
        b = pl.program_id(1)
        xg_, xv_, wg_, wv_ = xg_ref[...], xv_ref[...], wg_ref[...], wv_ref[...]
        gate = _conv_fwd(xg_, wg_, FFN_CONV)
        val = _conv_fwd(xv_, wv_, FFN_CONV)
        _, vjp = jax.vjp(_geglu_f, gate, val)
        dgate, dval = vjp(da_ref[...])
        dxg, dwg = _conv_bwd(xg_, wg_, dgate, FFN_CONV)
        dxv, dwv = _conv_bwd(xv_, wv_, dval, FFN_CONV)
        dxg_ref[...] = dxg.astype(BF16)
        dxv_ref[...] = dxv.astype(BF16)

        @pl.when(b == 0)
        def _():
            dwg_ref[...] = jnp.zeros_like(dwg_ref)
            dwv_ref[...] = jnp.zeros_like(dwv_ref)

        dwg_ref[...] += dwg
        dwv_ref[...] += dwv

    return pl.pallas_call(
        body, name="ffn_act_bwd", grid=(nj, bsz), in_specs=[xg, xv, wg, wv, da], out_specs=[da, da, dwo, dwo],
        out_shape=[jax.ShapeDtypeStruct((t, D_FF), BF16), jax.ShapeDtypeStruct((t, D_FF), BF16),
                   jax.ShapeDtypeStruct((FFN_CONV, D_FF), F32), jax.ShapeDtypeStruct((FFN_CONV, D_FF), F32)],
    )(up, up, conv_w, conv_w, dact)


def _bucket_table():
    qi = np.arange(WINDOW)[:, None]
    kj = np.arange(2 * WINDOW)[None, :]
    dist = WINDOW + qi - kj
    dc = np.maximum(dist, 0)
    max_exact = REL_BUCKETS // 2
    scaled = np.log(np.maximum(dc, 1).astype(np.float32) / np.float32(max_exact)) / np.float32(math.log(REL_MAX_DIST / max_exact))
    large = max_exact + (scaled.astype(np.float32) * np.float32(REL_BUCKETS - max_exact)).astype(np.int32)
    large = np.minimum(large, REL_BUCKETS - 1)
    bucket = np.where(dc < max_exact, dc, large).astype(np.int32)
    in_band = ((dist >= 0) & (dist < WINDOW)).astype(np.int32)
    return bucket, in_band


def _bias_build(rel_bias):
    bucket, _ = _bucket_table()

    def body(rb_ref, idx_ref, o_ref):
        h = pl.program_id(0)
        idx = idx_ref[...]
        acc = jnp.zeros(idx.shape, F32)
        for r in range(REL_BUCKETS):
            acc = jnp.where(idx == r, rb_ref[r, h], acc)
        o_ref[0] = acc

    return pl.pallas_call(
        body, name="bias_build", grid=(AQ_HEADS,),
        in_specs=[pl.BlockSpec(memory_space=pltpu.SMEM), _vspec((WINDOW, 2 * WINDOW), lambda h: (0, 0))],
        out_specs=_vspec((1, WINDOW, 2 * WINDOW), lambda h: (h, 0, 0)),
        out_shape=jax.ShapeDtypeStruct((AQ_HEADS, WINDOW, 2 * WINDOW), F32),
    )(rel_bias, jnp.asarray(bucket))


def _bias_grad(dbias):
    bucket, _ = _bucket_table()

    def body(db_ref, idx_ref, o_ref):
        idx = idx_ref[...]
        db = db_ref[0]
        lane = lax.broadcasted_iota(jnp.int32, (1, LANES), 1)
        acc = jnp.zeros((1, LANES), F32)
        for r in range(REL_BUCKETS):
            s = jnp.sum(jnp.sum(jnp.where(idx == r, db, 0.0), axis=1, keepdims=True), axis=0, keepdims=True)
            acc = jnp.where(lane == r, s, acc)
        o_ref[0] = acc

    return pl.pallas_call(
        body, name="bias_grad", grid=(AQ_HEADS,),
        in_specs=[_vspec((1, WINDOW, 2 * WINDOW), lambda h: (h, 0, 0)), _vspec((WINDOW, 2 * WINDOW), lambda h: (0, 0))],
        out_specs=_vspec((1, 1, LANES), lambda h: (h, 0, 0)),
        out_shape=jax.ShapeDtypeStruct((AQ_HEADS, 1, LANES), F32),
    )(dbias, jnp.asarray(bucket))


def _attn_mask(n):
    qi = lax.broadcasted_iota(jnp.int32, (WINDOW, 2 * WINDOW), 0)
    kj = lax.broadcasted_iota(jnp.int32, (WINDOW, 2 * WINDOW), 1)
    dist = WINDOW + qi - kj
    band = (dist >= 0) & (dist < WINDOW)
    return band & ((kj >= WINDOW) | (n > 0))


def _attn_probs(qk, bias, sink, mask):
    s = jnp.where(mask, qk * (A_HD ** -0.5) + bias, NEG_INF)
    m = jnp.maximum(jnp.max(s, axis=-1, keepdims=True), sink)
    p = jnp.exp(s - m)
    es = jnp.exp(sink - m)
    inv = 1.0 / (jnp.sum(p, axis=-1, keepdims=True) + es)
    return p * inv, es * inv


def _attn_fwd(proj, bias, sinks, bsz):
    t = proj.shape[0]
    s = t // bsz
    nb = s // WINDOW
    grp = AQ_HEADS // AKV_HEADS

    def body(q_ref, k_ref, v_ref, bias_ref, sink_ref, y_ref, kp_ref, vp_ref):
        kp_ref[0:WINDOW, :] = jnp.zeros((WINDOW, LANES), BF16)
        vp_ref[0:WINDOW, :] = jnp.zeros((WINDOW, LANES), BF16)
        kp_ref[WINDOW:, :] = k_ref[...].astype(BF16)
        vp_ref[WINDOW:, :] = v_ref[...].astype(BF16)

        def blk(n, carry):
            r0 = pl.multiple_of(n * WINDOW, WINDOW)
            mask = _attn_mask(n)
            kband = kp_ref[pl.ds(r0, 2 * WINDOW), :]
            vband = vp_ref[pl.ds(r0, 2 * WINDOW), :]
            qb = q_ref[pl.ds(r0, WINDOW), :].astype(BF16)
            heads = range(AQ_HEADS)
            hsl = lambda h: slice(h * A_HD, (h + 1) * A_HD)
            kbs = [kband[:, hsl(kv)] for kv in range(AKV_HEADS)]
            vbs = [vband[:, hsl(kv)] for kv in range(AKV_HEADS)]
            qks = [lax.dot_general(qb[:, hsl(h)], kbs[h // grp], _NT, preferred_element_type=F32) for h in heads]
            probs = [_attn_probs(qks[h], bias_ref[h], sink_ref[0, h], mask)[0] for h in heads]
            outs = [jnp.dot(probs[h].astype(BF16), vbs[h // grp], preferred_element_type=F32) for h in heads]
            y_ref[pl.ds(r0, WINDOW), :] = jnp.concatenate(outs, axis=1).astype(BF16)
            return carry

        lax.fori_loop(0, nb, blk, 0)

    return pl.pallas_call(
        body, name="attn_fwd", grid=(bsz,),
        in_specs=[_vspec((s, AQ), lambda b: (b, C_Q // AQ)), _vspec((s, AKV), lambda b: (b, C_K // AKV)),
                  _vspec((s, AKV), lambda b: (b, C_V // AKV)),
                  _vspec((AQ_HEADS, WINDOW, 2 * WINDOW), lambda b: (0, 0, 0)), pl.BlockSpec(memory_space=pltpu.SMEM)],
        out_specs=_vspec((s, AQ), lambda b: (b, 0)), out_shape=jax.ShapeDtypeStruct((t, AQ), BF16),
        scratch_shapes=[pltpu.VMEM((s + WINDOW, LANES), BF16), pltpu.VMEM((s + WINDOW, LANES), BF16)],
    )(proj, proj, proj, bias, sinks)


def _attn_bwd(proj, bias, sinks, dy, bsz, plug=None):
    t = proj.shape[0]
    s = t // bsz
    nb = s // WINDOW
    grp = AQ_HEADS // AKV_HEADS
    scale = A_HD ** -0.5

    def body(q_ref, k_ref, v_ref, bias_ref, sink_ref, dy_ref, dq_ref, dk_ref, dv_ref, dbias_ref, dsink_ref,
             kp_ref, vp_ref, dkp_ref, dvp_ref):
        b = pl.program_id(0)
        kp_ref[0:WINDOW, :] = jnp.zeros((WINDOW, LANES), BF16)
        vp_ref[0:WINDOW, :] = jnp.zeros((WINDOW, LANES), BF16)
        kp_ref[WINDOW:, :] = k_ref[...].astype(BF16)
        vp_ref[WINDOW:, :] = v_ref[...].astype(BF16)
        dkp_ref[...] = jnp.zeros_like(dkp_ref)
        dvp_ref[...] = jnp.zeros_like(dvp_ref)

        @pl.when(b == 0)
        def _():
            dbias_ref[...] = jnp.zeros_like(dbias_ref)
            dsink_ref[...] = jnp.zeros_like(dsink_ref)

        def blk(n, carry):
            r0 = pl.multiple_of(n * WINDOW, WINDOW)
            mask = _attn_mask(n)
            kband = kp_ref[pl.ds(r0, 2 * WINDOW), :]
            vband = vp_ref[pl.ds(r0, 2 * WINDOW), :]
            qb = q_ref[pl.ds(r0, WINDOW), :].astype(BF16)
            dyb = dy_ref[pl.ds(r0, WINDOW), :].astype(BF16)
            heads = range(AQ_HEADS)
            hsl = lambda h: slice(h * A_HD, (h + 1) * A_HD)
            kbs = [kband[:, hsl(kv)] for kv in range(AKV_HEADS)]
            vbs = [vband[:, hsl(kv)] for kv in range(AKV_HEADS)]
            qhs = [qb[:, hsl(h)] for h in heads]
            dyhs = [dyb[:, hsl(h)] for h in heads]
            qks = [lax.dot_general(qhs[h], kbs[h // grp], _NT, preferred_element_type=F32) for h in heads]
            dprobs = [lax.dot_general(dyhs[h], vbs[h // grp], _NT, preferred_element_type=F32) for h in heads]
            pbs, dsbs = [], []
            for h in heads:
                probs, psink = _attn_probs(qks[h], bias_ref[h], sink_ref[0, h], mask)
                rowdot = jnp.sum(probs * dprobs[h], axis=-1, keepdims=True)
                ds = probs * (dprobs[h] - rowdot)
                dbias_ref[h] += ds
                dsink_ref[h] += jnp.sum(-psink * rowdot, axis=0, keepdims=True) + jnp.zeros((1, LANES), F32)
                pbs.append(probs.astype(BF16))
                dsbs.append(ds.astype(BF16))
            dvhs = [lax.dot_general(pbs[h], dyhs[h], _TN, preferred_element_type=F32) for h in heads]
            dqs = [jnp.dot(dsbs[h], kbs[h // grp], preferred_element_type=F32) * scale for h in heads]
            dkhs = [lax.dot_general(dsbs[h], qhs[h], _TN, preferred_element_type=F32) * scale for h in heads]
            dks = [sum(dkhs[kv * grp + 1:(kv + 1) * grp], dkhs[kv * grp]) for kv in range(AKV_HEADS)]
            dvs = [sum(dvhs[kv * grp + 1:(kv + 1) * grp], dvhs[kv * grp]) for kv in range(AKV_HEADS)]
            dq_ref[pl.ds(r0, WINDOW), :] = jnp.concatenate(dqs, axis=1).astype(BF16)
            dkp_ref[pl.ds(r0, 2 * WINDOW), :] += jnp.concatenate(dks, axis=1)
            dvp_ref[pl.ds(r0, 2 * WINDOW), :] += jnp.concatenate(dvs, axis=1)
            return carry

        lax.fori_loop(0, nb, blk, 0)
        dk_ref[...] = dkp_ref[WINDOW:, :].astype(BF16)
        dv_ref[...] = dvp_ref[WINDOW:, :].astype(BF16)

    kvs = jax.ShapeDtypeStruct((t, AKV), BF16)
    outs, extra = _plugged_call(
        body, plug, _grid_ends((bsz,)), (proj, proj, proj, bias, sinks, dy), name="attn_bwd", grid=(bsz,),
        in_specs=[_vspec((s, AQ), lambda b: (b, C_Q // AQ)), _vspec((s, AKV), lambda b: (b, C_K // AKV)),
                  _vspec((s, AKV), lambda b: (b, C_V // AKV)),
                  _vspec((AQ_HEADS, WINDOW, 2 * WINDOW), lambda b: (0, 0, 0)), pl.BlockSpec(memory_space=pltpu.SMEM),
                  _vspec((s, AQ), lambda b: (b, 0))],
        out_specs=[_vspec((s, AQ), lambda b: (b, 0)), _vspec((s, AKV), lambda b: (b, 0)), _vspec((s, AKV), lambda b: (b, 0)),
                   _vspec((AQ_HEADS, WINDOW, 2 * WINDOW), lambda b: (0, 0, 0)), _vspec((AQ_HEADS, 1, LANES), lambda b: (0, 0, 0))],
        out_shape=[jax.ShapeDtypeStruct((t, AQ), BF16), kvs, kvs,
                   jax.ShapeDtypeStruct((AQ_HEADS, WINDOW, 2 * WINDOW), F32), jax.ShapeDtypeStruct((AQ_HEADS, 1, LANES), F32)],
        scratch_shapes=[pltpu.VMEM((s + WINDOW, LANES), BF16), pltpu.VMEM((s + WINDOW, LANES), BF16),
                        pltpu.VMEM((s + WINDOW, LANES), F32), pltpu.VMEM((s + WINDOW, LANES), F32)])
    return (*outs, extra)


def _dn_act_f(c, is_qk):
    a = jax.nn.silu(c)
    outs = []
    for h in range(DN_HEADS):
        ah = a[:, h * DN_HD:(h + 1) * DN_HD]
        nh = ah * lax.rsqrt(jnp.sum(ah * ah, axis=-1, keepdims=True) + L2_EPS)
        outs.append(jnp.where(is_qk, nh, ah))
    return jnp.concatenate(outs, axis=1)


def _dn_prep_fwd(proj, conv_w, bsz):
    t = proj.shape[0]
    s = t // bsz
    blk = _vspec((s, DN), lambda b, j: (b, j))
    wsp = _vspec((DN_CONV, DN), lambda b, j: (0, j))

    def body(x_ref, w_ref, o_ref):
        j = pl.program_id(1)
        o_ref[...] = _dn_act_f(_conv_fwd(x_ref[...], w_ref[...], DN_CONV), j < 2)

    return pl.pallas_call(body, name="dn_prep_fwd", grid=(bsz, 3), in_specs=[blk, wsp], out_specs=blk,
                          out_shape=jax.ShapeDtypeStruct((t, 3 * DN), F32))(proj, conv_w)


def _dn_prep_bwd(proj, conv_w, dqkvn, bsz):
    t = proj.shape[0]
    s = t // bsz
    blk = _vspec((s, DN), lambda j, b: (b, j))
    wsp = _vspec((DN_CONV, DN), lambda j, b: (0, j))

    def body(x_ref, w_ref, d_ref, dx_ref, dw_ref):
        j, b = pl.program_id(0), pl.program_id(1)
        x, w = x_ref[...], w_ref[...]
        c = _conv_fwd(x, w, DN_CONV)
        _, vjp = jax.vjp(lambda cc: _dn_act_f(cc, j < 2), c)
        (dc,) = vjp(d_ref[0])
        dx, dw = _conv_bwd(x, w, dc, DN_CONV)
        dx_ref[...] = dx.astype(BF16)

        @pl.when(b == 0)
        def _():
            dw_ref[...] = jnp.zeros_like(dw_ref)

        dw_ref[...] += dw

    return pl.pallas_call(
        body, name="dn_prep_bwd", grid=(3, bsz),
        in_specs=[blk, wsp, _vspec((1, s, DN), lambda j, b: (j, b, 0))], out_specs=[blk, wsp],
        out_shape=[jax.ShapeDtypeStruct((t, 3 * DN), BF16), jax.ShapeDtypeStruct((DN_CONV, 3 * DN), F32)],
    )(proj, conv_w, dqkvn)


def _bg_f(x, alog, dt):
    lane = lax.broadcasted_iota(jnp.int32, x.shape, 1)
    beta = jax.nn.sigmoid(x)
    g = -jnp.exp(alog) * jax.nn.softplus(x + dt)
    return jnp.where(lane < DN_HEADS, beta, jnp.where(lane < 2 * DN_HEADS, g, 0.0))


def _bg_fwd(proj, alog, dt, bsz):
    t = proj.shape[0]
    s = t // bsz
    vec = _vspec((1, LANES), lambda b: (0, 0))

    def body(x_ref, a_ref, d_ref, o_ref):
        o_ref[...] = _bg_f(x_ref[...], a_ref[...], d_ref[...])

    return pl.pallas_call(body, name="bg_fwd", grid=(bsz,), in_specs=[_vspec((s, LANES), lambda b: (b, C_BD // LANES)), vec, vec],
                          out_specs=_vspec((s, LANES), lambda b: (b, 0)), out_shape=jax.ShapeDtypeStruct((t, LANES), F32))(proj, alog, dt)


def _bg_bwd(proj, alog, dt, dbg4, bsz):
    t = proj.shape[0]
    s = t // bsz
    vec = _vspec((1, LANES), lambda b: (0, 0))

    def body(x_ref, a_ref, d_ref, g4_ref, dx_ref, da_ref, dd_ref):
        b = pl.program_id(0)
        lane = lax.broadcasted_iota(jnp.int32, (s, LANES), 1)
        dbg = jnp.zeros((s, LANES), F32)
        for h in range(DN_HEADS):
            gh = g4_ref[:, h * DN_HD:(h + 1) * DN_HD]
            dbg = jnp.where(lane == h, gh[:, 0:1], dbg)
            dbg = jnp.where(lane == DN_HEADS + h, gh[:, 1:2], dbg)
        _, vjp = jax.vjp(_bg_f, x_ref[...], a_ref[...], d_ref[...])
        dx, da, dd = vjp(dbg)
        dx_ref[...] = dx.astype(BF16)

        @pl.when(b == 0)
        def _():
            da_ref[...] = jnp.zeros_like(da_ref)
            dd_ref[...] = jnp.zeros_like(dd_ref)

        da_ref[...] += da
        dd_ref[...] += dd

    return pl.pallas_call(
        body, name="bg_bwd", grid=(bsz,),
        in_specs=[_vspec((s, LANES), lambda b: (b, C_BD // LANES)), vec, vec, _vspec((s, DN), lambda b: (b, 0))],
        out_specs=[_vspec((s, LANES), lambda b: (b, 0)), vec, vec],
        out_shape=[jax.ShapeDtypeStruct((t, LANES), BF16), jax.ShapeDtypeStruct((1, LANES), F32), jax.ShapeDtypeStruct((1, LANES), F32)],
    )(proj, alog, dt, dbg4)


def _dn_out_f(o, z, w):
    outs = []
    for h in range(DN_HEADS):
        sl = slice(h * DN_HD, (h + 1) * DN_HD)
        outs.append(_rms(o[:, sl], w) * jax.nn.silu(z[:, sl]))
    return jnp.concatenate(outs, axis=1)


def _dn_out_fwd(o, proj, w, ts=512):
    t = o.shape[0]
    blk = _vspec((ts, DN), lambda i: (i, 0))
    zsp = _vspec((ts, DN), lambda i: (i, C_DZ // DN))
    vec = _vspec((1, DN_HD), lambda i: (0, 0))

    def body(o_ref, z_ref, w_ref, y_ref):
        y_ref[...] = _dn_out_f(o_ref[...], z_ref[...], w_ref[...]).astype(BF16)

    return pl.pallas_call(body, name="dn_out_fwd", grid=(t // ts,), in_specs=[blk, zsp, vec], out_specs=blk,
                          out_shape=jax.ShapeDtypeStruct((t, DN), BF16))(o, proj, w)


def _dn_out_bwd(o, proj, w, dy, ts=512):
    t = o.shape[0]
    blk = _vspec((ts, DN), lambda i: (i, 0))
    zsp = _vspec((ts, DN), lambda i: (i, C_DZ // DN))
    vec = _vspec((1, DN_HD), lambda i: (0, 0))

    def body(o_ref, z_ref, w_ref, dy_ref, do_ref, dz_ref, dw_ref):
        i = pl.program_id(0)
        _, vjp = jax.vjp(_dn_out_f, o_ref[...], z_ref[...], w_ref[...])
        do, dz, dw = vjp(dy_ref[...])
        do_ref[...] = do
        dz_ref[...] = dz.astype(BF16)

        @pl.when(i == 0)
        def _():
            dw_ref[...] = jnp.zeros_like(dw_ref)

        dw_ref[...] += dw

    return pl.pallas_call(
        body, name="dn_out_bwd", grid=(t // ts,), in_specs=[blk, zsp, vec, blk], out_specs=[blk, blk, vec],
        out_shape=[jax.ShapeDtypeStruct((t, DN), F32), jax.ShapeDtypeStruct((t, DN), BF16), jax.ShapeDtypeStruct((1, DN_HD), F32)],
    )(o, proj, w, dy)


_C = DN_CHUNK


def _dot(a, b, dims):
    return lax.dot_general(a.astype(BF16), b.astype(BF16), dims, preferred_element_type=F32)


_NN = (((1,), (0,)), ((), ()))
_NT = (((1,), (1,)), ((), ()))
_TN = (((0,), (0,)), ((), ()))


_SUB = 8


def _tri_inverse(lt):
    ri = lax.broadcasted_iota(jnp.int32, (_SUB, _C), 0)
    ci = lax.broadcasted_iota(jnp.int32, (_SUB, _C), 1)
    blocks = [jnp.where(ci == ri + _SUB * b, 1.0, 0.0).astype(F32) for b in range(_C // _SUB)]
    for i in range(1, _C):
        acc = None
        for b in range((i + _SUB - 1) // _SUB):
            term = lt[_SUB * b:_SUB * (b + 1), i:i + 1] * blocks[b]
            acc = term if acc is None else acc + term
        row = jnp.sum(acc, axis=0, keepdims=True)
        bi, r = divmod(i, _SUB)
        blocks[bi] = jnp.where(ri == r, blocks[bi] - row, blocks[bi])
    return jnp.concatenate(blocks, axis=0)


_SEG = 512
_HEADS = tuple(range(DN_HEADS))


def _hsl(hh):
    return slice(hh * DN_HD, (hh + 1) * DN_HD)


def _chunk_specs(nseg, reverse):
    seg = (lambda i: nseg - 1 - i) if reverse else (lambda i: i)
    ncs = _SEG // _C
    col = lambda off: _vspec((_SEG, DN), lambda b, i: (b * nseg + seg(i), off))
    return (col, _vspec((_SEG, LANES), lambda b, i: (b * nseg + seg(i), 0)),
            _vspec((1, DN_HEADS, ncs, _C), lambda b, i: (b, 0, seg(i), 0)),
            _vspec((1, DN_HEADS, ncs, DN_HD, DN_HD), lambda b, i: (b, 0, seg(i), 0, 0)))


def _chunk_pre(q_ref, k_ref, v_ref, bg_ref, gr_ref, c, hh):
    r0 = pl.multiple_of(c * _C, _C)
    ri = lax.broadcasted_iota(jnp.int32, (_C, _C), 0)
    ci = lax.broadcasted_iota(jnp.int32, (_C, _C), 1)
    q = q_ref[pl.ds(r0, _C), _hsl(hh)] * (DN_HD ** -0.5)
    k = k_ref[pl.ds(r0, _C), _hsl(hh)]
    v = v_ref[pl.ds(r0, _C), _hsl(hh)]
    bgc = bg_ref[pl.ds(r0, _C), :]
    beta = bgc[:, hh:hh + 1]
    g_col = bgc[:, DN_HEADS + hh:DN_HEADS + hh + 1]
    g_row = gr_ref[0, hh, pl.ds(c, 1), :]
    gc_col = jnp.sum(jnp.where(ri >= ci, g_row, 0.0), axis=1, keepdims=True)
    gc_row = jnp.sum(jnp.where(ri <= ci, g_col, 0.0), axis=0, keepdims=True)
    gc_last = jnp.sum(g_col, axis=0, keepdims=True)
    diff = gc_col - gc_row
    decay = jnp.where(ri >= ci, jnp.exp(jnp.where(ri >= ci, diff, 0.0)), 0.0)
    diff_t = gc_row - gc_col
    decay_t = jnp.where(ri <= ci, jnp.exp(jnp.where(ri <= ci, diff_t, 0.0)), 0.0)
    eg = jnp.exp(gc_col)
    et = jnp.exp(gc_last - gc_col)
    gl = jnp.exp(gc_last)
    kb = k * beta
    vb = v * beta
    return dict(r0=r0, q=q, k=k, v=v, beta=beta, decay=decay, decay_t=decay_t, eg=eg, et=et, gl=gl, kb=kb, vb=vb, ri=ri, ci=ci)


def _chunk_solve(ms, with_transposes):
    for m in ms:
        m["kk_t"] = _dot(m["k"], m["kb"], _NT)
        m["qk"] = _dot(m["q"], m["k"], _NT)
        if with_transposes:
            m["kk"] = _dot(m["kb"], m["k"], _NT)
            m["qk_t"] = _dot(m["k"], m["q"], _NT)
    for m in ms:
        m["tm"] = _tri_inverse(jnp.where(m["ri"] < m["ci"], m["kk_t"] * m["decay_t"], 0.0))
    for m in ms:
        rhs = jnp.concatenate([m["vb"], m["kb"] * m["eg"]], axis=1)
        m["sol"] = jnp.dot(m["tm"], rhs, precision=HI, preferred_element_type=F32)
        m["intra"] = jnp.where(m["ri"] >= m["ci"], m["qk"] * m["decay"], 0.0)


def _dn_chunk_fwd(qkvn, bg, g_rows, bsz, plug=None):
    t = qkvn.shape[0]
    s = t // bsz
    nc, nseg = s // _C, s // _SEG

    def body(q_ref, k_ref, v_ref, bg_ref, gr_ref, o_ref, st_ref, s_ref):
        @pl.when(pl.program_id(1) == 0)
        def _():
            s_ref[...] = jnp.zeros_like(s_ref)

        def chunk(c, carry):
            ms = [_chunk_pre(q_ref, k_ref, v_ref, bg_ref, gr_ref, c, hh) for hh in _HEADS]
            _chunk_solve(ms, False)
            sts = [s_ref[hh] for hh in _HEADS]
            for hh in _HEADS:
                st_ref[0, hh, c] = sts[hh]
            ws = [_dot(m["sol"][:, DN_HD:], st, _NN) for m, st in zip(ms, sts)]
            qs = [_dot(m["q"] * m["eg"], st, _NN) for m, st in zip(ms, sts)]
            v_new = [m["sol"][:, :DN_HD] - a for m, a in zip(ms, ws)]
            iv = [_dot(m["intra"], vn, _NN) for m, vn in zip(ms, v_new)]
            upd = [_dot(m["k"] * m["et"], vn, _TN) for m, vn in zip(ms, v_new)]
            for hh in _HEADS:
                s_ref[hh] = sts[hh] * ms[hh]["gl"] + upd[hh]
                o_ref[pl.ds(ms[hh]["r0"], _C), _hsl(hh)] = qs[hh] + iv[hh]
            return carry

        lax.fori_loop(0, _SEG // _C, chunk, 0)

    col, bgs, grs, sts_spec = _chunk_specs(nseg, False)
    (o, states), extra = _plugged_call(
        body, plug, _grid_ends((bsz, nseg)), (qkvn, qkvn, qkvn, bg, g_rows), name="dn_chunk_fwd", grid=(bsz, nseg),
        in_specs=[col(0), col(1), col(2), bgs, grs], out_specs=[col(0), sts_spec],
        out_shape=[jax.ShapeDtypeStruct((t, DN), F32), jax.ShapeDtypeStruct((bsz, DN_HEADS, nc, DN_HD, DN_HD), F32)],
        scratch_shapes=[pltpu.VMEM((DN_HEADS, DN_HD, DN_HD), F32)])
    return o, states, extra


def _dn_chunk_bwd(qkvn, bg, g_rows, states, do, bsz, plug=None):
    t = qkvn.shape[0]
    s = t // bsz
    nc, nseg = s // _C, s // _SEG

    def body(q_ref, k_ref, v_ref, bg_ref, gr_ref, st_ref, do_ref, dqkv_ref, dbg_ref, ds_ref):
        @pl.when(pl.program_id(1) == 0)
        def _():
            ds_ref[...] = jnp.zeros_like(ds_ref)

        def chunk(cc, carry):
            c = _SEG // _C - 1 - cc
            ms = [_chunk_pre(q_ref, k_ref, v_ref, bg_ref, gr_ref, c, hh) for hh in _HEADS]
            _chunk_solve(ms, True)
            ri, ci = ms[0]["ri"], ms[0]["ci"]
            for hh, m in enumerate(ms):
                m["st"] = st_ref[0, hh, c]
                m["ds_out"] = ds_ref[hh]
                m["do"] = do_ref[pl.ds(m["r0"], _C), _hsl(hh)]
                m["w"] = m["sol"][:, DN_HD:]
            for m in ms:
                m["v_new"] = m["sol"][:, :DN_HD] - _dot(m["w"], m["st"], _NN)
            for m in ms:
                m["q_dec"], m["k_tail"] = m["q"] * m["eg"], m["k"] * m["et"]
                m["dk_tail"] = _dot(m["v_new"], m["ds_out"], _NT)
                m["dv_new"] = _dot(m["k_tail"], m["ds_out"], _NN) + _dot(m["intra"], m["do"], _TN)
                m["dq_dec"] = _dot(m["do"], m["st"], _NT)
                m["ds_in"] = m["ds_out"] * m["gl"] + _dot(m["q_dec"], m["do"], _TN)
                m["dintra"] = jnp.where(ri >= ci, _dot(m["do"], m["v_new"], _NT), 0.0)
                m["dintra_t"] = jnp.where(ri <= ci, _dot(m["v_new"], m["do"], _NT), 0.0)
            for m in ms:
                m["dw"] = -_dot(m["dv_new"], m["st"], _NT)
                m["ds_in"] = m["ds_in"] - _dot(m["w"], m["dv_new"], _TN)
            for m in ms:
                dsol = jnp.concatenate([m["dv_new"], m["dw"]], axis=1)
                m["drhs"] = lax.dot_general(m["tm"], dsol, _TN, precision=HI, preferred_element_type=F32)
            for m in ms:
                m["dl"] = jnp.where(ri > ci, -_dot(m["drhs"], m["sol"], _NT), 0.0)
                m["dl_t"] = jnp.where(ri < ci, -_dot(m["sol"], m["drhs"], _NT), 0.0)
            for m in ms:
                m["dkb2"] = _dot(m["dl"] * m["decay"], m["k"], _NN)
                m["dk"] = _dot(m["dl_t"] * m["decay_t"], m["kb"], _NN) + _dot(m["dintra_t"] * m["decay_t"], m["q"], _NN)
                m["dq"] = _dot(m["dintra"] * m["decay"], m["k"], _NN)
            for m in ms:
                _chunk_bwd_finish(m)
            for m in ms:
                m["dg_b"] = jnp.dot(jnp.where(ri <= ci, 1.0, 0.0).astype(F32), m["dgc"] + jnp.zeros((_C, LANES), F32),
                                    precision=HI, preferred_element_type=F32)
            lane = lax.broadcasted_iota(jnp.int32, (_C, LANES), 1)
            for hh, m in enumerate(ms):
                rows = pl.ds(m["r0"], _C)
                dqkv_ref[0, rows, _hsl(hh)] = m["dq"] * (DN_HD ** -0.5)
                dqkv_ref[1, rows, _hsl(hh)] = m["dk"]
                dqkv_ref[2, rows, _hsl(hh)] = m["dv"]
                dbg_ref[rows, _hsl(hh)] = jnp.where(lane == 0, m["dbeta"], jnp.where(lane == 1, m["dg_b"], 0.0))
                ds_ref[hh] = m["ds_in"]
            return carry

        lax.fori_loop(0, _SEG // _C, chunk, 0)

    col, bgs, grs, sts_spec = _chunk_specs(nseg, True)
    o = jax.ShapeDtypeStruct((t, DN), F32)
    outs, extra = _plugged_call(
        body, plug, _grid_ends((bsz, nseg)), (qkvn, qkvn, qkvn, bg, g_rows, states, do), name="dn_chunk_bwd", grid=(bsz, nseg),
        in_specs=[col(0), col(1), col(2), bgs, grs, sts_spec, col(0)],
        out_specs=[_vspec((3, _SEG, DN), lambda b, i: (0, b * nseg + nseg - 1 - i, 0)), col(0)],
        out_shape=[jax.ShapeDtypeStruct((3, t, DN), F32), o],
        scratch_shapes=[pltpu.VMEM((DN_HEADS, DN_HD, DN_HD), F32)])
    return (*outs, extra)


def _chunk_bwd_finish(m):
    q, k, v, beta, decay, decay_t = m["q"], m["k"], m["v"], m["beta"], m["decay"], m["decay_t"]
    eg, et, gl, kb, dl, dl_t, dintra, dintra_t = m["eg"], m["et"], m["gl"], m["kb"], m["dl"], m["dl_t"], m["dintra"], m["dintra_t"]
    dq_dec, dk_tail, dq, dk = m["dq_dec"], m["dk_tail"], m["dq"], m["dk"]
    dgl = jnp.sum(jnp.sum(m["ds_out"] * m["st"], axis=1, keepdims=True), axis=0, keepdims=True)
    dvb, dkbeg = m["drhs"][:, :DN_HD], m["drhs"][:, DN_HD:]
    dkb = dkbeg * eg + m["dkb2"]
    deg = jnp.sum(dkbeg * kb, axis=1, keepdims=True)
    em = (dl * m["kk"] + dintra * m["qk"]) * decay
    em_t = (dl_t * m["kk_t"] + dintra_t * m["qk_t"]) * decay_t
    dgc = jnp.sum(em, axis=1, keepdims=True) - jnp.sum(em_t, axis=1, keepdims=True)
    dq = dq + dq_dec * eg
    deg = deg + jnp.sum(dq_dec * q, axis=1, keepdims=True)
    dk = dk + dk_tail * et
    det = jnp.sum(dk_tail * k, axis=1, keepdims=True)
    dgc = dgc + deg * eg - det * et
    dgc_last = jnp.sum(det * et, axis=0, keepdims=True) + dgl * gl
    rcol = lax.broadcasted_iota(jnp.int32, (_C, 1), 0)
    m["dgc"] = dgc + jnp.where(rcol == _C - 1, dgc_last, 0.0)
    m["dq"] = dq
    m["dk"] = dk + dkb * beta
    m["dbeta"] = jnp.sum(dkb * k, axis=1, keepdims=True) + jnp.sum(dvb * v, axis=1, keepdims=True)
    m["dv"] = dvb * beta


def _mod_fwd(c_all, ada_w_loc, ada_b_loc):
    n, cols = c_all.shape[0], ada_w_loc.shape[1]

    def body(c_ref, w_ref, b_ref, o_ref):
        o_ref[...] = _dot(jax.nn.silu(c_ref[...]), w_ref[...], _NN) + b_ref[...]

    return pl.pallas_call(body, name="mod_fwd", out_shape=jax.ShapeDtypeStruct((n, cols), F32))(c_all, ada_w_loc, ada_b_loc)


def _ada_grad(c_all, dmod_loc, dmod_all):
    d, cols = c_all.shape[1], dmod_loc.shape[1]

    def body(c_ref, dl_ref, da_ref, gw_ref, gb_ref):
        gw_ref[...] = _dot(jax.nn.silu(c_ref[...]), dl_ref[...], _TN)
        gb_ref[...] = jnp.sum(da_ref[...], axis=0, keepdims=True)

    return pl.pallas_call(body, name="ada_grad", out_shape=[jax.ShapeDtypeStruct((d, cols), F32),
                                                           jax.ShapeDtypeStruct((1, dmod_all.shape[1]), F32)])(c_all, dmod_loc, dmod_all)


ELEMENTWISE_BLOCK_BYTES = 3 * 2 ** 19


def _row_tile(r, c=1024):
    fits = [tr for tr in range(16, r + 1, 16) if tr * c * 4 <= ELEMENTWISE_BLOCK_BYTES]
    if not fits:
        return r
    whole = [tr for tr in fits if r % tr == 0]
    return whole[-1] if whole else fits[-1]


def _adamw(w, m, v, grads, name):
    r, c = w.shape
    tr = _row_tile(r, c)
    blk = _vspec((tr, c), lambda i: (i, 0))
    n = len(grads)

    def body(*refs):
        w_ref, m_ref, v_ref = refs[:3]
        g_ref, d_ref, mo_ref, vo_ref = refs[3 + n:]
        g = refs[3][...]
        for p in refs[4:3 + n]:
            g = g + p[...]
        m_new = ADAM_B1 * m_ref[...] + (1.0 - ADAM_B1) * g
        v_new = ADAM_B2 * v_ref[...] + (1.0 - ADAM_B2) * jnp.square(g)
        m_hat = m_new / (1.0 - ADAM_B1 ** ADAM_STEP)
        v_hat = v_new / (1.0 - ADAM_B2 ** ADAM_STEP)
        g_ref[...] = g
        d_ref[...] = -ADAM_LR * (m_hat / (jnp.sqrt(v_hat) + ADAM_EPS) + ADAM_WD * w_ref[...])
        mo_ref[...] = m_new
        vo_ref[...] = v_new

    o = jax.ShapeDtypeStruct((r, c), F32)
    return pl.pallas_call(body, name=name, grid=(pl.cdiv(r, tr),), in_specs=[blk] * (3 + n), out_specs=[blk] * 4,
                          out_shape=[o] * 4)(w, m, v, *grads)


def _sum_lead(x, name):
    p, r, c = x.shape
    tr = _row_tile(r, c)

    def body(x_ref, o_ref):
        acc = x_ref[0].astype(F32)
        for i in range(1, p):
            acc = acc + x_ref[i].astype(F32)
        o_ref[...] = acc

    return pl.pallas_call(body, name=name, grid=(pl.cdiv(r, tr),), in_specs=[_vspec((p, tr, c), lambda i: (0, i, 0))],
                          out_specs=_vspec((tr, c), lambda i: (i, 0)), out_shape=jax.ShapeDtypeStruct((r, c), F32))(x)


def _allgather8(x_shard, name):
    m_per, n = x_shard.shape

    def body(x_ref, out_ref, send_sems, recv_sems, local_sem):
        x, y, c = lax.axis_index("x"), lax.axis_index("y"), lax.axis_index("c")
        me, sibling = (x, y, c), (x, y, 1 - c)
        chips = [(1 - x, y), (x, 1 - y), (1 - x, 1 - y)]

        def rows(px, py, pc):
            return out_ref.at[pl.ds((4 * px + 2 * py + pc) * m_per, m_per), :]

        def copy(k, block, to, src=None):
            return pltpu.make_async_remote_copy(
                src_ref=rows(*block) if src is None else src, dst_ref=rows(*block), send_sem=send_sems.at[k],
                recv_sem=recv_sems.at[k], device_id=to, device_id_type=MESH)

        mine = pltpu.make_async_copy(x_ref, rows(*me), local_sem)
        mine.start()
        first = [copy(0, me, sibling, src=x_ref)]
        first += [copy(1 + j, me, (*chip, c), src=x_ref) for j, chip in enumerate(chips)]
        for cp in first:
            cp.start()
        passed = [copy(4 + j, (*chip, c), sibling) for j, chip in enumerate(chips)]
        for j, chip in enumerate(chips):
            copy(1 + j, (*chip, c), me).wait_recv()
            passed[j].start()
        copy(0, sibling, me).wait_recv()
        for j, chip in enumerate(chips):
            copy(4 + j, (*chip, 1 - c), me).wait_recv()
        for cp in first + passed:
            cp.wait_send()
        mine.wait()

    return pl.pallas_call(
        body, name=name, out_shape=jax.ShapeDtypeStruct((8 * m_per, n), x_shard.dtype),
        in_specs=[pl.BlockSpec(memory_space=pltpu.VMEM)], out_specs=pl.BlockSpec(memory_space=pltpu.VMEM),
        scratch_shapes=[pltpu.SemaphoreType.DMA((7,)), pltpu.SemaphoreType.DMA((7,)), pltpu.SemaphoreType.DMA],
    )(x_shard)


_HBM = pl.BlockSpec(memory_space=pltpu.HBM)


def _mesh_place():
    x, y, c = lax.axis_index("x"), lax.axis_index("y"), lax.axis_index("c")
    return x, y, c, 2 * x + y, [(1 - x, y), (x, 1 - y), (1 - x, 1 - y)]


def _gather_plug(shards):
    n = len(shards)

    def half(ref, c, lead=None):
        r, cols = ref.shape[-2] // 2, ref.shape[-1] // 2
        if r % 16 == 0:
            rows = pl.ds(pl.multiple_of(c * r, 16), r)
            return ref.at[rows, :] if lead is None else ref.at[lead, rows, :]
        lanes = pl.ds(pl.multiple_of(c * cols, LANES), cols)
        return ref.at[:, lanes] if lead is None else ref.at[lead, :, lanes]

    def copies(ins, outs, send, recv):
        x, y, c, me, chips = _mesh_place()
        ici, fwd, fwd_in = [], [], []
        for i in range(n):
            for j, (px, py) in enumerate(chips):
                q = 2 * px + py
                ici.append((pltpu.make_async_remote_copy(
                    src_ref=half(ins[i], c), dst_ref=half(outs[i], c, me), send_sem=send.at[6 * i + j], recv_sem=recv.at[6 * i + j],
                    device_id=(px, py, c), device_id_type=MESH),
                    pltpu.make_async_remote_copy(
                    src_ref=half(ins[i], c), dst_ref=half(outs[i], c, q), send_sem=send.at[6 * i + j], recv_sem=recv.at[6 * i + j],
                    device_id=(px, py, c), device_id_type=MESH)))
                fwd.append(pltpu.make_async_remote_copy(
                    src_ref=half(outs[i], c, q), dst_ref=half(outs[i], c, q), send_sem=send.at[6 * i + 3 + j],
                    recv_sem=recv.at[6 * i + 3 + j], device_id=(x, y, 1 - c), device_id_type=MESH))
                fwd_in.append(pltpu.make_async_remote_copy(
                    src_ref=half(outs[i], 1 - c, q), dst_ref=half(outs[i], 1 - c, q), send_sem=send.at[6 * i + 3 + j],
                    recv_sem=recv.at[6 * i + 3 + j], device_id=(x, y, 1 - c), device_id_type=MESH))
        return ici, fwd, fwd_in, me

    def start(ins, outs, send, recv, loc):
        ici, _, _, me = copies(ins, outs, send, recv)
        for i in range(n):
            pltpu.make_async_copy(ins[i], outs[i].at[me], loc.at[i]).start()
        for out_cp, _ in ici:
            out_cp.start()

    def finish(ins, outs, send, recv, loc):
        ici, fwd, fwd_in, me = copies(ins, outs, send, recv)
        for (_, in_cp), f in zip(ici, fwd):
            in_cp.wait_recv()
            f.start()
        for f in fwd_in:
            f.wait_recv()
        for (out_cp, _), f in zip(ici, fwd):
            out_cp.wait_send()
            f.wait_send()
        for i in range(n):
            pltpu.make_async_copy(ins[i], outs[i].at[me], loc.at[i]).wait()

    return dict(ins=list(shards), out_shape=[jax.ShapeDtypeStruct((N_CHIPS,) + a.shape, a.dtype) for a in shards],
                scratch=[pltpu.SemaphoreType.DMA((6 * n,)), pltpu.SemaphoreType.DMA((6 * n,)), pltpu.SemaphoreType.DMA((n,))],
                start=start, finish=finish)


def _exchange_plug(pieces):
    n = len(pieces)

    def copies(ins, outs, send, recv):
        x, y, c, me, chips = _mesh_place()
        out_cps, in_cps = [], []
        for i in range(n):
            for j, (px, py) in enumerate(chips):
                q = 2 * px + py
                out_cps.append(pltpu.make_async_remote_copy(src_ref=ins[i].at[q], dst_ref=outs[i].at[me], send_sem=send.at[3 * i + j],
                                                            recv_sem=recv.at[3 * i + j], device_id=(px, py, c), device_id_type=MESH))
                in_cps.append(pltpu.make_async_remote_copy(src_ref=ins[i].at[me], dst_ref=outs[i].at[q], send_sem=send.at[3 * i + j],
                                                           recv_sem=recv.at[3 * i + j], device_id=(px, py, c), device_id_type=MESH))
        return out_cps, in_cps, me

    def start(ins, outs, send, recv, loc):
        out_cps, _, me = copies(ins, outs, send, recv)
        for i in range(n):
            pltpu.make_async_copy(ins[i].at[me], outs[i].at[me], loc.at[i]).start()
        for cp in out_cps:
            cp.start()

    def finish(ins, outs, send, recv, loc):
        out_cps, in_cps, me = copies(ins, outs, send, recv)
        for cp in in_cps:
            cp.wait_recv()
        for cp in out_cps:
            cp.wait_send()
        for i in range(n):
            pltpu.make_async_copy(ins[i].at[me], outs[i].at[me], loc.at[i]).wait()

    return dict(ins=list(pieces), out_shape=[jax.ShapeDtypeStruct(a.shape, a.dtype) for a in pieces],
                scratch=[pltpu.SemaphoreType.DMA((3 * n,)), pltpu.SemaphoreType.DMA((3 * n,)), pltpu.SemaphoreType.DMA((n,))],
                start=start, finish=finish)


def _comm_call(plug, name):
    n_in, n_out = len(plug["ins"]), len(plug["out_shape"])

    def body(*refs):
        ins, outs, sems = refs[:n_in], refs[n_in:n_in + n_out], refs[n_in + n_out:]
        plug["start"](ins, outs, *sems)
        plug["finish"](ins, outs, *sems)

    return pl.pallas_call(body, name=name, out_shape=plug["out_shape"], in_specs=[_HBM] * n_in, out_specs=[_HBM] * n_out,
                          scratch_shapes=plug["scratch"])(*plug["ins"])


def _plugged_call(body, plug, first_last, args, *, name, grid, in_specs, out_specs, out_shape, scratch_shapes=()):
    in_specs, out_specs, out_shape, scratch_shapes = list(in_specs), list(out_specs), list(out_shape), list(scratch_shapes)
    if plug is None:
        return pl.pallas_call(body, name=name, grid=grid, in_specs=in_specs, out_specs=out_specs, out_shape=out_shape,
                              scratch_shapes=scratch_shapes)(*args), []
    n_in, n_out, n_sc = len(in_specs), len(out_specs), len(scratch_shapes)
    p_in, p_out = len(plug["ins"]), len(plug["out_shape"])

    def full(*refs):
        ins, refs = refs[:n_in], refs[n_in:]
        pins, refs = refs[:p_in], refs[p_in:]
        outs, refs = refs[:n_out], refs[n_out:]
        pouts, refs = refs[:p_out], refs[p_out:]
        scr, psems = refs[:n_sc], refs[n_sc:]
        first, last = first_last()

        @pl.when(first)
        def _():
            plug["start"](pins, pouts, *psems)

        body(*ins, *outs, *scr)

        @pl.when(last)
        def _():
            plug["finish"](pins, pouts, *psems)

    res = pl.pallas_call(full, name=name, grid=grid, in_specs=in_specs + [_HBM] * p_in, out_specs=out_specs + [_HBM] * p_out,
                         out_shape=out_shape + plug["out_shape"], scratch_shapes=scratch_shapes + plug["scratch"])(*args, *plug["ins"])
    return res[:n_out], res[n_out:]


def _grid_ends(grid):
    def ends():
        first = last = None
        for ax, n in enumerate(grid):
            i = pl.program_id(ax)
            first = (i == 0) if first is None else first & (i == 0)
            last = (i == n - 1) if last is None else last & (i == n - 1)
        return first, last
    return ends


def _sibling_exchange(arrs, name):
    n = len(arrs)

    def body(*refs):
        ins, outs = refs[:n], refs[n:2 * n]
        send_sems, recv_sems = refs[2 * n:]
        sibling = (lax.axis_index("x"), lax.axis_index("y"), 1 - lax.axis_index("c"))
        cps = [pltpu.make_async_remote_copy(src_ref=ins[i], dst_ref=outs[i], send_sem=send_sems.at[i], recv_sem=recv_sems.at[i],
                                            device_id=sibling, device_id_type=MESH) for i in range(n)]
        for cp in cps:
            cp.start()
        for cp in cps:
            cp.wait()

    return pl.pallas_call(
        body, name=name, out_shape=[jax.ShapeDtypeStruct(a.shape, a.dtype) for a in arrs], in_specs=[_HBM] * n, out_specs=[_HBM] * n,
        scratch_shapes=[pltpu.SemaphoreType.DMA((n,)), pltpu.SemaphoreType.DMA((n,))],
    )(*arrs)


def _to_padded(wt):
    z = jnp.zeros((IN_PAD - IN_DIM, wt.shape[1]), wt.dtype)
    return jnp.concatenate([wt[768:2304], wt[0:512], wt[2304:2816], wt[2824:3848], wt[3848:4872],
                            wt[512:640], wt[640:768], wt[2816:2824], z], axis=0)


def _from_padded(gt):
    return jnp.concatenate([gt[C_Q:C_Q + AQ], gt[C_K:C_K + AKV], gt[C_V:C_V + AKV], gt[C_DQKV:C_DQKV + 3 * DN],
                            gt[C_DZ:C_DZ + DN], gt[C_BD:C_BD + 2 * DN_HEADS], gt[C_GA:C_GA + D_MODEL],
                            gt[C_GD:C_GD + D_MODEL]], axis=0)


def _lane_vec(a):
    return jnp.zeros((1, LANES), F32).at[0, DN_HEADS:2 * DN_HEADS].set(a)


_ROW_SHARDED = ("w_in", "w_out", "ffn_w_down")
_FFN = ("ffn_w_up", "ffn_w_down")
_LATE_MIXER = ("w_attn_branch", "w_dn_branch", "w_out")


def _pieces(k, a):
    if a.ndim == 3:
        return a
    if k in _ROW_SHARDED:
        return a.reshape(N_CHIPS, a.shape[0] // N_CHIPS, a.shape[1]).astype(BF16)
    return jnp.transpose(a.reshape(a.shape[0], N_CHIPS, a.shape[1] // N_CHIPS), (1, 0, 2)).astype(BF16)


def _assemble(k, a):
    if k in _ROW_SHARDED:
        return a.reshape(-1, a.shape[2])
    return jnp.transpose(a, (1, 0, 2)).reshape(a.shape[1], -1)


def _device_step(x2, tgt2, mod, p, bsz, shards=None):
    d = D_MODEL
    on_mesh = shards is not None
    p = dict(p)
    sh1, sc1, g1, sh2, sc2, g2 = [mod[:, i * d:(i + 1) * d].reshape(bsz, 1, d) for i in range(N_MOD)]
    alog_v, dt_v = _lane_vec(p["dn_a_log"]), _lane_vec(p["dn_dt_bias"])
    sinks = p["attn_sinks"].reshape(1, AQ_HEADS)
    u1 = _pre_fwd(x2, p["norm_mix_pre"], sc1, sh1, "pre1_fwd")
    if on_mesh:
        proj, got = _mm(u1, p["w_in"], "nt", F32, "mm_proj", _gather_plug(shards["late_mixer"]))
        p.update({k: _assemble(k, a) for k, a in zip(_LATE_MIXER, got)})
    else:
        proj = _mm(u1, p["w_in"], "nt", F32, "mm_proj")
    bias = _bias_build(p["rel_bias"])
    y_attn = _attn_fwd(proj, bias, sinks, bsz)
    qkvn = _dn_prep_fwd(proj, p["dn_conv_w"], bsz)
    bg = _bg_fwd(proj, alog_v, dt_v, bsz)
    nc = x2.shape[0] // bsz // DN_CHUNK
    g_rows = jnp.transpose(bg[:, DN_HEADS:2 * DN_HEADS].reshape(bsz, nc, DN_CHUNK, DN_HEADS), (0, 3, 1, 2))
    o, states, got = _dn_chunk_fwd(qkvn, bg, g_rows, bsz, _gather_plug(shards["ffn"]) if on_mesh else None)
    for k, a in zip(_FFN, got):
        p[k] = _assemble(k, a)
    y_dn = _dn_out_fwd(o, proj, p["dn_norm_w"])
    ya = _mm(y_attn, p["w_attn_branch"], "nn", F32, "mm_ya")
    yd = _mm(y_dn, p["w_dn_branch"], "nn", F32, "mm_yd")
    merged = _merge_fwd(proj, ya, yd)
    y1 = _mm(merged, p["w_out"], "nn", F32, "mm_y1")
    h1 = _post_fwd(x2, y1, p["norm_mix_post"], g1, "post1_fwd")
    u2 = _pre_fwd(h1, p["norm_ffn_pre"], sc2, sh2, "pre2_fwd")
    up = _mm(u2, p["ffn_w_up"], "nn", F32, "mm_up")
    act = _ffn_act_fwd(up, p["ffn_conv_w"], bsz)
    y2 = _mm(act, p["ffn_w_down"], "nn", F32, "mm_y2")
    dh2, sq = _post_loss(h1, y2, p["norm_ffn_post"], g2, tgt2, "post2_loss")
    g = {}
    dy2, g["norm_ffn_post"], dg2 = _post_bwd(dh2, y2, p["norm_ffn_post"], g2, "post2_bwd")
    dact = _mm(dy2, p["ffn_w_down"], "nt", F32, "mm_dact")
    g["ffn_w_down"] = _mm(act, dy2, "tn", BF16, "mm_dwdown")
    dupg, dupv, dcwg, dcwv = _ffn_act_bwd(up, p["ffn_conv_w"], dact, bsz)
    g["ffn_conv_w"] = jnp.concatenate([dcwg, dcwv], axis=1)
    dup = jnp.concatenate([dupg, dupv], axis=1)
    g["ffn_w_up"] = _mm(u2, dup, "tn", BF16, "mm_dwup", split=N_CHIPS)
    du2 = _mm(dup, p["ffn_w_up"], "nt", F32, "mm_du2")
    dh1, g["norm_ffn_pre"], dsc2, dsh2 = _pre_bwd(h1, p["norm_ffn_pre"], sc2, sh2, du2, dh2, "pre2_bwd")
    dy1, g["norm_mix_post"], dg1 = _post_bwd(dh1, y1, p["norm_mix_post"], g1, "post1_bwd")
    dmerged = _mm(dy1, p["w_out"], "nt", F32, "mm_dmerged")
    g["w_out"] = _mm(merged, dy1, "tn", BF16, "mm_dwout")
    dga, dgd, dya, dyd = _merge_bwd(proj, ya, yd, dmerged)
    dy_attn = _mm(dya, p["w_attn_branch"], "nt", BF16, "mm_dyattn")
    g["w_attn_branch"] = _mm(y_attn, dya, "tn", BF16, "mm_dwab", split=N_CHIPS)
    dy_dn = _mm(dyd, p["w_dn_branch"], "nt", F32, "mm_dydn")
    g["w_dn_branch"] = _mm(y_dn, dyd, "tn", BF16, "mm_dwdb", split=N_CHIPS)
    do, dz, g["dn_norm_w"] = _dn_out_bwd(o, proj, p["dn_norm_w"], dy_dn)
    def plug_for(names):
        return _exchange_plug([_pieces(k, g[k]) for k in names]) if on_mesh else None

    early = ("w_out", "w_attn_branch", "w_dn_branch")
    dqkvn, dbg4, got_ffn = _dn_chunk_bwd(qkvn, bg, g_rows, states, do, bsz, plug_for(_FFN))
    d_dqkv, g["dn_conv_w"] = _dn_prep_bwd(proj, p["dn_conv_w"], dqkvn, bsz)
    dbd, g["dn_a_log"], g["dn_dt_bias"] = _bg_bwd(proj, alog_v, dt_v, dbg4, bsz)
    dq, dk, dv, dbias, g["attn_sinks"], got_early = _attn_bwd(proj, bias, sinks, dy_attn, bsz, plug_for(early))
    g["rel_bias"] = _bias_grad(dbias)
    dproj = jnp.concatenate([d_dqkv, dq, dz, dga, dgd, dk, dv, dbd], axis=1)
    g["w_in"] = _from_padded(_mm(dproj, u1, "tn", BF16, "mm_dwin"))
    if on_mesh:
        du1, got_in = _mm(dproj, p["w_in"], "nn", F32, "mm_du1", plug_for(("w_in",)))
        g.update(zip(_FFN + early + ("w_in",), list(got_ffn) + list(got_early) + list(got_in)))
    else:
        du1 = _mm(dproj, p["w_in"], "nn", F32, "mm_du1")
    dx, g["norm_mix_pre"], dsc1, dsh1 = _pre_bwd(x2, p["norm_mix_pre"], sc1, sh1, du1, dh1, "pre1_bwd")
    dmod = jnp.concatenate([dsh1, dsc1, dg1, dsh2, dsc2, dg2], axis=-1).reshape(bsz, N_MOD * d)
    return sq, dx, dmod, g


_SMALL = (("norm_mix_pre", D_MODEL), ("norm_mix_post", D_MODEL), ("norm_ffn_pre", D_MODEL), ("norm_ffn_post", D_MODEL),
          ("dn_norm_w", DN_HD), ("dn_a_log", LANES), ("dn_dt_bias", LANES), ("attn_sinks", AQ_HEADS * LANES),
          ("rel_bias", AQ_HEADS * LANES), ("dn_conv_w", DN_CONV * 3 * DN), ("ffn_conv_w", FFN_CONV * 2 * D_FF))


def _pack_rows(parts, rows):
    flat = jnp.concatenate([a.reshape(-1) for a in parts])
    return jnp.concatenate([flat, jnp.zeros((rows * LANES - flat.shape[0],), F32)]).reshape(rows, LANES)


def _pad128(a):
    flat = a.reshape(-1)
    n = -(-flat.shape[0] // LANES) * LANES
    return jnp.concatenate([flat, jnp.zeros((n - flat.shape[0],), F32)]) if n != flat.shape[0] else flat


_W_NAMES = ("ada_w", "ada_b", "norm_mix_pre", "norm_mix_post", "norm_ffn_pre", "norm_ffn_post", "w_in", "dn_conv_w", "dn_a_log",
            "dn_dt_bias", "dn_norm_w", "attn_sinks", "rel_bias", "w_attn_branch", "w_dn_branch", "w_out", "ffn_w_up", "ffn_conv_w",
            "ffn_w_down")
_BIG = ("w_in", "w_attn_branch", "w_dn_branch", "w_out", "ffn_w_up", "ffn_w_down")


def kernel(x, c, *rest):
    nw = len(_W_NAMES)
    w = dict(zip(_W_NAMES, rest[:nw]))
    loss_target = rest[nw]
    m = dict(zip(_W_NAMES, rest[nw + 1:2 * nw + 1]))
    v = dict(zip(_W_NAMES, rest[2 * nw + 1:3 * nw + 1]))
    ix, iy, ic = lax.axis_index("x"), lax.axis_index("y"), lax.axis_index("c")
    chip, dev = 2 * ix + iy, 4 * ix + 2 * iy + ic
    bsz, s, d = x.shape
    t = bsz * s
    n_dev = 8

    front_rows = 64
    front = _pack_rows([c, w["dn_conv_w"], w["ffn_conv_w"]], front_rows)
    front_all = _allgather8(front, "ag_front").reshape(n_dev, front_rows * LANES)
    n_c, n_dc, n_fc = bsz * d, DN_CONV * 3 * DN // N_CHIPS, FFN_CONV * 2 * D_FF // N_CHIPS
    c_all = front_all[:, :n_c].reshape(n_dev * bsz, d)
    per_chip = front_all[0::2]
    dn_conv_full = jnp.transpose(per_chip[:, n_c:n_c + n_dc].reshape(N_CHIPS, DN_CONV, -1), (1, 0, 2)).reshape(DN_CONV, 3 * DN)
    ffn_conv_full = jnp.transpose(per_chip[:, n_c + n_dc:n_c + n_dc + n_fc].reshape(N_CHIPS, FFN_CONV, -1), (1, 0, 2)).reshape(FFN_CONV, 2 * D_FF)

    mod_cols = N_MOD * d // N_CHIPS
    ada_b_loc = lax.dynamic_slice(w["ada_b"], (0, chip * mod_cols), (1, mod_cols))
    mod_part = _mod_fwd(c_all, w["ada_w"][0], ada_b_loc)
    mod_all = _allgather8(mod_part, "ag_mod").reshape(n_dev, n_dev * bsz, mod_cols)[0::2]
    mod = jnp.transpose(lax.dynamic_slice(mod_all, (0, dev * bsz, 0), (N_CHIPS, bsz, mod_cols)), (1, 0, 2)).reshape(bsz, N_MOD * d)

    (w_in_all,) = _comm_call(_gather_plug([jnp.swapaxes(w["w_in"][0], 0, 1).astype(BF16)]), "gather_w_in")
    p = {"w_in": _to_padded(_assemble("w_in", w_in_all))}
    shards = {"late_mixer": [w[k][0].astype(BF16) for k in _LATE_MIXER], "ffn": [w[k][0].astype(BF16) for k in _FFN]}
    for k in ("norm_mix_pre", "norm_mix_post", "norm_ffn_pre", "norm_ffn_post", "dn_norm_w", "attn_sinks"):
        p[k] = w[k]
    p["dn_a_log"], p["dn_dt_bias"], p["rel_bias"] = w["dn_a_log"][0], w["dn_dt_bias"][0], w["rel_bias"]
    p["dn_conv_w"], p["ffn_conv_w"] = dn_conv_full, ffn_conv_full

    sq, dx, dmod, g = _device_step(x.reshape(t, d), loss_target.reshape(t, d), mod, p, bsz, shards)
    loss = lax.psum(0.5 * jnp.sum(sq), ("x", "y", "c"))

    g["dn_a_log"], g["dn_dt_bias"] = g["dn_a_log"].reshape(-1), g["dn_dt_bias"].reshape(-1)
    small_rows = 328
    small = _pack_rows([dmod] + [g[k] for k, _ in _SMALL], small_rows)
    small_all = _allgather8(small, "ag_small").reshape(n_dev, small_rows, LANES)
    n_dm = bsz * N_MOD * d
    dmod_all = small_all.reshape(n_dev, -1)[:, :n_dm].reshape(n_dev * bsz, N_MOD * d)
    tot = _sum_lead(small_all, "sum_small").reshape(-1)
    gs, off = {}, n_dm
    for k, n in _SMALL:
        gs[k] = tot[off:off + n]
        off += n
    grad = {}
    grad["ada_w"], grad["ada_b"] = _ada_grad(c_all, lax.dynamic_slice(dmod_all, (0, chip * mod_cols), (n_dev * bsz, mod_cols)), dmod_all)
    for k in ("norm_mix_pre", "norm_mix_post", "norm_ffn_pre", "norm_ffn_post", "dn_norm_w"):
        grad[k] = gs[k]
    grad["dn_a_log"] = gs["dn_a_log"][DN_HEADS:2 * DN_HEADS]
    grad["dn_dt_bias"] = gs["dn_dt_bias"][DN_HEADS:2 * DN_HEADS]
    grad["attn_sinks"] = gs["attn_sinks"].reshape(AQ_HEADS, LANES)[:, 0]
    grad["rel_bias"] = gs["rel_bias"].reshape(AQ_HEADS, LANES)[:, :REL_BUCKETS].T
    grad["dn_conv_w"] = lax.dynamic_slice(gs["dn_conv_w"].reshape(DN_CONV, 3 * DN), (0, chip * (3 * DN // N_CHIPS)), (DN_CONV, 3 * DN // N_CHIPS))
    grad["ffn_conv_w"] = lax.dynamic_slice(gs["ffn_conv_w"].reshape(FFN_CONV, 2 * D_FF), (0, chip * (2 * D_FF // N_CHIPS)), (FFN_CONV, 2 * D_FF // N_CHIPS))

    mine = [_sum_lead(g[k], "sum_" + k) for k in _BIG]
    theirs = _sibling_exchange(mine, "exchange_cores")

    out = {}
    for k, a, b in zip(_BIG, mine, theirs):
        if k == "w_in":
            tr = lambda z: jnp.swapaxes(z, 0, 1)
            out[k] = [tr(r) for r in _adamw(tr(w[k][0]), tr(m[k][0]), tr(v[k][0]), [a, b], "adamw_" + k)]
        else:
            out[k] = _adamw(w[k][0], m[k][0], v[k][0], [a, b], "adamw_" + k)
    out["ada_w"] = _adamw(w["ada_w"][0], m["ada_w"][0], v["ada_w"][0], [grad["ada_w"]], "adamw_ada_w")
    small_names = [k for k in _W_NAMES if k not in _BIG and k != "ada_w"]
    offs, n_tot = {}, 0
    for k in small_names:
        offs[k] = n_tot
        n_tot += -(-w[k].size // LANES) * LANES
    pack = lambda dct: jnp.concatenate([_pad128(dct[k]) for k in small_names]).reshape(-1, LANES)
    rows_s = n_tot // LANES
    pad_rows = -(-rows_s // 8) * 8 - rows_s
    padr = lambda a: jnp.concatenate([a, jnp.zeros((pad_rows, LANES), F32)]) if pad_rows else a
    res_small = _adamw(padr(pack(w)), padr(pack(m)), padr(pack(v)), [padr(pack(grad))], "adamw_small")
    for k in small_names:
        out[k] = [r.reshape(-1)[offs[k]:offs[k] + w[k].size].reshape(w[k].shape) for r in res_small]
    for k in _BIG + ("ada_w",):
        out[k] = [r.reshape(w[k].shape) for r in out[k]]
    grads, deltas, new_m, new_v = ([out[k][i] for k in _W_NAMES] for i in range(4))
    return (loss, dx.reshape(bsz, s, d), *grads, *deltas, *new_m, *new_v)
```

```python
import functools
import math

import numpy as np
import jax
import jax.numpy as jnp
from jax import lax
from jax.experimental import pallas as pl
from jax.experimental.pallas import tpu as pltpu

F32 = jnp.float32
BF16 = jnp.bfloat16
MESH = pl.DeviceIdType.MESH

D_MODEL = 1024
N_MOD = 6
AQ_HEADS, AKV_HEADS, A_HD, WINDOW = 8, 2, 64, 128
REL_BUCKETS, REL_MAX_DIST = 32, 128
DN_HEADS, DN_HD, DN_CONV, DN_CHUNK = 4, 128, 4, 64
D_FF, FFN_CONV = 2816, 3
RMS_EPS, L2_EPS, NEG_INF = 1e-6, 1e-6, -1e30
AQ, AKV, DN = AQ_HEADS * A_HD, AKV_HEADS * A_HD, DN_HEADS * DN_HD
IN_DIM = AQ + 2 * AKV + 3 * DN + DN + 2 * DN_HEADS + 2 * D_MODEL
C_DQKV, C_Q, C_DZ, C_GA, C_GD, C_K, C_V, C_BD = 0, 1536, 2048, 2560, 3584, 4608, 4736, 4864
IN_PAD = 4992
LANES = 128
N_CHIPS = 4

ADAM_LR, ADAM_B1, ADAM_B2, ADAM_EPS, ADAM_WD, ADAM_STEP = 0.001, 0.9, 0.999, 1e-08, 0.01, 10


def _pick(n, cap):
    best = None
    for t in range(LANES, cap + 1, LANES):
        if n % t == 0:
            best = t
    return best if best is not None else n


def _vspec(shape, index_map):
    return pl.BlockSpec(shape, index_map)


MM_VMEM_BUDGET = 40 * 2 ** 20
GRID_STEP_S = 0.35e-6
HBM_BYTES_PER_S = 3.0e12
MXU_FLOPS_PER_S = 9.0e14
MXU_DIM = 256


def _mm_tiles(m, n, k, mode, in_bytes, out_bytes, split=1):
    best = None
    for tm in [t for t in range(LANES, m + 1, LANES) if m % t == 0]:
        for tn in [t for t in range(LANES, n // split + 1, LANES) if (n // split) % t == 0]:
            a, b, o = k * tm * in_bytes, k * tn * in_bytes, tm * tn * out_bytes
            if 2 * (a + b + o) + (a if mode == "tn" else 0) > MM_VMEM_BUDGET:
                continue
            hbm_s = (m * k * in_bytes + (m // tm) * n * k * in_bytes + m * n * out_bytes) / HBM_BYTES_PER_S
            mxu_s = 2 * m * n * k / (MXU_FLOPS_PER_S * min(1.0, tm / MXU_DIM) * min(1.0, tn / MXU_DIM))
            cost = (m // tm) * (n // tn) * GRID_STEP_S + max(hbm_s, mxu_s)
            if best is None or cost < best[0]:
                best = (cost, tm, tn)
    return best[1], best[2]


def _mm(a, b, mode, out_dtype, name, plug=None, split=1):
    if mode == "nn":
        (m, k), n = a.shape, b.shape[1]
        dims = (((1,), (0,)), ((), ()))
    elif mode == "nt":
        (m, k), n = a.shape, b.shape[0]
        dims = (((1,), (1,)), ((), ()))
    else:
        (k, m), n = a.shape, b.shape[1]
        dims = (((0,), (0,)), ((), ()))
    tm, tn = _mm_tiles(m, n, k, mode, a.dtype.itemsize, jnp.dtype(out_dtype).itemsize, split)
    if mode == "tn":
        a_spec = _vspec((k, tm), lambda i, j: (0, i))
    else:
        a_spec = _vspec((tm, k), lambda i, j: (i, 0))
    if mode == "nt":
        b_spec = _vspec((tn, k), lambda i, j: (j, 0))
    else:
        b_spec = _vspec((k, tn), lambda i, j: (0, j))

    def body(a_ref, b_ref, o_ref):
        o_ref[...] = lax.dot_general(a_ref[...].astype(BF16), b_ref[...].astype(BF16), dims,
                                     preferred_element_type=F32).astype(out_dtype).reshape(o_ref.shape)

    grid = (m // tm, n // tn)
    if split == 1:
        out_spec, out_shape = _vspec((tm, tn), lambda i, j: (i, j)), (m, n)
    else:
        per = n // split // tn
        out_spec, out_shape = _vspec((1, tm, tn), lambda i, j: (j // per, i, j % per)), (split, m, n // split)
    (out,), extra = _plugged_call(body, plug, _grid_ends(grid), (a, b), name=name, grid=grid, in_specs=[a_spec, b_spec],
                                  out_specs=[out_spec], out_shape=[jax.ShapeDtypeStruct(out_shape, out_dtype)])
    return out if plug is None else (out, extra)


def _rms(x, w):
    return (x * lax.rsqrt(jnp.mean(x * x, axis=-1, keepdims=True) + RMS_EPS)) * w


def _pre_f(x, w, sc, sh):
    return _rms(x, w) * (1.0 + sc) + sh


def _post_f(y, w, g):
    return g * _rms(y, w)


def _tok_grid(t, bsz, ts):
    nt = t // bsz // ts
    return nt, (bsz, nt)


def _pre_fwd(x, w, sc, sh, name, ts=512):
    t, d = x.shape
    bsz = sc.shape[0]
    nt, grid = _tok_grid(t, bsz, ts)
    row = _vspec((ts, d), lambda b, i: (b * nt + i, 0))
    vec = _vspec((1, d), lambda b, i: (0, 0))
    bvec = _vspec((1, 1, d), lambda b, i: (b, 0, 0))

    def body(x_ref, w_ref, sc_ref, sh_ref, u_ref):
        u_ref[...] = _pre_f(x_ref[...], w_ref[...], sc_ref[0], sh_ref[0]).astype(BF16)

    return pl.pallas_call(body, name=name, grid=grid, in_specs=[row, vec, bvec, bvec], out_specs=row,
                          out_shape=jax.ShapeDtypeStruct((t, d), BF16))(x, w, sc, sh)


def _pre_bwd(x, w, sc, sh, du, dres, name, ts=512):
    t, d = x.shape
    bsz = sc.shape[0]
    nt, grid = _tok_grid(t, bsz, ts)
    row = _vspec((ts, d), lambda b, i: (b * nt + i, 0))
    vec = _vspec((1, d), lambda b, i: (0, 0))
    bvec = _vspec((1, 1, d), lambda b, i: (b, 0, 0))

    def body(x_ref, w_ref, sc_ref, sh_ref, du_ref, dres_ref, dx_ref, dw_ref, dsc_ref, dsh_ref):
        b, i = pl.program_id(0), pl.program_id(1)
        _, vjp = jax.vjp(_pre_f, x_ref[...], w_ref[...], sc_ref[0], sh_ref[0])
        dx, dw, dsc, dsh = vjp(du_ref[...])
        dx_ref[...] = dres_ref[...] + dx

        @pl.when((b == 0) & (i == 0))
        def _():
            dw_ref[...] = jnp.zeros_like(dw_ref)

        @pl.when(i == 0)
        def _():
            dsc_ref[...] = jnp.zeros_like(dsc_ref)
            dsh_ref[...] = jnp.zeros_like(dsh_ref)

        dw_ref[...] += dw
        dsc_ref[0] += dsc
        dsh_ref[0] += dsh

    return pl.pallas_call(
        body, name=name, grid=grid, in_specs=[row, vec, bvec, bvec, row, row], out_specs=[row, vec, bvec, bvec],
        out_shape=[jax.ShapeDtypeStruct((t, d), F32), jax.ShapeDtypeStruct((1, d), F32),
                   jax.ShapeDtypeStruct((bsz, 1, d), F32), jax.ShapeDtypeStruct((bsz, 1, d), F32)],
    )(x, w, sc, sh, du, dres)


def _post_fwd(res, y, w, g, name, ts=512):
    t, d = y.shape
    bsz = g.shape[0]
    nt, grid = _tok_grid(t, bsz, ts)
    row = _vspec((ts, d), lambda b, i: (b * nt + i, 0))
    vec = _vspec((1, d), lambda b, i: (0, 0))
    bvec = _vspec((1, 1, d), lambda b, i: (b, 0, 0))

    def body(res_ref, y_ref, w_ref, g_ref, h_ref):
        h_ref[...] = res_ref[...] + _post_f(y_ref[...], w_ref[...], g_ref[0])

    return pl.pallas_call(body, name=name, grid=grid, in_specs=[row, row, vec, bvec], out_specs=row,
                          out_shape=jax.ShapeDtypeStruct((t, d), F32))(res, y, w, g)


def _post_loss(res, y, w, g, tgt, name, ts=512):
    t, d = y.shape
    bsz = g.shape[0]
    nt, grid = _tok_grid(t, bsz, ts)
    row = _vspec((ts, d), lambda b, i: (b * nt + i, 0))
    vec = _vspec((1, d), lambda b, i: (0, 0))
    bvec = _vspec((1, 1, d), lambda b, i: (b, 0, 0))
    acc = _vspec((1, d), lambda b, i: (0, 0))

    def body(res_ref, y_ref, w_ref, g_ref, tgt_ref, dh_ref, sq_ref):
        b, i = pl.program_id(0), pl.program_id(1)
        e = res_ref[...] + _post_f(y_ref[...], w_ref[...], g_ref[0]) - tgt_ref[...]
        dh_ref[...] = e * (1.0 / d)

        @pl.when((b == 0) & (i == 0))
        def _():
            sq_ref[...] = jnp.zeros_like(sq_ref)

        sq_ref[...] += jnp.sum(e * e, axis=0, keepdims=True) * (1.0 / d)

    return pl.pallas_call(
        body, name=name, grid=grid, in_specs=[row, row, vec, bvec, row], out_specs=[row, acc],
        out_shape=[jax.ShapeDtypeStruct((t, d), F32), jax.ShapeDtypeStruct((1, d), F32)],
    )(res, y, w, g, tgt)


def _post_bwd(dh, y, w, g, name, ts=512):
    t, d = y.shape
    bsz = g.shape[0]
    nt, grid = _tok_grid(t, bsz, ts)
    row = _vspec((ts, d), lambda b, i: (b * nt + i, 0))
    vec = _vspec((1, d), lambda b, i: (0, 0))
    bvec = _vspec((1, 1, d), lambda b, i: (b, 0, 0))

    def body(dh_ref, y_ref, w_ref, g_ref, dy_ref, dw_ref, dg_ref):
        b, i = pl.program_id(0), pl.program_id(1)
        _, vjp = jax.vjp(_post_f, y_ref[...], w_ref[...], g_ref[0])
        dy, dw, dg = vjp(dh_ref[...])
        dy_ref[...] = dy.astype(BF16)

        @pl.when((b == 0) & (i == 0))
        def _():
            dw_ref[...] = jnp.zeros_like(dw_ref)

        @pl.when(i == 0)
        def _():
            dg_ref[...] = jnp.zeros_like(dg_ref)

        dw_ref[...] += dw
        dg_ref[0] += dg

    return pl.pallas_call(
        body, name=name, grid=grid, in_specs=[row, row, vec, bvec], out_specs=[row, vec, bvec],
        out_shape=[jax.ShapeDtypeStruct((t, d), BF16), jax.ShapeDtypeStruct((1, d), F32),
                   jax.ShapeDtypeStruct((bsz, 1, d), F32)],
    )(dh, y, w, g)


def _merge_f(ga, gd, ya, yd):
    return jax.nn.sigmoid(ga) * ya + jax.nn.sigmoid(gd) * yd


_MW = 512


def _merge_fwd(proj, ya, yd, ts=512):
    t, d = ya.shape
    blk = _vspec((ts, _MW), lambda i, j: (i, j))
    ga = _vspec((ts, _MW), lambda i, j: (i, C_GA // _MW + j))
    gd = _vspec((ts, _MW), lambda i, j: (i, C_GD // _MW + j))

    def body(ga_ref, gd_ref, ya_ref, yd_ref, o_ref):
        o_ref[...] = _merge_f(ga_ref[...], gd_ref[...], ya_ref[...], yd_ref[...]).astype(BF16)

    return pl.pallas_call(body, name="merge_fwd", grid=(t // ts, d // _MW), in_specs=[ga, gd, blk, blk], out_specs=blk,
                          out_shape=jax.ShapeDtypeStruct((t, d), BF16))(proj, proj, ya, yd)


def _merge_bwd(proj, ya, yd, dm, ts=512):
    t, d = ya.shape
    blk = _vspec((ts, _MW), lambda i, j: (i, j))
    ga = _vspec((ts, _MW), lambda i, j: (i, C_GA // _MW + j))
    gd = _vspec((ts, _MW), lambda i, j: (i, C_GD // _MW + j))

    def body(ga_ref, gd_ref, ya_ref, yd_ref, dm_ref, dga_ref, dgd_ref, dya_ref, dyd_ref):
        _, vjp = jax.vjp(_merge_f, ga_ref[...], gd_ref[...], ya_ref[...], yd_ref[...])
        dga, dgd, dya, dyd = vjp(dm_ref[...])
        dga_ref[...] = dga.astype(BF16)
        dgd_ref[...] = dgd.astype(BF16)
        dya_ref[...] = dya.astype(BF16)
        dyd_ref[...] = dyd.astype(BF16)

    o = jax.ShapeDtypeStruct((t, d), BF16)
    return pl.pallas_call(body, name="merge_bwd", grid=(t // ts, d // _MW), in_specs=[ga, gd, blk, blk, blk],
                          out_specs=[blk] * 4, out_shape=[o] * 4)(proj, proj, ya, yd, dm)


def _shift_down(x, s):
    if s == 0:
        return x
    r = lax.broadcasted_iota(jnp.int32, x.shape, 0)
    return jnp.where(r >= s, pltpu.roll(x, s, 0), 0.0)


def _shift_up(x, s):
    if s == 0:
        return x
    n = x.shape[0]
    r = lax.broadcasted_iota(jnp.int32, x.shape, 0)
    return jnp.where(r < n - s, pltpu.roll(x, n - s, 0), 0.0)


def _conv_fwd(x, w, k):
    out = None
    for j in range(k):
        term = w[j:j + 1, :] * _shift_down(x, k - 1 - j)
        out = term if out is None else out + term
    return out


def _conv_bwd(x, w, dc, k):
    dx = None
    dws = []
    for j in range(k):
        s = k - 1 - j
        term = w[j:j + 1, :] * _shift_up(dc, s)
        dx = term if dx is None else dx + term
        dws.append(jnp.sum(dc * _shift_down(x, s), axis=0, keepdims=True))
    return dx, jnp.concatenate(dws, axis=0)


def _geglu_f(gate, val):
    return jax.nn.gelu(gate, approximate=True) * val


_FW = 256


def _ffn_act_fwd(up, conv_w, bsz):
    t = up.shape[0]
    s = t // bsz
    nj = D_FF // _FW
    xg = _vspec((s, _FW), lambda b, j: (b, j))
    xv = _vspec((s, _FW), lambda b, j: (b, nj + j))
    wg = _vspec((FFN_CONV, _FW), lambda b, j: (0, j))
    wv = _vspec((FFN_CONV, _FW), lambda b, j: (0, nj + j))

    def body(xg_ref, xv_ref, wg_ref, wv_ref, o_ref):
        gate = _conv_fwd(xg_ref[...], wg_ref[...], FFN_CONV)
        val = _conv_fwd(xv_ref[...], wv_ref[...], FFN_CONV)
        o_ref[...] = _geglu_f(gate, val).astype(BF16)

    return pl.pallas_call(body, name="ffn_act_fwd", grid=(bsz, nj), in_specs=[xg, xv, wg, wv],
                          out_specs=_vspec((s, _FW), lambda b, j: (b, j)),
                          out_shape=jax.ShapeDtypeStruct((t, D_FF), BF16))(up, up, conv_w, conv_w)


def _ffn_act_bwd(up, conv_w, dact, bsz):
    t = up.shape[0]
    s = t // bsz
    nj = D_FF // _FW
    xg = _vspec((s, _FW), lambda j, b: (b, j))
    xv = _vspec((s, _FW), lambda j, b: (b, nj + j))
    wg = _vspec((FFN_CONV, _FW), lambda j, b: (0, j))
    wv = _vspec((FFN_CONV, _FW), lambda j, b: (0, nj + j))
    da = _vspec((s, _FW), lambda j, b: (b, j))
    dwo = _vspec((FFN_CONV, _FW), lambda j, b: (0, j))

    def body(xg_ref, xv_ref, wg_ref, wv_ref, da_ref, dxg_ref, dxv_ref, dwg_ref, dwv_ref):
        b = pl.program_id(1)
        xg_, xv_, wg_, wv_ = xg_ref[...], xv_ref[...], wg_ref[...], wv_ref[...]
        gate = _conv_fwd(xg_, wg_, FFN_CONV)
        val = _conv_fwd(xv_, wv_, FFN_CONV)
        _, vjp = jax.vjp(_geglu_f, gate, val)
        dgate, dval = vjp(da_ref[...])
        dxg, dwg = _conv_bwd(xg_, wg_, dgate, FFN_CONV)
        dxv, dwv = _conv_bwd(xv_, wv_, dval, FFN_CONV)
        dxg_ref[...] = dxg.astype(BF16)
        dxv_ref[...] = dxv.astype(BF16)

        @pl.when(b == 0)
        def _():
            dwg_ref[...] = jnp.zeros_like(dwg_ref)
            dwv_ref[...] = jnp.zeros_like(dwv_ref)

        dwg_ref[...] += dwg
        dwv_ref[...] += dwv

    return pl.pallas_call(
        body, name="ffn_act_bwd", grid=(nj, bsz), in_specs=[xg, xv, wg, wv, da], out_specs=[da, da, dwo, dwo],
        out_shape=[jax.ShapeDtypeStruct((t, D_FF), BF16), jax.ShapeDtypeStruct((t, D_FF), BF16),
                   jax.ShapeDtypeStruct((FFN_CONV, D_FF), F32), jax.ShapeDtypeStruct((FFN_CONV, D_FF), F32)],
    )(up, up, conv_w, conv_w, dact)


def _bucket_table():
    qi = np.arange(WINDOW)[:, None]
    kj = np.arange(2 * WINDOW)[None, :]
    dist = WINDOW + qi - kj
    dc = np.maximum(dist, 0)
    max_exact = REL_BUCKETS // 2
    scaled = np.log(np.maximum(dc, 1).astype(np.float32) / np.float32(max_exact)) / np.float32(math.log(REL_MAX_DIST / max_exact))
    large = max_exact + (scaled.astype(np.float32) * np.float32(REL_BUCKETS - max_exact)).astype(np.int32)
    large = np.minimum(large, REL_BUCKETS - 1)
    bucket = np.where(dc < max_exact, dc, large).astype(np.int32)
    in_band = ((dist >= 0) & (dist < WINDOW)).astype(np.int32)
    return bucket, in_band


def _bias_build(rel_bias):
    bucket, _ = _bucket_table()

    def body(rb_ref, idx_ref, o_ref):
        h = pl.program_id(0)
        idx = idx_ref[...]
        acc = jnp.zeros(idx.shape, F32)
        for r in range(REL_BUCKETS):
            acc = jnp.where(idx == r, rb_ref[r, h], acc)
        o_ref[0] = acc

    return pl.pallas_call(
        body, name="bias_build", grid=(AQ_HEADS,),
        in_specs=[pl.BlockSpec(memory_space=pltpu.SMEM), _vspec((WINDOW, 2 * WINDOW), lambda h: (0, 0))],
        out_specs=_vspec((1, WINDOW, 2 * WINDOW), lambda h: (h, 0, 0)),
        out_shape=jax.ShapeDtypeStruct((AQ_HEADS, WINDOW, 2 * WINDOW), F32),
    )(rel_bias, jnp.asarray(bucket))


def _bias_grad(dbias):
    bucket, _ = _bucket_table()

    def body(db_ref, idx_ref, o_ref):
        idx = idx_ref[...]
        db = db_ref[0]
        lane = lax.broadcasted_iota(jnp.int32, (1, LANES), 1)
        acc = jnp.zeros((1, LANES), F32)
        for r in range(REL_BUCKETS):
            s = jnp.sum(jnp.sum(jnp.where(idx == r, db, 0.0), axis=1, keepdims=True), axis=0, keepdims=True)
            acc = jnp.where(lane == r, s, acc)
        o_ref[0] = acc

    return pl.pallas_call(
        body, name="bias_grad", grid=(AQ_HEADS,),
        in_specs=[_vspec((1, WINDOW, 2 * WINDOW), lambda h: (h, 0, 0)), _vspec((WINDOW, 2 * WINDOW), lambda h: (0, 0))],
        out_specs=_vspec((1, 1, LANES), lambda h: (h, 0, 0)),
        out_shape=jax.ShapeDtypeStruct((AQ_HEADS, 1, LANES), F32),
    )(dbias, jnp.asarray(bucket))


def _attn_mask(n):
    qi = lax.broadcasted_iota(jnp.int32, (WINDOW, 2 * WINDOW), 0)
    kj = lax.broadcasted_iota(jnp.int32, (WINDOW, 2 * WINDOW), 1)
    dist = WINDOW + qi - kj
    band = (dist >= 0) & (dist < WINDOW)
    return band & ((kj >= WINDOW) | (n > 0))


def _attn_probs(qk, bias, sink, mask):
    s = jnp.where(mask, qk * (A_HD ** -0.5) + bias, NEG_INF)
    m = jnp.maximum(jnp.max(s, axis=-1, keepdims=True), sink)
    p = jnp.exp(s - m)
    es = jnp.exp(sink - m)
    inv = 1.0 / (jnp.sum(p, axis=-1, keepdims=True) + es)
    return p * inv, es * inv


def _attn_fwd(proj, bias, sinks, bsz):
    t = proj.shape[0]
    s = t // bsz
    nb = s // WINDOW
    grp = AQ_HEADS // AKV_HEADS

    def body(q_ref, k_ref, v_ref, bias_ref, sink_ref, y_ref, kp_ref, vp_ref):
        kp_ref[0:WINDOW, :] = jnp.zeros((WINDOW, LANES), BF16)
        vp_ref[0:WINDOW, :] = jnp.zeros((WINDOW, LANES), BF16)
        kp_ref[WINDOW:, :] = k_ref[...].astype(BF16)
        vp_ref[WINDOW:, :] = v_ref[...].astype(BF16)

        def blk(n, carry):
            r0 = pl.multiple_of(n * WINDOW, WINDOW)
            mask = _attn_mask(n)
            kband = kp_ref[pl.ds(r0, 2 * WINDOW), :]
            vband = vp_ref[pl.ds(r0, 2 * WINDOW), :]
            qb = q_ref[pl.ds(r0, WINDOW), :].astype(BF16)
            heads = range(AQ_HEADS)
            hsl = lambda h: slice(h * A_HD, (h + 1) * A_HD)
            kbs = [kband[:, hsl(kv)] for kv in range(AKV_HEADS)]
            vbs = [vband[:, hsl(kv)] for kv in range(AKV_HEADS)]
            qks = [lax.dot_general(qb[:, hsl(h)], kbs[h // grp], _NT, preferred_element_type=F32) for h in heads]
            probs = [_attn_probs(qks[h], bias_ref[h], sink_ref[0, h], mask)[0] for h in heads]
            outs = [jnp.dot(probs[h].astype(BF16), vbs[h // grp], preferred_element_type=F32) for h in heads]
            y_ref[pl.ds(r0, WINDOW), :] = jnp.concatenate(outs, axis=1).astype(BF16)
            return carry

        lax.fori_loop(0, nb, blk, 0)

    return pl.pallas_call(
        body, name="attn_fwd", grid=(bsz,),
        in_specs=[_vspec((s, AQ), lambda b: (b, C_Q // AQ)), _vspec((s, AKV), lambda b: (b, C_K // AKV)),
                  _vspec((s, AKV), lambda b: (b, C_V // AKV)),
                  _vspec((AQ_HEADS, WINDOW, 2 * WINDOW), lambda b: (0, 0, 0)), pl.BlockSpec(memory_space=pltpu.SMEM)],
        out_specs=_vspec((s, AQ), lambda b: (b, 0)), out_shape=jax.ShapeDtypeStruct((t, AQ), BF16),
        scratch_shapes=[pltpu.VMEM((s + WINDOW, LANES), BF16), pltpu.VMEM((s + WINDOW, LANES), BF16)],
    )(proj, proj, proj, bias, sinks)


def _attn_bwd(proj, bias, sinks, dy, bsz, plug=None):
    t = proj.shape[0]
    s = t // bsz
    nb = s // WINDOW
    grp = AQ_HEADS // AKV_HEADS
    scale = A_HD ** -0.5

    def body(q_ref, k_ref, v_ref, bias_ref, sink_ref, dy_ref, dq_ref, dk_ref, dv_ref, dbias_ref, dsink_ref,
             kp_ref, vp_ref, dkp_ref, dvp_ref):
        b = pl.program_id(0)
        kp_ref[0:WINDOW, :] = jnp.zeros((WINDOW, LANES), BF16)
        vp_ref[0:WINDOW, :] = jnp.zeros((WINDOW, LANES), BF16)
        kp_ref[WINDOW:, :] = k_ref[...].astype(BF16)
        vp_ref[WINDOW:, :] = v_ref[...].astype(BF16)
        dkp_ref[...] = jnp.zeros_like(dkp_ref)
        dvp_ref[...] = jnp.zeros_like(dvp_ref)

        @pl.when(b == 0)
        def _():
            dbias_ref[...] = jnp.zeros_like(dbias_ref)
            dsink_ref[...] = jnp.zeros_like(dsink_ref)

        def blk(n, carry):
            r0 = pl.multiple_of(n * WINDOW, WINDOW)
            mask = _attn_mask(n)
            kband = kp_ref[pl.ds(r0, 2 * WINDOW), :]
            vband = vp_ref[pl.ds(r0, 2 * WINDOW), :]
            qb = q_ref[pl.ds(r0, WINDOW), :].astype(BF16)
            dyb = dy_ref[pl.ds(r0, WINDOW), :].astype(BF16)
            heads = range(AQ_HEADS)
            hsl = lambda h: slice(h * A_HD, (h + 1) * A_HD)
            kbs = [kband[:, hsl(kv)] for kv in range(AKV_HEADS)]
            vbs = [vband[:, hsl(kv)] for kv in range(AKV_HEADS)]
            qhs = [qb[:, hsl(h)] for h in heads]
            dyhs = [dyb[:, hsl(h)] for h in heads]
            qks = [lax.dot_general(qhs[h], kbs[h // grp], _NT, preferred_element_type=F32) for h in heads]
            dprobs = [lax.dot_general(dyhs[h], vbs[h // grp], _NT, preferred_element_type=F32) for h in heads]
            pbs, dsbs = [], []
            for h in heads:
                probs, psink = _attn_probs(qks[h], bias_ref[h], sink_ref[0, h], mask)
                rowdot = jnp.sum(probs * dprobs[h], axis=-1, keepdims=True)
                ds = probs * (dprobs[h] - rowdot)
                dbias_ref[h] += ds
                dsink_ref[h] += jnp.sum(-psink * rowdot, axis=0, keepdims=True) + jnp.zeros((1, LANES), F32)
                pbs.append(probs.astype(BF16))
                dsbs.append(ds.astype(BF16))
            dvhs = [lax.dot_general(pbs[h], dyhs[h], _TN, preferred_element_type=F32) for h in heads]
            dqs = [jnp.dot(dsbs[h], kbs[h // grp], preferred_element_type=F32) * scale for h in heads]
            dkhs = [lax.dot_general(dsbs[h], qhs[h], _TN, preferred_element_type=F32) * scale for h in heads]
            dks = [sum(dkhs[kv * grp + 1:(kv + 1) * grp], dkhs[kv * grp]) for kv in range(AKV_HEADS)]
            dvs = [sum(dvhs[kv * grp + 1:(kv + 1) * grp], dvhs[kv * grp]) for kv in range(AKV_HEADS)]
            dq_ref[pl.ds(r0, WINDOW), :] = jnp.concatenate(dqs, axis=1).astype(BF16)
            dkp_ref[pl.ds(r0, 2 * WINDOW), :] += jnp.concatenate(dks, axis=1)
            dvp_ref[pl.ds(r0, 2 * WINDOW), :] += jnp.concatenate(dvs, axis=1)
            return carry

        lax.fori_loop(0, nb, blk, 0)
        dk_ref[...] = dkp_ref[WINDOW:, :].astype(BF16)
        dv_ref[...] = dvp_ref[WINDOW:, :].astype(BF16)

    kvs = jax.ShapeDtypeStruct((t, AKV), BF16)
    outs, extra = _plugged_call(
        body, plug, _grid_ends((bsz,)), (proj, proj, proj, bias, sinks, dy), name="attn_bwd", grid=(bsz,),
        in_specs=[_vspec((s, AQ), lambda b: (b, C_Q // AQ)), _vspec((s, AKV), lambda b: (b, C_K // AKV)),
                  _vspec((s, AKV), lambda b: (b, C_V // AKV)),
                  _vspec((AQ_HEADS, WINDOW, 2 * WINDOW), lambda b: (0, 0, 0)), pl.BlockSpec(memory_space=pltpu.SMEM),
                  _vspec((s, AQ), lambda b: (b, 0))],
        out_specs=[_vspec((s, AQ), lambda b: (b, 0)), _vspec((s, AKV), lambda b: (b, 0)), _vspec((s, AKV), lambda b: (b, 0)),
                   _vspec((AQ_HEADS, WINDOW, 2 * WINDOW), lambda b: (0, 0, 0)), _vspec((AQ_HEADS, 1, LANES), lambda b: (0, 0, 0))],
        out_shape=[jax.ShapeDtypeStruct((t, AQ), BF16), kvs, kvs,
                   jax.ShapeDtypeStruct((AQ_HEADS, WINDOW, 2 * WINDOW), F32), jax.ShapeDtypeStruct((AQ_HEADS, 1, LANES), F32)],
        scratch_shapes=[pltpu.VMEM((s + WINDOW, LANES), BF16), pltpu.VMEM((s + WINDOW, LANES), BF16),
                        pltpu.VMEM((s + WINDOW, LANES), F32), pltpu.VMEM((s + WINDOW, LANES), F32)])
    return (*outs, extra)


def _dn_act_f(c, is_qk):
    a = jax.nn.silu(c)
    outs = []
    for h in range(DN_HEADS):
        ah = a[:, h * DN_HD:(h + 1) * DN_HD]
        nh = ah * lax.rsqrt(jnp.sum(ah * ah, axis=-1, keepdims=True) + L2_EPS)
        outs.append(jnp.where(is_qk, nh, ah))
    return jnp.concatenate(outs, axis=1)


def _dn_prep_fwd(proj, conv_w, bsz):
    t = proj.shape[0]
    s = t // bsz
    blk = _vspec((s, DN), lambda b, j: (b, j))
    wsp = _vspec((DN_CONV, DN), lambda b, j: (0, j))

    def body(x_ref, w_ref, o_ref):
        j = pl.program_id(1)
        o_ref[...] = _dn_act_f(_conv_fwd(x_ref[...], w_ref[...], DN_CONV), j < 2)

    return pl.pallas_call(body, name="dn_prep_fwd", grid=(bsz, 3), in_specs=[blk, wsp], out_specs=blk,
                          out_shape=jax.ShapeDtypeStruct((t, 3 * DN), F32))(proj, conv_w)


def _dn_prep_bwd(proj, conv_w, dqkvn, bsz):
    t = proj.shape[0]
    s = t // bsz
    blk = _vspec((s, DN), lambda j, b: (b, j))
    wsp = _vspec((DN_CONV, DN), lambda j, b: (0, j))

    def body(x_ref, w_ref, d_ref, dx_ref, dw_ref):
        j, b = pl.program_id(0), pl.program_id(1)
        x, w = x_ref[...], w_ref[...]
        c = _conv_fwd(x, w, DN_CONV)
        _, vjp = jax.vjp(lambda cc: _dn_act_f(cc, j < 2), c)
        (dc,) = vjp(d_ref[0])
        dx, dw = _conv_bwd(x, w, dc, DN_CONV)
        dx_ref[...] = dx.astype(BF16)

        @pl.when(b == 0)
        def _():
            dw_ref[...] = jnp.zeros_like(dw_ref)

        dw_ref[...] += dw

    return pl.pallas_call(
        body, name="dn_prep_bwd", grid=(3, bsz),
        in_specs=[blk, wsp, _vspec((1, s, DN), lambda j, b: (j, b, 0))], out_specs=[blk, wsp],
        out_shape=[jax.ShapeDtypeStruct((t, 3 * DN), BF16), jax.ShapeDtypeStruct((DN_CONV, 3 * DN), F32)],
    )(proj, conv_w, dqkvn)


def _bg_f(x, alog, dt):
    lane = lax.broadcasted_iota(jnp.int32, x.shape, 1)
    beta = jax.nn.sigmoid(x)
    g = -jnp.exp(alog) * jax.nn.softplus(x + dt)
    return jnp.where(lane < DN_HEADS, beta, jnp.where(lane < 2 * DN_HEADS, g, 0.0))


def _bg_fwd(proj, alog, dt, bsz):
    t = proj.shape[0]
    s = t // bsz
    vec = _vspec((1, LANES), lambda b: (0, 0))

    def body(x_ref, a_ref, d_ref, o_ref):
        o_ref[...] = _bg_f(x_ref[...], a_ref[...], d_ref[...])

    return pl.pallas_call(body, name="bg_fwd", grid=(bsz,), in_specs=[_vspec((s, LANES), lambda b: (b, C_BD // LANES)), vec, vec],
                          out_specs=_vspec((s, LANES), lambda b: (b, 0)), out_shape=jax.ShapeDtypeStruct((t, LANES), F32))(proj, alog, dt)


def _bg_bwd(proj, alog, dt, dbg4, bsz):
    t = proj.shape[0]
    s = t // bsz
    vec = _vspec((1, LANES), lambda b: (0, 0))

    def body(x_ref, a_ref, d_ref, g4_ref, dx_ref, da_ref, dd_ref):
        b = pl.program_id(0)
        lane = lax.broadcasted_iota(jnp.int32, (s, LANES), 1)
        dbg = jnp.zeros((s, LANES), F32)
        for h in range(DN_HEADS):
            gh = g4_ref[:, h * DN_HD:(h + 1) * DN_HD]
            dbg = jnp.where(lane == h, gh[:, 0:1], dbg)
            dbg = jnp.where(lane == DN_HEADS + h, gh[:, 1:2], dbg)
        _, vjp = jax.vjp(_bg_f, x_ref[...], a_ref[...], d_ref[...])
        dx, da, dd = vjp(dbg)
        dx_ref[...] = dx.astype(BF16)

        @pl.when(b == 0)
        def _():
            da_ref[...] = jnp.zeros_like(da_ref)
            dd_ref[...] = jnp.zeros_like(dd_ref)

        da_ref[...] += da
        dd_ref[...] += dd

    return pl.pallas_call(
        body, name="bg_bwd", grid=(bsz,),
        in_specs=[_vspec((s, LANES), lambda b: (b, C_BD // LANES)), vec, vec, _vspec((s, DN), lambda b: (b, 0))],
        out_specs=[_vspec((s, LANES), lambda b: (b, 0)), vec, vec],
        out_shape=[jax.ShapeDtypeStruct((t, LANES), BF16), jax.ShapeDtypeStruct((1, LANES), F32), jax.ShapeDtypeStruct((1, LANES), F32)],
    )(proj, alog, dt, dbg4)


def _dn_out_f(o, z, w):
    outs = []
    for h in range(DN_HEADS):
        sl = slice(h * DN_HD, (h + 1) * DN_HD)
        outs.append(_rms(o[:, sl], w) * jax.nn.silu(z[:, sl]))
    return jnp.concatenate(outs, axis=1)


def _dn_out_fwd(o, proj, w, ts=512):
    t = o.shape[0]
    blk = _vspec((ts, DN), lambda i: (i, 0))
    zsp = _vspec((ts, DN), lambda i: (i, C_DZ // DN))
    vec = _vspec((1, DN_HD), lambda i: (0, 0))

    def body(o_ref, z_ref, w_ref, y_ref):
        y_ref[...] = _dn_out_f(o_ref[...], z_ref[...], w_ref[...]).astype(BF16)

    return pl.pallas_call(body, name="dn_out_fwd", grid=(t // ts,), in_specs=[blk, zsp, vec], out_specs=blk,
                          out_shape=jax.ShapeDtypeStruct((t, DN), BF16))(o, proj, w)


def _dn_out_bwd(o, proj, w, dy, ts=512):
    t = o.shape[0]
    blk = _vspec((ts, DN), lambda i: (i, 0))
    zsp = _vspec((ts, DN), lambda i: (i, C_DZ // DN))
    vec = _vspec((1, DN_HD), lambda i: (0, 0))

    def body(o_ref, z_ref, w_ref, dy_ref, do_ref, dz_ref, dw_ref):
        i = pl.program_id(0)
        _, vjp = jax.vjp(_dn_out_f, o_ref[...], z_ref[...], w_ref[...])
        do, dz, dw = vjp(dy_ref[...])
        do_ref[...] = do
        dz_ref[...] = dz.astype(BF16)

        @pl.when(i == 0)
        def _():
            dw_ref[...] = jnp.zeros_like(dw_ref)

        dw_ref[...] += dw

    return pl.pallas_call(
        body, name="dn_out_bwd", grid=(t // ts,), in_specs=[blk, zsp, vec, blk], out_specs=[blk, blk, vec],
        out_shape=[jax.ShapeDtypeStruct((t, DN), F32), jax.ShapeDtypeStruct((t, DN), BF16), jax.ShapeDtypeStruct((1, DN_HD), F32)],
    )(o, proj, w, dy)


_C = DN_CHUNK


def _dot(a, b, dims):
    return lax.dot_general(a.astype(BF16), b.astype(BF16), dims, preferred_element_type=F32)


def _split(a):
    hi = a.astype(BF16)
    return hi, (a - hi.astype(F32)).astype(BF16)


def _dot3(a, b, dims):
    (ah, al), (bh, bl) = (a if isinstance(a, tuple) else _split(a)), (b if isinstance(b, tuple) else _split(b))
    mm = lambda x, y: lax.dot_general(x, y, dims, preferred_element_type=F32)
    return mm(ah, bh) + (mm(ah, bl) + mm(al, bh))


_NN = (((1,), (0,)), ((), ()))
_NT = (((1,), (1,)), ((), ()))
_TN = (((0,), (0,)), ((), ()))


_SUB = 8


def _tri_inverses(ls, lts):
    ri8 = lax.broadcasted_iota(jnp.int32, (_SUB, _C), 0)
    ci8 = lax.broadcasted_iota(jnp.int32, (_SUB, _C), 1)
    nblk = _C // _SUB
    ts = []
    for lt in lts:
        blocks = [jnp.where(ci8 == ri8 + _SUB * b, 1.0, 0.0).astype(F32) for b in range(nblk)]
        for r in range(1, _SUB):
            for b in range(nblk):
                coef = lt[_SUB * b:_SUB * (b + 1), _SUB * b + r:_SUB * b + r + 1]
                row = jnp.sum(coef * blocks[b], axis=0, keepdims=True)
                blocks[b] = jnp.where(ri8 == r, blocks[b] - row, blocks[b])
        ts.append(jnp.concatenate(blocks, axis=0))
    ri = lax.broadcasted_iota(jnp.int32, (_C, _C), 0)
    ci = lax.broadcasted_iota(jnp.int32, (_C, _C), 1)
    s = _SUB
    while s < _C:
        shift = s.bit_length()
        quad = ((ri >> shift) == (ci >> shift)) & ((ri & s) != 0) & ((ci & s) == 0)
        offs = [jnp.where(quad, l, 0.0) for l in ls]
        tsp = [_split(t) for t in ts]
        left = [_dot3(tp, off, _NN) for tp, off in zip(tsp, offs)]
        ts = [t - _dot3(lo, tp, _NN) for t, lo, tp in zip(ts, left, tsp)]
        s *= 2
    return ts


_SEG = 512
_HEADS = tuple(range(DN_HEADS))


def _hsl(hh):
    return slice(hh * DN_HD, (hh + 1) * DN_HD)


def _chunk_specs(bsz, nseg, reverse):
    seg = (lambda i: nseg - 1 - i) if reverse else (lambda i: i)
    ncs = _SEG // _C
    col = lambda off: _vspec((bsz, _SEG, DN), lambda i: (0, seg(i), off))
    return (col, _vspec((bsz, _SEG, LANES), lambda i: (0, seg(i), 0)),
            _vspec((bsz, DN_HEADS, ncs, _C), lambda i: (0, 0, seg(i), 0)),
            _vspec((bsz, DN_HEADS, ncs, DN_HD, DN_HD), lambda i: (0, 0, seg(i), 0, 0)))


def _chunk_pre(q_ref, k_ref, v_ref, bg_ref, gr_ref, c, bb, hh):
    r0 = pl.multiple_of(c * _C, _C)
    ri = lax.broadcasted_iota(jnp.int32, (_C, _C), 0)
    ci = lax.broadcasted_iota(jnp.int32, (_C, _C), 1)
    q = q_ref[bb, pl.ds(r0, _C), _hsl(hh)] * (DN_HD ** -0.5)
    k = k_ref[bb, pl.ds(r0, _C), _hsl(hh)]
    v = v_ref[bb, pl.ds(r0, _C), _hsl(hh)]
    bgc = bg_ref[bb, pl.ds(r0, _C), :]
    beta = bgc[:, hh:hh + 1]
    g_col = bgc[:, DN_HEADS + hh:DN_HEADS + hh + 1]
    g_row = gr_ref[bb, hh, pl.ds(c, 1), :]
    gc_col = jnp.sum(jnp.where(ri >= ci, g_row, 0.0), axis=1, keepdims=True)
    gc_row = jnp.sum(jnp.where(ri <= ci, g_col, 0.0), axis=0, keepdims=True)
    gc_last = jnp.sum(g_col, axis=0, keepdims=True)
    diff = gc_col - gc_row
    decay = jnp.where(ri >= ci, jnp.exp(jnp.where(ri >= ci, diff, 0.0)), 0.0)
    diff_t = gc_row - gc_col
    decay_t = jnp.where(ri <= ci, jnp.exp(jnp.where(ri <= ci, diff_t, 0.0)), 0.0)
    eg = jnp.exp(gc_col)
    et = jnp.exp(gc_last - gc_col)
    gl = jnp.exp(gc_last)
    kb = k * beta
    vb = v * beta
    return dict(r0=r0, bb=bb, hh=hh, q=q, k=k, v=v, beta=beta, decay=decay, decay_t=decay_t, eg=eg, et=et, gl=gl, kb=kb, vb=vb,
                ri=ri, ci=ci)


def _chunk_solve(ms, with_transposes):
    for m in ms:
        m["kk_t"] = _dot(m["k"], m["kb"], _NT)
        m["qk"] = _dot(m["q"], m["k"], _NT)
        m["kk"] = _dot(m["kb"], m["k"], _NT)
        if with_transposes:
            m["qk_t"] = _dot(m["k"], m["q"], _NT)
    tms = _tri_inverses([jnp.where(m["ri"] > m["ci"], m["kk"] * m["decay"], 0.0) for m in ms],
                        [jnp.where(m["ri"] < m["ci"], m["kk_t"] * m["decay_t"], 0.0) for m in ms])
    for m, tm in zip(ms, tms):
        m["tm"] = tm
    for m in ms:
        rhs = jnp.concatenate([m["vb"], m["kb"] * m["eg"]], axis=1)
        m["tm"] = _split(m["tm"])
        m["sol"] = _dot3(m["tm"], rhs, _NN)
        m["intra"] = jnp.where(m["ri"] >= m["ci"], m["qk"] * m["decay"], 0.0)


def _dn_chunk_fwd(qkvn, bg, g_rows, bsz, plug=None):
    t = qkvn.shape[0]
    s = t // bsz
    nc, nseg = s // _C, s // _SEG
    pairs = [(bb, hh) for bb in range(bsz) for hh in _HEADS]

    def body(q_ref, k_ref, v_ref, bg_ref, gr_ref, o_ref, st_ref, s_ref):
        @pl.when(pl.program_id(0) == 0)
        def _():
            s_ref[...] = jnp.zeros_like(s_ref)

        def chunk(c, carry):
            ms = [_chunk_pre(q_ref, k_ref, v_ref, bg_ref, gr_ref, c, bb, hh) for bb, hh in pairs]
            _chunk_solve(ms, False)
            sts = [s_ref[i] for i in range(len(pairs))]
            for (bb, hh), st in zip(pairs, sts):
                st_ref[bb, hh, c] = st
            ws = [_dot(m["sol"][:, DN_HD:], st, _NN) for m, st in zip(ms, sts)]
            qs = [_dot(m["q"] * m["eg"], st, _NN) for m, st in zip(ms, sts)]
            v_new = [m["sol"][:, :DN_HD] - a for m, a in zip(ms, ws)]
            iv = [_dot(m["intra"], vn, _NN) for m, vn in zip(ms, v_new)]
            upd = [_dot(m["k"] * m["et"], vn, _TN) for m, vn in zip(ms, v_new)]
            for i, (bb, hh) in enumerate(pairs):
                s_ref[i] = sts[i] * ms[i]["gl"] + upd[i]
                o_ref[bb, pl.ds(ms[i]["r0"], _C), _hsl(hh)] = qs[i] + iv[i]
            return carry

        lax.fori_loop(0, _SEG // _C, chunk, 0)

    col, bgs, grs, sts_spec = _chunk_specs(bsz, nseg, False)
    q3, bg3 = qkvn.reshape(bsz, s, 3 * DN), bg.reshape(bsz, s, LANES)
    (o, states), extra = _plugged_call(
        body, plug, _grid_ends((nseg,)), (q3, q3, q3, bg3, g_rows), name="dn_chunk_fwd", grid=(nseg,),
        in_specs=[col(0), col(1), col(2), bgs, grs], out_specs=[col(0), sts_spec],
        out_shape=[jax.ShapeDtypeStruct((bsz, s, DN), F32), jax.ShapeDtypeStruct((bsz, DN_HEADS, nc, DN_HD, DN_HD), F32)],
        scratch_shapes=[pltpu.VMEM((bsz * DN_HEADS, DN_HD, DN_HD), F32)])
    return o.reshape(t, DN), states, extra


def _dn_chunk_bwd(qkvn, bg, g_rows, states, do, bsz, plug=None):
    t = qkvn.shape[0]
    s = t // bsz
    nc, nseg = s // _C, s // _SEG
    pairs = [(bb, hh) for bb in range(bsz) for hh in _HEADS]

    def body(q_ref, k_ref, v_ref, bg_ref, gr_ref, st_ref, do_ref, dqkv_ref, dbg_ref, ds_ref):
        @pl.when(pl.program_id(0) == 0)
        def _():
            ds_ref[...] = jnp.zeros_like(ds_ref)

        def chunk(cc, carry):
            c = _SEG // _C - 1 - cc
            ms = [_chunk_pre(q_ref, k_ref, v_ref, bg_ref, gr_ref, c, bb, hh) for bb, hh in pairs]
            _chunk_solve(ms, True)
            ri, ci = ms[0]["ri"], ms[0]["ci"]
            for i, m in enumerate(ms):
                m["st"] = st_ref[m["bb"], m["hh"], c]
                m["ds_out"] = ds_ref[i]
                m["do"] = do_ref[m["bb"], pl.ds(m["r0"], _C), _hsl(m["hh"])]
                m["w"] = m["sol"][:, DN_HD:]
            for m in ms:
                m["v_new"] = m["sol"][:, :DN_HD] - _dot(m["w"], m["st"], _NN)
            for m in ms:
                m["q_dec"], m["k_tail"] = m["q"] * m["eg"], m["k"] * m["et"]
                m["dk_tail"] = _dot(m["v_new"], m["ds_out"], _NT)
                m["dv_new"] = _dot(m["k_tail"], m["ds_out"], _NN) + _dot(m["intra"], m["do"], _TN)
                m["dq_dec"] = _dot(m["do"], m["st"], _NT)
                m["ds_in"] = m["ds_out"] * m["gl"] + _dot(m["q_dec"], m["do"], _TN)
                m["dintra"] = jnp.where(ri >= ci, _dot(m["do"], m["v_new"], _NT), 0.0)
                m["dintra_t"] = jnp.where(ri <= ci, _dot(m["v_new"], m["do"], _NT), 0.0)
            for m in ms:
                m["dw"] = -_dot(m["dv_new"], m["st"], _NT)
                m["ds_in"] = m["ds_in"] - _dot(m["w"], m["dv_new"], _TN)
            for m in ms:
                dsol = jnp.concatenate([m["dv_new"], m["dw"]], axis=1)
                m["drhs"] = _dot3(m["tm"], dsol, _TN)
            for m in ms:
                m["dl"] = jnp.where(ri > ci, -_dot(m["drhs"], m["sol"], _NT), 0.0)
                m["dl_t"] = jnp.where(ri < ci, -_dot(m["sol"], m["drhs"], _NT), 0.0)
            for m in ms:
                m["dkb2"] = _dot(m["dl"] * m["decay"], m["k"], _NN)
                m["dk"] = _dot(m["dl_t"] * m["decay_t"], m["kb"], _NN) + _dot(m["dintra_t"] * m["decay_t"], m["q"], _NN)
                m["dq"] = _dot(m["dintra"] * m["decay"], m["k"], _NN)
            for m in ms:
                _chunk_bwd_finish(m)
            for m in ms:
                ones_ge = jnp.where(ri <= ci, 1.0, 0.0).astype(BF16)
                gh, gl_ = _split(m["dgc"] + jnp.zeros((_C, LANES), F32))
                m["dg_b"] = jnp.dot(ones_ge, gh, preferred_element_type=F32) + jnp.dot(ones_ge, gl_, preferred_element_type=F32)
            lane = lax.broadcasted_iota(jnp.int32, (_C, LANES), 1)
            for i, m in enumerate(ms):
                bb, hh, rows = m["bb"], m["hh"], pl.ds(m["r0"], _C)
                dqkv_ref[0, bb, rows, _hsl(hh)] = m["dq"] * (DN_HD ** -0.5)
                dqkv_ref[1, bb, rows, _hsl(hh)] = m["dk"]
                dqkv_ref[2, bb, rows, _hsl(hh)] = m["dv"]
                dbg_ref[bb, rows, _hsl(hh)] = jnp.where(lane == 0, m["dbeta"], jnp.where(lane == 1, m["dg_b"], 0.0))
                ds_ref[i] = m["ds_in"]
            return carry

        lax.fori_loop(0, _SEG // _C, chunk, 0)

    col, bgs, grs, sts_spec = _chunk_specs(bsz, nseg, True)
    q3, bg3, do3 = qkvn.reshape(bsz, s, 3 * DN), bg.reshape(bsz, s, LANES), do.reshape(bsz, s, DN)
    (dqkv, dbg), extra = _plugged_call(
        body, plug, _grid_ends((nseg,)), (q3, q3, q3, bg3, g_rows, states, do3), name="dn_chunk_bwd", grid=(nseg,),
        in_specs=[col(0), col(1), col(2), bgs, grs, sts_spec, col(0)],
        out_specs=[_vspec((3, bsz, _SEG, DN), lambda i: (0, 0, nseg - 1 - i, 0)), col(0)],
        out_shape=[jax.ShapeDtypeStruct((3, bsz, s, DN), F32), jax.ShapeDtypeStruct((bsz, s, DN), F32)],
        scratch_shapes=[pltpu.VMEM((bsz * DN_HEADS, DN_HD, DN_HD), F32)])
    return dqkv.reshape(3, t, DN), dbg.reshape(t, DN), extra


def _chunk_bwd_finish(m):
    q, k, v, beta, decay, decay_t = m["q"], m["k"], m["v"], m["beta"], m["decay"], m["decay_t"]
    eg, et, gl, kb, dl, dl_t, dintra, dintra_t = m["eg"], m["et"], m["gl"], m["kb"], m["dl"], m["dl_t"], m["dintra"], m["dintra_t"]
    dq_dec, dk_tail, dq, dk = m["dq_dec"], m["dk_tail"], m["dq"], m["dk"]
    dgl = jnp.sum(jnp.sum(m["ds_out"] * m["st"], axis=1, keepdims=True), axis=0, keepdims=True)
    dvb, dkbeg = m["drhs"][:, :DN_HD], m["drhs"][:, DN_HD:]
    dkb = dkbeg * eg + m["dkb2"]
    deg = jnp.sum(dkbeg * kb, axis=1, keepdims=True)
    em = (dl * m["kk"] + dintra * m["qk"]) * decay
    em_t = (dl_t * m["kk_t"] + dintra_t * m["qk_t"]) * decay_t
    dgc = jnp.sum(em, axis=1, keepdims=True) - jnp.sum(em_t, axis=1, keepdims=True)
    dq = dq + dq_dec * eg
    deg = deg + jnp.sum(dq_dec * q, axis=1, keepdims=True)
    dk = dk + dk_tail * et
    det = jnp.sum(dk_tail * k, axis=1, keepdims=True)
    dgc = dgc + deg * eg - det * et
    dgc_last = jnp.sum(det * et, axis=0, keepdims=True) + dgl * gl
    rcol = lax.broadcasted_iota(jnp.int32, (_C, 1), 0)
    m["dgc"] = dgc + jnp.where(rcol == _C - 1, dgc_last, 0.0)
    m["dq"] = dq
    m["dk"] = dk + dkb * beta
    m["dbeta"] = jnp.sum(dkb * k, axis=1, keepdims=True) + jnp.sum(dvb * v, axis=1, keepdims=True)
    m["dv"] = dvb * beta


def _mod_fwd(c_all, ada_w_loc, ada_b_loc):
    n, cols = c_all.shape[0], ada_w_loc.shape[1]

    def body(c_ref, w_ref, b_ref, o_ref):
        o_ref[...] = _dot(jax.nn.silu(c_ref[...]), w_ref[...], _NN) + b_ref[...]

    return pl.pallas_call(body, name="mod_fwd", out_shape=jax.ShapeDtypeStruct((n, cols), F32))(c_all, ada_w_loc, ada_b_loc)


def _ada_grad(c_all, dmod_loc, dmod_all):
    d, cols = c_all.shape[1], dmod_loc.shape[1]

    def body(c_ref, dl_ref, da_ref, gw_ref, gb_ref):
        gw_ref[...] = _dot(jax.nn.silu(c_ref[...]), dl_ref[...], _TN)
        gb_ref[...] = jnp.sum(da_ref[...], axis=0, keepdims=True)

    return pl.pallas_call(body, name="ada_grad", out_shape=[jax.ShapeDtypeStruct((d, cols), F32),
                                                           jax.ShapeDtypeStruct((1, dmod_all.shape[1]), F32)])(c_all, dmod_loc, dmod_all)


ELEMENTWISE_BLOCK_BYTES = 3 * 2 ** 19


def _row_tile(r, c=1024):
    fits = [tr for tr in range(16, r + 1, 16) if tr * c * 4 <= ELEMENTWISE_BLOCK_BYTES]
    if not fits:
        return r
    whole = [tr for tr in fits if r % tr == 0]
    return whole[-1] if whole else fits[-1]


def _adamw(w, m, v, grads, name):
    r, c = w.shape
    tr = _row_tile(r, c)
    blk = _vspec((tr, c), lambda i: (i, 0))
    n = len(grads)

    def body(*refs):
        w_ref, m_ref, v_ref = refs[:3]
        g_ref, d_ref, mo_ref, vo_ref = refs[3 + n:]
        g = refs[3][...]
        for p in refs[4:3 + n]:
            g = g + p[...]
        m_new = ADAM_B1 * m_ref[...] + (1.0 - ADAM_B1) * g
        v_new = ADAM_B2 * v_ref[...] + (1.0 - ADAM_B2) * jnp.square(g)
        m_hat = m_new / (1.0 - ADAM_B1 ** ADAM_STEP)
        v_hat = v_new / (1.0 - ADAM_B2 ** ADAM_STEP)
        g_ref[...] = g
        d_ref[...] = -ADAM_LR * (m_hat / (jnp.sqrt(v_hat) + ADAM_EPS) + ADAM_WD * w_ref[...])
        mo_ref[...] = m_new
        vo_ref[...] = v_new

    o = jax.ShapeDtypeStruct((r, c), F32)
    return pl.pallas_call(body, name=name, grid=(pl.cdiv(r, tr),), in_specs=[blk] * (3 + n), out_specs=[blk] * 4,
                          out_shape=[o] * 4)(w, m, v, *grads)


def _sum_lead(x, name):
    p, r, c = x.shape
    tr = _row_tile(r, c)

    def body(x_ref, o_ref):
        acc = x_ref[0].astype(F32)
        for i in range(1, p):
            acc = acc + x_ref[i].astype(F32)
        o_ref[...] = acc

    return pl.pallas_call(body, name=name, grid=(pl.cdiv(r, tr),), in_specs=[_vspec((p, tr, c), lambda i: (0, i, 0))],
                          out_specs=_vspec((tr, c), lambda i: (i, 0)), out_shape=jax.ShapeDtypeStruct((r, c), F32))(x)


def _allgather8(x_shard, name):
    m_per, n = x_shard.shape

    def body(x_ref, out_ref, send_sems, recv_sems, local_sem):
        x, y, c = lax.axis_index("x"), lax.axis_index("y"), lax.axis_index("c")
        me, sibling = (x, y, c), (x, y, 1 - c)
        chips = [(1 - x, y), (x, 1 - y), (1 - x, 1 - y)]

        def rows(px, py, pc):
            return out_ref.at[pl.ds((4 * px + 2 * py + pc) * m_per, m_per), :]

        def copy(k, block, to, src=None):
            return pltpu.make_async_remote_copy(
                src_ref=rows(*block) if src is None else src, dst_ref=rows(*block), send_sem=send_sems.at[k],
                recv_sem=recv_sems.at[k], device_id=to, device_id_type=MESH)

        mine = pltpu.make_async_copy(x_ref, rows(*me), local_sem)
        mine.start()
        first = [copy(0, me, sibling, src=x_ref)]
        first += [copy(1 + j, me, (*chip, c), src=x_ref) for j, chip in enumerate(chips)]
        for cp in first:
            cp.start()
        passed = [copy(4 + j, (*chip, c), sibling) for j, chip in enumerate(chips)]
        for j, chip in enumerate(chips):
            copy(1 + j, (*chip, c), me).wait_recv()
            passed[j].start()
        copy(0, sibling, me).wait_recv()
        for j, chip in enumerate(chips):
            copy(4 + j, (*chip, 1 - c), me).wait_recv()
        for cp in first + passed:
            cp.wait_send()
        mine.wait()

    return pl.pallas_call(
        body, name=name, out_shape=jax.ShapeDtypeStruct((8 * m_per, n), x_shard.dtype),
        in_specs=[pl.BlockSpec(memory_space=pltpu.VMEM)], out_specs=pl.BlockSpec(memory_space=pltpu.VMEM),
        scratch_shapes=[pltpu.SemaphoreType.DMA((7,)), pltpu.SemaphoreType.DMA((7,)), pltpu.SemaphoreType.DMA],
    )(x_shard)


_HBM = pl.BlockSpec(memory_space=pltpu.HBM)


def _mesh_place():
    x, y, c = lax.axis_index("x"), lax.axis_index("y"), lax.axis_index("c")
    return x, y, c, 2 * x + y, [(1 - x, y), (x, 1 - y), (1 - x, 1 - y)]


def _gather_plug(shards):
    n = len(shards)

    def half(ref, c, lead=None):
        r, cols = ref.shape[-2] // 2, ref.shape[-1] // 2
        if r % 16 == 0:
            rows = pl.ds(pl.multiple_of(c * r, 16), r)
            return ref.at[rows, :] if lead is None else ref.at[lead, rows, :]
        lanes = pl.ds(pl.multiple_of(c * cols, LANES), cols)
        return ref.at[:, lanes] if lead is None else ref.at[lead, :, lanes]

    def copies(ins, outs, send, recv):
        x, y, c, me, chips = _mesh_place()
        ici, fwd, fwd_in = [], [], []
        for i in range(n):
            for j, (px, py) in enumerate(chips):
                q = 2 * px + py
                ici.append((pltpu.make_async_remote_copy(
                    src_ref=half(ins[i], c), dst_ref=half(outs[i], c, me), send_sem=send.at[6 * i + j], recv_sem=recv.at[6 * i + j],
                    device_id=(px, py, c), device_id_type=MESH),
                    pltpu.make_async_remote_copy(
                    src_ref=half(ins[i], c), dst_ref=half(outs[i], c, q), send_sem=send.at[6 * i + j], recv_sem=recv.at[6 * i + j],
                    device_id=(px, py, c), device_id_type=MESH)))
                fwd.append(pltpu.make_async_remote_copy(
                    src_ref=half(outs[i], c, q), dst_ref=half(outs[i], c, q), send_sem=send.at[6 * i + 3 + j],
                    recv_sem=recv.at[6 * i + 3 + j], device_id=(x, y, 1 - c), device_id_type=MESH))
                fwd_in.append(pltpu.make_async_remote_copy(
                    src_ref=half(outs[i], 1 - c, q), dst_ref=half(outs[i], 1 - c, q), send_sem=send.at[6 * i + 3 + j],
                    recv_sem=recv.at[6 * i + 3 + j], device_id=(x, y, 1 - c), device_id_type=MESH))
        return ici, fwd, fwd_in, me

    def start(ins, outs, send, recv, loc):
        ici, _, _, me = copies(ins, outs, send, recv)
        for i in range(n):
            pltpu.make_async_copy(ins[i], outs[i].at[me], loc.at[i]).start()
        for out_cp, _ in ici:
            out_cp.start()

    def finish(ins, outs, send, recv, loc):
        ici, fwd, fwd_in, me = copies(ins, outs, send, recv)
        for (_, in_cp), f in zip(ici, fwd):
            in_cp.wait_recv()
            f.start()
        for f in fwd_in:
            f.wait_recv()
        for (out_cp, _), f in zip(ici, fwd):
            out_cp.wait_send()
            f.wait_send()
        for i in range(n):
            pltpu.make_async_copy(ins[i], outs[i].at[me], loc.at[i]).wait()

    return dict(ins=list(shards), out_shape=[jax.ShapeDtypeStruct((N_CHIPS,) + a.shape, a.dtype) for a in shards],
                scratch=[pltpu.SemaphoreType.DMA((6 * n,)), pltpu.SemaphoreType.DMA((6 * n,)), pltpu.SemaphoreType.DMA((n,))],
                start=start, finish=finish)


def _exchange_plug(pieces):
    n = len(pieces)

    def copies(ins, outs, send, recv):
        x, y, c, me, chips = _mesh_place()
        out_cps, in_cps = [], []
        for i in range(n):
            for j, (px, py) in enumerate(chips):
                q = 2 * px + py
                out_cps.append(pltpu.make_async_remote_copy(src_ref=ins[i].at[q], dst_ref=outs[i].at[me], send_sem=send.at[3 * i + j],
                                                            recv_sem=recv.at[3 * i + j], device_id=(px, py, c), device_id_type=MESH))
                in_cps.append(pltpu.make_async_remote_copy(src_ref=ins[i].at[me], dst_ref=outs[i].at[q], send_sem=send.at[3 * i + j],
                                                           recv_sem=recv.at[3 * i + j], device_id=(px, py, c), device_id_type=MESH))
        return out_cps, in_cps, me

    def start(ins, outs, send, recv, loc):
        out_cps, _, me = copies(ins, outs, send, recv)
        for i in range(n):
            pltpu.make_async_copy(ins[i].at[me], outs[i].at[me], loc.at[i]).start()
        for cp in out_cps:
            cp.start()

    def finish(ins, outs, send, recv, loc):
        out_cps, in_cps, me = copies(ins, outs, send, recv)
        for cp in in_cps:
            cp.wait_recv()
        for cp in out_cps:
            cp.wait_send()
        for i in range(n):
            pltpu.make_async_copy(ins[i].at[me], outs[i].at[me], loc.at[i]).wait()

    return dict(ins=list(pieces), out_shape=[jax.ShapeDtypeStruct(a.shape, a.dtype) for a in pieces],
                scratch=[pltpu.SemaphoreType.DMA((3 * n,)), pltpu.SemaphoreType.DMA((3 * n,)), pltpu.SemaphoreType.DMA((n,))],
                start=start, finish=finish)


def _comm_call(plug, name):
    n_in, n_out = len(plug["ins"]), len(plug["out_shape"])

    def body(*refs):
        ins, outs, sems = refs[:n_in], refs[n_in:n_in + n_out], refs[n_in + n_out:]
        plug["start"](ins, outs, *sems)
        plug["finish"](ins, outs, *sems)

    return pl.pallas_call(body, name=name, out_shape=plug["out_shape"], in_specs=[_HBM] * n_in, out_specs=[_HBM] * n_out,
                          scratch_shapes=plug["scratch"])(*plug["ins"])


def _plugged_call(body, plug, first_last, args, *, name, grid, in_specs, out_specs, out_shape, scratch_shapes=()):
    in_specs, out_specs, out_shape, scratch_shapes = list(in_specs), list(out_specs), list(out_shape), list(scratch_shapes)
    if plug is None:
        return pl.pallas_call(body, name=name, grid=grid, in_specs=in_specs, out_specs=out_specs, out_shape=out_shape,
                              scratch_shapes=scratch_shapes)(*args), []
    n_in, n_out, n_sc = len(in_specs), len(out_specs), len(scratch_shapes)
    p_in, p_out = len(plug["ins"]), len(plug["out_shape"])

    def full(*refs):
        ins, refs = refs[:n_in], refs[n_in:]
        pins, refs = refs[:p_in], refs[p_in:]
        outs, refs = refs[:n_out], refs[n_out:]
        pouts, refs = refs[:p_out], refs[p_out:]
        scr, psems = refs[:n_sc], refs[n_sc:]
        first, last = first_last()

        @pl.when(first)
        def _():
            plug["start"](pins, pouts, *psems)

        body(*ins, *outs, *scr)

        @pl.when(last)
        def _():
            plug["finish"](pins, pouts, *psems)

    res = pl.pallas_call(full, name=name, grid=grid, in_specs=in_specs + [_HBM] * p_in, out_specs=out_specs + [_HBM] * p_out,
                         out_shape=out_shape + plug["out_shape"], scratch_shapes=scratch_shapes + plug["scratch"])(*args, *plug["ins"])
    return res[:n_out], res[n_out:]


def _grid_ends(grid):
    def ends():
        first = last = None
        for ax, n in enumerate(grid):
            i = pl.program_id(ax)
            first = (i == 0) if first is None else first & (i == 0)
            last = (i == n - 1) if last is None else last & (i == n - 1)
        return first, last
    return ends


def _sibling_exchange(arrs, name):
    n = len(arrs)

    def body(*refs):
        ins, outs = refs[:n], refs[n:2 * n]
        send_sems, recv_sems = refs[2 * n:]
        sibling = (lax.axis_index("x"), lax.axis_index("y"), 1 - lax.axis_index("c"))
        cps = [pltpu.make_async_remote_copy(src_ref=ins[i], dst_ref=outs[i], send_sem=send_sems.at[i], recv_sem=recv_sems.at[i],
                                            device_id=sibling, device_id_type=MESH) for i in range(n)]
        for cp in cps:
            cp.start()
        for cp in cps:
            cp.wait()

    return pl.pallas_call(
        body, name=name, out_shape=[jax.ShapeDtypeStruct(a.shape, a.dtype) for a in arrs], in_specs=[_HBM] * n, out_specs=[_HBM] * n,
        scratch_shapes=[pltpu.SemaphoreType.DMA((n,)), pltpu.SemaphoreType.DMA((n,))],
    )(*arrs)


def _to_padded(wt):
    z = jnp.zeros((IN_PAD - IN_DIM, wt.shape[1]), wt.dtype)
    return jnp.concatenate([wt[768:2304], wt[0:512], wt[2304:2816], wt[2824:3848], wt[3848:4872],
                            wt[512:640], wt[640:768], wt[2816:2824], z], axis=0)


def _from_padded(gt):
    return jnp.concatenate([gt[C_Q:C_Q + AQ], gt[C_K:C_K + AKV], gt[C_V:C_V + AKV], gt[C_DQKV:C_DQKV + 3 * DN],
                            gt[C_DZ:C_DZ + DN], gt[C_BD:C_BD + 2 * DN_HEADS], gt[C_GA:C_GA + D_MODEL],
                            gt[C_GD:C_GD + D_MODEL]], axis=0)


def _lane_vec(a):
    return jnp.zeros((1, LANES), F32).at[0, DN_HEADS:2 * DN_HEADS].set(a)


_ROW_SHARDED = ("w_in", "w_out", "ffn_w_down")
_FFN = ("ffn_w_up", "ffn_w_down")
_LATE_MIXER = ("w_attn_branch", "w_dn_branch", "w_out")


def _pieces(k, a):
    if a.ndim == 3:
        return a
    if k in _ROW_SHARDED:
        return a.reshape(N_CHIPS, a.shape[0] // N_CHIPS, a.shape[1]).astype(BF16)
    return jnp.transpose(a.reshape(a.shape[0], N_CHIPS, a.shape[1] // N_CHIPS), (1, 0, 2)).astype(BF16)


def _assemble(k, a):
    if k in _ROW_SHARDED:
        return a.reshape(-1, a.shape[2])
    return jnp.transpose(a, (1, 0, 2)).reshape(a.shape[1], -1)


def _device_step(x2, tgt2, mod, p, bsz, shards=None):
    d = D_MODEL
    on_mesh = shards is not None
    p = dict(p)
    sh1, sc1, g1, sh2, sc2, g2 = [mod[:, i * d:(i + 1) * d].reshape(bsz, 1, d) for i in range(N_MOD)]
    alog_v, dt_v = _lane_vec(p["dn_a_log"]), _lane_vec(p["dn_dt_bias"])
    sinks = p["attn_sinks"].reshape(1, AQ_HEADS)
    u1 = _pre_fwd(x2, p["norm_mix_pre"], sc1, sh1, "pre1_fwd")
    if on_mesh:
        proj, got = _mm(u1, p["w_in"], "nt", F32, "mm_proj", _gather_plug(shards["late_mixer"]))
        p.update({k: _assemble(k, a) for k, a in zip(_LATE_MIXER, got)})
    else:
        proj = _mm(u1, p["w_in"], "nt", F32, "mm_proj")
    bias = _bias_build(p["rel_bias"])
    y_attn = _attn_fwd(proj, bias, sinks, bsz)
    qkvn = _dn_prep_fwd(proj, p["dn_conv_w"], bsz)
    bg = _bg_fwd(proj, alog_v, dt_v, bsz)
    nc = x2.shape[0] // bsz // DN_CHUNK
    g_rows = jnp.transpose(bg[:, DN_HEADS:2 * DN_HEADS].reshape(bsz, nc, DN_CHUNK, DN_HEADS), (0, 3, 1, 2))
    o, states, got = _dn_chunk_fwd(qkvn, bg, g_rows, bsz, _gather_plug(shards["ffn"]) if on_mesh else None)
    for k, a in zip(_FFN, got):
        p[k] = _assemble(k, a)
    y_dn = _dn_out_fwd(o, proj, p["dn_norm_w"])
    ya = _mm(y_attn, p["w_attn_branch"], "nn", F32, "mm_ya")
    yd = _mm(y_dn, p["w_dn_branch"], "nn", F32, "mm_yd")
    merged = _merge_fwd(proj, ya, yd)
    y1 = _mm(merged, p["w_out"], "nn", F32, "mm_y1")
    h1 = _post_fwd(x2, y1, p["norm_mix_post"], g1, "post1_fwd")
    u2 = _pre_fwd(h1, p["norm_ffn_pre"], sc2, sh2, "pre2_fwd")
    up = _mm(u2, p["ffn_w_up"], "nn", F32, "mm_up")
    act = _ffn_act_fwd(up, p["ffn_conv_w"], bsz)
    y2 = _mm(act, p["ffn_w_down"], "nn", F32, "mm_y2")
    dh2, sq = _post_loss(h1, y2, p["norm_ffn_post"], g2, tgt2, "post2_loss")
    g = {}
    dy2, g["norm_ffn_post"], dg2 = _post_bwd(dh2, y2, p["norm_ffn_post"], g2, "post2_bwd")
    dact = _mm(dy2, p["ffn_w_down"], "nt", F32, "mm_dact")
    g["ffn_w_down"] = _mm(act, dy2, "tn", BF16, "mm_dwdown")
    dupg, dupv, dcwg, dcwv = _ffn_act_bwd(up, p["ffn_conv_w"], dact, bsz)
    g["ffn_conv_w"] = jnp.concatenate([dcwg, dcwv], axis=1)
    dup = jnp.concatenate([dupg, dupv], axis=1)
    g["ffn_w_up"] = _mm(u2, dup, "tn", BF16, "mm_dwup", split=N_CHIPS)
    du2 = _mm(dup, p["ffn_w_up"], "nt", F32, "mm_du2")
    dh1, g["norm_ffn_pre"], dsc2, dsh2 = _pre_bwd(h1, p["norm_ffn_pre"], sc2, sh2, du2, dh2, "pre2_bwd")
    dy1, g["norm_mix_post"], dg1 = _post_bwd(dh1, y1, p["norm_mix_post"], g1, "post1_bwd")
    dmerged = _mm(dy1, p["w_out"], "nt", F32, "mm_dmerged")
    g["w_out"] = _mm(merged, dy1, "tn", BF16, "mm_dwout")
    dga, dgd, dya, dyd = _merge_bwd(proj, ya, yd, dmerged)
    dy_attn = _mm(dya, p["w_attn_branch"], "nt", BF16, "mm_dyattn")
    g["w_attn_branch"] = _mm(y_attn, dya, "tn", BF16, "mm_dwab", split=N_CHIPS)
    dy_dn = _mm(dyd, p["w_dn_branch"], "nt", F32, "mm_dydn")
    g["w_dn_branch"] = _mm(y_dn, dyd, "tn", BF16, "mm_dwdb", split=N_CHIPS)
    do, dz, g["dn_norm_w"] = _dn_out_bwd(o, proj, p["dn_norm_w"], dy_dn)
    def plug_for(names):
        return _exchange_plug([_pieces(k, g[k]) for k in names]) if on_mesh else None

    early = ("w_out", "w_attn_branch", "w_dn_branch")
    dqkvn, dbg4, got_ffn = _dn_chunk_bwd(qkvn, bg, g_rows, states, do, bsz, plug_for(_FFN))
    d_dqkv, g["dn_conv_w"] = _dn_prep_bwd(proj, p["dn_conv_w"], dqkvn, bsz)
    dbd, g["dn_a_log"], g["dn_dt_bias"] = _bg_bwd(proj, alog_v, dt_v, dbg4, bsz)
    dq, dk, dv, dbias, g["attn_sinks"], got_early = _attn_bwd(proj, bias, sinks, dy_attn, bsz, plug_for(early))
    g["rel_bias"] = _bias_grad(dbias)
    dproj = jnp.concatenate([d_dqkv, dq, dz, dga, dgd, dk, dv, dbd], axis=1)
    g["w_in"] = _from_padded(_mm(dproj, u1, "tn", BF16, "mm_dwin"))
    if on_mesh:
        du1, got_in = _mm(dproj, p["w_in"], "nn", F32, "mm_du1", plug_for(("w_in",)))
        g.update(zip(_FFN + early + ("w_in",), list(got_ffn) + list(got_early) + list(got_in)))
    else:
        du1 = _mm(dproj, p["w_in"], "nn", F32, "mm_du1")
    dx, g["norm_mix_pre"], dsc1, dsh1 = _pre_bwd(x2, p["norm_mix_pre"], sc1, sh1, du1, dh1, "pre1_bwd")
    dmod = jnp.concatenate([dsh1, dsc1, dg1, dsh2, dsc2, dg2], axis=-1).reshape(bsz, N_MOD * d)
    return sq, dx, dmod, g


_SMALL = (("norm_mix_pre", D_MODEL), ("norm_mix_post", D_MODEL), ("norm_ffn_pre", D_MODEL), ("norm_ffn_post", D_MODEL),
          ("dn_norm_w", DN_HD), ("dn_a_log", LANES), ("dn_dt_bias", LANES), ("attn_sinks", AQ_HEADS * LANES),
          ("rel_bias", AQ_HEADS * LANES), ("dn_conv_w", DN_CONV * 3 * DN), ("ffn_conv_w", FFN_CONV * 2 * D_FF))


def _pack_rows(parts, rows):
    flat = jnp.concatenate([a.reshape(-1) for a in parts])
    return jnp.concatenate([flat, jnp.zeros((rows * LANES - flat.shape[0],), F32)]).reshape(rows, LANES)


def _pad128(a):
    flat = a.reshape(-1)
    n = -(-flat.shape[0] // LANES) * LANES
    return jnp.concatenate([flat, jnp.zeros((n - flat.shape[0],), F32)]) if n != flat.shape[0] else flat


_W_NAMES = ("ada_w", "ada_b", "norm_mix_pre", "norm_mix_post", "norm_ffn_pre", "norm_ffn_post", "w_in", "dn_conv_w", "dn_a_log",
            "dn_dt_bias", "dn_norm_w", "attn_sinks", "rel_bias", "w_attn_branch", "w_dn_branch", "w_out", "ffn_w_up", "ffn_conv_w",
            "ffn_w_down")
_BIG = ("w_in", "w_attn_branch", "w_dn_branch", "w_out", "ffn_w_up", "ffn_w_down")


def kernel(x, c, *rest):
    nw = len(_W_NAMES)
    w = dict(zip(_W_NAMES, rest[:nw]))
    loss_target = rest[nw]
    m = dict(zip(_W_NAMES, rest[nw + 1:2 * nw + 1]))
    v = dict(zip(_W_NAMES, rest[2 * nw + 1:3 * nw + 1]))
    ix, iy, ic = lax.axis_index("x"), lax.axis_index("y"), lax.axis_index("c")
    chip, dev = 2 * ix + iy, 4 * ix + 2 * iy + ic
    bsz, s, d = x.shape
    t = bsz * s
    n_dev = 8

    front_rows = 64
    front = _pack_rows([c, w["dn_conv_w"], w["ffn_conv_w"]], front_rows)
    front_all = _allgather8(front, "ag_front").reshape(n_dev, front_rows * LANES)
    n_c, n_dc, n_fc = bsz * d, DN_CONV * 3 * DN // N_CHIPS, FFN_CONV * 2 * D_FF // N_CHIPS
    c_all = front_all[:, :n_c].reshape(n_dev * bsz, d)
    per_chip = front_all[0::2]
    dn_conv_full = jnp.transpose(per_chip[:, n_c:n_c + n_dc].reshape(N_CHIPS, DN_CONV, -1), (1, 0, 2)).reshape(DN_CONV, 3 * DN)
    ffn_conv_full = jnp.transpose(per_chip[:, n_c + n_dc:n_c + n_dc + n_fc].reshape(N_CHIPS, FFN_CONV, -1), (1, 0, 2)).reshape(FFN_CONV, 2 * D_FF)

    mod_cols = N_MOD * d // N_CHIPS
    ada_b_loc = lax.dynamic_slice(w["ada_b"], (0, chip * mod_cols), (1, mod_cols))
    mod_part = _mod_fwd(c_all, w["ada_w"][0], ada_b_loc)
    mod_all = _allgather8(mod_part, "ag_mod").reshape(n_dev, n_dev * bsz, mod_cols)[0::2]
    mod = jnp.transpose(lax.dynamic_slice(mod_all, (0, dev * bsz, 0), (N_CHIPS, bsz, mod_cols)), (1, 0, 2)).reshape(bsz, N_MOD * d)

    (w_in_all,) = _comm_call(_gather_plug([jnp.swapaxes(w["w_in"][0], 0, 1).astype(BF16)]), "gather_w_in")
    p = {"w_in": _to_padded(_assemble("w_in", w_in_all))}
    shards = {"late_mixer": [w[k][0].astype(BF16) for k in _LATE_MIXER], "ffn": [w[k][0].astype(BF16) for k in _FFN]}
    for k in ("norm_mix_pre", "norm_mix_post", "norm_ffn_pre", "norm_ffn_post", "dn_norm_w", "attn_sinks"):
        p[k] = w[k]
    p["dn_a_log"], p["dn_dt_bias"], p["rel_bias"] = w["dn_a_log"][0], w["dn_dt_bias"][0], w["rel_bias"]
    p["dn_conv_w"], p["ffn_conv_w"] = dn_conv_full, ffn_conv_full

    sq, dx, dmod, g = _device_step(x.reshape(t, d), loss_target.reshape(t, d), mod, p, bsz, shards)
    loss = lax.psum(0.5 * jnp.sum(sq), ("x", "y", "c"))

    g["dn_a_log"], g["dn_dt_bias"] = g["dn_a_log"].reshape(-1), g["dn_dt_bias"].reshape(-1)
    small_rows = 328
    small = _pack_rows([dmod] + [g[k] for k, _ in _SMALL], small_rows)
    small_all = _allgather8(small, "ag_small").reshape(n_dev, small_rows, LANES)
    n_dm = bsz * N_MOD * d
    dmod_all = small_all.reshape(n_dev, -1)[:, :n_dm].reshape(n_dev * bsz, N_MOD * d)
    tot = _sum_lead(small_all, "sum_small").reshape(-1)
    gs, off = {}, n_dm
    for k, n in _SMALL:
        gs[k] = tot[off:off + n]
        off += n
    grad = {}
    grad["ada_w"], grad["ada_b"] = _ada_grad(c_all, lax.dynamic_slice(dmod_all, (0, chip * mod_cols), (n_dev * bsz, mod_cols)), dmod_all)
    for k in ("norm_mix_pre", "norm_mix_post", "norm_ffn_pre", "norm_ffn_post", "dn_norm_w"):
        grad[k] = gs[k]
    grad["dn_a_log"] = gs["dn_a_log"][DN_HEADS:2 * DN_HEADS]
    grad["dn_dt_bias"] = gs["dn_dt_bias"][DN_HEADS:2 * DN_HEADS]
    grad["attn_sinks"] = gs["attn_sinks"].reshape(AQ_HEADS, LANES)[:, 0]
    grad["rel_bias"] = gs["rel_bias"].reshape(AQ_HEADS, LANES)[:, :REL_BUCKETS].T
    grad["dn_conv_w"] = lax.dynamic_slice(gs["dn_conv_w"].reshape(DN_CONV, 3 * DN), (0, chip * (3 * DN // N_CHIPS)), (DN_CONV, 3 * DN // N_CHIPS))
    grad["ffn_conv_w"] = lax.dynamic_slice(gs["ffn_conv_w"].reshape(FFN_CONV, 2 * D_FF), (0, chip * (2 * D_FF // N_CHIPS)), (FFN_CONV, 2 * D_FF // N_CHIPS))

    mine = [_sum_lead(g[k], "sum_" + k) for k in _BIG]
    theirs = _sibling_exchange(mine, "exchange_cores")

    out = {}
    for k, a, b in zip(_BIG, mine, theirs):
        if k == "w_in":
            tr = lambda z: jnp.swapaxes(z, 0, 1)
            out[k] = [tr(r) for r in _adamw(tr(w[k][0]), tr(m[k][0]), tr(v[k][0]), [a, b], "adamw_" + k)]
        else:
            out[k] = _adamw(w[k][0], m[k][0], v[k][0], [a, b], "adamw_" + k)
    out["ada_w"] = _adamw(w["ada_w"][0], m["ada_w"][0], v["ada_w"][0], [grad["ada_w"]], "adamw_ada_w")
    small_names = [k for k in _W_NAMES if k not in _BIG and k != "ada_w"]
    offs, n_tot = {}, 0
    for k in small_names:
        offs[k] = n_tot
        n_tot += -(-w[k].size // LANES) * LANES
    pack = lambda dct: jnp.concatenate([_pad128(dct[k]) for k in small_names]).reshape(-1, LANES)
    rows_s = n_tot // LANES
    pad_rows = -(-rows_s // 8) * 8 - rows_s
    padr = lambda a: jnp.concatenate([a, jnp.zeros((pad_rows, LANES), F32)]) if pad_rows else a
    res_small = _adamw(padr(pack(w)), padr(pack(m)), padr(pack(v)), [padr(pack(grad))], "adamw_small")
    for k in small_names:
        out[k] = [r.reshape(-1)[offs[k]:offs[k] + w[k].size].reshape(w[k].shape) for r in res_small]
    for k in _BIG + ("ada_w",):
        out[k] = [r.reshape(w[k].shape) for r in out[k]]
    grads, deltas, new_m, new_v = ([out[k][i] for k in _W_NAMES] for i in range(4))
    return (loss, dx.reshape(bsz, s, d), *grads, *deltas, *new_m, *new_v)
```

```python
import functools
import math

import numpy as np
import jax
import jax.numpy as jnp
from jax import lax
from jax.experimental import pallas as pl
from jax.experimental.pallas import tpu as pltpu

F32 = jnp.float32
BF16 = jnp.bfloat16
MESH = pl.DeviceIdType.MESH

D_MODEL = 1024
N_MOD = 6
AQ_HEADS, AKV_HEADS, A_HD, WINDOW = 8, 2, 64, 128
REL_BUCKETS, REL_MAX_DIST = 32, 128
DN_HEADS, DN_HD, DN_CONV, DN_CHUNK = 4, 128, 4, 64
D_FF, FFN_CONV = 2816, 3
RMS_EPS, L2_EPS, NEG_INF = 1e-6, 1e-6, -1e30
AQ, AKV, DN = AQ_HEADS * A_HD, AKV_HEADS * A_HD, DN_HEADS * DN_HD
IN_DIM = AQ + 2 * AKV + 3 * DN + DN + 2 * DN_HEADS + 2 * D_MODEL
C_DQKV, C_Q, C_DZ, C_GA, C_GD, C_K, C_V, C_BD = 0, 1536, 2048, 2560, 3584, 4608, 4736, 4864
IN_PAD = 4992
LANES = 128
N_CHIPS = 4

ADAM_LR, ADAM_B1, ADAM_B2, ADAM_EPS, ADAM_WD, ADAM_STEP = 0.001, 0.9, 0.999, 1e-08, 0.01, 10


def _pick(n, cap):
    best = None
    for t in range(LANES, cap + 1, LANES):
        if n % t == 0:
            best = t
    return best if best is not None else n


def _vspec(shape, index_map):
    return pl.BlockSpec(shape, index_map)


MM_VMEM_BUDGET = 40 * 2 ** 20
GRID_STEP_S = 0.35e-6
HBM_BYTES_PER_S = 3.0e12
MXU_FLOPS_PER_S = 9.0e14
MXU_DIM = 256


def _mm_tiles(m, n, k, mode, in_bytes, out_bytes, split=1):
    best = None
    for tm in [t for t in range(LANES, m + 1, LANES) if m % t == 0]:
        for tn in [t for t in range(LANES, n // split + 1, LANES) if (n // split) % t == 0]:
            a, b, o = k * tm * in_bytes, k * tn * in_bytes, tm * tn * out_bytes
            if 2 * (a + b + o) + (a if mode == "tn" else 0) > MM_VMEM_BUDGET:
                continue
            hbm_s = (m * k * in_bytes + (m // tm) * n * k * in_bytes + m * n * out_bytes) / HBM_BYTES_PER_S
            mxu_s = 2 * m * n * k / (MXU_FLOPS_PER_S * min(1.0, tm / MXU_DIM) * min(1.0, tn / MXU_DIM))
            cost = (m // tm) * (n // tn) * GRID_STEP_S + max(hbm_s, mxu_s)
            if best is None or cost < best[0]:
                best = (cost, tm, tn)
    return best[1], best[2]


def _mm(a, b, mode, out_dtype, name, plug=None, split=1):
    if mode == "nn":
        (m, k), n = a.shape, b.shape[1]
        dims = (((1,), (0,)), ((), ()))
    elif mode == "nt":
        (m, k), n = a.shape, b.shape[0]
        dims = (((1,), (1,)), ((), ()))
    else:
        (k, m), n = a.shape, b.shape[1]
        dims = (((0,), (0,)), ((), ()))
    tm, tn = _mm_tiles(m, n, k, mode, a.dtype.itemsize, jnp.dtype(out_dtype).itemsize, split)
    if mode == "tn":
        a_spec = _vspec((k, tm), lambda i, j: (0, i))
    else:
        a_spec = _vspec((tm, k), lambda i, j: (i, 0))
    if mode == "nt":
        b_spec = _vspec((tn, k), lambda i, j: (j, 0))
    else:
        b_spec = _vspec((k, tn), lambda i, j: (0, j))

    def body(a_ref, b_ref, o_ref):
        o_ref[...] = lax.dot_general(a_ref[...].astype(BF16), b_ref[...].astype(BF16), dims,
                                     preferred_element_type=F32).astype(out_dtype).reshape(o_ref.shape)

    grid = (m // tm, n // tn)
    if split == 1:
        out_spec, out_shape = _vspec((tm, tn), lambda i, j: (i, j)), (m, n)
    else:
        per = n // split // tn
        out_spec, out_shape = _vspec((1, tm, tn), lambda i, j: (j // per, i, j % per)), (split, m, n // split)
    (out,), extra = _plugged_call(body, plug, _grid_ends(grid), (a, b), name=name, grid=grid, in_specs=[a_spec, b_spec],
                                  out_specs=[out_spec], out_shape=[jax.ShapeDtypeStruct(out_shape, out_dtype)])
    return out if plug is None else (out, extra)


def _rms(x, w):
    return (x * lax.rsqrt(jnp.mean(x * x, axis=-1, keepdims=True) + RMS_EPS)) * w


def _pre_f(x, w, sc, sh):
    return _rms(x, w) * (1.0 + sc) + sh


def _post_f(y, w, g):
    return g * _rms(y, w)


def _tok_grid(t, bsz, ts):
    nt = t // bsz // ts
    return nt, (bsz, nt)


def _pre_fwd(x, w, sc, sh, name, ts=512):
    t, d = x.shape
    bsz = sc.shape[0]
    nt, grid = _tok_grid(t, bsz, ts)
    row = _vspec((ts, d), lambda b, i: (b * nt + i, 0))
    vec = _vspec((1, d), lambda b, i: (0, 0))
    bvec = _vspec((1, 1, d), lambda b, i: (b, 0, 0))

    def body(x_ref, w_ref, sc_ref, sh_ref, u_ref):
        u_ref[...] = _pre_f(x_ref[...], w_ref[...], sc_ref[0], sh_ref[0]).astype(BF16)

    return pl.pallas_call(body, name=name, grid=grid, in_specs=[row, vec, bvec, bvec], out_specs=row,
                          out_shape=jax.ShapeDtypeStruct((t, d), BF16))(x, w, sc, sh)


def _pre_bwd(x, w, sc, sh, du, dres, name, ts=512):
    t, d = x.shape
    bsz = sc.shape[0]
    nt, grid = _tok_grid(t, bsz, ts)
    row = _vspec((ts, d), lambda b, i: (b * nt + i, 0))
    vec = _vspec((1, d), lambda b, i: (0, 0))
    bvec = _vspec((1, 1, d), lambda b, i: (b, 0, 0))

    def body(x_ref, w_ref, sc_ref, sh_ref, du_ref, dres_ref, dx_ref, dw_ref, dsc_ref, dsh_ref):
        b, i = pl.program_id(0), pl.program_id(1)
        _, vjp = jax.vjp(_pre_f, x_ref[...], w_ref[...], sc_ref[0], sh_ref[0])
        dx, dw, dsc, dsh = vjp(du_ref[...])
        dx_ref[...] = dres_ref[...] + dx

        @pl.when((b == 0) & (i == 0))
        def _():
            dw_ref[...] = jnp.zeros_like(dw_ref)

        @pl.when(i == 0)
        def _():
            dsc_ref[...] = jnp.zeros_like(dsc_ref)
            dsh_ref[...] = jnp.zeros_like(dsh_ref)

        dw_ref[...] += dw
        dsc_ref[0] += dsc
        dsh_ref[0] += dsh

    return pl.pallas_call(
        body, name=name, grid=grid, in_specs=[row, vec, bvec, bvec, row, row], out_specs=[row, vec, bvec, bvec],
        out_shape=[jax.ShapeDtypeStruct((t, d), F32), jax.ShapeDtypeStruct((1, d), F32),
                   jax.ShapeDtypeStruct((bsz, 1, d), F32), jax.ShapeDtypeStruct((bsz, 1, d), F32)],
    )(x, w, sc, sh, du, dres)


def _accumulate(ref, val, first):
    @pl.when(first)
    def _():
        ref[...] = jnp.zeros_like(ref)

    ref[...] += val.reshape(ref.shape)


def _post_pre_fwd(res, y, w_post, g, w_pre, sc, sh, name, ts=512):
    t, d = y.shape
    bsz = g.shape[0]
    nt, grid = _tok_grid(t, bsz, ts)
    row = _vspec((ts, d), lambda b, i: (b * nt + i, 0))
    vec = _vspec((1, d), lambda b, i: (0, 0))
    bvec = _vspec((1, 1, d), lambda b, i: (b, 0, 0))

    def body(res_ref, y_ref, wp_ref, g_ref, w_ref, sc_ref, sh_ref, h_ref, u_ref):
        h = res_ref[...] + _post_f(y_ref[...], wp_ref[...], g_ref[0])
        h_ref[...] = h
        u_ref[...] = _pre_f(h, w_ref[...], sc_ref[0], sh_ref[0]).astype(BF16)

    return pl.pallas_call(body, name=name, grid=grid, in_specs=[row, row, vec, bvec, vec, bvec, bvec], out_specs=[row, row],
                          out_shape=[jax.ShapeDtypeStruct((t, d), F32), jax.ShapeDtypeStruct((t, d), BF16)],
                          )(res, y, w_post, g, w_pre, sc, sh)


def _post_loss_bwd(res, y, w, g, tgt, name, ts=512):
    t, d = y.shape
    bsz = g.shape[0]
    nt, grid = _tok_grid(t, bsz, ts)
    row = _vspec((ts, d), lambda b, i: (b * nt + i, 0))
    vec = _vspec((1, d), lambda b, i: (0, 0))
    bvec = _vspec((1, 1, d), lambda b, i: (b, 0, 0))

    def body(res_ref, y_ref, w_ref, g_ref, tgt_ref, dh_ref, dy_ref, dw_ref, dg_ref, sq_ref):
        b, i = pl.program_id(0), pl.program_id(1)
        part, vjp = jax.vjp(_post_f, y_ref[...], w_ref[...], g_ref[0])
        e = res_ref[...] + part - tgt_ref[...]
        dh = e * (1.0 / d)
        dh_ref[...] = dh
        dy, dw, dg = vjp(dh)
        dy_ref[...] = dy.astype(BF16)
        _accumulate(dw_ref, dw, (b == 0) & (i == 0))
        _accumulate(dg_ref, dg, i == 0)
        _accumulate(sq_ref, jnp.sum(e * e, axis=0, keepdims=True) * (1.0 / d), (b == 0) & (i == 0))

    return pl.pallas_call(
        body, name=name, grid=grid, in_specs=[row, row, vec, bvec, row], out_specs=[row, row, vec, bvec, vec],
        out_shape=[jax.ShapeDtypeStruct((t, d), F32), jax.ShapeDtypeStruct((t, d), BF16), jax.ShapeDtypeStruct((1, d), F32),
                   jax.ShapeDtypeStruct((bsz, 1, d), F32), jax.ShapeDtypeStruct((1, d), F32)],
    )(res, y, w, g, tgt)


def _pre_post_bwd(x, w, sc, sh, du, dres, y, w_post, g, name, ts=512):
    t, d = x.shape
    bsz = sc.shape[0]
    nt, grid = _tok_grid(t, bsz, ts)
    row = _vspec((ts, d), lambda b, i: (b * nt + i, 0))
    vec = _vspec((1, d), lambda b, i: (0, 0))
    bvec = _vspec((1, 1, d), lambda b, i: (b, 0, 0))

    def body(x_ref, w_ref, sc_ref, sh_ref, du_ref, dres_ref, y_ref, wp_ref, g_ref,
             dx_ref, dy_ref, dw_ref, dsc_ref, dsh_ref, dwp_ref, dg_ref):
        b, i = pl.program_id(0), pl.program_id(1)
        _, vjp = jax.vjp(_pre_f, x_ref[...], w_ref[...], sc_ref[0], sh_ref[0])
        dx, dw, dsc, dsh = vjp(du_ref[...])
        dx = dres_ref[...] + dx
        dx_ref[...] = dx
        _, vjp_post = jax.vjp(_post_f, y_ref[...], wp_ref[...], g_ref[0])
        dy, dwp, dg = vjp_post(dx)
        dy_ref[...] = dy.astype(BF16)
        first = (b == 0) & (i == 0)
        _accumulate(dw_ref, dw, first)
        _accumulate(dwp_ref, dwp, first)
        _accumulate(dsc_ref, dsc, i == 0)
        _accumulate(dsh_ref, dsh, i == 0)
        _accumulate(dg_ref, dg, i == 0)

    v1, vb = jax.ShapeDtypeStruct((1, d), F32), jax.ShapeDtypeStruct((bsz, 1, d), F32)
    return pl.pallas_call(
        body, name=name, grid=grid, in_specs=[row, vec, bvec, bvec, row, row, row, vec, bvec],
        out_specs=[row, row, vec, bvec, bvec, vec, bvec],
        out_shape=[jax.ShapeDtypeStruct((t, d), F32), jax.ShapeDtypeStruct((t, d), BF16), v1, vb, vb, v1, vb],
    )(x, w, sc, sh, du, dres, y, w_post, g)


def _merge_f(ga, gd, ya, yd):
    return jax.nn.sigmoid(ga) * ya + jax.nn.sigmoid(gd) * yd


_MW = 512


def _merge_fwd(proj, ya, yd, ts=512):
    t, d = ya.shape
    blk = _vspec((ts, _MW), lambda i, j: (i, j))
    ga = _vspec((ts, _MW), lambda i, j: (i, C_GA // _MW + j))
    gd = _vspec((ts, _MW), lambda i, j: (i, C_GD // _MW + j))

    def body(ga_ref, gd_ref, ya_ref, yd_ref, o_ref):
        o_ref[...] = _merge_f(ga_ref[...], gd_ref[...], ya_ref[...], yd_ref[...]).astype(BF16)

    return pl.pallas_call(body, name="merge_fwd", grid=(t // ts, d // _MW), in_specs=[ga, gd, blk, blk], out_specs=blk,
                          out_shape=jax.ShapeDtypeStruct((t, d), BF16))(proj, proj, ya, yd)


def _merge_bwd(proj, ya, yd, dm, ts=512):
    t, d = ya.shape
    blk = _vspec((ts, _MW), lambda i, j: (i, j))
    ga = _vspec((ts, _MW), lambda i, j: (i, C_GA // _MW + j))
    gd = _vspec((ts, _MW), lambda i, j: (i, C_GD // _MW + j))

    def body(ga_ref, gd_ref, ya_ref, yd_ref, dm_ref, dga_ref, dgd_ref, dya_ref, dyd_ref):
        _, vjp = jax.vjp(_merge_f, ga_ref[...], gd_ref[...], ya_ref[...], yd_ref[...])
        dga, dgd, dya, dyd = vjp(dm_ref[...])
        dga_ref[...] = dga.astype(BF16)
        dgd_ref[...] = dgd.astype(BF16)
        dya_ref[...] = dya.astype(BF16)
        dyd_ref[...] = dyd.astype(BF16)

    o = jax.ShapeDtypeStruct((t, d), BF16)
    return pl.pallas_call(body, name="merge_bwd", grid=(t // ts, d // _MW), in_specs=[ga, gd, blk, blk, blk],
                          out_specs=[blk] * 4, out_shape=[o] * 4)(proj, proj, ya, yd, dm)


def _shift_down(x, s):
    if s == 0:
        return x
    r = lax.broadcasted_iota(jnp.int32, x.shape, 0)
    return jnp.where(r >= s, pltpu.roll(x, s, 0), 0.0)


def _shift_up(x, s):
    if s == 0:
        return x
    n = x.shape[0]
    r = lax.broadcasted_iota(jnp.int32, x.shape, 0)
    return jnp.where(r < n - s, pltpu.roll(x, n - s, 0), 0.0)


def _conv_fwd(x, w, k):
    out = None
    for j in range(k):
        term = w[j:j + 1, :] * _shift_down(x, k - 1 - j)
        out = term if out is None else out + term
    return out


def _conv_bwd(x, w, dc, k):
    dx = None
    dws = []
    for j in range(k):
        up = _shift_up(dc, k - 1 - j)
        term = w[j:j + 1, :] * up
        dx = term if dx is None else dx + term
        dws.append(jnp.sum(up * x, axis=0, keepdims=True))
    return dx, jnp.concatenate(dws, axis=0)


def _geglu_f(gate, val):
    return jax.nn.gelu(gate, approximate=True) * val


_FW = 256


def _ffn_act_fwd(up, conv_w, bsz):
    t = up.shape[0]
    s = t // bsz
    nj = D_FF // _FW
    xg = _vspec((s, _FW), lambda b, j: (b, j))
    xv = _vspec((s, _FW), lambda b, j: (b, nj + j))
    wg = _vspec((FFN_CONV, _FW), lambda b, j: (0, j))
    wv = _vspec((FFN_CONV, _FW), lambda b, j: (0, nj + j))

    def body(xg_ref, xv_ref, wg_ref, wv_ref, o_ref):
        gate = _conv_fwd(xg_ref[...], wg_ref[...], FFN_CONV)
        val = _conv_fwd(xv_ref[...], wv_ref[...], FFN_CONV)
        o_ref[...] = _geglu_f(gate, val).astype(BF16)

    return pl.pallas_call(body, name="ffn_act_fwd", grid=(bsz, nj), in_specs=[xg, xv, wg, wv],
                          out_specs=_vspec((s, _FW), lambda b, j: (b, j)),
                          out_shape=jax.ShapeDtypeStruct((t, D_FF), BF16))(up, up, conv_w, conv_w)


def _ffn_act_bwd(up, conv_w, dact, bsz):
    t = up.shape[0]
    s = t // bsz
    nj = D_FF // _FW
    xg = _vspec((s, _FW), lambda j, b: (b, j))
    xv = _vspec((s, _FW), lambda j, b: (b, nj + j))
    wg = _vspec((FFN_CONV, _FW), lambda j, b: (0, j))
    wv = _vspec((FFN_CONV, _FW), lambda j, b: (0, nj + j))
    da = _vspec((s, _FW), lambda j, b: (b, j))
    dwo = _vspec((FFN_CONV, _FW), lambda j, b: (0, j))

    def body(xg_ref, xv_ref, wg_ref, wv_ref, da_ref, dxg_ref, dxv_ref, dwg_ref, dwv_ref):
        b = pl.program_id(1)
        xg_, xv_, wg_, wv_ = xg_ref[...], xv_ref[...], wg_ref[...], wv_ref[...]
        gate = _conv_fwd(xg_, wg_, FFN_CONV)
        val = _conv_fwd(xv_, wv_, FFN_CONV)
        _, vjp = jax.vjp(_geglu_f, gate, val)
        dgate, dval = vjp(da_ref[...])
        dxg, dwg = _conv_bwd(xg_, wg_, dgate, FFN_CONV)
        dxv, dwv = _conv_bwd(xv_, wv_, dval, FFN_CONV)
        dxg_ref[...] = dxg.astype(BF16)
        dxv_ref[...] = dxv.astype(BF16)

        @pl.when(b == 0)
        def _():
            dwg_ref[...] = jnp.zeros_like(dwg_ref)
            dwv_ref[...] = jnp.zeros_like(dwv_ref)

        dwg_ref[...] += dwg
        dwv_ref[...] += dwv

    return pl.pallas_call(
        body, name="ffn_act_bwd", grid=(nj, bsz), in_specs=[xg, xv, wg, wv, da], out_specs=[da, da, dwo, dwo],
        out_shape=[jax.ShapeDtypeStruct((t, D_FF), BF16), jax.ShapeDtypeStruct((t, D_FF), BF16),
                   jax.ShapeDtypeStruct((FFN_CONV, D_FF), F32), jax.ShapeDtypeStruct((FFN_CONV, D_FF), F32)],
    )(up, up, conv_w, conv_w, dact)


def _bucket_table():
    qi = np.arange(WINDOW)[:, None]
    kj = np.arange(2 * WINDOW)[None, :]
    dist = WINDOW + qi - kj
    dc = np.maximum(dist, 0)
    max_exact = REL_BUCKETS // 2
    scaled = np.log(np.maximum(dc, 1).astype(np.float32) / np.float32(max_exact)) / np.float32(math.log(REL_MAX_DIST / max_exact))
    large = max_exact + (scaled.astype(np.float32) * np.float32(REL_BUCKETS - max_exact)).astype(np.int32)
    large = np.minimum(large, REL_BUCKETS - 1)
    bucket = np.where(dc < max_exact, dc, large).astype(np.int32)
    in_band = ((dist >= 0) & (dist < WINDOW)).astype(np.int32)
    return bucket, in_band


def _bias_build(rel_bias):
    bucket, _ = _bucket_table()

    def body(rb_ref, idx_ref, o_ref):
        h = pl.program_id(0)
        idx = idx_ref[...]
        acc = jnp.zeros(idx.shape, F32)
        for r in range(REL_BUCKETS):
            acc = jnp.where(idx == r, rb_ref[r, h], acc)
        o_ref[0] = acc

    return pl.pallas_call(
        body, name="bias_build", grid=(AQ_HEADS,),
        in_specs=[pl.BlockSpec(memory_space=pltpu.SMEM), _vspec((WINDOW, 2 * WINDOW), lambda h: (0, 0))],
        out_specs=_vspec((1, WINDOW, 2 * WINDOW), lambda h: (h, 0, 0)),
        out_shape=jax.ShapeDtypeStruct((AQ_HEADS, WINDOW, 2 * WINDOW), F32),
    )(rel_bias, jnp.asarray(bucket))


def _bias_grad(dbias):
    bucket, _ = _bucket_table()

    def body(db_ref, idx_ref, o_ref):
        idx = idx_ref[...]
        db = db_ref[0]
        lane = lax.broadcasted_iota(jnp.int32, (1, LANES), 1)
        acc = jnp.zeros((1, LANES), F32)
        for r in range(REL_BUCKETS):
            s = jnp.sum(jnp.sum(jnp.where(idx == r, db, 0.0), axis=1, keepdims=True), axis=0, keepdims=True)
            acc = jnp.where(lane == r, s, acc)
        o_ref[0] = acc

    return pl.pallas_call(
        body, name="bias_grad", grid=(AQ_HEADS,),
        in_specs=[_vspec((1, WINDOW, 2 * WINDOW), lambda h: (h, 0, 0)), _vspec((WINDOW, 2 * WINDOW), lambda h: (0, 0))],
        out_specs=_vspec((1, 1, LANES), lambda h: (h, 0, 0)),
        out_shape=jax.ShapeDtypeStruct((AQ_HEADS, 1, LANES), F32),
    )(dbias, jnp.asarray(bucket))


def _attn_mask(n):
    qi = lax.broadcasted_iota(jnp.int32, (WINDOW, 2 * WINDOW), 0)
    kj = lax.broadcasted_iota(jnp.int32, (WINDOW, 2 * WINDOW), 1)
    dist = WINDOW + qi - kj
    band = (dist >= 0) & (dist < WINDOW)
    return band & ((kj >= WINDOW) | (n > 0))


def _attn_probs(qk, bias, sink, mask):
    s = jnp.where(mask, qk * (A_HD ** -0.5) + bias, NEG_INF)
    m = jnp.maximum(jnp.max(s, axis=-1, keepdims=True), sink)
    p = jnp.exp(s - m)
    es = jnp.exp(sink - m)
    inv = 1.0 / (jnp.sum(p, axis=-1, keepdims=True) + es)
    return p * inv, es * inv


def _attn_fwd(proj, bias, sinks, bsz):
    t = proj.shape[0]
    s = t // bsz
    nb = s // WINDOW
    grp = AQ_HEADS // AKV_HEADS

    def body(q_ref, k_ref, v_ref, bias_ref, sink_ref, y_ref, kp_ref, vp_ref):
        kp_ref[0:WINDOW, :] = jnp.zeros((WINDOW, LANES), BF16)
        vp_ref[0:WINDOW, :] = jnp.zeros((WINDOW, LANES), BF16)
        kp_ref[WINDOW:, :] = k_ref[...].astype(BF16)
        vp_ref[WINDOW:, :] = v_ref[...].astype(BF16)

        def blk(n, carry):
            r0 = pl.multiple_of(n * WINDOW, WINDOW)
            mask = _attn_mask(n)
            kband = kp_ref[pl.ds(r0, 2 * WINDOW), :]
            vband = vp_ref[pl.ds(r0, 2 * WINDOW), :]
            qb = q_ref[pl.ds(r0, WINDOW), :].astype(BF16)
            heads = range(AQ_HEADS)
            hsl = lambda h: slice(h * A_HD, (h + 1) * A_HD)
            kbs = [kband[:, hsl(kv)] for kv in range(AKV_HEADS)]
            vbs = [vband[:, hsl(kv)] for kv in range(AKV_HEADS)]
            qks = [lax.dot_general(qb[:, hsl(h)], kbs[h // grp], _NT, preferred_element_type=F32) for h in heads]
            probs = [_attn_probs(qks[h], bias_ref[h], sink_ref[0, h], mask)[0] for h in heads]
            outs = [jnp.dot(probs[h].astype(BF16), vbs[h // grp], preferred_element_type=F32) for h in heads]
            y_ref[pl.ds(r0, WINDOW), :] = jnp.concatenate(outs, axis=1).astype(BF16)
            return carry

        lax.fori_loop(0, nb, blk, 0)

    return pl.pallas_call(
        body, name="attn_fwd", grid=(bsz,),
        in_specs=[_vspec((s, AQ), lambda b: (b, C_Q // AQ)), _vspec((s, AKV), lambda b: (b, C_K // AKV)),
                  _vspec((s, AKV), lambda b: (b, C_V // AKV)),
                  _vspec((AQ_HEADS, WINDOW, 2 * WINDOW), lambda b: (0, 0, 0)), pl.BlockSpec(memory_space=pltpu.SMEM)],
        out_specs=_vspec((s, AQ), lambda b: (b, 0)), out_shape=jax.ShapeDtypeStruct((t, AQ), BF16),
        scratch_shapes=[pltpu.VMEM((s + WINDOW, LANES), BF16), pltpu.VMEM((s + WINDOW, LANES), BF16)],
    )(proj, proj, proj, bias, sinks)


def _attn_bwd(proj, bias, sinks, dy, bsz, plug=None):
    t = proj.shape[0]
    s = t // bsz
    nb = s // WINDOW
    grp = AQ_HEADS // AKV_HEADS
    scale = A_HD ** -0.5

    def body(q_ref, k_ref, v_ref, bias_ref, sink_ref, dy_ref, dq_ref, dk_ref, dv_ref, dbias_ref, dsink_ref,
             kp_ref, vp_ref, dkp_ref, dvp_ref):
        b = pl.program_id(0)
        kp_ref[0:WINDOW, :] = jnp.zeros((WINDOW, LANES), BF16)
        vp_ref[0:WINDOW, :] = jnp.zeros((WINDOW, LANES), BF16)
        kp_ref[WINDOW:, :] = k_ref[...].astype(BF16)
        vp_ref[WINDOW:, :] = v_ref[...].astype(BF16)
        dkp_ref[...] = jnp.zeros_like(dkp_ref)
        dvp_ref[...] = jnp.zeros_like(dvp_ref)

        @pl.when(b == 0)
        def _():
            dbias_ref[...] = jnp.zeros_like(dbias_ref)
            dsink_ref[...] = jnp.zeros_like(dsink_ref)

        def blk(n, carry):
            r0 = pl.multiple_of(n * WINDOW, WINDOW)
            mask = _attn_mask(n)
            kband = kp_ref[pl.ds(r0, 2 * WINDOW), :]
            vband = vp_ref[pl.ds(r0, 2 * WINDOW), :]
            qb = q_ref[pl.ds(r0, WINDOW), :].astype(BF16)
            dyb = dy_ref[pl.ds(r0, WINDOW), :].astype(BF16)
            heads = range(AQ_HEADS)
            hsl = lambda h: slice(h * A_HD, (h + 1) * A_HD)
            kbs = [kband[:, hsl(kv)] for kv in range(AKV_HEADS)]
            vbs = [vband[:, hsl(kv)] for kv in range(AKV_HEADS)]
            qhs = [qb[:, hsl(h)] for h in heads]
            dyhs = [dyb[:, hsl(h)] for h in heads]
            qks = [lax.dot_general(qhs[h], kbs[h // grp], _NT, preferred_element_type=F32) for h in heads]
            dprobs = [lax.dot_general(dyhs[h], vbs[h // grp], _NT, preferred_element_type=F32) for h in heads]
            pbs, dsbs = [], []
            for h in heads:
                probs, psink = _attn_probs(qks[h], bias_ref[h], sink_ref[0, h], mask)
                rowdot = jnp.sum(probs * dprobs[h], axis=-1, keepdims=True)
                ds = probs * (dprobs[h] - rowdot)
                dbias_ref[h] += ds
                dsink_ref[h] += jnp.sum(-psink * rowdot, axis=0, keepdims=True) + jnp.zeros((1, LANES), F32)
                pbs.append(probs.astype(BF16))
                dsbs.append(ds.astype(BF16))
            dvhs = [lax.dot_general(pbs[h], dyhs[h], _TN, preferred_element_type=F32) for h in heads]
            dqs = [jnp.dot(dsbs[h], kbs[h // grp], preferred_element_type=F32) * scale for h in heads]
            dkhs = [lax.dot_general(dsbs[h], qhs[h], _TN, preferred_element_type=F32) * scale for h in heads]
            dks = [sum(dkhs[kv * grp + 1:(kv + 1) * grp], dkhs[kv * grp]) for kv in range(AKV_HEADS)]
            dvs = [sum(dvhs[kv * grp + 1:(kv + 1) * grp], dvhs[kv * grp]) for kv in range(AKV_HEADS)]
            dq_ref[pl.ds(r0, WINDOW), :] = jnp.concatenate(dqs, axis=1).astype(BF16)
            dkp_ref[pl.ds(r0, 2 * WINDOW), :] += jnp.concatenate(dks, axis=1)
            dvp_ref[pl.ds(r0, 2 * WINDOW), :] += jnp.concatenate(dvs, axis=1)
            return carry

        lax.fori_loop(0, nb, blk, 0)
        dk_ref[...] = dkp_ref[WINDOW:, :].astype(BF16)
        dv_ref[...] = dvp_ref[WINDOW:, :].astype(BF16)

    kvs = jax.ShapeDtypeStruct((t, AKV), BF16)
    outs, extra = _plugged_call(
        body, plug, _grid_ends((bsz,)), (proj, proj, proj, bias, sinks, dy), name="attn_bwd", grid=(bsz,),
        in_specs=[_vspec((s, AQ), lambda b: (b, C_Q // AQ)), _vspec((s, AKV), lambda b: (b, C_K // AKV)),
                  _vspec((s, AKV), lambda b: (b, C_V // AKV)),
                  _vspec((AQ_HEADS, WINDOW, 2 * WINDOW), lambda b: (0, 0, 0)), pl.BlockSpec(memory_space=pltpu.SMEM),
                  _vspec((s, AQ), lambda b: (b, 0))],
        out_specs=[_vspec((s, AQ), lambda b: (b, 0)), _vspec((s, AKV), lambda b: (b, 0)), _vspec((s, AKV), lambda b: (b, 0)),
                   _vspec((AQ_HEADS, WINDOW, 2 * WINDOW), lambda b: (0, 0, 0)), _vspec((AQ_HEADS, 1, LANES), lambda b: (0, 0, 0))],
        out_shape=[jax.ShapeDtypeStruct((t, AQ), BF16), kvs, kvs,
                   jax.ShapeDtypeStruct((AQ_HEADS, WINDOW, 2 * WINDOW), F32), jax.ShapeDtypeStruct((AQ_HEADS, 1, LANES), F32)],
        scratch_shapes=[pltpu.VMEM((s + WINDOW, LANES), BF16), pltpu.VMEM((s + WINDOW, LANES), BF16),
                        pltpu.VMEM((s + WINDOW, LANES), F32), pltpu.VMEM((s + WINDOW, LANES), F32)])
    return (*outs, extra)


def _dn_act_f(c, is_qk):
    a = jax.nn.silu(c)
    outs = []
    for h in range(DN_HEADS):
        ah = a[:, h * DN_HD:(h + 1) * DN_HD]
        nh = ah * lax.rsqrt(jnp.sum(ah * ah, axis=-1, keepdims=True) + L2_EPS)
        outs.append(jnp.where(is_qk, nh, ah))
    return jnp.concatenate(outs, axis=1)


def _dn_prep_fwd(proj, conv_w, bsz):
    t = proj.shape[0]
    s = t // bsz
    blk = _vspec((s, DN), lambda b, j: (b, j))
    wsp = _vspec((DN_CONV, DN), lambda b, j: (0, j))

    def body(x_ref, w_ref, o_ref):
        j = pl.program_id(1)
        o_ref[...] = _dn_act_f(_conv_fwd(x_ref[...], w_ref[...], DN_CONV), j < 2)

    return pl.pallas_call(body, name="dn_prep_fwd", grid=(bsz, 3), in_specs=[blk, wsp], out_specs=blk,
                          out_shape=jax.ShapeDtypeStruct((t, 3 * DN), F32))(proj, conv_w)


def _dn_prep_bwd(proj, conv_w, dqkvn, bsz):
    t = proj.shape[0]
    s = t // bsz
    blk = _vspec((s, DN), lambda j, b: (b, j))
    wsp = _vspec((DN_CONV, DN), lambda j, b: (0, j))

    def body(x_ref, w_ref, d_ref, dx_ref, dw_ref):
        j, b = pl.program_id(0), pl.program_id(1)
        x, w = x_ref[...], w_ref[...]
        c = _conv_fwd(x, w, DN_CONV)
        _, vjp = jax.vjp(lambda cc: _dn_act_f(cc, j < 2), c)
        (dc,) = vjp(d_ref[0])
        dx, dw = _conv_bwd(x, w, dc, DN_CONV)
        dx_ref[...] = dx.astype(BF16)

        @pl.when(b == 0)
        def _():
            dw_ref[...] = jnp.zeros_like(dw_ref)

        dw_ref[...] += dw

    return pl.pallas_call(
        body, name="dn_prep_bwd", grid=(3, bsz),
        in_specs=[blk, wsp, _vspec((1, s, DN), lambda j, b: (j, b, 0))], out_specs=[blk, wsp],
        out_shape=[jax.ShapeDtypeStruct((t, 3 * DN), BF16), jax.ShapeDtypeStruct((DN_CONV, 3 * DN), F32)],
    )(proj, conv_w, dqkvn)


def _bg_f(x, alog, dt):
    lane = lax.broadcasted_iota(jnp.int32, x.shape, 1)
    beta = jax.nn.sigmoid(x)
    g = -jnp.exp(alog) * jax.nn.softplus(x + dt)
    return jnp.where(lane < DN_HEADS, beta, jnp.where(lane < 2 * DN_HEADS, g, 0.0))


def _bg_fwd(proj, alog, dt, bsz):
    t = proj.shape[0]
    s = t // bsz
    vec = _vspec((1, LANES), lambda b: (0, 0))

    def body(x_ref, a_ref, d_ref, o_ref):
        o_ref[...] = _bg_f(x_ref[...], a_ref[...], d_ref[...])

    return pl.pallas_call(body, name="bg_fwd", grid=(bsz,), in_specs=[_vspec((s, LANES), lambda b: (b, C_BD // LANES)), vec, vec],
                          out_specs=_vspec((s, LANES), lambda b: (b, 0)), out_shape=jax.ShapeDtypeStruct((t, LANES), F32))(proj, alog, dt)


def _bg_bwd(proj, alog, dt, dbg4, bsz):
    t = proj.shape[0]
    s = t // bsz
    vec = _vspec((1, LANES), lambda b: (0, 0))

    def body(x_ref, a_ref, d_ref, g4_ref, dx_ref, da_ref, dd_ref):
        b = pl.program_id(0)
        lane = lax.broadcasted_iota(jnp.int32, (s, LANES), 1)
        dbg = jnp.zeros((s, LANES), F32)
        for h in range(DN_HEADS):
            gh = g4_ref[:, h * DN_HD:(h + 1) * DN_HD]
            dbg = jnp.where(lane == h, gh[:, 0:1], dbg)
            dbg = jnp.where(lane == DN_HEADS + h, gh[:, 1:2], dbg)
        _, vjp = jax.vjp(_bg_f, x_ref[...], a_ref[...], d_ref[...])
        dx, da, dd = vjp(dbg)
        dx_ref[...] = dx.astype(BF16)

        @pl.when(b == 0)
        def _():
            da_ref[...] = jnp.zeros_like(da_ref)
            dd_ref[...] = jnp.zeros_like(dd_ref)

        da_ref[...] += da
        dd_ref[...] += dd

    return pl.pallas_call(
        body, name="bg_bwd", grid=(bsz,),
        in_specs=[_vspec((s, LANES), lambda b: (b, C_BD // LANES)), vec, vec, _vspec((s, DN), lambda b: (b, 0))],
        out_specs=[_vspec((s, LANES), lambda b: (b, 0)), vec, vec],
        out_shape=[jax.ShapeDtypeStruct((t, LANES), BF16), jax.ShapeDtypeStruct((1, LANES), F32), jax.ShapeDtypeStruct((1, LANES), F32)],
    )(proj, alog, dt, dbg4)


def _dn_out_f(o, z, w):
    outs = []
    for h in range(DN_HEADS):
        sl = slice(h * DN_HD, (h + 1) * DN_HD)
        outs.append(_rms(o[:, sl], w) * jax.nn.silu(z[:, sl]))
    return jnp.concatenate(outs, axis=1)


def _dn_out_fwd(o, proj, w, ts=512):
    t = o.shape[0]
    blk = _vspec((ts, DN), lambda i: (i, 0))
    zsp = _vspec((ts, DN), lambda i: (i, C_DZ // DN))
    vec = _vspec((1, DN_HD), lambda i: (0, 0))

    def body(o_ref, z_ref, w_ref, y_ref):
        y_ref[...] = _dn_out_f(o_ref[...], z_ref[...], w_ref[...]).astype(BF16)

    return pl.pallas_call(body, name="dn_out_fwd", grid=(t // ts,), in_specs=[blk, zsp, vec], out_specs=blk,
                          out_shape=jax.ShapeDtypeStruct((t, DN), BF16))(o, proj, w)


def _dn_out_bwd(o, proj, w, dy, ts=512):
    t = o.shape[0]
    blk = _vspec((ts, DN), lambda i: (i, 0))
    zsp = _vspec((ts, DN), lambda i: (i, C_DZ // DN))
    vec = _vspec((1, DN_HD), lambda i: (0, 0))

    def body(o_ref, z_ref, w_ref, dy_ref, do_ref, dz_ref, dw_ref):
        i = pl.program_id(0)
        _, vjp = jax.vjp(_dn_out_f, o_ref[...], z_ref[...], w_ref[...])
        do, dz, dw = vjp(dy_ref[...])
        do_ref[...] = do
        dz_ref[...] = dz.astype(BF16)

        @pl.when(i == 0)
        def _():
            dw_ref[...] = jnp.zeros_like(dw_ref)

        dw_ref[...] += dw

    return pl.pallas_call(
        body, name="dn_out_bwd", grid=(t // ts,), in_specs=[blk, zsp, vec, blk], out_specs=[blk, blk, vec],
        out_shape=[jax.ShapeDtypeStruct((t, DN), F32), jax.ShapeDtypeStruct((t, DN), BF16), jax.ShapeDtypeStruct((1, DN_HD), F32)],
    )(o, proj, w, dy)


_C = DN_CHUNK


def _dot(a, b, dims):
    return lax.dot_general(a.astype(BF16), b.astype(BF16), dims, preferred_element_type=F32)


def _split(a):
    hi = a.astype(BF16)
    return hi, (a - hi.astype(F32)).astype(BF16)


def _dot3(a, b, dims):
    (ah, al), (bh, bl) = (a if isinstance(a, tuple) else _split(a)), (b if isinstance(b, tuple) else _split(b))
    mm = lambda x, y: lax.dot_general(x, y, dims, preferred_element_type=F32)
    return mm(ah, bh) + (mm(ah, bl) + mm(al, bh))


_NN = (((1,), (0,)), ((), ()))
_NT = (((1,), (1,)), ((), ()))
_TN = (((0,), (0,)), ((), ()))


_SUB = 8


def _tri_inverses(ls, lts):
    ri8 = lax.broadcasted_iota(jnp.int32, (_SUB, _C), 0)
    ci8 = lax.broadcasted_iota(jnp.int32, (_SUB, _C), 1)
    nblk = _C // _SUB
    ts = []
    for lt in lts:
        blocks = [jnp.where(ci8 == ri8 + _SUB * b, 1.0, 0.0).astype(F32) for b in range(nblk)]
        for r in range(1, _SUB):
            for b in range(nblk):
                coef = lt[_SUB * b:_SUB * (b + 1), _SUB * b + r:_SUB * b + r + 1]
                row = jnp.sum(coef * blocks[b], axis=0, keepdims=True)
                blocks[b] = jnp.where(ri8 == r, blocks[b] - row, blocks[b])
        ts.append(jnp.concatenate(blocks, axis=0))
    ri = lax.broadcasted_iota(jnp.int32, (_C, _C), 0)
    ci = lax.broadcasted_iota(jnp.int32, (_C, _C), 1)
    s = _SUB
    while s < _C:
        shift = s.bit_length()
        quad = ((ri >> shift) == (ci >> shift)) & ((ri & s) != 0) & ((ci & s) == 0)
        offs = [jnp.where(quad, l, 0.0) for l in ls]
        tsp = [_split(t) for t in ts]
        left = [_dot3(tp, off, _NN) for tp, off in zip(tsp, offs)]
        ts = [t - _dot3(lo, tp, _NN) for t, lo, tp in zip(ts, left, tsp)]
        s *= 2
    return ts


_SEG = 512
_HEADS = tuple(range(DN_HEADS))


def _hsl(hh):
    return slice(hh * DN_HD, (hh + 1) * DN_HD)


def _chunk_specs(bsz, nseg, reverse):
    seg = (lambda i: nseg - 1 - i) if reverse else (lambda i: i)
    ncs = _SEG // _C
    col = lambda off: _vspec((bsz, _SEG, DN), lambda i: (0, seg(i), off))
    return (col, _vspec((bsz, _SEG, LANES), lambda i: (0, seg(i), 0)),
            _vspec((bsz, DN_HEADS, ncs, _C), lambda i: (0, 0, seg(i), 0)),
            _vspec((bsz, DN_HEADS, ncs, DN_HD, DN_HD), lambda i: (0, 0, seg(i), 0, 0)))


def _chunk_pre(q_ref, k_ref, v_ref, bg_ref, gr_ref, c, bb, hh):
    r0 = pl.multiple_of(c * _C, _C)
    ri = lax.broadcasted_iota(jnp.int32, (_C, _C), 0)
    ci = lax.broadcasted_iota(jnp.int32, (_C, _C), 1)
    q = q_ref[bb, pl.ds(r0, _C), _hsl(hh)] * (DN_HD ** -0.5)
    k = k_ref[bb, pl.ds(r0, _C), _hsl(hh)]
    v = v_ref[bb, pl.ds(r0, _C), _hsl(hh)]
    bgc = bg_ref[bb, pl.ds(r0, _C), :]
    beta = bgc[:, hh:hh + 1]
    g_col = bgc[:, DN_HEADS + hh:DN_HEADS + hh + 1]
    g_row = gr_ref[bb, hh, pl.ds(c, 1), :]
    gc_col = jnp.sum(jnp.where(ri >= ci, g_row, 0.0), axis=1, keepdims=True)
    gc_row = jnp.sum(jnp.where(ri <= ci, g_col, 0.0), axis=0, keepdims=True)
    gc_last = jnp.sum(g_col, axis=0, keepdims=True)
    diff = gc_col - gc_row
    decay = jnp.where(ri >= ci, jnp.exp(jnp.where(ri >= ci, diff, 0.0)), 0.0)
    diff_t = gc_row - gc_col
    decay_t = jnp.where(ri <= ci, jnp.exp(jnp.where(ri <= ci, diff_t, 0.0)), 0.0)
    eg = jnp.exp(gc_col)
    et = jnp.exp(gc_last - gc_col)
    gl = jnp.exp(gc_last)
    kb = k * beta
    vb = v * beta
    return dict(r0=r0, bb=bb, hh=hh, q=q, k=k, v=v, beta=beta, decay=decay, decay_t=decay_t, eg=eg, et=et, gl=gl, kb=kb, vb=vb,
                ri=ri, ci=ci)


def _chunk_solve(ms, with_transposes):
    for m in ms:
        m["kk_t"] = _dot(m["k"], m["kb"], _NT)
        m["qk"] = _dot(m["q"], m["k"], _NT)
        m["kk"] = _dot(m["kb"], m["k"], _NT)
        if with_transposes:
            m["qk_t"] = _dot(m["k"], m["q"], _NT)
    tms = _tri_inverses([jnp.where(m["ri"] > m["ci"], m["kk"] * m["decay"], 0.0) for m in ms],
                        [jnp.where(m["ri"] < m["ci"], m["kk_t"] * m["decay_t"], 0.0) for m in ms])
    for m, tm in zip(ms, tms):
        m["tm"] = tm
    for m in ms:
        rhs = jnp.concatenate([m["vb"], m["kb"] * m["eg"]], axis=1)
        m["tm"] = _split(m["tm"])
        m["sol"] = _dot3(m["tm"], rhs, _NN)
        m["intra"] = jnp.where(m["ri"] >= m["ci"], m["qk"] * m["decay"], 0.0)


def _dn_chunk_fwd(qkvn, bg, g_rows, bsz, plug=None):
    t = qkvn.shape[0]
    s = t // bsz
    nc, nseg = s // _C, s // _SEG
    pairs = [(bb, hh) for bb in range(bsz) for hh in _HEADS]

    def body(q_ref, k_ref, v_ref, bg_ref, gr_ref, o_ref, st_ref, s_ref):
        @pl.when(pl.program_id(0) == 0)
        def _():
            s_ref[...] = jnp.zeros_like(s_ref)

        def chunk(c, carry):
            ms = [_chunk_pre(q_ref, k_ref, v_ref, bg_ref, gr_ref, c, bb, hh) for bb, hh in pairs]
            _chunk_solve(ms, False)
            sts = [s_ref[i] for i in range(len(pairs))]
            for (bb, hh), st in zip(pairs, sts):
                st_ref[bb, hh, c] = st
            ws = [_dot(m["sol"][:, DN_HD:], st, _NN) for m, st in zip(ms, sts)]
            qs = [_dot(m["q"] * m["eg"], st, _NN) for m, st in zip(ms, sts)]
            v_new = [m["sol"][:, :DN_HD] - a for m, a in zip(ms, ws)]
            iv = [_dot(m["intra"], vn, _NN) for m, vn in zip(ms, v_new)]
            upd = [_dot(m["k"] * m["et"], vn, _TN) for m, vn in zip(ms, v_new)]
            for i, (bb, hh) in enumerate(pairs):
                s_ref[i] = sts[i] * ms[i]["gl"] + upd[i]
                o_ref[bb, pl.ds(ms[i]["r0"], _C), _hsl(hh)] = qs[i] + iv[i]
            return carry

        lax.fori_loop(0, _SEG // _C, chunk, 0)

    col, bgs, grs, sts_spec = _chunk_specs(bsz, nseg, False)
    q3, bg3 = qkvn.reshape(bsz, s, 3 * DN), bg.reshape(bsz, s, LANES)
    (o, states), extra = _plugged_call(
        body, plug, _grid_ends((nseg,)), (q3, q3, q3, bg3, g_rows), name="dn_chunk_fwd", grid=(nseg,),
        in_specs=[col(0), col(1), col(2), bgs, grs], out_specs=[col(0), sts_spec],
        out_shape=[jax.ShapeDtypeStruct((bsz, s, DN), F32), jax.ShapeDtypeStruct((bsz, DN_HEADS, nc, DN_HD, DN_HD), F32)],
        scratch_shapes=[pltpu.VMEM((bsz * DN_HEADS, DN_HD, DN_HD), F32)])
    return o.reshape(t, DN), states, extra


def _dn_chunk_bwd(qkvn, bg, g_rows, states, do, bsz, plug=None):
    t = qkvn.shape[0]
    s = t // bsz
    nc, nseg = s // _C, s // _SEG
    pairs = [(bb, hh) for bb in range(bsz) for hh in _HEADS]

    def body(q_ref, k_ref, v_ref, bg_ref, gr_ref, st_ref, do_ref, dqkv_ref, dbg_ref, ds_ref):
        @pl.when(pl.program_id(0) == 0)
        def _():
            ds_ref[...] = jnp.zeros_like(ds_ref)

        def chunk(cc, carry):
            c = _SEG // _C - 1 - cc
            ms = [_chunk_pre(q_ref, k_ref, v_ref, bg_ref, gr_ref, c, bb, hh) for bb, hh in pairs]
            _chunk_solve(ms, True)
            ri, ci = ms[0]["ri"], ms[0]["ci"]
            for i, m in enumerate(ms):
                m["st"] = st_ref[m["bb"], m["hh"], c]
                m["ds_out"] = ds_ref[i]
                m["do"] = do_ref[m["bb"], pl.ds(m["r0"], _C), _hsl(m["hh"])]
                m["w"] = m["sol"][:, DN_HD:]
            for m in ms:
                m["v_new"] = m["sol"][:, :DN_HD] - _dot(m["w"], m["st"], _NN)
            for m in ms:
                m["q_dec"], m["k_tail"] = m["q"] * m["eg"], m["k"] * m["et"]
                m["dk_tail"] = _dot(m["v_new"], m["ds_out"], _NT)
                m["dv_new"] = _dot(m["k_tail"], m["ds_out"], _NN) + _dot(m["intra"], m["do"], _TN)
                m["dq_dec"] = _dot(m["do"], m["st"], _NT)
                m["ds_in"] = m["ds_out"] * m["gl"] + _dot(m["q_dec"], m["do"], _TN)
                m["dintra"] = jnp.where(ri >= ci, _dot(m["do"], m["v_new"], _NT), 0.0)
                m["dintra_t"] = jnp.where(ri <= ci, _dot(m["v_new"], m["do"], _NT), 0.0)
            for m in ms:
                m["dw"] = -_dot(m["dv_new"], m["st"], _NT)
                m["ds_in"] = m["ds_in"] - _dot(m["w"], m["dv_new"], _TN)
            for m in ms:
                dsol = jnp.concatenate([m["dv_new"], m["dw"]], axis=1)
                m["drhs"] = _dot3(m["tm"], dsol, _TN)
            for m in ms:
                m["dl"] = jnp.where(ri > ci, -_dot(m["drhs"], m["sol"], _NT), 0.0)
                m["dl_t"] = jnp.where(ri < ci, -_dot(m["sol"], m["drhs"], _NT), 0.0)
            for m in ms:
                m["dkb2"] = _dot(m["dl"] * m["decay"], m["k"], _NN)
                m["dk"] = _dot(m["dl_t"] * m["decay_t"], m["kb"], _NN) + _dot(m["dintra_t"] * m["decay_t"], m["q"], _NN)
                m["dq"] = _dot(m["dintra"] * m["decay"], m["k"], _NN)
            for m in ms:
                _chunk_bwd_finish(m)
            for m in ms:
                ones_ge = jnp.where(ri <= ci, 1.0, 0.0).astype(BF16)
                gh, gl_ = _split(m["dgc"] + jnp.zeros((_C, LANES), F32))
                m["dg_b"] = jnp.dot(ones_ge, gh, preferred_element_type=F32) + jnp.dot(ones_ge, gl_, preferred_element_type=F32)
            lane = lax.broadcasted_iota(jnp.int32, (_C, LANES), 1)
            for i, m in enumerate(ms):
                bb, hh, rows = m["bb"], m["hh"], pl.ds(m["r0"], _C)
                dqkv_ref[0, bb, rows, _hsl(hh)] = m["dq"] * (DN_HD ** -0.5)
                dqkv_ref[1, bb, rows, _hsl(hh)] = m["dk"]
                dqkv_ref[2, bb, rows, _hsl(hh)] = m["dv"]
                dbg_ref[bb, rows, _hsl(hh)] = jnp.where(lane == 0, m["dbeta"], jnp.where(lane == 1, m["dg_b"], 0.0))
                ds_ref[i] = m["ds_in"]
            return carry

        lax.fori_loop(0, _SEG // _C, chunk, 0)

    col, bgs, grs, sts_spec = _chunk_specs(bsz, nseg, True)
    q3, bg3, do3 = qkvn.reshape(bsz, s, 3 * DN), bg.reshape(bsz, s, LANES), do.reshape(bsz, s, DN)
    (dqkv, dbg), extra = _plugged_call(
        body, plug, _grid_ends((nseg,)), (q3, q3, q3, bg3, g_rows, states, do3), name="dn_chunk_bwd", grid=(nseg,),
        in_specs=[col(0), col(1), col(2), bgs, grs, sts_spec, col(0)],
        out_specs=[_vspec((3, bsz, _SEG, DN), lambda i: (0, 0, nseg - 1 - i, 0)), col(0)],
        out_shape=[jax.ShapeDtypeStruct((3, bsz, s, DN), F32), jax.ShapeDtypeStruct((bsz, s, DN), F32)],
        scratch_shapes=[pltpu.VMEM((bsz * DN_HEADS, DN_HD, DN_HD), F32)])
    return dqkv.reshape(3, t, DN), dbg.reshape(t, DN), extra


def _chunk_bwd_finish(m):
    q, k, v, beta, decay, decay_t = m["q"], m["k"], m["v"], m["beta"], m["decay"], m["decay_t"]
    eg, et, gl, kb, dl, dl_t, dintra, dintra_t = m["eg"], m["et"], m["gl"], m["kb"], m["dl"], m["dl_t"], m["dintra"], m["dintra_t"]
    dq_dec, dk_tail, dq, dk = m["dq_dec"], m["dk_tail"], m["dq"], m["dk"]
    dgl = jnp.sum(jnp.sum(m["ds_out"] * m["st"], axis=1, keepdims=True), axis=0, keepdims=True)
    dvb, dkbeg = m["drhs"][:, :DN_HD], m["drhs"][:, DN_HD:]
    dkb = dkbeg * eg + m["dkb2"]
    deg = jnp.sum(dkbeg * kb, axis=1, keepdims=True)
    em = (dl * m["kk"] + dintra * m["qk"]) * decay
    em_t = (dl_t * m["kk_t"] + dintra_t * m["qk_t"]) * decay_t
    dgc = jnp.sum(em, axis=1, keepdims=True) - jnp.sum(em_t, axis=1, keepdims=True)
    dq = dq + dq_dec * eg
    deg = deg + jnp.sum(dq_dec * q, axis=1, keepdims=True)
    dk = dk + dk_tail * et
    det = jnp.sum(dk_tail * k, axis=1, keepdims=True)
    dgc = dgc + deg * eg - det * et
    dgc_last = jnp.sum(det * et, axis=0, keepdims=True) + dgl * gl
    rcol = lax.broadcasted_iota(jnp.int32, (_C, 1), 0)
    m["dgc"] = dgc + jnp.where(rcol == _C - 1, dgc_last, 0.0)
    m["dq"] = dq
    m["dk"] = dk + dkb * beta
    m["dbeta"] = jnp.sum(dkb * k, axis=1, keepdims=True) + jnp.sum(dvb * v, axis=1, keepdims=True)
    m["dv"] = dvb * beta


def _mod_fwd(c_all, ada_w_loc, ada_b_loc):
    n, cols = c_all.shape[0], ada_w_loc.shape[1]

    def body(c_ref, w_ref, b_ref, o_ref):
        o_ref[...] = _dot(jax.nn.silu(c_ref[...]), w_ref[...], _NN) + b_ref[...]

    return pl.pallas_call(body, name="mod_fwd", out_shape=jax.ShapeDtypeStruct((n, cols), F32))(c_all, ada_w_loc, ada_b_loc)


def _ada_grad(c_all, dmod_loc, dmod_all):
    d, cols = c_all.shape[1], dmod_loc.shape[1]

    def body(c_ref, dl_ref, da_ref, gw_ref, gb_ref):
        gw_ref[...] = _dot(jax.nn.silu(c_ref[...]), dl_ref[...], _TN)
        gb_ref[...] = jnp.sum(da_ref[...], axis=0, keepdims=True)

    return pl.pallas_call(body, name="ada_grad", out_shape=[jax.ShapeDtypeStruct((d, cols), F32),
                                                           jax.ShapeDtypeStruct((1, dmod_all.shape[1]), F32)])(c_all, dmod_loc, dmod_all)


ELEMENTWISE_BLOCK_BYTES = 3 * 2 ** 19


def _row_tile(r, c=1024):
    fits = [tr for tr in range(16, r + 1, 16) if tr * c * 4 <= ELEMENTWISE_BLOCK_BYTES]
    if not fits:
        return r
    whole = [tr for tr in fits if r % tr == 0]
    return whole[-1] if whole else fits[-1]


def _adamw(w, m, v, grads, name):
    r, c = w.shape
    tr = _row_tile(r, c)
    blk = _vspec((tr, c), lambda i: (i, 0))
    n = len(grads)

    def body(*refs):
        w_ref, m_ref, v_ref = refs[:3]
        g_ref, d_ref, mo_ref, vo_ref = refs[3 + n:]
        g = refs[3][...]
        for p in refs[4:3 + n]:
            g = g + p[...]
        g_ref[...] = g
        d_ref[...], mo_ref[...], vo_ref[...] = _adamw_math(w_ref[...], m_ref[...], v_ref[...], g)

    o = jax.ShapeDtypeStruct((r, c), F32)
    return pl.pallas_call(body, name=name, grid=(pl.cdiv(r, tr),), in_specs=[blk] * (3 + n), out_specs=[blk] * 4,
                          out_shape=[o] * 4)(w, m, v, *grads)


def _adamw_math(w, m, v, g):
    m_new = ADAM_B1 * m + (1.0 - ADAM_B1) * g
    v_new = ADAM_B2 * v + (1.0 - ADAM_B2) * jnp.square(g)
    m_hat = m_new / (1.0 - ADAM_B1 ** ADAM_STEP)
    v_hat = v_new / (1.0 - ADAM_B2 ** ADAM_STEP)
    return -ADAM_LR * (m_hat / (jnp.sqrt(v_hat) + ADAM_EPS) + ADAM_WD * w), m_new, v_new


def _adamw_small(ws, ms, vs, gs, name):
    n = len(ws)

    def body(*refs):
        for i in range(n):
            w_ref, m_ref, v_ref, g_ref = (refs[j * n + i] for j in range(4))
            go_ref, d_ref, mo_ref, vo_ref = refs[4 * n + 4 * i:4 * n + 4 * i + 4]
            g = g_ref[...]
            go_ref[...] = g
            d_ref[...], mo_ref[...], vo_ref[...] = _adamw_math(w_ref[...], m_ref[...], v_ref[...], g)

    res = pl.pallas_call(body, name=name, out_shape=[jax.ShapeDtypeStruct(a.shape, F32) for a in ws for _ in range(4)])(
        *ws, *ms, *vs, *gs)
    return [res[4 * i:4 * i + 4] for i in range(n)]


def _sum_lead(x, name):
    p, r, c = x.shape
    tr = _row_tile(r, c)

    def body(x_ref, o_ref):
        acc = x_ref[0].astype(F32)
        for i in range(1, p):
            acc = acc + x_ref[i].astype(F32)
        o_ref[...] = acc

    return pl.pallas_call(body, name=name, grid=(pl.cdiv(r, tr),), in_specs=[_vspec((p, tr, c), lambda i: (0, i, 0))],
                          out_specs=_vspec((tr, c), lambda i: (i, 0)), out_shape=jax.ShapeDtypeStruct((r, c), F32))(x)


def _allgather8(x_shard, name):
    m_per, n = x_shard.shape

    def body(x_ref, out_ref, send_sems, recv_sems, local_sem):
        x, y, c = lax.axis_index("x"), lax.axis_index("y"), lax.axis_index("c")
        me, sibling = (x, y, c), (x, y, 1 - c)
        chips = [(1 - x, y), (x, 1 - y), (1 - x, 1 - y)]

        def rows(px, py, pc):
            return out_ref.at[pl.ds((4 * px + 2 * py + pc) * m_per, m_per), :]

        def copy(k, block, to, src=None):
            return pltpu.make_async_remote_copy(
                src_ref=rows(*block) if src is None else src, dst_ref=rows(*block), send_sem=send_sems.at[k],
                recv_sem=recv_sems.at[k], device_id=to, device_id_type=MESH)

        mine = pltpu.make_async_copy(x_ref, rows(*me), local_sem)
        mine.start()
        first = [copy(0, me, sibling, src=x_ref)]
        first += [copy(1 + j, me, (*chip, c), src=x_ref) for j, chip in enumerate(chips)]
        for cp in first:
            cp.start()
        passed = [copy(4 + j, (*chip, c), sibling) for j, chip in enumerate(chips)]
        for j, chip in enumerate(chips):
            copy(1 + j, (*chip, c), me).wait_recv()
            passed[j].start()
        copy(0, sibling, me).wait_recv()
        for j, chip in enumerate(chips):
            copy(4 + j, (*chip, 1 - c), me).wait_recv()
        for cp in first + passed:
            cp.wait_send()
        mine.wait()

    return pl.pallas_call(
        body, name=name, out_shape=jax.ShapeDtypeStruct((8 * m_per, n), x_shard.dtype),
        in_specs=[pl.BlockSpec(memory_space=pltpu.VMEM)], out_specs=pl.BlockSpec(memory_space=pltpu.VMEM),
        scratch_shapes=[pltpu.SemaphoreType.DMA((7,)), pltpu.SemaphoreType.DMA((7,)), pltpu.SemaphoreType.DMA],
    )(x_shard)


_HBM = pl.BlockSpec(memory_space=pltpu.HBM)


def _mesh_place():
    x, y, c = lax.axis_index("x"), lax.axis_index("y"), lax.axis_index("c")
    return x, y, c, 2 * x + y, [(1 - x, y), (x, 1 - y), (1 - x, 1 - y)]


def _gather_plug(shards):
    n = len(shards)

    def half(ref, c, lead=None):
        r, cols = ref.shape[-2] // 2, ref.shape[-1] // 2
        if r % 16 == 0:
            rows = pl.ds(pl.multiple_of(c * r, 16), r)
            return ref.at[rows, :] if lead is None else ref.at[lead, rows, :]
        lanes = pl.ds(pl.multiple_of(c * cols, LANES), cols)
        return ref.at[:, lanes] if lead is None else ref.at[lead, :, lanes]

    def copies(ins, outs, send, recv):
        x, y, c, me, chips = _mesh_place()
        ici, fwd, fwd_in = [], [], []
        for i in range(n):
            for j, (px, py) in enumerate(chips):
                q = 2 * px + py
                ici.append((pltpu.make_async_remote_copy(
                    src_ref=half(ins[i], c), dst_ref=half(outs[i], c, me), send_sem=send.at[6 * i + j], recv_sem=recv.at[6 * i + j],
                    device_id=(px, py, c), device_id_type=MESH),
                    pltpu.make_async_remote_copy(
                    src_ref=half(ins[i], c), dst_ref=half(outs[i], c, q), send_sem=send.at[6 * i + j], recv_sem=recv.at[6 * i + j],
                    device_id=(px, py, c), device_id_type=MESH)))
                fwd.append(pltpu.make_async_remote_copy(
                    src_ref=half(outs[i], c, q), dst_ref=half(outs[i], c, q), send_sem=send.at[6 * i + 3 + j],
                    recv_sem=recv.at[6 * i + 3 + j], device_id=(x, y, 1 - c), device_id_type=MESH))
                fwd_in.append(pltpu.make_async_remote_copy(
                    src_ref=half(outs[i], 1 - c, q), dst_ref=half(outs[i], 1 - c, q), send_sem=send.at[6 * i + 3 + j],
                    recv_sem=recv.at[6 * i + 3 + j], device_id=(x, y, 1 - c), device_id_type=MESH))
        return ici, fwd, fwd_in, me

    def start(ins, outs, send, recv, loc):
        ici, _, _, me = copies(ins, outs, send, recv)
        for i in range(n):
            pltpu.make_async_copy(ins[i], outs[i].at[me], loc.at[i]).start()
        for out_cp, _ in ici:
            out_cp.start()

    def finish(ins, outs, send, recv, loc):
        ici, fwd, fwd_in, me = copies(ins, outs, send, recv)
        for (_, in_cp), f in zip(ici, fwd):
            in_cp.wait_recv()
            f.start()
        for f in fwd_in:
            f.wait_recv()
        for (out_cp, _), f in zip(ici, fwd):
            out_cp.wait_send()
            f.wait_send()
        for i in range(n):
            pltpu.make_async_copy(ins[i], outs[i].at[me], loc.at[i]).wait()

    return dict(ins=list(shards), out_shape=[jax.ShapeDtypeStruct((N_CHIPS,) + a.shape, a.dtype) for a in shards],
                scratch=[pltpu.SemaphoreType.DMA((6 * n,)), pltpu.SemaphoreType.DMA((6 * n,)), pltpu.SemaphoreType.DMA((n,))],
                start=start, finish=finish)


def _exchange_plug(pieces):
    n = len(pieces)

    def copies(ins, outs, send, recv):
        x, y, c, me, chips = _mesh_place()
        out_cps, in_cps = [], []
        for i in range(n):
            for j, (px, py) in enumerate(chips):
                q = 2 * px + py
                out_cps.append(pltpu.make_async_remote_copy(src_ref=ins[i].at[q], dst_ref=outs[i].at[me], send_sem=send.at[3 * i + j],
                                                            recv_sem=recv.at[3 * i + j], device_id=(px, py, c), device_id_type=MESH))
                in_cps.append(pltpu.make_async_remote_copy(src_ref=ins[i].at[me], dst_ref=outs[i].at[q], send_sem=send.at[3 * i + j],
                                                           recv_sem=recv.at[3 * i + j], device_id=(px, py, c), device_id_type=MESH))
        return out_cps, in_cps, me

    def start(ins, outs, send, recv, loc):
        out_cps, _, me = copies(ins, outs, send, recv)
        for i in range(n):
            pltpu.make_async_copy(ins[i].at[me], outs[i].at[me], loc.at[i]).start()
        for cp in out_cps:
            cp.start()

    def finish(ins, outs, send, recv, loc):
        out_cps, in_cps, me = copies(ins, outs, send, recv)
        for cp in in_cps:
            cp.wait_recv()
        for cp in out_cps:
            cp.wait_send()
        for i in range(n):
            pltpu.make_async_copy(ins[i].at[me], outs[i].at[me], loc.at[i]).wait()

    return dict(ins=list(pieces), out_shape=[jax.ShapeDtypeStruct(a.shape, a.dtype) for a in pieces],
                scratch=[pltpu.SemaphoreType.DMA((3 * n,)), pltpu.SemaphoreType.DMA((3 * n,)), pltpu.SemaphoreType.DMA((n,))],
                start=start, finish=finish)


def _comm_call(plug, name):
    n_in, n_out = len(plug["ins"]), len(plug["out_shape"])

    def body(*refs):
        ins, outs, sems = refs[:n_in], refs[n_in:n_in + n_out], refs[n_in + n_out:]
        plug["start"](ins, outs, *sems)
        plug["finish"](ins, outs, *sems)

    return pl.pallas_call(body, name=name, out_shape=plug["out_shape"], in_specs=[_HBM] * n_in, out_specs=[_HBM] * n_out,
                          scratch_shapes=plug["scratch"])(*plug["ins"])


def _plugged_call(body, plug, first_last, args, *, name, grid, in_specs, out_specs, out_shape, scratch_shapes=()):
    in_specs, out_specs, out_shape, scratch_shapes = list(in_specs), list(out_specs), list(out_shape), list(scratch_shapes)
    if plug is None:
        return pl.pallas_call(body, name=name, grid=grid, in_specs=in_specs, out_specs=out_specs, out_shape=out_shape,
                              scratch_shapes=scratch_shapes)(*args), []
    n_in, n_out, n_sc = len(in_specs), len(out_specs), len(scratch_shapes)
    p_in, p_out = len(plug["ins"]), len(plug["out_shape"])

    def full(*refs):
        ins, refs = refs[:n_in], refs[n_in:]
        pins, refs = refs[:p_in], refs[p_in:]
        outs, refs = refs[:n_out], refs[n_out:]
        pouts, refs = refs[:p_out], refs[p_out:]
        scr, psems = refs[:n_sc], refs[n_sc:]
        first, last = first_last()

        @pl.when(first)
        def _():
            plug["start"](pins, pouts, *psems)

        body(*ins, *outs, *scr)

        @pl.when(last)
        def _():
            plug["finish"](pins, pouts, *psems)

    res = pl.pallas_call(full, name=name, grid=grid, in_specs=in_specs + [_HBM] * p_in, out_specs=out_specs + [_HBM] * p_out,
                         out_shape=out_shape + plug["out_shape"], scratch_shapes=scratch_shapes + plug["scratch"])(*args, *plug["ins"])
    return res[:n_out], res[n_out:]


def _grid_ends(grid):
    def ends():
        first = last = None
        for ax, n in enumerate(grid):
            i = pl.program_id(ax)
            first = (i == 0) if first is None else first & (i == 0)
            last = (i == n - 1) if last is None else last & (i == n - 1)
        return first, last
    return ends


def _sibling_exchange(arrs, name):
    n = len(arrs)

    def body(*refs):
        ins, outs = refs[:n], refs[n:2 * n]
        send_sems, recv_sems = refs[2 * n:]
        sibling = (lax.axis_index("x"), lax.axis_index("y"), 1 - lax.axis_index("c"))
        cps = [pltpu.make_async_remote_copy(src_ref=ins[i], dst_ref=outs[i], send_sem=send_sems.at[i], recv_sem=recv_sems.at[i],
                                            device_id=sibling, device_id_type=MESH) for i in range(n)]
        for cp in cps:
            cp.start()
        for cp in cps:
            cp.wait()

    return pl.pallas_call(
        body, name=name, out_shape=[jax.ShapeDtypeStruct(a.shape, a.dtype) for a in arrs], in_specs=[_HBM] * n, out_specs=[_HBM] * n,
        scratch_shapes=[pltpu.SemaphoreType.DMA((n,)), pltpu.SemaphoreType.DMA((n,))],
    )(*arrs)


def _to_padded(wt):
    z = jnp.zeros((IN_PAD - IN_DIM, wt.shape[1]), wt.dtype)
    return jnp.concatenate([wt[768:2304], wt[0:512], wt[2304:2816], wt[2824:3848], wt[3848:4872],
                            wt[512:640], wt[640:768], wt[2816:2824], z], axis=0)


def _from_padded(gt):
    return jnp.concatenate([gt[C_Q:C_Q + AQ], gt[C_K:C_K + AKV], gt[C_V:C_V + AKV], gt[C_DQKV:C_DQKV + 3 * DN],
                            gt[C_DZ:C_DZ + DN], gt[C_BD:C_BD + 2 * DN_HEADS], gt[C_GA:C_GA + D_MODEL],
                            gt[C_GD:C_GD + D_MODEL]], axis=0)


def _lane_vec(a):
    return jnp.zeros((1, LANES), F32).at[0, DN_HEADS:2 * DN_HEADS].set(a)


_ROW_SHARDED = ("w_in", "w_out", "ffn_w_down")
_FFN = ("ffn_w_up", "ffn_w_down")
_LATE_MIXER = ("w_attn_branch", "w_dn_branch", "w_out")


def _pieces(k, a):
    if a.ndim == 3:
        return a
    if k in _ROW_SHARDED:
        return a.reshape(N_CHIPS, a.shape[0] // N_CHIPS, a.shape[1]).astype(BF16)
    return jnp.transpose(a.reshape(a.shape[0], N_CHIPS, a.shape[1] // N_CHIPS), (1, 0, 2)).astype(BF16)


def _assemble(k, a):
    if k in _ROW_SHARDED:
        return a.reshape(-1, a.shape[2])
    return jnp.transpose(a, (1, 0, 2)).reshape(a.shape[1], -1)


def _device_step(x2, tgt2, mod, p, bsz, shards=None):
    d = D_MODEL
    on_mesh = shards is not None
    p = dict(p)
    sh1, sc1, g1, sh2, sc2, g2 = [mod[:, i * d:(i + 1) * d].reshape(bsz, 1, d) for i in range(N_MOD)]
    alog_v, dt_v = _lane_vec(p["dn_a_log"]), _lane_vec(p["dn_dt_bias"])
    sinks = p["attn_sinks"].reshape(1, AQ_HEADS)
    u1 = _pre_fwd(x2, p["norm_mix_pre"], sc1, sh1, "pre1_fwd")
    if on_mesh:
        proj, got = _mm(u1, p["w_in"], "nt", F32, "mm_proj", _gather_plug(shards["late_mixer"]))
        p.update({k: _assemble(k, a) for k, a in zip(_LATE_MIXER, got)})
    else:
        proj = _mm(u1, p["w_in"], "nt", F32, "mm_proj")
    bias = _bias_build(p["rel_bias"])
    y_attn = _attn_fwd(proj, bias, sinks, bsz)
    qkvn = _dn_prep_fwd(proj, p["dn_conv_w"], bsz)
    bg = _bg_fwd(proj, alog_v, dt_v, bsz)
    nc = x2.shape[0] // bsz // DN_CHUNK
    g_rows = jnp.transpose(bg[:, DN_HEADS:2 * DN_HEADS].reshape(bsz, nc, DN_CHUNK, DN_HEADS), (0, 3, 1, 2))
    o, states, got = _dn_chunk_fwd(qkvn, bg, g_rows, bsz, _gather_plug(shards["ffn"]) if on_mesh else None)
    for k, a in zip(_FFN, got):
        p[k] = _assemble(k, a)
    y_dn = _dn_out_fwd(o, proj, p["dn_norm_w"])
    ya = _mm(y_attn, p["w_attn_branch"], "nn", F32, "mm_ya")
    yd = _mm(y_dn, p["w_dn_branch"], "nn", F32, "mm_yd")
    merged = _merge_fwd(proj, ya, yd)
    y1 = _mm(merged, p["w_out"], "nn", F32, "mm_y1")
    h1, u2 = _post_pre_fwd(x2, y1, p["norm_mix_post"], g1, p["norm_ffn_pre"], sc2, sh2, "post1_pre2_fwd")
    up = _mm(u2, p["ffn_w_up"], "nn", F32, "mm_up")
    act = _ffn_act_fwd(up, p["ffn_conv_w"], bsz)
    y2 = _mm(act, p["ffn_w_down"], "nn", F32, "mm_y2")
    dh2, dy2, g_ffn_post, dg2, sq = _post_loss_bwd(h1, y2, p["norm_ffn_post"], g2, tgt2, "post2_loss_bwd")
    g = {}
    g["norm_ffn_post"] = g_ffn_post
    dact = _mm(dy2, p["ffn_w_down"], "nt", F32, "mm_dact")
    g["ffn_w_down"] = _mm(act, dy2, "tn", BF16, "mm_dwdown")
    dupg, dupv, dcwg, dcwv = _ffn_act_bwd(up, p["ffn_conv_w"], dact, bsz)
    g["ffn_conv_w"] = jnp.concatenate([dcwg, dcwv], axis=1)
    dup = jnp.concatenate([dupg, dupv], axis=1)
    g["ffn_w_up"] = _mm(u2, dup, "tn", BF16, "mm_dwup", split=N_CHIPS)
    du2 = _mm(dup, p["ffn_w_up"], "nt", F32, "mm_du2")
    dh1, dy1, g["norm_ffn_pre"], dsc2, dsh2, g["norm_mix_post"], dg1 = _pre_post_bwd(
        h1, p["norm_ffn_pre"], sc2, sh2, du2, dh2, y1, p["norm_mix_post"], g1, "pre2_post1_bwd")
    dmerged = _mm(dy1, p["w_out"], "nt", F32, "mm_dmerged")
    g["w_out"] = _mm(merged, dy1, "tn", BF16, "mm_dwout")
    dga, dgd, dya, dyd = _merge_bwd(proj, ya, yd, dmerged)
    dy_attn = _mm(dya, p["w_attn_branch"], "nt", BF16, "mm_dyattn")
    g["w_attn_branch"] = _mm(y_attn, dya, "tn", BF16, "mm_dwab", split=N_CHIPS)
    dy_dn = _mm(dyd, p["w_dn_branch"], "nt", F32, "mm_dydn")
    g["w_dn_branch"] = _mm(y_dn, dyd, "tn", BF16, "mm_dwdb", split=N_CHIPS)
    do, dz, g["dn_norm_w"] = _dn_out_bwd(o, proj, p["dn_norm_w"], dy_dn)
    def plug_for(names):
        return _exchange_plug([_pieces(k, g[k]) for k in names]) if on_mesh else None

    early = ("w_out", "w_attn_branch", "w_dn_branch")
    dqkvn, dbg4, got_ffn = _dn_chunk_bwd(qkvn, bg, g_rows, states, do, bsz, plug_for(_FFN))
    d_dqkv, g["dn_conv_w"] = _dn_prep_bwd(proj, p["dn_conv_w"], dqkvn, bsz)
    dbd, g["dn_a_log"], g["dn_dt_bias"] = _bg_bwd(proj, alog_v, dt_v, dbg4, bsz)
    dq, dk, dv, dbias, g["attn_sinks"], got_early = _attn_bwd(proj, bias, sinks, dy_attn, bsz, plug_for(early))
    g["rel_bias"] = _bias_grad(dbias)
    dproj = jnp.concatenate([d_dqkv, dq, dz, dga, dgd, dk, dv, dbd], axis=1)
    g["w_in"] = _from_padded(_mm(dproj, u1, "tn", BF16, "mm_dwin"))
    if on_mesh:
        du1, got_in = _mm(dproj, p["w_in"], "nn", F32, "mm_du1", plug_for(("w_in",)))
        g.update(zip(_FFN + early + ("w_in",), list(got_ffn) + list(got_early) + list(got_in)))
    else:
        du1 = _mm(dproj, p["w_in"], "nn", F32, "mm_du1")
    dx, g["norm_mix_pre"], dsc1, dsh1 = _pre_bwd(x2, p["norm_mix_pre"], sc1, sh1, du1, dh1, "pre1_bwd")
    dmod = jnp.concatenate([dsh1, dsc1, dg1, dsh2, dsc2, dg2], axis=-1).reshape(bsz, N_MOD * d)
    return sq, dx, dmod, g


_SMALL = (("norm_mix_pre", D_MODEL), ("norm_mix_post", D_MODEL), ("norm_ffn_pre", D_MODEL), ("norm_ffn_post", D_MODEL),
          ("dn_norm_w", DN_HD), ("dn_a_log", LANES), ("dn_dt_bias", LANES), ("attn_sinks", AQ_HEADS * LANES),
          ("rel_bias", AQ_HEADS * LANES), ("dn_conv_w", DN_CONV * 3 * DN), ("ffn_conv_w", FFN_CONV * 2 * D_FF))


def _pack_rows(parts, rows):
    flat = jnp.concatenate([a.reshape(-1) for a in parts])
    return jnp.concatenate([flat, jnp.zeros((rows * LANES - flat.shape[0],), F32)]).reshape(rows, LANES)


def _pad128(a):
    flat = a.reshape(-1)
    n = -(-flat.shape[0] // LANES) * LANES
    return jnp.concatenate([flat, jnp.zeros((n - flat.shape[0],), F32)]) if n != flat.shape[0] else flat


_W_NAMES = ("ada_w", "ada_b", "norm_mix_pre", "norm_mix_post", "norm_ffn_pre", "norm_ffn_post", "w_in", "dn_conv_w", "dn_a_log",
            "dn_dt_bias", "dn_norm_w", "attn_sinks", "rel_bias", "w_attn_branch", "w_dn_branch", "w_out", "ffn_w_up", "ffn_conv_w",
            "ffn_w_down")
_BIG = ("w_in", "w_attn_branch", "w_dn_branch", "w_out", "ffn_w_up", "ffn_w_down")


def kernel(x, c, *rest):
    nw = len(_W_NAMES)
    w = dict(zip(_W_NAMES, rest[:nw]))
    loss_target = rest[nw]
    m = dict(zip(_W_NAMES, rest[nw + 1:2 * nw + 1]))
    v = dict(zip(_W_NAMES, rest[2 * nw + 1:3 * nw + 1]))
    ix, iy, ic = lax.axis_index("x"), lax.axis_index("y"), lax.axis_index("c")
    chip, dev = 2 * ix + iy, 4 * ix + 2 * iy + ic
    bsz, s, d = x.shape
    t = bsz * s
    n_dev = 8

    front_rows = 64
    front = _pack_rows([c, w["dn_conv_w"], w["ffn_conv_w"]], front_rows)
    front_all = _allgather8(front, "ag_front").reshape(n_dev, front_rows * LANES)
    n_c, n_dc, n_fc = bsz * d, DN_CONV * 3 * DN // N_CHIPS, FFN_CONV * 2 * D_FF // N_CHIPS
    c_all = front_all[:, :n_c].reshape(n_dev * bsz, d)
    per_chip = front_all[0::2]
    dn_conv_full = jnp.transpose(per_chip[:, n_c:n_c + n_dc].reshape(N_CHIPS, DN_CONV, -1), (1, 0, 2)).reshape(DN_CONV, 3 * DN)
    ffn_conv_full = jnp.transpose(per_chip[:, n_c + n_dc:n_c + n_dc + n_fc].reshape(N_CHIPS, FFN_CONV, -1), (1, 0, 2)).reshape(FFN_CONV, 2 * D_FF)

    mod_cols = N_MOD * d // N_CHIPS
    ada_b_loc = lax.dynamic_slice(w["ada_b"], (0, chip * mod_cols), (1, mod_cols))
    mod_part = _mod_fwd(c_all, w["ada_w"][0], ada_b_loc)
    mod_all = _allgather8(mod_part, "ag_mod").reshape(n_dev, n_dev * bsz, mod_cols)[0::2]
    mod = jnp.transpose(lax.dynamic_slice(mod_all, (0, dev * bsz, 0), (N_CHIPS, bsz, mod_cols)), (1, 0, 2)).reshape(bsz, N_MOD * d)

    (w_in_all,) = _comm_call(_gather_plug([jnp.swapaxes(w["w_in"][0], 0, 1).astype(BF16)]), "gather_w_in")
    p = {"w_in": _to_padded(_assemble("w_in", w_in_all))}
    shards = {"late_mixer": [w[k][0].astype(BF16) for k in _LATE_MIXER], "ffn": [w[k][0].astype(BF16) for k in _FFN]}
    for k in ("norm_mix_pre", "norm_mix_post", "norm_ffn_pre", "norm_ffn_post", "dn_norm_w", "attn_sinks"):
        p[k] = w[k]
    p["dn_a_log"], p["dn_dt_bias"], p["rel_bias"] = w["dn_a_log"][0], w["dn_dt_bias"][0], w["rel_bias"]
    p["dn_conv_w"], p["ffn_conv_w"] = dn_conv_full, ffn_conv_full

    sq, dx, dmod, g = _device_step(x.reshape(t, d), loss_target.reshape(t, d), mod, p, bsz, shards)
    loss = lax.psum(0.5 * jnp.sum(sq), ("x", "y", "c"))

    g["dn_a_log"], g["dn_dt_bias"] = g["dn_a_log"].reshape(-1), g["dn_dt_bias"].reshape(-1)
    small_rows = 328
    small = _pack_rows([dmod] + [g[k] for k, _ in _SMALL], small_rows)
    small_all = _allgather8(small, "ag_small").reshape(n_dev, small_rows, LANES)
    n_dm = bsz * N_MOD * d
    dmod_all = small_all.reshape(n_dev, -1)[:, :n_dm].reshape(n_dev * bsz, N_MOD * d)
    tot = _sum_lead(small_all, "sum_small").reshape(-1)
    gs, off = {}, n_dm
    for k, n in _SMALL:
        gs[k] = tot[off:off + n]
        off += n
    grad = {}
    grad["ada_w"], grad["ada_b"] = _ada_grad(c_all, lax.dynamic_slice(dmod_all, (0, chip * mod_cols), (n_dev * bsz, mod_cols)), dmod_all)
    for k in ("norm_mix_pre", "norm_mix_post", "norm_ffn_pre", "norm_ffn_post", "dn_norm_w"):
        grad[k] = gs[k]
    grad["dn_a_log"] = gs["dn_a_log"][DN_HEADS:2 * DN_HEADS]
    grad["dn_dt_bias"] = gs["dn_dt_bias"][DN_HEADS:2 * DN_HEADS]
    grad["attn_sinks"] = gs["attn_sinks"].reshape(AQ_HEADS, LANES)[:, 0]
    grad["rel_bias"] = gs["rel_bias"].reshape(AQ_HEADS, LANES)[:, :REL_BUCKETS].T
    grad["dn_conv_w"] = lax.dynamic_slice(gs["dn_conv_w"].reshape(DN_CONV, 3 * DN), (0, chip * (3 * DN // N_CHIPS)), (DN_CONV, 3 * DN // N_CHIPS))
    grad["ffn_conv_w"] = lax.dynamic_slice(gs["ffn_conv_w"].reshape(FFN_CONV, 2 * D_FF), (0, chip * (2 * D_FF // N_CHIPS)), (FFN_CONV, 2 * D_FF // N_CHIPS))

    mine = [_sum_lead(g[k], "sum_" + k) for k in _BIG]
    theirs = _sibling_exchange(mine, "exchange_cores")

    out = {}
    for k, a, b in zip(_BIG, mine, theirs):
        if k == "w_in":
            tr = lambda z: jnp.swapaxes(z, 0, 1)
            out[k] = [tr(r) for r in _adamw(tr(w[k][0]), tr(m[k][0]), tr(v[k][0]), [a, b], "adamw_" + k)]
        else:
            out[k] = _adamw(w[k][0], m[k][0], v[k][0], [a, b], "adamw_" + k)
    out["ada_w"] = _adamw(w["ada_w"][0], m["ada_w"][0], v["ada_w"][0], [grad["ada_w"]], "adamw_ada_w")
    small_names = [k for k in _W_NAMES if k not in _BIG and k != "ada_w"]
    res_small = _adamw_small([w[k] for k in small_names], [m[k] for k in small_names], [v[k] for k in small_names],
                             [grad[k].reshape(w[k].shape) for k in small_names], "adamw_small")
    out.update(zip(small_names, res_small))
    for k in _BIG + ("ada_w",):
        out[k] = [r.reshape(w[k].shape) for r in out[k]]
    grads, deltas, new_m, new_v = ([out[k][i] for k in _W_NAMES] for i in range(4))
    return (loss, dx.reshape(bsz, s, d), *grads, *deltas, *new_m, *new_v)
```

```python
import functools
import math

import numpy as np
import jax
import jax.numpy as jnp
from jax import lax
from jax.experimental import pallas as pl
from jax.experimental.pallas import tpu as pltpu

F32 = jnp.float32
BF16 = jnp.bfloat16
MESH = pl.DeviceIdType.MESH

D_MODEL = 1024
N_MOD = 6
AQ_HEADS, AKV_HEADS, A_HD, WINDOW = 8, 2, 64, 128
REL_BUCKETS, REL_MAX_DIST = 32, 128
DN_HEADS, DN_HD, DN_CONV, DN_CHUNK = 4, 128, 4, 64
D_FF, FFN_CONV = 2816, 3
RMS_EPS, L2_EPS, NEG_INF = 1e-6, 1e-6, -1e30
AQ, AKV, DN = AQ_HEADS * A_HD, AKV_HEADS * A_HD, DN_HEADS * DN_HD
IN_DIM = AQ + 2 * AKV + 3 * DN + DN + 2 * DN_HEADS + 2 * D_MODEL
C_DQKV, C_Q, C_DZ, C_GA, C_GD, C_K, C_V, C_BD = 0, 1536, 2048, 2560, 3584, 4608, 4736, 4864
IN_PAD = 4992
LANES = 128
N_CHIPS = 4

ADAM_LR, ADAM_B1, ADAM_B2, ADAM_EPS, ADAM_WD, ADAM_STEP = 0.001, 0.9, 0.999, 1e-08, 0.01, 10


def _pick(n, cap):
    best = None
    for t in range(LANES, cap + 1, LANES):
        if n % t == 0:
            best = t
    return best if best is not None else n


def _vspec(shape, index_map):
    return pl.BlockSpec(shape, index_map)


MM_VMEM_BUDGET = 40 * 2 ** 20
GRID_STEP_S = 0.35e-6
HBM_BYTES_PER_S = 3.0e12
MXU_FLOPS_PER_S = 9.0e14
MXU_DIM = 256


def _mm_tiles(m, n, k, mode, in_bytes, out_bytes, split=1):
    best = None
    for tm in [t for t in range(LANES, m + 1, LANES) if m % t == 0]:
        for tn in [t for t in range(LANES, n // split + 1, LANES) if (n // split) % t == 0]:
            a, b, o = k * tm * in_bytes, k * tn * in_bytes, tm * tn * out_bytes
            if 2 * (a + b + o) + (a if mode == "tn" else 0) > MM_VMEM_BUDGET:
                continue
            hbm_s = (m * k * in_bytes + (m // tm) * n * k * in_bytes + m * n * out_bytes) / HBM_BYTES_PER_S
            mxu_s = 2 * m * n * k / (MXU_FLOPS_PER_S * min(1.0, tm / MXU_DIM) * min(1.0, tn / MXU_DIM))
            cost = (m // tm) * (n // tn) * GRID_STEP_S + max(hbm_s, mxu_s)
            if best is None or cost < best[0]:
                best = (cost, tm, tn)
    return best[1], best[2]


def _mm(a, b, mode, out_dtype, name, plug=None, split=1):
    if mode == "nn":
        (m, k), n = a.shape, b.shape[1]
        dims = (((1,), (0,)), ((), ()))
    elif mode == "nt":
        (m, k), n = a.shape, b.shape[0]
        dims = (((1,), (1,)), ((), ()))
    else:
        (k, m), n = a.shape, b.shape[1]
        dims = (((0,), (0,)), ((), ()))
    tm, tn = _mm_tiles(m, n, k, mode, a.dtype.itemsize, jnp.dtype(out_dtype).itemsize, split)
    if mode == "tn":
        a_spec = _vspec((k, tm), lambda i, j: (0, i))
    else:
        a_spec = _vspec((tm, k), lambda i, j: (i, 0))
    if mode == "nt":
        b_spec = _vspec((tn, k), lambda i, j: (j, 0))
    else:
        b_spec = _vspec((k, tn), lambda i, j: (0, j))

    def body(a_ref, b_ref, o_ref):
        o_ref[...] = lax.dot_general(a_ref[...].astype(BF16), b_ref[...].astype(BF16), dims,
                                     preferred_element_type=F32).astype(out_dtype).reshape(o_ref.shape)

    grid = (m // tm, n // tn)
    if split == 1:
        out_spec, out_shape = _vspec((tm, tn), lambda i, j: (i, j)), (m, n)
    else:
        per = n // split // tn
        out_spec, out_shape = _vspec((1, tm, tn), lambda i, j: (j // per, i, j % per)), (split, m, n // split)
    (out,), extra = _plugged_call(body, plug, _grid_ends(grid), (a, b), name=name, grid=grid, in_specs=[a_spec, b_spec],
                                  out_specs=[out_spec], out_shape=[jax.ShapeDtypeStruct(out_shape, out_dtype)])
    return out if plug is None else (out, extra)


def _rms(x, w):
    return (x * lax.rsqrt(jnp.mean(x * x, axis=-1, keepdims=True) + RMS_EPS)) * w


def _pre_f(x, w, sc, sh):
    return _rms(x, w) * (1.0 + sc) + sh


def _post_f(y, w, g):
    return g * _rms(y, w)


def _tok_grid(t, bsz, ts):
    nt = t // bsz // ts
    return nt, (bsz, nt)


def _pre_fwd(x, w, sc, sh, name, ts=512):
    t, d = x.shape
    bsz = sc.shape[0]
    nt, grid = _tok_grid(t, bsz, ts)
    row = _vspec((ts, d), lambda b, i: (b * nt + i, 0))
    vec = _vspec((1, d), lambda b, i: (0, 0))
    bvec = _vspec((1, 1, d), lambda b, i: (b, 0, 0))

    def body(x_ref, w_ref, sc_ref, sh_ref, u_ref):
        u_ref[...] = _pre_f(x_ref[...], w_ref[...], sc_ref[0], sh_ref[0]).astype(BF16)

    return pl.pallas_call(body, name=name, grid=grid, in_specs=[row, vec, bvec, bvec], out_specs=row,
                          out_shape=jax.ShapeDtypeStruct((t, d), BF16))(x, w, sc, sh)


def _pre_bwd(x, w, sc, sh, du, dres, name, ts=512):
    t, d = x.shape
    bsz = sc.shape[0]
    nt, grid = _tok_grid(t, bsz, ts)
    row = _vspec((ts, d), lambda b, i: (b * nt + i, 0))
    vec = _vspec((1, d), lambda b, i: (0, 0))
    bvec = _vspec((1, 1, d), lambda b, i: (b, 0, 0))

    def body(x_ref, w_ref, sc_ref, sh_ref, du_ref, dres_ref, dx_ref, dw_ref, dsc_ref, dsh_ref):
        b, i = pl.program_id(0), pl.program_id(1)
        _, vjp = jax.vjp(_pre_f, x_ref[...], w_ref[...], sc_ref[0], sh_ref[0])
        dx, dw, dsc, dsh = vjp(du_ref[...])
        dx_ref[...] = dres_ref[...] + dx

        @pl.when((b == 0) & (i == 0))
        def _():
            dw_ref[...] = jnp.zeros_like(dw_ref)

        @pl.when(i == 0)
        def _():
            dsc_ref[...] = jnp.zeros_like(dsc_ref)
            dsh_ref[...] = jnp.zeros_like(dsh_ref)

        dw_ref[...] += dw
        dsc_ref[0] += dsc
        dsh_ref[0] += dsh

    return pl.pallas_call(
        body, name=name, grid=grid, in_specs=[row, vec, bvec, bvec, row, row], out_specs=[row, vec, bvec, bvec],
        out_shape=[jax.ShapeDtypeStruct((t, d), F32), jax.ShapeDtypeStruct((1, d), F32),
                   jax.ShapeDtypeStruct((bsz, 1, d), F32), jax.ShapeDtypeStruct((bsz, 1, d), F32)],
    )(x, w, sc, sh, du, dres)


def _accumulate(ref, val, first):
    @pl.when(first)
    def _():
        ref[...] = jnp.zeros_like(ref)

    ref[...] += val.reshape(ref.shape)


def _post_pre_fwd(res, y, w_post, g, w_pre, sc, sh, name, ts=512):
    t, d = y.shape
    bsz = g.shape[0]
    nt, grid = _tok_grid(t, bsz, ts)
    row = _vspec((ts, d), lambda b, i: (b * nt + i, 0))
    vec = _vspec((1, d), lambda b, i: (0, 0))
    bvec = _vspec((1, 1, d), lambda b, i: (b, 0, 0))

    def body(res_ref, y_ref, wp_ref, g_ref, w_ref, sc_ref, sh_ref, h_ref, u_ref):
        h = res_ref[...] + _post_f(y_ref[...], wp_ref[...], g_ref[0])
        h_ref[...] = h
        u_ref[...] = _pre_f(h, w_ref[...], sc_ref[0], sh_ref[0]).astype(BF16)

    return pl.pallas_call(body, name=name, grid=grid, in_specs=[row, row, vec, bvec, vec, bvec, bvec], out_specs=[row, row],
                          out_shape=[jax.ShapeDtypeStruct((t, d), F32), jax.ShapeDtypeStruct((t, d), BF16)],
                          )(res, y, w_post, g, w_pre, sc, sh)


def _post_loss_bwd(res, y, w, g, tgt, name, ts=512):
    t, d = y.shape
    bsz = g.shape[0]
    nt, grid = _tok_grid(t, bsz, ts)
    row = _vspec((ts, d), lambda b, i: (b * nt + i, 0))
    vec = _vspec((1, d), lambda b, i: (0, 0))
    bvec = _vspec((1, 1, d), lambda b, i: (b, 0, 0))

    def body(res_ref, y_ref, w_ref, g_ref, tgt_ref, dh_ref, dy_ref, dw_ref, dg_ref, sq_ref):
        b, i = pl.program_id(0), pl.program_id(1)
        part, vjp = jax.vjp(_post_f, y_ref[...], w_ref[...], g_ref[0])
        e = res_ref[...] + part - tgt_ref[...]
        dh = e * (1.0 / d)
        dh_ref[...] = dh
        dy, dw, dg = vjp(dh)
        dy_ref[...] = dy.astype(BF16)
        _accumulate(dw_ref, dw, (b == 0) & (i == 0))
        _accumulate(dg_ref, dg, i == 0)
        _accumulate(sq_ref, jnp.sum(e * e, axis=0, keepdims=True) * (1.0 / d), (b == 0) & (i == 0))

    return pl.pallas_call(
        body, name=name, grid=grid, in_specs=[row, row, vec, bvec, row], out_specs=[row, row, vec, bvec, vec],
        out_shape=[jax.ShapeDtypeStruct((t, d), F32), jax.ShapeDtypeStruct((t, d), BF16), jax.ShapeDtypeStruct((1, d), F32),
                   jax.ShapeDtypeStruct((bsz, 1, d), F32), jax.ShapeDtypeStruct((1, d), F32)],
    )(res, y, w, g, tgt)


def _pre_post_bwd(x, w, sc, sh, du, dres, y, w_post, g, name, ts=512):
    t, d = x.shape
    bsz = sc.shape[0]
    nt, grid = _tok_grid(t, bsz, ts)
    row = _vspec((ts, d), lambda b, i: (b * nt + i, 0))
    vec = _vspec((1, d), lambda b, i: (0, 0))
    bvec = _vspec((1, 1, d), lambda b, i: (b, 0, 0))

    def body(x_ref, w_ref, sc_ref, sh_ref, du_ref, dres_ref, y_ref, wp_ref, g_ref,
             dx_ref, dy_ref, dw_ref, dsc_ref, dsh_ref, dwp_ref, dg_ref):
        b, i = pl.program_id(0), pl.program_id(1)
        _, vjp = jax.vjp(_pre_f, x_ref[...], w_ref[...], sc_ref[0], sh_ref[0])
        dx, dw, dsc, dsh = vjp(du_ref[...])
        dx = dres_ref[...] + dx
        dx_ref[...] = dx
        _, vjp_post = jax.vjp(_post_f, y_ref[...], wp_ref[...], g_ref[0])
        dy, dwp, dg = vjp_post(dx)
        dy_ref[...] = dy.astype(BF16)
        first = (b == 0) & (i == 0)
        _accumulate(dw_ref, dw, first)
        _accumulate(dwp_ref, dwp, first)
        _accumulate(dsc_ref, dsc, i == 0)
        _accumulate(dsh_ref, dsh, i == 0)
        _accumulate(dg_ref, dg, i == 0)

    v1, vb = jax.ShapeDtypeStruct((1, d), F32), jax.ShapeDtypeStruct((bsz, 1, d), F32)
    return pl.pallas_call(
        body, name=name, grid=grid, in_specs=[row, vec, bvec, bvec, row, row, row, vec, bvec],
        out_specs=[row, row, vec, bvec, bvec, vec, bvec],
        out_shape=[jax.ShapeDtypeStruct((t, d), F32), jax.ShapeDtypeStruct((t, d), BF16), v1, vb, vb, v1, vb],
    )(x, w, sc, sh, du, dres, y, w_post, g)


def _merge_f(ga, gd, ya, yd):
    return jax.nn.sigmoid(ga) * ya + jax.nn.sigmoid(gd) * yd


_MW = 512


def _merge_fwd(proj, ya, yd, ts=512):
    t, d = ya.shape
    blk = _vspec((ts, _MW), lambda i, j: (i, j))
    ga = _vspec((ts, _MW), lambda i, j: (i, C_GA // _MW + j))
    gd = _vspec((ts, _MW), lambda i, j: (i, C_GD // _MW + j))

    def body(ga_ref, gd_ref, ya_ref, yd_ref, o_ref):
        o_ref[...] = _merge_f(ga_ref[...], gd_ref[...], ya_ref[...], yd_ref[...]).astype(BF16)

    return pl.pallas_call(body, name="merge_fwd", grid=(t // ts, d // _MW), in_specs=[ga, gd, blk, blk], out_specs=blk,
                          out_shape=jax.ShapeDtypeStruct((t, d), BF16))(proj, proj, ya, yd)


def _merge_bwd(proj, ya, yd, dm, ts=512):
    t, d = ya.shape
    blk = _vspec((ts, _MW), lambda i, j: (i, j))
    ga = _vspec((ts, _MW), lambda i, j: (i, C_GA // _MW + j))
    gd = _vspec((ts, _MW), lambda i, j: (i, C_GD // _MW + j))

    def body(ga_ref, gd_ref, ya_ref, yd_ref, dm_ref, dga_ref, dgd_ref, dya_ref, dyd_ref):
        _, vjp = jax.vjp(_merge_f, ga_ref[...], gd_ref[...], ya_ref[...], yd_ref[...])
        dga, dgd, dya, dyd = vjp(dm_ref[...])
        dga_ref[...] = dga.astype(BF16)
        dgd_ref[...] = dgd.astype(BF16)
        dya_ref[...] = dya.astype(BF16)
        dyd_ref[...] = dyd.astype(BF16)

    o = jax.ShapeDtypeStruct((t, d), BF16)
    return pl.pallas_call(body, name="merge_bwd", grid=(t // ts, d // _MW), in_specs=[ga, gd, blk, blk, blk],
                          out_specs=[blk] * 4, out_shape=[o] * 4)(proj, proj, ya, yd, dm)


def _shift_down(x, s):
    if s == 0:
        return x
    r = lax.broadcasted_iota(jnp.int32, x.shape, 0)
    return jnp.where(r >= s, pltpu.roll(x, s, 0), 0.0)


def _shift_up(x, s):
    if s == 0:
        return x
    n = x.shape[0]
    r = lax.broadcasted_iota(jnp.int32, x.shape, 0)
    return jnp.where(r < n - s, pltpu.roll(x, n - s, 0), 0.0)


def _conv_fwd(x, w, k):
    out = None
    for j in range(k):
        term = w[j:j + 1, :] * _shift_down(x, k - 1 - j)
        out = term if out is None else out + term
    return out


def _conv_bwd(x, w, dc, k):
    dx = None
    dws = []
    for j in range(k):
        up = _shift_up(dc, k - 1 - j)
        term = w[j:j + 1, :] * up
        dx = term if dx is None else dx + term
        dws.append(jnp.sum(up * x, axis=0, keepdims=True))
    return dx, jnp.concatenate(dws, axis=0)


def _geglu_f(gate, val):
    return jax.nn.gelu(gate, approximate=True) * val


_FW = 256


def _ffn_act_fwd(up, conv_w, bsz):
    t = up.shape[0]
    s = t // bsz
    nj = D_FF // _FW
    xg = _vspec((s, _FW), lambda b, j: (b, j))
    xv = _vspec((s, _FW), lambda b, j: (b, nj + j))
    wg = _vspec((FFN_CONV, _FW), lambda b, j: (0, j))
    wv = _vspec((FFN_CONV, _FW), lambda b, j: (0, nj + j))

    def body(xg_ref, xv_ref, wg_ref, wv_ref, o_ref):
        gate = _conv_fwd(xg_ref[...], wg_ref[...], FFN_CONV)
        val = _conv_fwd(xv_ref[...], wv_ref[...], FFN_CONV)
        o_ref[...] = _geglu_f(gate, val).astype(BF16)

    return pl.pallas_call(body, name="ffn_act_fwd", grid=(bsz, nj), in_specs=[xg, xv, wg, wv],
                          out_specs=_vspec((s, _FW), lambda b, j: (b, j)),
                          out_shape=jax.ShapeDtypeStruct((t, D_FF), BF16))(up, up, conv_w, conv_w)


def _ffn_act_bwd(up, conv_w, dact, bsz, plug=None):
    t = up.shape[0]
    s = t // bsz
    nj = D_FF // _FW
    xg = _vspec((s, _FW), lambda j, b: (b, j))
    xv = _vspec((s, _FW), lambda j, b: (b, nj + j))
    wg = _vspec((FFN_CONV, _FW), lambda j, b: (0, j))
    wv = _vspec((FFN_CONV, _FW), lambda j, b: (0, nj + j))
    da = _vspec((s, _FW), lambda j, b: (b, j))
    dwo = _vspec((FFN_CONV, _FW), lambda j, b: (0, j))

    def body(xg_ref, xv_ref, wg_ref, wv_ref, da_ref, dxg_ref, dxv_ref, dwg_ref, dwv_ref):
        b = pl.program_id(1)
        xg_, xv_, wg_, wv_ = xg_ref[...], xv_ref[...], wg_ref[...], wv_ref[...]
        gate = _conv_fwd(xg_, wg_, FFN_CONV)
        val = _conv_fwd(xv_, wv_, FFN_CONV)
        _, vjp = jax.vjp(_geglu_f, gate, val)
        dgate, dval = vjp(da_ref[...])
        dxg, dwg = _conv_bwd(xg_, wg_, dgate, FFN_CONV)
        dxv, dwv = _conv_bwd(xv_, wv_, dval, FFN_CONV)
        dxg_ref[...] = dxg.astype(BF16)
        dxv_ref[...] = dxv.astype(BF16)

        @pl.when(b == 0)
        def _():
            dwg_ref[...] = jnp.zeros_like(dwg_ref)
            dwv_ref[...] = jnp.zeros_like(dwv_ref)

        dwg_ref[...] += dwg
        dwv_ref[...] += dwv

    outs, extra = _plugged_call(
        body, plug, _grid_ends((nj, bsz)), (up, up, conv_w, conv_w, dact), name="ffn_act_bwd", grid=(nj, bsz),
        in_specs=[xg, xv, wg, wv, da], out_specs=[da, da, dwo, dwo],
        out_shape=[jax.ShapeDtypeStruct((t, D_FF), BF16), jax.ShapeDtypeStruct((t, D_FF), BF16),
                   jax.ShapeDtypeStruct((FFN_CONV, D_FF), F32), jax.ShapeDtypeStruct((FFN_CONV, D_FF), F32)])
    return (*outs, extra)


def _bucket_table():
    qi = np.arange(WINDOW)[:, None]
    kj = np.arange(2 * WINDOW)[None, :]
    dist = WINDOW + qi - kj
    dc = np.maximum(dist, 0)
    max_exact = REL_BUCKETS // 2
    scaled = np.log(np.maximum(dc, 1).astype(np.float32) / np.float32(max_exact)) / np.float32(math.log(REL_MAX_DIST / max_exact))
    large = max_exact + (scaled.astype(np.float32) * np.float32(REL_BUCKETS - max_exact)).astype(np.int32)
    large = np.minimum(large, REL_BUCKETS - 1)
    bucket = np.where(dc < max_exact, dc, large).astype(np.int32)
    in_band = ((dist >= 0) & (dist < WINDOW)).astype(np.int32)
    return bucket, in_band


def _bias_build(rel_bias):
    bucket, _ = _bucket_table()

    def body(rb_ref, idx_ref, o_ref):
        h = pl.program_id(0)
        idx = idx_ref[...]
        acc = jnp.zeros(idx.shape, F32)
        for r in range(REL_BUCKETS):
            acc = jnp.where(idx == r, rb_ref[r, h], acc)
        o_ref[0] = acc

    return pl.pallas_call(
        body, name="bias_build", grid=(AQ_HEADS,),
        in_specs=[pl.BlockSpec(memory_space=pltpu.SMEM), _vspec((WINDOW, 2 * WINDOW), lambda h: (0, 0))],
        out_specs=_vspec((1, WINDOW, 2 * WINDOW), lambda h: (h, 0, 0)),
        out_shape=jax.ShapeDtypeStruct((AQ_HEADS, WINDOW, 2 * WINDOW), F32),
    )(rel_bias, jnp.asarray(bucket))


def _bias_grad(dbias):
    bucket, _ = _bucket_table()

    def body(db_ref, idx_ref, o_ref):
        idx = idx_ref[...]
        db = db_ref[0]
        lane = lax.broadcasted_iota(jnp.int32, (1, LANES), 1)
        acc = jnp.zeros((1, LANES), F32)
        for r in range(REL_BUCKETS):
            s = jnp.sum(jnp.sum(jnp.where(idx == r, db, 0.0), axis=1, keepdims=True), axis=0, keepdims=True)
            acc = jnp.where(lane == r, s, acc)
        o_ref[0] = acc

    return pl.pallas_call(
        body, name="bias_grad", grid=(AQ_HEADS,),
        in_specs=[_vspec((1, WINDOW, 2 * WINDOW), lambda h: (h, 0, 0)), _vspec((WINDOW, 2 * WINDOW), lambda h: (0, 0))],
        out_specs=_vspec((1, 1, LANES), lambda h: (h, 0, 0)),
        out_shape=jax.ShapeDtypeStruct((AQ_HEADS, 1, LANES), F32),
    )(dbias, jnp.asarray(bucket))


def _attn_mask(n):
    qi = lax.broadcasted_iota(jnp.int32, (WINDOW, 2 * WINDOW), 0)
    kj = lax.broadcasted_iota(jnp.int32, (WINDOW, 2 * WINDOW), 1)
    dist = WINDOW + qi - kj
    band = (dist >= 0) & (dist < WINDOW)
    return band & ((kj >= WINDOW) | (n > 0))


def _attn_probs(qk, bias, sink, mask):
    s = jnp.where(mask, qk * (A_HD ** -0.5) + bias, NEG_INF)
    m = jnp.maximum(jnp.max(s, axis=-1, keepdims=True), sink)
    p = jnp.exp(s - m)
    es = jnp.exp(sink - m)
    inv = 1.0 / (jnp.sum(p, axis=-1, keepdims=True) + es)
    return p * inv, es * inv


def _attn_fwd(proj, bias, sinks, bsz):
    t = proj.shape[0]
    s = t // bsz
    nb = s // WINDOW
    grp = AQ_HEADS // AKV_HEADS

    def body(q_ref, k_ref, v_ref, bias_ref, sink_ref, y_ref, kp_ref, vp_ref):
        kp_ref[0:WINDOW, :] = jnp.zeros((WINDOW, LANES), BF16)
        vp_ref[0:WINDOW, :] = jnp.zeros((WINDOW, LANES), BF16)
        kp_ref[WINDOW:, :] = k_ref[...].astype(BF16)
        vp_ref[WINDOW:, :] = v_ref[...].astype(BF16)

        def blk(n, carry):
            r0 = pl.multiple_of(n * WINDOW, WINDOW)
            mask = _attn_mask(n)
            kband = kp_ref[pl.ds(r0, 2 * WINDOW), :]
            vband = vp_ref[pl.ds(r0, 2 * WINDOW), :]
            qb = q_ref[pl.ds(r0, WINDOW), :].astype(BF16)
            heads = range(AQ_HEADS)
            hsl = lambda h: slice(h * A_HD, (h + 1) * A_HD)
            kbs = [kband[:, hsl(kv)] for kv in range(AKV_HEADS)]
            vbs = [vband[:, hsl(kv)] for kv in range(AKV_HEADS)]
            qks = [lax.dot_general(qb[:, hsl(h)], kbs[h // grp], _NT, preferred_element_type=F32) for h in heads]
            probs = [_attn_probs(qks[h], bias_ref[h], sink_ref[0, h], mask)[0] for h in heads]
            outs = [jnp.dot(probs[h].astype(BF16), vbs[h // grp], preferred_element_type=F32) for h in heads]
            y_ref[pl.ds(r0, WINDOW), :] = jnp.concatenate(outs, axis=1).astype(BF16)
            return carry

        lax.fori_loop(0, nb, blk, 0)

    return pl.pallas_call(
        body, name="attn_fwd", grid=(bsz,),
        in_specs=[_vspec((s, AQ), lambda b: (b, C_Q // AQ)), _vspec((s, AKV), lambda b: (b, C_K // AKV)),
                  _vspec((s, AKV), lambda b: (b, C_V // AKV)),
                  _vspec((AQ_HEADS, WINDOW, 2 * WINDOW), lambda b: (0, 0, 0)), pl.BlockSpec(memory_space=pltpu.SMEM)],
        out_specs=_vspec((s, AQ), lambda b: (b, 0)), out_shape=jax.ShapeDtypeStruct((t, AQ), BF16),
        scratch_shapes=[pltpu.VMEM((s + WINDOW, LANES), BF16), pltpu.VMEM((s + WINDOW, LANES), BF16)],
    )(proj, proj, proj, bias, sinks)


def _attn_bwd(proj, bias, sinks, dy, bsz, plug=None):
    t = proj.shape[0]
    s = t // bsz
    nb = s // WINDOW
    grp = AQ_HEADS // AKV_HEADS
    scale = A_HD ** -0.5

    def body(q_ref, k_ref, v_ref, bias_ref, sink_ref, dy_ref, dq_ref, dk_ref, dv_ref, dbias_ref, dsink_ref,
             kp_ref, vp_ref, dkp_ref, dvp_ref):
        b = pl.program_id(0)
        kp_ref[0:WINDOW, :] = jnp.zeros((WINDOW, LANES), BF16)
        vp_ref[0:WINDOW, :] = jnp.zeros((WINDOW, LANES), BF16)
        kp_ref[WINDOW:, :] = k_ref[...].astype(BF16)
        vp_ref[WINDOW:, :] = v_ref[...].astype(BF16)
        dkp_ref[...] = jnp.zeros_like(dkp_ref)
        dvp_ref[...] = jnp.zeros_like(dvp_ref)

        @pl.when(b == 0)
        def _():
            dbias_ref[...] = jnp.zeros_like(dbias_ref)
            dsink_ref[...] = jnp.zeros_like(dsink_ref)

        def blk(n, carry):
            r0 = pl.multiple_of(n * WINDOW, WINDOW)
            mask = _attn_mask(n)
            kband = kp_ref[pl.ds(r0, 2 * WINDOW), :]
            vband = vp_ref[pl.ds(r0, 2 * WINDOW), :]
            qb = q_ref[pl.ds(r0, WINDOW), :].astype(BF16)
            dyb = dy_ref[pl.ds(r0, WINDOW), :].astype(BF16)
            heads = range(AQ_HEADS)
            hsl = lambda h: slice(h * A_HD, (h + 1) * A_HD)
            kbs = [kband[:, hsl(kv)] for kv in range(AKV_HEADS)]
            vbs = [vband[:, hsl(kv)] for kv in range(AKV_HEADS)]
            qhs = [qb[:, hsl(h)] for h in heads]
            dyhs = [dyb[:, hsl(h)] for h in heads]
            qks = [lax.dot_general(qhs[h], kbs[h // grp], _NT, preferred_element_type=F32) for h in heads]
            dprobs = [lax.dot_general(dyhs[h], vbs[h // grp], _NT, preferred_element_type=F32) for h in heads]
            pbs, dsbs = [], []
            for h in heads:
                probs, psink = _attn_probs(qks[h], bias_ref[h], sink_ref[0, h], mask)
                rowdot = jnp.sum(probs * dprobs[h], axis=-1, keepdims=True)
                ds = probs * (dprobs[h] - rowdot)
                dbias_ref[h] += ds
                dsink_ref[h] += jnp.sum(-psink * rowdot, axis=0, keepdims=True) + jnp.zeros((1, LANES), F32)
                pbs.append(probs.astype(BF16))
                dsbs.append(ds.astype(BF16))
            dvhs = [lax.dot_general(pbs[h], dyhs[h], _TN, preferred_element_type=F32) for h in heads]
            dqs = [jnp.dot(dsbs[h], kbs[h // grp], preferred_element_type=F32) * scale for h in heads]
            dkhs = [lax.dot_general(dsbs[h], qhs[h], _TN, preferred_element_type=F32) * scale for h in heads]
            dks = [sum(dkhs[kv * grp + 1:(kv + 1) * grp], dkhs[kv * grp]) for kv in range(AKV_HEADS)]
            dvs = [sum(dvhs[kv * grp + 1:(kv + 1) * grp], dvhs[kv * grp]) for kv in range(AKV_HEADS)]
            dq_ref[pl.ds(r0, WINDOW), :] = jnp.concatenate(dqs, axis=1).astype(BF16)
            dkp_ref[pl.ds(r0, 2 * WINDOW), :] += jnp.concatenate(dks, axis=1)
            dvp_ref[pl.ds(r0, 2 * WINDOW), :] += jnp.concatenate(dvs, axis=1)
            return carry

        lax.fori_loop(0, nb, blk, 0)
        dk_ref[...] = dkp_ref[WINDOW:, :].astype(BF16)
        dv_ref[...] = dvp_ref[WINDOW:, :].astype(BF16)

    kvs = jax.ShapeDtypeStruct((t, AKV), BF16)
    outs, extra = _plugged_call(
        body, plug, _grid_ends((bsz,)), (proj, proj, proj, bias, sinks, dy), name="attn_bwd", grid=(bsz,),
        in_specs=[_vspec((s, AQ), lambda b: (b, C_Q // AQ)), _vspec((s, AKV), lambda b: (b, C_K // AKV)),
                  _vspec((s, AKV), lambda b: (b, C_V // AKV)),
                  _vspec((AQ_HEADS, WINDOW, 2 * WINDOW), lambda b: (0, 0, 0)), pl.BlockSpec(memory_space=pltpu.SMEM),
                  _vspec((s, AQ), lambda b: (b, 0))],
        out_specs=[_vspec((s, AQ), lambda b: (b, 0)), _vspec((s, AKV), lambda b: (b, 0)), _vspec((s, AKV), lambda b: (b, 0)),
                   _vspec((AQ_HEADS, WINDOW, 2 * WINDOW), lambda b: (0, 0, 0)), _vspec((AQ_HEADS, 1, LANES), lambda b: (0, 0, 0))],
        out_shape=[jax.ShapeDtypeStruct((t, AQ), BF16), kvs, kvs,
                   jax.ShapeDtypeStruct((AQ_HEADS, WINDOW, 2 * WINDOW), F32), jax.ShapeDtypeStruct((AQ_HEADS, 1, LANES), F32)],
        scratch_shapes=[pltpu.VMEM((s + WINDOW, LANES), BF16), pltpu.VMEM((s + WINDOW, LANES), BF16),
                        pltpu.VMEM((s + WINDOW, LANES), F32), pltpu.VMEM((s + WINDOW, LANES), F32)])
    return (*outs, extra)


def _dn_act_f(c, is_qk):
    a = jax.nn.silu(c)
    outs = []
    for h in range(DN_HEADS):
        ah = a[:, h * DN_HD:(h + 1) * DN_HD]
        nh = ah * lax.rsqrt(jnp.sum(ah * ah, axis=-1, keepdims=True) + L2_EPS)
        outs.append(jnp.where(is_qk, nh, ah))
    return jnp.concatenate(outs, axis=1)


def _dn_prep_fwd(proj, conv_w, bsz):
    t = proj.shape[0]
    s = t // bsz
    blk = _vspec((s, DN), lambda b, j: (b, j))
    wsp = _vspec((DN_CONV, DN), lambda b, j: (0, j))

    def body(x_ref, w_ref, o_ref):
        j = pl.program_id(1)
        o_ref[...] = _dn_act_f(_conv_fwd(x_ref[...], w_ref[...], DN_CONV), j < 2)

    return pl.pallas_call(body, name="dn_prep_fwd", grid=(bsz, 3), in_specs=[blk, wsp], out_specs=blk,
                          out_shape=jax.ShapeDtypeStruct((t, 3 * DN), F32))(proj, conv_w)


def _dn_prep_bwd(proj, conv_w, dqkvn, bsz):
    t = proj.shape[0]
    s = t // bsz
    blk = _vspec((s, DN), lambda j, b: (b, j))
    wsp = _vspec((DN_CONV, DN), lambda j, b: (0, j))

    def body(x_ref, w_ref, d_ref, dx_ref, dw_ref):
        j, b = pl.program_id(0), pl.program_id(1)
        x, w = x_ref[...], w_ref[...]
        c = _conv_fwd(x, w, DN_CONV)
        _, vjp = jax.vjp(lambda cc: _dn_act_f(cc, j < 2), c)
        (dc,) = vjp(d_ref[0])
        dx, dw = _conv_bwd(x, w, dc, DN_CONV)
        dx_ref[...] = dx.astype(BF16)

        @pl.when(b == 0)
        def _():
            dw_ref[...] = jnp.zeros_like(dw_ref)

        dw_ref[...] += dw

    return pl.pallas_call(
        body, name="dn_prep_bwd", grid=(3, bsz),
        in_specs=[blk, wsp, _vspec((1, s, DN), lambda j, b: (j, b, 0))], out_specs=[blk, wsp],
        out_shape=[jax.ShapeDtypeStruct((t, 3 * DN), BF16), jax.ShapeDtypeStruct((DN_CONV, 3 * DN), F32)],
    )(proj, conv_w, dqkvn)


def _bg_f(x, alog, dt):
    lane = lax.broadcasted_iota(jnp.int32, x.shape, 1)
    beta = jax.nn.sigmoid(x)
    g = -jnp.exp(alog) * jax.nn.softplus(x + dt)
    return jnp.where(lane < DN_HEADS, beta, jnp.where(lane < 2 * DN_HEADS, g, 0.0))


def _bg_fwd(proj, alog, dt, bsz):
    t = proj.shape[0]
    s = t // bsz
    vec = _vspec((1, LANES), lambda b: (0, 0))

    def body(x_ref, a_ref, d_ref, o_ref):
        o_ref[...] = _bg_f(x_ref[...], a_ref[...], d_ref[...])

    return pl.pallas_call(body, name="bg_fwd", grid=(bsz,), in_specs=[_vspec((s, LANES), lambda b: (b, C_BD // LANES)), vec, vec],
                          out_specs=_vspec((s, LANES), lambda b: (b, 0)), out_shape=jax.ShapeDtypeStruct((t, LANES), F32))(proj, alog, dt)


def _bg_bwd(proj, alog, dt, dbg4, bsz):
    t = proj.shape[0]
    s = t // bsz
    vec = _vspec((1, LANES), lambda b: (0, 0))

    def body(x_ref, a_ref, d_ref, g4_ref, dx_ref, da_ref, dd_ref):
        b = pl.program_id(0)
        lane = lax.broadcasted_iota(jnp.int32, (s, LANES), 1)
        dbg = jnp.zeros((s, LANES), F32)
        for h in range(DN_HEADS):
            gh = g4_ref[:, h * DN_HD:(h + 1) * DN_HD]
            dbg = jnp.where(lane == h, gh[:, 0:1], dbg)
            dbg = jnp.where(lane == DN_HEADS + h, gh[:, 1:2], dbg)
        _, vjp = jax.vjp(_bg_f, x_ref[...], a_ref[...], d_ref[...])
        dx, da, dd = vjp(dbg)
        dx_ref[...] = dx.astype(BF16)

        @pl.when(b == 0)
        def _():
            da_ref[...] = jnp.zeros_like(da_ref)
            dd_ref[...] = jnp.zeros_like(dd_ref)

        da_ref[...] += da
        dd_ref[...] += dd

    return pl.pallas_call(
        body, name="bg_bwd", grid=(bsz,),
        in_specs=[_vspec((s, LANES), lambda b: (b, C_BD // LANES)), vec, vec, _vspec((s, DN), lambda b: (b, 0))],
        out_specs=[_vspec((s, LANES), lambda b: (b, 0)), vec, vec],
        out_shape=[jax.ShapeDtypeStruct((t, LANES), BF16), jax.ShapeDtypeStruct((1, LANES), F32), jax.ShapeDtypeStruct((1, LANES), F32)],
    )(proj, alog, dt, dbg4)


def _dn_out_f(o, z, w):
    outs = []
    for h in range(DN_HEADS):
        sl = slice(h * DN_HD, (h + 1) * DN_HD)
        outs.append(_rms(o[:, sl], w) * jax.nn.silu(z[:, sl]))
    return jnp.concatenate(outs, axis=1)


def _dn_out_fwd(o, proj, w, ts=512):
    t = o.shape[0]
    blk = _vspec((ts, DN), lambda i: (i, 0))
    zsp = _vspec((ts, DN), lambda i: (i, C_DZ // DN))
    vec = _vspec((1, DN_HD), lambda i: (0, 0))

    def body(o_ref, z_ref, w_ref, y_ref):
        y_ref[...] = _dn_out_f(o_ref[...], z_ref[...], w_ref[...]).astype(BF16)

    return pl.pallas_call(body, name="dn_out_fwd", grid=(t // ts,), in_specs=[blk, zsp, vec], out_specs=blk,
                          out_shape=jax.ShapeDtypeStruct((t, DN), BF16))(o, proj, w)


def _dn_out_bwd(o, proj, w, dy, ts=512):
    t = o.shape[0]
    blk = _vspec((ts, DN), lambda i: (i, 0))
    zsp = _vspec((ts, DN), lambda i: (i, C_DZ // DN))
    vec = _vspec((1, DN_HD), lambda i: (0, 0))

    def body(o_ref, z_ref, w_ref, dy_ref, do_ref, dz_ref, dw_ref):
        i = pl.program_id(0)
        _, vjp = jax.vjp(_dn_out_f, o_ref[...], z_ref[...], w_ref[...])
        do, dz, dw = vjp(dy_ref[...])
        do_ref[...] = do
        dz_ref[...] = dz.astype(BF16)

        @pl.when(i == 0)
        def _():
            dw_ref[...] = jnp.zeros_like(dw_ref)

        dw_ref[...] += dw

    return pl.pallas_call(
        body, name="dn_out_bwd", grid=(t // ts,), in_specs=[blk, zsp, vec, blk], out_specs=[blk, blk, vec],
        out_shape=[jax.ShapeDtypeStruct((t, DN), F32), jax.ShapeDtypeStruct((t, DN), BF16), jax.ShapeDtypeStruct((1, DN_HD), F32)],
    )(o, proj, w, dy)


_C = DN_CHUNK


def _dot(a, b, dims):
    return lax.dot_general(a.astype(BF16), b.astype(BF16), dims, preferred_element_type=F32)


def _split(a):
    hi = a.astype(BF16)
    return hi, (a - hi.astype(F32)).astype(BF16)


def _dot3(a, b, dims):
    (ah, al), (bh, bl) = (a if isinstance(a, tuple) else _split(a)), (b if isinstance(b, tuple) else _split(b))
    mm = lambda x, y: lax.dot_general(x, y, dims, preferred_element_type=F32)
    return mm(ah, bh) + (mm(ah, bl) + mm(al, bh))


_NN = (((1,), (0,)), ((), ()))
_NT = (((1,), (1,)), ((), ()))
_TN = (((0,), (0,)), ((), ()))


_SUB = 8


def _tri_inverses(ls, lts):
    ri8 = lax.broadcasted_iota(jnp.int32, (_SUB, _C), 0)
    ci8 = lax.broadcasted_iota(jnp.int32, (_SUB, _C), 1)
    nblk = _C // _SUB
    ts = []
    for lt in lts:
        blocks = [jnp.where(ci8 == ri8 + _SUB * b, 1.0, 0.0).astype(F32) for b in range(nblk)]
        for r in range(1, _SUB):
            for b in range(nblk):
                coef = lt[_SUB * b:_SUB * (b + 1), _SUB * b + r:_SUB * b + r + 1]
                row = jnp.sum(coef * blocks[b], axis=0, keepdims=True)
                blocks[b] = jnp.where(ri8 == r, blocks[b] - row, blocks[b])
        ts.append(jnp.concatenate(blocks, axis=0))
    ri = lax.broadcasted_iota(jnp.int32, (_C, _C), 0)
    ci = lax.broadcasted_iota(jnp.int32, (_C, _C), 1)
    s = _SUB
    while s < _C:
        shift = s.bit_length()
        quad = ((ri >> shift) == (ci >> shift)) & ((ri & s) != 0) & ((ci & s) == 0)
        offs = [jnp.where(quad, l, 0.0) for l in ls]
        tsp = [_split(t) for t in ts]
        left = [_dot3(tp, off, _NN) for tp, off in zip(tsp, offs)]
        ts = [t - _dot3(lo, tp, _NN) for t, lo, tp in zip(ts, left, tsp)]
        s *= 2
    return ts


_SEG = 512
_HEADS = tuple(range(DN_HEADS))


def _hsl(hh):
    return slice(hh * DN_HD, (hh + 1) * DN_HD)


def _chunk_specs(bsz, nseg, reverse):
    seg = (lambda i: nseg - 1 - i) if reverse else (lambda i: i)
    ncs = _SEG // _C
    col = lambda off: _vspec((bsz, _SEG, DN), lambda i: (0, seg(i), off))
    return (col, _vspec((bsz, _SEG, LANES), lambda i: (0, seg(i), 0)),
            _vspec((bsz, DN_HEADS, ncs, _C), lambda i: (0, 0, seg(i), 0)),
            _vspec((bsz, DN_HEADS, ncs, DN_HD, DN_HD), lambda i: (0, 0, seg(i), 0, 0)))


def _chunk_pre(q_ref, k_ref, v_ref, bg_ref, gr_ref, c, bb, hh):
    r0 = pl.multiple_of(c * _C, _C)
    ri = lax.broadcasted_iota(jnp.int32, (_C, _C), 0)
    ci = lax.broadcasted_iota(jnp.int32, (_C, _C), 1)
    q = q_ref[bb, pl.ds(r0, _C), _hsl(hh)] * (DN_HD ** -0.5)
    k = k_ref[bb, pl.ds(r0, _C), _hsl(hh)]
    v = v_ref[bb, pl.ds(r0, _C), _hsl(hh)]
    bgc = bg_ref[bb, pl.ds(r0, _C), :]
    beta = bgc[:, hh:hh + 1]
    g_col = bgc[:, DN_HEADS + hh:DN_HEADS + hh + 1]
    g_row = gr_ref[bb, hh, pl.ds(c, 1), :]
    gc_col = jnp.sum(jnp.where(ri >= ci, g_row, 0.0), axis=1, keepdims=True)
    gc_row = jnp.sum(jnp.where(ri <= ci, g_col, 0.0), axis=0, keepdims=True)
    gc_last = jnp.sum(g_col, axis=0, keepdims=True)
    diff = gc_col - gc_row
    decay = jnp.where(ri >= ci, jnp.exp(jnp.where(ri >= ci, diff, 0.0)), 0.0)
    diff_t = gc_row - gc_col
    decay_t = jnp.where(ri <= ci, jnp.exp(jnp.where(ri <= ci, diff_t, 0.0)), 0.0)
    eg = jnp.exp(gc_col)
    et = jnp.exp(gc_last - gc_col)
    gl = jnp.exp(gc_last)
    kb = k * beta
    vb = v * beta
    return dict(r0=r0, bb=bb, hh=hh, q=q, k=k, v=v, beta=beta, decay=decay, decay_t=decay_t, eg=eg, et=et, gl=gl, kb=kb, vb=vb,
                ri=ri, ci=ci)


def _chunk_solve(ms, with_transposes):
    for m in ms:
        m["kk_t"] = _dot(m["k"], m["kb"], _NT)
        m["qk"] = _dot(m["q"], m["k"], _NT)
        m["kk"] = _dot(m["kb"], m["k"], _NT)
        if with_transposes:
            m["qk_t"] = _dot(m["k"], m["q"], _NT)
    tms = _tri_inverses([jnp.where(m["ri"] > m["ci"], m["kk"] * m["decay"], 0.0) for m in ms],
                        [jnp.where(m["ri"] < m["ci"], m["kk_t"] * m["decay_t"], 0.0) for m in ms])
    for m, tm in zip(ms, tms):
        m["tm"] = tm
    for m in ms:
        rhs = jnp.concatenate([m["vb"], m["kb"] * m["eg"]], axis=1)
        m["tm"] = _split(m["tm"])
        m["sol"] = _dot3(m["tm"], rhs, _NN)
        m["intra"] = jnp.where(m["ri"] >= m["ci"], m["qk"] * m["decay"], 0.0)


def _dn_chunk_fwd(qkvn, bg, g_rows, bsz, plug=None):
    t = qkvn.shape[0]
    s = t // bsz
    nc, nseg = s // _C, s // _SEG
    pairs = [(bb, hh) for bb in range(bsz) for hh in _HEADS]

    def body(q_ref, k_ref, v_ref, bg_ref, gr_ref, o_ref, st_ref, s_ref):
        @pl.when(pl.program_id(0) == 0)
        def _():
            s_ref[...] = jnp.zeros_like(s_ref)

        def chunk(c, carry):
            ms = [_chunk_pre(q_ref, k_ref, v_ref, bg_ref, gr_ref, c, bb, hh) for bb, hh in pairs]
            _chunk_solve(ms, False)
            sts = [s_ref[i] for i in range(len(pairs))]
            for (bb, hh), st in zip(pairs, sts):
                st_ref[bb, hh, c] = st
            ws = [_dot(m["sol"][:, DN_HD:], st, _NN) for m, st in zip(ms, sts)]
            qs = [_dot(m["q"] * m["eg"], st, _NN) for m, st in zip(ms, sts)]
            v_new = [m["sol"][:, :DN_HD] - a for m, a in zip(ms, ws)]
            iv = [_dot(m["intra"], vn, _NN) for m, vn in zip(ms, v_new)]
            upd = [_dot(m["k"] * m["et"], vn, _TN) for m, vn in zip(ms, v_new)]
            for i, (bb, hh) in enumerate(pairs):
                s_ref[i] = sts[i] * ms[i]["gl"] + upd[i]
                o_ref[bb, pl.ds(ms[i]["r0"], _C), _hsl(hh)] = qs[i] + iv[i]
            return carry

        lax.fori_loop(0, _SEG // _C, chunk, 0)

    col, bgs, grs, sts_spec = _chunk_specs(bsz, nseg, False)
    q3, bg3 = qkvn.reshape(bsz, s, 3 * DN), bg.reshape(bsz, s, LANES)
    (o, states), extra = _plugged_call(
        body, plug, _grid_ends((nseg,)), (q3, q3, q3, bg3, g_rows), name="dn_chunk_fwd", grid=(nseg,),
        in_specs=[col(0), col(1), col(2), bgs, grs], out_specs=[col(0), sts_spec],
        out_shape=[jax.ShapeDtypeStruct((bsz, s, DN), F32), jax.ShapeDtypeStruct((bsz, DN_HEADS, nc, DN_HD, DN_HD), F32)],
        scratch_shapes=[pltpu.VMEM((bsz * DN_HEADS, DN_HD, DN_HD), F32)])
    return o.reshape(t, DN), states, extra


def _dn_chunk_bwd(qkvn, bg, g_rows, states, do, bsz, plug=None):
    t = qkvn.shape[0]
    s = t // bsz
    nc, nseg = s // _C, s // _SEG
    pairs = [(bb, hh) for bb in range(bsz) for hh in _HEADS]

    def body(q_ref, k_ref, v_ref, bg_ref, gr_ref, st_ref, do_ref, dqkv_ref, dbg_ref, ds_ref):
        @pl.when(pl.program_id(0) == 0)
        def _():
            ds_ref[...] = jnp.zeros_like(ds_ref)

        def chunk(cc, carry):
            c = _SEG // _C - 1 - cc
            ms = [_chunk_pre(q_ref, k_ref, v_ref, bg_ref, gr_ref, c, bb, hh) for bb, hh in pairs]
            _chunk_solve(ms, True)
            ri, ci = ms[0]["ri"], ms[0]["ci"]
            for i, m in enumerate(ms):
                m["st"] = st_ref[m["bb"], m["hh"], c]
                m["ds_out"] = ds_ref[i]
                m["do"] = do_ref[m["bb"], pl.ds(m["r0"], _C), _hsl(m["hh"])]
                m["w"] = m["sol"][:, DN_HD:]
            for m in ms:
                m["v_new"] = m["sol"][:, :DN_HD] - _dot(m["w"], m["st"], _NN)
            for m in ms:
                m["q_dec"], m["k_tail"] = m["q"] * m["eg"], m["k"] * m["et"]
                m["dk_tail"] = _dot(m["v_new"], m["ds_out"], _NT)
                m["dv_new"] = _dot(m["k_tail"], m["ds_out"], _NN) + _dot(m["intra"], m["do"], _TN)
                m["dq_dec"] = _dot(m["do"], m["st"], _NT)
                m["ds_in"] = m["ds_out"] * m["gl"] + _dot(m["q_dec"], m["do"], _TN)
                m["dintra"] = jnp.where(ri >= ci, _dot(m["do"], m["v_new"], _NT), 0.0)
                m["dintra_t"] = jnp.where(ri <= ci, _dot(m["v_new"], m["do"], _NT), 0.0)
            for m in ms:
                m["dw"] = -_dot(m["dv_new"], m["st"], _NT)
                m["ds_in"] = m["ds_in"] - _dot(m["w"], m["dv_new"], _TN)
            for m in ms:
                dsol = jnp.concatenate([m["dv_new"], m["dw"]], axis=1)
                m["drhs"] = _dot3(m["tm"], dsol, _TN)
            for m in ms:
                m["dl"] = jnp.where(ri > ci, -_dot(m["drhs"], m["sol"], _NT), 0.0)
                m["dl_t"] = jnp.where(ri < ci, -_dot(m["sol"], m["drhs"], _NT), 0.0)
            for m in ms:
                m["dkb2"] = _dot(m["dl"] * m["decay"], m["k"], _NN)
                m["dk"] = _dot(m["dl_t"] * m["decay_t"], m["kb"], _NN) + _dot(m["dintra_t"] * m["decay_t"], m["q"], _NN)
                m["dq"] = _dot(m["dintra"] * m["decay"], m["k"], _NN)
            for m in ms:
                _chunk_bwd_finish(m)
            for m in ms:
                ones_ge = jnp.where(ri <= ci, 1.0, 0.0).astype(BF16)
                gh, gl_ = _split(m["dgc"] + jnp.zeros((_C, LANES), F32))
                m["dg_b"] = jnp.dot(ones_ge, gh, preferred_element_type=F32) + jnp.dot(ones_ge, gl_, preferred_element_type=F32)
            lane = lax.broadcasted_iota(jnp.int32, (_C, LANES), 1)
            for i, m in enumerate(ms):
                bb, hh, rows = m["bb"], m["hh"], pl.ds(m["r0"], _C)
                dqkv_ref[0, bb, rows, _hsl(hh)] = m["dq"] * (DN_HD ** -0.5)
                dqkv_ref[1, bb, rows, _hsl(hh)] = m["dk"]
                dqkv_ref[2, bb, rows, _hsl(hh)] = m["dv"]
                dbg_ref[bb, rows, _hsl(hh)] = jnp.where(lane == 0, m["dbeta"], jnp.where(lane == 1, m["dg_b"], 0.0))
                ds_ref[i] = m["ds_in"]
            return carry

        lax.fori_loop(0, _SEG // _C, chunk, 0)

    col, bgs, grs, sts_spec = _chunk_specs(bsz, nseg, True)
    q3, bg3, do3 = qkvn.reshape(bsz, s, 3 * DN), bg.reshape(bsz, s, LANES), do.reshape(bsz, s, DN)
    (dqkv, dbg), extra = _plugged_call(
        body, plug, _grid_ends((nseg,)), (q3, q3, q3, bg3, g_rows, states, do3), name="dn_chunk_bwd", grid=(nseg,),
        in_specs=[col(0), col(1), col(2), bgs, grs, sts_spec, col(0)],
        out_specs=[_vspec((3, bsz, _SEG, DN), lambda i: (0, 0, nseg - 1 - i, 0)), col(0)],
        out_shape=[jax.ShapeDtypeStruct((3, bsz, s, DN), F32), jax.ShapeDtypeStruct((bsz, s, DN), F32)],
        scratch_shapes=[pltpu.VMEM((bsz * DN_HEADS, DN_HD, DN_HD), F32)])
    return dqkv.reshape(3, t, DN), dbg.reshape(t, DN), extra


def _chunk_bwd_finish(m):
    q, k, v, beta, decay, decay_t = m["q"], m["k"], m["v"], m["beta"], m["decay"], m["decay_t"]
    eg, et, gl, kb, dl, dl_t, dintra, dintra_t = m["eg"], m["et"], m["gl"], m["kb"], m["dl"], m["dl_t"], m["dintra"], m["dintra_t"]
    dq_dec, dk_tail, dq, dk = m["dq_dec"], m["dk_tail"], m["dq"], m["dk"]
    dgl = jnp.sum(jnp.sum(m["ds_out"] * m["st"], axis=1, keepdims=True), axis=0, keepdims=True)
    dvb, dkbeg = m["drhs"][:, :DN_HD], m["drhs"][:, DN_HD:]
    dkb = dkbeg * eg + m["dkb2"]
    deg = jnp.sum(dkbeg * kb, axis=1, keepdims=True)
    em = (dl * m["kk"] + dintra * m["qk"]) * decay
    em_t = (dl_t * m["kk_t"] + dintra_t * m["qk_t"]) * decay_t
    dgc = jnp.sum(em, axis=1, keepdims=True) - jnp.sum(em_t, axis=1, keepdims=True)
    dq = dq + dq_dec * eg
    deg = deg + jnp.sum(dq_dec * q, axis=1, keepdims=True)
    dk = dk + dk_tail * et
    det = jnp.sum(dk_tail * k, axis=1, keepdims=True)
    dgc = dgc + deg * eg - det * et
    dgc_last = jnp.sum(det * et, axis=0, keepdims=True) + dgl * gl
    rcol = lax.broadcasted_iota(jnp.int32, (_C, 1), 0)
    m["dgc"] = dgc + jnp.where(rcol == _C - 1, dgc_last, 0.0)
    m["dq"] = dq
    m["dk"] = dk + dkb * beta
    m["dbeta"] = jnp.sum(dkb * k, axis=1, keepdims=True) + jnp.sum(dvb * v, axis=1, keepdims=True)
    m["dv"] = dvb * beta


def _mod_fwd(c_all, ada_w_loc, ada_b_loc):
    n, cols = c_all.shape[0], ada_w_loc.shape[1]

    def body(c_ref, w_ref, b_ref, o_ref):
        o_ref[...] = _dot(jax.nn.silu(c_ref[...]), w_ref[...], _NN) + b_ref[...]

    return pl.pallas_call(body, name="mod_fwd", out_shape=jax.ShapeDtypeStruct((n, cols), F32))(c_all, ada_w_loc, ada_b_loc)


def _ada_grad(c_all, dmod_loc, dmod_all):
    d, cols = c_all.shape[1], dmod_loc.shape[1]

    def body(c_ref, dl_ref, da_ref, gw_ref, gb_ref):
        gw_ref[...] = _dot(jax.nn.silu(c_ref[...]), dl_ref[...], _TN)
        gb_ref[...] = jnp.sum(da_ref[...], axis=0, keepdims=True)

    return pl.pallas_call(body, name="ada_grad", out_shape=[jax.ShapeDtypeStruct((d, cols), F32),
                                                           jax.ShapeDtypeStruct((1, dmod_all.shape[1]), F32)])(c_all, dmod_loc, dmod_all)


ELEMENTWISE_BLOCK_BYTES = 3 * 2 ** 19


def _row_tile(r, c=1024):
    fits = [tr for tr in range(16, r + 1, 16) if tr * c * 4 <= ELEMENTWISE_BLOCK_BYTES]
    if not fits:
        return r
    whole = [tr for tr in fits if r % tr == 0]
    return whole[-1] if whole else fits[-1]


def _adamw(w, m, v, grads, name):
    r, c = w.shape
    tr = _row_tile(r, c)
    blk = _vspec((tr, c), lambda i: (i, 0))
    n = len(grads)

    def body(*refs):
        w_ref, m_ref, v_ref = refs[:3]
        g_ref, d_ref, mo_ref, vo_ref = refs[3 + n:]
        g = refs[3][...]
        for p in refs[4:3 + n]:
            g = g + p[...]
        g_ref[...] = g
        d_ref[...], mo_ref[...], vo_ref[...] = _adamw_math(w_ref[...], m_ref[...], v_ref[...], g)

    o = jax.ShapeDtypeStruct((r, c), F32)
    return pl.pallas_call(body, name=name, grid=(pl.cdiv(r, tr),), in_specs=[blk] * (3 + n), out_specs=[blk] * 4,
                          out_shape=[o] * 4)(w, m, v, *grads)


def _adamw_math(w, m, v, g):
    m_new = ADAM_B1 * m + (1.0 - ADAM_B1) * g
    v_new = ADAM_B2 * v + (1.0 - ADAM_B2) * jnp.square(g)
    m_hat = m_new / (1.0 - ADAM_B1 ** ADAM_STEP)
    v_hat = v_new / (1.0 - ADAM_B2 ** ADAM_STEP)
    return -ADAM_LR * (m_hat / (jnp.sqrt(v_hat) + ADAM_EPS) + ADAM_WD * w), m_new, v_new


def _adamw_small(ws, ms, vs, gs, name):
    n = len(ws)

    def body(*refs):
        for i in range(n):
            w_ref, m_ref, v_ref, g_ref = (refs[j * n + i] for j in range(4))
            go_ref, d_ref, mo_ref, vo_ref = refs[4 * n + 4 * i:4 * n + 4 * i + 4]
            g = g_ref[...]
            go_ref[...] = g
            d_ref[...], mo_ref[...], vo_ref[...] = _adamw_math(w_ref[...], m_ref[...], v_ref[...], g)

    res = pl.pallas_call(body, name=name, out_shape=[jax.ShapeDtypeStruct(a.shape, F32) for a in ws for _ in range(4)])(
        *ws, *ms, *vs, *gs)
    return [res[4 * i:4 * i + 4] for i in range(n)]


def _sum_lead(x, name):
    p, r, c = x.shape
    tr = _row_tile(r, c)

    def body(x_ref, o_ref):
        acc = x_ref[0].astype(F32)
        for i in range(1, p):
            acc = acc + x_ref[i].astype(F32)
        o_ref[...] = acc

    return pl.pallas_call(body, name=name, grid=(pl.cdiv(r, tr),), in_specs=[_vspec((p, tr, c), lambda i: (0, i, 0))],
                          out_specs=_vspec((tr, c), lambda i: (i, 0)), out_shape=jax.ShapeDtypeStruct((r, c), F32))(x)


def _allgather8(x_shard, name):
    m_per, n = x_shard.shape

    def body(x_ref, out_ref, send_sems, recv_sems, local_sem):
        x, y, c = lax.axis_index("x"), lax.axis_index("y"), lax.axis_index("c")
        me, sibling = (x, y, c), (x, y, 1 - c)
        chips = [(1 - x, y), (x, 1 - y), (1 - x, 1 - y)]

        def rows(px, py, pc):
            return out_ref.at[pl.ds((4 * px + 2 * py + pc) * m_per, m_per), :]

        def copy(k, block, to, src=None):
            return pltpu.make_async_remote_copy(
                src_ref=rows(*block) if src is None else src, dst_ref=rows(*block), send_sem=send_sems.at[k],
                recv_sem=recv_sems.at[k], device_id=to, device_id_type=MESH)

        mine = pltpu.make_async_copy(x_ref, rows(*me), local_sem)
        mine.start()
        first = [copy(0, me, sibling, src=x_ref)]
        first += [copy(1 + j, me, (*chip, c), src=x_ref) for j, chip in enumerate(chips)]
        for cp in first:
            cp.start()
        passed = [copy(4 + j, (*chip, c), sibling) for j, chip in enumerate(chips)]
        for j, chip in enumerate(chips):
            copy(1 + j, (*chip, c), me).wait_recv()
            passed[j].start()
        copy(0, sibling, me).wait_recv()
        for j, chip in enumerate(chips):
            copy(4 + j, (*chip, 1 - c), me).wait_recv()
        for cp in first + passed:
            cp.wait_send()
        mine.wait()

    return pl.pallas_call(
        body, name=name, out_shape=jax.ShapeDtypeStruct((8 * m_per, n), x_shard.dtype),
        in_specs=[pl.BlockSpec(memory_space=pltpu.VMEM)], out_specs=pl.BlockSpec(memory_space=pltpu.VMEM),
        scratch_shapes=[pltpu.SemaphoreType.DMA((7,)), pltpu.SemaphoreType.DMA((7,)), pltpu.SemaphoreType.DMA],
    )(x_shard)


_HBM = pl.BlockSpec(memory_space=pltpu.HBM)


def _mesh_place():
    x, y, c = lax.axis_index("x"), lax.axis_index("y"), lax.axis_index("c")
    return x, y, c, 2 * x + y, [(1 - x, y), (x, 1 - y), (1 - x, 1 - y)]


def _gather_plug(shards):
    n = len(shards)

    def half(ref, c, lead=None):
        r, cols = ref.shape[-2] // 2, ref.shape[-1] // 2
        if r % 16 == 0:
            rows = pl.ds(pl.multiple_of(c * r, 16), r)
            return ref.at[rows, :] if lead is None else ref.at[lead, rows, :]
        lanes = pl.ds(pl.multiple_of(c * cols, LANES), cols)
        return ref.at[:, lanes] if lead is None else ref.at[lead, :, lanes]

    def copies(ins, outs, send, recv):
        x, y, c, me, chips = _mesh_place()
        ici, fwd, fwd_in = [], [], []
        for i in range(n):
            for j, (px, py) in enumerate(chips):
                q = 2 * px + py
                ici.append((pltpu.make_async_remote_copy(
                    src_ref=half(ins[i], c), dst_ref=half(outs[i], c, me), send_sem=send.at[6 * i + j], recv_sem=recv.at[6 * i + j],
                    device_id=(px, py, c), device_id_type=MESH),
                    pltpu.make_async_remote_copy(
                    src_ref=half(ins[i], c), dst_ref=half(outs[i], c, q), send_sem=send.at[6 * i + j], recv_sem=recv.at[6 * i + j],
                    device_id=(px, py, c), device_id_type=MESH)))
                fwd.append(pltpu.make_async_remote_copy(
                    src_ref=half(outs[i], c, q), dst_ref=half(outs[i], c, q), send_sem=send.at[6 * i + 3 + j],
                    recv_sem=recv.at[6 * i + 3 + j], device_id=(x, y, 1 - c), device_id_type=MESH))
                fwd_in.append(pltpu.make_async_remote_copy(
                    src_ref=half(outs[i], 1 - c, q), dst_ref=half(outs[i], 1 - c, q), send_sem=send.at[6 * i + 3 + j],
                    recv_sem=recv.at[6 * i + 3 + j], device_id=(x, y, 1 - c), device_id_type=MESH))
        return ici, fwd, fwd_in, me

    def start(ins, outs, send, recv, loc):
        ici, _, _, me = copies(ins, outs, send, recv)
        for i in range(n):
            pltpu.make_async_copy(ins[i], outs[i].at[me], loc.at[i]).start()
        for out_cp, _ in ici:
            out_cp.start()

    def finish(ins, outs, send, recv, loc):
        ici, fwd, fwd_in, me = copies(ins, outs, send, recv)
        for (_, in_cp), f in zip(ici, fwd):
            in_cp.wait_recv()
            f.start()
        for f in fwd_in:
            f.wait_recv()
        for (out_cp, _), f in zip(ici, fwd):
            out_cp.wait_send()
            f.wait_send()
        for i in range(n):
            pltpu.make_async_copy(ins[i], outs[i].at[me], loc.at[i]).wait()

    return dict(ins=list(shards), out_shape=[jax.ShapeDtypeStruct((N_CHIPS,) + a.shape, a.dtype) for a in shards],
                scratch=[pltpu.SemaphoreType.DMA((6 * n,)), pltpu.SemaphoreType.DMA((6 * n,)), pltpu.SemaphoreType.DMA((n,))],
                start=start, finish=finish)


def _exchange_plug(pieces):
    n = len(pieces)

    def copies(ins, outs, send, recv):
        x, y, c, me, chips = _mesh_place()
        out_cps, in_cps = [], []
        for i in range(n):
            for j, (px, py) in enumerate(chips):
                q = 2 * px + py
                out_cps.append(pltpu.make_async_remote_copy(src_ref=ins[i].at[q], dst_ref=outs[i].at[me], send_sem=send.at[3 * i + j],
                                                            recv_sem=recv.at[3 * i + j], device_id=(px, py, c), device_id_type=MESH))
                in_cps.append(pltpu.make_async_remote_copy(src_ref=ins[i].at[me], dst_ref=outs[i].at[q], send_sem=send.at[3 * i + j],
                                                           recv_sem=recv.at[3 * i + j], device_id=(px, py, c), device_id_type=MESH))
        return out_cps, in_cps, me

    def start(ins, outs, send, recv, loc):
        out_cps, _, me = copies(ins, outs, send, recv)
        for i in range(n):
            pltpu.make_async_copy(ins[i].at[me], outs[i].at[me], loc.at[i]).start()
        for cp in out_cps:
            cp.start()

    def finish(ins, outs, send, recv, loc):
        out_cps, in_cps, me = copies(ins, outs, send, recv)
        for cp in in_cps:
            cp.wait_recv()
        for cp in out_cps:
            cp.wait_send()
        for i in range(n):
            pltpu.make_async_copy(ins[i].at[me], outs[i].at[me], loc.at[i]).wait()

    return dict(ins=list(pieces), out_shape=[jax.ShapeDtypeStruct(a.shape, a.dtype) for a in pieces],
                scratch=[pltpu.SemaphoreType.DMA((3 * n,)), pltpu.SemaphoreType.DMA((3 * n,)), pltpu.SemaphoreType.DMA((n,))],
                start=start, finish=finish)


def _comm_call(plug, name):
    n_in, n_out = len(plug["ins"]), len(plug["out_shape"])

    def body(*refs):
        ins, outs, sems = refs[:n_in], refs[n_in:n_in + n_out], refs[n_in + n_out:]
        plug["start"](ins, outs, *sems)
        plug["finish"](ins, outs, *sems)

    return pl.pallas_call(body, name=name, out_shape=plug["out_shape"], in_specs=[_HBM] * n_in, out_specs=[_HBM] * n_out,
                          scratch_shapes=plug["scratch"])(*plug["ins"])


def _plugged_call(body, plug, first_last, args, *, name, grid, in_specs, out_specs, out_shape, scratch_shapes=()):
    in_specs, out_specs, out_shape, scratch_shapes = list(in_specs), list(out_specs), list(out_shape), list(scratch_shapes)
    if plug is None:
        return pl.pallas_call(body, name=name, grid=grid, in_specs=in_specs, out_specs=out_specs, out_shape=out_shape,
                              scratch_shapes=scratch_shapes)(*args), []
    n_in, n_out, n_sc = len(in_specs), len(out_specs), len(scratch_shapes)
    p_in, p_out = len(plug["ins"]), len(plug["out_shape"])

    def full(*refs):
        ins, refs = refs[:n_in], refs[n_in:]
        pins, refs = refs[:p_in], refs[p_in:]
        outs, refs = refs[:n_out], refs[n_out:]
        pouts, refs = refs[:p_out], refs[p_out:]
        scr, psems = refs[:n_sc], refs[n_sc:]
        first, last = first_last()

        @pl.when(first)
        def _():
            plug["start"](pins, pouts, *psems)

        body(*ins, *outs, *scr)

        @pl.when(last)
        def _():
            plug["finish"](pins, pouts, *psems)

    res = pl.pallas_call(full, name=name, grid=grid, in_specs=in_specs + [_HBM] * p_in, out_specs=out_specs + [_HBM] * p_out,
                         out_shape=out_shape + plug["out_shape"], scratch_shapes=scratch_shapes + plug["scratch"])(*args, *plug["ins"])
    return res[:n_out], res[n_out:]


def _grid_ends(grid):
    def ends():
        first = last = None
        for ax, n in enumerate(grid):
            i = pl.program_id(ax)
            first = (i == 0) if first is None else first & (i == 0)
            last = (i == n - 1) if last is None else last & (i == n - 1)
        return first, last
    return ends


def _sibling_exchange(arrs, name):
    n = len(arrs)

    def body(*refs):
        ins, outs = refs[:n], refs[n:2 * n]
        send_sems, recv_sems = refs[2 * n:]
        sibling = (lax.axis_index("x"), lax.axis_index("y"), 1 - lax.axis_index("c"))
        cps = [pltpu.make_async_remote_copy(src_ref=ins[i], dst_ref=outs[i], send_sem=send_sems.at[i], recv_sem=recv_sems.at[i],
                                            device_id=sibling, device_id_type=MESH) for i in range(n)]
        for cp in cps:
            cp.start()
        for cp in cps:
            cp.wait()

    return pl.pallas_call(
        body, name=name, out_shape=[jax.ShapeDtypeStruct(a.shape, a.dtype) for a in arrs], in_specs=[_HBM] * n, out_specs=[_HBM] * n,
        scratch_shapes=[pltpu.SemaphoreType.DMA((n,)), pltpu.SemaphoreType.DMA((n,))],
    )(*arrs)


def _to_padded(wt):
    z = jnp.zeros((IN_PAD - IN_DIM, wt.shape[1]), wt.dtype)
    return jnp.concatenate([wt[768:2304], wt[0:512], wt[2304:2816], wt[2824:3848], wt[3848:4872],
                            wt[512:640], wt[640:768], wt[2816:2824], z], axis=0)


def _from_padded(gt):
    return jnp.concatenate([gt[C_Q:C_Q + AQ], gt[C_K:C_K + AKV], gt[C_V:C_V + AKV], gt[C_DQKV:C_DQKV + 3 * DN],
                            gt[C_DZ:C_DZ + DN], gt[C_BD:C_BD + 2 * DN_HEADS], gt[C_GA:C_GA + D_MODEL],
                            gt[C_GD:C_GD + D_MODEL]], axis=0)


def _lane_vec(a):
    return jnp.zeros((1, LANES), F32).at[0, DN_HEADS:2 * DN_HEADS].set(a)


_ROW_SHARDED = ("w_in", "w_out", "ffn_w_down")
_FFN = ("ffn_w_up", "ffn_w_down")
_LATE_MIXER = ("w_attn_branch", "w_dn_branch", "w_out")


def _pieces(k, a):
    if a.ndim == 3:
        return a
    if k in _ROW_SHARDED:
        return a.reshape(N_CHIPS, a.shape[0] // N_CHIPS, a.shape[1]).astype(BF16)
    return jnp.transpose(a.reshape(a.shape[0], N_CHIPS, a.shape[1] // N_CHIPS), (1, 0, 2)).astype(BF16)


def _assemble(k, a):
    if k in _ROW_SHARDED:
        return a.reshape(-1, a.shape[2])
    return jnp.transpose(a, (1, 0, 2)).reshape(a.shape[1], -1)


def _device_step(x2, tgt2, mod, p, bsz, shards=None):
    d = D_MODEL
    on_mesh = shards is not None
    p = dict(p)
    sh1, sc1, g1, sh2, sc2, g2 = [mod[:, i * d:(i + 1) * d].reshape(bsz, 1, d) for i in range(N_MOD)]
    alog_v, dt_v = _lane_vec(p["dn_a_log"]), _lane_vec(p["dn_dt_bias"])
    sinks = p["attn_sinks"].reshape(1, AQ_HEADS)
    u1 = _pre_fwd(x2, p["norm_mix_pre"], sc1, sh1, "pre1_fwd")
    if on_mesh:
        proj, got = _mm(u1, p["w_in"], "nt", F32, "mm_proj", _gather_plug(shards["late_mixer"]))
        p.update({k: _assemble(k, a) for k, a in zip(_LATE_MIXER, got)})
    else:
        proj = _mm(u1, p["w_in"], "nt", F32, "mm_proj")
    bias = _bias_build(p["rel_bias"])
    y_attn = _attn_fwd(proj, bias, sinks, bsz)
    qkvn = _dn_prep_fwd(proj, p["dn_conv_w"], bsz)
    bg = _bg_fwd(proj, alog_v, dt_v, bsz)
    nc = x2.shape[0] // bsz // DN_CHUNK
    g_rows = jnp.transpose(bg[:, DN_HEADS:2 * DN_HEADS].reshape(bsz, nc, DN_CHUNK, DN_HEADS), (0, 3, 1, 2))
    o, states, got = _dn_chunk_fwd(qkvn, bg, g_rows, bsz, _gather_plug(shards["ffn"][:1]) if on_mesh else None)
    for k, a in zip(_FFN[:1], got):
        p[k] = _assemble(k, a)
    y_dn = _dn_out_fwd(o, proj, p["dn_norm_w"])
    ya = _mm(y_attn, p["w_attn_branch"], "nn", F32, "mm_ya")
    yd = _mm(y_dn, p["w_dn_branch"], "nn", F32, "mm_yd")
    merged = _merge_fwd(proj, ya, yd)
    y1 = _mm(merged, p["w_out"], "nn", F32, "mm_y1")
    h1, u2 = _post_pre_fwd(x2, y1, p["norm_mix_post"], g1, p["norm_ffn_pre"], sc2, sh2, "post1_pre2_fwd")
    if on_mesh:
        up, got = _mm(u2, p["ffn_w_up"], "nn", F32, "mm_up", _gather_plug(shards["ffn"][1:]))
        p["ffn_w_down"] = _assemble("ffn_w_down", got[0])
    else:
        up = _mm(u2, p["ffn_w_up"], "nn", F32, "mm_up")
    act = _ffn_act_fwd(up, p["ffn_conv_w"], bsz)
    y2 = _mm(act, p["ffn_w_down"], "nn", F32, "mm_y2")
    dh2, dy2, g_ffn_post, dg2, sq = _post_loss_bwd(h1, y2, p["norm_ffn_post"], g2, tgt2, "post2_loss_bwd")
    g = {}
    g["norm_ffn_post"] = g_ffn_post
    dact = _mm(dy2, p["ffn_w_down"], "nt", F32, "mm_dact")
    g["ffn_w_down"] = _mm(act, dy2, "tn", BF16, "mm_dwdown")
    dupg, dupv, dcwg, dcwv, got_down = _ffn_act_bwd(
        up, p["ffn_conv_w"], dact, bsz, _exchange_plug([_pieces("ffn_w_down", g["ffn_w_down"])]) if on_mesh else None)
    g["ffn_conv_w"] = jnp.concatenate([dcwg, dcwv], axis=1)
    dup = jnp.concatenate([dupg, dupv], axis=1)
    g["ffn_w_up"] = _mm(u2, dup, "tn", BF16, "mm_dwup", split=N_CHIPS)
    du2 = _mm(dup, p["ffn_w_up"], "nt", F32, "mm_du2")
    dh1, dy1, g["norm_ffn_pre"], dsc2, dsh2, g["norm_mix_post"], dg1 = _pre_post_bwd(
        h1, p["norm_ffn_pre"], sc2, sh2, du2, dh2, y1, p["norm_mix_post"], g1, "pre2_post1_bwd")
    dmerged = _mm(dy1, p["w_out"], "nt", F32, "mm_dmerged")
    g["w_out"] = _mm(merged, dy1, "tn", BF16, "mm_dwout")
    dga, dgd, dya, dyd = _merge_bwd(proj, ya, yd, dmerged)
    dy_attn = _mm(dya, p["w_attn_branch"], "nt", BF16, "mm_dyattn")
    g["w_attn_branch"] = _mm(y_attn, dya, "tn", BF16, "mm_dwab", split=N_CHIPS)
    dy_dn = _mm(dyd, p["w_dn_branch"], "nt", F32, "mm_dydn")
    g["w_dn_branch"] = _mm(y_dn, dyd, "tn", BF16, "mm_dwdb", split=N_CHIPS)
    do, dz, g["dn_norm_w"] = _dn_out_bwd(o, proj, p["dn_norm_w"], dy_dn)
    def plug_for(names):
        return _exchange_plug([_pieces(k, g[k]) for k in names]) if on_mesh else None

    early = ("w_out", "w_attn_branch", "w_dn_branch")
    dqkvn, dbg4, got_up = _dn_chunk_bwd(qkvn, bg, g_rows, states, do, bsz, plug_for(_FFN[:1]))
    got_ffn = list(got_up) + list(got_down)
    d_dqkv, g["dn_conv_w"] = _dn_prep_bwd(proj, p["dn_conv_w"], dqkvn, bsz)
    dbd, g["dn_a_log"], g["dn_dt_bias"] = _bg_bwd(proj, alog_v, dt_v, dbg4, bsz)
    dq, dk, dv, dbias, g["attn_sinks"], got_early = _attn_bwd(proj, bias, sinks, dy_attn, bsz, plug_for(early))
    g["rel_bias"] = _bias_grad(dbias)
    dproj = jnp.concatenate([d_dqkv, dq, dz, dga, dgd, dk, dv, dbd], axis=1)
    g["w_in"] = _from_padded(_mm(dproj, u1, "tn", BF16, "mm_dwin"))
    if on_mesh:
        du1, got_in = _mm(dproj, p["w_in"], "nn", F32, "mm_du1", plug_for(("w_in",)))
        g.update(zip(_FFN + early + ("w_in",), list(got_ffn) + list(got_early) + list(got_in)))
    else:
        du1 = _mm(dproj, p["w_in"], "nn", F32, "mm_du1")
    dx, g["norm_mix_pre"], dsc1, dsh1 = _pre_bwd(x2, p["norm_mix_pre"], sc1, sh1, du1, dh1, "pre1_bwd")
    dmod = jnp.concatenate([dsh1, dsc1, dg1, dsh2, dsc2, dg2], axis=-1).reshape(bsz, N_MOD * d)
    return sq, dx, dmod, g


_SMALL = (("norm_mix_pre", D_MODEL), ("norm_mix_post", D_MODEL), ("norm_ffn_pre", D_MODEL), ("norm_ffn_post", D_MODEL),
          ("dn_norm_w", DN_HD), ("dn_a_log", LANES), ("dn_dt_bias", LANES), ("attn_sinks", AQ_HEADS * LANES),
          ("rel_bias", AQ_HEADS * LANES), ("dn_conv_w", DN_CONV * 3 * DN), ("ffn_conv_w", FFN_CONV * 2 * D_FF))


def _pack_rows(parts, rows):
    flat = jnp.concatenate([a.reshape(-1) for a in parts])
    return jnp.concatenate([flat, jnp.zeros((rows * LANES - flat.shape[0],), F32)]).reshape(rows, LANES)


def _pad128(a):
    flat = a.reshape(-1)
    n = -(-flat.shape[0] // LANES) * LANES
    return jnp.concatenate([flat, jnp.zeros((n - flat.shape[0],), F32)]) if n != flat.shape[0] else flat


_W_NAMES = ("ada_w", "ada_b", "norm_mix_pre", "norm_mix_post", "norm_ffn_pre", "norm_ffn_post", "w_in", "dn_conv_w", "dn_a_log",
            "dn_dt_bias", "dn_norm_w", "attn_sinks", "rel_bias", "w_attn_branch", "w_dn_branch", "w_out", "ffn_w_up", "ffn_conv_w",
            "ffn_w_down")
_BIG = ("w_in", "w_attn_branch", "w_dn_branch", "w_out", "ffn_w_up", "ffn_w_down")


def kernel(x, c, *rest):
    nw = len(_W_NAMES)
    w = dict(zip(_W_NAMES, rest[:nw]))
    loss_target = rest[nw]
    m = dict(zip(_W_NAMES, rest[nw + 1:2 * nw + 1]))
    v = dict(zip(_W_NAMES, rest[2 * nw + 1:3 * nw + 1]))
    ix, iy, ic = lax.axis_index("x"), lax.axis_index("y"), lax.axis_index("c")
    chip, dev = 2 * ix + iy, 4 * ix + 2 * iy + ic
    bsz, s, d = x.shape
    t = bsz * s
    n_dev = 8

    front_rows = 64
    front = _pack_rows([c, w["dn_conv_w"], w["ffn_conv_w"]], front_rows)
    front_all = _allgather8(front, "ag_front").reshape(n_dev, front_rows * LANES)
    n_c, n_dc, n_fc = bsz * d, DN_CONV * 3 * DN // N_CHIPS, FFN_CONV * 2 * D_FF // N_CHIPS
    c_all = front_all[:, :n_c].reshape(n_dev * bsz, d)
    per_chip = front_all[0::2]
    dn_conv_full = jnp.transpose(per_chip[:, n_c:n_c + n_dc].reshape(N_CHIPS, DN_CONV, -1), (1, 0, 2)).reshape(DN_CONV, 3 * DN)
    ffn_conv_full = jnp.transpose(per_chip[:, n_c + n_dc:n_c + n_dc + n_fc].reshape(N_CHIPS, FFN_CONV, -1), (1, 0, 2)).reshape(FFN_CONV, 2 * D_FF)

    mod_cols = N_MOD * d // N_CHIPS
    ada_b_loc = lax.dynamic_slice(w["ada_b"], (0, chip * mod_cols), (1, mod_cols))
    mod_part = _mod_fwd(c_all, w["ada_w"][0], ada_b_loc)
    mod_all = _allgather8(mod_part, "ag_mod").reshape(n_dev, n_dev * bsz, mod_cols)[0::2]
    mod = jnp.transpose(lax.dynamic_slice(mod_all, (0, dev * bsz, 0), (N_CHIPS, bsz, mod_cols)), (1, 0, 2)).reshape(bsz, N_MOD * d)

    (w_in_all,) = _comm_call(_gather_plug([jnp.swapaxes(w["w_in"][0], 0, 1).astype(BF16)]), "gather_w_in")
    p = {"w_in": _to_padded(_assemble("w_in", w_in_all))}
    shards = {"late_mixer": [w[k][0].astype(BF16) for k in _LATE_MIXER], "ffn": [w[k][0].astype(BF16) for k in _FFN]}
    for k in ("norm_mix_pre", "norm_mix_post", "norm_ffn_pre", "norm_ffn_post", "dn_norm_w", "attn_sinks"):
        p[k] = w[k]
    p["dn_a_log"], p["dn_dt_bias"], p["rel_bias"] = w["dn_a_log"][0], w["dn_dt_bias"][0], w["rel_bias"]
    p["dn_conv_w"], p["ffn_conv_w"] = dn_conv_full, ffn_conv_full

    sq, dx, dmod, g = _device_step(x.reshape(t, d), loss_target.reshape(t, d), mod, p, bsz, shards)
    loss = lax.psum(0.5 * jnp.sum(sq), ("x", "y", "c"))

    g["dn_a_log"], g["dn_dt_bias"] = g["dn_a_log"].reshape(-1), g["dn_dt_bias"].reshape(-1)
    small_rows = 328
    small = _pack_rows([dmod] + [g[k] for k, _ in _SMALL], small_rows)
    small_all = _allgather8(small, "ag_small").reshape(n_dev, small_rows, LANES)
    n_dm = bsz * N_MOD * d
    dmod_all = small_all.reshape(n_dev, -1)[:, :n_dm].reshape(n_dev * bsz, N_MOD * d)
    tot = _sum_lead(small_all, "sum_small").reshape(-1)
    gs, off = {}, n_dm
    for k, n in _SMALL:
        gs[k] = tot[off:off + n]
        off += n
    grad = {}
    grad["ada_w"], grad["ada_b"] = _ada_grad(c_all, lax.dynamic_slice(dmod_all, (0, chip * mod_cols), (n_dev * bsz, mod_cols)), dmod_all)
    for k in ("norm_mix_pre", "norm_mix_post", "norm_ffn_pre", "norm_ffn_post", "dn_norm_w"):
        grad[k] = gs[k]
    grad["dn_a_log"] = gs["dn_a_log"][DN_HEADS:2 * DN_HEADS]
    grad["dn_dt_bias"] = gs["dn_dt_bias"][DN_HEADS:2 * DN_HEADS]
    grad["attn_sinks"] = gs["attn_sinks"].reshape(AQ_HEADS, LANES)[:, 0]
    grad["rel_bias"] = gs["rel_bias"].reshape(AQ_HEADS, LANES)[:, :REL_BUCKETS].T
    grad["dn_conv_w"] = lax.dynamic_slice(gs["dn_conv_w"].reshape(DN_CONV, 3 * DN), (0, chip * (3 * DN // N_CHIPS)), (DN_CONV, 3 * DN // N_CHIPS))
    grad["ffn_conv_w"] = lax.dynamic_slice(gs["ffn_conv_w"].reshape(FFN_CONV, 2 * D_FF), (0, chip * (2 * D_FF // N_CHIPS)), (FFN_CONV, 2 * D_FF // N_CHIPS))

    mine = [_sum_lead(g[k], "sum_" + k) for k in _BIG]
    theirs = _sibling_exchange(mine, "exchange_cores")

    out = {}
    for k, a, b in zip(_BIG, mine, theirs):
        if k == "w_in":
            tr = lambda z: jnp.swapaxes(z, 0, 1)
            out[k] = [tr(r) for r in _adamw(tr(w[k][0]), tr(m[k][0]), tr(v[k][0]), [a, b], "adamw_" + k)]
        else:
            out[k] = _adamw(w[k][0], m[k][0], v[k][0], [a, b], "adamw_" + k)
    out["ada_w"] = _adamw(w["ada_w"][0], m["ada_w"][0], v["ada_w"][0], [grad["ada_w"]], "adamw_ada_w")
    small_names = [k for k in _W_NAMES if k not in _BIG and k != "ada_w"]
    res_small = _adamw_small([w[k] for k in small_names], [m[k] for k in small_names], [v[k] for k in small_names],
                             [grad[k].reshape(w[k].shape) for k in small_names], "adamw_small")
    out.update(zip(small_names, res_small))
    for k in _BIG + ("ada_w",):
        out[k] = [r.reshape(w[k].shape) for r in out[k]]
    grads, deltas, new_m, new_v = ([out[k][i] for k in _W_NAMES] for i in range(4))
    return (loss, dx.reshape(bsz, s, d), *grads, *deltas, *new_m, *new_v)
```

```python
import functools
import math

import numpy as np
import jax
import jax.numpy as jnp
from jax import lax
from jax.experimental import pallas as pl
from jax.experimental.pallas import tpu as pltpu

F32 = jnp.float32
BF16 = jnp.bfloat16
MESH = pl.DeviceIdType.MESH

D_MODEL = 1024
N_MOD = 6
AQ_HEADS, AKV_HEADS, A_HD, WINDOW = 8, 2, 64, 128
REL_BUCKETS, REL_MAX_DIST = 32, 128
DN_HEADS, DN_HD, DN_CONV, DN_CHUNK = 4, 128, 4, 64
D_FF, FFN_CONV = 2816, 3
RMS_EPS, L2_EPS, NEG_INF = 1e-6, 1e-6, -1e30
AQ, AKV, DN = AQ_HEADS * A_HD, AKV_HEADS * A_HD, DN_HEADS * DN_HD
IN_DIM = AQ + 2 * AKV + 3 * DN + DN + 2 * DN_HEADS + 2 * D_MODEL
C_DQKV, C_Q, C_DZ, C_GA, C_GD, C_K, C_V, C_BD = 0, 1536, 2048, 2560, 3584, 4608, 4736, 4864
IN_PAD = 4992
LANES = 128
N_CHIPS = 4

ADAM_LR, ADAM_B1, ADAM_B2, ADAM_EPS, ADAM_WD, ADAM_STEP = 0.001, 0.9, 0.999, 1e-08, 0.01, 10


def _pick(n, cap):
    best = None
    for t in range(LANES, cap + 1, LANES):
        if n % t == 0:
            best = t
    return best if best is not None else n


def _vspec(shape, index_map):
    return pl.BlockSpec(shape, index_map)


MM_VMEM_BUDGET = 40 * 2 ** 20
GRID_STEP_S = 0.35e-6
HBM_BYTES_PER_S = 3.0e12
MXU_FLOPS_PER_S = 9.0e14
MXU_DIM = 256


def _mm_tiles(m, n, k, mode, in_bytes, out_bytes, split=1):
    best = None
    for tm in [t for t in range(LANES, m + 1, LANES) if m % t == 0]:
        for tn in [t for t in range(LANES, n // split + 1, LANES) if (n // split) % t == 0]:
            a, b, o = k * tm * in_bytes, k * tn * in_bytes, tm * tn * out_bytes
            if 2 * (a + b + o) + (a if mode == "tn" else 0) > MM_VMEM_BUDGET:
                continue
            hbm_s = (m * k * in_bytes + (m // tm) * n * k * in_bytes + m * n * out_bytes) / HBM_BYTES_PER_S
            mxu_s = 2 * m * n * k / (MXU_FLOPS_PER_S * min(1.0, tm / MXU_DIM) * min(1.0, tn / MXU_DIM))
            cost = (m // tm) * (n // tn) * GRID_STEP_S + max(hbm_s, mxu_s)
            if best is None or cost < best[0]:
                best = (cost, tm, tn)
    return best[1], best[2]


def _mm(a, b, mode, out_dtype, name, plug=None, split=1):
    if mode == "nn":
        (m, k), n = a.shape, b.shape[1]
        dims = (((1,), (0,)), ((), ()))
    elif mode == "nt":
        (m, k), n = a.shape, b.shape[0]
        dims = (((1,), (1,)), ((), ()))
    else:
        (k, m), n = a.shape, b.shape[1]
        dims = (((0,), (0,)), ((), ()))
    tm, tn = _mm_tiles(m, n, k, mode, a.dtype.itemsize, jnp.dtype(out_dtype).itemsize, split)
    if mode == "tn":
        a_spec = _vspec((k, tm), lambda i, j: (0, i))
    else:
        a_spec = _vspec((tm, k), lambda i, j: (i, 0))
    if mode == "nt":
        b_spec = _vspec((tn, k), lambda i, j: (j, 0))
    else:
        b_spec = _vspec((k, tn), lambda i, j: (0, j))

    def body(a_ref, b_ref, o_ref):
        o_ref[...] = lax.dot_general(a_ref[...].astype(BF16), b_ref[...].astype(BF16), dims,
                                     preferred_element_type=F32).astype(out_dtype).reshape(o_ref.shape)

    grid = (m // tm, n // tn)
    if split == 1:
        out_spec, out_shape = _vspec((tm, tn), lambda i, j: (i, j)), (m, n)
    else:
        per = n // split // tn
        out_spec, out_shape = _vspec((1, tm, tn), lambda i, j: (j // per, i, j % per)), (split, m, n // split)
    (out,), extra = _plugged_call(body, plug, _grid_ends(grid), (a, b), name=name, grid=grid, in_specs=[a_spec, b_spec],
                                  out_specs=[out_spec], out_shape=[jax.ShapeDtypeStruct(out_shape, out_dtype)])
    return out if plug is None else (out, extra)


def _rms(x, w):
    return (x * lax.rsqrt(jnp.mean(x * x, axis=-1, keepdims=True) + RMS_EPS)) * w


def _pre_f(x, w, sc, sh):
    return _rms(x, w) * (1.0 + sc) + sh


def _post_f(y, w, g):
    return g * _rms(y, w)


def _tok_grid(t, bsz, ts):
    nt = t // bsz // ts
    return nt, (bsz, nt)


def _pre_fwd(x, w, sc, sh, name, ts=512):
    t, d = x.shape
    bsz = sc.shape[0]
    nt, grid = _tok_grid(t, bsz, ts)
    row = _vspec((ts, d), lambda b, i: (b * nt + i, 0))
    vec = _vspec((1, d), lambda b, i: (0, 0))
    bvec = _vspec((1, 1, d), lambda b, i: (b, 0, 0))

    def body(x_ref, w_ref, sc_ref, sh_ref, u_ref):
        u_ref[...] = _pre_f(x_ref[...], w_ref[...], sc_ref[0], sh_ref[0]).astype(BF16)

    return pl.pallas_call(body, name=name, grid=grid, in_specs=[row, vec, bvec, bvec], out_specs=row,
                          out_shape=jax.ShapeDtypeStruct((t, d), BF16))(x, w, sc, sh)


def _pre_bwd(x, w, sc, sh, du, dres, name, ts=512):
    t, d = x.shape
    bsz = sc.shape[0]
    nt, grid = _tok_grid(t, bsz, ts)
    row = _vspec((ts, d), lambda b, i: (b * nt + i, 0))
    vec = _vspec((1, d), lambda b, i: (0, 0))
    bvec = _vspec((1, 1, d), lambda b, i: (b, 0, 0))

    def body(x_ref, w_ref, sc_ref, sh_ref, du_ref, dres_ref, dx_ref, dw_ref, dsc_ref, dsh_ref):
        b, i = pl.program_id(0), pl.program_id(1)
        _, vjp = jax.vjp(_pre_f, x_ref[...], w_ref[...], sc_ref[0], sh_ref[0])
        dx, dw, dsc, dsh = vjp(du_ref[...])
        dx_ref[...] = dres_ref[...] + dx

        @pl.when((b == 0) & (i == 0))
        def _():
            dw_ref[...] = jnp.zeros_like(dw_ref)

        @pl.when(i == 0)
        def _():
            dsc_ref[...] = jnp.zeros_like(dsc_ref)
            dsh_ref[...] = jnp.zeros_like(dsh_ref)

        dw_ref[...] += dw
        dsc_ref[0] += dsc
        dsh_ref[0] += dsh

    return pl.pallas_call(
        body, name=name, grid=grid, in_specs=[row, vec, bvec, bvec, row, row], out_specs=[row, vec, bvec, bvec],
        out_shape=[jax.ShapeDtypeStruct((t, d), F32), jax.ShapeDtypeStruct((1, d), F32),
                   jax.ShapeDtypeStruct((bsz, 1, d), F32), jax.ShapeDtypeStruct((bsz, 1, d), F32)],
    )(x, w, sc, sh, du, dres)


def _accumulate(ref, val, first):
    @pl.when(first)
    def _():
        ref[...] = jnp.zeros_like(ref)

    ref[...] += val.reshape(ref.shape)


def _post_pre_fwd(res, y, w_post, g, w_pre, sc, sh, name, ts=512):
    t, d = y.shape
    bsz = g.shape[0]
    nt, grid = _tok_grid(t, bsz, ts)
    row = _vspec((ts, d), lambda b, i: (b * nt + i, 0))
    vec = _vspec((1, d), lambda b, i: (0, 0))
    bvec = _vspec((1, 1, d), lambda b, i: (b, 0, 0))

    def body(res_ref, y_ref, wp_ref, g_ref, w_ref, sc_ref, sh_ref, h_ref, u_ref):
        h = res_ref[...] + _post_f(y_ref[...], wp_ref[...], g_ref[0])
        h_ref[...] = h
        u_ref[...] = _pre_f(h, w_ref[...], sc_ref[0], sh_ref[0]).astype(BF16)

    return pl.pallas_call(body, name=name, grid=grid, in_specs=[row, row, vec, bvec, vec, bvec, bvec], out_specs=[row, row],
                          out_shape=[jax.ShapeDtypeStruct((t, d), F32), jax.ShapeDtypeStruct((t, d), BF16)],
                          )(res, y, w_post, g, w_pre, sc, sh)


def _post_loss_bwd(res, y, w, g, tgt, name, ts=512):
    t, d = y.shape
    bsz = g.shape[0]
    nt, grid = _tok_grid(t, bsz, ts)
    row = _vspec((ts, d), lambda b, i: (b * nt + i, 0))
    vec = _vspec((1, d), lambda b, i: (0, 0))
    bvec = _vspec((1, 1, d), lambda b, i: (b, 0, 0))

    def body(res_ref, y_ref, w_ref, g_ref, tgt_ref, dh_ref, dy_ref, dw_ref, dg_ref, sq_ref):
        b, i = pl.program_id(0), pl.program_id(1)
        part, vjp = jax.vjp(_post_f, y_ref[...], w_ref[...], g_ref[0])
        e = res_ref[...] + part - tgt_ref[...]
        dh = e * (1.0 / d)
        dh_ref[...] = dh
        dy, dw, dg = vjp(dh)
        dy_ref[...] = dy.astype(BF16)
        _accumulate(dw_ref, dw, (b == 0) & (i == 0))
        _accumulate(dg_ref, dg, i == 0)
        _accumulate(sq_ref, jnp.sum(e * e, axis=0, keepdims=True) * (1.0 / d), (b == 0) & (i == 0))

    return pl.pallas_call(
        body, name=name, grid=grid, in_specs=[row, row, vec, bvec, row], out_specs=[row, row, vec, bvec, vec],
        out_shape=[jax.ShapeDtypeStruct((t, d), F32), jax.ShapeDtypeStruct((t, d), BF16), jax.ShapeDtypeStruct((1, d), F32),
                   jax.ShapeDtypeStruct((bsz, 1, d), F32), jax.ShapeDtypeStruct((1, d), F32)],
    )(res, y, w, g, tgt)


def _pre_post_bwd(x, w, sc, sh, du, dres, y, w_post, g, name, ts=512):
    t, d = x.shape
    bsz = sc.shape[0]
    nt, grid = _tok_grid(t, bsz, ts)
    row = _vspec((ts, d), lambda b, i: (b * nt + i, 0))
    vec = _vspec((1, d), lambda b, i: (0, 0))
    bvec = _vspec((1, 1, d), lambda b, i: (b, 0, 0))

    def body(x_ref, w_ref, sc_ref, sh_ref, du_ref, dres_ref, y_ref, wp_ref, g_ref,
             dx_ref, dy_ref, dw_ref, dsc_ref, dsh_ref, dwp_ref, dg_ref):
        b, i = pl.program_id(0), pl.program_id(1)
        _, vjp = jax.vjp(_pre_f, x_ref[...], w_ref[...], sc_ref[0], sh_ref[0])
        dx, dw, dsc, dsh = vjp(du_ref[...])
        dx = dres_ref[...] + dx
        dx_ref[...] = dx
        _, vjp_post = jax.vjp(_post_f, y_ref[...], wp_ref[...], g_ref[0])
        dy, dwp, dg = vjp_post(dx)
        dy_ref[...] = dy.astype(BF16)
        first = (b == 0) & (i == 0)
        _accumulate(dw_ref, dw, first)
        _accumulate(dwp_ref, dwp, first)
        _accumulate(dsc_ref, dsc, i == 0)
        _accumulate(dsh_ref, dsh, i == 0)
        _accumulate(dg_ref, dg, i == 0)

    v1, vb = jax.ShapeDtypeStruct((1, d), F32), jax.ShapeDtypeStruct((bsz, 1, d), F32)
    return pl.pallas_call(
        body, name=name, grid=grid, in_specs=[row, vec, bvec, bvec, row, row, row, vec, bvec],
        out_specs=[row, row, vec, bvec, bvec, vec, bvec],
        out_shape=[jax.ShapeDtypeStruct((t, d), F32), jax.ShapeDtypeStruct((t, d), BF16), v1, vb, vb, v1, vb],
    )(x, w, sc, sh, du, dres, y, w_post, g)


def _merge_f(ga, gd, ya, yd):
    return jax.nn.sigmoid(ga) * ya + jax.nn.sigmoid(gd) * yd


_MW = 512


def _merge_fwd(proj, ya, yd, ts=512):
    t, d = ya.shape
    blk = _vspec((ts, _MW), lambda i, j: (i, j))
    ga = _vspec((ts, _MW), lambda i, j: (i, C_GA // _MW + j))
    gd = _vspec((ts, _MW), lambda i, j: (i, C_GD // _MW + j))

    def body(ga_ref, gd_ref, ya_ref, yd_ref, o_ref):
        o_ref[...] = _merge_f(ga_ref[...], gd_ref[...], ya_ref[...], yd_ref[...]).astype(BF16)

    return pl.pallas_call(body, name="merge_fwd", grid=(t // ts, d // _MW), in_specs=[ga, gd, blk, blk], out_specs=blk,
                          out_shape=jax.ShapeDtypeStruct((t, d), BF16))(proj, proj, ya, yd)


def _merge_bwd(proj, ya, yd, dm, ts=512):
    t, d = ya.shape
    blk = _vspec((ts, _MW), lambda i, j: (i, j))
    ga = _vspec((ts, _MW), lambda i, j: (i, C_GA // _MW + j))
    gd = _vspec((ts, _MW), lambda i, j: (i, C_GD // _MW + j))

    def body(ga_ref, gd_ref, ya_ref, yd_ref, dm_ref, dga_ref, dgd_ref, dya_ref, dyd_ref):
        _, vjp = jax.vjp(_merge_f, ga_ref[...], gd_ref[...], ya_ref[...], yd_ref[...])
        dga, dgd, dya, dyd = vjp(dm_ref[...])
        dga_ref[...] = dga.astype(BF16)
        dgd_ref[...] = dgd.astype(BF16)
        dya_ref[...] = dya.astype(BF16)
        dyd_ref[...] = dyd.astype(BF16)

    o = jax.ShapeDtypeStruct((t, d), BF16)
    return pl.pallas_call(body, name="merge_bwd", grid=(t // ts, d // _MW), in_specs=[ga, gd, blk, blk, blk],
                          out_specs=[blk] * 4, out_shape=[o] * 4)(proj, proj, ya, yd, dm)


def _shift_down(x, s):
    if s == 0:
        return x
    r = lax.broadcasted_iota(jnp.int32, x.shape, 0)
    return jnp.where(r >= s, pltpu.roll(x, s, 0), 0.0)


def _shift_up(x, s):
    if s == 0:
        return x
    n = x.shape[0]
    r = lax.broadcasted_iota(jnp.int32, x.shape, 0)
    return jnp.where(r < n - s, pltpu.roll(x, n - s, 0), 0.0)


def _conv_fwd(x, w, k):
    out = None
    for j in range(k):
        term = w[j:j + 1, :] * _shift_down(x, k - 1 - j)
        out = term if out is None else out + term
    return out


def _conv_bwd(x, w, dc, k):
    dx = None
    dws = []
    for j in range(k):
        up = _shift_up(dc, k - 1 - j)
        term = w[j:j + 1, :] * up
        dx = term if dx is None else dx + term
        dws.append(jnp.sum(up * x, axis=0, keepdims=True))
    return dx, jnp.concatenate(dws, axis=0)


def _geglu_f(gate, val):
    return jax.nn.gelu(gate, approximate=True) * val


_FW = 256


def _ffn_act_fwd(up, conv_w, bsz):
    t = up.shape[0]
    s = t // bsz
    nj = D_FF // _FW
    xg = _vspec((s, _FW), lambda b, j: (b, j))
    xv = _vspec((s, _FW), lambda b, j: (b, nj + j))
    wg = _vspec((FFN_CONV, _FW), lambda b, j: (0, j))
    wv = _vspec((FFN_CONV, _FW), lambda b, j: (0, nj + j))

    def body(xg_ref, xv_ref, wg_ref, wv_ref, o_ref):
        gate = _conv_fwd(xg_ref[...], wg_ref[...], FFN_CONV)
        val = _conv_fwd(xv_ref[...], wv_ref[...], FFN_CONV)
        o_ref[...] = _geglu_f(gate, val).astype(BF16)

    return pl.pallas_call(body, name="ffn_act_fwd", grid=(bsz, nj), in_specs=[xg, xv, wg, wv],
                          out_specs=_vspec((s, _FW), lambda b, j: (b, j)),
                          out_shape=jax.ShapeDtypeStruct((t, D_FF), BF16))(up, up, conv_w, conv_w)


def _ffn_act_bwd(up, conv_w, dact, bsz, plug=None):
    t = up.shape[0]
    s = t // bsz
    nj = D_FF // _FW
    xg = _vspec((s, _FW), lambda j, b: (b, j))
    xv = _vspec((s, _FW), lambda j, b: (b, nj + j))
    wg = _vspec((FFN_CONV, _FW), lambda j, b: (0, j))
    wv = _vspec((FFN_CONV, _FW), lambda j, b: (0, nj + j))
    da = _vspec((s, _FW), lambda j, b: (b, j))
    dwo = _vspec((FFN_CONV, _FW), lambda j, b: (0, j))

    def body(xg_ref, xv_ref, wg_ref, wv_ref, da_ref, dxg_ref, dxv_ref, dwg_ref, dwv_ref):
        b = pl.program_id(1)
        xg_, xv_, wg_, wv_ = xg_ref[...], xv_ref[...], wg_ref[...], wv_ref[...]
        gate = _conv_fwd(xg_, wg_, FFN_CONV)
        val = _conv_fwd(xv_, wv_, FFN_CONV)
        _, vjp = jax.vjp(_geglu_f, gate, val)
        dgate, dval = vjp(da_ref[...])
        dxg, dwg = _conv_bwd(xg_, wg_, dgate, FFN_CONV)
        dxv, dwv = _conv_bwd(xv_, wv_, dval, FFN_CONV)
        dxg_ref[...] = dxg.astype(BF16)
        dxv_ref[...] = dxv.astype(BF16)

        @pl.when(b == 0)
        def _():
            dwg_ref[...] = jnp.zeros_like(dwg_ref)
            dwv_ref[...] = jnp.zeros_like(dwv_ref)

        dwg_ref[...] += dwg
        dwv_ref[...] += dwv

    outs, extra = _plugged_call(
        body, plug, _grid_ends((nj, bsz)), (up, up, conv_w, conv_w, dact), name="ffn_act_bwd", grid=(nj, bsz),
        in_specs=[xg, xv, wg, wv, da], out_specs=[da, da, dwo, dwo],
        out_shape=[jax.ShapeDtypeStruct((t, D_FF), BF16), jax.ShapeDtypeStruct((t, D_FF), BF16),
                   jax.ShapeDtypeStruct((FFN_CONV, D_FF), F32), jax.ShapeDtypeStruct((FFN_CONV, D_FF), F32)])
    return (*outs, extra)


def _bucket_table():
    qi = np.arange(WINDOW)[:, None]
    kj = np.arange(2 * WINDOW)[None, :]
    dist = WINDOW + qi - kj
    dc = np.maximum(dist, 0)
    max_exact = REL_BUCKETS // 2
    scaled = np.log(np.maximum(dc, 1).astype(np.float32) / np.float32(max_exact)) / np.float32(math.log(REL_MAX_DIST / max_exact))
    large = max_exact + (scaled.astype(np.float32) * np.float32(REL_BUCKETS - max_exact)).astype(np.int32)
    large = np.minimum(large, REL_BUCKETS - 1)
    bucket = np.where(dc < max_exact, dc, large).astype(np.int32)
    in_band = ((dist >= 0) & (dist < WINDOW)).astype(np.int32)
    return bucket, in_band


def _bias_build(rel_bias):
    bucket, _ = _bucket_table()

    def body(rb_ref, idx_ref, o_ref):
        h = pl.program_id(0)
        idx = idx_ref[...]
        acc = jnp.zeros(idx.shape, F32)
        for r in range(REL_BUCKETS):
            acc = jnp.where(idx == r, rb_ref[r, h], acc)
        o_ref[0] = acc

    return pl.pallas_call(
        body, name="bias_build", grid=(AQ_HEADS,),
        in_specs=[pl.BlockSpec(memory_space=pltpu.SMEM), _vspec((WINDOW, 2 * WINDOW), lambda h: (0, 0))],
        out_specs=_vspec((1, WINDOW, 2 * WINDOW), lambda h: (h, 0, 0)),
        out_shape=jax.ShapeDtypeStruct((AQ_HEADS, WINDOW, 2 * WINDOW), F32),
    )(rel_bias, jnp.asarray(bucket))


def _bias_grad(dbias):
    bucket, _ = _bucket_table()

    def body(db_ref, idx_ref, o_ref):
        idx = idx_ref[...]
        db = db_ref[0]
        lane = lax.broadcasted_iota(jnp.int32, (1, LANES), 1)
        acc = jnp.zeros((1, LANES), F32)
        for r in range(REL_BUCKETS):
            s = jnp.sum(jnp.sum(jnp.where(idx == r, db, 0.0), axis=1, keepdims=True), axis=0, keepdims=True)
            acc = jnp.where(lane == r, s, acc)
        o_ref[0] = acc

    return pl.pallas_call(
        body, name="bias_grad", grid=(AQ_HEADS,),
        in_specs=[_vspec((1, WINDOW, 2 * WINDOW), lambda h: (h, 0, 0)), _vspec((WINDOW, 2 * WINDOW), lambda h: (0, 0))],
        out_specs=_vspec((1, 1, LANES), lambda h: (h, 0, 0)),
        out_shape=jax.ShapeDtypeStruct((AQ_HEADS, 1, LANES), F32),
    )(dbias, jnp.asarray(bucket))


def _attn_mask(n):
    qi = lax.broadcasted_iota(jnp.int32, (WINDOW, 2 * WINDOW), 0)
    kj = lax.broadcasted_iota(jnp.int32, (WINDOW, 2 * WINDOW), 1)
    dist = WINDOW + qi - kj
    band = (dist >= 0) & (dist < WINDOW)
    return band & ((kj >= WINDOW) | (n > 0))


def _attn_probs(qk, bias, sink, mask):
    s = jnp.where(mask, qk * (A_HD ** -0.5) + bias, NEG_INF)
    m = jnp.maximum(jnp.max(s, axis=-1, keepdims=True), sink)
    p = jnp.exp(s - m)
    es = jnp.exp(sink - m)
    inv = 1.0 / (jnp.sum(p, axis=-1, keepdims=True) + es)
    return p * inv, es * inv


def _attn_fwd(proj, bias, sinks, bsz):
    t = proj.shape[0]
    s = t // bsz
    nb = s // WINDOW
    grp = AQ_HEADS // AKV_HEADS

    def body(q_ref, k_ref, v_ref, bias_ref, sink_ref, y_ref, kp_ref, vp_ref):
        kp_ref[0:WINDOW, :] = jnp.zeros((WINDOW, LANES), BF16)
        vp_ref[0:WINDOW, :] = jnp.zeros((WINDOW, LANES), BF16)
        kp_ref[WINDOW:, :] = k_ref[...].astype(BF16)
        vp_ref[WINDOW:, :] = v_ref[...].astype(BF16)

        def blk(n, carry):
            r0 = pl.multiple_of(n * WINDOW, WINDOW)
            mask = _attn_mask(n)
            kband = kp_ref[pl.ds(r0, 2 * WINDOW), :]
            vband = vp_ref[pl.ds(r0, 2 * WINDOW), :]
            qb = q_ref[pl.ds(r0, WINDOW), :].astype(BF16)
            heads = range(AQ_HEADS)
            hsl = lambda h: slice(h * A_HD, (h + 1) * A_HD)
            kbs = [kband[:, hsl(kv)] for kv in range(AKV_HEADS)]
            vbs = [vband[:, hsl(kv)] for kv in range(AKV_HEADS)]
            qks = [lax.dot_general(qb[:, hsl(h)], kbs[h // grp], _NT, preferred_element_type=F32) for h in heads]
            probs = [_attn_probs(qks[h], bias_ref[h], sink_ref[0, h], mask)[0] for h in heads]
            outs = [jnp.dot(probs[h].astype(BF16), vbs[h // grp], preferred_element_type=F32) for h in heads]
            y_ref[pl.ds(r0, WINDOW), :] = jnp.concatenate(outs, axis=1).astype(BF16)
            return carry

        lax.fori_loop(0, nb, blk, 0)

    return pl.pallas_call(
        body, name="attn_fwd", grid=(bsz,),
        in_specs=[_vspec((s, AQ), lambda b: (b, C_Q // AQ)), _vspec((s, AKV), lambda b: (b, C_K // AKV)),
                  _vspec((s, AKV), lambda b: (b, C_V // AKV)),
                  _vspec((AQ_HEADS, WINDOW, 2 * WINDOW), lambda b: (0, 0, 0)), pl.BlockSpec(memory_space=pltpu.SMEM)],
        out_specs=_vspec((s, AQ), lambda b: (b, 0)), out_shape=jax.ShapeDtypeStruct((t, AQ), BF16),
        scratch_shapes=[pltpu.VMEM((s + WINDOW, LANES), BF16), pltpu.VMEM((s + WINDOW, LANES), BF16)],
    )(proj, proj, proj, bias, sinks)


def _attn_bwd(proj, bias, sinks, dy, bsz, plug=None):
    t = proj.shape[0]
    s = t // bsz
    nb = s // WINDOW
    grp = AQ_HEADS // AKV_HEADS
    scale = A_HD ** -0.5

    def body(q_ref, k_ref, v_ref, bias_ref, sink_ref, dy_ref, dq_ref, dk_ref, dv_ref, dbias_ref, dsink_ref,
             kp_ref, vp_ref, dkp_ref, dvp_ref):
        b = pl.program_id(0)
        kp_ref[0:WINDOW, :] = jnp.zeros((WINDOW, LANES), BF16)
        vp_ref[0:WINDOW, :] = jnp.zeros((WINDOW, LANES), BF16)
        kp_ref[WINDOW:, :] = k_ref[...].astype(BF16)
        vp_ref[WINDOW:, :] = v_ref[...].astype(BF16)
        dkp_ref[...] = jnp.zeros_like(dkp_ref)
        dvp_ref[...] = jnp.zeros_like(dvp_ref)

        @pl.when(b == 0)
        def _():
            dbias_ref[...] = jnp.zeros_like(dbias_ref)
            dsink_ref[...] = jnp.zeros_like(dsink_ref)

        def blk(n, carry):
            r0 = pl.multiple_of(n * WINDOW, WINDOW)
            mask = _attn_mask(n)
            kband = kp_ref[pl.ds(r0, 2 * WINDOW), :]
            vband = vp_ref[pl.ds(r0, 2 * WINDOW), :]
            qb = q_ref[pl.ds(r0, WINDOW), :].astype(BF16)
            dyb = dy_ref[pl.ds(r0, WINDOW), :].astype(BF16)
            heads = range(AQ_HEADS)
            hsl = lambda h: slice(h * A_HD, (h + 1) * A_HD)
            kbs = [kband[:, hsl(kv)] for kv in range(AKV_HEADS)]
            vbs = [vband[:, hsl(kv)] for kv in range(AKV_HEADS)]
            qhs = [qb[:, hsl(h)] for h in heads]
            dyhs = [dyb[:, hsl(h)] for h in heads]
            qks = [lax.dot_general(qhs[h], kbs[h // grp], _NT, preferred_element_type=F32) for h in heads]
            dprobs = [lax.dot_general(dyhs[h], vbs[h // grp], _NT, preferred_element_type=F32) for h in heads]
            pbs, dsbs = [], []
            for h in heads:
                probs, psink = _attn_probs(qks[h], bias_ref[h], sink_ref[0, h], mask)
                rowdot = jnp.sum(probs * dprobs[h], axis=-1, keepdims=True)
                ds = probs * (dprobs[h] - rowdot)
                dbias_ref[h] += ds
                dsink_ref[h] += jnp.sum(-psink * rowdot, axis=0, keepdims=True) + jnp.zeros((1, LANES), F32)
                pbs.append(probs.astype(BF16))
                dsbs.append(ds.astype(BF16))
            dvhs = [lax.dot_general(pbs[h], dyhs[h], _TN, preferred_element_type=F32) for h in heads]
            dqs = [jnp.dot(dsbs[h], kbs[h // grp], preferred_element_type=F32) * scale for h in heads]
            dkhs = [lax.dot_general(dsbs[h], qhs[h], _TN, preferred_element_type=F32) * scale for h in heads]
            dks = [sum(dkhs[kv * grp + 1:(kv + 1) * grp], dkhs[kv * grp]) for kv in range(AKV_HEADS)]
            dvs = [sum(dvhs[kv * grp + 1:(kv + 1) * grp], dvhs[kv * grp]) for kv in range(AKV_HEADS)]
            dq_ref[pl.ds(r0, WINDOW), :] = jnp.concatenate(dqs, axis=1).astype(BF16)
            dkp_ref[pl.ds(r0, 2 * WINDOW), :] += jnp.concatenate(dks, axis=1)
            dvp_ref[pl.ds(r0, 2 * WINDOW), :] += jnp.concatenate(dvs, axis=1)
            return carry

        lax.fori_loop(0, nb, blk, 0)
        dk_ref[...] = dkp_ref[WINDOW:, :].astype(BF16)
        dv_ref[...] = dvp_ref[WINDOW:, :].astype(BF16)

    kvs = jax.ShapeDtypeStruct((t, AKV), BF16)
    outs, extra = _plugged_call(
        body, plug, _grid_ends((bsz,)), (proj, proj, proj, bias, sinks, dy), name="attn_bwd", grid=(bsz,),
        in_specs=[_vspec((s, AQ), lambda b: (b, C_Q // AQ)), _vspec((s, AKV), lambda b: (b, C_K // AKV)),
                  _vspec((s, AKV), lambda b: (b, C_V // AKV)),
                  _vspec((AQ_HEADS, WINDOW, 2 * WINDOW), lambda b: (0, 0, 0)), pl.BlockSpec(memory_space=pltpu.SMEM),
                  _vspec((s, AQ), lambda b: (b, 0))],
        out_specs=[_vspec((s, AQ), lambda b: (b, 0)), _vspec((s, AKV), lambda b: (b, 0)), _vspec((s, AKV), lambda b: (b, 0)),
                   _vspec((AQ_HEADS, WINDOW, 2 * WINDOW), lambda b: (0, 0, 0)), _vspec((AQ_HEADS, 1, LANES), lambda b: (0, 0, 0))],
        out_shape=[jax.ShapeDtypeStruct((t, AQ), BF16), kvs, kvs,
                   jax.ShapeDtypeStruct((AQ_HEADS, WINDOW, 2 * WINDOW), F32), jax.ShapeDtypeStruct((AQ_HEADS, 1, LANES), F32)],
        scratch_shapes=[pltpu.VMEM((s + WINDOW, LANES), BF16), pltpu.VMEM((s + WINDOW, LANES), BF16),
                        pltpu.VMEM((s + WINDOW, LANES), F32), pltpu.VMEM((s + WINDOW, LANES), F32)])
    return (*outs, extra)


def _dn_act_f(c, is_qk):
    a = jax.nn.silu(c)
    outs = []
    for h in range(DN_HEADS):
        ah = a[:, h * DN_HD:(h + 1) * DN_HD]
        nh = ah * lax.rsqrt(jnp.sum(ah * ah, axis=-1, keepdims=True) + L2_EPS)
        outs.append(jnp.where(is_qk, nh, ah))
    return jnp.concatenate(outs, axis=1)


def _dn_prep_fwd(proj, conv_w, bsz):
    t = proj.shape[0]
    s = t // bsz
    blk = _vspec((s, DN), lambda b, j: (b, j))
    wsp = _vspec((DN_CONV, DN), lambda b, j: (0, j))

    def body(x_ref, w_ref, o_ref):
        j = pl.program_id(1)
        o_ref[...] = _dn_act_f(_conv_fwd(x_ref[...], w_ref[...], DN_CONV), j < 2)

    return pl.pallas_call(body, name="dn_prep_fwd", grid=(bsz, 3), in_specs=[blk, wsp], out_specs=blk,
                          out_shape=jax.ShapeDtypeStruct((t, 3 * DN), F32))(proj, conv_w)


def _dn_prep_bwd(proj, conv_w, dqkvn, bsz):
    t = proj.shape[0]
    s = t // bsz
    blk = _vspec((s, DN), lambda j, b: (b, j))
    wsp = _vspec((DN_CONV, DN), lambda j, b: (0, j))

    def body(x_ref, w_ref, d_ref, dx_ref, dw_ref):
        j, b = pl.program_id(0), pl.program_id(1)
        x, w = x_ref[...], w_ref[...]
        c = _conv_fwd(x, w, DN_CONV)
        _, vjp = jax.vjp(lambda cc: _dn_act_f(cc, j < 2), c)
        (dc,) = vjp(d_ref[0])
        dx, dw = _conv_bwd(x, w, dc, DN_CONV)
        dx_ref[...] = dx.astype(BF16)

        @pl.when(b == 0)
        def _():
            dw_ref[...] = jnp.zeros_like(dw_ref)

        dw_ref[...] += dw

    return pl.pallas_call(
        body, name="dn_prep_bwd", grid=(3, bsz),
        in_specs=[blk, wsp, _vspec((1, s, DN), lambda j, b: (j, b, 0))], out_specs=[blk, wsp],
        out_shape=[jax.ShapeDtypeStruct((t, 3 * DN), BF16), jax.ShapeDtypeStruct((DN_CONV, 3 * DN), F32)],
    )(proj, conv_w, dqkvn)


def _bg_f(x, alog, dt):
    lane = lax.broadcasted_iota(jnp.int32, x.shape, 1)
    beta = jax.nn.sigmoid(x)
    g = -jnp.exp(alog) * jax.nn.softplus(x + dt)
    return jnp.where(lane < DN_HEADS, beta, jnp.where(lane < 2 * DN_HEADS, g, 0.0))


def _bg_fwd(proj, alog, dt, bsz):
    t = proj.shape[0]
    s = t // bsz
    vec = _vspec((1, LANES), lambda b: (0, 0))

    def body(x_ref, a_ref, d_ref, o_ref):
        o_ref[...] = _bg_f(x_ref[...], a_ref[...], d_ref[...])

    return pl.pallas_call(body, name="bg_fwd", grid=(bsz,), in_specs=[_vspec((s, LANES), lambda b: (b, C_BD // LANES)), vec, vec],
                          out_specs=_vspec((s, LANES), lambda b: (b, 0)), out_shape=jax.ShapeDtypeStruct((t, LANES), F32))(proj, alog, dt)


def _bg_bwd(proj, alog, dt, dbg4, bsz):
    t = proj.shape[0]
    s = t // bsz
    vec = _vspec((1, LANES), lambda b: (0, 0))

    def body(x_ref, a_ref, d_ref, g4_ref, dx_ref, da_ref, dd_ref):
        b = pl.program_id(0)
        lane = lax.broadcasted_iota(jnp.int32, (s, LANES), 1)
        dbg = jnp.zeros((s, LANES), F32)
        for h in range(DN_HEADS):
            gh = g4_ref[:, h * DN_HD:(h + 1) * DN_HD]
            dbg = jnp.where(lane == h, gh[:, 0:1], dbg)
            dbg = jnp.where(lane == DN_HEADS + h, gh[:, 1:2], dbg)
        _, vjp = jax.vjp(_bg_f, x_ref[...], a_ref[...], d_ref[...])
        dx, da, dd = vjp(dbg)
        dx_ref[...] = dx.astype(BF16)

        @pl.when(b == 0)
        def _():
            da_ref[...] = jnp.zeros_like(da_ref)
            dd_ref[...] = jnp.zeros_like(dd_ref)

        da_ref[...] += da
        dd_ref[...] += dd

    return pl.pallas_call(
        body, name="bg_bwd", grid=(bsz,),
        in_specs=[_vspec((s, LANES), lambda b: (b, C_BD // LANES)), vec, vec, _vspec((s, DN), lambda b: (b, 0))],
        out_specs=[_vspec((s, LANES), lambda b: (b, 0)), vec, vec],
        out_shape=[jax.ShapeDtypeStruct((t, LANES), BF16), jax.ShapeDtypeStruct((1, LANES), F32), jax.ShapeDtypeStruct((1, LANES), F32)],
    )(proj, alog, dt, dbg4)


def _dn_out_f(o, z, w):
    outs = []
    for h in range(DN_HEADS):
        sl = slice(h * DN_HD, (h + 1) * DN_HD)
        outs.append(_rms(o[:, sl], w) * jax.nn.silu(z[:, sl]))
    return jnp.concatenate(outs, axis=1)


def _dn_out_fwd(o, proj, w, ts=512):
    t = o.shape[0]
    blk = _vspec((ts, DN), lambda i: (i, 0))
    zsp = _vspec((ts, DN), lambda i: (i, C_DZ // DN))
    vec = _vspec((1, DN_HD), lambda i: (0, 0))

    def body(o_ref, z_ref, w_ref, y_ref):
        y_ref[...] = _dn_out_f(o_ref[...], z_ref[...], w_ref[...]).astype(BF16)

    return pl.pallas_call(body, name="dn_out_fwd", grid=(t // ts,), in_specs=[blk, zsp, vec], out_specs=blk,
                          out_shape=jax.ShapeDtypeStruct((t, DN), BF16))(o, proj, w)


def _dn_out_bwd(o, proj, w, dy, ts=512):
    t = o.shape[0]
    blk = _vspec((ts, DN), lambda i: (i, 0))
    zsp = _vspec((ts, DN), lambda i: (i, C_DZ // DN))
    vec = _vspec((1, DN_HD), lambda i: (0, 0))

    def body(o_ref, z_ref, w_ref, dy_ref, do_ref, dz_ref, dw_ref):
        i = pl.program_id(0)
        _, vjp = jax.vjp(_dn_out_f, o_ref[...], z_ref[...], w_ref[...])
        do, dz, dw = vjp(dy_ref[...])
        do_ref[...] = do
        dz_ref[...] = dz.astype(BF16)

        @pl.when(i == 0)
        def _():
            dw_ref[...] = jnp.zeros_like(dw_ref)

        dw_ref[...] += dw

    return pl.pallas_call(
        body, name="dn_out_bwd", grid=(t // ts,), in_specs=[blk, zsp, vec, blk], out_specs=[blk, blk, vec],
        out_shape=[jax.ShapeDtypeStruct((t, DN), F32), jax.ShapeDtypeStruct((t, DN), BF16), jax.ShapeDtypeStruct((1, DN_HD), F32)],
    )(o, proj, w, dy)


_C = DN_CHUNK


def _dot(a, b, dims):
    return lax.dot_general(a.astype(BF16), b.astype(BF16), dims, preferred_element_type=F32)


def _split(a):
    hi = a.astype(BF16)
    return hi, (a - hi.astype(F32)).astype(BF16)


def _dot3(a, b, dims):
    (ah, al), (bh, bl) = (a if isinstance(a, tuple) else _split(a)), (b if isinstance(b, tuple) else _split(b))
    mm = lambda x, y: lax.dot_general(x, y, dims, preferred_element_type=F32)
    return mm(ah, bh) + (mm(ah, bl) + mm(al, bh))


_NN = (((1,), (0,)), ((), ()))
_NT = (((1,), (1,)), ((), ()))
_TN = (((0,), (0,)), ((), ()))


_SUB = 8


def _tri_inverses(ls, lts):
    ri8 = lax.broadcasted_iota(jnp.int32, (_SUB, _C), 0)
    ci8 = lax.broadcasted_iota(jnp.int32, (_SUB, _C), 1)
    nblk = _C // _SUB
    ts = []
    for lt in lts:
        blocks = [jnp.where(ci8 == ri8 + _SUB * b, 1.0, 0.0).astype(F32) for b in range(nblk)]
        for r in range(1, _SUB):
            for b in range(nblk):
                coef = lt[_SUB * b:_SUB * (b + 1), _SUB * b + r:_SUB * b + r + 1]
                row = jnp.sum(coef * blocks[b], axis=0, keepdims=True)
                blocks[b] = jnp.where(ri8 == r, blocks[b] - row, blocks[b])
        ts.append(jnp.concatenate(blocks, axis=0))
    ri = lax.broadcasted_iota(jnp.int32, (_C, _C), 0)
    ci = lax.broadcasted_iota(jnp.int32, (_C, _C), 1)
    s = _SUB
    while s < _C:
        shift = s.bit_length()
        quad = ((ri >> shift) == (ci >> shift)) & ((ri & s) != 0) & ((ci & s) == 0)
        offs = [jnp.where(quad, l, 0.0) for l in ls]
        tsp = [_split(t) for t in ts]
        left = [_dot3(tp, off, _NN) for tp, off in zip(tsp, offs)]
        ts = [t - _dot3(lo, tp, _NN) for t, lo, tp in zip(ts, left, tsp)]
        s *= 2
    return ts


_SEG = 512
_HEADS = tuple(range(DN_HEADS))


def _hsl(hh):
    return slice(hh * DN_HD, (hh + 1) * DN_HD)


def _chunk_specs(bsz, nseg, reverse):
    seg = (lambda i: nseg - 1 - i) if reverse else (lambda i: i)
    ncs = _SEG // _C
    col = lambda off: _vspec((bsz, _SEG, DN), lambda i: (0, seg(i), off))
    return (col, _vspec((bsz, _SEG, LANES), lambda i: (0, seg(i), 0)),
            _vspec((bsz, DN_HEADS, ncs, _C), lambda i: (0, 0, seg(i), 0)),
            _vspec((bsz, DN_HEADS, ncs, DN_HD, DN_HD), lambda i: (0, 0, seg(i), 0, 0)),
            _vspec((bsz, DN_HEADS, ncs, _C, _C), lambda i: (0, 0, seg(i), 0, 0)))


def _chunk_pre(q_ref, k_ref, v_ref, bg_ref, gr_ref, c, bb, hh):
    r0 = pl.multiple_of(c * _C, _C)
    ri = lax.broadcasted_iota(jnp.int32, (_C, _C), 0)
    ci = lax.broadcasted_iota(jnp.int32, (_C, _C), 1)
    q = q_ref[bb, pl.ds(r0, _C), _hsl(hh)] * (DN_HD ** -0.5)
    k = k_ref[bb, pl.ds(r0, _C), _hsl(hh)]
    v = v_ref[bb, pl.ds(r0, _C), _hsl(hh)]
    bgc = bg_ref[bb, pl.ds(r0, _C), :]
    beta = bgc[:, hh:hh + 1]
    g_col = bgc[:, DN_HEADS + hh:DN_HEADS + hh + 1]
    g_row = gr_ref[bb, hh, pl.ds(c, 1), :]
    gc_col = jnp.sum(jnp.where(ri >= ci, g_row, 0.0), axis=1, keepdims=True)
    gc_row = jnp.sum(jnp.where(ri <= ci, g_col, 0.0), axis=0, keepdims=True)
    gc_last = jnp.sum(g_col, axis=0, keepdims=True)
    diff = gc_col - gc_row
    decay = jnp.where(ri >= ci, jnp.exp(jnp.where(ri >= ci, diff, 0.0)), 0.0)
    diff_t = gc_row - gc_col
    decay_t = jnp.where(ri <= ci, jnp.exp(jnp.where(ri <= ci, diff_t, 0.0)), 0.0)
    eg = jnp.exp(gc_col)
    et = jnp.exp(gc_last - gc_col)
    gl = jnp.exp(gc_last)
    kb = k * beta
    vb = v * beta
    return dict(r0=r0, bb=bb, hh=hh, q=q, k=k, v=v, beta=beta, decay=decay, decay_t=decay_t, eg=eg, et=et, gl=gl, kb=kb, vb=vb,
                ri=ri, ci=ci)


def _chunk_solve(ms, tms=None):
    for m in ms:
        m["kk_t"] = _dot(m["k"], m["kb"], _NT)
        m["qk"] = _dot(m["q"], m["k"], _NT)
        m["kk"] = _dot(m["kb"], m["k"], _NT)
        if tms is not None:
            m["qk_t"] = _dot(m["k"], m["q"], _NT)
    if tms is None:
        tms = _tri_inverses([jnp.where(m["ri"] > m["ci"], m["kk"] * m["decay"], 0.0) for m in ms],
                            [jnp.where(m["ri"] < m["ci"], m["kk_t"] * m["decay_t"], 0.0) for m in ms])
    for m, tm in zip(ms, tms):
        m["tm_f32"] = tm
    for m in ms:
        rhs = jnp.concatenate([m["vb"], m["kb"] * m["eg"]], axis=1)
        m["tm"] = _split(m["tm_f32"])
        m["sol"] = _dot3(m["tm"], rhs, _NN)
        m["intra"] = jnp.where(m["ri"] >= m["ci"], m["qk"] * m["decay"], 0.0)


def _dn_chunk_fwd(qkvn, bg, g_rows, bsz, plug=None):
    t = qkvn.shape[0]
    s = t // bsz
    nc, nseg = s // _C, s // _SEG
    pairs = [(bb, hh) for bb in range(bsz) for hh in _HEADS]

    def body(q_ref, k_ref, v_ref, bg_ref, gr_ref, o_ref, st_ref, tm_ref, s_ref):
        @pl.when(pl.program_id(0) == 0)
        def _():
            s_ref[...] = jnp.zeros_like(s_ref)

        def chunk(c, carry):
            ms = [_chunk_pre(q_ref, k_ref, v_ref, bg_ref, gr_ref, c, bb, hh) for bb, hh in pairs]
            _chunk_solve(ms)
            sts = [s_ref[i] for i in range(len(pairs))]
            for m, st in zip(ms, sts):
                st_ref[m["bb"], m["hh"], c] = st
                tm_ref[m["bb"], m["hh"], c] = m["tm_f32"]
            ws = [_dot(m["sol"][:, DN_HD:], st, _NN) for m, st in zip(ms, sts)]
            qs = [_dot(m["q"] * m["eg"], st, _NN) for m, st in zip(ms, sts)]
            v_new = [m["sol"][:, :DN_HD] - a for m, a in zip(ms, ws)]
            iv = [_dot(m["intra"], vn, _NN) for m, vn in zip(ms, v_new)]
            upd = [_dot(m["k"] * m["et"], vn, _TN) for m, vn in zip(ms, v_new)]
            for i, (bb, hh) in enumerate(pairs):
                s_ref[i] = sts[i] * ms[i]["gl"] + upd[i]
                o_ref[bb, pl.ds(ms[i]["r0"], _C), _hsl(hh)] = qs[i] + iv[i]
            return carry

        lax.fori_loop(0, _SEG // _C, chunk, 0)

    col, bgs, grs, sts_spec, tms_spec = _chunk_specs(bsz, nseg, False)
    q3, bg3 = qkvn.reshape(bsz, s, 3 * DN), bg.reshape(bsz, s, LANES)
    (o, states, tms), extra = _plugged_call(
        body, plug, _grid_ends((nseg,)), (q3, q3, q3, bg3, g_rows), name="dn_chunk_fwd", grid=(nseg,),
        in_specs=[col(0), col(1), col(2), bgs, grs], out_specs=[col(0), sts_spec, tms_spec],
        out_shape=[jax.ShapeDtypeStruct((bsz, s, DN), F32), jax.ShapeDtypeStruct((bsz, DN_HEADS, nc, DN_HD, DN_HD), F32),
                   jax.ShapeDtypeStruct((bsz, DN_HEADS, nc, _C, _C), F32)],
        scratch_shapes=[pltpu.VMEM((bsz * DN_HEADS, DN_HD, DN_HD), F32)])
    return o.reshape(t, DN), (states, tms), extra


def _dn_chunk_bwd(qkvn, bg, g_rows, states, do, bsz, plug=None):
    t = qkvn.shape[0]
    s = t // bsz
    nc, nseg = s // _C, s // _SEG
    pairs = [(bb, hh) for bb in range(bsz) for hh in _HEADS]

    def body(q_ref, k_ref, v_ref, bg_ref, gr_ref, st_ref, tm_ref, do_ref, dqkv_ref, dbg_ref, ds_ref):
        @pl.when(pl.program_id(0) == 0)
        def _():
            ds_ref[...] = jnp.zeros_like(ds_ref)

        def chunk(cc, carry):
            c = _SEG // _C - 1 - cc
            ms = [_chunk_pre(q_ref, k_ref, v_ref, bg_ref, gr_ref, c, bb, hh) for bb, hh in pairs]
            _chunk_solve(ms, [tm_ref[bb, hh, c] for bb, hh in pairs])
            ri, ci = ms[0]["ri"], ms[0]["ci"]
            for i, m in enumerate(ms):
                m["st"] = st_ref[m["bb"], m["hh"], c]
                m["ds_out"] = ds_ref[i]
                m["do"] = do_ref[m["bb"], pl.ds(m["r0"], _C), _hsl(m["hh"])]
                m["w"] = m["sol"][:, DN_HD:]
            for m in ms:
                m["v_new"] = m["sol"][:, :DN_HD] - _dot(m["w"], m["st"], _NN)
            for m in ms:
                m["q_dec"], m["k_tail"] = m["q"] * m["eg"], m["k"] * m["et"]
                m["dk_tail"] = _dot(m["v_new"], m["ds_out"], _NT)
                m["dv_new"] = _dot(m["k_tail"], m["ds_out"], _NN) + _dot(m["intra"], m["do"], _TN)
                m["dq_dec"] = _dot(m["do"], m["st"], _NT)
                m["ds_in"] = m["ds_out"] * m["gl"] + _dot(m["q_dec"], m["do"], _TN)
                m["dintra"] = jnp.where(ri >= ci, _dot(m["do"], m["v_new"], _NT), 0.0)
                m["dintra_t"] = jnp.where(ri <= ci, _dot(m["v_new"], m["do"], _NT), 0.0)
            for m in ms:
                m["dw"] = -_dot(m["dv_new"], m["st"], _NT)
                m["ds_in"] = m["ds_in"] - _dot(m["w"], m["dv_new"], _TN)
            for m in ms:
                dsol = jnp.concatenate([m["dv_new"], m["dw"]], axis=1)
                m["drhs"] = _dot3(m["tm"], dsol, _TN)
            for m in ms:
                m["dl"] = jnp.where(ri > ci, -_dot(m["drhs"], m["sol"], _NT), 0.0)
                m["dl_t"] = jnp.where(ri < ci, -_dot(m["sol"], m["drhs"], _NT), 0.0)
            for m in ms:
                m["dkb2"] = _dot(m["dl"] * m["decay"], m["k"], _NN)
                m["dk"] = _dot(m["dl_t"] * m["decay_t"], m["kb"], _NN) + _dot(m["dintra_t"] * m["decay_t"], m["q"], _NN)
                m["dq"] = _dot(m["dintra"] * m["decay"], m["k"], _NN)
            for m in ms:
                _chunk_bwd_finish(m)
            for m in ms:
                ones_ge = jnp.where(ri <= ci, 1.0, 0.0).astype(BF16)
                gh, gl_ = _split(m["dgc"] + jnp.zeros((_C, LANES), F32))
                m["dg_b"] = jnp.dot(ones_ge, gh, preferred_element_type=F32) + jnp.dot(ones_ge, gl_, preferred_element_type=F32)
            lane = lax.broadcasted_iota(jnp.int32, (_C, LANES), 1)
            for i, m in enumerate(ms):
                bb, hh, rows = m["bb"], m["hh"], pl.ds(m["r0"], _C)
                dqkv_ref[0, bb, rows, _hsl(hh)] = m["dq"] * (DN_HD ** -0.5)
                dqkv_ref[1, bb, rows, _hsl(hh)] = m["dk"]
                dqkv_ref[2, bb, rows, _hsl(hh)] = m["dv"]
                dbg_ref[bb, rows, _hsl(hh)] = jnp.where(lane == 0, m["dbeta"], jnp.where(lane == 1, m["dg_b"], 0.0))
                ds_ref[i] = m["ds_in"]
            return carry

        lax.fori_loop(0, _SEG // _C, chunk, 0)

    col, bgs, grs, sts_spec, tms_spec = _chunk_specs(bsz, nseg, True)
    q3, bg3, do3 = qkvn.reshape(bsz, s, 3 * DN), bg.reshape(bsz, s, LANES), do.reshape(bsz, s, DN)
    (dqkv, dbg), extra = _plugged_call(
        body, plug, _grid_ends((nseg,)), (q3, q3, q3, bg3, g_rows, *states, do3), name="dn_chunk_bwd", grid=(nseg,),
        in_specs=[col(0), col(1), col(2), bgs, grs, sts_spec, tms_spec, col(0)],
        out_specs=[_vspec((3, bsz, _SEG, DN), lambda i: (0, 0, nseg - 1 - i, 0)), col(0)],
        out_shape=[jax.ShapeDtypeStruct((3, bsz, s, DN), F32), jax.ShapeDtypeStruct((bsz, s, DN), F32)],
        scratch_shapes=[pltpu.VMEM((bsz * DN_HEADS, DN_HD, DN_HD), F32)])
    return dqkv.reshape(3, t, DN), dbg.reshape(t, DN), extra


def _chunk_bwd_finish(m):
    q, k, v, beta, decay, decay_t = m["q"], m["k"], m["v"], m["beta"], m["decay"], m["decay_t"]
    eg, et, gl, kb, dl, dl_t, dintra, dintra_t = m["eg"], m["et"], m["gl"], m["kb"], m["dl"], m["dl_t"], m["dintra"], m["dintra_t"]
    dq_dec, dk_tail, dq, dk = m["dq_dec"], m["dk_tail"], m["dq"], m["dk"]
    dgl = jnp.sum(jnp.sum(m["ds_out"] * m["st"], axis=1, keepdims=True), axis=0, keepdims=True)
    dvb, dkbeg = m["drhs"][:, :DN_HD], m["drhs"][:, DN_HD:]
    dkb = dkbeg * eg + m["dkb2"]
    deg = jnp.sum(dkbeg * kb, axis=1, keepdims=True)
    em = (dl * m["kk"] + dintra * m["qk"]) * decay
    em_t = (dl_t * m["kk_t"] + dintra_t * m["qk_t"]) * decay_t
    dgc = jnp.sum(em, axis=1, keepdims=True) - jnp.sum(em_t, axis=1, keepdims=True)
    dq = dq + dq_dec * eg
    deg = deg + jnp.sum(dq_dec * q, axis=1, keepdims=True)
    dk = dk + dk_tail * et
    det = jnp.sum(dk_tail * k, axis=1, keepdims=True)
    dgc = dgc + deg * eg - det * et
    dgc_last = jnp.sum(det * et, axis=0, keepdims=True) + dgl * gl
    rcol = lax.broadcasted_iota(jnp.int32, (_C, 1), 0)
    m["dgc"] = dgc + jnp.where(rcol == _C - 1, dgc_last, 0.0)
    m["dq"] = dq
    m["dk"] = dk + dkb * beta
    m["dbeta"] = jnp.sum(dkb * k, axis=1, keepdims=True) + jnp.sum(dvb * v, axis=1, keepdims=True)
    m["dv"] = dvb * beta


def _mod_fwd(c_all, ada_w_loc, ada_b_loc):
    n, cols = c_all.shape[0], ada_w_loc.shape[1]

    def body(c_ref, w_ref, b_ref, o_ref):
        o_ref[...] = _dot(jax.nn.silu(c_ref[...]), w_ref[...], _NN) + b_ref[...]

    return pl.pallas_call(body, name="mod_fwd", out_shape=jax.ShapeDtypeStruct((n, cols), F32))(c_all, ada_w_loc, ada_b_loc)


def _ada_grad(c_all, dmod_loc, dmod_all):
    d, cols = c_all.shape[1], dmod_loc.shape[1]

    def body(c_ref, dl_ref, da_ref, gw_ref, gb_ref):
        gw_ref[...] = _dot(jax.nn.silu(c_ref[...]), dl_ref[...], _TN)
        gb_ref[...] = jnp.sum(da_ref[...], axis=0, keepdims=True)

    return pl.pallas_call(body, name="ada_grad", out_shape=[jax.ShapeDtypeStruct((d, cols), F32),
                                                           jax.ShapeDtypeStruct((1, dmod_all.shape[1]), F32)])(c_all, dmod_loc, dmod_all)


ELEMENTWISE_BLOCK_BYTES = 3 * 2 ** 19


def _row_tile(r, c=1024):
    fits = [tr for tr in range(16, r + 1, 16) if tr * c * 4 <= ELEMENTWISE_BLOCK_BYTES]
    if not fits:
        return r
    whole = [tr for tr in fits if r % tr == 0]
    return whole[-1] if whole else fits[-1]


def _adamw(w, m, v, grads, name):
    r, c = w.shape
    tr = _row_tile(r, c)
    blk = _vspec((tr, c), lambda i: (i, 0))
    n = len(grads)

    def body(*refs):
        w_ref, m_ref, v_ref = refs[:3]
        g_ref, d_ref, mo_ref, vo_ref = refs[3 + n:]
        g = refs[3][...]
        for p in refs[4:3 + n]:
            g = g + p[...]
        g_ref[...] = g
        d_ref[...], mo_ref[...], vo_ref[...] = _adamw_math(w_ref[...], m_ref[...], v_ref[...], g)

    o = jax.ShapeDtypeStruct((r, c), F32)
    return pl.pallas_call(body, name=name, grid=(pl.cdiv(r, tr),), in_specs=[blk] * (3 + n), out_specs=[blk] * 4,
                          out_shape=[o] * 4)(w, m, v, *grads)


def _adamw_math(w, m, v, g):
    m_new = ADAM_B1 * m + (1.0 - ADAM_B1) * g
    v_new = ADAM_B2 * v + (1.0 - ADAM_B2) * jnp.square(g)
    m_hat = m_new / (1.0 - ADAM_B1 ** ADAM_STEP)
    v_hat = v_new / (1.0 - ADAM_B2 ** ADAM_STEP)
    return -ADAM_LR * (m_hat / (jnp.sqrt(v_hat) + ADAM_EPS) + ADAM_WD * w), m_new, v_new


def _adamw_small(ws, ms, vs, gs, name):
    n = len(ws)

    def body(*refs):
        for i in range(n):
            w_ref, m_ref, v_ref, g_ref = (refs[j * n + i] for j in range(4))
            go_ref, d_ref, mo_ref, vo_ref = refs[4 * n + 4 * i:4 * n + 4 * i + 4]
            g = g_ref[...]
            go_ref[...] = g
            d_ref[...], mo_ref[...], vo_ref[...] = _adamw_math(w_ref[...], m_ref[...], v_ref[...], g)

    res = pl.pallas_call(body, name=name, out_shape=[jax.ShapeDtypeStruct(a.shape, F32) for a in ws for _ in range(4)])(
        *ws, *ms, *vs, *gs)
    return [res[4 * i:4 * i + 4] for i in range(n)]


def _sum_lead(x, name):
    p, r, c = x.shape
    tr = _row_tile(r, c)

    def body(x_ref, o_ref):
        acc = x_ref[0].astype(F32)
        for i in range(1, p):
            acc = acc + x_ref[i].astype(F32)
        o_ref[...] = acc

    return pl.pallas_call(body, name=name, grid=(pl.cdiv(r, tr),), in_specs=[_vspec((p, tr, c), lambda i: (0, i, 0))],
                          out_specs=_vspec((tr, c), lambda i: (i, 0)), out_shape=jax.ShapeDtypeStruct((r, c), F32))(x)


def _allgather8(x_shard, name):
    m_per, n = x_shard.shape

    def body(x_ref, out_ref, send_sems, recv_sems, local_sem):
        x, y, c = lax.axis_index("x"), lax.axis_index("y"), lax.axis_index("c")
        me, sibling = (x, y, c), (x, y, 1 - c)
        chips = [(1 - x, y), (x, 1 - y), (1 - x, 1 - y)]

        def rows(px, py, pc):
            return out_ref.at[pl.ds((4 * px + 2 * py + pc) * m_per, m_per), :]

        def copy(k, block, to, src=None):
            return pltpu.make_async_remote_copy(
                src_ref=rows(*block) if src is None else src, dst_ref=rows(*block), send_sem=send_sems.at[k],
                recv_sem=recv_sems.at[k], device_id=to, device_id_type=MESH)

        mine = pltpu.make_async_copy(x_ref, rows(*me), local_sem)
        mine.start()
        first = [copy(0, me, sibling, src=x_ref)]
        first += [copy(1 + j, me, (*chip, c), src=x_ref) for j, chip in enumerate(chips)]
        for cp in first:
            cp.start()
        passed = [copy(4 + j, (*chip, c), sibling) for j, chip in enumerate(chips)]
        for j, chip in enumerate(chips):
            copy(1 + j, (*chip, c), me).wait_recv()
            passed[j].start()
        copy(0, sibling, me).wait_recv()
        for j, chip in enumerate(chips):
            copy(4 + j, (*chip, 1 - c), me).wait_recv()
        for cp in first + passed:
            cp.wait_send()
        mine.wait()

    return pl.pallas_call(
        body, name=name, out_shape=jax.ShapeDtypeStruct((8 * m_per, n), x_shard.dtype),
        in_specs=[pl.BlockSpec(memory_space=pltpu.VMEM)], out_specs=pl.BlockSpec(memory_space=pltpu.VMEM),
        scratch_shapes=[pltpu.SemaphoreType.DMA((7,)), pltpu.SemaphoreType.DMA((7,)), pltpu.SemaphoreType.DMA],
    )(x_shard)


_HBM = pl.BlockSpec(memory_space=pltpu.HBM)


def _mesh_place():
    x, y, c = lax.axis_index("x"), lax.axis_index("y"), lax.axis_index("c")
    return x, y, c, 2 * x + y, [(1 - x, y), (x, 1 - y), (1 - x, 1 - y)]


def _gather_plug(shards):
    n = len(shards)

    def half(ref, c, lead=None):
        r, cols = ref.shape[-2] // 2, ref.shape[-1] // 2
        if r % 16 == 0:
            rows = pl.ds(pl.multiple_of(c * r, 16), r)
            return ref.at[rows, :] if lead is None else ref.at[lead, rows, :]
        lanes = pl.ds(pl.multiple_of(c * cols, LANES), cols)
        return ref.at[:, lanes] if lead is None else ref.at[lead, :, lanes]

    def copies(ins, outs, send, recv):
        x, y, c, me, chips = _mesh_place()
        ici, fwd, fwd_in = [], [], []
        for i in range(n):
            for j, (px, py) in enumerate(chips):
                q = 2 * px + py
                ici.append((pltpu.make_async_remote_copy(
                    src_ref=half(ins[i], c), dst_ref=half(outs[i], c, me), send_sem=send.at[6 * i + j], recv_sem=recv.at[6 * i + j],
                    device_id=(px, py, c), device_id_type=MESH),
                    pltpu.make_async_remote_copy(
                    src_ref=half(ins[i], c), dst_ref=half(outs[i], c, q), send_sem=send.at[6 * i + j], recv_sem=recv.at[6 * i + j],
                    device_id=(px, py, c), device_id_type=MESH)))
                fwd.append(pltpu.make_async_remote_copy(
                    src_ref=half(outs[i], c, q), dst_ref=half(outs[i], c, q), send_sem=send.at[6 * i + 3 + j],
                    recv_sem=recv.at[6 * i + 3 + j], device_id=(x, y, 1 - c), device_id_type=MESH))
                fwd_in.append(pltpu.make_async_remote_copy(
                    src_ref=half(outs[i], 1 - c, q), dst_ref=half(outs[i], 1 - c, q), send_sem=send.at[6 * i + 3 + j],
                    recv_sem=recv.at[6 * i + 3 + j], device_id=(x, y, 1 - c), device_id_type=MESH))
        return ici, fwd, fwd_in, me

    def start(ins, outs, send, recv, loc):
        ici, _, _, me = copies(ins, outs, send, recv)
        for i in range(n):
            pltpu.make_async_copy(ins[i], outs[i].at[me], loc.at[i]).start()
        for out_cp, _ in ici:
            out_cp.start()

    def finish(ins, outs, send, recv, loc):
        ici, fwd, fwd_in, me = copies(ins, outs, send, recv)
        for (_, in_cp), f in zip(ici, fwd):
            in_cp.wait_recv()
            f.start()
        for f in fwd_in:
            f.wait_recv()
        for (out_cp, _), f in zip(ici, fwd):
            out_cp.wait_send()
            f.wait_send()
        for i in range(n):
            pltpu.make_async_copy(ins[i], outs[i].at[me], loc.at[i]).wait()

    return dict(ins=list(shards), out_shape=[jax.ShapeDtypeStruct((N_CHIPS,) + a.shape, a.dtype) for a in shards],
                scratch=[pltpu.SemaphoreType.DMA((6 * n,)), pltpu.SemaphoreType.DMA((6 * n,)), pltpu.SemaphoreType.DMA((n,))],
                start=start, finish=finish)


def _exchange_plug(pieces):
    n = len(pieces)

    def copies(ins, outs, send, recv):
        x, y, c, me, chips = _mesh_place()
        out_cps, in_cps = [], []
        for i in range(n):
            for j, (px, py) in enumerate(chips):
                q = 2 * px + py
                out_cps.append(pltpu.make_async_remote_copy(src_ref=ins[i].at[q], dst_ref=outs[i].at[me], send_sem=send.at[3 * i + j],
                                                            recv_sem=recv.at[3 * i + j], device_id=(px, py, c), device_id_type=MESH))
                in_cps.append(pltpu.make_async_remote_copy(src_ref=ins[i].at[me], dst_ref=outs[i].at[q], send_sem=send.at[3 * i + j],
                                                           recv_sem=recv.at[3 * i + j], device_id=(px, py, c), device_id_type=MESH))
        return out_cps, in_cps, me

    def start(ins, outs, send, recv, loc):
        out_cps, _, me = copies(ins, outs, send, recv)
        for i in range(n):
            pltpu.make_async_copy(ins[i].at[me], outs[i].at[me], loc.at[i]).start()
        for cp in out_cps:
            cp.start()

    def finish(ins, outs, send, recv, loc):
        out_cps, in_cps, me = copies(ins, outs, send, recv)
        for cp in in_cps:
            cp.wait_recv()
        for cp in out_cps:
            cp.wait_send()
        for i in range(n):
            pltpu.make_async_copy(ins[i].at[me], outs[i].at[me], loc.at[i]).wait()

    return dict(ins=list(pieces), out_shape=[jax.ShapeDtypeStruct(a.shape, a.dtype) for a in pieces],
                scratch=[pltpu.SemaphoreType.DMA((3 * n,)), pltpu.SemaphoreType.DMA((3 * n,)), pltpu.SemaphoreType.DMA((n,))],
                start=start, finish=finish)


def _comm_call(plug, name):
    n_in, n_out = len(plug["ins"]), len(plug["out_shape"])

    def body(*refs):
        ins, outs, sems = refs[:n_in], refs[n_in:n_in + n_out], refs[n_in + n_out:]
        plug["start"](ins, outs, *sems)
        plug["finish"](ins, outs, *sems)

    return pl.pallas_call(body, name=name, out_shape=plug["out_shape"], in_specs=[_HBM] * n_in, out_specs=[_HBM] * n_out,
                          scratch_shapes=plug["scratch"])(*plug["ins"])


def _plugged_call(body, plug, first_last, args, *, name, grid, in_specs, out_specs, out_shape, scratch_shapes=()):
    in_specs, out_specs, out_shape, scratch_shapes = list(in_specs), list(out_specs), list(out_shape), list(scratch_shapes)
    if plug is None:
        return pl.pallas_call(body, name=name, grid=grid, in_specs=in_specs, out_specs=out_specs, out_shape=out_shape,
                              scratch_shapes=scratch_shapes)(*args), []
    n_in, n_out, n_sc = len(in_specs), len(out_specs), len(scratch_shapes)
    p_in, p_out = len(plug["ins"]), len(plug["out_shape"])

    def full(*refs):
        ins, refs = refs[:n_in], refs[n_in:]
        pins, refs = refs[:p_in], refs[p_in:]
        outs, refs = refs[:n_out], refs[n_out:]
        pouts, refs = refs[:p_out], refs[p_out:]
        scr, psems = refs[:n_sc], refs[n_sc:]
        first, last = first_last()

        @pl.when(first)
        def _():
            plug["start"](pins, pouts, *psems)

        body(*ins, *outs, *scr)

        @pl.when(last)
        def _():
            plug["finish"](pins, pouts, *psems)

    res = pl.pallas_call(full, name=name, grid=grid, in_specs=in_specs + [_HBM] * p_in, out_specs=out_specs + [_HBM] * p_out,
                         out_shape=out_shape + plug["out_shape"], scratch_shapes=scratch_shapes + plug["scratch"])(*args, *plug["ins"])
    return res[:n_out], res[n_out:]


def _grid_ends(grid):
    def ends():
        first = last = None
        for ax, n in enumerate(grid):
            i = pl.program_id(ax)
            first = (i == 0) if first is None else first & (i == 0)
            last = (i == n - 1) if last is None else last & (i == n - 1)
        return first, last
    return ends


def _sibling_exchange(arrs, name):
    n = len(arrs)

    def body(*refs):
        ins, outs = refs[:n], refs[n:2 * n]
        send_sems, recv_sems = refs[2 * n:]
        sibling = (lax.axis_index("x"), lax.axis_index("y"), 1 - lax.axis_index("c"))
        cps = [pltpu.make_async_remote_copy(src_ref=ins[i], dst_ref=outs[i], send_sem=send_sems.at[i], recv_sem=recv_sems.at[i],
                                            device_id=sibling, device_id_type=MESH) for i in range(n)]
        for cp in cps:
            cp.start()
        for cp in cps:
            cp.wait()

    return pl.pallas_call(
        body, name=name, out_shape=[jax.ShapeDtypeStruct(a.shape, a.dtype) for a in arrs], in_specs=[_HBM] * n, out_specs=[_HBM] * n,
        scratch_shapes=[pltpu.SemaphoreType.DMA((n,)), pltpu.SemaphoreType.DMA((n,))],
    )(*arrs)


def _to_padded(wt):
    z = jnp.zeros((IN_PAD - IN_DIM, wt.shape[1]), wt.dtype)
    return jnp.concatenate([wt[768:2304], wt[0:512], wt[2304:2816], wt[2824:3848], wt[3848:4872],
                            wt[512:640], wt[640:768], wt[2816:2824], z], axis=0)


def _from_padded(gt):
    return jnp.concatenate([gt[C_Q:C_Q + AQ], gt[C_K:C_K + AKV], gt[C_V:C_V + AKV], gt[C_DQKV:C_DQKV + 3 * DN],
                            gt[C_DZ:C_DZ + DN], gt[C_BD:C_BD + 2 * DN_HEADS], gt[C_GA:C_GA + D_MODEL],
                            gt[C_GD:C_GD + D_MODEL]], axis=0)


def _lane_vec(a):
    return jnp.zeros((1, LANES), F32).at[0, DN_HEADS:2 * DN_HEADS].set(a)


_ROW_SHARDED = ("w_in", "w_out", "ffn_w_down")
_FFN = ("ffn_w_up", "ffn_w_down")
_LATE_MIXER = ("w_attn_branch", "w_dn_branch", "w_out")


def _pieces(k, a):
    if a.ndim == 3:
        return a
    if k in _ROW_SHARDED:
        return a.reshape(N_CHIPS, a.shape[0] // N_CHIPS, a.shape[1]).astype(BF16)
    return jnp.transpose(a.reshape(a.shape[0], N_CHIPS, a.shape[1] // N_CHIPS), (1, 0, 2)).astype(BF16)


def _assemble(k, a):
    if k in _ROW_SHARDED:
        return a.reshape(-1, a.shape[2])
    return jnp.transpose(a, (1, 0, 2)).reshape(a.shape[1], -1)


def _device_step(x2, tgt2, mod, p, bsz, shards=None):
    d = D_MODEL
    on_mesh = shards is not None
    p = dict(p)
    sh1, sc1, g1, sh2, sc2, g2 = [mod[:, i * d:(i + 1) * d].reshape(bsz, 1, d) for i in range(N_MOD)]
    alog_v, dt_v = _lane_vec(p["dn_a_log"]), _lane_vec(p["dn_dt_bias"])
    sinks = p["attn_sinks"].reshape(1, AQ_HEADS)
    u1 = _pre_fwd(x2, p["norm_mix_pre"], sc1, sh1, "pre1_fwd")
    if on_mesh:
        proj, got = _mm(u1, p["w_in"], "nt", F32, "mm_proj", _gather_plug(shards["late_mixer"]))
        p.update({k: _assemble(k, a) for k, a in zip(_LATE_MIXER, got)})
    else:
        proj = _mm(u1, p["w_in"], "nt", F32, "mm_proj")
    bias = _bias_build(p["rel_bias"])
    y_attn = _attn_fwd(proj, bias, sinks, bsz)
    qkvn = _dn_prep_fwd(proj, p["dn_conv_w"], bsz)
    bg = _bg_fwd(proj, alog_v, dt_v, bsz)
    nc = x2.shape[0] // bsz // DN_CHUNK
    g_rows = jnp.transpose(bg[:, DN_HEADS:2 * DN_HEADS].reshape(bsz, nc, DN_CHUNK, DN_HEADS), (0, 3, 1, 2))
    o, states, got = _dn_chunk_fwd(qkvn, bg, g_rows, bsz, _gather_plug(shards["ffn"][:1]) if on_mesh else None)
    for k, a in zip(_FFN[:1], got):
        p[k] = _assemble(k, a)
    y_dn = _dn_out_fwd(o, proj, p["dn_norm_w"])
    ya = _mm(y_attn, p["w_attn_branch"], "nn", F32, "mm_ya")
    yd = _mm(y_dn, p["w_dn_branch"], "nn", F32, "mm_yd")
    merged = _merge_fwd(proj, ya, yd)
    y1 = _mm(merged, p["w_out"], "nn", F32, "mm_y1")
    h1, u2 = _post_pre_fwd(x2, y1, p["norm_mix_post"], g1, p["norm_ffn_pre"], sc2, sh2, "post1_pre2_fwd")
    if on_mesh:
        up, got = _mm(u2, p["ffn_w_up"], "nn", F32, "mm_up", _gather_plug(shards["ffn"][1:]))
        p["ffn_w_down"] = _assemble("ffn_w_down", got[0])
    else:
        up = _mm(u2, p["ffn_w_up"], "nn", F32, "mm_up")
    act = _ffn_act_fwd(up, p["ffn_conv_w"], bsz)
    y2 = _mm(act, p["ffn_w_down"], "nn", F32, "mm_y2")
    dh2, dy2, g_ffn_post, dg2, sq = _post_loss_bwd(h1, y2, p["norm_ffn_post"], g2, tgt2, "post2_loss_bwd")
    g = {}
    g["norm_ffn_post"] = g_ffn_post
    dact = _mm(dy2, p["ffn_w_down"], "nt", F32, "mm_dact")
    g["ffn_w_down"] = _mm(act, dy2, "tn", BF16, "mm_dwdown")
    dupg, dupv, dcwg, dcwv, got_down = _ffn_act_bwd(
        up, p["ffn_conv_w"], dact, bsz, _exchange_plug([_pieces("ffn_w_down", g["ffn_w_down"])]) if on_mesh else None)
    g["ffn_conv_w"] = jnp.concatenate([dcwg, dcwv], axis=1)
    dup = jnp.concatenate([dupg, dupv], axis=1)
    g["ffn_w_up"] = _mm(u2, dup, "tn", BF16, "mm_dwup", split=N_CHIPS)
    du2 = _mm(dup, p["ffn_w_up"], "nt", F32, "mm_du2")
    dh1, dy1, g["norm_ffn_pre"], dsc2, dsh2, g["norm_mix_post"], dg1 = _pre_post_bwd(
        h1, p["norm_ffn_pre"], sc2, sh2, du2, dh2, y1, p["norm_mix_post"], g1, "pre2_post1_bwd")
    dmerged = _mm(dy1, p["w_out"], "nt", F32, "mm_dmerged")
    g["w_out"] = _mm(merged, dy1, "tn", BF16, "mm_dwout")
    dga, dgd, dya, dyd = _merge_bwd(proj, ya, yd, dmerged)
    dy_attn = _mm(dya, p["w_attn_branch"], "nt", BF16, "mm_dyattn")
    g["w_attn_branch"] = _mm(y_attn, dya, "tn", BF16, "mm_dwab", split=N_CHIPS)
    dy_dn = _mm(dyd, p["w_dn_branch"], "nt", F32, "mm_dydn")
    g["w_dn_branch"] = _mm(y_dn, dyd, "tn", BF16, "mm_dwdb", split=N_CHIPS)
    do, dz, g["dn_norm_w"] = _dn_out_bwd(o, proj, p["dn_norm_w"], dy_dn)
    def plug_for(names):
        return _exchange_plug([_pieces(k, g[k]) for k in names]) if on_mesh else None

    early = ("w_out", "w_attn_branch", "w_dn_branch")
    dqkvn, dbg4, got_up = _dn_chunk_bwd(qkvn, bg, g_rows, states, do, bsz, plug_for(_FFN[:1]))
    got_ffn = list(got_up) + list(got_down)
    d_dqkv, g["dn_conv_w"] = _dn_prep_bwd(proj, p["dn_conv_w"], dqkvn, bsz)
    dbd, g["dn_a_log"], g["dn_dt_bias"] = _bg_bwd(proj, alog_v, dt_v, dbg4, bsz)
    dq, dk, dv, dbias, g["attn_sinks"], got_early = _attn_bwd(proj, bias, sinks, dy_attn, bsz, plug_for(early))
    g["rel_bias"] = _bias_grad(dbias)
    dproj = jnp.concatenate([d_dqkv, dq, dz, dga, dgd, dk, dv, dbd], axis=1)
    g["w_in"] = _from_padded(_mm(dproj, u1, "tn", BF16, "mm_dwin"))
    if on_mesh:
        du1, got_in = _mm(dproj, p["w_in"], "nn", F32, "mm_du1", plug_for(("w_in",)))
        g.update(zip(_FFN + early + ("w_in",), list(got_ffn) + list(got_early) + list(got_in)))
    else:
        du1 = _mm(dproj, p["w_in"], "nn", F32, "mm_du1")
    dx, g["norm_mix_pre"], dsc1, dsh1 = _pre_bwd(x2, p["norm_mix_pre"], sc1, sh1, du1, dh1, "pre1_bwd")
    dmod = jnp.concatenate([dsh1, dsc1, dg1, dsh2, dsc2, dg2], axis=-1).reshape(bsz, N_MOD * d)
    return sq, dx, dmod, g


_SMALL = (("norm_mix_pre", D_MODEL), ("norm_mix_post", D_MODEL), ("norm_ffn_pre", D_MODEL), ("norm_ffn_post", D_MODEL),
          ("dn_norm_w", DN_HD), ("dn_a_log", LANES), ("dn_dt_bias", LANES), ("attn_sinks", AQ_HEADS * LANES),
          ("rel_bias", AQ_HEADS * LANES), ("dn_conv_w", DN_CONV * 3 * DN), ("ffn_conv_w", FFN_CONV * 2 * D_FF))


def _pack_rows(parts, rows):
    flat = jnp.concatenate([a.reshape(-1) for a in parts])
    return jnp.concatenate([flat, jnp.zeros((rows * LANES - flat.shape[0],), F32)]).reshape(rows, LANES)


def _pad128(a):
    flat = a.reshape(-1)
    n = -(-flat.shape[0] // LANES) * LANES
    return jnp.concatenate([flat, jnp.zeros((n - flat.shape[0],), F32)]) if n != flat.shape[0] else flat


_W_NAMES = ("ada_w", "ada_b", "norm_mix_pre", "norm_mix_post", "norm_ffn_pre", "norm_ffn_post", "w_in", "dn_conv_w", "dn_a_log",
            "dn_dt_bias", "dn_norm_w", "attn_sinks", "rel_bias", "w_attn_branch", "w_dn_branch", "w_out", "ffn_w_up", "ffn_conv_w",
            "ffn_w_down")
_BIG = ("w_in", "w_attn_branch", "w_dn_branch", "w_out", "ffn_w_up", "ffn_w_down")


def kernel(x, c, *rest):
    nw = len(_W_NAMES)
    w = dict(zip(_W_NAMES, rest[:nw]))
    loss_target = rest[nw]
    m = dict(zip(_W_NAMES, rest[nw + 1:2 * nw + 1]))
    v = dict(zip(_W_NAMES, rest[2 * nw + 1:3 * nw + 1]))
    ix, iy, ic = lax.axis_index("x"), lax.axis_index("y"), lax.axis_index("c")
    chip, dev = 2 * ix + iy, 4 * ix + 2 * iy + ic
    bsz, s, d = x.shape
    t = bsz * s
    n_dev = 8

    front_rows = 64
    front = _pack_rows([c, w["dn_conv_w"], w["ffn_conv_w"]], front_rows)
    front_all = _allgather8(front, "ag_front").reshape(n_dev, front_rows * LANES)
    n_c, n_dc, n_fc = bsz * d, DN_CONV * 3 * DN // N_CHIPS, FFN_CONV * 2 * D_FF // N_CHIPS
    c_all = front_all[:, :n_c].reshape(n_dev * bsz, d)
    per_chip = front_all[0::2]
    dn_conv_full = jnp.transpose(per_chip[:, n_c:n_c + n_dc].reshape(N_CHIPS, DN_CONV, -1), (1, 0, 2)).reshape(DN_CONV, 3 * DN)
    ffn_conv_full = jnp.transpose(per_chip[:, n_c + n_dc:n_c + n_dc + n_fc].reshape(N_CHIPS, FFN_CONV, -1), (1, 0, 2)).reshape(FFN_CONV, 2 * D_FF)

    mod_cols = N_MOD * d // N_CHIPS
    ada_b_loc = lax.dynamic_slice(w["ada_b"], (0, chip * mod_cols), (1, mod_cols))
    mod_part = _mod_fwd(c_all, w["ada_w"][0], ada_b_loc)
    mod_all = _allgather8(mod_part, "ag_mod").reshape(n_dev, n_dev * bsz, mod_cols)[0::2]
    mod = jnp.transpose(lax.dynamic_slice(mod_all, (0, dev * bsz, 0), (N_CHIPS, bsz, mod_cols)), (1, 0, 2)).reshape(bsz, N_MOD * d)

    (w_in_all,) = _comm_call(_gather_plug([jnp.swapaxes(w["w_in"][0], 0, 1).astype(BF16)]), "gather_w_in")
    p = {"w_in": _to_padded(_assemble("w_in", w_in_all))}
    shards = {"late_mixer": [w[k][0].astype(BF16) for k in _LATE_MIXER], "ffn": [w[k][0].astype(BF16) for k in _FFN]}
    for k in ("norm_mix_pre", "norm_mix_post", "norm_ffn_pre", "norm_ffn_post", "dn_norm_w", "attn_sinks"):
        p[k] = w[k]
    p["dn_a_log"], p["dn_dt_bias"], p["rel_bias"] = w["dn_a_log"][0], w["dn_dt_bias"][0], w["rel_bias"]
    p["dn_conv_w"], p["ffn_conv_w"] = dn_conv_full, ffn_conv_full

    sq, dx, dmod, g = _device_step(x.reshape(t, d), loss_target.reshape(t, d), mod, p, bsz, shards)
    loss = lax.psum(0.5 * jnp.sum(sq), ("x", "y", "c"))

    g["dn_a_log"], g["dn_dt_bias"] = g["dn_a_log"].reshape(-1), g["dn_dt_bias"].reshape(-1)
    small_rows = 328
    small = _pack_rows([dmod] + [g[k] for k, _ in _SMALL], small_rows)
    small_all = _allgather8(small, "ag_small").reshape(n_dev, small_rows, LANES)
    n_dm = bsz * N_MOD * d
    dmod_all = small_all.reshape(n_dev, -1)[:, :n_dm].reshape(n_dev * bsz, N_MOD * d)
    tot = _sum_lead(small_all, "sum_small").reshape(-1)
    gs, off = {}, n_dm
    for k, n in _SMALL:
        gs[k] = tot[off:off + n]
        off += n
    grad = {}
    grad["ada_w"], grad["ada_b"] = _ada_grad(c_all, lax.dynamic_slice(dmod_all, (0, chip * mod_cols), (n_dev * bsz, mod_cols)), dmod_all)
    for k in ("norm_mix_pre", "norm_mix_post", "norm_ffn_pre", "norm_ffn_post", "dn_norm_w"):
        grad[k] = gs[k]
    grad["dn_a_log"] = gs["dn_a_log"][DN_HEADS:2 * DN_HEADS]
    grad["dn_dt_bias"] = gs["dn_dt_bias"][DN_HEADS:2 * DN_HEADS]
    grad["attn_sinks"] = gs["attn_sinks"].reshape(AQ_HEADS, LANES)[:, 0]
    grad["rel_bias"] = gs["rel_bias"].reshape(AQ_HEADS, LANES)[:, :REL_BUCKETS].T
    grad["dn_conv_w"] = lax.dynamic_slice(gs["dn_conv_w"].reshape(DN_CONV, 3 * DN), (0, chip * (3 * DN // N_CHIPS)), (DN_CONV, 3 * DN // N_CHIPS))
    grad["ffn_conv_w"] = lax.dynamic_slice(gs["ffn_conv_w"].reshape(FFN_CONV, 2 * D_FF), (0, chip * (2 * D_FF // N_CHIPS)), (FFN_CONV, 2 * D_FF // N_CHIPS))

    mine = [_sum_lead(g[k], "sum_" + k) for k in _BIG]
    theirs = _sibling_exchange(mine, "exchange_cores")

    out = {}
    for k, a, b in zip(_BIG, mine, theirs):
        if k == "w_in":
            tr = lambda z: jnp.swapaxes(z, 0, 1)
            out[k] = [tr(r) for r in _adamw(tr(w[k][0]), tr(m[k][0]), tr(v[k][0]), [a, b], "adamw_" + k)]
        else:
            out[k] = _adamw(w[k][0], m[k][0], v[k][0], [a, b], "adamw_" + k)
    out["ada_w"] = _adamw(w["ada_w"][0], m["ada_w"][0], v["ada_w"][0], [grad["ada_w"]], "adamw_ada_w")
    small_names = [k for k in _W_NAMES if k not in _BIG and k != "ada_w"]
    res_small = _adamw_small([w[k] for k in small_names], [m[k] for k in small_names], [v[k] for k in small_names],
                             [grad[k].reshape(w[k].shape) for k in small_names], "adamw_small")
    out.update(zip(small_names, res_small))
    for k in _BIG + ("ada_w",):
        out[k] = [r.reshape(w[k].shape) for r in out[k]]
    grads, deltas, new_m, new_v = ([out[k][i] for k in _W_NAMES] for i in range(4))
    return (loss, dx.reshape(bsz, s, d), *grads, *deltas, *new_m, *new_v)
```

```python
import functools
import math

import numpy as np
import jax
import jax.numpy as jnp
from jax import lax
from jax.experimental import pallas as pl
from jax.experimental.pallas import tpu as pltpu

F32 = jnp.float32
BF16 = jnp.bfloat16
MESH = pl.DeviceIdType.MESH

D_MODEL = 1024
N_MOD = 6
AQ_HEADS, AKV_HEADS, A_HD, WINDOW = 8, 2, 64, 128
REL_BUCKETS, REL_MAX_DIST = 32, 128
DN_HEADS, DN_HD, DN_CONV, DN_CHUNK = 4, 128, 4, 64
D_FF, FFN_CONV = 2816, 3
RMS_EPS, L2_EPS, NEG_INF = 1e-6, 1e-6, -1e30
AQ, AKV, DN = AQ_HEADS * A_HD, AKV_HEADS * A_HD, DN_HEADS * DN_HD
IN_DIM = AQ + 2 * AKV + 3 * DN + DN + 2 * DN_HEADS + 2 * D_MODEL
C_DQKV, C_Q, C_DZ, C_GA, C_GD, C_K, C_V, C_BD = 0, 1536, 2048, 2560, 3584, 4608, 4736, 4864
IN_PAD = 4992
LANES = 128
N_CHIPS = 4

ADAM_LR, ADAM_B1, ADAM_B2, ADAM_EPS, ADAM_WD, ADAM_STEP = 0.001, 0.9, 0.999, 1e-08, 0.01, 10


def _pick(n, cap):
    best = None
    for t in range(LANES, cap + 1, LANES):
        if n % t == 0:
            best = t
    return best if best is not None else n


def _vspec(shape, index_map):
    return pl.BlockSpec(shape, index_map)


MM_VMEM_BUDGET = 40 * 2 ** 20
GRID_STEP_S = 0.35e-6
HBM_BYTES_PER_S = 3.0e12
MXU_FLOPS_PER_S = 9.0e14
MXU_DIM = 256


def _mm_tiles(m, n, k, mode, in_bytes, out_bytes, split=1):
    best = None
    for tm in [t for t in range(LANES, m + 1, LANES) if m % t == 0]:
        for tn in [t for t in range(LANES, n // split + 1, LANES) if (n // split) % t == 0]:
            a, b, o = k * tm * in_bytes, k * tn * in_bytes, tm * tn * out_bytes
            if 2 * (a + b + o) + (a if mode == "tn" else 0) > MM_VMEM_BUDGET:
                continue
            hbm_s = (m * k * in_bytes + (m // tm) * n * k * in_bytes + m * n * out_bytes) / HBM_BYTES_PER_S
            mxu_s = 2 * m * n * k / (MXU_FLOPS_PER_S * min(1.0, tm / MXU_DIM) * min(1.0, tn / MXU_DIM))
            cost = (m // tm) * (n // tn) * GRID_STEP_S + max(hbm_s, mxu_s)
            if best is None or cost < best[0]:
                best = (cost, tm, tn)
    return best[1], best[2]


def _mm(a, b, mode, out_dtype, name, plug=None, split=1):
    if mode == "nn":
        (m, k), n = a.shape, b.shape[1]
        dims = (((1,), (0,)), ((), ()))
    elif mode == "nt":
        (m, k), n = a.shape, b.shape[0]
        dims = (((1,), (1,)), ((), ()))
    else:
        (k, m), n = a.shape, b.shape[1]
        dims = (((0,), (0,)), ((), ()))
    tm, tn = _mm_tiles(m, n, k, mode, a.dtype.itemsize, jnp.dtype(out_dtype).itemsize, split)
    if mode == "tn":
        a_spec = _vspec((k, tm), lambda i, j: (0, i))
    else:
        a_spec = _vspec((tm, k), lambda i, j: (i, 0))
    if mode == "nt":
        b_spec = _vspec((tn, k), lambda i, j: (j, 0))
    else:
        b_spec = _vspec((k, tn), lambda i, j: (0, j))

    def body(a_ref, b_ref, o_ref):
        o_ref[...] = lax.dot_general(a_ref[...].astype(BF16), b_ref[...].astype(BF16), dims,
                                     preferred_element_type=F32).astype(out_dtype).reshape(o_ref.shape)

    grid = (m // tm, n // tn)
    if split == 1:
        out_spec, out_shape = _vspec((tm, tn), lambda i, j: (i, j)), (m, n)
    else:
        per = n // split // tn
        out_spec, out_shape = _vspec((1, tm, tn), lambda i, j: (j // per, i, j % per)), (split, m, n // split)
    (out,), extra = _plugged_call(body, plug, _grid_ends(grid), (a, b), name=name, grid=grid, in_specs=[a_spec, b_spec],
                                  out_specs=[out_spec], out_shape=[jax.ShapeDtypeStruct(out_shape, out_dtype)])
    return out if plug is None else (out, extra)


def _rms(x, w):
    return (x * lax.rsqrt(jnp.mean(x * x, axis=-1, keepdims=True) + RMS_EPS)) * w


def _pre_f(x, w, sc, sh):
    return _rms(x, w) * (1.0 + sc) + sh


def _post_f(y, w, g):
    return g * _rms(y, w)


def _tok_grid(t, bsz, ts):
    nt = t // bsz // ts
    return nt, (bsz, nt)


def _pre_fwd(x, w, sc, sh, name, ts=512):
    t, d = x.shape
    bsz = sc.shape[0]
    nt, grid = _tok_grid(t, bsz, ts)
    row = _vspec((ts, d), lambda b, i: (b * nt + i, 0))
    vec = _vspec((1, d), lambda b, i: (0, 0))
    bvec = _vspec((1, 1, d), lambda b, i: (b, 0, 0))

    def body(x_ref, w_ref, sc_ref, sh_ref, u_ref):
        u_ref[...] = _pre_f(x_ref[...], w_ref[...], sc_ref[0], sh_ref[0]).astype(BF16)

    return pl.pallas_call(body, name=name, grid=grid, in_specs=[row, vec, bvec, bvec], out_specs=row,
                          out_shape=jax.ShapeDtypeStruct((t, d), BF16))(x, w, sc, sh)


def _pre_bwd(x, w, sc, sh, du, dres, name, ts=512):
    t, d = x.shape
    bsz = sc.shape[0]
    nt, grid = _tok_grid(t, bsz, ts)
    row = _vspec((ts, d), lambda b, i: (b * nt + i, 0))
    vec = _vspec((1, d), lambda b, i: (0, 0))
    bvec = _vspec((1, 1, d), lambda b, i: (b, 0, 0))

    def body(x_ref, w_ref, sc_ref, sh_ref, du_ref, dres_ref, dx_ref, dw_ref, dsc_ref, dsh_ref):
        b, i = pl.program_id(0), pl.program_id(1)
        _, vjp = jax.vjp(_pre_f, x_ref[...], w_ref[...], sc_ref[0], sh_ref[0])
        dx, dw, dsc, dsh = vjp(du_ref[...])
        dx_ref[...] = dres_ref[...] + dx

        @pl.when((b == 0) & (i == 0))
        def _():
            dw_ref[...] = jnp.zeros_like(dw_ref)

        @pl.when(i == 0)
        def _():
            dsc_ref[...] = jnp.zeros_like(dsc_ref)
            dsh_ref[...] = jnp.zeros_like(dsh_ref)

        dw_ref[...] += dw
        dsc_ref[0] += dsc
        dsh_ref[0] += dsh

    return pl.pallas_call(
        body, name=name, grid=grid, in_specs=[row, vec, bvec, bvec, row, row], out_specs=[row, vec, bvec, bvec],
        out_shape=[jax.ShapeDtypeStruct((t, d), F32), jax.ShapeDtypeStruct((1, d), F32),
                   jax.ShapeDtypeStruct((bsz, 1, d), F32), jax.ShapeDtypeStruct((bsz, 1, d), F32)],
    )(x, w, sc, sh, du, dres)


def _accumulate(ref, val, first):
    @pl.when(first)
    def _():
        ref[...] = jnp.zeros_like(ref)

    ref[...] += val.reshape(ref.shape)


def _post_pre_fwd(res, y, w_post, g, w_pre, sc, sh, name, ts=512):
    t, d = y.shape
    bsz = g.shape[0]
    nt, grid = _tok_grid(t, bsz, ts)
    row = _vspec((ts, d), lambda b, i: (b * nt + i, 0))
    vec = _vspec((1, d), lambda b, i: (0, 0))
    bvec = _vspec((1, 1, d), lambda b, i: (b, 0, 0))

    def body(res_ref, y_ref, wp_ref, g_ref, w_ref, sc_ref, sh_ref, h_ref, u_ref):
        h = res_ref[...] + _post_f(y_ref[...], wp_ref[...], g_ref[0])
        h_ref[...] = h
        u_ref[...] = _pre_f(h, w_ref[...], sc_ref[0], sh_ref[0]).astype(BF16)

    return pl.pallas_call(body, name=name, grid=grid, in_specs=[row, row, vec, bvec, vec, bvec, bvec], out_specs=[row, row],
                          out_shape=[jax.ShapeDtypeStruct((t, d), F32), jax.ShapeDtypeStruct((t, d), BF16)],
                          )(res, y, w_post, g, w_pre, sc, sh)


def _post_loss_bwd(res, y, w, g, tgt, name, ts=512):
    t, d = y.shape
    bsz = g.shape[0]
    nt, grid = _tok_grid(t, bsz, ts)
    row = _vspec((ts, d), lambda b, i: (b * nt + i, 0))
    vec = _vspec((1, d), lambda b, i: (0, 0))
    bvec = _vspec((1, 1, d), lambda b, i: (b, 0, 0))

    def body(res_ref, y_ref, w_ref, g_ref, tgt_ref, dh_ref, dy_ref, dw_ref, dg_ref, sq_ref):
        b, i = pl.program_id(0), pl.program_id(1)
        part, vjp = jax.vjp(_post_f, y_ref[...], w_ref[...], g_ref[0])
        e = res_ref[...] + part - tgt_ref[...]
        dh = e * (1.0 / d)
        dh_ref[...] = dh
        dy, dw, dg = vjp(dh)
        dy_ref[...] = dy.astype(BF16)
        _accumulate(dw_ref, dw, (b == 0) & (i == 0))
        _accumulate(dg_ref, dg, i == 0)
        _accumulate(sq_ref, jnp.sum(e * e, axis=0, keepdims=True) * (1.0 / d), (b == 0) & (i == 0))

    return pl.pallas_call(
        body, name=name, grid=grid, in_specs=[row, row, vec, bvec, row], out_specs=[row, row, vec, bvec, vec],
        out_shape=[jax.ShapeDtypeStruct((t, d), F32), jax.ShapeDtypeStruct((t, d), BF16), jax.ShapeDtypeStruct((1, d), F32),
                   jax.ShapeDtypeStruct((bsz, 1, d), F32), jax.ShapeDtypeStruct((1, d), F32)],
    )(res, y, w, g, tgt)


def _pre_post_bwd(x, w, sc, sh, du, dres, y, w_post, g, name, ts=512):
    t, d = x.shape
    bsz = sc.shape[0]
    nt, grid = _tok_grid(t, bsz, ts)
    row = _vspec((ts, d), lambda b, i: (b * nt + i, 0))
    vec = _vspec((1, d), lambda b, i: (0, 0))
    bvec = _vspec((1, 1, d), lambda b, i: (b, 0, 0))

    def body(x_ref, w_ref, sc_ref, sh_ref, du_ref, dres_ref, y_ref, wp_ref, g_ref,
             dx_ref, dy_ref, dw_ref, dsc_ref, dsh_ref, dwp_ref, dg_ref):
        b, i = pl.program_id(0), pl.program_id(1)
        _, vjp = jax.vjp(_pre_f, x_ref[...], w_ref[...], sc_ref[0], sh_ref[0])
        dx, dw, dsc, dsh = vjp(du_ref[...])
        dx = dres_ref[...] + dx
        dx_ref[...] = dx
        _, vjp_post = jax.vjp(_post_f, y_ref[...], wp_ref[...], g_ref[0])
        dy, dwp, dg = vjp_post(dx)
        dy_ref[...] = dy.astype(BF16)
        first = (b == 0) & (i == 0)
        _accumulate(dw_ref, dw, first)
        _accumulate(dwp_ref, dwp, first)
        _accumulate(dsc_ref, dsc, i == 0)
        _accumulate(dsh_ref, dsh, i == 0)
        _accumulate(dg_ref, dg, i == 0)

    v1, vb = jax.ShapeDtypeStruct((1, d), F32), jax.ShapeDtypeStruct((bsz, 1, d), F32)
    return pl.pallas_call(
        body, name=name, grid=grid, in_specs=[row, vec, bvec, bvec, row, row, row, vec, bvec],
        out_specs=[row, row, vec, bvec, bvec, vec, bvec],
        out_shape=[jax.ShapeDtypeStruct((t, d), F32), jax.ShapeDtypeStruct((t, d), BF16), v1, vb, vb, v1, vb],
    )(x, w, sc, sh, du, dres, y, w_post, g)


def _merge_f(ga, gd, ya, yd):
    return jax.nn.sigmoid(ga) * ya + jax.nn.sigmoid(gd) * yd


_MW = 512


def _merge_fwd(proj, ya, yd, ts=512):
    t, d = ya.shape
    blk = _vspec((ts, _MW), lambda i, j: (i, j))
    ga = _vspec((ts, _MW), lambda i, j: (i, C_GA // _MW + j))
    gd = _vspec((ts, _MW), lambda i, j: (i, C_GD // _MW + j))

    def body(ga_ref, gd_ref, ya_ref, yd_ref, o_ref):
        o_ref[...] = _merge_f(ga_ref[...], gd_ref[...], ya_ref[...], yd_ref[...]).astype(BF16)

    return pl.pallas_call(body, name="merge_fwd", grid=(t // ts, d // _MW), in_specs=[ga, gd, blk, blk], out_specs=blk,
                          out_shape=jax.ShapeDtypeStruct((t, d), BF16))(proj, proj, ya, yd)


def _merge_bwd(proj, ya, yd, dm, ts=512):
    t, d = ya.shape
    blk = _vspec((ts, _MW), lambda i, j: (i, j))
    ga = _vspec((ts, _MW), lambda i, j: (i, C_GA // _MW + j))
    gd = _vspec((ts, _MW), lambda i, j: (i, C_GD // _MW + j))

    def body(ga_ref, gd_ref, ya_ref, yd_ref, dm_ref, dga_ref, dgd_ref, dya_ref, dyd_ref):
        _, vjp = jax.vjp(_merge_f, ga_ref[...], gd_ref[...], ya_ref[...], yd_ref[...])
        dga, dgd, dya, dyd = vjp(dm_ref[...])
        dga_ref[...] = dga.astype(BF16)
        dgd_ref[...] = dgd.astype(BF16)
        dya_ref[...] = dya.astype(BF16)
        dyd_ref[...] = dyd.astype(BF16)

    o = jax.ShapeDtypeStruct((t, d), BF16)
    return pl.pallas_call(body, name="merge_bwd", grid=(t // ts, d // _MW), in_specs=[ga, gd, blk, blk, blk],
                          out_specs=[blk] * 4, out_shape=[o] * 4)(proj, proj, ya, yd, dm)


def _shift_down(x, s):
    if s == 0:
        return x
    r = lax.broadcasted_iota(jnp.int32, x.shape, 0)
    return jnp.where(r >= s, pltpu.roll(x, s, 0), 0.0)


def _shift_up(x, s):
    if s == 0:
        return x
    n = x.shape[0]
    r = lax.broadcasted_iota(jnp.int32, x.shape, 0)
    return jnp.where(r < n - s, pltpu.roll(x, n - s, 0), 0.0)


def _conv_fwd(x, w, k):
    out = None
    for j in range(k):
        term = w[j:j + 1, :] * _shift_down(x, k - 1 - j)
        out = term if out is None else out + term
    return out


def _conv_bwd(x, w, dc, k):
    dx = None
    dws = []
    for j in range(k):
        up = _shift_up(dc, k - 1 - j)
        term = w[j:j + 1, :] * up
        dx = term if dx is None else dx + term
        dws.append(jnp.sum(up * x, axis=0, keepdims=True))
    return dx, jnp.concatenate(dws, axis=0)


def _geglu_f(gate, val):
    return jax.nn.gelu(gate, approximate=True) * val


_FW = 256


def _ffn_act_fwd(up, conv_w, bsz):
    t = up.shape[0]
    s = t // bsz
    nj = D_FF // _FW
    xg = _vspec((s, _FW), lambda b, j: (b, j))
    xv = _vspec((s, _FW), lambda b, j: (b, nj + j))
    wg = _vspec((FFN_CONV, _FW), lambda b, j: (0, j))
    wv = _vspec((FFN_CONV, _FW), lambda b, j: (0, nj + j))

    def body(xg_ref, xv_ref, wg_ref, wv_ref, o_ref):
        gate = _conv_fwd(xg_ref[...], wg_ref[...], FFN_CONV)
        val = _conv_fwd(xv_ref[...], wv_ref[...], FFN_CONV)
        o_ref[...] = _geglu_f(gate, val).astype(BF16)

    return pl.pallas_call(body, name="ffn_act_fwd", grid=(bsz, nj), in_specs=[xg, xv, wg, wv],
                          out_specs=_vspec((s, _FW), lambda b, j: (b, j)),
                          out_shape=jax.ShapeDtypeStruct((t, D_FF), BF16))(up, up, conv_w, conv_w)


def _ffn_act_bwd(up, conv_w, dact, bsz, plug=None):
    t = up.shape[0]
    s = t // bsz
    nj = D_FF // _FW
    xg = _vspec((s, _FW), lambda j, b: (b, j))
    xv = _vspec((s, _FW), lambda j, b: (b, nj + j))
    wg = _vspec((FFN_CONV, _FW), lambda j, b: (0, j))
    wv = _vspec((FFN_CONV, _FW), lambda j, b: (0, nj + j))
    da = _vspec((s, _FW), lambda j, b: (b, j))
    dwo = _vspec((FFN_CONV, _FW), lambda j, b: (0, j))

    def body(xg_ref, xv_ref, wg_ref, wv_ref, da_ref, dxg_ref, dxv_ref, dwg_ref, dwv_ref):
        b = pl.program_id(1)
        xg_, xv_, wg_, wv_ = xg_ref[...], xv_ref[...], wg_ref[...], wv_ref[...]
        gate = _conv_fwd(xg_, wg_, FFN_CONV)
        val = _conv_fwd(xv_, wv_, FFN_CONV)
        _, vjp = jax.vjp(_geglu_f, gate, val)
        dgate, dval = vjp(da_ref[...])
        dxg, dwg = _conv_bwd(xg_, wg_, dgate, FFN_CONV)
        dxv, dwv = _conv_bwd(xv_, wv_, dval, FFN_CONV)
        dxg_ref[...] = dxg.astype(BF16)
        dxv_ref[...] = dxv.astype(BF16)

        @pl.when(b == 0)
        def _():
            dwg_ref[...] = jnp.zeros_like(dwg_ref)
            dwv_ref[...] = jnp.zeros_like(dwv_ref)

        dwg_ref[...] += dwg
        dwv_ref[...] += dwv

    outs, extra = _plugged_call(
        body, plug, _grid_ends((nj, bsz)), (up, up, conv_w, conv_w, dact), name="ffn_act_bwd", grid=(nj, bsz),
        in_specs=[xg, xv, wg, wv, da], out_specs=[da, da, dwo, dwo],
        out_shape=[jax.ShapeDtypeStruct((t, D_FF), BF16), jax.ShapeDtypeStruct((t, D_FF), BF16),
                   jax.ShapeDtypeStruct((FFN_CONV, D_FF), F32), jax.ShapeDtypeStruct((FFN_CONV, D_FF), F32)])
    return (*outs, extra)


def _bucket_table():
    qi = np.arange(WINDOW)[:, None]
    kj = np.arange(2 * WINDOW)[None, :]
    dist = WINDOW + qi - kj
    dc = np.maximum(dist, 0)
    max_exact = REL_BUCKETS // 2
    scaled = np.log(np.maximum(dc, 1).astype(np.float32) / np.float32(max_exact)) / np.float32(math.log(REL_MAX_DIST / max_exact))
    large = max_exact + (scaled.astype(np.float32) * np.float32(REL_BUCKETS - max_exact)).astype(np.int32)
    large = np.minimum(large, REL_BUCKETS - 1)
    bucket = np.where(dc < max_exact, dc, large).astype(np.int32)
    in_band = ((dist >= 0) & (dist < WINDOW)).astype(np.int32)
    return bucket, in_band


def _bias_build(rel_bias):
    bucket, _ = _bucket_table()

    def body(rb_ref, idx_ref, o_ref):
        h = pl.program_id(0)
        idx = idx_ref[...]
        acc = jnp.zeros(idx.shape, F32)
        for r in range(REL_BUCKETS):
            acc = jnp.where(idx == r, rb_ref[r, h], acc)
        o_ref[0] = acc

    return pl.pallas_call(
        body, name="bias_build", grid=(AQ_HEADS,),
        in_specs=[pl.BlockSpec(memory_space=pltpu.SMEM), _vspec((WINDOW, 2 * WINDOW), lambda h: (0, 0))],
        out_specs=_vspec((1, WINDOW, 2 * WINDOW), lambda h: (h, 0, 0)),
        out_shape=jax.ShapeDtypeStruct((AQ_HEADS, WINDOW, 2 * WINDOW), F32),
    )(rel_bias, jnp.asarray(bucket))


def _bias_grad(dbias):
    bucket, _ = _bucket_table()

    def body(db_ref, idx_ref, o_ref):
        idx = idx_ref[...]
        db = db_ref[0]
        lane = lax.broadcasted_iota(jnp.int32, (1, LANES), 1)
        acc = jnp.zeros((1, LANES), F32)
        for r in range(REL_BUCKETS):
            s = jnp.sum(jnp.sum(jnp.where(idx == r, db, 0.0), axis=1, keepdims=True), axis=0, keepdims=True)
            acc = jnp.where(lane == r, s, acc)
        o_ref[0] = acc

    return pl.pallas_call(
        body, name="bias_grad", grid=(AQ_HEADS,),
        in_specs=[_vspec((1, WINDOW, 2 * WINDOW), lambda h: (h, 0, 0)), _vspec((WINDOW, 2 * WINDOW), lambda h: (0, 0))],
        out_specs=_vspec((1, 1, LANES), lambda h: (h, 0, 0)),
        out_shape=jax.ShapeDtypeStruct((AQ_HEADS, 1, LANES), F32),
    )(dbias, jnp.asarray(bucket))


def _attn_mask(n):
    qi = lax.broadcasted_iota(jnp.int32, (WINDOW, 2 * WINDOW), 0)
    kj = lax.broadcasted_iota(jnp.int32, (WINDOW, 2 * WINDOW), 1)
    dist = WINDOW + qi - kj
    band = (dist >= 0) & (dist < WINDOW)
    return band & ((kj >= WINDOW) | (n > 0))


def _attn_probs(qk, bias, sink, mask):
    s = jnp.where(mask, qk * (A_HD ** -0.5) + bias, NEG_INF)
    m = jnp.maximum(jnp.max(s, axis=-1, keepdims=True), sink)
    p = jnp.exp(s - m)
    es = jnp.exp(sink - m)
    inv = 1.0 / (jnp.sum(p, axis=-1, keepdims=True) + es)
    return p * inv, es * inv


def _attn_fwd(proj, bias, sinks, bsz):
    t = proj.shape[0]
    s = t // bsz
    nb = s // WINDOW
    grp = AQ_HEADS // AKV_HEADS

    def body(q_ref, k_ref, v_ref, bias_ref, sink_ref, y_ref, kp_ref, vp_ref):
        kp_ref[0:WINDOW, :] = jnp.zeros((WINDOW, LANES), BF16)
        vp_ref[0:WINDOW, :] = jnp.zeros((WINDOW, LANES), BF16)
        kp_ref[WINDOW:, :] = k_ref[...].astype(BF16)
        vp_ref[WINDOW:, :] = v_ref[...].astype(BF16)

        def blk(n, carry):
            r0 = pl.multiple_of(n * WINDOW, WINDOW)
            mask = _attn_mask(n)
            kband = kp_ref[pl.ds(r0, 2 * WINDOW), :]
            vband = vp_ref[pl.ds(r0, 2 * WINDOW), :]
            qb = q_ref[pl.ds(r0, WINDOW), :].astype(BF16)
            heads = range(AQ_HEADS)
            hsl = lambda h: slice(h * A_HD, (h + 1) * A_HD)
            kbs = [kband[:, hsl(kv)] for kv in range(AKV_HEADS)]
            vbs = [vband[:, hsl(kv)] for kv in range(AKV_HEADS)]
            qks = [lax.dot_general(qb[:, hsl(h)], kbs[h // grp], _NT, preferred_element_type=F32) for h in heads]
            probs = [_attn_probs(qks[h], bias_ref[h], sink_ref[0, h], mask)[0] for h in heads]
            outs = [jnp.dot(probs[h].astype(BF16), vbs[h // grp], preferred_element_type=F32) for h in heads]
            y_ref[pl.ds(r0, WINDOW), :] = jnp.concatenate(outs, axis=1).astype(BF16)
            return carry

        lax.fori_loop(0, nb, blk, 0)

    return pl.pallas_call(
        body, name="attn_fwd", grid=(bsz,),
        in_specs=[_vspec((s, AQ), lambda b: (b, C_Q // AQ)), _vspec((s, AKV), lambda b: (b, C_K // AKV)),
                  _vspec((s, AKV), lambda b: (b, C_V // AKV)),
                  _vspec((AQ_HEADS, WINDOW, 2 * WINDOW), lambda b: (0, 0, 0)), pl.BlockSpec(memory_space=pltpu.SMEM)],
        out_specs=_vspec((s, AQ), lambda b: (b, 0)), out_shape=jax.ShapeDtypeStruct((t, AQ), BF16),
        scratch_shapes=[pltpu.VMEM((s + WINDOW, LANES), BF16), pltpu.VMEM((s + WINDOW, LANES), BF16)],
    )(proj, proj, proj, bias, sinks)


def _attn_bwd(proj, bias, sinks, dy, bsz, plug=None):
    t = proj.shape[0]
    s = t // bsz
    nb = s // WINDOW
    grp = AQ_HEADS // AKV_HEADS
    scale = A_HD ** -0.5

    def body(q_ref, k_ref, v_ref, bias_ref, sink_ref, dy_ref, dq_ref, dk_ref, dv_ref, dbias_ref, dsink_ref,
             kp_ref, vp_ref, dkp_ref, dvp_ref):
        b = pl.program_id(0)
        kp_ref[0:WINDOW, :] = jnp.zeros((WINDOW, LANES), BF16)
        vp_ref[0:WINDOW, :] = jnp.zeros((WINDOW, LANES), BF16)
        kp_ref[WINDOW:, :] = k_ref[...].astype(BF16)
        vp_ref[WINDOW:, :] = v_ref[...].astype(BF16)
        dkp_ref[...] = jnp.zeros_like(dkp_ref)
        dvp_ref[...] = jnp.zeros_like(dvp_ref)

        @pl.when(b == 0)
        def _():
            dbias_ref[...] = jnp.zeros_like(dbias_ref)
            dsink_ref[...] = jnp.zeros_like(dsink_ref)

        def blk(n, carry):
            r0 = pl.multiple_of(n * WINDOW, WINDOW)
            mask = _attn_mask(n)
            kband = kp_ref[pl.ds(r0, 2 * WINDOW), :]
            vband = vp_ref[pl.ds(r0, 2 * WINDOW), :]
            qb = q_ref[pl.ds(r0, WINDOW), :].astype(BF16)
            dyb = dy_ref[pl.ds(r0, WINDOW), :].astype(BF16)
            heads = range(AQ_HEADS)
            hsl = lambda h: slice(h * A_HD, (h + 1) * A_HD)
            kbs = [kband[:, hsl(kv)] for kv in range(AKV_HEADS)]
            vbs = [vband[:, hsl(kv)] for kv in range(AKV_HEADS)]
            qhs = [qb[:, hsl(h)] for h in heads]
            dyhs = [dyb[:, hsl(h)] for h in heads]
            qks = [lax.dot_general(qhs[h], kbs[h // grp], _NT, preferred_element_type=F32) for h in heads]
            dprobs = [lax.dot_general(dyhs[h], vbs[h // grp], _NT, preferred_element_type=F32) for h in heads]
            pbs, dsbs = [], []
            for h in heads:
                probs, psink = _attn_probs(qks[h], bias_ref[h], sink_ref[0, h], mask)
                rowdot = jnp.sum(probs * dprobs[h], axis=-1, keepdims=True)
                ds = probs * (dprobs[h] - rowdot)
                dbias_ref[h] += ds
                dsink_ref[h] += jnp.sum(-psink * rowdot, axis=0, keepdims=True) + jnp.zeros((1, LANES), F32)
                pbs.append(probs.astype(BF16))
                dsbs.append(ds.astype(BF16))
            dvhs = [lax.dot_general(pbs[h], dyhs[h], _TN, preferred_element_type=F32) for h in heads]
            dqs = [jnp.dot(dsbs[h], kbs[h // grp], preferred_element_type=F32) * scale for h in heads]
            dkhs = [lax.dot_general(dsbs[h], qhs[h], _TN, preferred_element_type=F32) * scale for h in heads]
            dks = [sum(dkhs[kv * grp + 1:(kv + 1) * grp], dkhs[kv * grp]) for kv in range(AKV_HEADS)]
            dvs = [sum(dvhs[kv * grp + 1:(kv + 1) * grp], dvhs[kv * grp]) for kv in range(AKV_HEADS)]
            dq_ref[pl.ds(r0, WINDOW), :] = jnp.concatenate(dqs, axis=1).astype(BF16)
            dkp_ref[pl.ds(r0, 2 * WINDOW), :] += jnp.concatenate(dks, axis=1)
            dvp_ref[pl.ds(r0, 2 * WINDOW), :] += jnp.concatenate(dvs, axis=1)
            return carry

        lax.fori_loop(0, nb, blk, 0)
        dk_ref[...] = dkp_ref[WINDOW:, :].astype(BF16)
        dv_ref[...] = dvp_ref[WINDOW:, :].astype(BF16)

    kvs = jax.ShapeDtypeStruct((t, AKV), BF16)
    outs, extra = _plugged_call(
        body, plug, _grid_ends((bsz,)), (proj, proj, proj, bias, sinks, dy), name="attn_bwd", grid=(bsz,),
        in_specs=[_vspec((s, AQ), lambda b: (b, C_Q // AQ)), _vspec((s, AKV), lambda b: (b, C_K // AKV)),
                  _vspec((s, AKV), lambda b: (b, C_V // AKV)),
                  _vspec((AQ_HEADS, WINDOW, 2 * WINDOW), lambda b: (0, 0, 0)), pl.BlockSpec(memory_space=pltpu.SMEM),
                  _vspec((s, AQ), lambda b: (b, 0))],
        out_specs=[_vspec((s, AQ), lambda b: (b, 0)), _vspec((s, AKV), lambda b: (b, 0)), _vspec((s, AKV), lambda b: (b, 0)),
                   _vspec((AQ_HEADS, WINDOW, 2 * WINDOW), lambda b: (0, 0, 0)), _vspec((AQ_HEADS, 1, LANES), lambda b: (0, 0, 0))],
        out_shape=[jax.ShapeDtypeStruct((t, AQ), BF16), kvs, kvs,
                   jax.ShapeDtypeStruct((AQ_HEADS, WINDOW, 2 * WINDOW), F32), jax.ShapeDtypeStruct((AQ_HEADS, 1, LANES), F32)],
        scratch_shapes=[pltpu.VMEM((s + WINDOW, LANES), BF16), pltpu.VMEM((s + WINDOW, LANES), BF16),
                        pltpu.VMEM((s + WINDOW, LANES), F32), pltpu.VMEM((s + WINDOW, LANES), F32)])
    return (*outs, extra)


def _dn_act_f(c, is_qk):
    a = jax.nn.silu(c)
    outs = []
    for h in range(DN_HEADS):
        ah = a[:, h * DN_HD:(h + 1) * DN_HD]
        nh = ah * lax.rsqrt(jnp.sum(ah * ah, axis=-1, keepdims=True) + L2_EPS)
        outs.append(jnp.where(is_qk, nh, ah))
    return jnp.concatenate(outs, axis=1)


def _dn_prep_fwd(proj, conv_w, bsz):
    t = proj.shape[0]
    s = t // bsz
    blk = _vspec((s, DN), lambda b, j: (b, j))
    wsp = _vspec((DN_CONV, DN), lambda b, j: (0, j))

    def body(x_ref, w_ref, o_ref):
        j = pl.program_id(1)
        o_ref[...] = _dn_act_f(_conv_fwd(x_ref[...], w_ref[...], DN_CONV), j < 2)

    return pl.pallas_call(body, name="dn_prep_fwd", grid=(bsz, 3), in_specs=[blk, wsp], out_specs=blk,
                          out_shape=jax.ShapeDtypeStruct((t, 3 * DN), F32))(proj, conv_w)


def _dn_prep_bwd(proj, conv_w, dqkvn, bsz):
    t = proj.shape[0]
    s = t // bsz
    blk = _vspec((s, DN), lambda j, b: (b, j))
    wsp = _vspec((DN_CONV, DN), lambda j, b: (0, j))

    def body(x_ref, w_ref, d_ref, dx_ref, dw_ref):
        j, b = pl.program_id(0), pl.program_id(1)
        x, w = x_ref[...], w_ref[...]
        c = _conv_fwd(x, w, DN_CONV)
        _, vjp = jax.vjp(lambda cc: _dn_act_f(cc, j < 2), c)
        (dc,) = vjp(d_ref[0])
        dx, dw = _conv_bwd(x, w, dc, DN_CONV)
        dx_ref[...] = dx.astype(BF16)

        @pl.when(b == 0)
        def _():
            dw_ref[...] = jnp.zeros_like(dw_ref)

        dw_ref[...] += dw

    return pl.pallas_call(
        body, name="dn_prep_bwd", grid=(3, bsz),
        in_specs=[blk, wsp, _vspec((1, s, DN), lambda j, b: (j, b, 0))], out_specs=[blk, wsp],
        out_shape=[jax.ShapeDtypeStruct((t, 3 * DN), BF16), jax.ShapeDtypeStruct((DN_CONV, 3 * DN), F32)],
    )(proj, conv_w, dqkvn)


def _bg_f(x, alog, dt):
    lane = lax.broadcasted_iota(jnp.int32, x.shape, 1)
    beta = jax.nn.sigmoid(x)
    g = -jnp.exp(alog) * jax.nn.softplus(x + dt)
    return jnp.where(lane < DN_HEADS, beta, jnp.where(lane < 2 * DN_HEADS, g, 0.0))


def _bg_fwd(proj, alog, dt, bsz):
    t = proj.shape[0]
    s = t // bsz
    vec = _vspec((1, LANES), lambda b: (0, 0))

    def body(x_ref, a_ref, d_ref, o_ref):
        o_ref[...] = _bg_f(x_ref[...], a_ref[...], d_ref[...])

    return pl.pallas_call(body, name="bg_fwd", grid=(bsz,), in_specs=[_vspec((s, LANES), lambda b: (b, C_BD // LANES)), vec, vec],
                          out_specs=_vspec((s, LANES), lambda b: (b, 0)), out_shape=jax.ShapeDtypeStruct((t, LANES), F32))(proj, alog, dt)


def _bg_bwd(proj, alog, dt, dbg4, bsz):
    t = proj.shape[0]
    s = t // bsz
    vec = _vspec((1, LANES), lambda b: (0, 0))

    def body(x_ref, a_ref, d_ref, g4_ref, dx_ref, da_ref, dd_ref):
        b = pl.program_id(0)
        lane = lax.broadcasted_iota(jnp.int32, (s, LANES), 1)
        dbg = jnp.zeros((s, LANES), F32)
        for h in range(DN_HEADS):
            gh = g4_ref[:, h * DN_HD:(h + 1) * DN_HD]
            dbg = jnp.where(lane == h, gh[:, 0:1], dbg)
            dbg = jnp.where(lane == DN_HEADS + h, gh[:, 1:2], dbg)
        _, vjp = jax.vjp(_bg_f, x_ref[...], a_ref[...], d_ref[...])
        dx, da, dd = vjp(dbg)
        dx_ref[...] = dx.astype(BF16)

        @pl.when(b == 0)
        def _():
            da_ref[...] = jnp.zeros_like(da_ref)
            dd_ref[...] = jnp.zeros_like(dd_ref)

        da_ref[...] += da
        dd_ref[...] += dd

    return pl.pallas_call(
        body, name="bg_bwd", grid=(bsz,),
        in_specs=[_vspec((s, LANES), lambda b: (b, C_BD // LANES)), vec, vec, _vspec((s, DN), lambda b: (b, 0))],
        out_specs=[_vspec((s, LANES), lambda b: (b, 0)), vec, vec],
        out_shape=[jax.ShapeDtypeStruct((t, LANES), BF16), jax.ShapeDtypeStruct((1, LANES), F32), jax.ShapeDtypeStruct((1, LANES), F32)],
    )(proj, alog, dt, dbg4)


def _dn_out_f(o, z, w):
    outs = []
    for h in range(DN_HEADS):
        sl = slice(h * DN_HD, (h + 1) * DN_HD)
        outs.append(_rms(o[:, sl], w) * jax.nn.silu(z[:, sl]))
    return jnp.concatenate(outs, axis=1)


def _dn_out_fwd(o, proj, w, ts=512):
    t = o.shape[0]
    blk = _vspec((ts, DN), lambda i: (i, 0))
    zsp = _vspec((ts, DN), lambda i: (i, C_DZ // DN))
    vec = _vspec((1, DN_HD), lambda i: (0, 0))

    def body(o_ref, z_ref, w_ref, y_ref):
        y_ref[...] = _dn_out_f(o_ref[...], z_ref[...], w_ref[...]).astype(BF16)

    return pl.pallas_call(body, name="dn_out_fwd", grid=(t // ts,), in_specs=[blk, zsp, vec], out_specs=blk,
                          out_shape=jax.ShapeDtypeStruct((t, DN), BF16))(o, proj, w)


def _dn_out_bwd(o, proj, w, dy, ts=512):
    t = o.shape[0]
    blk = _vspec((ts, DN), lambda i: (i, 0))
    zsp = _vspec((ts, DN), lambda i: (i, C_DZ // DN))
    vec = _vspec((1, DN_HD), lambda i: (0, 0))

    def body(o_ref, z_ref, w_ref, dy_ref, do_ref, dz_ref, dw_ref):
        i = pl.program_id(0)
        _, vjp = jax.vjp(_dn_out_f, o_ref[...], z_ref[...], w_ref[...])
        do, dz, dw = vjp(dy_ref[...])
        do_ref[...] = do
        dz_ref[...] = dz.astype(BF16)

        @pl.when(i == 0)
        def _():
            dw_ref[...] = jnp.zeros_like(dw_ref)

        dw_ref[...] += dw

    return pl.pallas_call(
        body, name="dn_out_bwd", grid=(t // ts,), in_specs=[blk, zsp, vec, blk], out_specs=[blk, blk, vec],
        out_shape=[jax.ShapeDtypeStruct((t, DN), F32), jax.ShapeDtypeStruct((t, DN), BF16), jax.ShapeDtypeStruct((1, DN_HD), F32)],
    )(o, proj, w, dy)


_C = DN_CHUNK


def _dot(a, b, dims):
    return lax.dot_general(a.astype(BF16), b.astype(BF16), dims, preferred_element_type=F32)


def _split(a):
    hi = a.astype(BF16)
    return hi, (a - hi.astype(F32)).astype(BF16)


def _dot3(a, b, dims):
    (ah, al), (bh, bl) = (a if isinstance(a, tuple) else _split(a)), (b if isinstance(b, tuple) else _split(b))
    mm = lambda x, y: lax.dot_general(x, y, dims, preferred_element_type=F32)
    return mm(ah, bh) + (mm(ah, bl) + mm(al, bh))


_NN = (((1,), (0,)), ((), ()))
_NT = (((1,), (1,)), ((), ()))
_TN = (((0,), (0,)), ((), ()))


_SUB = 8


def _tri_inverses(ls, lts):
    ri8 = lax.broadcasted_iota(jnp.int32, (_SUB, _C), 0)
    ci8 = lax.broadcasted_iota(jnp.int32, (_SUB, _C), 1)
    nblk = _C // _SUB
    ts = []
    for lt in lts:
        blocks = [jnp.where(ci8 == ri8 + _SUB * b, 1.0, 0.0).astype(F32) for b in range(nblk)]
        for r in range(1, _SUB):
            for b in range(nblk):
                coef = lt[_SUB * b:_SUB * (b + 1), _SUB * b + r:_SUB * b + r + 1]
                row = jnp.sum(coef * blocks[b], axis=0, keepdims=True)
                blocks[b] = jnp.where(ri8 == r, blocks[b] - row, blocks[b])
        ts.append(jnp.concatenate(blocks, axis=0))
    ri = lax.broadcasted_iota(jnp.int32, (_C, _C), 0)
    ci = lax.broadcasted_iota(jnp.int32, (_C, _C), 1)
    s = _SUB
    while s < _C:
        shift = s.bit_length()
        quad = ((ri >> shift) == (ci >> shift)) & ((ri & s) != 0) & ((ci & s) == 0)
        offs = [jnp.where(quad, l, 0.0) for l in ls]
        tsp = [_split(t) for t in ts]
        left = [_dot3(tp, off, _NN) for tp, off in zip(tsp, offs)]
        ts = [t - _dot3(lo, tp, _NN) for t, lo, tp in zip(ts, left, tsp)]
        s *= 2
    return ts


_SEG = 512
_HEADS = tuple(range(DN_HEADS))


def _hsl(hh):
    return slice(hh * DN_HD, (hh + 1) * DN_HD)


def _chunk_specs(bsz, nseg, reverse):
    seg = (lambda i: nseg - 1 - i) if reverse else (lambda i: i)
    ncs = _SEG // _C
    col = lambda off: _vspec((bsz, _SEG, DN), lambda i: (0, seg(i), off))
    return (col, _vspec((bsz, _SEG, LANES), lambda i: (0, seg(i), 0)),
            _vspec((bsz, DN_HEADS, ncs, _C), lambda i: (0, 0, seg(i), 0)),
            _vspec((bsz, DN_HEADS, ncs, DN_HD, DN_HD), lambda i: (0, 0, seg(i), 0, 0)),
            _vspec((bsz, DN_HEADS, ncs, _C, _C), lambda i: (0, 0, seg(i), 0, 0)))


def _chunk_pre(q_ref, k_ref, v_ref, bg_ref, gr_ref, c, bb, hh):
    r0 = pl.multiple_of(c * _C, _C)
    ri = lax.broadcasted_iota(jnp.int32, (_C, _C), 0)
    ci = lax.broadcasted_iota(jnp.int32, (_C, _C), 1)
    q = q_ref[bb, pl.ds(r0, _C), _hsl(hh)] * (DN_HD ** -0.5)
    k = k_ref[bb, pl.ds(r0, _C), _hsl(hh)]
    v = v_ref[bb, pl.ds(r0, _C), _hsl(hh)]
    bgc = bg_ref[bb, pl.ds(r0, _C), :]
    beta = bgc[:, hh:hh + 1]
    g_col = bgc[:, DN_HEADS + hh:DN_HEADS + hh + 1]
    g_row = gr_ref[bb, hh, pl.ds(c, 1), :]
    gc_col = jnp.sum(jnp.where(ri >= ci, g_row, 0.0), axis=1, keepdims=True)
    gc_row = jnp.sum(jnp.where(ri <= ci, g_col, 0.0), axis=0, keepdims=True)
    gc_last = jnp.sum(g_col, axis=0, keepdims=True)
    diff = gc_col - gc_row
    decay = jnp.where(ri >= ci, jnp.exp(jnp.where(ri >= ci, diff, 0.0)), 0.0)
    diff_t = gc_row - gc_col
    decay_t = jnp.where(ri <= ci, jnp.exp(jnp.where(ri <= ci, diff_t, 0.0)), 0.0)
    eg = jnp.exp(gc_col)
    et = jnp.exp(gc_last - gc_col)
    gl = jnp.exp(gc_last)
    kb = k * beta
    vb = v * beta
    return dict(r0=r0, bb=bb, hh=hh, q=q, k=k, v=v, beta=beta, decay=decay, decay_t=decay_t, eg=eg, et=et, gl=gl, kb=kb, vb=vb,
                ri=ri, ci=ci)


def _chunk_solve(ms, tms=None):
    for m in ms:
        m["kk_t"] = _dot(m["k"], m["kb"], _NT)
        m["qk"] = _dot(m["q"], m["k"], _NT)
        m["kk"] = _dot(m["kb"], m["k"], _NT)
        if tms is not None:
            m["qk_t"] = _dot(m["k"], m["q"], _NT)
    if tms is None:
        tms = _tri_inverses([jnp.where(m["ri"] > m["ci"], m["kk"] * m["decay"], 0.0) for m in ms],
                            [jnp.where(m["ri"] < m["ci"], m["kk_t"] * m["decay_t"], 0.0) for m in ms])
    for m, tm in zip(ms, tms):
        m["tm_f32"] = tm
    for m in ms:
        rhs = jnp.concatenate([m["vb"], m["kb"] * m["eg"]], axis=1)
        m["tm"] = _split(m["tm_f32"])
        m["sol"] = _dot3(m["tm"], rhs, _NN)
        m["intra"] = jnp.where(m["ri"] >= m["ci"], m["qk"] * m["decay"], 0.0)


def _dn_chunk_fwd(qkvn, bg, g_rows, bsz, plug=None):
    t = qkvn.shape[0]
    s = t // bsz
    nc, nseg = s // _C, s // _SEG
    pairs = [(bb, hh) for bb in range(bsz) for hh in _HEADS]

    def body(q_ref, k_ref, v_ref, bg_ref, gr_ref, o_ref, st_ref, tm_ref, s_ref):
        @pl.when(pl.program_id(0) == 0)
        def _():
            s_ref[...] = jnp.zeros_like(s_ref)

        def chunk(c, carry):
            ms = [_chunk_pre(q_ref, k_ref, v_ref, bg_ref, gr_ref, c, bb, hh) for bb, hh in pairs]
            _chunk_solve(ms)
            sts = [s_ref[i] for i in range(len(pairs))]
            for m, st in zip(ms, sts):
                st_ref[m["bb"], m["hh"], c] = st
                tm_ref[m["bb"], m["hh"], c] = m["tm_f32"]
            ws = [_dot(m["sol"][:, DN_HD:], st, _NN) for m, st in zip(ms, sts)]
            qs = [_dot(m["q"] * m["eg"], st, _NN) for m, st in zip(ms, sts)]
            v_new = [m["sol"][:, :DN_HD] - a for m, a in zip(ms, ws)]
            iv = [_dot(m["intra"], vn, _NN) for m, vn in zip(ms, v_new)]
            upd = [_dot(m["k"] * m["et"], vn, _TN) for m, vn in zip(ms, v_new)]
            for i, (bb, hh) in enumerate(pairs):
                s_ref[i] = sts[i] * ms[i]["gl"] + upd[i]
                o_ref[bb, pl.ds(ms[i]["r0"], _C), _hsl(hh)] = qs[i] + iv[i]
            return carry

        lax.fori_loop(0, _SEG // _C, chunk, 0)

    col, bgs, grs, sts_spec, tms_spec = _chunk_specs(bsz, nseg, False)
    q3, bg3 = qkvn.reshape(bsz, s, 3 * DN), bg.reshape(bsz, s, LANES)
    (o, states, tms), extra = _plugged_call(
        body, plug, _grid_ends((nseg,)), (q3, q3, q3, bg3, g_rows), name="dn_chunk_fwd", grid=(nseg,),
        in_specs=[col(0), col(1), col(2), bgs, grs], out_specs=[col(0), sts_spec, tms_spec],
        out_shape=[jax.ShapeDtypeStruct((bsz, s, DN), F32), jax.ShapeDtypeStruct((bsz, DN_HEADS, nc, DN_HD, DN_HD), F32),
                   jax.ShapeDtypeStruct((bsz, DN_HEADS, nc, _C, _C), F32)],
        scratch_shapes=[pltpu.VMEM((bsz * DN_HEADS, DN_HD, DN_HD), F32)])
    return o.reshape(t, DN), (states, tms), extra


def _dn_chunk_bwd(qkvn, bg, g_rows, states, do, bsz, plug=None):
    t = qkvn.shape[0]
    s = t // bsz
    nc, nseg = s // _C, s // _SEG
    pairs = [(bb, hh) for bb in range(bsz) for hh in _HEADS]

    def body(q_ref, k_ref, v_ref, bg_ref, gr_ref, st_ref, tm_ref, do_ref, dqkv_ref, dbg_ref, ds_ref):
        @pl.when(pl.program_id(0) == 0)
        def _():
            ds_ref[...] = jnp.zeros_like(ds_ref)

        def chunk(cc, carry):
            c = _SEG // _C - 1 - cc
            ms = [_chunk_pre(q_ref, k_ref, v_ref, bg_ref, gr_ref, c, bb, hh) for bb, hh in pairs]
            _chunk_solve(ms, [tm_ref[bb, hh, c] for bb, hh in pairs])
            ri, ci = ms[0]["ri"], ms[0]["ci"]
            for i, m in enumerate(ms):
                m["st"] = st_ref[m["bb"], m["hh"], c]
                m["ds_out"] = ds_ref[i]
                m["do"] = do_ref[m["bb"], pl.ds(m["r0"], _C), _hsl(m["hh"])]
                m["w"] = m["sol"][:, DN_HD:]
            for m in ms:
                m["v_new"] = m["sol"][:, :DN_HD] - _dot(m["w"], m["st"], _NN)
            for m in ms:
                m["q_dec"], m["k_tail"] = m["q"] * m["eg"], m["k"] * m["et"]
                m["dk_tail"] = _dot(m["v_new"], m["ds_out"], _NT)
                m["dv_new"] = _dot(m["k_tail"], m["ds_out"], _NN) + _dot(m["intra"], m["do"], _TN)
                m["dq_dec"] = _dot(m["do"], m["st"], _NT)
                m["ds_in"] = m["ds_out"] * m["gl"] + _dot(m["q_dec"], m["do"], _TN)
                m["dintra"] = jnp.where(ri >= ci, _dot(m["do"], m["v_new"], _NT), 0.0)
                m["dintra_t"] = jnp.where(ri <= ci, _dot(m["v_new"], m["do"], _NT), 0.0)
            for m in ms:
                m["dw"] = -_dot(m["dv_new"], m["st"], _NT)
                m["ds_in"] = m["ds_in"] - _dot(m["w"], m["dv_new"], _TN)
            for m in ms:
                dsol = jnp.concatenate([m["dv_new"], m["dw"]], axis=1)
                m["drhs"] = _dot3(m["tm"], dsol, _TN)
            for m in ms:
                m["dl"] = jnp.where(ri > ci, -_dot(m["drhs"], m["sol"], _NT), 0.0)
                m["dl_t"] = jnp.where(ri < ci, -_dot(m["sol"], m["drhs"], _NT), 0.0)
            for m in ms:
                m["dkb2"] = _dot(m["dl"] * m["decay"], m["k"], _NN)
                m["dk"] = _dot(m["dl_t"] * m["decay_t"], m["kb"], _NN) + _dot(m["dintra_t"] * m["decay_t"], m["q"], _NN)
                m["dq"] = _dot(m["dintra"] * m["decay"], m["k"], _NN)
            for m in ms:
                _chunk_bwd_finish(m)
            for m in ms:
                ones_ge = jnp.where(ri <= ci, 1.0, 0.0).astype(BF16)
                gh, gl_ = _split(m["dgc"] + jnp.zeros((_C, LANES), F32))
                m["dg_b"] = jnp.dot(ones_ge, gh, preferred_element_type=F32) + jnp.dot(ones_ge, gl_, preferred_element_type=F32)
            lane = lax.broadcasted_iota(jnp.int32, (_C, LANES), 1)
            for i, m in enumerate(ms):
                bb, hh, rows = m["bb"], m["hh"], pl.ds(m["r0"], _C)
                dqkv_ref[0, bb, rows, _hsl(hh)] = m["dq"] * (DN_HD ** -0.5)
                dqkv_ref[1, bb, rows, _hsl(hh)] = m["dk"]
                dqkv_ref[2, bb, rows, _hsl(hh)] = m["dv"]
                dbg_ref[bb, rows, _hsl(hh)] = jnp.where(lane == 0, m["dbeta"], jnp.where(lane == 1, m["dg_b"], 0.0))
                ds_ref[i] = m["ds_in"]
            return carry

        lax.fori_loop(0, _SEG // _C, chunk, 0)

    col, bgs, grs, sts_spec, tms_spec = _chunk_specs(bsz, nseg, True)
    q3, bg3, do3 = qkvn.reshape(bsz, s, 3 * DN), bg.reshape(bsz, s, LANES), do.reshape(bsz, s, DN)
    (dqkv, dbg), extra = _plugged_call(
        body, plug, _grid_ends((nseg,)), (q3, q3, q3, bg3, g_rows, *states, do3), name="dn_chunk_bwd", grid=(nseg,),
        in_specs=[col(0), col(1), col(2), bgs, grs, sts_spec, tms_spec, col(0)],
        out_specs=[_vspec((3, bsz, _SEG, DN), lambda i: (0, 0, nseg - 1 - i, 0)), col(0)],
        out_shape=[jax.ShapeDtypeStruct((3, bsz, s, DN), F32), jax.ShapeDtypeStruct((bsz, s, DN), F32)],
        scratch_shapes=[pltpu.VMEM((bsz * DN_HEADS, DN_HD, DN_HD), F32)])
    return dqkv.reshape(3, t, DN), dbg.reshape(t, DN), extra


def _chunk_bwd_finish(m):
    q, k, v, beta, decay, decay_t = m["q"], m["k"], m["v"], m["beta"], m["decay"], m["decay_t"]
    eg, et, gl, kb, dl, dl_t, dintra, dintra_t = m["eg"], m["et"], m["gl"], m["kb"], m["dl"], m["dl_t"], m["dintra"], m["dintra_t"]
    dq_dec, dk_tail, dq, dk = m["dq_dec"], m["dk_tail"], m["dq"], m["dk"]
    dgl = jnp.sum(jnp.sum(m["ds_out"] * m["st"], axis=1, keepdims=True), axis=0, keepdims=True)
    dvb, dkbeg = m["drhs"][:, :DN_HD], m["drhs"][:, DN_HD:]
    dkb = dkbeg * eg + m["dkb2"]
    deg = jnp.sum(dkbeg * kb, axis=1, keepdims=True)
    em = (dl * m["kk"] + dintra * m["qk"]) * decay
    em_t = (dl_t * m["kk_t"] + dintra_t * m["qk_t"]) * decay_t
    dgc = jnp.sum(em, axis=1, keepdims=True) - jnp.sum(em_t, axis=1, keepdims=True)
    dq = dq + dq_dec * eg
    deg = deg + jnp.sum(dq_dec * q, axis=1, keepdims=True)
    dk = dk + dk_tail * et
    det = jnp.sum(dk_tail * k, axis=1, keepdims=True)
    dgc = dgc + deg * eg - det * et
    dgc_last = jnp.sum(det * et, axis=0, keepdims=True) + dgl * gl
    rcol = lax.broadcasted_iota(jnp.int32, (_C, 1), 0)
    m["dgc"] = dgc + jnp.where(rcol == _C - 1, dgc_last, 0.0)
    m["dq"] = dq
    m["dk"] = dk + dkb * beta
    m["dbeta"] = jnp.sum(dkb * k, axis=1, keepdims=True) + jnp.sum(dvb * v, axis=1, keepdims=True)
    m["dv"] = dvb * beta


def _mod_fwd(c_all, ada_w_loc, ada_b_loc):
    n, cols = c_all.shape[0], ada_w_loc.shape[1]

    def body(c_ref, w_ref, b_ref, o_ref):
        o_ref[...] = _dot(jax.nn.silu(c_ref[...]), w_ref[...], _NN) + b_ref[...]

    return pl.pallas_call(body, name="mod_fwd", out_shape=jax.ShapeDtypeStruct((n, cols), F32))(c_all, ada_w_loc, ada_b_loc)


def _ada_grad(c_all, dmod_loc, dmod_all):
    d, cols = c_all.shape[1], dmod_loc.shape[1]

    def body(c_ref, dl_ref, da_ref, gw_ref, gb_ref):
        gw_ref[...] = _dot(jax.nn.silu(c_ref[...]), dl_ref[...], _TN)
        gb_ref[...] = jnp.sum(da_ref[...], axis=0, keepdims=True)

    return pl.pallas_call(body, name="ada_grad", out_shape=[jax.ShapeDtypeStruct((d, cols), F32),
                                                           jax.ShapeDtypeStruct((1, dmod_all.shape[1]), F32)])(c_all, dmod_loc, dmod_all)


ELEMENTWISE_BLOCK_BYTES = 3 * 2 ** 19


def _row_tile(r, c=1024):
    fits = [tr for tr in range(16, r + 1, 16) if tr * c * 4 <= ELEMENTWISE_BLOCK_BYTES]
    if not fits:
        return r
    whole = [tr for tr in fits if r % tr == 0]
    return whole[-1] if whole else fits[-1]


def _adamw(w, m, v, grads, name):
    r, rest = w.shape[0], w.shape[1:]
    c = math.prod(rest)
    tr = _row_tile(r, c)
    blk = _vspec((tr,) + rest, lambda i: (i,) + (0,) * len(rest))
    n = len(grads)

    def body(*refs):
        w_ref, m_ref, v_ref = refs[:3]
        g_ref, d_ref, mo_ref, vo_ref = refs[3 + n:]
        g = refs[3][...]
        for p in refs[4:3 + n]:
            g = g + p[...]
        g_ref[...] = g
        d_ref[...], mo_ref[...], vo_ref[...] = _adamw_math(w_ref[...], m_ref[...], v_ref[...], g)

    o = jax.ShapeDtypeStruct(w.shape, F32)
    return pl.pallas_call(body, name=name, grid=(pl.cdiv(r, tr),), in_specs=[blk] * (3 + n), out_specs=[blk] * 4,
                          out_shape=[o] * 4)(w, m, v, *grads)


def _adamw_math(w, m, v, g):
    m_new = ADAM_B1 * m + (1.0 - ADAM_B1) * g
    v_new = ADAM_B2 * v + (1.0 - ADAM_B2) * jnp.square(g)
    m_hat = m_new / (1.0 - ADAM_B1 ** ADAM_STEP)
    v_hat = v_new / (1.0 - ADAM_B2 ** ADAM_STEP)
    return -ADAM_LR * (m_hat / (jnp.sqrt(v_hat) + ADAM_EPS) + ADAM_WD * w), m_new, v_new


def _adamw_small(ws, ms, vs, gs, name):
    n = len(ws)

    def body(*refs):
        for i in range(n):
            w_ref, m_ref, v_ref, g_ref = (refs[j * n + i] for j in range(4))
            go_ref, d_ref, mo_ref, vo_ref = refs[4 * n + 4 * i:4 * n + 4 * i + 4]
            g = g_ref[...]
            go_ref[...] = g
            d_ref[...], mo_ref[...], vo_ref[...] = _adamw_math(w_ref[...], m_ref[...], v_ref[...], g)

    res = pl.pallas_call(body, name=name, out_shape=[jax.ShapeDtypeStruct(a.shape, F32) for a in ws for _ in range(4)])(
        *ws, *ms, *vs, *gs)
    return [res[4 * i:4 * i + 4] for i in range(n)]


def _sum_lead(x, name, rows_apart=False):
    p, r, c = x.shape
    tr = _row_tile(r, c)
    mid = (1,) if rows_apart else ()

    def body(x_ref, o_ref):
        acc = x_ref[0].astype(F32)
        for i in range(1, p):
            acc = acc + x_ref[i].astype(F32)
        o_ref[...] = acc.reshape(o_ref.shape)

    return pl.pallas_call(body, name=name, grid=(pl.cdiv(r, tr),), in_specs=[_vspec((p, tr, c), lambda i: (0, i, 0))],
                          out_specs=_vspec((tr,) + mid + (c,), lambda i: (i,) + (0,) * (1 + len(mid))),
                          out_shape=jax.ShapeDtypeStruct((r,) + mid + (c,), F32))(x)


def _allgather8(x_shard, name):
    m_per, n = x_shard.shape

    def body(x_ref, out_ref, send_sems, recv_sems, local_sem):
        x, y, c = lax.axis_index("x"), lax.axis_index("y"), lax.axis_index("c")
        me, sibling = (x, y, c), (x, y, 1 - c)
        chips = [(1 - x, y), (x, 1 - y), (1 - x, 1 - y)]

        def rows(px, py, pc):
            return out_ref.at[pl.ds((4 * px + 2 * py + pc) * m_per, m_per), :]

        def copy(k, block, to, src=None):
            return pltpu.make_async_remote_copy(
                src_ref=rows(*block) if src is None else src, dst_ref=rows(*block), send_sem=send_sems.at[k],
                recv_sem=recv_sems.at[k], device_id=to, device_id_type=MESH)

        mine = pltpu.make_async_copy(x_ref, rows(*me), local_sem)
        mine.start()
        first = [copy(0, me, sibling, src=x_ref)]
        first += [copy(1 + j, me, (*chip, c), src=x_ref) for j, chip in enumerate(chips)]
        for cp in first:
            cp.start()
        passed = [copy(4 + j, (*chip, c), sibling) for j, chip in enumerate(chips)]
        for j, chip in enumerate(chips):
            copy(1 + j, (*chip, c), me).wait_recv()
            passed[j].start()
        copy(0, sibling, me).wait_recv()
        for j, chip in enumerate(chips):
            copy(4 + j, (*chip, 1 - c), me).wait_recv()
        for cp in first + passed:
            cp.wait_send()
        mine.wait()

    return pl.pallas_call(
        body, name=name, out_shape=jax.ShapeDtypeStruct((8 * m_per, n), x_shard.dtype),
        in_specs=[pl.BlockSpec(memory_space=pltpu.VMEM)], out_specs=pl.BlockSpec(memory_space=pltpu.VMEM),
        scratch_shapes=[pltpu.SemaphoreType.DMA((7,)), pltpu.SemaphoreType.DMA((7,)), pltpu.SemaphoreType.DMA],
    )(x_shard)


_HBM = pl.BlockSpec(memory_space=pltpu.HBM)


def _mesh_place():
    x, y, c = lax.axis_index("x"), lax.axis_index("y"), lax.axis_index("c")
    return x, y, c, 2 * x + y, [(1 - x, y), (x, 1 - y), (1 - x, 1 - y)]


def _gather_plug(shards):
    n = len(shards)

    def half(ref, c, lead=None):
        r, cols = ref.shape[-2] // 2, ref.shape[-1] // 2
        if r % 16 == 0:
            rows = pl.ds(pl.multiple_of(c * r, 16), r)
            return ref.at[rows, :] if lead is None else ref.at[lead, rows, :]
        lanes = pl.ds(pl.multiple_of(c * cols, LANES), cols)
        return ref.at[:, lanes] if lead is None else ref.at[lead, :, lanes]

    def copies(ins, outs, send, recv):
        x, y, c, me, chips = _mesh_place()
        ici, fwd, fwd_in = [], [], []
        for i in range(n):
            for j, (px, py) in enumerate(chips):
                q = 2 * px + py
                ici.append((pltpu.make_async_remote_copy(
                    src_ref=half(ins[i], c), dst_ref=half(outs[i], c, me), send_sem=send.at[6 * i + j], recv_sem=recv.at[6 * i + j],
                    device_id=(px, py, c), device_id_type=MESH),
                    pltpu.make_async_remote_copy(
                    src_ref=half(ins[i], c), dst_ref=half(outs[i], c, q), send_sem=send.at[6 * i + j], recv_sem=recv.at[6 * i + j],
                    device_id=(px, py, c), device_id_type=MESH)))
                fwd.append(pltpu.make_async_remote_copy(
                    src_ref=half(outs[i], c, q), dst_ref=half(outs[i], c, q), send_sem=send.at[6 * i + 3 + j],
                    recv_sem=recv.at[6 * i + 3 + j], device_id=(x, y, 1 - c), device_id_type=MESH))
                fwd_in.append(pltpu.make_async_remote_copy(
                    src_ref=half(outs[i], 1 - c, q), dst_ref=half(outs[i], 1 - c, q), send_sem=send.at[6 * i + 3 + j],
                    recv_sem=recv.at[6 * i + 3 + j], device_id=(x, y, 1 - c), device_id_type=MESH))
        return ici, fwd, fwd_in, me

    def start(ins, outs, send, recv, loc):
        ici, _, _, me = copies(ins, outs, send, recv)
        for i in range(n):
            pltpu.make_async_copy(ins[i], outs[i].at[me], loc.at[i]).start()
        for out_cp, _ in ici:
            out_cp.start()

    def finish(ins, outs, send, recv, loc):
        ici, fwd, fwd_in, me = copies(ins, outs, send, recv)
        for (_, in_cp), f in zip(ici, fwd):
            in_cp.wait_recv()
            f.start()
        for f in fwd_in:
            f.wait_recv()
        for (out_cp, _), f in zip(ici, fwd):
            out_cp.wait_send()
            f.wait_send()
        for i in range(n):
            pltpu.make_async_copy(ins[i], outs[i].at[me], loc.at[i]).wait()

    return dict(ins=list(shards), out_shape=[jax.ShapeDtypeStruct((N_CHIPS,) + a.shape, a.dtype) for a in shards],
                scratch=[pltpu.SemaphoreType.DMA((6 * n,)), pltpu.SemaphoreType.DMA((6 * n,)), pltpu.SemaphoreType.DMA((n,))],
                start=start, finish=finish)


def _exchange_plug(pieces):
    n = len(pieces)

    def copies(ins, outs, send, recv):
        x, y, c, me, chips = _mesh_place()
        out_cps, in_cps = [], []
        for i in range(n):
            for j, (px, py) in enumerate(chips):
                q = 2 * px + py
                out_cps.append(pltpu.make_async_remote_copy(src_ref=ins[i].at[q], dst_ref=outs[i].at[me], send_sem=send.at[3 * i + j],
                                                            recv_sem=recv.at[3 * i + j], device_id=(px, py, c), device_id_type=MESH))
                in_cps.append(pltpu.make_async_remote_copy(src_ref=ins[i].at[me], dst_ref=outs[i].at[q], send_sem=send.at[3 * i + j],
                                                           recv_sem=recv.at[3 * i + j], device_id=(px, py, c), device_id_type=MESH))
        return out_cps, in_cps, me

    def start(ins, outs, send, recv, loc):
        out_cps, _, me = copies(ins, outs, send, recv)
        for i in range(n):
            pltpu.make_async_copy(ins[i].at[me], outs[i].at[me], loc.at[i]).start()
        for cp in out_cps:
            cp.start()

    def finish(ins, outs, send, recv, loc):
        out_cps, in_cps, me = copies(ins, outs, send, recv)
        for cp in in_cps:
            cp.wait_recv()
        for cp in out_cps:
            cp.wait_send()
        for i in range(n):
            pltpu.make_async_copy(ins[i].at[me], outs[i].at[me], loc.at[i]).wait()

    return dict(ins=list(pieces), out_shape=[jax.ShapeDtypeStruct(a.shape, a.dtype) for a in pieces],
                scratch=[pltpu.SemaphoreType.DMA((3 * n,)), pltpu.SemaphoreType.DMA((3 * n,)), pltpu.SemaphoreType.DMA((n,))],
                start=start, finish=finish)


def _comm_call(plug, name):
    n_in, n_out = len(plug["ins"]), len(plug["out_shape"])

    def body(*refs):
        ins, outs, sems = refs[:n_in], refs[n_in:n_in + n_out], refs[n_in + n_out:]
        plug["start"](ins, outs, *sems)
        plug["finish"](ins, outs, *sems)

    return pl.pallas_call(body, name=name, out_shape=plug["out_shape"], in_specs=[_HBM] * n_in, out_specs=[_HBM] * n_out,
                          scratch_shapes=plug["scratch"])(*plug["ins"])


def _plugged_call(body, plug, first_last, args, *, name, grid, in_specs, out_specs, out_shape, scratch_shapes=()):
    in_specs, out_specs, out_shape, scratch_shapes = list(in_specs), list(out_specs), list(out_shape), list(scratch_shapes)
    if plug is None:
        return pl.pallas_call(body, name=name, grid=grid, in_specs=in_specs, out_specs=out_specs, out_shape=out_shape,
                              scratch_shapes=scratch_shapes)(*args), []
    n_in, n_out, n_sc = len(in_specs), len(out_specs), len(scratch_shapes)
    p_in, p_out = len(plug["ins"]), len(plug["out_shape"])

    def full(*refs):
        ins, refs = refs[:n_in], refs[n_in:]
        pins, refs = refs[:p_in], refs[p_in:]
        outs, refs = refs[:n_out], refs[n_out:]
        pouts, refs = refs[:p_out], refs[p_out:]
        scr, psems = refs[:n_sc], refs[n_sc:]
        first, last = first_last()

        @pl.when(first)
        def _():
            plug["start"](pins, pouts, *psems)

        body(*ins, *outs, *scr)

        @pl.when(last)
        def _():
            plug["finish"](pins, pouts, *psems)

    res = pl.pallas_call(full, name=name, grid=grid, in_specs=in_specs + [_HBM] * p_in, out_specs=out_specs + [_HBM] * p_out,
                         out_shape=out_shape + plug["out_shape"], scratch_shapes=scratch_shapes + plug["scratch"])(*args, *plug["ins"])
    return res[:n_out], res[n_out:]


def _grid_ends(grid):
    def ends():
        first = last = None
        for ax, n in enumerate(grid):
            i = pl.program_id(ax)
            first = (i == 0) if first is None else first & (i == 0)
            last = (i == n - 1) if last is None else last & (i == n - 1)
        return first, last
    return ends


def _sibling_exchange(arrs, name):
    n = len(arrs)

    def body(*refs):
        ins, outs = refs[:n], refs[n:2 * n]
        send_sems, recv_sems = refs[2 * n:]
        sibling = (lax.axis_index("x"), lax.axis_index("y"), 1 - lax.axis_index("c"))
        cps = [pltpu.make_async_remote_copy(src_ref=ins[i], dst_ref=outs[i], send_sem=send_sems.at[i], recv_sem=recv_sems.at[i],
                                            device_id=sibling, device_id_type=MESH) for i in range(n)]
        for cp in cps:
            cp.start()
        for cp in cps:
            cp.wait()

    return pl.pallas_call(
        body, name=name, out_shape=[jax.ShapeDtypeStruct(a.shape, a.dtype) for a in arrs], in_specs=[_HBM] * n, out_specs=[_HBM] * n,
        scratch_shapes=[pltpu.SemaphoreType.DMA((n,)), pltpu.SemaphoreType.DMA((n,))],
    )(*arrs)


def _to_padded(wt):
    z = jnp.zeros((IN_PAD - IN_DIM, wt.shape[1]), wt.dtype)
    return jnp.concatenate([wt[768:2304], wt[0:512], wt[2304:2816], wt[2824:3848], wt[3848:4872],
                            wt[512:640], wt[640:768], wt[2816:2824], z], axis=0)


def _from_padded(gt):
    return jnp.concatenate([gt[C_Q:C_Q + AQ], gt[C_K:C_K + AKV], gt[C_V:C_V + AKV], gt[C_DQKV:C_DQKV + 3 * DN],
                            gt[C_DZ:C_DZ + DN], gt[C_BD:C_BD + 2 * DN_HEADS], gt[C_GA:C_GA + D_MODEL],
                            gt[C_GD:C_GD + D_MODEL]], axis=0)


def _lane_vec(a):
    return jnp.zeros((1, LANES), F32).at[0, DN_HEADS:2 * DN_HEADS].set(a)


_ROW_SHARDED = ("w_in", "w_out", "ffn_w_down")
_FFN = ("ffn_w_up", "ffn_w_down")
_LATE_MIXER = ("w_attn_branch", "w_dn_branch", "w_out")


def _pieces(k, a):
    if a.ndim == 3:
        return a
    if k in _ROW_SHARDED:
        return a.reshape(N_CHIPS, a.shape[0] // N_CHIPS, a.shape[1]).astype(BF16)
    return jnp.transpose(a.reshape(a.shape[0], N_CHIPS, a.shape[1] // N_CHIPS), (1, 0, 2)).astype(BF16)


def _assemble(k, a):
    if k in _ROW_SHARDED:
        return a.reshape(-1, a.shape[2])
    return jnp.transpose(a, (1, 0, 2)).reshape(a.shape[1], -1)


def _device_step(x2, tgt2, mod, p, bsz, shards=None):
    d = D_MODEL
    on_mesh = shards is not None
    p = dict(p)
    sh1, sc1, g1, sh2, sc2, g2 = [mod[:, i * d:(i + 1) * d].reshape(bsz, 1, d) for i in range(N_MOD)]
    alog_v, dt_v = _lane_vec(p["dn_a_log"]), _lane_vec(p["dn_dt_bias"])
    sinks = p["attn_sinks"].reshape(1, AQ_HEADS)
    u1 = _pre_fwd(x2, p["norm_mix_pre"], sc1, sh1, "pre1_fwd")
    if on_mesh:
        proj, got = _mm(u1, p["w_in"], "nt", F32, "mm_proj", _gather_plug(shards["late_mixer"]))
        p.update({k: _assemble(k, a) for k, a in zip(_LATE_MIXER, got)})
    else:
        proj = _mm(u1, p["w_in"], "nt", F32, "mm_proj")
    bias = _bias_build(p["rel_bias"])
    y_attn = _attn_fwd(proj, bias, sinks, bsz)
    qkvn = _dn_prep_fwd(proj, p["dn_conv_w"], bsz)
    bg = _bg_fwd(proj, alog_v, dt_v, bsz)
    nc = x2.shape[0] // bsz // DN_CHUNK
    g_rows = jnp.transpose(bg[:, DN_HEADS:2 * DN_HEADS].reshape(bsz, nc, DN_CHUNK, DN_HEADS), (0, 3, 1, 2))
    o, states, got = _dn_chunk_fwd(qkvn, bg, g_rows, bsz, _gather_plug(shards["ffn"][:1]) if on_mesh else None)
    for k, a in zip(_FFN[:1], got):
        p[k] = _assemble(k, a)
    y_dn = _dn_out_fwd(o, proj, p["dn_norm_w"])
    ya = _mm(y_attn, p["w_attn_branch"], "nn", F32, "mm_ya")
    yd = _mm(y_dn, p["w_dn_branch"], "nn", F32, "mm_yd")
    merged = _merge_fwd(proj, ya, yd)
    y1 = _mm(merged, p["w_out"], "nn", F32, "mm_y1")
    h1, u2 = _post_pre_fwd(x2, y1, p["norm_mix_post"], g1, p["norm_ffn_pre"], sc2, sh2, "post1_pre2_fwd")
    if on_mesh:
        up, got = _mm(u2, p["ffn_w_up"], "nn", F32, "mm_up", _gather_plug(shards["ffn"][1:]))
        p["ffn_w_down"] = _assemble("ffn_w_down", got[0])
    else:
        up = _mm(u2, p["ffn_w_up"], "nn", F32, "mm_up")
    act = _ffn_act_fwd(up, p["ffn_conv_w"], bsz)
    y2 = _mm(act, p["ffn_w_down"], "nn", F32, "mm_y2")
    dh2, dy2, g_ffn_post, dg2, sq = _post_loss_bwd(h1, y2, p["norm_ffn_post"], g2, tgt2, "post2_loss_bwd")
    g = {}
    g["norm_ffn_post"] = g_ffn_post
    dact = _mm(dy2, p["ffn_w_down"], "nt", F32, "mm_dact")
    g["ffn_w_down"] = _mm(act, dy2, "tn", BF16, "mm_dwdown")
    dupg, dupv, dcwg, dcwv, got_down = _ffn_act_bwd(
        up, p["ffn_conv_w"], dact, bsz, _exchange_plug([_pieces("ffn_w_down", g["ffn_w_down"])]) if on_mesh else None)
    g["ffn_conv_w"] = jnp.concatenate([dcwg, dcwv], axis=1)
    dup = jnp.concatenate([dupg, dupv], axis=1)
    g["ffn_w_up"] = _mm(u2, dup, "tn", BF16, "mm_dwup", split=N_CHIPS)
    du2 = _mm(dup, p["ffn_w_up"], "nt", F32, "mm_du2")
    dh1, dy1, g["norm_ffn_pre"], dsc2, dsh2, g["norm_mix_post"], dg1 = _pre_post_bwd(
        h1, p["norm_ffn_pre"], sc2, sh2, du2, dh2, y1, p["norm_mix_post"], g1, "pre2_post1_bwd")
    dmerged = _mm(dy1, p["w_out"], "nt", F32, "mm_dmerged")
    g["w_out"] = _mm(merged, dy1, "tn", BF16, "mm_dwout")
    dga, dgd, dya, dyd = _merge_bwd(proj, ya, yd, dmerged)
    dy_attn = _mm(dya, p["w_attn_branch"], "nt", BF16, "mm_dyattn")
    g["w_attn_branch"] = _mm(y_attn, dya, "tn", BF16, "mm_dwab", split=N_CHIPS)
    dy_dn = _mm(dyd, p["w_dn_branch"], "nt", F32, "mm_dydn")
    g["w_dn_branch"] = _mm(y_dn, dyd, "tn", BF16, "mm_dwdb", split=N_CHIPS)
    do, dz, g["dn_norm_w"] = _dn_out_bwd(o, proj, p["dn_norm_w"], dy_dn)
    def plug_for(names):
        return _exchange_plug([_pieces(k, g[k]) for k in names]) if on_mesh else None

    early = ("w_out", "w_attn_branch", "w_dn_branch")
    dqkvn, dbg4, got_up = _dn_chunk_bwd(qkvn, bg, g_rows, states, do, bsz, plug_for(_FFN[:1]))
    got_ffn = list(got_up) + list(got_down)
    d_dqkv, g["dn_conv_w"] = _dn_prep_bwd(proj, p["dn_conv_w"], dqkvn, bsz)
    dbd, g["dn_a_log"], g["dn_dt_bias"] = _bg_bwd(proj, alog_v, dt_v, dbg4, bsz)
    dq, dk, dv, dbias, g["attn_sinks"], got_early = _attn_bwd(proj, bias, sinks, dy_attn, bsz, plug_for(early))
    g["rel_bias"] = _bias_grad(dbias)
    dproj = jnp.concatenate([d_dqkv, dq, dz, dga, dgd, dk, dv, dbd], axis=1)
    g["w_in"] = _from_padded(_mm(dproj, u1, "tn", BF16, "mm_dwin"))
    if on_mesh:
        du1, got_in = _mm(dproj, p["w_in"], "nn", F32, "mm_du1", plug_for(("w_in",)))
        g.update(zip(_FFN + early + ("w_in",), list(got_ffn) + list(got_early) + list(got_in)))
    else:
        du1 = _mm(dproj, p["w_in"], "nn", F32, "mm_du1")
    dx, g["norm_mix_pre"], dsc1, dsh1 = _pre_bwd(x2, p["norm_mix_pre"], sc1, sh1, du1, dh1, "pre1_bwd")
    dmod = jnp.concatenate([dsh1, dsc1, dg1, dsh2, dsc2, dg2], axis=-1).reshape(bsz, N_MOD * d)
    return sq, dx, dmod, g


_SMALL = (("norm_mix_pre", D_MODEL), ("norm_mix_post", D_MODEL), ("norm_ffn_pre", D_MODEL), ("norm_ffn_post", D_MODEL),
          ("dn_norm_w", DN_HD), ("dn_a_log", LANES), ("dn_dt_bias", LANES), ("attn_sinks", AQ_HEADS * LANES),
          ("rel_bias", AQ_HEADS * LANES), ("dn_conv_w", DN_CONV * 3 * DN), ("ffn_conv_w", FFN_CONV * 2 * D_FF))


def _pack_rows(parts, rows):
    flat = jnp.concatenate([a.reshape(-1) for a in parts])
    return jnp.concatenate([flat, jnp.zeros((rows * LANES - flat.shape[0],), F32)]).reshape(rows, LANES)


def _pad128(a):
    flat = a.reshape(-1)
    n = -(-flat.shape[0] // LANES) * LANES
    return jnp.concatenate([flat, jnp.zeros((n - flat.shape[0],), F32)]) if n != flat.shape[0] else flat


_W_NAMES = ("ada_w", "ada_b", "norm_mix_pre", "norm_mix_post", "norm_ffn_pre", "norm_ffn_post", "w_in", "dn_conv_w", "dn_a_log",
            "dn_dt_bias", "dn_norm_w", "attn_sinks", "rel_bias", "w_attn_branch", "w_dn_branch", "w_out", "ffn_w_up", "ffn_conv_w",
            "ffn_w_down")
_BIG = ("w_in", "w_attn_branch", "w_dn_branch", "w_out", "ffn_w_up", "ffn_w_down")


def kernel(x, c, *rest):
    nw = len(_W_NAMES)
    w = dict(zip(_W_NAMES, rest[:nw]))
    loss_target = rest[nw]
    m = dict(zip(_W_NAMES, rest[nw + 1:2 * nw + 1]))
    v = dict(zip(_W_NAMES, rest[2 * nw + 1:3 * nw + 1]))
    ix, iy, ic = lax.axis_index("x"), lax.axis_index("y"), lax.axis_index("c")
    chip, dev = 2 * ix + iy, 4 * ix + 2 * iy + ic
    bsz, s, d = x.shape
    t = bsz * s
    n_dev = 8

    front_rows = 64
    front = _pack_rows([c, w["dn_conv_w"], w["ffn_conv_w"]], front_rows)
    front_all = _allgather8(front, "ag_front").reshape(n_dev, front_rows * LANES)
    n_c, n_dc, n_fc = bsz * d, DN_CONV * 3 * DN // N_CHIPS, FFN_CONV * 2 * D_FF // N_CHIPS
    c_all = front_all[:, :n_c].reshape(n_dev * bsz, d)
    per_chip = front_all[0::2]
    dn_conv_full = jnp.transpose(per_chip[:, n_c:n_c + n_dc].reshape(N_CHIPS, DN_CONV, -1), (1, 0, 2)).reshape(DN_CONV, 3 * DN)
    ffn_conv_full = jnp.transpose(per_chip[:, n_c + n_dc:n_c + n_dc + n_fc].reshape(N_CHIPS, FFN_CONV, -1), (1, 0, 2)).reshape(FFN_CONV, 2 * D_FF)

    mod_cols = N_MOD * d // N_CHIPS
    ada_b_loc = lax.dynamic_slice(w["ada_b"], (0, chip * mod_cols), (1, mod_cols))
    mod_part = _mod_fwd(c_all, w["ada_w"][0], ada_b_loc)
    mod_all = _allgather8(mod_part, "ag_mod").reshape(n_dev, n_dev * bsz, mod_cols)[0::2]
    mod = jnp.transpose(lax.dynamic_slice(mod_all, (0, dev * bsz, 0), (N_CHIPS, bsz, mod_cols)), (1, 0, 2)).reshape(bsz, N_MOD * d)

    (w_in_all,) = _comm_call(_gather_plug([jnp.swapaxes(w["w_in"][0], 0, 1).astype(BF16)]), "gather_w_in")
    p = {"w_in": _to_padded(_assemble("w_in", w_in_all))}
    shards = {"late_mixer": [w[k][0].astype(BF16) for k in _LATE_MIXER], "ffn": [w[k][0].astype(BF16) for k in _FFN]}
    for k in ("norm_mix_pre", "norm_mix_post", "norm_ffn_pre", "norm_ffn_post", "dn_norm_w", "attn_sinks"):
        p[k] = w[k]
    p["dn_a_log"], p["dn_dt_bias"], p["rel_bias"] = w["dn_a_log"][0], w["dn_dt_bias"][0], w["rel_bias"]
    p["dn_conv_w"], p["ffn_conv_w"] = dn_conv_full, ffn_conv_full

    sq, dx, dmod, g = _device_step(x.reshape(t, d), loss_target.reshape(t, d), mod, p, bsz, shards)
    loss = lax.psum(0.5 * jnp.sum(sq), ("x", "y", "c"))

    g["dn_a_log"], g["dn_dt_bias"] = g["dn_a_log"].reshape(-1), g["dn_dt_bias"].reshape(-1)
    small_rows = 328
    small = _pack_rows([dmod] + [g[k] for k, _ in _SMALL], small_rows)
    small_all = _allgather8(small, "ag_small").reshape(n_dev, small_rows, LANES)
    n_dm = bsz * N_MOD * d
    dmod_all = small_all.reshape(n_dev, -1)[:, :n_dm].reshape(n_dev * bsz, N_MOD * d)
    tot = _sum_lead(small_all, "sum_small").reshape(-1)
    gs, off = {}, n_dm
    for k, n in _SMALL:
        gs[k] = tot[off:off + n]
        off += n
    grad = {}
    grad["ada_w"], grad["ada_b"] = _ada_grad(c_all, lax.dynamic_slice(dmod_all, (0, chip * mod_cols), (n_dev * bsz, mod_cols)), dmod_all)
    for k in ("norm_mix_pre", "norm_mix_post", "norm_ffn_pre", "norm_ffn_post", "dn_norm_w"):
        grad[k] = gs[k]
    grad["dn_a_log"] = gs["dn_a_log"][DN_HEADS:2 * DN_HEADS]
    grad["dn_dt_bias"] = gs["dn_dt_bias"][DN_HEADS:2 * DN_HEADS]
    grad["attn_sinks"] = gs["attn_sinks"].reshape(AQ_HEADS, LANES)[:, 0]
    grad["rel_bias"] = gs["rel_bias"].reshape(AQ_HEADS, LANES)[:, :REL_BUCKETS].T
    grad["dn_conv_w"] = lax.dynamic_slice(gs["dn_conv_w"].reshape(DN_CONV, 3 * DN), (0, chip * (3 * DN // N_CHIPS)), (DN_CONV, 3 * DN // N_CHIPS))
    grad["ffn_conv_w"] = lax.dynamic_slice(gs["ffn_conv_w"].reshape(FFN_CONV, 2 * D_FF), (0, chip * (2 * D_FF // N_CHIPS)), (FFN_CONV, 2 * D_FF // N_CHIPS))

    mine = [_sum_lead(g[k], "sum_" + k, rows_apart=(k == "w_in")) for k in _BIG]
    theirs = _sibling_exchange(mine, "exchange_cores")

    out = {}
    for k, a, b in zip(_BIG, mine, theirs):
        if k == "w_in":
            tr = lambda z: jnp.transpose(z, (2, 0, 1))
            out[k] = [jnp.transpose(r, (1, 2, 0)) for r in _adamw(tr(w[k]), tr(m[k]), tr(v[k]), [a, b], "adamw_" + k)]
        else:
            out[k] = _adamw(w[k][0], m[k][0], v[k][0], [a, b], "adamw_" + k)
    out["ada_w"] = _adamw(w["ada_w"][0], m["ada_w"][0], v["ada_w"][0], [grad["ada_w"]], "adamw_ada_w")
    small_names = [k for k in _W_NAMES if k not in _BIG and k != "ada_w"]
    res_small = _adamw_small([w[k] for k in small_names], [m[k] for k in small_names], [v[k] for k in small_names],
                             [grad[k].reshape(w[k].shape) for k in small_names], "adamw_small")
    out.update(zip(small_names, res_small))
    for k in _BIG + ("ada_w",):
        out[k] = [r.reshape(w[k].shape) for r in out[k]]
    grads, deltas, new_m, new_v = ([out[k][i] for k in _W_NAMES] for i in range(4))
    return (loss, dx.reshape(bsz, s, d), *grads, *deltas, *new_m, *new_v)
```

```python
import functools
import math

import numpy as np
import jax
import jax.numpy as jnp
from jax import lax
from jax.experimental import pallas as pl
from jax.experimental.pallas import tpu as pltpu

F32 = jnp.float32
BF16 = jnp.bfloat16
MESH = pl.DeviceIdType.MESH

D_MODEL = 1024
N_MOD = 6
AQ_HEADS, AKV_HEADS, A_HD, WINDOW = 8, 2, 64, 128
REL_BUCKETS, REL_MAX_DIST = 32, 128
DN_HEADS, DN_HD, DN_CONV, DN_CHUNK = 4, 128, 4, 64
D_FF, FFN_CONV = 2816, 3
RMS_EPS, L2_EPS, NEG_INF = 1e-6, 1e-6, -1e30
AQ, AKV, DN = AQ_HEADS * A_HD, AKV_HEADS * A_HD, DN_HEADS * DN_HD
IN_DIM = AQ + 2 * AKV + 3 * DN + DN + 2 * DN_HEADS + 2 * D_MODEL
C_DQKV, C_Q, C_DZ, C_GA, C_GD, C_K, C_V, C_BD = 0, 1536, 2048, 2560, 3584, 4608, 4736, 4864
IN_PAD = 4992
LANES = 128
N_CHIPS = 4

ADAM_LR, ADAM_B1, ADAM_B2, ADAM_EPS, ADAM_WD, ADAM_STEP = 0.001, 0.9, 0.999, 1e-08, 0.01, 10


def _pick(n, cap):
    best = None
    for t in range(LANES, cap + 1, LANES):
        if n % t == 0:
            best = t
    return best if best is not None else n


def _vspec(shape, index_map):
    return pl.BlockSpec(shape, index_map)


MM_VMEM_BUDGET = 40 * 2 ** 20
GRID_STEP_S = 0.35e-6
HBM_BYTES_PER_S = 3.0e12
MXU_FLOPS_PER_S = 9.0e14
MXU_DIM = 256


def _mm_tiles(m, n, k, mode, in_bytes, out_bytes, split=1):
    best = None
    for tm in [t for t in range(LANES, m + 1, LANES) if m % t == 0]:
        for tn in [t for t in range(LANES, n // split + 1, LANES) if (n // split) % t == 0]:
            a, b, o = k * tm * in_bytes, k * tn * in_bytes, tm * tn * out_bytes
            if 2 * (a + b + o) + (a if mode == "tn" else 0) > MM_VMEM_BUDGET:
                continue
            hbm_s = (m * k * in_bytes + (m // tm) * n * k * in_bytes + m * n * out_bytes) / HBM_BYTES_PER_S
            mxu_s = 2 * m * n * k / (MXU_FLOPS_PER_S * min(1.0, tm / MXU_DIM) * min(1.0, tn / MXU_DIM))
            cost = (m // tm) * (n // tn) * GRID_STEP_S + max(hbm_s, mxu_s)
            if best is None or cost < best[0]:
                best = (cost, tm, tn)
    return best[1], best[2]


def _mm(a, b, mode, out_dtype, name, plug=None, split=1):
    if mode == "nn":
        (m, k), n = a.shape, b.shape[1]
        dims = (((1,), (0,)), ((), ()))
    elif mode == "nt":
        (m, k), n = a.shape, b.shape[0]
        dims = (((1,), (1,)), ((), ()))
    else:
        (k, m), n = a.shape, b.shape[1]
        dims = (((0,), (0,)), ((), ()))
    tm, tn = _mm_tiles(m, n, k, mode, a.dtype.itemsize, jnp.dtype(out_dtype).itemsize, split)
    if mode == "tn":
        a_spec = _vspec((k, tm), lambda i, j: (0, i))
    else:
        a_spec = _vspec((tm, k), lambda i, j: (i, 0))
    if mode == "nt":
        b_spec = _vspec((tn, k), lambda i, j: (j, 0))
    else:
        b_spec = _vspec((k, tn), lambda i, j: (0, j))

    def body(a_ref, b_ref, o_ref):
        o_ref[...] = lax.dot_general(a_ref[...].astype(BF16), b_ref[...].astype(BF16), dims,
                                     preferred_element_type=F32).astype(out_dtype).reshape(o_ref.shape)

    grid = (m // tm, n // tn)
    if split == 1:
        out_spec, out_shape = _vspec((tm, tn), lambda i, j: (i, j)), (m, n)
    else:
        per = n // split // tn
        out_spec, out_shape = _vspec((1, tm, tn), lambda i, j: (j // per, i, j % per)), (split, m, n // split)
    (out,), extra = _plugged_call(body, plug, _grid_ends(grid), (a, b), name=name, grid=grid, in_specs=[a_spec, b_spec],
                                  out_specs=[out_spec], out_shape=[jax.ShapeDtypeStruct(out_shape, out_dtype)])
    return out if plug is None else (out, extra)


def _rms(x, w):
    return (x * lax.rsqrt(jnp.mean(x * x, axis=-1, keepdims=True) + RMS_EPS)) * w


def _pre_f(x, w, sc, sh):
    return _rms(x, w) * (1.0 + sc) + sh


def _post_f(y, w, g):
    return g * _rms(y, w)


def _tok_grid(t, bsz, ts):
    nt = t // bsz // ts
    return nt, (bsz, nt)


def _pre_fwd(x, w, sc, sh, name, ts=512):
    t, d = x.shape
    bsz = sc.shape[0]
    nt, grid = _tok_grid(t, bsz, ts)
    row = _vspec((ts, d), lambda b, i: (b * nt + i, 0))
    vec = _vspec((1, d), lambda b, i: (0, 0))
    bvec = _vspec((1, 1, d), lambda b, i: (b, 0, 0))

    def body(x_ref, w_ref, sc_ref, sh_ref, u_ref):
        u_ref[...] = _pre_f(x_ref[...], w_ref[...], sc_ref[0], sh_ref[0]).astype(BF16)

    return pl.pallas_call(body, name=name, grid=grid, in_specs=[row, vec, bvec, bvec], out_specs=row,
                          out_shape=jax.ShapeDtypeStruct((t, d), BF16))(x, w, sc, sh)


def _pre_bwd(x, w, sc, sh, du, dres, name, ts=512):
    t, d = x.shape
    bsz = sc.shape[0]
    nt, grid = _tok_grid(t, bsz, ts)
    row = _vspec((ts, d), lambda b, i: (b * nt + i, 0))
    vec = _vspec((1, d), lambda b, i: (0, 0))
    bvec = _vspec((1, 1, d), lambda b, i: (b, 0, 0))

    def body(x_ref, w_ref, sc_ref, sh_ref, du_ref, dres_ref, dx_ref, dw_ref, dsc_ref, dsh_ref):
        b, i = pl.program_id(0), pl.program_id(1)
        _, vjp = jax.vjp(_pre_f, x_ref[...], w_ref[...], sc_ref[0], sh_ref[0])
        dx, dw, dsc, dsh = vjp(du_ref[...])
        dx_ref[...] = dres_ref[...] + dx

        @pl.when((b == 0) & (i == 0))
        def _():
            dw_ref[...] = jnp.zeros_like(dw_ref)

        @pl.when(i == 0)
        def _():
            dsc_ref[...] = jnp.zeros_like(dsc_ref)
            dsh_ref[...] = jnp.zeros_like(dsh_ref)

        dw_ref[...] += dw
        dsc_ref[0] += dsc
        dsh_ref[0] += dsh

    return pl.pallas_call(
        body, name=name, grid=grid, in_specs=[row, vec, bvec, bvec, row, row], out_specs=[row, vec, bvec, bvec],
        out_shape=[jax.ShapeDtypeStruct((t, d), F32), jax.ShapeDtypeStruct((1, d), F32),
                   jax.ShapeDtypeStruct((bsz, 1, d), F32), jax.ShapeDtypeStruct((bsz, 1, d), F32)],
    )(x, w, sc, sh, du, dres)


def _accumulate(ref, val, first):
    @pl.when(first)
    def _():
        ref[...] = jnp.zeros_like(ref)

    ref[...] += val.reshape(ref.shape)


def _post_pre_fwd(res, y, w_post, g, w_pre, sc, sh, name, ts=512):
    t, d = y.shape
    bsz = g.shape[0]
    nt, grid = _tok_grid(t, bsz, ts)
    row = _vspec((ts, d), lambda b, i: (b * nt + i, 0))
    vec = _vspec((1, d), lambda b, i: (0, 0))
    bvec = _vspec((1, 1, d), lambda b, i: (b, 0, 0))

    def body(res_ref, y_ref, wp_ref, g_ref, w_ref, sc_ref, sh_ref, h_ref, u_ref):
        h = res_ref[...] + _post_f(y_ref[...], wp_ref[...], g_ref[0])
        h_ref[...] = h
        u_ref[...] = _pre_f(h, w_ref[...], sc_ref[0], sh_ref[0]).astype(BF16)

    return pl.pallas_call(body, name=name, grid=grid, in_specs=[row, row, vec, bvec, vec, bvec, bvec], out_specs=[row, row],
                          out_shape=[jax.ShapeDtypeStruct((t, d), F32), jax.ShapeDtypeStruct((t, d), BF16)],
                          )(res, y, w_post, g, w_pre, sc, sh)


def _post_loss_bwd(res, y, w, g, tgt, name, ts=512):
    t, d = y.shape
    bsz = g.shape[0]
    nt, grid = _tok_grid(t, bsz, ts)
    row = _vspec((ts, d), lambda b, i: (b * nt + i, 0))
    vec = _vspec((1, d), lambda b, i: (0, 0))
    bvec = _vspec((1, 1, d), lambda b, i: (b, 0, 0))

    def body(res_ref, y_ref, w_ref, g_ref, tgt_ref, dh_ref, dy_ref, dw_ref, dg_ref, sq_ref):
        b, i = pl.program_id(0), pl.program_id(1)
        part, vjp = jax.vjp(_post_f, y_ref[...], w_ref[...], g_ref[0])
        e = res_ref[...] + part - tgt_ref[...]
        dh = e * (1.0 / d)
        dh_ref[...] = dh
        dy, dw, dg = vjp(dh)
        dy_ref[...] = dy.astype(BF16)
        _accumulate(dw_ref, dw, (b == 0) & (i == 0))
        _accumulate(dg_ref, dg, i == 0)
        _accumulate(sq_ref, jnp.sum(e * e, axis=0, keepdims=True) * (1.0 / d), (b == 0) & (i == 0))

    return pl.pallas_call(
        body, name=name, grid=grid, in_specs=[row, row, vec, bvec, row], out_specs=[row, row, vec, bvec, vec],
        out_shape=[jax.ShapeDtypeStruct((t, d), F32), jax.ShapeDtypeStruct((t, d), BF16), jax.ShapeDtypeStruct((1, d), F32),
                   jax.ShapeDtypeStruct((bsz, 1, d), F32), jax.ShapeDtypeStruct((1, d), F32)],
    )(res, y, w, g, tgt)


def _pre_post_bwd(x, w, sc, sh, du, dres, y, w_post, g, name, ts=512):
    t, d = x.shape
    bsz = sc.shape[0]
    nt, grid = _tok_grid(t, bsz, ts)
    row = _vspec((ts, d), lambda b, i: (b * nt + i, 0))
    vec = _vspec((1, d), lambda b, i: (0, 0))
    bvec = _vspec((1, 1, d), lambda b, i: (b, 0, 0))

    def body(x_ref, w_ref, sc_ref, sh_ref, du_ref, dres_ref, y_ref, wp_ref, g_ref,
             dx_ref, dy_ref, dw_ref, dsc_ref, dsh_ref, dwp_ref, dg_ref):
        b, i = pl.program_id(0), pl.program_id(1)
        _, vjp = jax.vjp(_pre_f, x_ref[...], w_ref[...], sc_ref[0], sh_ref[0])
        dx, dw, dsc, dsh = vjp(du_ref[...])
        dx = dres_ref[...] + dx
        dx_ref[...] = dx
        _, vjp_post = jax.vjp(_post_f, y_ref[...], wp_ref[...], g_ref[0])
        dy, dwp, dg = vjp_post(dx)
        dy_ref[...] = dy.astype(BF16)
        first = (b == 0) & (i == 0)
        _accumulate(dw_ref, dw, first)
        _accumulate(dwp_ref, dwp, first)
        _accumulate(dsc_ref, dsc, i == 0)
        _accumulate(dsh_ref, dsh, i == 0)
        _accumulate(dg_ref, dg, i == 0)

    v1, vb = jax.ShapeDtypeStruct((1, d), F32), jax.ShapeDtypeStruct((bsz, 1, d), F32)
    return pl.pallas_call(
        body, name=name, grid=grid, in_specs=[row, vec, bvec, bvec, row, row, row, vec, bvec],
        out_specs=[row, row, vec, bvec, bvec, vec, bvec],
        out_shape=[jax.ShapeDtypeStruct((t, d), F32), jax.ShapeDtypeStruct((t, d), BF16), v1, vb, vb, v1, vb],
    )(x, w, sc, sh, du, dres, y, w_post, g)


def _merge_f(ga, gd, ya, yd):
    return jax.nn.sigmoid(ga) * ya + jax.nn.sigmoid(gd) * yd


_MW = 512


def _merge_fwd(proj, ya, yd, ts=512):
    t, d = ya.shape
    blk = _vspec((ts, _MW), lambda i, j: (i, j))
    ga = _vspec((ts, _MW), lambda i, j: (i, C_GA // _MW + j))
    gd = _vspec((ts, _MW), lambda i, j: (i, C_GD // _MW + j))

    def body(ga_ref, gd_ref, ya_ref, yd_ref, o_ref):
        o_ref[...] = _merge_f(ga_ref[...], gd_ref[...], ya_ref[...], yd_ref[...]).astype(BF16)

    return pl.pallas_call(body, name="merge_fwd", grid=(t // ts, d // _MW), in_specs=[ga, gd, blk, blk], out_specs=blk,
                          out_shape=jax.ShapeDtypeStruct((t, d), BF16))(proj, proj, ya, yd)


def _merge_bwd(proj, ya, yd, dm, ts=512):
    t, d = ya.shape
    blk = _vspec((ts, _MW), lambda i, j: (i, j))
    ga = _vspec((ts, _MW), lambda i, j: (i, C_GA // _MW + j))
    gd = _vspec((ts, _MW), lambda i, j: (i, C_GD // _MW + j))

    def body(ga_ref, gd_ref, ya_ref, yd_ref, dm_ref, dga_ref, dgd_ref, dya_ref, dyd_ref):
        _, vjp = jax.vjp(_merge_f, ga_ref[...], gd_ref[...], ya_ref[...], yd_ref[...])
        dga, dgd, dya, dyd = vjp(dm_ref[...])
        dga_ref[...] = dga.astype(BF16)
        dgd_ref[...] = dgd.astype(BF16)
        dya_ref[...] = dya.astype(BF16)
        dyd_ref[...] = dyd.astype(BF16)

    o = jax.ShapeDtypeStruct((t, d), BF16)
    return pl.pallas_call(body, name="merge_bwd", grid=(t // ts, d // _MW), in_specs=[ga, gd, blk, blk, blk],
                          out_specs=[blk] * 4, out_shape=[o] * 4)(proj, proj, ya, yd, dm)


def _shift_down(x, s):
    if s == 0:
        return x
    r = lax.broadcasted_iota(jnp.int32, x.shape, 0)
    return jnp.where(r >= s, pltpu.roll(x, s, 0), 0.0)


def _shift_up(x, s):
    if s == 0:
        return x
    n = x.shape[0]
    r = lax.broadcasted_iota(jnp.int32, x.shape, 0)
    return jnp.where(r < n - s, pltpu.roll(x, n - s, 0), 0.0)


def _conv_fwd(x, w, k):
    out = None
    for j in range(k):
        term = w[j:j + 1, :] * _shift_down(x, k - 1 - j)
        out = term if out is None else out + term
    return out


def _conv_bwd(x, w, dc, k):
    dx = None
    dws = []
    for j in range(k):
        up = _shift_up(dc, k - 1 - j)
        term = w[j:j + 1, :] * up
        dx = term if dx is None else dx + term
        dws.append(jnp.sum(up * x, axis=0, keepdims=True))
    return dx, jnp.concatenate(dws, axis=0)


def _geglu_f(gate, val):
    return jax.nn.gelu(gate, approximate=True) * val


_FW = 256


def _ffn_act_fwd(up, conv_w, bsz):
    t = up.shape[0]
    s = t // bsz
    nj = D_FF // _FW
    xg = _vspec((s, _FW), lambda b, j: (b, j))
    xv = _vspec((s, _FW), lambda b, j: (b, nj + j))
    wg = _vspec((FFN_CONV, _FW), lambda b, j: (0, j))
    wv = _vspec((FFN_CONV, _FW), lambda b, j: (0, nj + j))

    def body(xg_ref, xv_ref, wg_ref, wv_ref, o_ref):
        gate = _conv_fwd(xg_ref[...], wg_ref[...], FFN_CONV)
        val = _conv_fwd(xv_ref[...], wv_ref[...], FFN_CONV)
        o_ref[...] = _geglu_f(gate, val).astype(BF16)

    return pl.pallas_call(body, name="ffn_act_fwd", grid=(bsz, nj), in_specs=[xg, xv, wg, wv],
                          out_specs=_vspec((s, _FW), lambda b, j: (b, j)),
                          out_shape=jax.ShapeDtypeStruct((t, D_FF), BF16))(up, up, conv_w, conv_w)


def _ffn_act_bwd(up, conv_w, dact, bsz, plug=None):
    t = up.shape[0]
    s = t // bsz
    nj = D_FF // _FW
    xg = _vspec((s, _FW), lambda j, b: (b, j))
    xv = _vspec((s, _FW), lambda j, b: (b, nj + j))
    wg = _vspec((FFN_CONV, _FW), lambda j, b: (0, j))
    wv = _vspec((FFN_CONV, _FW), lambda j, b: (0, nj + j))
    da = _vspec((s, _FW), lambda j, b: (b, j))
    dwo = _vspec((FFN_CONV, _FW), lambda j, b: (0, j))

    def body(xg_ref, xv_ref, wg_ref, wv_ref, da_ref, dxg_ref, dxv_ref, dwg_ref, dwv_ref):
        b = pl.program_id(1)
        xg_, xv_, wg_, wv_ = xg_ref[...], xv_ref[...], wg_ref[...], wv_ref[...]
        gate = _conv_fwd(xg_, wg_, FFN_CONV)
        val = _conv_fwd(xv_, wv_, FFN_CONV)
        _, vjp = jax.vjp(_geglu_f, gate, val)
        dgate, dval = vjp(da_ref[...])
        dxg, dwg = _conv_bwd(xg_, wg_, dgate, FFN_CONV)
        dxv, dwv = _conv_bwd(xv_, wv_, dval, FFN_CONV)
        dxg_ref[...] = dxg.astype(BF16)
        dxv_ref[...] = dxv.astype(BF16)

        @pl.when(b == 0)
        def _():
            dwg_ref[...] = jnp.zeros_like(dwg_ref)
            dwv_ref[...] = jnp.zeros_like(dwv_ref)

        dwg_ref[...] += dwg
        dwv_ref[...] += dwv

    outs, extra = _plugged_call(
        body, plug, _grid_ends((nj, bsz)), (up, up, conv_w, conv_w, dact), name="ffn_act_bwd", grid=(nj, bsz),
        in_specs=[xg, xv, wg, wv, da], out_specs=[da, da, dwo, dwo],
        out_shape=[jax.ShapeDtypeStruct((t, D_FF), BF16), jax.ShapeDtypeStruct((t, D_FF), BF16),
                   jax.ShapeDtypeStruct((FFN_CONV, D_FF), F32), jax.ShapeDtypeStruct((FFN_CONV, D_FF), F32)])
    return (*outs, extra)


def _bucket_table():
    qi = np.arange(WINDOW)[:, None]
    kj = np.arange(2 * WINDOW)[None, :]
    dist = WINDOW + qi - kj
    dc = np.maximum(dist, 0)
    max_exact = REL_BUCKETS // 2
    scaled = np.log(np.maximum(dc, 1).astype(np.float32) / np.float32(max_exact)) / np.float32(math.log(REL_MAX_DIST / max_exact))
    large = max_exact + (scaled.astype(np.float32) * np.float32(REL_BUCKETS - max_exact)).astype(np.int32)
    large = np.minimum(large, REL_BUCKETS - 1)
    bucket = np.where(dc < max_exact, dc, large).astype(np.int32)
    in_band = ((dist >= 0) & (dist < WINDOW)).astype(np.int32)
    return bucket, in_band


def _bias_build(rel_bias):
    bucket, _ = _bucket_table()

    def body(rb_ref, idx_ref, o_ref):
        h = pl.program_id(0)
        idx = idx_ref[...]
        acc = jnp.zeros(idx.shape, F32)
        for r in range(REL_BUCKETS):
            acc = jnp.where(idx == r, rb_ref[r, h], acc)
        o_ref[0] = acc

    return pl.pallas_call(
        body, name="bias_build", grid=(AQ_HEADS,),
        in_specs=[pl.BlockSpec(memory_space=pltpu.SMEM), _vspec((WINDOW, 2 * WINDOW), lambda h: (0, 0))],
        out_specs=_vspec((1, WINDOW, 2 * WINDOW), lambda h: (h, 0, 0)),
        out_shape=jax.ShapeDtypeStruct((AQ_HEADS, WINDOW, 2 * WINDOW), F32),
    )(rel_bias, jnp.asarray(bucket))


def _bias_grad(dbias):
    bucket, _ = _bucket_table()

    def body(db_ref, idx_ref, o_ref):
        idx = idx_ref[...]
        db = db_ref[0]
        lane = lax.broadcasted_iota(jnp.int32, (1, LANES), 1)
        acc = jnp.zeros((1, LANES), F32)
        for r in range(REL_BUCKETS):
            s = jnp.sum(jnp.sum(jnp.where(idx == r, db, 0.0), axis=1, keepdims=True), axis=0, keepdims=True)
            acc = jnp.where(lane == r, s, acc)
        o_ref[0] = acc

    return pl.pallas_call(
        body, name="bias_grad", grid=(AQ_HEADS,),
        in_specs=[_vspec((1, WINDOW, 2 * WINDOW), lambda h: (h, 0, 0)), _vspec((WINDOW, 2 * WINDOW), lambda h: (0, 0))],
        out_specs=_vspec((1, 1, LANES), lambda h: (h, 0, 0)),
        out_shape=jax.ShapeDtypeStruct((AQ_HEADS, 1, LANES), F32),
    )(dbias, jnp.asarray(bucket))


def _attn_mask(n):
    qi = lax.broadcasted_iota(jnp.int32, (WINDOW, 2 * WINDOW), 0)
    kj = lax.broadcasted_iota(jnp.int32, (WINDOW, 2 * WINDOW), 1)
    dist = WINDOW + qi - kj
    band = (dist >= 0) & (dist < WINDOW)
    return band & ((kj >= WINDOW) | (n > 0))


def _attn_probs(qk, bias, sink, mask):
    s = jnp.where(mask, qk * (A_HD ** -0.5) + bias, NEG_INF)
    m = jnp.maximum(jnp.max(s, axis=-1, keepdims=True), sink)
    p = jnp.exp(s - m)
    es = jnp.exp(sink - m)
    inv = 1.0 / (jnp.sum(p, axis=-1, keepdims=True) + es)
    return p * inv, es * inv


def _attn_fwd(proj, bias, sinks, bsz):
    t = proj.shape[0]
    s = t // bsz
    nb = s // WINDOW
    grp = AQ_HEADS // AKV_HEADS

    def body(q_ref, k_ref, v_ref, bias_ref, sink_ref, y_ref, kp_ref, vp_ref):
        kp_ref[0:WINDOW, :] = jnp.zeros((WINDOW, LANES), BF16)
        vp_ref[0:WINDOW, :] = jnp.zeros((WINDOW, LANES), BF16)
        kp_ref[WINDOW:, :] = k_ref[...].astype(BF16)
        vp_ref[WINDOW:, :] = v_ref[...].astype(BF16)

        def blk(n, carry):
            r0 = pl.multiple_of(n * WINDOW, WINDOW)
            mask = _attn_mask(n)
            kband = kp_ref[pl.ds(r0, 2 * WINDOW), :]
            vband = vp_ref[pl.ds(r0, 2 * WINDOW), :]
            qb = q_ref[pl.ds(r0, WINDOW), :].astype(BF16)
            heads = range(AQ_HEADS)
            hsl = lambda h: slice(h * A_HD, (h + 1) * A_HD)
            kbs = [kband[:, hsl(kv)] for kv in range(AKV_HEADS)]
            vbs = [vband[:, hsl(kv)] for kv in range(AKV_HEADS)]
            qks = [lax.dot_general(qb[:, hsl(h)], kbs[h // grp], _NT, preferred_element_type=F32) for h in heads]
            probs = [_attn_probs(qks[h], bias_ref[h], sink_ref[0, h], mask)[0] for h in heads]
            outs = [jnp.dot(probs[h].astype(BF16), vbs[h // grp], preferred_element_type=F32) for h in heads]
            y_ref[pl.ds(r0, WINDOW), :] = jnp.concatenate(outs, axis=1).astype(BF16)
            return carry

        lax.fori_loop(0, nb, blk, 0)

    return pl.pallas_call(
        body, name="attn_fwd", grid=(bsz,),
        in_specs=[_vspec((s, AQ), lambda b: (b, C_Q // AQ)), _vspec((s, AKV), lambda b: (b, C_K // AKV)),
                  _vspec((s, AKV), lambda b: (b, C_V // AKV)),
                  _vspec((AQ_HEADS, WINDOW, 2 * WINDOW), lambda b: (0, 0, 0)), pl.BlockSpec(memory_space=pltpu.SMEM)],
        out_specs=_vspec((s, AQ), lambda b: (b, 0)), out_shape=jax.ShapeDtypeStruct((t, AQ), BF16),
        scratch_shapes=[pltpu.VMEM((s + WINDOW, LANES), BF16), pltpu.VMEM((s + WINDOW, LANES), BF16)],
    )(proj, proj, proj, bias, sinks)


def _attn_bwd(proj, bias, sinks, dy, bsz, plug=None):
    t = proj.shape[0]
    s = t // bsz
    nb = s // WINDOW
    grp = AQ_HEADS // AKV_HEADS
    scale = A_HD ** -0.5

    def body(q_ref, k_ref, v_ref, bias_ref, sink_ref, dy_ref, dq_ref, dk_ref, dv_ref, dbias_ref, dsink_ref,
             kp_ref, vp_ref, dkp_ref, dvp_ref):
        b = pl.program_id(0)
        kp_ref[0:WINDOW, :] = jnp.zeros((WINDOW, LANES), BF16)
        vp_ref[0:WINDOW, :] = jnp.zeros((WINDOW, LANES), BF16)
        kp_ref[WINDOW:, :] = k_ref[...].astype(BF16)
        vp_ref[WINDOW:, :] = v_ref[...].astype(BF16)
        dkp_ref[...] = jnp.zeros_like(dkp_ref)
        dvp_ref[...] = jnp.zeros_like(dvp_ref)

        @pl.when(b == 0)
        def _():
            dbias_ref[...] = jnp.zeros_like(dbias_ref)
            dsink_ref[...] = jnp.zeros_like(dsink_ref)

        def blk(n, carry):
            r0 = pl.multiple_of(n * WINDOW, WINDOW)
            mask = _attn_mask(n)
            kband = kp_ref[pl.ds(r0, 2 * WINDOW), :]
            vband = vp_ref[pl.ds(r0, 2 * WINDOW), :]
            qb = q_ref[pl.ds(r0, WINDOW), :].astype(BF16)
            dyb = dy_ref[pl.ds(r0, WINDOW), :].astype(BF16)
            heads = range(AQ_HEADS)
            hsl = lambda h: slice(h * A_HD, (h + 1) * A_HD)
            kbs = [kband[:, hsl(kv)] for kv in range(AKV_HEADS)]
            vbs = [vband[:, hsl(kv)] for kv in range(AKV_HEADS)]
            qhs = [qb[:, hsl(h)] for h in heads]
            dyhs = [dyb[:, hsl(h)] for h in heads]
            qks = [lax.dot_general(qhs[h], kbs[h // grp], _NT, preferred_element_type=F32) for h in heads]
            dprobs = [lax.dot_general(dyhs[h], vbs[h // grp], _NT, preferred_element_type=F32) for h in heads]
            pbs, dsbs = [], []
            for h in heads:
                probs, psink = _attn_probs(qks[h], bias_ref[h], sink_ref[0, h], mask)
                rowdot = jnp.sum(probs * dprobs[h], axis=-1, keepdims=True)
                ds = probs * (dprobs[h] - rowdot)
                dbias_ref[h] += ds
                dsink_ref[h] += jnp.sum(-psink * rowdot, axis=0, keepdims=True) + jnp.zeros((1, LANES), F32)
                pbs.append(probs.astype(BF16))
                dsbs.append(ds.astype(BF16))
            dvhs = [lax.dot_general(pbs[h], dyhs[h], _TN, preferred_element_type=F32) for h in heads]
            dqs = [jnp.dot(dsbs[h], kbs[h // grp], preferred_element_type=F32) * scale for h in heads]
            dkhs = [lax.dot_general(dsbs[h], qhs[h], _TN, preferred_element_type=F32) * scale for h in heads]
            dks = [sum(dkhs[kv * grp + 1:(kv + 1) * grp], dkhs[kv * grp]) for kv in range(AKV_HEADS)]
            dvs = [sum(dvhs[kv * grp + 1:(kv + 1) * grp], dvhs[kv * grp]) for kv in range(AKV_HEADS)]
            dq_ref[pl.ds(r0, WINDOW), :] = jnp.concatenate(dqs, axis=1).astype(BF16)
            dkp_ref[pl.ds(r0, 2 * WINDOW), :] += jnp.concatenate(dks, axis=1)
            dvp_ref[pl.ds(r0, 2 * WINDOW), :] += jnp.concatenate(dvs, axis=1)
            return carry

        lax.fori_loop(0, nb, blk, 0)
        dk_ref[...] = dkp_ref[WINDOW:, :].astype(BF16)
        dv_ref[...] = dvp_ref[WINDOW:, :].astype(BF16)

    kvs = jax.ShapeDtypeStruct((t, AKV), BF16)
    outs, extra = _plugged_call(
        body, plug, _grid_ends((bsz,)), (proj, proj, proj, bias, sinks, dy), name="attn_bwd", grid=(bsz,),
        in_specs=[_vspec((s, AQ), lambda b: (b, C_Q // AQ)), _vspec((s, AKV), lambda b: (b, C_K // AKV)),
                  _vspec((s, AKV), lambda b: (b, C_V // AKV)),
                  _vspec((AQ_HEADS, WINDOW, 2 * WINDOW), lambda b: (0, 0, 0)), pl.BlockSpec(memory_space=pltpu.SMEM),
                  _vspec((s, AQ), lambda b: (b, 0))],
        out_specs=[_vspec((s, AQ), lambda b: (b, 0)), _vspec((s, AKV), lambda b: (b, 0)), _vspec((s, AKV), lambda b: (b, 0)),
                   _vspec((AQ_HEADS, WINDOW, 2 * WINDOW), lambda b: (0, 0, 0)), _vspec((AQ_HEADS, 1, LANES), lambda b: (0, 0, 0))],
        out_shape=[jax.ShapeDtypeStruct((t, AQ), BF16), kvs, kvs,
                   jax.ShapeDtypeStruct((AQ_HEADS, WINDOW, 2 * WINDOW), F32), jax.ShapeDtypeStruct((AQ_HEADS, 1, LANES), F32)],
        scratch_shapes=[pltpu.VMEM((s + WINDOW, LANES), BF16), pltpu.VMEM((s + WINDOW, LANES), BF16),
                        pltpu.VMEM((s + WINDOW, LANES), F32), pltpu.VMEM((s + WINDOW, LANES), F32)])
    return (*outs, extra)


def _dn_act_f(c, is_qk):
    a = jax.nn.silu(c)
    outs = []
    for h in range(DN_HEADS):
        ah = a[:, h * DN_HD:(h + 1) * DN_HD]
        nh = ah * lax.rsqrt(jnp.sum(ah * ah, axis=-1, keepdims=True) + L2_EPS)
        outs.append(jnp.where(is_qk, nh, ah))
    return jnp.concatenate(outs, axis=1)


def _dn_prep_fwd(proj, conv_w, bsz):
    t = proj.shape[0]
    s = t // bsz
    blk = _vspec((s, DN), lambda b, j: (b, j))
    wsp = _vspec((DN_CONV, DN), lambda b, j: (0, j))

    def body(x_ref, w_ref, o_ref):
        j = pl.program_id(1)
        o_ref[...] = _dn_act_f(_conv_fwd(x_ref[...], w_ref[...], DN_CONV), j < 2)

    return pl.pallas_call(body, name="dn_prep_fwd", grid=(bsz, 3), in_specs=[blk, wsp], out_specs=blk,
                          out_shape=jax.ShapeDtypeStruct((t, 3 * DN), F32))(proj, conv_w)


def _dn_prep_bwd(proj, conv_w, dqkvn, bsz):
    t = proj.shape[0]
    s = t // bsz
    blk = _vspec((s, DN), lambda j, b: (b, j))
    wsp = _vspec((DN_CONV, DN), lambda j, b: (0, j))

    def body(x_ref, w_ref, d_ref, dx_ref, dw_ref):
        j, b = pl.program_id(0), pl.program_id(1)
        x, w = x_ref[...], w_ref[...]
        c = _conv_fwd(x, w, DN_CONV)
        _, vjp = jax.vjp(lambda cc: _dn_act_f(cc, j < 2), c)
        (dc,) = vjp(d_ref[0])
        dx, dw = _conv_bwd(x, w, dc, DN_CONV)
        dx_ref[...] = dx.astype(BF16)

        @pl.when(b == 0)
        def _():
            dw_ref[...] = jnp.zeros_like(dw_ref)

        dw_ref[...] += dw

    return pl.pallas_call(
        body, name="dn_prep_bwd", grid=(3, bsz),
        in_specs=[blk, wsp, _vspec((1, s, DN), lambda j, b: (j, b, 0))], out_specs=[blk, wsp],
        out_shape=[jax.ShapeDtypeStruct((t, 3 * DN), BF16), jax.ShapeDtypeStruct((DN_CONV, 3 * DN), F32)],
    )(proj, conv_w, dqkvn)


def _bg_f(x, alog, dt):
    lane = lax.broadcasted_iota(jnp.int32, x.shape, 1)
    beta = jax.nn.sigmoid(x)
    g = -jnp.exp(alog) * jax.nn.softplus(x + dt)
    return jnp.where(lane < DN_HEADS, beta, jnp.where(lane < 2 * DN_HEADS, g, 0.0))


def _bg_fwd(proj, alog, dt, bsz):
    t = proj.shape[0]
    s = t // bsz
    vec = _vspec((1, LANES), lambda b: (0, 0))

    def body(x_ref, a_ref, d_ref, o_ref):
        o_ref[...] = _bg_f(x_ref[...], a_ref[...], d_ref[...])

    return pl.pallas_call(body, name="bg_fwd", grid=(bsz,), in_specs=[_vspec((s, LANES), lambda b: (b, C_BD // LANES)), vec, vec],
                          out_specs=_vspec((s, LANES), lambda b: (b, 0)), out_shape=jax.ShapeDtypeStruct((t, LANES), F32))(proj, alog, dt)


def _bg_bwd(proj, alog, dt, dbg4, bsz):
    t = proj.shape[0]
    s = t // bsz
    vec = _vspec((1, LANES), lambda b: (0, 0))

    def body(x_ref, a_ref, d_ref, g4_ref, dx_ref, da_ref, dd_ref):
        b = pl.program_id(0)
        lane = lax.broadcasted_iota(jnp.int32, (s, LANES), 1)
        dbg = jnp.zeros((s, LANES), F32)
        for h in range(DN_HEADS):
            gh = g4_ref[:, h * DN_HD:(h + 1) * DN_HD]
            dbg = jnp.where(lane == h, gh[:, 0:1], dbg)
            dbg = jnp.where(lane == DN_HEADS + h, gh[:, 1:2], dbg)
        _, vjp = jax.vjp(_bg_f, x_ref[...], a_ref[...], d_ref[...])
        dx, da, dd = vjp(dbg)
        dx_ref[...] = dx.astype(BF16)

        @pl.when(b == 0)
        def _():
            da_ref[...] = jnp.zeros_like(da_ref)
            dd_ref[...] = jnp.zeros_like(dd_ref)

        da_ref[...] += da
        dd_ref[...] += dd

    return pl.pallas_call(
        body, name="bg_bwd", grid=(bsz,),
        in_specs=[_vspec((s, LANES), lambda b: (b, C_BD // LANES)), vec, vec, _vspec((s, DN), lambda b: (b, 0))],
        out_specs=[_vspec((s, LANES), lambda b: (b, 0)), vec, vec],
        out_shape=[jax.ShapeDtypeStruct((t, LANES), BF16), jax.ShapeDtypeStruct((1, LANES), F32), jax.ShapeDtypeStruct((1, LANES), F32)],
    )(proj, alog, dt, dbg4)


def _dn_out_f(o, z, w):
    outs = []
    for h in range(DN_HEADS):
        sl = slice(h * DN_HD, (h + 1) * DN_HD)
        outs.append(_rms(o[:, sl], w) * jax.nn.silu(z[:, sl]))
    return jnp.concatenate(outs, axis=1)


def _dn_out_fwd(o, proj, w, ts=512):
    t = o.shape[0]
    blk = _vspec((ts, DN), lambda i: (i, 0))
    zsp = _vspec((ts, DN), lambda i: (i, C_DZ // DN))
    vec = _vspec((1, DN_HD), lambda i: (0, 0))

    def body(o_ref, z_ref, w_ref, y_ref):
        y_ref[...] = _dn_out_f(o_ref[...], z_ref[...], w_ref[...]).astype(BF16)

    return pl.pallas_call(body, name="dn_out_fwd", grid=(t // ts,), in_specs=[blk, zsp, vec], out_specs=blk,
                          out_shape=jax.ShapeDtypeStruct((t, DN), BF16))(o, proj, w)


def _dn_out_bwd(o, proj, w, dy, ts=512):
    t = o.shape[0]
    blk = _vspec((ts, DN), lambda i: (i, 0))
    zsp = _vspec((ts, DN), lambda i: (i, C_DZ // DN))
    vec = _vspec((1, DN_HD), lambda i: (0, 0))

    def body(o_ref, z_ref, w_ref, dy_ref, do_ref, dz_ref, dw_ref):
        i = pl.program_id(0)
        _, vjp = jax.vjp(_dn_out_f, o_ref[...], z_ref[...], w_ref[...])
        do, dz, dw = vjp(dy_ref[...])
        do_ref[...] = do
        dz_ref[...] = dz.astype(BF16)

        @pl.when(i == 0)
        def _():
            dw_ref[...] = jnp.zeros_like(dw_ref)

        dw_ref[...] += dw

    return pl.pallas_call(
        body, name="dn_out_bwd", grid=(t // ts,), in_specs=[blk, zsp, vec, blk], out_specs=[blk, blk, vec],
        out_shape=[jax.ShapeDtypeStruct((t, DN), F32), jax.ShapeDtypeStruct((t, DN), BF16), jax.ShapeDtypeStruct((1, DN_HD), F32)],
    )(o, proj, w, dy)


_C = DN_CHUNK


def _dot(a, b, dims):
    return lax.dot_general(a.astype(BF16), b.astype(BF16), dims, preferred_element_type=F32)


def _split(a):
    hi = a.astype(BF16)
    return hi, (a - hi.astype(F32)).astype(BF16)


def _dot3(a, b, dims):
    (ah, al), (bh, bl) = (a if isinstance(a, tuple) else _split(a)), (b if isinstance(b, tuple) else _split(b))
    mm = lambda x, y: lax.dot_general(x, y, dims, preferred_element_type=F32)
    return mm(ah, bh) + (mm(ah, bl) + mm(al, bh))


_NN = (((1,), (0,)), ((), ()))
_NT = (((1,), (1,)), ((), ()))
_TN = (((0,), (0,)), ((), ()))


_SUB = 8


def _tri_inverses(ls, lts):
    ri8 = lax.broadcasted_iota(jnp.int32, (_SUB, _C), 0)
    ci8 = lax.broadcasted_iota(jnp.int32, (_SUB, _C), 1)
    nblk = _C // _SUB
    ts = []
    for lt in lts:
        blocks = [jnp.where(ci8 == ri8 + _SUB * b, 1.0, 0.0).astype(F32) for b in range(nblk)]
        for r in range(1, _SUB):
            for b in range(nblk):
                coef = lt[_SUB * b:_SUB * (b + 1), _SUB * b + r:_SUB * b + r + 1]
                row = jnp.sum(coef * blocks[b], axis=0, keepdims=True)
                blocks[b] = jnp.where(ri8 == r, blocks[b] - row, blocks[b])
        ts.append(jnp.concatenate(blocks, axis=0))
    ri = lax.broadcasted_iota(jnp.int32, (_C, _C), 0)
    ci = lax.broadcasted_iota(jnp.int32, (_C, _C), 1)
    s = _SUB
    while s < _C:
        shift = s.bit_length()
        quad = ((ri >> shift) == (ci >> shift)) & ((ri & s) != 0) & ((ci & s) == 0)
        offs = [jnp.where(quad, l, 0.0) for l in ls]
        tsp = [_split(t) for t in ts]
        left = [_dot3(tp, off, _NN) for tp, off in zip(tsp, offs)]
        ts = [t - _dot3(lo, tp, _NN) for t, lo, tp in zip(ts, left, tsp)]
        s *= 2
    return ts


_SEG = 512
_HEADS = tuple(range(DN_HEADS))


def _hsl(hh):
    return slice(hh * DN_HD, (hh + 1) * DN_HD)


def _chunk_specs(bsz, nseg, reverse):
    seg = (lambda i: nseg - 1 - i) if reverse else (lambda i: i)
    ncs = _SEG // _C
    col = lambda off: _vspec((bsz, _SEG, DN), lambda i: (0, seg(i), off))
    return (col, _vspec((bsz, _SEG, LANES), lambda i: (0, seg(i), 0)),
            _vspec((bsz, DN_HEADS, ncs, _C), lambda i: (0, 0, seg(i), 0)),
            _vspec((bsz, DN_HEADS, ncs, DN_HD, DN_HD), lambda i: (0, 0, seg(i), 0, 0)),
            _vspec((bsz, DN_HEADS, ncs, _C, _C), lambda i: (0, 0, seg(i), 0, 0)))


def _chunk_pre(q_ref, k_ref, v_ref, bg_ref, gr_ref, c, bb, hh):
    r0 = pl.multiple_of(c * _C, _C)
    ri = lax.broadcasted_iota(jnp.int32, (_C, _C), 0)
    ci = lax.broadcasted_iota(jnp.int32, (_C, _C), 1)
    q = q_ref[bb, pl.ds(r0, _C), _hsl(hh)] * (DN_HD ** -0.5)
    k = k_ref[bb, pl.ds(r0, _C), _hsl(hh)]
    v = v_ref[bb, pl.ds(r0, _C), _hsl(hh)]
    bgc = bg_ref[bb, pl.ds(r0, _C), :]
    beta = bgc[:, hh:hh + 1]
    g_col = bgc[:, DN_HEADS + hh:DN_HEADS + hh + 1]
    g_row = gr_ref[bb, hh, pl.ds(c, 1), :]
    gc_col = jnp.sum(jnp.where(ri >= ci, g_row, 0.0), axis=1, keepdims=True)
    gc_row = jnp.sum(jnp.where(ri <= ci, g_col, 0.0), axis=0, keepdims=True)
    gc_last = jnp.sum(g_col, axis=0, keepdims=True)
    diff = gc_col - gc_row
    decay = jnp.where(ri >= ci, jnp.exp(jnp.where(ri >= ci, diff, 0.0)), 0.0)
    diff_t = gc_row - gc_col
    decay_t = jnp.where(ri <= ci, jnp.exp(jnp.where(ri <= ci, diff_t, 0.0)), 0.0)
    eg = jnp.exp(gc_col)
    et = jnp.exp(gc_last - gc_col)
    gl = jnp.exp(gc_last)
    kb = k * beta
    vb = v * beta
    return dict(r0=r0, bb=bb, hh=hh, q=q, k=k, v=v, beta=beta, decay=decay, decay_t=decay_t, eg=eg, et=et, gl=gl, kb=kb, vb=vb,
                ri=ri, ci=ci)


def _chunk_solve(ms, tms=None):
    for m in ms:
        m["kk_t"] = _dot(m["k"], m["kb"], _NT)
        m["qk"] = _dot(m["q"], m["k"], _NT)
        m["kk"] = _dot(m["kb"], m["k"], _NT)
        if tms is not None:
            m["qk_t"] = _dot(m["k"], m["q"], _NT)
    if tms is None:
        tms = _tri_inverses([jnp.where(m["ri"] > m["ci"], m["kk"] * m["decay"], 0.0) for m in ms],
                            [jnp.where(m["ri"] < m["ci"], m["kk_t"] * m["decay_t"], 0.0) for m in ms])
    for m, tm in zip(ms, tms):
        m["tm_f32"] = tm
    for m in ms:
        rhs = jnp.concatenate([m["vb"], m["kb"] * m["eg"]], axis=1)
        m["tm"] = _split(m["tm_f32"])
        m["sol"] = _dot3(m["tm"], rhs, _NN)
        m["intra"] = jnp.where(m["ri"] >= m["ci"], m["qk"] * m["decay"], 0.0)


def _dn_chunk_fwd(qkvn, bg, g_rows, bsz, plug=None):
    t = qkvn.shape[0]
    s = t // bsz
    nc, nseg = s // _C, s // _SEG
    pairs = [(bb, hh) for bb in range(bsz) for hh in _HEADS]

    def body(q_ref, k_ref, v_ref, bg_ref, gr_ref, o_ref, st_ref, tm_ref, s_ref):
        @pl.when(pl.program_id(0) == 0)
        def _():
            s_ref[...] = jnp.zeros_like(s_ref)

        def chunk(c, carry):
            ms = [_chunk_pre(q_ref, k_ref, v_ref, bg_ref, gr_ref, c, bb, hh) for bb, hh in pairs]
            _chunk_solve(ms)
            sts = [s_ref[i] for i in range(len(pairs))]
            for m, st in zip(ms, sts):
                st_ref[m["bb"], m["hh"], c] = st
                tm_ref[m["bb"], m["hh"], c] = m["tm_f32"]
            ws = [_dot(m["sol"][:, DN_HD:], st, _NN) for m, st in zip(ms, sts)]
            qs = [_dot(m["q"] * m["eg"], st, _NN) for m, st in zip(ms, sts)]
            v_new = [m["sol"][:, :DN_HD] - a for m, a in zip(ms, ws)]
            iv = [_dot(m["intra"], vn, _NN) for m, vn in zip(ms, v_new)]
            upd = [_dot(m["k"] * m["et"], vn, _TN) for m, vn in zip(ms, v_new)]
            for i, (bb, hh) in enumerate(pairs):
                s_ref[i] = sts[i] * ms[i]["gl"] + upd[i]
                o_ref[bb, pl.ds(ms[i]["r0"], _C), _hsl(hh)] = qs[i] + iv[i]
            return carry

        lax.fori_loop(0, _SEG // _C, chunk, 0)

    col, bgs, grs, sts_spec, tms_spec = _chunk_specs(bsz, nseg, False)
    q3, bg3 = qkvn.reshape(bsz, s, 3 * DN), bg.reshape(bsz, s, LANES)
    (o, states, tms), extra = _plugged_call(
        body, plug, _grid_ends((nseg,)), (q3, q3, q3, bg3, g_rows), name="dn_chunk_fwd", grid=(nseg,),
        in_specs=[col(0), col(1), col(2), bgs, grs], out_specs=[col(0), sts_spec, tms_spec],
        out_shape=[jax.ShapeDtypeStruct((bsz, s, DN), F32), jax.ShapeDtypeStruct((bsz, DN_HEADS, nc, DN_HD, DN_HD), F32),
                   jax.ShapeDtypeStruct((bsz, DN_HEADS, nc, _C, _C), F32)],
        scratch_shapes=[pltpu.VMEM((bsz * DN_HEADS, DN_HD, DN_HD), F32)])
    return o.reshape(t, DN), (states, tms), extra


def _dn_chunk_bwd(qkvn, bg, g_rows, states, do, bsz, plug=None):
    t = qkvn.shape[0]
    s = t // bsz
    nc, nseg = s // _C, s // _SEG
    pairs = [(bb, hh) for bb in range(bsz) for hh in _HEADS]

    def body(q_ref, k_ref, v_ref, bg_ref, gr_ref, st_ref, tm_ref, do_ref, dqkv_ref, dbg_ref, ds_ref):
        @pl.when(pl.program_id(0) == 0)
        def _():
            ds_ref[...] = jnp.zeros_like(ds_ref)

        def chunk(cc, carry):
            c = _SEG // _C - 1 - cc
            ms = [_chunk_pre(q_ref, k_ref, v_ref, bg_ref, gr_ref, c, bb, hh) for bb, hh in pairs]
            _chunk_solve(ms, [tm_ref[bb, hh, c] for bb, hh in pairs])
            ri, ci = ms[0]["ri"], ms[0]["ci"]
            for i, m in enumerate(ms):
                m["st"] = st_ref[m["bb"], m["hh"], c]
                m["ds_out"] = ds_ref[i]
                m["do"] = do_ref[m["bb"], pl.ds(m["r0"], _C), _hsl(m["hh"])]
                m["w"] = m["sol"][:, DN_HD:]
            for m in ms:
                m["v_new"] = m["sol"][:, :DN_HD] - _dot(m["w"], m["st"], _NN)
            for m in ms:
                m["q_dec"], m["k_tail"] = m["q"] * m["eg"], m["k"] * m["et"]
                m["dk_tail"] = _dot(m["v_new"], m["ds_out"], _NT)
                m["dv_new"] = _dot(m["k_tail"], m["ds_out"], _NN) + _dot(m["intra"], m["do"], _TN)
                m["dq_dec"] = _dot(m["do"], m["st"], _NT)
                m["ds_in"] = m["ds_out"] * m["gl"] + _dot(m["q_dec"], m["do"], _TN)
                m["dintra"] = jnp.where(ri >= ci, _dot(m["do"], m["v_new"], _NT), 0.0)
                m["dintra_t"] = jnp.where(ri <= ci, _dot(m["v_new"], m["do"], _NT), 0.0)
            for m in ms:
                m["dw"] = -_dot(m["dv_new"], m["st"], _NT)
                m["ds_in"] = m["ds_in"] - _dot(m["w"], m["dv_new"], _TN)
            for m in ms:
                dsol = jnp.concatenate([m["dv_new"], m["dw"]], axis=1)
                m["drhs"] = _dot3(m["tm"], dsol, _TN)
            for m in ms:
                m["dl"] = jnp.where(ri > ci, -_dot(m["drhs"], m["sol"], _NT), 0.0)
                m["dl_t"] = jnp.where(ri < ci, -_dot(m["sol"], m["drhs"], _NT), 0.0)
            for m in ms:
                m["dkb2"] = _dot(m["dl"] * m["decay"], m["k"], _NN)
                m["dk"] = _dot(m["dl_t"] * m["decay_t"], m["kb"], _NN) + _dot(m["dintra_t"] * m["decay_t"], m["q"], _NN)
                m["dq"] = _dot(m["dintra"] * m["decay"], m["k"], _NN)
            for m in ms:
                _chunk_bwd_finish(m)
            for m in ms:
                ones_ge = jnp.where(ri <= ci, 1.0, 0.0).astype(BF16)
                gh, gl_ = _split(m["dgc"] + jnp.zeros((_C, LANES), F32))
                m["dg_b"] = jnp.dot(ones_ge, gh, preferred_element_type=F32) + jnp.dot(ones_ge, gl_, preferred_element_type=F32)
            lane = lax.broadcasted_iota(jnp.int32, (_C, LANES), 1)
            for i, m in enumerate(ms):
                bb, hh, rows = m["bb"], m["hh"], pl.ds(m["r0"], _C)
                dqkv_ref[0, bb, rows, _hsl(hh)] = m["dq"] * (DN_HD ** -0.5)
                dqkv_ref[1, bb, rows, _hsl(hh)] = m["dk"]
                dqkv_ref[2, bb, rows, _hsl(hh)] = m["dv"]
                dbg_ref[bb, rows, _hsl(hh)] = jnp.where(lane == 0, m["dbeta"], jnp.where(lane == 1, m["dg_b"], 0.0))
                ds_ref[i] = m["ds_in"]
            return carry

        lax.fori_loop(0, _SEG // _C, chunk, 0)

    col, bgs, grs, sts_spec, tms_spec = _chunk_specs(bsz, nseg, True)
    q3, bg3, do3 = qkvn.reshape(bsz, s, 3 * DN), bg.reshape(bsz, s, LANES), do.reshape(bsz, s, DN)
    (dqkv, dbg), extra = _plugged_call(
        body, plug, _grid_ends((nseg,)), (q3, q3, q3, bg3, g_rows, *states, do3), name="dn_chunk_bwd", grid=(nseg,),
        in_specs=[col(0), col(1), col(2), bgs, grs, sts_spec, tms_spec, col(0)],
        out_specs=[_vspec((3, bsz, _SEG, DN), lambda i: (0, 0, nseg - 1 - i, 0)), col(0)],
        out_shape=[jax.ShapeDtypeStruct((3, bsz, s, DN), F32), jax.ShapeDtypeStruct((bsz, s, DN), F32)],
        scratch_shapes=[pltpu.VMEM((bsz * DN_HEADS, DN_HD, DN_HD), F32)])
    return dqkv.reshape(3, t, DN), dbg.reshape(t, DN), extra


def _chunk_bwd_finish(m):
    q, k, v, beta, decay, decay_t = m["q"], m["k"], m["v"], m["beta"], m["decay"], m["decay_t"]
    eg, et, gl, kb, dl, dl_t, dintra, dintra_t = m["eg"], m["et"], m["gl"], m["kb"], m["dl"], m["dl_t"], m["dintra"], m["dintra_t"]
    dq_dec, dk_tail, dq, dk = m["dq_dec"], m["dk_tail"], m["dq"], m["dk"]
    dgl = jnp.sum(jnp.sum(m["ds_out"] * m["st"], axis=1, keepdims=True), axis=0, keepdims=True)
    dvb, dkbeg = m["drhs"][:, :DN_HD], m["drhs"][:, DN_HD:]
    dkb = dkbeg * eg + m["dkb2"]
    deg = jnp.sum(dkbeg * kb, axis=1, keepdims=True)
    em = (dl * m["kk"] + dintra * m["qk"]) * decay
    em_t = (dl_t * m["kk_t"] + dintra_t * m["qk_t"]) * decay_t
    dgc = jnp.sum(em, axis=1, keepdims=True) - jnp.sum(em_t, axis=1, keepdims=True)
    dq = dq + dq_dec * eg
    deg = deg + jnp.sum(dq_dec * q, axis=1, keepdims=True)
    dk = dk + dk_tail * et
    det = jnp.sum(dk_tail * k, axis=1, keepdims=True)
    dgc = dgc + deg * eg - det * et
    dgc_last = jnp.sum(det * et, axis=0, keepdims=True) + dgl * gl
    rcol = lax.broadcasted_iota(jnp.int32, (_C, 1), 0)
    m["dgc"] = dgc + jnp.where(rcol == _C - 1, dgc_last, 0.0)
    m["dq"] = dq
    m["dk"] = dk + dkb * beta
    m["dbeta"] = jnp.sum(dkb * k, axis=1, keepdims=True) + jnp.sum(dvb * v, axis=1, keepdims=True)
    m["dv"] = dvb * beta


def _mod_fwd(c_all, ada_w_loc, ada_b_loc):
    n, cols = c_all.shape[0], ada_w_loc.shape[1]

    def body(c_ref, w_ref, b_ref, o_ref):
        o_ref[...] = _dot(jax.nn.silu(c_ref[...]), w_ref[...], _NN) + b_ref[...]

    return pl.pallas_call(body, name="mod_fwd", out_shape=jax.ShapeDtypeStruct((n, cols), F32))(c_all, ada_w_loc, ada_b_loc)


def _ada_grad(c_all, dmod_loc, dmod_all):
    d, cols = c_all.shape[1], dmod_loc.shape[1]

    def body(c_ref, dl_ref, da_ref, gw_ref, gb_ref):
        gw_ref[...] = _dot(jax.nn.silu(c_ref[...]), dl_ref[...], _TN)
        gb_ref[...] = jnp.sum(da_ref[...], axis=0, keepdims=True)

    return pl.pallas_call(body, name="ada_grad", out_shape=[jax.ShapeDtypeStruct((d, cols), F32),
                                                           jax.ShapeDtypeStruct((1, dmod_all.shape[1]), F32)])(c_all, dmod_loc, dmod_all)


ELEMENTWISE_BLOCK_BYTES = 3 * 2 ** 19


def _row_tile(r, c=1024):
    fits = [tr for tr in range(16, r + 1, 16) if tr * c * 4 <= ELEMENTWISE_BLOCK_BYTES]
    if not fits:
        return r
    whole = [tr for tr in fits if r % tr == 0]
    return whole[-1] if whole else fits[-1]


def _adamw(w, m, v, grads, name):
    r, rest = w.shape[0], w.shape[1:]
    c = math.prod(rest)
    tr = _row_tile(r, c)
    blk = _vspec((tr,) + rest, lambda i: (i,) + (0,) * len(rest))
    n = len(grads)

    def body(*refs):
        w_ref, m_ref, v_ref = refs[:3]
        g_ref, d_ref, mo_ref, vo_ref = refs[3 + n:]
        g = refs[3][...]
        for p in refs[4:3 + n]:
            g = g + p[...]
        g_ref[...] = g
        d_ref[...], mo_ref[...], vo_ref[...] = _adamw_math(w_ref[...], m_ref[...], v_ref[...], g)

    o = jax.ShapeDtypeStruct(w.shape, F32)
    return pl.pallas_call(body, name=name, grid=(pl.cdiv(r, tr),), in_specs=[blk] * (3 + n), out_specs=[blk] * 4,
                          out_shape=[o] * 4)(w, m, v, *grads)


def _adamw_math(w, m, v, g):
    m_new = ADAM_B1 * m + (1.0 - ADAM_B1) * g
    v_new = ADAM_B2 * v + (1.0 - ADAM_B2) * jnp.square(g)
    m_hat = m_new / (1.0 - ADAM_B1 ** ADAM_STEP)
    v_hat = v_new / (1.0 - ADAM_B2 ** ADAM_STEP)
    return -ADAM_LR * (m_hat / (jnp.sqrt(v_hat) + ADAM_EPS) + ADAM_WD * w), m_new, v_new


def _adamw_small(ws, ms, vs, gs, name):
    n = len(ws)

    def body(*refs):
        for i in range(n):
            w_ref, m_ref, v_ref, g_ref = (refs[j * n + i] for j in range(4))
            go_ref, d_ref, mo_ref, vo_ref = refs[4 * n + 4 * i:4 * n + 4 * i + 4]
            g = g_ref[...]
            go_ref[...] = g
            d_ref[...], mo_ref[...], vo_ref[...] = _adamw_math(w_ref[...], m_ref[...], v_ref[...], g)

    res = pl.pallas_call(body, name=name, out_shape=[jax.ShapeDtypeStruct(a.shape, F32) for a in ws for _ in range(4)])(
        *ws, *ms, *vs, *gs)
    return [res[4 * i:4 * i + 4] for i in range(n)]


def _sum_lead(x, name, rows_apart=False):
    p, r, c = x.shape
    tr = _row_tile(r, c)
    mid = (1,) if rows_apart else ()

    def body(x_ref, o_ref):
        acc = x_ref[0].astype(F32)
        for i in range(1, p):
            acc = acc + x_ref[i].astype(F32)
        o_ref[...] = acc.reshape(o_ref.shape)

    return pl.pallas_call(body, name=name, grid=(pl.cdiv(r, tr),), in_specs=[_vspec((p, tr, c), lambda i: (0, i, 0))],
                          out_specs=_vspec((tr,) + mid + (c,), lambda i: (i,) + (0,) * (1 + len(mid))),
                          out_shape=jax.ShapeDtypeStruct((r,) + mid + (c,), F32))(x)


def _allgather8(x_shard, name):
    m_per, n = x_shard.shape

    def body(x_ref, out_ref, send_sems, recv_sems, local_sem):
        x, y, c = lax.axis_index("x"), lax.axis_index("y"), lax.axis_index("c")
        me, sibling = (x, y, c), (x, y, 1 - c)
        chips = [(1 - x, y), (x, 1 - y), (1 - x, 1 - y)]

        def rows(px, py, pc):
            return out_ref.at[pl.ds((4 * px + 2 * py + pc) * m_per, m_per), :]

        def copy(k, block, to, src=None):
            return pltpu.make_async_remote_copy(
                src_ref=rows(*block) if src is None else src, dst_ref=rows(*block), send_sem=send_sems.at[k],
                recv_sem=recv_sems.at[k], device_id=to, device_id_type=MESH)

        mine = pltpu.make_async_copy(x_ref, rows(*me), local_sem)
        mine.start()
        first = [copy(0, me, sibling, src=x_ref)]
        first += [copy(1 + j, me, (*chip, c), src=x_ref) for j, chip in enumerate(chips)]
        for cp in first:
            cp.start()
        passed = [copy(4 + j, (*chip, c), sibling) for j, chip in enumerate(chips)]
        for j, chip in enumerate(chips):
            copy(1 + j, (*chip, c), me).wait_recv()
            passed[j].start()
        copy(0, sibling, me).wait_recv()
        for j, chip in enumerate(chips):
            copy(4 + j, (*chip, 1 - c), me).wait_recv()
        for cp in first + passed:
            cp.wait_send()
        mine.wait()

    return pl.pallas_call(
        body, name=name, out_shape=jax.ShapeDtypeStruct((8 * m_per, n), x_shard.dtype),
        in_specs=[pl.BlockSpec(memory_space=pltpu.VMEM)], out_specs=pl.BlockSpec(memory_space=pltpu.VMEM),
        scratch_shapes=[pltpu.SemaphoreType.DMA((7,)), pltpu.SemaphoreType.DMA((7,)), pltpu.SemaphoreType.DMA],
    )(x_shard)


_HBM = pl.BlockSpec(memory_space=pltpu.HBM)


def _mesh_place():
    x, y, c = lax.axis_index("x"), lax.axis_index("y"), lax.axis_index("c")
    return x, y, c, 2 * x + y, [(1 - x, y), (x, 1 - y), (1 - x, 1 - y)]


def _gather_plug(shards):
    n = len(shards)

    def half(ref, c, lead=None):
        r, cols = ref.shape[-2] // 2, ref.shape[-1] // 2
        if r % 16 == 0:
            rows = pl.ds(pl.multiple_of(c * r, 16), r)
            return ref.at[rows, :] if lead is None else ref.at[lead, rows, :]
        lanes = pl.ds(pl.multiple_of(c * cols, LANES), cols)
        return ref.at[:, lanes] if lead is None else ref.at[lead, :, lanes]

    def copies(ins, outs, send, recv):
        x, y, c, me, chips = _mesh_place()
        ici, fwd, fwd_in = [], [], []
        for i in range(n):
            for j, (px, py) in enumerate(chips):
                q = 2 * px + py
                ici.append((pltpu.make_async_remote_copy(
                    src_ref=half(ins[i], c), dst_ref=half(outs[i], c, me), send_sem=send.at[6 * i + j], recv_sem=recv.at[6 * i + j],
                    device_id=(px, py, c), device_id_type=MESH),
                    pltpu.make_async_remote_copy(
                    src_ref=half(ins[i], c), dst_ref=half(outs[i], c, q), send_sem=send.at[6 * i + j], recv_sem=recv.at[6 * i + j],
                    device_id=(px, py, c), device_id_type=MESH)))
                fwd.append(pltpu.make_async_remote_copy(
                    src_ref=half(outs[i], c, q), dst_ref=half(outs[i], c, q), send_sem=send.at[6 * i + 3 + j],
                    recv_sem=recv.at[6 * i + 3 + j], device_id=(x, y, 1 - c), device_id_type=MESH))
                fwd_in.append(pltpu.make_async_remote_copy(
                    src_ref=half(outs[i], 1 - c, q), dst_ref=half(outs[i], 1 - c, q), send_sem=send.at[6 * i + 3 + j],
                    recv_sem=recv.at[6 * i + 3 + j], device_id=(x, y, 1 - c), device_id_type=MESH))
        return ici, fwd, fwd_in, me

    def start(ins, outs, send, recv, loc):
        ici, _, _, me = copies(ins, outs, send, recv)
        for i in range(n):
            pltpu.make_async_copy(ins[i], outs[i].at[me], loc.at[i]).start()
        for out_cp, _ in ici:
            out_cp.start()

    def finish(ins, outs, send, recv, loc):
        ici, fwd, fwd_in, me = copies(ins, outs, send, recv)
        for (_, in_cp), f in zip(ici, fwd):
            in_cp.wait_recv()
            f.start()
        for f in fwd_in:
            f.wait_recv()
        for (out_cp, _), f in zip(ici, fwd):
            out_cp.wait_send()
            f.wait_send()
        for i in range(n):
            pltpu.make_async_copy(ins[i], outs[i].at[me], loc.at[i]).wait()

    return dict(ins=list(shards), out_shape=[jax.ShapeDtypeStruct((N_CHIPS,) + a.shape, a.dtype) for a in shards],
                scratch=[pltpu.SemaphoreType.DMA((6 * n,)), pltpu.SemaphoreType.DMA((6 * n,)), pltpu.SemaphoreType.DMA((n,))],
                start=start, finish=finish)


def _exchange_plug(pieces):
    n = len(pieces)

    def copies(ins, outs, send, recv):
        x, y, c, me, chips = _mesh_place()
        out_cps, in_cps = [], []
        for i in range(n):
            for j, (px, py) in enumerate(chips):
                q = 2 * px + py
                out_cps.append(pltpu.make_async_remote_copy(src_ref=ins[i].at[q], dst_ref=outs[i].at[me], send_sem=send.at[3 * i + j],
                                                            recv_sem=recv.at[3 * i + j], device_id=(px, py, c), device_id_type=MESH))
                in_cps.append(pltpu.make_async_remote_copy(src_ref=ins[i].at[me], dst_ref=outs[i].at[q], send_sem=send.at[3 * i + j],
                                                           recv_sem=recv.at[3 * i + j], device_id=(px, py, c), device_id_type=MESH))
        return out_cps, in_cps, me

    def start(ins, outs, send, recv, loc):
        out_cps, _, me = copies(ins, outs, send, recv)
        for i in range(n):
            pltpu.make_async_copy(ins[i].at[me], outs[i].at[me], loc.at[i]).start()
        for cp in out_cps:
            cp.start()

    def finish(ins, outs, send, recv, loc):
        out_cps, in_cps, me = copies(ins, outs, send, recv)
        for cp in in_cps:
            cp.wait_recv()
        for cp in out_cps:
            cp.wait_send()
        for i in range(n):
            pltpu.make_async_copy(ins[i].at[me], outs[i].at[me], loc.at[i]).wait()

    return dict(ins=list(pieces), out_shape=[jax.ShapeDtypeStruct(a.shape, a.dtype) for a in pieces],
                scratch=[pltpu.SemaphoreType.DMA((3 * n,)), pltpu.SemaphoreType.DMA((3 * n,)), pltpu.SemaphoreType.DMA((n,))],
                start=start, finish=finish)


def _comm_call(plug, name):
    n_in, n_out = len(plug["ins"]), len(plug["out_shape"])

    def body(*refs):
        ins, outs, sems = refs[:n_in], refs[n_in:n_in + n_out], refs[n_in + n_out:]
        plug["start"](ins, outs, *sems)
        plug["finish"](ins, outs, *sems)

    return pl.pallas_call(body, name=name, out_shape=plug["out_shape"], in_specs=[_HBM] * n_in, out_specs=[_HBM] * n_out,
                          scratch_shapes=plug["scratch"])(*plug["ins"])


def _plugged_call(body, plug, first_last, args, *, name, grid, in_specs, out_specs, out_shape, scratch_shapes=()):
    in_specs, out_specs, out_shape, scratch_shapes = list(in_specs), list(out_specs), list(out_shape), list(scratch_shapes)
    if plug is None:
        return pl.pallas_call(body, name=name, grid=grid, in_specs=in_specs, out_specs=out_specs, out_shape=out_shape,
                              scratch_shapes=scratch_shapes)(*args), []
    n_in, n_out, n_sc = len(in_specs), len(out_specs), len(scratch_shapes)
    p_in, p_out = len(plug["ins"]), len(plug["out_shape"])

    def full(*refs):
        ins, refs = refs[:n_in], refs[n_in:]
        pins, refs = refs[:p_in], refs[p_in:]
        outs, refs = refs[:n_out], refs[n_out:]
        pouts, refs = refs[:p_out], refs[p_out:]
        scr, psems = refs[:n_sc], refs[n_sc:]
        first, last = first_last()

        @pl.when(first)
        def _():
            plug["start"](pins, pouts, *psems)

        body(*ins, *outs, *scr)

        @pl.when(last)
        def _():
            plug["finish"](pins, pouts, *psems)

    res = pl.pallas_call(full, name=name, grid=grid, in_specs=in_specs + [_HBM] * p_in, out_specs=out_specs + [_HBM] * p_out,
                         out_shape=out_shape + plug["out_shape"], scratch_shapes=scratch_shapes + plug["scratch"])(*args, *plug["ins"])
    return res[:n_out], res[n_out:]


def _grid_ends(grid):
    def ends():
        first = last = None
        for ax, n in enumerate(grid):
            i = pl.program_id(ax)
            first = (i == 0) if first is None else first & (i == 0)
            last = (i == n - 1) if last is None else last & (i == n - 1)
        return first, last
    return ends


def _pair_presum(pieces, name):
    n, r, cols = pieces.shape
    hc = cols // 2

    def body(p_ref, o_ref, mine_ref, land_ref, send_sem, recv_sem, local_sem):
        x, y, c = lax.axis_index("x"), lax.axis_index("y"), lax.axis_index("c")
        keep = pl.ds(pl.multiple_of(c * hc, LANES), hc)
        give = pl.ds(pl.multiple_of((1 - c) * hc, LANES), hc)
        push = pltpu.make_async_remote_copy(src_ref=p_ref.at[:, :, give], dst_ref=land_ref, send_sem=send_sem, recv_sem=recv_sem,
                                            device_id=(x, y, 1 - c), device_id_type=MESH)
        own = pltpu.make_async_copy(p_ref.at[:, :, keep], mine_ref, local_sem)
        push.start()
        own.start()
        own.wait()
        push.wait_recv()
        for i in range(n):
            o_ref[i] = (mine_ref[i].astype(F32) + land_ref[i].astype(F32)).astype(BF16)
        push.wait_send()

    return pl.pallas_call(
        body, name=name, out_shape=jax.ShapeDtypeStruct((n, r, hc), BF16), in_specs=[_HBM],
        out_specs=pl.BlockSpec(memory_space=pltpu.VMEM),
        scratch_shapes=[pltpu.VMEM((n, r, hc), BF16), pltpu.VMEM((n, r, hc), BF16), pltpu.SemaphoreType.DMA,
                        pltpu.SemaphoreType.DMA, pltpu.SemaphoreType.DMA],
    )(pieces)


def _sibling_exchange(arrs, name):
    n = len(arrs)

    def body(*refs):
        ins, outs = refs[:n], refs[n:2 * n]
        send_sems, recv_sems = refs[2 * n:]
        sibling = (lax.axis_index("x"), lax.axis_index("y"), 1 - lax.axis_index("c"))
        cps = [pltpu.make_async_remote_copy(src_ref=ins[i], dst_ref=outs[i], send_sem=send_sems.at[i], recv_sem=recv_sems.at[i],
                                            device_id=sibling, device_id_type=MESH) for i in range(n)]
        for cp in cps:
            cp.start()
        for cp in cps:
            cp.wait()

    return pl.pallas_call(
        body, name=name, out_shape=[jax.ShapeDtypeStruct(a.shape, a.dtype) for a in arrs], in_specs=[_HBM] * n, out_specs=[_HBM] * n,
        scratch_shapes=[pltpu.SemaphoreType.DMA((n,)), pltpu.SemaphoreType.DMA((n,))],
    )(*arrs)


_SEGS = ((0, AQ, C_Q), (AQ, AKV, C_K), (AQ + AKV, AKV, C_V), (AQ + 2 * AKV, 3 * DN, C_DQKV), (2304, DN, C_DZ),
         (2816, 2 * DN_HEADS, C_BD), (2824, D_MODEL, C_GA), (3848, D_MODEL, C_GD))
SHARD_ROWS = IN_DIM // N_CHIPS


def _to_padded(w4):
    wt = w4.reshape(IN_DIM, w4.shape[2])
    parts = [wt[o:o + n] for o, n, _ in sorted(_SEGS, key=lambda sg: sg[2])]
    return jnp.concatenate(parts + [jnp.zeros((IN_PAD - IN_DIM, wt.shape[1]), wt.dtype)], axis=0)


def _from_padded(gt):
    return jnp.concatenate([gt[ps:ps + n] for _, n, ps in sorted(_SEGS)], axis=0).reshape(N_CHIPS, SHARD_ROWS, gt.shape[1])


def _lane_vec(a):
    return jnp.zeros((1, LANES), F32).at[0, DN_HEADS:2 * DN_HEADS].set(a)


_ROW_SHARDED = ("w_in", "w_out", "ffn_w_down")
_FFN = ("ffn_w_up", "ffn_w_down")
_LATE_MIXER = ("w_attn_branch", "w_dn_branch", "w_out")


def _pieces(k, a):
    if a.ndim == 3:
        return a
    if k in _ROW_SHARDED:
        return a.reshape(N_CHIPS, a.shape[0] // N_CHIPS, a.shape[1]).astype(BF16)
    return jnp.transpose(a.reshape(a.shape[0], N_CHIPS, a.shape[1] // N_CHIPS), (1, 0, 2)).astype(BF16)


def _assemble(k, a):
    if k in _ROW_SHARDED:
        return a.reshape(-1, a.shape[2])
    return jnp.transpose(a, (1, 0, 2)).reshape(a.shape[1], -1)


def _device_step(x2, tgt2, mod, p, bsz, shards=None):
    d = D_MODEL
    on_mesh = shards is not None
    p = dict(p)
    sh1, sc1, g1, sh2, sc2, g2 = [mod[:, i * d:(i + 1) * d].reshape(bsz, 1, d) for i in range(N_MOD)]
    alog_v, dt_v = _lane_vec(p["dn_a_log"]), _lane_vec(p["dn_dt_bias"])
    sinks = p["attn_sinks"].reshape(1, AQ_HEADS)
    u1 = _pre_fwd(x2, p["norm_mix_pre"], sc1, sh1, "pre1_fwd")
    if on_mesh:
        proj, got = _mm(u1, p["w_in"], "nt", F32, "mm_proj", _gather_plug(shards["late_mixer"]))
        p.update({k: _assemble(k, a) for k, a in zip(_LATE_MIXER, got)})
    else:
        proj = _mm(u1, p["w_in"], "nt", F32, "mm_proj")
    bias = _bias_build(p["rel_bias"])
    y_attn = _attn_fwd(proj, bias, sinks, bsz)
    qkvn = _dn_prep_fwd(proj, p["dn_conv_w"], bsz)
    bg = _bg_fwd(proj, alog_v, dt_v, bsz)
    nc = x2.shape[0] // bsz // DN_CHUNK
    g_rows = jnp.transpose(bg[:, DN_HEADS:2 * DN_HEADS].reshape(bsz, nc, DN_CHUNK, DN_HEADS), (0, 3, 1, 2))
    o, states, got = _dn_chunk_fwd(qkvn, bg, g_rows, bsz, _gather_plug(shards["ffn"][:1]) if on_mesh else None)
    for k, a in zip(_FFN[:1], got):
        p[k] = _assemble(k, a)
    y_dn = _dn_out_fwd(o, proj, p["dn_norm_w"])
    ya = _mm(y_attn, p["w_attn_branch"], "nn", F32, "mm_ya")
    yd = _mm(y_dn, p["w_dn_branch"], "nn", F32, "mm_yd")
    merged = _merge_fwd(proj, ya, yd)
    y1 = _mm(merged, p["w_out"], "nn", F32, "mm_y1")
    h1, u2 = _post_pre_fwd(x2, y1, p["norm_mix_post"], g1, p["norm_ffn_pre"], sc2, sh2, "post1_pre2_fwd")
    if on_mesh:
        up, got = _mm(u2, p["ffn_w_up"], "nn", F32, "mm_up", _gather_plug(shards["ffn"][1:]))
        p["ffn_w_down"] = _assemble("ffn_w_down", got[0])
    else:
        up = _mm(u2, p["ffn_w_up"], "nn", F32, "mm_up")
    act = _ffn_act_fwd(up, p["ffn_conv_w"], bsz)
    y2 = _mm(act, p["ffn_w_down"], "nn", F32, "mm_y2")
    dh2, dy2, g_ffn_post, dg2, sq = _post_loss_bwd(h1, y2, p["norm_ffn_post"], g2, tgt2, "post2_loss_bwd")
    g = {}
    g["norm_ffn_post"] = g_ffn_post
    dact = _mm(dy2, p["ffn_w_down"], "nt", F32, "mm_dact")
    g["ffn_w_down"] = _mm(act, dy2, "tn", BF16, "mm_dwdown")
    dupg, dupv, dcwg, dcwv, got_down = _ffn_act_bwd(
        up, p["ffn_conv_w"], dact, bsz, _exchange_plug([_pieces("ffn_w_down", g["ffn_w_down"])]) if on_mesh else None)
    g["ffn_conv_w"] = jnp.concatenate([dcwg, dcwv], axis=1)
    dup = jnp.concatenate([dupg, dupv], axis=1)
    g["ffn_w_up"] = _mm(u2, dup, "tn", BF16, "mm_dwup", split=N_CHIPS)
    du2 = _mm(dup, p["ffn_w_up"], "nt", F32, "mm_du2")
    dh1, dy1, g["norm_ffn_pre"], dsc2, dsh2, g["norm_mix_post"], dg1 = _pre_post_bwd(
        h1, p["norm_ffn_pre"], sc2, sh2, du2, dh2, y1, p["norm_mix_post"], g1, "pre2_post1_bwd")
    dmerged = _mm(dy1, p["w_out"], "nt", F32, "mm_dmerged")
    g["w_out"] = _mm(merged, dy1, "tn", BF16, "mm_dwout")
    dga, dgd, dya, dyd = _merge_bwd(proj, ya, yd, dmerged)
    dy_attn = _mm(dya, p["w_attn_branch"], "nt", BF16, "mm_dyattn")
    g["w_attn_branch"] = _mm(y_attn, dya, "tn", BF16, "mm_dwab", split=N_CHIPS)
    dy_dn = _mm(dyd, p["w_dn_branch"], "nt", F32, "mm_dydn")
    g["w_dn_branch"] = _mm(y_dn, dyd, "tn", BF16, "mm_dwdb", split=N_CHIPS)
    do, dz, g["dn_norm_w"] = _dn_out_bwd(o, proj, p["dn_norm_w"], dy_dn)
    def plug_for(names):
        return _exchange_plug([_pieces(k, g[k]) for k in names]) if on_mesh else None

    early = ("w_out", "w_attn_branch", "w_dn_branch")
    dqkvn, dbg4, got_up = _dn_chunk_bwd(qkvn, bg, g_rows, states, do, bsz, plug_for(_FFN[:1]))
    got_ffn = list(got_up) + list(got_down)
    d_dqkv, g["dn_conv_w"] = _dn_prep_bwd(proj, p["dn_conv_w"], dqkvn, bsz)
    dbd, g["dn_a_log"], g["dn_dt_bias"] = _bg_bwd(proj, alog_v, dt_v, dbg4, bsz)
    dq, dk, dv, dbias, g["attn_sinks"], got_early = _attn_bwd(proj, bias, sinks, dy_attn, bsz, plug_for(early))
    g["rel_bias"] = _bias_grad(dbias)
    dproj = jnp.concatenate([d_dqkv, dq, dz, dga, dgd, dk, dv, dbd], axis=1)
    g["w_in"] = _from_padded(_mm(dproj, u1, "tn", BF16, "mm_dwin"))
    if on_mesh:
        g["w_in"] = _pair_presum(g["w_in"], "presum_w_in")
        du1, got_in = _mm(dproj, p["w_in"], "nn", F32, "mm_du1", plug_for(("w_in",)))
        g.update(zip(_FFN + early + ("w_in",), list(got_ffn) + list(got_early) + list(got_in)))
    else:
        du1 = _mm(dproj, p["w_in"], "nn", F32, "mm_du1")
    dx, g["norm_mix_pre"], dsc1, dsh1 = _pre_bwd(x2, p["norm_mix_pre"], sc1, sh1, du1, dh1, "pre1_bwd")
    dmod = jnp.concatenate([dsh1, dsc1, dg1, dsh2, dsc2, dg2], axis=-1).reshape(bsz, N_MOD * d)
    return sq, dx, dmod, g


_SMALL = (("norm_mix_pre", D_MODEL), ("norm_mix_post", D_MODEL), ("norm_ffn_pre", D_MODEL), ("norm_ffn_post", D_MODEL),
          ("dn_norm_w", DN_HD), ("dn_a_log", LANES), ("dn_dt_bias", LANES), ("attn_sinks", AQ_HEADS * LANES),
          ("rel_bias", AQ_HEADS * LANES), ("dn_conv_w", DN_CONV * 3 * DN), ("ffn_conv_w", FFN_CONV * 2 * D_FF))


def _pack_rows(parts, rows):
    flat = jnp.concatenate([a.reshape(-1) for a in parts])
    return jnp.concatenate([flat, jnp.zeros((rows * LANES - flat.shape[0],), F32)]).reshape(rows, LANES)


def _pad128(a):
    flat = a.reshape(-1)
    n = -(-flat.shape[0] // LANES) * LANES
    return jnp.concatenate([flat, jnp.zeros((n - flat.shape[0],), F32)]) if n != flat.shape[0] else flat


_W_NAMES = ("ada_w", "ada_b", "norm_mix_pre", "norm_mix_post", "norm_ffn_pre", "norm_ffn_post", "w_in", "dn_conv_w", "dn_a_log",
            "dn_dt_bias", "dn_norm_w", "attn_sinks", "rel_bias", "w_attn_branch", "w_dn_branch", "w_out", "ffn_w_up", "ffn_conv_w",
            "ffn_w_down")
_BIG = ("w_in", "w_attn_branch", "w_dn_branch", "w_out", "ffn_w_up", "ffn_w_down")


def kernel(x, c, *rest):
    nw = len(_W_NAMES)
    w = dict(zip(_W_NAMES, rest[:nw]))
    loss_target = rest[nw]
    m = dict(zip(_W_NAMES, rest[nw + 1:2 * nw + 1]))
    v = dict(zip(_W_NAMES, rest[2 * nw + 1:3 * nw + 1]))
    ix, iy, ic = lax.axis_index("x"), lax.axis_index("y"), lax.axis_index("c")
    chip, dev = 2 * ix + iy, 4 * ix + 2 * iy + ic
    bsz, s, d = x.shape
    t = bsz * s
    n_dev = 8

    front_rows = 64
    front = _pack_rows([c, w["dn_conv_w"], w["ffn_conv_w"]], front_rows)
    front_all = _allgather8(front, "ag_front").reshape(n_dev, front_rows * LANES)
    n_c, n_dc, n_fc = bsz * d, DN_CONV * 3 * DN // N_CHIPS, FFN_CONV * 2 * D_FF // N_CHIPS
    c_all = front_all[:, :n_c].reshape(n_dev * bsz, d)
    per_chip = front_all[0::2]
    dn_conv_full = jnp.transpose(per_chip[:, n_c:n_c + n_dc].reshape(N_CHIPS, DN_CONV, -1), (1, 0, 2)).reshape(DN_CONV, 3 * DN)
    ffn_conv_full = jnp.transpose(per_chip[:, n_c + n_dc:n_c + n_dc + n_fc].reshape(N_CHIPS, FFN_CONV, -1), (1, 0, 2)).reshape(FFN_CONV, 2 * D_FF)

    mod_cols = N_MOD * d // N_CHIPS
    ada_b_loc = lax.dynamic_slice(w["ada_b"], (0, chip * mod_cols), (1, mod_cols))
    mod_part = _mod_fwd(c_all, w["ada_w"][0], ada_b_loc)
    mod_all = _allgather8(mod_part, "ag_mod").reshape(n_dev, n_dev * bsz, mod_cols)[0::2]
    mod = jnp.transpose(lax.dynamic_slice(mod_all, (0, dev * bsz, 0), (N_CHIPS, bsz, mod_cols)), (1, 0, 2)).reshape(bsz, N_MOD * d)

    (w_in_all,) = _comm_call(_gather_plug([jnp.swapaxes(w["w_in"][0], 0, 1).astype(BF16)]), "gather_w_in")
    p = {"w_in": _to_padded(w_in_all)}
    shards = {"late_mixer": [w[k][0].astype(BF16) for k in _LATE_MIXER], "ffn": [w[k][0].astype(BF16) for k in _FFN]}
    for k in ("norm_mix_pre", "norm_mix_post", "norm_ffn_pre", "norm_ffn_post", "dn_norm_w", "attn_sinks"):
        p[k] = w[k]
    p["dn_a_log"], p["dn_dt_bias"], p["rel_bias"] = w["dn_a_log"][0], w["dn_dt_bias"][0], w["rel_bias"]
    p["dn_conv_w"], p["ffn_conv_w"] = dn_conv_full, ffn_conv_full

    sq, dx, dmod, g = _device_step(x.reshape(t, d), loss_target.reshape(t, d), mod, p, bsz, shards)
    loss = lax.psum(0.5 * jnp.sum(sq), ("x", "y", "c"))

    g["dn_a_log"], g["dn_dt_bias"] = g["dn_a_log"].reshape(-1), g["dn_dt_bias"].reshape(-1)
    small_rows = 328
    small = _pack_rows([dmod] + [g[k] for k, _ in _SMALL], small_rows)
    small_all = _allgather8(small, "ag_small").reshape(n_dev, small_rows, LANES)
    n_dm = bsz * N_MOD * d
    dmod_all = small_all.reshape(n_dev, -1)[:, :n_dm].reshape(n_dev * bsz, N_MOD * d)
    tot = _sum_lead(small_all, "sum_small").reshape(-1)
    gs, off = {}, n_dm
    for k, n in _SMALL:
        gs[k] = tot[off:off + n]
        off += n
    grad = {}
    grad["ada_w"], grad["ada_b"] = _ada_grad(c_all, lax.dynamic_slice(dmod_all, (0, chip * mod_cols), (n_dev * bsz, mod_cols)), dmod_all)
    for k in ("norm_mix_pre", "norm_mix_post", "norm_ffn_pre", "norm_ffn_post", "dn_norm_w"):
        grad[k] = gs[k]
    grad["dn_a_log"] = gs["dn_a_log"][DN_HEADS:2 * DN_HEADS]
    grad["dn_dt_bias"] = gs["dn_dt_bias"][DN_HEADS:2 * DN_HEADS]
    grad["attn_sinks"] = gs["attn_sinks"].reshape(AQ_HEADS, LANES)[:, 0]
    grad["rel_bias"] = gs["rel_bias"].reshape(AQ_HEADS, LANES)[:, :REL_BUCKETS].T
    grad["dn_conv_w"] = lax.dynamic_slice(gs["dn_conv_w"].reshape(DN_CONV, 3 * DN), (0, chip * (3 * DN // N_CHIPS)), (DN_CONV, 3 * DN // N_CHIPS))
    grad["ffn_conv_w"] = lax.dynamic_slice(gs["ffn_conv_w"].reshape(FFN_CONV, 2 * D_FF), (0, chip * (2 * D_FF // N_CHIPS)), (FFN_CONV, 2 * D_FF // N_CHIPS))

    mine = [_sum_lead(g[k], "sum_" + k, rows_apart=(k == "w_in")) for k in _BIG]
    theirs = _sibling_exchange(mine, "exchange_cores")

    out = {}
    for k, a, b in zip(_BIG, mine, theirs):
        if k == "w_in":
            tr = lambda z: jnp.transpose(z, (2, 0, 1))
            hc = a.shape[2]
            whole = lax.dynamic_update_slice(jnp.zeros(a.shape[:2] + (2 * hc,), F32), a, (0, 0, ic * hc))
            whole = lax.dynamic_update_slice(whole, b, (0, 0, (1 - ic) * hc))
            out[k] = [jnp.transpose(r, (1, 2, 0)) for r in _adamw(tr(w[k]), tr(m[k]), tr(v[k]), [whole], "adamw_" + k)]
        else:
            out[k] = _adamw(w[k][0], m[k][0], v[k][0], [a, b], "adamw_" + k)
    out["ada_w"] = _adamw(w["ada_w"][0], m["ada_w"][0], v["ada_w"][0], [grad["ada_w"]], "adamw_ada_w")
    small_names = [k for k in _W_NAMES if k not in _BIG and k != "ada_w"]
    res_small = _adamw_small([w[k] for k in small_names], [m[k] for k in small_names], [v[k] for k in small_names],
                             [grad[k].reshape(w[k].shape) for k in small_names], "adamw_small")
    out.update(zip(small_names, res_small))
    for k in _BIG + ("ada_w",):
        out[k] = [r.reshape(w[k].shape) for r in out[k]]
    grads, deltas, new_m, new_v = ([out[k][i] for k in _W_NAMES] for i in range(4))
    return (loss, dx.reshape(bsz, s, d), *grads, *deltas, *new_m, *new_v)
```

```python
import functools
import math

import numpy as np
import jax
import jax.numpy as jnp
from jax import lax
from jax.experimental import pallas as pl
from jax.experimental.pallas import tpu as pltpu

F32 = jnp.float32
BF16 = jnp.bfloat16
MESH = pl.DeviceIdType.MESH

D_MODEL = 1024
N_MOD = 6
AQ_HEADS, AKV_HEADS, A_HD, WINDOW = 8, 2, 64, 128
REL_BUCKETS, REL_MAX_DIST = 32, 128
DN_HEADS, DN_HD, DN_CONV, DN_CHUNK = 4, 128, 4, 64
D_FF, FFN_CONV = 2816, 3
RMS_EPS, L2_EPS, NEG_INF = 1e-6, 1e-6, -1e30
AQ, AKV, DN = AQ_HEADS * A_HD, AKV_HEADS * A_HD, DN_HEADS * DN_HD
IN_DIM = AQ + 2 * AKV + 3 * DN + DN + 2 * DN_HEADS + 2 * D_MODEL
C_DQKV, C_Q, C_DZ, C_GA, C_GD, C_K, C_V, C_BD = 0, 1536, 2048, 2560, 3584, 4608, 4736, 4864
IN_PAD = 4992
LANES = 128
N_CHIPS = 4

ADAM_LR, ADAM_B1, ADAM_B2, ADAM_EPS, ADAM_WD, ADAM_STEP = 0.001, 0.9, 0.999, 1e-08, 0.01, 10


def _pick(n, cap):
    best = None
    for t in range(LANES, cap + 1, LANES):
        if n % t == 0:
            best = t
    return best if best is not None else n


def _vspec(shape, index_map):
    return pl.BlockSpec(shape, index_map)


MM_VMEM_BUDGET = 40 * 2 ** 20
GRID_STEP_S = 0.35e-6
HBM_BYTES_PER_S = 3.0e12
MXU_FLOPS_PER_S = 9.0e14
MXU_DIM = 256


def _mm_tiles(m, n, k, mode, in_bytes, out_bytes, split=1):
    best = None
    for tm in [t for t in range(LANES, m + 1, LANES) if m % t == 0]:
        for tn in [t for t in range(LANES, n // split + 1, LANES) if (n // split) % t == 0]:
            a, b, o = k * tm * in_bytes, k * tn * in_bytes, tm * tn * out_bytes
            if 2 * (a + b + o) + (a if mode == "tn" else 0) > MM_VMEM_BUDGET:
                continue
            hbm_s = (m * k * in_bytes + (m // tm) * n * k * in_bytes + m * n * out_bytes) / HBM_BYTES_PER_S
            mxu_s = 2 * m * n * k / (MXU_FLOPS_PER_S * min(1.0, tm / MXU_DIM) * min(1.0, tn / MXU_DIM))
            cost = (m // tm) * (n // tn) * GRID_STEP_S + max(hbm_s, mxu_s)
            if best is None or cost < best[0]:
                best = (cost, tm, tn)
    return best[1], best[2]


def _mm(a, b, mode, out_dtype, name, plug=None, split=1):
    if mode == "nn":
        (m, k), n = a.shape, b.shape[1]
        dims = (((1,), (0,)), ((), ()))
    elif mode == "nt":
        (m, k), n = a.shape, b.shape[0]
        dims = (((1,), (1,)), ((), ()))
    else:
        (k, m), n = a.shape, b.shape[1]
        dims = (((0,), (0,)), ((), ()))
    tm, tn = _mm_tiles(m, n, k, mode, a.dtype.itemsize, jnp.dtype(out_dtype).itemsize, split)
    if mode == "tn":
        a_spec = _vspec((k, tm), lambda i, j: (0, i))
    else:
        a_spec = _vspec((tm, k), lambda i, j: (i, 0))
    if mode == "nt":
        b_spec = _vspec((tn, k), lambda i, j: (j, 0))
    else:
        b_spec = _vspec((k, tn), lambda i, j: (0, j))

    def body(a_ref, b_ref, o_ref):
        o_ref[...] = lax.dot_general(a_ref[...].astype(BF16), b_ref[...].astype(BF16), dims,
                                     preferred_element_type=F32).astype(out_dtype).reshape(o_ref.shape)

    grid = (m // tm, n // tn)
    if split == 1:
        out_spec, out_shape = _vspec((tm, tn), lambda i, j: (i, j)), (m, n)
    else:
        per = n // split // tn
        out_spec, out_shape = _vspec((1, tm, tn), lambda i, j: (j // per, i, j % per)), (split, m, n // split)
    (out,), extra = _plugged_call(body, plug, _grid_ends(grid), (a, b), name=name, grid=grid, in_specs=[a_spec, b_spec],
                                  out_specs=[out_spec], out_shape=[jax.ShapeDtypeStruct(out_shape, out_dtype)])
    return out if plug is None else (out, extra)


def _rms(x, w):
    return (x * lax.rsqrt(jnp.mean(x * x, axis=-1, keepdims=True) + RMS_EPS)) * w


def _pre_f(x, w, sc, sh):
    return _rms(x, w) * (1.0 + sc) + sh


def _post_f(y, w, g):
    return g * _rms(y, w)


def _tok_grid(t, bsz, ts):
    nt = t // bsz // ts
    return nt, (bsz, nt)


def _pre_fwd(x, w, sc, sh, name, ts=512):
    t, d = x.shape
    bsz = sc.shape[0]
    nt, grid = _tok_grid(t, bsz, ts)
    row = _vspec((ts, d), lambda b, i: (b * nt + i, 0))
    vec = _vspec((1, d), lambda b, i: (0, 0))
    bvec = _vspec((1, 1, d), lambda b, i: (b, 0, 0))

    def body(x_ref, w_ref, sc_ref, sh_ref, u_ref):
        u_ref[...] = _pre_f(x_ref[...], w_ref[...], sc_ref[0], sh_ref[0]).astype(BF16)

    return pl.pallas_call(body, name=name, grid=grid, in_specs=[row, vec, bvec, bvec], out_specs=row,
                          out_shape=jax.ShapeDtypeStruct((t, d), BF16))(x, w, sc, sh)


def _pre_bwd(x, w, sc, sh, du, dres, name, ts=512):
    t, d = x.shape
    bsz = sc.shape[0]
    nt, grid = _tok_grid(t, bsz, ts)
    row = _vspec((ts, d), lambda b, i: (b * nt + i, 0))
    vec = _vspec((1, d), lambda b, i: (0, 0))
    bvec = _vspec((1, 1, d), lambda b, i: (b, 0, 0))

    def body(x_ref, w_ref, sc_ref, sh_ref, du_ref, dres_ref, dx_ref, dw_ref, dsc_ref, dsh_ref):
        b, i = pl.program_id(0), pl.program_id(1)
        _, vjp = jax.vjp(_pre_f, x_ref[...], w_ref[...], sc_ref[0], sh_ref[0])
        dx, dw, dsc, dsh = vjp(du_ref[...])
        dx_ref[...] = dres_ref[...] + dx

        @pl.when((b == 0) & (i == 0))
        def _():
            dw_ref[...] = jnp.zeros_like(dw_ref)

        @pl.when(i == 0)
        def _():
            dsc_ref[...] = jnp.zeros_like(dsc_ref)
            dsh_ref[...] = jnp.zeros_like(dsh_ref)

        dw_ref[...] += dw
        dsc_ref[0] += dsc
        dsh_ref[0] += dsh

    return pl.pallas_call(
        body, name=name, grid=grid, in_specs=[row, vec, bvec, bvec, row, row], out_specs=[row, vec, bvec, bvec],
        out_shape=[jax.ShapeDtypeStruct((t, d), F32), jax.ShapeDtypeStruct((1, d), F32),
                   jax.ShapeDtypeStruct((bsz, 1, d), F32), jax.ShapeDtypeStruct((bsz, 1, d), F32)],
    )(x, w, sc, sh, du, dres)


def _accumulate(ref, val, first):
    @pl.when(first)
    def _():
        ref[...] = jnp.zeros_like(ref)

    ref[...] += val.reshape(ref.shape)


def _post_pre_fwd(res, y, w_post, g, w_pre, sc, sh, name, ts=512):
    t, d = y.shape
    bsz = g.shape[0]
    nt, grid = _tok_grid(t, bsz, ts)
    row = _vspec((ts, d), lambda b, i: (b * nt + i, 0))
    vec = _vspec((1, d), lambda b, i: (0, 0))
    bvec = _vspec((1, 1, d), lambda b, i: (b, 0, 0))

    def body(res_ref, y_ref, wp_ref, g_ref, w_ref, sc_ref, sh_ref, h_ref, u_ref):
        h = res_ref[...] + _post_f(y_ref[...], wp_ref[...], g_ref[0])
        h_ref[...] = h
        u_ref[...] = _pre_f(h, w_ref[...], sc_ref[0], sh_ref[0]).astype(BF16)

    return pl.pallas_call(body, name=name, grid=grid, in_specs=[row, row, vec, bvec, vec, bvec, bvec], out_specs=[row, row],
                          out_shape=[jax.ShapeDtypeStruct((t, d), F32), jax.ShapeDtypeStruct((t, d), BF16)],
                          )(res, y, w_post, g, w_pre, sc, sh)


def _post_loss_bwd(res, y, w, g, tgt, name, ts=512):
    t, d = y.shape
    bsz = g.shape[0]
    nt, grid = _tok_grid(t, bsz, ts)
    row = _vspec((ts, d), lambda b, i: (b * nt + i, 0))
    vec = _vspec((1, d), lambda b, i: (0, 0))
    bvec = _vspec((1, 1, d), lambda b, i: (b, 0, 0))

    def body(res_ref, y_ref, w_ref, g_ref, tgt_ref, dh_ref, dy_ref, dw_ref, dg_ref, sq_ref):
        b, i = pl.program_id(0), pl.program_id(1)
        part, vjp = jax.vjp(_post_f, y_ref[...], w_ref[...], g_ref[0])
        e = res_ref[...] + part - tgt_ref[...]
        dh = e * (1.0 / d)
        dh_ref[...] = dh
        dy, dw, dg = vjp(dh)
        dy_ref[...] = dy.astype(BF16)
        _accumulate(dw_ref, dw, (b == 0) & (i == 0))
        _accumulate(dg_ref, dg, i == 0)
        _accumulate(sq_ref, jnp.sum(e * e, axis=0, keepdims=True) * (1.0 / d), (b == 0) & (i == 0))

    return pl.pallas_call(
        body, name=name, grid=grid, in_specs=[row, row, vec, bvec, row], out_specs=[row, row, vec, bvec, vec],
        out_shape=[jax.ShapeDtypeStruct((t, d), F32), jax.ShapeDtypeStruct((t, d), BF16), jax.ShapeDtypeStruct((1, d), F32),
                   jax.ShapeDtypeStruct((bsz, 1, d), F32), jax.ShapeDtypeStruct((1, d), F32)],
    )(res, y, w, g, tgt)


def _pre_post_bwd(x, w, sc, sh, du, dres, y, w_post, g, name, ts=512):
    t, d = x.shape
    bsz = sc.shape[0]
    nt, grid = _tok_grid(t, bsz, ts)
    row = _vspec((ts, d), lambda b, i: (b * nt + i, 0))
    vec = _vspec((1, d), lambda b, i: (0, 0))
    bvec = _vspec((1, 1, d), lambda b, i: (b, 0, 0))

    def body(x_ref, w_ref, sc_ref, sh_ref, du_ref, dres_ref, y_ref, wp_ref, g_ref,
             dx_ref, dy_ref, dw_ref, dsc_ref, dsh_ref, dwp_ref, dg_ref):
        b, i = pl.program_id(0), pl.program_id(1)
        _, vjp = jax.vjp(_pre_f, x_ref[...], w_ref[...], sc_ref[0], sh_ref[0])
        dx, dw, dsc, dsh = vjp(du_ref[...])
        dx = dres_ref[...] + dx
        dx_ref[...] = dx
        _, vjp_post = jax.vjp(_post_f, y_ref[...], wp_ref[...], g_ref[0])
        dy, dwp, dg = vjp_post(dx)
        dy_ref[...] = dy.astype(BF16)
        first = (b == 0) & (i == 0)
        _accumulate(dw_ref, dw, first)
        _accumulate(dwp_ref, dwp, first)
        _accumulate(dsc_ref, dsc, i == 0)
        _accumulate(dsh_ref, dsh, i == 0)
        _accumulate(dg_ref, dg, i == 0)

    v1, vb = jax.ShapeDtypeStruct((1, d), F32), jax.ShapeDtypeStruct((bsz, 1, d), F32)
    return pl.pallas_call(
        body, name=name, grid=grid, in_specs=[row, vec, bvec, bvec, row, row, row, vec, bvec],
        out_specs=[row, row, vec, bvec, bvec, vec, bvec],
        out_shape=[jax.ShapeDtypeStruct((t, d), F32), jax.ShapeDtypeStruct((t, d), BF16), v1, vb, vb, v1, vb],
    )(x, w, sc, sh, du, dres, y, w_post, g)


def _merge_f(ga, gd, ya, yd):
    return jax.nn.sigmoid(ga) * ya + jax.nn.sigmoid(gd) * yd


_MW = 512


def _merge_fwd(proj, ya, yd, ts=512):
    t, d = ya.shape
    blk = _vspec((ts, _MW), lambda i, j: (i, j))
    ga = _vspec((ts, _MW), lambda i, j: (i, C_GA // _MW + j))
    gd = _vspec((ts, _MW), lambda i, j: (i, C_GD // _MW + j))

    def body(ga_ref, gd_ref, ya_ref, yd_ref, o_ref):
        o_ref[...] = _merge_f(ga_ref[...], gd_ref[...], ya_ref[...], yd_ref[...]).astype(BF16)

    return pl.pallas_call(body, name="merge_fwd", grid=(t // ts, d // _MW), in_specs=[ga, gd, blk, blk], out_specs=blk,
                          out_shape=jax.ShapeDtypeStruct((t, d), BF16))(proj, proj, ya, yd)


def _merge_bwd(proj, ya, yd, dm, ts=512):
    t, d = ya.shape
    blk = _vspec((ts, _MW), lambda i, j: (i, j))
    ga = _vspec((ts, _MW), lambda i, j: (i, C_GA // _MW + j))
    gd = _vspec((ts, _MW), lambda i, j: (i, C_GD // _MW + j))

    def body(ga_ref, gd_ref, ya_ref, yd_ref, dm_ref, dga_ref, dgd_ref, dya_ref, dyd_ref):
        _, vjp = jax.vjp(_merge_f, ga_ref[...], gd_ref[...], ya_ref[...], yd_ref[...])
        dga, dgd, dya, dyd = vjp(dm_ref[...])
        dga_ref[...] = dga.astype(BF16)
        dgd_ref[...] = dgd.astype(BF16)
        dya_ref[...] = dya.astype(BF16)
        dyd_ref[...] = dyd.astype(BF16)

    o = jax.ShapeDtypeStruct((t, d), BF16)
    return pl.pallas_call(body, name="merge_bwd", grid=(t // ts, d // _MW), in_specs=[ga, gd, blk, blk, blk],
                          out_specs=[blk] * 4, out_shape=[o] * 4)(proj, proj, ya, yd, dm)


def _shift_down(x, s):
    if s == 0:
        return x
    r = lax.broadcasted_iota(jnp.int32, x.shape, 0)
    return jnp.where(r >= s, pltpu.roll(x, s, 0), 0.0)


def _shift_up(x, s):
    if s == 0:
        return x
    n = x.shape[0]
    r = lax.broadcasted_iota(jnp.int32, x.shape, 0)
    return jnp.where(r < n - s, pltpu.roll(x, n - s, 0), 0.0)


def _conv_fwd(x, w, k):
    out = None
    for j in range(k):
        term = w[j:j + 1, :] * _shift_down(x, k - 1 - j)
        out = term if out is None else out + term
    return out


def _conv_bwd(x, w, dc, k):
    dx = None
    dws = []
    for j in range(k):
        up = _shift_up(dc, k - 1 - j)
        term = w[j:j + 1, :] * up
        dx = term if dx is None else dx + term
        dws.append(jnp.sum(up * x, axis=0, keepdims=True))
    return dx, jnp.concatenate(dws, axis=0)


def _geglu_f(gate, val):
    return jax.nn.gelu(gate, approximate=True) * val


_FW = 256


def _ffn_act_fwd(up, conv_w, bsz):
    t = up.shape[0]
    s = t // bsz
    nj = D_FF // _FW
    xg = _vspec((s, _FW), lambda b, j: (b, j))
    xv = _vspec((s, _FW), lambda b, j: (b, nj + j))
    wg = _vspec((FFN_CONV, _FW), lambda b, j: (0, j))
    wv = _vspec((FFN_CONV, _FW), lambda b, j: (0, nj + j))

    def body(xg_ref, xv_ref, wg_ref, wv_ref, o_ref):
        gate = _conv_fwd(xg_ref[...], wg_ref[...], FFN_CONV)
        val = _conv_fwd(xv_ref[...], wv_ref[...], FFN_CONV)
        o_ref[...] = _geglu_f(gate, val).astype(BF16)

    return pl.pallas_call(body, name="ffn_act_fwd", grid=(bsz, nj), in_specs=[xg, xv, wg, wv],
                          out_specs=_vspec((s, _FW), lambda b, j: (b, j)),
                          out_shape=jax.ShapeDtypeStruct((t, D_FF), BF16))(up, up, conv_w, conv_w)


def _ffn_act_bwd(up, conv_w, dact, bsz, plug=None):
    t = up.shape[0]
    s = t // bsz
    nj = D_FF // _FW
    xg = _vspec((s, _FW), lambda j, b: (b, j))
    xv = _vspec((s, _FW), lambda j, b: (b, nj + j))
    wg = _vspec((FFN_CONV, _FW), lambda j, b: (0, j))
    wv = _vspec((FFN_CONV, _FW), lambda j, b: (0, nj + j))
    da = _vspec((s, _FW), lambda j, b: (b, j))
    dwo = _vspec((FFN_CONV, _FW), lambda j, b: (0, j))

    def body(xg_ref, xv_ref, wg_ref, wv_ref, da_ref, dxg_ref, dxv_ref, dwg_ref, dwv_ref):
        b = pl.program_id(1)
        xg_, xv_, wg_, wv_ = xg_ref[...], xv_ref[...], wg_ref[...], wv_ref[...]
        gate = _conv_fwd(xg_, wg_, FFN_CONV)
        val = _conv_fwd(xv_, wv_, FFN_CONV)
        _, vjp = jax.vjp(_geglu_f, gate, val)
        dgate, dval = vjp(da_ref[...])
        dxg, dwg = _conv_bwd(xg_, wg_, dgate, FFN_CONV)
        dxv, dwv = _conv_bwd(xv_, wv_, dval, FFN_CONV)
        dxg_ref[...] = dxg.astype(BF16)
        dxv_ref[...] = dxv.astype(BF16)

        @pl.when(b == 0)
        def _():
            dwg_ref[...] = jnp.zeros_like(dwg_ref)
            dwv_ref[...] = jnp.zeros_like(dwv_ref)

        dwg_ref[...] += dwg
        dwv_ref[...] += dwv

    outs, extra = _plugged_call(
        body, plug, _grid_ends((nj, bsz)), (up, up, conv_w, conv_w, dact), name="ffn_act_bwd", grid=(nj, bsz),
        in_specs=[xg, xv, wg, wv, da], out_specs=[da, da, dwo, dwo],
        out_shape=[jax.ShapeDtypeStruct((t, D_FF), BF16), jax.ShapeDtypeStruct((t, D_FF), BF16),
                   jax.ShapeDtypeStruct((FFN_CONV, D_FF), F32), jax.ShapeDtypeStruct((FFN_CONV, D_FF), F32)])
    return (*outs, extra)


def _bucket_table():
    qi = np.arange(WINDOW)[:, None]
    kj = np.arange(2 * WINDOW)[None, :]
    dist = WINDOW + qi - kj
    dc = np.maximum(dist, 0)
    max_exact = REL_BUCKETS // 2
    scaled = np.log(np.maximum(dc, 1).astype(np.float32) / np.float32(max_exact)) / np.float32(math.log(REL_MAX_DIST / max_exact))
    large = max_exact + (scaled.astype(np.float32) * np.float32(REL_BUCKETS - max_exact)).astype(np.int32)
    large = np.minimum(large, REL_BUCKETS - 1)
    bucket = np.where(dc < max_exact, dc, large).astype(np.int32)
    in_band = ((dist >= 0) & (dist < WINDOW)).astype(np.int32)
    return bucket, in_band


def _bias_build(rel_bias):
    bucket, _ = _bucket_table()

    def body(rb_ref, idx_ref, o_ref):
        h = pl.program_id(0)
        idx = idx_ref[...]
        acc = jnp.zeros(idx.shape, F32)
        for r in range(REL_BUCKETS):
            acc = jnp.where(idx == r, rb_ref[r, h], acc)
        o_ref[0] = acc

    return pl.pallas_call(
        body, name="bias_build", grid=(AQ_HEADS,),
        in_specs=[pl.BlockSpec(memory_space=pltpu.SMEM), _vspec((WINDOW, 2 * WINDOW), lambda h: (0, 0))],
        out_specs=_vspec((1, WINDOW, 2 * WINDOW), lambda h: (h, 0, 0)),
        out_shape=jax.ShapeDtypeStruct((AQ_HEADS, WINDOW, 2 * WINDOW), F32),
    )(rel_bias, jnp.asarray(bucket))


def _bias_grad(dbias):
    bucket, _ = _bucket_table()

    def body(db_ref, idx_ref, o_ref):
        idx = idx_ref[...]
        db = db_ref[0]
        lane = lax.broadcasted_iota(jnp.int32, (1, LANES), 1)
        acc = jnp.zeros((1, LANES), F32)
        for r in range(REL_BUCKETS):
            s = jnp.sum(jnp.sum(jnp.where(idx == r, db, 0.0), axis=1, keepdims=True), axis=0, keepdims=True)
            acc = jnp.where(lane == r, s, acc)
        o_ref[0] = acc

    return pl.pallas_call(
        body, name="bias_grad", grid=(AQ_HEADS,),
        in_specs=[_vspec((1, WINDOW, 2 * WINDOW), lambda h: (h, 0, 0)), _vspec((WINDOW, 2 * WINDOW), lambda h: (0, 0))],
        out_specs=_vspec((1, 1, LANES), lambda h: (h, 0, 0)),
        out_shape=jax.ShapeDtypeStruct((AQ_HEADS, 1, LANES), F32),
    )(dbias, jnp.asarray(bucket))


def _attn_mask(n):
    qi = lax.broadcasted_iota(jnp.int32, (WINDOW, 2 * WINDOW), 0)
    kj = lax.broadcasted_iota(jnp.int32, (WINDOW, 2 * WINDOW), 1)
    dist = WINDOW + qi - kj
    band = (dist >= 0) & (dist < WINDOW)
    return band & ((kj >= WINDOW) | (n > 0))


def _attn_probs(qk, bias, sink, mask):
    s = jnp.where(mask, qk * (A_HD ** -0.5) + bias, NEG_INF)
    m = jnp.maximum(jnp.max(s, axis=-1, keepdims=True), sink)
    p = jnp.exp(s - m)
    es = jnp.exp(sink - m)
    inv = 1.0 / (jnp.sum(p, axis=-1, keepdims=True) + es)
    return p * inv, es * inv


def _attn_fwd(proj, bias, sinks, bsz):
    t = proj.shape[0]
    s = t // bsz
    nb = s // WINDOW
    grp = AQ_HEADS // AKV_HEADS

    def body(q_ref, k_ref, v_ref, bias_ref, sink_ref, y_ref, kp_ref, vp_ref):
        kp_ref[0:WINDOW, :] = jnp.zeros((WINDOW, LANES), BF16)
        vp_ref[0:WINDOW, :] = jnp.zeros((WINDOW, LANES), BF16)
        kp_ref[WINDOW:, :] = k_ref[...].astype(BF16)
        vp_ref[WINDOW:, :] = v_ref[...].astype(BF16)

        def blk(n, carry):
            r0 = pl.multiple_of(n * WINDOW, WINDOW)
            mask = _attn_mask(n)
            kband = kp_ref[pl.ds(r0, 2 * WINDOW), :]
            vband = vp_ref[pl.ds(r0, 2 * WINDOW), :]
            qb = q_ref[pl.ds(r0, WINDOW), :].astype(BF16)
            heads = range(AQ_HEADS)
            hsl = lambda h: slice(h * A_HD, (h + 1) * A_HD)
            kbs = [kband[:, hsl(kv)] for kv in range(AKV_HEADS)]
            vbs = [vband[:, hsl(kv)] for kv in range(AKV_HEADS)]
            qks = [lax.dot_general(qb[:, hsl(h)], kbs[h // grp], _NT, preferred_element_type=F32) for h in heads]
            probs = [_attn_probs(qks[h], bias_ref[h], sink_ref[0, h], mask)[0] for h in heads]
            outs = [jnp.dot(probs[h].astype(BF16), vbs[h // grp], preferred_element_type=F32) for h in heads]
            y_ref[pl.ds(r0, WINDOW), :] = jnp.concatenate(outs, axis=1).astype(BF16)
            return carry

        lax.fori_loop(0, nb, blk, 0)

    return pl.pallas_call(
        body, name="attn_fwd", grid=(bsz,),
        in_specs=[_vspec((s, AQ), lambda b: (b, C_Q // AQ)), _vspec((s, AKV), lambda b: (b, C_K // AKV)),
                  _vspec((s, AKV), lambda b: (b, C_V // AKV)),
                  _vspec((AQ_HEADS, WINDOW, 2 * WINDOW), lambda b: (0, 0, 0)), pl.BlockSpec(memory_space=pltpu.SMEM)],
        out_specs=_vspec((s, AQ), lambda b: (b, 0)), out_shape=jax.ShapeDtypeStruct((t, AQ), BF16),
        scratch_shapes=[pltpu.VMEM((s + WINDOW, LANES), BF16), pltpu.VMEM((s + WINDOW, LANES), BF16)],
    )(proj, proj, proj, bias, sinks)


def _attn_bwd(proj, bias, sinks, dy, bsz, plug=None):
    t = proj.shape[0]
    s = t // bsz
    nb = s // WINDOW
    grp = AQ_HEADS // AKV_HEADS
    scale = A_HD ** -0.5

    def body(q_ref, k_ref, v_ref, bias_ref, sink_ref, dy_ref, dq_ref, dk_ref, dv_ref, dbias_ref, dsink_ref,
             kp_ref, vp_ref, dkp_ref, dvp_ref):
        b = pl.program_id(0)
        kp_ref[0:WINDOW, :] = jnp.zeros((WINDOW, LANES), BF16)
        vp_ref[0:WINDOW, :] = jnp.zeros((WINDOW, LANES), BF16)
        kp_ref[WINDOW:, :] = k_ref[...].astype(BF16)
        vp_ref[WINDOW:, :] = v_ref[...].astype(BF16)
        dkp_ref[...] = jnp.zeros_like(dkp_ref)
        dvp_ref[...] = jnp.zeros_like(dvp_ref)

        @pl.when(b == 0)
        def _():
            dbias_ref[...] = jnp.zeros_like(dbias_ref)
            dsink_ref[...] = jnp.zeros_like(dsink_ref)

        def blk(n, carry):
            r0 = pl.multiple_of(n * WINDOW, WINDOW)
            mask = _attn_mask(n)
            kband = kp_ref[pl.ds(r0, 2 * WINDOW), :]
            vband = vp_ref[pl.ds(r0, 2 * WINDOW), :]
            qb = q_ref[pl.ds(r0, WINDOW), :].astype(BF16)
            dyb = dy_ref[pl.ds(r0, WINDOW), :].astype(BF16)
            heads = range(AQ_HEADS)
            hsl = lambda h: slice(h * A_HD, (h + 1) * A_HD)
            kbs = [kband[:, hsl(kv)] for kv in range(AKV_HEADS)]
            vbs = [vband[:, hsl(kv)] for kv in range(AKV_HEADS)]
            qhs = [qb[:, hsl(h)] for h in heads]
            dyhs = [dyb[:, hsl(h)] for h in heads]
            qks = [lax.dot_general(qhs[h], kbs[h // grp], _NT, preferred_element_type=F32) for h in heads]
            dprobs = [lax.dot_general(dyhs[h], vbs[h // grp], _NT, preferred_element_type=F32) for h in heads]
            pbs, dsbs = [], []
            for h in heads:
                probs, psink = _attn_probs(qks[h], bias_ref[h], sink_ref[0, h], mask)
                rowdot = jnp.sum(probs * dprobs[h], axis=-1, keepdims=True)
                ds = probs * (dprobs[h] - rowdot)
                dbias_ref[h] += ds
                dsink_ref[h] += jnp.sum(-psink * rowdot, axis=0, keepdims=True) + jnp.zeros((1, LANES), F32)
                pbs.append(probs.astype(BF16))
                dsbs.append(ds.astype(BF16))
            dvhs = [lax.dot_general(pbs[h], dyhs[h], _TN, preferred_element_type=F32) for h in heads]
            dqs = [jnp.dot(dsbs[h], kbs[h // grp], preferred_element_type=F32) * scale for h in heads]
            dkhs = [lax.dot_general(dsbs[h], qhs[h], _TN, preferred_element_type=F32) * scale for h in heads]
            dks = [sum(dkhs[kv * grp + 1:(kv + 1) * grp], dkhs[kv * grp]) for kv in range(AKV_HEADS)]
            dvs = [sum(dvhs[kv * grp + 1:(kv + 1) * grp], dvhs[kv * grp]) for kv in range(AKV_HEADS)]
            dq_ref[pl.ds(r0, WINDOW), :] = jnp.concatenate(dqs, axis=1).astype(BF16)
            dkp_ref[pl.ds(r0, 2 * WINDOW), :] += jnp.concatenate(dks, axis=1)
            dvp_ref[pl.ds(r0, 2 * WINDOW), :] += jnp.concatenate(dvs, axis=1)
            return carry

        lax.fori_loop(0, nb, blk, 0)
        dk_ref[...] = dkp_ref[WINDOW:, :].astype(BF16)
        dv_ref[...] = dvp_ref[WINDOW:, :].astype(BF16)

    kvs = jax.ShapeDtypeStruct((t, AKV), BF16)
    outs, extra = _plugged_call(
        body, plug, _grid_ends((bsz,)), (proj, proj, proj, bias, sinks, dy), name="attn_bwd", grid=(bsz,),
        in_specs=[_vspec((s, AQ), lambda b: (b, C_Q // AQ)), _vspec((s, AKV), lambda b: (b, C_K // AKV)),
                  _vspec((s, AKV), lambda b: (b, C_V // AKV)),
                  _vspec((AQ_HEADS, WINDOW, 2 * WINDOW), lambda b: (0, 0, 0)), pl.BlockSpec(memory_space=pltpu.SMEM),
                  _vspec((s, AQ), lambda b: (b, 0))],
        out_specs=[_vspec((s, AQ), lambda b: (b, 0)), _vspec((s, AKV), lambda b: (b, 0)), _vspec((s, AKV), lambda b: (b, 0)),
                   _vspec((AQ_HEADS, WINDOW, 2 * WINDOW), lambda b: (0, 0, 0)), _vspec((AQ_HEADS, 1, LANES), lambda b: (0, 0, 0))],
        out_shape=[jax.ShapeDtypeStruct((t, AQ), BF16), kvs, kvs,
                   jax.ShapeDtypeStruct((AQ_HEADS, WINDOW, 2 * WINDOW), F32), jax.ShapeDtypeStruct((AQ_HEADS, 1, LANES), F32)],
        scratch_shapes=[pltpu.VMEM((s + WINDOW, LANES), BF16), pltpu.VMEM((s + WINDOW, LANES), BF16),
                        pltpu.VMEM((s + WINDOW, LANES), F32), pltpu.VMEM((s + WINDOW, LANES), F32)])
    return (*outs, extra)


def _dn_act_f(c, is_qk):
    a = jax.nn.silu(c)
    outs = []
    for h in range(DN_HEADS):
        ah = a[:, h * DN_HD:(h + 1) * DN_HD]
        nh = ah * lax.rsqrt(jnp.sum(ah * ah, axis=-1, keepdims=True) + L2_EPS)
        outs.append(jnp.where(is_qk, nh, ah))
    return jnp.concatenate(outs, axis=1)


def _dn_prep_fwd(proj, conv_w, bsz):
    t = proj.shape[0]
    s = t // bsz
    blk = _vspec((s, DN), lambda b, j: (b, j))
    wsp = _vspec((DN_CONV, DN), lambda b, j: (0, j))

    def body(x_ref, w_ref, o_ref):
        j = pl.program_id(1)
        o_ref[...] = _dn_act_f(_conv_fwd(x_ref[...], w_ref[...], DN_CONV), j < 2)

    return pl.pallas_call(body, name="dn_prep_fwd", grid=(bsz, 3), in_specs=[blk, wsp], out_specs=blk,
                          out_shape=jax.ShapeDtypeStruct((t, 3 * DN), F32))(proj, conv_w)


def _dn_prep_bwd(proj, conv_w, dqkvn, bsz):
    t = proj.shape[0]
    s = t // bsz
    blk = _vspec((s, DN), lambda j, b: (b, j))
    wsp = _vspec((DN_CONV, DN), lambda j, b: (0, j))

    def body(x_ref, w_ref, d_ref, dx_ref, dw_ref):
        j, b = pl.program_id(0), pl.program_id(1)
        x, w = x_ref[...], w_ref[...]
        c = _conv_fwd(x, w, DN_CONV)
        _, vjp = jax.vjp(lambda cc: _dn_act_f(cc, j < 2), c)
        (dc,) = vjp(d_ref[0])
        dx, dw = _conv_bwd(x, w, dc, DN_CONV)
        dx_ref[...] = dx.astype(BF16)

        @pl.when(b == 0)
        def _():
            dw_ref[...] = jnp.zeros_like(dw_ref)

        dw_ref[...] += dw

    return pl.pallas_call(
        body, name="dn_prep_bwd", grid=(3, bsz),
        in_specs=[blk, wsp, _vspec((1, s, DN), lambda j, b: (j, b, 0))], out_specs=[blk, wsp],
        out_shape=[jax.ShapeDtypeStruct((t, 3 * DN), BF16), jax.ShapeDtypeStruct((DN_CONV, 3 * DN), F32)],
    )(proj, conv_w, dqkvn)


def _bg_f(x, alog, dt):
    lane = lax.broadcasted_iota(jnp.int32, x.shape, 1)
    beta = jax.nn.sigmoid(x)
    g = -jnp.exp(alog) * jax.nn.softplus(x + dt)
    return jnp.where(lane < DN_HEADS, beta, jnp.where(lane < 2 * DN_HEADS, g, 0.0))


def _bg_fwd(proj, alog, dt, bsz):
    t = proj.shape[0]
    s = t // bsz
    vec = _vspec((1, LANES), lambda b: (0, 0))

    def body(x_ref, a_ref, d_ref, o_ref):
        o_ref[...] = _bg_f(x_ref[...], a_ref[...], d_ref[...])

    return pl.pallas_call(body, name="bg_fwd", grid=(bsz,), in_specs=[_vspec((s, LANES), lambda b: (b, C_BD // LANES)), vec, vec],
                          out_specs=_vspec((s, LANES), lambda b: (b, 0)), out_shape=jax.ShapeDtypeStruct((t, LANES), F32))(proj, alog, dt)


def _bg_bwd(proj, alog, dt, dbg4, bsz):
    t = proj.shape[0]
    s = t // bsz
    vec = _vspec((1, LANES), lambda b: (0, 0))

    def body(x_ref, a_ref, d_ref, g4_ref, dx_ref, da_ref, dd_ref):
        b = pl.program_id(0)
        lane = lax.broadcasted_iota(jnp.int32, (s, LANES), 1)
        dbg = jnp.zeros((s, LANES), F32)
        for h in range(DN_HEADS):
            gh = g4_ref[:, h * DN_HD:(h + 1) * DN_HD]
            dbg = jnp.where(lane == h, gh[:, 0:1], dbg)
            dbg = jnp.where(lane == DN_HEADS + h, gh[:, 1:2], dbg)
        _, vjp = jax.vjp(_bg_f, x_ref[...], a_ref[...], d_ref[...])
        dx, da, dd = vjp(dbg)
        dx_ref[...] = dx.astype(BF16)

        @pl.when(b == 0)
        def _():
            da_ref[...] = jnp.zeros_like(da_ref)
            dd_ref[...] = jnp.zeros_like(dd_ref)

        da_ref[...] += da
        dd_ref[...] += dd

    return pl.pallas_call(
        body, name="bg_bwd", grid=(bsz,),
        in_specs=[_vspec((s, LANES), lambda b: (b, C_BD // LANES)), vec, vec, _vspec((s, DN), lambda b: (b, 0))],
        out_specs=[_vspec((s, LANES), lambda b: (b, 0)), vec, vec],
        out_shape=[jax.ShapeDtypeStruct((t, LANES), BF16), jax.ShapeDtypeStruct((1, LANES), F32), jax.ShapeDtypeStruct((1, LANES), F32)],
    )(proj, alog, dt, dbg4)


def _dn_out_f(o, z, w):
    outs = []
    for h in range(DN_HEADS):
        sl = slice(h * DN_HD, (h + 1) * DN_HD)
        outs.append(_rms(o[:, sl], w) * jax.nn.silu(z[:, sl]))
    return jnp.concatenate(outs, axis=1)


def _dn_out_fwd(o, proj, w, ts=512):
    t = o.shape[0]
    blk = _vspec((ts, DN), lambda i: (i, 0))
    zsp = _vspec((ts, DN), lambda i: (i, C_DZ // DN))
    vec = _vspec((1, DN_HD), lambda i: (0, 0))

    def body(o_ref, z_ref, w_ref, y_ref):
        y_ref[...] = _dn_out_f(o_ref[...], z_ref[...], w_ref[...]).astype(BF16)

    return pl.pallas_call(body, name="dn_out_fwd", grid=(t // ts,), in_specs=[blk, zsp, vec], out_specs=blk,
                          out_shape=jax.ShapeDtypeStruct((t, DN), BF16))(o, proj, w)


def _dn_out_bwd(o, proj, w, dy, ts=512):
    t = o.shape[0]
    blk = _vspec((ts, DN), lambda i: (i, 0))
    zsp = _vspec((ts, DN), lambda i: (i, C_DZ // DN))
    vec = _vspec((1, DN_HD), lambda i: (0, 0))

    def body(o_ref, z_ref, w_ref, dy_ref, do_ref, dz_ref, dw_ref):
        i = pl.program_id(0)
        _, vjp = jax.vjp(_dn_out_f, o_ref[...], z_ref[...], w_ref[...])
        do, dz, dw = vjp(dy_ref[...])
        do_ref[...] = do
        dz_ref[...] = dz.astype(BF16)

        @pl.when(i == 0)
        def _():
            dw_ref[...] = jnp.zeros_like(dw_ref)

        dw_ref[...] += dw

    return pl.pallas_call(
        body, name="dn_out_bwd", grid=(t // ts,), in_specs=[blk, zsp, vec, blk], out_specs=[blk, blk, vec],
        out_shape=[jax.ShapeDtypeStruct((t, DN), F32), jax.ShapeDtypeStruct((t, DN), BF16), jax.ShapeDtypeStruct((1, DN_HD), F32)],
    )(o, proj, w, dy)


_C = DN_CHUNK


def _dot(a, b, dims):
    return lax.dot_general(a.astype(BF16), b.astype(BF16), dims, preferred_element_type=F32)


def _split(a):
    hi = a.astype(BF16)
    return hi, (a - hi.astype(F32)).astype(BF16)


def _dot3(a, b, dims):
    (ah, al), (bh, bl) = (a if isinstance(a, tuple) else _split(a)), (b if isinstance(b, tuple) else _split(b))
    mm = lambda x, y: lax.dot_general(x, y, dims, preferred_element_type=F32)
    return mm(ah, bh) + (mm(ah, bl) + mm(al, bh))


_NN = (((1,), (0,)), ((), ()))
_NT = (((1,), (1,)), ((), ()))
_TN = (((0,), (0,)), ((), ()))


_SUB = 8


def _tri_inverses(ls, lts):
    ri8 = lax.broadcasted_iota(jnp.int32, (_SUB, _C), 0)
    ci8 = lax.broadcasted_iota(jnp.int32, (_SUB, _C), 1)
    nblk = _C // _SUB
    ts = []
    for lt in lts:
        blocks = [jnp.where(ci8 == ri8 + _SUB * b, 1.0, 0.0).astype(F32) for b in range(nblk)]
        for r in range(1, _SUB):
            for b in range(nblk):
                coef = lt[_SUB * b:_SUB * (b + 1), _SUB * b + r:_SUB * b + r + 1]
                row = jnp.sum(coef * blocks[b], axis=0, keepdims=True)
                blocks[b] = jnp.where(ri8 == r, blocks[b] - row, blocks[b])
        ts.append(jnp.concatenate(blocks, axis=0))
    ri = lax.broadcasted_iota(jnp.int32, (_C, _C), 0)
    ci = lax.broadcasted_iota(jnp.int32, (_C, _C), 1)
    s = _SUB
    while s < _C:
        shift = s.bit_length()
        quad = ((ri >> shift) == (ci >> shift)) & ((ri & s) != 0) & ((ci & s) == 0)
        offs = [jnp.where(quad, l, 0.0) for l in ls]
        tsp = [_split(t) for t in ts]
        left = [_dot3(tp, off, _NN) for tp, off in zip(tsp, offs)]
        ts = [t - _dot3(lo, tp, _NN) for t, lo, tp in zip(ts, left, tsp)]
        s *= 2
    return ts


_SEG = 512
_HEADS = tuple(range(DN_HEADS))


def _hsl(hh):
    return slice(hh * DN_HD, (hh + 1) * DN_HD)


def _chunk_specs(bsz, nseg, reverse):
    seg = (lambda i: nseg - 1 - i) if reverse else (lambda i: i)
    ncs = _SEG // _C
    col = lambda off: _vspec((bsz, _SEG, DN), lambda i: (0, seg(i), off))
    return (col, _vspec((bsz, _SEG, LANES), lambda i: (0, seg(i), 0)),
            _vspec((bsz, DN_HEADS, ncs, _C), lambda i: (0, 0, seg(i), 0)),
            _vspec((bsz, DN_HEADS, ncs, DN_HD, DN_HD), lambda i: (0, 0, seg(i), 0, 0)),
            _vspec((bsz, DN_HEADS, ncs, _C, _C), lambda i: (0, 0, seg(i), 0, 0)))


def _chunk_pre(q_ref, k_ref, v_ref, bg_ref, gr_ref, c, bb, hh):
    r0 = pl.multiple_of(c * _C, _C)
    ri = lax.broadcasted_iota(jnp.int32, (_C, _C), 0)
    ci = lax.broadcasted_iota(jnp.int32, (_C, _C), 1)
    q = q_ref[bb, pl.ds(r0, _C), _hsl(hh)] * (DN_HD ** -0.5)
    k = k_ref[bb, pl.ds(r0, _C), _hsl(hh)]
    v = v_ref[bb, pl.ds(r0, _C), _hsl(hh)]
    bgc = bg_ref[bb, pl.ds(r0, _C), :]
    beta = bgc[:, hh:hh + 1]
    g_col = bgc[:, DN_HEADS + hh:DN_HEADS + hh + 1]
    g_row = gr_ref[bb, hh, pl.ds(c, 1), :]
    gc_col = jnp.sum(jnp.where(ri >= ci, g_row, 0.0), axis=1, keepdims=True)
    gc_row = jnp.sum(jnp.where(ri <= ci, g_col, 0.0), axis=0, keepdims=True)
    gc_last = jnp.sum(g_col, axis=0, keepdims=True)
    diff = gc_col - gc_row
    decay = jnp.where(ri >= ci, jnp.exp(jnp.where(ri >= ci, diff, 0.0)), 0.0)
    diff_t = gc_row - gc_col
    decay_t = jnp.where(ri <= ci, jnp.exp(jnp.where(ri <= ci, diff_t, 0.0)), 0.0)
    eg = jnp.exp(gc_col)
    et = jnp.exp(gc_last - gc_col)
    gl = jnp.exp(gc_last)
    kb = k * beta
    vb = v * beta
    return dict(r0=r0, bb=bb, hh=hh, q=q, k=k, v=v, beta=beta, decay=decay, decay_t=decay_t, eg=eg, et=et, gl=gl, kb=kb, vb=vb,
                ri=ri, ci=ci)


def _chunk_solve(ms, tms=None):
    for m in ms:
        m["kk_t"] = _dot(m["k"], m["kb"], _NT)
        m["qk"] = _dot(m["q"], m["k"], _NT)
        m["kk"] = _dot(m["kb"], m["k"], _NT)
        if tms is not None:
            m["qk_t"] = _dot(m["k"], m["q"], _NT)
    if tms is None:
        tms = _tri_inverses([jnp.where(m["ri"] > m["ci"], m["kk"] * m["decay"], 0.0) for m in ms],
                            [jnp.where(m["ri"] < m["ci"], m["kk_t"] * m["decay_t"], 0.0) for m in ms])
    for m, tm in zip(ms, tms):
        m["tm_f32"] = tm
    for m in ms:
        rhs = jnp.concatenate([m["vb"], m["kb"] * m["eg"]], axis=1)
        m["tm"] = _split(m["tm_f32"])
        m["sol"] = _dot3(m["tm"], rhs, _NN)
        m["intra"] = jnp.where(m["ri"] >= m["ci"], m["qk"] * m["decay"], 0.0)


def _dn_chunk_fwd(qkvn, bg, g_rows, bsz, plug=None):
    t = qkvn.shape[0]
    s = t // bsz
    nc, nseg = s // _C, s // _SEG
    pairs = [(bb, hh) for bb in range(bsz) for hh in _HEADS]

    def body(q_ref, k_ref, v_ref, bg_ref, gr_ref, o_ref, st_ref, tm_ref, s_ref):
        @pl.when(pl.program_id(0) == 0)
        def _():
            s_ref[...] = jnp.zeros_like(s_ref)

        def chunk(c, carry):
            ms = [_chunk_pre(q_ref, k_ref, v_ref, bg_ref, gr_ref, c, bb, hh) for bb, hh in pairs]
            _chunk_solve(ms)
            sts = [s_ref[i] for i in range(len(pairs))]
            for m, st in zip(ms, sts):
                st_ref[m["bb"], m["hh"], c] = st
                tm_ref[m["bb"], m["hh"], c] = m["tm_f32"]
            ws = [_dot(m["sol"][:, DN_HD:], st, _NN) for m, st in zip(ms, sts)]
            qs = [_dot(m["q"] * m["eg"], st, _NN) for m, st in zip(ms, sts)]
            v_new = [m["sol"][:, :DN_HD] - a for m, a in zip(ms, ws)]
            iv = [_dot(m["intra"], vn, _NN) for m, vn in zip(ms, v_new)]
            upd = [_dot(m["k"] * m["et"], vn, _TN) for m, vn in zip(ms, v_new)]
            for i, (bb, hh) in enumerate(pairs):
                s_ref[i] = sts[i] * ms[i]["gl"] + upd[i]
                o_ref[bb, pl.ds(ms[i]["r0"], _C), _hsl(hh)] = qs[i] + iv[i]
            return carry

        lax.fori_loop(0, _SEG // _C, chunk, 0)

    col, bgs, grs, sts_spec, tms_spec = _chunk_specs(bsz, nseg, False)
    q3, bg3 = qkvn.reshape(bsz, s, 3 * DN), bg.reshape(bsz, s, LANES)
    (o, states, tms), extra = _plugged_call(
        body, plug, _grid_ends((nseg,)), (q3, q3, q3, bg3, g_rows), name="dn_chunk_fwd", grid=(nseg,),
        in_specs=[col(0), col(1), col(2), bgs, grs], out_specs=[col(0), sts_spec, tms_spec],
        out_shape=[jax.ShapeDtypeStruct((bsz, s, DN), F32), jax.ShapeDtypeStruct((bsz, DN_HEADS, nc, DN_HD, DN_HD), F32),
                   jax.ShapeDtypeStruct((bsz, DN_HEADS, nc, _C, _C), F32)],
        scratch_shapes=[pltpu.VMEM((bsz * DN_HEADS, DN_HD, DN_HD), F32)])
    return o.reshape(t, DN), (states, tms), extra


def _dn_chunk_bwd(qkvn, bg, g_rows, states, do, bsz, plug=None):
    t = qkvn.shape[0]
    s = t // bsz
    nc, nseg = s // _C, s // _SEG
    pairs = [(bb, hh) for bb in range(bsz) for hh in _HEADS]

    def body(q_ref, k_ref, v_ref, bg_ref, gr_ref, st_ref, tm_ref, do_ref, dqkv_ref, dbg_ref, ds_ref):
        @pl.when(pl.program_id(0) == 0)
        def _():
            ds_ref[...] = jnp.zeros_like(ds_ref)

        def chunk(cc, carry):
            c = _SEG // _C - 1 - cc
            ms = [_chunk_pre(q_ref, k_ref, v_ref, bg_ref, gr_ref, c, bb, hh) for bb, hh in pairs]
            _chunk_solve(ms, [tm_ref[bb, hh, c] for bb, hh in pairs])
            ri, ci = ms[0]["ri"], ms[0]["ci"]
            for i, m in enumerate(ms):
                m["st"] = st_ref[m["bb"], m["hh"], c]
                m["ds_out"] = ds_ref[i]
                m["do"] = do_ref[m["bb"], pl.ds(m["r0"], _C), _hsl(m["hh"])]
                m["w"] = m["sol"][:, DN_HD:]
            for m in ms:
                m["v_new"] = m["sol"][:, :DN_HD] - _dot(m["w"], m["st"], _NN)
            for m in ms:
                m["q_dec"], m["k_tail"] = m["q"] * m["eg"], m["k"] * m["et"]
                m["dk_tail"] = _dot(m["v_new"], m["ds_out"], _NT)
                m["dv_new"] = _dot(m["k_tail"], m["ds_out"], _NN) + _dot(m["intra"], m["do"], _TN)
                m["dq_dec"] = _dot(m["do"], m["st"], _NT)
                m["ds_in"] = m["ds_out"] * m["gl"] + _dot(m["q_dec"], m["do"], _TN)
                m["dintra"] = jnp.where(ri >= ci, _dot(m["do"], m["v_new"], _NT), 0.0)
                m["dintra_t"] = jnp.where(ri <= ci, _dot(m["v_new"], m["do"], _NT), 0.0)
            for m in ms:
                m["dw"] = -_dot(m["dv_new"], m["st"], _NT)
                m["ds_in"] = m["ds_in"] - _dot(m["w"], m["dv_new"], _TN)
            for m in ms:
                dsol = jnp.concatenate([m["dv_new"], m["dw"]], axis=1)
                m["drhs"] = _dot3(m["tm"], dsol, _TN)
            for m in ms:
                m["dl"] = jnp.where(ri > ci, -_dot(m["drhs"], m["sol"], _NT), 0.0)
                m["dl_t"] = jnp.where(ri < ci, -_dot(m["sol"], m["drhs"], _NT), 0.0)
            for m in ms:
                m["dkb2"] = _dot(m["dl"] * m["decay"], m["k"], _NN)
                m["dk"] = _dot(m["dl_t"] * m["decay_t"], m["kb"], _NN) + _dot(m["dintra_t"] * m["decay_t"], m["q"], _NN)
                m["dq"] = _dot(m["dintra"] * m["decay"], m["k"], _NN)
            for m in ms:
                _chunk_bwd_finish(m)
            for m in ms:
                ones_ge = jnp.where(ri <= ci, 1.0, 0.0).astype(BF16)
                gh, gl_ = _split(m["dgc"] + jnp.zeros((_C, LANES), F32))
                m["dg_b"] = jnp.dot(ones_ge, gh, preferred_element_type=F32) + jnp.dot(ones_ge, gl_, preferred_element_type=F32)
            lane = lax.broadcasted_iota(jnp.int32, (_C, LANES), 1)
            for i, m in enumerate(ms):
                bb, hh, rows = m["bb"], m["hh"], pl.ds(m["r0"], _C)
                dqkv_ref[0, bb, rows, _hsl(hh)] = m["dq"] * (DN_HD ** -0.5)
                dqkv_ref[1, bb, rows, _hsl(hh)] = m["dk"]
                dqkv_ref[2, bb, rows, _hsl(hh)] = m["dv"]
                dbg_ref[bb, rows, _hsl(hh)] = jnp.where(lane == 0, m["dbeta"], jnp.where(lane == 1, m["dg_b"], 0.0))
                ds_ref[i] = m["ds_in"]
            return carry

        lax.fori_loop(0, _SEG // _C, chunk, 0)

    col, bgs, grs, sts_spec, tms_spec = _chunk_specs(bsz, nseg, True)
    q3, bg3, do3 = qkvn.reshape(bsz, s, 3 * DN), bg.reshape(bsz, s, LANES), do.reshape(bsz, s, DN)
    (dqkv, dbg), extra = _plugged_call(
        body, plug, _grid_ends((nseg,)), (q3, q3, q3, bg3, g_rows, *states, do3), name="dn_chunk_bwd", grid=(nseg,),
        in_specs=[col(0), col(1), col(2), bgs, grs, sts_spec, tms_spec, col(0)],
        out_specs=[_vspec((3, bsz, _SEG, DN), lambda i: (0, 0, nseg - 1 - i, 0)), col(0)],
        out_shape=[jax.ShapeDtypeStruct((3, bsz, s, DN), F32), jax.ShapeDtypeStruct((bsz, s, DN), F32)],
        scratch_shapes=[pltpu.VMEM((bsz * DN_HEADS, DN_HD, DN_HD), F32)])
    return dqkv.reshape(3, t, DN), dbg.reshape(t, DN), extra


def _chunk_bwd_finish(m):
    q, k, v, beta, decay, decay_t = m["q"], m["k"], m["v"], m["beta"], m["decay"], m["decay_t"]
    eg, et, gl, kb, dl, dl_t, dintra, dintra_t = m["eg"], m["et"], m["gl"], m["kb"], m["dl"], m["dl_t"], m["dintra"], m["dintra_t"]
    dq_dec, dk_tail, dq, dk = m["dq_dec"], m["dk_tail"], m["dq"], m["dk"]
    dgl = jnp.sum(jnp.sum(m["ds_out"] * m["st"], axis=1, keepdims=True), axis=0, keepdims=True)
    dvb, dkbeg = m["drhs"][:, :DN_HD], m["drhs"][:, DN_HD:]
    dkb = dkbeg * eg + m["dkb2"]
    deg = jnp.sum(dkbeg * kb, axis=1, keepdims=True)
    em = (dl * m["kk"] + dintra * m["qk"]) * decay
    em_t = (dl_t * m["kk_t"] + dintra_t * m["qk_t"]) * decay_t
    dgc = jnp.sum(em, axis=1, keepdims=True) - jnp.sum(em_t, axis=1, keepdims=True)
    dq = dq + dq_dec * eg
    deg = deg + jnp.sum(dq_dec * q, axis=1, keepdims=True)
    dk = dk + dk_tail * et
    det = jnp.sum(dk_tail * k, axis=1, keepdims=True)
    dgc = dgc + deg * eg - det * et
    dgc_last = jnp.sum(det * et, axis=0, keepdims=True) + dgl * gl
    rcol = lax.broadcasted_iota(jnp.int32, (_C, 1), 0)
    m["dgc"] = dgc + jnp.where(rcol == _C - 1, dgc_last, 0.0)
    m["dq"] = dq
    m["dk"] = dk + dkb * beta
    m["dbeta"] = jnp.sum(dkb * k, axis=1, keepdims=True) + jnp.sum(dvb * v, axis=1, keepdims=True)
    m["dv"] = dvb * beta


def _mod_fwd(c_all, ada_w_loc, ada_b_loc):
    n, cols = c_all.shape[0], ada_w_loc.shape[1]

    def body(c_ref, w_ref, b_ref, o_ref):
        o_ref[...] = _dot(jax.nn.silu(c_ref[...]), w_ref[...], _NN) + b_ref[...]

    return pl.pallas_call(body, name="mod_fwd", out_shape=jax.ShapeDtypeStruct((n, cols), F32))(c_all, ada_w_loc, ada_b_loc)


def _ada_grad(c_all, dmod_loc, dmod_all):
    d, cols = c_all.shape[1], dmod_loc.shape[1]

    def body(c_ref, dl_ref, da_ref, gw_ref, gb_ref):
        gw_ref[...] = _dot(jax.nn.silu(c_ref[...]), dl_ref[...], _TN)
        gb_ref[...] = jnp.sum(da_ref[...], axis=0, keepdims=True)

    return pl.pallas_call(body, name="ada_grad", out_shape=[jax.ShapeDtypeStruct((d, cols), F32),
                                                           jax.ShapeDtypeStruct((1, dmod_all.shape[1]), F32)])(c_all, dmod_loc, dmod_all)


ELEMENTWISE_BLOCK_BYTES = 3 * 2 ** 19


def _row_tile(r, c=1024):
    fits = [tr for tr in range(16, r + 1, 16) if tr * c * 4 <= ELEMENTWISE_BLOCK_BYTES]
    if not fits:
        return r
    whole = [tr for tr in fits if r % tr == 0]
    return whole[-1] if whole else fits[-1]


def _adamw(w, m, v, grads, name):
    r, rest = w.shape[0], w.shape[1:]
    c = math.prod(rest)
    tr = _row_tile(r, c)
    blk = _vspec((tr,) + rest, lambda i: (i,) + (0,) * len(rest))
    n = len(grads)

    def body(*refs):
        w_ref, m_ref, v_ref = refs[:3]
        g_ref, d_ref, mo_ref, vo_ref = refs[3 + n:]
        g = refs[3][...]
        for p in refs[4:3 + n]:
            g = g + p[...]
        g_ref[...] = g
        d_ref[...], mo_ref[...], vo_ref[...] = _adamw_math(w_ref[...], m_ref[...], v_ref[...], g)

    o = jax.ShapeDtypeStruct(w.shape, F32)
    return pl.pallas_call(body, name=name, grid=(pl.cdiv(r, tr),), in_specs=[blk] * (3 + n), out_specs=[blk] * 4,
                          out_shape=[o] * 4)(w, m, v, *grads)


def _adamw_halves(w, m, v, own, other, ic, name):
    r, _, c = w.shape
    hc = c // 2
    tr = _row_tile(r, c)
    blk = _vspec((tr, 1, hc), lambda i, j: (i, 0, j))
    half = _vspec((tr, 1, hc), lambda i, j: (i, 0, 0))

    def body(ic_ref, w_ref, m_ref, v_ref, own_ref, other_ref, g_ref, d_ref, mo_ref, vo_ref):
        g = jnp.where(pl.program_id(1) == ic_ref[0], own_ref[...], other_ref[...])
        g_ref[...] = g
        d_ref[...], mo_ref[...], vo_ref[...] = _adamw_math(w_ref[...], m_ref[...], v_ref[...], g)

    o = jax.ShapeDtypeStruct(w.shape, F32)
    return pl.pallas_call(body, name=name, grid=(pl.cdiv(r, tr), 2),
                          in_specs=[pl.BlockSpec(memory_space=pltpu.SMEM), blk, blk, blk, half, half], out_specs=[blk] * 4,
                          out_shape=[o] * 4)(ic, w, m, v, own, other)


def _adamw_math(w, m, v, g):
    m_new = ADAM_B1 * m + (1.0 - ADAM_B1) * g
    v_new = ADAM_B2 * v + (1.0 - ADAM_B2) * jnp.square(g)
    m_hat = m_new / (1.0 - ADAM_B1 ** ADAM_STEP)
    v_hat = v_new / (1.0 - ADAM_B2 ** ADAM_STEP)
    return -ADAM_LR * (m_hat / (jnp.sqrt(v_hat) + ADAM_EPS) + ADAM_WD * w), m_new, v_new


def _adamw_small(ws, ms, vs, gs, name):
    n = len(ws)

    def body(*refs):
        for i in range(n):
            w_ref, m_ref, v_ref, g_ref = (refs[j * n + i] for j in range(4))
            go_ref, d_ref, mo_ref, vo_ref = refs[4 * n + 4 * i:4 * n + 4 * i + 4]
            g = g_ref[...]
            go_ref[...] = g
            d_ref[...], mo_ref[...], vo_ref[...] = _adamw_math(w_ref[...], m_ref[...], v_ref[...], g)

    res = pl.pallas_call(body, name=name, out_shape=[jax.ShapeDtypeStruct(a.shape, F32) for a in ws for _ in range(4)])(
        *ws, *ms, *vs, *gs)
    return [res[4 * i:4 * i + 4] for i in range(n)]


def _sum_lead(x, name, rows_apart=False):
    p, r, c = x.shape
    tr = _row_tile(r, c)
    mid = (1,) if rows_apart else ()

    def body(x_ref, o_ref):
        acc = x_ref[0].astype(F32)
        for i in range(1, p):
            acc = acc + x_ref[i].astype(F32)
        o_ref[...] = acc.reshape(o_ref.shape)

    return pl.pallas_call(body, name=name, grid=(pl.cdiv(r, tr),), in_specs=[_vspec((p, tr, c), lambda i: (0, i, 0))],
                          out_specs=_vspec((tr,) + mid + (c,), lambda i: (i,) + (0,) * (1 + len(mid))),
                          out_shape=jax.ShapeDtypeStruct((r,) + mid + (c,), F32))(x)


def _allgather8(x_shard, name):
    m_per, n = x_shard.shape

    def body(x_ref, out_ref, send_sems, recv_sems, local_sem):
        x, y, c = lax.axis_index("x"), lax.axis_index("y"), lax.axis_index("c")
        me, sibling = (x, y, c), (x, y, 1 - c)
        chips = [(1 - x, y), (x, 1 - y), (1 - x, 1 - y)]

        def rows(px, py, pc):
            return out_ref.at[pl.ds((4 * px + 2 * py + pc) * m_per, m_per), :]

        def copy(k, block, to, src=None):
            return pltpu.make_async_remote_copy(
                src_ref=rows(*block) if src is None else src, dst_ref=rows(*block), send_sem=send_sems.at[k],
                recv_sem=recv_sems.at[k], device_id=to, device_id_type=MESH)

        mine = pltpu.make_async_copy(x_ref, rows(*me), local_sem)
        mine.start()
        first = [copy(0, me, sibling, src=x_ref)]
        first += [copy(1 + j, me, (*chip, c), src=x_ref) for j, chip in enumerate(chips)]
        for cp in first:
            cp.start()
        passed = [copy(4 + j, (*chip, c), sibling) for j, chip in enumerate(chips)]
        for j, chip in enumerate(chips):
            copy(1 + j, (*chip, c), me).wait_recv()
            passed[j].start()
        copy(0, sibling, me).wait_recv()
        for j, chip in enumerate(chips):
            copy(4 + j, (*chip, 1 - c), me).wait_recv()
        for cp in first + passed:
            cp.wait_send()
        mine.wait()

    return pl.pallas_call(
        body, name=name, out_shape=jax.ShapeDtypeStruct((8 * m_per, n), x_shard.dtype),
        in_specs=[pl.BlockSpec(memory_space=pltpu.VMEM)], out_specs=pl.BlockSpec(memory_space=pltpu.VMEM),
        scratch_shapes=[pltpu.SemaphoreType.DMA((7,)), pltpu.SemaphoreType.DMA((7,)), pltpu.SemaphoreType.DMA],
    )(x_shard)


_HBM = pl.BlockSpec(memory_space=pltpu.HBM)


def _mesh_place():
    x, y, c = lax.axis_index("x"), lax.axis_index("y"), lax.axis_index("c")
    return x, y, c, 2 * x + y, [(1 - x, y), (x, 1 - y), (1 - x, 1 - y)]


def _gather_plug(shards):
    n = len(shards)

    def half(ref, c, lead=None):
        r, cols = ref.shape[-2] // 2, ref.shape[-1] // 2
        if r % 16 == 0:
            rows = pl.ds(pl.multiple_of(c * r, 16), r)
            return ref.at[rows, :] if lead is None else ref.at[lead, rows, :]
        lanes = pl.ds(pl.multiple_of(c * cols, LANES), cols)
        return ref.at[:, lanes] if lead is None else ref.at[lead, :, lanes]

    def copies(ins, outs, send, recv):
        x, y, c, me, chips = _mesh_place()
        ici, fwd, fwd_in = [], [], []
        for i in range(n):
            for j, (px, py) in enumerate(chips):
                q = 2 * px + py
                ici.append((pltpu.make_async_remote_copy(
                    src_ref=half(ins[i], c), dst_ref=half(outs[i], c, me), send_sem=send.at[6 * i + j], recv_sem=recv.at[6 * i + j],
                    device_id=(px, py, c), device_id_type=MESH),
                    pltpu.make_async_remote_copy(
                    src_ref=half(ins[i], c), dst_ref=half(outs[i], c, q), send_sem=send.at[6 * i + j], recv_sem=recv.at[6 * i + j],
                    device_id=(px, py, c), device_id_type=MESH)))
                fwd.append(pltpu.make_async_remote_copy(
                    src_ref=half(outs[i], c, q), dst_ref=half(outs[i], c, q), send_sem=send.at[6 * i + 3 + j],
                    recv_sem=recv.at[6 * i + 3 + j], device_id=(x, y, 1 - c), device_id_type=MESH))
                fwd_in.append(pltpu.make_async_remote_copy(
                    src_ref=half(outs[i], 1 - c, q), dst_ref=half(outs[i], 1 - c, q), send_sem=send.at[6 * i + 3 + j],
                    recv_sem=recv.at[6 * i + 3 + j], device_id=(x, y, 1 - c), device_id_type=MESH))
        return ici, fwd, fwd_in, me

    def start(ins, outs, send, recv, loc):
        ici, _, _, me = copies(ins, outs, send, recv)
        for i in range(n):
            pltpu.make_async_copy(ins[i], outs[i].at[me], loc.at[i]).start()
        for out_cp, _ in ici:
            out_cp.start()

    def finish(ins, outs, send, recv, loc):
        ici, fwd, fwd_in, me = copies(ins, outs, send, recv)
        for (_, in_cp), f in zip(ici, fwd):
            in_cp.wait_recv()
            f.start()
        for f in fwd_in:
            f.wait_recv()
        for (out_cp, _), f in zip(ici, fwd):
            out_cp.wait_send()
            f.wait_send()
        for i in range(n):
            pltpu.make_async_copy(ins[i], outs[i].at[me], loc.at[i]).wait()

    return dict(ins=list(shards), out_shape=[jax.ShapeDtypeStruct((N_CHIPS,) + a.shape, a.dtype) for a in shards],
                scratch=[pltpu.SemaphoreType.DMA((6 * n,)), pltpu.SemaphoreType.DMA((6 * n,)), pltpu.SemaphoreType.DMA((n,))],
                start=start, finish=finish)


def _exchange_plug(pieces):
    n = len(pieces)

    def copies(ins, outs, send, recv):
        x, y, c, me, chips = _mesh_place()
        out_cps, in_cps = [], []
        for i in range(n):
            for j, (px, py) in enumerate(chips):
                q = 2 * px + py
                out_cps.append(pltpu.make_async_remote_copy(src_ref=ins[i].at[q], dst_ref=outs[i].at[me], send_sem=send.at[3 * i + j],
                                                            recv_sem=recv.at[3 * i + j], device_id=(px, py, c), device_id_type=MESH))
                in_cps.append(pltpu.make_async_remote_copy(src_ref=ins[i].at[me], dst_ref=outs[i].at[q], send_sem=send.at[3 * i + j],
                                                           recv_sem=recv.at[3 * i + j], device_id=(px, py, c), device_id_type=MESH))
        return out_cps, in_cps, me

    def start(ins, outs, send, recv, loc):
        out_cps, _, me = copies(ins, outs, send, recv)
        for i in range(n):
            pltpu.make_async_copy(ins[i].at[me], outs[i].at[me], loc.at[i]).start()
        for cp in out_cps:
            cp.start()

    def finish(ins, outs, send, recv, loc):
        out_cps, in_cps, me = copies(ins, outs, send, recv)
        for cp in in_cps:
            cp.wait_recv()
        for cp in out_cps:
            cp.wait_send()
        for i in range(n):
            pltpu.make_async_copy(ins[i].at[me], outs[i].at[me], loc.at[i]).wait()

    return dict(ins=list(pieces), out_shape=[jax.ShapeDtypeStruct(a.shape, a.dtype) for a in pieces],
                scratch=[pltpu.SemaphoreType.DMA((3 * n,)), pltpu.SemaphoreType.DMA((3 * n,)), pltpu.SemaphoreType.DMA((n,))],
                start=start, finish=finish)


def _comm_call(plug, name):
    n_in, n_out = len(plug["ins"]), len(plug["out_shape"])

    def body(*refs):
        ins, outs, sems = refs[:n_in], refs[n_in:n_in + n_out], refs[n_in + n_out:]
        plug["start"](ins, outs, *sems)
        plug["finish"](ins, outs, *sems)

    return pl.pallas_call(body, name=name, out_shape=plug["out_shape"], in_specs=[_HBM] * n_in, out_specs=[_HBM] * n_out,
                          scratch_shapes=plug["scratch"])(*plug["ins"])


def _plugged_call(body, plug, first_last, args, *, name, grid, in_specs, out_specs, out_shape, scratch_shapes=()):
    in_specs, out_specs, out_shape, scratch_shapes = list(in_specs), list(out_specs), list(out_shape), list(scratch_shapes)
    if plug is None:
        return pl.pallas_call(body, name=name, grid=grid, in_specs=in_specs, out_specs=out_specs, out_shape=out_shape,
                              scratch_shapes=scratch_shapes)(*args), []
    n_in, n_out, n_sc = len(in_specs), len(out_specs), len(scratch_shapes)
    p_in, p_out = len(plug["ins"]), len(plug["out_shape"])

    def full(*refs):
        ins, refs = refs[:n_in], refs[n_in:]
        pins, refs = refs[:p_in], refs[p_in:]
        outs, refs = refs[:n_out], refs[n_out:]
        pouts, refs = refs[:p_out], refs[p_out:]
        scr, psems = refs[:n_sc], refs[n_sc:]
        first, last = first_last()

        @pl.when(first)
        def _():
            plug["start"](pins, pouts, *psems)

        body(*ins, *outs, *scr)

        @pl.when(last)
        def _():
            plug["finish"](pins, pouts, *psems)

    res = pl.pallas_call(full, name=name, grid=grid, in_specs=in_specs + [_HBM] * p_in, out_specs=out_specs + [_HBM] * p_out,
                         out_shape=out_shape + plug["out_shape"], scratch_shapes=scratch_shapes + plug["scratch"])(*args, *plug["ins"])
    return res[:n_out], res[n_out:]


def _grid_ends(grid):
    def ends():
        first = last = None
        for ax, n in enumerate(grid):
            i = pl.program_id(ax)
            first = (i == 0) if first is None else first & (i == 0)
            last = (i == n - 1) if last is None else last & (i == n - 1)
        return first, last
    return ends


def _pair_presum(pieces, name):
    n, r, cols = pieces.shape
    hc = cols // 2

    def body(p_ref, o_ref, mine_ref, land_ref, send_sem, recv_sem, local_sem):
        x, y, c = lax.axis_index("x"), lax.axis_index("y"), lax.axis_index("c")
        keep = pl.ds(pl.multiple_of(c * hc, LANES), hc)
        give = pl.ds(pl.multiple_of((1 - c) * hc, LANES), hc)
        push = pltpu.make_async_remote_copy(src_ref=p_ref.at[:, :, give], dst_ref=land_ref, send_sem=send_sem, recv_sem=recv_sem,
                                            device_id=(x, y, 1 - c), device_id_type=MESH)
        own = pltpu.make_async_copy(p_ref.at[:, :, keep], mine_ref, local_sem)
        push.start()
        own.start()
        own.wait()
        push.wait_recv()
        for i in range(n):
            o_ref[i] = (mine_ref[i].astype(F32) + land_ref[i].astype(F32)).astype(BF16)
        push.wait_send()

    return pl.pallas_call(
        body, name=name, out_shape=jax.ShapeDtypeStruct((n, r, hc), BF16), in_specs=[_HBM],
        out_specs=pl.BlockSpec(memory_space=pltpu.VMEM),
        scratch_shapes=[pltpu.VMEM((n, r, hc), BF16), pltpu.VMEM((n, r, hc), BF16), pltpu.SemaphoreType.DMA,
                        pltpu.SemaphoreType.DMA, pltpu.SemaphoreType.DMA],
    )(pieces)


def _sibling_exchange(arrs, name):
    n = len(arrs)

    def body(*refs):
        ins, outs = refs[:n], refs[n:2 * n]
        send_sems, recv_sems = refs[2 * n:]
        sibling = (lax.axis_index("x"), lax.axis_index("y"), 1 - lax.axis_index("c"))
        cps = [pltpu.make_async_remote_copy(src_ref=ins[i], dst_ref=outs[i], send_sem=send_sems.at[i], recv_sem=recv_sems.at[i],
                                            device_id=sibling, device_id_type=MESH) for i in range(n)]
        for cp in cps:
            cp.start()
        for cp in cps:
            cp.wait()

    return pl.pallas_call(
        body, name=name, out_shape=[jax.ShapeDtypeStruct(a.shape, a.dtype) for a in arrs], in_specs=[_HBM] * n, out_specs=[_HBM] * n,
        scratch_shapes=[pltpu.SemaphoreType.DMA((n,)), pltpu.SemaphoreType.DMA((n,))],
    )(*arrs)


_SEGS = ((0, AQ, C_Q), (AQ, AKV, C_K), (AQ + AKV, AKV, C_V), (AQ + 2 * AKV, 3 * DN, C_DQKV), (2304, DN, C_DZ),
         (2816, 2 * DN_HEADS, C_BD), (2824, D_MODEL, C_GA), (3848, D_MODEL, C_GD))
SHARD_ROWS = IN_DIM // N_CHIPS


def _to_padded(w4):
    wt = w4.reshape(IN_DIM, w4.shape[2])
    parts = [wt[o:o + n] for o, n, _ in sorted(_SEGS, key=lambda sg: sg[2])]
    return jnp.concatenate(parts + [jnp.zeros((IN_PAD - IN_DIM, wt.shape[1]), wt.dtype)], axis=0)


def _from_padded(gt):
    return jnp.concatenate([gt[ps:ps + n] for _, n, ps in sorted(_SEGS)], axis=0).reshape(N_CHIPS, SHARD_ROWS, gt.shape[1])


def _lane_vec(a):
    return jnp.zeros((1, LANES), F32).at[0, DN_HEADS:2 * DN_HEADS].set(a)


_ROW_SHARDED = ("w_in", "w_out", "ffn_w_down")
_FFN = ("ffn_w_up", "ffn_w_down")
_LATE_MIXER = ("w_attn_branch", "w_dn_branch", "w_out")


def _pieces(k, a):
    if a.ndim == 3:
        return a
    if k in _ROW_SHARDED:
        return a.reshape(N_CHIPS, a.shape[0] // N_CHIPS, a.shape[1]).astype(BF16)
    return jnp.transpose(a.reshape(a.shape[0], N_CHIPS, a.shape[1] // N_CHIPS), (1, 0, 2)).astype(BF16)


def _assemble(k, a):
    if k in _ROW_SHARDED:
        return a.reshape(-1, a.shape[2])
    return jnp.transpose(a, (1, 0, 2)).reshape(a.shape[1], -1)


def _device_step(x2, tgt2, mod, p, bsz, shards=None):
    d = D_MODEL
    on_mesh = shards is not None
    p = dict(p)
    sh1, sc1, g1, sh2, sc2, g2 = [mod[:, i * d:(i + 1) * d].reshape(bsz, 1, d) for i in range(N_MOD)]
    alog_v, dt_v = _lane_vec(p["dn_a_log"]), _lane_vec(p["dn_dt_bias"])
    sinks = p["attn_sinks"].reshape(1, AQ_HEADS)
    u1 = _pre_fwd(x2, p["norm_mix_pre"], sc1, sh1, "pre1_fwd")
    if on_mesh:
        proj, got = _mm(u1, p["w_in"], "nt", F32, "mm_proj", _gather_plug(shards["late_mixer"]))
        p.update({k: _assemble(k, a) for k, a in zip(_LATE_MIXER, got)})
    else:
        proj = _mm(u1, p["w_in"], "nt", F32, "mm_proj")
    bias = _bias_build(p["rel_bias"])
    y_attn = _attn_fwd(proj, bias, sinks, bsz)
    qkvn = _dn_prep_fwd(proj, p["dn_conv_w"], bsz)
    bg = _bg_fwd(proj, alog_v, dt_v, bsz)
    nc = x2.shape[0] // bsz // DN_CHUNK
    g_rows = jnp.transpose(bg[:, DN_HEADS:2 * DN_HEADS].reshape(bsz, nc, DN_CHUNK, DN_HEADS), (0, 3, 1, 2))
    o, states, got = _dn_chunk_fwd(qkvn, bg, g_rows, bsz, _gather_plug(shards["ffn"][:1]) if on_mesh else None)
    for k, a in zip(_FFN[:1], got):
        p[k] = _assemble(k, a)
    y_dn = _dn_out_fwd(o, proj, p["dn_norm_w"])
    ya = _mm(y_attn, p["w_attn_branch"], "nn", F32, "mm_ya")
    yd = _mm(y_dn, p["w_dn_branch"], "nn", F32, "mm_yd")
    merged = _merge_fwd(proj, ya, yd)
    y1 = _mm(merged, p["w_out"], "nn", F32, "mm_y1")
    h1, u2 = _post_pre_fwd(x2, y1, p["norm_mix_post"], g1, p["norm_ffn_pre"], sc2, sh2, "post1_pre2_fwd")
    if on_mesh:
        up, got = _mm(u2, p["ffn_w_up"], "nn", F32, "mm_up", _gather_plug(shards["ffn"][1:]))
        p["ffn_w_down"] = _assemble("ffn_w_down", got[0])
    else:
        up = _mm(u2, p["ffn_w_up"], "nn", F32, "mm_up")
    act = _ffn_act_fwd(up, p["ffn_conv_w"], bsz)
    y2 = _mm(act, p["ffn_w_down"], "nn", F32, "mm_y2")
    dh2, dy2, g_ffn_post, dg2, sq = _post_loss_bwd(h1, y2, p["norm_ffn_post"], g2, tgt2, "post2_loss_bwd")
    g = {}
    g["norm_ffn_post"] = g_ffn_post
    dact = _mm(dy2, p["ffn_w_down"], "nt", F32, "mm_dact")
    g["ffn_w_down"] = _mm(act, dy2, "tn", BF16, "mm_dwdown")
    dupg, dupv, dcwg, dcwv, got_down = _ffn_act_bwd(
        up, p["ffn_conv_w"], dact, bsz, _exchange_plug([_pieces("ffn_w_down", g["ffn_w_down"])]) if on_mesh else None)
    g["ffn_conv_w"] = jnp.concatenate([dcwg, dcwv], axis=1)
    dup = jnp.concatenate([dupg, dupv], axis=1)
    g["ffn_w_up"] = _mm(u2, dup, "tn", BF16, "mm_dwup", split=N_CHIPS)
    du2 = _mm(dup, p["ffn_w_up"], "nt", F32, "mm_du2")
    dh1, dy1, g["norm_ffn_pre"], dsc2, dsh2, g["norm_mix_post"], dg1 = _pre_post_bwd(
        h1, p["norm_ffn_pre"], sc2, sh2, du2, dh2, y1, p["norm_mix_post"], g1, "pre2_post1_bwd")
    dmerged = _mm(dy1, p["w_out"], "nt", F32, "mm_dmerged")
    g["w_out"] = _mm(merged, dy1, "tn", BF16, "mm_dwout")
    dga, dgd, dya, dyd = _merge_bwd(proj, ya, yd, dmerged)
    dy_attn = _mm(dya, p["w_attn_branch"], "nt", BF16, "mm_dyattn")
    g["w_attn_branch"] = _mm(y_attn, dya, "tn", BF16, "mm_dwab", split=N_CHIPS)
    dy_dn = _mm(dyd, p["w_dn_branch"], "nt", F32, "mm_dydn")
    g["w_dn_branch"] = _mm(y_dn, dyd, "tn", BF16, "mm_dwdb", split=N_CHIPS)
    do, dz, g["dn_norm_w"] = _dn_out_bwd(o, proj, p["dn_norm_w"], dy_dn)
    def plug_for(names):
        return _exchange_plug([_pieces(k, g[k]) for k in names]) if on_mesh else None

    early = ("w_out", "w_attn_branch", "w_dn_branch")
    dqkvn, dbg4, got_up = _dn_chunk_bwd(qkvn, bg, g_rows, states, do, bsz, plug_for(_FFN[:1]))
    got_ffn = list(got_up) + list(got_down)
    d_dqkv, g["dn_conv_w"] = _dn_prep_bwd(proj, p["dn_conv_w"], dqkvn, bsz)
    dbd, g["dn_a_log"], g["dn_dt_bias"] = _bg_bwd(proj, alog_v, dt_v, dbg4, bsz)
    dq, dk, dv, dbias, g["attn_sinks"], got_early = _attn_bwd(proj, bias, sinks, dy_attn, bsz, plug_for(early))
    g["rel_bias"] = _bias_grad(dbias)
    dproj = jnp.concatenate([d_dqkv, dq, dz, dga, dgd, dk, dv, dbd], axis=1)
    g["w_in"] = _from_padded(_mm(dproj, u1, "tn", BF16, "mm_dwin"))
    if on_mesh:
        g["w_in"] = _pair_presum(g["w_in"], "presum_w_in")
        du1, got_in = _mm(dproj, p["w_in"], "nn", F32, "mm_du1", plug_for(("w_in",)))
        g.update(zip(_FFN + early + ("w_in",), list(got_ffn) + list(got_early) + list(got_in)))
    else:
        du1 = _mm(dproj, p["w_in"], "nn", F32, "mm_du1")
    dx, g["norm_mix_pre"], dsc1, dsh1 = _pre_bwd(x2, p["norm_mix_pre"], sc1, sh1, du1, dh1, "pre1_bwd")
    dmod = jnp.concatenate([dsh1, dsc1, dg1, dsh2, dsc2, dg2], axis=-1).reshape(bsz, N_MOD * d)
    return sq, dx, dmod, g


_SMALL = (("norm_mix_pre", D_MODEL), ("norm_mix_post", D_MODEL), ("norm_ffn_pre", D_MODEL), ("norm_ffn_post", D_MODEL),
          ("dn_norm_w", DN_HD), ("dn_a_log", LANES), ("dn_dt_bias", LANES), ("attn_sinks", AQ_HEADS * LANES),
          ("rel_bias", AQ_HEADS * LANES), ("dn_conv_w", DN_CONV * 3 * DN), ("ffn_conv_w", FFN_CONV * 2 * D_FF))


def _pack_rows(parts, rows):
    flat = jnp.concatenate([a.reshape(-1) for a in parts])
    return jnp.concatenate([flat, jnp.zeros((rows * LANES - flat.shape[0],), F32)]).reshape(rows, LANES)


def _pad128(a):
    flat = a.reshape(-1)
    n = -(-flat.shape[0] // LANES) * LANES
    return jnp.concatenate([flat, jnp.zeros((n - flat.shape[0],), F32)]) if n != flat.shape[0] else flat


_W_NAMES = ("ada_w", "ada_b", "norm_mix_pre", "norm_mix_post", "norm_ffn_pre", "norm_ffn_post", "w_in", "dn_conv_w", "dn_a_log",
            "dn_dt_bias", "dn_norm_w", "attn_sinks", "rel_bias", "w_attn_branch", "w_dn_branch", "w_out", "ffn_w_up", "ffn_conv_w",
            "ffn_w_down")
_BIG = ("w_in", "w_attn_branch", "w_dn_branch", "w_out", "ffn_w_up", "ffn_w_down")


def kernel(x, c, *rest):
    nw = len(_W_NAMES)
    w = dict(zip(_W_NAMES, rest[:nw]))
    loss_target = rest[nw]
    m = dict(zip(_W_NAMES, rest[nw + 1:2 * nw + 1]))
    v = dict(zip(_W_NAMES, rest[2 * nw + 1:3 * nw + 1]))
    ix, iy, ic = lax.axis_index("x"), lax.axis_index("y"), lax.axis_index("c")
    chip, dev = 2 * ix + iy, 4 * ix + 2 * iy + ic
    bsz, s, d = x.shape
    t = bsz * s
    n_dev = 8

    front_rows = 64
    front = _pack_rows([c, w["dn_conv_w"], w["ffn_conv_w"]], front_rows)
    front_all = _allgather8(front, "ag_front").reshape(n_dev, front_rows * LANES)
    n_c, n_dc, n_fc = bsz * d, DN_CONV * 3 * DN // N_CHIPS, FFN_CONV * 2 * D_FF // N_CHIPS
    c_all = front_all[:, :n_c].reshape(n_dev * bsz, d)
    per_chip = front_all[0::2]
    dn_conv_full = jnp.transpose(per_chip[:, n_c:n_c + n_dc].reshape(N_CHIPS, DN_CONV, -1), (1, 0, 2)).reshape(DN_CONV, 3 * DN)
    ffn_conv_full = jnp.transpose(per_chip[:, n_c + n_dc:n_c + n_dc + n_fc].reshape(N_CHIPS, FFN_CONV, -1), (1, 0, 2)).reshape(FFN_CONV, 2 * D_FF)

    mod_cols = N_MOD * d // N_CHIPS
    ada_b_loc = lax.dynamic_slice(w["ada_b"], (0, chip * mod_cols), (1, mod_cols))
    mod_part = _mod_fwd(c_all, w["ada_w"][0], ada_b_loc)
    mod_all = _allgather8(mod_part, "ag_mod").reshape(n_dev, n_dev * bsz, mod_cols)[0::2]
    mod = jnp.transpose(lax.dynamic_slice(mod_all, (0, dev * bsz, 0), (N_CHIPS, bsz, mod_cols)), (1, 0, 2)).reshape(bsz, N_MOD * d)

    (w_in_all,) = _comm_call(_gather_plug([jnp.swapaxes(w["w_in"][0], 0, 1).astype(BF16)]), "gather_w_in")
    p = {"w_in": _to_padded(w_in_all)}
    shards = {"late_mixer": [w[k][0].astype(BF16) for k in _LATE_MIXER], "ffn": [w[k][0].astype(BF16) for k in _FFN]}
    for k in ("norm_mix_pre", "norm_mix_post", "norm_ffn_pre", "norm_ffn_post", "dn_norm_w", "attn_sinks"):
        p[k] = w[k]
    p["dn_a_log"], p["dn_dt_bias"], p["rel_bias"] = w["dn_a_log"][0], w["dn_dt_bias"][0], w["rel_bias"]
    p["dn_conv_w"], p["ffn_conv_w"] = dn_conv_full, ffn_conv_full

    sq, dx, dmod, g = _device_step(x.reshape(t, d), loss_target.reshape(t, d), mod, p, bsz, shards)
    loss = lax.psum(0.5 * jnp.sum(sq), ("x", "y", "c"))

    g["dn_a_log"], g["dn_dt_bias"] = g["dn_a_log"].reshape(-1), g["dn_dt_bias"].reshape(-1)
    small_rows = 328
    small = _pack_rows([dmod] + [g[k] for k, _ in _SMALL], small_rows)
    small_all = _allgather8(small, "ag_small").reshape(n_dev, small_rows, LANES)
    n_dm = bsz * N_MOD * d
    dmod_all = small_all.reshape(n_dev, -1)[:, :n_dm].reshape(n_dev * bsz, N_MOD * d)
    tot = _sum_lead(small_all, "sum_small").reshape(-1)
    gs, off = {}, n_dm
    for k, n in _SMALL:
        gs[k] = tot[off:off + n]
        off += n
    grad = {}
    grad["ada_w"], grad["ada_b"] = _ada_grad(c_all, lax.dynamic_slice(dmod_all, (0, chip * mod_cols), (n_dev * bsz, mod_cols)), dmod_all)
    for k in ("norm_mix_pre", "norm_mix_post", "norm_ffn_pre", "norm_ffn_post", "dn_norm_w"):
        grad[k] = gs[k]
    grad["dn_a_log"] = gs["dn_a_log"][DN_HEADS:2 * DN_HEADS]
    grad["dn_dt_bias"] = gs["dn_dt_bias"][DN_HEADS:2 * DN_HEADS]
    grad["attn_sinks"] = gs["attn_sinks"].reshape(AQ_HEADS, LANES)[:, 0]
    grad["rel_bias"] = gs["rel_bias"].reshape(AQ_HEADS, LANES)[:, :REL_BUCKETS].T
    grad["dn_conv_w"] = lax.dynamic_slice(gs["dn_conv_w"].reshape(DN_CONV, 3 * DN), (0, chip * (3 * DN // N_CHIPS)), (DN_CONV, 3 * DN // N_CHIPS))
    grad["ffn_conv_w"] = lax.dynamic_slice(gs["ffn_conv_w"].reshape(FFN_CONV, 2 * D_FF), (0, chip * (2 * D_FF // N_CHIPS)), (FFN_CONV, 2 * D_FF // N_CHIPS))

    mine = [_sum_lead(g[k], "sum_" + k, rows_apart=(k == "w_in")) for k in _BIG]
    theirs = _sibling_exchange(mine, "exchange_cores")

    out = {}
    for k, a, b in zip(_BIG, mine, theirs):
        if k == "w_in":
            tr = lambda z: jnp.transpose(z, (2, 0, 1))
            res = _adamw_halves(tr(w[k]), tr(m[k]), tr(v[k]), a, b, ic.reshape(1).astype(jnp.int32), "adamw_" + k)
            out[k] = [jnp.transpose(r, (1, 2, 0)) for r in res]
        else:
            out[k] = _adamw(w[k][0], m[k][0], v[k][0], [a, b], "adamw_" + k)
    out["ada_w"] = _adamw(w["ada_w"][0], m["ada_w"][0], v["ada_w"][0], [grad["ada_w"]], "adamw_ada_w")
    small_names = [k for k in _W_NAMES if k not in _BIG and k != "ada_w"]
    res_small = _adamw_small([w[k] for k in small_names], [m[k] for k in small_names], [v[k] for k in small_names],
                             [grad[k].reshape(w[k].shape) for k in small_names], "adamw_small")
    out.update(zip(small_names, res_small))
    for k in _BIG + ("ada_w",):
        out[k] = [r.reshape(w[k].shape) for r in out[k]]
    grads, deltas, new_m, new_v = ([out[k][i] for k in _W_NAMES] for i in range(4))
    return (loss, dx.reshape(bsz, s, d), *grads, *deltas, *new_m, *new_v)
```

```python
import functools
import math

import numpy as np
import jax
import jax.numpy as jnp
from jax import lax
from jax.experimental import pallas as pl
from jax.experimental.pallas import tpu as pltpu

F32 = jnp.float32
BF16 = jnp.bfloat16
MESH = pl.DeviceIdType.MESH

D_MODEL = 1024
N_MOD = 6
AQ_HEADS, AKV_HEADS, A_HD, WINDOW = 8, 2, 64, 128
REL_BUCKETS, REL_MAX_DIST = 32, 128
DN_HEADS, DN_HD, DN_CONV, DN_CHUNK = 4, 128, 4, 64
D_FF, FFN_CONV = 2816, 3
RMS_EPS, L2_EPS, NEG_INF = 1e-6, 1e-6, -1e30
AQ, AKV, DN = AQ_HEADS * A_HD, AKV_HEADS * A_HD, DN_HEADS * DN_HD
IN_DIM = AQ + 2 * AKV + 3 * DN + DN + 2 * DN_HEADS + 2 * D_MODEL
C_DQKV, C_Q, C_DZ, C_GA, C_GD, C_K, C_V, C_BD = 0, 1536, 2048, 2560, 3584, 4608, 4736, 4864
IN_PAD = 4992
LANES = 128
N_CHIPS = 4

ADAM_LR, ADAM_B1, ADAM_B2, ADAM_EPS, ADAM_WD, ADAM_STEP = 0.001, 0.9, 0.999, 1e-08, 0.01, 10


def _pick(n, cap):
    best = None
    for t in range(LANES, cap + 1, LANES):
        if n % t == 0:
            best = t
    return best if best is not None else n


def _vspec(shape, index_map):
    return pl.BlockSpec(shape, index_map)


MM_VMEM_BUDGET = 40 * 2 ** 20
GRID_STEP_S = 0.35e-6
HBM_BYTES_PER_S = 3.0e12
MXU_FLOPS_PER_S = 9.0e14
MXU_DIM = 256


def _mm_tiles(m, n, k, mode, in_bytes, out_bytes, split=1):
    best = None
    for tm in [t for t in range(LANES, m + 1, LANES) if m % t == 0]:
        for tn in [t for t in range(LANES, n // split + 1, LANES) if (n // split) % t == 0]:
            a, b, o = k * tm * in_bytes, k * tn * in_bytes, tm * tn * out_bytes
            if 2 * (a + b + o) + (a if mode == "tn" else 0) > MM_VMEM_BUDGET:
                continue
            hbm_s = (m * k * in_bytes + (m // tm) * n * k * in_bytes + m * n * out_bytes) / HBM_BYTES_PER_S
            mxu_s = 2 * m * n * k / (MXU_FLOPS_PER_S * min(1.0, tm / MXU_DIM) * min(1.0, tn / MXU_DIM))
            cost = (m // tm) * (n // tn) * GRID_STEP_S + max(hbm_s, mxu_s)
            if best is None or cost < best[0]:
                best = (cost, tm, tn)
    return best[1], best[2]


def _mm(a, b, mode, out_dtype, name, plug=None, split=1, b_kblock=0, add=None):
    if mode == "nn":
        (m, k), n = a.shape, b.shape[1]
        dims = (((1,), (0,)), ((), ()))
    elif mode == "nt":
        (m, k), n = a.shape, b.shape[0]
        dims = (((1,), (1,)), ((), ()))
    else:
        (k, m), n = a.shape, b.shape[1]
        dims = (((0,), (0,)), ((), ()))
    tm, tn = _mm_tiles(m, n, k, mode, a.dtype.itemsize, jnp.dtype(out_dtype).itemsize, split)
    if mode == "tn":
        a_spec = _vspec((k, tm), lambda i, j: (0, i))
    else:
        a_spec = _vspec((tm, k), lambda i, j: (i, 0))
    if mode == "nt":
        b_spec = _vspec((tn, k), lambda i, j: (j, b_kblock))
    else:
        b_spec = _vspec((k, tn), lambda i, j: (0, j))
    in_specs, args = [a_spec, b_spec], (a, b)
    if add is not None:
        in_specs, args = in_specs + [_vspec((tm, tn), lambda i, j: (i, j))], (a, b, add)

    def body(a_ref, b_ref, *rest):
        o_ref = rest[-1]
        acc = lax.dot_general(a_ref[...].astype(BF16), b_ref[...].astype(BF16), dims, preferred_element_type=F32)
        if add is not None:
            acc = acc + rest[0][...]
        o_ref[...] = acc.astype(out_dtype).reshape(o_ref.shape)

    grid = (m // tm, n // tn)
    if split == 1:
        out_spec, out_shape = _vspec((tm, tn), lambda i, j: (i, j)), (m, n)
    else:
        per = n // split // tn
        out_spec, out_shape = _vspec((1, tm, tn), lambda i, j: (j // per, i, j % per)), (split, m, n // split)
    (out,), extra = _plugged_call(body, plug, _grid_ends(grid), args, name=name, grid=grid, in_specs=in_specs,
                                  out_specs=[out_spec], out_shape=[jax.ShapeDtypeStruct(out_shape, out_dtype)])
    return out if plug is None else (out, extra)


def _rms(x, w):
    return (x * lax.rsqrt(jnp.mean(x * x, axis=-1, keepdims=True) + RMS_EPS)) * w


def _pre_f(x, w, sc, sh):
    return _rms(x, w) * (1.0 + sc) + sh


def _post_f(y, w, g):
    return g * _rms(y, w)


def _tok_grid(t, bsz, ts):
    nt = t // bsz // ts
    return nt, (bsz, nt)


def _pre_fwd(x, w, sc, sh, name, ts=512):
    t, d = x.shape
    bsz = sc.shape[0]
    nt, grid = _tok_grid(t, bsz, ts)
    row = _vspec((ts, d), lambda b, i: (b * nt + i, 0))
    vec = _vspec((1, d), lambda b, i: (0, 0))
    bvec = _vspec((1, 1, d), lambda b, i: (b, 0, 0))

    def body(x_ref, w_ref, sc_ref, sh_ref, u_ref):
        u_ref[...] = _pre_f(x_ref[...], w_ref[...], sc_ref[0], sh_ref[0]).astype(BF16)

    return pl.pallas_call(body, name=name, grid=grid, in_specs=[row, vec, bvec, bvec], out_specs=row,
                          out_shape=jax.ShapeDtypeStruct((t, d), BF16))(x, w, sc, sh)


def _pre_bwd(x, w, sc, sh, du, dres, name, ts=512):
    t, d = x.shape
    bsz = sc.shape[0]
    nt, grid = _tok_grid(t, bsz, ts)
    row = _vspec((ts, d), lambda b, i: (b * nt + i, 0))
    vec = _vspec((1, d), lambda b, i: (0, 0))
    bvec = _vspec((1, 1, d), lambda b, i: (b, 0, 0))

    def body(x_ref, w_ref, sc_ref, sh_ref, du_ref, dres_ref, dx_ref, dw_ref, dsc_ref, dsh_ref):
        b, i = pl.program_id(0), pl.program_id(1)
        _, vjp = jax.vjp(_pre_f, x_ref[...], w_ref[...], sc_ref[0], sh_ref[0])
        dx, dw, dsc, dsh = vjp(du_ref[...])
        dx_ref[...] = dres_ref[...] + dx

        @pl.when((b == 0) & (i == 0))
        def _():
            dw_ref[...] = jnp.zeros_like(dw_ref)

        @pl.when(i == 0)
        def _():
            dsc_ref[...] = jnp.zeros_like(dsc_ref)
            dsh_ref[...] = jnp.zeros_like(dsh_ref)

        dw_ref[...] += dw
        dsc_ref[0] += dsc
        dsh_ref[0] += dsh

    return pl.pallas_call(
        body, name=name, grid=grid, in_specs=[row, vec, bvec, bvec, row, row], out_specs=[row, vec, bvec, bvec],
        out_shape=[jax.ShapeDtypeStruct((t, d), F32), jax.ShapeDtypeStruct((1, d), F32),
                   jax.ShapeDtypeStruct((bsz, 1, d), F32), jax.ShapeDtypeStruct((bsz, 1, d), F32)],
    )(x, w, sc, sh, du, dres)


def _accumulate(ref, val, first):
    @pl.when(first)
    def _():
        ref[...] = jnp.zeros_like(ref)

    ref[...] += val.reshape(ref.shape)


def _post_pre_fwd(res, y, w_post, g, w_pre, sc, sh, name, ts=512):
    t, d = y.shape
    bsz = g.shape[0]
    nt, grid = _tok_grid(t, bsz, ts)
    row = _vspec((ts, d), lambda b, i: (b * nt + i, 0))
    vec = _vspec((1, d), lambda b, i: (0, 0))
    bvec = _vspec((1, 1, d), lambda b, i: (b, 0, 0))

    def body(res_ref, y_ref, wp_ref, g_ref, w_ref, sc_ref, sh_ref, h_ref, u_ref):
        h = res_ref[...] + _post_f(y_ref[...], wp_ref[...], g_ref[0])
        h_ref[...] = h
        u_ref[...] = _pre_f(h, w_ref[...], sc_ref[0], sh_ref[0]).astype(BF16)

    return pl.pallas_call(body, name=name, grid=grid, in_specs=[row, row, vec, bvec, vec, bvec, bvec], out_specs=[row, row],
                          out_shape=[jax.ShapeDtypeStruct((t, d), F32), jax.ShapeDtypeStruct((t, d), BF16)],
                          )(res, y, w_post, g, w_pre, sc, sh)


def _post_loss_bwd(res, y, w, g, tgt, name, ts=512):
    t, d = y.shape
    bsz = g.shape[0]
    nt, grid = _tok_grid(t, bsz, ts)
    row = _vspec((ts, d), lambda b, i: (b * nt + i, 0))
    vec = _vspec((1, d), lambda b, i: (0, 0))
    bvec = _vspec((1, 1, d), lambda b, i: (b, 0, 0))

    def body(res_ref, y_ref, w_ref, g_ref, tgt_ref, dh_ref, dy_ref, dw_ref, dg_ref, sq_ref):
        b, i = pl.program_id(0), pl.program_id(1)
        part, vjp = jax.vjp(_post_f, y_ref[...], w_ref[...], g_ref[0])
        e = res_ref[...] + part - tgt_ref[...]
        dh = e * (1.0 / d)
        dh_ref[...] = dh
        dy, dw, dg = vjp(dh)
        dy_ref[...] = dy.astype(BF16)
        _accumulate(dw_ref, dw, (b == 0) & (i == 0))
        _accumulate(dg_ref, dg, i == 0)
        _accumulate(sq_ref, jnp.sum(e * e, axis=0, keepdims=True) * (1.0 / d), (b == 0) & (i == 0))

    return pl.pallas_call(
        body, name=name, grid=grid, in_specs=[row, row, vec, bvec, row], out_specs=[row, row, vec, bvec, vec],
        out_shape=[jax.ShapeDtypeStruct((t, d), F32), jax.ShapeDtypeStruct((t, d), BF16), jax.ShapeDtypeStruct((1, d), F32),
                   jax.ShapeDtypeStruct((bsz, 1, d), F32), jax.ShapeDtypeStruct((1, d), F32)],
    )(res, y, w, g, tgt)


def _pre_post_bwd(x, w, sc, sh, du, dres, y, w_post, g, name, ts=512):
    t, d = x.shape
    bsz = sc.shape[0]
    nt, grid = _tok_grid(t, bsz, ts)
    row = _vspec((ts, d), lambda b, i: (b * nt + i, 0))
    vec = _vspec((1, d), lambda b, i: (0, 0))
    bvec = _vspec((1, 1, d), lambda b, i: (b, 0, 0))

    def body(x_ref, w_ref, sc_ref, sh_ref, du_ref, dres_ref, y_ref, wp_ref, g_ref,
             dx_ref, dy_ref, dw_ref, dsc_ref, dsh_ref, dwp_ref, dg_ref):
        b, i = pl.program_id(0), pl.program_id(1)
        _, vjp = jax.vjp(_pre_f, x_ref[...], w_ref[...], sc_ref[0], sh_ref[0])
        dx, dw, dsc, dsh = vjp(du_ref[...])
        dx = dres_ref[...] + dx
        dx_ref[...] = dx
        _, vjp_post = jax.vjp(_post_f, y_ref[...], wp_ref[...], g_ref[0])
        dy, dwp, dg = vjp_post(dx)
        dy_ref[...] = dy.astype(BF16)
        first = (b == 0) & (i == 0)
        _accumulate(dw_ref, dw, first)
        _accumulate(dwp_ref, dwp, first)
        _accumulate(dsc_ref, dsc, i == 0)
        _accumulate(dsh_ref, dsh, i == 0)
        _accumulate(dg_ref, dg, i == 0)

    v1, vb = jax.ShapeDtypeStruct((1, d), F32), jax.ShapeDtypeStruct((bsz, 1, d), F32)
    return pl.pallas_call(
        body, name=name, grid=grid, in_specs=[row, vec, bvec, bvec, row, row, row, vec, bvec],
        out_specs=[row, row, vec, bvec, bvec, vec, bvec],
        out_shape=[jax.ShapeDtypeStruct((t, d), F32), jax.ShapeDtypeStruct((t, d), BF16), v1, vb, vb, v1, vb],
    )(x, w, sc, sh, du, dres, y, w_post, g)


def _merge_f(ga, gd, ya, yd):
    return jax.nn.sigmoid(ga) * ya + jax.nn.sigmoid(gd) * yd


_MW = 512


def _merge_fwd(proj, ya, yd, ts=512):
    t, d = ya.shape
    blk = _vspec((ts, _MW), lambda i, j: (i, j))
    ga = _vspec((ts, _MW), lambda i, j: (i, C_GA // _MW + j))
    gd = _vspec((ts, _MW), lambda i, j: (i, C_GD // _MW + j))

    def body(ga_ref, gd_ref, ya_ref, yd_ref, o_ref):
        o_ref[...] = _merge_f(ga_ref[...], gd_ref[...], ya_ref[...], yd_ref[...]).astype(BF16)

    return pl.pallas_call(body, name="merge_fwd", grid=(t // ts, d // _MW), in_specs=[ga, gd, blk, blk], out_specs=blk,
                          out_shape=jax.ShapeDtypeStruct((t, d), BF16))(proj, proj, ya, yd)


def _merge_bwd(proj, ya, yd, dm, ts=512):
    t, d = ya.shape
    blk = _vspec((ts, _MW), lambda i, j: (i, j))
    ga = _vspec((ts, _MW), lambda i, j: (i, C_GA // _MW + j))
    gd = _vspec((ts, _MW), lambda i, j: (i, C_GD // _MW + j))

    def body(ga_ref, gd_ref, ya_ref, yd_ref, dm_ref, dga_ref, dgd_ref, dya_ref, dyd_ref):
        _, vjp = jax.vjp(_merge_f, ga_ref[...], gd_ref[...], ya_ref[...], yd_ref[...])
        dga, dgd, dya, dyd = vjp(dm_ref[...])
        dga_ref[...] = dga.astype(BF16)
        dgd_ref[...] = dgd.astype(BF16)
        dya_ref[...] = dya.astype(BF16)
        dyd_ref[...] = dyd.astype(BF16)

    o = jax.ShapeDtypeStruct((t, d), BF16)
    return pl.pallas_call(body, name="merge_bwd", grid=(t // ts, d // _MW), in_specs=[ga, gd, blk, blk, blk],
                          out_specs=[blk] * 4, out_shape=[o] * 4)(proj, proj, ya, yd, dm)


def _shift_down(x, s):
    if s == 0:
        return x
    r = lax.broadcasted_iota(jnp.int32, x.shape, 0)
    return jnp.where(r >= s, pltpu.roll(x, s, 0), 0.0)


def _shift_up(x, s):
    if s == 0:
        return x
    n = x.shape[0]
    r = lax.broadcasted_iota(jnp.int32, x.shape, 0)
    return jnp.where(r < n - s, pltpu.roll(x, n - s, 0), 0.0)


def _conv_fwd(x, w, k):
    out = None
    for j in range(k):
        term = w[j:j + 1, :] * _shift_down(x, k - 1 - j)
        out = term if out is None else out + term
    return out


def _conv_bwd(x, w, dc, k):
    dx = None
    dws = []
    for j in range(k):
        up = _shift_up(dc, k - 1 - j)
        term = w[j:j + 1, :] * up
        dx = term if dx is None else dx + term
        dws.append(jnp.sum(up * x, axis=0, keepdims=True))
    return dx, jnp.concatenate(dws, axis=0)


def _geglu_f(gate, val):
    return jax.nn.gelu(gate, approximate=True) * val


_FW = 256


def _ffn_act_fwd(up, conv_w, bsz):
    t = up.shape[0]
    s = t // bsz
    nj = D_FF // _FW
    xg = _vspec((s, _FW), lambda b, j: (b, j))
    xv = _vspec((s, _FW), lambda b, j: (b, nj + j))
    wg = _vspec((FFN_CONV, _FW), lambda b, j: (0, j))
    wv = _vspec((FFN_CONV, _FW), lambda b, j: (0, nj + j))

    def body(xg_ref, xv_ref, wg_ref, wv_ref, o_ref):
        gate = _conv_fwd(xg_ref[...], wg_ref[...], FFN_CONV)
        val = _conv_fwd(xv_ref[...], wv_ref[...], FFN_CONV)
        o_ref[...] = _geglu_f(gate, val).astype(BF16)

    return pl.pallas_call(body, name="ffn_act_fwd", grid=(bsz, nj), in_specs=[xg, xv, wg, wv],
                          out_specs=_vspec((s, _FW), lambda b, j: (b, j)),
                          out_shape=jax.ShapeDtypeStruct((t, D_FF), BF16))(up, up, conv_w, conv_w)


def _ffn_act_bwd(up, conv_w, dact, bsz, plug=None):
    t = up.shape[0]
    s = t // bsz
    nj = D_FF // _FW
    xg = _vspec((s, _FW), lambda j, b: (b, j))
    xv = _vspec((s, _FW), lambda j, b: (b, nj + j))
    wg = _vspec((FFN_CONV, _FW), lambda j, b: (0, j))
    wv = _vspec((FFN_CONV, _FW), lambda j, b: (0, nj + j))
    da = _vspec((s, _FW), lambda j, b: (b, j))
    dwo = _vspec((FFN_CONV, _FW), lambda j, b: (0, j))

    def body(xg_ref, xv_ref, wg_ref, wv_ref, da_ref, dxg_ref, dxv_ref, dwg_ref, dwv_ref):
        b = pl.program_id(1)
        xg_, xv_, wg_, wv_ = xg_ref[...], xv_ref[...], wg_ref[...], wv_ref[...]
        gate = _conv_fwd(xg_, wg_, FFN_CONV)
        val = _conv_fwd(xv_, wv_, FFN_CONV)
        _, vjp = jax.vjp(_geglu_f, gate, val)
        dgate, dval = vjp(da_ref[...])
        dxg, dwg = _conv_bwd(xg_, wg_, dgate, FFN_CONV)
        dxv, dwv = _conv_bwd(xv_, wv_, dval, FFN_CONV)
        dxg_ref[...] = dxg.astype(BF16)
        dxv_ref[...] = dxv.astype(BF16)

        @pl.when(b == 0)
        def _():
            dwg_ref[...] = jnp.zeros_like(dwg_ref)
            dwv_ref[...] = jnp.zeros_like(dwv_ref)

        dwg_ref[...] += dwg
        dwv_ref[...] += dwv

    outs, extra = _plugged_call(
        body, plug, _grid_ends((nj, bsz)), (up, up, conv_w, conv_w, dact), name="ffn_act_bwd", grid=(nj, bsz),
        in_specs=[xg, xv, wg, wv, da], out_specs=[da, da, dwo, dwo],
        out_shape=[jax.ShapeDtypeStruct((t, D_FF), BF16), jax.ShapeDtypeStruct((t, D_FF), BF16),
                   jax.ShapeDtypeStruct((FFN_CONV, D_FF), F32), jax.ShapeDtypeStruct((FFN_CONV, D_FF), F32)])
    return (*outs, extra)


def _bucket_table():
    qi = np.arange(WINDOW)[:, None]
    kj = np.arange(2 * WINDOW)[None, :]
    dist = WINDOW + qi - kj
    dc = np.maximum(dist, 0)
    max_exact = REL_BUCKETS // 2
    scaled = np.log(np.maximum(dc, 1).astype(np.float32) / np.float32(max_exact)) / np.float32(math.log(REL_MAX_DIST / max_exact))
    large = max_exact + (scaled.astype(np.float32) * np.float32(REL_BUCKETS - max_exact)).astype(np.int32)
    large = np.minimum(large, REL_BUCKETS - 1)
    bucket = np.where(dc < max_exact, dc, large).astype(np.int32)
    in_band = ((dist >= 0) & (dist < WINDOW)).astype(np.int32)
    return bucket, in_band


def _bias_build(rel_bias):
    bucket, _ = _bucket_table()

    def body(rb_ref, idx_ref, o_ref):
        h = pl.program_id(0)
        idx = idx_ref[...]
        acc = jnp.zeros(idx.shape, F32)
        for r in range(REL_BUCKETS):
            acc = jnp.where(idx == r, rb_ref[r, h], acc)
        o_ref[0] = acc

    return pl.pallas_call(
        body, name="bias_build", grid=(AQ_HEADS,),
        in_specs=[pl.BlockSpec(memory_space=pltpu.SMEM), _vspec((WINDOW, 2 * WINDOW), lambda h: (0, 0))],
        out_specs=_vspec((1, WINDOW, 2 * WINDOW), lambda h: (h, 0, 0)),
        out_shape=jax.ShapeDtypeStruct((AQ_HEADS, WINDOW, 2 * WINDOW), F32),
    )(rel_bias, jnp.asarray(bucket))


def _bias_grad(dbias):
    bucket, _ = _bucket_table()

    def body(db_ref, idx_ref, o_ref):
        idx = idx_ref[...]
        db = db_ref[0]
        lane = lax.broadcasted_iota(jnp.int32, (1, LANES), 1)
        acc = jnp.zeros((1, LANES), F32)
        for r in range(REL_BUCKETS):
            s = jnp.sum(jnp.sum(jnp.where(idx == r, db, 0.0), axis=1, keepdims=True), axis=0, keepdims=True)
            acc = jnp.where(lane == r, s, acc)
        o_ref[0] = acc

    return pl.pallas_call(
        body, name="bias_grad", grid=(AQ_HEADS,),
        in_specs=[_vspec((1, WINDOW, 2 * WINDOW), lambda h: (h, 0, 0)), _vspec((WINDOW, 2 * WINDOW), lambda h: (0, 0))],
        out_specs=_vspec((1, 1, LANES), lambda h: (h, 0, 0)),
        out_shape=jax.ShapeDtypeStruct((AQ_HEADS, 1, LANES), F32),
    )(dbias, jnp.asarray(bucket))


def _attn_mask(n):
    qi = lax.broadcasted_iota(jnp.int32, (WINDOW, 2 * WINDOW), 0)
    kj = lax.broadcasted_iota(jnp.int32, (WINDOW, 2 * WINDOW), 1)
    dist = WINDOW + qi - kj
    band = (dist >= 0) & (dist < WINDOW)
    return band & ((kj >= WINDOW) | (n > 0))


def _attn_probs(qk, bias, sink, mask):
    s = jnp.where(mask, qk * (A_HD ** -0.5) + bias, NEG_INF)
    m = jnp.maximum(jnp.max(s, axis=-1, keepdims=True), sink)
    p = jnp.exp(s - m)
    es = jnp.exp(sink - m)
    inv = 1.0 / (jnp.sum(p, axis=-1, keepdims=True) + es)
    return p * inv, es * inv


def _attn_fwd(proj, bias, sinks, bsz):
    t = proj.shape[0]
    s = t // bsz
    nb = s // WINDOW
    grp = AQ_HEADS // AKV_HEADS

    def body(q_ref, k_ref, v_ref, bias_ref, sink_ref, y_ref, kp_ref, vp_ref):
        kp_ref[0:WINDOW, :] = jnp.zeros((WINDOW, LANES), BF16)
        vp_ref[0:WINDOW, :] = jnp.zeros((WINDOW, LANES), BF16)
        kp_ref[WINDOW:, :] = k_ref[...].astype(BF16)
        vp_ref[WINDOW:, :] = v_ref[...].astype(BF16)

        def blk(n, carry):
            r0 = pl.multiple_of(n * WINDOW, WINDOW)
            mask = _attn_mask(n)
            kband = kp_ref[pl.ds(r0, 2 * WINDOW), :]
            vband = vp_ref[pl.ds(r0, 2 * WINDOW), :]
            qb = q_ref[pl.ds(r0, WINDOW), :].astype(BF16)
            heads = range(AQ_HEADS)
            hsl = lambda h: slice(h * A_HD, (h + 1) * A_HD)
            kbs = [kband[:, hsl(kv)] for kv in range(AKV_HEADS)]
            vbs = [vband[:, hsl(kv)] for kv in range(AKV_HEADS)]
            qks = [lax.dot_general(qb[:, hsl(h)], kbs[h // grp], _NT, preferred_element_type=F32) for h in heads]
            probs = [_attn_probs(qks[h], bias_ref[h], sink_ref[0, h], mask)[0] for h in heads]
            outs = [jnp.dot(probs[h].astype(BF16), vbs[h // grp], preferred_element_type=F32) for h in heads]
            y_ref[pl.ds(r0, WINDOW), :] = jnp.concatenate(outs, axis=1).astype(BF16)
            return carry

        lax.fori_loop(0, nb, blk, 0)

    return pl.pallas_call(
        body, name="attn_fwd", grid=(bsz,),
        in_specs=[_vspec((s, AQ), lambda b: (b, C_Q // AQ)), _vspec((s, AKV), lambda b: (b, C_K // AKV)),
                  _vspec((s, AKV), lambda b: (b, C_V // AKV)),
                  _vspec((AQ_HEADS, WINDOW, 2 * WINDOW), lambda b: (0, 0, 0)), pl.BlockSpec(memory_space=pltpu.SMEM)],
        out_specs=_vspec((s, AQ), lambda b: (b, 0)), out_shape=jax.ShapeDtypeStruct((t, AQ), BF16),
        scratch_shapes=[pltpu.VMEM((s + WINDOW, LANES), BF16), pltpu.VMEM((s + WINDOW, LANES), BF16)],
    )(proj, proj, proj, bias, sinks)


def _attn_bwd(proj, bias, sinks, dy, bsz, plug=None):
    t = proj.shape[0]
    s = t // bsz
    nb = s // WINDOW
    grp = AQ_HEADS // AKV_HEADS
    scale = A_HD ** -0.5

    def body(q_ref, k_ref, v_ref, bias_ref, sink_ref, dy_ref, dq_ref, dk_ref, dv_ref, dbias_ref, dsink_ref,
             kp_ref, vp_ref, dkp_ref, dvp_ref):
        b = pl.program_id(0)
        kp_ref[0:WINDOW, :] = jnp.zeros((WINDOW, LANES), BF16)
        vp_ref[0:WINDOW, :] = jnp.zeros((WINDOW, LANES), BF16)
        kp_ref[WINDOW:, :] = k_ref[...].astype(BF16)
        vp_ref[WINDOW:, :] = v_ref[...].astype(BF16)
        dkp_ref[...] = jnp.zeros_like(dkp_ref)
        dvp_ref[...] = jnp.zeros_like(dvp_ref)

        @pl.when(b == 0)
        def _():
            dbias_ref[...] = jnp.zeros_like(dbias_ref)
            dsink_ref[...] = jnp.zeros_like(dsink_ref)

        def blk(n, carry):
            r0 = pl.multiple_of(n * WINDOW, WINDOW)
            mask = _attn_mask(n)
            kband = kp_ref[pl.ds(r0, 2 * WINDOW), :]
            vband = vp_ref[pl.ds(r0, 2 * WINDOW), :]
            qb = q_ref[pl.ds(r0, WINDOW), :].astype(BF16)
            dyb = dy_ref[pl.ds(r0, WINDOW), :].astype(BF16)
            heads = range(AQ_HEADS)
            hsl = lambda h: slice(h * A_HD, (h + 1) * A_HD)
            kbs = [kband[:, hsl(kv)] for kv in range(AKV_HEADS)]
            vbs = [vband[:, hsl(kv)] for kv in range(AKV_HEADS)]
            qhs = [qb[:, hsl(h)] for h in heads]
            dyhs = [dyb[:, hsl(h)] for h in heads]
            qks = [lax.dot_general(qhs[h], kbs[h // grp], _NT, preferred_element_type=F32) for h in heads]
            dprobs = [lax.dot_general(dyhs[h], vbs[h // grp], _NT, preferred_element_type=F32) for h in heads]
            pbs, dsbs = [], []
            for h in heads:
                probs, psink = _attn_probs(qks[h], bias_ref[h], sink_ref[0, h], mask)
                rowdot = jnp.sum(probs * dprobs[h], axis=-1, keepdims=True)
                ds = probs * (dprobs[h] - rowdot)
                dbias_ref[h] += ds
                dsink_ref[h] += jnp.sum(-psink * rowdot, axis=0, keepdims=True) + jnp.zeros((1, LANES), F32)
                pbs.append(probs.astype(BF16))
                dsbs.append(ds.astype(BF16))
            dvhs = [lax.dot_general(pbs[h], dyhs[h], _TN, preferred_element_type=F32) for h in heads]
            dqs = [jnp.dot(dsbs[h], kbs[h // grp], preferred_element_type=F32) * scale for h in heads]
            dkhs = [lax.dot_general(dsbs[h], qhs[h], _TN, preferred_element_type=F32) * scale for h in heads]
            dks = [sum(dkhs[kv * grp + 1:(kv + 1) * grp], dkhs[kv * grp]) for kv in range(AKV_HEADS)]
            dvs = [sum(dvhs[kv * grp + 1:(kv + 1) * grp], dvhs[kv * grp]) for kv in range(AKV_HEADS)]
            dq_ref[pl.ds(r0, WINDOW), :] = jnp.concatenate(dqs, axis=1).astype(BF16)
            dkp_ref[pl.ds(r0, 2 * WINDOW), :] += jnp.concatenate(dks, axis=1)
            dvp_ref[pl.ds(r0, 2 * WINDOW), :] += jnp.concatenate(dvs, axis=1)
            return carry

        lax.fori_loop(0, nb, blk, 0)
        dk_ref[...] = dkp_ref[WINDOW:, :].astype(BF16)
        dv_ref[...] = dvp_ref[WINDOW:, :].astype(BF16)

    kvs = jax.ShapeDtypeStruct((t, AKV), BF16)
    outs, extra = _plugged_call(
        body, plug, _grid_ends((bsz,)), (proj, proj, proj, bias, sinks, dy), name="attn_bwd", grid=(bsz,),
        in_specs=[_vspec((s, AQ), lambda b: (b, C_Q // AQ)), _vspec((s, AKV), lambda b: (b, C_K // AKV)),
                  _vspec((s, AKV), lambda b: (b, C_V // AKV)),
                  _vspec((AQ_HEADS, WINDOW, 2 * WINDOW), lambda b: (0, 0, 0)), pl.BlockSpec(memory_space=pltpu.SMEM),
                  _vspec((s, AQ), lambda b: (b, 0))],
        out_specs=[_vspec((s, AQ), lambda b: (b, 0)), _vspec((s, AKV), lambda b: (b, 0)), _vspec((s, AKV), lambda b: (b, 0)),
                   _vspec((AQ_HEADS, WINDOW, 2 * WINDOW), lambda b: (0, 0, 0)), _vspec((AQ_HEADS, 1, LANES), lambda b: (0, 0, 0))],
        out_shape=[jax.ShapeDtypeStruct((t, AQ), BF16), kvs, kvs,
                   jax.ShapeDtypeStruct((AQ_HEADS, WINDOW, 2 * WINDOW), F32), jax.ShapeDtypeStruct((AQ_HEADS, 1, LANES), F32)],
        scratch_shapes=[pltpu.VMEM((s + WINDOW, LANES), BF16), pltpu.VMEM((s + WINDOW, LANES), BF16),
                        pltpu.VMEM((s + WINDOW, LANES), F32), pltpu.VMEM((s + WINDOW, LANES), F32)])
    return (*outs, extra)


def _dn_act_f(c, is_qk):
    a = jax.nn.silu(c)
    outs = []
    for h in range(DN_HEADS):
        ah = a[:, h * DN_HD:(h + 1) * DN_HD]
        nh = ah * lax.rsqrt(jnp.sum(ah * ah, axis=-1, keepdims=True) + L2_EPS)
        outs.append(jnp.where(is_qk, nh, ah))
    return jnp.concatenate(outs, axis=1)


def _dn_prep_fwd(proj, conv_w, bsz):
    t = proj.shape[0]
    s = t // bsz
    blk = _vspec((s, DN), lambda b, j: (b, j))
    wsp = _vspec((DN_CONV, DN), lambda b, j: (0, j))

    def body(x_ref, w_ref, o_ref):
        j = pl.program_id(1)
        o_ref[...] = _dn_act_f(_conv_fwd(x_ref[...], w_ref[...], DN_CONV), j < 2)

    return pl.pallas_call(body, name="dn_prep_fwd", grid=(bsz, 3), in_specs=[blk, wsp], out_specs=blk,
                          out_shape=jax.ShapeDtypeStruct((t, 3 * DN), F32))(proj, conv_w)


def _dn_prep_bwd(proj, conv_w, dqkvn, bsz):
    t = proj.shape[0]
    s = t // bsz
    blk = _vspec((s, DN), lambda j, b: (b, j))
    wsp = _vspec((DN_CONV, DN), lambda j, b: (0, j))

    def body(x_ref, w_ref, d_ref, dx_ref, dw_ref):
        j, b = pl.program_id(0), pl.program_id(1)
        x, w = x_ref[...], w_ref[...]
        c = _conv_fwd(x, w, DN_CONV)
        _, vjp = jax.vjp(lambda cc: _dn_act_f(cc, j < 2), c)
        (dc,) = vjp(d_ref[0])
        dx, dw = _conv_bwd(x, w, dc, DN_CONV)
        dx_ref[...] = dx.astype(BF16)

        @pl.when(b == 0)
        def _():
            dw_ref[...] = jnp.zeros_like(dw_ref)

        dw_ref[...] += dw

    return pl.pallas_call(
        body, name="dn_prep_bwd", grid=(3, bsz),
        in_specs=[blk, wsp, _vspec((1, s, DN), lambda j, b: (j, b, 0))], out_specs=[blk, wsp],
        out_shape=[jax.ShapeDtypeStruct((t, 3 * DN), BF16), jax.ShapeDtypeStruct((DN_CONV, 3 * DN), F32)],
    )(proj, conv_w, dqkvn)


def _bg_f(x, alog, dt):
    lane = lax.broadcasted_iota(jnp.int32, x.shape, 1)
    beta = jax.nn.sigmoid(x)
    g = -jnp.exp(alog) * jax.nn.softplus(x + dt)
    return jnp.where(lane < DN_HEADS, beta, jnp.where(lane < 2 * DN_HEADS, g, 0.0))


def _bg_fwd(proj, alog, dt, bsz):
    t = proj.shape[0]
    s = t // bsz
    vec = _vspec((1, LANES), lambda b: (0, 0))

    def body(x_ref, a_ref, d_ref, o_ref):
        o_ref[...] = _bg_f(x_ref[...], a_ref[...], d_ref[...])

    return pl.pallas_call(body, name="bg_fwd", grid=(bsz,), in_specs=[_vspec((s, LANES), lambda b: (b, C_BD // LANES)), vec, vec],
                          out_specs=_vspec((s, LANES), lambda b: (b, 0)), out_shape=jax.ShapeDtypeStruct((t, LANES), F32))(proj, alog, dt)


def _bg_bwd(proj, alog, dt, dbg4, bsz):
    t = proj.shape[0]
    s = t // bsz
    vec = _vspec((1, LANES), lambda b: (0, 0))

    def body(x_ref, a_ref, d_ref, g4_ref, dx_ref, da_ref, dd_ref):
        b = pl.program_id(0)
        lane = lax.broadcasted_iota(jnp.int32, (s, LANES), 1)
        dbg = jnp.zeros((s, LANES), F32)
        for h in range(DN_HEADS):
            gh = g4_ref[:, h * DN_HD:(h + 1) * DN_HD]
            dbg = jnp.where(lane == h, gh[:, 0:1], dbg)
            dbg = jnp.where(lane == DN_HEADS + h, gh[:, 1:2], dbg)
        _, vjp = jax.vjp(_bg_f, x_ref[...], a_ref[...], d_ref[...])
        dx, da, dd = vjp(dbg)
        dx_ref[...] = dx.astype(BF16)

        @pl.when(b == 0)
        def _():
            da_ref[...] = jnp.zeros_like(da_ref)
            dd_ref[...] = jnp.zeros_like(dd_ref)

        da_ref[...] += da
        dd_ref[...] += dd

    return pl.pallas_call(
        body, name="bg_bwd", grid=(bsz,),
        in_specs=[_vspec((s, LANES), lambda b: (b, C_BD // LANES)), vec, vec, _vspec((s, DN), lambda b: (b, 0))],
        out_specs=[_vspec((s, LANES), lambda b: (b, 0)), vec, vec],
        out_shape=[jax.ShapeDtypeStruct((t, LANES), BF16), jax.ShapeDtypeStruct((1, LANES), F32), jax.ShapeDtypeStruct((1, LANES), F32)],
    )(proj, alog, dt, dbg4)


def _dn_out_f(o, z, w):
    outs = []
    for h in range(DN_HEADS):
        sl = slice(h * DN_HD, (h + 1) * DN_HD)
        outs.append(_rms(o[:, sl], w) * jax.nn.silu(z[:, sl]))
    return jnp.concatenate(outs, axis=1)


def _dn_out_fwd(o, proj, w, ts=512):
    t = o.shape[0]
    blk = _vspec((ts, DN), lambda i: (i, 0))
    zsp = _vspec((ts, DN), lambda i: (i, C_DZ // DN))
    vec = _vspec((1, DN_HD), lambda i: (0, 0))

    def body(o_ref, z_ref, w_ref, y_ref):
        y_ref[...] = _dn_out_f(o_ref[...], z_ref[...], w_ref[...]).astype(BF16)

    return pl.pallas_call(body, name="dn_out_fwd", grid=(t // ts,), in_specs=[blk, zsp, vec], out_specs=blk,
                          out_shape=jax.ShapeDtypeStruct((t, DN), BF16))(o, proj, w)


def _dn_out_bwd(o, proj, w, dy, ts=512):
    t = o.shape[0]
    blk = _vspec((ts, DN), lambda i: (i, 0))
    zsp = _vspec((ts, DN), lambda i: (i, C_DZ // DN))
    vec = _vspec((1, DN_HD), lambda i: (0, 0))

    def body(o_ref, z_ref, w_ref, dy_ref, do_ref, dz_ref, dw_ref):
        i = pl.program_id(0)
        _, vjp = jax.vjp(_dn_out_f, o_ref[...], z_ref[...], w_ref[...])
        do, dz, dw = vjp(dy_ref[...])
        do_ref[...] = do
        dz_ref[...] = dz.astype(BF16)

        @pl.when(i == 0)
        def _():
            dw_ref[...] = jnp.zeros_like(dw_ref)

        dw_ref[...] += dw

    return pl.pallas_call(
        body, name="dn_out_bwd", grid=(t // ts,), in_specs=[blk, zsp, vec, blk], out_specs=[blk, blk, vec],
        out_shape=[jax.ShapeDtypeStruct((t, DN), F32), jax.ShapeDtypeStruct((t, DN), BF16), jax.ShapeDtypeStruct((1, DN_HD), F32)],
    )(o, proj, w, dy)


_C = DN_CHUNK


def _dot(a, b, dims):
    return lax.dot_general(a.astype(BF16), b.astype(BF16), dims, preferred_element_type=F32)


def _split(a):
    hi = a.astype(BF16)
    return hi, (a - hi.astype(F32)).astype(BF16)


def _dot3(a, b, dims):
    (ah, al), (bh, bl) = (a if isinstance(a, tuple) else _split(a)), (b if isinstance(b, tuple) else _split(b))
    mm = lambda x, y: lax.dot_general(x, y, dims, preferred_element_type=F32)
    return mm(ah, bh) + (mm(ah, bl) + mm(al, bh))


_NN = (((1,), (0,)), ((), ()))
_NT = (((1,), (1,)), ((), ()))
_TN = (((0,), (0,)), ((), ()))


_SUB = 8


def _tri_inverses(ls, lts):
    ri8 = lax.broadcasted_iota(jnp.int32, (_SUB, _C), 0)
    ci8 = lax.broadcasted_iota(jnp.int32, (_SUB, _C), 1)
    nblk = _C // _SUB
    ts = []
    for lt in lts:
        blocks = [jnp.where(ci8 == ri8 + _SUB * b, 1.0, 0.0).astype(F32) for b in range(nblk)]
        for r in range(1, _SUB):
            for b in range(nblk):
                coef = lt[_SUB * b:_SUB * (b + 1), _SUB * b + r:_SUB * b + r + 1]
                row = jnp.sum(coef * blocks[b], axis=0, keepdims=True)
                blocks[b] = jnp.where(ri8 == r, blocks[b] - row, blocks[b])
        ts.append(jnp.concatenate(blocks, axis=0))
    ri = lax.broadcasted_iota(jnp.int32, (_C, _C), 0)
    ci = lax.broadcasted_iota(jnp.int32, (_C, _C), 1)
    s = _SUB
    while s < _C:
        shift = s.bit_length()
        quad = ((ri >> shift) == (ci >> shift)) & ((ri & s) != 0) & ((ci & s) == 0)
        offs = [jnp.where(quad, l, 0.0) for l in ls]
        tsp = [_split(t) for t in ts]
        left = [_dot3(tp, off, _NN) for tp, off in zip(tsp, offs)]
        ts = [t - _dot3(lo, tp, _NN) for t, lo, tp in zip(ts, left, tsp)]
        s *= 2
    return ts


_SEG = 512
_HEADS = tuple(range(DN_HEADS))


def _hsl(hh):
    return slice(hh * DN_HD, (hh + 1) * DN_HD)


def _chunk_specs(bsz, nseg, reverse):
    seg = (lambda i: nseg - 1 - i) if reverse else (lambda i: i)
    ncs = _SEG // _C
    col = lambda off: _vspec((bsz, _SEG, DN), lambda i: (0, seg(i), off))
    return (col, _vspec((bsz, _SEG, LANES), lambda i: (0, seg(i), 0)),
            _vspec((bsz, DN_HEADS, ncs, _C), lambda i: (0, 0, seg(i), 0)),
            _vspec((bsz, DN_HEADS, ncs, DN_HD, DN_HD), lambda i: (0, 0, seg(i), 0, 0)),
            _vspec((bsz, DN_HEADS, ncs, _C, _C), lambda i: (0, 0, seg(i), 0, 0)))


def _chunk_pre(q_ref, k_ref, v_ref, bg_ref, gr_ref, c, bb, hh):
    r0 = pl.multiple_of(c * _C, _C)
    ri = lax.broadcasted_iota(jnp.int32, (_C, _C), 0)
    ci = lax.broadcasted_iota(jnp.int32, (_C, _C), 1)
    q = q_ref[bb, pl.ds(r0, _C), _hsl(hh)] * (DN_HD ** -0.5)
    k = k_ref[bb, pl.ds(r0, _C), _hsl(hh)]
    v = v_ref[bb, pl.ds(r0, _C), _hsl(hh)]
    bgc = bg_ref[bb, pl.ds(r0, _C), :]
    beta = bgc[:, hh:hh + 1]
    g_col = bgc[:, DN_HEADS + hh:DN_HEADS + hh + 1]
    g_row = gr_ref[bb, hh, pl.ds(c, 1), :]
    gc_col = jnp.sum(jnp.where(ri >= ci, g_row, 0.0), axis=1, keepdims=True)
    gc_row = jnp.sum(jnp.where(ri <= ci, g_col, 0.0), axis=0, keepdims=True)
    gc_last = jnp.sum(g_col, axis=0, keepdims=True)
    diff = gc_col - gc_row
    decay = jnp.where(ri >= ci, jnp.exp(jnp.where(ri >= ci, diff, 0.0)), 0.0)
    diff_t = gc_row - gc_col
    decay_t = jnp.where(ri <= ci, jnp.exp(jnp.where(ri <= ci, diff_t, 0.0)), 0.0)
    eg = jnp.exp(gc_col)
    et = jnp.exp(gc_last - gc_col)
    gl = jnp.exp(gc_last)
    kb = k * beta
    vb = v * beta
    return dict(r0=r0, bb=bb, hh=hh, q=q, k=k, v=v, beta=beta, decay=decay, decay_t=decay_t, eg=eg, et=et, gl=gl, kb=kb, vb=vb,
                ri=ri, ci=ci)


def _chunk_solve(ms, tms=None):
    for m in ms:
        m["kk_t"] = _dot(m["k"], m["kb"], _NT)
        m["qk"] = _dot(m["q"], m["k"], _NT)
        m["kk"] = _dot(m["kb"], m["k"], _NT)
        if tms is not None:
            m["qk_t"] = _dot(m["k"], m["q"], _NT)
    if tms is None:
        tms = _tri_inverses([jnp.where(m["ri"] > m["ci"], m["kk"] * m["decay"], 0.0) for m in ms],
                            [jnp.where(m["ri"] < m["ci"], m["kk_t"] * m["decay_t"], 0.0) for m in ms])
    for m, tm in zip(ms, tms):
        m["tm_f32"] = tm
    for m in ms:
        rhs = jnp.concatenate([m["vb"], m["kb"] * m["eg"]], axis=1)
        m["tm"] = _split(m["tm_f32"])
        m["sol"] = _dot3(m["tm"], rhs, _NN)
        m["intra"] = jnp.where(m["ri"] >= m["ci"], m["qk"] * m["decay"], 0.0)


def _dn_chunk_fwd(qkvn, bg, g_rows, bsz, plug=None):
    t = qkvn.shape[0]
    s = t // bsz
    nc, nseg = s // _C, s // _SEG
    pairs = [(bb, hh) for bb in range(bsz) for hh in _HEADS]

    def body(q_ref, k_ref, v_ref, bg_ref, gr_ref, o_ref, st_ref, tm_ref, s_ref):
        @pl.when(pl.program_id(0) == 0)
        def _():
            s_ref[...] = jnp.zeros_like(s_ref)

        def chunk(c, carry):
            ms = [_chunk_pre(q_ref, k_ref, v_ref, bg_ref, gr_ref, c, bb, hh) for bb, hh in pairs]
            _chunk_solve(ms)
            sts = [s_ref[i] for i in range(len(pairs))]
            for m, st in zip(ms, sts):
                st_ref[m["bb"], m["hh"], c] = st
                tm_ref[m["bb"], m["hh"], c] = m["tm_f32"]
            ws = [_dot(m["sol"][:, DN_HD:], st, _NN) for m, st in zip(ms, sts)]
            qs = [_dot(m["q"] * m["eg"], st, _NN) for m, st in zip(ms, sts)]
            v_new = [m["sol"][:, :DN_HD] - a for m, a in zip(ms, ws)]
            iv = [_dot(m["intra"], vn, _NN) for m, vn in zip(ms, v_new)]
            upd = [_dot(m["k"] * m["et"], vn, _TN) for m, vn in zip(ms, v_new)]
            for i, (bb, hh) in enumerate(pairs):
                s_ref[i] = sts[i] * ms[i]["gl"] + upd[i]
                o_ref[bb, pl.ds(ms[i]["r0"], _C), _hsl(hh)] = qs[i] + iv[i]
            return carry

        lax.fori_loop(0, _SEG // _C, chunk, 0)

    col, bgs, grs, sts_spec, tms_spec = _chunk_specs(bsz, nseg, False)
    q3, bg3 = qkvn.reshape(bsz, s, 3 * DN), bg.reshape(bsz, s, LANES)
    (o, states, tms), extra = _plugged_call(
        body, plug, _grid_ends((nseg,)), (q3, q3, q3, bg3, g_rows), name="dn_chunk_fwd", grid=(nseg,),
        in_specs=[col(0), col(1), col(2), bgs, grs], out_specs=[col(0), sts_spec, tms_spec],
        out_shape=[jax.ShapeDtypeStruct((bsz, s, DN), F32), jax.ShapeDtypeStruct((bsz, DN_HEADS, nc, DN_HD, DN_HD), F32),
                   jax.ShapeDtypeStruct((bsz, DN_HEADS, nc, _C, _C), F32)],
        scratch_shapes=[pltpu.VMEM((bsz * DN_HEADS, DN_HD, DN_HD), F32)])
    return o.reshape(t, DN), (states, tms), extra


def _dn_chunk_bwd(qkvn, bg, g_rows, states, do, bsz, plug=None):
    t = qkvn.shape[0]
    s = t // bsz
    nc, nseg = s // _C, s // _SEG
    pairs = [(bb, hh) for bb in range(bsz) for hh in _HEADS]

    def body(q_ref, k_ref, v_ref, bg_ref, gr_ref, st_ref, tm_ref, do_ref, dqkv_ref, dbg_ref, ds_ref):
        @pl.when(pl.program_id(0) == 0)
        def _():
            ds_ref[...] = jnp.zeros_like(ds_ref)

        def chunk(cc, carry):
            c = _SEG // _C - 1 - cc
            ms = [_chunk_pre(q_ref, k_ref, v_ref, bg_ref, gr_ref, c, bb, hh) for bb, hh in pairs]
            _chunk_solve(ms, [tm_ref[bb, hh, c] for bb, hh in pairs])
            ri, ci = ms[0]["ri"], ms[0]["ci"]
            for i, m in enumerate(ms):
                m["st"] = st_ref[m["bb"], m["hh"], c]
                m["ds_out"] = ds_ref[i]
                m["do"] = do_ref[m["bb"], pl.ds(m["r0"], _C), _hsl(m["hh"])]
                m["w"] = m["sol"][:, DN_HD:]
            for m in ms:
                m["v_new"] = m["sol"][:, :DN_HD] - _dot(m["w"], m["st"], _NN)
            for m in ms:
                m["q_dec"], m["k_tail"] = m["q"] * m["eg"], m["k"] * m["et"]
                m["dk_tail"] = _dot(m["v_new"], m["ds_out"], _NT)
                m["dv_new"] = _dot(m["k_tail"], m["ds_out"], _NN) + _dot(m["intra"], m["do"], _TN)
                m["dq_dec"] = _dot(m["do"], m["st"], _NT)
                m["ds_in"] = m["ds_out"] * m["gl"] + _dot(m["q_dec"], m["do"], _TN)
                m["dintra"] = jnp.where(ri >= ci, _dot(m["do"], m["v_new"], _NT), 0.0)
                m["dintra_t"] = jnp.where(ri <= ci, _dot(m["v_new"], m["do"], _NT), 0.0)
            for m in ms:
                m["dw"] = -_dot(m["dv_new"], m["st"], _NT)
                m["ds_in"] = m["ds_in"] - _dot(m["w"], m["dv_new"], _TN)
            for m in ms:
                dsol = jnp.concatenate([m["dv_new"], m["dw"]], axis=1)
                m["drhs"] = _dot3(m["tm"], dsol, _TN)
            for m in ms:
                m["dl"] = jnp.where(ri > ci, -_dot(m["drhs"], m["sol"], _NT), 0.0)
                m["dl_t"] = jnp.where(ri < ci, -_dot(m["sol"], m["drhs"], _NT), 0.0)
            for m in ms:
                m["dkb2"] = _dot(m["dl"] * m["decay"], m["k"], _NN)
                m["dk"] = _dot(m["dl_t"] * m["decay_t"], m["kb"], _NN) + _dot(m["dintra_t"] * m["decay_t"], m["q"], _NN)
                m["dq"] = _dot(m["dintra"] * m["decay"], m["k"], _NN)
            for m in ms:
                _chunk_bwd_finish(m)
            for m in ms:
                ones_ge = jnp.where(ri <= ci, 1.0, 0.0).astype(BF16)
                gh, gl_ = _split(m["dgc"] + jnp.zeros((_C, LANES), F32))
                m["dg_b"] = jnp.dot(ones_ge, gh, preferred_element_type=F32) + jnp.dot(ones_ge, gl_, preferred_element_type=F32)
            lane = lax.broadcasted_iota(jnp.int32, (_C, LANES), 1)
            for i, m in enumerate(ms):
                bb, hh, rows = m["bb"], m["hh"], pl.ds(m["r0"], _C)
                dqkv_ref[0, bb, rows, _hsl(hh)] = m["dq"] * (DN_HD ** -0.5)
                dqkv_ref[1, bb, rows, _hsl(hh)] = m["dk"]
                dqkv_ref[2, bb, rows, _hsl(hh)] = m["dv"]
                dbg_ref[bb, rows, _hsl(hh)] = jnp.where(lane == 0, m["dbeta"], jnp.where(lane == 1, m["dg_b"], 0.0))
                ds_ref[i] = m["ds_in"]
            return carry

        lax.fori_loop(0, _SEG // _C, chunk, 0)

    col, bgs, grs, sts_spec, tms_spec = _chunk_specs(bsz, nseg, True)
    q3, bg3, do3 = qkvn.reshape(bsz, s, 3 * DN), bg.reshape(bsz, s, LANES), do.reshape(bsz, s, DN)
    (dqkv, dbg), extra = _plugged_call(
        body, plug, _grid_ends((nseg,)), (q3, q3, q3, bg3, g_rows, *states, do3), name="dn_chunk_bwd", grid=(nseg,),
        in_specs=[col(0), col(1), col(2), bgs, grs, sts_spec, tms_spec, col(0)],
        out_specs=[_vspec((3, bsz, _SEG, DN), lambda i: (0, 0, nseg - 1 - i, 0)), col(0)],
        out_shape=[jax.ShapeDtypeStruct((3, bsz, s, DN), F32), jax.ShapeDtypeStruct((bsz, s, DN), F32)],
        scratch_shapes=[pltpu.VMEM((bsz * DN_HEADS, DN_HD, DN_HD), F32)])
    return dqkv.reshape(3, t, DN), dbg.reshape(t, DN), extra


def _chunk_bwd_finish(m):
    q, k, v, beta, decay, decay_t = m["q"], m["k"], m["v"], m["beta"], m["decay"], m["decay_t"]
    eg, et, gl, kb, dl, dl_t, dintra, dintra_t = m["eg"], m["et"], m["gl"], m["kb"], m["dl"], m["dl_t"], m["dintra"], m["dintra_t"]
    dq_dec, dk_tail, dq, dk = m["dq_dec"], m["dk_tail"], m["dq"], m["dk"]
    dgl = jnp.sum(jnp.sum(m["ds_out"] * m["st"], axis=1, keepdims=True), axis=0, keepdims=True)
    dvb, dkbeg = m["drhs"][:, :DN_HD], m["drhs"][:, DN_HD:]
    dkb = dkbeg * eg + m["dkb2"]
    deg = jnp.sum(dkbeg * kb, axis=1, keepdims=True)
    em = (dl * m["kk"] + dintra * m["qk"]) * decay
    em_t = (dl_t * m["kk_t"] + dintra_t * m["qk_t"]) * decay_t
    dgc = jnp.sum(em, axis=1, keepdims=True) - jnp.sum(em_t, axis=1, keepdims=True)
    dq = dq + dq_dec * eg
    deg = deg + jnp.sum(dq_dec * q, axis=1, keepdims=True)
    dk = dk + dk_tail * et
    det = jnp.sum(dk_tail * k, axis=1, keepdims=True)
    dgc = dgc + deg * eg - det * et
    dgc_last = jnp.sum(det * et, axis=0, keepdims=True) + dgl * gl
    rcol = lax.broadcasted_iota(jnp.int32, (_C, 1), 0)
    m["dgc"] = dgc + jnp.where(rcol == _C - 1, dgc_last, 0.0)
    m["dq"] = dq
    m["dk"] = dk + dkb * beta
    m["dbeta"] = jnp.sum(dkb * k, axis=1, keepdims=True) + jnp.sum(dvb * v, axis=1, keepdims=True)
    m["dv"] = dvb * beta


def _mod_fwd(c_all, ada_w_loc, ada_b_loc):
    n, cols = c_all.shape[0], ada_w_loc.shape[1]

    def body(c_ref, w_ref, b_ref, o_ref):
        o_ref[...] = _dot(jax.nn.silu(c_ref[...]), w_ref[...], _NN) + b_ref[...]

    return pl.pallas_call(body, name="mod_fwd", out_shape=jax.ShapeDtypeStruct((n, cols), F32))(c_all, ada_w_loc, ada_b_loc)


def _ada_grad(c_all, dmod_loc, dmod_all):
    d, cols = c_all.shape[1], dmod_loc.shape[1]

    def body(c_ref, dl_ref, da_ref, gw_ref, gb_ref):
        gw_ref[...] = _dot(jax.nn.silu(c_ref[...]), dl_ref[...], _TN)
        gb_ref[...] = jnp.sum(da_ref[...], axis=0, keepdims=True)

    return pl.pallas_call(body, name="ada_grad", out_shape=[jax.ShapeDtypeStruct((d, cols), F32),
                                                           jax.ShapeDtypeStruct((1, dmod_all.shape[1]), F32)])(c_all, dmod_loc, dmod_all)


ELEMENTWISE_BLOCK_BYTES = 3 * 2 ** 19


def _row_tile(r, c=1024):
    fits = [tr for tr in range(16, r + 1, 16) if tr * c * 4 <= ELEMENTWISE_BLOCK_BYTES]
    if not fits:
        return r
    whole = [tr for tr in fits if r % tr == 0]
    return whole[-1] if whole else fits[-1]


def _adamw(w, m, v, grads, name):
    r, rest = w.shape[0], w.shape[1:]
    c = math.prod(rest)
    tr = _row_tile(r, c)
    blk = _vspec((tr,) + rest, lambda i: (i,) + (0,) * len(rest))
    n = len(grads)

    def body(*refs):
        w_ref, m_ref, v_ref = refs[:3]
        g_ref, d_ref, mo_ref, vo_ref = refs[3 + n:]
        g = refs[3][...]
        for p in refs[4:3 + n]:
            g = g + p[...]
        g_ref[...] = g
        d_ref[...], mo_ref[...], vo_ref[...] = _adamw_math(w_ref[...], m_ref[...], v_ref[...], g)

    o = jax.ShapeDtypeStruct(w.shape, F32)
    return pl.pallas_call(body, name=name, grid=(pl.cdiv(r, tr),), in_specs=[blk] * (3 + n), out_specs=[blk] * 4,
                          out_shape=[o] * 4)(w, m, v, *grads)


def _adamw_halves(w, m, v, own, other, ic, name):
    r, _, c = w.shape
    hc = c // 2
    tr = _row_tile(r, c)
    blk = _vspec((tr, 1, hc), lambda i, j: (i, 0, j))
    half = _vspec((tr, 1, hc), lambda i, j: (i, 0, 0))

    def body(ic_ref, w_ref, m_ref, v_ref, own_ref, other_ref, g_ref, d_ref, mo_ref, vo_ref):
        g = jnp.where(pl.program_id(1) == ic_ref[0], own_ref[...], other_ref[...])
        g_ref[...] = g
        d_ref[...], mo_ref[...], vo_ref[...] = _adamw_math(w_ref[...], m_ref[...], v_ref[...], g)

    o = jax.ShapeDtypeStruct(w.shape, F32)
    return pl.pallas_call(body, name=name, grid=(pl.cdiv(r, tr), 2),
                          in_specs=[pl.BlockSpec(memory_space=pltpu.SMEM), blk, blk, blk, half, half], out_specs=[blk] * 4,
                          out_shape=[o] * 4)(ic, w, m, v, own, other)


def _adamw_math(w, m, v, g):
    m_new = ADAM_B1 * m + (1.0 - ADAM_B1) * g
    v_new = ADAM_B2 * v + (1.0 - ADAM_B2) * jnp.square(g)
    m_hat = m_new / (1.0 - ADAM_B1 ** ADAM_STEP)
    v_hat = v_new / (1.0 - ADAM_B2 ** ADAM_STEP)
    return -ADAM_LR * (m_hat / (jnp.sqrt(v_hat) + ADAM_EPS) + ADAM_WD * w), m_new, v_new


def _adamw_small(ws, ms, vs, gs, name):
    n = len(ws)

    def body(*refs):
        for i in range(n):
            w_ref, m_ref, v_ref, g_ref = (refs[j * n + i] for j in range(4))
            go_ref, d_ref, mo_ref, vo_ref = refs[4 * n + 4 * i:4 * n + 4 * i + 4]
            g = g_ref[...]
            go_ref[...] = g
            d_ref[...], mo_ref[...], vo_ref[...] = _adamw_math(w_ref[...], m_ref[...], v_ref[...], g)

    res = pl.pallas_call(body, name=name, out_shape=[jax.ShapeDtypeStruct(a.shape, F32) for a in ws for _ in range(4)])(
        *ws, *ms, *vs, *gs)
    return [res[4 * i:4 * i + 4] for i in range(n)]


def _sum_lead(x, name, rows_apart=False):
    p, r, c = x.shape
    tr = _row_tile(r, c)
    mid = (1,) if rows_apart else ()

    def body(x_ref, o_ref):
        acc = x_ref[0].astype(F32)
        for i in range(1, p):
            acc = acc + x_ref[i].astype(F32)
        o_ref[...] = acc.reshape(o_ref.shape)

    return pl.pallas_call(body, name=name, grid=(pl.cdiv(r, tr),), in_specs=[_vspec((p, tr, c), lambda i: (0, i, 0))],
                          out_specs=_vspec((tr,) + mid + (c,), lambda i: (i,) + (0,) * (1 + len(mid))),
                          out_shape=jax.ShapeDtypeStruct((r,) + mid + (c,), F32))(x)


def _allgather8(x_shard, name, plug=None):
    m_per, n = x_shard.shape

    def body(x_ref, out_ref, send_sems, recv_sems, local_sem):
        x, y, c = lax.axis_index("x"), lax.axis_index("y"), lax.axis_index("c")
        me, sibling = (x, y, c), (x, y, 1 - c)
        chips = [(1 - x, y), (x, 1 - y), (1 - x, 1 - y)]

        def rows(px, py, pc):
            return out_ref.at[pl.ds((4 * px + 2 * py + pc) * m_per, m_per), :]

        def copy(k, block, to, src=None):
            return pltpu.make_async_remote_copy(
                src_ref=rows(*block) if src is None else src, dst_ref=rows(*block), send_sem=send_sems.at[k],
                recv_sem=recv_sems.at[k], device_id=to, device_id_type=MESH)

        mine = pltpu.make_async_copy(x_ref, rows(*me), local_sem)
        mine.start()
        first = [copy(0, me, sibling, src=x_ref)]
        first += [copy(1 + j, me, (*chip, c), src=x_ref) for j, chip in enumerate(chips)]
        for cp in first:
            cp.start()
        passed = [copy(4 + j, (*chip, c), sibling) for j, chip in enumerate(chips)]
        for j, chip in enumerate(chips):
            copy(1 + j, (*chip, c), me).wait_recv()
            passed[j].start()
        copy(0, sibling, me).wait_recv()
        for j, chip in enumerate(chips):
            copy(4 + j, (*chip, 1 - c), me).wait_recv()
        for cp in first + passed:
            cp.wait_send()
        mine.wait()

    grid = (1,)
    (out,), extra = _plugged_call(
        body, plug, _grid_ends(grid), (x_shard,), name=name, grid=grid, out_shape=[jax.ShapeDtypeStruct((8 * m_per, n), x_shard.dtype)],
        in_specs=[pl.BlockSpec(memory_space=pltpu.VMEM)], out_specs=[pl.BlockSpec(memory_space=pltpu.VMEM)],
        scratch_shapes=[pltpu.SemaphoreType.DMA((7,)), pltpu.SemaphoreType.DMA((7,)), pltpu.SemaphoreType.DMA])
    return out if plug is None else (out, extra)


_HBM = pl.BlockSpec(memory_space=pltpu.HBM)


def _mesh_place():
    x, y, c = lax.axis_index("x"), lax.axis_index("y"), lax.axis_index("c")
    return x, y, c, 2 * x + y, [(1 - x, y), (x, 1 - y), (1 - x, 1 - y)]


def _gather_plug(shards):
    n = len(shards)

    def half(ref, c, lead=None):
        r, cols = ref.shape[-2] // 2, ref.shape[-1] // 2
        if r % 16 == 0:
            rows = pl.ds(pl.multiple_of(c * r, 16), r)
            return ref.at[rows, :] if lead is None else ref.at[lead, rows, :]
        lanes = pl.ds(pl.multiple_of(c * cols, LANES), cols)
        return ref.at[:, lanes] if lead is None else ref.at[lead, :, lanes]

    def copies(ins, outs, send, recv):
        x, y, c, me, chips = _mesh_place()
        ici, fwd, fwd_in = [], [], []
        for i in range(n):
            for j, (px, py) in enumerate(chips):
                q = 2 * px + py
                ici.append((pltpu.make_async_remote_copy(
                    src_ref=half(ins[i], c), dst_ref=half(outs[i], c, me), send_sem=send.at[6 * i + j], recv_sem=recv.at[6 * i + j],
                    device_id=(px, py, c), device_id_type=MESH),
                    pltpu.make_async_remote_copy(
                    src_ref=half(ins[i], c), dst_ref=half(outs[i], c, q), send_sem=send.at[6 * i + j], recv_sem=recv.at[6 * i + j],
                    device_id=(px, py, c), device_id_type=MESH)))
                fwd.append(pltpu.make_async_remote_copy(
                    src_ref=half(outs[i], c, q), dst_ref=half(outs[i], c, q), send_sem=send.at[6 * i + 3 + j],
                    recv_sem=recv.at[6 * i + 3 + j], device_id=(x, y, 1 - c), device_id_type=MESH))
                fwd_in.append(pltpu.make_async_remote_copy(
                    src_ref=half(outs[i], 1 - c, q), dst_ref=half(outs[i], 1 - c, q), send_sem=send.at[6 * i + 3 + j],
                    recv_sem=recv.at[6 * i + 3 + j], device_id=(x, y, 1 - c), device_id_type=MESH))
        return ici, fwd, fwd_in, me

    def start(ins, outs, send, recv, loc):
        ici, _, _, me = copies(ins, outs, send, recv)
        for i in range(n):
            pltpu.make_async_copy(ins[i], outs[i].at[me], loc.at[i]).start()
        for out_cp, _ in ici:
            out_cp.start()

    def finish(ins, outs, send, recv, loc):
        ici, fwd, fwd_in, me = copies(ins, outs, send, recv)
        for (_, in_cp), f in zip(ici, fwd):
            in_cp.wait_recv()
            f.start()
        for f in fwd_in:
            f.wait_recv()
        for (out_cp, _), f in zip(ici, fwd):
            out_cp.wait_send()
            f.wait_send()
        for i in range(n):
            pltpu.make_async_copy(ins[i], outs[i].at[me], loc.at[i]).wait()

    return dict(ins=list(shards), out_shape=[jax.ShapeDtypeStruct((N_CHIPS,) + a.shape, a.dtype) for a in shards],
                scratch=[pltpu.SemaphoreType.DMA((6 * n,)), pltpu.SemaphoreType.DMA((6 * n,)), pltpu.SemaphoreType.DMA((n,))],
                start=start, finish=finish)


def _exchange_plug(pieces):
    n = len(pieces)

    def copies(ins, outs, send, recv):
        x, y, c, me, chips = _mesh_place()
        out_cps, in_cps = [], []
        for i in range(n):
            for j, (px, py) in enumerate(chips):
                q = 2 * px + py
                out_cps.append(pltpu.make_async_remote_copy(src_ref=ins[i].at[q], dst_ref=outs[i].at[me], send_sem=send.at[3 * i + j],
                                                            recv_sem=recv.at[3 * i + j], device_id=(px, py, c), device_id_type=MESH))
                in_cps.append(pltpu.make_async_remote_copy(src_ref=ins[i].at[me], dst_ref=outs[i].at[q], send_sem=send.at[3 * i + j],
                                                           recv_sem=recv.at[3 * i + j], device_id=(px, py, c), device_id_type=MESH))
        return out_cps, in_cps, me

    def start(ins, outs, send, recv, loc):
        out_cps, _, me = copies(ins, outs, send, recv)
        for i in range(n):
            pltpu.make_async_copy(ins[i].at[me], outs[i].at[me], loc.at[i]).start()
        for cp in out_cps:
            cp.start()

    def finish(ins, outs, send, recv, loc):
        out_cps, in_cps, me = copies(ins, outs, send, recv)
        for cp in in_cps:
            cp.wait_recv()
        for cp in out_cps:
            cp.wait_send()
        for i in range(n):
            pltpu.make_async_copy(ins[i].at[me], outs[i].at[me], loc.at[i]).wait()

    return dict(ins=list(pieces), out_shape=[jax.ShapeDtypeStruct(a.shape, a.dtype) for a in pieces],
                scratch=[pltpu.SemaphoreType.DMA((3 * n,)), pltpu.SemaphoreType.DMA((3 * n,)), pltpu.SemaphoreType.DMA((n,))],
                start=start, finish=finish)


def _comm_call(plug, name):
    n_in, n_out = len(plug["ins"]), len(plug["out_shape"])

    def body(*refs):
        ins, outs, sems = refs[:n_in], refs[n_in:n_in + n_out], refs[n_in + n_out:]
        plug["start"](ins, outs, *sems)
        plug["finish"](ins, outs, *sems)

    return pl.pallas_call(body, name=name, out_shape=plug["out_shape"], in_specs=[_HBM] * n_in, out_specs=[_HBM] * n_out,
                          scratch_shapes=plug["scratch"])(*plug["ins"])


def _plugged_call(body, plug, first_last, args, *, name, grid, in_specs, out_specs, out_shape, scratch_shapes=()):
    in_specs, out_specs, out_shape, scratch_shapes = list(in_specs), list(out_specs), list(out_shape), list(scratch_shapes)
    if plug is None:
        return pl.pallas_call(body, name=name, grid=grid, in_specs=in_specs, out_specs=out_specs, out_shape=out_shape,
                              scratch_shapes=scratch_shapes)(*args), []
    n_in, n_out, n_sc = len(in_specs), len(out_specs), len(scratch_shapes)
    p_in, p_out = len(plug["ins"]), len(plug["out_shape"])

    def full(*refs):
        ins, refs = refs[:n_in], refs[n_in:]
        pins, refs = refs[:p_in], refs[p_in:]
        outs, refs = refs[:n_out], refs[n_out:]
        pouts, refs = refs[:p_out], refs[p_out:]
        scr, psems = refs[:n_sc], refs[n_sc:]
        first, last = first_last()

        @pl.when(first)
        def _():
            plug["start"](pins, pouts, *psems)

        body(*ins, *outs, *scr)

        @pl.when(last)
        def _():
            plug["finish"](pins, pouts, *psems)

    res = pl.pallas_call(full, name=name, grid=grid, in_specs=in_specs + [_HBM] * p_in, out_specs=out_specs + [_HBM] * p_out,
                         out_shape=out_shape + plug["out_shape"], scratch_shapes=scratch_shapes + plug["scratch"])(*args, *plug["ins"])
    return res[:n_out], res[n_out:]


def _grid_ends(grid):
    def ends():
        first = last = None
        for ax, n in enumerate(grid):
            i = pl.program_id(ax)
            first = (i == 0) if first is None else first & (i == 0)
            last = (i == n - 1) if last is None else last & (i == n - 1)
        return first, last
    return ends


def _pair_presum(pieces, name):
    n, r, cols = pieces.shape
    hc = cols // 2

    def body(p_ref, o_ref, mine_ref, land_ref, send_sem, recv_sem, local_sem):
        x, y, c = lax.axis_index("x"), lax.axis_index("y"), lax.axis_index("c")
        keep = pl.ds(pl.multiple_of(c * hc, LANES), hc)
        give = pl.ds(pl.multiple_of((1 - c) * hc, LANES), hc)
        push = pltpu.make_async_remote_copy(src_ref=p_ref.at[:, :, give], dst_ref=land_ref, send_sem=send_sem, recv_sem=recv_sem,
                                            device_id=(x, y, 1 - c), device_id_type=MESH)
        own = pltpu.make_async_copy(p_ref.at[:, :, keep], mine_ref, local_sem)
        push.start()
        own.start()
        own.wait()
        push.wait_recv()
        for i in range(n):
            o_ref[i] = (mine_ref[i].astype(F32) + land_ref[i].astype(F32)).astype(BF16)
        push.wait_send()

    return pl.pallas_call(
        body, name=name, out_shape=jax.ShapeDtypeStruct((n, r, hc), BF16), in_specs=[_HBM],
        out_specs=pl.BlockSpec(memory_space=pltpu.VMEM),
        scratch_shapes=[pltpu.VMEM((n, r, hc), BF16), pltpu.VMEM((n, r, hc), BF16), pltpu.SemaphoreType.DMA,
                        pltpu.SemaphoreType.DMA, pltpu.SemaphoreType.DMA],
    )(pieces)


def _sibling_exchange(arrs, name):
    n = len(arrs)

    def body(*refs):
        ins, outs = refs[:n], refs[n:2 * n]
        send_sems, recv_sems = refs[2 * n:]
        sibling = (lax.axis_index("x"), lax.axis_index("y"), 1 - lax.axis_index("c"))
        cps = [pltpu.make_async_remote_copy(src_ref=ins[i], dst_ref=outs[i], send_sem=send_sems.at[i], recv_sem=recv_sems.at[i],
                                            device_id=sibling, device_id_type=MESH) for i in range(n)]
        for cp in cps:
            cp.start()
        for cp in cps:
            cp.wait()

    return pl.pallas_call(
        body, name=name, out_shape=[jax.ShapeDtypeStruct(a.shape, a.dtype) for a in arrs], in_specs=[_HBM] * n, out_specs=[_HBM] * n,
        scratch_shapes=[pltpu.SemaphoreType.DMA((n,)), pltpu.SemaphoreType.DMA((n,))],
    )(*arrs)


_SEGS = ((0, AQ, C_Q), (AQ, AKV, C_K), (AQ + AKV, AKV, C_V), (AQ + 2 * AKV, 3 * DN, C_DQKV), (2304, DN, C_DZ),
         (2816, 2 * DN_HEADS, C_BD), (2824, D_MODEL, C_GA), (3848, D_MODEL, C_GD))
SHARD_ROWS = IN_DIM // N_CHIPS


def _to_padded(w4):
    wt = w4.reshape(IN_DIM, w4.shape[2])
    parts = [wt[o:o + n] for o, n, _ in sorted(_SEGS, key=lambda sg: sg[2])]
    return jnp.concatenate(parts + [jnp.zeros((IN_PAD - IN_DIM, wt.shape[1]), wt.dtype)], axis=0)


def _from_padded(gt):
    return jnp.concatenate([gt[ps:ps + n] for _, n, ps in sorted(_SEGS)], axis=0).reshape(N_CHIPS, SHARD_ROWS, gt.shape[1])


def _lane_vec(a):
    return jnp.zeros((1, LANES), F32).at[0, DN_HEADS:2 * DN_HEADS].set(a)


_ROW_SHARDED = ("w_in", "w_out", "ffn_w_down")
_FFN = ("ffn_w_up", "ffn_w_down")
_LATE_MIXER = ("w_attn_branch", "w_dn_branch", "w_out")


def _pieces(k, a):
    if a.ndim == 3:
        return a
    if k in _ROW_SHARDED:
        return a.reshape(N_CHIPS, a.shape[0] // N_CHIPS, a.shape[1]).astype(BF16)
    return jnp.transpose(a.reshape(a.shape[0], N_CHIPS, a.shape[1] // N_CHIPS), (1, 0, 2)).astype(BF16)


def _assemble(k, a):
    if k in _ROW_SHARDED:
        return a.reshape(-1, a.shape[2])
    return jnp.transpose(a, (1, 0, 2)).reshape(a.shape[1], -1)


def _device_step(x2, tgt2, mod, p, bsz, shards=None):
    d = D_MODEL
    on_mesh = shards is not None
    p = dict(p)
    sh1, sc1, g1, sh2, sc2, g2 = [mod[:, i * d:(i + 1) * d].reshape(bsz, 1, d) for i in range(N_MOD)]
    alog_v, dt_v = _lane_vec(p["dn_a_log"]), _lane_vec(p["dn_dt_bias"])
    sinks = p["attn_sinks"].reshape(1, AQ_HEADS)
    u1 = _pre_fwd(x2, p["norm_mix_pre"], sc1, sh1, "pre1_fwd")
    if on_mesh:
        proj, got = _mm(u1, p["w_in"], "nt", F32, "mm_proj", _gather_plug(shards["late_mixer"]))
        p.update({k: _assemble(k, a) for k, a in zip(_LATE_MIXER, got)})
    else:
        proj = _mm(u1, p["w_in"], "nt", F32, "mm_proj")
    bias = _bias_build(p["rel_bias"])
    y_attn = _attn_fwd(proj, bias, sinks, bsz)
    qkvn = _dn_prep_fwd(proj, p["dn_conv_w"], bsz)
    bg = _bg_fwd(proj, alog_v, dt_v, bsz)
    nc = x2.shape[0] // bsz // DN_CHUNK
    g_rows = jnp.transpose(bg[:, DN_HEADS:2 * DN_HEADS].reshape(bsz, nc, DN_CHUNK, DN_HEADS), (0, 3, 1, 2))
    o, states, got = _dn_chunk_fwd(qkvn, bg, g_rows, bsz, _gather_plug(shards["ffn"][:1]) if on_mesh else None)
    for k, a in zip(_FFN[:1], got):
        p[k] = _assemble(k, a)
    y_dn = _dn_out_fwd(o, proj, p["dn_norm_w"])
    ya = _mm(y_attn, p["w_attn_branch"], "nn", F32, "mm_ya")
    yd = _mm(y_dn, p["w_dn_branch"], "nn", F32, "mm_yd")
    merged = _merge_fwd(proj, ya, yd)
    y1 = _mm(merged, p["w_out"], "nn", F32, "mm_y1")
    h1, u2 = _post_pre_fwd(x2, y1, p["norm_mix_post"], g1, p["norm_ffn_pre"], sc2, sh2, "post1_pre2_fwd")
    if on_mesh:
        up, got = _mm(u2, p["ffn_w_up"], "nn", F32, "mm_up", _gather_plug(shards["ffn"][1:]))
        p["ffn_w_down"] = _assemble("ffn_w_down", got[0])
    else:
        up = _mm(u2, p["ffn_w_up"], "nn", F32, "mm_up")
    act = _ffn_act_fwd(up, p["ffn_conv_w"], bsz)
    y2 = _mm(act, p["ffn_w_down"], "nn", F32, "mm_y2")
    dh2, dy2, g_ffn_post, dg2, sq = _post_loss_bwd(h1, y2, p["norm_ffn_post"], g2, tgt2, "post2_loss_bwd")
    g = {}
    g["norm_ffn_post"] = g_ffn_post
    dact = _mm(dy2, p["ffn_w_down"], "nt", F32, "mm_dact")
    g["ffn_w_down"] = _mm(act, dy2, "tn", BF16, "mm_dwdown")
    dupg, dupv, dcwg, dcwv, got_down = _ffn_act_bwd(
        up, p["ffn_conv_w"], dact, bsz, _exchange_plug([_pieces("ffn_w_down", g["ffn_w_down"])]) if on_mesh else None)
    g["ffn_conv_w"] = jnp.concatenate([dcwg, dcwv], axis=1)
    g["ffn_w_up"] = jnp.concatenate([_mm(u2, dupg, "tn", BF16, "mm_dwup_gate", split=N_CHIPS // 2),
                                     _mm(u2, dupv, "tn", BF16, "mm_dwup_val", split=N_CHIPS // 2)], axis=0)
    du2 = _mm(dupg, p["ffn_w_up"], "nt", F32, "mm_du2_gate", b_kblock=0)
    du2 = _mm(dupv, p["ffn_w_up"], "nt", F32, "mm_du2_val", b_kblock=1, add=du2)
    dh1, dy1, g["norm_ffn_pre"], dsc2, dsh2, g["norm_mix_post"], dg1 = _pre_post_bwd(
        h1, p["norm_ffn_pre"], sc2, sh2, du2, dh2, y1, p["norm_mix_post"], g1, "pre2_post1_bwd")
    dmerged = _mm(dy1, p["w_out"], "nt", F32, "mm_dmerged")
    g["w_out"] = _mm(merged, dy1, "tn", BF16, "mm_dwout")
    dga, dgd, dya, dyd = _merge_bwd(proj, ya, yd, dmerged)
    dy_attn = _mm(dya, p["w_attn_branch"], "nt", BF16, "mm_dyattn")
    g["w_attn_branch"] = _mm(y_attn, dya, "tn", BF16, "mm_dwab", split=N_CHIPS)
    dy_dn = _mm(dyd, p["w_dn_branch"], "nt", F32, "mm_dydn")
    g["w_dn_branch"] = _mm(y_dn, dyd, "tn", BF16, "mm_dwdb", split=N_CHIPS)
    do, dz, g["dn_norm_w"] = _dn_out_bwd(o, proj, p["dn_norm_w"], dy_dn)
    def plug_for(names):
        return _exchange_plug([_pieces(k, g[k]) for k in names]) if on_mesh else None

    early = ("w_out", "w_attn_branch", "w_dn_branch")
    dqkvn, dbg4, got_up = _dn_chunk_bwd(qkvn, bg, g_rows, states, do, bsz, plug_for(_FFN[:1]))
    got_ffn = list(got_up) + list(got_down)
    d_dqkv, g["dn_conv_w"] = _dn_prep_bwd(proj, p["dn_conv_w"], dqkvn, bsz)
    dbd, g["dn_a_log"], g["dn_dt_bias"] = _bg_bwd(proj, alog_v, dt_v, dbg4, bsz)
    dq, dk, dv, dbias, g["attn_sinks"], got_early = _attn_bwd(proj, bias, sinks, dy_attn, bsz, plug_for(early))
    g["rel_bias"] = _bias_grad(dbias)
    dproj = jnp.concatenate([d_dqkv, dq, dz, dga, dgd, dk, dv, dbd], axis=1)
    g["w_in"] = _from_padded(_mm(dproj, u1, "tn", BF16, "mm_dwin"))
    if on_mesh:
        g["w_in"] = _pair_presum(g["w_in"], "presum_w_in")
        du1, got_in = _mm(dproj, p["w_in"], "nn", F32, "mm_du1", plug_for(("w_in",)))
        g.update(zip(_FFN + early + ("w_in",), list(got_ffn) + list(got_early) + list(got_in)))
    else:
        du1 = _mm(dproj, p["w_in"], "nn", F32, "mm_du1")
    dx, g["norm_mix_pre"], dsc1, dsh1 = _pre_bwd(x2, p["norm_mix_pre"], sc1, sh1, du1, dh1, "pre1_bwd")
    dmod = jnp.concatenate([dsh1, dsc1, dg1, dsh2, dsc2, dg2], axis=-1).reshape(bsz, N_MOD * d)
    return sq, dx, dmod, g


_SMALL = (("norm_mix_pre", D_MODEL), ("norm_mix_post", D_MODEL), ("norm_ffn_pre", D_MODEL), ("norm_ffn_post", D_MODEL),
          ("dn_norm_w", DN_HD), ("dn_a_log", LANES), ("dn_dt_bias", LANES), ("attn_sinks", AQ_HEADS * LANES),
          ("rel_bias", AQ_HEADS * LANES), ("dn_conv_w", DN_CONV * 3 * DN), ("ffn_conv_w", FFN_CONV * 2 * D_FF))


def _pack_rows(parts, rows):
    flat = jnp.concatenate([a.reshape(-1) for a in parts])
    return jnp.concatenate([flat, jnp.zeros((rows * LANES - flat.shape[0],), F32)]).reshape(rows, LANES)


def _pad128(a):
    flat = a.reshape(-1)
    n = -(-flat.shape[0] // LANES) * LANES
    return jnp.concatenate([flat, jnp.zeros((n - flat.shape[0],), F32)]) if n != flat.shape[0] else flat


_W_NAMES = ("ada_w", "ada_b", "norm_mix_pre", "norm_mix_post", "norm_ffn_pre", "norm_ffn_post", "w_in", "dn_conv_w", "dn_a_log",
            "dn_dt_bias", "dn_norm_w", "attn_sinks", "rel_bias", "w_attn_branch", "w_dn_branch", "w_out", "ffn_w_up", "ffn_conv_w",
            "ffn_w_down")
_BIG = ("w_in", "w_attn_branch", "w_dn_branch", "w_out", "ffn_w_up", "ffn_w_down")


def kernel(x, c, *rest):
    nw = len(_W_NAMES)
    w = dict(zip(_W_NAMES, rest[:nw]))
    loss_target = rest[nw]
    m = dict(zip(_W_NAMES, rest[nw + 1:2 * nw + 1]))
    v = dict(zip(_W_NAMES, rest[2 * nw + 1:3 * nw + 1]))
    ix, iy, ic = lax.axis_index("x"), lax.axis_index("y"), lax.axis_index("c")
    chip, dev = 2 * ix + iy, 4 * ix + 2 * iy + ic
    bsz, s, d = x.shape
    t = bsz * s
    n_dev = 8

    front_rows = 64
    front = _pack_rows([c, w["dn_conv_w"], w["ffn_conv_w"]], front_rows)
    w_in_t = jnp.swapaxes(w["w_in"][0], 0, 1).astype(BF16)
    w_in_cut = SHARD_ROWS // 32 * 16
    front_all, (w_in_lo,) = _allgather8(front, "ag_front", _gather_plug([w_in_t[:w_in_cut]]))
    front_all = front_all.reshape(n_dev, front_rows * LANES)
    n_c, n_dc, n_fc = bsz * d, DN_CONV * 3 * DN // N_CHIPS, FFN_CONV * 2 * D_FF // N_CHIPS
    c_all = front_all[:, :n_c].reshape(n_dev * bsz, d)
    per_chip = front_all[0::2]
    dn_conv_full = jnp.transpose(per_chip[:, n_c:n_c + n_dc].reshape(N_CHIPS, DN_CONV, -1), (1, 0, 2)).reshape(DN_CONV, 3 * DN)
    ffn_conv_full = jnp.transpose(per_chip[:, n_c + n_dc:n_c + n_dc + n_fc].reshape(N_CHIPS, FFN_CONV, -1), (1, 0, 2)).reshape(FFN_CONV, 2 * D_FF)

    mod_cols = N_MOD * d // N_CHIPS
    ada_b_loc = lax.dynamic_slice(w["ada_b"], (0, chip * mod_cols), (1, mod_cols))
    mod_part = _mod_fwd(c_all, w["ada_w"][0], ada_b_loc)
    mod_all, (w_in_hi,) = _allgather8(mod_part, "ag_mod", _gather_plug([w_in_t[w_in_cut:]]))
    mod_all = mod_all.reshape(n_dev, n_dev * bsz, mod_cols)[0::2]
    mod = jnp.transpose(lax.dynamic_slice(mod_all, (0, dev * bsz, 0), (N_CHIPS, bsz, mod_cols)), (1, 0, 2)).reshape(bsz, N_MOD * d)

    p = {"w_in": _to_padded(jnp.concatenate([w_in_lo, w_in_hi], axis=1))}
    shards = {"late_mixer": [w[k][0].astype(BF16) for k in _LATE_MIXER], "ffn": [w[k][0].astype(BF16) for k in _FFN]}
    for k in ("norm_mix_pre", "norm_mix_post", "norm_ffn_pre", "norm_ffn_post", "dn_norm_w", "attn_sinks"):
        p[k] = w[k]
    p["dn_a_log"], p["dn_dt_bias"], p["rel_bias"] = w["dn_a_log"][0], w["dn_dt_bias"][0], w["rel_bias"]
    p["dn_conv_w"], p["ffn_conv_w"] = dn_conv_full, ffn_conv_full

    sq, dx, dmod, g = _device_step(x.reshape(t, d), loss_target.reshape(t, d), mod, p, bsz, shards)
    loss = lax.psum(0.5 * jnp.sum(sq), ("x", "y", "c"))

    g["dn_a_log"], g["dn_dt_bias"] = g["dn_a_log"].reshape(-1), g["dn_dt_bias"].reshape(-1)
    small_rows = 328
    small = _pack_rows([dmod] + [g[k] for k, _ in _SMALL], small_rows)
    small_all = _allgather8(small, "ag_small").reshape(n_dev, small_rows, LANES)
    n_dm = bsz * N_MOD * d
    dmod_all = small_all.reshape(n_dev, -1)[:, :n_dm].reshape(n_dev * bsz, N_MOD * d)
    tot = _sum_lead(small_all, "sum_small").reshape(-1)
    gs, off = {}, n_dm
    for k, n in _SMALL:
        gs[k] = tot[off:off + n]
        off += n
    grad = {}
    grad["ada_w"], grad["ada_b"] = _ada_grad(c_all, lax.dynamic_slice(dmod_all, (0, chip * mod_cols), (n_dev * bsz, mod_cols)), dmod_all)
    for k in ("norm_mix_pre", "norm_mix_post", "norm_ffn_pre", "norm_ffn_post", "dn_norm_w"):
        grad[k] = gs[k]
    grad["dn_a_log"] = gs["dn_a_log"][DN_HEADS:2 * DN_HEADS]
    grad["dn_dt_bias"] = gs["dn_dt_bias"][DN_HEADS:2 * DN_HEADS]
    grad["attn_sinks"] = gs["attn_sinks"].reshape(AQ_HEADS, LANES)[:, 0]
    grad["rel_bias"] = gs["rel_bias"].reshape(AQ_HEADS, LANES)[:, :REL_BUCKETS].T
    grad["dn_conv_w"] = lax.dynamic_slice(gs["dn_conv_w"].reshape(DN_CONV, 3 * DN), (0, chip * (3 * DN // N_CHIPS)), (DN_CONV, 3 * DN // N_CHIPS))
    grad["ffn_conv_w"] = lax.dynamic_slice(gs["ffn_conv_w"].reshape(FFN_CONV, 2 * D_FF), (0, chip * (2 * D_FF // N_CHIPS)), (FFN_CONV, 2 * D_FF // N_CHIPS))

    mine = [_sum_lead(g[k], "sum_" + k, rows_apart=(k == "w_in")) for k in _BIG]
    theirs = _sibling_exchange(mine, "exchange_cores")

    out = {}
    for k, a, b in zip(_BIG, mine, theirs):
        if k == "w_in":
            tr = lambda z: jnp.transpose(z, (2, 0, 1))
            res = _adamw_halves(tr(w[k]), tr(m[k]), tr(v[k]), a, b, ic.reshape(1).astype(jnp.int32), "adamw_" + k)
            out[k] = [jnp.transpose(r, (1, 2, 0)) for r in res]
        else:
            out[k] = _adamw(w[k][0], m[k][0], v[k][0], [a, b], "adamw_" + k)
    out["ada_w"] = _adamw(w["ada_w"][0], m["ada_w"][0], v["ada_w"][0], [grad["ada_w"]], "adamw_ada_w")
    small_names = [k for k in _W_NAMES if k not in _BIG and k != "ada_w"]
    res_small = _adamw_small([w[k] for k in small_names], [m[k] for k in small_names], [v[k] for k in small_names],
                             [grad[k].reshape(w[k].shape) for k in small_names], "adamw_small")
    out.update(zip(small_names, res_small))
    for k in _BIG + ("ada_w",):
        out[k] = [r.reshape(w[k].shape) for r in out[k]]
    grads, deltas, new_m, new_v = ([out[k][i] for k in _W_NAMES] for i in range(4))
    return (loss, dx.reshape(bsz, s, d), *grads, *deltas, *new_m, *new_v)
```

```python
import functools
import math

import numpy as np
import jax
import jax.numpy as jnp
from jax import lax
from jax.experimental import pallas as pl
from jax.experimental.pallas import tpu as pltpu

F32 = jnp.float32
BF16 = jnp.bfloat16
MESH = pl.DeviceIdType.MESH

D_MODEL = 1024
N_MOD = 6
AQ_HEADS, AKV_HEADS, A_HD, WINDOW = 8, 2, 64, 128
REL_BUCKETS, REL_MAX_DIST = 32, 128
DN_HEADS, DN_HD, DN_CONV, DN_CHUNK = 4, 128, 4, 64
D_FF, FFN_CONV = 2816, 3
RMS_EPS, L2_EPS, NEG_INF = 1e-6, 1e-6, -1e30
AQ, AKV, DN = AQ_HEADS * A_HD, AKV_HEADS * A_HD, DN_HEADS * DN_HD
IN_DIM = AQ + 2 * AKV + 3 * DN + DN + 2 * DN_HEADS + 2 * D_MODEL
C_DQKV, C_Q, C_DZ, C_GA, C_GD, C_K, C_V, C_BD = 0, 1536, 2048, 2560, 3584, 4608, 4736, 4864
IN_PAD = 4992
LANES = 128
N_CHIPS = 4

ADAM_LR, ADAM_B1, ADAM_B2, ADAM_EPS, ADAM_WD, ADAM_STEP = 0.001, 0.9, 0.999, 1e-08, 0.01, 10


def _pick(n, cap):
    best = None
    for t in range(LANES, cap + 1, LANES):
        if n % t == 0:
            best = t
    return best if best is not None else n


def _vspec(shape, index_map):
    return pl.BlockSpec(shape, index_map)


MM_VMEM_BUDGET = 40 * 2 ** 20
GRID_STEP_S = 0.35e-6
HBM_BYTES_PER_S = 3.0e12
MXU_FLOPS_PER_S = 9.0e14
MXU_DIM = 256


def _mm_tiles(m, n, k, mode, in_bytes, out_bytes, split=1):
    best = None
    for tm in [t for t in range(LANES, m + 1, LANES) if m % t == 0]:
        for tn in [t for t in range(LANES, n // split + 1, LANES) if (n // split) % t == 0]:
            a, b, o = k * tm * in_bytes, k * tn * in_bytes, tm * tn * out_bytes
            if 2 * (a + b + o) + (a if mode == "tn" else 0) > MM_VMEM_BUDGET:
                continue
            hbm_s = (m * k * in_bytes + (m // tm) * n * k * in_bytes + m * n * out_bytes) / HBM_BYTES_PER_S
            mxu_s = 2 * m * n * k / (MXU_FLOPS_PER_S * min(1.0, tm / MXU_DIM) * min(1.0, tn / MXU_DIM))
            cost = (m // tm) * (n // tn) * GRID_STEP_S + max(hbm_s, mxu_s)
            if best is None or cost < best[0]:
                best = (cost, tm, tn)
    return best[1], best[2]


def _mm(a, b, mode, out_dtype, name, plug=None, split=1, b_kblock=0, add=None):
    if mode == "nn":
        (m, k), n = a.shape, b.shape[1]
        dims = (((1,), (0,)), ((), ()))
    elif mode == "nt":
        (m, k), n = a.shape, b.shape[0]
        dims = (((1,), (1,)), ((), ()))
    else:
        (k, m), n = a.shape, b.shape[1]
        dims = (((0,), (0,)), ((), ()))
    tm, tn = _mm_tiles(m, n, k, mode, a.dtype.itemsize, jnp.dtype(out_dtype).itemsize, split)
    if mode == "tn":
        a_spec = _vspec((k, tm), lambda i, j: (0, i))
    else:
        a_spec = _vspec((tm, k), lambda i, j: (i, 0))
    if mode == "nt":
        b_spec = _vspec((tn, k), lambda i, j: (j, b_kblock))
    else:
        b_spec = _vspec((k, tn), lambda i, j: (0, j))
    in_specs, args = [a_spec, b_spec], (a, b)
    if add is not None:
        in_specs, args = in_specs + [_vspec((tm, tn), lambda i, j: (i, j))], (a, b, add)

    def body(a_ref, b_ref, *rest):
        o_ref = rest[-1]
        acc = lax.dot_general(a_ref[...].astype(BF16), b_ref[...].astype(BF16), dims, preferred_element_type=F32)
        if add is not None:
            acc = acc + rest[0][...]
        o_ref[...] = acc.astype(out_dtype).reshape(o_ref.shape)

    grid = (m // tm, n // tn)
    if split == 1:
        out_spec, out_shape = _vspec((tm, tn), lambda i, j: (i, j)), (m, n)
    else:
        per = n // split // tn
        out_spec, out_shape = _vspec((1, tm, tn), lambda i, j: (j // per, i, j % per)), (split, m, n // split)
    (out,), extra = _plugged_call(body, plug, _grid_ends(grid), args, name=name, grid=grid, in_specs=in_specs,
                                  out_specs=[out_spec], out_shape=[jax.ShapeDtypeStruct(out_shape, out_dtype)])
    return out if plug is None else (out, extra)


def _rms(x, w):
    return (x * lax.rsqrt(jnp.mean(x * x, axis=-1, keepdims=True) + RMS_EPS)) * w


def _pre_f(x, w, sc, sh):
    return _rms(x, w) * (1.0 + sc) + sh


def _post_f(y, w, g):
    return g * _rms(y, w)


def _tok_grid(t, bsz, ts):
    nt = t // bsz // ts
    return nt, (bsz, nt)


def _pre_fwd(x, w, sc, sh, name, ts=512):
    t, d = x.shape
    bsz = sc.shape[0]
    nt, grid = _tok_grid(t, bsz, ts)
    row = _vspec((ts, d), lambda b, i: (b * nt + i, 0))
    vec = _vspec((1, d), lambda b, i: (0, 0))
    bvec = _vspec((1, 1, d), lambda b, i: (b, 0, 0))

    def body(x_ref, w_ref, sc_ref, sh_ref, u_ref):
        u_ref[...] = _pre_f(x_ref[...], w_ref[...], sc_ref[0], sh_ref[0]).astype(BF16)

    return pl.pallas_call(body, name=name, grid=grid, in_specs=[row, vec, bvec, bvec], out_specs=row,
                          out_shape=jax.ShapeDtypeStruct((t, d), BF16))(x, w, sc, sh)


def _pre_bwd(x, w, sc, sh, du, dres, name, ts=512):
    t, d = x.shape
    bsz = sc.shape[0]
    nt, grid = _tok_grid(t, bsz, ts)
    row = _vspec((ts, d), lambda b, i: (b * nt + i, 0))
    vec = _vspec((1, d), lambda b, i: (0, 0))
    bvec = _vspec((1, 1, d), lambda b, i: (b, 0, 0))

    def body(x_ref, w_ref, sc_ref, sh_ref, du_ref, dres_ref, dx_ref, dw_ref, dsc_ref, dsh_ref):
        b, i = pl.program_id(0), pl.program_id(1)
        _, vjp = jax.vjp(_pre_f, x_ref[...], w_ref[...], sc_ref[0], sh_ref[0])
        dx, dw, dsc, dsh = vjp(du_ref[...])
        dx_ref[...] = dres_ref[...] + dx

        @pl.when((b == 0) & (i == 0))
        def _():
            dw_ref[...] = jnp.zeros_like(dw_ref)

        @pl.when(i == 0)
        def _():
            dsc_ref[...] = jnp.zeros_like(dsc_ref)
            dsh_ref[...] = jnp.zeros_like(dsh_ref)

        dw_ref[...] += dw
        dsc_ref[0] += dsc
        dsh_ref[0] += dsh

    return pl.pallas_call(
        body, name=name, grid=grid, in_specs=[row, vec, bvec, bvec, row, row], out_specs=[row, vec, bvec, bvec],
        out_shape=[jax.ShapeDtypeStruct((t, d), F32), jax.ShapeDtypeStruct((1, d), F32),
                   jax.ShapeDtypeStruct((bsz, 1, d), F32), jax.ShapeDtypeStruct((bsz, 1, d), F32)],
    )(x, w, sc, sh, du, dres)


def _accumulate(ref, val, first):
    @pl.when(first)
    def _():
        ref[...] = jnp.zeros_like(ref)

    ref[...] += val.reshape(ref.shape)


def _post_pre_fwd(res, y, w_post, g, w_pre, sc, sh, name, ts=512):
    t, d = y.shape
    bsz = g.shape[0]
    nt, grid = _tok_grid(t, bsz, ts)
    row = _vspec((ts, d), lambda b, i: (b * nt + i, 0))
    vec = _vspec((1, d), lambda b, i: (0, 0))
    bvec = _vspec((1, 1, d), lambda b, i: (b, 0, 0))

    def body(res_ref, y_ref, wp_ref, g_ref, w_ref, sc_ref, sh_ref, h_ref, u_ref):
        h = res_ref[...] + _post_f(y_ref[...], wp_ref[...], g_ref[0])
        h_ref[...] = h
        u_ref[...] = _pre_f(h, w_ref[...], sc_ref[0], sh_ref[0]).astype(BF16)

    return pl.pallas_call(body, name=name, grid=grid, in_specs=[row, row, vec, bvec, vec, bvec, bvec], out_specs=[row, row],
                          out_shape=[jax.ShapeDtypeStruct((t, d), F32), jax.ShapeDtypeStruct((t, d), BF16)],
                          )(res, y, w_post, g, w_pre, sc, sh)


def _post_loss_bwd(res, y, w, g, tgt, name, ts=512):
    t, d = y.shape
    bsz = g.shape[0]
    nt, grid = _tok_grid(t, bsz, ts)
    row = _vspec((ts, d), lambda b, i: (b * nt + i, 0))
    vec = _vspec((1, d), lambda b, i: (0, 0))
    bvec = _vspec((1, 1, d), lambda b, i: (b, 0, 0))

    def body(res_ref, y_ref, w_ref, g_ref, tgt_ref, dh_ref, dy_ref, dw_ref, dg_ref, sq_ref):
        b, i = pl.program_id(0), pl.program_id(1)
        part, vjp = jax.vjp(_post_f, y_ref[...], w_ref[...], g_ref[0])
        e = res_ref[...] + part - tgt_ref[...]
        dh = e * (1.0 / d)
        dh_ref[...] = dh
        dy, dw, dg = vjp(dh)
        dy_ref[...] = dy.astype(BF16)
        _accumulate(dw_ref, dw, (b == 0) & (i == 0))
        _accumulate(dg_ref, dg, i == 0)
        _accumulate(sq_ref, jnp.sum(e * e, axis=0, keepdims=True) * (1.0 / d), (b == 0) & (i == 0))

    return pl.pallas_call(
        body, name=name, grid=grid, in_specs=[row, row, vec, bvec, row], out_specs=[row, row, vec, bvec, vec],
        out_shape=[jax.ShapeDtypeStruct((t, d), F32), jax.ShapeDtypeStruct((t, d), BF16), jax.ShapeDtypeStruct((1, d), F32),
                   jax.ShapeDtypeStruct((bsz, 1, d), F32), jax.ShapeDtypeStruct((1, d), F32)],
    )(res, y, w, g, tgt)


def _pre_post_bwd(x, w, sc, sh, du, dres, y, w_post, g, name, ts=512):
    t, d = x.shape
    bsz = sc.shape[0]
    nt, grid = _tok_grid(t, bsz, ts)
    row = _vspec((ts, d), lambda b, i: (b * nt + i, 0))
    vec = _vspec((1, d), lambda b, i: (0, 0))
    bvec = _vspec((1, 1, d), lambda b, i: (b, 0, 0))

    def body(x_ref, w_ref, sc_ref, sh_ref, du_ref, dres_ref, y_ref, wp_ref, g_ref,
             dx_ref, dy_ref, dw_ref, dsc_ref, dsh_ref, dwp_ref, dg_ref):
        b, i = pl.program_id(0), pl.program_id(1)
        _, vjp = jax.vjp(_pre_f, x_ref[...], w_ref[...], sc_ref[0], sh_ref[0])
        dx, dw, dsc, dsh = vjp(du_ref[...])
        dx = dres_ref[...] + dx
        dx_ref[...] = dx
        _, vjp_post = jax.vjp(_post_f, y_ref[...], wp_ref[...], g_ref[0])
        dy, dwp, dg = vjp_post(dx)
        dy_ref[...] = dy.astype(BF16)
        first = (b == 0) & (i == 0)
        _accumulate(dw_ref, dw, first)
        _accumulate(dwp_ref, dwp, first)
        _accumulate(dsc_ref, dsc, i == 0)
        _accumulate(dsh_ref, dsh, i == 0)
        _accumulate(dg_ref, dg, i == 0)

    v1, vb = jax.ShapeDtypeStruct((1, d), F32), jax.ShapeDtypeStruct((bsz, 1, d), F32)
    return pl.pallas_call(
        body, name=name, grid=grid, in_specs=[row, vec, bvec, bvec, row, row, row, vec, bvec],
        out_specs=[row, row, vec, bvec, bvec, vec, bvec],
        out_shape=[jax.ShapeDtypeStruct((t, d), F32), jax.ShapeDtypeStruct((t, d), BF16), v1, vb, vb, v1, vb],
    )(x, w, sc, sh, du, dres, y, w_post, g)


def _merge_f(ga, gd, ya, yd):
    return jax.nn.sigmoid(ga) * ya + jax.nn.sigmoid(gd) * yd


_MW = 512


def _merge_fwd(proj, ya, yd, ts=512):
    t, d = ya.shape
    blk = _vspec((ts, _MW), lambda i, j: (i, j))
    ga = _vspec((ts, _MW), lambda i, j: (i, C_GA // _MW + j))
    gd = _vspec((ts, _MW), lambda i, j: (i, C_GD // _MW + j))

    def body(ga_ref, gd_ref, ya_ref, yd_ref, o_ref):
        o_ref[...] = _merge_f(ga_ref[...], gd_ref[...], ya_ref[...].astype(F32), yd_ref[...].astype(F32)).astype(BF16)

    return pl.pallas_call(body, name="merge_fwd", grid=(t // ts, d // _MW), in_specs=[ga, gd, blk, blk], out_specs=blk,
                          out_shape=jax.ShapeDtypeStruct((t, d), BF16))(proj, proj, ya, yd)


def _merge_bwd(proj, ya, yd, dm, ts=512):
    t, d = ya.shape
    blk = _vspec((ts, _MW), lambda i, j: (i, j))
    ga = _vspec((ts, _MW), lambda i, j: (i, C_GA // _MW + j))
    gd = _vspec((ts, _MW), lambda i, j: (i, C_GD // _MW + j))

    def body(ga_ref, gd_ref, ya_ref, yd_ref, dm_ref, dga_ref, dgd_ref, dya_ref, dyd_ref):
        _, vjp = jax.vjp(_merge_f, ga_ref[...], gd_ref[...], ya_ref[...].astype(F32), yd_ref[...].astype(F32))
        dga, dgd, dya, dyd = vjp(dm_ref[...].astype(F32))
        dga_ref[...] = dga.astype(BF16)
        dgd_ref[...] = dgd.astype(BF16)
        dya_ref[...] = dya.astype(BF16)
        dyd_ref[...] = dyd.astype(BF16)

    o = jax.ShapeDtypeStruct((t, d), BF16)
    return pl.pallas_call(body, name="merge_bwd", grid=(t // ts, d // _MW), in_specs=[ga, gd, blk, blk, blk],
                          out_specs=[blk] * 4, out_shape=[o] * 4)(proj, proj, ya, yd, dm)


def _shift_down(x, s):
    if s == 0:
        return x
    r = lax.broadcasted_iota(jnp.int32, x.shape, 0)
    return jnp.where(r >= s, pltpu.roll(x, s, 0), 0.0)


def _shift_up(x, s):
    if s == 0:
        return x
    n = x.shape[0]
    r = lax.broadcasted_iota(jnp.int32, x.shape, 0)
    return jnp.where(r < n - s, pltpu.roll(x, n - s, 0), 0.0)


def _conv_fwd(x, w, k):
    out = None
    for j in range(k):
        term = w[j:j + 1, :] * _shift_down(x, k - 1 - j)
        out = term if out is None else out + term
    return out


def _conv_bwd(x, w, dc, k):
    dx = None
    dws = []
    for j in range(k):
        up = _shift_up(dc, k - 1 - j)
        term = w[j:j + 1, :] * up
        dx = term if dx is None else dx + term
        dws.append(jnp.sum(up * x, axis=0, keepdims=True))
    return dx, jnp.concatenate(dws, axis=0)


def _geglu_f(gate, val):
    return jax.nn.gelu(gate, approximate=True) * val


_FW = 256


def _ffn_act_fwd(up, conv_w, bsz):
    t = up.shape[0]
    s = t // bsz
    nj = D_FF // _FW
    xg = _vspec((s, _FW), lambda b, j: (b, j))
    xv = _vspec((s, _FW), lambda b, j: (b, nj + j))
    wg = _vspec((FFN_CONV, _FW), lambda b, j: (0, j))
    wv = _vspec((FFN_CONV, _FW), lambda b, j: (0, nj + j))

    def body(xg_ref, xv_ref, wg_ref, wv_ref, o_ref):
        gate = _conv_fwd(xg_ref[...], wg_ref[...], FFN_CONV)
        val = _conv_fwd(xv_ref[...], wv_ref[...], FFN_CONV)
        o_ref[...] = _geglu_f(gate, val).astype(BF16)

    return pl.pallas_call(body, name="ffn_act_fwd", grid=(bsz, nj), in_specs=[xg, xv, wg, wv],
                          out_specs=_vspec((s, _FW), lambda b, j: (b, j)),
                          out_shape=jax.ShapeDtypeStruct((t, D_FF), BF16))(up, up, conv_w, conv_w)


def _ffn_act_bwd(up, conv_w, dact, bsz, plug=None):
    t = up.shape[0]
    s = t // bsz
    nj = D_FF // _FW
    xg = _vspec((s, _FW), lambda j, b: (b, j))
    xv = _vspec((s, _FW), lambda j, b: (b, nj + j))
    wg = _vspec((FFN_CONV, _FW), lambda j, b: (0, j))
    wv = _vspec((FFN_CONV, _FW), lambda j, b: (0, nj + j))
    da = _vspec((s, _FW), lambda j, b: (b, j))
    dwo = _vspec((FFN_CONV, _FW), lambda j, b: (0, j))

    def body(xg_ref, xv_ref, wg_ref, wv_ref, da_ref, dxg_ref, dxv_ref, dwg_ref, dwv_ref):
        b = pl.program_id(1)
        xg_, xv_, wg_, wv_ = xg_ref[...], xv_ref[...], wg_ref[...], wv_ref[...]
        gate = _conv_fwd(xg_, wg_, FFN_CONV)
        val = _conv_fwd(xv_, wv_, FFN_CONV)
        _, vjp = jax.vjp(_geglu_f, gate, val)
        dgate, dval = vjp(da_ref[...])
        dxg, dwg = _conv_bwd(xg_, wg_, dgate, FFN_CONV)
        dxv, dwv = _conv_bwd(xv_, wv_, dval, FFN_CONV)
        dxg_ref[...] = dxg.astype(BF16)
        dxv_ref[...] = dxv.astype(BF16)

        @pl.when(b == 0)
        def _():
            dwg_ref[...] = jnp.zeros_like(dwg_ref)
            dwv_ref[...] = jnp.zeros_like(dwv_ref)

        dwg_ref[...] += dwg
        dwv_ref[...] += dwv

    outs, extra = _plugged_call(
        body, plug, _grid_ends((nj, bsz)), (up, up, conv_w, conv_w, dact), name="ffn_act_bwd", grid=(nj, bsz),
        in_specs=[xg, xv, wg, wv, da], out_specs=[da, da, dwo, dwo],
        out_shape=[jax.ShapeDtypeStruct((t, D_FF), BF16), jax.ShapeDtypeStruct((t, D_FF), BF16),
                   jax.ShapeDtypeStruct((FFN_CONV, D_FF), F32), jax.ShapeDtypeStruct((FFN_CONV, D_FF), F32)])
    return (*outs, extra)


def _bucket_table():
    qi = np.arange(WINDOW)[:, None]
    kj = np.arange(2 * WINDOW)[None, :]
    dist = WINDOW + qi - kj
    dc = np.maximum(dist, 0)
    max_exact = REL_BUCKETS // 2
    scaled = np.log(np.maximum(dc, 1).astype(np.float32) / np.float32(max_exact)) / np.float32(math.log(REL_MAX_DIST / max_exact))
    large = max_exact + (scaled.astype(np.float32) * np.float32(REL_BUCKETS - max_exact)).astype(np.int32)
    large = np.minimum(large, REL_BUCKETS - 1)
    bucket = np.where(dc < max_exact, dc, large).astype(np.int32)
    in_band = ((dist >= 0) & (dist < WINDOW)).astype(np.int32)
    return bucket, in_band


def _bias_build(rel_bias):
    bucket, _ = _bucket_table()

    def body(rb_ref, idx_ref, o_ref):
        h = pl.program_id(0)
        idx = idx_ref[...]
        acc = jnp.zeros(idx.shape, F32)
        for r in range(REL_BUCKETS):
            acc = jnp.where(idx == r, rb_ref[r, h], acc)
        o_ref[0] = acc

    return pl.pallas_call(
        body, name="bias_build", grid=(AQ_HEADS,),
        in_specs=[pl.BlockSpec(memory_space=pltpu.SMEM), _vspec((WINDOW, 2 * WINDOW), lambda h: (0, 0))],
        out_specs=_vspec((1, WINDOW, 2 * WINDOW), lambda h: (h, 0, 0)),
        out_shape=jax.ShapeDtypeStruct((AQ_HEADS, WINDOW, 2 * WINDOW), F32),
    )(rel_bias, jnp.asarray(bucket))


def _bias_grad(dbias):
    bucket, _ = _bucket_table()

    def body(db_ref, idx_ref, o_ref):
        idx = idx_ref[...]
        db = db_ref[0]
        lane = lax.broadcasted_iota(jnp.int32, (1, LANES), 1)
        acc = jnp.zeros((1, LANES), F32)
        for r in range(REL_BUCKETS):
            s = jnp.sum(jnp.sum(jnp.where(idx == r, db, 0.0), axis=1, keepdims=True), axis=0, keepdims=True)
            acc = jnp.where(lane == r, s, acc)
        o_ref[0] = acc

    return pl.pallas_call(
        body, name="bias_grad", grid=(AQ_HEADS,),
        in_specs=[_vspec((1, WINDOW, 2 * WINDOW), lambda h: (h, 0, 0)), _vspec((WINDOW, 2 * WINDOW), lambda h: (0, 0))],
        out_specs=_vspec((1, 1, LANES), lambda h: (h, 0, 0)),
        out_shape=jax.ShapeDtypeStruct((AQ_HEADS, 1, LANES), F32),
    )(dbias, jnp.asarray(bucket))


def _attn_mask(n):
    qi = lax.broadcasted_iota(jnp.int32, (WINDOW, 2 * WINDOW), 0)
    kj = lax.broadcasted_iota(jnp.int32, (WINDOW, 2 * WINDOW), 1)
    dist = WINDOW + qi - kj
    band = (dist >= 0) & (dist < WINDOW)
    return band & ((kj >= WINDOW) | (n > 0))


def _attn_probs(qk, bias, sink, mask):
    s = jnp.where(mask, qk * (A_HD ** -0.5) + bias, NEG_INF)
    m = jnp.maximum(jnp.max(s, axis=-1, keepdims=True), sink)
    p = jnp.exp(s - m)
    es = jnp.exp(sink - m)
    inv = 1.0 / (jnp.sum(p, axis=-1, keepdims=True) + es)
    return p * inv, es * inv


def _attn_fwd(proj, bias, sinks, bsz):
    t = proj.shape[0]
    s = t // bsz
    nb = s // WINDOW
    grp = AQ_HEADS // AKV_HEADS

    def body(q_ref, k_ref, v_ref, bias_ref, sink_ref, y_ref, kp_ref, vp_ref):
        kp_ref[0:WINDOW, :] = jnp.zeros((WINDOW, LANES), BF16)
        vp_ref[0:WINDOW, :] = jnp.zeros((WINDOW, LANES), BF16)
        kp_ref[WINDOW:, :] = k_ref[...].astype(BF16)
        vp_ref[WINDOW:, :] = v_ref[...].astype(BF16)

        def blk(n, carry):
            r0 = pl.multiple_of(n * WINDOW, WINDOW)
            mask = _attn_mask(n)
            kband = kp_ref[pl.ds(r0, 2 * WINDOW), :]
            vband = vp_ref[pl.ds(r0, 2 * WINDOW), :]
            qb = q_ref[pl.ds(r0, WINDOW), :].astype(BF16)
            heads = range(AQ_HEADS)
            hsl = lambda h: slice(h * A_HD, (h + 1) * A_HD)
            kbs = [kband[:, hsl(kv)] for kv in range(AKV_HEADS)]
            vbs = [vband[:, hsl(kv)] for kv in range(AKV_HEADS)]
            qks = [lax.dot_general(qb[:, hsl(h)], kbs[h // grp], _NT, preferred_element_type=F32) for h in heads]
            probs = [_attn_probs(qks[h], bias_ref[h], sink_ref[0, h], mask)[0] for h in heads]
            outs = [jnp.dot(probs[h].astype(BF16), vbs[h // grp], preferred_element_type=F32) for h in heads]
            y_ref[pl.ds(r0, WINDOW), :] = jnp.concatenate(outs, axis=1).astype(BF16)
            return carry

        lax.fori_loop(0, nb, blk, 0)

    return pl.pallas_call(
        body, name="attn_fwd", grid=(bsz,),
        in_specs=[_vspec((s, AQ), lambda b: (b, C_Q // AQ)), _vspec((s, AKV), lambda b: (b, C_K // AKV)),
                  _vspec((s, AKV), lambda b: (b, C_V // AKV)),
                  _vspec((AQ_HEADS, WINDOW, 2 * WINDOW), lambda b: (0, 0, 0)), pl.BlockSpec(memory_space=pltpu.SMEM)],
        out_specs=_vspec((s, AQ), lambda b: (b, 0)), out_shape=jax.ShapeDtypeStruct((t, AQ), BF16),
        scratch_shapes=[pltpu.VMEM((s + WINDOW, LANES), BF16), pltpu.VMEM((s + WINDOW, LANES), BF16)],
    )(proj, proj, proj, bias, sinks)


def _attn_bwd(proj, bias, sinks, dy, bsz, plug=None):
    t = proj.shape[0]
    s = t // bsz
    nb = s // WINDOW
    grp = AQ_HEADS // AKV_HEADS
    scale = A_HD ** -0.5

    def body(q_ref, k_ref, v_ref, bias_ref, sink_ref, dy_ref, dq_ref, dk_ref, dv_ref, dbias_ref, dsink_ref,
             kp_ref, vp_ref, dkp_ref, dvp_ref):
        b = pl.program_id(0)
        kp_ref[0:WINDOW, :] = jnp.zeros((WINDOW, LANES), BF16)
        vp_ref[0:WINDOW, :] = jnp.zeros((WINDOW, LANES), BF16)
        kp_ref[WINDOW:, :] = k_ref[...].astype(BF16)
        vp_ref[WINDOW:, :] = v_ref[...].astype(BF16)
        dkp_ref[...] = jnp.zeros_like(dkp_ref)
        dvp_ref[...] = jnp.zeros_like(dvp_ref)

        @pl.when(b == 0)
        def _():
            dbias_ref[...] = jnp.zeros_like(dbias_ref)
            dsink_ref[...] = jnp.zeros_like(dsink_ref)

        def blk(n, carry):
            r0 = pl.multiple_of(n * WINDOW, WINDOW)
            mask = _attn_mask(n)
            kband = kp_ref[pl.ds(r0, 2 * WINDOW), :]
            vband = vp_ref[pl.ds(r0, 2 * WINDOW), :]
            qb = q_ref[pl.ds(r0, WINDOW), :].astype(BF16)
            dyb = dy_ref[pl.ds(r0, WINDOW), :].astype(BF16)
            heads = range(AQ_HEADS)
            hsl = lambda h: slice(h * A_HD, (h + 1) * A_HD)
            kbs = [kband[:, hsl(kv)] for kv in range(AKV_HEADS)]
            vbs = [vband[:, hsl(kv)] for kv in range(AKV_HEADS)]
            qhs = [qb[:, hsl(h)] for h in heads]
            dyhs = [dyb[:, hsl(h)] for h in heads]
            qks = [lax.dot_general(qhs[h], kbs[h // grp], _NT, preferred_element_type=F32) for h in heads]
            dprobs = [lax.dot_general(dyhs[h], vbs[h // grp], _NT, preferred_element_type=F32) for h in heads]
            pbs, dsbs = [], []
            for h in heads:
                probs, psink = _attn_probs(qks[h], bias_ref[h], sink_ref[0, h], mask)
                rowdot = jnp.sum(probs * dprobs[h], axis=-1, keepdims=True)
                ds = probs * (dprobs[h] - rowdot)
                dbias_ref[h] += ds
                dsink_ref[h] += jnp.sum(-psink * rowdot, axis=0, keepdims=True) + jnp.zeros((1, LANES), F32)
                pbs.append(probs.astype(BF16))
                dsbs.append(ds.astype(BF16))
            dvhs = [lax.dot_general(pbs[h], dyhs[h], _TN, preferred_element_type=F32) for h in heads]
            dqs = [jnp.dot(dsbs[h], kbs[h // grp], preferred_element_type=F32) * scale for h in heads]
            dkhs = [lax.dot_general(dsbs[h], qhs[h], _TN, preferred_element_type=F32) * scale for h in heads]
            dks = [sum(dkhs[kv * grp + 1:(kv + 1) * grp], dkhs[kv * grp]) for kv in range(AKV_HEADS)]
            dvs = [sum(dvhs[kv * grp + 1:(kv + 1) * grp], dvhs[kv * grp]) for kv in range(AKV_HEADS)]
            dq_ref[pl.ds(r0, WINDOW), :] = jnp.concatenate(dqs, axis=1).astype(BF16)
            dkp_ref[pl.ds(r0, 2 * WINDOW), :] += jnp.concatenate(dks, axis=1)
            dvp_ref[pl.ds(r0, 2 * WINDOW), :] += jnp.concatenate(dvs, axis=1)
            return carry

        lax.fori_loop(0, nb, blk, 0)
        dk_ref[...] = dkp_ref[WINDOW:, :].astype(BF16)
        dv_ref[...] = dvp_ref[WINDOW:, :].astype(BF16)

    kvs = jax.ShapeDtypeStruct((t, AKV), BF16)
    outs, extra = _plugged_call(
        body, plug, _grid_ends((bsz,)), (proj, proj, proj, bias, sinks, dy), name="attn_bwd", grid=(bsz,),
        in_specs=[_vspec((s, AQ), lambda b: (b, C_Q // AQ)), _vspec((s, AKV), lambda b: (b, C_K // AKV)),
                  _vspec((s, AKV), lambda b: (b, C_V // AKV)),
                  _vspec((AQ_HEADS, WINDOW, 2 * WINDOW), lambda b: (0, 0, 0)), pl.BlockSpec(memory_space=pltpu.SMEM),
                  _vspec((s, AQ), lambda b: (b, 0))],
        out_specs=[_vspec((s, AQ), lambda b: (b, 0)), _vspec((s, AKV), lambda b: (b, 0)), _vspec((s, AKV), lambda b: (b, 0)),
                   _vspec((AQ_HEADS, WINDOW, 2 * WINDOW), lambda b: (0, 0, 0)), _vspec((AQ_HEADS, 1, LANES), lambda b: (0, 0, 0))],
        out_shape=[jax.ShapeDtypeStruct((t, AQ), BF16), kvs, kvs,
                   jax.ShapeDtypeStruct((AQ_HEADS, WINDOW, 2 * WINDOW), F32), jax.ShapeDtypeStruct((AQ_HEADS, 1, LANES), F32)],
        scratch_shapes=[pltpu.VMEM((s + WINDOW, LANES), BF16), pltpu.VMEM((s + WINDOW, LANES), BF16),
                        pltpu.VMEM((s + WINDOW, LANES), F32), pltpu.VMEM((s + WINDOW, LANES), F32)])
    return (*outs, extra)


def _dn_act_f(c, is_qk):
    a = jax.nn.silu(c)
    outs = []
    for h in range(DN_HEADS):
        ah = a[:, h * DN_HD:(h + 1) * DN_HD]
        nh = ah * lax.rsqrt(jnp.sum(ah * ah, axis=-1, keepdims=True) + L2_EPS)
        outs.append(jnp.where(is_qk, nh, ah))
    return jnp.concatenate(outs, axis=1)


def _dn_prep_fwd(proj, conv_w, bsz):
    t = proj.shape[0]
    s = t // bsz
    blk = _vspec((s, DN), lambda b, j: (b, j))
    wsp = _vspec((DN_CONV, DN), lambda b, j: (0, j))

    def body(x_ref, w_ref, o_ref):
        j = pl.program_id(1)
        o_ref[...] = _dn_act_f(_conv_fwd(x_ref[...], w_ref[...], DN_CONV), j < 2)

    return pl.pallas_call(body, name="dn_prep_fwd", grid=(bsz, 3), in_specs=[blk, wsp], out_specs=blk,
                          out_shape=jax.ShapeDtypeStruct((t, 3 * DN), F32))(proj, conv_w)


def _dn_prep_bwd(proj, conv_w, dqkvn, bsz):
    t = proj.shape[0]
    s = t // bsz
    blk = _vspec((s, DN), lambda j, b: (b, j))
    wsp = _vspec((DN_CONV, DN), lambda j, b: (0, j))

    def body(x_ref, w_ref, d_ref, dx_ref, dw_ref):
        j, b = pl.program_id(0), pl.program_id(1)
        x, w = x_ref[...], w_ref[...]
        c = _conv_fwd(x, w, DN_CONV)
        _, vjp = jax.vjp(lambda cc: _dn_act_f(cc, j < 2), c)
        (dc,) = vjp(d_ref[0])
        dx, dw = _conv_bwd(x, w, dc, DN_CONV)
        dx_ref[...] = dx.astype(BF16)

        @pl.when(b == 0)
        def _():
            dw_ref[...] = jnp.zeros_like(dw_ref)

        dw_ref[...] += dw

    return pl.pallas_call(
        body, name="dn_prep_bwd", grid=(3, bsz),
        in_specs=[blk, wsp, _vspec((1, s, DN), lambda j, b: (j, b, 0))], out_specs=[blk, wsp],
        out_shape=[jax.ShapeDtypeStruct((t, 3 * DN), BF16), jax.ShapeDtypeStruct((DN_CONV, 3 * DN), F32)],
    )(proj, conv_w, dqkvn)


def _bg_f(x, alog, dt):
    lane = lax.broadcasted_iota(jnp.int32, x.shape, 1)
    beta = jax.nn.sigmoid(x)
    g = -jnp.exp(alog) * jax.nn.softplus(x + dt)
    return jnp.where(lane < DN_HEADS, beta, jnp.where(lane < 2 * DN_HEADS, g, 0.0))


def _bg_fwd(proj, alog, dt, bsz):
    t = proj.shape[0]
    s = t // bsz
    vec = _vspec((1, LANES), lambda b: (0, 0))

    def body(x_ref, a_ref, d_ref, o_ref):
        o_ref[...] = _bg_f(x_ref[...], a_ref[...], d_ref[...])

    return pl.pallas_call(body, name="bg_fwd", grid=(bsz,), in_specs=[_vspec((s, LANES), lambda b: (b, C_BD // LANES)), vec, vec],
                          out_specs=_vspec((s, LANES), lambda b: (b, 0)), out_shape=jax.ShapeDtypeStruct((t, LANES), F32))(proj, alog, dt)


def _bg_bwd(proj, alog, dt, dbg4, bsz):
    t = proj.shape[0]
    s = t // bsz
    vec = _vspec((1, LANES), lambda b: (0, 0))

    def body(x_ref, a_ref, d_ref, g4_ref, dx_ref, da_ref, dd_ref):
        b = pl.program_id(0)
        lane = lax.broadcasted_iota(jnp.int32, (s, LANES), 1)
        dbg = jnp.zeros((s, LANES), F32)
        for h in range(DN_HEADS):
            gh = g4_ref[:, h * DN_HD:(h + 1) * DN_HD]
            dbg = jnp.where(lane == h, gh[:, 0:1], dbg)
            dbg = jnp.where(lane == DN_HEADS + h, gh[:, 1:2], dbg)
        _, vjp = jax.vjp(_bg_f, x_ref[...], a_ref[...], d_ref[...])
        dx, da, dd = vjp(dbg)
        dx_ref[...] = dx.astype(BF16)

        @pl.when(b == 0)
        def _():
            da_ref[...] = jnp.zeros_like(da_ref)
            dd_ref[...] = jnp.zeros_like(dd_ref)

        da_ref[...] += da
        dd_ref[...] += dd

    return pl.pallas_call(
        body, name="bg_bwd", grid=(bsz,),
        in_specs=[_vspec((s, LANES), lambda b: (b, C_BD // LANES)), vec, vec, _vspec((s, DN), lambda b: (b, 0))],
        out_specs=[_vspec((s, LANES), lambda b: (b, 0)), vec, vec],
        out_shape=[jax.ShapeDtypeStruct((t, LANES), BF16), jax.ShapeDtypeStruct((1, LANES), F32), jax.ShapeDtypeStruct((1, LANES), F32)],
    )(proj, alog, dt, dbg4)


def _dn_out_f(o, z, w):
    outs = []
    for h in range(DN_HEADS):
        sl = slice(h * DN_HD, (h + 1) * DN_HD)
        outs.append(_rms(o[:, sl], w) * jax.nn.silu(z[:, sl]))
    return jnp.concatenate(outs, axis=1)


def _dn_out_fwd(o, proj, w, ts=512):
    t = o.shape[0]
    blk = _vspec((ts, DN), lambda i: (i, 0))
    zsp = _vspec((ts, DN), lambda i: (i, C_DZ // DN))
    vec = _vspec((1, DN_HD), lambda i: (0, 0))

    def body(o_ref, z_ref, w_ref, y_ref):
        y_ref[...] = _dn_out_f(o_ref[...], z_ref[...], w_ref[...]).astype(BF16)

    return pl.pallas_call(body, name="dn_out_fwd", grid=(t // ts,), in_specs=[blk, zsp, vec], out_specs=blk,
                          out_shape=jax.ShapeDtypeStruct((t, DN), BF16))(o, proj, w)


def _dn_out_bwd(o, proj, w, dy, ts=512):
    t = o.shape[0]
    blk = _vspec((ts, DN), lambda i: (i, 0))
    zsp = _vspec((ts, DN), lambda i: (i, C_DZ // DN))
    vec = _vspec((1, DN_HD), lambda i: (0, 0))

    def body(o_ref, z_ref, w_ref, dy_ref, do_ref, dz_ref, dw_ref):
        i = pl.program_id(0)
        _, vjp = jax.vjp(_dn_out_f, o_ref[...], z_ref[...], w_ref[...])
        do, dz, dw = vjp(dy_ref[...])
        do_ref[...] = do
        dz_ref[...] = dz.astype(BF16)

        @pl.when(i == 0)
        def _():
            dw_ref[...] = jnp.zeros_like(dw_ref)

        dw_ref[...] += dw

    return pl.pallas_call(
        body, name="dn_out_bwd", grid=(t // ts,), in_specs=[blk, zsp, vec, blk], out_specs=[blk, blk, vec],
        out_shape=[jax.ShapeDtypeStruct((t, DN), F32), jax.ShapeDtypeStruct((t, DN), BF16), jax.ShapeDtypeStruct((1, DN_HD), F32)],
    )(o, proj, w, dy)


_C = DN_CHUNK


def _dot(a, b, dims):
    return lax.dot_general(a.astype(BF16), b.astype(BF16), dims, preferred_element_type=F32)


def _split(a):
    hi = a.astype(BF16)
    return hi, (a - hi.astype(F32)).astype(BF16)


def _dot3(a, b, dims):
    (ah, al), (bh, bl) = (a if isinstance(a, tuple) else _split(a)), (b if isinstance(b, tuple) else _split(b))
    mm = lambda x, y: lax.dot_general(x, y, dims, preferred_element_type=F32)
    return mm(ah, bh) + (mm(ah, bl) + mm(al, bh))


_NN = (((1,), (0,)), ((), ()))
_NT = (((1,), (1,)), ((), ()))
_TN = (((0,), (0,)), ((), ()))


_SUB = 8


def _tri_inverses(ls, lts):
    ri8 = lax.broadcasted_iota(jnp.int32, (_SUB, _C), 0)
    ci8 = lax.broadcasted_iota(jnp.int32, (_SUB, _C), 1)
    nblk = _C // _SUB
    ts = []
    for lt in lts:
        blocks = [jnp.where(ci8 == ri8 + _SUB * b, 1.0, 0.0).astype(F32) for b in range(nblk)]
        for r in range(1, _SUB):
            for b in range(nblk):
                coef = lt[_SUB * b:_SUB * (b + 1), _SUB * b + r:_SUB * b + r + 1]
                row = jnp.sum(coef * blocks[b], axis=0, keepdims=True)
                blocks[b] = jnp.where(ri8 == r, blocks[b] - row, blocks[b])
        ts.append(jnp.concatenate(blocks, axis=0))
    ri = lax.broadcasted_iota(jnp.int32, (_C, _C), 0)
    ci = lax.broadcasted_iota(jnp.int32, (_C, _C), 1)
    s = _SUB
    while s < _C:
        shift = s.bit_length()
        quad = ((ri >> shift) == (ci >> shift)) & ((ri & s) != 0) & ((ci & s) == 0)
        offs = [jnp.where(quad, l, 0.0) for l in ls]
        tsp = [_split(t) for t in ts]
        left = [_dot3(tp, off, _NN) for tp, off in zip(tsp, offs)]
        ts = [t - _dot3(lo, tp, _NN) for t, lo, tp in zip(ts, left, tsp)]
        s *= 2
    return ts


_SEG = 512
_HEADS = tuple(range(DN_HEADS))


def _hsl(hh):
    return slice(hh * DN_HD, (hh + 1) * DN_HD)


def _chunk_specs(bsz, nseg, reverse):
    seg = (lambda i: nseg - 1 - i) if reverse else (lambda i: i)
    ncs = _SEG // _C
    col = lambda off: _vspec((bsz, _SEG, DN), lambda i: (0, seg(i), off))
    return (col, _vspec((bsz, _SEG, LANES), lambda i: (0, seg(i), 0)),
            _vspec((bsz, DN_HEADS, ncs, _C), lambda i: (0, 0, seg(i), 0)),
            _vspec((bsz, DN_HEADS, ncs, DN_HD, DN_HD), lambda i: (0, 0, seg(i), 0, 0)),
            _vspec((bsz, DN_HEADS, ncs, _C, _C), lambda i: (0, 0, seg(i), 0, 0)))


def _chunk_pre(q_ref, k_ref, v_ref, bg_ref, gr_ref, c, bb, hh):
    r0 = pl.multiple_of(c * _C, _C)
    ri = lax.broadcasted_iota(jnp.int32, (_C, _C), 0)
    ci = lax.broadcasted_iota(jnp.int32, (_C, _C), 1)
    q = q_ref[bb, pl.ds(r0, _C), _hsl(hh)] * (DN_HD ** -0.5)
    k = k_ref[bb, pl.ds(r0, _C), _hsl(hh)]
    v = v_ref[bb, pl.ds(r0, _C), _hsl(hh)]
    bgc = bg_ref[bb, pl.ds(r0, _C), :]
    beta = bgc[:, hh:hh + 1]
    g_col = bgc[:, DN_HEADS + hh:DN_HEADS + hh + 1]
    g_row = gr_ref[bb, hh, pl.ds(c, 1), :]
    gc_col = jnp.sum(jnp.where(ri >= ci, g_row, 0.0), axis=1, keepdims=True)
    gc_row = jnp.sum(jnp.where(ri <= ci, g_col, 0.0), axis=0, keepdims=True)
    gc_last = jnp.sum(g_col, axis=0, keepdims=True)
    diff = gc_col - gc_row
    decay = jnp.where(ri >= ci, jnp.exp(jnp.where(ri >= ci, diff, 0.0)), 0.0)
    diff_t = gc_row - gc_col
    decay_t = jnp.where(ri <= ci, jnp.exp(jnp.where(ri <= ci, diff_t, 0.0)), 0.0)
    eg = jnp.exp(gc_col)
    et = jnp.exp(gc_last - gc_col)
    gl = jnp.exp(gc_last)
    kb = k * beta
    vb = v * beta
    return dict(r0=r0, bb=bb, hh=hh, q=q, k=k, v=v, beta=beta, decay=decay, decay_t=decay_t, eg=eg, et=et, gl=gl, kb=kb, vb=vb,
                ri=ri, ci=ci)


def _chunk_solve(ms, tms=None):
    for m in ms:
        m["kk_t"] = _dot(m["k"], m["kb"], _NT)
        m["qk"] = _dot(m["q"], m["k"], _NT)
        m["kk"] = _dot(m["kb"], m["k"], _NT)
        if tms is not None:
            m["qk_t"] = _dot(m["k"], m["q"], _NT)
    if tms is None:
        tms = _tri_inverses([jnp.where(m["ri"] > m["ci"], m["kk"] * m["decay"], 0.0) for m in ms],
                            [jnp.where(m["ri"] < m["ci"], m["kk_t"] * m["decay_t"], 0.0) for m in ms])
    for m, tm in zip(ms, tms):
        m["tm_f32"] = tm
    for m in ms:
        rhs = jnp.concatenate([m["vb"], m["kb"] * m["eg"]], axis=1)
        m["tm"] = _split(m["tm_f32"])
        m["sol"] = _dot3(m["tm"], rhs, _NN)
        m["intra"] = jnp.where(m["ri"] >= m["ci"], m["qk"] * m["decay"], 0.0)


def _dn_chunk_fwd(qkvn, bg, g_rows, bsz, plug=None):
    t = qkvn.shape[0]
    s = t // bsz
    nc, nseg = s // _C, s // _SEG
    pairs = [(bb, hh) for bb in range(bsz) for hh in _HEADS]

    def body(q_ref, k_ref, v_ref, bg_ref, gr_ref, o_ref, st_ref, tm_ref, s_ref):
        @pl.when(pl.program_id(0) == 0)
        def _():
            s_ref[...] = jnp.zeros_like(s_ref)

        def chunk(c, carry):
            ms = [_chunk_pre(q_ref, k_ref, v_ref, bg_ref, gr_ref, c, bb, hh) for bb, hh in pairs]
            _chunk_solve(ms)
            sts = [s_ref[i] for i in range(len(pairs))]
            for m, st in zip(ms, sts):
                st_ref[m["bb"], m["hh"], c] = st
                tm_ref[m["bb"], m["hh"], c] = m["tm_f32"]
            ws = [_dot(m["sol"][:, DN_HD:], st, _NN) for m, st in zip(ms, sts)]
            qs = [_dot(m["q"] * m["eg"], st, _NN) for m, st in zip(ms, sts)]
            v_new = [m["sol"][:, :DN_HD] - a for m, a in zip(ms, ws)]
            iv = [_dot(m["intra"], vn, _NN) for m, vn in zip(ms, v_new)]
            upd = [_dot(m["k"] * m["et"], vn, _TN) for m, vn in zip(ms, v_new)]
            for i, (bb, hh) in enumerate(pairs):
                s_ref[i] = sts[i] * ms[i]["gl"] + upd[i]
                o_ref[bb, pl.ds(ms[i]["r0"], _C), _hsl(hh)] = qs[i] + iv[i]
            return carry

        lax.fori_loop(0, _SEG // _C, chunk, 0)

    col, bgs, grs, sts_spec, tms_spec = _chunk_specs(bsz, nseg, False)
    q3, bg3 = qkvn.reshape(bsz, s, 3 * DN), bg.reshape(bsz, s, LANES)
    (o, states, tms), extra = _plugged_call(
        body, plug, _grid_ends((nseg,)), (q3, q3, q3, bg3, g_rows), name="dn_chunk_fwd", grid=(nseg,),
        in_specs=[col(0), col(1), col(2), bgs, grs], out_specs=[col(0), sts_spec, tms_spec],
        out_shape=[jax.ShapeDtypeStruct((bsz, s, DN), F32), jax.ShapeDtypeStruct((bsz, DN_HEADS, nc, DN_HD, DN_HD), F32),
                   jax.ShapeDtypeStruct((bsz, DN_HEADS, nc, _C, _C), F32)],
        scratch_shapes=[pltpu.VMEM((bsz * DN_HEADS, DN_HD, DN_HD), F32)])
    return o.reshape(t, DN), (states, tms), extra


def _dn_chunk_bwd(qkvn, bg, g_rows, states, do, bsz, plug=None):
    t = qkvn.shape[0]
    s = t // bsz
    nc, nseg = s // _C, s // _SEG
    pairs = [(bb, hh) for bb in range(bsz) for hh in _HEADS]

    def body(q_ref, k_ref, v_ref, bg_ref, gr_ref, st_ref, tm_ref, do_ref, dqkv_ref, dbg_ref, ds_ref):
        @pl.when(pl.program_id(0) == 0)
        def _():
            ds_ref[...] = jnp.zeros_like(ds_ref)

        def chunk(cc, carry):
            c = _SEG // _C - 1 - cc
            ms = [_chunk_pre(q_ref, k_ref, v_ref, bg_ref, gr_ref, c, bb, hh) for bb, hh in pairs]
            _chunk_solve(ms, [tm_ref[bb, hh, c] for bb, hh in pairs])
            ri, ci = ms[0]["ri"], ms[0]["ci"]
            for i, m in enumerate(ms):
                m["st"] = st_ref[m["bb"], m["hh"], c]
                m["ds_out"] = ds_ref[i]
                m["do"] = do_ref[m["bb"], pl.ds(m["r0"], _C), _hsl(m["hh"])]
                m["w"] = m["sol"][:, DN_HD:]
            for m in ms:
                m["v_new"] = m["sol"][:, :DN_HD] - _dot(m["w"], m["st"], _NN)
            for m in ms:
                m["q_dec"], m["k_tail"] = m["q"] * m["eg"], m["k"] * m["et"]
                m["dk_tail"] = _dot(m["v_new"], m["ds_out"], _NT)
                m["dv_new"] = _dot(m["k_tail"], m["ds_out"], _NN) + _dot(m["intra"], m["do"], _TN)
                m["dq_dec"] = _dot(m["do"], m["st"], _NT)
                m["ds_in"] = m["ds_out"] * m["gl"] + _dot(m["q_dec"], m["do"], _TN)
                m["dintra"] = jnp.where(ri >= ci, _dot(m["do"], m["v_new"], _NT), 0.0)
                m["dintra_t"] = jnp.where(ri <= ci, _dot(m["v_new"], m["do"], _NT), 0.0)
            for m in ms:
                m["dw"] = -_dot(m["dv_new"], m["st"], _NT)
                m["ds_in"] = m["ds_in"] - _dot(m["w"], m["dv_new"], _TN)
            for m in ms:
                dsol = jnp.concatenate([m["dv_new"], m["dw"]], axis=1)
                m["drhs"] = _dot3(m["tm"], dsol, _TN)
            for m in ms:
                m["dl"] = jnp.where(ri > ci, -_dot(m["drhs"], m["sol"], _NT), 0.0)
                m["dl_t"] = jnp.where(ri < ci, -_dot(m["sol"], m["drhs"], _NT), 0.0)
            for m in ms:
                m["dkb2"] = _dot(m["dl"] * m["decay"], m["k"], _NN)
                m["dk"] = _dot(m["dl_t"] * m["decay_t"], m["kb"], _NN) + _dot(m["dintra_t"] * m["decay_t"], m["q"], _NN)
                m["dq"] = _dot(m["dintra"] * m["decay"], m["k"], _NN)
            for m in ms:
                _chunk_bwd_finish(m)
            for m in ms:
                ones_ge = jnp.where(ri <= ci, 1.0, 0.0).astype(BF16)
                gh, gl_ = _split(m["dgc"] + jnp.zeros((_C, LANES), F32))
                m["dg_b"] = jnp.dot(ones_ge, gh, preferred_element_type=F32) + jnp.dot(ones_ge, gl_, preferred_element_type=F32)
            lane = lax.broadcasted_iota(jnp.int32, (_C, LANES), 1)
            for i, m in enumerate(ms):
                bb, hh, rows = m["bb"], m["hh"], pl.ds(m["r0"], _C)
                dqkv_ref[0, bb, rows, _hsl(hh)] = m["dq"] * (DN_HD ** -0.5)
                dqkv_ref[1, bb, rows, _hsl(hh)] = m["dk"]
                dqkv_ref[2, bb, rows, _hsl(hh)] = m["dv"]
                dbg_ref[bb, rows, _hsl(hh)] = jnp.where(lane == 0, m["dbeta"], jnp.where(lane == 1, m["dg_b"], 0.0))
                ds_ref[i] = m["ds_in"]
            return carry

        lax.fori_loop(0, _SEG // _C, chunk, 0)

    col, bgs, grs, sts_spec, tms_spec = _chunk_specs(bsz, nseg, True)
    q3, bg3, do3 = qkvn.reshape(bsz, s, 3 * DN), bg.reshape(bsz, s, LANES), do.reshape(bsz, s, DN)
    (dqkv, dbg), extra = _plugged_call(
        body, plug, _grid_ends((nseg,)), (q3, q3, q3, bg3, g_rows, *states, do3), name="dn_chunk_bwd", grid=(nseg,),
        in_specs=[col(0), col(1), col(2), bgs, grs, sts_spec, tms_spec, col(0)],
        out_specs=[_vspec((3, bsz, _SEG, DN), lambda i: (0, 0, nseg - 1 - i, 0)), col(0)],
        out_shape=[jax.ShapeDtypeStruct((3, bsz, s, DN), F32), jax.ShapeDtypeStruct((bsz, s, DN), F32)],
        scratch_shapes=[pltpu.VMEM((bsz * DN_HEADS, DN_HD, DN_HD), F32)])
    return dqkv.reshape(3, t, DN), dbg.reshape(t, DN), extra


def _chunk_bwd_finish(m):
    q, k, v, beta, decay, decay_t = m["q"], m["k"], m["v"], m["beta"], m["decay"], m["decay_t"]
    eg, et, gl, kb, dl, dl_t, dintra, dintra_t = m["eg"], m["et"], m["gl"], m["kb"], m["dl"], m["dl_t"], m["dintra"], m["dintra_t"]
    dq_dec, dk_tail, dq, dk = m["dq_dec"], m["dk_tail"], m["dq"], m["dk"]
    dgl = jnp.sum(jnp.sum(m["ds_out"] * m["st"], axis=1, keepdims=True), axis=0, keepdims=True)
    dvb, dkbeg = m["drhs"][:, :DN_HD], m["drhs"][:, DN_HD:]
    dkb = dkbeg * eg + m["dkb2"]
    deg = jnp.sum(dkbeg * kb, axis=1, keepdims=True)
    em = (dl * m["kk"] + dintra * m["qk"]) * decay
    em_t = (dl_t * m["kk_t"] + dintra_t * m["qk_t"]) * decay_t
    dgc = jnp.sum(em, axis=1, keepdims=True) - jnp.sum(em_t, axis=1, keepdims=True)
    dq = dq + dq_dec * eg
    deg = deg + jnp.sum(dq_dec * q, axis=1, keepdims=True)
    dk = dk + dk_tail * et
    det = jnp.sum(dk_tail * k, axis=1, keepdims=True)
    dgc = dgc + deg * eg - det * et
    dgc_last = jnp.sum(det * et, axis=0, keepdims=True) + dgl * gl
    rcol = lax.broadcasted_iota(jnp.int32, (_C, 1), 0)
    m["dgc"] = dgc + jnp.where(rcol == _C - 1, dgc_last, 0.0)
    m["dq"] = dq
    m["dk"] = dk + dkb * beta
    m["dbeta"] = jnp.sum(dkb * k, axis=1, keepdims=True) + jnp.sum(dvb * v, axis=1, keepdims=True)
    m["dv"] = dvb * beta


def _mod_fwd(c_all, ada_w_loc, ada_b_loc):
    n, cols = c_all.shape[0], ada_w_loc.shape[1]

    def body(c_ref, w_ref, b_ref, o_ref):
        o_ref[...] = _dot(jax.nn.silu(c_ref[...]), w_ref[...], _NN) + b_ref[...]

    return pl.pallas_call(body, name="mod_fwd", out_shape=jax.ShapeDtypeStruct((n, cols), F32))(c_all, ada_w_loc, ada_b_loc)


def _ada_grad(c_all, dmod_loc, dmod_all):
    d, cols = c_all.shape[1], dmod_loc.shape[1]

    def body(c_ref, dl_ref, da_ref, gw_ref, gb_ref):
        gw_ref[...] = _dot(jax.nn.silu(c_ref[...]), dl_ref[...], _TN)
        gb_ref[...] = jnp.sum(da_ref[...], axis=0, keepdims=True)

    return pl.pallas_call(body, name="ada_grad", out_shape=[jax.ShapeDtypeStruct((d, cols), F32),
                                                           jax.ShapeDtypeStruct((1, dmod_all.shape[1]), F32)])(c_all, dmod_loc, dmod_all)


ELEMENTWISE_BLOCK_BYTES = 3 * 2 ** 19


def _row_tile(r, c=1024):
    fits = [tr for tr in range(16, r + 1, 16) if tr * c * 4 <= ELEMENTWISE_BLOCK_BYTES]
    if not fits:
        return r
    whole = [tr for tr in fits if r % tr == 0]
    return whole[-1] if whole else fits[-1]


def _adamw(w, m, v, grads, name):
    r, rest = w.shape[0], w.shape[1:]
    c = math.prod(rest)
    tr = _row_tile(r, c)
    blk = _vspec((tr,) + rest, lambda i: (i,) + (0,) * len(rest))
    n = len(grads)

    def body(*refs):
        w_ref, m_ref, v_ref = refs[:3]
        g_ref, d_ref, mo_ref, vo_ref = refs[3 + n:]
        g = refs[3][...]
        for p in refs[4:3 + n]:
            g = g + p[...]
        g_ref[...] = g
        d_ref[...], mo_ref[...], vo_ref[...] = _adamw_math(w_ref[...], m_ref[...], v_ref[...], g)

    o = jax.ShapeDtypeStruct(w.shape, F32)
    return pl.pallas_call(body, name=name, grid=(pl.cdiv(r, tr),), in_specs=[blk] * (3 + n), out_specs=[blk] * 4,
                          out_shape=[o] * 4)(w, m, v, *grads)


def _adamw_halves(w, m, v, own, other, ic, name):
    if w.ndim == 3:
        r, _, c = w.shape
        tr = _row_tile(r, c)
        blk = _vspec((tr, 1, c // 2), lambda i, j: (i, 0, j))
        half = _vspec((tr, 1, c // 2), lambda i, j: (i, 0, 0))
        grid = (pl.cdiv(r, tr), 2)
    else:
        r, c = w.shape
        tr = _row_tile(r // 2, c)
        per = r // 2 // tr
        blk = _vspec((tr, c), lambda i, j: (j * per + i, 0))
        half = _vspec((tr, c), lambda i, j: (i, 0))
        grid = (per, 2)

    def body(ic_ref, w_ref, m_ref, v_ref, own_ref, other_ref, g_ref, d_ref, mo_ref, vo_ref):
        g = jnp.where(pl.program_id(1) == ic_ref[0], own_ref[...], other_ref[...])
        g_ref[...] = g
        d_ref[...], mo_ref[...], vo_ref[...] = _adamw_math(w_ref[...], m_ref[...], v_ref[...], g)

    o = jax.ShapeDtypeStruct(w.shape, F32)
    return pl.pallas_call(body, name=name, grid=grid,
                          in_specs=[pl.BlockSpec(memory_space=pltpu.SMEM), blk, blk, blk, half, half], out_specs=[blk] * 4,
                          out_shape=[o] * 4)(ic, w, m, v, own, other)


def _adamw_math(w, m, v, g):
    m_new = ADAM_B1 * m + (1.0 - ADAM_B1) * g
    v_new = ADAM_B2 * v + (1.0 - ADAM_B2) * jnp.square(g)
    m_hat = m_new / (1.0 - ADAM_B1 ** ADAM_STEP)
    v_hat = v_new / (1.0 - ADAM_B2 ** ADAM_STEP)
    return -ADAM_LR * (m_hat / (jnp.sqrt(v_hat) + ADAM_EPS) + ADAM_WD * w), m_new, v_new


def _adamw_small(ws, ms, vs, gs, name):
    n = len(ws)

    def body(*refs):
        for i in range(n):
            w_ref, m_ref, v_ref, g_ref = (refs[j * n + i] for j in range(4))
            go_ref, d_ref, mo_ref, vo_ref = refs[4 * n + 4 * i:4 * n + 4 * i + 4]
            g = g_ref[...]
            go_ref[...] = g
            d_ref[...], mo_ref[...], vo_ref[...] = _adamw_math(w_ref[...], m_ref[...], v_ref[...], g)

    res = pl.pallas_call(body, name=name, out_shape=[jax.ShapeDtypeStruct(a.shape, F32) for a in ws for _ in range(4)])(
        *ws, *ms, *vs, *gs)
    return [res[4 * i:4 * i + 4] for i in range(n)]


def _sum_lead(x, name, rows_apart=False):
    p, r, c = x.shape
    tr = _row_tile(r, c)
    mid = (1,) if rows_apart else ()

    def body(x_ref, o_ref):
        acc = x_ref[0].astype(F32)
        for i in range(1, p):
            acc = acc + x_ref[i].astype(F32)
        o_ref[...] = acc.reshape(o_ref.shape)

    return pl.pallas_call(body, name=name, grid=(pl.cdiv(r, tr),), in_specs=[_vspec((p, tr, c), lambda i: (0, i, 0))],
                          out_specs=_vspec((tr,) + mid + (c,), lambda i: (i,) + (0,) * (1 + len(mid))),
                          out_shape=jax.ShapeDtypeStruct((r,) + mid + (c,), F32))(x)


def _allgather8(x_shard, name, plug=None):
    m_per, n = x_shard.shape

    def body(x_ref, out_ref, send_sems, recv_sems, local_sem):
        x, y, c = lax.axis_index("x"), lax.axis_index("y"), lax.axis_index("c")
        me, sibling = (x, y, c), (x, y, 1 - c)
        chips = [(1 - x, y), (x, 1 - y), (1 - x, 1 - y)]

        def rows(px, py, pc):
            return out_ref.at[pl.ds((4 * px + 2 * py + pc) * m_per, m_per), :]

        def copy(k, block, to, src=None):
            return pltpu.make_async_remote_copy(
                src_ref=rows(*block) if src is None else src, dst_ref=rows(*block), send_sem=send_sems.at[k],
                recv_sem=recv_sems.at[k], device_id=to, device_id_type=MESH)

        mine = pltpu.make_async_copy(x_ref, rows(*me), local_sem)
        mine.start()
        first = [copy(0, me, sibling, src=x_ref)]
        first += [copy(1 + j, me, (*chip, c), src=x_ref) for j, chip in enumerate(chips)]
        for cp in first:
            cp.start()
        passed = [copy(4 + j, (*chip, c), sibling) for j, chip in enumerate(chips)]
        for j, chip in enumerate(chips):
            copy(1 + j, (*chip, c), me).wait_recv()
            passed[j].start()
        copy(0, sibling, me).wait_recv()
        for j, chip in enumerate(chips):
            copy(4 + j, (*chip, 1 - c), me).wait_recv()
        for cp in first + passed:
            cp.wait_send()
        mine.wait()

    grid = (1,)
    (out,), extra = _plugged_call(
        body, plug, _grid_ends(grid), (x_shard,), name=name, grid=grid, out_shape=[jax.ShapeDtypeStruct((8 * m_per, n), x_shard.dtype)],
        in_specs=[pl.BlockSpec(memory_space=pltpu.VMEM)], out_specs=[pl.BlockSpec(memory_space=pltpu.VMEM)],
        scratch_shapes=[pltpu.SemaphoreType.DMA((7,)), pltpu.SemaphoreType.DMA((7,)), pltpu.SemaphoreType.DMA])
    return out if plug is None else (out, extra)


_HBM = pl.BlockSpec(memory_space=pltpu.HBM)


def _mesh_place():
    x, y, c = lax.axis_index("x"), lax.axis_index("y"), lax.axis_index("c")
    return x, y, c, 2 * x + y, [(1 - x, y), (x, 1 - y), (1 - x, 1 - y)]


def _gather_plug(shards):
    n = len(shards)

    def half(ref, c, lead=None):
        r, cols = ref.shape[-2] // 2, ref.shape[-1] // 2
        if r % 16 == 0:
            rows = pl.ds(pl.multiple_of(c * r, 16), r)
            return ref.at[rows, :] if lead is None else ref.at[lead, rows, :]
        lanes = pl.ds(pl.multiple_of(c * cols, LANES), cols)
        return ref.at[:, lanes] if lead is None else ref.at[lead, :, lanes]

    def copies(ins, outs, send, recv):
        x, y, c, me, chips = _mesh_place()
        ici, fwd, fwd_in = [], [], []
        for i in range(n):
            for j, (px, py) in enumerate(chips):
                q = 2 * px + py
                ici.append((pltpu.make_async_remote_copy(
                    src_ref=half(ins[i], c), dst_ref=half(outs[i], c, me), send_sem=send.at[6 * i + j], recv_sem=recv.at[6 * i + j],
                    device_id=(px, py, c), device_id_type=MESH),
                    pltpu.make_async_remote_copy(
                    src_ref=half(ins[i], c), dst_ref=half(outs[i], c, q), send_sem=send.at[6 * i + j], recv_sem=recv.at[6 * i + j],
                    device_id=(px, py, c), device_id_type=MESH)))
                fwd.append(pltpu.make_async_remote_copy(
                    src_ref=half(outs[i], c, q), dst_ref=half(outs[i], c, q), send_sem=send.at[6 * i + 3 + j],
                    recv_sem=recv.at[6 * i + 3 + j], device_id=(x, y, 1 - c), device_id_type=MESH))
                fwd_in.append(pltpu.make_async_remote_copy(
                    src_ref=half(outs[i], 1 - c, q), dst_ref=half(outs[i], 1 - c, q), send_sem=send.at[6 * i + 3 + j],
                    recv_sem=recv.at[6 * i + 3 + j], device_id=(x, y, 1 - c), device_id_type=MESH))
        return ici, fwd, fwd_in, me

    def start(ins, outs, send, recv, loc):
        ici, _, _, me = copies(ins, outs, send, recv)
        for i in range(n):
            pltpu.make_async_copy(ins[i], outs[i].at[me], loc.at[i]).start()
        for out_cp, _ in ici:
            out_cp.start()

    def finish(ins, outs, send, recv, loc):
        ici, fwd, fwd_in, me = copies(ins, outs, send, recv)
        for (_, in_cp), f in zip(ici, fwd):
            in_cp.wait_recv()
            f.start()
        for f in fwd_in:
            f.wait_recv()
        for (out_cp, _), f in zip(ici, fwd):
            out_cp.wait_send()
            f.wait_send()
        for i in range(n):
            pltpu.make_async_copy(ins[i], outs[i].at[me], loc.at[i]).wait()

    return dict(ins=list(shards), out_shape=[jax.ShapeDtypeStruct((N_CHIPS,) + a.shape, a.dtype) for a in shards],
                scratch=[pltpu.SemaphoreType.DMA((6 * n,)), pltpu.SemaphoreType.DMA((6 * n,)), pltpu.SemaphoreType.DMA((n,))],
                start=start, finish=finish)


def _exchange_plug(pieces):
    n = len(pieces)

    def copies(ins, outs, send, recv):
        x, y, c, me, chips = _mesh_place()
        out_cps, in_cps = [], []
        for i in range(n):
            for j, (px, py) in enumerate(chips):
                q = 2 * px + py
                out_cps.append(pltpu.make_async_remote_copy(src_ref=ins[i].at[q], dst_ref=outs[i].at[me], send_sem=send.at[3 * i + j],
                                                            recv_sem=recv.at[3 * i + j], device_id=(px, py, c), device_id_type=MESH))
                in_cps.append(pltpu.make_async_remote_copy(src_ref=ins[i].at[me], dst_ref=outs[i].at[q], send_sem=send.at[3 * i + j],
                                                           recv_sem=recv.at[3 * i + j], device_id=(px, py, c), device_id_type=MESH))
        return out_cps, in_cps, me

    def start(ins, outs, send, recv, loc):
        out_cps, _, me = copies(ins, outs, send, recv)
        for i in range(n):
            pltpu.make_async_copy(ins[i].at[me], outs[i].at[me], loc.at[i]).start()
        for cp in out_cps:
            cp.start()

    def finish(ins, outs, send, recv, loc):
        out_cps, in_cps, me = copies(ins, outs, send, recv)
        for cp in in_cps:
            cp.wait_recv()
        for cp in out_cps:
            cp.wait_send()
        for i in range(n):
            pltpu.make_async_copy(ins[i].at[me], outs[i].at[me], loc.at[i]).wait()

    return dict(ins=list(pieces), out_shape=[jax.ShapeDtypeStruct(a.shape, a.dtype) for a in pieces],
                scratch=[pltpu.SemaphoreType.DMA((3 * n,)), pltpu.SemaphoreType.DMA((3 * n,)), pltpu.SemaphoreType.DMA((n,))],
                start=start, finish=finish)


def _comm_call(plug, name):
    n_in, n_out = len(plug["ins"]), len(plug["out_shape"])

    def body(*refs):
        ins, outs, sems = refs[:n_in], refs[n_in:n_in + n_out], refs[n_in + n_out:]
        plug["start"](ins, outs, *sems)
        plug["finish"](ins, outs, *sems)

    return pl.pallas_call(body, name=name, out_shape=plug["out_shape"], in_specs=[_HBM] * n_in, out_specs=[_HBM] * n_out,
                          scratch_shapes=plug["scratch"])(*plug["ins"])


def _plugged_call(body, plug, first_last, args, *, name, grid, in_specs, out_specs, out_shape, scratch_shapes=()):
    in_specs, out_specs, out_shape, scratch_shapes = list(in_specs), list(out_specs), list(out_shape), list(scratch_shapes)
    if plug is None:
        return pl.pallas_call(body, name=name, grid=grid, in_specs=in_specs, out_specs=out_specs, out_shape=out_shape,
                              scratch_shapes=scratch_shapes)(*args), []
    n_in, n_out, n_sc = len(in_specs), len(out_specs), len(scratch_shapes)
    p_in, p_out = len(plug["ins"]), len(plug["out_shape"])

    def full(*refs):
        ins, refs = refs[:n_in], refs[n_in:]
        pins, refs = refs[:p_in], refs[p_in:]
        outs, refs = refs[:n_out], refs[n_out:]
        pouts, refs = refs[:p_out], refs[p_out:]
        scr, psems = refs[:n_sc], refs[n_sc:]
        first, last = first_last()

        @pl.when(first)
        def _():
            plug["start"](pins, pouts, *psems)

        body(*ins, *outs, *scr)

        @pl.when(last)
        def _():
            plug["finish"](pins, pouts, *psems)

    res = pl.pallas_call(full, name=name, grid=grid, in_specs=in_specs + [_HBM] * p_in, out_specs=out_specs + [_HBM] * p_out,
                         out_shape=out_shape + plug["out_shape"], scratch_shapes=scratch_shapes + plug["scratch"])(*args, *plug["ins"])
    return res[:n_out], res[n_out:]


def _grid_ends(grid):
    def ends():
        first = last = None
        for ax, n in enumerate(grid):
            i = pl.program_id(ax)
            first = (i == 0) if first is None else first & (i == 0)
            last = (i == n - 1) if last is None else last & (i == n - 1)
        return first, last
    return ends


def _pair_presum(pieces, name):
    n, r, cols = pieces.shape
    by_cols = (cols // 2) % LANES == 0
    half_shape = (n, r, cols // 2) if by_cols else (n, r // 2, cols)

    def body(p_ref, o_ref, mine_ref, land_ref, send_sem, recv_sem, local_sem):
        x, y, c = lax.axis_index("x"), lax.axis_index("y"), lax.axis_index("c")

        def half(which):
            if by_cols:
                return p_ref.at[:, :, pl.ds(pl.multiple_of(which * (cols // 2), LANES), cols // 2)]
            return p_ref.at[:, pl.ds(pl.multiple_of(which * (r // 2), 16), r // 2), :]

        push = pltpu.make_async_remote_copy(src_ref=half(1 - c), dst_ref=land_ref, send_sem=send_sem, recv_sem=recv_sem,
                                            device_id=(x, y, 1 - c), device_id_type=MESH)
        own = pltpu.make_async_copy(half(c), mine_ref, local_sem)
        push.start()
        own.start()
        own.wait()
        push.wait_recv()
        for i in range(n):
            o_ref[i] = (mine_ref[i].astype(F32) + land_ref[i].astype(F32)).astype(BF16)
        push.wait_send()

    return pl.pallas_call(
        body, name=name, out_shape=jax.ShapeDtypeStruct(half_shape, BF16), in_specs=[_HBM],
        out_specs=pl.BlockSpec(memory_space=pltpu.VMEM),
        scratch_shapes=[pltpu.VMEM(half_shape, BF16), pltpu.VMEM(half_shape, BF16), pltpu.SemaphoreType.DMA,
                        pltpu.SemaphoreType.DMA, pltpu.SemaphoreType.DMA],
    )(pieces)


def _sibling_exchange(arrs, name):
    n = len(arrs)

    def body(*refs):
        ins, outs = refs[:n], refs[n:2 * n]
        send_sems, recv_sems = refs[2 * n:]
        sibling = (lax.axis_index("x"), lax.axis_index("y"), 1 - lax.axis_index("c"))
        cps = [pltpu.make_async_remote_copy(src_ref=ins[i], dst_ref=outs[i], send_sem=send_sems.at[i], recv_sem=recv_sems.at[i],
                                            device_id=sibling, device_id_type=MESH) for i in range(n)]
        for cp in cps:
            cp.start()
        for cp in cps:
            cp.wait()

    return pl.pallas_call(
        body, name=name, out_shape=[jax.ShapeDtypeStruct(a.shape, a.dtype) for a in arrs], in_specs=[_HBM] * n, out_specs=[_HBM] * n,
        scratch_shapes=[pltpu.SemaphoreType.DMA((n,)), pltpu.SemaphoreType.DMA((n,))],
    )(*arrs)


_SEGS = ((0, AQ, C_Q), (AQ, AKV, C_K), (AQ + AKV, AKV, C_V), (AQ + 2 * AKV, 3 * DN, C_DQKV), (2304, DN, C_DZ),
         (2816, 2 * DN_HEADS, C_BD), (2824, D_MODEL, C_GA), (3848, D_MODEL, C_GD))
SHARD_ROWS = IN_DIM // N_CHIPS


def _to_padded(w4):
    wt = w4.reshape(IN_DIM, w4.shape[2])
    parts = [wt[o:o + n] for o, n, _ in sorted(_SEGS, key=lambda sg: sg[2])]
    return jnp.concatenate(parts + [jnp.zeros((IN_PAD - IN_DIM, wt.shape[1]), wt.dtype)], axis=0)


def _from_padded(gt):
    return jnp.concatenate([gt[ps:ps + n] for _, n, ps in sorted(_SEGS)], axis=0).reshape(N_CHIPS, SHARD_ROWS, gt.shape[1])


def _lane_vec(a):
    return jnp.zeros((1, LANES), F32).at[0, DN_HEADS:2 * DN_HEADS].set(a)


_ROW_SHARDED = ("w_in", "w_out", "ffn_w_down")
_FFN = ("ffn_w_up", "ffn_w_down")
_LATE_MIXER = ("w_attn_branch", "w_dn_branch", "w_out")
_PRESUM = ("w_in", "ffn_w_up")


def _pieces(k, a):
    if a.ndim == 3:
        return a
    if k in _ROW_SHARDED:
        return a.reshape(N_CHIPS, a.shape[0] // N_CHIPS, a.shape[1]).astype(BF16)
    return jnp.transpose(a.reshape(a.shape[0], N_CHIPS, a.shape[1] // N_CHIPS), (1, 0, 2)).astype(BF16)


def _assemble(k, a):
    if k in _ROW_SHARDED:
        return a.reshape(-1, a.shape[2])
    return jnp.transpose(a, (1, 0, 2)).reshape(a.shape[1], -1)


def _device_step(x2, tgt2, mod, p, bsz, shards=None):
    d = D_MODEL
    on_mesh = shards is not None
    p = dict(p)
    sh1, sc1, g1, sh2, sc2, g2 = [mod[:, i * d:(i + 1) * d].reshape(bsz, 1, d) for i in range(N_MOD)]
    alog_v, dt_v = _lane_vec(p["dn_a_log"]), _lane_vec(p["dn_dt_bias"])
    sinks = p["attn_sinks"].reshape(1, AQ_HEADS)
    u1 = _pre_fwd(x2, p["norm_mix_pre"], sc1, sh1, "pre1_fwd")
    if on_mesh:
        proj, got = _mm(u1, p["w_in"], "nt", F32, "mm_proj", _gather_plug(shards["late_mixer"]))
        p.update({k: _assemble(k, a) for k, a in zip(_LATE_MIXER, got)})
    else:
        proj = _mm(u1, p["w_in"], "nt", F32, "mm_proj")
    bias = _bias_build(p["rel_bias"])
    y_attn = _attn_fwd(proj, bias, sinks, bsz)
    qkvn = _dn_prep_fwd(proj, p["dn_conv_w"], bsz)
    bg = _bg_fwd(proj, alog_v, dt_v, bsz)
    nc = x2.shape[0] // bsz // DN_CHUNK
    g_rows = jnp.transpose(bg[:, DN_HEADS:2 * DN_HEADS].reshape(bsz, nc, DN_CHUNK, DN_HEADS), (0, 3, 1, 2))
    o, states, got = _dn_chunk_fwd(qkvn, bg, g_rows, bsz, _gather_plug(shards["ffn"][:1]) if on_mesh else None)
    for k, a in zip(_FFN[:1], got):
        p[k] = _assemble(k, a)
    y_dn = _dn_out_fwd(o, proj, p["dn_norm_w"])
    ya = _mm(y_attn, p["w_attn_branch"], "nn", BF16, "mm_ya")
    yd = _mm(y_dn, p["w_dn_branch"], "nn", BF16, "mm_yd")
    merged = _merge_fwd(proj, ya, yd)
    y1 = _mm(merged, p["w_out"], "nn", F32, "mm_y1")
    h1, u2 = _post_pre_fwd(x2, y1, p["norm_mix_post"], g1, p["norm_ffn_pre"], sc2, sh2, "post1_pre2_fwd")
    if on_mesh:
        up, got = _mm(u2, p["ffn_w_up"], "nn", F32, "mm_up", _gather_plug(shards["ffn"][1:]))
        p["ffn_w_down"] = _assemble("ffn_w_down", got[0])
    else:
        up = _mm(u2, p["ffn_w_up"], "nn", F32, "mm_up")
    act = _ffn_act_fwd(up, p["ffn_conv_w"], bsz)
    y2 = _mm(act, p["ffn_w_down"], "nn", F32, "mm_y2")
    dh2, dy2, g_ffn_post, dg2, sq = _post_loss_bwd(h1, y2, p["norm_ffn_post"], g2, tgt2, "post2_loss_bwd")
    g = {}
    g["norm_ffn_post"] = g_ffn_post
    dact = _mm(dy2, p["ffn_w_down"], "nt", F32, "mm_dact")
    g["ffn_w_down"] = _mm(act, dy2, "tn", BF16, "mm_dwdown")
    dupg, dupv, dcwg, dcwv, got_down = _ffn_act_bwd(
        up, p["ffn_conv_w"], dact, bsz, _exchange_plug([_pieces("ffn_w_down", g["ffn_w_down"])]) if on_mesh else None)
    g["ffn_conv_w"] = jnp.concatenate([dcwg, dcwv], axis=1)
    g["ffn_w_up"] = jnp.concatenate([_mm(u2, dupg, "tn", BF16, "mm_dwup_gate", split=N_CHIPS // 2),
                                     _mm(u2, dupv, "tn", BF16, "mm_dwup_val", split=N_CHIPS // 2)], axis=0)
    du2 = _mm(dupg, p["ffn_w_up"], "nt", F32, "mm_du2_gate", b_kblock=0)
    du2 = _mm(dupv, p["ffn_w_up"], "nt", F32, "mm_du2_val", b_kblock=1, add=du2)
    dh1, dy1, g["norm_ffn_pre"], dsc2, dsh2, g["norm_mix_post"], dg1 = _pre_post_bwd(
        h1, p["norm_ffn_pre"], sc2, sh2, du2, dh2, y1, p["norm_mix_post"], g1, "pre2_post1_bwd")
    dmerged = _mm(dy1, p["w_out"], "nt", BF16, "mm_dmerged")
    g["w_out"] = _mm(merged, dy1, "tn", BF16, "mm_dwout")
    dga, dgd, dya, dyd = _merge_bwd(proj, ya, yd, dmerged)
    dy_attn = _mm(dya, p["w_attn_branch"], "nt", BF16, "mm_dyattn")
    g["w_attn_branch"] = _mm(y_attn, dya, "tn", BF16, "mm_dwab", split=N_CHIPS)
    dy_dn = _mm(dyd, p["w_dn_branch"], "nt", F32, "mm_dydn")
    g["w_dn_branch"] = _mm(y_dn, dyd, "tn", BF16, "mm_dwdb", split=N_CHIPS)
    do, dz, g["dn_norm_w"] = _dn_out_bwd(o, proj, p["dn_norm_w"], dy_dn)
    def plug_for(names):
        if not on_mesh:
            return None
        return _exchange_plug([_pair_presum(_pieces(k, g[k]), "presum_" + k) if k in _PRESUM else _pieces(k, g[k]) for k in names])

    early = ("w_out", "w_attn_branch", "w_dn_branch")
    dqkvn, dbg4, got_up = _dn_chunk_bwd(qkvn, bg, g_rows, states, do, bsz, plug_for(_FFN[:1]))
    got_ffn = list(got_up) + list(got_down)
    d_dqkv, g["dn_conv_w"] = _dn_prep_bwd(proj, p["dn_conv_w"], dqkvn, bsz)
    dbd, g["dn_a_log"], g["dn_dt_bias"] = _bg_bwd(proj, alog_v, dt_v, dbg4, bsz)
    dq, dk, dv, dbias, g["attn_sinks"], got_early = _attn_bwd(proj, bias, sinks, dy_attn, bsz, plug_for(early))
    g["rel_bias"] = _bias_grad(dbias)
    dproj = jnp.concatenate([d_dqkv, dq, dz, dga, dgd, dk, dv, dbd], axis=1)
    g["w_in"] = _from_padded(_mm(dproj, u1, "tn", BF16, "mm_dwin"))
    if on_mesh:
        du1, got_in = _mm(dproj, p["w_in"], "nn", F32, "mm_du1", plug_for(("w_in",)))
        g.update(zip(_FFN + early + ("w_in",), list(got_ffn) + list(got_early) + list(got_in)))
    else:
        du1 = _mm(dproj, p["w_in"], "nn", F32, "mm_du1")
    dx, g["norm_mix_pre"], dsc1, dsh1 = _pre_bwd(x2, p["norm_mix_pre"], sc1, sh1, du1, dh1, "pre1_bwd")
    dmod = jnp.concatenate([dsh1, dsc1, dg1, dsh2, dsc2, dg2], axis=-1).reshape(bsz, N_MOD * d)
    return sq, dx, dmod, g


_SMALL = (("norm_mix_pre", D_MODEL), ("norm_mix_post", D_MODEL), ("norm_ffn_pre", D_MODEL), ("norm_ffn_post", D_MODEL),
          ("dn_norm_w", DN_HD), ("dn_a_log", LANES), ("dn_dt_bias", LANES), ("attn_sinks", AQ_HEADS * LANES),
          ("rel_bias", AQ_HEADS * LANES), ("dn_conv_w", DN_CONV * 3 * DN), ("ffn_conv_w", FFN_CONV * 2 * D_FF))


def _pack_rows(parts, rows):
    flat = jnp.concatenate([a.reshape(-1) for a in parts])
    return jnp.concatenate([flat, jnp.zeros((rows * LANES - flat.shape[0],), F32)]).reshape(rows, LANES)


def _pad128(a):
    flat = a.reshape(-1)
    n = -(-flat.shape[0] // LANES) * LANES
    return jnp.concatenate([flat, jnp.zeros((n - flat.shape[0],), F32)]) if n != flat.shape[0] else flat


_W_NAMES = ("ada_w", "ada_b", "norm_mix_pre", "norm_mix_post", "norm_ffn_pre", "norm_ffn_post", "w_in", "dn_conv_w", "dn_a_log",
            "dn_dt_bias", "dn_norm_w", "attn_sinks", "rel_bias", "w_attn_branch", "w_dn_branch", "w_out", "ffn_w_up", "ffn_conv_w",
            "ffn_w_down")
_BIG = ("w_in", "w_attn_branch", "w_dn_branch", "w_out", "ffn_w_up", "ffn_w_down")


def kernel(x, c, *rest):
    nw = len(_W_NAMES)
    w = dict(zip(_W_NAMES, rest[:nw]))
    loss_target = rest[nw]
    m = dict(zip(_W_NAMES, rest[nw + 1:2 * nw + 1]))
    v = dict(zip(_W_NAMES, rest[2 * nw + 1:3 * nw + 1]))
    ix, iy, ic = lax.axis_index("x"), lax.axis_index("y"), lax.axis_index("c")
    chip, dev = 2 * ix + iy, 4 * ix + 2 * iy + ic
    bsz, s, d = x.shape
    t = bsz * s
    n_dev = 8

    front_rows = 64
    front = _pack_rows([c, w["dn_conv_w"], w["ffn_conv_w"]], front_rows)
    w_in_t = jnp.swapaxes(w["w_in"][0], 0, 1).astype(BF16)
    w_in_cut = SHARD_ROWS // 32 * 16
    front_all, (w_in_lo,) = _allgather8(front, "ag_front", _gather_plug([w_in_t[:w_in_cut]]))
    front_all = front_all.reshape(n_dev, front_rows * LANES)
    n_c, n_dc, n_fc = bsz * d, DN_CONV * 3 * DN // N_CHIPS, FFN_CONV * 2 * D_FF // N_CHIPS
    c_all = front_all[:, :n_c].reshape(n_dev * bsz, d)
    per_chip = front_all[0::2]
    dn_conv_full = jnp.transpose(per_chip[:, n_c:n_c + n_dc].reshape(N_CHIPS, DN_CONV, -1), (1, 0, 2)).reshape(DN_CONV, 3 * DN)
    ffn_conv_full = jnp.transpose(per_chip[:, n_c + n_dc:n_c + n_dc + n_fc].reshape(N_CHIPS, FFN_CONV, -1), (1, 0, 2)).reshape(FFN_CONV, 2 * D_FF)

    mod_cols = N_MOD * d // N_CHIPS
    ada_b_loc = lax.dynamic_slice(w["ada_b"], (0, chip * mod_cols), (1, mod_cols))
    mod_part = _mod_fwd(c_all, w["ada_w"][0], ada_b_loc)
    mod_all, (w_in_hi,) = _allgather8(mod_part, "ag_mod", _gather_plug([w_in_t[w_in_cut:]]))
    mod_all = mod_all.reshape(n_dev, n_dev * bsz, mod_cols)[0::2]
    mod = jnp.transpose(lax.dynamic_slice(mod_all, (0, dev * bsz, 0), (N_CHIPS, bsz, mod_cols)), (1, 0, 2)).reshape(bsz, N_MOD * d)

    p = {"w_in": _to_padded(jnp.concatenate([w_in_lo, w_in_hi], axis=1))}
    shards = {"late_mixer": [w[k][0].astype(BF16) for k in _LATE_MIXER], "ffn": [w[k][0].astype(BF16) for k in _FFN]}
    for k in ("norm_mix_pre", "norm_mix_post", "norm_ffn_pre", "norm_ffn_post", "dn_norm_w", "attn_sinks"):
        p[k] = w[k]
    p["dn_a_log"], p["dn_dt_bias"], p["rel_bias"] = w["dn_a_log"][0], w["dn_dt_bias"][0], w["rel_bias"]
    p["dn_conv_w"], p["ffn_conv_w"] = dn_conv_full, ffn_conv_full

    sq, dx, dmod, g = _device_step(x.reshape(t, d), loss_target.reshape(t, d), mod, p, bsz, shards)
    loss = lax.psum(0.5 * jnp.sum(sq), ("x", "y", "c"))

    g["dn_a_log"], g["dn_dt_bias"] = g["dn_a_log"].reshape(-1), g["dn_dt_bias"].reshape(-1)
    small_rows = 328
    small = _pack_rows([dmod] + [g[k] for k, _ in _SMALL], small_rows)
    small_all = _allgather8(small, "ag_small").reshape(n_dev, small_rows, LANES)
    n_dm = bsz * N_MOD * d
    dmod_all = small_all.reshape(n_dev, -1)[:, :n_dm].reshape(n_dev * bsz, N_MOD * d)
    tot = _sum_lead(small_all, "sum_small").reshape(-1)
    gs, off = {}, n_dm
    for k, n in _SMALL:
        gs[k] = tot[off:off + n]
        off += n
    grad = {}
    grad["ada_w"], grad["ada_b"] = _ada_grad(c_all, lax.dynamic_slice(dmod_all, (0, chip * mod_cols), (n_dev * bsz, mod_cols)), dmod_all)
    for k in ("norm_mix_pre", "norm_mix_post", "norm_ffn_pre", "norm_ffn_post", "dn_norm_w"):
        grad[k] = gs[k]
    grad["dn_a_log"] = gs["dn_a_log"][DN_HEADS:2 * DN_HEADS]
    grad["dn_dt_bias"] = gs["dn_dt_bias"][DN_HEADS:2 * DN_HEADS]
    grad["attn_sinks"] = gs["attn_sinks"].reshape(AQ_HEADS, LANES)[:, 0]
    grad["rel_bias"] = gs["rel_bias"].reshape(AQ_HEADS, LANES)[:, :REL_BUCKETS].T
    grad["dn_conv_w"] = lax.dynamic_slice(gs["dn_conv_w"].reshape(DN_CONV, 3 * DN), (0, chip * (3 * DN // N_CHIPS)), (DN_CONV, 3 * DN // N_CHIPS))
    grad["ffn_conv_w"] = lax.dynamic_slice(gs["ffn_conv_w"].reshape(FFN_CONV, 2 * D_FF), (0, chip * (2 * D_FF // N_CHIPS)), (FFN_CONV, 2 * D_FF // N_CHIPS))

    mine = [_sum_lead(g[k], "sum_" + k, rows_apart=(k == "w_in")) for k in _BIG]
    theirs = _sibling_exchange(mine, "exchange_cores")

    out = {}
    for k, a, b in zip(_BIG, mine, theirs):
        core = ic.reshape(1).astype(jnp.int32)
        if k == "w_in":
            tr = lambda z: jnp.transpose(z, (2, 0, 1))
            out[k] = [jnp.transpose(r, (1, 2, 0)) for r in _adamw_halves(tr(w[k]), tr(m[k]), tr(v[k]), a, b, core, "adamw_" + k)]
        elif k in _PRESUM:
            out[k] = _adamw_halves(w[k][0], m[k][0], v[k][0], a, b, core, "adamw_" + k)
        else:
            out[k] = _adamw(w[k][0], m[k][0], v[k][0], [a, b], "adamw_" + k)
    out["ada_w"] = _adamw(w["ada_w"][0], m["ada_w"][0], v["ada_w"][0], [grad["ada_w"]], "adamw_ada_w")
    small_names = [k for k in _W_NAMES if k not in _BIG and k != "ada_w"]
    res_small = _adamw_small([w[k] for k in small_names], [m[k] for k in small_names], [v[k] for k in small_names],
                             [grad[k].reshape(w[k].shape) for k in small_names], "adamw_small")
    out.update(zip(small_names, res_small))
    for k in _BIG + ("ada_w",):
        out[k] = [r.reshape(w[k].shape) for r in out[k]]
    grads, deltas, new_m, new_v = ([out[k][i] for k in _W_NAMES] for i in range(4))
    return (loss, dx.reshape(bsz, s, d), *grads, *deltas, *new_m, *new_v)
```

```python
import functools
import math

import numpy as np
import jax
import jax.numpy as jnp
from jax import lax
from jax.experimental import pallas as pl
from jax.experimental.pallas import tpu as pltpu

F32 = jnp.float32
BF16 = jnp.bfloat16
MESH = pl.DeviceIdType.MESH

D_MODEL = 1024
N_MOD = 6
AQ_HEADS, AKV_HEADS, A_HD, WINDOW = 8, 2, 64, 128
REL_BUCKETS, REL_MAX_DIST = 32, 128
DN_HEADS, DN_HD, DN_CONV, DN_CHUNK = 4, 128, 4, 64
D_FF, FFN_CONV = 2816, 3
RMS_EPS, L2_EPS, NEG_INF = 1e-6, 1e-6, -1e30
AQ, AKV, DN = AQ_HEADS * A_HD, AKV_HEADS * A_HD, DN_HEADS * DN_HD
IN_DIM = AQ + 2 * AKV + 3 * DN + DN + 2 * DN_HEADS + 2 * D_MODEL
C_DQKV, C_Q, C_DZ, C_GA, C_GD, C_K, C_V, C_BD = 0, 1536, 2048, 2560, 3584, 4608, 4736, 4864
IN_PAD = 4992
LANES = 128
N_CHIPS = 4

ADAM_LR, ADAM_B1, ADAM_B2, ADAM_EPS, ADAM_WD, ADAM_STEP = 0.001, 0.9, 0.999, 1e-08, 0.01, 10


def _pick(n, cap):
    best = None
    for t in range(LANES, cap + 1, LANES):
        if n % t == 0:
            best = t
    return best if best is not None else n


def _vspec(shape, index_map):
    return pl.BlockSpec(shape, index_map)


MM_VMEM_BUDGET = 40 * 2 ** 20
GRID_STEP_S = 0.35e-6
HBM_BYTES_PER_S = 3.0e12
MXU_FLOPS_PER_S = 9.0e14
MXU_DIM = 256


def _mm_tiles(m, n, k, mode, in_bytes, out_bytes, split=1):
    best = None
    for tm in [t for t in range(LANES, m + 1, LANES) if m % t == 0]:
        for tn in [t for t in range(LANES, n // split + 1, LANES) if (n // split) % t == 0]:
            a, b, o = k * tm * in_bytes, k * tn * in_bytes, tm * tn * out_bytes
            if 2 * (a + b + o) + (a if mode == "tn" else 0) > MM_VMEM_BUDGET:
                continue
            hbm_s = (m * k * in_bytes + (m // tm) * n * k * in_bytes + m * n * out_bytes) / HBM_BYTES_PER_S
            mxu_s = 2 * m * n * k / (MXU_FLOPS_PER_S * min(1.0, tm / MXU_DIM) * min(1.0, tn / MXU_DIM))
            cost = (m // tm) * (n // tn) * GRID_STEP_S + max(hbm_s, mxu_s)
            if best is None or cost < best[0]:
                best = (cost, tm, tn)
    return best[1], best[2]


def _mm(a, b, mode, out_dtype, name, plug=None, split=1, b_kblock=0, add=None):
    if mode == "nn":
        (m, k), n = a.shape, b.shape[1]
        dims = (((1,), (0,)), ((), ()))
    elif mode == "nt":
        (m, k), n = a.shape, b.shape[0]
        dims = (((1,), (1,)), ((), ()))
    else:
        (k, m), n = a.shape, b.shape[1]
        dims = (((0,), (0,)), ((), ()))
    tm, tn = _mm_tiles(m, n, k, mode, a.dtype.itemsize, jnp.dtype(out_dtype).itemsize, split)
    if mode == "tn":
        a_spec = _vspec((k, tm), lambda i, j: (0, i))
    else:
        a_spec = _vspec((tm, k), lambda i, j: (i, 0))
    if mode == "nt":
        b_spec = _vspec((tn, k), lambda i, j: (j, b_kblock))
    else:
        b_spec = _vspec((k, tn), lambda i, j: (0, j))
    in_specs, args = [a_spec, b_spec], (a, b)
    if add is not None:
        in_specs, args = in_specs + [_vspec((tm, tn), lambda i, j: (i, j))], (a, b, add)

    def body(a_ref, b_ref, *rest):
        o_ref = rest[-1]
        acc = lax.dot_general(a_ref[...].astype(BF16), b_ref[...].astype(BF16), dims, preferred_element_type=F32)
        if add is not None:
            acc = acc + rest[0][...]
        o_ref[...] = acc.astype(out_dtype).reshape(o_ref.shape)

    grid = (m // tm, n // tn)
    if split == 1:
        out_spec, out_shape = _vspec((tm, tn), lambda i, j: (i, j)), (m, n)
    else:
        per = n // split // tn
        out_spec, out_shape = _vspec((1, tm, tn), lambda i, j: (j // per, i, j % per)), (split, m, n // split)
    (out,), extra = _plugged_call(body, plug, _grid_ends(grid), args, name=name, grid=grid, in_specs=in_specs,
                                  out_specs=[out_spec], out_shape=[jax.ShapeDtypeStruct(out_shape, out_dtype)])
    return out if plug is None else (out, extra)


def _rms(x, w):
    return (x * lax.rsqrt(jnp.mean(x * x, axis=-1, keepdims=True) + RMS_EPS)) * w


def _pre_f(x, w, sc, sh):
    return _rms(x, w) * (1.0 + sc) + sh


def _post_f(y, w, g):
    return g * _rms(y, w)


def _tok_grid(t, bsz, ts):
    nt = t // bsz // ts
    return nt, (bsz, nt)


def _pre_fwd(x, w, sc, sh, name, ts=512):
    t, d = x.shape
    bsz = sc.shape[0]
    nt, grid = _tok_grid(t, bsz, ts)
    row = _vspec((ts, d), lambda b, i: (b * nt + i, 0))
    vec = _vspec((1, d), lambda b, i: (0, 0))
    bvec = _vspec((1, 1, d), lambda b, i: (b, 0, 0))

    def body(x_ref, w_ref, sc_ref, sh_ref, u_ref):
        u_ref[...] = _pre_f(x_ref[...], w_ref[...], sc_ref[0], sh_ref[0]).astype(BF16)

    return pl.pallas_call(body, name=name, grid=grid, in_specs=[row, vec, bvec, bvec], out_specs=row,
                          out_shape=jax.ShapeDtypeStruct((t, d), BF16))(x, w, sc, sh)


def _pre_bwd(x, w, sc, sh, du, dres, name, ts=512):
    t, d = x.shape
    bsz = sc.shape[0]
    nt, grid = _tok_grid(t, bsz, ts)
    row = _vspec((ts, d), lambda b, i: (b * nt + i, 0))
    vec = _vspec((1, d), lambda b, i: (0, 0))
    bvec = _vspec((1, 1, d), lambda b, i: (b, 0, 0))

    def body(x_ref, w_ref, sc_ref, sh_ref, du_ref, dres_ref, dx_ref, dw_ref, dsc_ref, dsh_ref):
        b, i = pl.program_id(0), pl.program_id(1)
        _, vjp = jax.vjp(_pre_f, x_ref[...], w_ref[...], sc_ref[0], sh_ref[0])
        dx, dw, dsc, dsh = vjp(du_ref[...])
        dx_ref[...] = dres_ref[...] + dx

        @pl.when((b == 0) & (i == 0))
        def _():
            dw_ref[...] = jnp.zeros_like(dw_ref)

        @pl.when(i == 0)
        def _():
            dsc_ref[...] = jnp.zeros_like(dsc_ref)
            dsh_ref[...] = jnp.zeros_like(dsh_ref)

        dw_ref[...] += dw
        dsc_ref[0] += dsc
        dsh_ref[0] += dsh

    return pl.pallas_call(
        body, name=name, grid=grid, in_specs=[row, vec, bvec, bvec, row, row], out_specs=[row, vec, bvec, bvec],
        out_shape=[jax.ShapeDtypeStruct((t, d), F32), jax.ShapeDtypeStruct((1, d), F32),
                   jax.ShapeDtypeStruct((bsz, 1, d), F32), jax.ShapeDtypeStruct((bsz, 1, d), F32)],
    )(x, w, sc, sh, du, dres)


def _accumulate(ref, val, first):
    @pl.when(first)
    def _():
        ref[...] = jnp.zeros_like(ref)

    ref[...] += val.reshape(ref.shape)


def _post_pre_fwd(res, y, w_post, g, w_pre, sc, sh, name, ts=512):
    t, d = y.shape
    bsz = g.shape[0]
    nt, grid = _tok_grid(t, bsz, ts)
    row = _vspec((ts, d), lambda b, i: (b * nt + i, 0))
    vec = _vspec((1, d), lambda b, i: (0, 0))
    bvec = _vspec((1, 1, d), lambda b, i: (b, 0, 0))

    def body(res_ref, y_ref, wp_ref, g_ref, w_ref, sc_ref, sh_ref, h_ref, u_ref):
        h = res_ref[...] + _post_f(y_ref[...], wp_ref[...], g_ref[0])
        h_ref[...] = h
        u_ref[...] = _pre_f(h, w_ref[...], sc_ref[0], sh_ref[0]).astype(BF16)

    return pl.pallas_call(body, name=name, grid=grid, in_specs=[row, row, vec, bvec, vec, bvec, bvec], out_specs=[row, row],
                          out_shape=[jax.ShapeDtypeStruct((t, d), F32), jax.ShapeDtypeStruct((t, d), BF16)],
                          )(res, y, w_post, g, w_pre, sc, sh)


def _post_loss_bwd(res, y, w, g, tgt, name, ts=512):
    t, d = y.shape
    bsz = g.shape[0]
    nt, grid = _tok_grid(t, bsz, ts)
    row = _vspec((ts, d), lambda b, i: (b * nt + i, 0))
    vec = _vspec((1, d), lambda b, i: (0, 0))
    bvec = _vspec((1, 1, d), lambda b, i: (b, 0, 0))

    def body(res_ref, y_ref, w_ref, g_ref, tgt_ref, dh_ref, dy_ref, dw_ref, dg_ref, sq_ref):
        b, i = pl.program_id(0), pl.program_id(1)
        part, vjp = jax.vjp(_post_f, y_ref[...], w_ref[...], g_ref[0])
        e = res_ref[...] + part - tgt_ref[...]
        dh = e * (1.0 / d)
        dh_ref[...] = dh
        dy, dw, dg = vjp(dh)
        dy_ref[...] = dy.astype(BF16)
        _accumulate(dw_ref, dw, (b == 0) & (i == 0))
        _accumulate(dg_ref, dg, i == 0)
        _accumulate(sq_ref, jnp.sum(e * e, axis=0, keepdims=True) * (1.0 / d), (b == 0) & (i == 0))

    return pl.pallas_call(
        body, name=name, grid=grid, in_specs=[row, row, vec, bvec, row], out_specs=[row, row, vec, bvec, vec],
        out_shape=[jax.ShapeDtypeStruct((t, d), F32), jax.ShapeDtypeStruct((t, d), BF16), jax.ShapeDtypeStruct((1, d), F32),
                   jax.ShapeDtypeStruct((bsz, 1, d), F32), jax.ShapeDtypeStruct((1, d), F32)],
    )(res, y, w, g, tgt)


def _pre_post_bwd(x, w, sc, sh, du, dres, y, w_post, g, name, ts=512):
    t, d = x.shape
    bsz = sc.shape[0]
    nt, grid = _tok_grid(t, bsz, ts)
    row = _vspec((ts, d), lambda b, i: (b * nt + i, 0))
    vec = _vspec((1, d), lambda b, i: (0, 0))
    bvec = _vspec((1, 1, d), lambda b, i: (b, 0, 0))

    def body(x_ref, w_ref, sc_ref, sh_ref, du_ref, dres_ref, y_ref, wp_ref, g_ref,
             dx_ref, dy_ref, dw_ref, dsc_ref, dsh_ref, dwp_ref, dg_ref):
        b, i = pl.program_id(0), pl.program_id(1)
        _, vjp = jax.vjp(_pre_f, x_ref[...], w_ref[...], sc_ref[0], sh_ref[0])
        dx, dw, dsc, dsh = vjp(du_ref[...])
        dx = dres_ref[...] + dx
        dx_ref[...] = dx
        _, vjp_post = jax.vjp(_post_f, y_ref[...], wp_ref[...], g_ref[0])
        dy, dwp, dg = vjp_post(dx)
        dy_ref[...] = dy.astype(BF16)
        first = (b == 0) & (i == 0)
        _accumulate(dw_ref, dw, first)
        _accumulate(dwp_ref, dwp, first)
        _accumulate(dsc_ref, dsc, i == 0)
        _accumulate(dsh_ref, dsh, i == 0)
        _accumulate(dg_ref, dg, i == 0)

    v1, vb = jax.ShapeDtypeStruct((1, d), F32), jax.ShapeDtypeStruct((bsz, 1, d), F32)
    return pl.pallas_call(
        body, name=name, grid=grid, in_specs=[row, vec, bvec, bvec, row, row, row, vec, bvec],
        out_specs=[row, row, vec, bvec, bvec, vec, bvec],
        out_shape=[jax.ShapeDtypeStruct((t, d), F32), jax.ShapeDtypeStruct((t, d), BF16), v1, vb, vb, v1, vb],
    )(x, w, sc, sh, du, dres, y, w_post, g)


def _merge_f(ga, gd, ya, yd):
    return jax.nn.sigmoid(ga) * ya + jax.nn.sigmoid(gd) * yd


_MW = 512


def _merge_fwd(proj, ya, yd, ts=512):
    t, d = ya.shape
    blk = _vspec((ts, _MW), lambda i, j: (i, j))
    ga = _vspec((ts, _MW), lambda i, j: (i, C_GA // _MW + j))
    gd = _vspec((ts, _MW), lambda i, j: (i, C_GD // _MW + j))

    def body(ga_ref, gd_ref, ya_ref, yd_ref, o_ref):
        o_ref[...] = _merge_f(ga_ref[...], gd_ref[...], ya_ref[...].astype(F32), yd_ref[...].astype(F32)).astype(BF16)

    return pl.pallas_call(body, name="merge_fwd", grid=(t // ts, d // _MW), in_specs=[ga, gd, blk, blk], out_specs=blk,
                          out_shape=jax.ShapeDtypeStruct((t, d), BF16))(proj, proj, ya, yd)


_ANY = pl.BlockSpec(memory_space=pl.ANY)


def _branch_bwd(dproj, proj, y, dm, col0, name, ts=512):
    t, d = y.shape
    blk = _vspec((ts, _MW), lambda i, j: (i, j))
    gate = _vspec((ts, _MW), lambda i, j: (i, col0 // _MW + j))

    def body(buf_ref, g_ref, y_ref, dm_ref, dg_ref, dy_ref):
        del buf_ref
        _, vjp = jax.vjp(lambda g, yy: jax.nn.sigmoid(g) * yy, g_ref[...], y_ref[...].astype(F32))
        dg, dy = vjp(dm_ref[...].astype(F32))
        dg_ref[...] = dg.astype(BF16)
        dy_ref[...] = dy.astype(BF16)

    return pl.pallas_call(body, name=name, grid=(t // ts, d // _MW), in_specs=[_ANY, gate, blk, blk], out_specs=[gate, blk],
                          out_shape=[jax.ShapeDtypeStruct(dproj.shape, BF16), jax.ShapeDtypeStruct((t, d), BF16)],
                          input_output_aliases={0: 0})(dproj, proj, y, dm)


def _shift_down(x, s):
    if s == 0:
        return x
    r = lax.broadcasted_iota(jnp.int32, x.shape, 0)
    return jnp.where(r >= s, pltpu.roll(x, s, 0), 0.0)


def _shift_up(x, s):
    if s == 0:
        return x
    n = x.shape[0]
    r = lax.broadcasted_iota(jnp.int32, x.shape, 0)
    return jnp.where(r < n - s, pltpu.roll(x, n - s, 0), 0.0)


def _conv_fwd(x, w, k):
    out = None
    for j in range(k):
        term = w[j:j + 1, :] * _shift_down(x, k - 1 - j)
        out = term if out is None else out + term
    return out


def _conv_bwd(x, w, dc, k):
    dx = None
    dws = []
    for j in range(k):
        up = _shift_up(dc, k - 1 - j)
        term = w[j:j + 1, :] * up
        dx = term if dx is None else dx + term
        dws.append(jnp.sum(up * x, axis=0, keepdims=True))
    return dx, jnp.concatenate(dws, axis=0)


def _geglu_f(gate, val):
    return jax.nn.gelu(gate, approximate=True) * val


_FW = 256


def _ffn_act_fwd(up, conv_w, bsz):
    t = up.shape[0]
    s = t // bsz
    nj = D_FF // _FW
    xg = _vspec((s, _FW), lambda b, j: (b, j))
    xv = _vspec((s, _FW), lambda b, j: (b, nj + j))
    wg = _vspec((FFN_CONV, _FW), lambda b, j: (0, j))
    wv = _vspec((FFN_CONV, _FW), lambda b, j: (0, nj + j))

    def body(xg_ref, xv_ref, wg_ref, wv_ref, o_ref):
        gate = _conv_fwd(xg_ref[...], wg_ref[...], FFN_CONV)
        val = _conv_fwd(xv_ref[...], wv_ref[...], FFN_CONV)
        o_ref[...] = _geglu_f(gate, val).astype(BF16)

    return pl.pallas_call(body, name="ffn_act_fwd", grid=(bsz, nj), in_specs=[xg, xv, wg, wv],
                          out_specs=_vspec((s, _FW), lambda b, j: (b, j)),
                          out_shape=jax.ShapeDtypeStruct((t, D_FF), BF16))(up, up, conv_w, conv_w)


def _ffn_act_bwd(up, conv_w, dact, bsz, plug=None):
    t = up.shape[0]
    s = t // bsz
    nj = D_FF // _FW
    xg = _vspec((s, _FW), lambda j, b: (b, j))
    xv = _vspec((s, _FW), lambda j, b: (b, nj + j))
    wg = _vspec((FFN_CONV, _FW), lambda j, b: (0, j))
    wv = _vspec((FFN_CONV, _FW), lambda j, b: (0, nj + j))
    da = _vspec((s, _FW), lambda j, b: (b, j))
    dwo = _vspec((FFN_CONV, _FW), lambda j, b: (0, j))

    def body(xg_ref, xv_ref, wg_ref, wv_ref, da_ref, dxg_ref, dxv_ref, dwg_ref, dwv_ref):
        b = pl.program_id(1)
        xg_, xv_, wg_, wv_ = xg_ref[...], xv_ref[...], wg_ref[...], wv_ref[...]
        gate = _conv_fwd(xg_, wg_, FFN_CONV)
        val = _conv_fwd(xv_, wv_, FFN_CONV)
        _, vjp = jax.vjp(_geglu_f, gate, val)
        dgate, dval = vjp(da_ref[...])
        dxg, dwg = _conv_bwd(xg_, wg_, dgate, FFN_CONV)
        dxv, dwv = _conv_bwd(xv_, wv_, dval, FFN_CONV)
        dxg_ref[...] = dxg.astype(BF16)
        dxv_ref[...] = dxv.astype(BF16)

        @pl.when(b == 0)
        def _():
            dwg_ref[...] = jnp.zeros_like(dwg_ref)
            dwv_ref[...] = jnp.zeros_like(dwv_ref)

        dwg_ref[...] += dwg
        dwv_ref[...] += dwv

    outs, extra = _plugged_call(
        body, plug, _grid_ends((nj, bsz)), (up, up, conv_w, conv_w, dact), name="ffn_act_bwd", grid=(nj, bsz),
        in_specs=[xg, xv, wg, wv, da], out_specs=[da, da, dwo, dwo],
        out_shape=[jax.ShapeDtypeStruct((t, D_FF), BF16), jax.ShapeDtypeStruct((t, D_FF), BF16),
                   jax.ShapeDtypeStruct((FFN_CONV, D_FF), F32), jax.ShapeDtypeStruct((FFN_CONV, D_FF), F32)])
    return (*outs, extra)


def _bucket_table():
    qi = np.arange(WINDOW)[:, None]
    kj = np.arange(2 * WINDOW)[None, :]
    dist = WINDOW + qi - kj
    dc = np.maximum(dist, 0)
    max_exact = REL_BUCKETS // 2
    scaled = np.log(np.maximum(dc, 1).astype(np.float32) / np.float32(max_exact)) / np.float32(math.log(REL_MAX_DIST / max_exact))
    large = max_exact + (scaled.astype(np.float32) * np.float32(REL_BUCKETS - max_exact)).astype(np.int32)
    large = np.minimum(large, REL_BUCKETS - 1)
    bucket = np.where(dc < max_exact, dc, large).astype(np.int32)
    in_band = ((dist >= 0) & (dist < WINDOW)).astype(np.int32)
    return bucket, in_band


def _bias_build(rel_bias):
    bucket, _ = _bucket_table()

    def body(rb_ref, idx_ref, o_ref):
        h = pl.program_id(0)
        idx = idx_ref[...]
        acc = jnp.zeros(idx.shape, F32)
        for r in range(REL_BUCKETS):
            acc = jnp.where(idx == r, rb_ref[r, h], acc)
        o_ref[0] = acc

    return pl.pallas_call(
        body, name="bias_build", grid=(AQ_HEADS,),
        in_specs=[pl.BlockSpec(memory_space=pltpu.SMEM), _vspec((WINDOW, 2 * WINDOW), lambda h: (0, 0))],
        out_specs=_vspec((1, WINDOW, 2 * WINDOW), lambda h: (h, 0, 0)),
        out_shape=jax.ShapeDtypeStruct((AQ_HEADS, WINDOW, 2 * WINDOW), F32),
    )(rel_bias, jnp.asarray(bucket))


def _bias_grad(dbias):
    bucket, _ = _bucket_table()

    def body(db_ref, idx_ref, o_ref):
        idx = idx_ref[...]
        db = db_ref[0]
        lane = lax.broadcasted_iota(jnp.int32, (1, LANES), 1)
        acc = jnp.zeros((1, LANES), F32)
        for r in range(REL_BUCKETS):
            s = jnp.sum(jnp.sum(jnp.where(idx == r, db, 0.0), axis=1, keepdims=True), axis=0, keepdims=True)
            acc = jnp.where(lane == r, s, acc)
        o_ref[0] = acc

    return pl.pallas_call(
        body, name="bias_grad", grid=(AQ_HEADS,),
        in_specs=[_vspec((1, WINDOW, 2 * WINDOW), lambda h: (h, 0, 0)), _vspec((WINDOW, 2 * WINDOW), lambda h: (0, 0))],
        out_specs=_vspec((1, 1, LANES), lambda h: (h, 0, 0)),
        out_shape=jax.ShapeDtypeStruct((AQ_HEADS, 1, LANES), F32),
    )(dbias, jnp.asarray(bucket))


def _attn_mask(n):
    qi = lax.broadcasted_iota(jnp.int32, (WINDOW, 2 * WINDOW), 0)
    kj = lax.broadcasted_iota(jnp.int32, (WINDOW, 2 * WINDOW), 1)
    dist = WINDOW + qi - kj
    band = (dist >= 0) & (dist < WINDOW)
    return band & ((kj >= WINDOW) | (n > 0))


def _attn_probs(qk, bias, sink, mask):
    s = jnp.where(mask, qk * (A_HD ** -0.5) + bias, NEG_INF)
    m = jnp.maximum(jnp.max(s, axis=-1, keepdims=True), sink)
    p = jnp.exp(s - m)
    es = jnp.exp(sink - m)
    inv = 1.0 / (jnp.sum(p, axis=-1, keepdims=True) + es)
    return p * inv, es * inv


def _attn_fwd(proj, bias, sinks, bsz):
    t = proj.shape[0]
    s = t // bsz
    nb = s // WINDOW
    grp = AQ_HEADS // AKV_HEADS

    def body(q_ref, k_ref, v_ref, bias_ref, sink_ref, y_ref, kp_ref, vp_ref):
        kp_ref[0:WINDOW, :] = jnp.zeros((WINDOW, LANES), BF16)
        vp_ref[0:WINDOW, :] = jnp.zeros((WINDOW, LANES), BF16)
        kp_ref[WINDOW:, :] = k_ref[...].astype(BF16)
        vp_ref[WINDOW:, :] = v_ref[...].astype(BF16)

        def blk(n, carry):
            r0 = pl.multiple_of(n * WINDOW, WINDOW)
            mask = _attn_mask(n)
            kband = kp_ref[pl.ds(r0, 2 * WINDOW), :]
            vband = vp_ref[pl.ds(r0, 2 * WINDOW), :]
            qb = q_ref[pl.ds(r0, WINDOW), :].astype(BF16)
            heads = range(AQ_HEADS)
            hsl = lambda h: slice(h * A_HD, (h + 1) * A_HD)
            kbs = [kband[:, hsl(kv)] for kv in range(AKV_HEADS)]
            vbs = [vband[:, hsl(kv)] for kv in range(AKV_HEADS)]
            qks = [lax.dot_general(qb[:, hsl(h)], kbs[h // grp], _NT, preferred_element_type=F32) for h in heads]
            probs = [_attn_probs(qks[h], bias_ref[h], sink_ref[0, h], mask)[0] for h in heads]
            outs = [jnp.dot(probs[h].astype(BF16), vbs[h // grp], preferred_element_type=F32) for h in heads]
            y_ref[pl.ds(r0, WINDOW), :] = jnp.concatenate(outs, axis=1).astype(BF16)
            return carry

        lax.fori_loop(0, nb, blk, 0)

    return pl.pallas_call(
        body, name="attn_fwd", grid=(bsz,),
        in_specs=[_vspec((s, AQ), lambda b: (b, C_Q // AQ)), _vspec((s, AKV), lambda b: (b, C_K // AKV)),
                  _vspec((s, AKV), lambda b: (b, C_V // AKV)),
                  _vspec((AQ_HEADS, WINDOW, 2 * WINDOW), lambda b: (0, 0, 0)), pl.BlockSpec(memory_space=pltpu.SMEM)],
        out_specs=_vspec((s, AQ), lambda b: (b, 0)), out_shape=jax.ShapeDtypeStruct((t, AQ), BF16),
        scratch_shapes=[pltpu.VMEM((s + WINDOW, LANES), BF16), pltpu.VMEM((s + WINDOW, LANES), BF16)],
    )(proj, proj, proj, bias, sinks)


def _attn_bwd(dproj, proj, bias, sinks, dy, bsz, plug=None):
    t = proj.shape[0]
    s = t // bsz
    nb = s // WINDOW
    grp = AQ_HEADS // AKV_HEADS
    scale = A_HD ** -0.5

    def body(q_ref, k_ref, v_ref, bias_ref, sink_ref, dy_ref, dq_ref, dk_ref, dv_ref, dbias_ref, dsink_ref,
             kp_ref, vp_ref, dkp_ref, dvp_ref):
        b = pl.program_id(0)
        kp_ref[0:WINDOW, :] = jnp.zeros((WINDOW, LANES), BF16)
        vp_ref[0:WINDOW, :] = jnp.zeros((WINDOW, LANES), BF16)
        kp_ref[WINDOW:, :] = k_ref[...].astype(BF16)
        vp_ref[WINDOW:, :] = v_ref[...].astype(BF16)
        dkp_ref[...] = jnp.zeros_like(dkp_ref)
        dvp_ref[...] = jnp.zeros_like(dvp_ref)

        @pl.when(b == 0)
        def _():
            dbias_ref[...] = jnp.zeros_like(dbias_ref)
            dsink_ref[...] = jnp.zeros_like(dsink_ref)

        def blk(n, carry):
            r0 = pl.multiple_of(n * WINDOW, WINDOW)
            mask = _attn_mask(n)
            kband = kp_ref[pl.ds(r0, 2 * WINDOW), :]
            vband = vp_ref[pl.ds(r0, 2 * WINDOW), :]
            qb = q_ref[pl.ds(r0, WINDOW), :].astype(BF16)
            dyb = dy_ref[pl.ds(r0, WINDOW), :].astype(BF16)
            heads = range(AQ_HEADS)
            hsl = lambda h: slice(h * A_HD, (h + 1) * A_HD)
            kbs = [kband[:, hsl(kv)] for kv in range(AKV_HEADS)]
            vbs = [vband[:, hsl(kv)] for kv in range(AKV_HEADS)]
            qhs = [qb[:, hsl(h)] for h in heads]
            dyhs = [dyb[:, hsl(h)] for h in heads]
            qks = [lax.dot_general(qhs[h], kbs[h // grp], _NT, preferred_element_type=F32) for h in heads]
            dprobs = [lax.dot_general(dyhs[h], vbs[h // grp], _NT, preferred_element_type=F32) for h in heads]
            pbs, dsbs = [], []
            for h in heads:
                probs, psink = _attn_probs(qks[h], bias_ref[h], sink_ref[0, h], mask)
                rowdot = jnp.sum(probs * dprobs[h], axis=-1, keepdims=True)
                ds = probs * (dprobs[h] - rowdot)
                dbias_ref[h] += ds
                dsink_ref[h] += jnp.sum(-psink * rowdot, axis=0, keepdims=True) + jnp.zeros((1, LANES), F32)
                pbs.append(probs.astype(BF16))
                dsbs.append(ds.astype(BF16))
            dvhs = [lax.dot_general(pbs[h], dyhs[h], _TN, preferred_element_type=F32) for h in heads]
            dqs = [jnp.dot(dsbs[h], kbs[h // grp], preferred_element_type=F32) * scale for h in heads]
            dkhs = [lax.dot_general(dsbs[h], qhs[h], _TN, preferred_element_type=F32) * scale for h in heads]
            dks = [sum(dkhs[kv * grp + 1:(kv + 1) * grp], dkhs[kv * grp]) for kv in range(AKV_HEADS)]
            dvs = [sum(dvhs[kv * grp + 1:(kv + 1) * grp], dvhs[kv * grp]) for kv in range(AKV_HEADS)]
            dq_ref[pl.ds(r0, WINDOW), :] = jnp.concatenate(dqs, axis=1).astype(BF16)
            dkp_ref[pl.ds(r0, 2 * WINDOW), :] += jnp.concatenate(dks, axis=1)
            dvp_ref[pl.ds(r0, 2 * WINDOW), :] += jnp.concatenate(dvs, axis=1)
            return carry

        lax.fori_loop(0, nb, blk, 0)
        dk_ref[...] = dkp_ref[WINDOW:, :].astype(BF16)
        dv_ref[...] = dvp_ref[WINDOW:, :].astype(BF16)

    kvs = jax.ShapeDtypeStruct((t, AKV), BF16)
    outs, extra = _plugged_call(
        lambda buf_ref, *refs: body(*refs), plug, _grid_ends((bsz,)), (dproj, proj, proj, proj, bias, sinks, dy),
        name="attn_bwd", grid=(bsz,),
        in_specs=[_ANY, _vspec((s, AQ), lambda b: (b, C_Q // AQ)), _vspec((s, AKV), lambda b: (b, C_K // AKV)),
                  _vspec((s, AKV), lambda b: (b, C_V // AKV)),
                  _vspec((AQ_HEADS, WINDOW, 2 * WINDOW), lambda b: (0, 0, 0)), pl.BlockSpec(memory_space=pltpu.SMEM),
                  _vspec((s, AQ), lambda b: (b, 0))],
        out_specs=[_vspec((s, AQ), lambda b: (b, C_Q // AQ)), _vspec((s, AKV), lambda b: (b, 0)), _vspec((s, AKV), lambda b: (b, 0)),
                   _vspec((AQ_HEADS, WINDOW, 2 * WINDOW), lambda b: (0, 0, 0)), _vspec((AQ_HEADS, 1, LANES), lambda b: (0, 0, 0))],
        out_shape=[jax.ShapeDtypeStruct(dproj.shape, BF16), kvs, kvs,
                   jax.ShapeDtypeStruct((AQ_HEADS, WINDOW, 2 * WINDOW), F32), jax.ShapeDtypeStruct((AQ_HEADS, 1, LANES), F32)],
        scratch_shapes=[pltpu.VMEM((s + WINDOW, LANES), BF16), pltpu.VMEM((s + WINDOW, LANES), BF16),
                        pltpu.VMEM((s + WINDOW, LANES), F32), pltpu.VMEM((s + WINDOW, LANES), F32)],
        aliases={0: 0})
    return (*outs, extra)


def _dn_act_f(c, is_qk):
    a = jax.nn.silu(c)
    outs = []
    for h in range(DN_HEADS):
        ah = a[:, h * DN_HD:(h + 1) * DN_HD]
        nh = ah * lax.rsqrt(jnp.sum(ah * ah, axis=-1, keepdims=True) + L2_EPS)
        outs.append(jnp.where(is_qk, nh, ah))
    return jnp.concatenate(outs, axis=1)


def _dn_prep_fwd(proj, conv_w, bsz):
    t = proj.shape[0]
    s = t // bsz
    blk = _vspec((s, DN), lambda b, j: (b, j))
    wsp = _vspec((DN_CONV, DN), lambda b, j: (0, j))

    def body(x_ref, w_ref, o_ref):
        j = pl.program_id(1)
        o_ref[...] = _dn_act_f(_conv_fwd(x_ref[...], w_ref[...], DN_CONV), j < 2)

    return pl.pallas_call(body, name="dn_prep_fwd", grid=(bsz, 3), in_specs=[blk, wsp], out_specs=blk,
                          out_shape=jax.ShapeDtypeStruct((t, 3 * DN), F32))(proj, conv_w)


def _dn_prep_bwd(dproj, proj, conv_w, dqkvn, bsz):
    t = proj.shape[0]
    s = t // bsz
    blk = _vspec((s, DN), lambda j, b: (b, j))
    wsp = _vspec((DN_CONV, DN), lambda j, b: (0, j))

    def body(buf_ref, x_ref, w_ref, d_ref, dx_ref, dw_ref):
        del buf_ref
        j, b = pl.program_id(0), pl.program_id(1)
        x, w = x_ref[...], w_ref[...]
        c = _conv_fwd(x, w, DN_CONV)
        _, vjp = jax.vjp(lambda cc: _dn_act_f(cc, j < 2), c)
        (dc,) = vjp(d_ref[0])
        dx, dw = _conv_bwd(x, w, dc, DN_CONV)
        dx_ref[...] = dx.astype(BF16)

        @pl.when(b == 0)
        def _():
            dw_ref[...] = jnp.zeros_like(dw_ref)

        dw_ref[...] += dw

    return pl.pallas_call(
        body, name="dn_prep_bwd", grid=(3, bsz),
        in_specs=[_ANY, blk, wsp, _vspec((1, s, DN), lambda j, b: (j, b, 0))], out_specs=[blk, wsp],
        out_shape=[jax.ShapeDtypeStruct(dproj.shape, BF16), jax.ShapeDtypeStruct((DN_CONV, 3 * DN), F32)],
        input_output_aliases={0: 0},
    )(dproj, proj, conv_w, dqkvn)


def _bg_f(x, alog, dt):
    lane = lax.broadcasted_iota(jnp.int32, x.shape, 1)
    beta = jax.nn.sigmoid(x)
    g = -jnp.exp(alog) * jax.nn.softplus(x + dt)
    return jnp.where(lane < DN_HEADS, beta, jnp.where(lane < 2 * DN_HEADS, g, 0.0))


def _bg_fwd(proj, alog, dt, bsz):
    t = proj.shape[0]
    s = t // bsz
    vec = _vspec((1, LANES), lambda b: (0, 0))

    def body(x_ref, a_ref, d_ref, o_ref):
        o_ref[...] = _bg_f(x_ref[...], a_ref[...], d_ref[...])

    return pl.pallas_call(body, name="bg_fwd", grid=(bsz,), in_specs=[_vspec((s, LANES), lambda b: (b, C_BD // LANES)), vec, vec],
                          out_specs=_vspec((s, LANES), lambda b: (b, 0)), out_shape=jax.ShapeDtypeStruct((t, LANES), F32))(proj, alog, dt)


def _bg_bwd(dproj, proj, alog, dt, dbg4, dk, dv, bsz):
    t = proj.shape[0]
    s = t // bsz
    vec = _vspec((1, LANES), lambda b: (0, 0))
    kv = _vspec((s, AKV), lambda b: (b, 0))
    tail = 3 * LANES

    def body(buf_ref, x_ref, a_ref, d_ref, g4_ref, dk_ref, dv_ref, dx_ref, da_ref, dd_ref):
        del buf_ref
        b = pl.program_id(0)
        lane = lax.broadcasted_iota(jnp.int32, (s, LANES), 1)
        dbg = jnp.zeros((s, LANES), F32)
        for h in range(DN_HEADS):
            gh = g4_ref[:, h * DN_HD:(h + 1) * DN_HD]
            dbg = jnp.where(lane == h, gh[:, 0:1], dbg)
            dbg = jnp.where(lane == DN_HEADS + h, gh[:, 1:2], dbg)
        _, vjp = jax.vjp(_bg_f, x_ref[...], a_ref[...], d_ref[...])
        dx, da, dd = vjp(dbg)
        dx_ref[...] = jnp.concatenate([dk_ref[...], dv_ref[...], dx.astype(BF16)], axis=1)

        @pl.when(b == 0)
        def _():
            da_ref[...] = jnp.zeros_like(da_ref)
            dd_ref[...] = jnp.zeros_like(dd_ref)

        da_ref[...] += da
        dd_ref[...] += dd

    return pl.pallas_call(
        body, name="bg_bwd", grid=(bsz,),
        in_specs=[_ANY, _vspec((s, LANES), lambda b: (b, C_BD // LANES)), vec, vec, _vspec((s, DN), lambda b: (b, 0)), kv, kv],
        out_specs=[_vspec((s, tail), lambda b: (b, C_K // tail)), vec, vec],
        out_shape=[jax.ShapeDtypeStruct(dproj.shape, BF16), jax.ShapeDtypeStruct((1, LANES), F32), jax.ShapeDtypeStruct((1, LANES), F32)],
        input_output_aliases={0: 0},
    )(dproj, proj, alog, dt, dbg4, dk, dv)


def _dn_out_f(o, z, w):
    outs = []
    for h in range(DN_HEADS):
        sl = slice(h * DN_HD, (h + 1) * DN_HD)
        outs.append(_rms(o[:, sl], w) * jax.nn.silu(z[:, sl]))
    return jnp.concatenate(outs, axis=1)


def _dn_out_fwd(o, proj, w, ts=512):
    t = o.shape[0]
    blk = _vspec((ts, DN), lambda i: (i, 0))
    zsp = _vspec((ts, DN), lambda i: (i, C_DZ // DN))
    vec = _vspec((1, DN_HD), lambda i: (0, 0))

    def body(o_ref, z_ref, w_ref, y_ref):
        y_ref[...] = _dn_out_f(o_ref[...], z_ref[...], w_ref[...]).astype(BF16)

    return pl.pallas_call(body, name="dn_out_fwd", grid=(t // ts,), in_specs=[blk, zsp, vec], out_specs=blk,
                          out_shape=jax.ShapeDtypeStruct((t, DN), BF16))(o, proj, w)


def _dn_out_bwd(dproj, o, proj, w, dy, ts=512):
    t = o.shape[0]
    blk = _vspec((ts, DN), lambda i: (i, 0))
    zsp = _vspec((ts, DN), lambda i: (i, C_DZ // DN))
    vec = _vspec((1, DN_HD), lambda i: (0, 0))

    def body(buf_ref, o_ref, z_ref, w_ref, dy_ref, do_ref, dz_ref, dw_ref):
        del buf_ref
        i = pl.program_id(0)
        _, vjp = jax.vjp(_dn_out_f, o_ref[...], z_ref[...], w_ref[...])
        do, dz, dw = vjp(dy_ref[...])
        do_ref[...] = do
        dz_ref[...] = dz.astype(BF16)

        @pl.when(i == 0)
        def _():
            dw_ref[...] = jnp.zeros_like(dw_ref)

        dw_ref[...] += dw

    return pl.pallas_call(
        body, name="dn_out_bwd", grid=(t // ts,), in_specs=[_ANY, blk, zsp, vec, blk], out_specs=[blk, zsp, vec],
        out_shape=[jax.ShapeDtypeStruct((t, DN), F32), jax.ShapeDtypeStruct(dproj.shape, BF16), jax.ShapeDtypeStruct((1, DN_HD), F32)],
        input_output_aliases={0: 1},
    )(dproj, o, proj, w, dy)


_C = DN_CHUNK


def _dot(a, b, dims):
    return lax.dot_general(a.astype(BF16), b.astype(BF16), dims, preferred_element_type=F32)


def _split(a):
    hi = a.astype(BF16)
    return hi, (a - hi.astype(F32)).astype(BF16)


def _dot3(a, b, dims):
    (ah, al), (bh, bl) = (a if isinstance(a, tuple) else _split(a)), (b if isinstance(b, tuple) else _split(b))
    mm = lambda x, y: lax.dot_general(x, y, dims, preferred_element_type=F32)
    return mm(ah, bh) + (mm(ah, bl) + mm(al, bh))


_NN = (((1,), (0,)), ((), ()))
_NT = (((1,), (1,)), ((), ()))
_TN = (((0,), (0,)), ((), ()))


_SUB = 8


def _tri_inverses(ls, lts):
    ri8 = lax.broadcasted_iota(jnp.int32, (_SUB, _C), 0)
    ci8 = lax.broadcasted_iota(jnp.int32, (_SUB, _C), 1)
    nblk = _C // _SUB
    ts = []
    for lt in lts:
        blocks = [jnp.where(ci8 == ri8 + _SUB * b, 1.0, 0.0).astype(F32) for b in range(nblk)]
        for r in range(1, _SUB):
            for b in range(nblk):
                coef = lt[_SUB * b:_SUB * (b + 1), _SUB * b + r:_SUB * b + r + 1]
                row = jnp.sum(coef * blocks[b], axis=0, keepdims=True)
                blocks[b] = jnp.where(ri8 == r, blocks[b] - row, blocks[b])
        ts.append(jnp.concatenate(blocks, axis=0))
    ri = lax.broadcasted_iota(jnp.int32, (_C, _C), 0)
    ci = lax.broadcasted_iota(jnp.int32, (_C, _C), 1)
    s = _SUB
    while s < _C:
        shift = s.bit_length()
        quad = ((ri >> shift) == (ci >> shift)) & ((ri & s) != 0) & ((ci & s) == 0)
        offs = [jnp.where(quad, l, 0.0) for l in ls]
        tsp = [_split(t) for t in ts]
        left = [_dot3(tp, off, _NN) for tp, off in zip(tsp, offs)]
        ts = [t - _dot3(lo, tp, _NN) for t, lo, tp in zip(ts, left, tsp)]
        s *= 2
    return ts


_SEG = 512
_HEADS = tuple(range(DN_HEADS))


def _hsl(hh):
    return slice(hh * DN_HD, (hh + 1) * DN_HD)


def _chunk_specs(bsz, nseg, reverse):
    seg = (lambda i: nseg - 1 - i) if reverse else (lambda i: i)
    ncs = _SEG // _C
    col = lambda off: _vspec((bsz, _SEG, DN), lambda i: (0, seg(i), off))
    return (col, _vspec((bsz, _SEG, LANES), lambda i: (0, seg(i), 0)),
            _vspec((bsz, DN_HEADS, ncs, _C), lambda i: (0, 0, seg(i), 0)),
            _vspec((bsz, DN_HEADS, ncs, DN_HD, DN_HD), lambda i: (0, 0, seg(i), 0, 0)),
            _vspec((bsz, DN_HEADS, ncs, _C, _C), lambda i: (0, 0, seg(i), 0, 0)))


def _chunk_pre(q_ref, k_ref, v_ref, bg_ref, gr_ref, c, bb, hh):
    r0 = pl.multiple_of(c * _C, _C)
    ri = lax.broadcasted_iota(jnp.int32, (_C, _C), 0)
    ci = lax.broadcasted_iota(jnp.int32, (_C, _C), 1)
    q = q_ref[bb, pl.ds(r0, _C), _hsl(hh)] * (DN_HD ** -0.5)
    k = k_ref[bb, pl.ds(r0, _C), _hsl(hh)]
    v = v_ref[bb, pl.ds(r0, _C), _hsl(hh)]
    bgc = bg_ref[bb, pl.ds(r0, _C), :]
    beta = bgc[:, hh:hh + 1]
    g_col = bgc[:, DN_HEADS + hh:DN_HEADS + hh + 1]
    g_row = gr_ref[bb, hh, pl.ds(c, 1), :]
    gc_col = jnp.sum(jnp.where(ri >= ci, g_row, 0.0), axis=1, keepdims=True)
    gc_row = jnp.sum(jnp.where(ri <= ci, g_col, 0.0), axis=0, keepdims=True)
    gc_last = jnp.sum(g_col, axis=0, keepdims=True)
    diff = gc_col - gc_row
    decay = jnp.where(ri >= ci, jnp.exp(jnp.where(ri >= ci, diff, 0.0)), 0.0)
    diff_t = gc_row - gc_col
    decay_t = jnp.where(ri <= ci, jnp.exp(jnp.where(ri <= ci, diff_t, 0.0)), 0.0)
    eg = jnp.exp(gc_col)
    et = jnp.exp(gc_last - gc_col)
    gl = jnp.exp(gc_last)
    kb = k * beta
    vb = v * beta
    return dict(r0=r0, bb=bb, hh=hh, q=q, k=k, v=v, beta=beta, decay=decay, decay_t=decay_t, eg=eg, et=et, gl=gl, kb=kb, vb=vb,
                ri=ri, ci=ci)


def _chunk_solve(ms, tms=None):
    for m in ms:
        m["kk_t"] = _dot(m["k"], m["kb"], _NT)
        m["qk"] = _dot(m["q"], m["k"], _NT)
        m["kk"] = _dot(m["kb"], m["k"], _NT)
        if tms is not None:
            m["qk_t"] = _dot(m["k"], m["q"], _NT)
    if tms is None:
        tms = _tri_inverses([jnp.where(m["ri"] > m["ci"], m["kk"] * m["decay"], 0.0) for m in ms],
                            [jnp.where(m["ri"] < m["ci"], m["kk_t"] * m["decay_t"], 0.0) for m in ms])
    for m, tm in zip(ms, tms):
        m["tm_f32"] = tm
    for m in ms:
        rhs = jnp.concatenate([m["vb"], m["kb"] * m["eg"]], axis=1)
        m["tm"] = _split(m["tm_f32"])
        m["sol"] = _dot3(m["tm"], rhs, _NN)
        m["intra"] = jnp.where(m["ri"] >= m["ci"], m["qk"] * m["decay"], 0.0)


def _dn_chunk_fwd(qkvn, bg, g_rows, bsz, plug=None):
    t = qkvn.shape[0]
    s = t // bsz
    nc, nseg = s // _C, s // _SEG
    pairs = [(bb, hh) for bb in range(bsz) for hh in _HEADS]

    def body(q_ref, k_ref, v_ref, bg_ref, gr_ref, o_ref, st_ref, tm_ref, s_ref):
        @pl.when(pl.program_id(0) == 0)
        def _():
            s_ref[...] = jnp.zeros_like(s_ref)

        def chunk(c, carry):
            ms = [_chunk_pre(q_ref, k_ref, v_ref, bg_ref, gr_ref, c, bb, hh) for bb, hh in pairs]
            _chunk_solve(ms)
            sts = [s_ref[i] for i in range(len(pairs))]
            for m, st in zip(ms, sts):
                st_ref[m["bb"], m["hh"], c] = st
                tm_ref[m["bb"], m["hh"], c] = m["tm_f32"]
            ws = [_dot(m["sol"][:, DN_HD:], st, _NN) for m, st in zip(ms, sts)]
            qs = [_dot(m["q"] * m["eg"], st, _NN) for m, st in zip(ms, sts)]
            v_new = [m["sol"][:, :DN_HD] - a for m, a in zip(ms, ws)]
            iv = [_dot(m["intra"], vn, _NN) for m, vn in zip(ms, v_new)]
            upd = [_dot(m["k"] * m["et"], vn, _TN) for m, vn in zip(ms, v_new)]
            for i, (bb, hh) in enumerate(pairs):
                s_ref[i] = sts[i] * ms[i]["gl"] + upd[i]
                o_ref[bb, pl.ds(ms[i]["r0"], _C), _hsl(hh)] = qs[i] + iv[i]
            return carry

        lax.fori_loop(0, _SEG // _C, chunk, 0)

    col, bgs, grs, sts_spec, tms_spec = _chunk_specs(bsz, nseg, False)
    q3, bg3 = qkvn.reshape(bsz, s, 3 * DN), bg.reshape(bsz, s, LANES)
    (o, states, tms), extra = _plugged_call(
        body, plug, _grid_ends((nseg,)), (q3, q3, q3, bg3, g_rows), name="dn_chunk_fwd", grid=(nseg,),
        in_specs=[col(0), col(1), col(2), bgs, grs], out_specs=[col(0), sts_spec, tms_spec],
        out_shape=[jax.ShapeDtypeStruct((bsz, s, DN), F32), jax.ShapeDtypeStruct((bsz, DN_HEADS, nc, DN_HD, DN_HD), F32),
                   jax.ShapeDtypeStruct((bsz, DN_HEADS, nc, _C, _C), F32)],
        scratch_shapes=[pltpu.VMEM((bsz * DN_HEADS, DN_HD, DN_HD), F32)])
    return o.reshape(t, DN), (states, tms), extra


def _dn_chunk_bwd(qkvn, bg, g_rows, states, do, bsz, plug=None):
    t = qkvn.shape[0]
    s = t // bsz
    nc, nseg = s // _C, s // _SEG
    pairs = [(bb, hh) for bb in range(bsz) for hh in _HEADS]

    def body(q_ref, k_ref, v_ref, bg_ref, gr_ref, st_ref, tm_ref, do_ref, dqkv_ref, dbg_ref, ds_ref):
        @pl.when(pl.program_id(0) == 0)
        def _():
            ds_ref[...] = jnp.zeros_like(ds_ref)

        def chunk(cc, carry):
            c = _SEG // _C - 1 - cc
            ms = [_chunk_pre(q_ref, k_ref, v_ref, bg_ref, gr_ref, c, bb, hh) for bb, hh in pairs]
            _chunk_solve(ms, [tm_ref[bb, hh, c] for bb, hh in pairs])
            ri, ci = ms[0]["ri"], ms[0]["ci"]
            for i, m in enumerate(ms):
                m["st"] = st_ref[m["bb"], m["hh"], c]
                m["ds_out"] = ds_ref[i]
                m["do"] = do_ref[m["bb"], pl.ds(m["r0"], _C), _hsl(m["hh"])]
                m["w"] = m["sol"][:, DN_HD:]
            for m in ms:
                m["v_new"] = m["sol"][:, :DN_HD] - _dot(m["w"], m["st"], _NN)
            for m in ms:
                m["q_dec"], m["k_tail"] = m["q"] * m["eg"], m["k"] * m["et"]
                m["dk_tail"] = _dot(m["v_new"], m["ds_out"], _NT)
                m["dv_new"] = _dot(m["k_tail"], m["ds_out"], _NN) + _dot(m["intra"], m["do"], _TN)
                m["dq_dec"] = _dot(m["do"], m["st"], _NT)
                m["ds_in"] = m["ds_out"] * m["gl"] + _dot(m["q_dec"], m["do"], _TN)
                m["dintra"] = jnp.where(ri >= ci, _dot(m["do"], m["v_new"], _NT), 0.0)
                m["dintra_t"] = jnp.where(ri <= ci, _dot(m["v_new"], m["do"], _NT), 0.0)
            for m in ms:
                m["dw"] = -_dot(m["dv_new"], m["st"], _NT)
                m["ds_in"] = m["ds_in"] - _dot(m["w"], m["dv_new"], _TN)
            for m in ms:
                dsol = jnp.concatenate([m["dv_new"], m["dw"]], axis=1)
                m["drhs"] = _dot3(m["tm"], dsol, _TN)
            for m in ms:
                m["dl"] = jnp.where(ri > ci, -_dot(m["drhs"], m["sol"], _NT), 0.0)
                m["dl_t"] = jnp.where(ri < ci, -_dot(m["sol"], m["drhs"], _NT), 0.0)
            for m in ms:
                m["dkb2"] = _dot(m["dl"] * m["decay"], m["k"], _NN)
                m["dk"] = _dot(m["dl_t"] * m["decay_t"], m["kb"], _NN) + _dot(m["dintra_t"] * m["decay_t"], m["q"], _NN)
                m["dq"] = _dot(m["dintra"] * m["decay"], m["k"], _NN)
            for m in ms:
                _chunk_bwd_finish(m)
            for m in ms:
                ones_ge = jnp.where(ri <= ci, 1.0, 0.0).astype(BF16)
                gh, gl_ = _split(m["dgc"] + jnp.zeros((_C, LANES), F32))
                m["dg_b"] = jnp.dot(ones_ge, gh, preferred_element_type=F32) + jnp.dot(ones_ge, gl_, preferred_element_type=F32)
            lane = lax.broadcasted_iota(jnp.int32, (_C, LANES), 1)
            for i, m in enumerate(ms):
                bb, hh, rows = m["bb"], m["hh"], pl.ds(m["r0"], _C)
                dqkv_ref[0, bb, rows, _hsl(hh)] = m["dq"] * (DN_HD ** -0.5)
                dqkv_ref[1, bb, rows, _hsl(hh)] = m["dk"]
                dqkv_ref[2, bb, rows, _hsl(hh)] = m["dv"]
                dbg_ref[bb, rows, _hsl(hh)] = jnp.where(lane == 0, m["dbeta"], jnp.where(lane == 1, m["dg_b"], 0.0))
                ds_ref[i] = m["ds_in"]
            return carry

        lax.fori_loop(0, _SEG // _C, chunk, 0)

    col, bgs, grs, sts_spec, tms_spec = _chunk_specs(bsz, nseg, True)
    q3, bg3, do3 = qkvn.reshape(bsz, s, 3 * DN), bg.reshape(bsz, s, LANES), do.reshape(bsz, s, DN)
    (dqkv, dbg), extra = _plugged_call(
        body, plug, _grid_ends((nseg,)), (q3, q3, q3, bg3, g_rows, *states, do3), name="dn_chunk_bwd", grid=(nseg,),
        in_specs=[col(0), col(1), col(2), bgs, grs, sts_spec, tms_spec, col(0)],
        out_specs=[_vspec((3, bsz, _SEG, DN), lambda i: (0, 0, nseg - 1 - i, 0)), col(0)],
        out_shape=[jax.ShapeDtypeStruct((3, bsz, s, DN), F32), jax.ShapeDtypeStruct((bsz, s, DN), F32)],
        scratch_shapes=[pltpu.VMEM((bsz * DN_HEADS, DN_HD, DN_HD), F32)])
    return dqkv.reshape(3, t, DN), dbg.reshape(t, DN), extra


def _chunk_bwd_finish(m):
    q, k, v, beta, decay, decay_t = m["q"], m["k"], m["v"], m["beta"], m["decay"], m["decay_t"]
    eg, et, gl, kb, dl, dl_t, dintra, dintra_t = m["eg"], m["et"], m["gl"], m["kb"], m["dl"], m["dl_t"], m["dintra"], m["dintra_t"]
    dq_dec, dk_tail, dq, dk = m["dq_dec"], m["dk_tail"], m["dq"], m["dk"]
    dgl = jnp.sum(jnp.sum(m["ds_out"] * m["st"], axis=1, keepdims=True), axis=0, keepdims=True)
    dvb, dkbeg = m["drhs"][:, :DN_HD], m["drhs"][:, DN_HD:]
    dkb = dkbeg * eg + m["dkb2"]
    deg = jnp.sum(dkbeg * kb, axis=1, keepdims=True)
    em = (dl * m["kk"] + dintra * m["qk"]) * decay
    em_t = (dl_t * m["kk_t"] + dintra_t * m["qk_t"]) * decay_t
    dgc = jnp.sum(em, axis=1, keepdims=True) - jnp.sum(em_t, axis=1, keepdims=True)
    dq = dq + dq_dec * eg
    deg = deg + jnp.sum(dq_dec * q, axis=1, keepdims=True)
    dk = dk + dk_tail * et
    det = jnp.sum(dk_tail * k, axis=1, keepdims=True)
    dgc = dgc + deg * eg - det * et
    dgc_last = jnp.sum(det * et, axis=0, keepdims=True) + dgl * gl
    rcol = lax.broadcasted_iota(jnp.int32, (_C, 1), 0)
    m["dgc"] = dgc + jnp.where(rcol == _C - 1, dgc_last, 0.0)
    m["dq"] = dq
    m["dk"] = dk + dkb * beta
    m["dbeta"] = jnp.sum(dkb * k, axis=1, keepdims=True) + jnp.sum(dvb * v, axis=1, keepdims=True)
    m["dv"] = dvb * beta


def _mod_fwd(c_all, ada_w_loc, ada_b_loc):
    n, cols = c_all.shape[0], ada_w_loc.shape[1]

    def body(c_ref, w_ref, b_ref, o_ref):
        o_ref[...] = _dot(jax.nn.silu(c_ref[...]), w_ref[...], _NN) + b_ref[...]

    return pl.pallas_call(body, name="mod_fwd", out_shape=jax.ShapeDtypeStruct((n, cols), F32))(c_all, ada_w_loc, ada_b_loc)


def _ada_grad(c_all, dmod_loc, dmod_all):
    d, cols = c_all.shape[1], dmod_loc.shape[1]

    def body(c_ref, dl_ref, da_ref, gw_ref, gb_ref):
        gw_ref[...] = _dot(jax.nn.silu(c_ref[...]), dl_ref[...], _TN)
        gb_ref[...] = jnp.sum(da_ref[...], axis=0, keepdims=True)

    return pl.pallas_call(body, name="ada_grad", out_shape=[jax.ShapeDtypeStruct((d, cols), F32),
                                                           jax.ShapeDtypeStruct((1, dmod_all.shape[1]), F32)])(c_all, dmod_loc, dmod_all)


ELEMENTWISE_BLOCK_BYTES = 3 * 2 ** 19


def _row_tile(r, c=1024):
    fits = [tr for tr in range(16, r + 1, 16) if tr * c * 4 <= ELEMENTWISE_BLOCK_BYTES]
    if not fits:
        return r
    whole = [tr for tr in fits if r % tr == 0]
    return whole[-1] if whole else fits[-1]


def _adamw(w, m, v, grads, name):
    r, rest = w.shape[0], w.shape[1:]
    c = math.prod(rest)
    tr = _row_tile(r, c)
    blk = _vspec((tr,) + rest, lambda i: (i,) + (0,) * len(rest))
    n = len(grads)

    def body(*refs):
        w_ref, m_ref, v_ref = refs[:3]
        g_ref, d_ref, mo_ref, vo_ref = refs[3 + n:]
        g = refs[3][...]
        for p in refs[4:3 + n]:
            g = g + p[...]
        g_ref[...] = g
        d_ref[...], mo_ref[...], vo_ref[...] = _adamw_math(w_ref[...], m_ref[...], v_ref[...], g)

    o = jax.ShapeDtypeStruct(w.shape, F32)
    return pl.pallas_call(body, name=name, grid=(pl.cdiv(r, tr),), in_specs=[blk] * (3 + n), out_specs=[blk] * 4,
                          out_shape=[o] * 4)(w, m, v, *grads)


def _adamw_halves(w, m, v, own, other, ic, name):
    if w.ndim == 3:
        r, _, c = w.shape
        tr = _row_tile(r, c)
        blk = _vspec((tr, 1, c // 2), lambda i, j: (i, 0, j))
        half = _vspec((tr, 1, c // 2), lambda i, j: (i, 0, 0))
        grid = (pl.cdiv(r, tr), 2)
    else:
        r, c = w.shape
        tr = _row_tile(r // 2, c)
        per = r // 2 // tr
        blk = _vspec((tr, c), lambda i, j: (j * per + i, 0))
        half = _vspec((tr, c), lambda i, j: (i, 0))
        grid = (per, 2)

    def body(ic_ref, w_ref, m_ref, v_ref, own_ref, other_ref, g_ref, d_ref, mo_ref, vo_ref):
        g = jnp.where(pl.program_id(1) == ic_ref[0], own_ref[...], other_ref[...])
        g_ref[...] = g
        d_ref[...], mo_ref[...], vo_ref[...] = _adamw_math(w_ref[...], m_ref[...], v_ref[...], g)

    o = jax.ShapeDtypeStruct(w.shape, F32)
    return pl.pallas_call(body, name=name, grid=grid,
                          in_specs=[pl.BlockSpec(memory_space=pltpu.SMEM), blk, blk, blk, half, half], out_specs=[blk] * 4,
                          out_shape=[o] * 4)(ic, w, m, v, own, other)


def _adamw_math(w, m, v, g):
    m_new = ADAM_B1 * m + (1.0 - ADAM_B1) * g
    v_new = ADAM_B2 * v + (1.0 - ADAM_B2) * jnp.square(g)
    m_hat = m_new / (1.0 - ADAM_B1 ** ADAM_STEP)
    v_hat = v_new / (1.0 - ADAM_B2 ** ADAM_STEP)
    return -ADAM_LR * (m_hat / (jnp.sqrt(v_hat) + ADAM_EPS) + ADAM_WD * w), m_new, v_new


def _adamw_small(ws, ms, vs, gs, name):
    n = len(ws)

    def body(*refs):
        for i in range(n):
            w_ref, m_ref, v_ref, g_ref = (refs[j * n + i] for j in range(4))
            go_ref, d_ref, mo_ref, vo_ref = refs[4 * n + 4 * i:4 * n + 4 * i + 4]
            g = g_ref[...]
            go_ref[...] = g
            d_ref[...], mo_ref[...], vo_ref[...] = _adamw_math(w_ref[...], m_ref[...], v_ref[...], g)

    res = pl.pallas_call(body, name=name, out_shape=[jax.ShapeDtypeStruct(a.shape, F32) for a in ws for _ in range(4)])(
        *ws, *ms, *vs, *gs)
    return [res[4 * i:4 * i + 4] for i in range(n)]


def _sum_lead(x, name, rows_apart=False):
    p, r, c = x.shape
    tr = _row_tile(r, c)
    mid = (1,) if rows_apart else ()

    def body(x_ref, o_ref):
        acc = x_ref[0].astype(F32)
        for i in range(1, p):
            acc = acc + x_ref[i].astype(F32)
        o_ref[...] = acc.reshape(o_ref.shape)

    return pl.pallas_call(body, name=name, grid=(pl.cdiv(r, tr),), in_specs=[_vspec((p, tr, c), lambda i: (0, i, 0))],
                          out_specs=_vspec((tr,) + mid + (c,), lambda i: (i,) + (0,) * (1 + len(mid))),
                          out_shape=jax.ShapeDtypeStruct((r,) + mid + (c,), F32))(x)


def _allgather8(x_shard, name, plug=None):
    m_per, n = x_shard.shape

    def body(x_ref, out_ref, send_sems, recv_sems, local_sem):
        x, y, c = lax.axis_index("x"), lax.axis_index("y"), lax.axis_index("c")
        me, sibling = (x, y, c), (x, y, 1 - c)
        chips = [(1 - x, y), (x, 1 - y), (1 - x, 1 - y)]

        def rows(px, py, pc):
            return out_ref.at[pl.ds((4 * px + 2 * py + pc) * m_per, m_per), :]

        def copy(k, block, to, src=None):
            return pltpu.make_async_remote_copy(
                src_ref=rows(*block) if src is None else src, dst_ref=rows(*block), send_sem=send_sems.at[k],
                recv_sem=recv_sems.at[k], device_id=to, device_id_type=MESH)

        mine = pltpu.make_async_copy(x_ref, rows(*me), local_sem)
        mine.start()
        first = [copy(0, me, sibling, src=x_ref)]
        first += [copy(1 + j, me, (*chip, c), src=x_ref) for j, chip in enumerate(chips)]
        for cp in first:
            cp.start()
        passed = [copy(4 + j, (*chip, c), sibling) for j, chip in enumerate(chips)]
        for j, chip in enumerate(chips):
            copy(1 + j, (*chip, c), me).wait_recv()
            passed[j].start()
        copy(0, sibling, me).wait_recv()
        for j, chip in enumerate(chips):
            copy(4 + j, (*chip, 1 - c), me).wait_recv()
        for cp in first + passed:
            cp.wait_send()
        mine.wait()

    grid = (1,)
    (out,), extra = _plugged_call(
        body, plug, _grid_ends(grid), (x_shard,), name=name, grid=grid, out_shape=[jax.ShapeDtypeStruct((8 * m_per, n), x_shard.dtype)],
        in_specs=[pl.BlockSpec(memory_space=pltpu.VMEM)], out_specs=[pl.BlockSpec(memory_space=pltpu.VMEM)],
        scratch_shapes=[pltpu.SemaphoreType.DMA((7,)), pltpu.SemaphoreType.DMA((7,)), pltpu.SemaphoreType.DMA])
    return out if plug is None else (out, extra)


_HBM = pl.BlockSpec(memory_space=pltpu.HBM)


def _mesh_place():
    x, y, c = lax.axis_index("x"), lax.axis_index("y"), lax.axis_index("c")
    return x, y, c, 2 * x + y, [(1 - x, y), (x, 1 - y), (1 - x, 1 - y)]


def _gather_plug(shards):
    n = len(shards)

    def half(ref, c, lead=None):
        r, cols = ref.shape[-2] // 2, ref.shape[-1] // 2
        if r % 16 == 0:
            rows = pl.ds(pl.multiple_of(c * r, 16), r)
            return ref.at[rows, :] if lead is None else ref.at[lead, rows, :]
        lanes = pl.ds(pl.multiple_of(c * cols, LANES), cols)
        return ref.at[:, lanes] if lead is None else ref.at[lead, :, lanes]

    def copies(ins, outs, send, recv):
        x, y, c, me, chips = _mesh_place()
        ici, fwd, fwd_in = [], [], []
        for i in range(n):
            for j, (px, py) in enumerate(chips):
                q = 2 * px + py
                ici.append((pltpu.make_async_remote_copy(
                    src_ref=half(ins[i], c), dst_ref=half(outs[i], c, me), send_sem=send.at[6 * i + j], recv_sem=recv.at[6 * i + j],
                    device_id=(px, py, c), device_id_type=MESH),
                    pltpu.make_async_remote_copy(
                    src_ref=half(ins[i], c), dst_ref=half(outs[i], c, q), send_sem=send.at[6 * i + j], recv_sem=recv.at[6 * i + j],
                    device_id=(px, py, c), device_id_type=MESH)))
                fwd.append(pltpu.make_async_remote_copy(
                    src_ref=half(outs[i], c, q), dst_ref=half(outs[i], c, q), send_sem=send.at[6 * i + 3 + j],
                    recv_sem=recv.at[6 * i + 3 + j], device_id=(x, y, 1 - c), device_id_type=MESH))
                fwd_in.append(pltpu.make_async_remote_copy(
                    src_ref=half(outs[i], 1 - c, q), dst_ref=half(outs[i], 1 - c, q), send_sem=send.at[6 * i + 3 + j],
                    recv_sem=recv.at[6 * i + 3 + j], device_id=(x, y, 1 - c), device_id_type=MESH))
        return ici, fwd, fwd_in, me

    def start(ins, outs, send, recv, loc):
        ici, _, _, me = copies(ins, outs, send, recv)
        for i in range(n):
            pltpu.make_async_copy(ins[i], outs[i].at[me], loc.at[i]).start()
        for out_cp, _ in ici:
            out_cp.start()

    def finish(ins, outs, send, recv, loc):
        ici, fwd, fwd_in, me = copies(ins, outs, send, recv)
        for (_, in_cp), f in zip(ici, fwd):
            in_cp.wait_recv()
            f.start()
        for f in fwd_in:
            f.wait_recv()
        for (out_cp, _), f in zip(ici, fwd):
            out_cp.wait_send()
            f.wait_send()
        for i in range(n):
            pltpu.make_async_copy(ins[i], outs[i].at[me], loc.at[i]).wait()

    return dict(ins=list(shards), out_shape=[jax.ShapeDtypeStruct((N_CHIPS,) + a.shape, a.dtype) for a in shards],
                scratch=[pltpu.SemaphoreType.DMA((6 * n,)), pltpu.SemaphoreType.DMA((6 * n,)), pltpu.SemaphoreType.DMA((n,))],
                start=start, finish=finish)


def _exchange_plug(pieces):
    n = len(pieces)

    def copies(ins, outs, send, recv):
        x, y, c, me, chips = _mesh_place()
        out_cps, in_cps = [], []
        for i in range(n):
            for j, (px, py) in enumerate(chips):
                q = 2 * px + py
                out_cps.append(pltpu.make_async_remote_copy(src_ref=ins[i].at[q], dst_ref=outs[i].at[me], send_sem=send.at[3 * i + j],
                                                            recv_sem=recv.at[3 * i + j], device_id=(px, py, c), device_id_type=MESH))
                in_cps.append(pltpu.make_async_remote_copy(src_ref=ins[i].at[me], dst_ref=outs[i].at[q], send_sem=send.at[3 * i + j],
                                                           recv_sem=recv.at[3 * i + j], device_id=(px, py, c), device_id_type=MESH))
        return out_cps, in_cps, me

    def start(ins, outs, send, recv, loc):
        out_cps, _, me = copies(ins, outs, send, recv)
        for i in range(n):
            pltpu.make_async_copy(ins[i].at[me], outs[i].at[me], loc.at[i]).start()
        for cp in out_cps:
            cp.start()

    def finish(ins, outs, send, recv, loc):
        out_cps, in_cps, me = copies(ins, outs, send, recv)
        for cp in in_cps:
            cp.wait_recv()
        for cp in out_cps:
            cp.wait_send()
        for i in range(n):
            pltpu.make_async_copy(ins[i].at[me], outs[i].at[me], loc.at[i]).wait()

    return dict(ins=list(pieces), out_shape=[jax.ShapeDtypeStruct(a.shape, a.dtype) for a in pieces],
                scratch=[pltpu.SemaphoreType.DMA((3 * n,)), pltpu.SemaphoreType.DMA((3 * n,)), pltpu.SemaphoreType.DMA((n,))],
                start=start, finish=finish)


def _comm_call(plug, name):
    n_in, n_out = len(plug["ins"]), len(plug["out_shape"])

    def body(*refs):
        ins, outs, sems = refs[:n_in], refs[n_in:n_in + n_out], refs[n_in + n_out:]
        plug["start"](ins, outs, *sems)
        plug["finish"](ins, outs, *sems)

    return pl.pallas_call(body, name=name, out_shape=plug["out_shape"], in_specs=[_HBM] * n_in, out_specs=[_HBM] * n_out,
                          scratch_shapes=plug["scratch"])(*plug["ins"])


def _plugged_call(body, plug, first_last, args, *, name, grid, in_specs, out_specs, out_shape, scratch_shapes=(), aliases=None):
    in_specs, out_specs, out_shape, scratch_shapes = list(in_specs), list(out_specs), list(out_shape), list(scratch_shapes)
    aliases = dict(aliases or {})
    if plug is None:
        return pl.pallas_call(body, name=name, grid=grid, in_specs=in_specs, out_specs=out_specs, out_shape=out_shape,
                              scratch_shapes=scratch_shapes, input_output_aliases=aliases)(*args), []
    n_in, n_out, n_sc = len(in_specs), len(out_specs), len(scratch_shapes)
    p_in, p_out = len(plug["ins"]), len(plug["out_shape"])

    def full(*refs):
        ins, refs = refs[:n_in], refs[n_in:]
        pins, refs = refs[:p_in], refs[p_in:]
        outs, refs = refs[:n_out], refs[n_out:]
        pouts, refs = refs[:p_out], refs[p_out:]
        scr, psems = refs[:n_sc], refs[n_sc:]
        first, last = first_last()

        @pl.when(first)
        def _():
            plug["start"](pins, pouts, *psems)

        body(*ins, *outs, *scr)

        @pl.when(last)
        def _():
            plug["finish"](pins, pouts, *psems)

    res = pl.pallas_call(full, name=name, grid=grid, in_specs=in_specs + [_HBM] * p_in, out_specs=out_specs + [_HBM] * p_out,
                         out_shape=out_shape + plug["out_shape"], scratch_shapes=scratch_shapes + plug["scratch"],
                         input_output_aliases=aliases)(*args, *plug["ins"])
    return res[:n_out], res[n_out:]


def _grid_ends(grid):
    def ends():
        first = last = None
        for ax, n in enumerate(grid):
            i = pl.program_id(ax)
            first = (i == 0) if first is None else first & (i == 0)
            last = (i == n - 1) if last is None else last & (i == n - 1)
        return first, last
    return ends


def _pair_presum(pieces, name):
    n, r, cols = pieces.shape
    by_cols = (cols // 2) % LANES == 0
    half_shape = (n, r, cols // 2) if by_cols else (n, r // 2, cols)

    def body(p_ref, o_ref, mine_ref, land_ref, send_sem, recv_sem, local_sem):
        x, y, c = lax.axis_index("x"), lax.axis_index("y"), lax.axis_index("c")

        def half(which):
            if by_cols:
                return p_ref.at[:, :, pl.ds(pl.multiple_of(which * (cols // 2), LANES), cols // 2)]
            return p_ref.at[:, pl.ds(pl.multiple_of(which * (r // 2), 16), r // 2), :]

        push = pltpu.make_async_remote_copy(src_ref=half(1 - c), dst_ref=land_ref, send_sem=send_sem, recv_sem=recv_sem,
                                            device_id=(x, y, 1 - c), device_id_type=MESH)
        own = pltpu.make_async_copy(half(c), mine_ref, local_sem)
        push.start()
        own.start()
        own.wait()
        push.wait_recv()
        for i in range(n):
            o_ref[i] = (mine_ref[i].astype(F32) + land_ref[i].astype(F32)).astype(BF16)
        push.wait_send()

    return pl.pallas_call(
        body, name=name, out_shape=jax.ShapeDtypeStruct(half_shape, BF16), in_specs=[_HBM],
        out_specs=pl.BlockSpec(memory_space=pltpu.VMEM),
        scratch_shapes=[pltpu.VMEM(half_shape, BF16), pltpu.VMEM(half_shape, BF16), pltpu.SemaphoreType.DMA,
                        pltpu.SemaphoreType.DMA, pltpu.SemaphoreType.DMA],
    )(pieces)


def _sibling_exchange(arrs, name):
    n = len(arrs)

    def body(*refs):
        ins, outs = refs[:n], refs[n:2 * n]
        send_sems, recv_sems = refs[2 * n:]
        sibling = (lax.axis_index("x"), lax.axis_index("y"), 1 - lax.axis_index("c"))
        cps = [pltpu.make_async_remote_copy(src_ref=ins[i], dst_ref=outs[i], send_sem=send_sems.at[i], recv_sem=recv_sems.at[i],
                                            device_id=sibling, device_id_type=MESH) for i in range(n)]
        for cp in cps:
            cp.start()
        for cp in cps:
            cp.wait()

    return pl.pallas_call(
        body, name=name, out_shape=[jax.ShapeDtypeStruct(a.shape, a.dtype) for a in arrs], in_specs=[_HBM] * n, out_specs=[_HBM] * n,
        scratch_shapes=[pltpu.SemaphoreType.DMA((n,)), pltpu.SemaphoreType.DMA((n,))],
    )(*arrs)


_SEGS = ((0, AQ, C_Q), (AQ, AKV, C_K), (AQ + AKV, AKV, C_V), (AQ + 2 * AKV, 3 * DN, C_DQKV), (2304, DN, C_DZ),
         (2816, 2 * DN_HEADS, C_BD), (2824, D_MODEL, C_GA), (3848, D_MODEL, C_GD))
SHARD_ROWS = IN_DIM // N_CHIPS


def _to_padded(w4):
    wt = w4.reshape(IN_DIM, w4.shape[2])
    parts = [wt[o:o + n] for o, n, _ in sorted(_SEGS, key=lambda sg: sg[2])]
    return jnp.concatenate(parts + [jnp.zeros((IN_PAD - IN_DIM, wt.shape[1]), wt.dtype)], axis=0)


def _from_padded(gt):
    return jnp.concatenate([gt[ps:ps + n] for _, n, ps in sorted(_SEGS)], axis=0).reshape(N_CHIPS, SHARD_ROWS, gt.shape[1])


def _lane_vec(a):
    return jnp.zeros((1, LANES), F32).at[0, DN_HEADS:2 * DN_HEADS].set(a)


_ROW_SHARDED = ("w_in", "w_out", "ffn_w_down")
_FFN = ("ffn_w_up", "ffn_w_down")
_LATE_MIXER = ("w_attn_branch", "w_dn_branch", "w_out")
_PRESUM = ("w_in", "ffn_w_up")


def _pieces(k, a):
    if a.ndim == 3:
        return a
    if k in _ROW_SHARDED:
        return a.reshape(N_CHIPS, a.shape[0] // N_CHIPS, a.shape[1]).astype(BF16)
    return jnp.transpose(a.reshape(a.shape[0], N_CHIPS, a.shape[1] // N_CHIPS), (1, 0, 2)).astype(BF16)


def _assemble(k, a):
    if k in _ROW_SHARDED:
        return a.reshape(-1, a.shape[2])
    return jnp.transpose(a, (1, 0, 2)).reshape(a.shape[1], -1)


def _device_step(x2, tgt2, mod, p, bsz, shards=None):
    d = D_MODEL
    on_mesh = shards is not None
    p = dict(p)
    sh1, sc1, g1, sh2, sc2, g2 = [mod[:, i * d:(i + 1) * d].reshape(bsz, 1, d) for i in range(N_MOD)]
    alog_v, dt_v = _lane_vec(p["dn_a_log"]), _lane_vec(p["dn_dt_bias"])
    sinks = p["attn_sinks"].reshape(1, AQ_HEADS)
    u1 = _pre_fwd(x2, p["norm_mix_pre"], sc1, sh1, "pre1_fwd")
    if on_mesh:
        proj, got = _mm(u1, p["w_in"], "nt", F32, "mm_proj", _gather_plug(shards["late_mixer"]))
        p.update({k: _assemble(k, a) for k, a in zip(_LATE_MIXER, got)})
    else:
        proj = _mm(u1, p["w_in"], "nt", F32, "mm_proj")
    bias = _bias_build(p["rel_bias"])
    y_attn = _attn_fwd(proj, bias, sinks, bsz)
    qkvn = _dn_prep_fwd(proj, p["dn_conv_w"], bsz)
    bg = _bg_fwd(proj, alog_v, dt_v, bsz)
    nc = x2.shape[0] // bsz // DN_CHUNK
    g_rows = jnp.transpose(bg[:, DN_HEADS:2 * DN_HEADS].reshape(bsz, nc, DN_CHUNK, DN_HEADS), (0, 3, 1, 2))
    o, states, got = _dn_chunk_fwd(qkvn, bg, g_rows, bsz, _gather_plug(shards["ffn"][:1]) if on_mesh else None)
    for k, a in zip(_FFN[:1], got):
        p[k] = _assemble(k, a)
    y_dn = _dn_out_fwd(o, proj, p["dn_norm_w"])
    ya = _mm(y_attn, p["w_attn_branch"], "nn", BF16, "mm_ya")
    yd = _mm(y_dn, p["w_dn_branch"], "nn", BF16, "mm_yd")
    merged = _merge_fwd(proj, ya, yd)
    y1 = _mm(merged, p["w_out"], "nn", F32, "mm_y1")
    h1, u2 = _post_pre_fwd(x2, y1, p["norm_mix_post"], g1, p["norm_ffn_pre"], sc2, sh2, "post1_pre2_fwd")
    if on_mesh:
        up, got = _mm(u2, p["ffn_w_up"], "nn", F32, "mm_up", _gather_plug(shards["ffn"][1:]))
        p["ffn_w_down"] = _assemble("ffn_w_down", got[0])
    else:
        up = _mm(u2, p["ffn_w_up"], "nn", F32, "mm_up")
    act = _ffn_act_fwd(up, p["ffn_conv_w"], bsz)
    y2 = _mm(act, p["ffn_w_down"], "nn", F32, "mm_y2")
    dh2, dy2, g_ffn_post, dg2, sq = _post_loss_bwd(h1, y2, p["norm_ffn_post"], g2, tgt2, "post2_loss_bwd")
    g = {}
    g["norm_ffn_post"] = g_ffn_post
    dact = _mm(dy2, p["ffn_w_down"], "nt", F32, "mm_dact")
    g["ffn_w_down"] = _mm(act, dy2, "tn", BF16, "mm_dwdown")
    dupg, dupv, dcwg, dcwv, got_down = _ffn_act_bwd(
        up, p["ffn_conv_w"], dact, bsz, _exchange_plug([_pieces("ffn_w_down", g["ffn_w_down"])]) if on_mesh else None)
    g["ffn_conv_w"] = jnp.concatenate([dcwg, dcwv], axis=1)
    g["ffn_w_up"] = jnp.concatenate([_mm(u2, dupg, "tn", BF16, "mm_dwup_gate", split=N_CHIPS // 2),
                                     _mm(u2, dupv, "tn", BF16, "mm_dwup_val", split=N_CHIPS // 2)], axis=0)
    du2 = _mm(dupg, p["ffn_w_up"], "nt", F32, "mm_du2_gate", b_kblock=0)
    du2 = _mm(dupv, p["ffn_w_up"], "nt", F32, "mm_du2_val", b_kblock=1, add=du2)
    dh1, dy1, g["norm_ffn_pre"], dsc2, dsh2, g["norm_mix_post"], dg1 = _pre_post_bwd(
        h1, p["norm_ffn_pre"], sc2, sh2, du2, dh2, y1, p["norm_mix_post"], g1, "pre2_post1_bwd")
    dmerged = _mm(dy1, p["w_out"], "nt", BF16, "mm_dmerged")
    g["w_out"] = _mm(merged, dy1, "tn", BF16, "mm_dwout")
    dproj = lax.empty((x2.shape[0], IN_PAD), BF16)
    dproj, dya = _branch_bwd(dproj, proj, ya, dmerged, C_GA, "merge_bwd_attn")
    dproj, dyd = _branch_bwd(dproj, proj, yd, dmerged, C_GD, "merge_bwd_dn")
    dy_attn = _mm(dya, p["w_attn_branch"], "nt", BF16, "mm_dyattn")
    g["w_attn_branch"] = _mm(y_attn, dya, "tn", BF16, "mm_dwab", split=N_CHIPS)
    dy_dn = _mm(dyd, p["w_dn_branch"], "nt", F32, "mm_dydn")
    g["w_dn_branch"] = _mm(y_dn, dyd, "tn", BF16, "mm_dwdb", split=N_CHIPS)
    do, dproj, g["dn_norm_w"] = _dn_out_bwd(dproj, o, proj, p["dn_norm_w"], dy_dn)
    def plug_for(names):
        if not on_mesh:
            return None
        return _exchange_plug([_pair_presum(_pieces(k, g[k]), "presum_" + k) if k in _PRESUM else _pieces(k, g[k]) for k in names])

    early = ("w_out", "w_attn_branch", "w_dn_branch")
    dqkvn, dbg4, got_up = _dn_chunk_bwd(qkvn, bg, g_rows, states, do, bsz, plug_for(_FFN[:1]))
    got_ffn = list(got_up) + list(got_down)
    dproj, g["dn_conv_w"] = _dn_prep_bwd(dproj, proj, p["dn_conv_w"], dqkvn, bsz)
    dproj, dk, dv, dbias, g["attn_sinks"], got_early = _attn_bwd(dproj, proj, bias, sinks, dy_attn, bsz, plug_for(early))
    dproj, g["dn_a_log"], g["dn_dt_bias"] = _bg_bwd(dproj, proj, alog_v, dt_v, dbg4, dk, dv, bsz)
    g["rel_bias"] = _bias_grad(dbias)
    g["w_in"] = _from_padded(_mm(dproj, u1, "tn", BF16, "mm_dwin"))
    if on_mesh:
        du1, got_in = _mm(dproj, p["w_in"], "nn", F32, "mm_du1", plug_for(("w_in",)))
        g.update(zip(_FFN + early + ("w_in",), list(got_ffn) + list(got_early) + list(got_in)))
    else:
        du1 = _mm(dproj, p["w_in"], "nn", F32, "mm_du1")
    dx, g["norm_mix_pre"], dsc1, dsh1 = _pre_bwd(x2, p["norm_mix_pre"], sc1, sh1, du1, dh1, "pre1_bwd")
    dmod = jnp.concatenate([dsh1, dsc1, dg1, dsh2, dsc2, dg2], axis=-1).reshape(bsz, N_MOD * d)
    return sq, dx, dmod, g


_SMALL = (("norm_mix_pre", D_MODEL), ("norm_mix_post", D_MODEL), ("norm_ffn_pre", D_MODEL), ("norm_ffn_post", D_MODEL),
          ("dn_norm_w", DN_HD), ("dn_a_log", LANES), ("dn_dt_bias", LANES), ("attn_sinks", AQ_HEADS * LANES),
          ("rel_bias", AQ_HEADS * LANES), ("dn_conv_w", DN_CONV * 3 * DN), ("ffn_conv_w", FFN_CONV * 2 * D_FF))


def _pack_rows(parts, rows):
    flat = jnp.concatenate([a.reshape(-1) for a in parts])
    return jnp.concatenate([flat, jnp.zeros((rows * LANES - flat.shape[0],), F32)]).reshape(rows, LANES)


def _pad128(a):
    flat = a.reshape(-1)
    n = -(-flat.shape[0] // LANES) * LANES
    return jnp.concatenate([flat, jnp.zeros((n - flat.shape[0],), F32)]) if n != flat.shape[0] else flat


_W_NAMES = ("ada_w", "ada_b", "norm_mix_pre", "norm_mix_post", "norm_ffn_pre", "norm_ffn_post", "w_in", "dn_conv_w", "dn_a_log",
            "dn_dt_bias", "dn_norm_w", "attn_sinks", "rel_bias", "w_attn_branch", "w_dn_branch", "w_out", "ffn_w_up", "ffn_conv_w",
            "ffn_w_down")
_BIG = ("w_in", "w_attn_branch", "w_dn_branch", "w_out", "ffn_w_up", "ffn_w_down")


def kernel(x, c, *rest):
    nw = len(_W_NAMES)
    w = dict(zip(_W_NAMES, rest[:nw]))
    loss_target = rest[nw]
    m = dict(zip(_W_NAMES, rest[nw + 1:2 * nw + 1]))
    v = dict(zip(_W_NAMES, rest[2 * nw + 1:3 * nw + 1]))
    ix, iy, ic = lax.axis_index("x"), lax.axis_index("y"), lax.axis_index("c")
    chip, dev = 2 * ix + iy, 4 * ix + 2 * iy + ic
    bsz, s, d = x.shape
    t = bsz * s
    n_dev = 8

    front_rows = 64
    front = _pack_rows([c, w["dn_conv_w"], w["ffn_conv_w"]], front_rows)
    w_in_t = jnp.swapaxes(w["w_in"][0], 0, 1).astype(BF16)
    w_in_cut = SHARD_ROWS // 32 * 16
    front_all, (w_in_lo,) = _allgather8(front, "ag_front", _gather_plug([w_in_t[:w_in_cut]]))
    front_all = front_all.reshape(n_dev, front_rows * LANES)
    n_c, n_dc, n_fc = bsz * d, DN_CONV * 3 * DN // N_CHIPS, FFN_CONV * 2 * D_FF // N_CHIPS
    c_all = front_all[:, :n_c].reshape(n_dev * bsz, d)
    per_chip = front_all[0::2]
    dn_conv_full = jnp.transpose(per_chip[:, n_c:n_c + n_dc].reshape(N_CHIPS, DN_CONV, -1), (1, 0, 2)).reshape(DN_CONV, 3 * DN)
    ffn_conv_full = jnp.transpose(per_chip[:, n_c + n_dc:n_c + n_dc + n_fc].reshape(N_CHIPS, FFN_CONV, -1), (1, 0, 2)).reshape(FFN_CONV, 2 * D_FF)

    mod_cols = N_MOD * d // N_CHIPS
    ada_b_loc = lax.dynamic_slice(w["ada_b"], (0, chip * mod_cols), (1, mod_cols))
    mod_part = _mod_fwd(c_all, w["ada_w"][0], ada_b_loc)
    mod_all, (w_in_hi,) = _allgather8(mod_part, "ag_mod", _gather_plug([w_in_t[w_in_cut:]]))
    mod_all = mod_all.reshape(n_dev, n_dev * bsz, mod_cols)[0::2]
    mod = jnp.transpose(lax.dynamic_slice(mod_all, (0, dev * bsz, 0), (N_CHIPS, bsz, mod_cols)), (1, 0, 2)).reshape(bsz, N_MOD * d)

    p = {"w_in": _to_padded(jnp.concatenate([w_in_lo, w_in_hi], axis=1))}
    shards = {"late_mixer": [w[k][0].astype(BF16) for k in _LATE_MIXER], "ffn": [w[k][0].astype(BF16) for k in _FFN]}
    for k in ("norm_mix_pre", "norm_mix_post", "norm_ffn_pre", "norm_ffn_post", "dn_norm_w", "attn_sinks"):
        p[k] = w[k]
    p["dn_a_log"], p["dn_dt_bias"], p["rel_bias"] = w["dn_a_log"][0], w["dn_dt_bias"][0], w["rel_bias"]
    p["dn_conv_w"], p["ffn_conv_w"] = dn_conv_full, ffn_conv_full

    sq, dx, dmod, g = _device_step(x.reshape(t, d), loss_target.reshape(t, d), mod, p, bsz, shards)
    loss = lax.psum(0.5 * jnp.sum(sq), ("x", "y", "c"))

    g["dn_a_log"], g["dn_dt_bias"] = g["dn_a_log"].reshape(-1), g["dn_dt_bias"].reshape(-1)
    small_rows = 328
    small = _pack_rows([dmod] + [g[k] for k, _ in _SMALL], small_rows)
    small_all = _allgather8(small, "ag_small").reshape(n_dev, small_rows, LANES)
    n_dm = bsz * N_MOD * d
    dmod_all = small_all.reshape(n_dev, -1)[:, :n_dm].reshape(n_dev * bsz, N_MOD * d)
    tot = _sum_lead(small_all, "sum_small").reshape(-1)
    gs, off = {}, n_dm
    for k, n in _SMALL:
        gs[k] = tot[off:off + n]
        off += n
    grad = {}
    grad["ada_w"], grad["ada_b"] = _ada_grad(c_all, lax.dynamic_slice(dmod_all, (0, chip * mod_cols), (n_dev * bsz, mod_cols)), dmod_all)
    for k in ("norm_mix_pre", "norm_mix_post", "norm_ffn_pre", "norm_ffn_post", "dn_norm_w"):
        grad[k] = gs[k]
    grad["dn_a_log"] = gs["dn_a_log"][DN_HEADS:2 * DN_HEADS]
    grad["dn_dt_bias"] = gs["dn_dt_bias"][DN_HEADS:2 * DN_HEADS]
    grad["attn_sinks"] = gs["attn_sinks"].reshape(AQ_HEADS, LANES)[:, 0]
    grad["rel_bias"] = gs["rel_bias"].reshape(AQ_HEADS, LANES)[:, :REL_BUCKETS].T
    grad["dn_conv_w"] = lax.dynamic_slice(gs["dn_conv_w"].reshape(DN_CONV, 3 * DN), (0, chip * (3 * DN // N_CHIPS)), (DN_CONV, 3 * DN // N_CHIPS))
    grad["ffn_conv_w"] = lax.dynamic_slice(gs["ffn_conv_w"].reshape(FFN_CONV, 2 * D_FF), (0, chip * (2 * D_FF // N_CHIPS)), (FFN_CONV, 2 * D_FF // N_CHIPS))

    mine = [_sum_lead(g[k], "sum_" + k, rows_apart=(k == "w_in")) for k in _BIG]
    theirs = _sibling_exchange(mine, "exchange_cores")

    out = {}
    for k, a, b in zip(_BIG, mine, theirs):
        core = ic.reshape(1).astype(jnp.int32)
        if k == "w_in":
            tr = lambda z: jnp.transpose(z, (2, 0, 1))
            out[k] = [jnp.transpose(r, (1, 2, 0)) for r in _adamw_halves(tr(w[k]), tr(m[k]), tr(v[k]), a, b, core, "adamw_" + k)]
        elif k in _PRESUM:
            out[k] = _adamw_halves(w[k][0], m[k][0], v[k][0], a, b, core, "adamw_" + k)
        else:
            out[k] = _adamw(w[k][0], m[k][0], v[k][0], [a, b], "adamw_" + k)
    out["ada_w"] = _adamw(w["ada_w"][0], m["ada_w"][0], v["ada_w"][0], [grad["ada_w"]], "adamw_ada_w")
    small_names = [k for k in _W_NAMES if k not in _BIG and k != "ada_w"]
    res_small = _adamw_small([w[k] for k in small_names], [m[k] for k in small_names], [v[k] for k in small_names],
                             [grad[k].reshape(w[k].shape) for k in small_names], "adamw_small")
    out.update(zip(small_names, res_small))
    for k in _BIG + ("ada_w",):
        out[k] = [r.reshape(w[k].shape) for r in out[k]]
    grads, deltas, new_m, new_v = ([out[k][i] for k in _W_NAMES] for i in range(4))
    return (loss, dx.reshape(bsz, s, d), *grads, *deltas, *new_m, *new_v)
```

```python
import functools
import math

import numpy as np
import jax
import jax.numpy as jnp
from jax import lax
from jax.experimental import pallas as pl
from jax.experimental.pallas import tpu as pltpu

F32 = jnp.float32
BF16 = jnp.bfloat16
MESH = pl.DeviceIdType.MESH

D_MODEL = 1024
N_MOD = 6
AQ_HEADS, AKV_HEADS, A_HD, WINDOW = 8, 2, 64, 128
REL_BUCKETS, REL_MAX_DIST = 32, 128
DN_HEADS, DN_HD, DN_CONV, DN_CHUNK = 4, 128, 4, 64
D_FF, FFN_CONV = 2816, 3
RMS_EPS, L2_EPS, NEG_INF = 1e-6, 1e-6, -1e30
AQ, AKV, DN = AQ_HEADS * A_HD, AKV_HEADS * A_HD, DN_HEADS * DN_HD
IN_DIM = AQ + 2 * AKV + 3 * DN + DN + 2 * DN_HEADS + 2 * D_MODEL
C_DQKV, C_Q, C_DZ, C_GA, C_GD, C_K, C_V, C_BD = 0, 1536, 2048, 2560, 3584, 4608, 4736, 4864
IN_PAD = 4992
LANES = 128
N_CHIPS = 4

ADAM_LR, ADAM_B1, ADAM_B2, ADAM_EPS, ADAM_WD, ADAM_STEP = 0.001, 0.9, 0.999, 1e-08, 0.01, 10


def _pick(n, cap):
    best = None
    for t in range(LANES, cap + 1, LANES):
        if n % t == 0:
            best = t
    return best if best is not None else n


def _vspec(shape, index_map):
    return pl.BlockSpec(shape, index_map)


MM_VMEM_BUDGET = 40 * 2 ** 20
GRID_STEP_S = 0.35e-6
HBM_BYTES_PER_S = 3.0e12
MXU_FLOPS_PER_S = 9.0e14
MXU_DIM = 256


def _mm_tiles(m, n, k, mode, in_bytes, out_bytes, split=1):
    best = None
    for tm in [t for t in range(LANES, m + 1, LANES) if m % t == 0]:
        for tn in [t for t in range(LANES, n // split + 1, LANES) if (n // split) % t == 0]:
            a, b, o = k * tm * in_bytes, k * tn * in_bytes, tm * tn * out_bytes
            if 2 * (a + b + o) + (a if mode == "tn" else 0) > MM_VMEM_BUDGET:
                continue
            hbm_s = (m * k * in_bytes + (m // tm) * n * k * in_bytes + m * n * out_bytes) / HBM_BYTES_PER_S
            mxu_s = 2 * m * n * k / (MXU_FLOPS_PER_S * min(1.0, tm / MXU_DIM) * min(1.0, tn / MXU_DIM))
            cost = (m // tm) * (n // tn) * GRID_STEP_S + max(hbm_s, mxu_s)
            if best is None or cost < best[0]:
                best = (cost, tm, tn)
    return best[1], best[2]


def _mm(a, b, mode, out_dtype, name, plug=None, split=1, b_kblock=0, add=None):
    if mode == "nn":
        (m, k), n = a.shape, b.shape[1]
        dims = (((1,), (0,)), ((), ()))
    elif mode == "nt":
        (m, k), n = a.shape, b.shape[0]
        dims = (((1,), (1,)), ((), ()))
    else:
        (k, m), n = a.shape, b.shape[1]
        dims = (((0,), (0,)), ((), ()))
    tm, tn = _mm_tiles(m, n, k, mode, a.dtype.itemsize, jnp.dtype(out_dtype).itemsize, split)
    if mode == "tn":
        a_spec = _vspec((k, tm), lambda i, j: (0, i))
    else:
        a_spec = _vspec((tm, k), lambda i, j: (i, 0))
    if mode == "nt":
        b_spec = _vspec((tn, k), lambda i, j: (j, b_kblock))
    else:
        b_spec = _vspec((k, tn), lambda i, j: (0, j))
    in_specs, args = [a_spec, b_spec], (a, b)
    if add is not None:
        in_specs, args = in_specs + [_vspec((tm, tn), lambda i, j: (i, j))], (a, b, add)

    def body(a_ref, b_ref, *rest):
        o_ref = rest[-1]
        acc = lax.dot_general(a_ref[...].astype(BF16), b_ref[...].astype(BF16), dims, preferred_element_type=F32)
        if add is not None:
            acc = acc + rest[0][...]
        o_ref[...] = acc.astype(out_dtype).reshape(o_ref.shape)

    grid = (m // tm, n // tn)
    if split == 1:
        out_spec, out_shape = _vspec((tm, tn), lambda i, j: (i, j)), (m, n)
    else:
        per = n // split // tn
        out_spec, out_shape = _vspec((1, tm, tn), lambda i, j: (j // per, i, j % per)), (split, m, n // split)
    (out,), extra = _plugged_call(body, plug, _grid_ends(grid), args, name=name, grid=grid, in_specs=in_specs,
                                  out_specs=[out_spec], out_shape=[jax.ShapeDtypeStruct(out_shape, out_dtype)])
    return out if plug is None else (out, extra)


def _rms(x, w):
    return (x * lax.rsqrt(jnp.mean(x * x, axis=-1, keepdims=True) + RMS_EPS)) * w


def _pre_f(x, w, sc, sh):
    return _rms(x, w) * (1.0 + sc) + sh


def _post_f(y, w, g):
    return g * _rms(y, w)


def _tok_grid(t, bsz, ts):
    nt = t // bsz // ts
    return nt, (bsz, nt)


def _pre_fwd(x, w, sc, sh, name, ts=512):
    t, d = x.shape
    bsz = sc.shape[0]
    nt, grid = _tok_grid(t, bsz, ts)
    row = _vspec((ts, d), lambda b, i: (b * nt + i, 0))
    vec = _vspec((1, d), lambda b, i: (0, 0))
    bvec = _vspec((1, 1, d), lambda b, i: (b, 0, 0))

    def body(x_ref, w_ref, sc_ref, sh_ref, u_ref):
        u_ref[...] = _pre_f(x_ref[...], w_ref[...], sc_ref[0], sh_ref[0]).astype(BF16)

    return pl.pallas_call(body, name=name, grid=grid, in_specs=[row, vec, bvec, bvec], out_specs=row,
                          out_shape=jax.ShapeDtypeStruct((t, d), BF16))(x, w, sc, sh)


def _pre_bwd(x, w, sc, sh, du, dres, name, ts=512):
    t, d = x.shape
    bsz = sc.shape[0]
    nt, grid = _tok_grid(t, bsz, ts)
    row = _vspec((ts, d), lambda b, i: (b * nt + i, 0))
    vec = _vspec((1, d), lambda b, i: (0, 0))
    bvec = _vspec((1, 1, d), lambda b, i: (b, 0, 0))

    def body(x_ref, w_ref, sc_ref, sh_ref, du_ref, dres_ref, dx_ref, dw_ref, dsc_ref, dsh_ref):
        b, i = pl.program_id(0), pl.program_id(1)
        _, vjp = jax.vjp(_pre_f, x_ref[...], w_ref[...], sc_ref[0], sh_ref[0])
        dx, dw, dsc, dsh = vjp(du_ref[...])
        dx_ref[...] = dres_ref[...] + dx

        @pl.when((b == 0) & (i == 0))
        def _():
            dw_ref[...] = jnp.zeros_like(dw_ref)

        @pl.when(i == 0)
        def _():
            dsc_ref[...] = jnp.zeros_like(dsc_ref)
            dsh_ref[...] = jnp.zeros_like(dsh_ref)

        dw_ref[...] += dw
        dsc_ref[0] += dsc
        dsh_ref[0] += dsh

    return pl.pallas_call(
        body, name=name, grid=grid, in_specs=[row, vec, bvec, bvec, row, row], out_specs=[row, vec, bvec, bvec],
        out_shape=[jax.ShapeDtypeStruct((t, d), F32), jax.ShapeDtypeStruct((1, d), F32),
                   jax.ShapeDtypeStruct((bsz, 1, d), F32), jax.ShapeDtypeStruct((bsz, 1, d), F32)],
    )(x, w, sc, sh, du, dres)


def _accumulate(ref, val, first):
    @pl.when(first)
    def _():
        ref[...] = jnp.zeros_like(ref)

    ref[...] += val.reshape(ref.shape)


def _post_pre_fwd(res, y, w_post, g, w_pre, sc, sh, name, ts=512):
    t, d = y.shape
    bsz = g.shape[0]
    nt, grid = _tok_grid(t, bsz, ts)
    row = _vspec((ts, d), lambda b, i: (b * nt + i, 0))
    vec = _vspec((1, d), lambda b, i: (0, 0))
    bvec = _vspec((1, 1, d), lambda b, i: (b, 0, 0))

    def body(res_ref, y_ref, wp_ref, g_ref, w_ref, sc_ref, sh_ref, h_ref, u_ref):
        h = res_ref[...] + _post_f(y_ref[...], wp_ref[...], g_ref[0])
        h_ref[...] = h
        u_ref[...] = _pre_f(h, w_ref[...], sc_ref[0], sh_ref[0]).astype(BF16)

    return pl.pallas_call(body, name=name, grid=grid, in_specs=[row, row, vec, bvec, vec, bvec, bvec], out_specs=[row, row],
                          out_shape=[jax.ShapeDtypeStruct((t, d), F32), jax.ShapeDtypeStruct((t, d), BF16)],
                          )(res, y, w_post, g, w_pre, sc, sh)


def _post_loss_bwd(res, y, w, g, tgt, name, ts=512):
    t, d = y.shape
    bsz = g.shape[0]
    nt, grid = _tok_grid(t, bsz, ts)
    row = _vspec((ts, d), lambda b, i: (b * nt + i, 0))
    vec = _vspec((1, d), lambda b, i: (0, 0))
    bvec = _vspec((1, 1, d), lambda b, i: (b, 0, 0))

    def body(res_ref, y_ref, w_ref, g_ref, tgt_ref, dh_ref, dy_ref, dw_ref, dg_ref, sq_ref):
        b, i = pl.program_id(0), pl.program_id(1)
        part, vjp = jax.vjp(_post_f, y_ref[...], w_ref[...], g_ref[0])
        e = res_ref[...] + part - tgt_ref[...]
        dh = e * (1.0 / d)
        dh_ref[...] = dh
        dy, dw, dg = vjp(dh)
        dy_ref[...] = dy.astype(BF16)
        _accumulate(dw_ref, dw, (b == 0) & (i == 0))
        _accumulate(dg_ref, dg, i == 0)
        _accumulate(sq_ref, jnp.sum(e * e, axis=0, keepdims=True) * (1.0 / d), (b == 0) & (i == 0))

    return pl.pallas_call(
        body, name=name, grid=grid, in_specs=[row, row, vec, bvec, row], out_specs=[row, row, vec, bvec, vec],
        out_shape=[jax.ShapeDtypeStruct((t, d), F32), jax.ShapeDtypeStruct((t, d), BF16), jax.ShapeDtypeStruct((1, d), F32),
                   jax.ShapeDtypeStruct((bsz, 1, d), F32), jax.ShapeDtypeStruct((1, d), F32)],
    )(res, y, w, g, tgt)


def _pre_post_bwd(x, w, sc, sh, du, dres, y, w_post, g, name, ts=512):
    t, d = x.shape
    bsz = sc.shape[0]
    nt, grid = _tok_grid(t, bsz, ts)
    row = _vspec((ts, d), lambda b, i: (b * nt + i, 0))
    vec = _vspec((1, d), lambda b, i: (0, 0))
    bvec = _vspec((1, 1, d), lambda b, i: (b, 0, 0))

    def body(x_ref, w_ref, sc_ref, sh_ref, du_ref, dres_ref, y_ref, wp_ref, g_ref,
             dx_ref, dy_ref, dw_ref, dsc_ref, dsh_ref, dwp_ref, dg_ref):
        b, i = pl.program_id(0), pl.program_id(1)
        _, vjp = jax.vjp(_pre_f, x_ref[...], w_ref[...], sc_ref[0], sh_ref[0])
        dx, dw, dsc, dsh = vjp(du_ref[...])
        dx = dres_ref[...] + dx
        dx_ref[...] = dx
        _, vjp_post = jax.vjp(_post_f, y_ref[...], wp_ref[...], g_ref[0])
        dy, dwp, dg = vjp_post(dx)
        dy_ref[...] = dy.astype(BF16)
        first = (b == 0) & (i == 0)
        _accumulate(dw_ref, dw, first)
        _accumulate(dwp_ref, dwp, first)
        _accumulate(dsc_ref, dsc, i == 0)
        _accumulate(dsh_ref, dsh, i == 0)
        _accumulate(dg_ref, dg, i == 0)

    v1, vb = jax.ShapeDtypeStruct((1, d), F32), jax.ShapeDtypeStruct((bsz, 1, d), F32)
    return pl.pallas_call(
        body, name=name, grid=grid, in_specs=[row, vec, bvec, bvec, row, row, row, vec, bvec],
        out_specs=[row, row, vec, bvec, bvec, vec, bvec],
        out_shape=[jax.ShapeDtypeStruct((t, d), F32), jax.ShapeDtypeStruct((t, d), BF16), v1, vb, vb, v1, vb],
    )(x, w, sc, sh, du, dres, y, w_post, g)


def _merge_f(ga, gd, ya, yd):
    return jax.nn.sigmoid(ga) * ya + jax.nn.sigmoid(gd) * yd


_MW = 512


def _merge_fwd(proj, ya, yd, ts=512):
    t, d = ya.shape
    blk = _vspec((ts, _MW), lambda i, j: (i, j))
    ga = _vspec((ts, _MW), lambda i, j: (i, C_GA // _MW + j))
    gd = _vspec((ts, _MW), lambda i, j: (i, C_GD // _MW + j))

    def body(ga_ref, gd_ref, ya_ref, yd_ref, o_ref):
        o_ref[...] = _merge_f(ga_ref[...], gd_ref[...], ya_ref[...].astype(F32), yd_ref[...].astype(F32)).astype(BF16)

    return pl.pallas_call(body, name="merge_fwd", grid=(t // ts, d // _MW), in_specs=[ga, gd, blk, blk], out_specs=blk,
                          out_shape=jax.ShapeDtypeStruct((t, d), BF16))(proj, proj, ya, yd)


_ANY = pl.BlockSpec(memory_space=pl.ANY)


def _branch_bwd(dproj, proj, y, dm, col0, name, ts=512):
    t, d = y.shape
    blk = _vspec((ts, _MW), lambda i, j: (i, j))
    gate = _vspec((ts, _MW), lambda i, j: (i, col0 // _MW + j))

    def body(buf_ref, g_ref, y_ref, dm_ref, dg_ref, dy_ref):
        del buf_ref
        _, vjp = jax.vjp(lambda g, yy: jax.nn.sigmoid(g) * yy, g_ref[...], y_ref[...].astype(F32))
        dg, dy = vjp(dm_ref[...].astype(F32))
        dg_ref[...] = dg.astype(BF16)
        dy_ref[...] = dy.astype(BF16)

    return pl.pallas_call(body, name=name, grid=(t // ts, d // _MW), in_specs=[_ANY, gate, blk, blk], out_specs=[gate, blk],
                          out_shape=[jax.ShapeDtypeStruct(dproj.shape, BF16), jax.ShapeDtypeStruct((t, d), BF16)],
                          input_output_aliases={0: 0})(dproj, proj, y, dm)


def _shift_down(x, s):
    if s == 0:
        return x
    r = lax.broadcasted_iota(jnp.int32, x.shape, 0)
    return jnp.where(r >= s, pltpu.roll(x, s, 0), 0.0)


def _shift_up(x, s):
    if s == 0:
        return x
    n = x.shape[0]
    r = lax.broadcasted_iota(jnp.int32, x.shape, 0)
    return jnp.where(r < n - s, pltpu.roll(x, n - s, 0), 0.0)


def _conv_fwd(x, w, k):
    out = None
    for j in range(k):
        term = w[j:j + 1, :] * _shift_down(x, k - 1 - j)
        out = term if out is None else out + term
    return out


def _conv_bwd(x, w, dc, k):
    dx = None
    dws = []
    for j in range(k):
        up = _shift_up(dc, k - 1 - j)
        term = w[j:j + 1, :] * up
        dx = term if dx is None else dx + term
        dws.append(jnp.sum(up * x, axis=0, keepdims=True))
    return dx, jnp.concatenate(dws, axis=0)


def _geglu_f(gate, val):
    return jax.nn.gelu(gate, approximate=True) * val


_FW = 256


def _ffn_act_fwd(up, conv_w, bsz):
    t = up.shape[0]
    s = t // bsz
    nj = D_FF // _FW
    xg = _vspec((s, _FW), lambda b, j: (b, j))
    xv = _vspec((s, _FW), lambda b, j: (b, nj + j))
    wg = _vspec((FFN_CONV, _FW), lambda b, j: (0, j))
    wv = _vspec((FFN_CONV, _FW), lambda b, j: (0, nj + j))

    def body(xg_ref, xv_ref, wg_ref, wv_ref, o_ref):
        gate = _conv_fwd(xg_ref[...], wg_ref[...], FFN_CONV)
        val = _conv_fwd(xv_ref[...], wv_ref[...], FFN_CONV)
        o_ref[...] = _geglu_f(gate, val).astype(BF16)

    return pl.pallas_call(body, name="ffn_act_fwd", grid=(bsz, nj), in_specs=[xg, xv, wg, wv],
                          out_specs=_vspec((s, _FW), lambda b, j: (b, j)),
                          out_shape=jax.ShapeDtypeStruct((t, D_FF), BF16))(up, up, conv_w, conv_w)


def _ffn_act_bwd(up, conv_w, dact, bsz, plug=None):
    t = up.shape[0]
    s = t // bsz
    nj = D_FF // _FW
    xg = _vspec((s, _FW), lambda j, b: (b, j))
    xv = _vspec((s, _FW), lambda j, b: (b, nj + j))
    wg = _vspec((FFN_CONV, _FW), lambda j, b: (0, j))
    wv = _vspec((FFN_CONV, _FW), lambda j, b: (0, nj + j))
    da = _vspec((s, _FW), lambda j, b: (b, j))
    dwo = _vspec((FFN_CONV, _FW), lambda j, b: (0, j))

    def body(xg_ref, xv_ref, wg_ref, wv_ref, da_ref, dxg_ref, dxv_ref, dwg_ref, dwv_ref):
        b = pl.program_id(1)
        xg_, xv_, wg_, wv_ = xg_ref[...], xv_ref[...], wg_ref[...], wv_ref[...]
        gate = _conv_fwd(xg_, wg_, FFN_CONV)
        val = _conv_fwd(xv_, wv_, FFN_CONV)
        _, vjp = jax.vjp(_geglu_f, gate, val)
        dgate, dval = vjp(da_ref[...])
        dxg, dwg = _conv_bwd(xg_, wg_, dgate, FFN_CONV)
        dxv, dwv = _conv_bwd(xv_, wv_, dval, FFN_CONV)
        dxg_ref[...] = dxg.astype(BF16)
        dxv_ref[...] = dxv.astype(BF16)

        @pl.when(b == 0)
        def _():
            dwg_ref[...] = jnp.zeros_like(dwg_ref)
            dwv_ref[...] = jnp.zeros_like(dwv_ref)

        dwg_ref[...] += dwg
        dwv_ref[...] += dwv

    outs, extra = _plugged_call(
        body, plug, _grid_ends((nj, bsz)), (up, up, conv_w, conv_w, dact), name="ffn_act_bwd", grid=(nj, bsz),
        in_specs=[xg, xv, wg, wv, da], out_specs=[da, da, dwo, dwo],
        out_shape=[jax.ShapeDtypeStruct((t, D_FF), BF16), jax.ShapeDtypeStruct((t, D_FF), BF16),
                   jax.ShapeDtypeStruct((FFN_CONV, D_FF), F32), jax.ShapeDtypeStruct((FFN_CONV, D_FF), F32)])
    return (*outs, extra)


def _bucket_table():
    qi = np.arange(WINDOW)[:, None]
    kj = np.arange(2 * WINDOW)[None, :]
    dist = WINDOW + qi - kj
    dc = np.maximum(dist, 0)
    max_exact = REL_BUCKETS // 2
    scaled = np.log(np.maximum(dc, 1).astype(np.float32) / np.float32(max_exact)) / np.float32(math.log(REL_MAX_DIST / max_exact))
    large = max_exact + (scaled.astype(np.float32) * np.float32(REL_BUCKETS - max_exact)).astype(np.int32)
    large = np.minimum(large, REL_BUCKETS - 1)
    bucket = np.where(dc < max_exact, dc, large).astype(np.int32)
    in_band = ((dist >= 0) & (dist < WINDOW)).astype(np.int32)
    return bucket, in_band


def _bias_build(rel_bias):
    bucket, _ = _bucket_table()

    def body(rb_ref, idx_ref, o_ref):
        h = pl.program_id(0)
        idx = idx_ref[...]
        acc = jnp.zeros(idx.shape, F32)
        for r in range(REL_BUCKETS):
            acc = jnp.where(idx == r, rb_ref[r, h], acc)
        o_ref[0] = acc

    return pl.pallas_call(
        body, name="bias_build", grid=(AQ_HEADS,),
        in_specs=[pl.BlockSpec(memory_space=pltpu.SMEM), _vspec((WINDOW, 2 * WINDOW), lambda h: (0, 0))],
        out_specs=_vspec((1, WINDOW, 2 * WINDOW), lambda h: (h, 0, 0)),
        out_shape=jax.ShapeDtypeStruct((AQ_HEADS, WINDOW, 2 * WINDOW), F32),
    )(rel_bias, jnp.asarray(bucket))


def _bias_grad(dbias):
    bucket, _ = _bucket_table()

    def body(db_ref, idx_ref, o_ref):
        idx = idx_ref[...]
        db = db_ref[0]
        lane = lax.broadcasted_iota(jnp.int32, (1, LANES), 1)
        acc = jnp.zeros((1, LANES), F32)
        for r in range(REL_BUCKETS):
            s = jnp.sum(jnp.sum(jnp.where(idx == r, db, 0.0), axis=1, keepdims=True), axis=0, keepdims=True)
            acc = jnp.where(lane == r, s, acc)
        o_ref[0] = acc

    return pl.pallas_call(
        body, name="bias_grad", grid=(AQ_HEADS,),
        in_specs=[_vspec((1, WINDOW, 2 * WINDOW), lambda h: (h, 0, 0)), _vspec((WINDOW, 2 * WINDOW), lambda h: (0, 0))],
        out_specs=_vspec((1, 1, LANES), lambda h: (h, 0, 0)),
        out_shape=jax.ShapeDtypeStruct((AQ_HEADS, 1, LANES), F32),
    )(dbias, jnp.asarray(bucket))


def _attn_mask(n):
    qi = lax.broadcasted_iota(jnp.int32, (WINDOW, 2 * WINDOW), 0)
    kj = lax.broadcasted_iota(jnp.int32, (WINDOW, 2 * WINDOW), 1)
    dist = WINDOW + qi - kj
    band = (dist >= 0) & (dist < WINDOW)
    return band & ((kj >= WINDOW) | (n > 0))


def _attn_probs(qk, bias, sink, mask):
    s = jnp.where(mask, qk * (A_HD ** -0.5) + bias, NEG_INF)
    m = jnp.maximum(jnp.max(s, axis=-1, keepdims=True), sink)
    p = jnp.exp(s - m)
    es = jnp.exp(sink - m)
    inv = 1.0 / (jnp.sum(p, axis=-1, keepdims=True) + es)
    return p * inv, es * inv


def _attn_fwd(proj, bias, sinks, bsz):
    t = proj.shape[0]
    s = t // bsz
    nb = s // WINDOW
    grp = AQ_HEADS // AKV_HEADS

    def body(q_ref, k_ref, v_ref, bias_ref, sink_ref, y_ref, kp_ref, vp_ref):
        kp_ref[0:WINDOW, :] = jnp.zeros((WINDOW, LANES), BF16)
        vp_ref[0:WINDOW, :] = jnp.zeros((WINDOW, LANES), BF16)
        kp_ref[WINDOW:, :] = k_ref[...].astype(BF16)
        vp_ref[WINDOW:, :] = v_ref[...].astype(BF16)

        def blk(n, carry):
            r0 = pl.multiple_of(n * WINDOW, WINDOW)
            mask = _attn_mask(n)
            kband = kp_ref[pl.ds(r0, 2 * WINDOW), :]
            vband = vp_ref[pl.ds(r0, 2 * WINDOW), :]
            qb = q_ref[pl.ds(r0, WINDOW), :].astype(BF16)
            heads = range(AQ_HEADS)
            hsl = lambda h: slice(h * A_HD, (h + 1) * A_HD)
            kbs = [kband[:, hsl(kv)] for kv in range(AKV_HEADS)]
            vbs = [vband[:, hsl(kv)] for kv in range(AKV_HEADS)]
            qks = [lax.dot_general(qb[:, hsl(h)], kbs[h // grp], _NT, preferred_element_type=F32) for h in heads]
            probs = [_attn_probs(qks[h], bias_ref[h], sink_ref[0, h], mask)[0] for h in heads]
            outs = [jnp.dot(probs[h].astype(BF16), vbs[h // grp], preferred_element_type=F32) for h in heads]
            y_ref[pl.ds(r0, WINDOW), :] = jnp.concatenate(outs, axis=1).astype(BF16)
            return carry

        lax.fori_loop(0, nb, blk, 0)

    return pl.pallas_call(
        body, name="attn_fwd", grid=(bsz,),
        in_specs=[_vspec((s, AQ), lambda b: (b, C_Q // AQ)), _vspec((s, AKV), lambda b: (b, C_K // AKV)),
                  _vspec((s, AKV), lambda b: (b, C_V // AKV)),
                  _vspec((AQ_HEADS, WINDOW, 2 * WINDOW), lambda b: (0, 0, 0)), pl.BlockSpec(memory_space=pltpu.SMEM)],
        out_specs=_vspec((s, AQ), lambda b: (b, 0)), out_shape=jax.ShapeDtypeStruct((t, AQ), BF16),
        scratch_shapes=[pltpu.VMEM((s + WINDOW, LANES), BF16), pltpu.VMEM((s + WINDOW, LANES), BF16)],
    )(proj, proj, proj, bias, sinks)


def _attn_bwd(dproj, proj, bias, sinks, dy, bsz, plug=None):
    t = proj.shape[0]
    s = t // bsz
    nb = s // WINDOW
    grp = AQ_HEADS // AKV_HEADS
    scale = A_HD ** -0.5

    def body(q_ref, k_ref, v_ref, bias_ref, sink_ref, dy_ref, dq_ref, dk_ref, dv_ref, dbias_ref, dsink_ref,
             kp_ref, vp_ref, dkp_ref, dvp_ref):
        b = pl.program_id(0)
        kp_ref[0:WINDOW, :] = jnp.zeros((WINDOW, LANES), BF16)
        vp_ref[0:WINDOW, :] = jnp.zeros((WINDOW, LANES), BF16)
        kp_ref[WINDOW:, :] = k_ref[...].astype(BF16)
        vp_ref[WINDOW:, :] = v_ref[...].astype(BF16)
        dkp_ref[...] = jnp.zeros_like(dkp_ref)
        dvp_ref[...] = jnp.zeros_like(dvp_ref)

        @pl.when(b == 0)
        def _():
            dbias_ref[...] = jnp.zeros_like(dbias_ref)
            dsink_ref[...] = jnp.zeros_like(dsink_ref)

        def blk(n, carry):
            r0 = pl.multiple_of(n * WINDOW, WINDOW)
            mask = _attn_mask(n)
            kband = kp_ref[pl.ds(r0, 2 * WINDOW), :]
            vband = vp_ref[pl.ds(r0, 2 * WINDOW), :]
            qb = q_ref[pl.ds(r0, WINDOW), :].astype(BF16)
            dyb = dy_ref[pl.ds(r0, WINDOW), :].astype(BF16)
            heads = range(AQ_HEADS)
            hsl = lambda h: slice(h * A_HD, (h + 1) * A_HD)
            kbs = [kband[:, hsl(kv)] for kv in range(AKV_HEADS)]
            vbs = [vband[:, hsl(kv)] for kv in range(AKV_HEADS)]
            qhs = [qb[:, hsl(h)] for h in heads]
            dyhs = [dyb[:, hsl(h)] for h in heads]
            qks = [lax.dot_general(qhs[h], kbs[h // grp], _NT, preferred_element_type=F32) for h in heads]
            dprobs = [lax.dot_general(dyhs[h], vbs[h // grp], _NT, preferred_element_type=F32) for h in heads]
            pbs, dsbs = [], []
            for h in heads:
                probs, psink = _attn_probs(qks[h], bias_ref[h], sink_ref[0, h], mask)
                rowdot = jnp.sum(probs * dprobs[h], axis=-1, keepdims=True)
                ds = probs * (dprobs[h] - rowdot)
                dbias_ref[h] += ds
                dsink_ref[h] += jnp.sum(-psink * rowdot, axis=0, keepdims=True) + jnp.zeros((1, LANES), F32)
                pbs.append(probs.astype(BF16))
                dsbs.append(ds.astype(BF16))
            dvhs = [lax.dot_general(pbs[h], dyhs[h], _TN, preferred_element_type=F32) for h in heads]
            dqs = [jnp.dot(dsbs[h], kbs[h // grp], preferred_element_type=F32) * scale for h in heads]
            dkhs = [lax.dot_general(dsbs[h], qhs[h], _TN, preferred_element_type=F32) * scale for h in heads]
            dks = [sum(dkhs[kv * grp + 1:(kv + 1) * grp], dkhs[kv * grp]) for kv in range(AKV_HEADS)]
            dvs = [sum(dvhs[kv * grp + 1:(kv + 1) * grp], dvhs[kv * grp]) for kv in range(AKV_HEADS)]
            dq_ref[pl.ds(r0, WINDOW), :] = jnp.concatenate(dqs, axis=1).astype(BF16)
            dkp_ref[pl.ds(r0, 2 * WINDOW), :] += jnp.concatenate(dks, axis=1)
            dvp_ref[pl.ds(r0, 2 * WINDOW), :] += jnp.concatenate(dvs, axis=1)
            return carry

        lax.fori_loop(0, nb, blk, 0)
        dk_ref[...] = dkp_ref[WINDOW:, :].astype(BF16)
        dv_ref[...] = dvp_ref[WINDOW:, :].astype(BF16)

    kvs = jax.ShapeDtypeStruct((t, AKV), BF16)
    outs, extra = _plugged_call(
        lambda buf_ref, *refs: body(*refs), plug, _grid_ends((bsz,)), (dproj, proj, proj, proj, bias, sinks, dy),
        name="attn_bwd", grid=(bsz,),
        in_specs=[_ANY, _vspec((s, AQ), lambda b: (b, C_Q // AQ)), _vspec((s, AKV), lambda b: (b, C_K // AKV)),
                  _vspec((s, AKV), lambda b: (b, C_V // AKV)),
                  _vspec((AQ_HEADS, WINDOW, 2 * WINDOW), lambda b: (0, 0, 0)), pl.BlockSpec(memory_space=pltpu.SMEM),
                  _vspec((s, AQ), lambda b: (b, 0))],
        out_specs=[_vspec((s, AQ), lambda b: (b, C_Q // AQ)), _vspec((s, AKV), lambda b: (b, 0)), _vspec((s, AKV), lambda b: (b, 0)),
                   _vspec((AQ_HEADS, WINDOW, 2 * WINDOW), lambda b: (0, 0, 0)), _vspec((AQ_HEADS, 1, LANES), lambda b: (0, 0, 0))],
        out_shape=[jax.ShapeDtypeStruct(dproj.shape, BF16), kvs, kvs,
                   jax.ShapeDtypeStruct((AQ_HEADS, WINDOW, 2 * WINDOW), F32), jax.ShapeDtypeStruct((AQ_HEADS, 1, LANES), F32)],
        scratch_shapes=[pltpu.VMEM((s + WINDOW, LANES), BF16), pltpu.VMEM((s + WINDOW, LANES), BF16),
                        pltpu.VMEM((s + WINDOW, LANES), F32), pltpu.VMEM((s + WINDOW, LANES), F32)],
        aliases={0: 0})
    return (*outs, extra)


def _dn_act_f(c, is_qk):
    a = jax.nn.silu(c)
    outs = []
    for h in range(DN_HEADS):
        ah = a[:, h * DN_HD:(h + 1) * DN_HD]
        nh = ah * lax.rsqrt(jnp.sum(ah * ah, axis=-1, keepdims=True) + L2_EPS)
        outs.append(jnp.where(is_qk, nh, ah))
    return jnp.concatenate(outs, axis=1)


def _dn_prep_fwd(proj, conv_w, bsz):
    t = proj.shape[0]
    s = t // bsz
    blk = _vspec((s, DN), lambda b, j: (b, j))
    wsp = _vspec((DN_CONV, DN), lambda b, j: (0, j))

    def body(x_ref, w_ref, o_ref):
        j = pl.program_id(1)
        o_ref[...] = _dn_act_f(_conv_fwd(x_ref[...], w_ref[...], DN_CONV), j < 2)

    return pl.pallas_call(body, name="dn_prep_fwd", grid=(bsz, 3), in_specs=[blk, wsp], out_specs=blk,
                          out_shape=jax.ShapeDtypeStruct((t, 3 * DN), F32))(proj, conv_w)


def _dn_prep_bwd(dproj, proj, conv_w, dqkvn, bsz):
    t = proj.shape[0]
    s = t // bsz
    blk = _vspec((s, DN), lambda j, b: (b, j))
    wsp = _vspec((DN_CONV, DN), lambda j, b: (0, j))

    def body(buf_ref, x_ref, w_ref, d_ref, dx_ref, dw_ref):
        del buf_ref
        j, b = pl.program_id(0), pl.program_id(1)
        x, w = x_ref[...], w_ref[...]
        c = _conv_fwd(x, w, DN_CONV)
        _, vjp = jax.vjp(lambda cc: _dn_act_f(cc, j < 2), c)
        (dc,) = vjp(d_ref[0])
        dx, dw = _conv_bwd(x, w, dc, DN_CONV)
        dx_ref[...] = dx.astype(BF16)

        @pl.when(b == 0)
        def _():
            dw_ref[...] = jnp.zeros_like(dw_ref)

        dw_ref[...] += dw

    return pl.pallas_call(
        body, name="dn_prep_bwd", grid=(3, bsz),
        in_specs=[_ANY, blk, wsp, _vspec((1, s, DN), lambda j, b: (j, b, 0))], out_specs=[blk, wsp],
        out_shape=[jax.ShapeDtypeStruct(dproj.shape, BF16), jax.ShapeDtypeStruct((DN_CONV, 3 * DN), F32)],
        input_output_aliases={0: 0},
    )(dproj, proj, conv_w, dqkvn)


def _bg_f(x, alog, dt):
    lane = lax.broadcasted_iota(jnp.int32, x.shape, 1)
    beta = jax.nn.sigmoid(x)
    g = -jnp.exp(alog) * jax.nn.softplus(x + dt)
    return jnp.where(lane < DN_HEADS, beta, jnp.where(lane < 2 * DN_HEADS, g, 0.0))


def _bg_fwd(proj, alog, dt, bsz):
    t = proj.shape[0]
    s = t // bsz
    vec = _vspec((1, LANES), lambda b: (0, 0))

    def body(x_ref, a_ref, d_ref, o_ref):
        o_ref[...] = _bg_f(x_ref[...], a_ref[...], d_ref[...])

    return pl.pallas_call(body, name="bg_fwd", grid=(bsz,), in_specs=[_vspec((s, LANES), lambda b: (b, C_BD // LANES)), vec, vec],
                          out_specs=_vspec((s, LANES), lambda b: (b, 0)), out_shape=jax.ShapeDtypeStruct((t, LANES), F32))(proj, alog, dt)


def _bg_bwd(dproj, proj, alog, dt, dbg4, dk, dv, bsz):
    t = proj.shape[0]
    s = t // bsz
    vec = _vspec((1, LANES), lambda b: (0, 0))
    kv = _vspec((s, AKV), lambda b: (b, 0))
    tail = 3 * LANES

    def body(buf_ref, x_ref, a_ref, d_ref, g4_ref, dk_ref, dv_ref, dx_ref, da_ref, dd_ref):
        del buf_ref
        b = pl.program_id(0)
        lane = lax.broadcasted_iota(jnp.int32, (s, LANES), 1)
        dbg = jnp.zeros((s, LANES), F32)
        for h in range(DN_HEADS):
            gh = g4_ref[:, h * DN_HD:(h + 1) * DN_HD]
            dbg = jnp.where(lane == h, gh[:, 0:1], dbg)
            dbg = jnp.where(lane == DN_HEADS + h, gh[:, 1:2], dbg)
        _, vjp = jax.vjp(_bg_f, x_ref[...], a_ref[...], d_ref[...])
        dx, da, dd = vjp(dbg)
        dx_ref[...] = jnp.concatenate([dk_ref[...], dv_ref[...], dx.astype(BF16)], axis=1)

        @pl.when(b == 0)
        def _():
            da_ref[...] = jnp.zeros_like(da_ref)
            dd_ref[...] = jnp.zeros_like(dd_ref)

        da_ref[...] += da
        dd_ref[...] += dd

    return pl.pallas_call(
        body, name="bg_bwd", grid=(bsz,),
        in_specs=[_ANY, _vspec((s, LANES), lambda b: (b, C_BD // LANES)), vec, vec, _vspec((s, DN), lambda b: (b, 0)), kv, kv],
        out_specs=[_vspec((s, tail), lambda b: (b, C_K // tail)), vec, vec],
        out_shape=[jax.ShapeDtypeStruct(dproj.shape, BF16), jax.ShapeDtypeStruct((1, LANES), F32), jax.ShapeDtypeStruct((1, LANES), F32)],
        input_output_aliases={0: 0},
    )(dproj, proj, alog, dt, dbg4, dk, dv)


def _dn_out_f(o, z, w):
    outs = []
    for h in range(DN_HEADS):
        sl = slice(h * DN_HD, (h + 1) * DN_HD)
        outs.append(_rms(o[:, sl], w) * jax.nn.silu(z[:, sl]))
    return jnp.concatenate(outs, axis=1)


def _dn_out_fwd(o, proj, w, ts=512):
    t = o.shape[0]
    blk = _vspec((ts, DN), lambda i: (i, 0))
    zsp = _vspec((ts, DN), lambda i: (i, C_DZ // DN))
    vec = _vspec((1, DN_HD), lambda i: (0, 0))

    def body(o_ref, z_ref, w_ref, y_ref):
        y_ref[...] = _dn_out_f(o_ref[...], z_ref[...], w_ref[...]).astype(BF16)

    return pl.pallas_call(body, name="dn_out_fwd", grid=(t // ts,), in_specs=[blk, zsp, vec], out_specs=blk,
                          out_shape=jax.ShapeDtypeStruct((t, DN), BF16))(o, proj, w)


def _dn_out_bwd(dproj, o, proj, w, dy, ts=512):
    t = o.shape[0]
    blk = _vspec((ts, DN), lambda i: (i, 0))
    zsp = _vspec((ts, DN), lambda i: (i, C_DZ // DN))
    vec = _vspec((1, DN_HD), lambda i: (0, 0))

    def body(buf_ref, o_ref, z_ref, w_ref, dy_ref, do_ref, dz_ref, dw_ref):
        del buf_ref
        i = pl.program_id(0)
        _, vjp = jax.vjp(_dn_out_f, o_ref[...], z_ref[...], w_ref[...])
        do, dz, dw = vjp(dy_ref[...])
        do_ref[...] = do
        dz_ref[...] = dz.astype(BF16)

        @pl.when(i == 0)
        def _():
            dw_ref[...] = jnp.zeros_like(dw_ref)

        dw_ref[...] += dw

    return pl.pallas_call(
        body, name="dn_out_bwd", grid=(t // ts,), in_specs=[_ANY, blk, zsp, vec, blk], out_specs=[blk, zsp, vec],
        out_shape=[jax.ShapeDtypeStruct((t, DN), F32), jax.ShapeDtypeStruct(dproj.shape, BF16), jax.ShapeDtypeStruct((1, DN_HD), F32)],
        input_output_aliases={0: 1},
    )(dproj, o, proj, w, dy)


_C = DN_CHUNK


def _dot(a, b, dims):
    return lax.dot_general(a.astype(BF16), b.astype(BF16), dims, preferred_element_type=F32)


def _split(a):
    hi = a.astype(BF16)
    return hi, (a - hi.astype(F32)).astype(BF16)


def _dot3(a, b, dims):
    (ah, al), (bh, bl) = (a if isinstance(a, tuple) else _split(a)), (b if isinstance(b, tuple) else _split(b))
    mm = lambda x, y: lax.dot_general(x, y, dims, preferred_element_type=F32)
    return mm(ah, bh) + (mm(ah, bl) + mm(al, bh))


_NN = (((1,), (0,)), ((), ()))
_NT = (((1,), (1,)), ((), ()))
_TN = (((0,), (0,)), ((), ()))


_SUB = 8


def _tri_inverses(ls, lts):
    ri8 = lax.broadcasted_iota(jnp.int32, (_SUB, _C), 0)
    ci8 = lax.broadcasted_iota(jnp.int32, (_SUB, _C), 1)
    nblk = _C // _SUB
    ts = []
    for lt in lts:
        blocks = [jnp.where(ci8 == ri8 + _SUB * b, 1.0, 0.0).astype(F32) for b in range(nblk)]
        for r in range(1, _SUB):
            for b in range(nblk):
                coef = lt[_SUB * b:_SUB * (b + 1), _SUB * b + r:_SUB * b + r + 1]
                row = jnp.sum(coef * blocks[b], axis=0, keepdims=True)
                blocks[b] = jnp.where(ri8 == r, blocks[b] - row, blocks[b])
        ts.append(jnp.concatenate(blocks, axis=0))
    ri = lax.broadcasted_iota(jnp.int32, (_C, _C), 0)
    ci = lax.broadcasted_iota(jnp.int32, (_C, _C), 1)
    s = _SUB
    while s < _C:
        shift = s.bit_length()
        quad = ((ri >> shift) == (ci >> shift)) & ((ri & s) != 0) & ((ci & s) == 0)
        offs = [jnp.where(quad, l, 0.0) for l in ls]
        tsp = [_split(t) for t in ts]
        left = [_dot3(tp, off, _NN) for tp, off in zip(tsp, offs)]
        ts = [t - _dot3(lo, tp, _NN) for t, lo, tp in zip(ts, left, tsp)]
        s *= 2
    return ts


_SEG = 512
_HEADS = tuple(range(DN_HEADS))


def _hsl(hh):
    return slice(hh * DN_HD, (hh + 1) * DN_HD)


def _chunk_specs(bsz, nseg, reverse):
    seg = (lambda i: nseg - 1 - i) if reverse else (lambda i: i)
    ncs = _SEG // _C
    col = lambda off: _vspec((bsz, _SEG, DN), lambda i: (0, seg(i), off))
    return (col, _vspec((bsz, _SEG, LANES), lambda i: (0, seg(i), 0)),
            _vspec((bsz, DN_HEADS, ncs, _C), lambda i: (0, 0, seg(i), 0)),
            _vspec((bsz, DN_HEADS, ncs, DN_HD, DN_HD), lambda i: (0, 0, seg(i), 0, 0)),
            _vspec((bsz, DN_HEADS, ncs, _C, _C), lambda i: (0, 0, seg(i), 0, 0)))


def _chunk_pre(q_ref, k_ref, v_ref, bg_ref, gr_ref, c, bb, hh):
    r0 = pl.multiple_of(c * _C, _C)
    ri = lax.broadcasted_iota(jnp.int32, (_C, _C), 0)
    ci = lax.broadcasted_iota(jnp.int32, (_C, _C), 1)
    q = q_ref[bb, pl.ds(r0, _C), _hsl(hh)] * (DN_HD ** -0.5)
    k = k_ref[bb, pl.ds(r0, _C), _hsl(hh)]
    v = v_ref[bb, pl.ds(r0, _C), _hsl(hh)]
    bgc = bg_ref[bb, pl.ds(r0, _C), :]
    beta = bgc[:, hh:hh + 1]
    g_col = bgc[:, DN_HEADS + hh:DN_HEADS + hh + 1]
    g_row = gr_ref[bb, hh, pl.ds(c, 1), :]
    gc_col = jnp.sum(jnp.where(ri >= ci, g_row, 0.0), axis=1, keepdims=True)
    gc_row = jnp.sum(jnp.where(ri <= ci, g_col, 0.0), axis=0, keepdims=True)
    gc_last = jnp.sum(g_col, axis=0, keepdims=True)
    diff = gc_col - gc_row
    decay = jnp.where(ri >= ci, jnp.exp(jnp.where(ri >= ci, diff, 0.0)), 0.0)
    diff_t = gc_row - gc_col
    decay_t = jnp.where(ri <= ci, jnp.exp(jnp.where(ri <= ci, diff_t, 0.0)), 0.0)
    eg = jnp.exp(gc_col)
    et = jnp.exp(gc_last - gc_col)
    gl = jnp.exp(gc_last)
    kb = k * beta
    vb = v * beta
    return dict(r0=r0, bb=bb, hh=hh, q=q, k=k, v=v, beta=beta, decay=decay, decay_t=decay_t, eg=eg, et=et, gl=gl, kb=kb, vb=vb,
                ri=ri, ci=ci)


def _chunk_solve(ms, tms=None):
    for m in ms:
        m["kk_t"] = _dot(m["k"], m["kb"], _NT)
        m["qk"] = _dot(m["q"], m["k"], _NT)
        m["kk"] = _dot(m["kb"], m["k"], _NT)
        if tms is not None:
            m["qk_t"] = _dot(m["k"], m["q"], _NT)
    if tms is None:
        tms = _tri_inverses([jnp.where(m["ri"] > m["ci"], m["kk"] * m["decay"], 0.0) for m in ms],
                            [jnp.where(m["ri"] < m["ci"], m["kk_t"] * m["decay_t"], 0.0) for m in ms])
    for m, tm in zip(ms, tms):
        m["tm_f32"] = tm
    for m in ms:
        rhs = jnp.concatenate([m["vb"], m["kb"] * m["eg"]], axis=1)
        m["tm"] = _split(m["tm_f32"])
        m["sol"] = _dot3(m["tm"], rhs, _NN)
        m["intra"] = jnp.where(m["ri"] >= m["ci"], m["qk"] * m["decay"], 0.0)


def _dn_chunk_fwd(qkvn, bg, g_rows, bsz, plug=None):
    t = qkvn.shape[0]
    s = t // bsz
    nc, nseg = s // _C, s // _SEG
    pairs = [(bb, hh) for bb in range(bsz) for hh in _HEADS]

    def body(q_ref, k_ref, v_ref, bg_ref, gr_ref, o_ref, st_ref, tm_ref, s_ref):
        @pl.when(pl.program_id(0) == 0)
        def _():
            s_ref[...] = jnp.zeros_like(s_ref)

        def chunk(c, carry):
            ms = [_chunk_pre(q_ref, k_ref, v_ref, bg_ref, gr_ref, c, bb, hh) for bb, hh in pairs]
            _chunk_solve(ms)
            sts = [s_ref[i] for i in range(len(pairs))]
            for m, st in zip(ms, sts):
                st_ref[m["bb"], m["hh"], c] = st
                tm_ref[m["bb"], m["hh"], c] = m["tm_f32"]
            ws = [_dot(m["sol"][:, DN_HD:], st, _NN) for m, st in zip(ms, sts)]
            qs = [_dot(m["q"] * m["eg"], st, _NN) for m, st in zip(ms, sts)]
            v_new = [m["sol"][:, :DN_HD] - a for m, a in zip(ms, ws)]
            iv = [_dot(m["intra"], vn, _NN) for m, vn in zip(ms, v_new)]
            upd = [_dot(m["k"] * m["et"], vn, _TN) for m, vn in zip(ms, v_new)]
            for i, (bb, hh) in enumerate(pairs):
                s_ref[i] = sts[i] * ms[i]["gl"] + upd[i]
                o_ref[bb, pl.ds(ms[i]["r0"], _C), _hsl(hh)] = qs[i] + iv[i]
            return carry

        lax.fori_loop(0, _SEG // _C, chunk, 0)

    col, bgs, grs, sts_spec, tms_spec = _chunk_specs(bsz, nseg, False)
    q3, bg3 = qkvn.reshape(bsz, s, 3 * DN), bg.reshape(bsz, s, LANES)
    (o, states, tms), extra = _plugged_call(
        body, plug, _grid_ends((nseg,)), (q3, q3, q3, bg3, g_rows), name="dn_chunk_fwd", grid=(nseg,),
        in_specs=[col(0), col(1), col(2), bgs, grs], out_specs=[col(0), sts_spec, tms_spec],
        out_shape=[jax.ShapeDtypeStruct((bsz, s, DN), F32), jax.ShapeDtypeStruct((bsz, DN_HEADS, nc, DN_HD, DN_HD), F32),
                   jax.ShapeDtypeStruct((bsz, DN_HEADS, nc, _C, _C), F32)],
        scratch_shapes=[pltpu.VMEM((bsz * DN_HEADS, DN_HD, DN_HD), F32)])
    return o.reshape(t, DN), (states, tms), extra


def _dn_chunk_bwd(qkvn, bg, g_rows, states, do, bsz, plug=None):
    t = qkvn.shape[0]
    s = t // bsz
    nc, nseg = s // _C, s // _SEG
    pairs = [(bb, hh) for bb in range(bsz) for hh in _HEADS]

    def body(q_ref, k_ref, v_ref, bg_ref, gr_ref, st_ref, tm_ref, do_ref, dqkv_ref, dbg_ref, ds_ref):
        @pl.when(pl.program_id(0) == 0)
        def _():
            ds_ref[...] = jnp.zeros_like(ds_ref)

        def chunk(cc, carry):
            c = _SEG // _C - 1 - cc
            ms = [_chunk_pre(q_ref, k_ref, v_ref, bg_ref, gr_ref, c, bb, hh) for bb, hh in pairs]
            _chunk_solve(ms, [tm_ref[bb, hh, c] for bb, hh in pairs])
            ri, ci = ms[0]["ri"], ms[0]["ci"]
            for i, m in enumerate(ms):
                m["st"] = st_ref[m["bb"], m["hh"], c]
                m["ds_out"] = ds_ref[i]
                m["do"] = do_ref[m["bb"], pl.ds(m["r0"], _C), _hsl(m["hh"])]
                m["w"] = m["sol"][:, DN_HD:]
            for m in ms:
                m["v_new"] = m["sol"][:, :DN_HD] - _dot(m["w"], m["st"], _NN)
            for m in ms:
                m["q_dec"], m["k_tail"] = m["q"] * m["eg"], m["k"] * m["et"]
                m["dk_tail"] = _dot(m["v_new"], m["ds_out"], _NT)
                m["dv_new"] = _dot(m["k_tail"], m["ds_out"], _NN) + _dot(m["intra"], m["do"], _TN)
                m["dq_dec"] = _dot(m["do"], m["st"], _NT)
                m["ds_in"] = m["ds_out"] * m["gl"] + _dot(m["q_dec"], m["do"], _TN)
                m["dintra"] = jnp.where(ri >= ci, _dot(m["do"], m["v_new"], _NT), 0.0)
                m["dintra_t"] = jnp.where(ri <= ci, _dot(m["v_new"], m["do"], _NT), 0.0)
            for m in ms:
                m["dw"] = -_dot(m["dv_new"], m["st"], _NT)
                m["ds_in"] = m["ds_in"] - _dot(m["w"], m["dv_new"], _TN)
            for m in ms:
                dsol = jnp.concatenate([m["dv_new"], m["dw"]], axis=1)
                m["drhs"] = _dot3(m["tm"], dsol, _TN)
            for m in ms:
                m["dl"] = jnp.where(ri > ci, -_dot(m["drhs"], m["sol"], _NT), 0.0)
                m["dl_t"] = jnp.where(ri < ci, -_dot(m["sol"], m["drhs"], _NT), 0.0)
            for m in ms:
                m["dkb2"] = _dot(m["dl"] * m["decay"], m["k"], _NN)
                m["dk"] = _dot(m["dl_t"] * m["decay_t"], m["kb"], _NN) + _dot(m["dintra_t"] * m["decay_t"], m["q"], _NN)
                m["dq"] = _dot(m["dintra"] * m["decay"], m["k"], _NN)
            for m in ms:
                _chunk_bwd_finish(m)
            for m in ms:
                ones_ge = jnp.where(ri <= ci, 1.0, 0.0).astype(BF16)
                gh, gl_ = _split(m["dgc"] + jnp.zeros((_C, LANES), F32))
                m["dg_b"] = jnp.dot(ones_ge, gh, preferred_element_type=F32) + jnp.dot(ones_ge, gl_, preferred_element_type=F32)
            lane = lax.broadcasted_iota(jnp.int32, (_C, LANES), 1)
            for i, m in enumerate(ms):
                bb, hh, rows = m["bb"], m["hh"], pl.ds(m["r0"], _C)
                dqkv_ref[0, bb, rows, _hsl(hh)] = m["dq"] * (DN_HD ** -0.5)
                dqkv_ref[1, bb, rows, _hsl(hh)] = m["dk"]
                dqkv_ref[2, bb, rows, _hsl(hh)] = m["dv"]
                dbg_ref[bb, rows, _hsl(hh)] = jnp.where(lane == 0, m["dbeta"], jnp.where(lane == 1, m["dg_b"], 0.0))
                ds_ref[i] = m["ds_in"]
            return carry

        lax.fori_loop(0, _SEG // _C, chunk, 0)

    col, bgs, grs, sts_spec, tms_spec = _chunk_specs(bsz, nseg, True)
    q3, bg3, do3 = qkvn.reshape(bsz, s, 3 * DN), bg.reshape(bsz, s, LANES), do.reshape(bsz, s, DN)
    (dqkv, dbg), extra = _plugged_call(
        body, plug, _grid_ends((nseg,)), (q3, q3, q3, bg3, g_rows, *states, do3), name="dn_chunk_bwd", grid=(nseg,),
        in_specs=[col(0), col(1), col(2), bgs, grs, sts_spec, tms_spec, col(0)],
        out_specs=[_vspec((3, bsz, _SEG, DN), lambda i: (0, 0, nseg - 1 - i, 0)), col(0)],
        out_shape=[jax.ShapeDtypeStruct((3, bsz, s, DN), F32), jax.ShapeDtypeStruct((bsz, s, DN), F32)],
        scratch_shapes=[pltpu.VMEM((bsz * DN_HEADS, DN_HD, DN_HD), F32)])
    return dqkv.reshape(3, t, DN), dbg.reshape(t, DN), extra


def _chunk_bwd_finish(m):
    q, k, v, beta, decay, decay_t = m["q"], m["k"], m["v"], m["beta"], m["decay"], m["decay_t"]
    eg, et, gl, kb, dl, dl_t, dintra, dintra_t = m["eg"], m["et"], m["gl"], m["kb"], m["dl"], m["dl_t"], m["dintra"], m["dintra_t"]
    dq_dec, dk_tail, dq, dk = m["dq_dec"], m["dk_tail"], m["dq"], m["dk"]
    dgl = jnp.sum(jnp.sum(m["ds_out"] * m["st"], axis=1, keepdims=True), axis=0, keepdims=True)
    dvb, dkbeg = m["drhs"][:, :DN_HD], m["drhs"][:, DN_HD:]
    dkb = dkbeg * eg + m["dkb2"]
    deg = jnp.sum(dkbeg * kb, axis=1, keepdims=True)
    em = (dl * m["kk"] + dintra * m["qk"]) * decay
    em_t = (dl_t * m["kk_t"] + dintra_t * m["qk_t"]) * decay_t
    dgc = jnp.sum(em, axis=1, keepdims=True) - jnp.sum(em_t, axis=1, keepdims=True)
    dq = dq + dq_dec * eg
    deg = deg + jnp.sum(dq_dec * q, axis=1, keepdims=True)
    dk = dk + dk_tail * et
    det = jnp.sum(dk_tail * k, axis=1, keepdims=True)
    dgc = dgc + deg * eg - det * et
    dgc_last = jnp.sum(det * et, axis=0, keepdims=True) + dgl * gl
    rcol = lax.broadcasted_iota(jnp.int32, (_C, 1), 0)
    m["dgc"] = dgc + jnp.where(rcol == _C - 1, dgc_last, 0.0)
    m["dq"] = dq
    m["dk"] = dk + dkb * beta
    m["dbeta"] = jnp.sum(dkb * k, axis=1, keepdims=True) + jnp.sum(dvb * v, axis=1, keepdims=True)
    m["dv"] = dvb * beta


def _mod_fwd(c_all, ada_w_loc, ada_b_loc):
    n, cols = c_all.shape[0], ada_w_loc.shape[1]

    def body(c_ref, w_ref, b_ref, o_ref):
        o_ref[...] = _dot(jax.nn.silu(c_ref[...]), w_ref[...], _NN) + b_ref[...]

    return pl.pallas_call(body, name="mod_fwd", out_shape=jax.ShapeDtypeStruct((n, cols), F32))(c_all, ada_w_loc, ada_b_loc)


def _ada_grad(c_all, dmod_loc, dmod_all):
    d, cols = c_all.shape[1], dmod_loc.shape[1]

    def body(c_ref, dl_ref, da_ref, gw_ref, gb_ref):
        gw_ref[...] = _dot(jax.nn.silu(c_ref[...]), dl_ref[...], _TN)
        gb_ref[...] = jnp.sum(da_ref[...], axis=0, keepdims=True)

    return pl.pallas_call(body, name="ada_grad", out_shape=[jax.ShapeDtypeStruct((d, cols), F32),
                                                           jax.ShapeDtypeStruct((1, dmod_all.shape[1]), F32)])(c_all, dmod_loc, dmod_all)


ELEMENTWISE_BLOCK_BYTES = 3 * 2 ** 19


def _row_tile(r, c=1024):
    fits = [tr for tr in range(16, r + 1, 16) if tr * c * 4 <= ELEMENTWISE_BLOCK_BYTES]
    if not fits:
        return r
    whole = [tr for tr in fits if r % tr == 0]
    return whole[-1] if whole else fits[-1]


def _adamw(w, m, v, grads, name, plug=None):
    r, rest = w.shape[0], w.shape[1:]
    c = math.prod(rest)
    tr = _row_tile(r, c)
    blk = _vspec((tr,) + rest, lambda i: (i,) + (0,) * len(rest))
    n = len(grads)

    def body(*refs):
        w_ref, m_ref, v_ref = refs[:3]
        g_ref, d_ref, mo_ref, vo_ref = refs[3 + n:]
        g = refs[3][...]
        for p in refs[4:3 + n]:
            g = g + p[...]
        g_ref[...] = g
        d_ref[...], mo_ref[...], vo_ref[...] = _adamw_math(w_ref[...], m_ref[...], v_ref[...], g)

    o = jax.ShapeDtypeStruct(w.shape, F32)
    grid = (pl.cdiv(r, tr),)
    res, extra = _plugged_call(body, plug, _grid_ends(grid), (w, m, v, *grads), name=name, grid=grid, in_specs=[blk] * (3 + n),
                               out_specs=[blk] * 4, out_shape=[o] * 4)
    return res if plug is None else (res, extra)


def _adamw_halves(w, m, v, own, other, ic, name):
    if w.ndim == 3:
        r, _, c = w.shape
        tr = _row_tile(r, c)
        blk = _vspec((tr, 1, c // 2), lambda i, j: (i, 0, j))
        half = _vspec((tr, 1, c // 2), lambda i, j: (i, 0, 0))
        grid = (pl.cdiv(r, tr), 2)
    else:
        r, c = w.shape
        tr = _row_tile(r // 2, c)
        per = r // 2 // tr
        blk = _vspec((tr, c), lambda i, j: (j * per + i, 0))
        half = _vspec((tr, c), lambda i, j: (i, 0))
        grid = (per, 2)

    def body(ic_ref, w_ref, m_ref, v_ref, own_ref, other_ref, g_ref, d_ref, mo_ref, vo_ref):
        g = jnp.where(pl.program_id(1) == ic_ref[0], own_ref[...], other_ref[...])
        g_ref[...] = g
        d_ref[...], mo_ref[...], vo_ref[...] = _adamw_math(w_ref[...], m_ref[...], v_ref[...], g)

    o = jax.ShapeDtypeStruct(w.shape, F32)
    return pl.pallas_call(body, name=name, grid=grid,
                          in_specs=[pl.BlockSpec(memory_space=pltpu.SMEM), blk, blk, blk, half, half], out_specs=[blk] * 4,
                          out_shape=[o] * 4)(ic, w, m, v, own, other)


def _adamw_math(w, m, v, g):
    m_new = ADAM_B1 * m + (1.0 - ADAM_B1) * g
    v_new = ADAM_B2 * v + (1.0 - ADAM_B2) * jnp.square(g)
    m_hat = m_new / (1.0 - ADAM_B1 ** ADAM_STEP)
    v_hat = v_new / (1.0 - ADAM_B2 ** ADAM_STEP)
    return -ADAM_LR * (m_hat / (jnp.sqrt(v_hat) + ADAM_EPS) + ADAM_WD * w), m_new, v_new


def _adamw_small(ws, ms, vs, gs, name):
    n = len(ws)

    def body(*refs):
        for i in range(n):
            w_ref, m_ref, v_ref, g_ref = (refs[j * n + i] for j in range(4))
            go_ref, d_ref, mo_ref, vo_ref = refs[4 * n + 4 * i:4 * n + 4 * i + 4]
            g = g_ref[...]
            go_ref[...] = g
            d_ref[...], mo_ref[...], vo_ref[...] = _adamw_math(w_ref[...], m_ref[...], v_ref[...], g)

    res = pl.pallas_call(body, name=name, out_shape=[jax.ShapeDtypeStruct(a.shape, F32) for a in ws for _ in range(4)])(
        *ws, *ms, *vs, *gs)
    return [res[4 * i:4 * i + 4] for i in range(n)]


def _sum_lead(x, name, rows_apart=False):
    p, r, c = x.shape
    tr = _row_tile(r, c)
    mid = (1,) if rows_apart else ()

    def body(x_ref, o_ref):
        acc = x_ref[0].astype(F32)
        for i in range(1, p):
            acc = acc + x_ref[i].astype(F32)
        o_ref[...] = acc.reshape(o_ref.shape)

    return pl.pallas_call(body, name=name, grid=(pl.cdiv(r, tr),), in_specs=[_vspec((p, tr, c), lambda i: (0, i, 0))],
                          out_specs=_vspec((tr,) + mid + (c,), lambda i: (i,) + (0,) * (1 + len(mid))),
                          out_shape=jax.ShapeDtypeStruct((r,) + mid + (c,), F32))(x)


def _allgather8(x_shard, name, plug=None):
    m_per, n = x_shard.shape

    def body(x_ref, out_ref, send_sems, recv_sems, local_sem):
        x, y, c = lax.axis_index("x"), lax.axis_index("y"), lax.axis_index("c")
        me, sibling = (x, y, c), (x, y, 1 - c)
        chips = [(1 - x, y), (x, 1 - y), (1 - x, 1 - y)]

        def rows(px, py, pc):
            return out_ref.at[pl.ds((4 * px + 2 * py + pc) * m_per, m_per), :]

        def copy(k, block, to, src=None):
            return pltpu.make_async_remote_copy(
                src_ref=rows(*block) if src is None else src, dst_ref=rows(*block), send_sem=send_sems.at[k],
                recv_sem=recv_sems.at[k], device_id=to, device_id_type=MESH)

        mine = pltpu.make_async_copy(x_ref, rows(*me), local_sem)
        mine.start()
        first = [copy(0, me, sibling, src=x_ref)]
        first += [copy(1 + j, me, (*chip, c), src=x_ref) for j, chip in enumerate(chips)]
        for cp in first:
            cp.start()
        passed = [copy(4 + j, (*chip, c), sibling) for j, chip in enumerate(chips)]
        for j, chip in enumerate(chips):
            copy(1 + j, (*chip, c), me).wait_recv()
            passed[j].start()
        copy(0, sibling, me).wait_recv()
        for j, chip in enumerate(chips):
            copy(4 + j, (*chip, 1 - c), me).wait_recv()
        for cp in first + passed:
            cp.wait_send()
        mine.wait()

    grid = (1,)
    (out,), extra = _plugged_call(
        body, plug, _grid_ends(grid), (x_shard,), name=name, grid=grid, out_shape=[jax.ShapeDtypeStruct((8 * m_per, n), x_shard.dtype)],
        in_specs=[pl.BlockSpec(memory_space=pltpu.VMEM)], out_specs=[pl.BlockSpec(memory_space=pltpu.VMEM)],
        scratch_shapes=[pltpu.SemaphoreType.DMA((7,)), pltpu.SemaphoreType.DMA((7,)), pltpu.SemaphoreType.DMA])
    return out if plug is None else (out, extra)


_HBM = pl.BlockSpec(memory_space=pltpu.HBM)


def _mesh_place():
    x, y, c = lax.axis_index("x"), lax.axis_index("y"), lax.axis_index("c")
    return x, y, c, 2 * x + y, [(1 - x, y), (x, 1 - y), (1 - x, 1 - y)]


def _gather_plug(shards):
    n = len(shards)

    def half(ref, c, lead=None):
        r, cols = ref.shape[-2] // 2, ref.shape[-1] // 2
        if r % 16 == 0:
            rows = pl.ds(pl.multiple_of(c * r, 16), r)
            return ref.at[rows, :] if lead is None else ref.at[lead, rows, :]
        lanes = pl.ds(pl.multiple_of(c * cols, LANES), cols)
        return ref.at[:, lanes] if lead is None else ref.at[lead, :, lanes]

    def copies(ins, outs, send, recv):
        x, y, c, me, chips = _mesh_place()
        ici, fwd, fwd_in = [], [], []
        for i in range(n):
            for j, (px, py) in enumerate(chips):
                q = 2 * px + py
                ici.append((pltpu.make_async_remote_copy(
                    src_ref=half(ins[i], c), dst_ref=half(outs[i], c, me), send_sem=send.at[6 * i + j], recv_sem=recv.at[6 * i + j],
                    device_id=(px, py, c), device_id_type=MESH),
                    pltpu.make_async_remote_copy(
                    src_ref=half(ins[i], c), dst_ref=half(outs[i], c, q), send_sem=send.at[6 * i + j], recv_sem=recv.at[6 * i + j],
                    device_id=(px, py, c), device_id_type=MESH)))
                fwd.append(pltpu.make_async_remote_copy(
                    src_ref=half(outs[i], c, q), dst_ref=half(outs[i], c, q), send_sem=send.at[6 * i + 3 + j],
                    recv_sem=recv.at[6 * i + 3 + j], device_id=(x, y, 1 - c), device_id_type=MESH))
                fwd_in.append(pltpu.make_async_remote_copy(
                    src_ref=half(outs[i], 1 - c, q), dst_ref=half(outs[i], 1 - c, q), send_sem=send.at[6 * i + 3 + j],
                    recv_sem=recv.at[6 * i + 3 + j], device_id=(x, y, 1 - c), device_id_type=MESH))
        return ici, fwd, fwd_in, me

    def start(ins, outs, send, recv, loc):
        ici, _, _, me = copies(ins, outs, send, recv)
        for i in range(n):
            pltpu.make_async_copy(ins[i], outs[i].at[me], loc.at[i]).start()
        for out_cp, _ in ici:
            out_cp.start()

    def finish(ins, outs, send, recv, loc):
        ici, fwd, fwd_in, me = copies(ins, outs, send, recv)
        for (_, in_cp), f in zip(ici, fwd):
            in_cp.wait_recv()
            f.start()
        for f in fwd_in:
            f.wait_recv()
        for (out_cp, _), f in zip(ici, fwd):
            out_cp.wait_send()
            f.wait_send()
        for i in range(n):
            pltpu.make_async_copy(ins[i], outs[i].at[me], loc.at[i]).wait()

    return dict(ins=list(shards), out_shape=[jax.ShapeDtypeStruct((N_CHIPS,) + a.shape, a.dtype) for a in shards],
                scratch=[pltpu.SemaphoreType.DMA((6 * n,)), pltpu.SemaphoreType.DMA((6 * n,)), pltpu.SemaphoreType.DMA((n,))],
                start=start, finish=finish)


def _exchange_plug(pieces):
    n = len(pieces)

    def copies(ins, outs, send, recv):
        x, y, c, me, chips = _mesh_place()
        out_cps, in_cps = [], []
        for i in range(n):
            for j, (px, py) in enumerate(chips):
                q = 2 * px + py
                out_cps.append(pltpu.make_async_remote_copy(src_ref=ins[i].at[q], dst_ref=outs[i].at[me], send_sem=send.at[3 * i + j],
                                                            recv_sem=recv.at[3 * i + j], device_id=(px, py, c), device_id_type=MESH))
                in_cps.append(pltpu.make_async_remote_copy(src_ref=ins[i].at[me], dst_ref=outs[i].at[q], send_sem=send.at[3 * i + j],
                                                           recv_sem=recv.at[3 * i + j], device_id=(px, py, c), device_id_type=MESH))
        return out_cps, in_cps, me

    def start(ins, outs, send, recv, loc):
        out_cps, _, me = copies(ins, outs, send, recv)
        for i in range(n):
            pltpu.make_async_copy(ins[i].at[me], outs[i].at[me], loc.at[i]).start()
        for cp in out_cps:
            cp.start()

    def finish(ins, outs, send, recv, loc):
        out_cps, in_cps, me = copies(ins, outs, send, recv)
        for cp in in_cps:
            cp.wait_recv()
        for cp in out_cps:
            cp.wait_send()
        for i in range(n):
            pltpu.make_async_copy(ins[i].at[me], outs[i].at[me], loc.at[i]).wait()

    return dict(ins=list(pieces), out_shape=[jax.ShapeDtypeStruct(a.shape, a.dtype) for a in pieces],
                scratch=[pltpu.SemaphoreType.DMA((3 * n,)), pltpu.SemaphoreType.DMA((3 * n,)), pltpu.SemaphoreType.DMA((n,))],
                start=start, finish=finish)


def _comm_call(plug, name):
    n_in, n_out = len(plug["ins"]), len(plug["out_shape"])

    def body(*refs):
        ins, outs, sems = refs[:n_in], refs[n_in:n_in + n_out], refs[n_in + n_out:]
        plug["start"](ins, outs, *sems)
        plug["finish"](ins, outs, *sems)

    return pl.pallas_call(body, name=name, out_shape=plug["out_shape"], in_specs=[_HBM] * n_in, out_specs=[_HBM] * n_out,
                          scratch_shapes=plug["scratch"])(*plug["ins"])


def _plugged_call(body, plug, first_last, args, *, name, grid, in_specs, out_specs, out_shape, scratch_shapes=(), aliases=None):
    in_specs, out_specs, out_shape, scratch_shapes = list(in_specs), list(out_specs), list(out_shape), list(scratch_shapes)
    aliases = dict(aliases or {})
    if plug is None:
        return pl.pallas_call(body, name=name, grid=grid, in_specs=in_specs, out_specs=out_specs, out_shape=out_shape,
                              scratch_shapes=scratch_shapes, input_output_aliases=aliases)(*args), []
    n_in, n_out, n_sc = len(in_specs), len(out_specs), len(scratch_shapes)
    p_in, p_out = len(plug["ins"]), len(plug["out_shape"])

    def full(*refs):
        ins, refs = refs[:n_in], refs[n_in:]
        pins, refs = refs[:p_in], refs[p_in:]
        outs, refs = refs[:n_out], refs[n_out:]
        pouts, refs = refs[:p_out], refs[p_out:]
        scr, psems = refs[:n_sc], refs[n_sc:]
        first, last = first_last()

        @pl.when(first)
        def _():
            plug["start"](pins, pouts, *psems)

        body(*ins, *outs, *scr)

        @pl.when(last)
        def _():
            plug["finish"](pins, pouts, *psems)

    res = pl.pallas_call(full, name=name, grid=grid, in_specs=in_specs + [_HBM] * p_in, out_specs=out_specs + [_HBM] * p_out,
                         out_shape=out_shape + plug["out_shape"], scratch_shapes=scratch_shapes + plug["scratch"],
                         input_output_aliases=aliases)(*args, *plug["ins"])
    return res[:n_out], res[n_out:]


def _grid_ends(grid):
    def ends():
        first = last = None
        for ax, n in enumerate(grid):
            i = pl.program_id(ax)
            first = (i == 0) if first is None else first & (i == 0)
            last = (i == n - 1) if last is None else last & (i == n - 1)
        return first, last
    return ends


def _pair_presum(pieces, name):
    n, r, cols = pieces.shape
    by_cols = (cols // 2) % LANES == 0
    half_shape = (n, r, cols // 2) if by_cols else (n, r // 2, cols)

    def body(p_ref, o_ref, mine_ref, land_ref, send_sem, recv_sem, local_sem):
        x, y, c = lax.axis_index("x"), lax.axis_index("y"), lax.axis_index("c")

        def half(which):
            if by_cols:
                return p_ref.at[:, :, pl.ds(pl.multiple_of(which * (cols // 2), LANES), cols // 2)]
            return p_ref.at[:, pl.ds(pl.multiple_of(which * (r // 2), 16), r // 2), :]

        push = pltpu.make_async_remote_copy(src_ref=half(1 - c), dst_ref=land_ref, send_sem=send_sem, recv_sem=recv_sem,
                                            device_id=(x, y, 1 - c), device_id_type=MESH)
        own = pltpu.make_async_copy(half(c), mine_ref, local_sem)
        push.start()
        own.start()
        own.wait()
        push.wait_recv()
        for i in range(n):
            o_ref[i] = (mine_ref[i].astype(F32) + land_ref[i].astype(F32)).astype(BF16)
        push.wait_send()

    return pl.pallas_call(
        body, name=name, out_shape=jax.ShapeDtypeStruct(half_shape, BF16), in_specs=[_HBM],
        out_specs=pl.BlockSpec(memory_space=pltpu.VMEM),
        scratch_shapes=[pltpu.VMEM(half_shape, BF16), pltpu.VMEM(half_shape, BF16), pltpu.SemaphoreType.DMA,
                        pltpu.SemaphoreType.DMA, pltpu.SemaphoreType.DMA],
    )(pieces)


def _sibling_plug(arrs):
    n = len(arrs)

    def copies(ins, outs, send, recv):
        sibling = (lax.axis_index("x"), lax.axis_index("y"), 1 - lax.axis_index("c"))
        return [pltpu.make_async_remote_copy(src_ref=ins[i], dst_ref=outs[i], send_sem=send.at[i], recv_sem=recv.at[i],
                                             device_id=sibling, device_id_type=MESH) for i in range(n)]

    def start(ins, outs, send, recv):
        for cp in copies(ins, outs, send, recv):
            cp.start()

    def finish(ins, outs, send, recv):
        for cp in copies(ins, outs, send, recv):
            cp.wait()

    return dict(ins=list(arrs), out_shape=[jax.ShapeDtypeStruct(a.shape, a.dtype) for a in arrs],
                scratch=[pltpu.SemaphoreType.DMA((n,)), pltpu.SemaphoreType.DMA((n,))], start=start, finish=finish)


_SEGS = ((0, AQ, C_Q), (AQ, AKV, C_K), (AQ + AKV, AKV, C_V), (AQ + 2 * AKV, 3 * DN, C_DQKV), (2304, DN, C_DZ),
         (2816, 2 * DN_HEADS, C_BD), (2824, D_MODEL, C_GA), (3848, D_MODEL, C_GD))
SHARD_ROWS = IN_DIM // N_CHIPS


def _to_padded(w4):
    wt = w4.reshape(IN_DIM, w4.shape[2])
    parts = [wt[o:o + n] for o, n, _ in sorted(_SEGS, key=lambda sg: sg[2])]
    return jnp.concatenate(parts + [jnp.zeros((IN_PAD - IN_DIM, wt.shape[1]), wt.dtype)], axis=0)


def _from_padded(gt):
    return jnp.concatenate([gt[ps:ps + n] for _, n, ps in sorted(_SEGS)], axis=0).reshape(N_CHIPS, SHARD_ROWS, gt.shape[1])


def _lane_vec(a):
    return jnp.zeros((1, LANES), F32).at[0, DN_HEADS:2 * DN_HEADS].set(a)


_ROW_SHARDED = ("w_in", "w_out", "ffn_w_down")
_FFN = ("ffn_w_up", "ffn_w_down")
_LATE_MIXER = ("w_attn_branch", "w_dn_branch", "w_out")
_PRESUM = ("w_in", "ffn_w_up")


def _pieces(k, a):
    if a.ndim == 3:
        return a
    if k in _ROW_SHARDED:
        return a.reshape(N_CHIPS, a.shape[0] // N_CHIPS, a.shape[1]).astype(BF16)
    return jnp.transpose(a.reshape(a.shape[0], N_CHIPS, a.shape[1] // N_CHIPS), (1, 0, 2)).astype(BF16)


def _assemble(k, a):
    if k in _ROW_SHARDED:
        return a.reshape(-1, a.shape[2])
    return jnp.transpose(a, (1, 0, 2)).reshape(a.shape[1], -1)


def _device_step(x2, tgt2, mod, p, bsz, shards=None):
    d = D_MODEL
    on_mesh = shards is not None
    p = dict(p)
    sh1, sc1, g1, sh2, sc2, g2 = [mod[:, i * d:(i + 1) * d].reshape(bsz, 1, d) for i in range(N_MOD)]
    alog_v, dt_v = _lane_vec(p["dn_a_log"]), _lane_vec(p["dn_dt_bias"])
    sinks = p["attn_sinks"].reshape(1, AQ_HEADS)
    u1 = _pre_fwd(x2, p["norm_mix_pre"], sc1, sh1, "pre1_fwd")
    if on_mesh:
        proj, got = _mm(u1, p["w_in"], "nt", F32, "mm_proj", _gather_plug(shards["late_mixer"]))
        p.update({k: _assemble(k, a) for k, a in zip(_LATE_MIXER, got)})
    else:
        proj = _mm(u1, p["w_in"], "nt", F32, "mm_proj")
    bias = _bias_build(p["rel_bias"])
    y_attn = _attn_fwd(proj, bias, sinks, bsz)
    qkvn = _dn_prep_fwd(proj, p["dn_conv_w"], bsz)
    bg = _bg_fwd(proj, alog_v, dt_v, bsz)
    nc = x2.shape[0] // bsz // DN_CHUNK
    g_rows = jnp.transpose(bg[:, DN_HEADS:2 * DN_HEADS].reshape(bsz, nc, DN_CHUNK, DN_HEADS), (0, 3, 1, 2))
    o, states, got = _dn_chunk_fwd(qkvn, bg, g_rows, bsz, _gather_plug(shards["ffn"][:1]) if on_mesh else None)
    for k, a in zip(_FFN[:1], got):
        p[k] = _assemble(k, a)
    y_dn = _dn_out_fwd(o, proj, p["dn_norm_w"])
    ya = _mm(y_attn, p["w_attn_branch"], "nn", BF16, "mm_ya")
    yd = _mm(y_dn, p["w_dn_branch"], "nn", BF16, "mm_yd")
    merged = _merge_fwd(proj, ya, yd)
    y1 = _mm(merged, p["w_out"], "nn", F32, "mm_y1")
    h1, u2 = _post_pre_fwd(x2, y1, p["norm_mix_post"], g1, p["norm_ffn_pre"], sc2, sh2, "post1_pre2_fwd")
    if on_mesh:
        up, got = _mm(u2, p["ffn_w_up"], "nn", F32, "mm_up", _gather_plug(shards["ffn"][1:]))
        p["ffn_w_down"] = _assemble("ffn_w_down", got[0])
    else:
        up = _mm(u2, p["ffn_w_up"], "nn", F32, "mm_up")
    act = _ffn_act_fwd(up, p["ffn_conv_w"], bsz)
    y2 = _mm(act, p["ffn_w_down"], "nn", F32, "mm_y2")
    dh2, dy2, g_ffn_post, dg2, sq = _post_loss_bwd(h1, y2, p["norm_ffn_post"], g2, tgt2, "post2_loss_bwd")
    g = {}
    g["norm_ffn_post"] = g_ffn_post
    dact = _mm(dy2, p["ffn_w_down"], "nt", F32, "mm_dact")
    g["ffn_w_down"] = _mm(act, dy2, "tn", BF16, "mm_dwdown")
    dupg, dupv, dcwg, dcwv, got_down = _ffn_act_bwd(
        up, p["ffn_conv_w"], dact, bsz, _exchange_plug([_pieces("ffn_w_down", g["ffn_w_down"])]) if on_mesh else None)
    g["ffn_conv_w"] = jnp.concatenate([dcwg, dcwv], axis=1)
    g["ffn_w_up"] = jnp.concatenate([_mm(u2, dupg, "tn", BF16, "mm_dwup_gate", split=N_CHIPS // 2),
                                     _mm(u2, dupv, "tn", BF16, "mm_dwup_val", split=N_CHIPS // 2)], axis=0)
    du2 = _mm(dupg, p["ffn_w_up"], "nt", F32, "mm_du2_gate", b_kblock=0)
    du2 = _mm(dupv, p["ffn_w_up"], "nt", F32, "mm_du2_val", b_kblock=1, add=du2)
    dh1, dy1, g["norm_ffn_pre"], dsc2, dsh2, g["norm_mix_post"], dg1 = _pre_post_bwd(
        h1, p["norm_ffn_pre"], sc2, sh2, du2, dh2, y1, p["norm_mix_post"], g1, "pre2_post1_bwd")
    dmerged = _mm(dy1, p["w_out"], "nt", BF16, "mm_dmerged")
    g["w_out"] = _mm(merged, dy1, "tn", BF16, "mm_dwout")
    dproj = lax.empty((x2.shape[0], IN_PAD), BF16)
    dproj, dya = _branch_bwd(dproj, proj, ya, dmerged, C_GA, "merge_bwd_attn")
    dproj, dyd = _branch_bwd(dproj, proj, yd, dmerged, C_GD, "merge_bwd_dn")
    dy_attn = _mm(dya, p["w_attn_branch"], "nt", BF16, "mm_dyattn")
    g["w_attn_branch"] = _mm(y_attn, dya, "tn", BF16, "mm_dwab", split=N_CHIPS)
    dy_dn = _mm(dyd, p["w_dn_branch"], "nt", F32, "mm_dydn")
    g["w_dn_branch"] = _mm(y_dn, dyd, "tn", BF16, "mm_dwdb", split=N_CHIPS)
    do, dproj, g["dn_norm_w"] = _dn_out_bwd(dproj, o, proj, p["dn_norm_w"], dy_dn)
    def plug_for(names):
        if not on_mesh:
            return None
        return _exchange_plug([_pair_presum(_pieces(k, g[k]), "presum_" + k) if k in _PRESUM else _pieces(k, g[k]) for k in names])

    early = ("w_out", "w_attn_branch", "w_dn_branch")
    dqkvn, dbg4, got_up = _dn_chunk_bwd(qkvn, bg, g_rows, states, do, bsz, plug_for(_FFN[:1]))
    got_ffn = list(got_up) + list(got_down)
    dproj, g["dn_conv_w"] = _dn_prep_bwd(dproj, proj, p["dn_conv_w"], dqkvn, bsz)
    dproj, dk, dv, dbias, g["attn_sinks"], got_early = _attn_bwd(dproj, proj, bias, sinks, dy_attn, bsz, plug_for(early))
    dproj, g["dn_a_log"], g["dn_dt_bias"] = _bg_bwd(dproj, proj, alog_v, dt_v, dbg4, dk, dv, bsz)
    g["rel_bias"] = _bias_grad(dbias)
    g["w_in"] = _from_padded(_mm(dproj, u1, "tn", BF16, "mm_dwin"))
    if on_mesh:
        du1, got_in = _mm(dproj, p["w_in"], "nn", F32, "mm_du1", plug_for(("w_in",)))
        g.update(zip(_FFN + early + ("w_in",), list(got_ffn) + list(got_early) + list(got_in)))
    else:
        du1 = _mm(dproj, p["w_in"], "nn", F32, "mm_du1")
    dx, g["norm_mix_pre"], dsc1, dsh1 = _pre_bwd(x2, p["norm_mix_pre"], sc1, sh1, du1, dh1, "pre1_bwd")
    dmod = jnp.concatenate([dsh1, dsc1, dg1, dsh2, dsc2, dg2], axis=-1).reshape(bsz, N_MOD * d)
    return sq, dx, dmod, g


_SMALL = (("norm_mix_pre", D_MODEL), ("norm_mix_post", D_MODEL), ("norm_ffn_pre", D_MODEL), ("norm_ffn_post", D_MODEL),
          ("dn_norm_w", DN_HD), ("dn_a_log", LANES), ("dn_dt_bias", LANES), ("attn_sinks", AQ_HEADS * LANES),
          ("rel_bias", AQ_HEADS * LANES), ("dn_conv_w", DN_CONV * 3 * DN), ("ffn_conv_w", FFN_CONV * 2 * D_FF))


def _pack_rows(parts, rows):
    flat = jnp.concatenate([a.reshape(-1) for a in parts])
    return jnp.concatenate([flat, jnp.zeros((rows * LANES - flat.shape[0],), F32)]).reshape(rows, LANES)


def _pad128(a):
    flat = a.reshape(-1)
    n = -(-flat.shape[0] // LANES) * LANES
    return jnp.concatenate([flat, jnp.zeros((n - flat.shape[0],), F32)]) if n != flat.shape[0] else flat


_W_NAMES = ("ada_w", "ada_b", "norm_mix_pre", "norm_mix_post", "norm_ffn_pre", "norm_ffn_post", "w_in", "dn_conv_w", "dn_a_log",
            "dn_dt_bias", "dn_norm_w", "attn_sinks", "rel_bias", "w_attn_branch", "w_dn_branch", "w_out", "ffn_w_up", "ffn_conv_w",
            "ffn_w_down")
_BIG = ("w_in", "w_attn_branch", "w_dn_branch", "w_out", "ffn_w_up", "ffn_w_down")


def kernel(x, c, *rest):
    nw = len(_W_NAMES)
    w = dict(zip(_W_NAMES, rest[:nw]))
    loss_target = rest[nw]
    m = dict(zip(_W_NAMES, rest[nw + 1:2 * nw + 1]))
    v = dict(zip(_W_NAMES, rest[2 * nw + 1:3 * nw + 1]))
    ix, iy, ic = lax.axis_index("x"), lax.axis_index("y"), lax.axis_index("c")
    chip, dev = 2 * ix + iy, 4 * ix + 2 * iy + ic
    bsz, s, d = x.shape
    t = bsz * s
    n_dev = 8

    front_rows = 64
    front = _pack_rows([c, w["dn_conv_w"], w["ffn_conv_w"]], front_rows)
    w_in_t = jnp.swapaxes(w["w_in"][0], 0, 1).astype(BF16)
    w_in_cut = SHARD_ROWS // 32 * 16
    front_all, (w_in_lo,) = _allgather8(front, "ag_front", _gather_plug([w_in_t[:w_in_cut]]))
    front_all = front_all.reshape(n_dev, front_rows * LANES)
    n_c, n_dc, n_fc = bsz * d, DN_CONV * 3 * DN // N_CHIPS, FFN_CONV * 2 * D_FF // N_CHIPS
    c_all = front_all[:, :n_c].reshape(n_dev * bsz, d)
    per_chip = front_all[0::2]
    dn_conv_full = jnp.transpose(per_chip[:, n_c:n_c + n_dc].reshape(N_CHIPS, DN_CONV, -1), (1, 0, 2)).reshape(DN_CONV, 3 * DN)
    ffn_conv_full = jnp.transpose(per_chip[:, n_c + n_dc:n_c + n_dc + n_fc].reshape(N_CHIPS, FFN_CONV, -1), (1, 0, 2)).reshape(FFN_CONV, 2 * D_FF)

    mod_cols = N_MOD * d // N_CHIPS
    ada_b_loc = lax.dynamic_slice(w["ada_b"], (0, chip * mod_cols), (1, mod_cols))
    mod_part = _mod_fwd(c_all, w["ada_w"][0], ada_b_loc)
    mod_all, (w_in_hi,) = _allgather8(mod_part, "ag_mod", _gather_plug([w_in_t[w_in_cut:]]))
    mod_all = mod_all.reshape(n_dev, n_dev * bsz, mod_cols)[0::2]
    mod = jnp.transpose(lax.dynamic_slice(mod_all, (0, dev * bsz, 0), (N_CHIPS, bsz, mod_cols)), (1, 0, 2)).reshape(bsz, N_MOD * d)

    p = {"w_in": _to_padded(jnp.concatenate([w_in_lo, w_in_hi], axis=1))}
    shards = {"late_mixer": [w[k][0].astype(BF16) for k in _LATE_MIXER], "ffn": [w[k][0].astype(BF16) for k in _FFN]}
    for k in ("norm_mix_pre", "norm_mix_post", "norm_ffn_pre", "norm_ffn_post", "dn_norm_w", "attn_sinks"):
        p[k] = w[k]
    p["dn_a_log"], p["dn_dt_bias"], p["rel_bias"] = w["dn_a_log"][0], w["dn_dt_bias"][0], w["rel_bias"]
    p["dn_conv_w"], p["ffn_conv_w"] = dn_conv_full, ffn_conv_full

    sq, dx, dmod, g = _device_step(x.reshape(t, d), loss_target.reshape(t, d), mod, p, bsz, shards)
    loss = lax.psum(0.5 * jnp.sum(sq), ("x", "y", "c"))

    g["dn_a_log"], g["dn_dt_bias"] = g["dn_a_log"].reshape(-1), g["dn_dt_bias"].reshape(-1)
    small_rows = 328
    small = _pack_rows([dmod] + [g[k] for k, _ in _SMALL], small_rows)
    small_all = _allgather8(small, "ag_small").reshape(n_dev, small_rows, LANES)
    n_dm = bsz * N_MOD * d
    dmod_all = small_all.reshape(n_dev, -1)[:, :n_dm].reshape(n_dev * bsz, N_MOD * d)
    tot = _sum_lead(small_all, "sum_small").reshape(-1)
    gs, off = {}, n_dm
    for k, n in _SMALL:
        gs[k] = tot[off:off + n]
        off += n
    grad = {}
    grad["ada_w"], grad["ada_b"] = _ada_grad(c_all, lax.dynamic_slice(dmod_all, (0, chip * mod_cols), (n_dev * bsz, mod_cols)), dmod_all)
    for k in ("norm_mix_pre", "norm_mix_post", "norm_ffn_pre", "norm_ffn_post", "dn_norm_w"):
        grad[k] = gs[k]
    grad["dn_a_log"] = gs["dn_a_log"][DN_HEADS:2 * DN_HEADS]
    grad["dn_dt_bias"] = gs["dn_dt_bias"][DN_HEADS:2 * DN_HEADS]
    grad["attn_sinks"] = gs["attn_sinks"].reshape(AQ_HEADS, LANES)[:, 0]
    grad["rel_bias"] = gs["rel_bias"].reshape(AQ_HEADS, LANES)[:, :REL_BUCKETS].T
    grad["dn_conv_w"] = lax.dynamic_slice(gs["dn_conv_w"].reshape(DN_CONV, 3 * DN), (0, chip * (3 * DN // N_CHIPS)), (DN_CONV, 3 * DN // N_CHIPS))
    grad["ffn_conv_w"] = lax.dynamic_slice(gs["ffn_conv_w"].reshape(FFN_CONV, 2 * D_FF), (0, chip * (2 * D_FF // N_CHIPS)), (FFN_CONV, 2 * D_FF // N_CHIPS))

    mine = [_sum_lead(g[k], "sum_" + k, rows_apart=(k == "w_in")) for k in _BIG]
    out = {}
    out["ada_w"], theirs = _adamw(w["ada_w"][0], m["ada_w"][0], v["ada_w"][0], [grad["ada_w"]], "adamw_ada_w", _sibling_plug(mine))
    for k, a, b in zip(_BIG, mine, theirs):
        core = ic.reshape(1).astype(jnp.int32)
        if k == "w_in":
            tr = lambda z: jnp.transpose(z, (2, 0, 1))
            out[k] = [jnp.transpose(r, (1, 2, 0)) for r in _adamw_halves(tr(w[k]), tr(m[k]), tr(v[k]), a, b, core, "adamw_" + k)]
        elif k in _PRESUM:
            out[k] = _adamw_halves(w[k][0], m[k][0], v[k][0], a, b, core, "adamw_" + k)
        else:
            out[k] = _adamw(w[k][0], m[k][0], v[k][0], [a, b], "adamw_" + k)
    small_names = [k for k in _W_NAMES if k not in _BIG and k != "ada_w"]
    res_small = _adamw_small([w[k] for k in small_names], [m[k] for k in small_names], [v[k] for k in small_names],
                             [grad[k].reshape(w[k].shape) for k in small_names], "adamw_small")
    out.update(zip(small_names, res_small))
    for k in _BIG + ("ada_w",):
        out[k] = [r.reshape(w[k].shape) for r in out[k]]
    grads, deltas, new_m, new_v = ([out[k][i] for k in _W_NAMES] for i in range(4))
    return (loss, dx.reshape(bsz, s, d), *grads, *deltas, *new_m, *new_v)
```

```python
import functools
import math

import numpy as np
import jax
import jax.numpy as jnp
from jax import lax
from jax.experimental import pallas as pl
from jax.experimental.pallas import tpu as pltpu

F32 = jnp.float32
BF16 = jnp.bfloat16
MESH = pl.DeviceIdType.MESH

D_MODEL = 1024
N_MOD = 6
AQ_HEADS, AKV_HEADS, A_HD, WINDOW = 8, 2, 64, 128
REL_BUCKETS, REL_MAX_DIST = 32, 128
DN_HEADS, DN_HD, DN_CONV, DN_CHUNK = 4, 128, 4, 64
D_FF, FFN_CONV = 2816, 3
RMS_EPS, L2_EPS, NEG_INF = 1e-6, 1e-6, -1e30
AQ, AKV, DN = AQ_HEADS * A_HD, AKV_HEADS * A_HD, DN_HEADS * DN_HD
IN_DIM = AQ + 2 * AKV + 3 * DN + DN + 2 * DN_HEADS + 2 * D_MODEL
C_DQKV, C_Q, C_DZ, C_GA, C_GD, C_K, C_V, C_BD = 0, 1536, 2048, 2560, 3584, 4608, 4736, 4864
IN_PAD = 4992
LANES = 128
N_CHIPS = 4

ADAM_LR, ADAM_B1, ADAM_B2, ADAM_EPS, ADAM_WD, ADAM_STEP = 0.001, 0.9, 0.999, 1e-08, 0.01, 10


def _pick(n, cap):
    best = None
    for t in range(LANES, cap + 1, LANES):
        if n % t == 0:
            best = t
    return best if best is not None else n


def _vspec(shape, index_map):
    return pl.BlockSpec(shape, index_map)


MM_VMEM_BUDGET = 40 * 2 ** 20
GRID_STEP_S = 0.35e-6
HBM_BYTES_PER_S = 3.0e12
MXU_FLOPS_PER_S = 9.0e14
MXU_DIM = 256


def _mm_tiles(m, n, k, mode, in_bytes, out_bytes, split=1):
    best = None
    for tm in [t for t in range(LANES, m + 1, LANES) if m % t == 0]:
        for tn in [t for t in range(LANES, n // split + 1, LANES) if (n // split) % t == 0]:
            a, b, o = k * tm * in_bytes, k * tn * in_bytes, tm * tn * out_bytes
            if 2 * (a + b + o) + (a if mode == "tn" else 0) > MM_VMEM_BUDGET:
                continue
            hbm_s = (m * k * in_bytes + (m // tm) * n * k * in_bytes + m * n * out_bytes) / HBM_BYTES_PER_S
            mxu_s = 2 * m * n * k / (MXU_FLOPS_PER_S * min(1.0, tm / MXU_DIM) * min(1.0, tn / MXU_DIM))
            cost = (m // tm) * (n // tn) * GRID_STEP_S + max(hbm_s, mxu_s)
            if best is None or cost < best[0]:
                best = (cost, tm, tn)
    return best[1], best[2]


def _mm(a, b, mode, out_dtype, name, plug=None, split=1, b_kblock=0, add=None):
    if mode == "nn":
        (m, k), n = a.shape, b.shape[1]
        dims = (((1,), (0,)), ((), ()))
    elif mode == "nt":
        (m, k), n = a.shape, b.shape[0]
        dims = (((1,), (1,)), ((), ()))
    else:
        (k, m), n = a.shape, b.shape[1]
        dims = (((0,), (0,)), ((), ()))
    tm, tn = _mm_tiles(m, n, k, mode, a.dtype.itemsize, jnp.dtype(out_dtype).itemsize, split)
    if mode == "tn":
        a_spec = _vspec((k, tm), lambda i, j: (0, i))
    else:
        a_spec = _vspec((tm, k), lambda i, j: (i, 0))
    if mode == "nt":
        b_spec = _vspec((tn, k), lambda i, j: (j, b_kblock))
    else:
        b_spec = _vspec((k, tn), lambda i, j: (0, j))
    in_specs, args = [a_spec, b_spec], (a, b)
    if add is not None:
        in_specs, args = in_specs + [_vspec((tm, tn), lambda i, j: (i, j))], (a, b, add)

    def body(a_ref, b_ref, *rest):
        o_ref = rest[-1]
        acc = lax.dot_general(a_ref[...].astype(BF16), b_ref[...].astype(BF16), dims, preferred_element_type=F32)
        if add is not None:
            acc = acc + rest[0][...]
        o_ref[...] = acc.astype(out_dtype).reshape(o_ref.shape)

    grid = (m // tm, n // tn)
    if split == 1:
        out_spec, out_shape = _vspec((tm, tn), lambda i, j: (i, j)), (m, n)
    else:
        per = n // split // tn
        out_spec, out_shape = _vspec((1, tm, tn), lambda i, j: (j // per, i, j % per)), (split, m, n // split)
    (out,), extra = _plugged_call(body, plug, _grid_ends(grid), args, name=name, grid=grid, in_specs=in_specs,
                                  out_specs=[out_spec], out_shape=[jax.ShapeDtypeStruct(out_shape, out_dtype)])
    return out if plug is None else (out, extra)


def _rms(x, w):
    return (x * lax.rsqrt(jnp.mean(x * x, axis=-1, keepdims=True) + RMS_EPS)) * w


def _pre_f(x, w, sc, sh):
    return _rms(x, w) * (1.0 + sc) + sh


def _post_f(y, w, g):
    return g * _rms(y, w)


def _tok_grid(t, bsz, ts):
    nt = t // bsz // ts
    return nt, (bsz, nt)


def _pre_fwd(x, w, sc, sh, name, ts=512):
    t, d = x.shape
    bsz = sc.shape[0]
    nt, grid = _tok_grid(t, bsz, ts)
    row = _vspec((ts, d), lambda b, i: (b * nt + i, 0))
    vec = _vspec((1, d), lambda b, i: (0, 0))
    bvec = _vspec((1, 1, d), lambda b, i: (b, 0, 0))

    def body(x_ref, w_ref, sc_ref, sh_ref, u_ref):
        u_ref[...] = _pre_f(x_ref[...], w_ref[...], sc_ref[0], sh_ref[0]).astype(BF16)

    return pl.pallas_call(body, name=name, grid=grid, in_specs=[row, vec, bvec, bvec], out_specs=row,
                          out_shape=jax.ShapeDtypeStruct((t, d), BF16))(x, w, sc, sh)


def _pre_bwd(x, w, sc, sh, du, dres, name, ts=512):
    t, d = x.shape
    bsz = sc.shape[0]
    nt, grid = _tok_grid(t, bsz, ts)
    row = _vspec((ts, d), lambda b, i: (b * nt + i, 0))
    vec = _vspec((1, d), lambda b, i: (0, 0))
    bvec = _vspec((1, 1, d), lambda b, i: (b, 0, 0))

    def body(x_ref, w_ref, sc_ref, sh_ref, du_ref, dres_ref, dx_ref, dw_ref, dsc_ref, dsh_ref):
        b, i = pl.program_id(0), pl.program_id(1)
        _, vjp = jax.vjp(_pre_f, x_ref[...], w_ref[...], sc_ref[0], sh_ref[0])
        dx, dw, dsc, dsh = vjp(du_ref[...])
        dx_ref[...] = dres_ref[...] + dx

        @pl.when((b == 0) & (i == 0))
        def _():
            dw_ref[...] = jnp.zeros_like(dw_ref)

        @pl.when(i == 0)
        def _():
            dsc_ref[...] = jnp.zeros_like(dsc_ref)
            dsh_ref[...] = jnp.zeros_like(dsh_ref)

        dw_ref[...] += dw
        dsc_ref[0] += dsc
        dsh_ref[0] += dsh

    return pl.pallas_call(
        body, name=name, grid=grid, in_specs=[row, vec, bvec, bvec, row, row], out_specs=[row, vec, bvec, bvec],
        out_shape=[jax.ShapeDtypeStruct((t, d), F32), jax.ShapeDtypeStruct((1, d), F32),
                   jax.ShapeDtypeStruct((bsz, 1, d), F32), jax.ShapeDtypeStruct((bsz, 1, d), F32)],
    )(x, w, sc, sh, du, dres)


def _accumulate(ref, val, first):
    @pl.when(first)
    def _():
        ref[...] = jnp.zeros_like(ref)

    ref[...] += val.reshape(ref.shape)


def _post_pre_fwd(res, y, w_post, g, w_pre, sc, sh, name, ts=512):
    t, d = y.shape
    bsz = g.shape[0]
    nt, grid = _tok_grid(t, bsz, ts)
    row = _vspec((ts, d), lambda b, i: (b * nt + i, 0))
    vec = _vspec((1, d), lambda b, i: (0, 0))
    bvec = _vspec((1, 1, d), lambda b, i: (b, 0, 0))

    def body(res_ref, y_ref, wp_ref, g_ref, w_ref, sc_ref, sh_ref, h_ref, u_ref):
        h = res_ref[...] + _post_f(y_ref[...], wp_ref[...], g_ref[0])
        h_ref[...] = h
        u_ref[...] = _pre_f(h, w_ref[...], sc_ref[0], sh_ref[0]).astype(BF16)

    return pl.pallas_call(body, name=name, grid=grid, in_specs=[row, row, vec, bvec, vec, bvec, bvec], out_specs=[row, row],
                          out_shape=[jax.ShapeDtypeStruct((t, d), F32), jax.ShapeDtypeStruct((t, d), BF16)],
                          )(res, y, w_post, g, w_pre, sc, sh)


def _post_loss_bwd(res, y, w, g, tgt, name, ts=512):
    t, d = y.shape
    bsz = g.shape[0]
    nt, grid = _tok_grid(t, bsz, ts)
    row = _vspec((ts, d), lambda b, i: (b * nt + i, 0))
    vec = _vspec((1, d), lambda b, i: (0, 0))
    bvec = _vspec((1, 1, d), lambda b, i: (b, 0, 0))

    def body(res_ref, y_ref, w_ref, g_ref, tgt_ref, dh_ref, dy_ref, dw_ref, dg_ref, sq_ref):
        b, i = pl.program_id(0), pl.program_id(1)
        part, vjp = jax.vjp(_post_f, y_ref[...], w_ref[...], g_ref[0])
        e = res_ref[...] + part - tgt_ref[...]
        dh = e * (1.0 / d)
        dh_ref[...] = dh
        dy, dw, dg = vjp(dh)
        dy_ref[...] = dy.astype(BF16)
        _accumulate(dw_ref, dw, (b == 0) & (i == 0))
        _accumulate(dg_ref, dg, i == 0)
        _accumulate(sq_ref, jnp.sum(e * e, axis=0, keepdims=True) * (1.0 / d), (b == 0) & (i == 0))

    return pl.pallas_call(
        body, name=name, grid=grid, in_specs=[row, row, vec, bvec, row], out_specs=[row, row, vec, bvec, vec],
        out_shape=[jax.ShapeDtypeStruct((t, d), F32), jax.ShapeDtypeStruct((t, d), BF16), jax.ShapeDtypeStruct((1, d), F32),
                   jax.ShapeDtypeStruct((bsz, 1, d), F32), jax.ShapeDtypeStruct((1, d), F32)],
    )(res, y, w, g, tgt)


def _pre_post_bwd(x, w, sc, sh, du, dres, y, w_post, g, name, ts=512):
    t, d = x.shape
    bsz = sc.shape[0]
    nt, grid = _tok_grid(t, bsz, ts)
    row = _vspec((ts, d), lambda b, i: (b * nt + i, 0))
    vec = _vspec((1, d), lambda b, i: (0, 0))
    bvec = _vspec((1, 1, d), lambda b, i: (b, 0, 0))

    def body(x_ref, w_ref, sc_ref, sh_ref, du_ref, dres_ref, y_ref, wp_ref, g_ref,
             dx_ref, dy_ref, dw_ref, dsc_ref, dsh_ref, dwp_ref, dg_ref):
        b, i = pl.program_id(0), pl.program_id(1)
        _, vjp = jax.vjp(_pre_f, x_ref[...], w_ref[...], sc_ref[0], sh_ref[0])
        dx, dw, dsc, dsh = vjp(du_ref[...])
        dx = dres_ref[...] + dx
        dx_ref[...] = dx
        _, vjp_post = jax.vjp(_post_f, y_ref[...], wp_ref[...], g_ref[0])
        dy, dwp, dg = vjp_post(dx)
        dy_ref[...] = dy.astype(BF16)
        first = (b == 0) & (i == 0)
        _accumulate(dw_ref, dw, first)
        _accumulate(dwp_ref, dwp, first)
        _accumulate(dsc_ref, dsc, i == 0)
        _accumulate(dsh_ref, dsh, i == 0)
        _accumulate(dg_ref, dg, i == 0)

    v1, vb = jax.ShapeDtypeStruct((1, d), F32), jax.ShapeDtypeStruct((bsz, 1, d), F32)
    return pl.pallas_call(
        body, name=name, grid=grid, in_specs=[row, vec, bvec, bvec, row, row, row, vec, bvec],
        out_specs=[row, row, vec, bvec, bvec, vec, bvec],
        out_shape=[jax.ShapeDtypeStruct((t, d), F32), jax.ShapeDtypeStruct((t, d), BF16), v1, vb, vb, v1, vb],
    )(x, w, sc, sh, du, dres, y, w_post, g)


def _merge_f(ga, gd, ya, yd):
    return jax.nn.sigmoid(ga) * ya + jax.nn.sigmoid(gd) * yd


_MW = 512


def _merge_fwd(proj, ya, yd, ts=512):
    t, d = ya.shape
    blk = _vspec((ts, _MW), lambda i, j: (i, j))
    ga = _vspec((ts, _MW), lambda i, j: (i, C_GA // _MW + j))
    gd = _vspec((ts, _MW), lambda i, j: (i, C_GD // _MW + j))

    def body(ga_ref, gd_ref, ya_ref, yd_ref, o_ref):
        o_ref[...] = _merge_f(ga_ref[...], gd_ref[...], ya_ref[...].astype(F32), yd_ref[...].astype(F32)).astype(BF16)

    return pl.pallas_call(body, name="merge_fwd", grid=(t // ts, d // _MW), in_specs=[ga, gd, blk, blk], out_specs=blk,
                          out_shape=jax.ShapeDtypeStruct((t, d), BF16))(proj, proj, ya, yd)


_ANY = pl.BlockSpec(memory_space=pl.ANY)


def _branch_bwd(dproj, proj, y, dm, col0, name, ts=512):
    t, d = y.shape
    blk = _vspec((ts, _MW), lambda i, j: (i, j))
    gate = _vspec((ts, _MW), lambda i, j: (i, col0 // _MW + j))

    def body(buf_ref, g_ref, y_ref, dm_ref, dg_ref, dy_ref):
        del buf_ref
        _, vjp = jax.vjp(lambda g, yy: jax.nn.sigmoid(g) * yy, g_ref[...], y_ref[...].astype(F32))
        dg, dy = vjp(dm_ref[...].astype(F32))
        dg_ref[...] = dg.astype(BF16)
        dy_ref[...] = dy.astype(BF16)

    return pl.pallas_call(body, name=name, grid=(t // ts, d // _MW), in_specs=[_ANY, gate, blk, blk], out_specs=[gate, blk],
                          out_shape=[jax.ShapeDtypeStruct(dproj.shape, BF16), jax.ShapeDtypeStruct((t, d), BF16)],
                          input_output_aliases={0: 0})(dproj, proj, y, dm)


def _shift_down(x, s):
    if s == 0:
        return x
    r = lax.broadcasted_iota(jnp.int32, x.shape, 0)
    return jnp.where(r >= s, pltpu.roll(x, s, 0), 0.0)


def _shift_up(x, s):
    if s == 0:
        return x
    n = x.shape[0]
    r = lax.broadcasted_iota(jnp.int32, x.shape, 0)
    return jnp.where(r < n - s, pltpu.roll(x, n - s, 0), 0.0)


def _conv_fwd(x, w, k):
    out = None
    for j in range(k):
        term = w[j:j + 1, :] * _shift_down(x, k - 1 - j)
        out = term if out is None else out + term
    return out


def _conv_bwd(x, w, dc, k):
    dx = None
    dws = []
    for j in range(k):
        up = _shift_up(dc, k - 1 - j)
        term = w[j:j + 1, :] * up
        dx = term if dx is None else dx + term
        dws.append(jnp.sum(up * x, axis=0, keepdims=True))
    return dx, jnp.concatenate(dws, axis=0)


def _geglu_f(gate, val):
    return jax.nn.gelu(gate, approximate=True) * val


_FW = 256


def _ffn_act_fwd(up, conv_w, bsz):
    t = up.shape[0]
    s = t // bsz
    nj = D_FF // _FW
    xg = _vspec((s, _FW), lambda b, j: (b, j))
    xv = _vspec((s, _FW), lambda b, j: (b, nj + j))
    wg = _vspec((FFN_CONV, _FW), lambda b, j: (0, j))
    wv = _vspec((FFN_CONV, _FW), lambda b, j: (0, nj + j))

    def body(xg_ref, xv_ref, wg_ref, wv_ref, o_ref):
        gate = _conv_fwd(xg_ref[...], wg_ref[...], FFN_CONV)
        val = _conv_fwd(xv_ref[...], wv_ref[...], FFN_CONV)
        o_ref[...] = _geglu_f(gate, val).astype(BF16)

    return pl.pallas_call(body, name="ffn_act_fwd", grid=(bsz, nj), in_specs=[xg, xv, wg, wv],
                          out_specs=_vspec((s, _FW), lambda b, j: (b, j)),
                          out_shape=jax.ShapeDtypeStruct((t, D_FF), BF16))(up, up, conv_w, conv_w)


def _ffn_act_bwd(up, conv_w, dact, bsz, plug=None):
    t = up.shape[0]
    s = t // bsz
    nj = D_FF // _FW
    xg = _vspec((s, _FW), lambda j, b: (b, j))
    xv = _vspec((s, _FW), lambda j, b: (b, nj + j))
    wg = _vspec((FFN_CONV, _FW), lambda j, b: (0, j))
    wv = _vspec((FFN_CONV, _FW), lambda j, b: (0, nj + j))
    da = _vspec((s, _FW), lambda j, b: (b, j))
    dwo = _vspec((FFN_CONV, _FW), lambda j, b: (0, j))

    def body(xg_ref, xv_ref, wg_ref, wv_ref, da_ref, dxg_ref, dxv_ref, dwg_ref, dwv_ref):
        b = pl.program_id(1)
        xg_, xv_, wg_, wv_ = xg_ref[...], xv_ref[...], wg_ref[...], wv_ref[...]
        gate = _conv_fwd(xg_, wg_, FFN_CONV)
        val = _conv_fwd(xv_, wv_, FFN_CONV)
        _, vjp = jax.vjp(_geglu_f, gate, val)
        dgate, dval = vjp(da_ref[...])
        dxg, dwg = _conv_bwd(xg_, wg_, dgate, FFN_CONV)
        dxv, dwv = _conv_bwd(xv_, wv_, dval, FFN_CONV)
        dxg_ref[...] = dxg.astype(BF16)
        dxv_ref[...] = dxv.astype(BF16)

        @pl.when(b == 0)
        def _():
            dwg_ref[...] = jnp.zeros_like(dwg_ref)
            dwv_ref[...] = jnp.zeros_like(dwv_ref)

        dwg_ref[...] += dwg
        dwv_ref[...] += dwv

    outs, extra = _plugged_call(
        body, plug, _grid_ends((nj, bsz)), (up, up, conv_w, conv_w, dact), name="ffn_act_bwd", grid=(nj, bsz),
        in_specs=[xg, xv, wg, wv, da], out_specs=[da, da, dwo, dwo],
        out_shape=[jax.ShapeDtypeStruct((t, D_FF), BF16), jax.ShapeDtypeStruct((t, D_FF), BF16),
                   jax.ShapeDtypeStruct((FFN_CONV, D_FF), F32), jax.ShapeDtypeStruct((FFN_CONV, D_FF), F32)])
    return (*outs, extra)


def _bucket_table():
    qi = np.arange(WINDOW)[:, None]
    kj = np.arange(2 * WINDOW)[None, :]
    dist = WINDOW + qi - kj
    dc = np.maximum(dist, 0)
    max_exact = REL_BUCKETS // 2
    scaled = np.log(np.maximum(dc, 1).astype(np.float32) / np.float32(max_exact)) / np.float32(math.log(REL_MAX_DIST / max_exact))
    large = max_exact + (scaled.astype(np.float32) * np.float32(REL_BUCKETS - max_exact)).astype(np.int32)
    large = np.minimum(large, REL_BUCKETS - 1)
    bucket = np.where(dc < max_exact, dc, large).astype(np.int32)
    in_band = ((dist >= 0) & (dist < WINDOW)).astype(np.int32)
    return bucket, in_band


def _bias_build(rel_bias):
    bucket, _ = _bucket_table()

    def body(rb_ref, idx_ref, o_ref):
        h = pl.program_id(0)
        idx = idx_ref[...]
        acc = jnp.zeros(idx.shape, F32)
        for r in range(REL_BUCKETS):
            acc = jnp.where(idx == r, rb_ref[r, h], acc)
        o_ref[0] = acc

    return pl.pallas_call(
        body, name="bias_build", grid=(AQ_HEADS,),
        in_specs=[pl.BlockSpec(memory_space=pltpu.SMEM), _vspec((WINDOW, 2 * WINDOW), lambda h: (0, 0))],
        out_specs=_vspec((1, WINDOW, 2 * WINDOW), lambda h: (h, 0, 0)),
        out_shape=jax.ShapeDtypeStruct((AQ_HEADS, WINDOW, 2 * WINDOW), F32),
    )(rel_bias, jnp.asarray(bucket))


def _bias_grad(dbias):
    bucket, _ = _bucket_table()

    def body(db_ref, idx_ref, o_ref):
        idx = idx_ref[...]
        db = db_ref[0]
        lane = lax.broadcasted_iota(jnp.int32, (1, LANES), 1)
        acc = jnp.zeros((1, LANES), F32)
        for r in range(REL_BUCKETS):
            s = jnp.sum(jnp.sum(jnp.where(idx == r, db, 0.0), axis=1, keepdims=True), axis=0, keepdims=True)
            acc = jnp.where(lane == r, s, acc)
        o_ref[0] = acc

    return pl.pallas_call(
        body, name="bias_grad", grid=(AQ_HEADS,),
        in_specs=[_vspec((1, WINDOW, 2 * WINDOW), lambda h: (h, 0, 0)), _vspec((WINDOW, 2 * WINDOW), lambda h: (0, 0))],
        out_specs=_vspec((1, 1, LANES), lambda h: (h, 0, 0)),
        out_shape=jax.ShapeDtypeStruct((AQ_HEADS, 1, LANES), F32),
    )(dbias, jnp.asarray(bucket))


def _attn_mask(n):
    qi = lax.broadcasted_iota(jnp.int32, (WINDOW, 2 * WINDOW), 0)
    kj = lax.broadcasted_iota(jnp.int32, (WINDOW, 2 * WINDOW), 1)
    dist = WINDOW + qi - kj
    band = (dist >= 0) & (dist < WINDOW)
    return band & ((kj >= WINDOW) | (n > 0))


def _attn_probs(qk, bias, sink, mask):
    s = jnp.where(mask, qk * (A_HD ** -0.5) + bias, NEG_INF)
    m = jnp.maximum(jnp.max(s, axis=-1, keepdims=True), sink)
    p = jnp.exp(s - m)
    es = jnp.exp(sink - m)
    inv = 1.0 / (jnp.sum(p, axis=-1, keepdims=True) + es)
    return p * inv, es * inv


def _attn_fwd(proj, bias, sinks, bsz):
    t = proj.shape[0]
    s = t // bsz
    nb = s // WINDOW
    grp = AQ_HEADS // AKV_HEADS

    def body(q_ref, k_ref, v_ref, bias_ref, sink_ref, y_ref, kp_ref, vp_ref):
        kp_ref[0:WINDOW, :] = jnp.zeros((WINDOW, LANES), BF16)
        vp_ref[0:WINDOW, :] = jnp.zeros((WINDOW, LANES), BF16)
        kp_ref[WINDOW:, :] = k_ref[...].astype(BF16)
        vp_ref[WINDOW:, :] = v_ref[...].astype(BF16)

        def blk(n, carry):
            r0 = pl.multiple_of(n * WINDOW, WINDOW)
            mask = _attn_mask(n)
            kband = kp_ref[pl.ds(r0, 2 * WINDOW), :]
            vband = vp_ref[pl.ds(r0, 2 * WINDOW), :]
            qb = q_ref[pl.ds(r0, WINDOW), :].astype(BF16)
            heads = range(AQ_HEADS)
            hsl = lambda h: slice(h * A_HD, (h + 1) * A_HD)
            kbs = [kband[:, hsl(kv)] for kv in range(AKV_HEADS)]
            vbs = [vband[:, hsl(kv)] for kv in range(AKV_HEADS)]
            qks = [lax.dot_general(qb[:, hsl(h)], kbs[h // grp], _NT, preferred_element_type=F32) for h in heads]
            probs = [_attn_probs(qks[h], bias_ref[h], sink_ref[0, h], mask)[0] for h in heads]
            outs = [jnp.dot(probs[h].astype(BF16), vbs[h // grp], preferred_element_type=F32) for h in heads]
            y_ref[pl.ds(r0, WINDOW), :] = jnp.concatenate(outs, axis=1).astype(BF16)
            return carry

        lax.fori_loop(0, nb, blk, 0)

    return pl.pallas_call(
        body, name="attn_fwd", grid=(bsz,),
        in_specs=[_vspec((s, AQ), lambda b: (b, C_Q // AQ)), _vspec((s, AKV), lambda b: (b, C_K // AKV)),
                  _vspec((s, AKV), lambda b: (b, C_V // AKV)),
                  _vspec((AQ_HEADS, WINDOW, 2 * WINDOW), lambda b: (0, 0, 0)), pl.BlockSpec(memory_space=pltpu.SMEM)],
        out_specs=_vspec((s, AQ), lambda b: (b, 0)), out_shape=jax.ShapeDtypeStruct((t, AQ), BF16),
        scratch_shapes=[pltpu.VMEM((s + WINDOW, LANES), BF16), pltpu.VMEM((s + WINDOW, LANES), BF16)],
    )(proj, proj, proj, bias, sinks)


def _attn_bwd(dproj, proj, bias, sinks, dy, bsz, plug=None):
    t = proj.shape[0]
    s = t // bsz
    nb = s // WINDOW
    grp = AQ_HEADS // AKV_HEADS
    scale = A_HD ** -0.5

    def body(q_ref, k_ref, v_ref, bias_ref, sink_ref, dy_ref, dq_ref, dk_ref, dv_ref, dbias_ref, dsink_ref,
             kp_ref, vp_ref, dkp_ref, dvp_ref):
        b = pl.program_id(0)
        kp_ref[0:WINDOW, :] = jnp.zeros((WINDOW, LANES), BF16)
        vp_ref[0:WINDOW, :] = jnp.zeros((WINDOW, LANES), BF16)
        kp_ref[WINDOW:, :] = k_ref[...].astype(BF16)
        vp_ref[WINDOW:, :] = v_ref[...].astype(BF16)
        dkp_ref[...] = jnp.zeros_like(dkp_ref)
        dvp_ref[...] = jnp.zeros_like(dvp_ref)

        @pl.when(b == 0)
        def _():
            dbias_ref[...] = jnp.zeros_like(dbias_ref)
            dsink_ref[...] = jnp.zeros_like(dsink_ref)

        def blk(n, carry):
            r0 = pl.multiple_of(n * WINDOW, WINDOW)
            mask = _attn_mask(n)
            kband = kp_ref[pl.ds(r0, 2 * WINDOW), :]
            vband = vp_ref[pl.ds(r0, 2 * WINDOW), :]
            qb = q_ref[pl.ds(r0, WINDOW), :].astype(BF16)
            dyb = dy_ref[pl.ds(r0, WINDOW), :].astype(BF16)
            heads = range(AQ_HEADS)
            hsl = lambda h: slice(h * A_HD, (h + 1) * A_HD)
            kbs = [kband[:, hsl(kv)] for kv in range(AKV_HEADS)]
            vbs = [vband[:, hsl(kv)] for kv in range(AKV_HEADS)]
            qhs = [qb[:, hsl(h)] for h in heads]
            dyhs = [dyb[:, hsl(h)] for h in heads]
            qks = [lax.dot_general(qhs[h], kbs[h // grp], _NT, preferred_element_type=F32) for h in heads]
            dprobs = [lax.dot_general(dyhs[h], vbs[h // grp], _NT, preferred_element_type=F32) for h in heads]
            pbs, dsbs = [], []
            for h in heads:
                probs, psink = _attn_probs(qks[h], bias_ref[h], sink_ref[0, h], mask)
                rowdot = jnp.sum(probs * dprobs[h], axis=-1, keepdims=True)
                ds = probs * (dprobs[h] - rowdot)
                dbias_ref[h] += ds
                dsink_ref[h] += jnp.sum(-psink * rowdot, axis=0, keepdims=True) + jnp.zeros((1, LANES), F32)
                pbs.append(probs.astype(BF16))
                dsbs.append(ds.astype(BF16))
            dvhs = [lax.dot_general(pbs[h], dyhs[h], _TN, preferred_element_type=F32) for h in heads]
            dqs = [jnp.dot(dsbs[h], kbs[h // grp], preferred_element_type=F32) * scale for h in heads]
            dkhs = [lax.dot_general(dsbs[h], qhs[h], _TN, preferred_element_type=F32) * scale for h in heads]
            dks = [sum(dkhs[kv * grp + 1:(kv + 1) * grp], dkhs[kv * grp]) for kv in range(AKV_HEADS)]
            dvs = [sum(dvhs[kv * grp + 1:(kv + 1) * grp], dvhs[kv * grp]) for kv in range(AKV_HEADS)]
            dq_ref[pl.ds(r0, WINDOW), :] = jnp.concatenate(dqs, axis=1).astype(BF16)
            dkp_ref[pl.ds(r0, 2 * WINDOW), :] += jnp.concatenate(dks, axis=1)
            dvp_ref[pl.ds(r0, 2 * WINDOW), :] += jnp.concatenate(dvs, axis=1)
            return carry

        lax.fori_loop(0, nb, blk, 0)
        dk_ref[...] = dkp_ref[WINDOW:, :].astype(BF16)
        dv_ref[...] = dvp_ref[WINDOW:, :].astype(BF16)

    kvs = jax.ShapeDtypeStruct((t, AKV), BF16)
    outs, extra = _plugged_call(
        lambda buf_ref, *refs: body(*refs), plug, _grid_ends((bsz,)), (dproj, proj, proj, proj, bias, sinks, dy),
        name="attn_bwd", grid=(bsz,),
        in_specs=[_ANY, _vspec((s, AQ), lambda b: (b, C_Q // AQ)), _vspec((s, AKV), lambda b: (b, C_K // AKV)),
                  _vspec((s, AKV), lambda b: (b, C_V // AKV)),
                  _vspec((AQ_HEADS, WINDOW, 2 * WINDOW), lambda b: (0, 0, 0)), pl.BlockSpec(memory_space=pltpu.SMEM),
                  _vspec((s, AQ), lambda b: (b, 0))],
        out_specs=[_vspec((s, AQ), lambda b: (b, C_Q // AQ)), _vspec((s, AKV), lambda b: (b, 0)), _vspec((s, AKV), lambda b: (b, 0)),
                   _vspec((AQ_HEADS, WINDOW, 2 * WINDOW), lambda b: (0, 0, 0)), _vspec((AQ_HEADS, 1, LANES), lambda b: (0, 0, 0))],
        out_shape=[jax.ShapeDtypeStruct(dproj.shape, BF16), kvs, kvs,
                   jax.ShapeDtypeStruct((AQ_HEADS, WINDOW, 2 * WINDOW), F32), jax.ShapeDtypeStruct((AQ_HEADS, 1, LANES), F32)],
        scratch_shapes=[pltpu.VMEM((s + WINDOW, LANES), BF16), pltpu.VMEM((s + WINDOW, LANES), BF16),
                        pltpu.VMEM((s + WINDOW, LANES), F32), pltpu.VMEM((s + WINDOW, LANES), F32)],
        aliases={0: 0})
    return (*outs, extra)


def _dn_act_f(c, is_qk):
    a = jax.nn.silu(c)
    outs = []
    for h in range(DN_HEADS):
        ah = a[:, h * DN_HD:(h + 1) * DN_HD]
        nh = ah * lax.rsqrt(jnp.sum(ah * ah, axis=-1, keepdims=True) + L2_EPS)
        outs.append(jnp.where(is_qk, nh, ah))
    return jnp.concatenate(outs, axis=1)


def _dn_prep_fwd(proj, conv_w, bsz):
    t = proj.shape[0]
    s = t // bsz
    blk = _vspec((s, DN), lambda b, j: (b, j))
    wsp = _vspec((DN_CONV, DN), lambda b, j: (0, j))

    def body(x_ref, w_ref, o_ref):
        j = pl.program_id(1)
        o_ref[...] = _dn_act_f(_conv_fwd(x_ref[...], w_ref[...], DN_CONV), j < 2)

    return pl.pallas_call(body, name="dn_prep_fwd", grid=(bsz, 3), in_specs=[blk, wsp], out_specs=blk,
                          out_shape=jax.ShapeDtypeStruct((t, 3 * DN), F32))(proj, conv_w)


def _dn_prep_bwd(dproj, proj, conv_w, dqkvn, bsz):
    t = proj.shape[0]
    s = t // bsz
    blk = _vspec((s, DN), lambda j, b: (b, j))
    wsp = _vspec((DN_CONV, DN), lambda j, b: (0, j))

    def body(buf_ref, x_ref, w_ref, d_ref, dx_ref, dw_ref):
        del buf_ref
        j, b = pl.program_id(0), pl.program_id(1)
        x, w = x_ref[...], w_ref[...]
        c = _conv_fwd(x, w, DN_CONV)
        _, vjp = jax.vjp(lambda cc: _dn_act_f(cc, j < 2), c)
        (dc,) = vjp(d_ref[0])
        dx, dw = _conv_bwd(x, w, dc, DN_CONV)
        dx_ref[...] = dx.astype(BF16)

        @pl.when(b == 0)
        def _():
            dw_ref[...] = jnp.zeros_like(dw_ref)

        dw_ref[...] += dw

    return pl.pallas_call(
        body, name="dn_prep_bwd", grid=(3, bsz),
        in_specs=[_ANY, blk, wsp, _vspec((1, s, DN), lambda j, b: (j, b, 0))], out_specs=[blk, wsp],
        out_shape=[jax.ShapeDtypeStruct(dproj.shape, BF16), jax.ShapeDtypeStruct((DN_CONV, 3 * DN), F32)],
        input_output_aliases={0: 0},
    )(dproj, proj, conv_w, dqkvn)


def _bg_f(x, alog, dt):
    lane = lax.broadcasted_iota(jnp.int32, x.shape, 1)
    beta = jax.nn.sigmoid(x)
    g = -jnp.exp(alog) * jax.nn.softplus(x + dt)
    return jnp.where(lane < DN_HEADS, beta, jnp.where(lane < 2 * DN_HEADS, g, 0.0))


def _bg_fwd(proj, alog, dt, bsz):
    t = proj.shape[0]
    s = t // bsz
    vec = _vspec((1, LANES), lambda b: (0, 0))

    def body(x_ref, a_ref, d_ref, o_ref):
        o_ref[...] = _bg_f(x_ref[...], a_ref[...], d_ref[...])

    return pl.pallas_call(body, name="bg_fwd", grid=(bsz,), in_specs=[_vspec((s, LANES), lambda b: (b, C_BD // LANES)), vec, vec],
                          out_specs=_vspec((s, LANES), lambda b: (b, 0)), out_shape=jax.ShapeDtypeStruct((t, LANES), F32))(proj, alog, dt)


def _bg_bwd(dproj, proj, alog, dt, dbg4, dk, dv, bsz):
    t = proj.shape[0]
    s = t // bsz
    vec = _vspec((1, LANES), lambda b: (0, 0))
    kv = _vspec((s, AKV), lambda b: (b, 0))
    tail = 3 * LANES

    def body(buf_ref, x_ref, a_ref, d_ref, g4_ref, dk_ref, dv_ref, dx_ref, da_ref, dd_ref):
        del buf_ref
        b = pl.program_id(0)
        lane = lax.broadcasted_iota(jnp.int32, (s, LANES), 1)
        dbg = jnp.zeros((s, LANES), F32)
        for h in range(DN_HEADS):
            gh = g4_ref[:, h * DN_HD:(h + 1) * DN_HD]
            dbg = jnp.where(lane == h, gh[:, 0:1], dbg)
            dbg = jnp.where(lane == DN_HEADS + h, gh[:, 1:2], dbg)
        _, vjp = jax.vjp(_bg_f, x_ref[...], a_ref[...], d_ref[...])
        dx, da, dd = vjp(dbg)
        dx_ref[...] = jnp.concatenate([dk_ref[...], dv_ref[...], dx.astype(BF16)], axis=1)

        @pl.when(b == 0)
        def _():
            da_ref[...] = jnp.zeros_like(da_ref)
            dd_ref[...] = jnp.zeros_like(dd_ref)

        da_ref[...] += da
        dd_ref[...] += dd

    return pl.pallas_call(
        body, name="bg_bwd", grid=(bsz,),
        in_specs=[_ANY, _vspec((s, LANES), lambda b: (b, C_BD // LANES)), vec, vec, _vspec((s, DN), lambda b: (b, 0)), kv, kv],
        out_specs=[_vspec((s, tail), lambda b: (b, C_K // tail)), vec, vec],
        out_shape=[jax.ShapeDtypeStruct(dproj.shape, BF16), jax.ShapeDtypeStruct((1, LANES), F32), jax.ShapeDtypeStruct((1, LANES), F32)],
        input_output_aliases={0: 0},
    )(dproj, proj, alog, dt, dbg4, dk, dv)


def _dn_out_f(o, z, w):
    outs = []
    for h in range(DN_HEADS):
        sl = slice(h * DN_HD, (h + 1) * DN_HD)
        outs.append(_rms(o[:, sl], w) * jax.nn.silu(z[:, sl]))
    return jnp.concatenate(outs, axis=1)


def _dn_out_fwd(o, proj, w, ts=512):
    t = o.shape[0]
    blk = _vspec((ts, DN), lambda i: (i, 0))
    zsp = _vspec((ts, DN), lambda i: (i, C_DZ // DN))
    vec = _vspec((1, DN_HD), lambda i: (0, 0))

    def body(o_ref, z_ref, w_ref, y_ref):
        y_ref[...] = _dn_out_f(o_ref[...], z_ref[...], w_ref[...]).astype(BF16)

    return pl.pallas_call(body, name="dn_out_fwd", grid=(t // ts,), in_specs=[blk, zsp, vec], out_specs=blk,
                          out_shape=jax.ShapeDtypeStruct((t, DN), BF16))(o, proj, w)


def _dn_out_bwd(dproj, o, proj, w, dy, ts=512):
    t = o.shape[0]
    blk = _vspec((ts, DN), lambda i: (i, 0))
    zsp = _vspec((ts, DN), lambda i: (i, C_DZ // DN))
    vec = _vspec((1, DN_HD), lambda i: (0, 0))

    def body(buf_ref, o_ref, z_ref, w_ref, dy_ref, do_ref, dz_ref, dw_ref):
        del buf_ref
        i = pl.program_id(0)
        _, vjp = jax.vjp(_dn_out_f, o_ref[...], z_ref[...], w_ref[...])
        do, dz, dw = vjp(dy_ref[...])
        do_ref[...] = do
        dz_ref[...] = dz.astype(BF16)

        @pl.when(i == 0)
        def _():
            dw_ref[...] = jnp.zeros_like(dw_ref)

        dw_ref[...] += dw

    return pl.pallas_call(
        body, name="dn_out_bwd", grid=(t // ts,), in_specs=[_ANY, blk, zsp, vec, blk], out_specs=[blk, zsp, vec],
        out_shape=[jax.ShapeDtypeStruct((t, DN), F32), jax.ShapeDtypeStruct(dproj.shape, BF16), jax.ShapeDtypeStruct((1, DN_HD), F32)],
        input_output_aliases={0: 1},
    )(dproj, o, proj, w, dy)


_C = DN_CHUNK


def _dot(a, b, dims):
    return lax.dot_general(a.astype(BF16), b.astype(BF16), dims, preferred_element_type=F32)


def _split(a):
    hi = a.astype(BF16)
    return hi, (a - hi.astype(F32)).astype(BF16)


def _dot3(a, b, dims):
    (ah, al), (bh, bl) = (a if isinstance(a, tuple) else _split(a)), (b if isinstance(b, tuple) else _split(b))
    mm = lambda x, y: lax.dot_general(x, y, dims, preferred_element_type=F32)
    return mm(ah, bh) + (mm(ah, bl) + mm(al, bh))


_NN = (((1,), (0,)), ((), ()))
_NT = (((1,), (1,)), ((), ()))
_TN = (((0,), (0,)), ((), ()))


_SUB = 8


def _tri_inverses(ls, lts):
    ri8 = lax.broadcasted_iota(jnp.int32, (_SUB, _C), 0)
    ci8 = lax.broadcasted_iota(jnp.int32, (_SUB, _C), 1)
    nblk = _C // _SUB
    ts = []
    for lt in lts:
        blocks = [jnp.where(ci8 == ri8 + _SUB * b, 1.0, 0.0).astype(F32) for b in range(nblk)]
        for r in range(1, _SUB):
            for b in range(nblk):
                coef = lt[_SUB * b:_SUB * (b + 1), _SUB * b + r:_SUB * b + r + 1]
                row = jnp.sum(coef * blocks[b], axis=0, keepdims=True)
                blocks[b] = jnp.where(ri8 == r, blocks[b] - row, blocks[b])
        ts.append(jnp.concatenate(blocks, axis=0))
    ri = lax.broadcasted_iota(jnp.int32, (_C, _C), 0)
    ci = lax.broadcasted_iota(jnp.int32, (_C, _C), 1)
    s = _SUB
    while s < _C:
        shift = s.bit_length()
        quad = ((ri >> shift) == (ci >> shift)) & ((ri & s) != 0) & ((ci & s) == 0)
        offs = [jnp.where(quad, l, 0.0) for l in ls]
        tsp = [_split(t) for t in ts]
        left = [_dot3(tp, off, _NN) for tp, off in zip(tsp, offs)]
        ts = [t - _dot3(lo, tp, _NN) for t, lo, tp in zip(ts, left, tsp)]
        s *= 2
    return ts


_SEG = 512
_HEADS = tuple(range(DN_HEADS))


def _hsl(hh):
    return slice(hh * DN_HD, (hh + 1) * DN_HD)


def _chunk_specs(bsz, nseg, reverse):
    seg = (lambda i: nseg - 1 - i) if reverse else (lambda i: i)
    ncs = _SEG // _C
    col = lambda off: _vspec((bsz, _SEG, DN), lambda i: (0, seg(i), off))
    return (col, _vspec((bsz, _SEG, LANES), lambda i: (0, seg(i), 0)),
            _vspec((bsz, DN_HEADS, ncs, _C), lambda i: (0, 0, seg(i), 0)),
            _vspec((bsz, DN_HEADS, ncs, DN_HD, DN_HD), lambda i: (0, 0, seg(i), 0, 0)),
            _vspec((bsz, DN_HEADS, ncs, _C, _C), lambda i: (0, 0, seg(i), 0, 0)))


def _chunk_pre(q_ref, k_ref, v_ref, bg_ref, gr_ref, c, bb, hh):
    r0 = pl.multiple_of(c * _C, _C)
    ri = lax.broadcasted_iota(jnp.int32, (_C, _C), 0)
    ci = lax.broadcasted_iota(jnp.int32, (_C, _C), 1)
    q = q_ref[bb, pl.ds(r0, _C), _hsl(hh)] * (DN_HD ** -0.5)
    k = k_ref[bb, pl.ds(r0, _C), _hsl(hh)]
    v = v_ref[bb, pl.ds(r0, _C), _hsl(hh)]
    bgc = bg_ref[bb, pl.ds(r0, _C), :]
    beta = bgc[:, hh:hh + 1]
    g_col = bgc[:, DN_HEADS + hh:DN_HEADS + hh + 1]
    g_row = gr_ref[bb, hh, pl.ds(c, 1), :]
    gc_col = jnp.sum(jnp.where(ri >= ci, g_row, 0.0), axis=1, keepdims=True)
    gc_row = jnp.sum(jnp.where(ri <= ci, g_col, 0.0), axis=0, keepdims=True)
    gc_last = jnp.sum(g_col, axis=0, keepdims=True)
    diff = gc_col - gc_row
    decay = jnp.where(ri >= ci, jnp.exp(jnp.where(ri >= ci, diff, 0.0)), 0.0)
    diff_t = gc_row - gc_col
    decay_t = jnp.where(ri <= ci, jnp.exp(jnp.where(ri <= ci, diff_t, 0.0)), 0.0)
    eg = jnp.exp(gc_col)
    et = jnp.exp(gc_last - gc_col)
    gl = jnp.exp(gc_last)
    kb = k * beta
    vb = v * beta
    return dict(r0=r0, bb=bb, hh=hh, q=q, k=k, v=v, beta=beta, decay=decay, decay_t=decay_t, eg=eg, et=et, gl=gl, kb=kb, vb=vb,
                ri=ri, ci=ci)


def _chunk_solve(ms, tms=None):
    for m in ms:
        m["kk_t"] = _dot(m["k"], m["kb"], _NT)
        m["qk"] = _dot(m["q"], m["k"], _NT)
        m["kk"] = _dot(m["kb"], m["k"], _NT)
        if tms is not None:
            m["qk_t"] = _dot(m["k"], m["q"], _NT)
    if tms is None:
        tms = _tri_inverses([jnp.where(m["ri"] > m["ci"], m["kk"] * m["decay"], 0.0) for m in ms],
                            [jnp.where(m["ri"] < m["ci"], m["kk_t"] * m["decay_t"], 0.0) for m in ms])
    for m, tm in zip(ms, tms):
        m["tm_f32"] = tm
    for m in ms:
        rhs = jnp.concatenate([m["vb"], m["kb"] * m["eg"]], axis=1)
        m["tm"] = _split(m["tm_f32"])
        m["sol"] = _dot3(m["tm"], rhs, _NN)
        m["intra"] = jnp.where(m["ri"] >= m["ci"], m["qk"] * m["decay"], 0.0)


def _dn_chunk_fwd(qkvn, bg, g_rows, bsz, plug=None):
    t = qkvn.shape[0]
    s = t // bsz
    nc, nseg = s // _C, s // _SEG
    pairs = [(bb, hh) for bb in range(bsz) for hh in _HEADS]

    def body(q_ref, k_ref, v_ref, bg_ref, gr_ref, o_ref, st_ref, tm_ref, s_ref):
        @pl.when(pl.program_id(0) == 0)
        def _():
            s_ref[...] = jnp.zeros_like(s_ref)

        def chunk(c, carry):
            ms = [_chunk_pre(q_ref, k_ref, v_ref, bg_ref, gr_ref, c, bb, hh) for bb, hh in pairs]
            _chunk_solve(ms)
            sts = [s_ref[i] for i in range(len(pairs))]
            for m, st in zip(ms, sts):
                st_ref[m["bb"], m["hh"], c] = st
                tm_ref[m["bb"], m["hh"], c] = m["tm_f32"]
            ws = [_dot(m["sol"][:, DN_HD:], st, _NN) for m, st in zip(ms, sts)]
            qs = [_dot(m["q"] * m["eg"], st, _NN) for m, st in zip(ms, sts)]
            v_new = [m["sol"][:, :DN_HD] - a for m, a in zip(ms, ws)]
            iv = [_dot(m["intra"], vn, _NN) for m, vn in zip(ms, v_new)]
            upd = [_dot(m["k"] * m["et"], vn, _TN) for m, vn in zip(ms, v_new)]
            for i, (bb, hh) in enumerate(pairs):
                s_ref[i] = sts[i] * ms[i]["gl"] + upd[i]
                o_ref[bb, pl.ds(ms[i]["r0"], _C), _hsl(hh)] = qs[i] + iv[i]
            return carry

        lax.fori_loop(0, _SEG // _C, chunk, 0)

    col, bgs, grs, sts_spec, tms_spec = _chunk_specs(bsz, nseg, False)
    q3, bg3 = qkvn.reshape(bsz, s, 3 * DN), bg.reshape(bsz, s, LANES)
    (o, states, tms), extra = _plugged_call(
        body, plug, _grid_ends((nseg,)), (q3, q3, q3, bg3, g_rows), name="dn_chunk_fwd", grid=(nseg,),
        in_specs=[col(0), col(1), col(2), bgs, grs], out_specs=[col(0), sts_spec, tms_spec],
        out_shape=[jax.ShapeDtypeStruct((bsz, s, DN), F32), jax.ShapeDtypeStruct((bsz, DN_HEADS, nc, DN_HD, DN_HD), F32),
                   jax.ShapeDtypeStruct((bsz, DN_HEADS, nc, _C, _C), F32)],
        scratch_shapes=[pltpu.VMEM((bsz * DN_HEADS, DN_HD, DN_HD), F32)])
    return o.reshape(t, DN), (states, tms), extra


def _dn_chunk_bwd(qkvn, bg, g_rows, states, do, bsz, plug=None):
    t = qkvn.shape[0]
    s = t // bsz
    nc, nseg = s // _C, s // _SEG
    pairs = [(bb, hh) for bb in range(bsz) for hh in _HEADS]

    def body(q_ref, k_ref, v_ref, bg_ref, gr_ref, st_ref, tm_ref, do_ref, dqkv_ref, dbg_ref, ds_ref):
        @pl.when(pl.program_id(0) == 0)
        def _():
            ds_ref[...] = jnp.zeros_like(ds_ref)

        def chunk(cc, carry):
            c = _SEG // _C - 1 - cc
            ms = [_chunk_pre(q_ref, k_ref, v_ref, bg_ref, gr_ref, c, bb, hh) for bb, hh in pairs]
            _chunk_solve(ms, [tm_ref[bb, hh, c] for bb, hh in pairs])
            ri, ci = ms[0]["ri"], ms[0]["ci"]
            for i, m in enumerate(ms):
                m["st"] = st_ref[m["bb"], m["hh"], c]
                m["ds_out"] = ds_ref[i]
                m["do"] = do_ref[m["bb"], pl.ds(m["r0"], _C), _hsl(m["hh"])]
                m["w"] = m["sol"][:, DN_HD:]
            for m in ms:
                m["v_new"] = m["sol"][:, :DN_HD] - _dot(m["w"], m["st"], _NN)
            for m in ms:
                m["q_dec"], m["k_tail"] = m["q"] * m["eg"], m["k"] * m["et"]
                m["dk_tail"] = _dot(m["v_new"], m["ds_out"], _NT)
                m["dv_new"] = _dot(m["k_tail"], m["ds_out"], _NN) + _dot(m["intra"], m["do"], _TN)
                m["dq_dec"] = _dot(m["do"], m["st"], _NT)
                m["ds_in"] = m["ds_out"] * m["gl"] + _dot(m["q_dec"], m["do"], _TN)
                m["dintra"] = jnp.where(ri >= ci, _dot(m["do"], m["v_new"], _NT), 0.0)
                m["dintra_t"] = jnp.where(ri <= ci, _dot(m["v_new"], m["do"], _NT), 0.0)
            for m in ms:
                m["dw"] = -_dot(m["dv_new"], m["st"], _NT)
                m["ds_in"] = m["ds_in"] - _dot(m["w"], m["dv_new"], _TN)
            for m in ms:
                dsol = jnp.concatenate([m["dv_new"], m["dw"]], axis=1)
                m["drhs"] = _dot3(m["tm"], dsol, _TN)
            for m in ms:
                m["dl"] = jnp.where(ri > ci, -_dot(m["drhs"], m["sol"], _NT), 0.0)
                m["dl_t"] = jnp.where(ri < ci, -_dot(m["sol"], m["drhs"], _NT), 0.0)
            for m in ms:
                m["dkb2"] = _dot(m["dl"] * m["decay"], m["k"], _NN)
                m["dk"] = _dot(m["dl_t"] * m["decay_t"], m["kb"], _NN) + _dot(m["dintra_t"] * m["decay_t"], m["q"], _NN)
                m["dq"] = _dot(m["dintra"] * m["decay"], m["k"], _NN)
            for m in ms:
                _chunk_bwd_finish(m)
            for m in ms:
                ones_ge = jnp.where(ri <= ci, 1.0, 0.0).astype(BF16)
                gh, gl_ = _split(m["dgc"] + jnp.zeros((_C, LANES), F32))
                m["dg_b"] = jnp.dot(ones_ge, gh, preferred_element_type=F32) + jnp.dot(ones_ge, gl_, preferred_element_type=F32)
            lane = lax.broadcasted_iota(jnp.int32, (_C, LANES), 1)
            for i, m in enumerate(ms):
                bb, hh, rows = m["bb"], m["hh"], pl.ds(m["r0"], _C)
                dqkv_ref[0, bb, rows, _hsl(hh)] = m["dq"] * (DN_HD ** -0.5)
                dqkv_ref[1, bb, rows, _hsl(hh)] = m["dk"]
                dqkv_ref[2, bb, rows, _hsl(hh)] = m["dv"]
                dbg_ref[bb, rows, _hsl(hh)] = jnp.where(lane == 0, m["dbeta"], jnp.where(lane == 1, m["dg_b"], 0.0))
                ds_ref[i] = m["ds_in"]
            return carry

        lax.fori_loop(0, _SEG // _C, chunk, 0)

    col, bgs, grs, sts_spec, tms_spec = _chunk_specs(bsz, nseg, True)
    q3, bg3, do3 = qkvn.reshape(bsz, s, 3 * DN), bg.reshape(bsz, s, LANES), do.reshape(bsz, s, DN)
    (dqkv, dbg), extra = _plugged_call(
        body, plug, _grid_ends((nseg,)), (q3, q3, q3, bg3, g_rows, *states, do3), name="dn_chunk_bwd", grid=(nseg,),
        in_specs=[col(0), col(1), col(2), bgs, grs, sts_spec, tms_spec, col(0)],
        out_specs=[_vspec((3, bsz, _SEG, DN), lambda i: (0, 0, nseg - 1 - i, 0)), col(0)],
        out_shape=[jax.ShapeDtypeStruct((3, bsz, s, DN), F32), jax.ShapeDtypeStruct((bsz, s, DN), F32)],
        scratch_shapes=[pltpu.VMEM((bsz * DN_HEADS, DN_HD, DN_HD), F32)])
    return dqkv.reshape(3, t, DN), dbg.reshape(t, DN), extra


def _chunk_bwd_finish(m):
    q, k, v, beta, decay, decay_t = m["q"], m["k"], m["v"], m["beta"], m["decay"], m["decay_t"]
    eg, et, gl, kb, dl, dl_t, dintra, dintra_t = m["eg"], m["et"], m["gl"], m["kb"], m["dl"], m["dl_t"], m["dintra"], m["dintra_t"]
    dq_dec, dk_tail, dq, dk = m["dq_dec"], m["dk_tail"], m["dq"], m["dk"]
    dgl = jnp.sum(jnp.sum(m["ds_out"] * m["st"], axis=1, keepdims=True), axis=0, keepdims=True)
    dvb, dkbeg = m["drhs"][:, :DN_HD], m["drhs"][:, DN_HD:]
    dkb = dkbeg * eg + m["dkb2"]
    deg = jnp.sum(dkbeg * kb, axis=1, keepdims=True)
    em = (dl * m["kk"] + dintra * m["qk"]) * decay
    em_t = (dl_t * m["kk_t"] + dintra_t * m["qk_t"]) * decay_t
    dgc = jnp.sum(em, axis=1, keepdims=True) - jnp.sum(em_t, axis=1, keepdims=True)
    dq = dq + dq_dec * eg
    deg = deg + jnp.sum(dq_dec * q, axis=1, keepdims=True)
    dk = dk + dk_tail * et
    det = jnp.sum(dk_tail * k, axis=1, keepdims=True)
    dgc = dgc + deg * eg - det * et
    dgc_last = jnp.sum(det * et, axis=0, keepdims=True) + dgl * gl
    rcol = lax.broadcasted_iota(jnp.int32, (_C, 1), 0)
    m["dgc"] = dgc + jnp.where(rcol == _C - 1, dgc_last, 0.0)
    m["dq"] = dq
    m["dk"] = dk + dkb * beta
    m["dbeta"] = jnp.sum(dkb * k, axis=1, keepdims=True) + jnp.sum(dvb * v, axis=1, keepdims=True)
    m["dv"] = dvb * beta


def _mod_fwd(c_all, ada_w_loc, ada_b_loc):
    n, cols = c_all.shape[0], ada_w_loc.shape[1]

    def body(c_ref, w_ref, b_ref, o_ref):
        o_ref[...] = _dot(jax.nn.silu(c_ref[...]), w_ref[...], _NN) + b_ref[...]

    return pl.pallas_call(body, name="mod_fwd", out_shape=jax.ShapeDtypeStruct((n, cols), F32))(c_all, ada_w_loc, ada_b_loc)


def _ada_grad(c_all, dmod_loc, dmod_all):
    d, cols = c_all.shape[1], dmod_loc.shape[1]

    def body(c_ref, dl_ref, da_ref, gw_ref, gb_ref):
        gw_ref[...] = _dot(jax.nn.silu(c_ref[...]), dl_ref[...], _TN)
        gb_ref[...] = jnp.sum(da_ref[...], axis=0, keepdims=True)

    return pl.pallas_call(body, name="ada_grad", out_shape=[jax.ShapeDtypeStruct((d, cols), F32),
                                                           jax.ShapeDtypeStruct((1, dmod_all.shape[1]), F32)])(c_all, dmod_loc, dmod_all)


ELEMENTWISE_BLOCK_BYTES = 3 * 2 ** 19


def _row_tile(r, c=1024):
    fits = [tr for tr in range(16, r + 1, 16) if tr * c * 4 <= ELEMENTWISE_BLOCK_BYTES]
    if not fits:
        return r
    whole = [tr for tr in fits if r % tr == 0]
    return whole[-1] if whole else fits[-1]


def _adamw(w, m, v, grads, name, plug=None):
    r, rest = w.shape[0], w.shape[1:]
    c = math.prod(rest)
    tr = _row_tile(r, c)
    blk = _vspec((tr,) + rest, lambda i: (i,) + (0,) * len(rest))
    n = len(grads)

    def body(*refs):
        w_ref, m_ref, v_ref = refs[:3]
        g_ref, d_ref, mo_ref, vo_ref = refs[3 + n:]
        g = refs[3][...]
        for p in refs[4:3 + n]:
            g = g + p[...]
        g_ref[...] = g
        d_ref[...], mo_ref[...], vo_ref[...] = _adamw_math(w_ref[...], m_ref[...], v_ref[...], g)

    o = jax.ShapeDtypeStruct(w.shape, F32)
    grid = (pl.cdiv(r, tr),)
    res, extra = _plugged_call(body, plug, _grid_ends(grid), (w, m, v, *grads), name=name, grid=grid, in_specs=[blk] * (3 + n),
                               out_specs=[blk] * 4, out_shape=[o] * 4)
    return res if plug is None else (res, extra)


def _adamw_halves(w, m, v, own, other, ic, name):
    if w.ndim == 3:
        r, _, c = w.shape
        tr = _row_tile(r, c)
        blk = _vspec((tr, 1, c // 2), lambda i, j: (i, 0, j))
        half = _vspec((tr, 1, c // 2), lambda i, j: (i, 0, 0))
        grid = (pl.cdiv(r, tr), 2)
    else:
        r, c = w.shape
        tr = _row_tile(r // 2, c)
        per = r // 2 // tr
        blk = _vspec((tr, c), lambda i, j: (j * per + i, 0))
        half = _vspec((tr, c), lambda i, j: (i, 0))
        grid = (per, 2)

    def body(ic_ref, w_ref, m_ref, v_ref, own_ref, other_ref, g_ref, d_ref, mo_ref, vo_ref):
        g = jnp.where(pl.program_id(1) == ic_ref[0], own_ref[...], other_ref[...])
        g_ref[...] = g
        d_ref[...], mo_ref[...], vo_ref[...] = _adamw_math(w_ref[...], m_ref[...], v_ref[...], g)

    o = jax.ShapeDtypeStruct(w.shape, F32)
    return pl.pallas_call(body, name=name, grid=grid,
                          in_specs=[pl.BlockSpec(memory_space=pltpu.SMEM), blk, blk, blk, half, half], out_specs=[blk] * 4,
                          out_shape=[o] * 4)(ic, w, m, v, own, other)


def _adamw_math(w, m, v, g):
    m_new = ADAM_B1 * m + (1.0 - ADAM_B1) * g
    v_new = ADAM_B2 * v + (1.0 - ADAM_B2) * jnp.square(g)
    m_hat = m_new / (1.0 - ADAM_B1 ** ADAM_STEP)
    v_hat = v_new / (1.0 - ADAM_B2 ** ADAM_STEP)
    return -ADAM_LR * (m_hat / (jnp.sqrt(v_hat) + ADAM_EPS) + ADAM_WD * w), m_new, v_new


def _adamw_small(ws, ms, vs, gs, name):
    n = len(ws)

    def body(*refs):
        for i in range(n):
            w_ref, m_ref, v_ref, g_ref = (refs[j * n + i] for j in range(4))
            go_ref, d_ref, mo_ref, vo_ref = refs[4 * n + 4 * i:4 * n + 4 * i + 4]
            g = g_ref[...]
            go_ref[...] = g
            d_ref[...], mo_ref[...], vo_ref[...] = _adamw_math(w_ref[...], m_ref[...], v_ref[...], g)

    res = pl.pallas_call(body, name=name, out_shape=[jax.ShapeDtypeStruct(a.shape, F32) for a in ws for _ in range(4)])(
        *ws, *ms, *vs, *gs)
    return [res[4 * i:4 * i + 4] for i in range(n)]


def _sum_lead(x, name, rows_apart=False):
    p, r, c = x.shape
    tr = _row_tile(r, c)
    mid = (1,) if rows_apart else ()

    def body(x_ref, o_ref):
        acc = x_ref[0].astype(F32)
        for i in range(1, p):
            acc = acc + x_ref[i].astype(F32)
        o_ref[...] = acc.reshape(o_ref.shape)

    return pl.pallas_call(body, name=name, grid=(pl.cdiv(r, tr),), in_specs=[_vspec((p, tr, c), lambda i: (0, i, 0))],
                          out_specs=_vspec((tr,) + mid + (c,), lambda i: (i,) + (0,) * (1 + len(mid))),
                          out_shape=jax.ShapeDtypeStruct((r,) + mid + (c,), F32))(x)


def _allgather8(x_shard, name, plug=None):
    m_per, n = x_shard.shape

    def body(x_ref, out_ref, send_sems, recv_sems, local_sem):
        x, y, c = lax.axis_index("x"), lax.axis_index("y"), lax.axis_index("c")
        me, sibling = (x, y, c), (x, y, 1 - c)
        chips = [(1 - x, y), (x, 1 - y), (1 - x, 1 - y)]

        def rows(px, py, pc):
            return out_ref.at[pl.ds((4 * px + 2 * py + pc) * m_per, m_per), :]

        def copy(k, block, to, src=None):
            return pltpu.make_async_remote_copy(
                src_ref=rows(*block) if src is None else src, dst_ref=rows(*block), send_sem=send_sems.at[k],
                recv_sem=recv_sems.at[k], device_id=to, device_id_type=MESH)

        mine = pltpu.make_async_copy(x_ref, rows(*me), local_sem)
        mine.start()
        first = [copy(0, me, sibling, src=x_ref)]
        first += [copy(1 + j, me, (*chip, c), src=x_ref) for j, chip in enumerate(chips)]
        for cp in first:
            cp.start()
        passed = [copy(4 + j, (*chip, c), sibling) for j, chip in enumerate(chips)]
        for j, chip in enumerate(chips):
            copy(1 + j, (*chip, c), me).wait_recv()
            passed[j].start()
        copy(0, sibling, me).wait_recv()
        for j, chip in enumerate(chips):
            copy(4 + j, (*chip, 1 - c), me).wait_recv()
        for cp in first + passed:
            cp.wait_send()
        mine.wait()

    grid = (1,)
    (out,), extra = _plugged_call(
        body, plug, _grid_ends(grid), (x_shard,), name=name, grid=grid, out_shape=[jax.ShapeDtypeStruct((8 * m_per, n), x_shard.dtype)],
        in_specs=[pl.BlockSpec(memory_space=pltpu.VMEM)], out_specs=[pl.BlockSpec(memory_space=pltpu.VMEM)],
        scratch_shapes=[pltpu.SemaphoreType.DMA((7,)), pltpu.SemaphoreType.DMA((7,)), pltpu.SemaphoreType.DMA])
    return out if plug is None else (out, extra)


_HBM = pl.BlockSpec(memory_space=pltpu.HBM)


def _mesh_place():
    x, y, c = lax.axis_index("x"), lax.axis_index("y"), lax.axis_index("c")
    return x, y, c, 2 * x + y, [(1 - x, y), (x, 1 - y), (1 - x, 1 - y)]


def _gather_plug(shards):
    n = len(shards)

    def half(ref, c, lead=None):
        r, cols = ref.shape[-2] // 2, ref.shape[-1] // 2
        if r % 16 == 0:
            rows = pl.ds(pl.multiple_of(c * r, 16), r)
            return ref.at[rows, :] if lead is None else ref.at[lead, rows, :]
        lanes = pl.ds(pl.multiple_of(c * cols, LANES), cols)
        return ref.at[:, lanes] if lead is None else ref.at[lead, :, lanes]

    def copies(ins, outs, send, recv):
        x, y, c, me, chips = _mesh_place()
        ici, fwd, fwd_in = [], [], []
        for i in range(n):
            for j, (px, py) in enumerate(chips):
                q = 2 * px + py
                ici.append((pltpu.make_async_remote_copy(
                    src_ref=half(ins[i], c), dst_ref=half(outs[i], c, me), send_sem=send.at[6 * i + j], recv_sem=recv.at[6 * i + j],
                    device_id=(px, py, c), device_id_type=MESH),
                    pltpu.make_async_remote_copy(
                    src_ref=half(ins[i], c), dst_ref=half(outs[i], c, q), send_sem=send.at[6 * i + j], recv_sem=recv.at[6 * i + j],
                    device_id=(px, py, c), device_id_type=MESH)))
                fwd.append(pltpu.make_async_remote_copy(
                    src_ref=half(outs[i], c, q), dst_ref=half(outs[i], c, q), send_sem=send.at[6 * i + 3 + j],
                    recv_sem=recv.at[6 * i + 3 + j], device_id=(x, y, 1 - c), device_id_type=MESH))
                fwd_in.append(pltpu.make_async_remote_copy(
                    src_ref=half(outs[i], 1 - c, q), dst_ref=half(outs[i], 1 - c, q), send_sem=send.at[6 * i + 3 + j],
                    recv_sem=recv.at[6 * i + 3 + j], device_id=(x, y, 1 - c), device_id_type=MESH))
        return ici, fwd, fwd_in, me

    def start(ins, outs, send, recv, loc):
        ici, _, _, me = copies(ins, outs, send, recv)
        for i in range(n):
            pltpu.make_async_copy(ins[i], outs[i].at[me], loc.at[i]).start()
        for out_cp, _ in ici:
            out_cp.start()

    def finish(ins, outs, send, recv, loc):
        ici, fwd, fwd_in, me = copies(ins, outs, send, recv)
        for (_, in_cp), f in zip(ici, fwd):
            in_cp.wait_recv()
            f.start()
        for f in fwd_in:
            f.wait_recv()
        for (out_cp, _), f in zip(ici, fwd):
            out_cp.wait_send()
            f.wait_send()
        for i in range(n):
            pltpu.make_async_copy(ins[i], outs[i].at[me], loc.at[i]).wait()

    return dict(ins=list(shards), out_shape=[jax.ShapeDtypeStruct((N_CHIPS,) + a.shape, a.dtype) for a in shards],
                scratch=[pltpu.SemaphoreType.DMA((6 * n,)), pltpu.SemaphoreType.DMA((6 * n,)), pltpu.SemaphoreType.DMA((n,))],
                start=start, finish=finish)


def _exchange_plug(pieces):
    n = len(pieces)

    def copies(ins, outs, send, recv):
        x, y, c, me, chips = _mesh_place()
        out_cps, in_cps = [], []
        for i in range(n):
            for j, (px, py) in enumerate(chips):
                q = 2 * px + py
                out_cps.append(pltpu.make_async_remote_copy(src_ref=ins[i].at[q], dst_ref=outs[i].at[me], send_sem=send.at[3 * i + j],
                                                            recv_sem=recv.at[3 * i + j], device_id=(px, py, c), device_id_type=MESH))
                in_cps.append(pltpu.make_async_remote_copy(src_ref=ins[i].at[me], dst_ref=outs[i].at[q], send_sem=send.at[3 * i + j],
                                                           recv_sem=recv.at[3 * i + j], device_id=(px, py, c), device_id_type=MESH))
        return out_cps, in_cps, me

    def start(ins, outs, send, recv, loc):
        out_cps, _, me = copies(ins, outs, send, recv)
        for i in range(n):
            pltpu.make_async_copy(ins[i].at[me], outs[i].at[me], loc.at[i]).start()
        for cp in out_cps:
            cp.start()

    def finish(ins, outs, send, recv, loc):
        out_cps, in_cps, me = copies(ins, outs, send, recv)
        for cp in in_cps:
            cp.wait_recv()
        for cp in out_cps:
            cp.wait_send()
        for i in range(n):
            pltpu.make_async_copy(ins[i].at[me], outs[i].at[me], loc.at[i]).wait()

    return dict(ins=list(pieces), out_shape=[jax.ShapeDtypeStruct(a.shape, a.dtype) for a in pieces],
                scratch=[pltpu.SemaphoreType.DMA((3 * n,)), pltpu.SemaphoreType.DMA((3 * n,)), pltpu.SemaphoreType.DMA((n,))],
                start=start, finish=finish)


def _plugged_call(body, plug, first_last, args, *, name, grid, in_specs, out_specs, out_shape, scratch_shapes=(), aliases=None):
    in_specs, out_specs, out_shape, scratch_shapes = list(in_specs), list(out_specs), list(out_shape), list(scratch_shapes)
    aliases = dict(aliases or {})
    if plug is None:
        return pl.pallas_call(body, name=name, grid=grid, in_specs=in_specs, out_specs=out_specs, out_shape=out_shape,
                              scratch_shapes=scratch_shapes, input_output_aliases=aliases)(*args), []
    n_in, n_out, n_sc = len(in_specs), len(out_specs), len(scratch_shapes)
    p_in, p_out = len(plug["ins"]), len(plug["out_shape"])

    def full(*refs):
        ins, refs = refs[:n_in], refs[n_in:]
        pins, refs = refs[:p_in], refs[p_in:]
        outs, refs = refs[:n_out], refs[n_out:]
        pouts, refs = refs[:p_out], refs[p_out:]
        scr, psems = refs[:n_sc], refs[n_sc:]
        first, last = first_last()

        @pl.when(first)
        def _():
            plug["start"](pins, pouts, *psems)

        body(*ins, *outs, *scr)

        @pl.when(last)
        def _():
            plug["finish"](pins, pouts, *psems)

    res = pl.pallas_call(full, name=name, grid=grid, in_specs=in_specs + [_HBM] * p_in, out_specs=out_specs + [_HBM] * p_out,
                         out_shape=out_shape + plug["out_shape"], scratch_shapes=scratch_shapes + plug["scratch"],
                         input_output_aliases=aliases)(*args, *plug["ins"])
    return res[:n_out], res[n_out:]


def _grid_ends(grid):
    def ends():
        first = last = None
        for ax, n in enumerate(grid):
            i = pl.program_id(ax)
            first = (i == 0) if first is None else first & (i == 0)
            last = (i == n - 1) if last is None else last & (i == n - 1)
        return first, last
    return ends


def _pair_presum(pieces, name):
    n, r, cols = pieces.shape
    by_cols = (cols // 2) % LANES == 0
    half_shape = (n, r, cols // 2) if by_cols else (n, r // 2, cols)

    def body(p_ref, o_ref, mine_ref, land_ref, send_sem, recv_sem, local_sem):
        x, y, c = lax.axis_index("x"), lax.axis_index("y"), lax.axis_index("c")

        def half(which):
            if by_cols:
                return p_ref.at[:, :, pl.ds(pl.multiple_of(which * (cols // 2), LANES), cols // 2)]
            return p_ref.at[:, pl.ds(pl.multiple_of(which * (r // 2), 16), r // 2), :]

        push = pltpu.make_async_remote_copy(src_ref=half(1 - c), dst_ref=land_ref, send_sem=send_sem, recv_sem=recv_sem,
                                            device_id=(x, y, 1 - c), device_id_type=MESH)
        own = pltpu.make_async_copy(half(c), mine_ref, local_sem)
        push.start()
        own.start()
        own.wait()
        push.wait_recv()
        for i in range(n):
            o_ref[i] = (mine_ref[i].astype(F32) + land_ref[i].astype(F32)).astype(BF16)
        push.wait_send()

    return pl.pallas_call(
        body, name=name, out_shape=jax.ShapeDtypeStruct(half_shape, BF16), in_specs=[_HBM],
        out_specs=pl.BlockSpec(memory_space=pltpu.VMEM),
        scratch_shapes=[pltpu.VMEM(half_shape, BF16), pltpu.VMEM(half_shape, BF16), pltpu.SemaphoreType.DMA,
                        pltpu.SemaphoreType.DMA, pltpu.SemaphoreType.DMA],
    )(pieces)


def _sibling_plug(arrs):
    n = len(arrs)

    def copies(ins, outs, send, recv):
        sibling = (lax.axis_index("x"), lax.axis_index("y"), 1 - lax.axis_index("c"))
        return [pltpu.make_async_remote_copy(src_ref=ins[i], dst_ref=outs[i], send_sem=send.at[i], recv_sem=recv.at[i],
                                             device_id=sibling, device_id_type=MESH) for i in range(n)]

    def start(ins, outs, send, recv):
        for cp in copies(ins, outs, send, recv):
            cp.start()

    def finish(ins, outs, send, recv):
        for cp in copies(ins, outs, send, recv):
            cp.wait()

    return dict(ins=list(arrs), out_shape=[jax.ShapeDtypeStruct(a.shape, a.dtype) for a in arrs],
                scratch=[pltpu.SemaphoreType.DMA((n,)), pltpu.SemaphoreType.DMA((n,))], start=start, finish=finish)


_SEGS = ((0, AQ, C_Q), (AQ, AKV, C_K), (AQ + AKV, AKV, C_V), (AQ + 2 * AKV, 3 * DN, C_DQKV), (2304, DN, C_DZ),
         (2816, 2 * DN_HEADS, C_BD), (2824, D_MODEL, C_GA), (3848, D_MODEL, C_GD))
SHARD_ROWS = IN_DIM // N_CHIPS


def _to_padded(w4):
    wt = w4.reshape(IN_DIM, w4.shape[2])
    parts = [wt[o:o + n] for o, n, _ in sorted(_SEGS, key=lambda sg: sg[2])]
    return jnp.concatenate(parts + [jnp.zeros((IN_PAD - IN_DIM, wt.shape[1]), wt.dtype)], axis=0)


def _from_padded(gt):
    return jnp.concatenate([gt[ps:ps + n] for _, n, ps in sorted(_SEGS)], axis=0).reshape(N_CHIPS, SHARD_ROWS, gt.shape[1])


def _lane_vec(a):
    return jnp.zeros((1, LANES), F32).at[0, DN_HEADS:2 * DN_HEADS].set(a)


_ROW_SHARDED = ("w_in", "w_out", "ffn_w_down")
_FFN = ("ffn_w_up", "ffn_w_down")
_LATE_MIXER = ("w_attn_branch", "w_dn_branch", "w_out")
_PRESUM = ("w_in", "ffn_w_up")


def _pieces(k, a):
    if a.ndim == 3:
        return a
    if k in _ROW_SHARDED:
        return a.reshape(N_CHIPS, a.shape[0] // N_CHIPS, a.shape[1]).astype(BF16)
    return jnp.transpose(a.reshape(a.shape[0], N_CHIPS, a.shape[1] // N_CHIPS), (1, 0, 2)).astype(BF16)


def _assemble(k, a):
    if k in _ROW_SHARDED:
        return a.reshape(-1, a.shape[2])
    return jnp.transpose(a, (1, 0, 2)).reshape(a.shape[1], -1)


def _device_step(x2, tgt2, mod, p, bsz, shards=None):
    d = D_MODEL
    on_mesh = shards is not None
    p = dict(p)
    sh1, sc1, g1, sh2, sc2, g2 = [mod[:, i * d:(i + 1) * d].reshape(bsz, 1, d) for i in range(N_MOD)]
    alog_v, dt_v = _lane_vec(p["dn_a_log"]), _lane_vec(p["dn_dt_bias"])
    sinks = p["attn_sinks"].reshape(1, AQ_HEADS)
    u1 = _pre_fwd(x2, p["norm_mix_pre"], sc1, sh1, "pre1_fwd")
    if on_mesh:
        proj, got = _mm(u1, p["w_in"], "nt", F32, "mm_proj", _gather_plug(shards["late_mixer"]))
        p.update({k: _assemble(k, a) for k, a in zip(_LATE_MIXER, got)})
    else:
        proj = _mm(u1, p["w_in"], "nt", F32, "mm_proj")
    bias = _bias_build(p["rel_bias"])
    y_attn = _attn_fwd(proj, bias, sinks, bsz)
    qkvn = _dn_prep_fwd(proj, p["dn_conv_w"], bsz)
    bg = _bg_fwd(proj, alog_v, dt_v, bsz)
    nc = x2.shape[0] // bsz // DN_CHUNK
    g_rows = jnp.transpose(bg[:, DN_HEADS:2 * DN_HEADS].reshape(bsz, nc, DN_CHUNK, DN_HEADS), (0, 3, 1, 2))
    o, states, got = _dn_chunk_fwd(qkvn, bg, g_rows, bsz, _gather_plug(shards["ffn"][:1]) if on_mesh else None)
    for k, a in zip(_FFN[:1], got):
        p[k] = _assemble(k, a)
    y_dn = _dn_out_fwd(o, proj, p["dn_norm_w"])
    ya = _mm(y_attn, p["w_attn_branch"], "nn", BF16, "mm_ya")
    yd = _mm(y_dn, p["w_dn_branch"], "nn", BF16, "mm_yd")
    merged = _merge_fwd(proj, ya, yd)
    y1 = _mm(merged, p["w_out"], "nn", F32, "mm_y1")
    h1, u2 = _post_pre_fwd(x2, y1, p["norm_mix_post"], g1, p["norm_ffn_pre"], sc2, sh2, "post1_pre2_fwd")
    if on_mesh:
        up, got = _mm(u2, p["ffn_w_up"], "nn", F32, "mm_up", _gather_plug(shards["ffn"][1:]))
        p["ffn_w_down"] = _assemble("ffn_w_down", got[0])
    else:
        up = _mm(u2, p["ffn_w_up"], "nn", F32, "mm_up")
    act = _ffn_act_fwd(up, p["ffn_conv_w"], bsz)
    y2 = _mm(act, p["ffn_w_down"], "nn", F32, "mm_y2")
    dh2, dy2, g_ffn_post, dg2, sq = _post_loss_bwd(h1, y2, p["norm_ffn_post"], g2, tgt2, "post2_loss_bwd")
    g = {}
    g["norm_ffn_post"] = g_ffn_post
    dact = _mm(dy2, p["ffn_w_down"], "nt", F32, "mm_dact")
    g["ffn_w_down"] = _mm(act, dy2, "tn", BF16, "mm_dwdown")
    dupg, dupv, dcwg, dcwv, got_down = _ffn_act_bwd(
        up, p["ffn_conv_w"], dact, bsz, _exchange_plug([_pieces("ffn_w_down", g["ffn_w_down"])]) if on_mesh else None)
    g["ffn_conv_w"] = jnp.concatenate([dcwg, dcwv], axis=1)
    g["ffn_w_up"] = jnp.concatenate([_mm(u2, dupg, "tn", BF16, "mm_dwup_gate", split=N_CHIPS // 2),
                                     _mm(u2, dupv, "tn", BF16, "mm_dwup_val", split=N_CHIPS // 2)], axis=0)
    du2 = _mm(dupg, p["ffn_w_up"], "nt", F32, "mm_du2_gate", b_kblock=0)
    du2 = _mm(dupv, p["ffn_w_up"], "nt", F32, "mm_du2_val", b_kblock=1, add=du2)
    dh1, dy1, g["norm_ffn_pre"], dsc2, dsh2, g["norm_mix_post"], dg1 = _pre_post_bwd(
        h1, p["norm_ffn_pre"], sc2, sh2, du2, dh2, y1, p["norm_mix_post"], g1, "pre2_post1_bwd")
    dmerged = _mm(dy1, p["w_out"], "nt", BF16, "mm_dmerged")
    g["w_out"] = _mm(merged, dy1, "tn", BF16, "mm_dwout")
    dproj = lax.empty((x2.shape[0], IN_PAD), BF16)
    dproj, dya = _branch_bwd(dproj, proj, ya, dmerged, C_GA, "merge_bwd_attn")
    dproj, dyd = _branch_bwd(dproj, proj, yd, dmerged, C_GD, "merge_bwd_dn")
    dy_attn = _mm(dya, p["w_attn_branch"], "nt", BF16, "mm_dyattn")
    g["w_attn_branch"] = _mm(y_attn, dya, "tn", BF16, "mm_dwab", split=N_CHIPS)
    dy_dn = _mm(dyd, p["w_dn_branch"], "nt", F32, "mm_dydn")
    g["w_dn_branch"] = _mm(y_dn, dyd, "tn", BF16, "mm_dwdb", split=N_CHIPS)
    do, dproj, g["dn_norm_w"] = _dn_out_bwd(dproj, o, proj, p["dn_norm_w"], dy_dn)
    def plug_for(names):
        if not on_mesh:
            return None
        return _exchange_plug([_pair_presum(_pieces(k, g[k]), "presum_" + k) if k in _PRESUM else _pieces(k, g[k]) for k in names])

    early = ("w_out", "w_attn_branch", "w_dn_branch")
    dqkvn, dbg4, got_up = _dn_chunk_bwd(qkvn, bg, g_rows, states, do, bsz, plug_for(_FFN[:1]))
    got_ffn = list(got_up) + list(got_down)
    dproj, g["dn_conv_w"] = _dn_prep_bwd(dproj, proj, p["dn_conv_w"], dqkvn, bsz)
    dproj, dk, dv, dbias, g["attn_sinks"], got_early = _attn_bwd(dproj, proj, bias, sinks, dy_attn, bsz, plug_for(early))
    dproj, g["dn_a_log"], g["dn_dt_bias"] = _bg_bwd(dproj, proj, alog_v, dt_v, dbg4, dk, dv, bsz)
    g["rel_bias"] = _bias_grad(dbias)
    g["w_in"] = _from_padded(_mm(dproj, u1, "tn", BF16, "mm_dwin"))
    if on_mesh:
        du1, got_in = _mm(dproj, p["w_in"], "nn", F32, "mm_du1", plug_for(("w_in",)))
        g.update(zip(_FFN + early + ("w_in",), list(got_ffn) + list(got_early) + list(got_in)))
    else:
        du1 = _mm(dproj, p["w_in"], "nn", F32, "mm_du1")
    dx, g["norm_mix_pre"], dsc1, dsh1 = _pre_bwd(x2, p["norm_mix_pre"], sc1, sh1, du1, dh1, "pre1_bwd")
    dmod = jnp.concatenate([dsh1, dsc1, dg1, dsh2, dsc2, dg2], axis=-1).reshape(bsz, N_MOD * d)
    return sq, dx, dmod, g


_SMALL = (("norm_mix_pre", D_MODEL), ("norm_mix_post", D_MODEL), ("norm_ffn_pre", D_MODEL), ("norm_ffn_post", D_MODEL),
          ("dn_norm_w", DN_HD), ("dn_a_log", LANES), ("dn_dt_bias", LANES), ("attn_sinks", AQ_HEADS * LANES),
          ("rel_bias", AQ_HEADS * LANES), ("dn_conv_w", DN_CONV * 3 * DN), ("ffn_conv_w", FFN_CONV * 2 * D_FF),
          ("loss_sq", D_MODEL))


def _pack_rows(parts, rows):
    flat = jnp.concatenate([a.reshape(-1) for a in parts])
    return jnp.concatenate([flat, jnp.zeros((rows * LANES - flat.shape[0],), F32)]).reshape(rows, LANES)


_W_NAMES = ("ada_w", "ada_b", "norm_mix_pre", "norm_mix_post", "norm_ffn_pre", "norm_ffn_post", "w_in", "dn_conv_w", "dn_a_log",
            "dn_dt_bias", "dn_norm_w", "attn_sinks", "rel_bias", "w_attn_branch", "w_dn_branch", "w_out", "ffn_w_up", "ffn_conv_w",
            "ffn_w_down")
_BIG = ("w_in", "w_attn_branch", "w_dn_branch", "w_out", "ffn_w_up", "ffn_w_down")


def kernel(x, c, *rest):
    nw = len(_W_NAMES)
    w = dict(zip(_W_NAMES, rest[:nw]))
    loss_target = rest[nw]
    m = dict(zip(_W_NAMES, rest[nw + 1:2 * nw + 1]))
    v = dict(zip(_W_NAMES, rest[2 * nw + 1:3 * nw + 1]))
    ix, iy, ic = lax.axis_index("x"), lax.axis_index("y"), lax.axis_index("c")
    chip, dev = 2 * ix + iy, 4 * ix + 2 * iy + ic
    bsz, s, d = x.shape
    t = bsz * s
    n_dev = 8

    front_rows = 64
    front = _pack_rows([c, w["dn_conv_w"], w["ffn_conv_w"]], front_rows)
    w_in_t = jnp.swapaxes(w["w_in"][0], 0, 1).astype(BF16)
    w_in_cut = SHARD_ROWS // 32 * 16
    front_all, (w_in_lo,) = _allgather8(front, "ag_front", _gather_plug([w_in_t[:w_in_cut]]))
    front_all = front_all.reshape(n_dev, front_rows * LANES)
    n_c, n_dc, n_fc = bsz * d, DN_CONV * 3 * DN // N_CHIPS, FFN_CONV * 2 * D_FF // N_CHIPS
    c_all = front_all[:, :n_c].reshape(n_dev * bsz, d)
    per_chip = front_all[0::2]
    dn_conv_full = jnp.transpose(per_chip[:, n_c:n_c + n_dc].reshape(N_CHIPS, DN_CONV, -1), (1, 0, 2)).reshape(DN_CONV, 3 * DN)
    ffn_conv_full = jnp.transpose(per_chip[:, n_c + n_dc:n_c + n_dc + n_fc].reshape(N_CHIPS, FFN_CONV, -1), (1, 0, 2)).reshape(FFN_CONV, 2 * D_FF)

    mod_cols = N_MOD * d // N_CHIPS
    ada_b_loc = lax.dynamic_slice(w["ada_b"], (0, chip * mod_cols), (1, mod_cols))
    mod_part = _mod_fwd(c_all, w["ada_w"][0], ada_b_loc)
    mod_all, (w_in_hi,) = _allgather8(mod_part, "ag_mod", _gather_plug([w_in_t[w_in_cut:]]))
    mod_all = mod_all.reshape(n_dev, n_dev * bsz, mod_cols)[0::2]
    mod = jnp.transpose(lax.dynamic_slice(mod_all, (0, dev * bsz, 0), (N_CHIPS, bsz, mod_cols)), (1, 0, 2)).reshape(bsz, N_MOD * d)

    p = {"w_in": _to_padded(jnp.concatenate([w_in_lo, w_in_hi], axis=1))}
    shards = {"late_mixer": [w[k][0].astype(BF16) for k in _LATE_MIXER], "ffn": [w[k][0].astype(BF16) for k in _FFN]}
    for k in ("norm_mix_pre", "norm_mix_post", "norm_ffn_pre", "norm_ffn_post", "dn_norm_w", "attn_sinks"):
        p[k] = w[k]
    p["dn_a_log"], p["dn_dt_bias"], p["rel_bias"] = w["dn_a_log"][0], w["dn_dt_bias"][0], w["rel_bias"]
    p["dn_conv_w"], p["ffn_conv_w"] = dn_conv_full, ffn_conv_full

    sq, dx, dmod, g = _device_step(x.reshape(t, d), loss_target.reshape(t, d), mod, p, bsz, shards)
    g["dn_a_log"], g["dn_dt_bias"], g["loss_sq"] = g["dn_a_log"].reshape(-1), g["dn_dt_bias"].reshape(-1), sq
    small_rows = 336
    small = _pack_rows([dmod] + [g[k] for k, _ in _SMALL], small_rows)
    small_all = _allgather8(small, "ag_small").reshape(n_dev, small_rows, LANES)
    n_dm = bsz * N_MOD * d
    dmod_all = small_all.reshape(n_dev, -1)[:, :n_dm].reshape(n_dev * bsz, N_MOD * d)
    tot = _sum_lead(small_all, "sum_small").reshape(-1)
    gs, off = {}, n_dm
    for k, n in _SMALL:
        gs[k] = tot[off:off + n]
        off += n
    loss = 0.5 * jnp.sum(gs["loss_sq"])
    grad = {}
    grad["ada_w"], grad["ada_b"] = _ada_grad(c_all, lax.dynamic_slice(dmod_all, (0, chip * mod_cols), (n_dev * bsz, mod_cols)), dmod_all)
    for k in ("norm_mix_pre", "norm_mix_post", "norm_ffn_pre", "norm_ffn_post", "dn_norm_w"):
        grad[k] = gs[k]
    grad["dn_a_log"] = gs["dn_a_log"][DN_HEADS:2 * DN_HEADS]
    grad["dn_dt_bias"] = gs["dn_dt_bias"][DN_HEADS:2 * DN_HEADS]
    grad["attn_sinks"] = gs["attn_sinks"].reshape(AQ_HEADS, LANES)[:, 0]
    grad["rel_bias"] = gs["rel_bias"].reshape(AQ_HEADS, LANES)[:, :REL_BUCKETS].T
    grad["dn_conv_w"] = lax.dynamic_slice(gs["dn_conv_w"].reshape(DN_CONV, 3 * DN), (0, chip * (3 * DN // N_CHIPS)), (DN_CONV, 3 * DN // N_CHIPS))
    grad["ffn_conv_w"] = lax.dynamic_slice(gs["ffn_conv_w"].reshape(FFN_CONV, 2 * D_FF), (0, chip * (2 * D_FF // N_CHIPS)), (FFN_CONV, 2 * D_FF // N_CHIPS))

    mine = [_sum_lead(g[k], "sum_" + k, rows_apart=(k == "w_in")) for k in _BIG]
    out = {}
    out["ada_w"], theirs = _adamw(w["ada_w"][0], m["ada_w"][0], v["ada_w"][0], [grad["ada_w"]], "adamw_ada_w", _sibling_plug(mine))
    for k, a, b in zip(_BIG, mine, theirs):
        core = ic.reshape(1).astype(jnp.int32)
        if k == "w_in":
            tr = lambda z: jnp.transpose(z, (2, 0, 1))
            out[k] = [jnp.transpose(r, (1, 2, 0)) for r in _adamw_halves(tr(w[k]), tr(m[k]), tr(v[k]), a, b, core, "adamw_" + k)]
        elif k in _PRESUM:
            out[k] = _adamw_halves(w[k][0], m[k][0], v[k][0], a, b, core, "adamw_" + k)
        else:
            out[k] = _adamw(w[k][0], m[k][0], v[k][0], [a, b], "adamw_" + k)
    small_names = [k for k in _W_NAMES if k not in _BIG and k != "ada_w"]
    res_small = _adamw_small([w[k] for k in small_names], [m[k] for k in small_names], [v[k] for k in small_names],
                             [grad[k].reshape(w[k].shape) for k in small_names], "adamw_small")
    out.update(zip(small_names, res_small))
    for k in _BIG + ("ada_w",):
        out[k] = [r.reshape(w[k].shape) for r in out[k]]
    grads, deltas, new_m, new_v = ([out[k][i] for k in _W_NAMES] for i in range(4))
    return (loss, dx.reshape(bsz, s, d), *grads, *deltas, *new_m, *new_v)
```

```python
import functools
import math

import numpy as np
import jax
import jax.numpy as jnp
from jax import lax
from jax.experimental import pallas as pl
from jax.experimental.pallas import tpu as pltpu

F32 = jnp.float32
BF16 = jnp.bfloat16
MESH = pl.DeviceIdType.MESH

D_MODEL = 1024
N_MOD = 6
AQ_HEADS, AKV_HEADS, A_HD, WINDOW = 8, 2, 64, 128
REL_BUCKETS, REL_MAX_DIST = 32, 128
DN_HEADS, DN_HD, DN_CONV, DN_CHUNK = 4, 128, 4, 64
D_FF, FFN_CONV = 2816, 3
RMS_EPS, L2_EPS, NEG_INF = 1e-6, 1e-6, -1e30
AQ, AKV, DN = AQ_HEADS * A_HD, AKV_HEADS * A_HD, DN_HEADS * DN_HD
IN_DIM = AQ + 2 * AKV + 3 * DN + DN + 2 * DN_HEADS + 2 * D_MODEL
C_DQKV, C_Q, C_DZ, C_GA, C_GD, C_K, C_V, C_BD = 0, 1536, 2048, 2560, 3584, 4608, 4736, 4864
IN_PAD = 4992
LANES = 128
N_CHIPS = 4

ADAM_LR, ADAM_B1, ADAM_B2, ADAM_EPS, ADAM_WD, ADAM_STEP = 0.001, 0.9, 0.999, 1e-08, 0.01, 10


def _pick(n, cap):
    best = None
    for t in range(LANES, cap + 1, LANES):
        if n % t == 0:
            best = t
    return best if best is not None else n


def _vspec(shape, index_map):
    return pl.BlockSpec(shape, index_map)


MM_VMEM_BUDGET = 40 * 2 ** 20
GRID_STEP_S = 0.35e-6
HBM_BYTES_PER_S = 3.0e12
MXU_FLOPS_PER_S = 9.0e14
MXU_DIM = 256


def _mm_tiles(m, n, k, mode, in_bytes, out_bytes, split=1):
    best = None
    for tm in [t for t in range(LANES, m + 1, LANES) if m % t == 0]:
        for tn in [t for t in range(LANES, n // split + 1, LANES) if (n // split) % t == 0]:
            a, b, o = k * tm * in_bytes, k * tn * in_bytes, tm * tn * out_bytes
            if 2 * (a + b + o) + (a if mode == "tn" else 0) > MM_VMEM_BUDGET:
                continue
            hbm_s = (m * k * in_bytes + (m // tm) * n * k * in_bytes + m * n * out_bytes) / HBM_BYTES_PER_S
            mxu_s = 2 * m * n * k / (MXU_FLOPS_PER_S * min(1.0, tm / MXU_DIM) * min(1.0, tn / MXU_DIM))
            cost = (m // tm) * (n // tn) * GRID_STEP_S + max(hbm_s, mxu_s)
            if best is None or cost < best[0]:
                best = (cost, tm, tn)
    return best[1], best[2]


def _mm(a, b, mode, out_dtype, name, plug=None, split=1, b_kblock=0, add=None):
    if mode == "nn":
        (m, k), n = a.shape, b.shape[1]
        dims = (((1,), (0,)), ((), ()))
    elif mode == "nt":
        (m, k), n = a.shape, b.shape[0]
        dims = (((1,), (1,)), ((), ()))
    else:
        (k, m), n = a.shape, b.shape[1]
        dims = (((0,), (0,)), ((), ()))
    tm, tn = _mm_tiles(m, n, k, mode, a.dtype.itemsize, jnp.dtype(out_dtype).itemsize, split)
    if mode == "tn":
        a_spec = _vspec((k, tm), lambda i, j: (0, i))
    else:
        a_spec = _vspec((tm, k), lambda i, j: (i, 0))
    if mode == "nt":
        b_spec = _vspec((tn, k), lambda i, j: (j, b_kblock))
    else:
        b_spec = _vspec((k, tn), lambda i, j: (0, j))
    in_specs, args = [a_spec, b_spec], (a, b)
    if add is not None:
        in_specs, args = in_specs + [_vspec((tm, tn), lambda i, j: (i, j))], (a, b, add)

    def body(a_ref, b_ref, *rest):
        o_ref = rest[-1]
        acc = lax.dot_general(a_ref[...].astype(BF16), b_ref[...].astype(BF16), dims, preferred_element_type=F32)
        if add is not None:
            acc = acc + rest[0][...]
        o_ref[...] = acc.astype(out_dtype).reshape(o_ref.shape)

    grid = (m // tm, n // tn)
    if split == 1:
        out_spec, out_shape = _vspec((tm, tn), lambda i, j: (i, j)), (m, n)
    else:
        per = n // split // tn
        out_spec, out_shape = _vspec((1, tm, tn), lambda i, j: (j // per, i, j % per)), (split, m, n // split)
    (out,), extra = _plugged_call(body, plug, _grid_ends(grid), args, name=name, grid=grid, in_specs=in_specs,
                                  out_specs=[out_spec], out_shape=[jax.ShapeDtypeStruct(out_shape, out_dtype)])
    return out if plug is None else (out, extra)


def _rms(x, w):
    return (x * lax.rsqrt(jnp.mean(x * x, axis=-1, keepdims=True) + RMS_EPS)) * w


def _pre_f(x, w, sc, sh):
    return _rms(x, w) * (1.0 + sc) + sh


def _post_f(y, w, g):
    return g * _rms(y, w)


def _tok_grid(t, bsz, ts):
    nt = t // bsz // ts
    return nt, (bsz, nt)


def _pre_fwd(x, w, sc, sh, name, ts=512, plug=None):
    t, d = x.shape
    bsz = sc.shape[0]
    nt, grid = _tok_grid(t, bsz, ts)
    row = _vspec((ts, d), lambda b, i: (b * nt + i, 0))
    vec = _vspec((1, d), lambda b, i: (0, 0))
    bvec = _vspec((1, 1, d), lambda b, i: (b, 0, 0))

    def body(x_ref, w_ref, sc_ref, sh_ref, u_ref):
        u_ref[...] = _pre_f(x_ref[...], w_ref[...], sc_ref[0], sh_ref[0]).astype(BF16)

    (u,), extra = _plugged_call(body, plug, _grid_ends(grid), (x, w, sc, sh), name=name, grid=grid, in_specs=[row, vec, bvec, bvec],
                                out_specs=[row], out_shape=[jax.ShapeDtypeStruct((t, d), BF16)])
    return u if plug is None else (u, extra)


def _pre_bwd(x, w, sc, sh, du, dres, name, ts=512):
    t, d = x.shape
    bsz = sc.shape[0]
    nt, grid = _tok_grid(t, bsz, ts)
    row = _vspec((ts, d), lambda b, i: (b * nt + i, 0))
    vec = _vspec((1, d), lambda b, i: (0, 0))
    bvec = _vspec((1, 1, d), lambda b, i: (b, 0, 0))

    def body(x_ref, w_ref, sc_ref, sh_ref, du_ref, dres_ref, dx_ref, dw_ref, dsc_ref, dsh_ref):
        b, i = pl.program_id(0), pl.program_id(1)
        _, vjp = jax.vjp(_pre_f, x_ref[...], w_ref[...], sc_ref[0], sh_ref[0])
        dx, dw, dsc, dsh = vjp(du_ref[...])
        dx_ref[...] = dres_ref[...] + dx

        @pl.when((b == 0) & (i == 0))
        def _():
            dw_ref[...] = jnp.zeros_like(dw_ref)

        @pl.when(i == 0)
        def _():
            dsc_ref[...] = jnp.zeros_like(dsc_ref)
            dsh_ref[...] = jnp.zeros_like(dsh_ref)

        dw_ref[...] += dw
        dsc_ref[0] += dsc
        dsh_ref[0] += dsh

    return pl.pallas_call(
        body, name=name, grid=grid, in_specs=[row, vec, bvec, bvec, row, row], out_specs=[row, vec, bvec, bvec],
        out_shape=[jax.ShapeDtypeStruct((t, d), F32), jax.ShapeDtypeStruct((1, d), F32),
                   jax.ShapeDtypeStruct((bsz, 1, d), F32), jax.ShapeDtypeStruct((bsz, 1, d), F32)],
    )(x, w, sc, sh, du, dres)


def _accumulate(ref, val, first):
    @pl.when(first)
    def _():
        ref[...] = jnp.zeros_like(ref)

    ref[...] += val.reshape(ref.shape)


def _post_pre_fwd(res, y, w_post, g, w_pre, sc, sh, name, ts=512):
    t, d = y.shape
    bsz = g.shape[0]
    nt, grid = _tok_grid(t, bsz, ts)
    row = _vspec((ts, d), lambda b, i: (b * nt + i, 0))
    vec = _vspec((1, d), lambda b, i: (0, 0))
    bvec = _vspec((1, 1, d), lambda b, i: (b, 0, 0))

    def body(res_ref, y_ref, wp_ref, g_ref, w_ref, sc_ref, sh_ref, h_ref, u_ref):
        h = res_ref[...] + _post_f(y_ref[...], wp_ref[...], g_ref[0])
        h_ref[...] = h
        u_ref[...] = _pre_f(h, w_ref[...], sc_ref[0], sh_ref[0]).astype(BF16)

    return pl.pallas_call(body, name=name, grid=grid, in_specs=[row, row, vec, bvec, vec, bvec, bvec], out_specs=[row, row],
                          out_shape=[jax.ShapeDtypeStruct((t, d), F32), jax.ShapeDtypeStruct((t, d), BF16)],
                          )(res, y, w_post, g, w_pre, sc, sh)


def _post_loss_bwd(res, y, w, g, tgt, name, ts=512):
    t, d = y.shape
    bsz = g.shape[0]
    nt, grid = _tok_grid(t, bsz, ts)
    row = _vspec((ts, d), lambda b, i: (b * nt + i, 0))
    vec = _vspec((1, d), lambda b, i: (0, 0))
    bvec = _vspec((1, 1, d), lambda b, i: (b, 0, 0))

    def body(res_ref, y_ref, w_ref, g_ref, tgt_ref, dh_ref, dy_ref, dw_ref, dg_ref, sq_ref):
        b, i = pl.program_id(0), pl.program_id(1)
        part, vjp = jax.vjp(_post_f, y_ref[...], w_ref[...], g_ref[0])
        e = res_ref[...] + part - tgt_ref[...]
        dh = e * (1.0 / d)
        dh_ref[...] = dh
        dy, dw, dg = vjp(dh)
        dy_ref[...] = dy.astype(BF16)
        _accumulate(dw_ref, dw, (b == 0) & (i == 0))
        _accumulate(dg_ref, dg, i == 0)
        _accumulate(sq_ref, jnp.sum(e * e, axis=0, keepdims=True) * (1.0 / d), (b == 0) & (i == 0))

    return pl.pallas_call(
        body, name=name, grid=grid, in_specs=[row, row, vec, bvec, row], out_specs=[row, row, vec, bvec, vec],
        out_shape=[jax.ShapeDtypeStruct((t, d), F32), jax.ShapeDtypeStruct((t, d), BF16), jax.ShapeDtypeStruct((1, d), F32),
                   jax.ShapeDtypeStruct((bsz, 1, d), F32), jax.ShapeDtypeStruct((1, d), F32)],
    )(res, y, w, g, tgt)


def _pre_post_bwd(x, w, sc, sh, du, dres, y, w_post, g, name, ts=512):
    t, d = x.shape
    bsz = sc.shape[0]
    nt, grid = _tok_grid(t, bsz, ts)
    row = _vspec((ts, d), lambda b, i: (b * nt + i, 0))
    vec = _vspec((1, d), lambda b, i: (0, 0))
    bvec = _vspec((1, 1, d), lambda b, i: (b, 0, 0))

    def body(x_ref, w_ref, sc_ref, sh_ref, du_ref, dres_ref, y_ref, wp_ref, g_ref,
             dx_ref, dy_ref, dw_ref, dsc_ref, dsh_ref, dwp_ref, dg_ref):
        b, i = pl.program_id(0), pl.program_id(1)
        _, vjp = jax.vjp(_pre_f, x_ref[...], w_ref[...], sc_ref[0], sh_ref[0])
        dx, dw, dsc, dsh = vjp(du_ref[...])
        dx = dres_ref[...] + dx
        dx_ref[...] = dx
        _, vjp_post = jax.vjp(_post_f, y_ref[...], wp_ref[...], g_ref[0])
        dy, dwp, dg = vjp_post(dx)
        dy_ref[...] = dy.astype(BF16)
        first = (b == 0) & (i == 0)
        _accumulate(dw_ref, dw, first)
        _accumulate(dwp_ref, dwp, first)
        _accumulate(dsc_ref, dsc, i == 0)
        _accumulate(dsh_ref, dsh, i == 0)
        _accumulate(dg_ref, dg, i == 0)

    v1, vb = jax.ShapeDtypeStruct((1, d), F32), jax.ShapeDtypeStruct((bsz, 1, d), F32)
    return pl.pallas_call(
        body, name=name, grid=grid, in_specs=[row, vec, bvec, bvec, row, row, row, vec, bvec],
        out_specs=[row, row, vec, bvec, bvec, vec, bvec],
        out_shape=[jax.ShapeDtypeStruct((t, d), F32), jax.ShapeDtypeStruct((t, d), BF16), v1, vb, vb, v1, vb],
    )(x, w, sc, sh, du, dres, y, w_post, g)


def _merge_f(ga, gd, ya, yd):
    return jax.nn.sigmoid(ga) * ya + jax.nn.sigmoid(gd) * yd


_MW = 512


def _merge_fwd(proj, ya, yd, ts=512):
    t, d = ya.shape
    blk = _vspec((ts, _MW), lambda i, j: (i, j))
    ga = _vspec((ts, _MW), lambda i, j: (i, C_GA // _MW + j))
    gd = _vspec((ts, _MW), lambda i, j: (i, C_GD // _MW + j))

    def body(ga_ref, gd_ref, ya_ref, yd_ref, o_ref):
        o_ref[...] = _merge_f(ga_ref[...], gd_ref[...], ya_ref[...].astype(F32), yd_ref[...].astype(F32)).astype(BF16)

    return pl.pallas_call(body, name="merge_fwd", grid=(t // ts, d // _MW), in_specs=[ga, gd, blk, blk], out_specs=blk,
                          out_shape=jax.ShapeDtypeStruct((t, d), BF16))(proj, proj, ya, yd)


_ANY = pl.BlockSpec(memory_space=pl.ANY)


def _branch_bwd(dproj, proj, y, dm, col0, name, ts=512):
    t, d = y.shape
    blk = _vspec((ts, _MW), lambda i, j: (i, j))
    gate = _vspec((ts, _MW), lambda i, j: (i, col0 // _MW + j))

    def body(buf_ref, g_ref, y_ref, dm_ref, dg_ref, dy_ref):
        del buf_ref
        _, vjp = jax.vjp(lambda g, yy: jax.nn.sigmoid(g) * yy, g_ref[...], y_ref[...].astype(F32))
        dg, dy = vjp(dm_ref[...].astype(F32))
        dg_ref[...] = dg.astype(BF16)
        dy_ref[...] = dy.astype(BF16)

    return pl.pallas_call(body, name=name, grid=(t // ts, d // _MW), in_specs=[_ANY, gate, blk, blk], out_specs=[gate, blk],
                          out_shape=[jax.ShapeDtypeStruct(dproj.shape, BF16), jax.ShapeDtypeStruct((t, d), BF16)],
                          input_output_aliases={0: 0})(dproj, proj, y, dm)


def _shift_down(x, s):
    if s == 0:
        return x
    r = lax.broadcasted_iota(jnp.int32, x.shape, 0)
    return jnp.where(r >= s, pltpu.roll(x, s, 0), 0.0)


def _shift_up(x, s):
    if s == 0:
        return x
    n = x.shape[0]
    r = lax.broadcasted_iota(jnp.int32, x.shape, 0)
    return jnp.where(r < n - s, pltpu.roll(x, n - s, 0), 0.0)


def _conv_fwd(x, w, k):
    out = None
    for j in range(k):
        term = w[j:j + 1, :] * _shift_down(x, k - 1 - j)
        out = term if out is None else out + term
    return out


def _conv_bwd(x, w, dc, k):
    dx = None
    dws = []
    for j in range(k):
        up = _shift_up(dc, k - 1 - j)
        term = w[j:j + 1, :] * up
        dx = term if dx is None else dx + term
        dws.append(jnp.sum(up * x, axis=0, keepdims=True))
    return dx, jnp.concatenate(dws, axis=0)


def _geglu_f(gate, val):
    return jax.nn.gelu(gate, approximate=True) * val


_FW = 256


def _ffn_act_fwd(up, conv_w, bsz):
    t = up.shape[0]
    s = t // bsz
    nj = D_FF // _FW
    xg = _vspec((s, _FW), lambda b, j: (b, j))
    xv = _vspec((s, _FW), lambda b, j: (b, nj + j))
    wg = _vspec((FFN_CONV, _FW), lambda b, j: (0, j))
    wv = _vspec((FFN_CONV, _FW), lambda b, j: (0, nj + j))

    def body(xg_ref, xv_ref, wg_ref, wv_ref, o_ref):
        gate = _conv_fwd(xg_ref[...], wg_ref[...], FFN_CONV)
        val = _conv_fwd(xv_ref[...], wv_ref[...], FFN_CONV)
        o_ref[...] = _geglu_f(gate, val).astype(BF16)

    return pl.pallas_call(body, name="ffn_act_fwd", grid=(bsz, nj), in_specs=[xg, xv, wg, wv],
                          out_specs=_vspec((s, _FW), lambda b, j: (b, j)),
                          out_shape=jax.ShapeDtypeStruct((t, D_FF), BF16))(up, up, conv_w, conv_w)


def _ffn_act_bwd(up, conv_w, dact, bsz, plug=None):
    t = up.shape[0]
    s = t // bsz
    nj = D_FF // _FW
    xg = _vspec((s, _FW), lambda j, b: (b, j))
    xv = _vspec((s, _FW), lambda j, b: (b, nj + j))
    wg = _vspec((FFN_CONV, _FW), lambda j, b: (0, j))
    wv = _vspec((FFN_CONV, _FW), lambda j, b: (0, nj + j))
    da = _vspec((s, _FW), lambda j, b: (b, j))
    dwo = _vspec((FFN_CONV, _FW), lambda j, b: (0, j))

    def body(xg_ref, xv_ref, wg_ref, wv_ref, da_ref, dxg_ref, dxv_ref, dwg_ref, dwv_ref):
        b = pl.program_id(1)
        xg_, xv_, wg_, wv_ = xg_ref[...], xv_ref[...], wg_ref[...], wv_ref[...]
        gate = _conv_fwd(xg_, wg_, FFN_CONV)
        val = _conv_fwd(xv_, wv_, FFN_CONV)
        _, vjp = jax.vjp(_geglu_f, gate, val)
        dgate, dval = vjp(da_ref[...])
        dxg, dwg = _conv_bwd(xg_, wg_, dgate, FFN_CONV)
        dxv, dwv = _conv_bwd(xv_, wv_, dval, FFN_CONV)
        dxg_ref[...] = dxg.astype(BF16)
        dxv_ref[...] = dxv.astype(BF16)

        @pl.when(b == 0)
        def _():
            dwg_ref[...] = jnp.zeros_like(dwg_ref)
            dwv_ref[...] = jnp.zeros_like(dwv_ref)

        dwg_ref[...] += dwg
        dwv_ref[...] += dwv

    outs, extra = _plugged_call(
        body, plug, _grid_ends((nj, bsz)), (up, up, conv_w, conv_w, dact), name="ffn_act_bwd", grid=(nj, bsz),
        in_specs=[xg, xv, wg, wv, da], out_specs=[da, da, dwo, dwo],
        out_shape=[jax.ShapeDtypeStruct((t, D_FF), BF16), jax.ShapeDtypeStruct((t, D_FF), BF16),
                   jax.ShapeDtypeStruct((FFN_CONV, D_FF), F32), jax.ShapeDtypeStruct((FFN_CONV, D_FF), F32)])
    return (*outs, extra)


def _bucket_table():
    qi = np.arange(WINDOW)[:, None]
    kj = np.arange(2 * WINDOW)[None, :]
    dist = WINDOW + qi - kj
    dc = np.maximum(dist, 0)
    max_exact = REL_BUCKETS // 2
    scaled = np.log(np.maximum(dc, 1).astype(np.float32) / np.float32(max_exact)) / np.float32(math.log(REL_MAX_DIST / max_exact))
    large = max_exact + (scaled.astype(np.float32) * np.float32(REL_BUCKETS - max_exact)).astype(np.int32)
    large = np.minimum(large, REL_BUCKETS - 1)
    bucket = np.where(dc < max_exact, dc, large).astype(np.int32)
    in_band = ((dist >= 0) & (dist < WINDOW)).astype(np.int32)
    return bucket, in_band


def _bias_build(rel_bias):
    bucket, _ = _bucket_table()

    def body(rb_ref, idx_ref, o_ref):
        h = pl.program_id(0)
        idx = idx_ref[...]
        acc = jnp.zeros(idx.shape, F32)
        for r in range(REL_BUCKETS):
            acc = jnp.where(idx == r, rb_ref[r, h], acc)
        o_ref[0] = acc

    return pl.pallas_call(
        body, name="bias_build", grid=(AQ_HEADS,),
        in_specs=[pl.BlockSpec(memory_space=pltpu.SMEM), _vspec((WINDOW, 2 * WINDOW), lambda h: (0, 0))],
        out_specs=_vspec((1, WINDOW, 2 * WINDOW), lambda h: (h, 0, 0)),
        out_shape=jax.ShapeDtypeStruct((AQ_HEADS, WINDOW, 2 * WINDOW), F32),
    )(rel_bias, jnp.asarray(bucket))


def _bias_grad(dbias):
    bucket, _ = _bucket_table()

    def body(db_ref, idx_ref, o_ref):
        idx = idx_ref[...]
        db = db_ref[0]
        lane = lax.broadcasted_iota(jnp.int32, (1, LANES), 1)
        acc = jnp.zeros((1, LANES), F32)
        for r in range(REL_BUCKETS):
            s = jnp.sum(jnp.sum(jnp.where(idx == r, db, 0.0), axis=1, keepdims=True), axis=0, keepdims=True)
            acc = jnp.where(lane == r, s, acc)
        o_ref[0] = acc

    return pl.pallas_call(
        body, name="bias_grad", grid=(AQ_HEADS,),
        in_specs=[_vspec((1, WINDOW, 2 * WINDOW), lambda h: (h, 0, 0)), _vspec((WINDOW, 2 * WINDOW), lambda h: (0, 0))],
        out_specs=_vspec((1, 1, LANES), lambda h: (h, 0, 0)),
        out_shape=jax.ShapeDtypeStruct((AQ_HEADS, 1, LANES), F32),
    )(dbias, jnp.asarray(bucket))


def _attn_mask(n):
    qi = lax.broadcasted_iota(jnp.int32, (WINDOW, 2 * WINDOW), 0)
    kj = lax.broadcasted_iota(jnp.int32, (WINDOW, 2 * WINDOW), 1)
    dist = WINDOW + qi - kj
    band = (dist >= 0) & (dist < WINDOW)
    return band & ((kj >= WINDOW) | (n > 0))


def _attn_probs(qk, bias, sink, mask):
    s = jnp.where(mask, qk * (A_HD ** -0.5) + bias, NEG_INF)
    m = jnp.maximum(jnp.max(s, axis=-1, keepdims=True), sink)
    p = jnp.exp(s - m)
    es = jnp.exp(sink - m)
    inv = 1.0 / (jnp.sum(p, axis=-1, keepdims=True) + es)
    return p * inv, es * inv


def _attn_fwd(proj, bias, sinks, bsz):
    t = proj.shape[0]
    s = t // bsz
    nb = s // WINDOW
    grp = AQ_HEADS // AKV_HEADS

    def body(q_ref, k_ref, v_ref, bias_ref, sink_ref, y_ref, kp_ref, vp_ref):
        kp_ref[0:WINDOW, :] = jnp.zeros((WINDOW, LANES), BF16)
        vp_ref[0:WINDOW, :] = jnp.zeros((WINDOW, LANES), BF16)
        kp_ref[WINDOW:, :] = k_ref[...].astype(BF16)
        vp_ref[WINDOW:, :] = v_ref[...].astype(BF16)

        def blk(n, carry):
            r0 = pl.multiple_of(n * WINDOW, WINDOW)
            mask = _attn_mask(n)
            kband = kp_ref[pl.ds(r0, 2 * WINDOW), :]
            vband = vp_ref[pl.ds(r0, 2 * WINDOW), :]
            qb = q_ref[pl.ds(r0, WINDOW), :].astype(BF16)
            heads = range(AQ_HEADS)
            hsl = lambda h: slice(h * A_HD, (h + 1) * A_HD)
            kbs = [kband[:, hsl(kv)] for kv in range(AKV_HEADS)]
            vbs = [vband[:, hsl(kv)] for kv in range(AKV_HEADS)]
            qks = [lax.dot_general(qb[:, hsl(h)], kbs[h // grp], _NT, preferred_element_type=F32) for h in heads]
            probs = [_attn_probs(qks[h], bias_ref[h], sink_ref[0, h], mask)[0] for h in heads]
            outs = [jnp.dot(probs[h].astype(BF16), vbs[h // grp], preferred_element_type=F32) for h in heads]
            y_ref[pl.ds(r0, WINDOW), :] = jnp.concatenate(outs, axis=1).astype(BF16)
            return carry

        lax.fori_loop(0, nb, blk, 0)

    return pl.pallas_call(
        body, name="attn_fwd", grid=(bsz,),
        in_specs=[_vspec((s, AQ), lambda b: (b, C_Q // AQ)), _vspec((s, AKV), lambda b: (b, C_K // AKV)),
                  _vspec((s, AKV), lambda b: (b, C_V // AKV)),
                  _vspec((AQ_HEADS, WINDOW, 2 * WINDOW), lambda b: (0, 0, 0)), pl.BlockSpec(memory_space=pltpu.SMEM)],
        out_specs=_vspec((s, AQ), lambda b: (b, 0)), out_shape=jax.ShapeDtypeStruct((t, AQ), BF16),
        scratch_shapes=[pltpu.VMEM((s + WINDOW, LANES), BF16), pltpu.VMEM((s + WINDOW, LANES), BF16)],
    )(proj, proj, proj, bias, sinks)


def _attn_bwd(dproj, proj, bias, sinks, dy, bsz, plug=None):
    t = proj.shape[0]
    s = t // bsz
    nb = s // WINDOW
    grp = AQ_HEADS // AKV_HEADS
    scale = A_HD ** -0.5

    def body(q_ref, k_ref, v_ref, bias_ref, sink_ref, dy_ref, dq_ref, dk_ref, dv_ref, dbias_ref, dsink_ref,
             kp_ref, vp_ref, dkp_ref, dvp_ref):
        b = pl.program_id(0)
        kp_ref[0:WINDOW, :] = jnp.zeros((WINDOW, LANES), BF16)
        vp_ref[0:WINDOW, :] = jnp.zeros((WINDOW, LANES), BF16)
        kp_ref[WINDOW:, :] = k_ref[...].astype(BF16)
        vp_ref[WINDOW:, :] = v_ref[...].astype(BF16)
        dkp_ref[...] = jnp.zeros_like(dkp_ref)
        dvp_ref[...] = jnp.zeros_like(dvp_ref)

        @pl.when(b == 0)
        def _():
            dbias_ref[...] = jnp.zeros_like(dbias_ref)
            dsink_ref[...] = jnp.zeros_like(dsink_ref)

        def blk(n, carry):
            r0 = pl.multiple_of(n * WINDOW, WINDOW)
            mask = _attn_mask(n)
            kband = kp_ref[pl.ds(r0, 2 * WINDOW), :]
            vband = vp_ref[pl.ds(r0, 2 * WINDOW), :]
            qb = q_ref[pl.ds(r0, WINDOW), :].astype(BF16)
            dyb = dy_ref[pl.ds(r0, WINDOW), :].astype(BF16)
            heads = range(AQ_HEADS)
            hsl = lambda h: slice(h * A_HD, (h + 1) * A_HD)
            kbs = [kband[:, hsl(kv)] for kv in range(AKV_HEADS)]
            vbs = [vband[:, hsl(kv)] for kv in range(AKV_HEADS)]
            qhs = [qb[:, hsl(h)] for h in heads]
            dyhs = [dyb[:, hsl(h)] for h in heads]
            qks = [lax.dot_general(qhs[h], kbs[h // grp], _NT, preferred_element_type=F32) for h in heads]
            dprobs = [lax.dot_general(dyhs[h], vbs[h // grp], _NT, preferred_element_type=F32) for h in heads]
            pbs, dsbs = [], []
            for h in heads:
                probs, psink = _attn_probs(qks[h], bias_ref[h], sink_ref[0, h], mask)
                rowdot = jnp.sum(probs * dprobs[h], axis=-1, keepdims=True)
                ds = probs * (dprobs[h] - rowdot)
                dbias_ref[h] += ds
                dsink_ref[h] += jnp.sum(-psink * rowdot, axis=0, keepdims=True) + jnp.zeros((1, LANES), F32)
                pbs.append(probs.astype(BF16))
                dsbs.append(ds.astype(BF16))
            dvhs = [lax.dot_general(pbs[h], dyhs[h], _TN, preferred_element_type=F32) for h in heads]
            dqs = [jnp.dot(dsbs[h], kbs[h // grp], preferred_element_type=F32) * scale for h in heads]
            dkhs = [lax.dot_general(dsbs[h], qhs[h], _TN, preferred_element_type=F32) * scale for h in heads]
            dks = [sum(dkhs[kv * grp + 1:(kv + 1) * grp], dkhs[kv * grp]) for kv in range(AKV_HEADS)]
            dvs = [sum(dvhs[kv * grp + 1:(kv + 1) * grp], dvhs[kv * grp]) for kv in range(AKV_HEADS)]
            dq_ref[pl.ds(r0, WINDOW), :] = jnp.concatenate(dqs, axis=1).astype(BF16)
            dkp_ref[pl.ds(r0, 2 * WINDOW), :] += jnp.concatenate(dks, axis=1)
            dvp_ref[pl.ds(r0, 2 * WINDOW), :] += jnp.concatenate(dvs, axis=1)
            return carry

        lax.fori_loop(0, nb, blk, 0)
        dk_ref[...] = dkp_ref[WINDOW:, :].astype(BF16)
        dv_ref[...] = dvp_ref[WINDOW:, :].astype(BF16)

    kvs = jax.ShapeDtypeStruct((t, AKV), BF16)
    outs, extra = _plugged_call(
        lambda buf_ref, *refs: body(*refs), plug, _grid_ends((bsz,)), (dproj, proj, proj, proj, bias, sinks, dy),
        name="attn_bwd", grid=(bsz,),
        in_specs=[_ANY, _vspec((s, AQ), lambda b: (b, C_Q // AQ)), _vspec((s, AKV), lambda b: (b, C_K // AKV)),
                  _vspec((s, AKV), lambda b: (b, C_V // AKV)),
                  _vspec((AQ_HEADS, WINDOW, 2 * WINDOW), lambda b: (0, 0, 0)), pl.BlockSpec(memory_space=pltpu.SMEM),
                  _vspec((s, AQ), lambda b: (b, 0))],
        out_specs=[_vspec((s, AQ), lambda b: (b, C_Q // AQ)), _vspec((s, AKV), lambda b: (b, 0)), _vspec((s, AKV), lambda b: (b, 0)),
                   _vspec((AQ_HEADS, WINDOW, 2 * WINDOW), lambda b: (0, 0, 0)), _vspec((AQ_HEADS, 1, LANES), lambda b: (0, 0, 0))],
        out_shape=[jax.ShapeDtypeStruct(dproj.shape, BF16), kvs, kvs,
                   jax.ShapeDtypeStruct((AQ_HEADS, WINDOW, 2 * WINDOW), F32), jax.ShapeDtypeStruct((AQ_HEADS, 1, LANES), F32)],
        scratch_shapes=[pltpu.VMEM((s + WINDOW, LANES), BF16), pltpu.VMEM((s + WINDOW, LANES), BF16),
                        pltpu.VMEM((s + WINDOW, LANES), F32), pltpu.VMEM((s + WINDOW, LANES), F32)],
        aliases={0: 0})
    return (*outs, extra)


def _dn_act_f(c, is_qk):
    a = jax.nn.silu(c)
    outs = []
    for h in range(DN_HEADS):
        ah = a[:, h * DN_HD:(h + 1) * DN_HD]
        nh = ah * lax.rsqrt(jnp.sum(ah * ah, axis=-1, keepdims=True) + L2_EPS)
        outs.append(jnp.where(is_qk, nh, ah))
    return jnp.concatenate(outs, axis=1)


def _dn_prep_fwd(proj, conv_w, bsz):
    t = proj.shape[0]
    s = t // bsz
    blk = _vspec((s, DN), lambda b, j: (b, j))
    wsp = _vspec((DN_CONV, DN), lambda b, j: (0, j))

    def body(x_ref, w_ref, o_ref):
        j = pl.program_id(1)
        o_ref[...] = _dn_act_f(_conv_fwd(x_ref[...], w_ref[...], DN_CONV), j < 2)

    return pl.pallas_call(body, name="dn_prep_fwd", grid=(bsz, 3), in_specs=[blk, wsp], out_specs=blk,
                          out_shape=jax.ShapeDtypeStruct((t, 3 * DN), F32))(proj, conv_w)


def _dn_prep_bwd(dproj, proj, conv_w, dqkvn, bsz):
    t = proj.shape[0]
    s = t // bsz
    blk = _vspec((s, DN), lambda j, b: (b, j))
    wsp = _vspec((DN_CONV, DN), lambda j, b: (0, j))

    def body(buf_ref, x_ref, w_ref, d_ref, dx_ref, dw_ref):
        del buf_ref
        j, b = pl.program_id(0), pl.program_id(1)
        x, w = x_ref[...], w_ref[...]
        c = _conv_fwd(x, w, DN_CONV)
        _, vjp = jax.vjp(lambda cc: _dn_act_f(cc, j < 2), c)
        (dc,) = vjp(d_ref[0])
        dx, dw = _conv_bwd(x, w, dc, DN_CONV)
        dx_ref[...] = dx.astype(BF16)

        @pl.when(b == 0)
        def _():
            dw_ref[...] = jnp.zeros_like(dw_ref)

        dw_ref[...] += dw

    return pl.pallas_call(
        body, name="dn_prep_bwd", grid=(3, bsz),
        in_specs=[_ANY, blk, wsp, _vspec((1, s, DN), lambda j, b: (j, b, 0))], out_specs=[blk, wsp],
        out_shape=[jax.ShapeDtypeStruct(dproj.shape, BF16), jax.ShapeDtypeStruct((DN_CONV, 3 * DN), F32)],
        input_output_aliases={0: 0},
    )(dproj, proj, conv_w, dqkvn)


def _bg_f(x, alog, dt):
    lane = lax.broadcasted_iota(jnp.int32, x.shape, 1)
    beta = jax.nn.sigmoid(x)
    g = -jnp.exp(alog) * jax.nn.softplus(x + dt)
    return jnp.where(lane < DN_HEADS, beta, jnp.where(lane < 2 * DN_HEADS, g, 0.0))


def _bg_fwd(proj, alog, dt, bsz):
    t = proj.shape[0]
    s = t // bsz
    vec = _vspec((1, LANES), lambda b: (0, 0))

    def body(x_ref, a_ref, d_ref, o_ref):
        o_ref[...] = _bg_f(x_ref[...], a_ref[...], d_ref[...])

    return pl.pallas_call(body, name="bg_fwd", grid=(bsz,), in_specs=[_vspec((s, LANES), lambda b: (b, C_BD // LANES)), vec, vec],
                          out_specs=_vspec((s, LANES), lambda b: (b, 0)), out_shape=jax.ShapeDtypeStruct((t, LANES), F32))(proj, alog, dt)


def _bg_bwd(dproj, proj, alog, dt, dbg4, dk, dv, bsz):
    t = proj.shape[0]
    s = t // bsz
    vec = _vspec((1, LANES), lambda b: (0, 0))
    kv = _vspec((s, AKV), lambda b: (b, 0))
    tail = 3 * LANES

    def body(buf_ref, x_ref, a_ref, d_ref, g4_ref, dk_ref, dv_ref, dx_ref, da_ref, dd_ref):
        del buf_ref
        b = pl.program_id(0)
        lane = lax.broadcasted_iota(jnp.int32, (s, LANES), 1)
        dbg = jnp.zeros((s, LANES), F32)
        for h in range(DN_HEADS):
            gh = g4_ref[:, h * DN_HD:(h + 1) * DN_HD]
            dbg = jnp.where(lane == h, gh[:, 0:1], dbg)
            dbg = jnp.where(lane == DN_HEADS + h, gh[:, 1:2], dbg)
        _, vjp = jax.vjp(_bg_f, x_ref[...], a_ref[...], d_ref[...])
        dx, da, dd = vjp(dbg)
        dx_ref[...] = jnp.concatenate([dk_ref[...], dv_ref[...], dx.astype(BF16)], axis=1)

        @pl.when(b == 0)
        def _():
            da_ref[...] = jnp.zeros_like(da_ref)
            dd_ref[...] = jnp.zeros_like(dd_ref)

        da_ref[...] += da
        dd_ref[...] += dd

    return pl.pallas_call(
        body, name="bg_bwd", grid=(bsz,),
        in_specs=[_ANY, _vspec((s, LANES), lambda b: (b, C_BD // LANES)), vec, vec, _vspec((s, DN), lambda b: (b, 0)), kv, kv],
        out_specs=[_vspec((s, tail), lambda b: (b, C_K // tail)), vec, vec],
        out_shape=[jax.ShapeDtypeStruct(dproj.shape, BF16), jax.ShapeDtypeStruct((1, LANES), F32), jax.ShapeDtypeStruct((1, LANES), F32)],
        input_output_aliases={0: 0},
    )(dproj, proj, alog, dt, dbg4, dk, dv)


def _dn_out_f(o, z, w):
    outs = []
    for h in range(DN_HEADS):
        sl = slice(h * DN_HD, (h + 1) * DN_HD)
        outs.append(_rms(o[:, sl], w) * jax.nn.silu(z[:, sl]))
    return jnp.concatenate(outs, axis=1)


def _dn_out_fwd(o, proj, w, ts=512):
    t = o.shape[0]
    blk = _vspec((ts, DN), lambda i: (i, 0))
    zsp = _vspec((ts, DN), lambda i: (i, C_DZ // DN))
    vec = _vspec((1, DN_HD), lambda i: (0, 0))

    def body(o_ref, z_ref, w_ref, y_ref):
        y_ref[...] = _dn_out_f(o_ref[...], z_ref[...], w_ref[...]).astype(BF16)

    return pl.pallas_call(body, name="dn_out_fwd", grid=(t // ts,), in_specs=[blk, zsp, vec], out_specs=blk,
                          out_shape=jax.ShapeDtypeStruct((t, DN), BF16))(o, proj, w)


def _dn_out_bwd(dproj, o, proj, w, dy, ts=512):
    t = o.shape[0]
    blk = _vspec((ts, DN), lambda i: (i, 0))
    zsp = _vspec((ts, DN), lambda i: (i, C_DZ // DN))
    vec = _vspec((1, DN_HD), lambda i: (0, 0))

    def body(buf_ref, o_ref, z_ref, w_ref, dy_ref, do_ref, dz_ref, dw_ref):
        del buf_ref
        i = pl.program_id(0)
        _, vjp = jax.vjp(_dn_out_f, o_ref[...], z_ref[...], w_ref[...])
        do, dz, dw = vjp(dy_ref[...])
        do_ref[...] = do
        dz_ref[...] = dz.astype(BF16)

        @pl.when(i == 0)
        def _():
            dw_ref[...] = jnp.zeros_like(dw_ref)

        dw_ref[...] += dw

    return pl.pallas_call(
        body, name="dn_out_bwd", grid=(t // ts,), in_specs=[_ANY, blk, zsp, vec, blk], out_specs=[blk, zsp, vec],
        out_shape=[jax.ShapeDtypeStruct((t, DN), F32), jax.ShapeDtypeStruct(dproj.shape, BF16), jax.ShapeDtypeStruct((1, DN_HD), F32)],
        input_output_aliases={0: 1},
    )(dproj, o, proj, w, dy)


_C = DN_CHUNK


def _dot(a, b, dims):
    return lax.dot_general(a.astype(BF16), b.astype(BF16), dims, preferred_element_type=F32)


def _split(a):
    hi = a.astype(BF16)
    return hi, (a - hi.astype(F32)).astype(BF16)


def _dot3(a, b, dims):
    (ah, al), (bh, bl) = (a if isinstance(a, tuple) else _split(a)), (b if isinstance(b, tuple) else _split(b))
    mm = lambda x, y: lax.dot_general(x, y, dims, preferred_element_type=F32)
    return mm(ah, bh) + (mm(ah, bl) + mm(al, bh))


_NN = (((1,), (0,)), ((), ()))
_NT = (((1,), (1,)), ((), ()))
_TN = (((0,), (0,)), ((), ()))


_SUB = 8


def _tri_inverses(ls, lts):
    ri8 = lax.broadcasted_iota(jnp.int32, (_SUB, _C), 0)
    ci8 = lax.broadcasted_iota(jnp.int32, (_SUB, _C), 1)
    nblk = _C // _SUB
    ts = []
    for lt in lts:
        blocks = [jnp.where(ci8 == ri8 + _SUB * b, 1.0, 0.0).astype(F32) for b in range(nblk)]
        for r in range(1, _SUB):
            for b in range(nblk):
                coef = lt[_SUB * b:_SUB * (b + 1), _SUB * b + r:_SUB * b + r + 1]
                row = jnp.sum(coef * blocks[b], axis=0, keepdims=True)
                blocks[b] = jnp.where(ri8 == r, blocks[b] - row, blocks[b])
        ts.append(jnp.concatenate(blocks, axis=0))
    ri = lax.broadcasted_iota(jnp.int32, (_C, _C), 0)
    ci = lax.broadcasted_iota(jnp.int32, (_C, _C), 1)
    s = _SUB
    while s < _C:
        shift = s.bit_length()
        quad = ((ri >> shift) == (ci >> shift)) & ((ri & s) != 0) & ((ci & s) == 0)
        offs = [jnp.where(quad, l, 0.0) for l in ls]
        tsp = [_split(t) for t in ts]
        left = [_dot3(tp, off, _NN) for tp, off in zip(tsp, offs)]
        ts = [t - _dot3(lo, tp, _NN) for t, lo, tp in zip(ts, left, tsp)]
        s *= 2
    return ts


_SEG = 512
_HEADS = tuple(range(DN_HEADS))


def _hsl(hh):
    return slice(hh * DN_HD, (hh + 1) * DN_HD)


def _chunk_specs(bsz, nseg, reverse):
    seg = (lambda i: nseg - 1 - i) if reverse else (lambda i: i)
    ncs = _SEG // _C
    col = lambda off: _vspec((bsz, _SEG, DN), lambda i: (0, seg(i), off))
    return (col, _vspec((bsz, _SEG, LANES), lambda i: (0, seg(i), 0)),
            _vspec((bsz, DN_HEADS, ncs, _C), lambda i: (0, 0, seg(i), 0)),
            _vspec((bsz, DN_HEADS, ncs, DN_HD, DN_HD), lambda i: (0, 0, seg(i), 0, 0)),
            _vspec((bsz, DN_HEADS, ncs, _C, _C), lambda i: (0, 0, seg(i), 0, 0)))


def _chunk_pre(q_ref, k_ref, v_ref, bg_ref, gr_ref, c, bb, hh):
    r0 = pl.multiple_of(c * _C, _C)
    ri = lax.broadcasted_iota(jnp.int32, (_C, _C), 0)
    ci = lax.broadcasted_iota(jnp.int32, (_C, _C), 1)
    q = q_ref[bb, pl.ds(r0, _C), _hsl(hh)] * (DN_HD ** -0.5)
    k = k_ref[bb, pl.ds(r0, _C), _hsl(hh)]
    v = v_ref[bb, pl.ds(r0, _C), _hsl(hh)]
    bgc = bg_ref[bb, pl.ds(r0, _C), :]
    beta = bgc[:, hh:hh + 1]
    g_col = bgc[:, DN_HEADS + hh:DN_HEADS + hh + 1]
    g_row = gr_ref[bb, hh, pl.ds(c, 1), :]
    gc_col = jnp.sum(jnp.where(ri >= ci, g_row, 0.0), axis=1, keepdims=True)
    gc_row = jnp.sum(jnp.where(ri <= ci, g_col, 0.0), axis=0, keepdims=True)
    gc_last = jnp.sum(g_col, axis=0, keepdims=True)
    diff = gc_col - gc_row
    decay = jnp.where(ri >= ci, jnp.exp(jnp.where(ri >= ci, diff, 0.0)), 0.0)
    diff_t = gc_row - gc_col
    decay_t = jnp.where(ri <= ci, jnp.exp(jnp.where(ri <= ci, diff_t, 0.0)), 0.0)
    eg = jnp.exp(gc_col)
    et = jnp.exp(gc_last - gc_col)
    gl = jnp.exp(gc_last)
    kb = k * beta
    vb = v * beta
    return dict(r0=r0, bb=bb, hh=hh, q=q, k=k, v=v, beta=beta, decay=decay, decay_t=decay_t, eg=eg, et=et, gl=gl, kb=kb, vb=vb,
                ri=ri, ci=ci)


def _chunk_solve(ms, tms=None):
    for m in ms:
        m["kk_t"] = _dot(m["k"], m["kb"], _NT)
        m["qk"] = _dot(m["q"], m["k"], _NT)
        m["kk"] = _dot(m["kb"], m["k"], _NT)
        if tms is not None:
            m["qk_t"] = _dot(m["k"], m["q"], _NT)
    if tms is None:
        tms = _tri_inverses([jnp.where(m["ri"] > m["ci"], m["kk"] * m["decay"], 0.0) for m in ms],
                            [jnp.where(m["ri"] < m["ci"], m["kk_t"] * m["decay_t"], 0.0) for m in ms])
    for m, tm in zip(ms, tms):
        m["tm_f32"] = tm
    for m in ms:
        rhs = jnp.concatenate([m["vb"], m["kb"] * m["eg"]], axis=1)
        m["tm"] = _split(m["tm_f32"])
        m["sol"] = _dot3(m["tm"], rhs, _NN)
        m["intra"] = jnp.where(m["ri"] >= m["ci"], m["qk"] * m["decay"], 0.0)


def _dn_chunk_fwd(qkvn, bg, g_rows, bsz, plug=None):
    t = qkvn.shape[0]
    s = t // bsz
    nc, nseg = s // _C, s // _SEG
    pairs = [(bb, hh) for bb in range(bsz) for hh in _HEADS]

    def body(q_ref, k_ref, v_ref, bg_ref, gr_ref, o_ref, st_ref, tm_ref, s_ref):
        @pl.when(pl.program_id(0) == 0)
        def _():
            s_ref[...] = jnp.zeros_like(s_ref)

        def chunk(c, carry):
            ms = [_chunk_pre(q_ref, k_ref, v_ref, bg_ref, gr_ref, c, bb, hh) for bb, hh in pairs]
            _chunk_solve(ms)
            sts = [s_ref[i] for i in range(len(pairs))]
            for m, st in zip(ms, sts):
                st_ref[m["bb"], m["hh"], c] = st
                tm_ref[m["bb"], m["hh"], c] = m["tm_f32"]
            ws = [_dot(m["sol"][:, DN_HD:], st, _NN) for m, st in zip(ms, sts)]
            qs = [_dot(m["q"] * m["eg"], st, _NN) for m, st in zip(ms, sts)]
            v_new = [m["sol"][:, :DN_HD] - a for m, a in zip(ms, ws)]
            iv = [_dot(m["intra"], vn, _NN) for m, vn in zip(ms, v_new)]
            upd = [_dot(m["k"] * m["et"], vn, _TN) for m, vn in zip(ms, v_new)]
            for i, (bb, hh) in enumerate(pairs):
                s_ref[i] = sts[i] * ms[i]["gl"] + upd[i]
                o_ref[bb, pl.ds(ms[i]["r0"], _C), _hsl(hh)] = qs[i] + iv[i]
            return carry

        lax.fori_loop(0, _SEG // _C, chunk, 0)

    col, bgs, grs, sts_spec, tms_spec = _chunk_specs(bsz, nseg, False)
    q3, bg3 = qkvn.reshape(bsz, s, 3 * DN), bg.reshape(bsz, s, LANES)
    (o, states, tms), extra = _plugged_call(
        body, plug, _grid_ends((nseg,)), (q3, q3, q3, bg3, g_rows), name="dn_chunk_fwd", grid=(nseg,),
        in_specs=[col(0), col(1), col(2), bgs, grs], out_specs=[col(0), sts_spec, tms_spec],
        out_shape=[jax.ShapeDtypeStruct((bsz, s, DN), F32), jax.ShapeDtypeStruct((bsz, DN_HEADS, nc, DN_HD, DN_HD), F32),
                   jax.ShapeDtypeStruct((bsz, DN_HEADS, nc, _C, _C), F32)],
        scratch_shapes=[pltpu.VMEM((bsz * DN_HEADS, DN_HD, DN_HD), F32)])
    return o.reshape(t, DN), (states, tms), extra


def _dn_chunk_bwd(qkvn, bg, g_rows, states, do, bsz, plug=None):
    t = qkvn.shape[0]
    s = t // bsz
    nc, nseg = s // _C, s // _SEG
    pairs = [(bb, hh) for bb in range(bsz) for hh in _HEADS]

    def body(q_ref, k_ref, v_ref, bg_ref, gr_ref, st_ref, tm_ref, do_ref, dqkv_ref, dbg_ref, ds_ref):
        @pl.when(pl.program_id(0) == 0)
        def _():
            ds_ref[...] = jnp.zeros_like(ds_ref)

        def chunk(cc, carry):
            c = _SEG // _C - 1 - cc
            ms = [_chunk_pre(q_ref, k_ref, v_ref, bg_ref, gr_ref, c, bb, hh) for bb, hh in pairs]
            _chunk_solve(ms, [tm_ref[bb, hh, c] for bb, hh in pairs])
            ri, ci = ms[0]["ri"], ms[0]["ci"]
            for i, m in enumerate(ms):
                m["st"] = st_ref[m["bb"], m["hh"], c]
                m["ds_out"] = ds_ref[i]
                m["do"] = do_ref[m["bb"], pl.ds(m["r0"], _C), _hsl(m["hh"])]
                m["w"] = m["sol"][:, DN_HD:]
            for m in ms:
                m["v_new"] = m["sol"][:, :DN_HD] - _dot(m["w"], m["st"], _NN)
            for m in ms:
                m["q_dec"], m["k_tail"] = m["q"] * m["eg"], m["k"] * m["et"]
                m["dk_tail"] = _dot(m["v_new"], m["ds_out"], _NT)
                m["dv_new"] = _dot(m["k_tail"], m["ds_out"], _NN) + _dot(m["intra"], m["do"], _TN)
                m["dq_dec"] = _dot(m["do"], m["st"], _NT)
                m["ds_in"] = m["ds_out"] * m["gl"] + _dot(m["q_dec"], m["do"], _TN)
                m["dintra"] = jnp.where(ri >= ci, _dot(m["do"], m["v_new"], _NT), 0.0)
                m["dintra_t"] = jnp.where(ri <= ci, _dot(m["v_new"], m["do"], _NT), 0.0)
            for m in ms:
                m["dw"] = -_dot(m["dv_new"], m["st"], _NT)
                m["ds_in"] = m["ds_in"] - _dot(m["w"], m["dv_new"], _TN)
            for m in ms:
                dsol = jnp.concatenate([m["dv_new"], m["dw"]], axis=1)
                m["drhs"] = _dot3(m["tm"], dsol, _TN)
            for m in ms:
                m["dl"] = jnp.where(ri > ci, -_dot(m["drhs"], m["sol"], _NT), 0.0)
                m["dl_t"] = jnp.where(ri < ci, -_dot(m["sol"], m["drhs"], _NT), 0.0)
            for m in ms:
                m["dkb2"] = _dot(m["dl"] * m["decay"], m["k"], _NN)
                m["dk"] = _dot(m["dl_t"] * m["decay_t"], m["kb"], _NN) + _dot(m["dintra_t"] * m["decay_t"], m["q"], _NN)
                m["dq"] = _dot(m["dintra"] * m["decay"], m["k"], _NN)
            for m in ms:
                _chunk_bwd_finish(m)
            for m in ms:
                ones_ge = jnp.where(ri <= ci, 1.0, 0.0).astype(BF16)
                gh, gl_ = _split(m["dgc"] + jnp.zeros((_C, LANES), F32))
                m["dg_b"] = jnp.dot(ones_ge, gh, preferred_element_type=F32) + jnp.dot(ones_ge, gl_, preferred_element_type=F32)
            lane = lax.broadcasted_iota(jnp.int32, (_C, LANES), 1)
            for i, m in enumerate(ms):
                bb, hh, rows = m["bb"], m["hh"], pl.ds(m["r0"], _C)
                dqkv_ref[0, bb, rows, _hsl(hh)] = m["dq"] * (DN_HD ** -0.5)
                dqkv_ref[1, bb, rows, _hsl(hh)] = m["dk"]
                dqkv_ref[2, bb, rows, _hsl(hh)] = m["dv"]
                dbg_ref[bb, rows, _hsl(hh)] = jnp.where(lane == 0, m["dbeta"], jnp.where(lane == 1, m["dg_b"], 0.0))
                ds_ref[i] = m["ds_in"]
            return carry

        lax.fori_loop(0, _SEG // _C, chunk, 0)

    col, bgs, grs, sts_spec, tms_spec = _chunk_specs(bsz, nseg, True)
    q3, bg3, do3 = qkvn.reshape(bsz, s, 3 * DN), bg.reshape(bsz, s, LANES), do.reshape(bsz, s, DN)
    (dqkv, dbg), extra = _plugged_call(
        body, plug, _grid_ends((nseg,)), (q3, q3, q3, bg3, g_rows, *states, do3), name="dn_chunk_bwd", grid=(nseg,),
        in_specs=[col(0), col(1), col(2), bgs, grs, sts_spec, tms_spec, col(0)],
        out_specs=[_vspec((3, bsz, _SEG, DN), lambda i: (0, 0, nseg - 1 - i, 0)), col(0)],
        out_shape=[jax.ShapeDtypeStruct((3, bsz, s, DN), F32), jax.ShapeDtypeStruct((bsz, s, DN), F32)],
        scratch_shapes=[pltpu.VMEM((bsz * DN_HEADS, DN_HD, DN_HD), F32)])
    return dqkv.reshape(3, t, DN), dbg.reshape(t, DN), extra


def _chunk_bwd_finish(m):
    q, k, v, beta, decay, decay_t = m["q"], m["k"], m["v"], m["beta"], m["decay"], m["decay_t"]
    eg, et, gl, kb, dl, dl_t, dintra, dintra_t = m["eg"], m["et"], m["gl"], m["kb"], m["dl"], m["dl_t"], m["dintra"], m["dintra_t"]
    dq_dec, dk_tail, dq, dk = m["dq_dec"], m["dk_tail"], m["dq"], m["dk"]
    dgl = jnp.sum(jnp.sum(m["ds_out"] * m["st"], axis=1, keepdims=True), axis=0, keepdims=True)
    dvb, dkbeg = m["drhs"][:, :DN_HD], m["drhs"][:, DN_HD:]
    dkb = dkbeg * eg + m["dkb2"]
    deg = jnp.sum(dkbeg * kb, axis=1, keepdims=True)
    em = (dl * m["kk"] + dintra * m["qk"]) * decay
    em_t = (dl_t * m["kk_t"] + dintra_t * m["qk_t"]) * decay_t
    dgc = jnp.sum(em, axis=1, keepdims=True) - jnp.sum(em_t, axis=1, keepdims=True)
    dq = dq + dq_dec * eg
    deg = deg + jnp.sum(dq_dec * q, axis=1, keepdims=True)
    dk = dk + dk_tail * et
    det = jnp.sum(dk_tail * k, axis=1, keepdims=True)
    dgc = dgc + deg * eg - det * et
    dgc_last = jnp.sum(det * et, axis=0, keepdims=True) + dgl * gl
    rcol = lax.broadcasted_iota(jnp.int32, (_C, 1), 0)
    m["dgc"] = dgc + jnp.where(rcol == _C - 1, dgc_last, 0.0)
    m["dq"] = dq
    m["dk"] = dk + dkb * beta
    m["dbeta"] = jnp.sum(dkb * k, axis=1, keepdims=True) + jnp.sum(dvb * v, axis=1, keepdims=True)
    m["dv"] = dvb * beta


def _mod_fwd(c_all, ada_w_loc, ada_b_loc):
    n, cols = c_all.shape[0], ada_w_loc.shape[1]

    def body(c_ref, w_ref, b_ref, o_ref):
        o_ref[...] = _dot(jax.nn.silu(c_ref[...]), w_ref[...], _NN) + b_ref[...]

    return pl.pallas_call(body, name="mod_fwd", out_shape=jax.ShapeDtypeStruct((n, cols), F32))(c_all, ada_w_loc, ada_b_loc)


def _ada_grad(c_all, dmod_loc, dmod_all):
    d, cols = c_all.shape[1], dmod_loc.shape[1]

    def body(c_ref, dl_ref, da_ref, gw_ref, gb_ref):
        gw_ref[...] = _dot(jax.nn.silu(c_ref[...]), dl_ref[...], _TN)
        gb_ref[...] = jnp.sum(da_ref[...], axis=0, keepdims=True)

    return pl.pallas_call(body, name="ada_grad", out_shape=[jax.ShapeDtypeStruct((d, cols), F32),
                                                           jax.ShapeDtypeStruct((1, dmod_all.shape[1]), F32)])(c_all, dmod_loc, dmod_all)


ELEMENTWISE_BLOCK_BYTES = 3 * 2 ** 19


def _row_tile(r, c=1024):
    fits = [tr for tr in range(16, r + 1, 16) if tr * c * 4 <= ELEMENTWISE_BLOCK_BYTES]
    if not fits:
        return r
    whole = [tr for tr in fits if r % tr == 0]
    return whole[-1] if whole else fits[-1]


def _adamw(w, m, v, grads, name, plug=None):
    r, rest = w.shape[0], w.shape[1:]
    c = math.prod(rest)
    tr = _row_tile(r, c)
    blk = _vspec((tr,) + rest, lambda i: (i,) + (0,) * len(rest))
    n = len(grads)

    def body(*refs):
        w_ref, m_ref, v_ref = refs[:3]
        g_ref, d_ref, mo_ref, vo_ref = refs[3 + n:]
        g = refs[3][...]
        for p in refs[4:3 + n]:
            g = g + p[...]
        g_ref[...] = g
        d_ref[...], mo_ref[...], vo_ref[...] = _adamw_math(w_ref[...], m_ref[...], v_ref[...], g)

    o = jax.ShapeDtypeStruct(w.shape, F32)
    grid = (pl.cdiv(r, tr),)
    res, extra = _plugged_call(body, plug, _grid_ends(grid), (w, m, v, *grads), name=name, grid=grid, in_specs=[blk] * (3 + n),
                               out_specs=[blk] * 4, out_shape=[o] * 4)
    return res if plug is None else (res, extra)


def _adamw_halves(w, m, v, own, other, ic, name):
    if w.ndim == 3:
        r, _, c = w.shape
        tr = _row_tile(r, c)
        blk = _vspec((tr, 1, c // 2), lambda i, j: (i, 0, j))
        half = _vspec((tr, 1, c // 2), lambda i, j: (i, 0, 0))
        grid = (pl.cdiv(r, tr), 2)
    else:
        r, c = w.shape
        tr = _row_tile(r // 2, c)
        per = r // 2 // tr
        blk = _vspec((tr, c), lambda i, j: (j * per + i, 0))
        half = _vspec((tr, c), lambda i, j: (i, 0))
        grid = (per, 2)

    def body(ic_ref, w_ref, m_ref, v_ref, own_ref, other_ref, g_ref, d_ref, mo_ref, vo_ref):
        g = jnp.where(pl.program_id(1) == ic_ref[0], own_ref[...], other_ref[...])
        g_ref[...] = g
        d_ref[...], mo_ref[...], vo_ref[...] = _adamw_math(w_ref[...], m_ref[...], v_ref[...], g)

    o = jax.ShapeDtypeStruct(w.shape, F32)
    return pl.pallas_call(body, name=name, grid=grid,
                          in_specs=[pl.BlockSpec(memory_space=pltpu.SMEM), blk, blk, blk, half, half], out_specs=[blk] * 4,
                          out_shape=[o] * 4)(ic, w, m, v, own, other)


def _adamw_math(w, m, v, g):
    m_new = ADAM_B1 * m + (1.0 - ADAM_B1) * g
    v_new = ADAM_B2 * v + (1.0 - ADAM_B2) * jnp.square(g)
    m_hat = m_new / (1.0 - ADAM_B1 ** ADAM_STEP)
    v_hat = v_new / (1.0 - ADAM_B2 ** ADAM_STEP)
    return -ADAM_LR * (m_hat / (jnp.sqrt(v_hat) + ADAM_EPS) + ADAM_WD * w), m_new, v_new


def _adamw_small(ws, ms, vs, gs, name):
    n = len(ws)

    def body(*refs):
        for i in range(n):
            w_ref, m_ref, v_ref, g_ref = (refs[j * n + i] for j in range(4))
            go_ref, d_ref, mo_ref, vo_ref = refs[4 * n + 4 * i:4 * n + 4 * i + 4]
            g = g_ref[...]
            go_ref[...] = g
            d_ref[...], mo_ref[...], vo_ref[...] = _adamw_math(w_ref[...], m_ref[...], v_ref[...], g)

    res = pl.pallas_call(body, name=name, out_shape=[jax.ShapeDtypeStruct(a.shape, F32) for a in ws for _ in range(4)])(
        *ws, *ms, *vs, *gs)
    return [res[4 * i:4 * i + 4] for i in range(n)]


def _sum_lead(x, name, rows_apart=False):
    p, r, c = x.shape
    tr = _row_tile(r, c)
    mid = (1,) if rows_apart else ()

    def body(x_ref, o_ref):
        acc = x_ref[0].astype(F32)
        for i in range(1, p):
            acc = acc + x_ref[i].astype(F32)
        o_ref[...] = acc.reshape(o_ref.shape)

    return pl.pallas_call(body, name=name, grid=(pl.cdiv(r, tr),), in_specs=[_vspec((p, tr, c), lambda i: (0, i, 0))],
                          out_specs=_vspec((tr,) + mid + (c,), lambda i: (i,) + (0,) * (1 + len(mid))),
                          out_shape=jax.ShapeDtypeStruct((r,) + mid + (c,), F32))(x)


def _allgather8(x_shard, name, plug=None):
    m_per, n = x_shard.shape

    def body(x_ref, out_ref, send_sems, recv_sems, local_sem):
        x, y, c = lax.axis_index("x"), lax.axis_index("y"), lax.axis_index("c")
        me, sibling = (x, y, c), (x, y, 1 - c)
        chips = [(1 - x, y), (x, 1 - y), (1 - x, 1 - y)]

        def rows(px, py, pc):
            return out_ref.at[pl.ds((4 * px + 2 * py + pc) * m_per, m_per), :]

        def copy(k, block, to, src=None):
            return pltpu.make_async_remote_copy(
                src_ref=rows(*block) if src is None else src, dst_ref=rows(*block), send_sem=send_sems.at[k],
                recv_sem=recv_sems.at[k], device_id=to, device_id_type=MESH)

        mine = pltpu.make_async_copy(x_ref, rows(*me), local_sem)
        mine.start()
        first = [copy(0, me, sibling, src=x_ref)]
        first += [copy(1 + j, me, (*chip, c), src=x_ref) for j, chip in enumerate(chips)]
        for cp in first:
            cp.start()
        passed = [copy(4 + j, (*chip, c), sibling) for j, chip in enumerate(chips)]
        for j, chip in enumerate(chips):
            copy(1 + j, (*chip, c), me).wait_recv()
            passed[j].start()
        copy(0, sibling, me).wait_recv()
        for j, chip in enumerate(chips):
            copy(4 + j, (*chip, 1 - c), me).wait_recv()
        for cp in first + passed:
            cp.wait_send()
        mine.wait()

    grid = (1,)
    (out,), extra = _plugged_call(
        body, plug, _grid_ends(grid), (x_shard,), name=name, grid=grid, out_shape=[jax.ShapeDtypeStruct((8 * m_per, n), x_shard.dtype)],
        in_specs=[pl.BlockSpec(memory_space=pltpu.VMEM)], out_specs=[pl.BlockSpec(memory_space=pltpu.VMEM)],
        scratch_shapes=[pltpu.SemaphoreType.DMA((7,)), pltpu.SemaphoreType.DMA((7,)), pltpu.SemaphoreType.DMA])
    return out if plug is None else (out, extra)


_HBM = pl.BlockSpec(memory_space=pltpu.HBM)


def _mesh_place():
    x, y, c = lax.axis_index("x"), lax.axis_index("y"), lax.axis_index("c")
    return x, y, c, 2 * x + y, [(1 - x, y), (x, 1 - y), (1 - x, 1 - y)]


def _gather_plug(shards):
    n = len(shards)

    def half(ref, c, lead=None):
        r, cols = ref.shape[-2] // 2, ref.shape[-1] // 2
        if r % 16 == 0:
            rows = pl.ds(pl.multiple_of(c * r, 16), r)
            return ref.at[rows, :] if lead is None else ref.at[lead, rows, :]
        lanes = pl.ds(pl.multiple_of(c * cols, LANES), cols)
        return ref.at[:, lanes] if lead is None else ref.at[lead, :, lanes]

    def copies(ins, outs, send, recv):
        x, y, c, me, chips = _mesh_place()
        ici, fwd, fwd_in = [], [], []
        for i in range(n):
            for j, (px, py) in enumerate(chips):
                q = 2 * px + py
                ici.append((pltpu.make_async_remote_copy(
                    src_ref=half(ins[i], c), dst_ref=half(outs[i], c, me), send_sem=send.at[6 * i + j], recv_sem=recv.at[6 * i + j],
                    device_id=(px, py, c), device_id_type=MESH),
                    pltpu.make_async_remote_copy(
                    src_ref=half(ins[i], c), dst_ref=half(outs[i], c, q), send_sem=send.at[6 * i + j], recv_sem=recv.at[6 * i + j],
                    device_id=(px, py, c), device_id_type=MESH)))
                fwd.append(pltpu.make_async_remote_copy(
                    src_ref=half(outs[i], c, q), dst_ref=half(outs[i], c, q), send_sem=send.at[6 * i + 3 + j],
                    recv_sem=recv.at[6 * i + 3 + j], device_id=(x, y, 1 - c), device_id_type=MESH))
                fwd_in.append(pltpu.make_async_remote_copy(
                    src_ref=half(outs[i], 1 - c, q), dst_ref=half(outs[i], 1 - c, q), send_sem=send.at[6 * i + 3 + j],
                    recv_sem=recv.at[6 * i + 3 + j], device_id=(x, y, 1 - c), device_id_type=MESH))
        return ici, fwd, fwd_in, me

    def start(ins, outs, send, recv, loc):
        ici, _, _, me = copies(ins, outs, send, recv)
        for i in range(n):
            pltpu.make_async_copy(ins[i], outs[i].at[me], loc.at[i]).start()
        for out_cp, _ in ici:
            out_cp.start()

    def finish(ins, outs, send, recv, loc):
        ici, fwd, fwd_in, me = copies(ins, outs, send, recv)
        for (_, in_cp), f in zip(ici, fwd):
            in_cp.wait_recv()
            f.start()
        for f in fwd_in:
            f.wait_recv()
        for (out_cp, _), f in zip(ici, fwd):
            out_cp.wait_send()
            f.wait_send()
        for i in range(n):
            pltpu.make_async_copy(ins[i], outs[i].at[me], loc.at[i]).wait()

    return dict(ins=list(shards), out_shape=[jax.ShapeDtypeStruct((N_CHIPS,) + a.shape, a.dtype) for a in shards],
                scratch=[pltpu.SemaphoreType.DMA((6 * n,)), pltpu.SemaphoreType.DMA((6 * n,)), pltpu.SemaphoreType.DMA((n,))],
                start=start, finish=finish)


def _exchange_plug(pieces):
    n = len(pieces)

    def copies(ins, outs, send, recv):
        x, y, c, me, chips = _mesh_place()
        out_cps, in_cps = [], []
        for i in range(n):
            for j, (px, py) in enumerate(chips):
                q = 2 * px + py
                out_cps.append(pltpu.make_async_remote_copy(src_ref=ins[i].at[q], dst_ref=outs[i].at[me], send_sem=send.at[3 * i + j],
                                                            recv_sem=recv.at[3 * i + j], device_id=(px, py, c), device_id_type=MESH))
                in_cps.append(pltpu.make_async_remote_copy(src_ref=ins[i].at[me], dst_ref=outs[i].at[q], send_sem=send.at[3 * i + j],
                                                           recv_sem=recv.at[3 * i + j], device_id=(px, py, c), device_id_type=MESH))
        return out_cps, in_cps, me

    def start(ins, outs, send, recv, loc):
        out_cps, _, me = copies(ins, outs, send, recv)
        for i in range(n):
            pltpu.make_async_copy(ins[i].at[me], outs[i].at[me], loc.at[i]).start()
        for cp in out_cps:
            cp.start()

    def finish(ins, outs, send, recv, loc):
        out_cps, in_cps, me = copies(ins, outs, send, recv)
        for cp in in_cps:
            cp.wait_recv()
        for cp in out_cps:
            cp.wait_send()
        for i in range(n):
            pltpu.make_async_copy(ins[i].at[me], outs[i].at[me], loc.at[i]).wait()

    return dict(ins=list(pieces), out_shape=[jax.ShapeDtypeStruct(a.shape, a.dtype) for a in pieces],
                scratch=[pltpu.SemaphoreType.DMA((3 * n,)), pltpu.SemaphoreType.DMA((3 * n,)), pltpu.SemaphoreType.DMA((n,))],
                start=start, finish=finish)


def _plugged_call(body, plug, first_last, args, *, name, grid, in_specs, out_specs, out_shape, scratch_shapes=(), aliases=None):
    in_specs, out_specs, out_shape, scratch_shapes = list(in_specs), list(out_specs), list(out_shape), list(scratch_shapes)
    aliases = dict(aliases or {})
    if plug is None:
        return pl.pallas_call(body, name=name, grid=grid, in_specs=in_specs, out_specs=out_specs, out_shape=out_shape,
                              scratch_shapes=scratch_shapes, input_output_aliases=aliases)(*args), []
    n_in, n_out, n_sc = len(in_specs), len(out_specs), len(scratch_shapes)
    p_in, p_out = len(plug["ins"]), len(plug["out_shape"])

    def full(*refs):
        ins, refs = refs[:n_in], refs[n_in:]
        pins, refs = refs[:p_in], refs[p_in:]
        outs, refs = refs[:n_out], refs[n_out:]
        pouts, refs = refs[:p_out], refs[p_out:]
        scr, psems = refs[:n_sc], refs[n_sc:]
        first, last = first_last()

        @pl.when(first)
        def _():
            plug["start"](pins, pouts, *psems)

        body(*ins, *outs, *scr)

        @pl.when(last)
        def _():
            plug["finish"](pins, pouts, *psems)

    res = pl.pallas_call(full, name=name, grid=grid, in_specs=in_specs + [_HBM] * p_in, out_specs=out_specs + [_HBM] * p_out,
                         out_shape=out_shape + plug["out_shape"], scratch_shapes=scratch_shapes + plug["scratch"],
                         input_output_aliases=aliases)(*args, *plug["ins"])
    return res[:n_out], res[n_out:]


def _grid_ends(grid):
    def ends():
        first = last = None
        for ax, n in enumerate(grid):
            i = pl.program_id(ax)
            first = (i == 0) if first is None else first & (i == 0)
            last = (i == n - 1) if last is None else last & (i == n - 1)
        return first, last
    return ends


def _pair_presum(pieces, name):
    n, r, cols = pieces.shape
    by_cols = (cols // 2) % LANES == 0
    half_shape = (n, r, cols // 2) if by_cols else (n, r // 2, cols)

    def body(p_ref, o_ref, mine_ref, land_ref, send_sem, recv_sem, local_sem):
        x, y, c = lax.axis_index("x"), lax.axis_index("y"), lax.axis_index("c")

        def half(which):
            if by_cols:
                return p_ref.at[:, :, pl.ds(pl.multiple_of(which * (cols // 2), LANES), cols // 2)]
            return p_ref.at[:, pl.ds(pl.multiple_of(which * (r // 2), 16), r // 2), :]

        push = pltpu.make_async_remote_copy(src_ref=half(1 - c), dst_ref=land_ref, send_sem=send_sem, recv_sem=recv_sem,
                                            device_id=(x, y, 1 - c), device_id_type=MESH)
        own = pltpu.make_async_copy(half(c), mine_ref, local_sem)
        push.start()
        own.start()
        own.wait()
        push.wait_recv()
        for i in range(n):
            o_ref[i] = (mine_ref[i].astype(F32) + land_ref[i].astype(F32)).astype(BF16)
        push.wait_send()

    return pl.pallas_call(
        body, name=name, out_shape=jax.ShapeDtypeStruct(half_shape, BF16), in_specs=[_HBM],
        out_specs=pl.BlockSpec(memory_space=pltpu.VMEM),
        scratch_shapes=[pltpu.VMEM(half_shape, BF16), pltpu.VMEM(half_shape, BF16), pltpu.SemaphoreType.DMA,
                        pltpu.SemaphoreType.DMA, pltpu.SemaphoreType.DMA],
    )(pieces)


def _sibling_plug(arrs):
    n = len(arrs)

    def copies(ins, outs, send, recv):
        sibling = (lax.axis_index("x"), lax.axis_index("y"), 1 - lax.axis_index("c"))
        return [pltpu.make_async_remote_copy(src_ref=ins[i], dst_ref=outs[i], send_sem=send.at[i], recv_sem=recv.at[i],
                                             device_id=sibling, device_id_type=MESH) for i in range(n)]

    def start(ins, outs, send, recv):
        for cp in copies(ins, outs, send, recv):
            cp.start()

    def finish(ins, outs, send, recv):
        for cp in copies(ins, outs, send, recv):
            cp.wait()

    return dict(ins=list(arrs), out_shape=[jax.ShapeDtypeStruct(a.shape, a.dtype) for a in arrs],
                scratch=[pltpu.SemaphoreType.DMA((n,)), pltpu.SemaphoreType.DMA((n,))], start=start, finish=finish)


_SEGS = ((0, AQ, C_Q), (AQ, AKV, C_K), (AQ + AKV, AKV, C_V), (AQ + 2 * AKV, 3 * DN, C_DQKV), (2304, DN, C_DZ),
         (2816, 2 * DN_HEADS, C_BD), (2824, D_MODEL, C_GA), (3848, D_MODEL, C_GD))
SHARD_ROWS = IN_DIM // N_CHIPS


def _to_padded(w4):
    wt = w4.reshape(IN_DIM, w4.shape[2])
    parts = [wt[o:o + n] for o, n, _ in sorted(_SEGS, key=lambda sg: sg[2])]
    return jnp.concatenate(parts + [jnp.zeros((IN_PAD - IN_DIM, wt.shape[1]), wt.dtype)], axis=0)


def _from_padded(gt):
    return jnp.concatenate([gt[ps:ps + n] for _, n, ps in sorted(_SEGS)], axis=0).reshape(N_CHIPS, SHARD_ROWS, gt.shape[1])


def _lane_vec(a):
    return jnp.zeros((1, LANES), F32).at[0, DN_HEADS:2 * DN_HEADS].set(a)


_ROW_SHARDED = ("w_in", "w_out", "ffn_w_down")
_FFN = ("ffn_w_up", "ffn_w_down")
_LATE_MIXER = ("w_attn_branch", "w_dn_branch", "w_out")
_PRESUM = ("w_in", "ffn_w_up")


def _pieces(k, a):
    if a.ndim == 3:
        return a
    if k in _ROW_SHARDED:
        return a.reshape(N_CHIPS, a.shape[0] // N_CHIPS, a.shape[1]).astype(BF16)
    return jnp.transpose(a.reshape(a.shape[0], N_CHIPS, a.shape[1] // N_CHIPS), (1, 0, 2)).astype(BF16)


def _assemble(k, a):
    if k in _ROW_SHARDED:
        return a.reshape(-1, a.shape[2])
    return jnp.transpose(a, (1, 0, 2)).reshape(a.shape[1], -1)


def _device_step(x2, tgt2, mod, p, bsz, shards=None):
    d = D_MODEL
    on_mesh = shards is not None
    p = dict(p)
    sh1, sc1, g1, sh2, sc2, g2 = [mod[:, i * d:(i + 1) * d].reshape(bsz, 1, d) for i in range(N_MOD)]
    alog_v, dt_v = _lane_vec(p["dn_a_log"]), _lane_vec(p["dn_dt_bias"])
    sinks = p["attn_sinks"].reshape(1, AQ_HEADS)
    if on_mesh:
        u1, got = _pre_fwd(x2, p["norm_mix_pre"], sc1, sh1, "pre1_fwd", plug=_gather_plug(shards["w_in_last"]))
        p["w_in"] = _to_padded(jnp.concatenate(list(shards["w_in_gathered"]) + list(got), axis=1))
    else:
        u1 = _pre_fwd(x2, p["norm_mix_pre"], sc1, sh1, "pre1_fwd")
    if on_mesh:
        proj, got = _mm(u1, p["w_in"], "nt", F32, "mm_proj", _gather_plug(shards["late_mixer"]))
        p.update({k: _assemble(k, a) for k, a in zip(_LATE_MIXER, got)})
    else:
        proj = _mm(u1, p["w_in"], "nt", F32, "mm_proj")
    bias = _bias_build(p["rel_bias"])
    y_attn = _attn_fwd(proj, bias, sinks, bsz)
    qkvn = _dn_prep_fwd(proj, p["dn_conv_w"], bsz)
    bg = _bg_fwd(proj, alog_v, dt_v, bsz)
    nc = x2.shape[0] // bsz // DN_CHUNK
    g_rows = jnp.transpose(bg[:, DN_HEADS:2 * DN_HEADS].reshape(bsz, nc, DN_CHUNK, DN_HEADS), (0, 3, 1, 2))
    o, states, got = _dn_chunk_fwd(qkvn, bg, g_rows, bsz, _gather_plug(shards["ffn"][:1]) if on_mesh else None)
    for k, a in zip(_FFN[:1], got):
        p[k] = _assemble(k, a)
    y_dn = _dn_out_fwd(o, proj, p["dn_norm_w"])
    ya = _mm(y_attn, p["w_attn_branch"], "nn", BF16, "mm_ya")
    yd = _mm(y_dn, p["w_dn_branch"], "nn", BF16, "mm_yd")
    merged = _merge_fwd(proj, ya, yd)
    y1 = _mm(merged, p["w_out"], "nn", F32, "mm_y1")
    h1, u2 = _post_pre_fwd(x2, y1, p["norm_mix_post"], g1, p["norm_ffn_pre"], sc2, sh2, "post1_pre2_fwd")
    if on_mesh:
        up, got = _mm(u2, p["ffn_w_up"], "nn", F32, "mm_up", _gather_plug(shards["ffn"][1:]))
        p["ffn_w_down"] = _assemble("ffn_w_down", got[0])
    else:
        up = _mm(u2, p["ffn_w_up"], "nn", F32, "mm_up")
    act = _ffn_act_fwd(up, p["ffn_conv_w"], bsz)
    y2 = _mm(act, p["ffn_w_down"], "nn", F32, "mm_y2")
    dh2, dy2, g_ffn_post, dg2, sq = _post_loss_bwd(h1, y2, p["norm_ffn_post"], g2, tgt2, "post2_loss_bwd")
    g = {}
    g["norm_ffn_post"] = g_ffn_post
    dact = _mm(dy2, p["ffn_w_down"], "nt", F32, "mm_dact")
    g["ffn_w_down"] = _mm(act, dy2, "tn", BF16, "mm_dwdown")
    dupg, dupv, dcwg, dcwv, got_down = _ffn_act_bwd(
        up, p["ffn_conv_w"], dact, bsz, _exchange_plug([_pieces("ffn_w_down", g["ffn_w_down"])]) if on_mesh else None)
    g["ffn_conv_w"] = jnp.concatenate([dcwg, dcwv], axis=1)
    g["ffn_w_up"] = jnp.concatenate([_mm(u2, dupg, "tn", BF16, "mm_dwup_gate", split=N_CHIPS // 2),
                                     _mm(u2, dupv, "tn", BF16, "mm_dwup_val", split=N_CHIPS // 2)], axis=0)
    du2 = _mm(dupg, p["ffn_w_up"], "nt", F32, "mm_du2_gate", b_kblock=0)
    du2 = _mm(dupv, p["ffn_w_up"], "nt", F32, "mm_du2_val", b_kblock=1, add=du2)
    dh1, dy1, g["norm_ffn_pre"], dsc2, dsh2, g["norm_mix_post"], dg1 = _pre_post_bwd(
        h1, p["norm_ffn_pre"], sc2, sh2, du2, dh2, y1, p["norm_mix_post"], g1, "pre2_post1_bwd")
    dmerged = _mm(dy1, p["w_out"], "nt", BF16, "mm_dmerged")
    g["w_out"] = _mm(merged, dy1, "tn", BF16, "mm_dwout")
    dproj = lax.empty((x2.shape[0], IN_PAD), BF16)
    dproj, dya = _branch_bwd(dproj, proj, ya, dmerged, C_GA, "merge_bwd_attn")
    dproj, dyd = _branch_bwd(dproj, proj, yd, dmerged, C_GD, "merge_bwd_dn")
    dy_attn = _mm(dya, p["w_attn_branch"], "nt", BF16, "mm_dyattn")
    g["w_attn_branch"] = _mm(y_attn, dya, "tn", BF16, "mm_dwab", split=N_CHIPS)
    dy_dn = _mm(dyd, p["w_dn_branch"], "nt", F32, "mm_dydn")
    g["w_dn_branch"] = _mm(y_dn, dyd, "tn", BF16, "mm_dwdb", split=N_CHIPS)
    do, dproj, g["dn_norm_w"] = _dn_out_bwd(dproj, o, proj, p["dn_norm_w"], dy_dn)
    def plug_for(names):
        if not on_mesh:
            return None
        return _exchange_plug([_pair_presum(_pieces(k, g[k]), "presum_" + k) if k in _PRESUM else _pieces(k, g[k]) for k in names])

    early = ("w_out", "w_attn_branch", "w_dn_branch")
    dqkvn, dbg4, got_up = _dn_chunk_bwd(qkvn, bg, g_rows, states, do, bsz, plug_for(_FFN[:1]))
    got_ffn = list(got_up) + list(got_down)
    dproj, g["dn_conv_w"] = _dn_prep_bwd(dproj, proj, p["dn_conv_w"], dqkvn, bsz)
    dproj, dk, dv, dbias, g["attn_sinks"], got_early = _attn_bwd(dproj, proj, bias, sinks, dy_attn, bsz, plug_for(early))
    dproj, g["dn_a_log"], g["dn_dt_bias"] = _bg_bwd(dproj, proj, alog_v, dt_v, dbg4, dk, dv, bsz)
    g["rel_bias"] = _bias_grad(dbias)
    g["w_in"] = _from_padded(_mm(dproj, u1, "tn", BF16, "mm_dwin"))
    if on_mesh:
        du1, got_in = _mm(dproj, p["w_in"], "nn", F32, "mm_du1", plug_for(("w_in",)))
        g.update(zip(_FFN + early + ("w_in",), list(got_ffn) + list(got_early) + list(got_in)))
    else:
        du1 = _mm(dproj, p["w_in"], "nn", F32, "mm_du1")
    dx, g["norm_mix_pre"], dsc1, dsh1 = _pre_bwd(x2, p["norm_mix_pre"], sc1, sh1, du1, dh1, "pre1_bwd")
    dmod = jnp.concatenate([dsh1, dsc1, dg1, dsh2, dsc2, dg2], axis=-1).reshape(bsz, N_MOD * d)
    return sq, dx, dmod, g


_SMALL = (("norm_mix_pre", D_MODEL), ("norm_mix_post", D_MODEL), ("norm_ffn_pre", D_MODEL), ("norm_ffn_post", D_MODEL),
          ("dn_norm_w", DN_HD), ("dn_a_log", LANES), ("dn_dt_bias", LANES), ("attn_sinks", AQ_HEADS * LANES),
          ("rel_bias", AQ_HEADS * LANES), ("dn_conv_w", DN_CONV * 3 * DN), ("ffn_conv_w", FFN_CONV * 2 * D_FF),
          ("loss_sq", D_MODEL))


def _pack_rows(parts, rows):
    flat = jnp.concatenate([a.reshape(-1) for a in parts])
    return jnp.concatenate([flat, jnp.zeros((rows * LANES - flat.shape[0],), F32)]).reshape(rows, LANES)


_W_NAMES = ("ada_w", "ada_b", "norm_mix_pre", "norm_mix_post", "norm_ffn_pre", "norm_ffn_post", "w_in", "dn_conv_w", "dn_a_log",
            "dn_dt_bias", "dn_norm_w", "attn_sinks", "rel_bias", "w_attn_branch", "w_dn_branch", "w_out", "ffn_w_up", "ffn_conv_w",
            "ffn_w_down")
_BIG = ("w_in", "w_attn_branch", "w_dn_branch", "w_out", "ffn_w_up", "ffn_w_down")


def kernel(x, c, *rest):
    nw = len(_W_NAMES)
    w = dict(zip(_W_NAMES, rest[:nw]))
    loss_target = rest[nw]
    m = dict(zip(_W_NAMES, rest[nw + 1:2 * nw + 1]))
    v = dict(zip(_W_NAMES, rest[2 * nw + 1:3 * nw + 1]))
    ix, iy, ic = lax.axis_index("x"), lax.axis_index("y"), lax.axis_index("c")
    chip, dev = 2 * ix + iy, 4 * ix + 2 * iy + ic
    bsz, s, d = x.shape
    t = bsz * s
    n_dev = 8

    front_rows = 64
    front = _pack_rows([c, w["dn_conv_w"], w["ffn_conv_w"]], front_rows)
    w_in_t = jnp.swapaxes(w["w_in"][0], 0, 1).astype(BF16)
    cut1 = SHARD_ROWS // 48 * 16
    cut2 = 2 * cut1
    front_all, (w_in_lo,) = _allgather8(front, "ag_front", _gather_plug([w_in_t[:cut1]]))
    front_all = front_all.reshape(n_dev, front_rows * LANES)
    n_c, n_dc, n_fc = bsz * d, DN_CONV * 3 * DN // N_CHIPS, FFN_CONV * 2 * D_FF // N_CHIPS
    c_all = front_all[:, :n_c].reshape(n_dev * bsz, d)
    per_chip = front_all[0::2]
    dn_conv_full = jnp.transpose(per_chip[:, n_c:n_c + n_dc].reshape(N_CHIPS, DN_CONV, -1), (1, 0, 2)).reshape(DN_CONV, 3 * DN)
    ffn_conv_full = jnp.transpose(per_chip[:, n_c + n_dc:n_c + n_dc + n_fc].reshape(N_CHIPS, FFN_CONV, -1), (1, 0, 2)).reshape(FFN_CONV, 2 * D_FF)

    mod_cols = N_MOD * d // N_CHIPS
    ada_b_loc = lax.dynamic_slice(w["ada_b"], (0, chip * mod_cols), (1, mod_cols))
    mod_part = _mod_fwd(c_all, w["ada_w"][0], ada_b_loc)
    mod_all, (w_in_mid,) = _allgather8(mod_part, "ag_mod", _gather_plug([w_in_t[cut1:cut2]]))
    mod_all = mod_all.reshape(n_dev, n_dev * bsz, mod_cols)[0::2]
    mod = jnp.transpose(lax.dynamic_slice(mod_all, (0, dev * bsz, 0), (N_CHIPS, bsz, mod_cols)), (1, 0, 2)).reshape(bsz, N_MOD * d)

    p = {}
    shards = {"late_mixer": [w[k][0].astype(BF16) for k in _LATE_MIXER], "ffn": [w[k][0].astype(BF16) for k in _FFN],
              "w_in_gathered": [w_in_lo, w_in_mid], "w_in_last": [w_in_t[cut2:]]}
    for k in ("norm_mix_pre", "norm_mix_post", "norm_ffn_pre", "norm_ffn_post", "dn_norm_w", "attn_sinks"):
        p[k] = w[k]
    p["dn_a_log"], p["dn_dt_bias"], p["rel_bias"] = w["dn_a_log"][0], w["dn_dt_bias"][0], w["rel_bias"]
    p["dn_conv_w"], p["ffn_conv_w"] = dn_conv_full, ffn_conv_full

    sq, dx, dmod, g = _device_step(x.reshape(t, d), loss_target.reshape(t, d), mod, p, bsz, shards)
    g["dn_a_log"], g["dn_dt_bias"], g["loss_sq"] = g["dn_a_log"].reshape(-1), g["dn_dt_bias"].reshape(-1), sq
    small_rows = 336
    small = _pack_rows([dmod] + [g[k] for k, _ in _SMALL], small_rows)
    small_all = _allgather8(small, "ag_small").reshape(n_dev, small_rows, LANES)
    n_dm = bsz * N_MOD * d
    dmod_all = small_all.reshape(n_dev, -1)[:, :n_dm].reshape(n_dev * bsz, N_MOD * d)
    tot = _sum_lead(small_all, "sum_small").reshape(-1)
    gs, off = {}, n_dm
    for k, n in _SMALL:
        gs[k] = tot[off:off + n]
        off += n
    loss = 0.5 * jnp.sum(gs["loss_sq"])
    grad = {}
    grad["ada_w"], grad["ada_b"] = _ada_grad(c_all, lax.dynamic_slice(dmod_all, (0, chip * mod_cols), (n_dev * bsz, mod_cols)), dmod_all)
    for k in ("norm_mix_pre", "norm_mix_post", "norm_ffn_pre", "norm_ffn_post", "dn_norm_w"):
        grad[k] = gs[k]
    grad["dn_a_log"] = gs["dn_a_log"][DN_HEADS:2 * DN_HEADS]
    grad["dn_dt_bias"] = gs["dn_dt_bias"][DN_HEADS:2 * DN_HEADS]
    grad["attn_sinks"] = gs["attn_sinks"].reshape(AQ_HEADS, LANES)[:, 0]
    grad["rel_bias"] = gs["rel_bias"].reshape(AQ_HEADS, LANES)[:, :REL_BUCKETS].T
    grad["dn_conv_w"] = lax.dynamic_slice(gs["dn_conv_w"].reshape(DN_CONV, 3 * DN), (0, chip * (3 * DN // N_CHIPS)), (DN_CONV, 3 * DN // N_CHIPS))
    grad["ffn_conv_w"] = lax.dynamic_slice(gs["ffn_conv_w"].reshape(FFN_CONV, 2 * D_FF), (0, chip * (2 * D_FF // N_CHIPS)), (FFN_CONV, 2 * D_FF // N_CHIPS))

    mine = [_sum_lead(g[k], "sum_" + k, rows_apart=(k == "w_in")) for k in _BIG]
    out = {}
    out["ada_w"], theirs = _adamw(w["ada_w"][0], m["ada_w"][0], v["ada_w"][0], [grad["ada_w"]], "adamw_ada_w", _sibling_plug(mine))
    for k, a, b in zip(_BIG, mine, theirs):
        core = ic.reshape(1).astype(jnp.int32)
        if k == "w_in":
            tr = lambda z: jnp.transpose(z, (2, 0, 1))
            out[k] = [jnp.transpose(r, (1, 2, 0)) for r in _adamw_halves(tr(w[k]), tr(m[k]), tr(v[k]), a, b, core, "adamw_" + k)]
        elif k in _PRESUM:
            out[k] = _adamw_halves(w[k][0], m[k][0], v[k][0], a, b, core, "adamw_" + k)
        else:
            out[k] = _adamw(w[k][0], m[k][0], v[k][0], [a, b], "adamw_" + k)
    small_names = [k for k in _W_NAMES if k not in _BIG and k != "ada_w"]
    res_small = _adamw_small([w[k] for k in small_names], [m[k] for k in small_names], [v[k] for k in small_names],
                             [grad[k].reshape(w[k].shape) for k in small_names], "adamw_small")
    out.update(zip(small_names, res_small))
    for k in _BIG + ("ada_w",):
        out[k] = [r.reshape(w[k].shape) for r in out[k]]
    grads, deltas, new_m, new_v = ([out[k][i] for k in _W_NAMES] for i in range(4))
    return (loss, dx.reshape(bsz, s, d), *grads, *deltas, *new_m, *new_v)
```

```python
import math

import numpy as np
import jax
import jax.numpy as jnp
from jax import lax
from jax.experimental import pallas as pl
from jax.experimental.pallas import tpu as pltpu

F32 = jnp.float32
BF16 = jnp.bfloat16
MESH = pl.DeviceIdType.MESH

D_MODEL = 1024
N_MOD = 6
AQ_HEADS, AKV_HEADS, A_HD, WINDOW = 8, 2, 64, 128
REL_BUCKETS, REL_MAX_DIST = 32, 128
DN_HEADS, DN_HD, DN_CONV, DN_CHUNK = 4, 128, 4, 64
D_FF, FFN_CONV = 2816, 3
RMS_EPS, L2_EPS, NEG_INF = 1e-6, 1e-6, -1e30
AQ, AKV, DN = AQ_HEADS * A_HD, AKV_HEADS * A_HD, DN_HEADS * DN_HD
IN_DIM = AQ + 2 * AKV + 3 * DN + DN + 2 * DN_HEADS + 2 * D_MODEL
C_DQKV, C_Q, C_DZ, C_GA, C_GD, C_K, C_V, C_BD = 0, 1536, 2048, 2560, 3584, 4608, 4736, 4864
IN_PAD = 4992
LANES = 128
N_CHIPS = 4

ADAM_LR, ADAM_B1, ADAM_B2, ADAM_EPS, ADAM_WD, ADAM_STEP = 0.001, 0.9, 0.999, 1e-08, 0.01, 10


def _vspec(shape, index_map):
    return pl.BlockSpec(shape, index_map)


MM_VMEM_BUDGET = 40 * 2 ** 20
GRID_STEP_S = 0.35e-6
HBM_BYTES_PER_S = 3.0e12
MXU_FLOPS_PER_S = 9.0e14
MXU_DIM = 256


def _mm_tiles(m, n, k, mode, in_bytes, out_bytes, split=1):
    best = None
    for tm in [t for t in range(LANES, m + 1, LANES) if m % t == 0]:
        for tn in [t for t in range(LANES, n // split + 1, LANES) if (n // split) % t == 0]:
            a, b, o = k * tm * in_bytes, k * tn * in_bytes, tm * tn * out_bytes
            if 2 * (a + b + o) + (a if mode == "tn" else 0) > MM_VMEM_BUDGET:
                continue
            hbm_s = (m * k * in_bytes + (m // tm) * n * k * in_bytes + m * n * out_bytes) / HBM_BYTES_PER_S
            mxu_s = 2 * m * n * k / (MXU_FLOPS_PER_S * min(1.0, tm / MXU_DIM) * min(1.0, tn / MXU_DIM))
            cost = (m // tm) * (n // tn) * GRID_STEP_S + max(hbm_s, mxu_s)
            if best is None or cost < best[0]:
                best = (cost, tm, tn)
    return best[1], best[2]


def _mm(a, b, mode, out_dtype, name, plug=None, split=1, b_kblock=0, add=None):
    if mode == "nn":
        (m, k), n = a.shape, b.shape[1]
        dims = (((1,), (0,)), ((), ()))
    elif mode == "nt":
        (m, k), n = a.shape, b.shape[0]
        dims = (((1,), (1,)), ((), ()))
    else:
        (k, m), n = a.shape, b.shape[1]
        dims = (((0,), (0,)), ((), ()))
    tm, tn = _mm_tiles(m, n, k, mode, a.dtype.itemsize, jnp.dtype(out_dtype).itemsize, split)
    if mode == "tn":
        a_spec = _vspec((k, tm), lambda i, j: (0, i))
    else:
        a_spec = _vspec((tm, k), lambda i, j: (i, 0))
    if mode == "nt":
        b_spec = _vspec((tn, k), lambda i, j: (j, b_kblock))
    else:
        b_spec = _vspec((k, tn), lambda i, j: (0, j))
    in_specs, args = [a_spec, b_spec], (a, b)
    if add is not None:
        in_specs, args = in_specs + [_vspec((tm, tn), lambda i, j: (i, j))], (a, b, add)

    def body(a_ref, b_ref, *rest):
        o_ref = rest[-1]
        acc = lax.dot_general(a_ref[...].astype(BF16), b_ref[...].astype(BF16), dims, preferred_element_type=F32)
        if add is not None:
            acc = acc + rest[0][...]
        o_ref[...] = acc.astype(out_dtype).reshape(o_ref.shape)

    grid = (m // tm, n // tn)
    if split == 1:
        out_spec, out_shape = _vspec((tm, tn), lambda i, j: (i, j)), (m, n)
    else:
        per = n // split // tn
        out_spec, out_shape = _vspec((1, tm, tn), lambda i, j: (j // per, i, j % per)), (split, m, n // split)
    (out,), extra = _plugged_call(body, plug, _grid_ends(grid), args, name=name, grid=grid, in_specs=in_specs,
                                  out_specs=[out_spec], out_shape=[jax.ShapeDtypeStruct(out_shape, out_dtype)])
    return out if plug is None else (out, extra)


def _rms(x, w):
    return (x * lax.rsqrt(jnp.mean(x * x, axis=-1, keepdims=True) + RMS_EPS)) * w


def _pre_f(x, w, sc, sh):
    return _rms(x, w) * (1.0 + sc) + sh


def _post_f(y, w, g):
    return g * _rms(y, w)


def _tok_grid(t, bsz, ts):
    nt = t // bsz // ts
    return nt, (bsz, nt)


def _pre_fwd(x, w, sc, sh, name, ts=512, plug=None):
    t, d = x.shape
    bsz = sc.shape[0]
    nt, grid = _tok_grid(t, bsz, ts)
    row = _vspec((ts, d), lambda b, i: (b * nt + i, 0))
    vec = _vspec((1, d), lambda b, i: (0, 0))
    bvec = _vspec((1, 1, d), lambda b, i: (b, 0, 0))

    def body(x_ref, w_ref, sc_ref, sh_ref, u_ref):
        u_ref[...] = _pre_f(x_ref[...], w_ref[...], sc_ref[0], sh_ref[0]).astype(BF16)

    (u,), extra = _plugged_call(body, plug, _grid_ends(grid), (x, w, sc, sh), name=name, grid=grid, in_specs=[row, vec, bvec, bvec],
                                out_specs=[row], out_shape=[jax.ShapeDtypeStruct((t, d), BF16)])
    return u if plug is None else (u, extra)


def _pre_bwd(x, w, sc, sh, du, dres, name, ts=512):
    t, d = x.shape
    bsz = sc.shape[0]
    nt, grid = _tok_grid(t, bsz, ts)
    row = _vspec((ts, d), lambda b, i: (b * nt + i, 0))
    vec = _vspec((1, d), lambda b, i: (0, 0))
    bvec = _vspec((1, 1, d), lambda b, i: (b, 0, 0))

    def body(x_ref, w_ref, sc_ref, sh_ref, du_ref, dres_ref, dx_ref, dw_ref, dsc_ref, dsh_ref):
        b, i = pl.program_id(0), pl.program_id(1)
        _, vjp = jax.vjp(_pre_f, x_ref[...], w_ref[...], sc_ref[0], sh_ref[0])
        dx, dw, dsc, dsh = vjp(du_ref[...])
        dx_ref[...] = dres_ref[...] + dx

        @pl.when((b == 0) & (i == 0))
        def _():
            dw_ref[...] = jnp.zeros_like(dw_ref)

        @pl.when(i == 0)
        def _():
            dsc_ref[...] = jnp.zeros_like(dsc_ref)
            dsh_ref[...] = jnp.zeros_like(dsh_ref)

        dw_ref[...] += dw
        dsc_ref[0] += dsc
        dsh_ref[0] += dsh

    return pl.pallas_call(
        body, name=name, grid=grid, in_specs=[row, vec, bvec, bvec, row, row], out_specs=[row, vec, bvec, bvec],
        out_shape=[jax.ShapeDtypeStruct((t, d), F32), jax.ShapeDtypeStruct((1, d), F32),
                   jax.ShapeDtypeStruct((bsz, 1, d), F32), jax.ShapeDtypeStruct((bsz, 1, d), F32)],
    )(x, w, sc, sh, du, dres)


def _accumulate(ref, val, first):
    @pl.when(first)
    def _():
        ref[...] = jnp.zeros_like(ref)

    ref[...] += val.reshape(ref.shape)


def _post_pre_fwd(res, y, w_post, g, w_pre, sc, sh, name, ts=512):
    t, d = y.shape
    bsz = g.shape[0]
    nt, grid = _tok_grid(t, bsz, ts)
    row = _vspec((ts, d), lambda b, i: (b * nt + i, 0))
    vec = _vspec((1, d), lambda b, i: (0, 0))
    bvec = _vspec((1, 1, d), lambda b, i: (b, 0, 0))

    def body(res_ref, y_ref, wp_ref, g_ref, w_ref, sc_ref, sh_ref, h_ref, u_ref):
        h = res_ref[...] + _post_f(y_ref[...], wp_ref[...], g_ref[0])
        h_ref[...] = h
        u_ref[...] = _pre_f(h, w_ref[...], sc_ref[0], sh_ref[0]).astype(BF16)

    return pl.pallas_call(body, name=name, grid=grid, in_specs=[row, row, vec, bvec, vec, bvec, bvec], out_specs=[row, row],
                          out_shape=[jax.ShapeDtypeStruct((t, d), F32), jax.ShapeDtypeStruct((t, d), BF16)],
                          )(res, y, w_post, g, w_pre, sc, sh)


def _post_loss_bwd(res, y, w, g, tgt, name, ts=512):
    t, d = y.shape
    bsz = g.shape[0]
    nt, grid = _tok_grid(t, bsz, ts)
    row = _vspec((ts, d), lambda b, i: (b * nt + i, 0))
    vec = _vspec((1, d), lambda b, i: (0, 0))
    bvec = _vspec((1, 1, d), lambda b, i: (b, 0, 0))

    def body(res_ref, y_ref, w_ref, g_ref, tgt_ref, dh_ref, dy_ref, dw_ref, dg_ref, sq_ref):
        b, i = pl.program_id(0), pl.program_id(1)
        part, vjp = jax.vjp(_post_f, y_ref[...], w_ref[...], g_ref[0])
        e = res_ref[...] + part - tgt_ref[...]
        dh = e * (1.0 / d)
        dh_ref[...] = dh
        dy, dw, dg = vjp(dh)
        dy_ref[...] = dy.astype(BF16)
        _accumulate(dw_ref, dw, (b == 0) & (i == 0))
        _accumulate(dg_ref, dg, i == 0)
        _accumulate(sq_ref, jnp.sum(e * e, axis=0, keepdims=True) * (1.0 / d), (b == 0) & (i == 0))

    return pl.pallas_call(
        body, name=name, grid=grid, in_specs=[row, row, vec, bvec, row], out_specs=[row, row, vec, bvec, vec],
        out_shape=[jax.ShapeDtypeStruct((t, d), F32), jax.ShapeDtypeStruct((t, d), BF16), jax.ShapeDtypeStruct((1, d), F32),
                   jax.ShapeDtypeStruct((bsz, 1, d), F32), jax.ShapeDtypeStruct((1, d), F32)],
    )(res, y, w, g, tgt)


def _pre_post_bwd(x, w, sc, sh, du, dres, y, w_post, g, name, ts=512):
    t, d = x.shape
    bsz = sc.shape[0]
    nt, grid = _tok_grid(t, bsz, ts)
    row = _vspec((ts, d), lambda b, i: (b * nt + i, 0))
    vec = _vspec((1, d), lambda b, i: (0, 0))
    bvec = _vspec((1, 1, d), lambda b, i: (b, 0, 0))

    def body(x_ref, w_ref, sc_ref, sh_ref, du_ref, dres_ref, y_ref, wp_ref, g_ref,
             dx_ref, dy_ref, dw_ref, dsc_ref, dsh_ref, dwp_ref, dg_ref):
        b, i = pl.program_id(0), pl.program_id(1)
        _, vjp = jax.vjp(_pre_f, x_ref[...], w_ref[...], sc_ref[0], sh_ref[0])
        dx, dw, dsc, dsh = vjp(du_ref[...])
        dx = dres_ref[...] + dx
        dx_ref[...] = dx
        _, vjp_post = jax.vjp(_post_f, y_ref[...], wp_ref[...], g_ref[0])
        dy, dwp, dg = vjp_post(dx)
        dy_ref[...] = dy.astype(BF16)
        first = (b == 0) & (i == 0)
        _accumulate(dw_ref, dw, first)
        _accumulate(dwp_ref, dwp, first)
        _accumulate(dsc_ref, dsc, i == 0)
        _accumulate(dsh_ref, dsh, i == 0)
        _accumulate(dg_ref, dg, i == 0)

    v1, vb = jax.ShapeDtypeStruct((1, d), F32), jax.ShapeDtypeStruct((bsz, 1, d), F32)
    return pl.pallas_call(
        body, name=name, grid=grid, in_specs=[row, vec, bvec, bvec, row, row, row, vec, bvec],
        out_specs=[row, row, vec, bvec, bvec, vec, bvec],
        out_shape=[jax.ShapeDtypeStruct((t, d), F32), jax.ShapeDtypeStruct((t, d), BF16), v1, vb, vb, v1, vb],
    )(x, w, sc, sh, du, dres, y, w_post, g)


def _merge_f(ga, gd, ya, yd):
    return jax.nn.sigmoid(ga) * ya + jax.nn.sigmoid(gd) * yd


_MW = 512


def _merge_fwd(proj, ya, yd, ts=512):
    t, d = ya.shape
    blk = _vspec((ts, _MW), lambda i, j: (i, j))
    ga = _vspec((ts, _MW), lambda i, j: (i, C_GA // _MW + j))
    gd = _vspec((ts, _MW), lambda i, j: (i, C_GD // _MW + j))

    def body(ga_ref, gd_ref, ya_ref, yd_ref, o_ref):
        o_ref[...] = _merge_f(ga_ref[...], gd_ref[...], ya_ref[...].astype(F32), yd_ref[...].astype(F32)).astype(BF16)

    return pl.pallas_call(body, name="merge_fwd", grid=(t // ts, d // _MW), in_specs=[ga, gd, blk, blk], out_specs=blk,
                          out_shape=jax.ShapeDtypeStruct((t, d), BF16))(proj, proj, ya, yd)


_ANY = pl.BlockSpec(memory_space=pl.ANY)


def _branch_bwd(dproj, proj, y, dm, col0, name, ts=512):
    t, d = y.shape
    blk = _vspec((ts, _MW), lambda i, j: (i, j))
    gate = _vspec((ts, _MW), lambda i, j: (i, col0 // _MW + j))

    def body(buf_ref, g_ref, y_ref, dm_ref, dg_ref, dy_ref):
        del buf_ref
        _, vjp = jax.vjp(lambda g, yy: jax.nn.sigmoid(g) * yy, g_ref[...], y_ref[...].astype(F32))
        dg, dy = vjp(dm_ref[...].astype(F32))
        dg_ref[...] = dg.astype(BF16)
        dy_ref[...] = dy.astype(BF16)

    return pl.pallas_call(body, name=name, grid=(t // ts, d // _MW), in_specs=[_ANY, gate, blk, blk], out_specs=[gate, blk],
                          out_shape=[jax.ShapeDtypeStruct(dproj.shape, BF16), jax.ShapeDtypeStruct((t, d), BF16)],
                          input_output_aliases={0: 0})(dproj, proj, y, dm)


def _shift_down(x, s):
    if s == 0:
        return x
    r = lax.broadcasted_iota(jnp.int32, x.shape, 0)
    return jnp.where(r >= s, pltpu.roll(x, s, 0), 0.0)


def _shift_up(x, s):
    if s == 0:
        return x
    n = x.shape[0]
    r = lax.broadcasted_iota(jnp.int32, x.shape, 0)
    return jnp.where(r < n - s, pltpu.roll(x, n - s, 0), 0.0)


def _conv_fwd(x, w, k):
    out = None
    for j in range(k):
        term = w[j:j + 1, :] * _shift_down(x, k - 1 - j)
        out = term if out is None else out + term
    return out


def _conv_bwd(x, w, dc, k):
    dx = None
    dws = []
    for j in range(k):
        up = _shift_up(dc, k - 1 - j)
        term = w[j:j + 1, :] * up
        dx = term if dx is None else dx + term
        dws.append(jnp.sum(up * x, axis=0, keepdims=True))
    return dx, jnp.concatenate(dws, axis=0)


def _geglu_f(gate, val):
    return jax.nn.gelu(gate, approximate=True) * val


_FW = 256


def _ffn_act_fwd(up, conv_w, bsz):
    t = up.shape[0]
    s = t // bsz
    nj = D_FF // _FW
    xg = _vspec((s, _FW), lambda b, j: (b, j))
    xv = _vspec((s, _FW), lambda b, j: (b, nj + j))
    wg = _vspec((FFN_CONV, _FW), lambda b, j: (0, j))
    wv = _vspec((FFN_CONV, _FW), lambda b, j: (0, nj + j))

    def body(xg_ref, xv_ref, wg_ref, wv_ref, o_ref, cg_ref, cv_ref):
        gate = _conv_fwd(xg_ref[...], wg_ref[...], FFN_CONV)
        val = _conv_fwd(xv_ref[...], wv_ref[...], FFN_CONV)
        o_ref[...] = _geglu_f(gate, val).astype(BF16)
        cg_ref[...] = gate.astype(BF16)
        cv_ref[...] = val.astype(BF16)

    blk = _vspec((s, _FW), lambda b, j: (b, j))
    o = jax.ShapeDtypeStruct((t, D_FF), BF16)
    return pl.pallas_call(body, name="ffn_act_fwd", grid=(bsz, nj), in_specs=[xg, xv, wg, wv], out_specs=[blk] * 3,
                          out_shape=[o] * 3)(up, up, conv_w, conv_w)


def _ffn_act_bwd(up, conv_w, conv_out, dact, bsz, plug=None):
    t = up.shape[0]
    s = t // bsz
    nj = D_FF // _FW
    xg = _vspec((s, _FW), lambda j, b: (b, j))
    xv = _vspec((s, _FW), lambda j, b: (b, nj + j))
    wg = _vspec((FFN_CONV, _FW), lambda j, b: (0, j))
    wv = _vspec((FFN_CONV, _FW), lambda j, b: (0, nj + j))
    da = _vspec((s, _FW), lambda j, b: (b, j))
    dwo = _vspec((FFN_CONV, _FW), lambda j, b: (0, j))

    def body(xg_ref, xv_ref, wg_ref, wv_ref, cg_ref, cv_ref, da_ref, dxg_ref, dxv_ref, dwg_ref, dwv_ref):
        b = pl.program_id(1)
        xg_, xv_, wg_, wv_ = xg_ref[...], xv_ref[...], wg_ref[...], wv_ref[...]
        _, vjp = jax.vjp(_geglu_f, cg_ref[...].astype(F32), cv_ref[...].astype(F32))
        dgate, dval = vjp(da_ref[...])
        dxg, dwg = _conv_bwd(xg_, wg_, dgate, FFN_CONV)
        dxv, dwv = _conv_bwd(xv_, wv_, dval, FFN_CONV)
        dxg_ref[...] = dxg.astype(BF16)
        dxv_ref[...] = dxv.astype(BF16)

        @pl.when(b == 0)
        def _():
            dwg_ref[...] = jnp.zeros_like(dwg_ref)
            dwv_ref[...] = jnp.zeros_like(dwv_ref)

        dwg_ref[...] += dwg
        dwv_ref[...] += dwv

    outs, extra = _plugged_call(
        body, plug, _grid_ends((nj, bsz)), (up, up, conv_w, conv_w, *conv_out, dact), name="ffn_act_bwd", grid=(nj, bsz),
        in_specs=[xg, xv, wg, wv, da, da, da], out_specs=[da, da, dwo, dwo],
        out_shape=[jax.ShapeDtypeStruct((t, D_FF), BF16), jax.ShapeDtypeStruct((t, D_FF), BF16),
                   jax.ShapeDtypeStruct((FFN_CONV, D_FF), F32), jax.ShapeDtypeStruct((FFN_CONV, D_FF), F32)])
    return (*outs, extra)


def _bucket_table():
    qi = np.arange(WINDOW)[:, None]
    kj = np.arange(2 * WINDOW)[None, :]
    dist = WINDOW + qi - kj
    dc = np.maximum(dist, 0)
    max_exact = REL_BUCKETS // 2
    scaled = np.log(np.maximum(dc, 1).astype(np.float32) / np.float32(max_exact)) / np.float32(math.log(REL_MAX_DIST / max_exact))
    large = max_exact + (scaled.astype(np.float32) * np.float32(REL_BUCKETS - max_exact)).astype(np.int32)
    large = np.minimum(large, REL_BUCKETS - 1)
    bucket = np.where(dc < max_exact, dc, large).astype(np.int32)
    in_band = ((dist >= 0) & (dist < WINDOW)).astype(np.int32)
    return bucket, in_band


def _bias_build(rel_bias):
    bucket, _ = _bucket_table()

    def body(rb_ref, idx_ref, o_ref):
        h = pl.program_id(0)
        idx = idx_ref[...]
        acc = jnp.zeros(idx.shape, F32)
        for r in range(REL_BUCKETS):
            acc = jnp.where(idx == r, rb_ref[r, h], acc)
        o_ref[0] = acc

    return pl.pallas_call(
        body, name="bias_build", grid=(AQ_HEADS,),
        in_specs=[pl.BlockSpec(memory_space=pltpu.SMEM), _vspec((WINDOW, 2 * WINDOW), lambda h: (0, 0))],
        out_specs=_vspec((1, WINDOW, 2 * WINDOW), lambda h: (h, 0, 0)),
        out_shape=jax.ShapeDtypeStruct((AQ_HEADS, WINDOW, 2 * WINDOW), F32),
    )(rel_bias, jnp.asarray(bucket))


def _bias_grad(dbias):
    bucket, _ = _bucket_table()

    def body(db_ref, idx_ref, o_ref):
        idx = idx_ref[...]
        db = db_ref[0]
        lane = lax.broadcasted_iota(jnp.int32, (1, LANES), 1)
        acc = jnp.zeros((1, LANES), F32)
        for r in range(REL_BUCKETS):
            s = jnp.sum(jnp.sum(jnp.where(idx == r, db, 0.0), axis=1, keepdims=True), axis=0, keepdims=True)
            acc = jnp.where(lane == r, s, acc)
        o_ref[0] = acc

    return pl.pallas_call(
        body, name="bias_grad", grid=(AQ_HEADS,),
        in_specs=[_vspec((1, WINDOW, 2 * WINDOW), lambda h: (h, 0, 0)), _vspec((WINDOW, 2 * WINDOW), lambda h: (0, 0))],
        out_specs=_vspec((1, 1, LANES), lambda h: (h, 0, 0)),
        out_shape=jax.ShapeDtypeStruct((AQ_HEADS, 1, LANES), F32),
    )(dbias, jnp.asarray(bucket))


def _attn_mask(n):
    qi = lax.broadcasted_iota(jnp.int32, (WINDOW, 2 * WINDOW), 0)
    kj = lax.broadcasted_iota(jnp.int32, (WINDOW, 2 * WINDOW), 1)
    dist = WINDOW + qi - kj
    band = (dist >= 0) & (dist < WINDOW)
    return band & ((kj >= WINDOW) | (n > 0))


def _attn_probs(qk, bias, sink, mask):
    s = jnp.where(mask, qk * (A_HD ** -0.5) + bias, NEG_INF)
    m = jnp.maximum(jnp.max(s, axis=-1, keepdims=True), sink)
    p = jnp.exp(s - m)
    es = jnp.exp(sink - m)
    inv = 1.0 / (jnp.sum(p, axis=-1, keepdims=True) + es)
    return p * inv, es * inv


def _attn_fwd(proj, bias, sinks, bsz):
    t = proj.shape[0]
    s = t // bsz
    nb = s // WINDOW
    grp = AQ_HEADS // AKV_HEADS

    def body(q_ref, k_ref, v_ref, bias_ref, sink_ref, y_ref, kp_ref, vp_ref):
        kp_ref[0:WINDOW, :] = jnp.zeros((WINDOW, LANES), BF16)
        vp_ref[0:WINDOW, :] = jnp.zeros((WINDOW, LANES), BF16)
        kp_ref[WINDOW:, :] = k_ref[...].astype(BF16)
        vp_ref[WINDOW:, :] = v_ref[...].astype(BF16)

        def blk(n, carry):
            r0 = pl.multiple_of(n * WINDOW, WINDOW)
            mask = _attn_mask(n)
            kband = kp_ref[pl.ds(r0, 2 * WINDOW), :]
            vband = vp_ref[pl.ds(r0, 2 * WINDOW), :]
            qb = q_ref[pl.ds(r0, WINDOW), :].astype(BF16)
            heads = range(AQ_HEADS)
            hsl = lambda h: slice(h * A_HD, (h + 1) * A_HD)
            kbs = [kband[:, hsl(kv)] for kv in range(AKV_HEADS)]
            vbs = [vband[:, hsl(kv)] for kv in range(AKV_HEADS)]
            qks = [lax.dot_general(qb[:, hsl(h)], kbs[h // grp], _NT, preferred_element_type=F32) for h in heads]
            probs = [_attn_probs(qks[h], bias_ref[h], sink_ref[0, h], mask)[0] for h in heads]
            outs = [jnp.dot(probs[h].astype(BF16), vbs[h // grp], preferred_element_type=F32) for h in heads]
            y_ref[pl.ds(r0, WINDOW), :] = jnp.concatenate(outs, axis=1).astype(BF16)
            return carry

        lax.fori_loop(0, nb, blk, 0)

    return pl.pallas_call(
        body, name="attn_fwd", grid=(bsz,),
        in_specs=[_vspec((s, AQ), lambda b: (b, C_Q // AQ)), _vspec((s, AKV), lambda b: (b, C_K // AKV)),
                  _vspec((s, AKV), lambda b: (b, C_V // AKV)),
                  _vspec((AQ_HEADS, WINDOW, 2 * WINDOW), lambda b: (0, 0, 0)), pl.BlockSpec(memory_space=pltpu.SMEM)],
        out_specs=_vspec((s, AQ), lambda b: (b, 0)), out_shape=jax.ShapeDtypeStruct((t, AQ), BF16),
        scratch_shapes=[pltpu.VMEM((s + WINDOW, LANES), BF16), pltpu.VMEM((s + WINDOW, LANES), BF16)],
    )(proj, proj, proj, bias, sinks)


def _attn_bwd(dproj, proj, bias, sinks, dy, bsz, plug=None):
    t = proj.shape[0]
    s = t // bsz
    nb = s // WINDOW
    grp = AQ_HEADS // AKV_HEADS
    scale = A_HD ** -0.5

    def body(q_ref, k_ref, v_ref, bias_ref, sink_ref, dy_ref, dq_ref, dk_ref, dv_ref, dbias_ref, dsink_ref,
             kp_ref, vp_ref, dkp_ref, dvp_ref):
        b = pl.program_id(0)
        kp_ref[0:WINDOW, :] = jnp.zeros((WINDOW, LANES), BF16)
        vp_ref[0:WINDOW, :] = jnp.zeros((WINDOW, LANES), BF16)
        kp_ref[WINDOW:, :] = k_ref[...].astype(BF16)
        vp_ref[WINDOW:, :] = v_ref[...].astype(BF16)
        dkp_ref[...] = jnp.zeros_like(dkp_ref)
        dvp_ref[...] = jnp.zeros_like(dvp_ref)

        @pl.when(b == 0)
        def _():
            dbias_ref[...] = jnp.zeros_like(dbias_ref)
            dsink_ref[...] = jnp.zeros_like(dsink_ref)

        def blk(n, carry):
            r0 = pl.multiple_of(n * WINDOW, WINDOW)
            mask = _attn_mask(n)
            kband = kp_ref[pl.ds(r0, 2 * WINDOW), :]
            vband = vp_ref[pl.ds(r0, 2 * WINDOW), :]
            qb = q_ref[pl.ds(r0, WINDOW), :].astype(BF16)
            dyb = dy_ref[pl.ds(r0, WINDOW), :].astype(BF16)
            heads = range(AQ_HEADS)
            hsl = lambda h: slice(h * A_HD, (h + 1) * A_HD)
            kbs = [kband[:, hsl(kv)] for kv in range(AKV_HEADS)]
            vbs = [vband[:, hsl(kv)] for kv in range(AKV_HEADS)]
            qhs = [qb[:, hsl(h)] for h in heads]
            dyhs = [dyb[:, hsl(h)] for h in heads]
            qks = [lax.dot_general(qhs[h], kbs[h // grp], _NT, preferred_element_type=F32) for h in heads]
            dprobs = [lax.dot_general(dyhs[h], vbs[h // grp], _NT, preferred_element_type=F32) for h in heads]
            pbs, dsbs = [], []
            for h in heads:
                probs, psink = _attn_probs(qks[h], bias_ref[h], sink_ref[0, h], mask)
                rowdot = jnp.sum(probs * dprobs[h], axis=-1, keepdims=True)
                ds = probs * (dprobs[h] - rowdot)
                dbias_ref[h] += ds
                dsink_ref[h] += jnp.sum(-psink * rowdot, axis=0, keepdims=True) + jnp.zeros((1, LANES), F32)
                pbs.append(probs.astype(BF16))
                dsbs.append(ds.astype(BF16))
            dvhs = [lax.dot_general(pbs[h], dyhs[h], _TN, preferred_element_type=F32) for h in heads]
            dqs = [jnp.dot(dsbs[h], kbs[h // grp], preferred_element_type=F32) * scale for h in heads]
            dkhs = [lax.dot_general(dsbs[h], qhs[h], _TN, preferred_element_type=F32) * scale for h in heads]
            dks = [sum(dkhs[kv * grp + 1:(kv + 1) * grp], dkhs[kv * grp]) for kv in range(AKV_HEADS)]
            dvs = [sum(dvhs[kv * grp + 1:(kv + 1) * grp], dvhs[kv * grp]) for kv in range(AKV_HEADS)]
            dq_ref[pl.ds(r0, WINDOW), :] = jnp.concatenate(dqs, axis=1).astype(BF16)
            dkp_ref[pl.ds(r0, 2 * WINDOW), :] += jnp.concatenate(dks, axis=1)
            dvp_ref[pl.ds(r0, 2 * WINDOW), :] += jnp.concatenate(dvs, axis=1)
            return carry

        lax.fori_loop(0, nb, blk, 0)
        dk_ref[...] = dkp_ref[WINDOW:, :].astype(BF16)
        dv_ref[...] = dvp_ref[WINDOW:, :].astype(BF16)

    kvs = jax.ShapeDtypeStruct((t, AKV), BF16)
    outs, extra = _plugged_call(
        lambda buf_ref, *refs: body(*refs), plug, _grid_ends((bsz,)), (dproj, proj, proj, proj, bias, sinks, dy),
        name="attn_bwd", grid=(bsz,),
        in_specs=[_ANY, _vspec((s, AQ), lambda b: (b, C_Q // AQ)), _vspec((s, AKV), lambda b: (b, C_K // AKV)),
                  _vspec((s, AKV), lambda b: (b, C_V // AKV)),
                  _vspec((AQ_HEADS, WINDOW, 2 * WINDOW), lambda b: (0, 0, 0)), pl.BlockSpec(memory_space=pltpu.SMEM),
                  _vspec((s, AQ), lambda b: (b, 0))],
        out_specs=[_vspec((s, AQ), lambda b: (b, C_Q // AQ)), _vspec((s, AKV), lambda b: (b, 0)), _vspec((s, AKV), lambda b: (b, 0)),
                   _vspec((AQ_HEADS, WINDOW, 2 * WINDOW), lambda b: (0, 0, 0)), _vspec((AQ_HEADS, 1, LANES), lambda b: (0, 0, 0))],
        out_shape=[jax.ShapeDtypeStruct(dproj.shape, BF16), kvs, kvs,
                   jax.ShapeDtypeStruct((AQ_HEADS, WINDOW, 2 * WINDOW), F32), jax.ShapeDtypeStruct((AQ_HEADS, 1, LANES), F32)],
        scratch_shapes=[pltpu.VMEM((s + WINDOW, LANES), BF16), pltpu.VMEM((s + WINDOW, LANES), BF16),
                        pltpu.VMEM((s + WINDOW, LANES), F32), pltpu.VMEM((s + WINDOW, LANES), F32)],
        aliases={0: 0})
    return (*outs, extra)


def _dn_act_f(c, is_qk):
    a = jax.nn.silu(c)
    outs = []
    for h in range(DN_HEADS):
        ah = a[:, h * DN_HD:(h + 1) * DN_HD]
        nh = ah * lax.rsqrt(jnp.sum(ah * ah, axis=-1, keepdims=True) + L2_EPS)
        outs.append(jnp.where(is_qk, nh, ah))
    return jnp.concatenate(outs, axis=1)


def _dn_prep_fwd(proj, conv_w, bsz):
    t = proj.shape[0]
    s = t // bsz
    blk = _vspec((s, DN), lambda b, j: (b, j))
    wsp = _vspec((DN_CONV, DN), lambda b, j: (0, j))

    def body(x_ref, w_ref, o_ref):
        j = pl.program_id(1)
        o_ref[...] = _dn_act_f(_conv_fwd(x_ref[...], w_ref[...], DN_CONV), j < 2)

    return pl.pallas_call(body, name="dn_prep_fwd", grid=(bsz, 3), in_specs=[blk, wsp], out_specs=blk,
                          out_shape=jax.ShapeDtypeStruct((t, 3 * DN), F32))(proj, conv_w)


def _dn_prep_bwd(dproj, proj, conv_w, dqkvn, bsz):
    t = proj.shape[0]
    s = t // bsz
    blk = _vspec((s, DN), lambda j, b: (b, j))
    wsp = _vspec((DN_CONV, DN), lambda j, b: (0, j))

    def body(buf_ref, x_ref, w_ref, d_ref, dx_ref, dw_ref):
        del buf_ref
        j, b = pl.program_id(0), pl.program_id(1)
        x, w = x_ref[...], w_ref[...]
        c = _conv_fwd(x, w, DN_CONV)
        _, vjp = jax.vjp(lambda cc: _dn_act_f(cc, j < 2), c)
        (dc,) = vjp(d_ref[0])
        dx, dw = _conv_bwd(x, w, dc, DN_CONV)
        dx_ref[...] = dx.astype(BF16)

        @pl.when(b == 0)
        def _():
            dw_ref[...] = jnp.zeros_like(dw_ref)

        dw_ref[...] += dw

    return pl.pallas_call(
        body, name="dn_prep_bwd", grid=(3, bsz),
        in_specs=[_ANY, blk, wsp, _vspec((1, s, DN), lambda j, b: (j, b, 0))], out_specs=[blk, wsp],
        out_shape=[jax.ShapeDtypeStruct(dproj.shape, BF16), jax.ShapeDtypeStruct((DN_CONV, 3 * DN), F32)],
        input_output_aliases={0: 0},
    )(dproj, proj, conv_w, dqkvn)


def _bg_f(x, alog, dt):
    lane = lax.broadcasted_iota(jnp.int32, x.shape, 1)
    beta = jax.nn.sigmoid(x)
    g = -jnp.exp(alog) * jax.nn.softplus(x + dt)
    return jnp.where(lane < DN_HEADS, beta, jnp.where(lane < 2 * DN_HEADS, g, 0.0))


def _bg_fwd(proj, alog, dt, bsz):
    t = proj.shape[0]
    s = t // bsz
    vec = _vspec((1, LANES), lambda b: (0, 0))

    def body(x_ref, a_ref, d_ref, o_ref):
        o_ref[...] = _bg_f(x_ref[...], a_ref[...], d_ref[...])

    return pl.pallas_call(body, name="bg_fwd", grid=(bsz,), in_specs=[_vspec((s, LANES), lambda b: (b, C_BD // LANES)), vec, vec],
                          out_specs=_vspec((s, LANES), lambda b: (b, 0)), out_shape=jax.ShapeDtypeStruct((t, LANES), F32))(proj, alog, dt)


def _bg_bwd(dproj, proj, alog, dt, dbg4, dk, dv, bsz):
    t = proj.shape[0]
    s = t // bsz
    vec = _vspec((1, LANES), lambda b: (0, 0))
    kv = _vspec((s, AKV), lambda b: (b, 0))
    tail = 3 * LANES

    def body(buf_ref, x_ref, a_ref, d_ref, g4_ref, dk_ref, dv_ref, dx_ref, da_ref, dd_ref):
        del buf_ref
        b = pl.program_id(0)
        lane = lax.broadcasted_iota(jnp.int32, (s, LANES), 1)
        dbg = jnp.zeros((s, LANES), F32)
        for h in range(DN_HEADS):
            gh = g4_ref[:, h * DN_HD:(h + 1) * DN_HD]
            dbg = jnp.where(lane == h, gh[:, 0:1], dbg)
            dbg = jnp.where(lane == DN_HEADS + h, gh[:, 1:2], dbg)
        _, vjp = jax.vjp(_bg_f, x_ref[...], a_ref[...], d_ref[...])
        dx, da, dd = vjp(dbg)
        dx_ref[...] = jnp.concatenate([dk_ref[...], dv_ref[...], dx.astype(BF16)], axis=1)

        @pl.when(b == 0)
        def _():
            da_ref[...] = jnp.zeros_like(da_ref)
            dd_ref[...] = jnp.zeros_like(dd_ref)

        da_ref[...] += da
        dd_ref[...] += dd

    return pl.pallas_call(
        body, name="bg_bwd", grid=(bsz,),
        in_specs=[_ANY, _vspec((s, LANES), lambda b: (b, C_BD // LANES)), vec, vec, _vspec((s, DN), lambda b: (b, 0)), kv, kv],
        out_specs=[_vspec((s, tail), lambda b: (b, C_K // tail)), vec, vec],
        out_shape=[jax.ShapeDtypeStruct(dproj.shape, BF16), jax.ShapeDtypeStruct((1, LANES), F32), jax.ShapeDtypeStruct((1, LANES), F32)],
        input_output_aliases={0: 0},
    )(dproj, proj, alog, dt, dbg4, dk, dv)


def _dn_out_f(o, z, w):
    outs = []
    for h in range(DN_HEADS):
        sl = slice(h * DN_HD, (h + 1) * DN_HD)
        outs.append(_rms(o[:, sl], w) * jax.nn.silu(z[:, sl]))
    return jnp.concatenate(outs, axis=1)


def _dn_out_fwd(o, proj, w, ts=512):
    t = o.shape[0]
    blk = _vspec((ts, DN), lambda i: (i, 0))
    zsp = _vspec((ts, DN), lambda i: (i, C_DZ // DN))
    vec = _vspec((1, DN_HD), lambda i: (0, 0))

    def body(o_ref, z_ref, w_ref, y_ref):
        y_ref[...] = _dn_out_f(o_ref[...], z_ref[...], w_ref[...]).astype(BF16)

    return pl.pallas_call(body, name="dn_out_fwd", grid=(t // ts,), in_specs=[blk, zsp, vec], out_specs=blk,
                          out_shape=jax.ShapeDtypeStruct((t, DN), BF16))(o, proj, w)


def _dn_out_bwd(dproj, o, proj, w, dy, ts=512):
    t = o.shape[0]
    blk = _vspec((ts, DN), lambda i: (i, 0))
    zsp = _vspec((ts, DN), lambda i: (i, C_DZ // DN))
    vec = _vspec((1, DN_HD), lambda i: (0, 0))

    def body(buf_ref, o_ref, z_ref, w_ref, dy_ref, do_ref, dz_ref, dw_ref):
        del buf_ref
        i = pl.program_id(0)
        _, vjp = jax.vjp(_dn_out_f, o_ref[...], z_ref[...], w_ref[...])
        do, dz, dw = vjp(dy_ref[...])
        do_ref[...] = do
        dz_ref[...] = dz.astype(BF16)

        @pl.when(i == 0)
        def _():
            dw_ref[...] = jnp.zeros_like(dw_ref)

        dw_ref[...] += dw

    return pl.pallas_call(
        body, name="dn_out_bwd", grid=(t // ts,), in_specs=[_ANY, blk, zsp, vec, blk], out_specs=[blk, zsp, vec],
        out_shape=[jax.ShapeDtypeStruct((t, DN), F32), jax.ShapeDtypeStruct(dproj.shape, BF16), jax.ShapeDtypeStruct((1, DN_HD), F32)],
        input_output_aliases={0: 1},
    )(dproj, o, proj, w, dy)


_C = DN_CHUNK


def _dot(a, b, dims):
    return lax.dot_general(a.astype(BF16), b.astype(BF16), dims, preferred_element_type=F32)


def _split(a):
    hi = a.astype(BF16)
    return hi, (a - hi.astype(F32)).astype(BF16)


def _dot3(a, b, dims):
    (ah, al), (bh, bl) = (a if isinstance(a, tuple) else _split(a)), (b if isinstance(b, tuple) else _split(b))
    mm = lambda x, y: lax.dot_general(x, y, dims, preferred_element_type=F32)
    return mm(ah, bh) + (mm(ah, bl) + mm(al, bh))


_NN = (((1,), (0,)), ((), ()))
_NT = (((1,), (1,)), ((), ()))
_TN = (((0,), (0,)), ((), ()))


_SUB = 8


def _tri_inverses(ls, lts):
    ri8 = lax.broadcasted_iota(jnp.int32, (_SUB, _C), 0)
    ci8 = lax.broadcasted_iota(jnp.int32, (_SUB, _C), 1)
    nblk = _C // _SUB
    ts = []
    for lt in lts:
        blocks = [jnp.where(ci8 == ri8 + _SUB * b, 1.0, 0.0).astype(F32) for b in range(nblk)]
        for r in range(1, _SUB):
            for b in range(nblk):
                coef = lt[_SUB * b:_SUB * (b + 1), _SUB * b + r:_SUB * b + r + 1]
                row = jnp.sum(coef * blocks[b], axis=0, keepdims=True)
                blocks[b] = jnp.where(ri8 == r, blocks[b] - row, blocks[b])
        ts.append(jnp.concatenate(blocks, axis=0))
    ri = lax.broadcasted_iota(jnp.int32, (_C, _C), 0)
    ci = lax.broadcasted_iota(jnp.int32, (_C, _C), 1)
    s = _SUB
    while s < _C:
        shift = s.bit_length()
        quad = ((ri >> shift) == (ci >> shift)) & ((ri & s) != 0) & ((ci & s) == 0)
        offs = [jnp.where(quad, l, 0.0) for l in ls]
        tsp = [_split(t) for t in ts]
        left = [_dot3(tp, off, _NN) for tp, off in zip(tsp, offs)]
        ts = [t - _dot3(lo, tp, _NN) for t, lo, tp in zip(ts, left, tsp)]
        s *= 2
    return ts


_SEG = 512
_HEADS = tuple(range(DN_HEADS))


def _hsl(hh):
    return slice(hh * DN_HD, (hh + 1) * DN_HD)


def _chunk_specs(bsz, nseg, reverse):
    seg = (lambda i: nseg - 1 - i) if reverse else (lambda i: i)
    ncs = _SEG // _C
    col = lambda off: _vspec((bsz, _SEG, DN), lambda i: (0, seg(i), off))
    return (col, _vspec((bsz, _SEG, LANES), lambda i: (0, seg(i), 0)),
            _vspec((bsz, DN_HEADS, ncs, _C), lambda i: (0, 0, seg(i), 0)),
            _vspec((bsz, DN_HEADS, ncs, DN_HD, DN_HD), lambda i: (0, 0, seg(i), 0, 0)),
            _vspec((bsz, DN_HEADS, ncs, _C, _C), lambda i: (0, 0, seg(i), 0, 0)))


def _chunk_pre(q_ref, k_ref, v_ref, bg_ref, gr_ref, c, bb, hh):
    r0 = pl.multiple_of(c * _C, _C)
    ri = lax.broadcasted_iota(jnp.int32, (_C, _C), 0)
    ci = lax.broadcasted_iota(jnp.int32, (_C, _C), 1)
    q = q_ref[bb, pl.ds(r0, _C), _hsl(hh)] * (DN_HD ** -0.5)
    k = k_ref[bb, pl.ds(r0, _C), _hsl(hh)]
    v = v_ref[bb, pl.ds(r0, _C), _hsl(hh)]
    bgc = bg_ref[bb, pl.ds(r0, _C), :]
    beta = bgc[:, hh:hh + 1]
    g_col = bgc[:, DN_HEADS + hh:DN_HEADS + hh + 1]
    g_row = gr_ref[bb, hh, pl.ds(c, 1), :]
    gc_col = jnp.sum(jnp.where(ri >= ci, g_row, 0.0), axis=1, keepdims=True)
    gc_row = jnp.sum(jnp.where(ri <= ci, g_col, 0.0), axis=0, keepdims=True)
    gc_last = jnp.sum(g_col, axis=0, keepdims=True)
    diff = gc_col - gc_row
    decay = jnp.where(ri >= ci, jnp.exp(jnp.where(ri >= ci, diff, 0.0)), 0.0)
    diff_t = gc_row - gc_col
    decay_t = jnp.where(ri <= ci, jnp.exp(jnp.where(ri <= ci, diff_t, 0.0)), 0.0)
    eg = jnp.exp(gc_col)
    et = jnp.exp(gc_last - gc_col)
    gl = jnp.exp(gc_last)
    kb = k * beta
    vb = v * beta
    return dict(r0=r0, bb=bb, hh=hh, q=q, k=k, v=v, beta=beta, decay=decay, decay_t=decay_t, eg=eg, et=et, gl=gl, kb=kb, vb=vb,
                ri=ri, ci=ci)


def _chunk_solve(ms, tms=None):
    for m in ms:
        m["kk_t"] = _dot(m["k"], m["kb"], _NT)
        m["qk"] = _dot(m["q"], m["k"], _NT)
        m["kk"] = _dot(m["kb"], m["k"], _NT)
        if tms is not None:
            m["qk_t"] = _dot(m["k"], m["q"], _NT)
    if tms is None:
        tms = _tri_inverses([jnp.where(m["ri"] > m["ci"], m["kk"] * m["decay"], 0.0) for m in ms],
                            [jnp.where(m["ri"] < m["ci"], m["kk_t"] * m["decay_t"], 0.0) for m in ms])
    for m, tm in zip(ms, tms):
        m["tm_f32"] = tm
    for m in ms:
        rhs = jnp.concatenate([m["vb"], m["kb"] * m["eg"]], axis=1)
        m["tm"] = _split(m["tm_f32"])
        m["sol"] = _dot3(m["tm"], rhs, _NN)
        m["intra"] = jnp.where(m["ri"] >= m["ci"], m["qk"] * m["decay"], 0.0)


def _dn_chunk_fwd(qkvn, bg, g_rows, bsz, plug=None):
    t = qkvn.shape[0]
    s = t // bsz
    nc, nseg = s // _C, s // _SEG
    pairs = [(bb, hh) for bb in range(bsz) for hh in _HEADS]

    def body(q_ref, k_ref, v_ref, bg_ref, gr_ref, o_ref, st_ref, tm_ref, s_ref):
        @pl.when(pl.program_id(0) == 0)
        def _():
            s_ref[...] = jnp.zeros_like(s_ref)

        def chunk(c, carry):
            ms = [_chunk_pre(q_ref, k_ref, v_ref, bg_ref, gr_ref, c, bb, hh) for bb, hh in pairs]
            _chunk_solve(ms)
            sts = [s_ref[i] for i in range(len(pairs))]
            for m, st in zip(ms, sts):
                st_ref[m["bb"], m["hh"], c] = st
                tm_ref[m["bb"], m["hh"], c] = m["tm_f32"]
            ws = [_dot(m["sol"][:, DN_HD:], st, _NN) for m, st in zip(ms, sts)]
            qs = [_dot(m["q"] * m["eg"], st, _NN) for m, st in zip(ms, sts)]
            v_new = [m["sol"][:, :DN_HD] - a for m, a in zip(ms, ws)]
            iv = [_dot(m["intra"], vn, _NN) for m, vn in zip(ms, v_new)]
            upd = [_dot(m["k"] * m["et"], vn, _TN) for m, vn in zip(ms, v_new)]
            for i, (bb, hh) in enumerate(pairs):
                s_ref[i] = sts[i] * ms[i]["gl"] + upd[i]
                o_ref[bb, pl.ds(ms[i]["r0"], _C), _hsl(hh)] = qs[i] + iv[i]
            return carry

        lax.fori_loop(0, _SEG // _C, chunk, 0)

    col, bgs, grs, sts_spec, tms_spec = _chunk_specs(bsz, nseg, False)
    q3, bg3 = qkvn.reshape(bsz, s, 3 * DN), bg.reshape(bsz, s, LANES)
    (o, states, tms), extra = _plugged_call(
        body, plug, _grid_ends((nseg,)), (q3, q3, q3, bg3, g_rows), name="dn_chunk_fwd", grid=(nseg,),
        in_specs=[col(0), col(1), col(2), bgs, grs], out_specs=[col(0), sts_spec, tms_spec],
        out_shape=[jax.ShapeDtypeStruct((bsz, s, DN), F32), jax.ShapeDtypeStruct((bsz, DN_HEADS, nc, DN_HD, DN_HD), F32),
                   jax.ShapeDtypeStruct((bsz, DN_HEADS, nc, _C, _C), F32)],
        scratch_shapes=[pltpu.VMEM((bsz * DN_HEADS, DN_HD, DN_HD), F32)])
    return o.reshape(t, DN), (states, tms), extra


def _dn_chunk_bwd(qkvn, bg, g_rows, states, do, bsz, plug=None):
    t = qkvn.shape[0]
    s = t // bsz
    nc, nseg = s // _C, s // _SEG
    pairs = [(bb, hh) for bb in range(bsz) for hh in _HEADS]

    def body(q_ref, k_ref, v_ref, bg_ref, gr_ref, st_ref, tm_ref, do_ref, dqkv_ref, dbg_ref, ds_ref):
        @pl.when(pl.program_id(0) == 0)
        def _():
            ds_ref[...] = jnp.zeros_like(ds_ref)

        def chunk(cc, carry):
            c = _SEG // _C - 1 - cc
            ms = [_chunk_pre(q_ref, k_ref, v_ref, bg_ref, gr_ref, c, bb, hh) for bb, hh in pairs]
            _chunk_solve(ms, [tm_ref[bb, hh, c] for bb, hh in pairs])
            ri, ci = ms[0]["ri"], ms[0]["ci"]
            for i, m in enumerate(ms):
                m["st"] = st_ref[m["bb"], m["hh"], c]
                m["ds_out"] = ds_ref[i]
                m["do"] = do_ref[m["bb"], pl.ds(m["r0"], _C), _hsl(m["hh"])]
                m["w"] = m["sol"][:, DN_HD:]
            for m in ms:
                m["v_new"] = m["sol"][:, :DN_HD] - _dot(m["w"], m["st"], _NN)
            for m in ms:
                m["q_dec"], m["k_tail"] = m["q"] * m["eg"], m["k"] * m["et"]
                m["dk_tail"] = _dot(m["v_new"], m["ds_out"], _NT)
                m["dv_new"] = _dot(m["k_tail"], m["ds_out"], _NN) + _dot(m["intra"], m["do"], _TN)
                m["dq_dec"] = _dot(m["do"], m["st"], _NT)
                m["ds_in"] = m["ds_out"] * m["gl"] + _dot(m["q_dec"], m["do"], _TN)
                m["dintra"] = jnp.where(ri >= ci, _dot(m["do"], m["v_new"], _NT), 0.0)
                m["dintra_t"] = jnp.where(ri <= ci, _dot(m["v_new"], m["do"], _NT), 0.0)
            for m in ms:
                m["dw"] = -_dot(m["dv_new"], m["st"], _NT)
                m["ds_in"] = m["ds_in"] - _dot(m["w"], m["dv_new"], _TN)
            for m in ms:
                dsol = jnp.concatenate([m["dv_new"], m["dw"]], axis=1)
                m["drhs"] = _dot3(m["tm"], dsol, _TN)
            for m in ms:
                m["dl"] = jnp.where(ri > ci, -_dot(m["drhs"], m["sol"], _NT), 0.0)
                m["dl_t"] = jnp.where(ri < ci, -_dot(m["sol"], m["drhs"], _NT), 0.0)
            for m in ms:
                m["dkb2"] = _dot(m["dl"] * m["decay"], m["k"], _NN)
                m["dk"] = _dot(m["dl_t"] * m["decay_t"], m["kb"], _NN) + _dot(m["dintra_t"] * m["decay_t"], m["q"], _NN)
                m["dq"] = _dot(m["dintra"] * m["decay"], m["k"], _NN)
            for m in ms:
                _chunk_bwd_finish(m)
            for m in ms:
                ones_ge = jnp.where(ri <= ci, 1.0, 0.0).astype(BF16)
                gh, gl_ = _split(m["dgc"] + jnp.zeros((_C, LANES), F32))
                m["dg_b"] = jnp.dot(ones_ge, gh, preferred_element_type=F32) + jnp.dot(ones_ge, gl_, preferred_element_type=F32)
            lane = lax.broadcasted_iota(jnp.int32, (_C, LANES), 1)
            for i, m in enumerate(ms):
                bb, hh, rows = m["bb"], m["hh"], pl.ds(m["r0"], _C)
                dqkv_ref[0, bb, rows, _hsl(hh)] = m["dq"] * (DN_HD ** -0.5)
                dqkv_ref[1, bb, rows, _hsl(hh)] = m["dk"]
                dqkv_ref[2, bb, rows, _hsl(hh)] = m["dv"]
                dbg_ref[bb, rows, _hsl(hh)] = jnp.where(lane == 0, m["dbeta"], jnp.where(lane == 1, m["dg_b"], 0.0))
                ds_ref[i] = m["ds_in"]
            return carry

        lax.fori_loop(0, _SEG // _C, chunk, 0)

    col, bgs, grs, sts_spec, tms_spec = _chunk_specs(bsz, nseg, True)
    q3, bg3, do3 = qkvn.reshape(bsz, s, 3 * DN), bg.reshape(bsz, s, LANES), do.reshape(bsz, s, DN)
    (dqkv, dbg), extra = _plugged_call(
        body, plug, _grid_ends((nseg,)), (q3, q3, q3, bg3, g_rows, *states, do3), name="dn_chunk_bwd", grid=(nseg,),
        in_specs=[col(0), col(1), col(2), bgs, grs, sts_spec, tms_spec, col(0)],
        out_specs=[_vspec((3, bsz, _SEG, DN), lambda i: (0, 0, nseg - 1 - i, 0)), col(0)],
        out_shape=[jax.ShapeDtypeStruct((3, bsz, s, DN), F32), jax.ShapeDtypeStruct((bsz, s, DN), F32)],
        scratch_shapes=[pltpu.VMEM((bsz * DN_HEADS, DN_HD, DN_HD), F32)])
    return dqkv.reshape(3, t, DN), dbg.reshape(t, DN), extra


def _chunk_bwd_finish(m):
    q, k, v, beta, decay, decay_t = m["q"], m["k"], m["v"], m["beta"], m["decay"], m["decay_t"]
    eg, et, gl, kb, dl, dl_t, dintra, dintra_t = m["eg"], m["et"], m["gl"], m["kb"], m["dl"], m["dl_t"], m["dintra"], m["dintra_t"]
    dq_dec, dk_tail, dq, dk = m["dq_dec"], m["dk_tail"], m["dq"], m["dk"]
    dgl = jnp.sum(jnp.sum(m["ds_out"] * m["st"], axis=1, keepdims=True), axis=0, keepdims=True)
    dvb, dkbeg = m["drhs"][:, :DN_HD], m["drhs"][:, DN_HD:]
    dkb = dkbeg * eg + m["dkb2"]
    deg = jnp.sum(dkbeg * kb, axis=1, keepdims=True)
    em = (dl * m["kk"] + dintra * m["qk"]) * decay
    em_t = (dl_t * m["kk_t"] + dintra_t * m["qk_t"]) * decay_t
    dgc = jnp.sum(em, axis=1, keepdims=True) - jnp.sum(em_t, axis=1, keepdims=True)
    dq = dq + dq_dec * eg
    deg = deg + jnp.sum(dq_dec * q, axis=1, keepdims=True)
    dk = dk + dk_tail * et
    det = jnp.sum(dk_tail * k, axis=1, keepdims=True)
    dgc = dgc + deg * eg - det * et
    dgc_last = jnp.sum(det * et, axis=0, keepdims=True) + dgl * gl
    rcol = lax.broadcasted_iota(jnp.int32, (_C, 1), 0)
    m["dgc"] = dgc + jnp.where(rcol == _C - 1, dgc_last, 0.0)
    m["dq"] = dq
    m["dk"] = dk + dkb * beta
    m["dbeta"] = jnp.sum(dkb * k, axis=1, keepdims=True) + jnp.sum(dvb * v, axis=1, keepdims=True)
    m["dv"] = dvb * beta


def _mod_fwd(c_all, ada_w_loc, ada_b_loc):
    n, cols = c_all.shape[0], ada_w_loc.shape[1]

    def body(c_ref, w_ref, b_ref, o_ref):
        o_ref[...] = _dot(jax.nn.silu(c_ref[...]), w_ref[...], _NN) + b_ref[...]

    return pl.pallas_call(body, name="mod_fwd", out_shape=jax.ShapeDtypeStruct((n, cols), F32))(c_all, ada_w_loc, ada_b_loc)


def _ada_grad(c_all, dmod_loc, dmod_all):
    d, cols = c_all.shape[1], dmod_loc.shape[1]

    def body(c_ref, dl_ref, da_ref, gw_ref, gb_ref):
        gw_ref[...] = _dot(jax.nn.silu(c_ref[...]), dl_ref[...], _TN)
        gb_ref[...] = jnp.sum(da_ref[...], axis=0, keepdims=True)

    return pl.pallas_call(body, name="ada_grad", out_shape=[jax.ShapeDtypeStruct((d, cols), F32),
                                                           jax.ShapeDtypeStruct((1, dmod_all.shape[1]), F32)])(c_all, dmod_loc, dmod_all)


ELEMENTWISE_BLOCK_BYTES = 3 * 2 ** 19


def _row_tile(r, c=1024):
    fits = [tr for tr in range(16, r + 1, 16) if tr * c * 4 <= ELEMENTWISE_BLOCK_BYTES]
    if not fits:
        return r
    whole = [tr for tr in fits if r % tr == 0]
    return whole[-1] if whole else fits[-1]


def _adamw(w, m, v, grads, name, plug=None):
    r, rest = w.shape[0], w.shape[1:]
    c = math.prod(rest)
    tr = _row_tile(r, c)
    blk = _vspec((tr,) + rest, lambda i: (i,) + (0,) * len(rest))
    n = len(grads)

    def body(*refs):
        w_ref, m_ref, v_ref = refs[:3]
        g_ref, d_ref, mo_ref, vo_ref = refs[3 + n:]
        g = refs[3][...]
        for p in refs[4:3 + n]:
            g = g + p[...]
        g_ref[...] = g
        d_ref[...], mo_ref[...], vo_ref[...] = _adamw_math(w_ref[...], m_ref[...], v_ref[...], g)

    o = jax.ShapeDtypeStruct(w.shape, F32)
    grid = (pl.cdiv(r, tr),)
    res, extra = _plugged_call(body, plug, _grid_ends(grid), (w, m, v, *grads), name=name, grid=grid, in_specs=[blk] * (3 + n),
                               out_specs=[blk] * 4, out_shape=[o] * 4)
    return res if plug is None else (res, extra)


def _adamw_halves(w, m, v, own, other, ic, name):
    if w.ndim == 3:
        r, _, c = w.shape
        tr = _row_tile(r, c)
        blk = _vspec((tr, 1, c // 2), lambda i, j: (i, 0, j))
        half = _vspec((tr, 1, c // 2), lambda i, j: (i, 0, 0))
        grid = (pl.cdiv(r, tr), 2)
    else:
        r, c = w.shape
        tr = _row_tile(r // 2, c)
        per = r // 2 // tr
        blk = _vspec((tr, c), lambda i, j: (j * per + i, 0))
        half = _vspec((tr, c), lambda i, j: (i, 0))
        grid = (per, 2)

    def body(ic_ref, w_ref, m_ref, v_ref, own_ref, other_ref, g_ref, d_ref, mo_ref, vo_ref):
        g = jnp.where(pl.program_id(1) == ic_ref[0], own_ref[...], other_ref[...])
        g_ref[...] = g
        d_ref[...], mo_ref[...], vo_ref[...] = _adamw_math(w_ref[...], m_ref[...], v_ref[...], g)

    o = jax.ShapeDtypeStruct(w.shape, F32)
    return pl.pallas_call(body, name=name, grid=grid,
                          in_specs=[pl.BlockSpec(memory_space=pltpu.SMEM), blk, blk, blk, half, half], out_specs=[blk] * 4,
                          out_shape=[o] * 4)(ic, w, m, v, own, other)


def _adamw_math(w, m, v, g):
    m_new = ADAM_B1 * m + (1.0 - ADAM_B1) * g
    v_new = ADAM_B2 * v + (1.0 - ADAM_B2) * jnp.square(g)
    m_hat = m_new / (1.0 - ADAM_B1 ** ADAM_STEP)
    v_hat = v_new / (1.0 - ADAM_B2 ** ADAM_STEP)
    return -ADAM_LR * (m_hat / (jnp.sqrt(v_hat) + ADAM_EPS) + ADAM_WD * w), m_new, v_new


def _adamw_small(ws, ms, vs, gs, name):
    n = len(ws)

    def body(*refs):
        for i in range(n):
            w_ref, m_ref, v_ref, g_ref = (refs[j * n + i] for j in range(4))
            go_ref, d_ref, mo_ref, vo_ref = refs[4 * n + 4 * i:4 * n + 4 * i + 4]
            g = g_ref[...]
            go_ref[...] = g
            d_ref[...], mo_ref[...], vo_ref[...] = _adamw_math(w_ref[...], m_ref[...], v_ref[...], g)

    res = pl.pallas_call(body, name=name, out_shape=[jax.ShapeDtypeStruct(a.shape, F32) for a in ws for _ in range(4)])(
        *ws, *ms, *vs, *gs)
    return [res[4 * i:4 * i + 4] for i in range(n)]


def _sum_lead(x, name, rows_apart=False):
    p, r, c = x.shape
    tr = _row_tile(r, c)
    mid = (1,) if rows_apart else ()

    def body(x_ref, o_ref):
        acc = x_ref[0].astype(F32)
        for i in range(1, p):
            acc = acc + x_ref[i].astype(F32)
        o_ref[...] = acc.reshape(o_ref.shape)

    return pl.pallas_call(body, name=name, grid=(pl.cdiv(r, tr),), in_specs=[_vspec((p, tr, c), lambda i: (0, i, 0))],
                          out_specs=_vspec((tr,) + mid + (c,), lambda i: (i,) + (0,) * (1 + len(mid))),
                          out_shape=jax.ShapeDtypeStruct((r,) + mid + (c,), F32))(x)


def _allgather8(x_shard, name, plug=None):
    m_per, n = x_shard.shape

    def body(x_ref, out_ref, send_sems, recv_sems, local_sem):
        x, y, c = lax.axis_index("x"), lax.axis_index("y"), lax.axis_index("c")
        me, sibling = (x, y, c), (x, y, 1 - c)
        chips = [(1 - x, y), (x, 1 - y), (1 - x, 1 - y)]

        def rows(px, py, pc):
            return out_ref.at[pl.ds((4 * px + 2 * py + pc) * m_per, m_per), :]

        def copy(k, block, to, src=None):
            return pltpu.make_async_remote_copy(
                src_ref=rows(*block) if src is None else src, dst_ref=rows(*block), send_sem=send_sems.at[k],
                recv_sem=recv_sems.at[k], device_id=to, device_id_type=MESH)

        mine = pltpu.make_async_copy(x_ref, rows(*me), local_sem)
        mine.start()
        first = [copy(0, me, sibling, src=x_ref)]
        first += [copy(1 + j, me, (*chip, c), src=x_ref) for j, chip in enumerate(chips)]
        for cp in first:
            cp.start()
        passed = [copy(4 + j, (*chip, c), sibling) for j, chip in enumerate(chips)]
        for j, chip in enumerate(chips):
            copy(1 + j, (*chip, c), me).wait_recv()
            passed[j].start()
        copy(0, sibling, me).wait_recv()
        for j, chip in enumerate(chips):
            copy(4 + j, (*chip, 1 - c), me).wait_recv()
        for cp in first + passed:
            cp.wait_send()
        mine.wait()

    grid = (1,)
    (out,), extra = _plugged_call(
        body, plug, _grid_ends(grid), (x_shard,), name=name, grid=grid, out_shape=[jax.ShapeDtypeStruct((8 * m_per, n), x_shard.dtype)],
        in_specs=[pl.BlockSpec(memory_space=pltpu.VMEM)], out_specs=[pl.BlockSpec(memory_space=pltpu.VMEM)],
        scratch_shapes=[pltpu.SemaphoreType.DMA((7,)), pltpu.SemaphoreType.DMA((7,)), pltpu.SemaphoreType.DMA])
    return out if plug is None else (out, extra)


_HBM = pl.BlockSpec(memory_space=pltpu.HBM)


def _mesh_place():
    x, y, c = lax.axis_index("x"), lax.axis_index("y"), lax.axis_index("c")
    return x, y, c, 2 * x + y, [(1 - x, y), (x, 1 - y), (1 - x, 1 - y)]


def _gather_plug(shards):
    n = len(shards)

    def half(ref, c, lead=None):
        r, cols = ref.shape[-2] // 2, ref.shape[-1] // 2
        if r % 16 == 0:
            rows = pl.ds(pl.multiple_of(c * r, 16), r)
            return ref.at[rows, :] if lead is None else ref.at[lead, rows, :]
        lanes = pl.ds(pl.multiple_of(c * cols, LANES), cols)
        return ref.at[:, lanes] if lead is None else ref.at[lead, :, lanes]

    def copies(ins, outs, send, recv):
        x, y, c, me, chips = _mesh_place()
        ici, fwd, fwd_in = [], [], []
        for i in range(n):
            for j, (px, py) in enumerate(chips):
                q = 2 * px + py
                ici.append((pltpu.make_async_remote_copy(
                    src_ref=half(ins[i], c), dst_ref=half(outs[i], c, me), send_sem=send.at[6 * i + j], recv_sem=recv.at[6 * i + j],
                    device_id=(px, py, c), device_id_type=MESH),
                    pltpu.make_async_remote_copy(
                    src_ref=half(ins[i], c), dst_ref=half(outs[i], c, q), send_sem=send.at[6 * i + j], recv_sem=recv.at[6 * i + j],
                    device_id=(px, py, c), device_id_type=MESH)))
                fwd.append(pltpu.make_async_remote_copy(
                    src_ref=half(outs[i], c, q), dst_ref=half(outs[i], c, q), send_sem=send.at[6 * i + 3 + j],
                    recv_sem=recv.at[6 * i + 3 + j], device_id=(x, y, 1 - c), device_id_type=MESH))
                fwd_in.append(pltpu.make_async_remote_copy(
                    src_ref=half(outs[i], 1 - c, q), dst_ref=half(outs[i], 1 - c, q), send_sem=send.at[6 * i + 3 + j],
                    recv_sem=recv.at[6 * i + 3 + j], device_id=(x, y, 1 - c), device_id_type=MESH))
        return ici, fwd, fwd_in, me

    def start(ins, outs, send, recv, loc):
        ici, _, _, me = copies(ins, outs, send, recv)
        for i in range(n):
            pltpu.make_async_copy(ins[i], outs[i].at[me], loc.at[i]).start()
        for out_cp, _ in ici:
            out_cp.start()

    def finish(ins, outs, send, recv, loc):
        ici, fwd, fwd_in, me = copies(ins, outs, send, recv)
        for (_, in_cp), f in zip(ici, fwd):
            in_cp.wait_recv()
            f.start()
        for f in fwd_in:
            f.wait_recv()
        for (out_cp, _), f in zip(ici, fwd):
            out_cp.wait_send()
            f.wait_send()
        for i in range(n):
            pltpu.make_async_copy(ins[i], outs[i].at[me], loc.at[i]).wait()

    return dict(ins=list(shards), out_shape=[jax.ShapeDtypeStruct((N_CHIPS,) + a.shape, a.dtype) for a in shards],
                scratch=[pltpu.SemaphoreType.DMA((6 * n,)), pltpu.SemaphoreType.DMA((6 * n,)), pltpu.SemaphoreType.DMA((n,))],
                start=start, finish=finish)


def _exchange_plug(pieces):
    n = len(pieces)

    def copies(ins, outs, send, recv):
        x, y, c, me, chips = _mesh_place()
        out_cps, in_cps = [], []
        for i in range(n):
            for j, (px, py) in enumerate(chips):
                q = 2 * px + py
                out_cps.append(pltpu.make_async_remote_copy(src_ref=ins[i].at[q], dst_ref=outs[i].at[me], send_sem=send.at[3 * i + j],
                                                            recv_sem=recv.at[3 * i + j], device_id=(px, py, c), device_id_type=MESH))
                in_cps.append(pltpu.make_async_remote_copy(src_ref=ins[i].at[me], dst_ref=outs[i].at[q], send_sem=send.at[3 * i + j],
                                                           recv_sem=recv.at[3 * i + j], device_id=(px, py, c), device_id_type=MESH))
        return out_cps, in_cps, me

    def start(ins, outs, send, recv, loc):
        out_cps, _, me = copies(ins, outs, send, recv)
        for i in range(n):
            pltpu.make_async_copy(ins[i].at[me], outs[i].at[me], loc.at[i]).start()
        for cp in out_cps:
            cp.start()

    def finish(ins, outs, send, recv, loc):
        out_cps, in_cps, me = copies(ins, outs, send, recv)
        for cp in in_cps:
            cp.wait_recv()
        for cp in out_cps:
            cp.wait_send()
        for i in range(n):
            pltpu.make_async_copy(ins[i].at[me], outs[i].at[me], loc.at[i]).wait()

    return dict(ins=list(pieces), out_shape=[jax.ShapeDtypeStruct(a.shape, a.dtype) for a in pieces],
                scratch=[pltpu.SemaphoreType.DMA((3 * n,)), pltpu.SemaphoreType.DMA((3 * n,)), pltpu.SemaphoreType.DMA((n,))],
                start=start, finish=finish)


def _plugged_call(body, plug, first_last, args, *, name, grid, in_specs, out_specs, out_shape, scratch_shapes=(), aliases=None):
    in_specs, out_specs, out_shape, scratch_shapes = list(in_specs), list(out_specs), list(out_shape), list(scratch_shapes)
    aliases = dict(aliases or {})
    if plug is None:
        return pl.pallas_call(body, name=name, grid=grid, in_specs=in_specs, out_specs=out_specs, out_shape=out_shape,
                              scratch_shapes=scratch_shapes, input_output_aliases=aliases)(*args), []
    n_in, n_out, n_sc = len(in_specs), len(out_specs), len(scratch_shapes)
    p_in, p_out = len(plug["ins"]), len(plug["out_shape"])

    def full(*refs):
        ins, refs = refs[:n_in], refs[n_in:]
        pins, refs = refs[:p_in], refs[p_in:]
        outs, refs = refs[:n_out], refs[n_out:]
        pouts, refs = refs[:p_out], refs[p_out:]
        scr, psems = refs[:n_sc], refs[n_sc:]
        first, last = first_last()

        @pl.when(first)
        def _():
            plug["start"](pins, pouts, *psems)

        body(*ins, *outs, *scr)

        @pl.when(last)
        def _():
            plug["finish"](pins, pouts, *psems)

    res = pl.pallas_call(full, name=name, grid=grid, in_specs=in_specs + [_HBM] * p_in, out_specs=out_specs + [_HBM] * p_out,
                         out_shape=out_shape + plug["out_shape"], scratch_shapes=scratch_shapes + plug["scratch"],
                         input_output_aliases=aliases)(*args, *plug["ins"])
    return res[:n_out], res[n_out:]


def _grid_ends(grid):
    def ends():
        first = last = None
        for ax, n in enumerate(grid):
            i = pl.program_id(ax)
            first = (i == 0) if first is None else first & (i == 0)
            last = (i == n - 1) if last is None else last & (i == n - 1)
        return first, last
    return ends


def _pair_presum(pieces, name):
    n, r, cols = pieces.shape
    by_cols = (cols // 2) % LANES == 0
    half_shape = (n, r, cols // 2) if by_cols else (n, r // 2, cols)

    def body(p_ref, o_ref, mine_ref, land_ref, send_sem, recv_sem, local_sem):
        x, y, c = lax.axis_index("x"), lax.axis_index("y"), lax.axis_index("c")

        def half(which):
            if by_cols:
                return p_ref.at[:, :, pl.ds(pl.multiple_of(which * (cols // 2), LANES), cols // 2)]
            return p_ref.at[:, pl.ds(pl.multiple_of(which * (r // 2), 16), r // 2), :]

        push = pltpu.make_async_remote_copy(src_ref=half(1 - c), dst_ref=land_ref, send_sem=send_sem, recv_sem=recv_sem,
                                            device_id=(x, y, 1 - c), device_id_type=MESH)
        own = pltpu.make_async_copy(half(c), mine_ref, local_sem)
        push.start()
        own.start()
        own.wait()
        push.wait_recv()
        for i in range(n):
            o_ref[i] = (mine_ref[i].astype(F32) + land_ref[i].astype(F32)).astype(BF16)
        push.wait_send()

    return pl.pallas_call(
        body, name=name, out_shape=jax.ShapeDtypeStruct(half_shape, BF16), in_specs=[_HBM],
        out_specs=pl.BlockSpec(memory_space=pltpu.VMEM),
        scratch_shapes=[pltpu.VMEM(half_shape, BF16), pltpu.VMEM(half_shape, BF16), pltpu.SemaphoreType.DMA,
                        pltpu.SemaphoreType.DMA, pltpu.SemaphoreType.DMA],
    )(pieces)


def _sibling_plug(arrs):
    n = len(arrs)

    def copies(ins, outs, send, recv):
        sibling = (lax.axis_index("x"), lax.axis_index("y"), 1 - lax.axis_index("c"))
        return [pltpu.make_async_remote_copy(src_ref=ins[i], dst_ref=outs[i], send_sem=send.at[i], recv_sem=recv.at[i],
                                             device_id=sibling, device_id_type=MESH) for i in range(n)]

    def start(ins, outs, send, recv):
        for cp in copies(ins, outs, send, recv):
            cp.start()

    def finish(ins, outs, send, recv):
        for cp in copies(ins, outs, send, recv):
            cp.wait()

    return dict(ins=list(arrs), out_shape=[jax.ShapeDtypeStruct(a.shape, a.dtype) for a in arrs],
                scratch=[pltpu.SemaphoreType.DMA((n,)), pltpu.SemaphoreType.DMA((n,))], start=start, finish=finish)


_SEGS = ((0, AQ, C_Q), (AQ, AKV, C_K), (AQ + AKV, AKV, C_V), (AQ + 2 * AKV, 3 * DN, C_DQKV), (2304, DN, C_DZ),
         (2816, 2 * DN_HEADS, C_BD), (2824, D_MODEL, C_GA), (3848, D_MODEL, C_GD))
SHARD_ROWS = IN_DIM // N_CHIPS


def _to_padded(w4):
    wt = w4.reshape(IN_DIM, w4.shape[2])
    parts = [wt[o:o + n] for o, n, _ in sorted(_SEGS, key=lambda sg: sg[2])]
    return jnp.concatenate(parts + [jnp.zeros((IN_PAD - IN_DIM, wt.shape[1]), wt.dtype)], axis=0)


def _from_padded(gt):
    return jnp.concatenate([gt[ps:ps + n] for _, n, ps in sorted(_SEGS)], axis=0).reshape(N_CHIPS, SHARD_ROWS, gt.shape[1])


def _lane_vec(a):
    return jnp.zeros((1, LANES), F32).at[0, DN_HEADS:2 * DN_HEADS].set(a)


_ROW_SHARDED = ("w_in", "w_out", "ffn_w_down")
_FFN = ("ffn_w_up", "ffn_w_down")
_LATE_MIXER = ("w_attn_branch", "w_dn_branch", "w_out")
_PRESUM = ("w_in", "ffn_w_up")


def _pieces(k, a):
    if a.ndim == 3:
        return a
    if k in _ROW_SHARDED:
        return a.reshape(N_CHIPS, a.shape[0] // N_CHIPS, a.shape[1]).astype(BF16)
    return jnp.transpose(a.reshape(a.shape[0], N_CHIPS, a.shape[1] // N_CHIPS), (1, 0, 2)).astype(BF16)


def _assemble(k, a):
    if k in _ROW_SHARDED:
        return a.reshape(-1, a.shape[2])
    return jnp.transpose(a, (1, 0, 2)).reshape(a.shape[1], -1)


def _device_step(x2, tgt2, mod, p, bsz, shards=None):
    d = D_MODEL
    on_mesh = shards is not None
    p = dict(p)
    sh1, sc1, g1, sh2, sc2, g2 = [mod[:, i * d:(i + 1) * d].reshape(bsz, 1, d) for i in range(N_MOD)]
    alog_v, dt_v = _lane_vec(p["dn_a_log"]), _lane_vec(p["dn_dt_bias"])
    sinks = p["attn_sinks"].reshape(1, AQ_HEADS)
    if on_mesh:
        u1, got = _pre_fwd(x2, p["norm_mix_pre"], sc1, sh1, "pre1_fwd", plug=_gather_plug(shards["w_in_last"]))
        p["w_in"] = _to_padded(jnp.concatenate(list(shards["w_in_gathered"]) + list(got), axis=1))
    else:
        u1 = _pre_fwd(x2, p["norm_mix_pre"], sc1, sh1, "pre1_fwd")
    if on_mesh:
        proj, got = _mm(u1, p["w_in"], "nt", F32, "mm_proj", _gather_plug(shards["late_mixer"]))
        p.update({k: _assemble(k, a) for k, a in zip(_LATE_MIXER, got)})
    else:
        proj = _mm(u1, p["w_in"], "nt", F32, "mm_proj")
    bias = _bias_build(p["rel_bias"])
    y_attn = _attn_fwd(proj, bias, sinks, bsz)
    qkvn = _dn_prep_fwd(proj, p["dn_conv_w"], bsz)
    bg = _bg_fwd(proj, alog_v, dt_v, bsz)
    nc = x2.shape[0] // bsz // DN_CHUNK
    g_rows = jnp.transpose(bg[:, DN_HEADS:2 * DN_HEADS].reshape(bsz, nc, DN_CHUNK, DN_HEADS), (0, 3, 1, 2))
    o, states, got = _dn_chunk_fwd(qkvn, bg, g_rows, bsz, _gather_plug(shards["ffn"][:1]) if on_mesh else None)
    for k, a in zip(_FFN[:1], got):
        p[k] = _assemble(k, a)
    y_dn = _dn_out_fwd(o, proj, p["dn_norm_w"])
    ya = _mm(y_attn, p["w_attn_branch"], "nn", BF16, "mm_ya")
    yd = _mm(y_dn, p["w_dn_branch"], "nn", BF16, "mm_yd")
    merged = _merge_fwd(proj, ya, yd)
    y1 = _mm(merged, p["w_out"], "nn", F32, "mm_y1")
    h1, u2 = _post_pre_fwd(x2, y1, p["norm_mix_post"], g1, p["norm_ffn_pre"], sc2, sh2, "post1_pre2_fwd")
    if on_mesh:
        up, got = _mm(u2, p["ffn_w_up"], "nn", F32, "mm_up", _gather_plug(shards["ffn"][1:]))
        p["ffn_w_down"] = _assemble("ffn_w_down", got[0])
    else:
        up = _mm(u2, p["ffn_w_up"], "nn", F32, "mm_up")
    act, conv_g, conv_v = _ffn_act_fwd(up, p["ffn_conv_w"], bsz)
    y2 = _mm(act, p["ffn_w_down"], "nn", F32, "mm_y2")
    dh2, dy2, g_ffn_post, dg2, sq = _post_loss_bwd(h1, y2, p["norm_ffn_post"], g2, tgt2, "post2_loss_bwd")
    g = {}
    g["norm_ffn_post"] = g_ffn_post
    dact = _mm(dy2, p["ffn_w_down"], "nt", F32, "mm_dact")
    g["ffn_w_down"] = _mm(act, dy2, "tn", BF16, "mm_dwdown")
    dupg, dupv, dcwg, dcwv, got_down = _ffn_act_bwd(
        up, p["ffn_conv_w"], (conv_g, conv_v), dact, bsz,
        _exchange_plug([_pieces("ffn_w_down", g["ffn_w_down"])]) if on_mesh else None)
    g["ffn_conv_w"] = jnp.concatenate([dcwg, dcwv], axis=1)
    g["ffn_w_up"] = jnp.concatenate([_mm(u2, dupg, "tn", BF16, "mm_dwup_gate", split=N_CHIPS // 2),
                                     _mm(u2, dupv, "tn", BF16, "mm_dwup_val", split=N_CHIPS // 2)], axis=0)
    du2 = _mm(dupg, p["ffn_w_up"], "nt", F32, "mm_du2_gate", b_kblock=0)
    du2 = _mm(dupv, p["ffn_w_up"], "nt", F32, "mm_du2_val", b_kblock=1, add=du2)
    dh1, dy1, g["norm_ffn_pre"], dsc2, dsh2, g["norm_mix_post"], dg1 = _pre_post_bwd(
        h1, p["norm_ffn_pre"], sc2, sh2, du2, dh2, y1, p["norm_mix_post"], g1, "pre2_post1_bwd")
    dmerged = _mm(dy1, p["w_out"], "nt", BF16, "mm_dmerged")
    g["w_out"] = _mm(merged, dy1, "tn", BF16, "mm_dwout")
    dproj = lax.empty((x2.shape[0], IN_PAD), BF16)
    dproj, dya = _branch_bwd(dproj, proj, ya, dmerged, C_GA, "merge_bwd_attn")
    dproj, dyd = _branch_bwd(dproj, proj, yd, dmerged, C_GD, "merge_bwd_dn")
    dy_attn = _mm(dya, p["w_attn_branch"], "nt", BF16, "mm_dyattn")
    g["w_attn_branch"] = _mm(y_attn, dya, "tn", BF16, "mm_dwab", split=N_CHIPS)
    dy_dn = _mm(dyd, p["w_dn_branch"], "nt", F32, "mm_dydn")
    g["w_dn_branch"] = _mm(y_dn, dyd, "tn", BF16, "mm_dwdb", split=N_CHIPS)
    do, dproj, g["dn_norm_w"] = _dn_out_bwd(dproj, o, proj, p["dn_norm_w"], dy_dn)
    def plug_for(names):
        if not on_mesh:
            return None
        return _exchange_plug([_pair_presum(_pieces(k, g[k]), "presum_" + k) if k in _PRESUM else _pieces(k, g[k]) for k in names])

    early = ("w_out", "w_attn_branch", "w_dn_branch")
    dqkvn, dbg4, got_up = _dn_chunk_bwd(qkvn, bg, g_rows, states, do, bsz, plug_for(_FFN[:1]))
    got_ffn = list(got_up) + list(got_down)
    dproj, g["dn_conv_w"] = _dn_prep_bwd(dproj, proj, p["dn_conv_w"], dqkvn, bsz)
    dproj, dk, dv, dbias, g["attn_sinks"], got_early = _attn_bwd(dproj, proj, bias, sinks, dy_attn, bsz, plug_for(early))
    dproj, g["dn_a_log"], g["dn_dt_bias"] = _bg_bwd(dproj, proj, alog_v, dt_v, dbg4, dk, dv, bsz)
    g["rel_bias"] = _bias_grad(dbias)
    g["w_in"] = _from_padded(_mm(dproj, u1, "tn", BF16, "mm_dwin"))
    if on_mesh:
        du1, got_in = _mm(dproj, p["w_in"], "nn", F32, "mm_du1", plug_for(("w_in",)))
        g.update(zip(_FFN + early + ("w_in",), list(got_ffn) + list(got_early) + list(got_in)))
    else:
        du1 = _mm(dproj, p["w_in"], "nn", F32, "mm_du1")
    dx, g["norm_mix_pre"], dsc1, dsh1 = _pre_bwd(x2, p["norm_mix_pre"], sc1, sh1, du1, dh1, "pre1_bwd")
    dmod = jnp.concatenate([dsh1, dsc1, dg1, dsh2, dsc2, dg2], axis=-1).reshape(bsz, N_MOD * d)
    return sq, dx, dmod, g


_SMALL = (("norm_mix_pre", D_MODEL), ("norm_mix_post", D_MODEL), ("norm_ffn_pre", D_MODEL), ("norm_ffn_post", D_MODEL),
          ("dn_norm_w", DN_HD), ("dn_a_log", LANES), ("dn_dt_bias", LANES), ("attn_sinks", AQ_HEADS * LANES),
          ("rel_bias", AQ_HEADS * LANES), ("dn_conv_w", DN_CONV * 3 * DN), ("ffn_conv_w", FFN_CONV * 2 * D_FF),
          ("loss_sq", D_MODEL))


def _pack_rows(parts, rows):
    flat = jnp.concatenate([a.reshape(-1) for a in parts])
    return jnp.concatenate([flat, jnp.zeros((rows * LANES - flat.shape[0],), F32)]).reshape(rows, LANES)


_W_NAMES = ("ada_w", "ada_b", "norm_mix_pre", "norm_mix_post", "norm_ffn_pre", "norm_ffn_post", "w_in", "dn_conv_w", "dn_a_log",
            "dn_dt_bias", "dn_norm_w", "attn_sinks", "rel_bias", "w_attn_branch", "w_dn_branch", "w_out", "ffn_w_up", "ffn_conv_w",
            "ffn_w_down")
_BIG = ("w_in", "w_attn_branch", "w_dn_branch", "w_out", "ffn_w_up", "ffn_w_down")


def kernel(x, c, *rest):
    nw = len(_W_NAMES)
    w = dict(zip(_W_NAMES, rest[:nw]))
    loss_target = rest[nw]
    m = dict(zip(_W_NAMES, rest[nw + 1:2 * nw + 1]))
    v = dict(zip(_W_NAMES, rest[2 * nw + 1:3 * nw + 1]))
    ix, iy, ic = lax.axis_index("x"), lax.axis_index("y"), lax.axis_index("c")
    chip, dev = 2 * ix + iy, 4 * ix + 2 * iy + ic
    bsz, s, d = x.shape
    t = bsz * s
    n_dev = 8

    front_rows = 64
    front = _pack_rows([c, w["dn_conv_w"], w["ffn_conv_w"]], front_rows)
    w_in_t = jnp.swapaxes(w["w_in"][0], 0, 1).astype(BF16)
    cut1 = SHARD_ROWS // 48 * 16
    cut2 = 2 * cut1
    front_all, (w_in_lo,) = _allgather8(front, "ag_front", _gather_plug([w_in_t[:cut1]]))
    front_all = front_all.reshape(n_dev, front_rows * LANES)
    n_c, n_dc, n_fc = bsz * d, DN_CONV * 3 * DN // N_CHIPS, FFN_CONV * 2 * D_FF // N_CHIPS
    c_all = front_all[:, :n_c].reshape(n_dev * bsz, d)
    per_chip = front_all[0::2]
    dn_conv_full = jnp.transpose(per_chip[:, n_c:n_c + n_dc].reshape(N_CHIPS, DN_CONV, -1), (1, 0, 2)).reshape(DN_CONV, 3 * DN)
    ffn_conv_full = jnp.transpose(per_chip[:, n_c + n_dc:n_c + n_dc + n_fc].reshape(N_CHIPS, FFN_CONV, -1), (1, 0, 2)).reshape(FFN_CONV, 2 * D_FF)

    mod_cols = N_MOD * d // N_CHIPS
    ada_b_loc = lax.dynamic_slice(w["ada_b"], (0, chip * mod_cols), (1, mod_cols))
    mod_part = _mod_fwd(c_all, w["ada_w"][0], ada_b_loc)
    mod_all, (w_in_mid,) = _allgather8(mod_part, "ag_mod", _gather_plug([w_in_t[cut1:cut2]]))
    mod_all = mod_all.reshape(n_dev, n_dev * bsz, mod_cols)[0::2]
    mod = jnp.transpose(lax.dynamic_slice(mod_all, (0, dev * bsz, 0), (N_CHIPS, bsz, mod_cols)), (1, 0, 2)).reshape(bsz, N_MOD * d)

    p = {}
    shards = {"late_mixer": [w[k][0].astype(BF16) for k in _LATE_MIXER], "ffn": [w[k][0].astype(BF16) for k in _FFN],
              "w_in_gathered": [w_in_lo, w_in_mid], "w_in_last": [w_in_t[cut2:]]}
    for k in ("norm_mix_pre", "norm_mix_post", "norm_ffn_pre", "norm_ffn_post", "dn_norm_w", "attn_sinks"):
        p[k] = w[k]
    p["dn_a_log"], p["dn_dt_bias"], p["rel_bias"] = w["dn_a_log"][0], w["dn_dt_bias"][0], w["rel_bias"]
    p["dn_conv_w"], p["ffn_conv_w"] = dn_conv_full, ffn_conv_full

    sq, dx, dmod, g = _device_step(x.reshape(t, d), loss_target.reshape(t, d), mod, p, bsz, shards)
    g["dn_a_log"], g["dn_dt_bias"], g["loss_sq"] = g["dn_a_log"].reshape(-1), g["dn_dt_bias"].reshape(-1), sq
    small_rows = 336
    small = _pack_rows([dmod] + [g[k] for k, _ in _SMALL], small_rows)
    small_all = _allgather8(small, "ag_small").reshape(n_dev, small_rows, LANES)
    n_dm = bsz * N_MOD * d
    dmod_all = small_all.reshape(n_dev, -1)[:, :n_dm].reshape(n_dev * bsz, N_MOD * d)
    tot = _sum_lead(small_all, "sum_small").reshape(-1)
    gs, off = {}, n_dm
    for k, n in _SMALL:
        gs[k] = tot[off:off + n]
        off += n
    loss = 0.5 * jnp.sum(gs["loss_sq"])
    grad = {}
    grad["ada_w"], grad["ada_b"] = _ada_grad(c_all, lax.dynamic_slice(dmod_all, (0, chip * mod_cols), (n_dev * bsz, mod_cols)), dmod_all)
    for k in ("norm_mix_pre", "norm_mix_post", "norm_ffn_pre", "norm_ffn_post", "dn_norm_w"):
        grad[k] = gs[k]
    grad["dn_a_log"] = gs["dn_a_log"][DN_HEADS:2 * DN_HEADS]
    grad["dn_dt_bias"] = gs["dn_dt_bias"][DN_HEADS:2 * DN_HEADS]
    grad["attn_sinks"] = gs["attn_sinks"].reshape(AQ_HEADS, LANES)[:, 0]
    grad["rel_bias"] = gs["rel_bias"].reshape(AQ_HEADS, LANES)[:, :REL_BUCKETS].T
    grad["dn_conv_w"] = lax.dynamic_slice(gs["dn_conv_w"].reshape(DN_CONV, 3 * DN), (0, chip * (3 * DN // N_CHIPS)), (DN_CONV, 3 * DN // N_CHIPS))
    grad["ffn_conv_w"] = lax.dynamic_slice(gs["ffn_conv_w"].reshape(FFN_CONV, 2 * D_FF), (0, chip * (2 * D_FF // N_CHIPS)), (FFN_CONV, 2 * D_FF // N_CHIPS))

    mine = [_sum_lead(g[k], "sum_" + k, rows_apart=(k == "w_in")) for k in _BIG]
    out = {}
    out["ada_w"], theirs = _adamw(w["ada_w"][0], m["ada_w"][0], v["ada_w"][0], [grad["ada_w"]], "adamw_ada_w", _sibling_plug(mine))
    for k, a, b in zip(_BIG, mine, theirs):
        core = ic.reshape(1).astype(jnp.int32)
        if k == "w_in":
            tr = lambda z: jnp.transpose(z, (2, 0, 1))
            out[k] = [jnp.transpose(r, (1, 2, 0)) for r in _adamw_halves(tr(w[k]), tr(m[k]), tr(v[k]), a, b, core, "adamw_" + k)]
        elif k in _PRESUM:
            out[k] = _adamw_halves(w[k][0], m[k][0], v[k][0], a, b, core, "adamw_" + k)
        else:
            out[k] = _adamw(w[k][0], m[k][0], v[k][0], [a, b], "adamw_" + k)
    small_names = [k for k in _W_NAMES if k not in _BIG and k != "ada_w"]
    res_small = _adamw_small([w[k] for k in small_names], [m[k] for k in small_names], [v[k] for k in small_names],
                             [grad[k].reshape(w[k].shape) for k in small_names], "adamw_small")
    out.update(zip(small_names, res_small))
    for k in _BIG + ("ada_w",):
        out[k] = [r.reshape(w[k].shape) for r in out[k]]
    grads, deltas, new_m, new_v = ([out[k][i] for k in _W_NAMES] for i in range(4))
    return (loss, dx.reshape(bsz, s, d), *grads, *deltas, *new_m, *new_v)
```

```python
import math

import numpy as np
import jax
import jax.numpy as jnp
from jax import lax
from jax.experimental import pallas as pl
from jax.experimental.pallas import tpu as pltpu

F32 = jnp.float32
BF16 = jnp.bfloat16
MESH = pl.DeviceIdType.MESH

D_MODEL = 1024
N_MOD = 6
AQ_HEADS, AKV_HEADS, A_HD, WINDOW = 8, 2, 64, 128
REL_BUCKETS, REL_MAX_DIST = 32, 128
DN_HEADS, DN_HD, DN_CONV, DN_CHUNK = 4, 128, 4, 64
D_FF, FFN_CONV = 2816, 3
RMS_EPS, L2_EPS, NEG_INF = 1e-6, 1e-6, -1e30
AQ, AKV, DN = AQ_HEADS * A_HD, AKV_HEADS * A_HD, DN_HEADS * DN_HD
IN_DIM = AQ + 2 * AKV + 3 * DN + DN + 2 * DN_HEADS + 2 * D_MODEL
C_DQKV, C_Q, C_DZ, C_GA, C_GD, C_K, C_V, C_BD = 0, 1536, 2048, 2560, 3584, 4608, 4736, 4864
IN_PAD = 4992
LANES = 128
N_CHIPS = 4

ADAM_LR, ADAM_B1, ADAM_B2, ADAM_EPS, ADAM_WD, ADAM_STEP = 0.001, 0.9, 0.999, 1e-08, 0.01, 10


def _vspec(shape, index_map):
    return pl.BlockSpec(shape, index_map)


MM_VMEM_BUDGET = 40 * 2 ** 20
GRID_STEP_S = 0.35e-6
HBM_BYTES_PER_S = 3.0e12
MXU_FLOPS_PER_S = 9.0e14
MXU_DIM = 256


def _mm_tiles(m, n, k, mode, in_bytes, out_bytes, split=1):
    best = None
    for tm in [t for t in range(LANES, m + 1, LANES) if m % t == 0]:
        for tn in [t for t in range(LANES, n // split + 1, LANES) if (n // split) % t == 0]:
            a, b, o = k * tm * in_bytes, k * tn * in_bytes, tm * tn * out_bytes
            if 2 * (a + b + o) + (a if mode == "tn" else 0) > MM_VMEM_BUDGET:
                continue
            hbm_s = (m * k * in_bytes + (m // tm) * n * k * in_bytes + m * n * out_bytes) / HBM_BYTES_PER_S
            mxu_s = 2 * m * n * k / (MXU_FLOPS_PER_S * min(1.0, tm / MXU_DIM) * min(1.0, tn / MXU_DIM))
            cost = (m // tm) * (n // tn) * GRID_STEP_S + max(hbm_s, mxu_s)
            if best is None or cost < best[0]:
                best = (cost, tm, tn)
    return best[1], best[2]


def _mm(a, b, mode, out_dtype, name, plug=None, split=1, b_kblock=0, add=None):
    if mode == "nn":
        (m, k), n = a.shape, b.shape[1]
        dims = (((1,), (0,)), ((), ()))
    elif mode == "nt":
        (m, k), n = a.shape, b.shape[0]
        dims = (((1,), (1,)), ((), ()))
    else:
        (k, m), n = a.shape, b.shape[1]
        dims = (((0,), (0,)), ((), ()))
    tm, tn = _mm_tiles(m, n, k, mode, a.dtype.itemsize, jnp.dtype(out_dtype).itemsize, split)
    if mode == "tn":
        a_spec = _vspec((k, tm), lambda i, j: (0, i))
    else:
        a_spec = _vspec((tm, k), lambda i, j: (i, 0))
    if mode == "nt":
        b_spec = _vspec((tn, k), lambda i, j: (j, b_kblock))
    else:
        b_spec = _vspec((k, tn), lambda i, j: (0, j))
    in_specs, args = [a_spec, b_spec], (a, b)
    if add is not None:
        in_specs, args = in_specs + [_vspec((tm, tn), lambda i, j: (i, j))], (a, b, add)

    def body(a_ref, b_ref, *rest):
        o_ref = rest[-1]
        acc = lax.dot_general(a_ref[...].astype(BF16), b_ref[...].astype(BF16), dims, preferred_element_type=F32)
        if add is not None:
            acc = acc + rest[0][...]
        o_ref[...] = acc.astype(out_dtype).reshape(o_ref.shape)

    grid = (m // tm, n // tn)
    if split == 1:
        out_spec, out_shape = _vspec((tm, tn), lambda i, j: (i, j)), (m, n)
    else:
        per = n // split // tn
        out_spec, out_shape = _vspec((1, tm, tn), lambda i, j: (j // per, i, j % per)), (split, m, n // split)
    (out,), extra = _plugged_call(body, plug, _grid_ends(grid), args, name=name, grid=grid, in_specs=in_specs,
                                  out_specs=[out_spec], out_shape=[jax.ShapeDtypeStruct(out_shape, out_dtype)])
    return out if plug is None else (out, extra)


def _rms(x, w):
    return (x * lax.rsqrt(jnp.mean(x * x, axis=-1, keepdims=True) + RMS_EPS)) * w


def _pre_f(x, w, sc, sh):
    return _rms(x, w) * (1.0 + sc) + sh


def _post_f(y, w, g):
    return g * _rms(y, w)


def _tok_grid(t, bsz, ts):
    nt = t // bsz // ts
    return nt, (bsz, nt)


def _pre_fwd(x, w, sc, sh, name, ts=512, plug=None):
    t, d = x.shape
    bsz = sc.shape[0]
    nt, grid = _tok_grid(t, bsz, ts)
    row = _vspec((ts, d), lambda b, i: (b * nt + i, 0))
    vec = _vspec((1, d), lambda b, i: (0, 0))
    bvec = _vspec((1, 1, d), lambda b, i: (b, 0, 0))

    def body(x_ref, w_ref, sc_ref, sh_ref, u_ref):
        u_ref[...] = _pre_f(x_ref[...], w_ref[...], sc_ref[0], sh_ref[0]).astype(BF16)

    (u,), extra = _plugged_call(body, plug, _grid_ends(grid), (x, w, sc, sh), name=name, grid=grid, in_specs=[row, vec, bvec, bvec],
                                out_specs=[row], out_shape=[jax.ShapeDtypeStruct((t, d), BF16)])
    return u if plug is None else (u, extra)


def _pre_bwd(x, w, sc, sh, du, dres, name, ts=512):
    t, d = x.shape
    bsz = sc.shape[0]
    nt, grid = _tok_grid(t, bsz, ts)
    row = _vspec((ts, d), lambda b, i: (b * nt + i, 0))
    vec = _vspec((1, d), lambda b, i: (0, 0))
    bvec = _vspec((1, 1, d), lambda b, i: (b, 0, 0))

    def body(x_ref, w_ref, sc_ref, sh_ref, du_ref, dres_ref, dx_ref, dw_ref, dsc_ref, dsh_ref):
        b, i = pl.program_id(0), pl.program_id(1)
        _, vjp = jax.vjp(_pre_f, x_ref[...], w_ref[...], sc_ref[0], sh_ref[0])
        dx, dw, dsc, dsh = vjp(du_ref[...])
        dx_ref[...] = dres_ref[...] + dx

        @pl.when((b == 0) & (i == 0))
        def _():
            dw_ref[...] = jnp.zeros_like(dw_ref)

        @pl.when(i == 0)
        def _():
            dsc_ref[...] = jnp.zeros_like(dsc_ref)
            dsh_ref[...] = jnp.zeros_like(dsh_ref)

        dw_ref[...] += dw
        dsc_ref[0] += dsc
        dsh_ref[0] += dsh

    return pl.pallas_call(
        body, name=name, grid=grid, in_specs=[row, vec, bvec, bvec, row, row], out_specs=[row, vec, bvec, bvec],
        out_shape=[jax.ShapeDtypeStruct((t, d), F32), jax.ShapeDtypeStruct((1, d), F32),
                   jax.ShapeDtypeStruct((bsz, 1, d), F32), jax.ShapeDtypeStruct((bsz, 1, d), F32)],
    )(x, w, sc, sh, du, dres)


def _accumulate(ref, val, first):
    @pl.when(first)
    def _():
        ref[...] = jnp.zeros_like(ref)

    ref[...] += val.reshape(ref.shape)


def _post_pre_fwd(res, y, w_post, g, w_pre, sc, sh, name, ts=512):
    t, d = y.shape
    bsz = g.shape[0]
    nt, grid = _tok_grid(t, bsz, ts)
    row = _vspec((ts, d), lambda b, i: (b * nt + i, 0))
    vec = _vspec((1, d), lambda b, i: (0, 0))
    bvec = _vspec((1, 1, d), lambda b, i: (b, 0, 0))

    def body(res_ref, y_ref, wp_ref, g_ref, w_ref, sc_ref, sh_ref, h_ref, u_ref):
        h = res_ref[...] + _post_f(y_ref[...], wp_ref[...], g_ref[0])
        h_ref[...] = h
        u_ref[...] = _pre_f(h, w_ref[...], sc_ref[0], sh_ref[0]).astype(BF16)

    return pl.pallas_call(body, name=name, grid=grid, in_specs=[row, row, vec, bvec, vec, bvec, bvec], out_specs=[row, row],
                          out_shape=[jax.ShapeDtypeStruct((t, d), F32), jax.ShapeDtypeStruct((t, d), BF16)],
                          )(res, y, w_post, g, w_pre, sc, sh)


def _post_loss_bwd(res, y, w, g, tgt, name, ts=512):
    t, d = y.shape
    bsz = g.shape[0]
    nt, grid = _tok_grid(t, bsz, ts)
    row = _vspec((ts, d), lambda b, i: (b * nt + i, 0))
    vec = _vspec((1, d), lambda b, i: (0, 0))
    bvec = _vspec((1, 1, d), lambda b, i: (b, 0, 0))

    def body(res_ref, y_ref, w_ref, g_ref, tgt_ref, dh_ref, dy_ref, dw_ref, dg_ref, sq_ref):
        b, i = pl.program_id(0), pl.program_id(1)
        part, vjp = jax.vjp(_post_f, y_ref[...], w_ref[...], g_ref[0])
        e = res_ref[...] + part - tgt_ref[...]
        dh = e * (1.0 / d)
        dh_ref[...] = dh
        dy, dw, dg = vjp(dh)
        dy_ref[...] = dy.astype(BF16)
        _accumulate(dw_ref, dw, (b == 0) & (i == 0))
        _accumulate(dg_ref, dg, i == 0)
        _accumulate(sq_ref, jnp.sum(e * e, axis=0, keepdims=True) * (1.0 / d), (b == 0) & (i == 0))

    return pl.pallas_call(
        body, name=name, grid=grid, in_specs=[row, row, vec, bvec, row], out_specs=[row, row, vec, bvec, vec],
        out_shape=[jax.ShapeDtypeStruct((t, d), F32), jax.ShapeDtypeStruct((t, d), BF16), jax.ShapeDtypeStruct((1, d), F32),
                   jax.ShapeDtypeStruct((bsz, 1, d), F32), jax.ShapeDtypeStruct((1, d), F32)],
    )(res, y, w, g, tgt)


def _pre_post_bwd(x, w, sc, sh, du, dres, y, w_post, g, name, ts=512):
    t, d = x.shape
    bsz = sc.shape[0]
    nt, grid = _tok_grid(t, bsz, ts)
    row = _vspec((ts, d), lambda b, i: (b * nt + i, 0))
    vec = _vspec((1, d), lambda b, i: (0, 0))
    bvec = _vspec((1, 1, d), lambda b, i: (b, 0, 0))

    def body(x_ref, w_ref, sc_ref, sh_ref, du_ref, dres_ref, y_ref, wp_ref, g_ref,
             dx_ref, dy_ref, dw_ref, dsc_ref, dsh_ref, dwp_ref, dg_ref):
        b, i = pl.program_id(0), pl.program_id(1)
        _, vjp = jax.vjp(_pre_f, x_ref[...], w_ref[...], sc_ref[0], sh_ref[0])
        dx, dw, dsc, dsh = vjp(du_ref[...])
        dx = dres_ref[...] + dx
        dx_ref[...] = dx
        _, vjp_post = jax.vjp(_post_f, y_ref[...], wp_ref[...], g_ref[0])
        dy, dwp, dg = vjp_post(dx)
        dy_ref[...] = dy.astype(BF16)
        first = (b == 0) & (i == 0)
        _accumulate(dw_ref, dw, first)
        _accumulate(dwp_ref, dwp, first)
        _accumulate(dsc_ref, dsc, i == 0)
        _accumulate(dsh_ref, dsh, i == 0)
        _accumulate(dg_ref, dg, i == 0)

    v1, vb = jax.ShapeDtypeStruct((1, d), F32), jax.ShapeDtypeStruct((bsz, 1, d), F32)
    return pl.pallas_call(
        body, name=name, grid=grid, in_specs=[row, vec, bvec, bvec, row, row, row, vec, bvec],
        out_specs=[row, row, vec, bvec, bvec, vec, bvec],
        out_shape=[jax.ShapeDtypeStruct((t, d), F32), jax.ShapeDtypeStruct((t, d), BF16), v1, vb, vb, v1, vb],
    )(x, w, sc, sh, du, dres, y, w_post, g)


def _merge_f(ga, gd, ya, yd):
    return jax.nn.sigmoid(ga) * ya + jax.nn.sigmoid(gd) * yd


_MW = 512


def _merge_fwd(proj, ya, yd, ts=512):
    t, d = ya.shape
    blk = _vspec((ts, _MW), lambda i, j: (i, j))
    ga = _vspec((ts, _MW), lambda i, j: (i, C_GA // _MW + j))
    gd = _vspec((ts, _MW), lambda i, j: (i, C_GD // _MW + j))

    def body(ga_ref, gd_ref, ya_ref, yd_ref, o_ref):
        o_ref[...] = _merge_f(ga_ref[...], gd_ref[...], ya_ref[...].astype(F32), yd_ref[...].astype(F32)).astype(BF16)

    return pl.pallas_call(body, name="merge_fwd", grid=(t // ts, d // _MW), in_specs=[ga, gd, blk, blk], out_specs=blk,
                          out_shape=jax.ShapeDtypeStruct((t, d), BF16))(proj, proj, ya, yd)


_ANY = pl.BlockSpec(memory_space=pl.ANY)


def _branch_bwd(dproj, proj, y, dm, col0, name, ts=512):
    t, d = y.shape
    blk = _vspec((ts, _MW), lambda i, j: (i, j))
    gate = _vspec((ts, _MW), lambda i, j: (i, col0 // _MW + j))

    def body(buf_ref, g_ref, y_ref, dm_ref, dg_ref, dy_ref):
        del buf_ref
        _, vjp = jax.vjp(lambda g, yy: jax.nn.sigmoid(g) * yy, g_ref[...], y_ref[...].astype(F32))
        dg, dy = vjp(dm_ref[...].astype(F32))
        dg_ref[...] = dg.astype(BF16)
        dy_ref[...] = dy.astype(BF16)

    return pl.pallas_call(body, name=name, grid=(t // ts, d // _MW), in_specs=[_ANY, gate, blk, blk], out_specs=[gate, blk],
                          out_shape=[jax.ShapeDtypeStruct(dproj.shape, BF16), jax.ShapeDtypeStruct((t, d), BF16)],
                          input_output_aliases={0: 0})(dproj, proj, y, dm)


def _shift_down(x, s):
    if s == 0:
        return x
    r = lax.broadcasted_iota(jnp.int32, x.shape, 0)
    return jnp.where(r >= s, pltpu.roll(x, s, 0), 0.0)


def _shift_up(x, s):
    if s == 0:
        return x
    n = x.shape[0]
    r = lax.broadcasted_iota(jnp.int32, x.shape, 0)
    return jnp.where(r < n - s, pltpu.roll(x, n - s, 0), 0.0)


def _conv_fwd(x, w, k):
    out = None
    for j in range(k):
        term = w[j:j + 1, :] * _shift_down(x, k - 1 - j)
        out = term if out is None else out + term
    return out


def _conv_bwd(x, w, dc, k):
    dx = None
    dws = []
    for j in range(k):
        up = _shift_up(dc, k - 1 - j)
        term = w[j:j + 1, :] * up
        dx = term if dx is None else dx + term
        dws.append(jnp.sum(up * x, axis=0, keepdims=True))
    return dx, jnp.concatenate(dws, axis=0)


def _geglu_f(gate, val):
    return jax.nn.gelu(gate, approximate=True) * val


_FW = 256


def _ffn_act_fwd(up, conv_w, bsz):
    t = up.shape[0]
    s = t // bsz
    nj = D_FF // _FW
    xg = _vspec((s, _FW), lambda b, j: (b, j))
    xv = _vspec((s, _FW), lambda b, j: (b, nj + j))
    wg = _vspec((FFN_CONV, _FW), lambda b, j: (0, j))
    wv = _vspec((FFN_CONV, _FW), lambda b, j: (0, nj + j))

    def body(xg_ref, xv_ref, wg_ref, wv_ref, o_ref, cg_ref, cv_ref):
        gate = _conv_fwd(xg_ref[...].astype(F32), wg_ref[...], FFN_CONV)
        val = _conv_fwd(xv_ref[...].astype(F32), wv_ref[...], FFN_CONV)
        o_ref[...] = _geglu_f(gate, val).astype(BF16)
        cg_ref[...] = gate.astype(BF16)
        cv_ref[...] = val.astype(BF16)

    blk = _vspec((s, _FW), lambda b, j: (b, j))
    o = jax.ShapeDtypeStruct((t, D_FF), BF16)
    return pl.pallas_call(body, name="ffn_act_fwd", grid=(bsz, nj), in_specs=[xg, xv, wg, wv], out_specs=[blk] * 3,
                          out_shape=[o] * 3)(up, up, conv_w, conv_w)


def _ffn_act_bwd(up, conv_w, conv_out, dact, bsz, plug=None):
    t = up.shape[0]
    s = t // bsz
    nj = D_FF // _FW
    xg = _vspec((s, _FW), lambda j, b: (b, j))
    xv = _vspec((s, _FW), lambda j, b: (b, nj + j))
    wg = _vspec((FFN_CONV, _FW), lambda j, b: (0, j))
    wv = _vspec((FFN_CONV, _FW), lambda j, b: (0, nj + j))
    da = _vspec((s, _FW), lambda j, b: (b, j))
    dwo = _vspec((FFN_CONV, _FW), lambda j, b: (0, j))

    def body(xg_ref, xv_ref, wg_ref, wv_ref, cg_ref, cv_ref, da_ref, dxg_ref, dxv_ref, dwg_ref, dwv_ref):
        b = pl.program_id(1)
        xg_, xv_, wg_, wv_ = xg_ref[...].astype(F32), xv_ref[...].astype(F32), wg_ref[...], wv_ref[...]
        _, vjp = jax.vjp(_geglu_f, cg_ref[...].astype(F32), cv_ref[...].astype(F32))
        dgate, dval = vjp(da_ref[...])
        dxg, dwg = _conv_bwd(xg_, wg_, dgate, FFN_CONV)
        dxv, dwv = _conv_bwd(xv_, wv_, dval, FFN_CONV)
        dxg_ref[...] = dxg.astype(BF16)
        dxv_ref[...] = dxv.astype(BF16)

        @pl.when(b == 0)
        def _():
            dwg_ref[...] = jnp.zeros_like(dwg_ref)
            dwv_ref[...] = jnp.zeros_like(dwv_ref)

        dwg_ref[...] += dwg
        dwv_ref[...] += dwv

    outs, extra = _plugged_call(
        body, plug, _grid_ends((nj, bsz)), (up, up, conv_w, conv_w, *conv_out, dact), name="ffn_act_bwd", grid=(nj, bsz),
        in_specs=[xg, xv, wg, wv, da, da, da], out_specs=[da, da, dwo, dwo],
        out_shape=[jax.ShapeDtypeStruct((t, D_FF), BF16), jax.ShapeDtypeStruct((t, D_FF), BF16),
                   jax.ShapeDtypeStruct((FFN_CONV, D_FF), F32), jax.ShapeDtypeStruct((FFN_CONV, D_FF), F32)])
    return (*outs, extra)


def _bucket_table():
    qi = np.arange(WINDOW)[:, None]
    kj = np.arange(2 * WINDOW)[None, :]
    dist = WINDOW + qi - kj
    dc = np.maximum(dist, 0)
    max_exact = REL_BUCKETS // 2
    scaled = np.log(np.maximum(dc, 1).astype(np.float32) / np.float32(max_exact)) / np.float32(math.log(REL_MAX_DIST / max_exact))
    large = max_exact + (scaled.astype(np.float32) * np.float32(REL_BUCKETS - max_exact)).astype(np.int32)
    large = np.minimum(large, REL_BUCKETS - 1)
    bucket = np.where(dc < max_exact, dc, large).astype(np.int32)
    in_band = ((dist >= 0) & (dist < WINDOW)).astype(np.int32)
    return bucket, in_band


def _bias_build(rel_bias):
    bucket, _ = _bucket_table()

    def body(rb_ref, idx_ref, o_ref):
        h = pl.program_id(0)
        idx = idx_ref[...]
        acc = jnp.zeros(idx.shape, F32)
        for r in range(REL_BUCKETS):
            acc = jnp.where(idx == r, rb_ref[r, h], acc)
        o_ref[0] = acc

    return pl.pallas_call(
        body, name="bias_build", grid=(AQ_HEADS,),
        in_specs=[pl.BlockSpec(memory_space=pltpu.SMEM), _vspec((WINDOW, 2 * WINDOW), lambda h: (0, 0))],
        out_specs=_vspec((1, WINDOW, 2 * WINDOW), lambda h: (h, 0, 0)),
        out_shape=jax.ShapeDtypeStruct((AQ_HEADS, WINDOW, 2 * WINDOW), F32),
    )(rel_bias, jnp.asarray(bucket))


def _bias_grad(dbias):
    bucket, _ = _bucket_table()

    def body(db_ref, idx_ref, o_ref):
        idx = idx_ref[...]
        db = db_ref[0]
        lane = lax.broadcasted_iota(jnp.int32, (1, LANES), 1)
        acc = jnp.zeros((1, LANES), F32)
        for r in range(REL_BUCKETS):
            s = jnp.sum(jnp.sum(jnp.where(idx == r, db, 0.0), axis=1, keepdims=True), axis=0, keepdims=True)
            acc = jnp.where(lane == r, s, acc)
        o_ref[0] = acc

    return pl.pallas_call(
        body, name="bias_grad", grid=(AQ_HEADS,),
        in_specs=[_vspec((1, WINDOW, 2 * WINDOW), lambda h: (h, 0, 0)), _vspec((WINDOW, 2 * WINDOW), lambda h: (0, 0))],
        out_specs=_vspec((1, 1, LANES), lambda h: (h, 0, 0)),
        out_shape=jax.ShapeDtypeStruct((AQ_HEADS, 1, LANES), F32),
    )(dbias, jnp.asarray(bucket))


def _attn_mask(n):
    qi = lax.broadcasted_iota(jnp.int32, (WINDOW, 2 * WINDOW), 0)
    kj = lax.broadcasted_iota(jnp.int32, (WINDOW, 2 * WINDOW), 1)
    dist = WINDOW + qi - kj
    band = (dist >= 0) & (dist < WINDOW)
    return band & ((kj >= WINDOW) | (n > 0))


def _attn_probs(qk, bias, sink, mask):
    s = jnp.where(mask, qk * (A_HD ** -0.5) + bias, NEG_INF)
    m = jnp.maximum(jnp.max(s, axis=-1, keepdims=True), sink)
    p = jnp.exp(s - m)
    es = jnp.exp(sink - m)
    inv = 1.0 / (jnp.sum(p, axis=-1, keepdims=True) + es)
    return p * inv, es * inv


def _attn_fwd(proj, bias, sinks, bsz):
    t = proj.shape[0]
    s = t // bsz
    nb = s // WINDOW
    grp = AQ_HEADS // AKV_HEADS

    def body(q_ref, k_ref, v_ref, bias_ref, sink_ref, y_ref, kp_ref, vp_ref):
        kp_ref[0:WINDOW, :] = jnp.zeros((WINDOW, LANES), BF16)
        vp_ref[0:WINDOW, :] = jnp.zeros((WINDOW, LANES), BF16)
        kp_ref[WINDOW:, :] = k_ref[...].astype(BF16)
        vp_ref[WINDOW:, :] = v_ref[...].astype(BF16)

        def blk(n, carry):
            r0 = pl.multiple_of(n * WINDOW, WINDOW)
            mask = _attn_mask(n)
            kband = kp_ref[pl.ds(r0, 2 * WINDOW), :]
            vband = vp_ref[pl.ds(r0, 2 * WINDOW), :]
            qb = q_ref[pl.ds(r0, WINDOW), :].astype(BF16)
            heads = range(AQ_HEADS)
            hsl = lambda h: slice(h * A_HD, (h + 1) * A_HD)
            kbs = [kband[:, hsl(kv)] for kv in range(AKV_HEADS)]
            vbs = [vband[:, hsl(kv)] for kv in range(AKV_HEADS)]
            qks = [lax.dot_general(qb[:, hsl(h)], kbs[h // grp], _NT, preferred_element_type=F32) for h in heads]
            probs = [_attn_probs(qks[h], bias_ref[h], sink_ref[0, h], mask)[0] for h in heads]
            outs = [jnp.dot(probs[h].astype(BF16), vbs[h // grp], preferred_element_type=F32) for h in heads]
            y_ref[pl.ds(r0, WINDOW), :] = jnp.concatenate(outs, axis=1).astype(BF16)
            return carry

        lax.fori_loop(0, nb, blk, 0)

    return pl.pallas_call(
        body, name="attn_fwd", grid=(bsz,),
        in_specs=[_vspec((s, AQ), lambda b: (b, C_Q // AQ)), _vspec((s, AKV), lambda b: (b, C_K // AKV)),
                  _vspec((s, AKV), lambda b: (b, C_V // AKV)),
                  _vspec((AQ_HEADS, WINDOW, 2 * WINDOW), lambda b: (0, 0, 0)), pl.BlockSpec(memory_space=pltpu.SMEM)],
        out_specs=_vspec((s, AQ), lambda b: (b, 0)), out_shape=jax.ShapeDtypeStruct((t, AQ), BF16),
        scratch_shapes=[pltpu.VMEM((s + WINDOW, LANES), BF16), pltpu.VMEM((s + WINDOW, LANES), BF16)],
    )(proj, proj, proj, bias, sinks)


def _attn_bwd(dproj, proj, bias, sinks, dy, bsz, plug=None):
    t = proj.shape[0]
    s = t // bsz
    nb = s // WINDOW
    grp = AQ_HEADS // AKV_HEADS
    scale = A_HD ** -0.5

    def body(q_ref, k_ref, v_ref, bias_ref, sink_ref, dy_ref, dq_ref, dk_ref, dv_ref, dbias_ref, dsink_ref,
             kp_ref, vp_ref, dkp_ref, dvp_ref):
        b = pl.program_id(0)
        kp_ref[0:WINDOW, :] = jnp.zeros((WINDOW, LANES), BF16)
        vp_ref[0:WINDOW, :] = jnp.zeros((WINDOW, LANES), BF16)
        kp_ref[WINDOW:, :] = k_ref[...].astype(BF16)
        vp_ref[WINDOW:, :] = v_ref[...].astype(BF16)
        dkp_ref[...] = jnp.zeros_like(dkp_ref)
        dvp_ref[...] = jnp.zeros_like(dvp_ref)

        @pl.when(b == 0)
        def _():
            dbias_ref[...] = jnp.zeros_like(dbias_ref)
            dsink_ref[...] = jnp.zeros_like(dsink_ref)

        def blk(n, carry):
            r0 = pl.multiple_of(n * WINDOW, WINDOW)
            mask = _attn_mask(n)
            kband = kp_ref[pl.ds(r0, 2 * WINDOW), :]
            vband = vp_ref[pl.ds(r0, 2 * WINDOW), :]
            qb = q_ref[pl.ds(r0, WINDOW), :].astype(BF16)
            dyb = dy_ref[pl.ds(r0, WINDOW), :].astype(BF16)
            heads = range(AQ_HEADS)
            hsl = lambda h: slice(h * A_HD, (h + 1) * A_HD)
            kbs = [kband[:, hsl(kv)] for kv in range(AKV_HEADS)]
            vbs = [vband[:, hsl(kv)] for kv in range(AKV_HEADS)]
            qhs = [qb[:, hsl(h)] for h in heads]
            dyhs = [dyb[:, hsl(h)] for h in heads]
            qks = [lax.dot_general(qhs[h], kbs[h // grp], _NT, preferred_element_type=F32) for h in heads]
            dprobs = [lax.dot_general(dyhs[h], vbs[h // grp], _NT, preferred_element_type=F32) for h in heads]
            pbs, dsbs = [], []
            for h in heads:
                probs, psink = _attn_probs(qks[h], bias_ref[h], sink_ref[0, h], mask)
                rowdot = jnp.sum(probs * dprobs[h], axis=-1, keepdims=True)
                ds = probs * (dprobs[h] - rowdot)
                dbias_ref[h] += ds
                dsink_ref[h] += jnp.sum(-psink * rowdot, axis=0, keepdims=True) + jnp.zeros((1, LANES), F32)
                pbs.append(probs.astype(BF16))
                dsbs.append(ds.astype(BF16))
            dvhs = [lax.dot_general(pbs[h], dyhs[h], _TN, preferred_element_type=F32) for h in heads]
            dqs = [jnp.dot(dsbs[h], kbs[h // grp], preferred_element_type=F32) * scale for h in heads]
            dkhs = [lax.dot_general(dsbs[h], qhs[h], _TN, preferred_element_type=F32) * scale for h in heads]
            dks = [sum(dkhs[kv * grp + 1:(kv + 1) * grp], dkhs[kv * grp]) for kv in range(AKV_HEADS)]
            dvs = [sum(dvhs[kv * grp + 1:(kv + 1) * grp], dvhs[kv * grp]) for kv in range(AKV_HEADS)]
            dq_ref[pl.ds(r0, WINDOW), :] = jnp.concatenate(dqs, axis=1).astype(BF16)
            dkp_ref[pl.ds(r0, 2 * WINDOW), :] += jnp.concatenate(dks, axis=1)
            dvp_ref[pl.ds(r0, 2 * WINDOW), :] += jnp.concatenate(dvs, axis=1)
            return carry

        lax.fori_loop(0, nb, blk, 0)
        dk_ref[...] = dkp_ref[WINDOW:, :].astype(BF16)
        dv_ref[...] = dvp_ref[WINDOW:, :].astype(BF16)

    kvs = jax.ShapeDtypeStruct((t, AKV), BF16)
    outs, extra = _plugged_call(
        lambda buf_ref, *refs: body(*refs), plug, _grid_ends((bsz,)), (dproj, proj, proj, proj, bias, sinks, dy),
        name="attn_bwd", grid=(bsz,),
        in_specs=[_ANY, _vspec((s, AQ), lambda b: (b, C_Q // AQ)), _vspec((s, AKV), lambda b: (b, C_K // AKV)),
                  _vspec((s, AKV), lambda b: (b, C_V // AKV)),
                  _vspec((AQ_HEADS, WINDOW, 2 * WINDOW), lambda b: (0, 0, 0)), pl.BlockSpec(memory_space=pltpu.SMEM),
                  _vspec((s, AQ), lambda b: (b, 0))],
        out_specs=[_vspec((s, AQ), lambda b: (b, C_Q // AQ)), _vspec((s, AKV), lambda b: (b, 0)), _vspec((s, AKV), lambda b: (b, 0)),
                   _vspec((AQ_HEADS, WINDOW, 2 * WINDOW), lambda b: (0, 0, 0)), _vspec((AQ_HEADS, 1, LANES), lambda b: (0, 0, 0))],
        out_shape=[jax.ShapeDtypeStruct(dproj.shape, BF16), kvs, kvs,
                   jax.ShapeDtypeStruct((AQ_HEADS, WINDOW, 2 * WINDOW), F32), jax.ShapeDtypeStruct((AQ_HEADS, 1, LANES), F32)],
        scratch_shapes=[pltpu.VMEM((s + WINDOW, LANES), BF16), pltpu.VMEM((s + WINDOW, LANES), BF16),
                        pltpu.VMEM((s + WINDOW, LANES), F32), pltpu.VMEM((s + WINDOW, LANES), F32)],
        aliases={0: 0})
    return (*outs, extra)


def _dn_act_f(c, is_qk):
    a = jax.nn.silu(c)
    outs = []
    for h in range(DN_HEADS):
        ah = a[:, h * DN_HD:(h + 1) * DN_HD]
        nh = ah * lax.rsqrt(jnp.sum(ah * ah, axis=-1, keepdims=True) + L2_EPS)
        outs.append(jnp.where(is_qk, nh, ah))
    return jnp.concatenate(outs, axis=1)


def _dn_prep_fwd(proj, conv_w, bsz):
    t = proj.shape[0]
    s = t // bsz
    blk = _vspec((s, DN), lambda b, j: (b, j))
    wsp = _vspec((DN_CONV, DN), lambda b, j: (0, j))

    def body(x_ref, w_ref, o_ref):
        j = pl.program_id(1)
        o_ref[...] = _dn_act_f(_conv_fwd(x_ref[...], w_ref[...], DN_CONV), j < 2)

    return pl.pallas_call(body, name="dn_prep_fwd", grid=(bsz, 3), in_specs=[blk, wsp], out_specs=blk,
                          out_shape=jax.ShapeDtypeStruct((t, 3 * DN), F32))(proj, conv_w)


def _dn_prep_bwd(dproj, proj, conv_w, dqkvn, bsz):
    t = proj.shape[0]
    s = t // bsz
    blk = _vspec((s, DN), lambda j, b: (b, j))
    wsp = _vspec((DN_CONV, DN), lambda j, b: (0, j))

    def body(buf_ref, x_ref, w_ref, d_ref, dx_ref, dw_ref):
        del buf_ref
        j, b = pl.program_id(0), pl.program_id(1)
        x, w = x_ref[...], w_ref[...]
        c = _conv_fwd(x, w, DN_CONV)
        _, vjp = jax.vjp(lambda cc: _dn_act_f(cc, j < 2), c)
        (dc,) = vjp(d_ref[0])
        dx, dw = _conv_bwd(x, w, dc, DN_CONV)
        dx_ref[...] = dx.astype(BF16)

        @pl.when(b == 0)
        def _():
            dw_ref[...] = jnp.zeros_like(dw_ref)

        dw_ref[...] += dw

    return pl.pallas_call(
        body, name="dn_prep_bwd", grid=(3, bsz),
        in_specs=[_ANY, blk, wsp, _vspec((1, s, DN), lambda j, b: (j, b, 0))], out_specs=[blk, wsp],
        out_shape=[jax.ShapeDtypeStruct(dproj.shape, BF16), jax.ShapeDtypeStruct((DN_CONV, 3 * DN), F32)],
        input_output_aliases={0: 0},
    )(dproj, proj, conv_w, dqkvn)


def _bg_f(x, alog, dt):
    lane = lax.broadcasted_iota(jnp.int32, x.shape, 1)
    beta = jax.nn.sigmoid(x)
    g = -jnp.exp(alog) * jax.nn.softplus(x + dt)
    return jnp.where(lane < DN_HEADS, beta, jnp.where(lane < 2 * DN_HEADS, g, 0.0))


def _bg_fwd(proj, alog, dt, bsz):
    t = proj.shape[0]
    s = t // bsz
    vec = _vspec((1, LANES), lambda b: (0, 0))

    def body(x_ref, a_ref, d_ref, o_ref):
        o_ref[...] = _bg_f(x_ref[...], a_ref[...], d_ref[...])

    return pl.pallas_call(body, name="bg_fwd", grid=(bsz,), in_specs=[_vspec((s, LANES), lambda b: (b, C_BD // LANES)), vec, vec],
                          out_specs=_vspec((s, LANES), lambda b: (b, 0)), out_shape=jax.ShapeDtypeStruct((t, LANES), F32))(proj, alog, dt)


def _bg_bwd(dproj, proj, alog, dt, dbg4, dk, dv, bsz):
    t = proj.shape[0]
    s = t // bsz
    vec = _vspec((1, LANES), lambda b: (0, 0))
    kv = _vspec((s, AKV), lambda b: (b, 0))
    tail = 3 * LANES

    def body(buf_ref, x_ref, a_ref, d_ref, g4_ref, dk_ref, dv_ref, dx_ref, da_ref, dd_ref):
        del buf_ref
        b = pl.program_id(0)
        lane = lax.broadcasted_iota(jnp.int32, (s, LANES), 1)
        dbg = jnp.zeros((s, LANES), F32)
        for h in range(DN_HEADS):
            gh = g4_ref[:, h * DN_HD:(h + 1) * DN_HD]
            dbg = jnp.where(lane == h, gh[:, 0:1], dbg)
            dbg = jnp.where(lane == DN_HEADS + h, gh[:, 1:2], dbg)
        _, vjp = jax.vjp(_bg_f, x_ref[...], a_ref[...], d_ref[...])
        dx, da, dd = vjp(dbg)
        dx_ref[...] = jnp.concatenate([dk_ref[...], dv_ref[...], dx.astype(BF16)], axis=1)

        @pl.when(b == 0)
        def _():
            da_ref[...] = jnp.zeros_like(da_ref)
            dd_ref[...] = jnp.zeros_like(dd_ref)

        da_ref[...] += da
        dd_ref[...] += dd

    return pl.pallas_call(
        body, name="bg_bwd", grid=(bsz,),
        in_specs=[_ANY, _vspec((s, LANES), lambda b: (b, C_BD // LANES)), vec, vec, _vspec((s, DN), lambda b: (b, 0)), kv, kv],
        out_specs=[_vspec((s, tail), lambda b: (b, C_K // tail)), vec, vec],
        out_shape=[jax.ShapeDtypeStruct(dproj.shape, BF16), jax.ShapeDtypeStruct((1, LANES), F32), jax.ShapeDtypeStruct((1, LANES), F32)],
        input_output_aliases={0: 0},
    )(dproj, proj, alog, dt, dbg4, dk, dv)


def _dn_out_f(o, z, w):
    outs = []
    for h in range(DN_HEADS):
        sl = slice(h * DN_HD, (h + 1) * DN_HD)
        outs.append(_rms(o[:, sl], w) * jax.nn.silu(z[:, sl]))
    return jnp.concatenate(outs, axis=1)


def _dn_out_fwd(o, proj, w, ts=512):
    t = o.shape[0]
    blk = _vspec((ts, DN), lambda i: (i, 0))
    zsp = _vspec((ts, DN), lambda i: (i, C_DZ // DN))
    vec = _vspec((1, DN_HD), lambda i: (0, 0))

    def body(o_ref, z_ref, w_ref, y_ref):
        y_ref[...] = _dn_out_f(o_ref[...], z_ref[...], w_ref[...]).astype(BF16)

    return pl.pallas_call(body, name="dn_out_fwd", grid=(t // ts,), in_specs=[blk, zsp, vec], out_specs=blk,
                          out_shape=jax.ShapeDtypeStruct((t, DN), BF16))(o, proj, w)


def _dn_out_bwd(dproj, o, proj, w, dy, ts=512):
    t = o.shape[0]
    blk = _vspec((ts, DN), lambda i: (i, 0))
    zsp = _vspec((ts, DN), lambda i: (i, C_DZ // DN))
    vec = _vspec((1, DN_HD), lambda i: (0, 0))

    def body(buf_ref, o_ref, z_ref, w_ref, dy_ref, do_ref, dz_ref, dw_ref):
        del buf_ref
        i = pl.program_id(0)
        _, vjp = jax.vjp(_dn_out_f, o_ref[...], z_ref[...], w_ref[...])
        do, dz, dw = vjp(dy_ref[...])
        do_ref[...] = do
        dz_ref[...] = dz.astype(BF16)

        @pl.when(i == 0)
        def _():
            dw_ref[...] = jnp.zeros_like(dw_ref)

        dw_ref[...] += dw

    return pl.pallas_call(
        body, name="dn_out_bwd", grid=(t // ts,), in_specs=[_ANY, blk, zsp, vec, blk], out_specs=[blk, zsp, vec],
        out_shape=[jax.ShapeDtypeStruct((t, DN), F32), jax.ShapeDtypeStruct(dproj.shape, BF16), jax.ShapeDtypeStruct((1, DN_HD), F32)],
        input_output_aliases={0: 1},
    )(dproj, o, proj, w, dy)


_C = DN_CHUNK


def _dot(a, b, dims):
    return lax.dot_general(a.astype(BF16), b.astype(BF16), dims, preferred_element_type=F32)


def _split(a):
    hi = a.astype(BF16)
    return hi, (a - hi.astype(F32)).astype(BF16)


def _dot3(a, b, dims):
    (ah, al), (bh, bl) = (a if isinstance(a, tuple) else _split(a)), (b if isinstance(b, tuple) else _split(b))
    mm = lambda x, y: lax.dot_general(x, y, dims, preferred_element_type=F32)
    return mm(ah, bh) + (mm(ah, bl) + mm(al, bh))


_NN = (((1,), (0,)), ((), ()))
_NT = (((1,), (1,)), ((), ()))
_TN = (((0,), (0,)), ((), ()))


_SUB = 8


def _tri_inverses(ls, lts):
    ri8 = lax.broadcasted_iota(jnp.int32, (_SUB, _C), 0)
    ci8 = lax.broadcasted_iota(jnp.int32, (_SUB, _C), 1)
    nblk = _C // _SUB
    ts = []
    for lt in lts:
        blocks = [jnp.where(ci8 == ri8 + _SUB * b, 1.0, 0.0).astype(F32) for b in range(nblk)]
        for r in range(1, _SUB):
            for b in range(nblk):
                coef = lt[_SUB * b:_SUB * (b + 1), _SUB * b + r:_SUB * b + r + 1]
                row = jnp.sum(coef * blocks[b], axis=0, keepdims=True)
                blocks[b] = jnp.where(ri8 == r, blocks[b] - row, blocks[b])
        ts.append(jnp.concatenate(blocks, axis=0))
    ri = lax.broadcasted_iota(jnp.int32, (_C, _C), 0)
    ci = lax.broadcasted_iota(jnp.int32, (_C, _C), 1)
    s = _SUB
    while s < _C:
        shift = s.bit_length()
        quad = ((ri >> shift) == (ci >> shift)) & ((ri & s) != 0) & ((ci & s) == 0)
        offs = [jnp.where(quad, l, 0.0) for l in ls]
        tsp = [_split(t) for t in ts]
        left = [_dot3(tp, off, _NN) for tp, off in zip(tsp, offs)]
        ts = [t - _dot3(lo, tp, _NN) for t, lo, tp in zip(ts, left, tsp)]
        s *= 2
    return ts


_SEG = 512
_HEADS = tuple(range(DN_HEADS))


def _hsl(hh):
    return slice(hh * DN_HD, (hh + 1) * DN_HD)


def _chunk_specs(bsz, nseg, reverse):
    seg = (lambda i: nseg - 1 - i) if reverse else (lambda i: i)
    ncs = _SEG // _C
    col = lambda off: _vspec((bsz, _SEG, DN), lambda i: (0, seg(i), off))
    return (col, _vspec((bsz, _SEG, LANES), lambda i: (0, seg(i), 0)),
            _vspec((bsz, DN_HEADS, ncs, _C), lambda i: (0, 0, seg(i), 0)),
            _vspec((bsz, DN_HEADS, ncs, DN_HD, DN_HD), lambda i: (0, 0, seg(i), 0, 0)),
            _vspec((bsz, DN_HEADS, ncs, _C, _C), lambda i: (0, 0, seg(i), 0, 0)))


def _chunk_pre(q_ref, k_ref, v_ref, bg_ref, gr_ref, c, bb, hh):
    r0 = pl.multiple_of(c * _C, _C)
    ri = lax.broadcasted_iota(jnp.int32, (_C, _C), 0)
    ci = lax.broadcasted_iota(jnp.int32, (_C, _C), 1)
    q = q_ref[bb, pl.ds(r0, _C), _hsl(hh)] * (DN_HD ** -0.5)
    k = k_ref[bb, pl.ds(r0, _C), _hsl(hh)]
    v = v_ref[bb, pl.ds(r0, _C), _hsl(hh)]
    bgc = bg_ref[bb, pl.ds(r0, _C), :]
    beta = bgc[:, hh:hh + 1]
    g_col = bgc[:, DN_HEADS + hh:DN_HEADS + hh + 1]
    g_row = gr_ref[bb, hh, pl.ds(c, 1), :]
    gc_col = jnp.sum(jnp.where(ri >= ci, g_row, 0.0), axis=1, keepdims=True)
    gc_row = jnp.sum(jnp.where(ri <= ci, g_col, 0.0), axis=0, keepdims=True)
    gc_last = jnp.sum(g_col, axis=0, keepdims=True)
    diff = gc_col - gc_row
    decay = jnp.where(ri >= ci, jnp.exp(jnp.where(ri >= ci, diff, 0.0)), 0.0)
    diff_t = gc_row - gc_col
    decay_t = jnp.where(ri <= ci, jnp.exp(jnp.where(ri <= ci, diff_t, 0.0)), 0.0)
    eg = jnp.exp(gc_col)
    et = jnp.exp(gc_last - gc_col)
    gl = jnp.exp(gc_last)
    kb = k * beta
    vb = v * beta
    return dict(r0=r0, bb=bb, hh=hh, q=q, k=k, v=v, beta=beta, decay=decay, decay_t=decay_t, eg=eg, et=et, gl=gl, kb=kb, vb=vb,
                ri=ri, ci=ci)


def _chunk_solve(ms, tms=None):
    for m in ms:
        m["kk_t"] = _dot(m["k"], m["kb"], _NT)
        m["qk"] = _dot(m["q"], m["k"], _NT)
        m["kk"] = _dot(m["kb"], m["k"], _NT)
        if tms is not None:
            m["qk_t"] = _dot(m["k"], m["q"], _NT)
    if tms is None:
        tms = _tri_inverses([jnp.where(m["ri"] > m["ci"], m["kk"] * m["decay"], 0.0) for m in ms],
                            [jnp.where(m["ri"] < m["ci"], m["kk_t"] * m["decay_t"], 0.0) for m in ms])
    for m, tm in zip(ms, tms):
        m["tm_f32"] = tm
    for m in ms:
        rhs = jnp.concatenate([m["vb"], m["kb"] * m["eg"]], axis=1)
        m["tm"] = _split(m["tm_f32"])
        m["sol"] = _dot3(m["tm"], rhs, _NN)
        m["intra"] = jnp.where(m["ri"] >= m["ci"], m["qk"] * m["decay"], 0.0)


def _dn_chunk_fwd(qkvn, bg, g_rows, bsz, plug=None):
    t = qkvn.shape[0]
    s = t // bsz
    nc, nseg = s // _C, s // _SEG
    pairs = [(bb, hh) for bb in range(bsz) for hh in _HEADS]

    def body(q_ref, k_ref, v_ref, bg_ref, gr_ref, o_ref, st_ref, tm_ref, s_ref):
        @pl.when(pl.program_id(0) == 0)
        def _():
            s_ref[...] = jnp.zeros_like(s_ref)

        def chunk(c, carry):
            ms = [_chunk_pre(q_ref, k_ref, v_ref, bg_ref, gr_ref, c, bb, hh) for bb, hh in pairs]
            _chunk_solve(ms)
            sts = [s_ref[i] for i in range(len(pairs))]
            for m, st in zip(ms, sts):
                st_ref[m["bb"], m["hh"], c] = st
                tm_ref[m["bb"], m["hh"], c] = m["tm_f32"]
            ws = [_dot(m["sol"][:, DN_HD:], st, _NN) for m, st in zip(ms, sts)]
            qs = [_dot(m["q"] * m["eg"], st, _NN) for m, st in zip(ms, sts)]
            v_new = [m["sol"][:, :DN_HD] - a for m, a in zip(ms, ws)]
            iv = [_dot(m["intra"], vn, _NN) for m, vn in zip(ms, v_new)]
            upd = [_dot(m["k"] * m["et"], vn, _TN) for m, vn in zip(ms, v_new)]
            for i, (bb, hh) in enumerate(pairs):
                s_ref[i] = sts[i] * ms[i]["gl"] + upd[i]
                o_ref[bb, pl.ds(ms[i]["r0"], _C), _hsl(hh)] = qs[i] + iv[i]
            return carry

        lax.fori_loop(0, _SEG // _C, chunk, 0)

    col, bgs, grs, sts_spec, tms_spec = _chunk_specs(bsz, nseg, False)
    q3, bg3 = qkvn.reshape(bsz, s, 3 * DN), bg.reshape(bsz, s, LANES)
    (o, states, tms), extra = _plugged_call(
        body, plug, _grid_ends((nseg,)), (q3, q3, q3, bg3, g_rows), name="dn_chunk_fwd", grid=(nseg,),
        in_specs=[col(0), col(1), col(2), bgs, grs], out_specs=[col(0), sts_spec, tms_spec],
        out_shape=[jax.ShapeDtypeStruct((bsz, s, DN), F32), jax.ShapeDtypeStruct((bsz, DN_HEADS, nc, DN_HD, DN_HD), F32),
                   jax.ShapeDtypeStruct((bsz, DN_HEADS, nc, _C, _C), F32)],
        scratch_shapes=[pltpu.VMEM((bsz * DN_HEADS, DN_HD, DN_HD), F32)])
    return o.reshape(t, DN), (states, tms), extra


def _dn_chunk_bwd(qkvn, bg, g_rows, states, do, bsz, plug=None):
    t = qkvn.shape[0]
    s = t // bsz
    nc, nseg = s // _C, s // _SEG
    pairs = [(bb, hh) for bb in range(bsz) for hh in _HEADS]

    def body(q_ref, k_ref, v_ref, bg_ref, gr_ref, st_ref, tm_ref, do_ref, dqkv_ref, dbg_ref, ds_ref):
        @pl.when(pl.program_id(0) == 0)
        def _():
            ds_ref[...] = jnp.zeros_like(ds_ref)

        def chunk(cc, carry):
            c = _SEG // _C - 1 - cc
            ms = [_chunk_pre(q_ref, k_ref, v_ref, bg_ref, gr_ref, c, bb, hh) for bb, hh in pairs]
            _chunk_solve(ms, [tm_ref[bb, hh, c] for bb, hh in pairs])
            ri, ci = ms[0]["ri"], ms[0]["ci"]
            for i, m in enumerate(ms):
                m["st"] = st_ref[m["bb"], m["hh"], c]
                m["ds_out"] = ds_ref[i]
                m["do"] = do_ref[m["bb"], pl.ds(m["r0"], _C), _hsl(m["hh"])]
                m["w"] = m["sol"][:, DN_HD:]
            for m in ms:
                m["v_new"] = m["sol"][:, :DN_HD] - _dot(m["w"], m["st"], _NN)
            for m in ms:
                m["q_dec"], m["k_tail"] = m["q"] * m["eg"], m["k"] * m["et"]
                m["dk_tail"] = _dot(m["v_new"], m["ds_out"], _NT)
                m["dv_new"] = _dot(m["k_tail"], m["ds_out"], _NN) + _dot(m["intra"], m["do"], _TN)
                m["dq_dec"] = _dot(m["do"], m["st"], _NT)
                m["ds_in"] = m["ds_out"] * m["gl"] + _dot(m["q_dec"], m["do"], _TN)
                m["dintra"] = jnp.where(ri >= ci, _dot(m["do"], m["v_new"], _NT), 0.0)
                m["dintra_t"] = jnp.where(ri <= ci, _dot(m["v_new"], m["do"], _NT), 0.0)
            for m in ms:
                m["dw"] = -_dot(m["dv_new"], m["st"], _NT)
                m["ds_in"] = m["ds_in"] - _dot(m["w"], m["dv_new"], _TN)
            for m in ms:
                dsol = jnp.concatenate([m["dv_new"], m["dw"]], axis=1)
                m["drhs"] = _dot3(m["tm"], dsol, _TN)
            for m in ms:
                m["dl"] = jnp.where(ri > ci, -_dot(m["drhs"], m["sol"], _NT), 0.0)
                m["dl_t"] = jnp.where(ri < ci, -_dot(m["sol"], m["drhs"], _NT), 0.0)
            for m in ms:
                m["dkb2"] = _dot(m["dl"] * m["decay"], m["k"], _NN)
                m["dk"] = _dot(m["dl_t"] * m["decay_t"], m["kb"], _NN) + _dot(m["dintra_t"] * m["decay_t"], m["q"], _NN)
                m["dq"] = _dot(m["dintra"] * m["decay"], m["k"], _NN)
            for m in ms:
                _chunk_bwd_finish(m)
            for m in ms:
                ones_ge = jnp.where(ri <= ci, 1.0, 0.0).astype(BF16)
                gh, gl_ = _split(m["dgc"] + jnp.zeros((_C, LANES), F32))
                m["dg_b"] = jnp.dot(ones_ge, gh, preferred_element_type=F32) + jnp.dot(ones_ge, gl_, preferred_element_type=F32)
            lane = lax.broadcasted_iota(jnp.int32, (_C, LANES), 1)
            for i, m in enumerate(ms):
                bb, hh, rows = m["bb"], m["hh"], pl.ds(m["r0"], _C)
                dqkv_ref[0, bb, rows, _hsl(hh)] = m["dq"] * (DN_HD ** -0.5)
                dqkv_ref[1, bb, rows, _hsl(hh)] = m["dk"]
                dqkv_ref[2, bb, rows, _hsl(hh)] = m["dv"]
                dbg_ref[bb, rows, _hsl(hh)] = jnp.where(lane == 0, m["dbeta"], jnp.where(lane == 1, m["dg_b"], 0.0))
                ds_ref[i] = m["ds_in"]
            return carry

        lax.fori_loop(0, _SEG // _C, chunk, 0)

    col, bgs, grs, sts_spec, tms_spec = _chunk_specs(bsz, nseg, True)
    q3, bg3, do3 = qkvn.reshape(bsz, s, 3 * DN), bg.reshape(bsz, s, LANES), do.reshape(bsz, s, DN)
    (dqkv, dbg), extra = _plugged_call(
        body, plug, _grid_ends((nseg,)), (q3, q3, q3, bg3, g_rows, *states, do3), name="dn_chunk_bwd", grid=(nseg,),
        in_specs=[col(0), col(1), col(2), bgs, grs, sts_spec, tms_spec, col(0)],
        out_specs=[_vspec((3, bsz, _SEG, DN), lambda i: (0, 0, nseg - 1 - i, 0)), col(0)],
        out_shape=[jax.ShapeDtypeStruct((3, bsz, s, DN), F32), jax.ShapeDtypeStruct((bsz, s, DN), F32)],
        scratch_shapes=[pltpu.VMEM((bsz * DN_HEADS, DN_HD, DN_HD), F32)])
    return dqkv.reshape(3, t, DN), dbg.reshape(t, DN), extra


def _chunk_bwd_finish(m):
    q, k, v, beta, decay, decay_t = m["q"], m["k"], m["v"], m["beta"], m["decay"], m["decay_t"]
    eg, et, gl, kb, dl, dl_t, dintra, dintra_t = m["eg"], m["et"], m["gl"], m["kb"], m["dl"], m["dl_t"], m["dintra"], m["dintra_t"]
    dq_dec, dk_tail, dq, dk = m["dq_dec"], m["dk_tail"], m["dq"], m["dk"]
    dgl = jnp.sum(jnp.sum(m["ds_out"] * m["st"], axis=1, keepdims=True), axis=0, keepdims=True)
    dvb, dkbeg = m["drhs"][:, :DN_HD], m["drhs"][:, DN_HD:]
    dkb = dkbeg * eg + m["dkb2"]
    deg = jnp.sum(dkbeg * kb, axis=1, keepdims=True)
    em = (dl * m["kk"] + dintra * m["qk"]) * decay
    em_t = (dl_t * m["kk_t"] + dintra_t * m["qk_t"]) * decay_t
    dgc = jnp.sum(em, axis=1, keepdims=True) - jnp.sum(em_t, axis=1, keepdims=True)
    dq = dq + dq_dec * eg
    deg = deg + jnp.sum(dq_dec * q, axis=1, keepdims=True)
    dk = dk + dk_tail * et
    det = jnp.sum(dk_tail * k, axis=1, keepdims=True)
    dgc = dgc + deg * eg - det * et
    dgc_last = jnp.sum(det * et, axis=0, keepdims=True) + dgl * gl
    rcol = lax.broadcasted_iota(jnp.int32, (_C, 1), 0)
    m["dgc"] = dgc + jnp.where(rcol == _C - 1, dgc_last, 0.0)
    m["dq"] = dq
    m["dk"] = dk + dkb * beta
    m["dbeta"] = jnp.sum(dkb * k, axis=1, keepdims=True) + jnp.sum(dvb * v, axis=1, keepdims=True)
    m["dv"] = dvb * beta


def _mod_fwd(c_all, ada_w_loc, ada_b_loc):
    n, cols = c_all.shape[0], ada_w_loc.shape[1]

    def body(c_ref, w_ref, b_ref, o_ref):
        o_ref[...] = _dot(jax.nn.silu(c_ref[...]), w_ref[...], _NN) + b_ref[...]

    return pl.pallas_call(body, name="mod_fwd", out_shape=jax.ShapeDtypeStruct((n, cols), F32))(c_all, ada_w_loc, ada_b_loc)


def _ada_grad(c_all, dmod_loc, dmod_all):
    d, cols = c_all.shape[1], dmod_loc.shape[1]

    def body(c_ref, dl_ref, da_ref, gw_ref, gb_ref):
        gw_ref[...] = _dot(jax.nn.silu(c_ref[...]), dl_ref[...], _TN)
        gb_ref[...] = jnp.sum(da_ref[...], axis=0, keepdims=True)

    return pl.pallas_call(body, name="ada_grad", out_shape=[jax.ShapeDtypeStruct((d, cols), F32),
                                                           jax.ShapeDtypeStruct((1, dmod_all.shape[1]), F32)])(c_all, dmod_loc, dmod_all)


ELEMENTWISE_BLOCK_BYTES = 3 * 2 ** 19


def _row_tile(r, c=1024):
    fits = [tr for tr in range(16, r + 1, 16) if tr * c * 4 <= ELEMENTWISE_BLOCK_BYTES]
    if not fits:
        return r
    whole = [tr for tr in fits if r % tr == 0]
    return whole[-1] if whole else fits[-1]


def _adamw(w, m, v, grads, name, plug=None):
    r, rest = w.shape[0], w.shape[1:]
    c = math.prod(rest)
    tr = _row_tile(r, c)
    blk = _vspec((tr,) + rest, lambda i: (i,) + (0,) * len(rest))
    n = len(grads)

    def body(*refs):
        w_ref, m_ref, v_ref = refs[:3]
        g_ref, d_ref, mo_ref, vo_ref = refs[3 + n:]
        g = refs[3][...]
        for p in refs[4:3 + n]:
            g = g + p[...]
        g_ref[...] = g
        d_ref[...], mo_ref[...], vo_ref[...] = _adamw_math(w_ref[...], m_ref[...], v_ref[...], g)

    o = jax.ShapeDtypeStruct(w.shape, F32)
    grid = (pl.cdiv(r, tr),)
    res, extra = _plugged_call(body, plug, _grid_ends(grid), (w, m, v, *grads), name=name, grid=grid, in_specs=[blk] * (3 + n),
                               out_specs=[blk] * 4, out_shape=[o] * 4)
    return res if plug is None else (res, extra)


def _adamw_halves(w, m, v, own, other, ic, name):
    if w.ndim == 3:
        r, _, c = w.shape
        tr = _row_tile(r, c)
        blk = _vspec((tr, 1, c // 2), lambda i, j: (i, 0, j))
        half = _vspec((tr, 1, c // 2), lambda i, j: (i, 0, 0))
        grid = (pl.cdiv(r, tr), 2)
    else:
        r, c = w.shape
        tr = _row_tile(r // 2, c)
        per = r // 2 // tr
        blk = _vspec((tr, c), lambda i, j: (j * per + i, 0))
        half = _vspec((tr, c), lambda i, j: (i, 0))
        grid = (per, 2)

    def body(ic_ref, w_ref, m_ref, v_ref, own_ref, other_ref, g_ref, d_ref, mo_ref, vo_ref):
        g = jnp.where(pl.program_id(1) == ic_ref[0], own_ref[...], other_ref[...])
        g_ref[...] = g
        d_ref[...], mo_ref[...], vo_ref[...] = _adamw_math(w_ref[...], m_ref[...], v_ref[...], g)

    o = jax.ShapeDtypeStruct(w.shape, F32)
    return pl.pallas_call(body, name=name, grid=grid,
                          in_specs=[pl.BlockSpec(memory_space=pltpu.SMEM), blk, blk, blk, half, half], out_specs=[blk] * 4,
                          out_shape=[o] * 4)(ic, w, m, v, own, other)


def _adamw_math(w, m, v, g):
    m_new = ADAM_B1 * m + (1.0 - ADAM_B1) * g
    v_new = ADAM_B2 * v + (1.0 - ADAM_B2) * jnp.square(g)
    m_hat = m_new / (1.0 - ADAM_B1 ** ADAM_STEP)
    v_hat = v_new / (1.0 - ADAM_B2 ** ADAM_STEP)
    return -ADAM_LR * (m_hat / (jnp.sqrt(v_hat) + ADAM_EPS) + ADAM_WD * w), m_new, v_new


def _adamw_small(ws, ms, vs, gs, name):
    n = len(ws)

    def body(*refs):
        for i in range(n):
            w_ref, m_ref, v_ref, g_ref = (refs[j * n + i] for j in range(4))
            go_ref, d_ref, mo_ref, vo_ref = refs[4 * n + 4 * i:4 * n + 4 * i + 4]
            g = g_ref[...]
            go_ref[...] = g
            d_ref[...], mo_ref[...], vo_ref[...] = _adamw_math(w_ref[...], m_ref[...], v_ref[...], g)

    res = pl.pallas_call(body, name=name, out_shape=[jax.ShapeDtypeStruct(a.shape, F32) for a in ws for _ in range(4)])(
        *ws, *ms, *vs, *gs)
    return [res[4 * i:4 * i + 4] for i in range(n)]


def _sum_lead(x, name, rows_apart=False):
    p, r, c = x.shape
    tr = _row_tile(r, c)
    mid = (1,) if rows_apart else ()

    def body(x_ref, o_ref):
        acc = x_ref[0].astype(F32)
        for i in range(1, p):
            acc = acc + x_ref[i].astype(F32)
        o_ref[...] = acc.reshape(o_ref.shape)

    return pl.pallas_call(body, name=name, grid=(pl.cdiv(r, tr),), in_specs=[_vspec((p, tr, c), lambda i: (0, i, 0))],
                          out_specs=_vspec((tr,) + mid + (c,), lambda i: (i,) + (0,) * (1 + len(mid))),
                          out_shape=jax.ShapeDtypeStruct((r,) + mid + (c,), F32))(x)


def _allgather8(x_shard, name, plug=None):
    m_per, n = x_shard.shape

    def body(x_ref, out_ref, send_sems, recv_sems, local_sem):
        x, y, c = lax.axis_index("x"), lax.axis_index("y"), lax.axis_index("c")
        me, sibling = (x, y, c), (x, y, 1 - c)
        chips = [(1 - x, y), (x, 1 - y), (1 - x, 1 - y)]

        def rows(px, py, pc):
            return out_ref.at[pl.ds((4 * px + 2 * py + pc) * m_per, m_per), :]

        def copy(k, block, to, src=None):
            return pltpu.make_async_remote_copy(
                src_ref=rows(*block) if src is None else src, dst_ref=rows(*block), send_sem=send_sems.at[k],
                recv_sem=recv_sems.at[k], device_id=to, device_id_type=MESH)

        mine = pltpu.make_async_copy(x_ref, rows(*me), local_sem)
        mine.start()
        first = [copy(0, me, sibling, src=x_ref)]
        first += [copy(1 + j, me, (*chip, c), src=x_ref) for j, chip in enumerate(chips)]
        for cp in first:
            cp.start()
        passed = [copy(4 + j, (*chip, c), sibling) for j, chip in enumerate(chips)]
        for j, chip in enumerate(chips):
            copy(1 + j, (*chip, c), me).wait_recv()
            passed[j].start()
        copy(0, sibling, me).wait_recv()
        for j, chip in enumerate(chips):
            copy(4 + j, (*chip, 1 - c), me).wait_recv()
        for cp in first + passed:
            cp.wait_send()
        mine.wait()

    grid = (1,)
    (out,), extra = _plugged_call(
        body, plug, _grid_ends(grid), (x_shard,), name=name, grid=grid, out_shape=[jax.ShapeDtypeStruct((8 * m_per, n), x_shard.dtype)],
        in_specs=[pl.BlockSpec(memory_space=pltpu.VMEM)], out_specs=[pl.BlockSpec(memory_space=pltpu.VMEM)],
        scratch_shapes=[pltpu.SemaphoreType.DMA((7,)), pltpu.SemaphoreType.DMA((7,)), pltpu.SemaphoreType.DMA])
    return out if plug is None else (out, extra)


_HBM = pl.BlockSpec(memory_space=pltpu.HBM)


def _mesh_place():
    x, y, c = lax.axis_index("x"), lax.axis_index("y"), lax.axis_index("c")
    return x, y, c, 2 * x + y, [(1 - x, y), (x, 1 - y), (1 - x, 1 - y)]


def _gather_plug(shards):
    n = len(shards)

    def half(ref, c, lead=None):
        r, cols = ref.shape[-2] // 2, ref.shape[-1] // 2
        if r % 16 == 0:
            rows = pl.ds(pl.multiple_of(c * r, 16), r)
            return ref.at[rows, :] if lead is None else ref.at[lead, rows, :]
        lanes = pl.ds(pl.multiple_of(c * cols, LANES), cols)
        return ref.at[:, lanes] if lead is None else ref.at[lead, :, lanes]

    def copies(ins, outs, send, recv):
        x, y, c, me, chips = _mesh_place()
        ici, fwd, fwd_in = [], [], []
        for i in range(n):
            for j, (px, py) in enumerate(chips):
                q = 2 * px + py
                ici.append((pltpu.make_async_remote_copy(
                    src_ref=half(ins[i], c), dst_ref=half(outs[i], c, me), send_sem=send.at[6 * i + j], recv_sem=recv.at[6 * i + j],
                    device_id=(px, py, c), device_id_type=MESH),
                    pltpu.make_async_remote_copy(
                    src_ref=half(ins[i], c), dst_ref=half(outs[i], c, q), send_sem=send.at[6 * i + j], recv_sem=recv.at[6 * i + j],
                    device_id=(px, py, c), device_id_type=MESH)))
                fwd.append(pltpu.make_async_remote_copy(
                    src_ref=half(outs[i], c, q), dst_ref=half(outs[i], c, q), send_sem=send.at[6 * i + 3 + j],
                    recv_sem=recv.at[6 * i + 3 + j], device_id=(x, y, 1 - c), device_id_type=MESH))
                fwd_in.append(pltpu.make_async_remote_copy(
                    src_ref=half(outs[i], 1 - c, q), dst_ref=half(outs[i], 1 - c, q), send_sem=send.at[6 * i + 3 + j],
                    recv_sem=recv.at[6 * i + 3 + j], device_id=(x, y, 1 - c), device_id_type=MESH))
        return ici, fwd, fwd_in, me

    def start(ins, outs, send, recv, loc):
        ici, _, _, me = copies(ins, outs, send, recv)
        for i in range(n):
            pltpu.make_async_copy(ins[i], outs[i].at[me], loc.at[i]).start()
        for out_cp, _ in ici:
            out_cp.start()

    def finish(ins, outs, send, recv, loc):
        ici, fwd, fwd_in, me = copies(ins, outs, send, recv)
        for (_, in_cp), f in zip(ici, fwd):
            in_cp.wait_recv()
            f.start()
        for f in fwd_in:
            f.wait_recv()
        for (out_cp, _), f in zip(ici, fwd):
            out_cp.wait_send()
            f.wait_send()
        for i in range(n):
            pltpu.make_async_copy(ins[i], outs[i].at[me], loc.at[i]).wait()

    return dict(ins=list(shards), out_shape=[jax.ShapeDtypeStruct((N_CHIPS,) + a.shape, a.dtype) for a in shards],
                scratch=[pltpu.SemaphoreType.DMA((6 * n,)), pltpu.SemaphoreType.DMA((6 * n,)), pltpu.SemaphoreType.DMA((n,))],
                start=start, finish=finish)


def _exchange_plug(pieces):
    n = len(pieces)

    def copies(ins, outs, send, recv):
        x, y, c, me, chips = _mesh_place()
        out_cps, in_cps = [], []
        for i in range(n):
            for j, (px, py) in enumerate(chips):
                q = 2 * px + py
                out_cps.append(pltpu.make_async_remote_copy(src_ref=ins[i].at[q], dst_ref=outs[i].at[me], send_sem=send.at[3 * i + j],
                                                            recv_sem=recv.at[3 * i + j], device_id=(px, py, c), device_id_type=MESH))
                in_cps.append(pltpu.make_async_remote_copy(src_ref=ins[i].at[me], dst_ref=outs[i].at[q], send_sem=send.at[3 * i + j],
                                                           recv_sem=recv.at[3 * i + j], device_id=(px, py, c), device_id_type=MESH))
        return out_cps, in_cps, me

    def start(ins, outs, send, recv, loc):
        out_cps, _, me = copies(ins, outs, send, recv)
        for i in range(n):
            pltpu.make_async_copy(ins[i].at[me], outs[i].at[me], loc.at[i]).start()
        for cp in out_cps:
            cp.start()

    def finish(ins, outs, send, recv, loc):
        out_cps, in_cps, me = copies(ins, outs, send, recv)
        for cp in in_cps:
            cp.wait_recv()
        for cp in out_cps:
            cp.wait_send()
        for i in range(n):
            pltpu.make_async_copy(ins[i].at[me], outs[i].at[me], loc.at[i]).wait()

    return dict(ins=list(pieces), out_shape=[jax.ShapeDtypeStruct(a.shape, a.dtype) for a in pieces],
                scratch=[pltpu.SemaphoreType.DMA((3 * n,)), pltpu.SemaphoreType.DMA((3 * n,)), pltpu.SemaphoreType.DMA((n,))],
                start=start, finish=finish)


def _plugged_call(body, plug, first_last, args, *, name, grid, in_specs, out_specs, out_shape, scratch_shapes=(), aliases=None):
    in_specs, out_specs, out_shape, scratch_shapes = list(in_specs), list(out_specs), list(out_shape), list(scratch_shapes)
    aliases = dict(aliases or {})
    if plug is None:
        return pl.pallas_call(body, name=name, grid=grid, in_specs=in_specs, out_specs=out_specs, out_shape=out_shape,
                              scratch_shapes=scratch_shapes, input_output_aliases=aliases)(*args), []
    n_in, n_out, n_sc = len(in_specs), len(out_specs), len(scratch_shapes)
    p_in, p_out = len(plug["ins"]), len(plug["out_shape"])

    def full(*refs):
        ins, refs = refs[:n_in], refs[n_in:]
        pins, refs = refs[:p_in], refs[p_in:]
        outs, refs = refs[:n_out], refs[n_out:]
        pouts, refs = refs[:p_out], refs[p_out:]
        scr, psems = refs[:n_sc], refs[n_sc:]
        first, last = first_last()

        @pl.when(first)
        def _():
            plug["start"](pins, pouts, *psems)

        body(*ins, *outs, *scr)

        @pl.when(last)
        def _():
            plug["finish"](pins, pouts, *psems)

    res = pl.pallas_call(full, name=name, grid=grid, in_specs=in_specs + [_HBM] * p_in, out_specs=out_specs + [_HBM] * p_out,
                         out_shape=out_shape + plug["out_shape"], scratch_shapes=scratch_shapes + plug["scratch"],
                         input_output_aliases=aliases)(*args, *plug["ins"])
    return res[:n_out], res[n_out:]


def _grid_ends(grid):
    def ends():
        first = last = None
        for ax, n in enumerate(grid):
            i = pl.program_id(ax)
            first = (i == 0) if first is None else first & (i == 0)
            last = (i == n - 1) if last is None else last & (i == n - 1)
        return first, last
    return ends


def _pair_presum(pieces, name):
    n, r, cols = pieces.shape
    by_cols = (cols // 2) % LANES == 0
    half_shape = (n, r, cols // 2) if by_cols else (n, r // 2, cols)

    def body(p_ref, o_ref, mine_ref, land_ref, send_sem, recv_sem, local_sem):
        x, y, c = lax.axis_index("x"), lax.axis_index("y"), lax.axis_index("c")

        def half(which):
            if by_cols:
                return p_ref.at[:, :, pl.ds(pl.multiple_of(which * (cols // 2), LANES), cols // 2)]
            return p_ref.at[:, pl.ds(pl.multiple_of(which * (r // 2), 16), r // 2), :]

        push = pltpu.make_async_remote_copy(src_ref=half(1 - c), dst_ref=land_ref, send_sem=send_sem, recv_sem=recv_sem,
                                            device_id=(x, y, 1 - c), device_id_type=MESH)
        own = pltpu.make_async_copy(half(c), mine_ref, local_sem)
        push.start()
        own.start()
        own.wait()
        push.wait_recv()
        for i in range(n):
            o_ref[i] = (mine_ref[i].astype(F32) + land_ref[i].astype(F32)).astype(BF16)
        push.wait_send()

    return pl.pallas_call(
        body, name=name, out_shape=jax.ShapeDtypeStruct(half_shape, BF16), in_specs=[_HBM],
        out_specs=pl.BlockSpec(memory_space=pltpu.VMEM),
        scratch_shapes=[pltpu.VMEM(half_shape, BF16), pltpu.VMEM(half_shape, BF16), pltpu.SemaphoreType.DMA,
                        pltpu.SemaphoreType.DMA, pltpu.SemaphoreType.DMA],
    )(pieces)


def _sibling_plug(arrs):
    n = len(arrs)

    def copies(ins, outs, send, recv):
        sibling = (lax.axis_index("x"), lax.axis_index("y"), 1 - lax.axis_index("c"))
        return [pltpu.make_async_remote_copy(src_ref=ins[i], dst_ref=outs[i], send_sem=send.at[i], recv_sem=recv.at[i],
                                             device_id=sibling, device_id_type=MESH) for i in range(n)]

    def start(ins, outs, send, recv):
        for cp in copies(ins, outs, send, recv):
            cp.start()

    def finish(ins, outs, send, recv):
        for cp in copies(ins, outs, send, recv):
            cp.wait()

    return dict(ins=list(arrs), out_shape=[jax.ShapeDtypeStruct(a.shape, a.dtype) for a in arrs],
                scratch=[pltpu.SemaphoreType.DMA((n,)), pltpu.SemaphoreType.DMA((n,))], start=start, finish=finish)


_SEGS = ((0, AQ, C_Q), (AQ, AKV, C_K), (AQ + AKV, AKV, C_V), (AQ + 2 * AKV, 3 * DN, C_DQKV), (2304, DN, C_DZ),
         (2816, 2 * DN_HEADS, C_BD), (2824, D_MODEL, C_GA), (3848, D_MODEL, C_GD))
SHARD_ROWS = IN_DIM // N_CHIPS


def _to_padded(w4):
    wt = w4.reshape(IN_DIM, w4.shape[2])
    parts = [wt[o:o + n] for o, n, _ in sorted(_SEGS, key=lambda sg: sg[2])]
    return jnp.concatenate(parts + [jnp.zeros((IN_PAD - IN_DIM, wt.shape[1]), wt.dtype)], axis=0)


def _from_padded(gt):
    return jnp.concatenate([gt[ps:ps + n] for _, n, ps in sorted(_SEGS)], axis=0).reshape(N_CHIPS, SHARD_ROWS, gt.shape[1])


def _lane_vec(a):
    return jnp.zeros((1, LANES), F32).at[0, DN_HEADS:2 * DN_HEADS].set(a)


_ROW_SHARDED = ("w_in", "w_out", "ffn_w_down")
_FFN = ("ffn_w_up", "ffn_w_down")
_LATE_MIXER = ("w_attn_branch", "w_dn_branch", "w_out")
_PRESUM = ("w_in", "ffn_w_up")


def _pieces(k, a):
    if a.ndim == 3:
        return a
    if k in _ROW_SHARDED:
        return a.reshape(N_CHIPS, a.shape[0] // N_CHIPS, a.shape[1]).astype(BF16)
    return jnp.transpose(a.reshape(a.shape[0], N_CHIPS, a.shape[1] // N_CHIPS), (1, 0, 2)).astype(BF16)


def _assemble(k, a):
    if k in _ROW_SHARDED:
        return a.reshape(-1, a.shape[2])
    return jnp.transpose(a, (1, 0, 2)).reshape(a.shape[1], -1)


def _device_step(x2, tgt2, mod, p, bsz, shards=None):
    d = D_MODEL
    on_mesh = shards is not None
    p = dict(p)
    sh1, sc1, g1, sh2, sc2, g2 = [mod[:, i * d:(i + 1) * d].reshape(bsz, 1, d) for i in range(N_MOD)]
    alog_v, dt_v = _lane_vec(p["dn_a_log"]), _lane_vec(p["dn_dt_bias"])
    sinks = p["attn_sinks"].reshape(1, AQ_HEADS)
    if on_mesh:
        u1, got = _pre_fwd(x2, p["norm_mix_pre"], sc1, sh1, "pre1_fwd", plug=_gather_plug(shards["w_in_last"]))
        p["w_in"] = _to_padded(jnp.concatenate(list(shards["w_in_gathered"]) + list(got), axis=1))
    else:
        u1 = _pre_fwd(x2, p["norm_mix_pre"], sc1, sh1, "pre1_fwd")
    if on_mesh:
        proj, got = _mm(u1, p["w_in"], "nt", F32, "mm_proj", _gather_plug(shards["late_mixer"]))
        p.update({k: _assemble(k, a) for k, a in zip(_LATE_MIXER, got)})
    else:
        proj = _mm(u1, p["w_in"], "nt", F32, "mm_proj")
    bias = _bias_build(p["rel_bias"])
    y_attn = _attn_fwd(proj, bias, sinks, bsz)
    qkvn = _dn_prep_fwd(proj, p["dn_conv_w"], bsz)
    bg = _bg_fwd(proj, alog_v, dt_v, bsz)
    nc = x2.shape[0] // bsz // DN_CHUNK
    g_rows = jnp.transpose(bg[:, DN_HEADS:2 * DN_HEADS].reshape(bsz, nc, DN_CHUNK, DN_HEADS), (0, 3, 1, 2))
    o, states, got = _dn_chunk_fwd(qkvn, bg, g_rows, bsz, _gather_plug(shards["ffn"][:1]) if on_mesh else None)
    for k, a in zip(_FFN[:1], got):
        p[k] = _assemble(k, a)
    y_dn = _dn_out_fwd(o, proj, p["dn_norm_w"])
    ya = _mm(y_attn, p["w_attn_branch"], "nn", BF16, "mm_ya")
    yd = _mm(y_dn, p["w_dn_branch"], "nn", BF16, "mm_yd")
    merged = _merge_fwd(proj, ya, yd)
    y1 = _mm(merged, p["w_out"], "nn", F32, "mm_y1")
    h1, u2 = _post_pre_fwd(x2, y1, p["norm_mix_post"], g1, p["norm_ffn_pre"], sc2, sh2, "post1_pre2_fwd")
    if on_mesh:
        up, got = _mm(u2, p["ffn_w_up"], "nn", BF16, "mm_up", _gather_plug(shards["ffn"][1:]))
        p["ffn_w_down"] = _assemble("ffn_w_down", got[0])
    else:
        up = _mm(u2, p["ffn_w_up"], "nn", BF16, "mm_up")
    act, conv_g, conv_v = _ffn_act_fwd(up, p["ffn_conv_w"], bsz)
    y2 = _mm(act, p["ffn_w_down"], "nn", F32, "mm_y2")
    dh2, dy2, g_ffn_post, dg2, sq = _post_loss_bwd(h1, y2, p["norm_ffn_post"], g2, tgt2, "post2_loss_bwd")
    g = {}
    g["norm_ffn_post"] = g_ffn_post
    dact = _mm(dy2, p["ffn_w_down"], "nt", F32, "mm_dact")
    g["ffn_w_down"] = _mm(act, dy2, "tn", BF16, "mm_dwdown")
    dupg, dupv, dcwg, dcwv, got_down = _ffn_act_bwd(
        up, p["ffn_conv_w"], (conv_g, conv_v), dact, bsz,
        _exchange_plug([_pieces("ffn_w_down", g["ffn_w_down"])]) if on_mesh else None)
    g["ffn_conv_w"] = jnp.concatenate([dcwg, dcwv], axis=1)
    g["ffn_w_up"] = jnp.concatenate([_mm(u2, dupg, "tn", BF16, "mm_dwup_gate", split=N_CHIPS // 2),
                                     _mm(u2, dupv, "tn", BF16, "mm_dwup_val", split=N_CHIPS // 2)], axis=0)
    du2 = _mm(dupg, p["ffn_w_up"], "nt", F32, "mm_du2_gate", b_kblock=0)
    du2 = _mm(dupv, p["ffn_w_up"], "nt", F32, "mm_du2_val", b_kblock=1, add=du2)
    dh1, dy1, g["norm_ffn_pre"], dsc2, dsh2, g["norm_mix_post"], dg1 = _pre_post_bwd(
        h1, p["norm_ffn_pre"], sc2, sh2, du2, dh2, y1, p["norm_mix_post"], g1, "pre2_post1_bwd")
    dmerged = _mm(dy1, p["w_out"], "nt", BF16, "mm_dmerged")
    g["w_out"] = _mm(merged, dy1, "tn", BF16, "mm_dwout")
    dproj = lax.empty((x2.shape[0], IN_PAD), BF16)
    dproj, dya = _branch_bwd(dproj, proj, ya, dmerged, C_GA, "merge_bwd_attn")
    dproj, dyd = _branch_bwd(dproj, proj, yd, dmerged, C_GD, "merge_bwd_dn")
    dy_attn = _mm(dya, p["w_attn_branch"], "nt", BF16, "mm_dyattn")
    g["w_attn_branch"] = _mm(y_attn, dya, "tn", BF16, "mm_dwab", split=N_CHIPS)
    dy_dn = _mm(dyd, p["w_dn_branch"], "nt", F32, "mm_dydn")
    g["w_dn_branch"] = _mm(y_dn, dyd, "tn", BF16, "mm_dwdb", split=N_CHIPS)
    do, dproj, g["dn_norm_w"] = _dn_out_bwd(dproj, o, proj, p["dn_norm_w"], dy_dn)
    def plug_for(names):
        if not on_mesh:
            return None
        return _exchange_plug([_pair_presum(_pieces(k, g[k]), "presum_" + k) if k in _PRESUM else _pieces(k, g[k]) for k in names])

    early = ("w_out", "w_attn_branch", "w_dn_branch")
    dqkvn, dbg4, got_up = _dn_chunk_bwd(qkvn, bg, g_rows, states, do, bsz, plug_for(_FFN[:1]))
    got_ffn = list(got_up) + list(got_down)
    dproj, g["dn_conv_w"] = _dn_prep_bwd(dproj, proj, p["dn_conv_w"], dqkvn, bsz)
    dproj, dk, dv, dbias, g["attn_sinks"], got_early = _attn_bwd(dproj, proj, bias, sinks, dy_attn, bsz, plug_for(early))
    dproj, g["dn_a_log"], g["dn_dt_bias"] = _bg_bwd(dproj, proj, alog_v, dt_v, dbg4, dk, dv, bsz)
    g["rel_bias"] = _bias_grad(dbias)
    g["w_in"] = _from_padded(_mm(dproj, u1, "tn", BF16, "mm_dwin"))
    if on_mesh:
        du1, got_in = _mm(dproj, p["w_in"], "nn", F32, "mm_du1", plug_for(("w_in",)))
        g.update(zip(_FFN + early + ("w_in",), list(got_ffn) + list(got_early) + list(got_in)))
    else:
        du1 = _mm(dproj, p["w_in"], "nn", F32, "mm_du1")
    dx, g["norm_mix_pre"], dsc1, dsh1 = _pre_bwd(x2, p["norm_mix_pre"], sc1, sh1, du1, dh1, "pre1_bwd")
    dmod = jnp.concatenate([dsh1, dsc1, dg1, dsh2, dsc2, dg2], axis=-1).reshape(bsz, N_MOD * d)
    return sq, dx, dmod, g


_SMALL = (("norm_mix_pre", D_MODEL), ("norm_mix_post", D_MODEL), ("norm_ffn_pre", D_MODEL), ("norm_ffn_post", D_MODEL),
          ("dn_norm_w", DN_HD), ("dn_a_log", LANES), ("dn_dt_bias", LANES), ("attn_sinks", AQ_HEADS * LANES),
          ("rel_bias", AQ_HEADS * LANES), ("dn_conv_w", DN_CONV * 3 * DN), ("ffn_conv_w", FFN_CONV * 2 * D_FF),
          ("loss_sq", D_MODEL))


def _pack_rows(parts, rows):
    flat = jnp.concatenate([a.reshape(-1) for a in parts])
    return jnp.concatenate([flat, jnp.zeros((rows * LANES - flat.shape[0],), F32)]).reshape(rows, LANES)


_W_NAMES = ("ada_w", "ada_b", "norm_mix_pre", "norm_mix_post", "norm_ffn_pre", "norm_ffn_post", "w_in", "dn_conv_w", "dn_a_log",
            "dn_dt_bias", "dn_norm_w", "attn_sinks", "rel_bias", "w_attn_branch", "w_dn_branch", "w_out", "ffn_w_up", "ffn_conv_w",
            "ffn_w_down")
_BIG = ("w_in", "w_attn_branch", "w_dn_branch", "w_out", "ffn_w_up", "ffn_w_down")


def kernel(x, c, *rest):
    nw = len(_W_NAMES)
    w = dict(zip(_W_NAMES, rest[:nw]))
    loss_target = rest[nw]
    m = dict(zip(_W_NAMES, rest[nw + 1:2 * nw + 1]))
    v = dict(zip(_W_NAMES, rest[2 * nw + 1:3 * nw + 1]))
    ix, iy, ic = lax.axis_index("x"), lax.axis_index("y"), lax.axis_index("c")
    chip, dev = 2 * ix + iy, 4 * ix + 2 * iy + ic
    bsz, s, d = x.shape
    t = bsz * s
    n_dev = 8

    front_rows = 64
    front = _pack_rows([c, w["dn_conv_w"], w["ffn_conv_w"]], front_rows)
    w_in_t = jnp.swapaxes(w["w_in"][0], 0, 1).astype(BF16)
    cut1 = SHARD_ROWS // 48 * 16
    cut2 = 2 * cut1
    front_all, (w_in_lo,) = _allgather8(front, "ag_front", _gather_plug([w_in_t[:cut1]]))
    front_all = front_all.reshape(n_dev, front_rows * LANES)
    n_c, n_dc, n_fc = bsz * d, DN_CONV * 3 * DN // N_CHIPS, FFN_CONV * 2 * D_FF // N_CHIPS
    c_all = front_all[:, :n_c].reshape(n_dev * bsz, d)
    per_chip = front_all[0::2]
    dn_conv_full = jnp.transpose(per_chip[:, n_c:n_c + n_dc].reshape(N_CHIPS, DN_CONV, -1), (1, 0, 2)).reshape(DN_CONV, 3 * DN)
    ffn_conv_full = jnp.transpose(per_chip[:, n_c + n_dc:n_c + n_dc + n_fc].reshape(N_CHIPS, FFN_CONV, -1), (1, 0, 2)).reshape(FFN_CONV, 2 * D_FF)

    mod_cols = N_MOD * d // N_CHIPS
    ada_b_loc = lax.dynamic_slice(w["ada_b"], (0, chip * mod_cols), (1, mod_cols))
    mod_part = _mod_fwd(c_all, w["ada_w"][0], ada_b_loc)
    mod_all, (w_in_mid,) = _allgather8(mod_part, "ag_mod", _gather_plug([w_in_t[cut1:cut2]]))
    mod_all = mod_all.reshape(n_dev, n_dev * bsz, mod_cols)[0::2]
    mod = jnp.transpose(lax.dynamic_slice(mod_all, (0, dev * bsz, 0), (N_CHIPS, bsz, mod_cols)), (1, 0, 2)).reshape(bsz, N_MOD * d)

    p = {}
    shards = {"late_mixer": [w[k][0].astype(BF16) for k in _LATE_MIXER], "ffn": [w[k][0].astype(BF16) for k in _FFN],
              "w_in_gathered": [w_in_lo, w_in_mid], "w_in_last": [w_in_t[cut2:]]}
    for k in ("norm_mix_pre", "norm_mix_post", "norm_ffn_pre", "norm_ffn_post", "dn_norm_w", "attn_sinks"):
        p[k] = w[k]
    p["dn_a_log"], p["dn_dt_bias"], p["rel_bias"] = w["dn_a_log"][0], w["dn_dt_bias"][0], w["rel_bias"]
    p["dn_conv_w"], p["ffn_conv_w"] = dn_conv_full, ffn_conv_full

    sq, dx, dmod, g = _device_step(x.reshape(t, d), loss_target.reshape(t, d), mod, p, bsz, shards)
    g["dn_a_log"], g["dn_dt_bias"], g["loss_sq"] = g["dn_a_log"].reshape(-1), g["dn_dt_bias"].reshape(-1), sq
    small_rows = 336
    small = _pack_rows([dmod] + [g[k] for k, _ in _SMALL], small_rows)
    small_all = _allgather8(small, "ag_small").reshape(n_dev, small_rows, LANES)
    n_dm = bsz * N_MOD * d
    dmod_all = small_all.reshape(n_dev, -1)[:, :n_dm].reshape(n_dev * bsz, N_MOD * d)
    tot = _sum_lead(small_all, "sum_small").reshape(-1)
    gs, off = {}, n_dm
    for k, n in _SMALL:
        gs[k] = tot[off:off + n]
        off += n
    loss = 0.5 * jnp.sum(gs["loss_sq"])
    grad = {}
    grad["ada_w"], grad["ada_b"] = _ada_grad(c_all, lax.dynamic_slice(dmod_all, (0, chip * mod_cols), (n_dev * bsz, mod_cols)), dmod_all)
    for k in ("norm_mix_pre", "norm_mix_post", "norm_ffn_pre", "norm_ffn_post", "dn_norm_w"):
        grad[k] = gs[k]
    grad["dn_a_log"] = gs["dn_a_log"][DN_HEADS:2 * DN_HEADS]
    grad["dn_dt_bias"] = gs["dn_dt_bias"][DN_HEADS:2 * DN_HEADS]
    grad["attn_sinks"] = gs["attn_sinks"].reshape(AQ_HEADS, LANES)[:, 0]
    grad["rel_bias"] = gs["rel_bias"].reshape(AQ_HEADS, LANES)[:, :REL_BUCKETS].T
    grad["dn_conv_w"] = lax.dynamic_slice(gs["dn_conv_w"].reshape(DN_CONV, 3 * DN), (0, chip * (3 * DN // N_CHIPS)), (DN_CONV, 3 * DN // N_CHIPS))
    grad["ffn_conv_w"] = lax.dynamic_slice(gs["ffn_conv_w"].reshape(FFN_CONV, 2 * D_FF), (0, chip * (2 * D_FF // N_CHIPS)), (FFN_CONV, 2 * D_FF // N_CHIPS))

    mine = [_sum_lead(g[k], "sum_" + k, rows_apart=(k == "w_in")) for k in _BIG]
    out = {}
    out["ada_w"], theirs = _adamw(w["ada_w"][0], m["ada_w"][0], v["ada_w"][0], [grad["ada_w"]], "adamw_ada_w", _sibling_plug(mine))
    for k, a, b in zip(_BIG, mine, theirs):
        core = ic.reshape(1).astype(jnp.int32)
        if k == "w_in":
            tr = lambda z: jnp.transpose(z, (2, 0, 1))
            out[k] = [jnp.transpose(r, (1, 2, 0)) for r in _adamw_halves(tr(w[k]), tr(m[k]), tr(v[k]), a, b, core, "adamw_" + k)]
        elif k in _PRESUM:
            out[k] = _adamw_halves(w[k][0], m[k][0], v[k][0], a, b, core, "adamw_" + k)
        else:
            out[k] = _adamw(w[k][0], m[k][0], v[k][0], [a, b], "adamw_" + k)
    small_names = [k for k in _W_NAMES if k not in _BIG and k != "ada_w"]
    res_small = _adamw_small([w[k] for k in small_names], [m[k] for k in small_names], [v[k] for k in small_names],
                             [grad[k].reshape(w[k].shape) for k in small_names], "adamw_small")
    out.update(zip(small_names, res_small))
    for k in _BIG + ("ada_w",):
        out[k] = [r.reshape(w[k].shape) for r in out[k]]
    grads, deltas, new_m, new_v = ([out[k][i] for k in _W_NAMES] for i in range(4))
    return (loss, dx.reshape(bsz, s, d), *grads, *deltas, *new_m, *new_v)
```

```python
import math

import numpy as np
import jax
import jax.numpy as jnp
from jax import lax
from jax.experimental import pallas as pl
from jax.experimental.pallas import tpu as pltpu

F32 = jnp.float32
BF16 = jnp.bfloat16
MESH = pl.DeviceIdType.MESH

D_MODEL = 1024
N_MOD = 6
AQ_HEADS, AKV_HEADS, A_HD, WINDOW = 8, 2, 64, 128
REL_BUCKETS, REL_MAX_DIST = 32, 128
DN_HEADS, DN_HD, DN_CONV, DN_CHUNK = 4, 128, 4, 64
D_FF, FFN_CONV = 2816, 3
RMS_EPS, L2_EPS, NEG_INF = 1e-6, 1e-6, -1e30
AQ, AKV, DN = AQ_HEADS * A_HD, AKV_HEADS * A_HD, DN_HEADS * DN_HD
IN_DIM = AQ + 2 * AKV + 3 * DN + DN + 2 * DN_HEADS + 2 * D_MODEL
C_DQKV, C_Q, C_DZ, C_GA, C_GD, C_K, C_V, C_BD = 0, 1536, 2048, 2560, 3584, 4608, 4736, 4864
IN_PAD = 4992
LANES = 128
N_CHIPS = 4

ADAM_LR, ADAM_B1, ADAM_B2, ADAM_EPS, ADAM_WD, ADAM_STEP = 0.001, 0.9, 0.999, 1e-08, 0.01, 10


def _vspec(shape, index_map):
    return pl.BlockSpec(shape, index_map)


MM_VMEM_BUDGET = 40 * 2 ** 20
GRID_STEP_S = 0.35e-6
HBM_BYTES_PER_S = 3.0e12
MXU_FLOPS_PER_S = 9.0e14
MXU_DIM = 256


def _mm_tiles(m, n, k, mode, in_bytes, out_bytes, split=1):
    best = None
    for tm in [t for t in range(LANES, m + 1, LANES) if m % t == 0]:
        for tn in [t for t in range(LANES, n // split + 1, LANES) if (n // split) % t == 0]:
            a, b, o = k * tm * in_bytes, k * tn * in_bytes, tm * tn * out_bytes
            if 2 * (a + b + o) + (a if mode == "tn" else 0) > MM_VMEM_BUDGET:
                continue
            hbm_s = (m * k * in_bytes + (m // tm) * n * k * in_bytes + m * n * out_bytes) / HBM_BYTES_PER_S
            mxu_s = 2 * m * n * k / (MXU_FLOPS_PER_S * min(1.0, tm / MXU_DIM) * min(1.0, tn / MXU_DIM))
            cost = (m // tm) * (n // tn) * GRID_STEP_S + max(hbm_s, mxu_s)
            if best is None or cost < best[0]:
                best = (cost, tm, tn)
    return best[1], best[2]


def _mm(a, b, mode, out_dtype, name, plug=None, split=1, b_kblock=0, add=None):
    if mode == "nn":
        (m, k), n = a.shape, b.shape[1]
        dims = (((1,), (0,)), ((), ()))
    elif mode == "nt":
        (m, k), n = a.shape, b.shape[0]
        dims = (((1,), (1,)), ((), ()))
    else:
        (k, m), n = a.shape, b.shape[1]
        dims = (((0,), (0,)), ((), ()))
    tm, tn = _mm_tiles(m, n, k, mode, a.dtype.itemsize, jnp.dtype(out_dtype).itemsize, split)
    if mode == "tn":
        a_spec = _vspec((k, tm), lambda i, j: (0, i))
    else:
        a_spec = _vspec((tm, k), lambda i, j: (i, 0))
    if mode == "nt":
        b_spec = _vspec((tn, k), lambda i, j: (j, b_kblock))
    else:
        b_spec = _vspec((k, tn), lambda i, j: (0, j))
    in_specs, args = [a_spec, b_spec], (a, b)
    if add is not None:
        in_specs, args = in_specs + [_vspec((tm, tn), lambda i, j: (i, j))], (a, b, add)

    def body(a_ref, b_ref, *rest):
        o_ref = rest[-1]
        acc = lax.dot_general(a_ref[...].astype(BF16), b_ref[...].astype(BF16), dims, preferred_element_type=F32)
        if add is not None:
            acc = acc + rest[0][...]
        o_ref[...] = acc.astype(out_dtype).reshape(o_ref.shape)

    grid = (m // tm, n // tn)
    if split == 1:
        out_spec, out_shape = _vspec((tm, tn), lambda i, j: (i, j)), (m, n)
    else:
        per = n // split // tn
        out_spec, out_shape = _vspec((1, tm, tn), lambda i, j: (j // per, i, j % per)), (split, m, n // split)
    (out,), extra = _plugged_call(body, plug, _grid_ends(grid), args, name=name, grid=grid, in_specs=in_specs,
                                  out_specs=[out_spec], out_shape=[jax.ShapeDtypeStruct(out_shape, out_dtype)])
    return out if plug is None else (out, extra)


def _rms(x, w):
    return (x * lax.rsqrt(jnp.mean(x * x, axis=-1, keepdims=True) + RMS_EPS)) * w


def _pre_f(x, w, sc, sh):
    return _rms(x, w) * (1.0 + sc) + sh


def _post_f(y, w, g):
    return g * _rms(y, w)


def _tok_grid(t, bsz, ts):
    nt = t // bsz // ts
    return nt, (bsz, nt)


def _pre_fwd(x, w, sc, sh, name, ts=512, plug=None):
    t, d = x.shape
    bsz = sc.shape[0]
    nt, grid = _tok_grid(t, bsz, ts)
    row = _vspec((ts, d), lambda b, i: (b * nt + i, 0))
    vec = _vspec((1, d), lambda b, i: (0, 0))
    bvec = _vspec((1, 1, d), lambda b, i: (b, 0, 0))

    def body(x_ref, w_ref, sc_ref, sh_ref, u_ref):
        u_ref[...] = _pre_f(x_ref[...], w_ref[...], sc_ref[0], sh_ref[0]).astype(BF16)

    (u,), extra = _plugged_call(body, plug, _grid_ends(grid), (x, w, sc, sh), name=name, grid=grid, in_specs=[row, vec, bvec, bvec],
                                out_specs=[row], out_shape=[jax.ShapeDtypeStruct((t, d), BF16)])
    return u if plug is None else (u, extra)


def _pre_bwd(x, w, sc, sh, du, dres, name, ts=512):
    t, d = x.shape
    bsz = sc.shape[0]
    nt, grid = _tok_grid(t, bsz, ts)
    row = _vspec((ts, d), lambda b, i: (b * nt + i, 0))
    vec = _vspec((1, d), lambda b, i: (0, 0))
    bvec = _vspec((1, 1, d), lambda b, i: (b, 0, 0))

    def body(x_ref, w_ref, sc_ref, sh_ref, du_ref, dres_ref, dx_ref, dw_ref, dsc_ref, dsh_ref):
        b, i = pl.program_id(0), pl.program_id(1)
        _, vjp = jax.vjp(_pre_f, x_ref[...], w_ref[...], sc_ref[0], sh_ref[0])
        dx, dw, dsc, dsh = vjp(du_ref[...])
        dx_ref[...] = dres_ref[...] + dx

        @pl.when((b == 0) & (i == 0))
        def _():
            dw_ref[...] = jnp.zeros_like(dw_ref)

        @pl.when(i == 0)
        def _():
            dsc_ref[...] = jnp.zeros_like(dsc_ref)
            dsh_ref[...] = jnp.zeros_like(dsh_ref)

        dw_ref[...] += dw
        dsc_ref[0] += dsc
        dsh_ref[0] += dsh

    return pl.pallas_call(
        body, name=name, grid=grid, in_specs=[row, vec, bvec, bvec, row, row], out_specs=[row, vec, bvec, bvec],
        out_shape=[jax.ShapeDtypeStruct((t, d), F32), jax.ShapeDtypeStruct((1, d), F32),
                   jax.ShapeDtypeStruct((bsz, 1, d), F32), jax.ShapeDtypeStruct((bsz, 1, d), F32)],
    )(x, w, sc, sh, du, dres)


def _accumulate(ref, val, first):
    @pl.when(first)
    def _():
        ref[...] = jnp.zeros_like(ref)

    ref[...] += val.reshape(ref.shape)


def _post_pre_fwd(res, y, w_post, g, w_pre, sc, sh, name, ts=512):
    t, d = y.shape
    bsz = g.shape[0]
    nt, grid = _tok_grid(t, bsz, ts)
    row = _vspec((ts, d), lambda b, i: (b * nt + i, 0))
    vec = _vspec((1, d), lambda b, i: (0, 0))
    bvec = _vspec((1, 1, d), lambda b, i: (b, 0, 0))

    def body(res_ref, y_ref, wp_ref, g_ref, w_ref, sc_ref, sh_ref, h_ref, u_ref):
        h = res_ref[...] + _post_f(y_ref[...], wp_ref[...], g_ref[0])
        h_ref[...] = h
        u_ref[...] = _pre_f(h, w_ref[...], sc_ref[0], sh_ref[0]).astype(BF16)

    return pl.pallas_call(body, name=name, grid=grid, in_specs=[row, row, vec, bvec, vec, bvec, bvec], out_specs=[row, row],
                          out_shape=[jax.ShapeDtypeStruct((t, d), F32), jax.ShapeDtypeStruct((t, d), BF16)],
                          )(res, y, w_post, g, w_pre, sc, sh)


def _post_loss_bwd(res, y, w, g, tgt, name, ts=512):
    t, d = y.shape
    bsz = g.shape[0]
    nt, grid = _tok_grid(t, bsz, ts)
    row = _vspec((ts, d), lambda b, i: (b * nt + i, 0))
    vec = _vspec((1, d), lambda b, i: (0, 0))
    bvec = _vspec((1, 1, d), lambda b, i: (b, 0, 0))

    def body(res_ref, y_ref, w_ref, g_ref, tgt_ref, dh_ref, dy_ref, dw_ref, dg_ref, sq_ref):
        b, i = pl.program_id(0), pl.program_id(1)
        part, vjp = jax.vjp(_post_f, y_ref[...], w_ref[...], g_ref[0])
        e = res_ref[...] + part - tgt_ref[...]
        dh = e * (1.0 / d)
        dh_ref[...] = dh
        dy, dw, dg = vjp(dh)
        dy_ref[...] = dy.astype(BF16)
        _accumulate(dw_ref, dw, (b == 0) & (i == 0))
        _accumulate(dg_ref, dg, i == 0)
        _accumulate(sq_ref, jnp.sum(e * e, axis=0, keepdims=True) * (1.0 / d), (b == 0) & (i == 0))

    return pl.pallas_call(
        body, name=name, grid=grid, in_specs=[row, row, vec, bvec, row], out_specs=[row, row, vec, bvec, vec],
        out_shape=[jax.ShapeDtypeStruct((t, d), F32), jax.ShapeDtypeStruct((t, d), BF16), jax.ShapeDtypeStruct((1, d), F32),
                   jax.ShapeDtypeStruct((bsz, 1, d), F32), jax.ShapeDtypeStruct((1, d), F32)],
    )(res, y, w, g, tgt)


def _pre_post_bwd(x, w, sc, sh, du, dres, y, w_post, g, name, ts=512):
    t, d = x.shape
    bsz = sc.shape[0]
    nt, grid = _tok_grid(t, bsz, ts)
    row = _vspec((ts, d), lambda b, i: (b * nt + i, 0))
    vec = _vspec((1, d), lambda b, i: (0, 0))
    bvec = _vspec((1, 1, d), lambda b, i: (b, 0, 0))

    def body(x_ref, w_ref, sc_ref, sh_ref, du_ref, dres_ref, y_ref, wp_ref, g_ref,
             dx_ref, dy_ref, dw_ref, dsc_ref, dsh_ref, dwp_ref, dg_ref):
        b, i = pl.program_id(0), pl.program_id(1)
        _, vjp = jax.vjp(_pre_f, x_ref[...], w_ref[...], sc_ref[0], sh_ref[0])
        dx, dw, dsc, dsh = vjp(du_ref[...])
        dx = dres_ref[...] + dx
        dx_ref[...] = dx
        _, vjp_post = jax.vjp(_post_f, y_ref[...], wp_ref[...], g_ref[0])
        dy, dwp, dg = vjp_post(dx)
        dy_ref[...] = dy.astype(BF16)
        first = (b == 0) & (i == 0)
        _accumulate(dw_ref, dw, first)
        _accumulate(dwp_ref, dwp, first)
        _accumulate(dsc_ref, dsc, i == 0)
        _accumulate(dsh_ref, dsh, i == 0)
        _accumulate(dg_ref, dg, i == 0)

    v1, vb = jax.ShapeDtypeStruct((1, d), F32), jax.ShapeDtypeStruct((bsz, 1, d), F32)
    return pl.pallas_call(
        body, name=name, grid=grid, in_specs=[row, vec, bvec, bvec, row, row, row, vec, bvec],
        out_specs=[row, row, vec, bvec, bvec, vec, bvec],
        out_shape=[jax.ShapeDtypeStruct((t, d), F32), jax.ShapeDtypeStruct((t, d), BF16), v1, vb, vb, v1, vb],
    )(x, w, sc, sh, du, dres, y, w_post, g)


def _merge_f(ga, gd, ya, yd):
    return jax.nn.sigmoid(ga) * ya + jax.nn.sigmoid(gd) * yd


_MW = 512


def _merge_fwd(proj, ya, yd, ts=512):
    t, d = ya.shape
    blk = _vspec((ts, _MW), lambda i, j: (i, j))
    ga = _vspec((ts, _MW), lambda i, j: (i, C_GA // _MW + j))
    gd = _vspec((ts, _MW), lambda i, j: (i, C_GD // _MW + j))

    def body(ga_ref, gd_ref, ya_ref, yd_ref, o_ref):
        o_ref[...] = _merge_f(ga_ref[...], gd_ref[...], ya_ref[...].astype(F32), yd_ref[...].astype(F32)).astype(BF16)

    return pl.pallas_call(body, name="merge_fwd", grid=(t // ts, d // _MW), in_specs=[ga, gd, blk, blk], out_specs=blk,
                          out_shape=jax.ShapeDtypeStruct((t, d), BF16))(proj, proj, ya, yd)


_ANY = pl.BlockSpec(memory_space=pl.ANY)


def _branch_bwd(dproj, proj, y, dm, col0, name, ts=512):
    t, d = y.shape
    blk = _vspec((ts, _MW), lambda i, j: (i, j))
    gate = _vspec((ts, _MW), lambda i, j: (i, col0 // _MW + j))

    def body(buf_ref, g_ref, y_ref, dm_ref, dg_ref, dy_ref):
        del buf_ref
        _, vjp = jax.vjp(lambda g, yy: jax.nn.sigmoid(g) * yy, g_ref[...], y_ref[...].astype(F32))
        dg, dy = vjp(dm_ref[...].astype(F32))
        dg_ref[...] = dg.astype(BF16)
        dy_ref[...] = dy.astype(BF16)

    return pl.pallas_call(body, name=name, grid=(t // ts, d // _MW), in_specs=[_ANY, gate, blk, blk], out_specs=[gate, blk],
                          out_shape=[jax.ShapeDtypeStruct(dproj.shape, BF16), jax.ShapeDtypeStruct((t, d), BF16)],
                          input_output_aliases={0: 0})(dproj, proj, y, dm)


def _shift_down(x, s):
    if s == 0:
        return x
    r = lax.broadcasted_iota(jnp.int32, x.shape, 0)
    return jnp.where(r >= s, pltpu.roll(x, s, 0), 0.0)


def _shift_up(x, s):
    if s == 0:
        return x
    n = x.shape[0]
    r = lax.broadcasted_iota(jnp.int32, x.shape, 0)
    return jnp.where(r < n - s, pltpu.roll(x, n - s, 0), 0.0)


def _conv_fwd(x, w, k):
    out = None
    for j in range(k):
        term = w[j:j + 1, :] * _shift_down(x, k - 1 - j)
        out = term if out is None else out + term
    return out


def _conv_bwd(x, w, dc, k):
    dx = None
    dws = []
    for j in range(k):
        up = _shift_up(dc, k - 1 - j)
        term = w[j:j + 1, :] * up
        dx = term if dx is None else dx + term
        dws.append(jnp.sum(up * x, axis=0, keepdims=True))
    return dx, jnp.concatenate(dws, axis=0)


def _geglu_f(gate, val):
    return jax.nn.gelu(gate, approximate=True) * val


_FW = 256


def _ffn_act_fwd(up, conv_w, bsz):
    t = up.shape[0]
    s = t // bsz
    nj = D_FF // _FW
    xg = _vspec((s, _FW), lambda b, j: (b, j))
    xv = _vspec((s, _FW), lambda b, j: (b, nj + j))
    wg = _vspec((FFN_CONV, _FW), lambda b, j: (0, j))
    wv = _vspec((FFN_CONV, _FW), lambda b, j: (0, nj + j))

    def body(xg_ref, xv_ref, wg_ref, wv_ref, o_ref, cg_ref, cv_ref):
        gate = _conv_fwd(xg_ref[...].astype(F32), wg_ref[...], FFN_CONV)
        val = _conv_fwd(xv_ref[...].astype(F32), wv_ref[...], FFN_CONV)
        o_ref[...] = _geglu_f(gate, val).astype(BF16)
        cg_ref[...] = gate.astype(BF16)
        cv_ref[...] = val.astype(BF16)

    blk = _vspec((s, _FW), lambda b, j: (b, j))
    o = jax.ShapeDtypeStruct((t, D_FF), BF16)
    return pl.pallas_call(body, name="ffn_act_fwd", grid=(bsz, nj), in_specs=[xg, xv, wg, wv], out_specs=[blk] * 3,
                          out_shape=[o] * 3)(up, up, conv_w, conv_w)


def _ffn_act_bwd(up, conv_w, conv_out, dact, bsz, plug=None):
    t = up.shape[0]
    s = t // bsz
    nj = D_FF // _FW
    xg = _vspec((s, _FW), lambda j, b: (b, j))
    xv = _vspec((s, _FW), lambda j, b: (b, nj + j))
    wg = _vspec((FFN_CONV, _FW), lambda j, b: (0, j))
    wv = _vspec((FFN_CONV, _FW), lambda j, b: (0, nj + j))
    da = _vspec((s, _FW), lambda j, b: (b, j))
    dwo = _vspec((FFN_CONV, _FW), lambda j, b: (0, j))

    def body(xg_ref, xv_ref, wg_ref, wv_ref, cg_ref, cv_ref, da_ref, dxg_ref, dxv_ref, dwg_ref, dwv_ref):
        b = pl.program_id(1)
        xg_, xv_, wg_, wv_ = xg_ref[...].astype(F32), xv_ref[...].astype(F32), wg_ref[...], wv_ref[...]
        _, vjp = jax.vjp(_geglu_f, cg_ref[...].astype(F32), cv_ref[...].astype(F32))
        dgate, dval = vjp(da_ref[...].astype(F32))
        dxg, dwg = _conv_bwd(xg_, wg_, dgate, FFN_CONV)
        dxv, dwv = _conv_bwd(xv_, wv_, dval, FFN_CONV)
        dxg_ref[...] = dxg.astype(BF16)
        dxv_ref[...] = dxv.astype(BF16)

        @pl.when(b == 0)
        def _():
            dwg_ref[...] = jnp.zeros_like(dwg_ref)
            dwv_ref[...] = jnp.zeros_like(dwv_ref)

        dwg_ref[...] += dwg
        dwv_ref[...] += dwv

    outs, extra = _plugged_call(
        body, plug, _grid_ends((nj, bsz)), (up, up, conv_w, conv_w, *conv_out, dact), name="ffn_act_bwd", grid=(nj, bsz),
        in_specs=[xg, xv, wg, wv, da, da, da], out_specs=[da, da, dwo, dwo],
        out_shape=[jax.ShapeDtypeStruct((t, D_FF), BF16), jax.ShapeDtypeStruct((t, D_FF), BF16),
                   jax.ShapeDtypeStruct((FFN_CONV, D_FF), F32), jax.ShapeDtypeStruct((FFN_CONV, D_FF), F32)])
    return (*outs, extra)


def _bucket_table():
    qi = np.arange(WINDOW)[:, None]
    kj = np.arange(2 * WINDOW)[None, :]
    dist = WINDOW + qi - kj
    dc = np.maximum(dist, 0)
    max_exact = REL_BUCKETS // 2
    scaled = np.log(np.maximum(dc, 1).astype(np.float32) / np.float32(max_exact)) / np.float32(math.log(REL_MAX_DIST / max_exact))
    large = max_exact + (scaled.astype(np.float32) * np.float32(REL_BUCKETS - max_exact)).astype(np.int32)
    large = np.minimum(large, REL_BUCKETS - 1)
    bucket = np.where(dc < max_exact, dc, large).astype(np.int32)
    in_band = ((dist >= 0) & (dist < WINDOW)).astype(np.int32)
    return bucket, in_band


def _bias_build(rel_bias):
    bucket, _ = _bucket_table()

    def body(rb_ref, idx_ref, o_ref):
        h = pl.program_id(0)
        idx = idx_ref[...]
        acc = jnp.zeros(idx.shape, F32)
        for r in range(REL_BUCKETS):
            acc = jnp.where(idx == r, rb_ref[r, h], acc)
        o_ref[0] = acc

    return pl.pallas_call(
        body, name="bias_build", grid=(AQ_HEADS,),
        in_specs=[pl.BlockSpec(memory_space=pltpu.SMEM), _vspec((WINDOW, 2 * WINDOW), lambda h: (0, 0))],
        out_specs=_vspec((1, WINDOW, 2 * WINDOW), lambda h: (h, 0, 0)),
        out_shape=jax.ShapeDtypeStruct((AQ_HEADS, WINDOW, 2 * WINDOW), F32),
    )(rel_bias, jnp.asarray(bucket))


def _bias_grad(dbias):
    bucket, _ = _bucket_table()

    def body(db_ref, idx_ref, o_ref):
        idx = idx_ref[...]
        db = db_ref[0]
        lane = lax.broadcasted_iota(jnp.int32, (1, LANES), 1)
        acc = jnp.zeros((1, LANES), F32)
        for r in range(REL_BUCKETS):
            s = jnp.sum(jnp.sum(jnp.where(idx == r, db, 0.0), axis=1, keepdims=True), axis=0, keepdims=True)
            acc = jnp.where(lane == r, s, acc)
        o_ref[0] = acc

    return pl.pallas_call(
        body, name="bias_grad", grid=(AQ_HEADS,),
        in_specs=[_vspec((1, WINDOW, 2 * WINDOW), lambda h: (h, 0, 0)), _vspec((WINDOW, 2 * WINDOW), lambda h: (0, 0))],
        out_specs=_vspec((1, 1, LANES), lambda h: (h, 0, 0)),
        out_shape=jax.ShapeDtypeStruct((AQ_HEADS, 1, LANES), F32),
    )(dbias, jnp.asarray(bucket))


def _attn_mask(n):
    qi = lax.broadcasted_iota(jnp.int32, (WINDOW, 2 * WINDOW), 0)
    kj = lax.broadcasted_iota(jnp.int32, (WINDOW, 2 * WINDOW), 1)
    dist = WINDOW + qi - kj
    band = (dist >= 0) & (dist < WINDOW)
    return band & ((kj >= WINDOW) | (n > 0))


def _attn_probs(qk, bias, sink, mask):
    s = jnp.where(mask, qk * (A_HD ** -0.5) + bias, NEG_INF)
    m = jnp.maximum(jnp.max(s, axis=-1, keepdims=True), sink)
    p = jnp.exp(s - m)
    es = jnp.exp(sink - m)
    inv = 1.0 / (jnp.sum(p, axis=-1, keepdims=True) + es)
    return p * inv, es * inv


def _attn_fwd(proj, bias, sinks, bsz):
    t = proj.shape[0]
    s = t // bsz
    nb = s // WINDOW
    grp = AQ_HEADS // AKV_HEADS

    def body(q_ref, k_ref, v_ref, bias_ref, sink_ref, y_ref, kp_ref, vp_ref):
        kp_ref[0:WINDOW, :] = jnp.zeros((WINDOW, LANES), BF16)
        vp_ref[0:WINDOW, :] = jnp.zeros((WINDOW, LANES), BF16)
        kp_ref[WINDOW:, :] = k_ref[...].astype(BF16)
        vp_ref[WINDOW:, :] = v_ref[...].astype(BF16)

        def blk(n, carry):
            r0 = pl.multiple_of(n * WINDOW, WINDOW)
            mask = _attn_mask(n)
            kband = kp_ref[pl.ds(r0, 2 * WINDOW), :]
            vband = vp_ref[pl.ds(r0, 2 * WINDOW), :]
            qb = q_ref[pl.ds(r0, WINDOW), :].astype(BF16)
            heads = range(AQ_HEADS)
            hsl = lambda h: slice(h * A_HD, (h + 1) * A_HD)
            kbs = [kband[:, hsl(kv)] for kv in range(AKV_HEADS)]
            vbs = [vband[:, hsl(kv)] for kv in range(AKV_HEADS)]
            qks = [lax.dot_general(qb[:, hsl(h)], kbs[h // grp], _NT, preferred_element_type=F32) for h in heads]
            probs = [_attn_probs(qks[h], bias_ref[h], sink_ref[0, h], mask)[0] for h in heads]
            outs = [jnp.dot(probs[h].astype(BF16), vbs[h // grp], preferred_element_type=F32) for h in heads]
            y_ref[pl.ds(r0, WINDOW), :] = jnp.concatenate(outs, axis=1).astype(BF16)
            return carry

        lax.fori_loop(0, nb, blk, 0)

    return pl.pallas_call(
        body, name="attn_fwd", grid=(bsz,),
        in_specs=[_vspec((s, AQ), lambda b: (b, C_Q // AQ)), _vspec((s, AKV), lambda b: (b, C_K // AKV)),
                  _vspec((s, AKV), lambda b: (b, C_V // AKV)),
                  _vspec((AQ_HEADS, WINDOW, 2 * WINDOW), lambda b: (0, 0, 0)), pl.BlockSpec(memory_space=pltpu.SMEM)],
        out_specs=_vspec((s, AQ), lambda b: (b, 0)), out_shape=jax.ShapeDtypeStruct((t, AQ), BF16),
        scratch_shapes=[pltpu.VMEM((s + WINDOW, LANES), BF16), pltpu.VMEM((s + WINDOW, LANES), BF16)],
    )(proj, proj, proj, bias, sinks)


def _attn_bwd(dproj, proj, bias, sinks, dy, bsz, plug=None):
    t = proj.shape[0]
    s = t // bsz
    nb = s // WINDOW
    grp = AQ_HEADS // AKV_HEADS
    scale = A_HD ** -0.5

    def body(q_ref, k_ref, v_ref, bias_ref, sink_ref, dy_ref, dq_ref, dk_ref, dv_ref, dbias_ref, dsink_ref,
             kp_ref, vp_ref, dkp_ref, dvp_ref):
        b = pl.program_id(0)
        kp_ref[0:WINDOW, :] = jnp.zeros((WINDOW, LANES), BF16)
        vp_ref[0:WINDOW, :] = jnp.zeros((WINDOW, LANES), BF16)
        kp_ref[WINDOW:, :] = k_ref[...].astype(BF16)
        vp_ref[WINDOW:, :] = v_ref[...].astype(BF16)
        dkp_ref[...] = jnp.zeros_like(dkp_ref)
        dvp_ref[...] = jnp.zeros_like(dvp_ref)

        @pl.when(b == 0)
        def _():
            dbias_ref[...] = jnp.zeros_like(dbias_ref)
            dsink_ref[...] = jnp.zeros_like(dsink_ref)

        def blk(n, carry):
            r0 = pl.multiple_of(n * WINDOW, WINDOW)
            mask = _attn_mask(n)
            kband = kp_ref[pl.ds(r0, 2 * WINDOW), :]
            vband = vp_ref[pl.ds(r0, 2 * WINDOW), :]
            qb = q_ref[pl.ds(r0, WINDOW), :].astype(BF16)
            dyb = dy_ref[pl.ds(r0, WINDOW), :].astype(BF16)
            heads = range(AQ_HEADS)
            hsl = lambda h: slice(h * A_HD, (h + 1) * A_HD)
            kbs = [kband[:, hsl(kv)] for kv in range(AKV_HEADS)]
            vbs = [vband[:, hsl(kv)] for kv in range(AKV_HEADS)]
            qhs = [qb[:, hsl(h)] for h in heads]
            dyhs = [dyb[:, hsl(h)] for h in heads]
            qks = [lax.dot_general(qhs[h], kbs[h // grp], _NT, preferred_element_type=F32) for h in heads]
            dprobs = [lax.dot_general(dyhs[h], vbs[h // grp], _NT, preferred_element_type=F32) for h in heads]
            pbs, dsbs = [], []
            for h in heads:
                probs, psink = _attn_probs(qks[h], bias_ref[h], sink_ref[0, h], mask)
                rowdot = jnp.sum(probs * dprobs[h], axis=-1, keepdims=True)
                ds = probs * (dprobs[h] - rowdot)
                dbias_ref[h] += ds
                dsink_ref[h] += jnp.sum(-psink * rowdot, axis=0, keepdims=True) + jnp.zeros((1, LANES), F32)
                pbs.append(probs.astype(BF16))
                dsbs.append(ds.astype(BF16))
            dvhs = [lax.dot_general(pbs[h], dyhs[h], _TN, preferred_element_type=F32) for h in heads]
            dqs = [jnp.dot(dsbs[h], kbs[h // grp], preferred_element_type=F32) * scale for h in heads]
            dkhs = [lax.dot_general(dsbs[h], qhs[h], _TN, preferred_element_type=F32) * scale for h in heads]
            dks = [sum(dkhs[kv * grp + 1:(kv + 1) * grp], dkhs[kv * grp]) for kv in range(AKV_HEADS)]
            dvs = [sum(dvhs[kv * grp + 1:(kv + 1) * grp], dvhs[kv * grp]) for kv in range(AKV_HEADS)]
            dq_ref[pl.ds(r0, WINDOW), :] = jnp.concatenate(dqs, axis=1).astype(BF16)
            dkp_ref[pl.ds(r0, 2 * WINDOW), :] += jnp.concatenate(dks, axis=1)
            dvp_ref[pl.ds(r0, 2 * WINDOW), :] += jnp.concatenate(dvs, axis=1)
            return carry

        lax.fori_loop(0, nb, blk, 0)
        dk_ref[...] = dkp_ref[WINDOW:, :].astype(BF16)
        dv_ref[...] = dvp_ref[WINDOW:, :].astype(BF16)

    kvs = jax.ShapeDtypeStruct((t, AKV), BF16)
    outs, extra = _plugged_call(
        lambda buf_ref, *refs: body(*refs), plug, _grid_ends((bsz,)), (dproj, proj, proj, proj, bias, sinks, dy),
        name="attn_bwd", grid=(bsz,),
        in_specs=[_ANY, _vspec((s, AQ), lambda b: (b, C_Q // AQ)), _vspec((s, AKV), lambda b: (b, C_K // AKV)),
                  _vspec((s, AKV), lambda b: (b, C_V // AKV)),
                  _vspec((AQ_HEADS, WINDOW, 2 * WINDOW), lambda b: (0, 0, 0)), pl.BlockSpec(memory_space=pltpu.SMEM),
                  _vspec((s, AQ), lambda b: (b, 0))],
        out_specs=[_vspec((s, AQ), lambda b: (b, C_Q // AQ)), _vspec((s, AKV), lambda b: (b, 0)), _vspec((s, AKV), lambda b: (b, 0)),
                   _vspec((AQ_HEADS, WINDOW, 2 * WINDOW), lambda b: (0, 0, 0)), _vspec((AQ_HEADS, 1, LANES), lambda b: (0, 0, 0))],
        out_shape=[jax.ShapeDtypeStruct(dproj.shape, BF16), kvs, kvs,
                   jax.ShapeDtypeStruct((AQ_HEADS, WINDOW, 2 * WINDOW), F32), jax.ShapeDtypeStruct((AQ_HEADS, 1, LANES), F32)],
        scratch_shapes=[pltpu.VMEM((s + WINDOW, LANES), BF16), pltpu.VMEM((s + WINDOW, LANES), BF16),
                        pltpu.VMEM((s + WINDOW, LANES), F32), pltpu.VMEM((s + WINDOW, LANES), F32)],
        aliases={0: 0})
    return (*outs, extra)


def _dn_act_f(c, is_qk):
    a = jax.nn.silu(c)
    outs = []
    for h in range(DN_HEADS):
        ah = a[:, h * DN_HD:(h + 1) * DN_HD]
        nh = ah * lax.rsqrt(jnp.sum(ah * ah, axis=-1, keepdims=True) + L2_EPS)
        outs.append(jnp.where(is_qk, nh, ah))
    return jnp.concatenate(outs, axis=1)


def _dn_prep_fwd(proj, conv_w, bsz):
    t = proj.shape[0]
    s = t // bsz
    blk = _vspec((s, DN), lambda b, j: (b, j))
    wsp = _vspec((DN_CONV, DN), lambda b, j: (0, j))

    def body(x_ref, w_ref, o_ref, c_ref):
        j = pl.program_id(1)
        c = _conv_fwd(x_ref[...], w_ref[...], DN_CONV)
        o_ref[...] = _dn_act_f(c, j < 2)
        c_ref[...] = c.astype(BF16)

    return pl.pallas_call(body, name="dn_prep_fwd", grid=(bsz, 3), in_specs=[blk, wsp], out_specs=[blk, blk],
                          out_shape=[jax.ShapeDtypeStruct((t, 3 * DN), F32), jax.ShapeDtypeStruct((t, 3 * DN), BF16)])(proj, conv_w)


def _dn_prep_bwd(dproj, proj, conv_w, conv_out, dqkvn, bsz):
    t = proj.shape[0]
    s = t // bsz
    blk = _vspec((s, DN), lambda j, b: (b, j))
    wsp = _vspec((DN_CONV, DN), lambda j, b: (0, j))

    def body(buf_ref, x_ref, w_ref, c_ref, d_ref, dx_ref, dw_ref):
        del buf_ref
        j, b = pl.program_id(0), pl.program_id(1)
        x, w = x_ref[...], w_ref[...]
        _, vjp = jax.vjp(lambda cc: _dn_act_f(cc, j < 2), c_ref[...].astype(F32))
        (dc,) = vjp(d_ref[0])
        dx, dw = _conv_bwd(x, w, dc, DN_CONV)
        dx_ref[...] = dx.astype(BF16)

        @pl.when(b == 0)
        def _():
            dw_ref[...] = jnp.zeros_like(dw_ref)

        dw_ref[...] += dw

    return pl.pallas_call(
        body, name="dn_prep_bwd", grid=(3, bsz),
        in_specs=[_ANY, blk, wsp, blk, _vspec((1, s, DN), lambda j, b: (j, b, 0))], out_specs=[blk, wsp],
        out_shape=[jax.ShapeDtypeStruct(dproj.shape, BF16), jax.ShapeDtypeStruct((DN_CONV, 3 * DN), F32)],
        input_output_aliases={0: 0},
    )(dproj, proj, conv_w, conv_out, dqkvn)


def _bg_f(x, alog, dt):
    lane = lax.broadcasted_iota(jnp.int32, x.shape, 1)
    beta = jax.nn.sigmoid(x)
    g = -jnp.exp(alog) * jax.nn.softplus(x + dt)
    return jnp.where(lane < DN_HEADS, beta, jnp.where(lane < 2 * DN_HEADS, g, 0.0))


def _bg_fwd(proj, alog, dt, bsz):
    t = proj.shape[0]
    s = t // bsz
    vec = _vspec((1, LANES), lambda b: (0, 0))

    def body(x_ref, a_ref, d_ref, o_ref):
        o_ref[...] = _bg_f(x_ref[...], a_ref[...], d_ref[...])

    return pl.pallas_call(body, name="bg_fwd", grid=(bsz,), in_specs=[_vspec((s, LANES), lambda b: (b, C_BD // LANES)), vec, vec],
                          out_specs=_vspec((s, LANES), lambda b: (b, 0)), out_shape=jax.ShapeDtypeStruct((t, LANES), F32))(proj, alog, dt)


def _bg_bwd(dproj, proj, alog, dt, dbg4, dk, dv, bsz):
    t = proj.shape[0]
    s = t // bsz
    vec = _vspec((1, LANES), lambda b: (0, 0))
    kv = _vspec((s, AKV), lambda b: (b, 0))
    tail = 3 * LANES

    def body(buf_ref, x_ref, a_ref, d_ref, g4_ref, dk_ref, dv_ref, dx_ref, da_ref, dd_ref):
        del buf_ref
        b = pl.program_id(0)
        lane = lax.broadcasted_iota(jnp.int32, (s, LANES), 1)
        dbg = jnp.zeros((s, LANES), F32)
        for h in range(DN_HEADS):
            gh = g4_ref[:, h * DN_HD:(h + 1) * DN_HD]
            dbg = jnp.where(lane == h, gh[:, 0:1], dbg)
            dbg = jnp.where(lane == DN_HEADS + h, gh[:, 1:2], dbg)
        _, vjp = jax.vjp(_bg_f, x_ref[...], a_ref[...], d_ref[...])
        dx, da, dd = vjp(dbg)
        dx_ref[...] = jnp.concatenate([dk_ref[...], dv_ref[...], dx.astype(BF16)], axis=1)

        @pl.when(b == 0)
        def _():
            da_ref[...] = jnp.zeros_like(da_ref)
            dd_ref[...] = jnp.zeros_like(dd_ref)

        da_ref[...] += da
        dd_ref[...] += dd

    return pl.pallas_call(
        body, name="bg_bwd", grid=(bsz,),
        in_specs=[_ANY, _vspec((s, LANES), lambda b: (b, C_BD // LANES)), vec, vec, _vspec((s, DN), lambda b: (b, 0)), kv, kv],
        out_specs=[_vspec((s, tail), lambda b: (b, C_K // tail)), vec, vec],
        out_shape=[jax.ShapeDtypeStruct(dproj.shape, BF16), jax.ShapeDtypeStruct((1, LANES), F32), jax.ShapeDtypeStruct((1, LANES), F32)],
        input_output_aliases={0: 0},
    )(dproj, proj, alog, dt, dbg4, dk, dv)


def _dn_out_f(o, z, w):
    outs = []
    for h in range(DN_HEADS):
        sl = slice(h * DN_HD, (h + 1) * DN_HD)
        outs.append(_rms(o[:, sl], w) * jax.nn.silu(z[:, sl]))
    return jnp.concatenate(outs, axis=1)


def _dn_out_fwd(o, proj, w, ts=512):
    t = o.shape[0]
    blk = _vspec((ts, DN), lambda i: (i, 0))
    zsp = _vspec((ts, DN), lambda i: (i, C_DZ // DN))
    vec = _vspec((1, DN_HD), lambda i: (0, 0))

    def body(o_ref, z_ref, w_ref, y_ref):
        y_ref[...] = _dn_out_f(o_ref[...], z_ref[...], w_ref[...]).astype(BF16)

    return pl.pallas_call(body, name="dn_out_fwd", grid=(t // ts,), in_specs=[blk, zsp, vec], out_specs=blk,
                          out_shape=jax.ShapeDtypeStruct((t, DN), BF16))(o, proj, w)


def _dn_out_bwd(dproj, o, proj, w, dy, ts=512):
    t = o.shape[0]
    blk = _vspec((ts, DN), lambda i: (i, 0))
    zsp = _vspec((ts, DN), lambda i: (i, C_DZ // DN))
    vec = _vspec((1, DN_HD), lambda i: (0, 0))

    def body(buf_ref, o_ref, z_ref, w_ref, dy_ref, do_ref, dz_ref, dw_ref):
        del buf_ref
        i = pl.program_id(0)
        _, vjp = jax.vjp(_dn_out_f, o_ref[...], z_ref[...], w_ref[...])
        do, dz, dw = vjp(dy_ref[...].astype(F32))
        do_ref[...] = do
        dz_ref[...] = dz.astype(BF16)

        @pl.when(i == 0)
        def _():
            dw_ref[...] = jnp.zeros_like(dw_ref)

        dw_ref[...] += dw

    return pl.pallas_call(
        body, name="dn_out_bwd", grid=(t // ts,), in_specs=[_ANY, blk, zsp, vec, blk], out_specs=[blk, zsp, vec],
        out_shape=[jax.ShapeDtypeStruct((t, DN), F32), jax.ShapeDtypeStruct(dproj.shape, BF16), jax.ShapeDtypeStruct((1, DN_HD), F32)],
        input_output_aliases={0: 1},
    )(dproj, o, proj, w, dy)


_C = DN_CHUNK


def _dot(a, b, dims):
    return lax.dot_general(a.astype(BF16), b.astype(BF16), dims, preferred_element_type=F32)


def _split(a):
    hi = a.astype(BF16)
    return hi, (a - hi.astype(F32)).astype(BF16)


def _dot3(a, b, dims):
    (ah, al), (bh, bl) = (a if isinstance(a, tuple) else _split(a)), (b if isinstance(b, tuple) else _split(b))
    mm = lambda x, y: lax.dot_general(x, y, dims, preferred_element_type=F32)
    return mm(ah, bh) + (mm(ah, bl) + mm(al, bh))


_NN = (((1,), (0,)), ((), ()))
_NT = (((1,), (1,)), ((), ()))
_TN = (((0,), (0,)), ((), ()))


_SUB = 8


def _tri_inverses(ls, lts):
    ri8 = lax.broadcasted_iota(jnp.int32, (_SUB, _C), 0)
    ci8 = lax.broadcasted_iota(jnp.int32, (_SUB, _C), 1)
    nblk = _C // _SUB
    ts = []
    for lt in lts:
        blocks = [jnp.where(ci8 == ri8 + _SUB * b, 1.0, 0.0).astype(F32) for b in range(nblk)]
        for r in range(1, _SUB):
            for b in range(nblk):
                coef = lt[_SUB * b:_SUB * (b + 1), _SUB * b + r:_SUB * b + r + 1]
                row = jnp.sum(coef * blocks[b], axis=0, keepdims=True)
                blocks[b] = jnp.where(ri8 == r, blocks[b] - row, blocks[b])
        ts.append(jnp.concatenate(blocks, axis=0))
    ri = lax.broadcasted_iota(jnp.int32, (_C, _C), 0)
    ci = lax.broadcasted_iota(jnp.int32, (_C, _C), 1)
    s = _SUB
    while s < _C:
        shift = s.bit_length()
        quad = ((ri >> shift) == (ci >> shift)) & ((ri & s) != 0) & ((ci & s) == 0)
        offs = [jnp.where(quad, l, 0.0) for l in ls]
        tsp = [_split(t) for t in ts]
        left = [_dot3(tp, off, _NN) for tp, off in zip(tsp, offs)]
        ts = [t - _dot3(lo, tp, _NN) for t, lo, tp in zip(ts, left, tsp)]
        s *= 2
    return ts


_SEG = 512
_HEADS = tuple(range(DN_HEADS))


def _hsl(hh):
    return slice(hh * DN_HD, (hh + 1) * DN_HD)


def _chunk_specs(bsz, nseg, reverse):
    seg = (lambda i: nseg - 1 - i) if reverse else (lambda i: i)
    ncs = _SEG // _C
    col = lambda off: _vspec((bsz, _SEG, DN), lambda i: (0, seg(i), off))
    return (col, _vspec((bsz, _SEG, LANES), lambda i: (0, seg(i), 0)),
            _vspec((bsz, DN_HEADS, ncs, _C), lambda i: (0, 0, seg(i), 0)),
            _vspec((bsz, DN_HEADS, ncs, DN_HD, DN_HD), lambda i: (0, 0, seg(i), 0, 0)),
            _vspec((bsz, DN_HEADS, ncs, _C, _C), lambda i: (0, 0, seg(i), 0, 0)))


def _chunk_pre(q_ref, k_ref, v_ref, bg_ref, gr_ref, c, bb, hh):
    r0 = pl.multiple_of(c * _C, _C)
    ri = lax.broadcasted_iota(jnp.int32, (_C, _C), 0)
    ci = lax.broadcasted_iota(jnp.int32, (_C, _C), 1)
    q = q_ref[bb, pl.ds(r0, _C), _hsl(hh)] * (DN_HD ** -0.5)
    k = k_ref[bb, pl.ds(r0, _C), _hsl(hh)]
    v = v_ref[bb, pl.ds(r0, _C), _hsl(hh)]
    bgc = bg_ref[bb, pl.ds(r0, _C), :]
    beta = bgc[:, hh:hh + 1]
    g_col = bgc[:, DN_HEADS + hh:DN_HEADS + hh + 1]
    g_row = gr_ref[bb, hh, pl.ds(c, 1), :]
    gc_col = jnp.sum(jnp.where(ri >= ci, g_row, 0.0), axis=1, keepdims=True)
    gc_row = jnp.sum(jnp.where(ri <= ci, g_col, 0.0), axis=0, keepdims=True)
    gc_last = jnp.sum(g_col, axis=0, keepdims=True)
    diff = gc_col - gc_row
    decay = jnp.where(ri >= ci, jnp.exp(jnp.where(ri >= ci, diff, 0.0)), 0.0)
    diff_t = gc_row - gc_col
    decay_t = jnp.where(ri <= ci, jnp.exp(jnp.where(ri <= ci, diff_t, 0.0)), 0.0)
    eg = jnp.exp(gc_col)
    et = jnp.exp(gc_last - gc_col)
    gl = jnp.exp(gc_last)
    kb = k * beta
    vb = v * beta
    return dict(r0=r0, bb=bb, hh=hh, q=q, k=k, v=v, beta=beta, decay=decay, decay_t=decay_t, eg=eg, et=et, gl=gl, kb=kb, vb=vb,
                ri=ri, ci=ci)


def _chunk_solve(ms, tms=None):
    for m in ms:
        m["kk_t"] = _dot(m["k"], m["kb"], _NT)
        m["qk"] = _dot(m["q"], m["k"], _NT)
        m["kk"] = _dot(m["kb"], m["k"], _NT)
        if tms is not None:
            m["qk_t"] = _dot(m["k"], m["q"], _NT)
    if tms is None:
        tms = _tri_inverses([jnp.where(m["ri"] > m["ci"], m["kk"] * m["decay"], 0.0) for m in ms],
                            [jnp.where(m["ri"] < m["ci"], m["kk_t"] * m["decay_t"], 0.0) for m in ms])
    for m, tm in zip(ms, tms):
        m["tm_f32"] = tm
    for m in ms:
        rhs = jnp.concatenate([m["vb"], m["kb"] * m["eg"]], axis=1)
        m["tm"] = _split(m["tm_f32"])
        m["sol"] = _dot3(m["tm"], rhs, _NN)
        m["intra"] = jnp.where(m["ri"] >= m["ci"], m["qk"] * m["decay"], 0.0)


def _dn_chunk_fwd(qkvn, bg, g_rows, bsz, plug=None):
    t = qkvn.shape[0]
    s = t // bsz
    nc, nseg = s // _C, s // _SEG
    pairs = [(bb, hh) for bb in range(bsz) for hh in _HEADS]

    def body(q_ref, k_ref, v_ref, bg_ref, gr_ref, o_ref, st_ref, tm_ref, s_ref):
        @pl.when(pl.program_id(0) == 0)
        def _():
            s_ref[...] = jnp.zeros_like(s_ref)

        def chunk(c, carry):
            ms = [_chunk_pre(q_ref, k_ref, v_ref, bg_ref, gr_ref, c, bb, hh) for bb, hh in pairs]
            _chunk_solve(ms)
            sts = [s_ref[i] for i in range(len(pairs))]
            for m, st in zip(ms, sts):
                st_ref[m["bb"], m["hh"], c] = st
                tm_ref[m["bb"], m["hh"], c] = m["tm_f32"]
            ws = [_dot(m["sol"][:, DN_HD:], st, _NN) for m, st in zip(ms, sts)]
            qs = [_dot(m["q"] * m["eg"], st, _NN) for m, st in zip(ms, sts)]
            v_new = [m["sol"][:, :DN_HD] - a for m, a in zip(ms, ws)]
            iv = [_dot(m["intra"], vn, _NN) for m, vn in zip(ms, v_new)]
            upd = [_dot(m["k"] * m["et"], vn, _TN) for m, vn in zip(ms, v_new)]
            for i, (bb, hh) in enumerate(pairs):
                s_ref[i] = sts[i] * ms[i]["gl"] + upd[i]
                o_ref[bb, pl.ds(ms[i]["r0"], _C), _hsl(hh)] = qs[i] + iv[i]
            return carry

        lax.fori_loop(0, _SEG // _C, chunk, 0)

    col, bgs, grs, sts_spec, tms_spec = _chunk_specs(bsz, nseg, False)
    q3, bg3 = qkvn.reshape(bsz, s, 3 * DN), bg.reshape(bsz, s, LANES)
    (o, states, tms), extra = _plugged_call(
        body, plug, _grid_ends((nseg,)), (q3, q3, q3, bg3, g_rows), name="dn_chunk_fwd", grid=(nseg,),
        in_specs=[col(0), col(1), col(2), bgs, grs], out_specs=[col(0), sts_spec, tms_spec],
        out_shape=[jax.ShapeDtypeStruct((bsz, s, DN), F32), jax.ShapeDtypeStruct((bsz, DN_HEADS, nc, DN_HD, DN_HD), F32),
                   jax.ShapeDtypeStruct((bsz, DN_HEADS, nc, _C, _C), F32)],
        scratch_shapes=[pltpu.VMEM((bsz * DN_HEADS, DN_HD, DN_HD), F32)])
    return o.reshape(t, DN), (states, tms), extra


def _dn_chunk_bwd(qkvn, bg, g_rows, states, do, bsz, plug=None):
    t = qkvn.shape[0]
    s = t // bsz
    nc, nseg = s // _C, s // _SEG
    pairs = [(bb, hh) for bb in range(bsz) for hh in _HEADS]

    def body(q_ref, k_ref, v_ref, bg_ref, gr_ref, st_ref, tm_ref, do_ref, dqkv_ref, dbg_ref, ds_ref):
        @pl.when(pl.program_id(0) == 0)
        def _():
            ds_ref[...] = jnp.zeros_like(ds_ref)

        def chunk(cc, carry):
            c = _SEG // _C - 1 - cc
            ms = [_chunk_pre(q_ref, k_ref, v_ref, bg_ref, gr_ref, c, bb, hh) for bb, hh in pairs]
            _chunk_solve(ms, [tm_ref[bb, hh, c] for bb, hh in pairs])
            ri, ci = ms[0]["ri"], ms[0]["ci"]
            for i, m in enumerate(ms):
                m["st"] = st_ref[m["bb"], m["hh"], c]
                m["ds_out"] = ds_ref[i]
                m["do"] = do_ref[m["bb"], pl.ds(m["r0"], _C), _hsl(m["hh"])]
                m["w"] = m["sol"][:, DN_HD:]
            for m in ms:
                m["v_new"] = m["sol"][:, :DN_HD] - _dot(m["w"], m["st"], _NN)
            for m in ms:
                m["q_dec"], m["k_tail"] = m["q"] * m["eg"], m["k"] * m["et"]
                m["dk_tail"] = _dot(m["v_new"], m["ds_out"], _NT)
                m["dv_new"] = _dot(m["k_tail"], m["ds_out"], _NN) + _dot(m["intra"], m["do"], _TN)
                m["dq_dec"] = _dot(m["do"], m["st"], _NT)
                m["ds_in"] = m["ds_out"] * m["gl"] + _dot(m["q_dec"], m["do"], _TN)
                m["dintra"] = jnp.where(ri >= ci, _dot(m["do"], m["v_new"], _NT), 0.0)
                m["dintra_t"] = jnp.where(ri <= ci, _dot(m["v_new"], m["do"], _NT), 0.0)
            for m in ms:
                m["dw"] = -_dot(m["dv_new"], m["st"], _NT)
                m["ds_in"] = m["ds_in"] - _dot(m["w"], m["dv_new"], _TN)
            for m in ms:
                dsol = jnp.concatenate([m["dv_new"], m["dw"]], axis=1)
                m["drhs"] = _dot3(m["tm"], dsol, _TN)
            for m in ms:
                m["dl"] = jnp.where(ri > ci, -_dot(m["drhs"], m["sol"], _NT), 0.0)
                m["dl_t"] = jnp.where(ri < ci, -_dot(m["sol"], m["drhs"], _NT), 0.0)
            for m in ms:
                m["dkb2"] = _dot(m["dl"] * m["decay"], m["k"], _NN)
                m["dk"] = _dot(m["dl_t"] * m["decay_t"], m["kb"], _NN) + _dot(m["dintra_t"] * m["decay_t"], m["q"], _NN)
                m["dq"] = _dot(m["dintra"] * m["decay"], m["k"], _NN)
            for m in ms:
                _chunk_bwd_finish(m)
            for m in ms:
                ones_ge = jnp.where(ri <= ci, 1.0, 0.0).astype(BF16)
                gh, gl_ = _split(m["dgc"] + jnp.zeros((_C, LANES), F32))
                m["dg_b"] = jnp.dot(ones_ge, gh, preferred_element_type=F32) + jnp.dot(ones_ge, gl_, preferred_element_type=F32)
            lane = lax.broadcasted_iota(jnp.int32, (_C, LANES), 1)
            for i, m in enumerate(ms):
                bb, hh, rows = m["bb"], m["hh"], pl.ds(m["r0"], _C)
                dqkv_ref[0, bb, rows, _hsl(hh)] = m["dq"] * (DN_HD ** -0.5)
                dqkv_ref[1, bb, rows, _hsl(hh)] = m["dk"]
                dqkv_ref[2, bb, rows, _hsl(hh)] = m["dv"]
                dbg_ref[bb, rows, _hsl(hh)] = jnp.where(lane == 0, m["dbeta"], jnp.where(lane == 1, m["dg_b"], 0.0))
                ds_ref[i] = m["ds_in"]
            return carry

        lax.fori_loop(0, _SEG // _C, chunk, 0)

    col, bgs, grs, sts_spec, tms_spec = _chunk_specs(bsz, nseg, True)
    q3, bg3, do3 = qkvn.reshape(bsz, s, 3 * DN), bg.reshape(bsz, s, LANES), do.reshape(bsz, s, DN)
    (dqkv, dbg), extra = _plugged_call(
        body, plug, _grid_ends((nseg,)), (q3, q3, q3, bg3, g_rows, *states, do3), name="dn_chunk_bwd", grid=(nseg,),
        in_specs=[col(0), col(1), col(2), bgs, grs, sts_spec, tms_spec, col(0)],
        out_specs=[_vspec((3, bsz, _SEG, DN), lambda i: (0, 0, nseg - 1 - i, 0)), col(0)],
        out_shape=[jax.ShapeDtypeStruct((3, bsz, s, DN), F32), jax.ShapeDtypeStruct((bsz, s, DN), F32)],
        scratch_shapes=[pltpu.VMEM((bsz * DN_HEADS, DN_HD, DN_HD), F32)])
    return dqkv.reshape(3, t, DN), dbg.reshape(t, DN), extra


def _chunk_bwd_finish(m):
    q, k, v, beta, decay, decay_t = m["q"], m["k"], m["v"], m["beta"], m["decay"], m["decay_t"]
    eg, et, gl, kb, dl, dl_t, dintra, dintra_t = m["eg"], m["et"], m["gl"], m["kb"], m["dl"], m["dl_t"], m["dintra"], m["dintra_t"]
    dq_dec, dk_tail, dq, dk = m["dq_dec"], m["dk_tail"], m["dq"], m["dk"]
    dgl = jnp.sum(jnp.sum(m["ds_out"] * m["st"], axis=1, keepdims=True), axis=0, keepdims=True)
    dvb, dkbeg = m["drhs"][:, :DN_HD], m["drhs"][:, DN_HD:]
    dkb = dkbeg * eg + m["dkb2"]
    deg = jnp.sum(dkbeg * kb, axis=1, keepdims=True)
    em = (dl * m["kk"] + dintra * m["qk"]) * decay
    em_t = (dl_t * m["kk_t"] + dintra_t * m["qk_t"]) * decay_t
    dgc = jnp.sum(em, axis=1, keepdims=True) - jnp.sum(em_t, axis=1, keepdims=True)
    dq = dq + dq_dec * eg
    deg = deg + jnp.sum(dq_dec * q, axis=1, keepdims=True)
    dk = dk + dk_tail * et
    det = jnp.sum(dk_tail * k, axis=1, keepdims=True)
    dgc = dgc + deg * eg - det * et
    dgc_last = jnp.sum(det * et, axis=0, keepdims=True) + dgl * gl
    rcol = lax.broadcasted_iota(jnp.int32, (_C, 1), 0)
    m["dgc"] = dgc + jnp.where(rcol == _C - 1, dgc_last, 0.0)
    m["dq"] = dq
    m["dk"] = dk + dkb * beta
    m["dbeta"] = jnp.sum(dkb * k, axis=1, keepdims=True) + jnp.sum(dvb * v, axis=1, keepdims=True)
    m["dv"] = dvb * beta


def _mod_fwd(c_all, ada_w_loc, ada_b_loc):
    n, cols = c_all.shape[0], ada_w_loc.shape[1]

    def body(c_ref, w_ref, b_ref, o_ref):
        o_ref[...] = _dot(jax.nn.silu(c_ref[...]), w_ref[...], _NN) + b_ref[...]

    return pl.pallas_call(body, name="mod_fwd", out_shape=jax.ShapeDtypeStruct((n, cols), F32))(c_all, ada_w_loc, ada_b_loc)


def _ada_grad(c_all, dmod_loc, dmod_all):
    d, cols = c_all.shape[1], dmod_loc.shape[1]

    def body(c_ref, dl_ref, da_ref, gw_ref, gb_ref):
        gw_ref[...] = _dot(jax.nn.silu(c_ref[...]), dl_ref[...], _TN)
        gb_ref[...] = jnp.sum(da_ref[...], axis=0, keepdims=True)

    return pl.pallas_call(body, name="ada_grad", out_shape=[jax.ShapeDtypeStruct((d, cols), F32),
                                                           jax.ShapeDtypeStruct((1, dmod_all.shape[1]), F32)])(c_all, dmod_loc, dmod_all)


ELEMENTWISE_BLOCK_BYTES = 3 * 2 ** 19


def _row_tile(r, c=1024):
    fits = [tr for tr in range(16, r + 1, 16) if tr * c * 4 <= ELEMENTWISE_BLOCK_BYTES]
    if not fits:
        return r
    whole = [tr for tr in fits if r % tr == 0]
    return whole[-1] if whole else fits[-1]


def _adamw(w, m, v, grads, name, plug=None):
    r, rest = w.shape[0], w.shape[1:]
    c = math.prod(rest)
    tr = _row_tile(r, c)
    blk = _vspec((tr,) + rest, lambda i: (i,) + (0,) * len(rest))
    n = len(grads)

    def body(*refs):
        w_ref, m_ref, v_ref = refs[:3]
        g_ref, d_ref, mo_ref, vo_ref = refs[3 + n:]
        g = refs[3][...]
        for p in refs[4:3 + n]:
            g = g + p[...]
        g_ref[...] = g
        d_ref[...], mo_ref[...], vo_ref[...] = _adamw_math(w_ref[...], m_ref[...], v_ref[...], g)

    o = jax.ShapeDtypeStruct(w.shape, F32)
    grid = (pl.cdiv(r, tr),)
    res, extra = _plugged_call(body, plug, _grid_ends(grid), (w, m, v, *grads), name=name, grid=grid, in_specs=[blk] * (3 + n),
                               out_specs=[blk] * 4, out_shape=[o] * 4)
    return res if plug is None else (res, extra)


def _adamw_halves(w, m, v, own, other, ic, name):
    if w.ndim == 3:
        r, _, c = w.shape
        tr = _row_tile(r, c)
        blk = _vspec((tr, 1, c // 2), lambda i, j: (i, 0, j))
        half = _vspec((tr, 1, c // 2), lambda i, j: (i, 0, 0))
        grid = (pl.cdiv(r, tr), 2)
    else:
        r, c = w.shape
        tr = _row_tile(r // 2, c)
        per = r // 2 // tr
        blk = _vspec((tr, c), lambda i, j: (j * per + i, 0))
        half = _vspec((tr, c), lambda i, j: (i, 0))
        grid = (per, 2)

    def body(ic_ref, w_ref, m_ref, v_ref, own_ref, other_ref, g_ref, d_ref, mo_ref, vo_ref):
        g = jnp.where(pl.program_id(1) == ic_ref[0], own_ref[...], other_ref[...])
        g_ref[...] = g
        d_ref[...], mo_ref[...], vo_ref[...] = _adamw_math(w_ref[...], m_ref[...], v_ref[...], g)

    o = jax.ShapeDtypeStruct(w.shape, F32)
    return pl.pallas_call(body, name=name, grid=grid,
                          in_specs=[pl.BlockSpec(memory_space=pltpu.SMEM), blk, blk, blk, half, half], out_specs=[blk] * 4,
                          out_shape=[o] * 4)(ic, w, m, v, own, other)


def _adamw_math(w, m, v, g):
    m_new = ADAM_B1 * m + (1.0 - ADAM_B1) * g
    v_new = ADAM_B2 * v + (1.0 - ADAM_B2) * jnp.square(g)
    m_hat = m_new / (1.0 - ADAM_B1 ** ADAM_STEP)
    v_hat = v_new / (1.0 - ADAM_B2 ** ADAM_STEP)
    return -ADAM_LR * (m_hat / (jnp.sqrt(v_hat) + ADAM_EPS) + ADAM_WD * w), m_new, v_new


def _adamw_small(ws, ms, vs, gs, name):
    n = len(ws)

    def body(*refs):
        for i in range(n):
            w_ref, m_ref, v_ref, g_ref = (refs[j * n + i] for j in range(4))
            go_ref, d_ref, mo_ref, vo_ref = refs[4 * n + 4 * i:4 * n + 4 * i + 4]
            g = g_ref[...]
            go_ref[...] = g
            d_ref[...], mo_ref[...], vo_ref[...] = _adamw_math(w_ref[...], m_ref[...], v_ref[...], g)

    res = pl.pallas_call(body, name=name, out_shape=[jax.ShapeDtypeStruct(a.shape, F32) for a in ws for _ in range(4)])(
        *ws, *ms, *vs, *gs)
    return [res[4 * i:4 * i + 4] for i in range(n)]


def _sum_lead(x, name, rows_apart=False):
    p, r, c = x.shape
    tr = _row_tile(r, c)
    mid = (1,) if rows_apart else ()

    def body(x_ref, o_ref):
        acc = x_ref[0].astype(F32)
        for i in range(1, p):
            acc = acc + x_ref[i].astype(F32)
        o_ref[...] = acc.reshape(o_ref.shape)

    return pl.pallas_call(body, name=name, grid=(pl.cdiv(r, tr),), in_specs=[_vspec((p, tr, c), lambda i: (0, i, 0))],
                          out_specs=_vspec((tr,) + mid + (c,), lambda i: (i,) + (0,) * (1 + len(mid))),
                          out_shape=jax.ShapeDtypeStruct((r,) + mid + (c,), F32))(x)


def _allgather8(x_shard, name, plug=None):
    m_per, n = x_shard.shape

    def body(x_ref, out_ref, send_sems, recv_sems, local_sem):
        x, y, c = lax.axis_index("x"), lax.axis_index("y"), lax.axis_index("c")
        me, sibling = (x, y, c), (x, y, 1 - c)
        chips = [(1 - x, y), (x, 1 - y), (1 - x, 1 - y)]

        def rows(px, py, pc):
            return out_ref.at[pl.ds((4 * px + 2 * py + pc) * m_per, m_per), :]

        def copy(k, block, to, src=None):
            return pltpu.make_async_remote_copy(
                src_ref=rows(*block) if src is None else src, dst_ref=rows(*block), send_sem=send_sems.at[k],
                recv_sem=recv_sems.at[k], device_id=to, device_id_type=MESH)

        mine = pltpu.make_async_copy(x_ref, rows(*me), local_sem)
        mine.start()
        first = [copy(0, me, sibling, src=x_ref)]
        first += [copy(1 + j, me, (*chip, c), src=x_ref) for j, chip in enumerate(chips)]
        for cp in first:
            cp.start()
        passed = [copy(4 + j, (*chip, c), sibling) for j, chip in enumerate(chips)]
        for j, chip in enumerate(chips):
            copy(1 + j, (*chip, c), me).wait_recv()
            passed[j].start()
        copy(0, sibling, me).wait_recv()
        for j, chip in enumerate(chips):
            copy(4 + j, (*chip, 1 - c), me).wait_recv()
        for cp in first + passed:
            cp.wait_send()
        mine.wait()

    grid = (1,)
    (out,), extra = _plugged_call(
        body, plug, _grid_ends(grid), (x_shard,), name=name, grid=grid, out_shape=[jax.ShapeDtypeStruct((8 * m_per, n), x_shard.dtype)],
        in_specs=[pl.BlockSpec(memory_space=pltpu.VMEM)], out_specs=[pl.BlockSpec(memory_space=pltpu.VMEM)],
        scratch_shapes=[pltpu.SemaphoreType.DMA((7,)), pltpu.SemaphoreType.DMA((7,)), pltpu.SemaphoreType.DMA])
    return out if plug is None else (out, extra)


_HBM = pl.BlockSpec(memory_space=pltpu.HBM)


def _mesh_place():
    x, y, c = lax.axis_index("x"), lax.axis_index("y"), lax.axis_index("c")
    return x, y, c, 2 * x + y, [(1 - x, y), (x, 1 - y), (1 - x, 1 - y)]


def _gather_plug(shards):
    n = len(shards)

    def half(ref, c, lead=None):
        r, cols = ref.shape[-2] // 2, ref.shape[-1] // 2
        if r % 16 == 0:
            rows = pl.ds(pl.multiple_of(c * r, 16), r)
            return ref.at[rows, :] if lead is None else ref.at[lead, rows, :]
        lanes = pl.ds(pl.multiple_of(c * cols, LANES), cols)
        return ref.at[:, lanes] if lead is None else ref.at[lead, :, lanes]

    def copies(ins, outs, send, recv):
        x, y, c, me, chips = _mesh_place()
        ici, fwd, fwd_in = [], [], []
        for i in range(n):
            for j, (px, py) in enumerate(chips):
                q = 2 * px + py
                ici.append((pltpu.make_async_remote_copy(
                    src_ref=half(ins[i], c), dst_ref=half(outs[i], c, me), send_sem=send.at[6 * i + j], recv_sem=recv.at[6 * i + j],
                    device_id=(px, py, c), device_id_type=MESH),
                    pltpu.make_async_remote_copy(
                    src_ref=half(ins[i], c), dst_ref=half(outs[i], c, q), send_sem=send.at[6 * i + j], recv_sem=recv.at[6 * i + j],
                    device_id=(px, py, c), device_id_type=MESH)))
                fwd.append(pltpu.make_async_remote_copy(
                    src_ref=half(outs[i], c, q), dst_ref=half(outs[i], c, q), send_sem=send.at[6 * i + 3 + j],
                    recv_sem=recv.at[6 * i + 3 + j], device_id=(x, y, 1 - c), device_id_type=MESH))
                fwd_in.append(pltpu.make_async_remote_copy(
                    src_ref=half(outs[i], 1 - c, q), dst_ref=half(outs[i], 1 - c, q), send_sem=send.at[6 * i + 3 + j],
                    recv_sem=recv.at[6 * i + 3 + j], device_id=(x, y, 1 - c), device_id_type=MESH))
        return ici, fwd, fwd_in, me

    def start(ins, outs, send, recv, loc):
        ici, _, _, me = copies(ins, outs, send, recv)
        for i in range(n):
            pltpu.make_async_copy(ins[i], outs[i].at[me], loc.at[i]).start()
        for out_cp, _ in ici:
            out_cp.start()

    def finish(ins, outs, send, recv, loc):
        ici, fwd, fwd_in, me = copies(ins, outs, send, recv)
        for (_, in_cp), f in zip(ici, fwd):
            in_cp.wait_recv()
            f.start()
        for f in fwd_in:
            f.wait_recv()
        for (out_cp, _), f in zip(ici, fwd):
            out_cp.wait_send()
            f.wait_send()
        for i in range(n):
            pltpu.make_async_copy(ins[i], outs[i].at[me], loc.at[i]).wait()

    return dict(ins=list(shards), out_shape=[jax.ShapeDtypeStruct((N_CHIPS,) + a.shape, a.dtype) for a in shards],
                scratch=[pltpu.SemaphoreType.DMA((6 * n,)), pltpu.SemaphoreType.DMA((6 * n,)), pltpu.SemaphoreType.DMA((n,))],
                start=start, finish=finish)


def _exchange_plug(pieces):
    n = len(pieces)

    def copies(ins, outs, send, recv):
        x, y, c, me, chips = _mesh_place()
        out_cps, in_cps = [], []
        for i in range(n):
            for j, (px, py) in enumerate(chips):
                q = 2 * px + py
                out_cps.append(pltpu.make_async_remote_copy(src_ref=ins[i].at[q], dst_ref=outs[i].at[me], send_sem=send.at[3 * i + j],
                                                            recv_sem=recv.at[3 * i + j], device_id=(px, py, c), device_id_type=MESH))
                in_cps.append(pltpu.make_async_remote_copy(src_ref=ins[i].at[me], dst_ref=outs[i].at[q], send_sem=send.at[3 * i + j],
                                                           recv_sem=recv.at[3 * i + j], device_id=(px, py, c), device_id_type=MESH))
        return out_cps, in_cps, me

    def start(ins, outs, send, recv, loc):
        out_cps, _, me = copies(ins, outs, send, recv)
        for i in range(n):
            pltpu.make_async_copy(ins[i].at[me], outs[i].at[me], loc.at[i]).start()
        for cp in out_cps:
            cp.start()

    def finish(ins, outs, send, recv, loc):
        out_cps, in_cps, me = copies(ins, outs, send, recv)
        for cp in in_cps:
            cp.wait_recv()
        for cp in out_cps:
            cp.wait_send()
        for i in range(n):
            pltpu.make_async_copy(ins[i].at[me], outs[i].at[me], loc.at[i]).wait()

    return dict(ins=list(pieces), out_shape=[jax.ShapeDtypeStruct(a.shape, a.dtype) for a in pieces],
                scratch=[pltpu.SemaphoreType.DMA((3 * n,)), pltpu.SemaphoreType.DMA((3 * n,)), pltpu.SemaphoreType.DMA((n,))],
                start=start, finish=finish)


def _plugged_call(body, plug, first_last, args, *, name, grid, in_specs, out_specs, out_shape, scratch_shapes=(), aliases=None):
    in_specs, out_specs, out_shape, scratch_shapes = list(in_specs), list(out_specs), list(out_shape), list(scratch_shapes)
    aliases = dict(aliases or {})
    if plug is None:
        return pl.pallas_call(body, name=name, grid=grid, in_specs=in_specs, out_specs=out_specs, out_shape=out_shape,
                              scratch_shapes=scratch_shapes, input_output_aliases=aliases)(*args), []
    n_in, n_out, n_sc = len(in_specs), len(out_specs), len(scratch_shapes)
    p_in, p_out = len(plug["ins"]), len(plug["out_shape"])

    def full(*refs):
        ins, refs = refs[:n_in], refs[n_in:]
        pins, refs = refs[:p_in], refs[p_in:]
        outs, refs = refs[:n_out], refs[n_out:]
        pouts, refs = refs[:p_out], refs[p_out:]
        scr, psems = refs[:n_sc], refs[n_sc:]
        first, last = first_last()

        @pl.when(first)
        def _():
            plug["start"](pins, pouts, *psems)

        body(*ins, *outs, *scr)

        @pl.when(last)
        def _():
            plug["finish"](pins, pouts, *psems)

    res = pl.pallas_call(full, name=name, grid=grid, in_specs=in_specs + [_HBM] * p_in, out_specs=out_specs + [_HBM] * p_out,
                         out_shape=out_shape + plug["out_shape"], scratch_shapes=scratch_shapes + plug["scratch"],
                         input_output_aliases=aliases)(*args, *plug["ins"])
    return res[:n_out], res[n_out:]


def _grid_ends(grid):
    def ends():
        first = last = None
        for ax, n in enumerate(grid):
            i = pl.program_id(ax)
            first = (i == 0) if first is None else first & (i == 0)
            last = (i == n - 1) if last is None else last & (i == n - 1)
        return first, last
    return ends


def _pair_presum(pieces, name):
    n, r, cols = pieces.shape
    by_cols = (cols // 2) % LANES == 0
    half_shape = (n, r, cols // 2) if by_cols else (n, r // 2, cols)

    def body(p_ref, o_ref, mine_ref, land_ref, send_sem, recv_sem, local_sem):
        x, y, c = lax.axis_index("x"), lax.axis_index("y"), lax.axis_index("c")

        def half(which):
            if by_cols:
                return p_ref.at[:, :, pl.ds(pl.multiple_of(which * (cols // 2), LANES), cols // 2)]
            return p_ref.at[:, pl.ds(pl.multiple_of(which * (r // 2), 16), r // 2), :]

        push = pltpu.make_async_remote_copy(src_ref=half(1 - c), dst_ref=land_ref, send_sem=send_sem, recv_sem=recv_sem,
                                            device_id=(x, y, 1 - c), device_id_type=MESH)
        own = pltpu.make_async_copy(half(c), mine_ref, local_sem)
        push.start()
        own.start()
        own.wait()
        push.wait_recv()
        for i in range(n):
            o_ref[i] = (mine_ref[i].astype(F32) + land_ref[i].astype(F32)).astype(BF16)
        push.wait_send()

    return pl.pallas_call(
        body, name=name, out_shape=jax.ShapeDtypeStruct(half_shape, BF16), in_specs=[_HBM],
        out_specs=pl.BlockSpec(memory_space=pltpu.VMEM),
        scratch_shapes=[pltpu.VMEM(half_shape, BF16), pltpu.VMEM(half_shape, BF16), pltpu.SemaphoreType.DMA,
                        pltpu.SemaphoreType.DMA, pltpu.SemaphoreType.DMA],
    )(pieces)


def _sibling_plug(arrs):
    n = len(arrs)

    def copies(ins, outs, send, recv):
        sibling = (lax.axis_index("x"), lax.axis_index("y"), 1 - lax.axis_index("c"))
        return [pltpu.make_async_remote_copy(src_ref=ins[i], dst_ref=outs[i], send_sem=send.at[i], recv_sem=recv.at[i],
                                             device_id=sibling, device_id_type=MESH) for i in range(n)]

    def start(ins, outs, send, recv):
        for cp in copies(ins, outs, send, recv):
            cp.start()

    def finish(ins, outs, send, recv):
        for cp in copies(ins, outs, send, recv):
            cp.wait()

    return dict(ins=list(arrs), out_shape=[jax.ShapeDtypeStruct(a.shape, a.dtype) for a in arrs],
                scratch=[pltpu.SemaphoreType.DMA((n,)), pltpu.SemaphoreType.DMA((n,))], start=start, finish=finish)


_SEGS = ((0, AQ, C_Q), (AQ, AKV, C_K), (AQ + AKV, AKV, C_V), (AQ + 2 * AKV, 3 * DN, C_DQKV), (2304, DN, C_DZ),
         (2816, 2 * DN_HEADS, C_BD), (2824, D_MODEL, C_GA), (3848, D_MODEL, C_GD))
SHARD_ROWS = IN_DIM // N_CHIPS


def _to_padded(w4):
    wt = w4.reshape(IN_DIM, w4.shape[2])
    parts = [wt[o:o + n] for o, n, _ in sorted(_SEGS, key=lambda sg: sg[2])]
    return jnp.concatenate(parts + [jnp.zeros((IN_PAD - IN_DIM, wt.shape[1]), wt.dtype)], axis=0)


def _from_padded(gt):
    return jnp.concatenate([gt[ps:ps + n] for _, n, ps in sorted(_SEGS)], axis=0).reshape(N_CHIPS, SHARD_ROWS, gt.shape[1])


def _lane_vec(a):
    return jnp.zeros((1, LANES), F32).at[0, DN_HEADS:2 * DN_HEADS].set(a)


_ROW_SHARDED = ("w_in", "w_out", "ffn_w_down")
_FFN = ("ffn_w_up", "ffn_w_down")
_LATE_MIXER = ("w_attn_branch", "w_dn_branch", "w_out")
_PRESUM = ("w_in", "ffn_w_up")


def _pieces(k, a):
    if a.ndim == 3:
        return a
    if k in _ROW_SHARDED:
        return a.reshape(N_CHIPS, a.shape[0] // N_CHIPS, a.shape[1]).astype(BF16)
    return jnp.transpose(a.reshape(a.shape[0], N_CHIPS, a.shape[1] // N_CHIPS), (1, 0, 2)).astype(BF16)


def _assemble(k, a):
    if k in _ROW_SHARDED:
        return a.reshape(-1, a.shape[2])
    return jnp.transpose(a, (1, 0, 2)).reshape(a.shape[1], -1)


def _device_step(x2, tgt2, mod, p, bsz, shards=None):
    d = D_MODEL
    on_mesh = shards is not None
    p = dict(p)
    sh1, sc1, g1, sh2, sc2, g2 = [mod[:, i * d:(i + 1) * d].reshape(bsz, 1, d) for i in range(N_MOD)]
    alog_v, dt_v = _lane_vec(p["dn_a_log"]), _lane_vec(p["dn_dt_bias"])
    sinks = p["attn_sinks"].reshape(1, AQ_HEADS)
    if on_mesh:
        u1, got = _pre_fwd(x2, p["norm_mix_pre"], sc1, sh1, "pre1_fwd", plug=_gather_plug(shards["w_in_last"]))
        p["w_in"] = _to_padded(jnp.concatenate(list(shards["w_in_gathered"]) + list(got), axis=1))
    else:
        u1 = _pre_fwd(x2, p["norm_mix_pre"], sc1, sh1, "pre1_fwd")
    if on_mesh:
        proj, got = _mm(u1, p["w_in"], "nt", F32, "mm_proj", _gather_plug(shards["late_mixer"]))
        p.update({k: _assemble(k, a) for k, a in zip(_LATE_MIXER, got)})
    else:
        proj = _mm(u1, p["w_in"], "nt", F32, "mm_proj")
    bias = _bias_build(p["rel_bias"])
    y_attn = _attn_fwd(proj, bias, sinks, bsz)
    qkvn, dn_conv_out = _dn_prep_fwd(proj, p["dn_conv_w"], bsz)
    bg = _bg_fwd(proj, alog_v, dt_v, bsz)
    nc = x2.shape[0] // bsz // DN_CHUNK
    g_rows = jnp.transpose(bg[:, DN_HEADS:2 * DN_HEADS].reshape(bsz, nc, DN_CHUNK, DN_HEADS), (0, 3, 1, 2))
    o, states, got = _dn_chunk_fwd(qkvn, bg, g_rows, bsz, _gather_plug(shards["ffn"][:1]) if on_mesh else None)
    for k, a in zip(_FFN[:1], got):
        p[k] = _assemble(k, a)
    y_dn = _dn_out_fwd(o, proj, p["dn_norm_w"])
    ya = _mm(y_attn, p["w_attn_branch"], "nn", BF16, "mm_ya")
    yd = _mm(y_dn, p["w_dn_branch"], "nn", BF16, "mm_yd")
    merged = _merge_fwd(proj, ya, yd)
    y1 = _mm(merged, p["w_out"], "nn", F32, "mm_y1")
    h1, u2 = _post_pre_fwd(x2, y1, p["norm_mix_post"], g1, p["norm_ffn_pre"], sc2, sh2, "post1_pre2_fwd")
    if on_mesh:
        up, got = _mm(u2, p["ffn_w_up"], "nn", BF16, "mm_up", _gather_plug(shards["ffn"][1:]))
        p["ffn_w_down"] = _assemble("ffn_w_down", got[0])
    else:
        up = _mm(u2, p["ffn_w_up"], "nn", BF16, "mm_up")
    act, conv_g, conv_v = _ffn_act_fwd(up, p["ffn_conv_w"], bsz)
    y2 = _mm(act, p["ffn_w_down"], "nn", F32, "mm_y2")
    dh2, dy2, g_ffn_post, dg2, sq = _post_loss_bwd(h1, y2, p["norm_ffn_post"], g2, tgt2, "post2_loss_bwd")
    g = {}
    g["norm_ffn_post"] = g_ffn_post
    dact = _mm(dy2, p["ffn_w_down"], "nt", BF16, "mm_dact")
    g["ffn_w_down"] = _mm(act, dy2, "tn", BF16, "mm_dwdown")
    dupg, dupv, dcwg, dcwv, got_down = _ffn_act_bwd(
        up, p["ffn_conv_w"], (conv_g, conv_v), dact, bsz,
        _exchange_plug([_pieces("ffn_w_down", g["ffn_w_down"])]) if on_mesh else None)
    g["ffn_conv_w"] = jnp.concatenate([dcwg, dcwv], axis=1)
    g["ffn_w_up"] = jnp.concatenate([_mm(u2, dupg, "tn", BF16, "mm_dwup_gate", split=N_CHIPS // 2),
                                     _mm(u2, dupv, "tn", BF16, "mm_dwup_val", split=N_CHIPS // 2)], axis=0)
    du2 = _mm(dupg, p["ffn_w_up"], "nt", F32, "mm_du2_gate", b_kblock=0)
    du2 = _mm(dupv, p["ffn_w_up"], "nt", F32, "mm_du2_val", b_kblock=1, add=du2)
    dh1, dy1, g["norm_ffn_pre"], dsc2, dsh2, g["norm_mix_post"], dg1 = _pre_post_bwd(
        h1, p["norm_ffn_pre"], sc2, sh2, du2, dh2, y1, p["norm_mix_post"], g1, "pre2_post1_bwd")
    dmerged = _mm(dy1, p["w_out"], "nt", BF16, "mm_dmerged")
    g["w_out"] = _mm(merged, dy1, "tn", BF16, "mm_dwout")
    dproj = lax.empty((x2.shape[0], IN_PAD), BF16)
    dproj, dya = _branch_bwd(dproj, proj, ya, dmerged, C_GA, "merge_bwd_attn")
    dproj, dyd = _branch_bwd(dproj, proj, yd, dmerged, C_GD, "merge_bwd_dn")
    dy_attn = _mm(dya, p["w_attn_branch"], "nt", BF16, "mm_dyattn")
    g["w_attn_branch"] = _mm(y_attn, dya, "tn", BF16, "mm_dwab", split=N_CHIPS)
    dy_dn = _mm(dyd, p["w_dn_branch"], "nt", BF16, "mm_dydn")
    g["w_dn_branch"] = _mm(y_dn, dyd, "tn", BF16, "mm_dwdb", split=N_CHIPS)
    do, dproj, g["dn_norm_w"] = _dn_out_bwd(dproj, o, proj, p["dn_norm_w"], dy_dn)
    def plug_for(names):
        if not on_mesh:
            return None
        return _exchange_plug([_pair_presum(_pieces(k, g[k]), "presum_" + k) if k in _PRESUM else _pieces(k, g[k]) for k in names])

    early = ("w_out", "w_attn_branch", "w_dn_branch")
    dqkvn, dbg4, got_up = _dn_chunk_bwd(qkvn, bg, g_rows, states, do, bsz, plug_for(_FFN[:1]))
    got_ffn = list(got_up) + list(got_down)
    dproj, g["dn_conv_w"] = _dn_prep_bwd(dproj, proj, p["dn_conv_w"], dn_conv_out, dqkvn, bsz)
    dproj, dk, dv, dbias, g["attn_sinks"], got_early = _attn_bwd(dproj, proj, bias, sinks, dy_attn, bsz, plug_for(early))
    dproj, g["dn_a_log"], g["dn_dt_bias"] = _bg_bwd(dproj, proj, alog_v, dt_v, dbg4, dk, dv, bsz)
    g["rel_bias"] = _bias_grad(dbias)
    g["w_in"] = _from_padded(_mm(dproj, u1, "tn", BF16, "mm_dwin"))
    if on_mesh:
        du1, got_in = _mm(dproj, p["w_in"], "nn", F32, "mm_du1", plug_for(("w_in",)))
        g.update(zip(_FFN + early + ("w_in",), list(got_ffn) + list(got_early) + list(got_in)))
    else:
        du1 = _mm(dproj, p["w_in"], "nn", F32, "mm_du1")
    dx, g["norm_mix_pre"], dsc1, dsh1 = _pre_bwd(x2, p["norm_mix_pre"], sc1, sh1, du1, dh1, "pre1_bwd")
    dmod = jnp.concatenate([dsh1, dsc1, dg1, dsh2, dsc2, dg2], axis=-1).reshape(bsz, N_MOD * d)
    return sq, dx, dmod, g


_SMALL = (("norm_mix_pre", D_MODEL), ("norm_mix_post", D_MODEL), ("norm_ffn_pre", D_MODEL), ("norm_ffn_post", D_MODEL),
          ("dn_norm_w", DN_HD), ("dn_a_log", LANES), ("dn_dt_bias", LANES), ("attn_sinks", AQ_HEADS * LANES),
          ("rel_bias", AQ_HEADS * LANES), ("dn_conv_w", DN_CONV * 3 * DN), ("ffn_conv_w", FFN_CONV * 2 * D_FF),
          ("loss_sq", D_MODEL))


def _pack_rows(parts, rows):
    flat = jnp.concatenate([a.reshape(-1) for a in parts])
    return jnp.concatenate([flat, jnp.zeros((rows * LANES - flat.shape[0],), F32)]).reshape(rows, LANES)


_W_NAMES = ("ada_w", "ada_b", "norm_mix_pre", "norm_mix_post", "norm_ffn_pre", "norm_ffn_post", "w_in", "dn_conv_w", "dn_a_log",
            "dn_dt_bias", "dn_norm_w", "attn_sinks", "rel_bias", "w_attn_branch", "w_dn_branch", "w_out", "ffn_w_up", "ffn_conv_w",
            "ffn_w_down")
_BIG = ("w_in", "w_attn_branch", "w_dn_branch", "w_out", "ffn_w_up", "ffn_w_down")


def kernel(x, c, *rest):
    nw = len(_W_NAMES)
    w = dict(zip(_W_NAMES, rest[:nw]))
    loss_target = rest[nw]
    m = dict(zip(_W_NAMES, rest[nw + 1:2 * nw + 1]))
    v = dict(zip(_W_NAMES, rest[2 * nw + 1:3 * nw + 1]))
    ix, iy, ic = lax.axis_index("x"), lax.axis_index("y"), lax.axis_index("c")
    chip, dev = 2 * ix + iy, 4 * ix + 2 * iy + ic
    bsz, s, d = x.shape
    t = bsz * s
    n_dev = 8

    front_rows = 64
    front = _pack_rows([c, w["dn_conv_w"], w["ffn_conv_w"]], front_rows)
    w_in_t = jnp.swapaxes(w["w_in"][0], 0, 1).astype(BF16)
    cut1 = SHARD_ROWS // 48 * 16
    cut2 = 2 * cut1
    front_all, (w_in_lo,) = _allgather8(front, "ag_front", _gather_plug([w_in_t[:cut1]]))
    front_all = front_all.reshape(n_dev, front_rows * LANES)
    n_c, n_dc, n_fc = bsz * d, DN_CONV * 3 * DN // N_CHIPS, FFN_CONV * 2 * D_FF // N_CHIPS
    c_all = front_all[:, :n_c].reshape(n_dev * bsz, d)
    per_chip = front_all[0::2]
    dn_conv_full = jnp.transpose(per_chip[:, n_c:n_c + n_dc].reshape(N_CHIPS, DN_CONV, -1), (1, 0, 2)).reshape(DN_CONV, 3 * DN)
    ffn_conv_full = jnp.transpose(per_chip[:, n_c + n_dc:n_c + n_dc + n_fc].reshape(N_CHIPS, FFN_CONV, -1), (1, 0, 2)).reshape(FFN_CONV, 2 * D_FF)

    mod_cols = N_MOD * d // N_CHIPS
    ada_b_loc = lax.dynamic_slice(w["ada_b"], (0, chip * mod_cols), (1, mod_cols))
    mod_part = _mod_fwd(c_all, w["ada_w"][0], ada_b_loc)
    mod_all, (w_in_mid,) = _allgather8(mod_part, "ag_mod", _gather_plug([w_in_t[cut1:cut2]]))
    mod_all = mod_all.reshape(n_dev, n_dev * bsz, mod_cols)[0::2]
    mod = jnp.transpose(lax.dynamic_slice(mod_all, (0, dev * bsz, 0), (N_CHIPS, bsz, mod_cols)), (1, 0, 2)).reshape(bsz, N_MOD * d)

    p = {}
    shards = {"late_mixer": [w[k][0].astype(BF16) for k in _LATE_MIXER], "ffn": [w[k][0].astype(BF16) for k in _FFN],
              "w_in_gathered": [w_in_lo, w_in_mid], "w_in_last": [w_in_t[cut2:]]}
    for k in ("norm_mix_pre", "norm_mix_post", "norm_ffn_pre", "norm_ffn_post", "dn_norm_w", "attn_sinks"):
        p[k] = w[k]
    p["dn_a_log"], p["dn_dt_bias"], p["rel_bias"] = w["dn_a_log"][0], w["dn_dt_bias"][0], w["rel_bias"]
    p["dn_conv_w"], p["ffn_conv_w"] = dn_conv_full, ffn_conv_full

    sq, dx, dmod, g = _device_step(x.reshape(t, d), loss_target.reshape(t, d), mod, p, bsz, shards)
    g["dn_a_log"], g["dn_dt_bias"], g["loss_sq"] = g["dn_a_log"].reshape(-1), g["dn_dt_bias"].reshape(-1), sq
    small_rows = 336
    small = _pack_rows([dmod] + [g[k] for k, _ in _SMALL], small_rows)
    small_all = _allgather8(small, "ag_small").reshape(n_dev, small_rows, LANES)
    n_dm = bsz * N_MOD * d
    dmod_all = small_all.reshape(n_dev, -1)[:, :n_dm].reshape(n_dev * bsz, N_MOD * d)
    tot = _sum_lead(small_all, "sum_small").reshape(-1)
    gs, off = {}, n_dm
    for k, n in _SMALL:
        gs[k] = tot[off:off + n]
        off += n
    loss = 0.5 * jnp.sum(gs["loss_sq"])
    grad = {}
    grad["ada_w"], grad["ada_b"] = _ada_grad(c_all, lax.dynamic_slice(dmod_all, (0, chip * mod_cols), (n_dev * bsz, mod_cols)), dmod_all)
    for k in ("norm_mix_pre", "norm_mix_post", "norm_ffn_pre", "norm_ffn_post", "dn_norm_w"):
        grad[k] = gs[k]
    grad["dn_a_log"] = gs["dn_a_log"][DN_HEADS:2 * DN_HEADS]
    grad["dn_dt_bias"] = gs["dn_dt_bias"][DN_HEADS:2 * DN_HEADS]
    grad["attn_sinks"] = gs["attn_sinks"].reshape(AQ_HEADS, LANES)[:, 0]
    grad["rel_bias"] = gs["rel_bias"].reshape(AQ_HEADS, LANES)[:, :REL_BUCKETS].T
    grad["dn_conv_w"] = lax.dynamic_slice(gs["dn_conv_w"].reshape(DN_CONV, 3 * DN), (0, chip * (3 * DN // N_CHIPS)), (DN_CONV, 3 * DN // N_CHIPS))
    grad["ffn_conv_w"] = lax.dynamic_slice(gs["ffn_conv_w"].reshape(FFN_CONV, 2 * D_FF), (0, chip * (2 * D_FF // N_CHIPS)), (FFN_CONV, 2 * D_FF // N_CHIPS))

    mine = [_sum_lead(g[k], "sum_" + k, rows_apart=(k == "w_in")) for k in _BIG]
    out = {}
    out["ada_w"], theirs = _adamw(w["ada_w"][0], m["ada_w"][0], v["ada_w"][0], [grad["ada_w"]], "adamw_ada_w", _sibling_plug(mine))
    for k, a, b in zip(_BIG, mine, theirs):
        core = ic.reshape(1).astype(jnp.int32)
        if k == "w_in":
            tr = lambda z: jnp.transpose(z, (2, 0, 1))
            out[k] = [jnp.transpose(r, (1, 2, 0)) for r in _adamw_halves(tr(w[k]), tr(m[k]), tr(v[k]), a, b, core, "adamw_" + k)]
        elif k in _PRESUM:
            out[k] = _adamw_halves(w[k][0], m[k][0], v[k][0], a, b, core, "adamw_" + k)
        else:
            out[k] = _adamw(w[k][0], m[k][0], v[k][0], [a, b], "adamw_" + k)
    small_names = [k for k in _W_NAMES if k not in _BIG and k != "ada_w"]
    res_small = _adamw_small([w[k] for k in small_names], [m[k] for k in small_names], [v[k] for k in small_names],
                             [grad[k].reshape(w[k].shape) for k in small_names], "adamw_small")
    out.update(zip(small_names, res_small))
    for k in _BIG + ("ada_w",):
        out[k] = [r.reshape(w[k].shape) for r in out[k]]
    grads, deltas, new_m, new_v = ([out[k][i] for k in _W_NAMES] for i in range(4))
    return (loss, dx.reshape(bsz, s, d), *grads, *deltas, *new_m, *new_v)
```

```python
import math

import numpy as np
import jax
import jax.numpy as jnp
from jax import lax
from jax.experimental import pallas as pl
from jax.experimental.pallas import tpu as pltpu

F32 = jnp.float32
BF16 = jnp.bfloat16
MESH = pl.DeviceIdType.MESH

D_MODEL = 1024
N_MOD = 6
AQ_HEADS, AKV_HEADS, A_HD, WINDOW = 8, 2, 64, 128
REL_BUCKETS, REL_MAX_DIST = 32, 128
DN_HEADS, DN_HD, DN_CONV, DN_CHUNK = 4, 128, 4, 64
D_FF, FFN_CONV = 2816, 3
RMS_EPS, L2_EPS, NEG_INF = 1e-6, 1e-6, -1e30
AQ, AKV, DN = AQ_HEADS * A_HD, AKV_HEADS * A_HD, DN_HEADS * DN_HD
IN_DIM = AQ + 2 * AKV + 3 * DN + DN + 2 * DN_HEADS + 2 * D_MODEL
C_DQKV, C_Q, C_DZ, C_GA, C_GD, C_K, C_V, C_BD = 0, 1536, 2048, 2560, 3584, 4608, 4736, 4864
IN_PAD = 4992
LANES = 128
N_CHIPS = 4

ADAM_LR, ADAM_B1, ADAM_B2, ADAM_EPS, ADAM_WD, ADAM_STEP = 0.001, 0.9, 0.999, 1e-08, 0.01, 10


def _vspec(shape, index_map):
    return pl.BlockSpec(shape, index_map)


MM_VMEM_BUDGET = 40 * 2 ** 20
GRID_STEP_S = 0.35e-6
HBM_BYTES_PER_S = 3.0e12
MXU_FLOPS_PER_S = 9.0e14
MXU_DIM = 256


def _mm_tiles(m, n, k, mode, in_bytes, out_bytes, split=1):
    best = None
    for tm in [t for t in range(LANES, m + 1, LANES) if m % t == 0]:
        for tn in [t for t in range(LANES, n // split + 1, LANES) if (n // split) % t == 0]:
            a, b, o = k * tm * in_bytes, k * tn * in_bytes, tm * tn * out_bytes
            if 2 * (a + b + o) + (a if mode == "tn" else 0) > MM_VMEM_BUDGET:
                continue
            hbm_s = (m * k * in_bytes + (m // tm) * n * k * in_bytes + m * n * out_bytes) / HBM_BYTES_PER_S
            mxu_s = 2 * m * n * k / (MXU_FLOPS_PER_S * min(1.0, tm / MXU_DIM) * min(1.0, tn / MXU_DIM))
            cost = (m // tm) * (n // tn) * GRID_STEP_S + max(hbm_s, mxu_s)
            if best is None or cost < best[0]:
                best = (cost, tm, tn)
    return best[1], best[2]


def _mm(a, b, mode, out_dtype, name, plug=None, split=1, b_kblock=0, add=None):
    if mode == "nn":
        (m, k), n = a.shape, b.shape[1]
        dims = (((1,), (0,)), ((), ()))
    elif mode == "nt":
        (m, k), n = a.shape, b.shape[0]
        dims = (((1,), (1,)), ((), ()))
    else:
        (k, m), n = a.shape, b.shape[1]
        dims = (((0,), (0,)), ((), ()))
    tm, tn = _mm_tiles(m, n, k, mode, a.dtype.itemsize, jnp.dtype(out_dtype).itemsize, split)
    if mode == "tn":
        a_spec = _vspec((k, tm), lambda i, j: (0, i))
    else:
        a_spec = _vspec((tm, k), lambda i, j: (i, 0))
    if mode == "nt":
        b_spec = _vspec((tn, k), lambda i, j: (j, b_kblock))
    else:
        b_spec = _vspec((k, tn), lambda i, j: (0, j))
    in_specs, args = [a_spec, b_spec], (a, b)
    if add is not None:
        in_specs, args = in_specs + [_vspec((tm, tn), lambda i, j: (i, j))], (a, b, add)

    def body(a_ref, b_ref, *rest):
        o_ref = rest[-1]
        acc = lax.dot_general(a_ref[...].astype(BF16), b_ref[...].astype(BF16), dims, preferred_element_type=F32)
        if add is not None:
            acc = acc + rest[0][...]
        o_ref[...] = acc.astype(out_dtype).reshape(o_ref.shape)

    grid = (m // tm, n // tn)
    if split == 1:
        out_spec, out_shape = _vspec((tm, tn), lambda i, j: (i, j)), (m, n)
    else:
        per = n // split // tn
        out_spec, out_shape = _vspec((1, tm, tn), lambda i, j: (j // per, i, j % per)), (split, m, n // split)
    (out,), extra = _plugged_call(body, plug, _grid_ends(grid), args, name=name, grid=grid, in_specs=in_specs,
                                  out_specs=[out_spec], out_shape=[jax.ShapeDtypeStruct(out_shape, out_dtype)])
    return out if plug is None else (out, extra)


def _rms(x, w):
    return (x * lax.rsqrt(jnp.mean(x * x, axis=-1, keepdims=True) + RMS_EPS)) * w


def _pre_f(x, w, sc, sh):
    return _rms(x, w) * (1.0 + sc) + sh


def _post_f(y, w, g):
    return g * _rms(y, w)


def _tok_grid(t, bsz, ts):
    nt = t // bsz // ts
    return nt, (bsz, nt)


def _pre_fwd(x, w, sc, sh, name, ts=512, plug=None):
    t, d = x.shape
    bsz = sc.shape[0]
    nt, grid = _tok_grid(t, bsz, ts)
    row = _vspec((ts, d), lambda b, i: (b * nt + i, 0))
    vec = _vspec((1, d), lambda b, i: (0, 0))
    bvec = _vspec((1, 1, d), lambda b, i: (b, 0, 0))

    def body(x_ref, w_ref, sc_ref, sh_ref, u_ref):
        u_ref[...] = _pre_f(x_ref[...], w_ref[...], sc_ref[0], sh_ref[0]).astype(BF16)

    (u,), extra = _plugged_call(body, plug, _grid_ends(grid), (x, w, sc, sh), name=name, grid=grid, in_specs=[row, vec, bvec, bvec],
                                out_specs=[row], out_shape=[jax.ShapeDtypeStruct((t, d), BF16)])
    return u if plug is None else (u, extra)


def _pre_bwd(x, w, sc, sh, du, dres, name, ts=512):
    t, d = x.shape
    bsz = sc.shape[0]
    nt, grid = _tok_grid(t, bsz, ts)
    row = _vspec((ts, d), lambda b, i: (b * nt + i, 0))
    vec = _vspec((1, d), lambda b, i: (0, 0))
    bvec = _vspec((1, 1, d), lambda b, i: (b, 0, 0))

    def body(x_ref, w_ref, sc_ref, sh_ref, du_ref, dres_ref, dx_ref, dw_ref, dsc_ref, dsh_ref):
        b, i = pl.program_id(0), pl.program_id(1)
        _, vjp = jax.vjp(_pre_f, x_ref[...], w_ref[...], sc_ref[0], sh_ref[0])
        dx, dw, dsc, dsh = vjp(du_ref[...].astype(F32))
        dx_ref[...] = dres_ref[...] + dx

        @pl.when((b == 0) & (i == 0))
        def _():
            dw_ref[...] = jnp.zeros_like(dw_ref)

        @pl.when(i == 0)
        def _():
            dsc_ref[...] = jnp.zeros_like(dsc_ref)
            dsh_ref[...] = jnp.zeros_like(dsh_ref)

        dw_ref[...] += dw
        dsc_ref[0] += dsc
        dsh_ref[0] += dsh

    return pl.pallas_call(
        body, name=name, grid=grid, in_specs=[row, vec, bvec, bvec, row, row], out_specs=[row, vec, bvec, bvec],
        out_shape=[jax.ShapeDtypeStruct((t, d), F32), jax.ShapeDtypeStruct((1, d), F32),
                   jax.ShapeDtypeStruct((bsz, 1, d), F32), jax.ShapeDtypeStruct((bsz, 1, d), F32)],
    )(x, w, sc, sh, du, dres)


def _accumulate(ref, val, first):
    @pl.when(first)
    def _():
        ref[...] = jnp.zeros_like(ref)

    ref[...] += val.reshape(ref.shape)


def _post_pre_fwd(res, y, w_post, g, w_pre, sc, sh, name, ts=512):
    t, d = y.shape
    bsz = g.shape[0]
    nt, grid = _tok_grid(t, bsz, ts)
    row = _vspec((ts, d), lambda b, i: (b * nt + i, 0))
    vec = _vspec((1, d), lambda b, i: (0, 0))
    bvec = _vspec((1, 1, d), lambda b, i: (b, 0, 0))

    def body(res_ref, y_ref, wp_ref, g_ref, w_ref, sc_ref, sh_ref, h_ref, u_ref):
        h = res_ref[...] + _post_f(y_ref[...], wp_ref[...], g_ref[0])
        h_ref[...] = h
        u_ref[...] = _pre_f(h, w_ref[...], sc_ref[0], sh_ref[0]).astype(BF16)

    return pl.pallas_call(body, name=name, grid=grid, in_specs=[row, row, vec, bvec, vec, bvec, bvec], out_specs=[row, row],
                          out_shape=[jax.ShapeDtypeStruct((t, d), F32), jax.ShapeDtypeStruct((t, d), BF16)],
                          )(res, y, w_post, g, w_pre, sc, sh)


def _post_loss_bwd(res, y, w, g, tgt, name, ts=512):
    t, d = y.shape
    bsz = g.shape[0]
    nt, grid = _tok_grid(t, bsz, ts)
    row = _vspec((ts, d), lambda b, i: (b * nt + i, 0))
    vec = _vspec((1, d), lambda b, i: (0, 0))
    bvec = _vspec((1, 1, d), lambda b, i: (b, 0, 0))

    def body(res_ref, y_ref, w_ref, g_ref, tgt_ref, dh_ref, dy_ref, dw_ref, dg_ref, sq_ref):
        b, i = pl.program_id(0), pl.program_id(1)
        part, vjp = jax.vjp(_post_f, y_ref[...], w_ref[...], g_ref[0])
        e = res_ref[...] + part - tgt_ref[...]
        dh = e * (1.0 / d)
        dh_ref[...] = dh
        dy, dw, dg = vjp(dh)
        dy_ref[...] = dy.astype(BF16)
        _accumulate(dw_ref, dw, (b == 0) & (i == 0))
        _accumulate(dg_ref, dg, i == 0)
        _accumulate(sq_ref, jnp.sum(e * e, axis=0, keepdims=True) * (1.0 / d), (b == 0) & (i == 0))

    return pl.pallas_call(
        body, name=name, grid=grid, in_specs=[row, row, vec, bvec, row], out_specs=[row, row, vec, bvec, vec],
        out_shape=[jax.ShapeDtypeStruct((t, d), F32), jax.ShapeDtypeStruct((t, d), BF16), jax.ShapeDtypeStruct((1, d), F32),
                   jax.ShapeDtypeStruct((bsz, 1, d), F32), jax.ShapeDtypeStruct((1, d), F32)],
    )(res, y, w, g, tgt)


def _pre_post_bwd(x, w, sc, sh, du, dres, y, w_post, g, name, ts=512):
    t, d = x.shape
    bsz = sc.shape[0]
    nt, grid = _tok_grid(t, bsz, ts)
    row = _vspec((ts, d), lambda b, i: (b * nt + i, 0))
    vec = _vspec((1, d), lambda b, i: (0, 0))
    bvec = _vspec((1, 1, d), lambda b, i: (b, 0, 0))

    def body(x_ref, w_ref, sc_ref, sh_ref, du_ref, dres_ref, y_ref, wp_ref, g_ref,
             dx_ref, dy_ref, dw_ref, dsc_ref, dsh_ref, dwp_ref, dg_ref):
        b, i = pl.program_id(0), pl.program_id(1)
        _, vjp = jax.vjp(_pre_f, x_ref[...], w_ref[...], sc_ref[0], sh_ref[0])
        dx, dw, dsc, dsh = vjp(du_ref[...].astype(F32))
        dx = dres_ref[...] + dx
        dx_ref[...] = dx
        _, vjp_post = jax.vjp(_post_f, y_ref[...], wp_ref[...], g_ref[0])
        dy, dwp, dg = vjp_post(dx)
        dy_ref[...] = dy.astype(BF16)
        first = (b == 0) & (i == 0)
        _accumulate(dw_ref, dw, first)
        _accumulate(dwp_ref, dwp, first)
        _accumulate(dsc_ref, dsc, i == 0)
        _accumulate(dsh_ref, dsh, i == 0)
        _accumulate(dg_ref, dg, i == 0)

    v1, vb = jax.ShapeDtypeStruct((1, d), F32), jax.ShapeDtypeStruct((bsz, 1, d), F32)
    return pl.pallas_call(
        body, name=name, grid=grid, in_specs=[row, vec, bvec, bvec, row, row, row, vec, bvec],
        out_specs=[row, row, vec, bvec, bvec, vec, bvec],
        out_shape=[jax.ShapeDtypeStruct((t, d), F32), jax.ShapeDtypeStruct((t, d), BF16), v1, vb, vb, v1, vb],
    )(x, w, sc, sh, du, dres, y, w_post, g)


def _merge_f(ga, gd, ya, yd):
    return jax.nn.sigmoid(ga) * ya + jax.nn.sigmoid(gd) * yd


_MW = 512


def _merge_fwd(proj, ya, yd, ts=512):
    t, d = ya.shape
    blk = _vspec((ts, _MW), lambda i, j: (i, j))
    ga = _vspec((ts, _MW), lambda i, j: (i, C_GA // _MW + j))
    gd = _vspec((ts, _MW), lambda i, j: (i, C_GD // _MW + j))

    def body(ga_ref, gd_ref, ya_ref, yd_ref, o_ref):
        o_ref[...] = _merge_f(ga_ref[...], gd_ref[...], ya_ref[...].astype(F32), yd_ref[...].astype(F32)).astype(BF16)

    return pl.pallas_call(body, name="merge_fwd", grid=(t // ts, d // _MW), in_specs=[ga, gd, blk, blk], out_specs=blk,
                          out_shape=jax.ShapeDtypeStruct((t, d), BF16))(proj, proj, ya, yd)


_ANY = pl.BlockSpec(memory_space=pl.ANY)


def _branch_bwd(dproj, proj, y, dm, col0, name, ts=512):
    t, d = y.shape
    blk = _vspec((ts, _MW), lambda i, j: (i, j))
    gate = _vspec((ts, _MW), lambda i, j: (i, col0 // _MW + j))

    def body(buf_ref, g_ref, y_ref, dm_ref, dg_ref, dy_ref):
        del buf_ref
        _, vjp = jax.vjp(lambda g, yy: jax.nn.sigmoid(g) * yy, g_ref[...], y_ref[...].astype(F32))
        dg, dy = vjp(dm_ref[...].astype(F32))
        dg_ref[...] = dg.astype(BF16)
        dy_ref[...] = dy.astype(BF16)

    return pl.pallas_call(body, name=name, grid=(t // ts, d // _MW), in_specs=[_ANY, gate, blk, blk], out_specs=[gate, blk],
                          out_shape=[jax.ShapeDtypeStruct(dproj.shape, BF16), jax.ShapeDtypeStruct((t, d), BF16)],
                          input_output_aliases={0: 0})(dproj, proj, y, dm)


def _shift_down(x, s):
    if s == 0:
        return x
    r = lax.broadcasted_iota(jnp.int32, x.shape, 0)
    return jnp.where(r >= s, pltpu.roll(x, s, 0), 0.0)


def _shift_up(x, s):
    if s == 0:
        return x
    n = x.shape[0]
    r = lax.broadcasted_iota(jnp.int32, x.shape, 0)
    return jnp.where(r < n - s, pltpu.roll(x, n - s, 0), 0.0)


def _conv_fwd(x, w, k):
    out = None
    for j in range(k):
        term = w[j:j + 1, :] * _shift_down(x, k - 1 - j)
        out = term if out is None else out + term
    return out


def _conv_bwd(x, w, dc, k):
    dx = None
    dws = []
    for j in range(k):
        up = _shift_up(dc, k - 1 - j)
        term = w[j:j + 1, :] * up
        dx = term if dx is None else dx + term
        dws.append(jnp.sum(up * x, axis=0, keepdims=True))
    return dx, jnp.concatenate(dws, axis=0)


def _geglu_f(gate, val):
    return jax.nn.gelu(gate, approximate=True) * val


_FW = 256


def _ffn_act_fwd(up, conv_w, bsz):
    t = up.shape[0]
    s = t // bsz
    nj = D_FF // _FW
    xg = _vspec((s, _FW), lambda b, j: (b, j))
    xv = _vspec((s, _FW), lambda b, j: (b, nj + j))
    wg = _vspec((FFN_CONV, _FW), lambda b, j: (0, j))
    wv = _vspec((FFN_CONV, _FW), lambda b, j: (0, nj + j))

    def body(xg_ref, xv_ref, wg_ref, wv_ref, o_ref, cg_ref, cv_ref):
        gate = _conv_fwd(xg_ref[...].astype(F32), wg_ref[...], FFN_CONV)
        val = _conv_fwd(xv_ref[...].astype(F32), wv_ref[...], FFN_CONV)
        o_ref[...] = _geglu_f(gate, val).astype(BF16)
        cg_ref[...] = gate.astype(BF16)
        cv_ref[...] = val.astype(BF16)

    blk = _vspec((s, _FW), lambda b, j: (b, j))
    o = jax.ShapeDtypeStruct((t, D_FF), BF16)
    return pl.pallas_call(body, name="ffn_act_fwd", grid=(bsz, nj), in_specs=[xg, xv, wg, wv], out_specs=[blk] * 3,
                          out_shape=[o] * 3)(up, up, conv_w, conv_w)


def _ffn_act_bwd(up, conv_w, conv_out, dact, bsz, plug=None):
    t = up.shape[0]
    s = t // bsz
    nj = D_FF // _FW
    xg = _vspec((s, _FW), lambda j, b: (b, j))
    xv = _vspec((s, _FW), lambda j, b: (b, nj + j))
    wg = _vspec((FFN_CONV, _FW), lambda j, b: (0, j))
    wv = _vspec((FFN_CONV, _FW), lambda j, b: (0, nj + j))
    da = _vspec((s, _FW), lambda j, b: (b, j))
    dwo = _vspec((FFN_CONV, _FW), lambda j, b: (0, j))

    def body(xg_ref, xv_ref, wg_ref, wv_ref, cg_ref, cv_ref, da_ref, dxg_ref, dxv_ref, dwg_ref, dwv_ref):
        b = pl.program_id(1)
        xg_, xv_, wg_, wv_ = xg_ref[...].astype(F32), xv_ref[...].astype(F32), wg_ref[...], wv_ref[...]
        _, vjp = jax.vjp(_geglu_f, cg_ref[...].astype(F32), cv_ref[...].astype(F32))
        dgate, dval = vjp(da_ref[...].astype(F32))
        dxg, dwg = _conv_bwd(xg_, wg_, dgate, FFN_CONV)
        dxv, dwv = _conv_bwd(xv_, wv_, dval, FFN_CONV)
        dxg_ref[...] = dxg.astype(BF16)
        dxv_ref[...] = dxv.astype(BF16)

        @pl.when(b == 0)
        def _():
            dwg_ref[...] = jnp.zeros_like(dwg_ref)
            dwv_ref[...] = jnp.zeros_like(dwv_ref)

        dwg_ref[...] += dwg
        dwv_ref[...] += dwv

    outs, extra = _plugged_call(
        body, plug, _grid_ends((nj, bsz)), (up, up, conv_w, conv_w, *conv_out, dact), name="ffn_act_bwd", grid=(nj, bsz),
        in_specs=[xg, xv, wg, wv, da, da, da], out_specs=[da, da, dwo, dwo],
        out_shape=[jax.ShapeDtypeStruct((t, D_FF), BF16), jax.ShapeDtypeStruct((t, D_FF), BF16),
                   jax.ShapeDtypeStruct((FFN_CONV, D_FF), F32), jax.ShapeDtypeStruct((FFN_CONV, D_FF), F32)])
    return (*outs, extra)


def _bucket_table():
    qi = np.arange(WINDOW)[:, None]
    kj = np.arange(2 * WINDOW)[None, :]
    dist = WINDOW + qi - kj
    dc = np.maximum(dist, 0)
    max_exact = REL_BUCKETS // 2
    scaled = np.log(np.maximum(dc, 1).astype(np.float32) / np.float32(max_exact)) / np.float32(math.log(REL_MAX_DIST / max_exact))
    large = max_exact + (scaled.astype(np.float32) * np.float32(REL_BUCKETS - max_exact)).astype(np.int32)
    large = np.minimum(large, REL_BUCKETS - 1)
    bucket = np.where(dc < max_exact, dc, large).astype(np.int32)
    in_band = ((dist >= 0) & (dist < WINDOW)).astype(np.int32)
    return bucket, in_band


def _bias_build(rel_bias):
    bucket, _ = _bucket_table()

    def body(rb_ref, idx_ref, o_ref):
        h = pl.program_id(0)
        idx = idx_ref[...]
        acc = jnp.zeros(idx.shape, F32)
        for r in range(REL_BUCKETS):
            acc = jnp.where(idx == r, rb_ref[r, h], acc)
        o_ref[0] = acc

    return pl.pallas_call(
        body, name="bias_build", grid=(AQ_HEADS,),
        in_specs=[pl.BlockSpec(memory_space=pltpu.SMEM), _vspec((WINDOW, 2 * WINDOW), lambda h: (0, 0))],
        out_specs=_vspec((1, WINDOW, 2 * WINDOW), lambda h: (h, 0, 0)),
        out_shape=jax.ShapeDtypeStruct((AQ_HEADS, WINDOW, 2 * WINDOW), F32),
    )(rel_bias, jnp.asarray(bucket))


def _bias_grad(dbias):
    bucket, _ = _bucket_table()

    def body(db_ref, idx_ref, o_ref):
        idx = idx_ref[...]
        db = db_ref[0]
        lane = lax.broadcasted_iota(jnp.int32, (1, LANES), 1)
        acc = jnp.zeros((1, LANES), F32)
        for r in range(REL_BUCKETS):
            s = jnp.sum(jnp.sum(jnp.where(idx == r, db, 0.0), axis=1, keepdims=True), axis=0, keepdims=True)
            acc = jnp.where(lane == r, s, acc)
        o_ref[0] = acc

    return pl.pallas_call(
        body, name="bias_grad", grid=(AQ_HEADS,),
        in_specs=[_vspec((1, WINDOW, 2 * WINDOW), lambda h: (h, 0, 0)), _vspec((WINDOW, 2 * WINDOW), lambda h: (0, 0))],
        out_specs=_vspec((1, 1, LANES), lambda h: (h, 0, 0)),
        out_shape=jax.ShapeDtypeStruct((AQ_HEADS, 1, LANES), F32),
    )(dbias, jnp.asarray(bucket))


def _attn_mask(n):
    qi = lax.broadcasted_iota(jnp.int32, (WINDOW, 2 * WINDOW), 0)
    kj = lax.broadcasted_iota(jnp.int32, (WINDOW, 2 * WINDOW), 1)
    dist = WINDOW + qi - kj
    band = (dist >= 0) & (dist < WINDOW)
    return band & ((kj >= WINDOW) | (n > 0))


def _attn_probs(qk, bias, sink, mask):
    s = jnp.where(mask, qk * (A_HD ** -0.5) + bias, NEG_INF)
    m = jnp.maximum(jnp.max(s, axis=-1, keepdims=True), sink)
    p = jnp.exp(s - m)
    es = jnp.exp(sink - m)
    inv = 1.0 / (jnp.sum(p, axis=-1, keepdims=True) + es)
    return p * inv, es * inv


def _attn_fwd(proj, bias, sinks, bsz):
    t = proj.shape[0]
    s = t // bsz
    nb = s // WINDOW
    grp = AQ_HEADS // AKV_HEADS

    def body(q_ref, k_ref, v_ref, bias_ref, sink_ref, y_ref, kp_ref, vp_ref):
        kp_ref[0:WINDOW, :] = jnp.zeros((WINDOW, LANES), BF16)
        vp_ref[0:WINDOW, :] = jnp.zeros((WINDOW, LANES), BF16)
        kp_ref[WINDOW:, :] = k_ref[...].astype(BF16)
        vp_ref[WINDOW:, :] = v_ref[...].astype(BF16)

        def blk(n, carry):
            r0 = pl.multiple_of(n * WINDOW, WINDOW)
            mask = _attn_mask(n)
            kband = kp_ref[pl.ds(r0, 2 * WINDOW), :]
            vband = vp_ref[pl.ds(r0, 2 * WINDOW), :]
            qb = q_ref[pl.ds(r0, WINDOW), :].astype(BF16)
            heads = range(AQ_HEADS)
            hsl = lambda h: slice(h * A_HD, (h + 1) * A_HD)
            kbs = [kband[:, hsl(kv)] for kv in range(AKV_HEADS)]
            vbs = [vband[:, hsl(kv)] for kv in range(AKV_HEADS)]
            qks = [lax.dot_general(qb[:, hsl(h)], kbs[h // grp], _NT, preferred_element_type=F32) for h in heads]
            probs = [_attn_probs(qks[h], bias_ref[h], sink_ref[0, h], mask)[0] for h in heads]
            outs = [jnp.dot(probs[h].astype(BF16), vbs[h // grp], preferred_element_type=F32) for h in heads]
            y_ref[pl.ds(r0, WINDOW), :] = jnp.concatenate(outs, axis=1).astype(BF16)
            return carry

        lax.fori_loop(0, nb, blk, 0)

    return pl.pallas_call(
        body, name="attn_fwd", grid=(bsz,),
        in_specs=[_vspec((s, AQ), lambda b: (b, C_Q // AQ)), _vspec((s, AKV), lambda b: (b, C_K // AKV)),
                  _vspec((s, AKV), lambda b: (b, C_V // AKV)),
                  _vspec((AQ_HEADS, WINDOW, 2 * WINDOW), lambda b: (0, 0, 0)), pl.BlockSpec(memory_space=pltpu.SMEM)],
        out_specs=_vspec((s, AQ), lambda b: (b, 0)), out_shape=jax.ShapeDtypeStruct((t, AQ), BF16),
        scratch_shapes=[pltpu.VMEM((s + WINDOW, LANES), BF16), pltpu.VMEM((s + WINDOW, LANES), BF16)],
    )(proj, proj, proj, bias, sinks)


def _attn_bwd(dproj, proj, bias, sinks, dy, bsz, plug=None):
    t = proj.shape[0]
    s = t // bsz
    nb = s // WINDOW
    grp = AQ_HEADS // AKV_HEADS
    scale = A_HD ** -0.5

    def body(q_ref, k_ref, v_ref, bias_ref, sink_ref, dy_ref, dq_ref, dk_ref, dv_ref, dbias_ref, dsink_ref,
             kp_ref, vp_ref, dkp_ref, dvp_ref):
        b = pl.program_id(0)
        kp_ref[0:WINDOW, :] = jnp.zeros((WINDOW, LANES), BF16)
        vp_ref[0:WINDOW, :] = jnp.zeros((WINDOW, LANES), BF16)
        kp_ref[WINDOW:, :] = k_ref[...].astype(BF16)
        vp_ref[WINDOW:, :] = v_ref[...].astype(BF16)
        dkp_ref[...] = jnp.zeros_like(dkp_ref)
        dvp_ref[...] = jnp.zeros_like(dvp_ref)

        @pl.when(b == 0)
        def _():
            dbias_ref[...] = jnp.zeros_like(dbias_ref)
            dsink_ref[...] = jnp.zeros_like(dsink_ref)

        def blk(n, carry):
            r0 = pl.multiple_of(n * WINDOW, WINDOW)
            mask = _attn_mask(n)
            kband = kp_ref[pl.ds(r0, 2 * WINDOW), :]
            vband = vp_ref[pl.ds(r0, 2 * WINDOW), :]
            qb = q_ref[pl.ds(r0, WINDOW), :].astype(BF16)
            dyb = dy_ref[pl.ds(r0, WINDOW), :].astype(BF16)
            heads = range(AQ_HEADS)
            hsl = lambda h: slice(h * A_HD, (h + 1) * A_HD)
            kbs = [kband[:, hsl(kv)] for kv in range(AKV_HEADS)]
            vbs = [vband[:, hsl(kv)] for kv in range(AKV_HEADS)]
            qhs = [qb[:, hsl(h)] for h in heads]
            dyhs = [dyb[:, hsl(h)] for h in heads]
            qks = [lax.dot_general(qhs[h], kbs[h // grp], _NT, preferred_element_type=F32) for h in heads]
            dprobs = [lax.dot_general(dyhs[h], vbs[h // grp], _NT, preferred_element_type=F32) for h in heads]
            pbs, dsbs = [], []
            for h in heads:
                probs, psink = _attn_probs(qks[h], bias_ref[h], sink_ref[0, h], mask)
                rowdot = jnp.sum(probs * dprobs[h], axis=-1, keepdims=True)
                ds = probs * (dprobs[h] - rowdot)
                dbias_ref[h] += ds
                dsink_ref[h] += jnp.sum(-psink * rowdot, axis=0, keepdims=True) + jnp.zeros((1, LANES), F32)
                pbs.append(probs.astype(BF16))
                dsbs.append(ds.astype(BF16))
            dvhs = [lax.dot_general(pbs[h], dyhs[h], _TN, preferred_element_type=F32) for h in heads]
            dqs = [jnp.dot(dsbs[h], kbs[h // grp], preferred_element_type=F32) * scale for h in heads]
            dkhs = [lax.dot_general(dsbs[h], qhs[h], _TN, preferred_element_type=F32) * scale for h in heads]
            dks = [sum(dkhs[kv * grp + 1:(kv + 1) * grp], dkhs[kv * grp]) for kv in range(AKV_HEADS)]
            dvs = [sum(dvhs[kv * grp + 1:(kv + 1) * grp], dvhs[kv * grp]) for kv in range(AKV_HEADS)]
            dq_ref[pl.ds(r0, WINDOW), :] = jnp.concatenate(dqs, axis=1).astype(BF16)
            dkp_ref[pl.ds(r0, 2 * WINDOW), :] += jnp.concatenate(dks, axis=1)
            dvp_ref[pl.ds(r0, 2 * WINDOW), :] += jnp.concatenate(dvs, axis=1)
            return carry

        lax.fori_loop(0, nb, blk, 0)
        dk_ref[...] = dkp_ref[WINDOW:, :].astype(BF16)
        dv_ref[...] = dvp_ref[WINDOW:, :].astype(BF16)

    kvs = jax.ShapeDtypeStruct((t, AKV), BF16)
    outs, extra = _plugged_call(
        lambda buf_ref, *refs: body(*refs), plug, _grid_ends((bsz,)), (dproj, proj, proj, proj, bias, sinks, dy),
        name="attn_bwd", grid=(bsz,),
        in_specs=[_ANY, _vspec((s, AQ), lambda b: (b, C_Q // AQ)), _vspec((s, AKV), lambda b: (b, C_K // AKV)),
                  _vspec((s, AKV), lambda b: (b, C_V // AKV)),
                  _vspec((AQ_HEADS, WINDOW, 2 * WINDOW), lambda b: (0, 0, 0)), pl.BlockSpec(memory_space=pltpu.SMEM),
                  _vspec((s, AQ), lambda b: (b, 0))],
        out_specs=[_vspec((s, AQ), lambda b: (b, C_Q // AQ)), _vspec((s, AKV), lambda b: (b, 0)), _vspec((s, AKV), lambda b: (b, 0)),
                   _vspec((AQ_HEADS, WINDOW, 2 * WINDOW), lambda b: (0, 0, 0)), _vspec((AQ_HEADS, 1, LANES), lambda b: (0, 0, 0))],
        out_shape=[jax.ShapeDtypeStruct(dproj.shape, BF16), kvs, kvs,
                   jax.ShapeDtypeStruct((AQ_HEADS, WINDOW, 2 * WINDOW), F32), jax.ShapeDtypeStruct((AQ_HEADS, 1, LANES), F32)],
        scratch_shapes=[pltpu.VMEM((s + WINDOW, LANES), BF16), pltpu.VMEM((s + WINDOW, LANES), BF16),
                        pltpu.VMEM((s + WINDOW, LANES), F32), pltpu.VMEM((s + WINDOW, LANES), F32)],
        aliases={0: 0})
    return (*outs, extra)


def _dn_act_f(c, is_qk):
    a = jax.nn.silu(c)
    outs = []
    for h in range(DN_HEADS):
        ah = a[:, h * DN_HD:(h + 1) * DN_HD]
        nh = ah * lax.rsqrt(jnp.sum(ah * ah, axis=-1, keepdims=True) + L2_EPS)
        outs.append(jnp.where(is_qk, nh, ah))
    return jnp.concatenate(outs, axis=1)


def _dn_prep_fwd(proj, conv_w, bsz):
    t = proj.shape[0]
    s = t // bsz
    blk = _vspec((s, DN), lambda b, j: (b, j))
    wsp = _vspec((DN_CONV, DN), lambda b, j: (0, j))

    def body(x_ref, w_ref, o_ref, c_ref):
        j = pl.program_id(1)
        c = _conv_fwd(x_ref[...], w_ref[...], DN_CONV)
        o_ref[...] = _dn_act_f(c, j < 2)
        c_ref[...] = c.astype(BF16)

    return pl.pallas_call(body, name="dn_prep_fwd", grid=(bsz, 3), in_specs=[blk, wsp], out_specs=[blk, blk],
                          out_shape=[jax.ShapeDtypeStruct((t, 3 * DN), F32), jax.ShapeDtypeStruct((t, 3 * DN), BF16)])(proj, conv_w)


def _dn_prep_bwd(dproj, proj, conv_w, conv_out, dqkvn, bsz):
    t = proj.shape[0]
    s = t // bsz
    blk = _vspec((s, DN), lambda j, b: (b, j))
    wsp = _vspec((DN_CONV, DN), lambda j, b: (0, j))

    def body(buf_ref, x_ref, w_ref, c_ref, d_ref, dx_ref, dw_ref):
        del buf_ref
        j, b = pl.program_id(0), pl.program_id(1)
        x, w = x_ref[...], w_ref[...]
        _, vjp = jax.vjp(lambda cc: _dn_act_f(cc, j < 2), c_ref[...].astype(F32))
        (dc,) = vjp(d_ref[0])
        dx, dw = _conv_bwd(x, w, dc, DN_CONV)
        dx_ref[...] = dx.astype(BF16)

        @pl.when(b == 0)
        def _():
            dw_ref[...] = jnp.zeros_like(dw_ref)

        dw_ref[...] += dw

    return pl.pallas_call(
        body, name="dn_prep_bwd", grid=(3, bsz),
        in_specs=[_ANY, blk, wsp, blk, _vspec((1, s, DN), lambda j, b: (j, b, 0))], out_specs=[blk, wsp],
        out_shape=[jax.ShapeDtypeStruct(dproj.shape, BF16), jax.ShapeDtypeStruct((DN_CONV, 3 * DN), F32)],
        input_output_aliases={0: 0},
    )(dproj, proj, conv_w, conv_out, dqkvn)


def _bg_f(x, alog, dt):
    lane = lax.broadcasted_iota(jnp.int32, x.shape, 1)
    beta = jax.nn.sigmoid(x)
    g = -jnp.exp(alog) * jax.nn.softplus(x + dt)
    return jnp.where(lane < DN_HEADS, beta, jnp.where(lane < 2 * DN_HEADS, g, 0.0))


def _bg_fwd(proj, alog, dt, bsz):
    t = proj.shape[0]
    s = t // bsz
    vec = _vspec((1, LANES), lambda b: (0, 0))

    def body(x_ref, a_ref, d_ref, o_ref):
        o_ref[...] = _bg_f(x_ref[...], a_ref[...], d_ref[...])

    return pl.pallas_call(body, name="bg_fwd", grid=(bsz,), in_specs=[_vspec((s, LANES), lambda b: (b, C_BD // LANES)), vec, vec],
                          out_specs=_vspec((s, LANES), lambda b: (b, 0)), out_shape=jax.ShapeDtypeStruct((t, LANES), F32))(proj, alog, dt)


def _bg_bwd(dproj, proj, alog, dt, dbg4, dk, dv, bsz):
    t = proj.shape[0]
    s = t // bsz
    vec = _vspec((1, LANES), lambda b: (0, 0))
    kv = _vspec((s, AKV), lambda b: (b, 0))
    tail = 3 * LANES

    def body(buf_ref, x_ref, a_ref, d_ref, g4_ref, dk_ref, dv_ref, dx_ref, da_ref, dd_ref):
        del buf_ref
        b = pl.program_id(0)
        lane = lax.broadcasted_iota(jnp.int32, (s, LANES), 1)
        dbg = jnp.zeros((s, LANES), F32)
        for h in range(DN_HEADS):
            gh = g4_ref[:, h * DN_HD:(h + 1) * DN_HD]
            dbg = jnp.where(lane == h, gh[:, 0:1], dbg)
            dbg = jnp.where(lane == DN_HEADS + h, gh[:, 1:2], dbg)
        _, vjp = jax.vjp(_bg_f, x_ref[...], a_ref[...], d_ref[...])
        dx, da, dd = vjp(dbg)
        dx_ref[...] = jnp.concatenate([dk_ref[...], dv_ref[...], dx.astype(BF16)], axis=1)

        @pl.when(b == 0)
        def _():
            da_ref[...] = jnp.zeros_like(da_ref)
            dd_ref[...] = jnp.zeros_like(dd_ref)

        da_ref[...] += da
        dd_ref[...] += dd

    return pl.pallas_call(
        body, name="bg_bwd", grid=(bsz,),
        in_specs=[_ANY, _vspec((s, LANES), lambda b: (b, C_BD // LANES)), vec, vec, _vspec((s, DN), lambda b: (b, 0)), kv, kv],
        out_specs=[_vspec((s, tail), lambda b: (b, C_K // tail)), vec, vec],
        out_shape=[jax.ShapeDtypeStruct(dproj.shape, BF16), jax.ShapeDtypeStruct((1, LANES), F32), jax.ShapeDtypeStruct((1, LANES), F32)],
        input_output_aliases={0: 0},
    )(dproj, proj, alog, dt, dbg4, dk, dv)


def _dn_out_f(o, z, w):
    outs = []
    for h in range(DN_HEADS):
        sl = slice(h * DN_HD, (h + 1) * DN_HD)
        outs.append(_rms(o[:, sl], w) * jax.nn.silu(z[:, sl]))
    return jnp.concatenate(outs, axis=1)


def _dn_out_fwd(o, proj, w, ts=512):
    t = o.shape[0]
    blk = _vspec((ts, DN), lambda i: (i, 0))
    zsp = _vspec((ts, DN), lambda i: (i, C_DZ // DN))
    vec = _vspec((1, DN_HD), lambda i: (0, 0))

    def body(o_ref, z_ref, w_ref, y_ref):
        y_ref[...] = _dn_out_f(o_ref[...], z_ref[...], w_ref[...]).astype(BF16)

    return pl.pallas_call(body, name="dn_out_fwd", grid=(t // ts,), in_specs=[blk, zsp, vec], out_specs=blk,
                          out_shape=jax.ShapeDtypeStruct((t, DN), BF16))(o, proj, w)


def _dn_out_bwd(dproj, o, proj, w, dy, ts=512):
    t = o.shape[0]
    blk = _vspec((ts, DN), lambda i: (i, 0))
    zsp = _vspec((ts, DN), lambda i: (i, C_DZ // DN))
    vec = _vspec((1, DN_HD), lambda i: (0, 0))

    def body(buf_ref, o_ref, z_ref, w_ref, dy_ref, do_ref, dz_ref, dw_ref):
        del buf_ref
        i = pl.program_id(0)
        _, vjp = jax.vjp(_dn_out_f, o_ref[...], z_ref[...], w_ref[...])
        do, dz, dw = vjp(dy_ref[...].astype(F32))
        do_ref[...] = do
        dz_ref[...] = dz.astype(BF16)

        @pl.when(i == 0)
        def _():
            dw_ref[...] = jnp.zeros_like(dw_ref)

        dw_ref[...] += dw

    return pl.pallas_call(
        body, name="dn_out_bwd", grid=(t // ts,), in_specs=[_ANY, blk, zsp, vec, blk], out_specs=[blk, zsp, vec],
        out_shape=[jax.ShapeDtypeStruct((t, DN), F32), jax.ShapeDtypeStruct(dproj.shape, BF16), jax.ShapeDtypeStruct((1, DN_HD), F32)],
        input_output_aliases={0: 1},
    )(dproj, o, proj, w, dy)


_C = DN_CHUNK


def _dot(a, b, dims):
    return lax.dot_general(a.astype(BF16), b.astype(BF16), dims, preferred_element_type=F32)


def _split(a):
    hi = a.astype(BF16)
    return hi, (a - hi.astype(F32)).astype(BF16)


def _dot3(a, b, dims):
    (ah, al), (bh, bl) = (a if isinstance(a, tuple) else _split(a)), (b if isinstance(b, tuple) else _split(b))
    mm = lambda x, y: lax.dot_general(x, y, dims, preferred_element_type=F32)
    return mm(ah, bh) + (mm(ah, bl) + mm(al, bh))


_NN = (((1,), (0,)), ((), ()))
_NT = (((1,), (1,)), ((), ()))
_TN = (((0,), (0,)), ((), ()))


_SUB = 8


def _tri_inverses(ls, lts):
    ri8 = lax.broadcasted_iota(jnp.int32, (_SUB, _C), 0)
    ci8 = lax.broadcasted_iota(jnp.int32, (_SUB, _C), 1)
    nblk = _C // _SUB
    ts = []
    for lt in lts:
        blocks = [jnp.where(ci8 == ri8 + _SUB * b, 1.0, 0.0).astype(F32) for b in range(nblk)]
        for r in range(1, _SUB):
            for b in range(nblk):
                coef = lt[_SUB * b:_SUB * (b + 1), _SUB * b + r:_SUB * b + r + 1]
                row = jnp.sum(coef * blocks[b], axis=0, keepdims=True)
                blocks[b] = jnp.where(ri8 == r, blocks[b] - row, blocks[b])
        ts.append(jnp.concatenate(blocks, axis=0))
    ri = lax.broadcasted_iota(jnp.int32, (_C, _C), 0)
    ci = lax.broadcasted_iota(jnp.int32, (_C, _C), 1)
    s = _SUB
    while s < _C:
        shift = s.bit_length()
        quad = ((ri >> shift) == (ci >> shift)) & ((ri & s) != 0) & ((ci & s) == 0)
        offs = [jnp.where(quad, l, 0.0) for l in ls]
        tsp = [_split(t) for t in ts]
        left = [_dot3(tp, off, _NN) for tp, off in zip(tsp, offs)]
        ts = [t - _dot3(lo, tp, _NN) for t, lo, tp in zip(ts, left, tsp)]
        s *= 2
    return ts


_SEG = 512
_HEADS = tuple(range(DN_HEADS))


def _hsl(hh):
    return slice(hh * DN_HD, (hh + 1) * DN_HD)


def _chunk_specs(bsz, nseg, reverse):
    seg = (lambda i: nseg - 1 - i) if reverse else (lambda i: i)
    ncs = _SEG // _C
    col = lambda off: _vspec((bsz, _SEG, DN), lambda i: (0, seg(i), off))
    return (col, _vspec((bsz, _SEG, LANES), lambda i: (0, seg(i), 0)),
            _vspec((bsz, DN_HEADS, ncs, _C), lambda i: (0, 0, seg(i), 0)),
            _vspec((bsz, DN_HEADS, ncs, DN_HD, DN_HD), lambda i: (0, 0, seg(i), 0, 0)),
            _vspec((bsz, DN_HEADS, ncs, _C, _C), lambda i: (0, 0, seg(i), 0, 0)))


def _chunk_pre(q_ref, k_ref, v_ref, bg_ref, gr_ref, c, bb, hh):
    r0 = pl.multiple_of(c * _C, _C)
    ri = lax.broadcasted_iota(jnp.int32, (_C, _C), 0)
    ci = lax.broadcasted_iota(jnp.int32, (_C, _C), 1)
    q = q_ref[bb, pl.ds(r0, _C), _hsl(hh)] * (DN_HD ** -0.5)
    k = k_ref[bb, pl.ds(r0, _C), _hsl(hh)]
    v = v_ref[bb, pl.ds(r0, _C), _hsl(hh)]
    bgc = bg_ref[bb, pl.ds(r0, _C), :]
    beta = bgc[:, hh:hh + 1]
    g_col = bgc[:, DN_HEADS + hh:DN_HEADS + hh + 1]
    g_row = gr_ref[bb, hh, pl.ds(c, 1), :]
    gc_col = jnp.sum(jnp.where(ri >= ci, g_row, 0.0), axis=1, keepdims=True)
    gc_row = jnp.sum(jnp.where(ri <= ci, g_col, 0.0), axis=0, keepdims=True)
    gc_last = jnp.sum(g_col, axis=0, keepdims=True)
    diff = gc_col - gc_row
    decay = jnp.where(ri >= ci, jnp.exp(jnp.where(ri >= ci, diff, 0.0)), 0.0)
    diff_t = gc_row - gc_col
    decay_t = jnp.where(ri <= ci, jnp.exp(jnp.where(ri <= ci, diff_t, 0.0)), 0.0)
    eg = jnp.exp(gc_col)
    et = jnp.exp(gc_last - gc_col)
    gl = jnp.exp(gc_last)
    kb = k * beta
    vb = v * beta
    return dict(r0=r0, bb=bb, hh=hh, q=q, k=k, v=v, beta=beta, decay=decay, decay_t=decay_t, eg=eg, et=et, gl=gl, kb=kb, vb=vb,
                ri=ri, ci=ci)


def _chunk_solve(ms, tms=None):
    for m in ms:
        m["kk_t"] = _dot(m["k"], m["kb"], _NT)
        m["qk"] = _dot(m["q"], m["k"], _NT)
        m["kk"] = _dot(m["kb"], m["k"], _NT)
        if tms is not None:
            m["qk_t"] = _dot(m["k"], m["q"], _NT)
    if tms is None:
        tms = _tri_inverses([jnp.where(m["ri"] > m["ci"], m["kk"] * m["decay"], 0.0) for m in ms],
                            [jnp.where(m["ri"] < m["ci"], m["kk_t"] * m["decay_t"], 0.0) for m in ms])
    for m, tm in zip(ms, tms):
        m["tm_f32"] = tm
    for m in ms:
        rhs = jnp.concatenate([m["vb"], m["kb"] * m["eg"]], axis=1)
        m["tm"] = _split(m["tm_f32"])
        m["sol"] = _dot3(m["tm"], rhs, _NN)
        m["intra"] = jnp.where(m["ri"] >= m["ci"], m["qk"] * m["decay"], 0.0)


def _dn_chunk_fwd(qkvn, bg, g_rows, bsz, plug=None):
    t = qkvn.shape[0]
    s = t // bsz
    nc, nseg = s // _C, s // _SEG
    pairs = [(bb, hh) for bb in range(bsz) for hh in _HEADS]

    def body(q_ref, k_ref, v_ref, bg_ref, gr_ref, o_ref, st_ref, tm_ref, s_ref):
        @pl.when(pl.program_id(0) == 0)
        def _():
            s_ref[...] = jnp.zeros_like(s_ref)

        def chunk(c, carry):
            ms = [_chunk_pre(q_ref, k_ref, v_ref, bg_ref, gr_ref, c, bb, hh) for bb, hh in pairs]
            _chunk_solve(ms)
            sts = [s_ref[i] for i in range(len(pairs))]
            for m, st in zip(ms, sts):
                st_ref[m["bb"], m["hh"], c] = st
                tm_ref[m["bb"], m["hh"], c] = m["tm_f32"]
            ws = [_dot(m["sol"][:, DN_HD:], st, _NN) for m, st in zip(ms, sts)]
            qs = [_dot(m["q"] * m["eg"], st, _NN) for m, st in zip(ms, sts)]
            v_new = [m["sol"][:, :DN_HD] - a for m, a in zip(ms, ws)]
            iv = [_dot(m["intra"], vn, _NN) for m, vn in zip(ms, v_new)]
            upd = [_dot(m["k"] * m["et"], vn, _TN) for m, vn in zip(ms, v_new)]
            for i, (bb, hh) in enumerate(pairs):
                s_ref[i] = sts[i] * ms[i]["gl"] + upd[i]
                o_ref[bb, pl.ds(ms[i]["r0"], _C), _hsl(hh)] = qs[i] + iv[i]
            return carry

        lax.fori_loop(0, _SEG // _C, chunk, 0)

    col, bgs, grs, sts_spec, tms_spec = _chunk_specs(bsz, nseg, False)
    q3, bg3 = qkvn.reshape(bsz, s, 3 * DN), bg.reshape(bsz, s, LANES)
    (o, states, tms), extra = _plugged_call(
        body, plug, _grid_ends((nseg,)), (q3, q3, q3, bg3, g_rows), name="dn_chunk_fwd", grid=(nseg,),
        in_specs=[col(0), col(1), col(2), bgs, grs], out_specs=[col(0), sts_spec, tms_spec],
        out_shape=[jax.ShapeDtypeStruct((bsz, s, DN), F32), jax.ShapeDtypeStruct((bsz, DN_HEADS, nc, DN_HD, DN_HD), F32),
                   jax.ShapeDtypeStruct((bsz, DN_HEADS, nc, _C, _C), F32)],
        scratch_shapes=[pltpu.VMEM((bsz * DN_HEADS, DN_HD, DN_HD), F32)])
    return o.reshape(t, DN), (states, tms), extra


def _dn_chunk_bwd(qkvn, bg, g_rows, states, do, bsz, plug=None):
    t = qkvn.shape[0]
    s = t // bsz
    nc, nseg = s // _C, s // _SEG
    pairs = [(bb, hh) for bb in range(bsz) for hh in _HEADS]

    def body(q_ref, k_ref, v_ref, bg_ref, gr_ref, st_ref, tm_ref, do_ref, dqkv_ref, dbg_ref, ds_ref):
        @pl.when(pl.program_id(0) == 0)
        def _():
            ds_ref[...] = jnp.zeros_like(ds_ref)

        def chunk(cc, carry):
            c = _SEG // _C - 1 - cc
            ms = [_chunk_pre(q_ref, k_ref, v_ref, bg_ref, gr_ref, c, bb, hh) for bb, hh in pairs]
            _chunk_solve(ms, [tm_ref[bb, hh, c] for bb, hh in pairs])
            ri, ci = ms[0]["ri"], ms[0]["ci"]
            for i, m in enumerate(ms):
                m["st"] = st_ref[m["bb"], m["hh"], c]
                m["ds_out"] = ds_ref[i]
                m["do"] = do_ref[m["bb"], pl.ds(m["r0"], _C), _hsl(m["hh"])]
                m["w"] = m["sol"][:, DN_HD:]
            for m in ms:
                m["v_new"] = m["sol"][:, :DN_HD] - _dot(m["w"], m["st"], _NN)
            for m in ms:
                m["q_dec"], m["k_tail"] = m["q"] * m["eg"], m["k"] * m["et"]
                m["dk_tail"] = _dot(m["v_new"], m["ds_out"], _NT)
                m["dv_new"] = _dot(m["k_tail"], m["ds_out"], _NN) + _dot(m["intra"], m["do"], _TN)
                m["dq_dec"] = _dot(m["do"], m["st"], _NT)
                m["ds_in"] = m["ds_out"] * m["gl"] + _dot(m["q_dec"], m["do"], _TN)
                m["dintra"] = jnp.where(ri >= ci, _dot(m["do"], m["v_new"], _NT), 0.0)
                m["dintra_t"] = jnp.where(ri <= ci, _dot(m["v_new"], m["do"], _NT), 0.0)
            for m in ms:
                m["dw"] = -_dot(m["dv_new"], m["st"], _NT)
                m["ds_in"] = m["ds_in"] - _dot(m["w"], m["dv_new"], _TN)
            for m in ms:
                dsol = jnp.concatenate([m["dv_new"], m["dw"]], axis=1)
                m["drhs"] = _dot3(m["tm"], dsol, _TN)
            for m in ms:
                m["dl"] = jnp.where(ri > ci, -_dot(m["drhs"], m["sol"], _NT), 0.0)
                m["dl_t"] = jnp.where(ri < ci, -_dot(m["sol"], m["drhs"], _NT), 0.0)
            for m in ms:
                m["dkb2"] = _dot(m["dl"] * m["decay"], m["k"], _NN)
                m["dk"] = _dot(m["dl_t"] * m["decay_t"], m["kb"], _NN) + _dot(m["dintra_t"] * m["decay_t"], m["q"], _NN)
                m["dq"] = _dot(m["dintra"] * m["decay"], m["k"], _NN)
            for m in ms:
                _chunk_bwd_finish(m)
            for m in ms:
                ones_ge = jnp.where(ri <= ci, 1.0, 0.0).astype(BF16)
                gh, gl_ = _split(m["dgc"] + jnp.zeros((_C, LANES), F32))
                m["dg_b"] = jnp.dot(ones_ge, gh, preferred_element_type=F32) + jnp.dot(ones_ge, gl_, preferred_element_type=F32)
            lane = lax.broadcasted_iota(jnp.int32, (_C, LANES), 1)
            for i, m in enumerate(ms):
                bb, hh, rows = m["bb"], m["hh"], pl.ds(m["r0"], _C)
                dqkv_ref[0, bb, rows, _hsl(hh)] = m["dq"] * (DN_HD ** -0.5)
                dqkv_ref[1, bb, rows, _hsl(hh)] = m["dk"]
                dqkv_ref[2, bb, rows, _hsl(hh)] = m["dv"]
                dbg_ref[bb, rows, _hsl(hh)] = jnp.where(lane == 0, m["dbeta"], jnp.where(lane == 1, m["dg_b"], 0.0))
                ds_ref[i] = m["ds_in"]
            return carry

        lax.fori_loop(0, _SEG // _C, chunk, 0)

    col, bgs, grs, sts_spec, tms_spec = _chunk_specs(bsz, nseg, True)
    q3, bg3, do3 = qkvn.reshape(bsz, s, 3 * DN), bg.reshape(bsz, s, LANES), do.reshape(bsz, s, DN)
    (dqkv, dbg), extra = _plugged_call(
        body, plug, _grid_ends((nseg,)), (q3, q3, q3, bg3, g_rows, *states, do3), name="dn_chunk_bwd", grid=(nseg,),
        in_specs=[col(0), col(1), col(2), bgs, grs, sts_spec, tms_spec, col(0)],
        out_specs=[_vspec((3, bsz, _SEG, DN), lambda i: (0, 0, nseg - 1 - i, 0)), col(0)],
        out_shape=[jax.ShapeDtypeStruct((3, bsz, s, DN), F32), jax.ShapeDtypeStruct((bsz, s, DN), F32)],
        scratch_shapes=[pltpu.VMEM((bsz * DN_HEADS, DN_HD, DN_HD), F32)])
    return dqkv.reshape(3, t, DN), dbg.reshape(t, DN), extra


def _chunk_bwd_finish(m):
    q, k, v, beta, decay, decay_t = m["q"], m["k"], m["v"], m["beta"], m["decay"], m["decay_t"]
    eg, et, gl, kb, dl, dl_t, dintra, dintra_t = m["eg"], m["et"], m["gl"], m["kb"], m["dl"], m["dl_t"], m["dintra"], m["dintra_t"]
    dq_dec, dk_tail, dq, dk = m["dq_dec"], m["dk_tail"], m["dq"], m["dk"]
    dgl = jnp.sum(jnp.sum(m["ds_out"] * m["st"], axis=1, keepdims=True), axis=0, keepdims=True)
    dvb, dkbeg = m["drhs"][:, :DN_HD], m["drhs"][:, DN_HD:]
    dkb = dkbeg * eg + m["dkb2"]
    deg = jnp.sum(dkbeg * kb, axis=1, keepdims=True)
    em = (dl * m["kk"] + dintra * m["qk"]) * decay
    em_t = (dl_t * m["kk_t"] + dintra_t * m["qk_t"]) * decay_t
    dgc = jnp.sum(em, axis=1, keepdims=True) - jnp.sum(em_t, axis=1, keepdims=True)
    dq = dq + dq_dec * eg
    deg = deg + jnp.sum(dq_dec * q, axis=1, keepdims=True)
    dk = dk + dk_tail * et
    det = jnp.sum(dk_tail * k, axis=1, keepdims=True)
    dgc = dgc + deg * eg - det * et
    dgc_last = jnp.sum(det * et, axis=0, keepdims=True) + dgl * gl
    rcol = lax.broadcasted_iota(jnp.int32, (_C, 1), 0)
    m["dgc"] = dgc + jnp.where(rcol == _C - 1, dgc_last, 0.0)
    m["dq"] = dq
    m["dk"] = dk + dkb * beta
    m["dbeta"] = jnp.sum(dkb * k, axis=1, keepdims=True) + jnp.sum(dvb * v, axis=1, keepdims=True)
    m["dv"] = dvb * beta


def _mod_fwd(c_all, ada_w_loc, ada_b_loc):
    n, cols = c_all.shape[0], ada_w_loc.shape[1]

    def body(c_ref, w_ref, b_ref, o_ref):
        o_ref[...] = _dot(jax.nn.silu(c_ref[...]), w_ref[...], _NN) + b_ref[...]

    return pl.pallas_call(body, name="mod_fwd", out_shape=jax.ShapeDtypeStruct((n, cols), F32))(c_all, ada_w_loc, ada_b_loc)


def _ada_grad(c_all, dmod_loc, dmod_all):
    d, cols = c_all.shape[1], dmod_loc.shape[1]

    def body(c_ref, dl_ref, da_ref, gw_ref, gb_ref):
        gw_ref[...] = _dot(jax.nn.silu(c_ref[...]), dl_ref[...], _TN)
        gb_ref[...] = jnp.sum(da_ref[...], axis=0, keepdims=True)

    return pl.pallas_call(body, name="ada_grad", out_shape=[jax.ShapeDtypeStruct((d, cols), F32),
                                                           jax.ShapeDtypeStruct((1, dmod_all.shape[1]), F32)])(c_all, dmod_loc, dmod_all)


ELEMENTWISE_BLOCK_BYTES = 3 * 2 ** 19


def _row_tile(r, c=1024):
    fits = [tr for tr in range(16, r + 1, 16) if tr * c * 4 <= ELEMENTWISE_BLOCK_BYTES]
    if not fits:
        return r
    whole = [tr for tr in fits if r % tr == 0]
    return whole[-1] if whole else fits[-1]


def _adamw(w, m, v, grads, name, plug=None):
    r, rest = w.shape[0], w.shape[1:]
    c = math.prod(rest)
    tr = _row_tile(r, c)
    blk = _vspec((tr,) + rest, lambda i: (i,) + (0,) * len(rest))
    n = len(grads)

    def body(*refs):
        w_ref, m_ref, v_ref = refs[:3]
        g_ref, d_ref, mo_ref, vo_ref = refs[3 + n:]
        g = refs[3][...]
        for p in refs[4:3 + n]:
            g = g + p[...]
        g_ref[...] = g
        d_ref[...], mo_ref[...], vo_ref[...] = _adamw_math(w_ref[...], m_ref[...], v_ref[...], g)

    o = jax.ShapeDtypeStruct(w.shape, F32)
    grid = (pl.cdiv(r, tr),)
    res, extra = _plugged_call(body, plug, _grid_ends(grid), (w, m, v, *grads), name=name, grid=grid, in_specs=[blk] * (3 + n),
                               out_specs=[blk] * 4, out_shape=[o] * 4)
    return res if plug is None else (res, extra)


def _adamw_halves(w, m, v, own, other, ic, name):
    if w.ndim == 3:
        r, _, c = w.shape
        tr = _row_tile(r, c)
        blk = _vspec((tr, 1, c // 2), lambda i, j: (i, 0, j))
        half = _vspec((tr, 1, c // 2), lambda i, j: (i, 0, 0))
        grid = (pl.cdiv(r, tr), 2)
    else:
        r, c = w.shape
        tr = _row_tile(r // 2, c)
        per = r // 2 // tr
        blk = _vspec((tr, c), lambda i, j: (j * per + i, 0))
        half = _vspec((tr, c), lambda i, j: (i, 0))
        grid = (per, 2)

    def body(ic_ref, w_ref, m_ref, v_ref, own_ref, other_ref, g_ref, d_ref, mo_ref, vo_ref):
        g = jnp.where(pl.program_id(1) == ic_ref[0], own_ref[...], other_ref[...])
        g_ref[...] = g
        d_ref[...], mo_ref[...], vo_ref[...] = _adamw_math(w_ref[...], m_ref[...], v_ref[...], g)

    o = jax.ShapeDtypeStruct(w.shape, F32)
    return pl.pallas_call(body, name=name, grid=grid,
                          in_specs=[pl.BlockSpec(memory_space=pltpu.SMEM), blk, blk, blk, half, half], out_specs=[blk] * 4,
                          out_shape=[o] * 4)(ic, w, m, v, own, other)


def _adamw_math(w, m, v, g):
    m_new = ADAM_B1 * m + (1.0 - ADAM_B1) * g
    v_new = ADAM_B2 * v + (1.0 - ADAM_B2) * jnp.square(g)
    m_hat = m_new / (1.0 - ADAM_B1 ** ADAM_STEP)
    v_hat = v_new / (1.0 - ADAM_B2 ** ADAM_STEP)
    return -ADAM_LR * (m_hat / (jnp.sqrt(v_hat) + ADAM_EPS) + ADAM_WD * w), m_new, v_new


def _adamw_small(ws, ms, vs, gs, name):
    n = len(ws)

    def body(*refs):
        for i in range(n):
            w_ref, m_ref, v_ref, g_ref = (refs[j * n + i] for j in range(4))
            go_ref, d_ref, mo_ref, vo_ref = refs[4 * n + 4 * i:4 * n + 4 * i + 4]
            g = g_ref[...]
            go_ref[...] = g
            d_ref[...], mo_ref[...], vo_ref[...] = _adamw_math(w_ref[...], m_ref[...], v_ref[...], g)

    res = pl.pallas_call(body, name=name, out_shape=[jax.ShapeDtypeStruct(a.shape, F32) for a in ws for _ in range(4)])(
        *ws, *ms, *vs, *gs)
    return [res[4 * i:4 * i + 4] for i in range(n)]


def _sum_lead(x, name, rows_apart=False):
    p, r, c = x.shape
    tr = _row_tile(r, c)
    mid = (1,) if rows_apart else ()

    def body(x_ref, o_ref):
        acc = x_ref[0].astype(F32)
        for i in range(1, p):
            acc = acc + x_ref[i].astype(F32)
        o_ref[...] = acc.reshape(o_ref.shape)

    return pl.pallas_call(body, name=name, grid=(pl.cdiv(r, tr),), in_specs=[_vspec((p, tr, c), lambda i: (0, i, 0))],
                          out_specs=_vspec((tr,) + mid + (c,), lambda i: (i,) + (0,) * (1 + len(mid))),
                          out_shape=jax.ShapeDtypeStruct((r,) + mid + (c,), F32))(x)


def _allgather8(x_shard, name, plug=None):
    m_per, n = x_shard.shape

    def body(x_ref, out_ref, send_sems, recv_sems, local_sem):
        x, y, c = lax.axis_index("x"), lax.axis_index("y"), lax.axis_index("c")
        me, sibling = (x, y, c), (x, y, 1 - c)
        chips = [(1 - x, y), (x, 1 - y), (1 - x, 1 - y)]

        def rows(px, py, pc):
            return out_ref.at[pl.ds((4 * px + 2 * py + pc) * m_per, m_per), :]

        def copy(k, block, to, src=None):
            return pltpu.make_async_remote_copy(
                src_ref=rows(*block) if src is None else src, dst_ref=rows(*block), send_sem=send_sems.at[k],
                recv_sem=recv_sems.at[k], device_id=to, device_id_type=MESH)

        mine = pltpu.make_async_copy(x_ref, rows(*me), local_sem)
        mine.start()
        first = [copy(0, me, sibling, src=x_ref)]
        first += [copy(1 + j, me, (*chip, c), src=x_ref) for j, chip in enumerate(chips)]
        for cp in first:
            cp.start()
        passed = [copy(4 + j, (*chip, c), sibling) for j, chip in enumerate(chips)]
        for j, chip in enumerate(chips):
            copy(1 + j, (*chip, c), me).wait_recv()
            passed[j].start()
        copy(0, sibling, me).wait_recv()
        for j, chip in enumerate(chips):
            copy(4 + j, (*chip, 1 - c), me).wait_recv()
        for cp in first + passed:
            cp.wait_send()
        mine.wait()

    grid = (1,)
    (out,), extra = _plugged_call(
        body, plug, _grid_ends(grid), (x_shard,), name=name, grid=grid, out_shape=[jax.ShapeDtypeStruct((8 * m_per, n), x_shard.dtype)],
        in_specs=[pl.BlockSpec(memory_space=pltpu.VMEM)], out_specs=[pl.BlockSpec(memory_space=pltpu.VMEM)],
        scratch_shapes=[pltpu.SemaphoreType.DMA((7,)), pltpu.SemaphoreType.DMA((7,)), pltpu.SemaphoreType.DMA])
    return out if plug is None else (out, extra)


_HBM = pl.BlockSpec(memory_space=pltpu.HBM)


def _mesh_place():
    x, y, c = lax.axis_index("x"), lax.axis_index("y"), lax.axis_index("c")
    return x, y, c, 2 * x + y, [(1 - x, y), (x, 1 - y), (1 - x, 1 - y)]


def _gather_plug(shards):
    n = len(shards)

    def half(ref, c, lead=None):
        r, cols = ref.shape[-2] // 2, ref.shape[-1] // 2
        if r % 16 == 0:
            rows = pl.ds(pl.multiple_of(c * r, 16), r)
            return ref.at[rows, :] if lead is None else ref.at[lead, rows, :]
        lanes = pl.ds(pl.multiple_of(c * cols, LANES), cols)
        return ref.at[:, lanes] if lead is None else ref.at[lead, :, lanes]

    def copies(ins, outs, send, recv):
        x, y, c, me, chips = _mesh_place()
        ici, fwd, fwd_in = [], [], []
        for i in range(n):
            for j, (px, py) in enumerate(chips):
                q = 2 * px + py
                ici.append((pltpu.make_async_remote_copy(
                    src_ref=half(ins[i], c), dst_ref=half(outs[i], c, me), send_sem=send.at[6 * i + j], recv_sem=recv.at[6 * i + j],
                    device_id=(px, py, c), device_id_type=MESH),
                    pltpu.make_async_remote_copy(
                    src_ref=half(ins[i], c), dst_ref=half(outs[i], c, q), send_sem=send.at[6 * i + j], recv_sem=recv.at[6 * i + j],
                    device_id=(px, py, c), device_id_type=MESH)))
                fwd.append(pltpu.make_async_remote_copy(
                    src_ref=half(outs[i], c, q), dst_ref=half(outs[i], c, q), send_sem=send.at[6 * i + 3 + j],
                    recv_sem=recv.at[6 * i + 3 + j], device_id=(x, y, 1 - c), device_id_type=MESH))
                fwd_in.append(pltpu.make_async_remote_copy(
                    src_ref=half(outs[i], 1 - c, q), dst_ref=half(outs[i], 1 - c, q), send_sem=send.at[6 * i + 3 + j],
                    recv_sem=recv.at[6 * i + 3 + j], device_id=(x, y, 1 - c), device_id_type=MESH))
        return ici, fwd, fwd_in, me

    def start(ins, outs, send, recv, loc):
        ici, _, _, me = copies(ins, outs, send, recv)
        for i in range(n):
            pltpu.make_async_copy(ins[i], outs[i].at[me], loc.at[i]).start()
        for out_cp, _ in ici:
            out_cp.start()

    def finish(ins, outs, send, recv, loc):
        ici, fwd, fwd_in, me = copies(ins, outs, send, recv)
        for (_, in_cp), f in zip(ici, fwd):
            in_cp.wait_recv()
            f.start()
        for f in fwd_in:
            f.wait_recv()
        for (out_cp, _), f in zip(ici, fwd):
            out_cp.wait_send()
            f.wait_send()
        for i in range(n):
            pltpu.make_async_copy(ins[i], outs[i].at[me], loc.at[i]).wait()

    return dict(ins=list(shards), out_shape=[jax.ShapeDtypeStruct((N_CHIPS,) + a.shape, a.dtype) for a in shards],
                scratch=[pltpu.SemaphoreType.DMA((6 * n,)), pltpu.SemaphoreType.DMA((6 * n,)), pltpu.SemaphoreType.DMA((n,))],
                start=start, finish=finish)


def _exchange_plug(pieces):
    n = len(pieces)

    def copies(ins, outs, send, recv):
        x, y, c, me, chips = _mesh_place()
        out_cps, in_cps = [], []
        for i in range(n):
            for j, (px, py) in enumerate(chips):
                q = 2 * px + py
                out_cps.append(pltpu.make_async_remote_copy(src_ref=ins[i].at[q], dst_ref=outs[i].at[me], send_sem=send.at[3 * i + j],
                                                            recv_sem=recv.at[3 * i + j], device_id=(px, py, c), device_id_type=MESH))
                in_cps.append(pltpu.make_async_remote_copy(src_ref=ins[i].at[me], dst_ref=outs[i].at[q], send_sem=send.at[3 * i + j],
                                                           recv_sem=recv.at[3 * i + j], device_id=(px, py, c), device_id_type=MESH))
        return out_cps, in_cps, me

    def start(ins, outs, send, recv, loc):
        out_cps, _, me = copies(ins, outs, send, recv)
        for i in range(n):
            pltpu.make_async_copy(ins[i].at[me], outs[i].at[me], loc.at[i]).start()
        for cp in out_cps:
            cp.start()

    def finish(ins, outs, send, recv, loc):
        out_cps, in_cps, me = copies(ins, outs, send, recv)
        for cp in in_cps:
            cp.wait_recv()
        for cp in out_cps:
            cp.wait_send()
        for i in range(n):
            pltpu.make_async_copy(ins[i].at[me], outs[i].at[me], loc.at[i]).wait()

    return dict(ins=list(pieces), out_shape=[jax.ShapeDtypeStruct(a.shape, a.dtype) for a in pieces],
                scratch=[pltpu.SemaphoreType.DMA((3 * n,)), pltpu.SemaphoreType.DMA((3 * n,)), pltpu.SemaphoreType.DMA((n,))],
                start=start, finish=finish)


def _plugged_call(body, plug, first_last, args, *, name, grid, in_specs, out_specs, out_shape, scratch_shapes=(), aliases=None):
    in_specs, out_specs, out_shape, scratch_shapes = list(in_specs), list(out_specs), list(out_shape), list(scratch_shapes)
    aliases = dict(aliases or {})
    if plug is None:
        return pl.pallas_call(body, name=name, grid=grid, in_specs=in_specs, out_specs=out_specs, out_shape=out_shape,
                              scratch_shapes=scratch_shapes, input_output_aliases=aliases)(*args), []
    n_in, n_out, n_sc = len(in_specs), len(out_specs), len(scratch_shapes)
    p_in, p_out = len(plug["ins"]), len(plug["out_shape"])

    def full(*refs):
        ins, refs = refs[:n_in], refs[n_in:]
        pins, refs = refs[:p_in], refs[p_in:]
        outs, refs = refs[:n_out], refs[n_out:]
        pouts, refs = refs[:p_out], refs[p_out:]
        scr, psems = refs[:n_sc], refs[n_sc:]
        first, last = first_last()

        @pl.when(first)
        def _():
            plug["start"](pins, pouts, *psems)

        body(*ins, *outs, *scr)

        @pl.when(last)
        def _():
            plug["finish"](pins, pouts, *psems)

    res = pl.pallas_call(full, name=name, grid=grid, in_specs=in_specs + [_HBM] * p_in, out_specs=out_specs + [_HBM] * p_out,
                         out_shape=out_shape + plug["out_shape"], scratch_shapes=scratch_shapes + plug["scratch"],
                         input_output_aliases=aliases)(*args, *plug["ins"])
    return res[:n_out], res[n_out:]


def _grid_ends(grid):
    def ends():
        first = last = None
        for ax, n in enumerate(grid):
            i = pl.program_id(ax)
            first = (i == 0) if first is None else first & (i == 0)
            last = (i == n - 1) if last is None else last & (i == n - 1)
        return first, last
    return ends


def _pair_presum(pieces, name):
    n, r, cols = pieces.shape
    by_cols = (cols // 2) % LANES == 0
    half_shape = (n, r, cols // 2) if by_cols else (n, r // 2, cols)

    def body(p_ref, o_ref, mine_ref, land_ref, send_sem, recv_sem, local_sem):
        x, y, c = lax.axis_index("x"), lax.axis_index("y"), lax.axis_index("c")

        def half(which):
            if by_cols:
                return p_ref.at[:, :, pl.ds(pl.multiple_of(which * (cols // 2), LANES), cols // 2)]
            return p_ref.at[:, pl.ds(pl.multiple_of(which * (r // 2), 16), r // 2), :]

        push = pltpu.make_async_remote_copy(src_ref=half(1 - c), dst_ref=land_ref, send_sem=send_sem, recv_sem=recv_sem,
                                            device_id=(x, y, 1 - c), device_id_type=MESH)
        own = pltpu.make_async_copy(half(c), mine_ref, local_sem)
        push.start()
        own.start()
        own.wait()
        push.wait_recv()
        for i in range(n):
            o_ref[i] = (mine_ref[i].astype(F32) + land_ref[i].astype(F32)).astype(BF16)
        push.wait_send()

    return pl.pallas_call(
        body, name=name, out_shape=jax.ShapeDtypeStruct(half_shape, BF16), in_specs=[_HBM],
        out_specs=pl.BlockSpec(memory_space=pltpu.VMEM),
        scratch_shapes=[pltpu.VMEM(half_shape, BF16), pltpu.VMEM(half_shape, BF16), pltpu.SemaphoreType.DMA,
                        pltpu.SemaphoreType.DMA, pltpu.SemaphoreType.DMA],
    )(pieces)


def _sibling_plug(arrs):
    n = len(arrs)

    def copies(ins, outs, send, recv):
        sibling = (lax.axis_index("x"), lax.axis_index("y"), 1 - lax.axis_index("c"))
        return [pltpu.make_async_remote_copy(src_ref=ins[i], dst_ref=outs[i], send_sem=send.at[i], recv_sem=recv.at[i],
                                             device_id=sibling, device_id_type=MESH) for i in range(n)]

    def start(ins, outs, send, recv):
        for cp in copies(ins, outs, send, recv):
            cp.start()

    def finish(ins, outs, send, recv):
        for cp in copies(ins, outs, send, recv):
            cp.wait()

    return dict(ins=list(arrs), out_shape=[jax.ShapeDtypeStruct(a.shape, a.dtype) for a in arrs],
                scratch=[pltpu.SemaphoreType.DMA((n,)), pltpu.SemaphoreType.DMA((n,))], start=start, finish=finish)


_SEGS = ((0, AQ, C_Q), (AQ, AKV, C_K), (AQ + AKV, AKV, C_V), (AQ + 2 * AKV, 3 * DN, C_DQKV), (2304, DN, C_DZ),
         (2816, 2 * DN_HEADS, C_BD), (2824, D_MODEL, C_GA), (3848, D_MODEL, C_GD))
SHARD_ROWS = IN_DIM // N_CHIPS


def _to_padded(w4):
    wt = w4.reshape(IN_DIM, w4.shape[2])
    parts = [wt[o:o + n] for o, n, _ in sorted(_SEGS, key=lambda sg: sg[2])]
    return jnp.concatenate(parts + [jnp.zeros((IN_PAD - IN_DIM, wt.shape[1]), wt.dtype)], axis=0)


def _from_padded(gt):
    return jnp.concatenate([gt[ps:ps + n] for _, n, ps in sorted(_SEGS)], axis=0).reshape(N_CHIPS, SHARD_ROWS, gt.shape[1])


def _lane_vec(a):
    return jnp.zeros((1, LANES), F32).at[0, DN_HEADS:2 * DN_HEADS].set(a)


_ROW_SHARDED = ("w_in", "w_out", "ffn_w_down")
_FFN = ("ffn_w_up", "ffn_w_down")
_LATE_MIXER = ("w_attn_branch", "w_dn_branch", "w_out")
_PRESUM = ("w_in", "ffn_w_up")


def _pieces(k, a):
    if a.ndim == 3:
        return a
    if k in _ROW_SHARDED:
        return a.reshape(N_CHIPS, a.shape[0] // N_CHIPS, a.shape[1]).astype(BF16)
    return jnp.transpose(a.reshape(a.shape[0], N_CHIPS, a.shape[1] // N_CHIPS), (1, 0, 2)).astype(BF16)


def _assemble(k, a):
    if k in _ROW_SHARDED:
        return a.reshape(-1, a.shape[2])
    return jnp.transpose(a, (1, 0, 2)).reshape(a.shape[1], -1)


def _device_step(x2, tgt2, mod, p, bsz, shards=None):
    d = D_MODEL
    on_mesh = shards is not None
    p = dict(p)
    sh1, sc1, g1, sh2, sc2, g2 = [mod[:, i * d:(i + 1) * d].reshape(bsz, 1, d) for i in range(N_MOD)]
    alog_v, dt_v = _lane_vec(p["dn_a_log"]), _lane_vec(p["dn_dt_bias"])
    sinks = p["attn_sinks"].reshape(1, AQ_HEADS)
    if on_mesh:
        u1, got = _pre_fwd(x2, p["norm_mix_pre"], sc1, sh1, "pre1_fwd", plug=_gather_plug(shards["w_in_last"]))
        p["w_in"] = _to_padded(jnp.concatenate(list(shards["w_in_gathered"]) + list(got), axis=1))
    else:
        u1 = _pre_fwd(x2, p["norm_mix_pre"], sc1, sh1, "pre1_fwd")
    if on_mesh:
        proj, got = _mm(u1, p["w_in"], "nt", F32, "mm_proj", _gather_plug(shards["late_mixer"]))
        p.update({k: _assemble(k, a) for k, a in zip(_LATE_MIXER, got)})
    else:
        proj = _mm(u1, p["w_in"], "nt", F32, "mm_proj")
    bias = _bias_build(p["rel_bias"])
    y_attn = _attn_fwd(proj, bias, sinks, bsz)
    qkvn, dn_conv_out = _dn_prep_fwd(proj, p["dn_conv_w"], bsz)
    bg = _bg_fwd(proj, alog_v, dt_v, bsz)
    nc = x2.shape[0] // bsz // DN_CHUNK
    g_rows = jnp.transpose(bg[:, DN_HEADS:2 * DN_HEADS].reshape(bsz, nc, DN_CHUNK, DN_HEADS), (0, 3, 1, 2))
    o, states, got = _dn_chunk_fwd(qkvn, bg, g_rows, bsz, _gather_plug(shards["ffn"][:1]) if on_mesh else None)
    for k, a in zip(_FFN[:1], got):
        p[k] = _assemble(k, a)
    y_dn = _dn_out_fwd(o, proj, p["dn_norm_w"])
    ya = _mm(y_attn, p["w_attn_branch"], "nn", BF16, "mm_ya")
    yd = _mm(y_dn, p["w_dn_branch"], "nn", BF16, "mm_yd")
    merged = _merge_fwd(proj, ya, yd)
    y1 = _mm(merged, p["w_out"], "nn", F32, "mm_y1")
    h1, u2 = _post_pre_fwd(x2, y1, p["norm_mix_post"], g1, p["norm_ffn_pre"], sc2, sh2, "post1_pre2_fwd")
    if on_mesh:
        up, got = _mm(u2, p["ffn_w_up"], "nn", BF16, "mm_up", _gather_plug(shards["ffn"][1:]))
        p["ffn_w_down"] = _assemble("ffn_w_down", got[0])
    else:
        up = _mm(u2, p["ffn_w_up"], "nn", BF16, "mm_up")
    act, conv_g, conv_v = _ffn_act_fwd(up, p["ffn_conv_w"], bsz)
    y2 = _mm(act, p["ffn_w_down"], "nn", F32, "mm_y2")
    dh2, dy2, g_ffn_post, dg2, sq = _post_loss_bwd(h1, y2, p["norm_ffn_post"], g2, tgt2, "post2_loss_bwd")
    g = {}
    g["norm_ffn_post"] = g_ffn_post
    dact = _mm(dy2, p["ffn_w_down"], "nt", BF16, "mm_dact")
    g["ffn_w_down"] = _mm(act, dy2, "tn", BF16, "mm_dwdown")
    dupg, dupv, dcwg, dcwv, got_down = _ffn_act_bwd(
        up, p["ffn_conv_w"], (conv_g, conv_v), dact, bsz,
        _exchange_plug([_pieces("ffn_w_down", g["ffn_w_down"])]) if on_mesh else None)
    g["ffn_conv_w"] = jnp.concatenate([dcwg, dcwv], axis=1)
    g["ffn_w_up"] = jnp.concatenate([_mm(u2, dupg, "tn", BF16, "mm_dwup_gate", split=N_CHIPS // 2),
                                     _mm(u2, dupv, "tn", BF16, "mm_dwup_val", split=N_CHIPS // 2)], axis=0)
    du2 = _mm(dupg, p["ffn_w_up"], "nt", F32, "mm_du2_gate", b_kblock=0)
    du2 = _mm(dupv, p["ffn_w_up"], "nt", BF16, "mm_du2_val", b_kblock=1, add=du2)
    dh1, dy1, g["norm_ffn_pre"], dsc2, dsh2, g["norm_mix_post"], dg1 = _pre_post_bwd(
        h1, p["norm_ffn_pre"], sc2, sh2, du2, dh2, y1, p["norm_mix_post"], g1, "pre2_post1_bwd")
    dmerged = _mm(dy1, p["w_out"], "nt", BF16, "mm_dmerged")
    g["w_out"] = _mm(merged, dy1, "tn", BF16, "mm_dwout")
    dproj = lax.empty((x2.shape[0], IN_PAD), BF16)
    dproj, dya = _branch_bwd(dproj, proj, ya, dmerged, C_GA, "merge_bwd_attn")
    dproj, dyd = _branch_bwd(dproj, proj, yd, dmerged, C_GD, "merge_bwd_dn")
    dy_attn = _mm(dya, p["w_attn_branch"], "nt", BF16, "mm_dyattn")
    g["w_attn_branch"] = _mm(y_attn, dya, "tn", BF16, "mm_dwab", split=N_CHIPS)
    dy_dn = _mm(dyd, p["w_dn_branch"], "nt", BF16, "mm_dydn")
    g["w_dn_branch"] = _mm(y_dn, dyd, "tn", BF16, "mm_dwdb", split=N_CHIPS)
    do, dproj, g["dn_norm_w"] = _dn_out_bwd(dproj, o, proj, p["dn_norm_w"], dy_dn)
    def plug_for(names):
        if not on_mesh:
            return None
        return _exchange_plug([_pair_presum(_pieces(k, g[k]), "presum_" + k) if k in _PRESUM else _pieces(k, g[k]) for k in names])

    early = ("w_out", "w_attn_branch", "w_dn_branch")
    dqkvn, dbg4, got_up = _dn_chunk_bwd(qkvn, bg, g_rows, states, do, bsz, plug_for(_FFN[:1]))
    got_ffn = list(got_up) + list(got_down)
    dproj, g["dn_conv_w"] = _dn_prep_bwd(dproj, proj, p["dn_conv_w"], dn_conv_out, dqkvn, bsz)
    dproj, dk, dv, dbias, g["attn_sinks"], got_early = _attn_bwd(dproj, proj, bias, sinks, dy_attn, bsz, plug_for(early))
    dproj, g["dn_a_log"], g["dn_dt_bias"] = _bg_bwd(dproj, proj, alog_v, dt_v, dbg4, dk, dv, bsz)
    g["rel_bias"] = _bias_grad(dbias)
    g["w_in"] = _from_padded(_mm(dproj, u1, "tn", BF16, "mm_dwin"))
    if on_mesh:
        du1, got_in = _mm(dproj, p["w_in"], "nn", BF16, "mm_du1", plug_for(("w_in",)))
        g.update(zip(_FFN + early + ("w_in",), list(got_ffn) + list(got_early) + list(got_in)))
    else:
        du1 = _mm(dproj, p["w_in"], "nn", BF16, "mm_du1")
    dx, g["norm_mix_pre"], dsc1, dsh1 = _pre_bwd(x2, p["norm_mix_pre"], sc1, sh1, du1, dh1, "pre1_bwd")
    dmod = jnp.concatenate([dsh1, dsc1, dg1, dsh2, dsc2, dg2], axis=-1).reshape(bsz, N_MOD * d)
    return sq, dx, dmod, g


_SMALL = (("norm_mix_pre", D_MODEL), ("norm_mix_post", D_MODEL), ("norm_ffn_pre", D_MODEL), ("norm_ffn_post", D_MODEL),
          ("dn_norm_w", DN_HD), ("dn_a_log", LANES), ("dn_dt_bias", LANES), ("attn_sinks", AQ_HEADS * LANES),
          ("rel_bias", AQ_HEADS * LANES), ("dn_conv_w", DN_CONV * 3 * DN), ("ffn_conv_w", FFN_CONV * 2 * D_FF),
          ("loss_sq", D_MODEL))


def _pack_rows(parts, rows):
    flat = jnp.concatenate([a.reshape(-1) for a in parts])
    return jnp.concatenate([flat, jnp.zeros((rows * LANES - flat.shape[0],), F32)]).reshape(rows, LANES)


_W_NAMES = ("ada_w", "ada_b", "norm_mix_pre", "norm_mix_post", "norm_ffn_pre", "norm_ffn_post", "w_in", "dn_conv_w", "dn_a_log",
            "dn_dt_bias", "dn_norm_w", "attn_sinks", "rel_bias", "w_attn_branch", "w_dn_branch", "w_out", "ffn_w_up", "ffn_conv_w",
            "ffn_w_down")
_BIG = ("w_in", "w_attn_branch", "w_dn_branch", "w_out", "ffn_w_up", "ffn_w_down")


def kernel(x, c, *rest):
    nw = len(_W_NAMES)
    w = dict(zip(_W_NAMES, rest[:nw]))
    loss_target = rest[nw]
    m = dict(zip(_W_NAMES, rest[nw + 1:2 * nw + 1]))
    v = dict(zip(_W_NAMES, rest[2 * nw + 1:3 * nw + 1]))
    ix, iy, ic = lax.axis_index("x"), lax.axis_index("y"), lax.axis_index("c")
    chip, dev = 2 * ix + iy, 4 * ix + 2 * iy + ic
    bsz, s, d = x.shape
    t = bsz * s
    n_dev = 8

    front_rows = 64
    front = _pack_rows([c, w["dn_conv_w"], w["ffn_conv_w"]], front_rows)
    w_in_t = jnp.swapaxes(w["w_in"][0], 0, 1).astype(BF16)
    cut1 = SHARD_ROWS // 48 * 16
    cut2 = 2 * cut1
    front_all, (w_in_lo,) = _allgather8(front, "ag_front", _gather_plug([w_in_t[:cut1]]))
    front_all = front_all.reshape(n_dev, front_rows * LANES)
    n_c, n_dc, n_fc = bsz * d, DN_CONV * 3 * DN // N_CHIPS, FFN_CONV * 2 * D_FF // N_CHIPS
    c_all = front_all[:, :n_c].reshape(n_dev * bsz, d)
    per_chip = front_all[0::2]
    dn_conv_full = jnp.transpose(per_chip[:, n_c:n_c + n_dc].reshape(N_CHIPS, DN_CONV, -1), (1, 0, 2)).reshape(DN_CONV, 3 * DN)
    ffn_conv_full = jnp.transpose(per_chip[:, n_c + n_dc:n_c + n_dc + n_fc].reshape(N_CHIPS, FFN_CONV, -1), (1, 0, 2)).reshape(FFN_CONV, 2 * D_FF)

    mod_cols = N_MOD * d // N_CHIPS
    ada_b_loc = lax.dynamic_slice(w["ada_b"], (0, chip * mod_cols), (1, mod_cols))
    mod_part = _mod_fwd(c_all, w["ada_w"][0], ada_b_loc)
    mod_all, (w_in_mid,) = _allgather8(mod_part, "ag_mod", _gather_plug([w_in_t[cut1:cut2]]))
    mod_all = mod_all.reshape(n_dev, n_dev * bsz, mod_cols)[0::2]
    mod = jnp.transpose(lax.dynamic_slice(mod_all, (0, dev * bsz, 0), (N_CHIPS, bsz, mod_cols)), (1, 0, 2)).reshape(bsz, N_MOD * d)

    p = {}
    shards = {"late_mixer": [w[k][0].astype(BF16) for k in _LATE_MIXER], "ffn": [w[k][0].astype(BF16) for k in _FFN],
              "w_in_gathered": [w_in_lo, w_in_mid], "w_in_last": [w_in_t[cut2:]]}
    for k in ("norm_mix_pre", "norm_mix_post", "norm_ffn_pre", "norm_ffn_post", "dn_norm_w", "attn_sinks"):
        p[k] = w[k]
    p["dn_a_log"], p["dn_dt_bias"], p["rel_bias"] = w["dn_a_log"][0], w["dn_dt_bias"][0], w["rel_bias"]
    p["dn_conv_w"], p["ffn_conv_w"] = dn_conv_full, ffn_conv_full

    sq, dx, dmod, g = _device_step(x.reshape(t, d), loss_target.reshape(t, d), mod, p, bsz, shards)
    g["dn_a_log"], g["dn_dt_bias"], g["loss_sq"] = g["dn_a_log"].reshape(-1), g["dn_dt_bias"].reshape(-1), sq
    small_rows = 336
    small = _pack_rows([dmod] + [g[k] for k, _ in _SMALL], small_rows)
    small_all = _allgather8(small, "ag_small").reshape(n_dev, small_rows, LANES)
    n_dm = bsz * N_MOD * d
    dmod_all = small_all.reshape(n_dev, -1)[:, :n_dm].reshape(n_dev * bsz, N_MOD * d)
    tot = _sum_lead(small_all, "sum_small").reshape(-1)
    gs, off = {}, n_dm
    for k, n in _SMALL:
        gs[k] = tot[off:off + n]
        off += n
    loss = 0.5 * jnp.sum(gs["loss_sq"])
    grad = {}
    grad["ada_w"], grad["ada_b"] = _ada_grad(c_all, lax.dynamic_slice(dmod_all, (0, chip * mod_cols), (n_dev * bsz, mod_cols)), dmod_all)
    for k in ("norm_mix_pre", "norm_mix_post", "norm_ffn_pre", "norm_ffn_post", "dn_norm_w"):
        grad[k] = gs[k]
    grad["dn_a_log"] = gs["dn_a_log"][DN_HEADS:2 * DN_HEADS]
    grad["dn_dt_bias"] = gs["dn_dt_bias"][DN_HEADS:2 * DN_HEADS]
    grad["attn_sinks"] = gs["attn_sinks"].reshape(AQ_HEADS, LANES)[:, 0]
    grad["rel_bias"] = gs["rel_bias"].reshape(AQ_HEADS, LANES)[:, :REL_BUCKETS].T
    grad["dn_conv_w"] = lax.dynamic_slice(gs["dn_conv_w"].reshape(DN_CONV, 3 * DN), (0, chip * (3 * DN // N_CHIPS)), (DN_CONV, 3 * DN // N_CHIPS))
    grad["ffn_conv_w"] = lax.dynamic_slice(gs["ffn_conv_w"].reshape(FFN_CONV, 2 * D_FF), (0, chip * (2 * D_FF // N_CHIPS)), (FFN_CONV, 2 * D_FF // N_CHIPS))

    mine = [_sum_lead(g[k], "sum_" + k, rows_apart=(k == "w_in")) for k in _BIG]
    out = {}
    out["ada_w"], theirs = _adamw(w["ada_w"][0], m["ada_w"][0], v["ada_w"][0], [grad["ada_w"]], "adamw_ada_w", _sibling_plug(mine))
    for k, a, b in zip(_BIG, mine, theirs):
        core = ic.reshape(1).astype(jnp.int32)
        if k == "w_in":
            tr = lambda z: jnp.transpose(z, (2, 0, 1))
            out[k] = [jnp.transpose(r, (1, 2, 0)) for r in _adamw_halves(tr(w[k]), tr(m[k]), tr(v[k]), a, b, core, "adamw_" + k)]
        elif k in _PRESUM:
            out[k] = _adamw_halves(w[k][0], m[k][0], v[k][0], a, b, core, "adamw_" + k)
        else:
            out[k] = _adamw(w[k][0], m[k][0], v[k][0], [a, b], "adamw_" + k)
    small_names = [k for k in _W_NAMES if k not in _BIG and k != "ada_w"]
    res_small = _adamw_small([w[k] for k in small_names], [m[k] for k in small_names], [v[k] for k in small_names],
                             [grad[k].reshape(w[k].shape) for k in small_names], "adamw_small")
    out.update(zip(small_names, res_small))
    for k in _BIG + ("ada_w",):
        out[k] = [r.reshape(w[k].shape) for r in out[k]]
    grads, deltas, new_m, new_v = ([out[k][i] for k in _W_NAMES] for i in range(4))
    return (loss, dx.reshape(bsz, s, d), *grads, *deltas, *new_m, *new_v)
```

```python
import math

import numpy as np
import jax
import jax.numpy as jnp
from jax import lax
from jax.experimental import pallas as pl
from jax.experimental.pallas import tpu as pltpu

F32 = jnp.float32
BF16 = jnp.bfloat16
MESH = pl.DeviceIdType.MESH

D_MODEL = 1024
N_MOD = 6
AQ_HEADS, AKV_HEADS, A_HD, WINDOW = 8, 2, 64, 128
REL_BUCKETS, REL_MAX_DIST = 32, 128
DN_HEADS, DN_HD, DN_CONV, DN_CHUNK = 4, 128, 4, 64
D_FF, FFN_CONV = 2816, 3
RMS_EPS, L2_EPS, NEG_INF = 1e-6, 1e-6, -1e30
AQ, AKV, DN = AQ_HEADS * A_HD, AKV_HEADS * A_HD, DN_HEADS * DN_HD
IN_DIM = AQ + 2 * AKV + 3 * DN + DN + 2 * DN_HEADS + 2 * D_MODEL
C_DQKV, C_Q, C_DZ, C_GA, C_GD, C_K, C_V, C_BD = 0, 1536, 2048, 2560, 3584, 4608, 4736, 4864
IN_PAD = 4992
LANES = 128
N_CHIPS = 4

ADAM_LR, ADAM_B1, ADAM_B2, ADAM_EPS, ADAM_WD, ADAM_STEP = 0.001, 0.9, 0.999, 1e-08, 0.01, 10


def _vspec(shape, index_map):
    return pl.BlockSpec(shape, index_map)


MM_VMEM_BUDGET = 40 * 2 ** 20
GRID_STEP_S = 0.35e-6
HBM_BYTES_PER_S = 3.0e12
MXU_FLOPS_PER_S = 9.0e14
MXU_DIM = 256


def _mm_tiles(m, n, k, mode, in_bytes, out_bytes, split=1):
    best = None
    for tm in [t for t in range(LANES, m + 1, LANES) if m % t == 0]:
        for tn in [t for t in range(LANES, n // split + 1, LANES) if (n // split) % t == 0]:
            a, b, o = k * tm * in_bytes, k * tn * in_bytes, tm * tn * out_bytes
            if 2 * (a + b + o) + (a if mode == "tn" else 0) > MM_VMEM_BUDGET:
                continue
            hbm_s = (m * k * in_bytes + (m // tm) * n * k * in_bytes + m * n * out_bytes) / HBM_BYTES_PER_S
            mxu_s = 2 * m * n * k / (MXU_FLOPS_PER_S * min(1.0, tm / MXU_DIM) * min(1.0, tn / MXU_DIM))
            cost = (m // tm) * (n // tn) * GRID_STEP_S + max(hbm_s, mxu_s)
            if best is None or cost < best[0]:
                best = (cost, tm, tn)
    return best[1], best[2]


def _mm(a, b, mode, out_dtype, name, plug=None, split=1, b_kblock=0, add=None):
    if mode == "nn":
        (m, k), n = a.shape, b.shape[1]
        dims = (((1,), (0,)), ((), ()))
    elif mode == "nt":
        (m, k), n = a.shape, b.shape[0]
        dims = (((1,), (1,)), ((), ()))
    else:
        (k, m), n = a.shape, b.shape[1]
        dims = (((0,), (0,)), ((), ()))
    tm, tn = _mm_tiles(m, n, k, mode, a.dtype.itemsize, jnp.dtype(out_dtype).itemsize, split)
    if mode == "tn":
        a_spec = _vspec((k, tm), lambda i, j: (0, i))
    else:
        a_spec = _vspec((tm, k), lambda i, j: (i, 0))
    if mode == "nt":
        b_spec = _vspec((tn, k), lambda i, j: (j, b_kblock))
    else:
        b_spec = _vspec((k, tn), lambda i, j: (0, j))
    in_specs, args = [a_spec, b_spec], (a, b)
    if add is not None:
        in_specs, args = in_specs + [_vspec((tm, tn), lambda i, j: (i, j))], (a, b, add)

    def body(a_ref, b_ref, *rest):
        o_ref = rest[-1]
        acc = lax.dot_general(a_ref[...].astype(BF16), b_ref[...].astype(BF16), dims, preferred_element_type=F32)
        if add is not None:
            acc = acc + rest[0][...]
        o_ref[...] = acc.astype(out_dtype).reshape(o_ref.shape)

    grid = (m // tm, n // tn)
    if split == 1:
        out_spec, out_shape = _vspec((tm, tn), lambda i, j: (i, j)), (m, n)
    else:
        per = n // split // tn
        out_spec, out_shape = _vspec((1, tm, tn), lambda i, j: (j // per, i, j % per)), (split, m, n // split)
    (out,), extra = _plugged_call(body, plug, _grid_ends(grid), args, name=name, grid=grid, in_specs=in_specs,
                                  out_specs=[out_spec], out_shape=[jax.ShapeDtypeStruct(out_shape, out_dtype)])
    return out if plug is None else (out, extra)


def _rms(x, w):
    return (x * lax.rsqrt(jnp.mean(x * x, axis=-1, keepdims=True) + RMS_EPS)) * w


def _pre_f(x, w, sc, sh):
    return _rms(x, w) * (1.0 + sc) + sh


def _post_f(y, w, g):
    return g * _rms(y, w)


def _tok_grid(t, bsz, ts):
    nt = t // bsz // ts
    return nt, (bsz, nt)


def _pre_fwd(x, w, sc, sh, name, ts=512, plug=None):
    t, d = x.shape
    bsz = sc.shape[0]
    nt, grid = _tok_grid(t, bsz, ts)
    row = _vspec((ts, d), lambda b, i: (b * nt + i, 0))
    vec = _vspec((1, d), lambda b, i: (0, 0))
    bvec = _vspec((1, 1, d), lambda b, i: (b, 0, 0))

    def body(x_ref, w_ref, sc_ref, sh_ref, u_ref):
        u_ref[...] = _pre_f(x_ref[...], w_ref[...], sc_ref[0], sh_ref[0]).astype(BF16)

    (u,), extra = _plugged_call(body, plug, _grid_ends(grid), (x, w, sc, sh), name=name, grid=grid, in_specs=[row, vec, bvec, bvec],
                                out_specs=[row], out_shape=[jax.ShapeDtypeStruct((t, d), BF16)])
    return u if plug is None else (u, extra)


def _pre_bwd(x, w, sc, sh, du, dres, name, ts=512):
    t, d = x.shape
    bsz = sc.shape[0]
    nt, grid = _tok_grid(t, bsz, ts)
    row = _vspec((ts, d), lambda b, i: (b * nt + i, 0))
    vec = _vspec((1, d), lambda b, i: (0, 0))
    bvec = _vspec((1, 1, d), lambda b, i: (b, 0, 0))

    def body(x_ref, w_ref, sc_ref, sh_ref, du_ref, dres_ref, dx_ref, dw_ref, dsc_ref, dsh_ref):
        b, i = pl.program_id(0), pl.program_id(1)
        _, vjp = jax.vjp(_pre_f, x_ref[...], w_ref[...], sc_ref[0], sh_ref[0])
        dx, dw, dsc, dsh = vjp(du_ref[...].astype(F32))
        dx_ref[...] = dres_ref[...] + dx

        @pl.when((b == 0) & (i == 0))
        def _():
            dw_ref[...] = jnp.zeros_like(dw_ref)

        @pl.when(i == 0)
        def _():
            dsc_ref[...] = jnp.zeros_like(dsc_ref)
            dsh_ref[...] = jnp.zeros_like(dsh_ref)

        dw_ref[...] += dw
        dsc_ref[0] += dsc
        dsh_ref[0] += dsh

    return pl.pallas_call(
        body, name=name, grid=grid, in_specs=[row, vec, bvec, bvec, row, row], out_specs=[row, vec, bvec, bvec],
        out_shape=[jax.ShapeDtypeStruct((t, d), F32), jax.ShapeDtypeStruct((1, d), F32),
                   jax.ShapeDtypeStruct((bsz, 1, d), F32), jax.ShapeDtypeStruct((bsz, 1, d), F32)],
    )(x, w, sc, sh, du, dres)


def _accumulate(ref, val, first):
    @pl.when(first)
    def _():
        ref[...] = jnp.zeros_like(ref)

    ref[...] += val.reshape(ref.shape)


def _post_pre_fwd(res, y, w_post, g, w_pre, sc, sh, name, ts=512):
    t, d = y.shape
    bsz = g.shape[0]
    nt, grid = _tok_grid(t, bsz, ts)
    row = _vspec((ts, d), lambda b, i: (b * nt + i, 0))
    vec = _vspec((1, d), lambda b, i: (0, 0))
    bvec = _vspec((1, 1, d), lambda b, i: (b, 0, 0))

    def body(res_ref, y_ref, wp_ref, g_ref, w_ref, sc_ref, sh_ref, h_ref, u_ref):
        h = res_ref[...] + _post_f(y_ref[...].astype(F32), wp_ref[...], g_ref[0])
        h_ref[...] = h
        u_ref[...] = _pre_f(h, w_ref[...], sc_ref[0], sh_ref[0]).astype(BF16)

    return pl.pallas_call(body, name=name, grid=grid, in_specs=[row, row, vec, bvec, vec, bvec, bvec], out_specs=[row, row],
                          out_shape=[jax.ShapeDtypeStruct((t, d), F32), jax.ShapeDtypeStruct((t, d), BF16)],
                          )(res, y, w_post, g, w_pre, sc, sh)


def _post_loss_bwd(res, y, w, g, tgt, name, ts=512):
    t, d = y.shape
    bsz = g.shape[0]
    nt, grid = _tok_grid(t, bsz, ts)
    row = _vspec((ts, d), lambda b, i: (b * nt + i, 0))
    vec = _vspec((1, d), lambda b, i: (0, 0))
    bvec = _vspec((1, 1, d), lambda b, i: (b, 0, 0))

    def body(res_ref, y_ref, w_ref, g_ref, tgt_ref, dh_ref, dy_ref, dw_ref, dg_ref, sq_ref):
        b, i = pl.program_id(0), pl.program_id(1)
        part, vjp = jax.vjp(_post_f, y_ref[...].astype(F32), w_ref[...], g_ref[0])
        e = res_ref[...] + part - tgt_ref[...]
        dh = e * (1.0 / d)
        dh_ref[...] = dh
        dy, dw, dg = vjp(dh)
        dy_ref[...] = dy.astype(BF16)
        _accumulate(dw_ref, dw, (b == 0) & (i == 0))
        _accumulate(dg_ref, dg, i == 0)
        _accumulate(sq_ref, jnp.sum(e * e, axis=0, keepdims=True) * (1.0 / d), (b == 0) & (i == 0))

    return pl.pallas_call(
        body, name=name, grid=grid, in_specs=[row, row, vec, bvec, row], out_specs=[row, row, vec, bvec, vec],
        out_shape=[jax.ShapeDtypeStruct((t, d), F32), jax.ShapeDtypeStruct((t, d), BF16), jax.ShapeDtypeStruct((1, d), F32),
                   jax.ShapeDtypeStruct((bsz, 1, d), F32), jax.ShapeDtypeStruct((1, d), F32)],
    )(res, y, w, g, tgt)


def _pre_post_bwd(x, w, sc, sh, du, dres, y, w_post, g, name, ts=512):
    t, d = x.shape
    bsz = sc.shape[0]
    nt, grid = _tok_grid(t, bsz, ts)
    row = _vspec((ts, d), lambda b, i: (b * nt + i, 0))
    vec = _vspec((1, d), lambda b, i: (0, 0))
    bvec = _vspec((1, 1, d), lambda b, i: (b, 0, 0))

    def body(x_ref, w_ref, sc_ref, sh_ref, du_ref, dres_ref, y_ref, wp_ref, g_ref,
             dx_ref, dy_ref, dw_ref, dsc_ref, dsh_ref, dwp_ref, dg_ref):
        b, i = pl.program_id(0), pl.program_id(1)
        _, vjp = jax.vjp(_pre_f, x_ref[...], w_ref[...], sc_ref[0], sh_ref[0])
        dx, dw, dsc, dsh = vjp(du_ref[...].astype(F32))
        dx = dres_ref[...] + dx
        dx_ref[...] = dx
        _, vjp_post = jax.vjp(_post_f, y_ref[...].astype(F32), wp_ref[...], g_ref[0])
        dy, dwp, dg = vjp_post(dx)
        dy_ref[...] = dy.astype(BF16)
        first = (b == 0) & (i == 0)
        _accumulate(dw_ref, dw, first)
        _accumulate(dwp_ref, dwp, first)
        _accumulate(dsc_ref, dsc, i == 0)
        _accumulate(dsh_ref, dsh, i == 0)
        _accumulate(dg_ref, dg, i == 0)

    v1, vb = jax.ShapeDtypeStruct((1, d), F32), jax.ShapeDtypeStruct((bsz, 1, d), F32)
    return pl.pallas_call(
        body, name=name, grid=grid, in_specs=[row, vec, bvec, bvec, row, row, row, vec, bvec],
        out_specs=[row, row, vec, bvec, bvec, vec, bvec],
        out_shape=[jax.ShapeDtypeStruct((t, d), F32), jax.ShapeDtypeStruct((t, d), BF16), v1, vb, vb, v1, vb],
    )(x, w, sc, sh, du, dres, y, w_post, g)


def _merge_f(ga, gd, ya, yd):
    return jax.nn.sigmoid(ga) * ya + jax.nn.sigmoid(gd) * yd


_MW = 512


def _merge_fwd(proj, ya, yd, ts=512):
    t, d = ya.shape
    blk = _vspec((ts, _MW), lambda i, j: (i, j))
    ga = _vspec((ts, _MW), lambda i, j: (i, C_GA // _MW + j))
    gd = _vspec((ts, _MW), lambda i, j: (i, C_GD // _MW + j))

    def body(ga_ref, gd_ref, ya_ref, yd_ref, o_ref):
        o_ref[...] = _merge_f(ga_ref[...], gd_ref[...], ya_ref[...].astype(F32), yd_ref[...].astype(F32)).astype(BF16)

    return pl.pallas_call(body, name="merge_fwd", grid=(t // ts, d // _MW), in_specs=[ga, gd, blk, blk], out_specs=blk,
                          out_shape=jax.ShapeDtypeStruct((t, d), BF16))(proj, proj, ya, yd)


_ANY = pl.BlockSpec(memory_space=pl.ANY)


def _branch_bwd(dproj, proj, y, dm, col0, name, ts=512):
    t, d = y.shape
    blk = _vspec((ts, _MW), lambda i, j: (i, j))
    gate = _vspec((ts, _MW), lambda i, j: (i, col0 // _MW + j))

    def body(buf_ref, g_ref, y_ref, dm_ref, dg_ref, dy_ref):
        del buf_ref
        _, vjp = jax.vjp(lambda g, yy: jax.nn.sigmoid(g) * yy, g_ref[...], y_ref[...].astype(F32))
        dg, dy = vjp(dm_ref[...].astype(F32))
        dg_ref[...] = dg.astype(BF16)
        dy_ref[...] = dy.astype(BF16)

    return pl.pallas_call(body, name=name, grid=(t // ts, d // _MW), in_specs=[_ANY, gate, blk, blk], out_specs=[gate, blk],
                          out_shape=[jax.ShapeDtypeStruct(dproj.shape, BF16), jax.ShapeDtypeStruct((t, d), BF16)],
                          input_output_aliases={0: 0})(dproj, proj, y, dm)


def _shift_down(x, s):
    if s == 0:
        return x
    r = lax.broadcasted_iota(jnp.int32, x.shape, 0)
    return jnp.where(r >= s, pltpu.roll(x, s, 0), 0.0)


def _shift_up(x, s):
    if s == 0:
        return x
    n = x.shape[0]
    r = lax.broadcasted_iota(jnp.int32, x.shape, 0)
    return jnp.where(r < n - s, pltpu.roll(x, n - s, 0), 0.0)


def _conv_fwd(x, w, k):
    out = None
    for j in range(k):
        term = w[j:j + 1, :] * _shift_down(x, k - 1 - j)
        out = term if out is None else out + term
    return out


def _conv_bwd(x, w, dc, k):
    dx = None
    dws = []
    for j in range(k):
        up = _shift_up(dc, k - 1 - j)
        term = w[j:j + 1, :] * up
        dx = term if dx is None else dx + term
        dws.append(jnp.sum(up * x, axis=0, keepdims=True))
    return dx, jnp.concatenate(dws, axis=0)


def _geglu_f(gate, val):
    return jax.nn.gelu(gate, approximate=True) * val


_FW = 256


def _ffn_act_fwd(up, conv_w, bsz):
    t = up.shape[0]
    s = t // bsz
    nj = D_FF // _FW
    xg = _vspec((s, _FW), lambda b, j: (b, j))
    xv = _vspec((s, _FW), lambda b, j: (b, nj + j))
    wg = _vspec((FFN_CONV, _FW), lambda b, j: (0, j))
    wv = _vspec((FFN_CONV, _FW), lambda b, j: (0, nj + j))

    def body(xg_ref, xv_ref, wg_ref, wv_ref, o_ref, cg_ref, cv_ref):
        gate = _conv_fwd(xg_ref[...].astype(F32), wg_ref[...], FFN_CONV)
        val = _conv_fwd(xv_ref[...].astype(F32), wv_ref[...], FFN_CONV)
        o_ref[...] = _geglu_f(gate, val).astype(BF16)
        cg_ref[...] = gate.astype(BF16)
        cv_ref[...] = val.astype(BF16)

    blk = _vspec((s, _FW), lambda b, j: (b, j))
    o = jax.ShapeDtypeStruct((t, D_FF), BF16)
    return pl.pallas_call(body, name="ffn_act_fwd", grid=(bsz, nj), in_specs=[xg, xv, wg, wv], out_specs=[blk] * 3,
                          out_shape=[o] * 3)(up, up, conv_w, conv_w)


def _ffn_act_bwd(up, conv_w, conv_out, dact, bsz, plug=None):
    t = up.shape[0]
    s = t // bsz
    nj = D_FF // _FW
    xg = _vspec((s, _FW), lambda j, b: (b, j))
    xv = _vspec((s, _FW), lambda j, b: (b, nj + j))
    wg = _vspec((FFN_CONV, _FW), lambda j, b: (0, j))
    wv = _vspec((FFN_CONV, _FW), lambda j, b: (0, nj + j))
    da = _vspec((s, _FW), lambda j, b: (b, j))
    dwo = _vspec((FFN_CONV, _FW), lambda j, b: (0, j))

    def body(xg_ref, xv_ref, wg_ref, wv_ref, cg_ref, cv_ref, da_ref, dxg_ref, dxv_ref, dwg_ref, dwv_ref):
        b = pl.program_id(1)
        xg_, xv_, wg_, wv_ = xg_ref[...].astype(F32), xv_ref[...].astype(F32), wg_ref[...], wv_ref[...]
        _, vjp = jax.vjp(_geglu_f, cg_ref[...].astype(F32), cv_ref[...].astype(F32))
        dgate, dval = vjp(da_ref[...].astype(F32))
        dxg, dwg = _conv_bwd(xg_, wg_, dgate, FFN_CONV)
        dxv, dwv = _conv_bwd(xv_, wv_, dval, FFN_CONV)
        dxg_ref[...] = dxg.astype(BF16)
        dxv_ref[...] = dxv.astype(BF16)

        @pl.when(b == 0)
        def _():
            dwg_ref[...] = jnp.zeros_like(dwg_ref)
            dwv_ref[...] = jnp.zeros_like(dwv_ref)

        dwg_ref[...] += dwg
        dwv_ref[...] += dwv

    outs, extra = _plugged_call(
        body, plug, _grid_ends((nj, bsz)), (up, up, conv_w, conv_w, *conv_out, dact), name="ffn_act_bwd", grid=(nj, bsz),
        in_specs=[xg, xv, wg, wv, da, da, da], out_specs=[da, da, dwo, dwo],
        out_shape=[jax.ShapeDtypeStruct((t, D_FF), BF16), jax.ShapeDtypeStruct((t, D_FF), BF16),
                   jax.ShapeDtypeStruct((FFN_CONV, D_FF), F32), jax.ShapeDtypeStruct((FFN_CONV, D_FF), F32)])
    return (*outs, extra)


def _bucket_table():
    qi = np.arange(WINDOW)[:, None]
    kj = np.arange(2 * WINDOW)[None, :]
    dist = WINDOW + qi - kj
    dc = np.maximum(dist, 0)
    max_exact = REL_BUCKETS // 2
    scaled = np.log(np.maximum(dc, 1).astype(np.float32) / np.float32(max_exact)) / np.float32(math.log(REL_MAX_DIST / max_exact))
    large = max_exact + (scaled.astype(np.float32) * np.float32(REL_BUCKETS - max_exact)).astype(np.int32)
    large = np.minimum(large, REL_BUCKETS - 1)
    bucket = np.where(dc < max_exact, dc, large).astype(np.int32)
    in_band = ((dist >= 0) & (dist < WINDOW)).astype(np.int32)
    return bucket, in_band


def _bias_build(rel_bias):
    bucket, _ = _bucket_table()

    def body(rb_ref, idx_ref, o_ref):
        h = pl.program_id(0)
        idx = idx_ref[...]
        acc = jnp.zeros(idx.shape, F32)
        for r in range(REL_BUCKETS):
            acc = jnp.where(idx == r, rb_ref[r, h], acc)
        o_ref[0] = acc

    return pl.pallas_call(
        body, name="bias_build", grid=(AQ_HEADS,),
        in_specs=[pl.BlockSpec(memory_space=pltpu.SMEM), _vspec((WINDOW, 2 * WINDOW), lambda h: (0, 0))],
        out_specs=_vspec((1, WINDOW, 2 * WINDOW), lambda h: (h, 0, 0)),
        out_shape=jax.ShapeDtypeStruct((AQ_HEADS, WINDOW, 2 * WINDOW), F32),
    )(rel_bias, jnp.asarray(bucket))


def _bias_grad(dbias):
    bucket, _ = _bucket_table()

    def body(db_ref, idx_ref, o_ref):
        idx = idx_ref[...]
        db = db_ref[0]
        lane = lax.broadcasted_iota(jnp.int32, (1, LANES), 1)
        acc = jnp.zeros((1, LANES), F32)
        for r in range(REL_BUCKETS):
            s = jnp.sum(jnp.sum(jnp.where(idx == r, db, 0.0), axis=1, keepdims=True), axis=0, keepdims=True)
            acc = jnp.where(lane == r, s, acc)
        o_ref[0] = acc

    return pl.pallas_call(
        body, name="bias_grad", grid=(AQ_HEADS,),
        in_specs=[_vspec((1, WINDOW, 2 * WINDOW), lambda h: (h, 0, 0)), _vspec((WINDOW, 2 * WINDOW), lambda h: (0, 0))],
        out_specs=_vspec((1, 1, LANES), lambda h: (h, 0, 0)),
        out_shape=jax.ShapeDtypeStruct((AQ_HEADS, 1, LANES), F32),
    )(dbias, jnp.asarray(bucket))


def _attn_mask(n):
    qi = lax.broadcasted_iota(jnp.int32, (WINDOW, 2 * WINDOW), 0)
    kj = lax.broadcasted_iota(jnp.int32, (WINDOW, 2 * WINDOW), 1)
    dist = WINDOW + qi - kj
    band = (dist >= 0) & (dist < WINDOW)
    return band & ((kj >= WINDOW) | (n > 0))


def _attn_probs(qk, bias, sink, mask):
    s = jnp.where(mask, qk * (A_HD ** -0.5) + bias, NEG_INF)
    m = jnp.maximum(jnp.max(s, axis=-1, keepdims=True), sink)
    p = jnp.exp(s - m)
    es = jnp.exp(sink - m)
    inv = 1.0 / (jnp.sum(p, axis=-1, keepdims=True) + es)
    return p * inv, es * inv


def _attn_fwd(proj, bias, sinks, bsz):
    t = proj.shape[0]
    s = t // bsz
    nb = s // WINDOW
    grp = AQ_HEADS // AKV_HEADS

    def body(q_ref, k_ref, v_ref, bias_ref, sink_ref, y_ref, kp_ref, vp_ref):
        kp_ref[0:WINDOW, :] = jnp.zeros((WINDOW, LANES), BF16)
        vp_ref[0:WINDOW, :] = jnp.zeros((WINDOW, LANES), BF16)
        kp_ref[WINDOW:, :] = k_ref[...].astype(BF16)
        vp_ref[WINDOW:, :] = v_ref[...].astype(BF16)

        def blk(n, carry):
            r0 = pl.multiple_of(n * WINDOW, WINDOW)
            mask = _attn_mask(n)
            kband = kp_ref[pl.ds(r0, 2 * WINDOW), :]
            vband = vp_ref[pl.ds(r0, 2 * WINDOW), :]
            qb = q_ref[pl.ds(r0, WINDOW), :].astype(BF16)
            heads = range(AQ_HEADS)
            hsl = lambda h: slice(h * A_HD, (h + 1) * A_HD)
            kbs = [kband[:, hsl(kv)] for kv in range(AKV_HEADS)]
            vbs = [vband[:, hsl(kv)] for kv in range(AKV_HEADS)]
            qks = [lax.dot_general(qb[:, hsl(h)], kbs[h // grp], _NT, preferred_element_type=F32) for h in heads]
            probs = [_attn_probs(qks[h], bias_ref[h], sink_ref[0, h], mask)[0] for h in heads]
            outs = [jnp.dot(probs[h].astype(BF16), vbs[h // grp], preferred_element_type=F32) for h in heads]
            y_ref[pl.ds(r0, WINDOW), :] = jnp.concatenate(outs, axis=1).astype(BF16)
            return carry

        lax.fori_loop(0, nb, blk, 0)

    return pl.pallas_call(
        body, name="attn_fwd", grid=(bsz,),
        in_specs=[_vspec((s, AQ), lambda b: (b, C_Q // AQ)), _vspec((s, AKV), lambda b: (b, C_K // AKV)),
                  _vspec((s, AKV), lambda b: (b, C_V // AKV)),
                  _vspec((AQ_HEADS, WINDOW, 2 * WINDOW), lambda b: (0, 0, 0)), pl.BlockSpec(memory_space=pltpu.SMEM)],
        out_specs=_vspec((s, AQ), lambda b: (b, 0)), out_shape=jax.ShapeDtypeStruct((t, AQ), BF16),
        scratch_shapes=[pltpu.VMEM((s + WINDOW, LANES), BF16), pltpu.VMEM((s + WINDOW, LANES), BF16)],
    )(proj, proj, proj, bias, sinks)


def _attn_bwd(dproj, proj, bias, sinks, dy, bsz, plug=None):
    t = proj.shape[0]
    s = t // bsz
    nb = s // WINDOW
    grp = AQ_HEADS // AKV_HEADS
    scale = A_HD ** -0.5

    def body(q_ref, k_ref, v_ref, bias_ref, sink_ref, dy_ref, dq_ref, dk_ref, dv_ref, dbias_ref, dsink_ref,
             kp_ref, vp_ref, dkp_ref, dvp_ref):
        b = pl.program_id(0)
        kp_ref[0:WINDOW, :] = jnp.zeros((WINDOW, LANES), BF16)
        vp_ref[0:WINDOW, :] = jnp.zeros((WINDOW, LANES), BF16)
        kp_ref[WINDOW:, :] = k_ref[...].astype(BF16)
        vp_ref[WINDOW:, :] = v_ref[...].astype(BF16)
        dkp_ref[...] = jnp.zeros_like(dkp_ref)
        dvp_ref[...] = jnp.zeros_like(dvp_ref)

        @pl.when(b == 0)
        def _():
            dbias_ref[...] = jnp.zeros_like(dbias_ref)
            dsink_ref[...] = jnp.zeros_like(dsink_ref)

        def blk(n, carry):
            r0 = pl.multiple_of(n * WINDOW, WINDOW)
            mask = _attn_mask(n)
            kband = kp_ref[pl.ds(r0, 2 * WINDOW), :]
            vband = vp_ref[pl.ds(r0, 2 * WINDOW), :]
            qb = q_ref[pl.ds(r0, WINDOW), :].astype(BF16)
            dyb = dy_ref[pl.ds(r0, WINDOW), :].astype(BF16)
            heads = range(AQ_HEADS)
            hsl = lambda h: slice(h * A_HD, (h + 1) * A_HD)
            kbs = [kband[:, hsl(kv)] for kv in range(AKV_HEADS)]
            vbs = [vband[:, hsl(kv)] for kv in range(AKV_HEADS)]
            qhs = [qb[:, hsl(h)] for h in heads]
            dyhs = [dyb[:, hsl(h)] for h in heads]
            qks = [lax.dot_general(qhs[h], kbs[h // grp], _NT, preferred_element_type=F32) for h in heads]
            dprobs = [lax.dot_general(dyhs[h], vbs[h // grp], _NT, preferred_element_type=F32) for h in heads]
            pbs, dsbs = [], []
            for h in heads:
                probs, psink = _attn_probs(qks[h], bias_ref[h], sink_ref[0, h], mask)
                rowdot = jnp.sum(probs * dprobs[h], axis=-1, keepdims=True)
                ds = probs * (dprobs[h] - rowdot)
                dbias_ref[h] += ds
                dsink_ref[h] += jnp.sum(-psink * rowdot, axis=0, keepdims=True) + jnp.zeros((1, LANES), F32)
                pbs.append(probs.astype(BF16))
                dsbs.append(ds.astype(BF16))
            dvhs = [lax.dot_general(pbs[h], dyhs[h], _TN, preferred_element_type=F32) for h in heads]
            dqs = [jnp.dot(dsbs[h], kbs[h // grp], preferred_element_type=F32) * scale for h in heads]
            dkhs = [lax.dot_general(dsbs[h], qhs[h], _TN, preferred_element_type=F32) * scale for h in heads]
            dks = [sum(dkhs[kv * grp + 1:(kv + 1) * grp], dkhs[kv * grp]) for kv in range(AKV_HEADS)]
            dvs = [sum(dvhs[kv * grp + 1:(kv + 1) * grp], dvhs[kv * grp]) for kv in range(AKV_HEADS)]
            dq_ref[pl.ds(r0, WINDOW), :] = jnp.concatenate(dqs, axis=1).astype(BF16)
            dkp_ref[pl.ds(r0, 2 * WINDOW), :] += jnp.concatenate(dks, axis=1)
            dvp_ref[pl.ds(r0, 2 * WINDOW), :] += jnp.concatenate(dvs, axis=1)
            return carry

        lax.fori_loop(0, nb, blk, 0)
        dk_ref[...] = dkp_ref[WINDOW:, :].astype(BF16)
        dv_ref[...] = dvp_ref[WINDOW:, :].astype(BF16)

    kvs = jax.ShapeDtypeStruct((t, AKV), BF16)
    outs, extra = _plugged_call(
        lambda buf_ref, *refs: body(*refs), plug, _grid_ends((bsz,)), (dproj, proj, proj, proj, bias, sinks, dy),
        name="attn_bwd", grid=(bsz,),
        in_specs=[_ANY, _vspec((s, AQ), lambda b: (b, C_Q // AQ)), _vspec((s, AKV), lambda b: (b, C_K // AKV)),
                  _vspec((s, AKV), lambda b: (b, C_V // AKV)),
                  _vspec((AQ_HEADS, WINDOW, 2 * WINDOW), lambda b: (0, 0, 0)), pl.BlockSpec(memory_space=pltpu.SMEM),
                  _vspec((s, AQ), lambda b: (b, 0))],
        out_specs=[_vspec((s, AQ), lambda b: (b, C_Q // AQ)), _vspec((s, AKV), lambda b: (b, 0)), _vspec((s, AKV), lambda b: (b, 0)),
                   _vspec((AQ_HEADS, WINDOW, 2 * WINDOW), lambda b: (0, 0, 0)), _vspec((AQ_HEADS, 1, LANES), lambda b: (0, 0, 0))],
        out_shape=[jax.ShapeDtypeStruct(dproj.shape, BF16), kvs, kvs,
                   jax.ShapeDtypeStruct((AQ_HEADS, WINDOW, 2 * WINDOW), F32), jax.ShapeDtypeStruct((AQ_HEADS, 1, LANES), F32)],
        scratch_shapes=[pltpu.VMEM((s + WINDOW, LANES), BF16), pltpu.VMEM((s + WINDOW, LANES), BF16),
                        pltpu.VMEM((s + WINDOW, LANES), F32), pltpu.VMEM((s + WINDOW, LANES), F32)],
        aliases={0: 0})
    return (*outs, extra)


def _dn_act_f(c, is_qk):
    a = jax.nn.silu(c)
    outs = []
    for h in range(DN_HEADS):
        ah = a[:, h * DN_HD:(h + 1) * DN_HD]
        nh = ah * lax.rsqrt(jnp.sum(ah * ah, axis=-1, keepdims=True) + L2_EPS)
        outs.append(jnp.where(is_qk, nh, ah))
    return jnp.concatenate(outs, axis=1)


def _dn_prep_fwd(proj, conv_w, bsz):
    t = proj.shape[0]
    s = t // bsz
    blk = _vspec((s, DN), lambda b, j: (b, j))
    wsp = _vspec((DN_CONV, DN), lambda b, j: (0, j))

    def body(x_ref, w_ref, o_ref, c_ref):
        j = pl.program_id(1)
        c = _conv_fwd(x_ref[...], w_ref[...], DN_CONV)
        o_ref[...] = _dn_act_f(c, j < 2)
        c_ref[...] = c.astype(BF16)

    return pl.pallas_call(body, name="dn_prep_fwd", grid=(bsz, 3), in_specs=[blk, wsp], out_specs=[blk, blk],
                          out_shape=[jax.ShapeDtypeStruct((t, 3 * DN), F32), jax.ShapeDtypeStruct((t, 3 * DN), BF16)])(proj, conv_w)


def _dn_prep_bwd(dproj, proj, conv_w, conv_out, dqkvn, bsz):
    t = proj.shape[0]
    s = t // bsz
    blk = _vspec((s, DN), lambda j, b: (b, j))
    wsp = _vspec((DN_CONV, DN), lambda j, b: (0, j))

    def body(buf_ref, x_ref, w_ref, c_ref, d_ref, dx_ref, dw_ref):
        del buf_ref
        j, b = pl.program_id(0), pl.program_id(1)
        x, w = x_ref[...], w_ref[...]
        _, vjp = jax.vjp(lambda cc: _dn_act_f(cc, j < 2), c_ref[...].astype(F32))
        (dc,) = vjp(d_ref[0])
        dx, dw = _conv_bwd(x, w, dc, DN_CONV)
        dx_ref[...] = dx.astype(BF16)

        @pl.when(b == 0)
        def _():
            dw_ref[...] = jnp.zeros_like(dw_ref)

        dw_ref[...] += dw

    return pl.pallas_call(
        body, name="dn_prep_bwd", grid=(3, bsz),
        in_specs=[_ANY, blk, wsp, blk, _vspec((1, s, DN), lambda j, b: (j, b, 0))], out_specs=[blk, wsp],
        out_shape=[jax.ShapeDtypeStruct(dproj.shape, BF16), jax.ShapeDtypeStruct((DN_CONV, 3 * DN), F32)],
        input_output_aliases={0: 0},
    )(dproj, proj, conv_w, conv_out, dqkvn)


def _bg_f(x, alog, dt):
    lane = lax.broadcasted_iota(jnp.int32, x.shape, 1)
    beta = jax.nn.sigmoid(x)
    g = -jnp.exp(alog) * jax.nn.softplus(x + dt)
    return jnp.where(lane < DN_HEADS, beta, jnp.where(lane < 2 * DN_HEADS, g, 0.0))


def _bg_fwd(proj, alog, dt, bsz):
    t = proj.shape[0]
    s = t // bsz
    vec = _vspec((1, LANES), lambda b: (0, 0))

    def body(x_ref, a_ref, d_ref, o_ref):
        o_ref[...] = _bg_f(x_ref[...], a_ref[...], d_ref[...])

    return pl.pallas_call(body, name="bg_fwd", grid=(bsz,), in_specs=[_vspec((s, LANES), lambda b: (b, C_BD // LANES)), vec, vec],
                          out_specs=_vspec((s, LANES), lambda b: (b, 0)), out_shape=jax.ShapeDtypeStruct((t, LANES), F32))(proj, alog, dt)


def _bg_bwd(dproj, proj, alog, dt, dbg4, dk, dv, bsz):
    t = proj.shape[0]
    s = t // bsz
    vec = _vspec((1, LANES), lambda b: (0, 0))
    kv = _vspec((s, AKV), lambda b: (b, 0))
    tail = 3 * LANES

    def body(buf_ref, x_ref, a_ref, d_ref, g4_ref, dk_ref, dv_ref, dx_ref, da_ref, dd_ref):
        del buf_ref
        b = pl.program_id(0)
        lane = lax.broadcasted_iota(jnp.int32, (s, LANES), 1)
        dbg = jnp.zeros((s, LANES), F32)
        for h in range(DN_HEADS):
            gh = g4_ref[:, h * DN_HD:(h + 1) * DN_HD]
            dbg = jnp.where(lane == h, gh[:, 0:1], dbg)
            dbg = jnp.where(lane == DN_HEADS + h, gh[:, 1:2], dbg)
        _, vjp = jax.vjp(_bg_f, x_ref[...], a_ref[...], d_ref[...])
        dx, da, dd = vjp(dbg)
        dx_ref[...] = jnp.concatenate([dk_ref[...], dv_ref[...], dx.astype(BF16)], axis=1)

        @pl.when(b == 0)
        def _():
            da_ref[...] = jnp.zeros_like(da_ref)
            dd_ref[...] = jnp.zeros_like(dd_ref)

        da_ref[...] += da
        dd_ref[...] += dd

    return pl.pallas_call(
        body, name="bg_bwd", grid=(bsz,),
        in_specs=[_ANY, _vspec((s, LANES), lambda b: (b, C_BD // LANES)), vec, vec, _vspec((s, DN), lambda b: (b, 0)), kv, kv],
        out_specs=[_vspec((s, tail), lambda b: (b, C_K // tail)), vec, vec],
        out_shape=[jax.ShapeDtypeStruct(dproj.shape, BF16), jax.ShapeDtypeStruct((1, LANES), F32), jax.ShapeDtypeStruct((1, LANES), F32)],
        input_output_aliases={0: 0},
    )(dproj, proj, alog, dt, dbg4, dk, dv)


def _dn_out_f(o, z, w):
    outs = []
    for h in range(DN_HEADS):
        sl = slice(h * DN_HD, (h + 1) * DN_HD)
        outs.append(_rms(o[:, sl], w) * jax.nn.silu(z[:, sl]))
    return jnp.concatenate(outs, axis=1)


def _dn_out_fwd(o, proj, w, ts=512):
    t = o.shape[0]
    blk = _vspec((ts, DN), lambda i: (i, 0))
    zsp = _vspec((ts, DN), lambda i: (i, C_DZ // DN))
    vec = _vspec((1, DN_HD), lambda i: (0, 0))

    def body(o_ref, z_ref, w_ref, y_ref):
        y_ref[...] = _dn_out_f(o_ref[...], z_ref[...], w_ref[...]).astype(BF16)

    return pl.pallas_call(body, name="dn_out_fwd", grid=(t // ts,), in_specs=[blk, zsp, vec], out_specs=blk,
                          out_shape=jax.ShapeDtypeStruct((t, DN), BF16))(o, proj, w)


def _dn_out_bwd(dproj, o, proj, w, dy, ts=512):
    t = o.shape[0]
    blk = _vspec((ts, DN), lambda i: (i, 0))
    zsp = _vspec((ts, DN), lambda i: (i, C_DZ // DN))
    vec = _vspec((1, DN_HD), lambda i: (0, 0))

    def body(buf_ref, o_ref, z_ref, w_ref, dy_ref, do_ref, dz_ref, dw_ref):
        del buf_ref
        i = pl.program_id(0)
        _, vjp = jax.vjp(_dn_out_f, o_ref[...], z_ref[...], w_ref[...])
        do, dz, dw = vjp(dy_ref[...].astype(F32))
        do_ref[...] = do
        dz_ref[...] = dz.astype(BF16)

        @pl.when(i == 0)
        def _():
            dw_ref[...] = jnp.zeros_like(dw_ref)

        dw_ref[...] += dw

    return pl.pallas_call(
        body, name="dn_out_bwd", grid=(t // ts,), in_specs=[_ANY, blk, zsp, vec, blk], out_specs=[blk, zsp, vec],
        out_shape=[jax.ShapeDtypeStruct((t, DN), F32), jax.ShapeDtypeStruct(dproj.shape, BF16), jax.ShapeDtypeStruct((1, DN_HD), F32)],
        input_output_aliases={0: 1},
    )(dproj, o, proj, w, dy)


_C = DN_CHUNK


def _dot(a, b, dims):
    return lax.dot_general(a.astype(BF16), b.astype(BF16), dims, preferred_element_type=F32)


def _split(a):
    hi = a.astype(BF16)
    return hi, (a - hi.astype(F32)).astype(BF16)


def _dot3(a, b, dims):
    (ah, al), (bh, bl) = (a if isinstance(a, tuple) else _split(a)), (b if isinstance(b, tuple) else _split(b))
    mm = lambda x, y: lax.dot_general(x, y, dims, preferred_element_type=F32)
    return mm(ah, bh) + (mm(ah, bl) + mm(al, bh))


_NN = (((1,), (0,)), ((), ()))
_NT = (((1,), (1,)), ((), ()))
_TN = (((0,), (0,)), ((), ()))


_SUB = 8


def _tri_inverses(ls, lts):
    ri8 = lax.broadcasted_iota(jnp.int32, (_SUB, _C), 0)
    ci8 = lax.broadcasted_iota(jnp.int32, (_SUB, _C), 1)
    nblk = _C // _SUB
    ts = []
    for lt in lts:
        blocks = [jnp.where(ci8 == ri8 + _SUB * b, 1.0, 0.0).astype(F32) for b in range(nblk)]
        for r in range(1, _SUB):
            for b in range(nblk):
                coef = lt[_SUB * b:_SUB * (b + 1), _SUB * b + r:_SUB * b + r + 1]
                row = jnp.sum(coef * blocks[b], axis=0, keepdims=True)
                blocks[b] = jnp.where(ri8 == r, blocks[b] - row, blocks[b])
        ts.append(jnp.concatenate(blocks, axis=0))
    ri = lax.broadcasted_iota(jnp.int32, (_C, _C), 0)
    ci = lax.broadcasted_iota(jnp.int32, (_C, _C), 1)
    s = _SUB
    while s < _C:
        shift = s.bit_length()
        quad = ((ri >> shift) == (ci >> shift)) & ((ri & s) != 0) & ((ci & s) == 0)
        offs = [jnp.where(quad, l, 0.0) for l in ls]
        tsp = [_split(t) for t in ts]
        left = [_dot3(tp, off, _NN) for tp, off in zip(tsp, offs)]
        ts = [t - _dot3(lo, tp, _NN) for t, lo, tp in zip(ts, left, tsp)]
        s *= 2
    return ts


_SEG = 512
_HEADS = tuple(range(DN_HEADS))


def _hsl(hh):
    return slice(hh * DN_HD, (hh + 1) * DN_HD)


def _chunk_specs(bsz, nseg, reverse):
    seg = (lambda i: nseg - 1 - i) if reverse else (lambda i: i)
    ncs = _SEG // _C
    col = lambda off: _vspec((bsz, _SEG, DN), lambda i: (0, seg(i), off))
    return (col, _vspec((bsz, _SEG, LANES), lambda i: (0, seg(i), 0)),
            _vspec((bsz, DN_HEADS, ncs, _C), lambda i: (0, 0, seg(i), 0)),
            _vspec((bsz, DN_HEADS, ncs, DN_HD, DN_HD), lambda i: (0, 0, seg(i), 0, 0)),
            _vspec((bsz, DN_HEADS, ncs, _C, _C), lambda i: (0, 0, seg(i), 0, 0)))


def _chunk_pre(q_ref, k_ref, v_ref, bg_ref, gr_ref, c, bb, hh):
    r0 = pl.multiple_of(c * _C, _C)
    ri = lax.broadcasted_iota(jnp.int32, (_C, _C), 0)
    ci = lax.broadcasted_iota(jnp.int32, (_C, _C), 1)
    q = q_ref[bb, pl.ds(r0, _C), _hsl(hh)] * (DN_HD ** -0.5)
    k = k_ref[bb, pl.ds(r0, _C), _hsl(hh)]
    v = v_ref[bb, pl.ds(r0, _C), _hsl(hh)]
    bgc = bg_ref[bb, pl.ds(r0, _C), :]
    beta = bgc[:, hh:hh + 1]
    g_col = bgc[:, DN_HEADS + hh:DN_HEADS + hh + 1]
    g_row = gr_ref[bb, hh, pl.ds(c, 1), :]
    gc_col = jnp.sum(jnp.where(ri >= ci, g_row, 0.0), axis=1, keepdims=True)
    gc_row = jnp.sum(jnp.where(ri <= ci, g_col, 0.0), axis=0, keepdims=True)
    gc_last = jnp.sum(g_col, axis=0, keepdims=True)
    diff = gc_col - gc_row
    decay = jnp.where(ri >= ci, jnp.exp(jnp.where(ri >= ci, diff, 0.0)), 0.0)
    diff_t = gc_row - gc_col
    decay_t = jnp.where(ri <= ci, jnp.exp(jnp.where(ri <= ci, diff_t, 0.0)), 0.0)
    eg = jnp.exp(gc_col)
    et = jnp.exp(gc_last - gc_col)
    gl = jnp.exp(gc_last)
    kb = k * beta
    vb = v * beta
    return dict(r0=r0, bb=bb, hh=hh, q=q, k=k, v=v, beta=beta, decay=decay, decay_t=decay_t, eg=eg, et=et, gl=gl, kb=kb, vb=vb,
                ri=ri, ci=ci)


def _chunk_solve(ms, tms=None):
    for m in ms:
        m["kk_t"] = _dot(m["k"], m["kb"], _NT)
        m["qk"] = _dot(m["q"], m["k"], _NT)
        m["kk"] = _dot(m["kb"], m["k"], _NT)
        if tms is not None:
            m["qk_t"] = _dot(m["k"], m["q"], _NT)
    if tms is None:
        tms = _tri_inverses([jnp.where(m["ri"] > m["ci"], m["kk"] * m["decay"], 0.0) for m in ms],
                            [jnp.where(m["ri"] < m["ci"], m["kk_t"] * m["decay_t"], 0.0) for m in ms])
    for m, tm in zip(ms, tms):
        m["tm_f32"] = tm
    for m in ms:
        rhs = jnp.concatenate([m["vb"], m["kb"] * m["eg"]], axis=1)
        m["tm"] = _split(m["tm_f32"])
        m["sol"] = _dot3(m["tm"], rhs, _NN)
        m["intra"] = jnp.where(m["ri"] >= m["ci"], m["qk"] * m["decay"], 0.0)


def _dn_chunk_fwd(qkvn, bg, g_rows, bsz, plug=None):
    t = qkvn.shape[0]
    s = t // bsz
    nc, nseg = s // _C, s // _SEG
    pairs = [(bb, hh) for bb in range(bsz) for hh in _HEADS]

    def body(q_ref, k_ref, v_ref, bg_ref, gr_ref, o_ref, st_ref, tm_ref, s_ref):
        @pl.when(pl.program_id(0) == 0)
        def _():
            s_ref[...] = jnp.zeros_like(s_ref)

        def chunk(c, carry):
            ms = [_chunk_pre(q_ref, k_ref, v_ref, bg_ref, gr_ref, c, bb, hh) for bb, hh in pairs]
            _chunk_solve(ms)
            sts = [s_ref[i] for i in range(len(pairs))]
            for m, st in zip(ms, sts):
                st_ref[m["bb"], m["hh"], c] = st
                tm_ref[m["bb"], m["hh"], c] = m["tm_f32"]
            ws = [_dot(m["sol"][:, DN_HD:], st, _NN) for m, st in zip(ms, sts)]
            qs = [_dot(m["q"] * m["eg"], st, _NN) for m, st in zip(ms, sts)]
            v_new = [m["sol"][:, :DN_HD] - a for m, a in zip(ms, ws)]
            iv = [_dot(m["intra"], vn, _NN) for m, vn in zip(ms, v_new)]
            upd = [_dot(m["k"] * m["et"], vn, _TN) for m, vn in zip(ms, v_new)]
            for i, (bb, hh) in enumerate(pairs):
                s_ref[i] = sts[i] * ms[i]["gl"] + upd[i]
                o_ref[bb, pl.ds(ms[i]["r0"], _C), _hsl(hh)] = qs[i] + iv[i]
            return carry

        lax.fori_loop(0, _SEG // _C, chunk, 0)

    col, bgs, grs, sts_spec, tms_spec = _chunk_specs(bsz, nseg, False)
    q3, bg3 = qkvn.reshape(bsz, s, 3 * DN), bg.reshape(bsz, s, LANES)
    (o, states, tms), extra = _plugged_call(
        body, plug, _grid_ends((nseg,)), (q3, q3, q3, bg3, g_rows), name="dn_chunk_fwd", grid=(nseg,),
        in_specs=[col(0), col(1), col(2), bgs, grs], out_specs=[col(0), sts_spec, tms_spec],
        out_shape=[jax.ShapeDtypeStruct((bsz, s, DN), F32), jax.ShapeDtypeStruct((bsz, DN_HEADS, nc, DN_HD, DN_HD), F32),
                   jax.ShapeDtypeStruct((bsz, DN_HEADS, nc, _C, _C), F32)],
        scratch_shapes=[pltpu.VMEM((bsz * DN_HEADS, DN_HD, DN_HD), F32)])
    return o.reshape(t, DN), (states, tms), extra


def _dn_chunk_bwd(qkvn, bg, g_rows, states, do, bsz, plug=None):
    t = qkvn.shape[0]
    s = t // bsz
    nc, nseg = s // _C, s // _SEG
    pairs = [(bb, hh) for bb in range(bsz) for hh in _HEADS]

    def body(q_ref, k_ref, v_ref, bg_ref, gr_ref, st_ref, tm_ref, do_ref, dqkv_ref, dbg_ref, ds_ref):
        @pl.when(pl.program_id(0) == 0)
        def _():
            ds_ref[...] = jnp.zeros_like(ds_ref)

        def chunk(cc, carry):
            c = _SEG // _C - 1 - cc
            ms = [_chunk_pre(q_ref, k_ref, v_ref, bg_ref, gr_ref, c, bb, hh) for bb, hh in pairs]
            _chunk_solve(ms, [tm_ref[bb, hh, c] for bb, hh in pairs])
            ri, ci = ms[0]["ri"], ms[0]["ci"]
            for i, m in enumerate(ms):
                m["st"] = st_ref[m["bb"], m["hh"], c]
                m["ds_out"] = ds_ref[i]
                m["do"] = do_ref[m["bb"], pl.ds(m["r0"], _C), _hsl(m["hh"])]
                m["w"] = m["sol"][:, DN_HD:]
            for m in ms:
                m["v_new"] = m["sol"][:, :DN_HD] - _dot(m["w"], m["st"], _NN)
            for m in ms:
                m["q_dec"], m["k_tail"] = m["q"] * m["eg"], m["k"] * m["et"]
                m["dk_tail"] = _dot(m["v_new"], m["ds_out"], _NT)
                m["dv_new"] = _dot(m["k_tail"], m["ds_out"], _NN) + _dot(m["intra"], m["do"], _TN)
                m["dq_dec"] = _dot(m["do"], m["st"], _NT)
                m["ds_in"] = m["ds_out"] * m["gl"] + _dot(m["q_dec"], m["do"], _TN)
                m["dintra"] = jnp.where(ri >= ci, _dot(m["do"], m["v_new"], _NT), 0.0)
                m["dintra_t"] = jnp.where(ri <= ci, _dot(m["v_new"], m["do"], _NT), 0.0)
            for m in ms:
                m["dw"] = -_dot(m["dv_new"], m["st"], _NT)
                m["ds_in"] = m["ds_in"] - _dot(m["w"], m["dv_new"], _TN)
            for m in ms:
                dsol = jnp.concatenate([m["dv_new"], m["dw"]], axis=1)
                m["drhs"] = _dot3(m["tm"], dsol, _TN)
            for m in ms:
                m["dl"] = jnp.where(ri > ci, -_dot(m["drhs"], m["sol"], _NT), 0.0)
                m["dl_t"] = jnp.where(ri < ci, -_dot(m["sol"], m["drhs"], _NT), 0.0)
            for m in ms:
                m["dkb2"] = _dot(m["dl"] * m["decay"], m["k"], _NN)
                m["dk"] = _dot(m["dl_t"] * m["decay_t"], m["kb"], _NN) + _dot(m["dintra_t"] * m["decay_t"], m["q"], _NN)
                m["dq"] = _dot(m["dintra"] * m["decay"], m["k"], _NN)
            for m in ms:
                _chunk_bwd_finish(m)
            for m in ms:
                ones_ge = jnp.where(ri <= ci, 1.0, 0.0).astype(BF16)
                gh, gl_ = _split(m["dgc"] + jnp.zeros((_C, LANES), F32))
                m["dg_b"] = jnp.dot(ones_ge, gh, preferred_element_type=F32) + jnp.dot(ones_ge, gl_, preferred_element_type=F32)
            lane = lax.broadcasted_iota(jnp.int32, (_C, LANES), 1)
            for i, m in enumerate(ms):
                bb, hh, rows = m["bb"], m["hh"], pl.ds(m["r0"], _C)
                dqkv_ref[0, bb, rows, _hsl(hh)] = m["dq"] * (DN_HD ** -0.5)
                dqkv_ref[1, bb, rows, _hsl(hh)] = m["dk"]
                dqkv_ref[2, bb, rows, _hsl(hh)] = m["dv"]
                dbg_ref[bb, rows, _hsl(hh)] = jnp.where(lane == 0, m["dbeta"], jnp.where(lane == 1, m["dg_b"], 0.0))
                ds_ref[i] = m["ds_in"]
            return carry

        lax.fori_loop(0, _SEG // _C, chunk, 0)

    col, bgs, grs, sts_spec, tms_spec = _chunk_specs(bsz, nseg, True)
    q3, bg3, do3 = qkvn.reshape(bsz, s, 3 * DN), bg.reshape(bsz, s, LANES), do.reshape(bsz, s, DN)
    (dqkv, dbg), extra = _plugged_call(
        body, plug, _grid_ends((nseg,)), (q3, q3, q3, bg3, g_rows, *states, do3), name="dn_chunk_bwd", grid=(nseg,),
        in_specs=[col(0), col(1), col(2), bgs, grs, sts_spec, tms_spec, col(0)],
        out_specs=[_vspec((3, bsz, _SEG, DN), lambda i: (0, 0, nseg - 1 - i, 0)), col(0)],
        out_shape=[jax.ShapeDtypeStruct((3, bsz, s, DN), F32), jax.ShapeDtypeStruct((bsz, s, DN), F32)],
        scratch_shapes=[pltpu.VMEM((bsz * DN_HEADS, DN_HD, DN_HD), F32)])
    return dqkv.reshape(3, t, DN), dbg.reshape(t, DN), extra


def _chunk_bwd_finish(m):
    q, k, v, beta, decay, decay_t = m["q"], m["k"], m["v"], m["beta"], m["decay"], m["decay_t"]
    eg, et, gl, kb, dl, dl_t, dintra, dintra_t = m["eg"], m["et"], m["gl"], m["kb"], m["dl"], m["dl_t"], m["dintra"], m["dintra_t"]
    dq_dec, dk_tail, dq, dk = m["dq_dec"], m["dk_tail"], m["dq"], m["dk"]
    dgl = jnp.sum(jnp.sum(m["ds_out"] * m["st"], axis=1, keepdims=True), axis=0, keepdims=True)
    dvb, dkbeg = m["drhs"][:, :DN_HD], m["drhs"][:, DN_HD:]
    dkb = dkbeg * eg + m["dkb2"]
    deg = jnp.sum(dkbeg * kb, axis=1, keepdims=True)
    em = (dl * m["kk"] + dintra * m["qk"]) * decay
    em_t = (dl_t * m["kk_t"] + dintra_t * m["qk_t"]) * decay_t
    dgc = jnp.sum(em, axis=1, keepdims=True) - jnp.sum(em_t, axis=1, keepdims=True)
    dq = dq + dq_dec * eg
    deg = deg + jnp.sum(dq_dec * q, axis=1, keepdims=True)
    dk = dk + dk_tail * et
    det = jnp.sum(dk_tail * k, axis=1, keepdims=True)
    dgc = dgc + deg * eg - det * et
    dgc_last = jnp.sum(det * et, axis=0, keepdims=True) + dgl * gl
    rcol = lax.broadcasted_iota(jnp.int32, (_C, 1), 0)
    m["dgc"] = dgc + jnp.where(rcol == _C - 1, dgc_last, 0.0)
    m["dq"] = dq
    m["dk"] = dk + dkb * beta
    m["dbeta"] = jnp.sum(dkb * k, axis=1, keepdims=True) + jnp.sum(dvb * v, axis=1, keepdims=True)
    m["dv"] = dvb * beta


def _mod_fwd(c_all, ada_w_loc, ada_b_loc):
    n, cols = c_all.shape[0], ada_w_loc.shape[1]

    def body(c_ref, w_ref, b_ref, o_ref):
        o_ref[...] = _dot(jax.nn.silu(c_ref[...]), w_ref[...], _NN) + b_ref[...]

    return pl.pallas_call(body, name="mod_fwd", out_shape=jax.ShapeDtypeStruct((n, cols), F32))(c_all, ada_w_loc, ada_b_loc)


def _ada_grad(c_all, dmod_loc, dmod_all):
    d, cols = c_all.shape[1], dmod_loc.shape[1]

    def body(c_ref, dl_ref, da_ref, gw_ref, gb_ref):
        gw_ref[...] = _dot(jax.nn.silu(c_ref[...]), dl_ref[...], _TN)
        gb_ref[...] = jnp.sum(da_ref[...], axis=0, keepdims=True)

    return pl.pallas_call(body, name="ada_grad", out_shape=[jax.ShapeDtypeStruct((d, cols), F32),
                                                           jax.ShapeDtypeStruct((1, dmod_all.shape[1]), F32)])(c_all, dmod_loc, dmod_all)


ELEMENTWISE_BLOCK_BYTES = 3 * 2 ** 19


def _row_tile(r, c=1024):
    fits = [tr for tr in range(16, r + 1, 16) if tr * c * 4 <= ELEMENTWISE_BLOCK_BYTES]
    if not fits:
        return r
    whole = [tr for tr in fits if r % tr == 0]
    return whole[-1] if whole else fits[-1]


def _adamw(w, m, v, grads, name, plug=None):
    r, rest = w.shape[0], w.shape[1:]
    c = math.prod(rest)
    tr = _row_tile(r, c)
    blk = _vspec((tr,) + rest, lambda i: (i,) + (0,) * len(rest))
    n = len(grads)

    def body(*refs):
        w_ref, m_ref, v_ref = refs[:3]
        g_ref, d_ref, mo_ref, vo_ref = refs[3 + n:]
        g = refs[3][...]
        for p in refs[4:3 + n]:
            g = g + p[...]
        g_ref[...] = g
        d_ref[...], mo_ref[...], vo_ref[...] = _adamw_math(w_ref[...], m_ref[...], v_ref[...], g)

    o = jax.ShapeDtypeStruct(w.shape, F32)
    grid = (pl.cdiv(r, tr),)
    res, extra = _plugged_call(body, plug, _grid_ends(grid), (w, m, v, *grads), name=name, grid=grid, in_specs=[blk] * (3 + n),
                               out_specs=[blk] * 4, out_shape=[o] * 4)
    return res if plug is None else (res, extra)


def _adamw_halves(w, m, v, own, other, ic, name):
    if w.ndim == 3:
        r, _, c = w.shape
        tr = _row_tile(r, c)
        blk = _vspec((tr, 1, c // 2), lambda i, j: (i, 0, j))
        half = _vspec((tr, 1, c // 2), lambda i, j: (i, 0, 0))
        grid = (pl.cdiv(r, tr), 2)
    else:
        r, c = w.shape
        tr = _row_tile(r // 2, c)
        per = r // 2 // tr
        blk = _vspec((tr, c), lambda i, j: (j * per + i, 0))
        half = _vspec((tr, c), lambda i, j: (i, 0))
        grid = (per, 2)

    def body(ic_ref, w_ref, m_ref, v_ref, own_ref, other_ref, g_ref, d_ref, mo_ref, vo_ref):
        g = jnp.where(pl.program_id(1) == ic_ref[0], own_ref[...], other_ref[...])
        g_ref[...] = g
        d_ref[...], mo_ref[...], vo_ref[...] = _adamw_math(w_ref[...], m_ref[...], v_ref[...], g)

    o = jax.ShapeDtypeStruct(w.shape, F32)
    return pl.pallas_call(body, name=name, grid=grid,
                          in_specs=[pl.BlockSpec(memory_space=pltpu.SMEM), blk, blk, blk, half, half], out_specs=[blk] * 4,
                          out_shape=[o] * 4)(ic, w, m, v, own, other)


def _adamw_math(w, m, v, g):
    m_new = ADAM_B1 * m + (1.0 - ADAM_B1) * g
    v_new = ADAM_B2 * v + (1.0 - ADAM_B2) * jnp.square(g)
    m_hat = m_new / (1.0 - ADAM_B1 ** ADAM_STEP)
    v_hat = v_new / (1.0 - ADAM_B2 ** ADAM_STEP)
    return -ADAM_LR * (m_hat / (jnp.sqrt(v_hat) + ADAM_EPS) + ADAM_WD * w), m_new, v_new


def _adamw_small(ws, ms, vs, gs, name):
    n = len(ws)

    def body(*refs):
        for i in range(n):
            w_ref, m_ref, v_ref, g_ref = (refs[j * n + i] for j in range(4))
            go_ref, d_ref, mo_ref, vo_ref = refs[4 * n + 4 * i:4 * n + 4 * i + 4]
            g = g_ref[...]
            go_ref[...] = g
            d_ref[...], mo_ref[...], vo_ref[...] = _adamw_math(w_ref[...], m_ref[...], v_ref[...], g)

    res = pl.pallas_call(body, name=name, out_shape=[jax.ShapeDtypeStruct(a.shape, F32) for a in ws for _ in range(4)])(
        *ws, *ms, *vs, *gs)
    return [res[4 * i:4 * i + 4] for i in range(n)]


def _sum_lead(x, name, rows_apart=False):
    p, r, c = x.shape
    tr = _row_tile(r, c)
    mid = (1,) if rows_apart else ()

    def body(x_ref, o_ref):
        acc = x_ref[0].astype(F32)
        for i in range(1, p):
            acc = acc + x_ref[i].astype(F32)
        o_ref[...] = acc.reshape(o_ref.shape)

    return pl.pallas_call(body, name=name, grid=(pl.cdiv(r, tr),), in_specs=[_vspec((p, tr, c), lambda i: (0, i, 0))],
                          out_specs=_vspec((tr,) + mid + (c,), lambda i: (i,) + (0,) * (1 + len(mid))),
                          out_shape=jax.ShapeDtypeStruct((r,) + mid + (c,), F32))(x)


def _allgather8(x_shard, name, plug=None):
    m_per, n = x_shard.shape

    def body(x_ref, out_ref, send_sems, recv_sems, local_sem):
        x, y, c = lax.axis_index("x"), lax.axis_index("y"), lax.axis_index("c")
        me, sibling = (x, y, c), (x, y, 1 - c)
        chips = [(1 - x, y), (x, 1 - y), (1 - x, 1 - y)]

        def rows(px, py, pc):
            return out_ref.at[pl.ds((4 * px + 2 * py + pc) * m_per, m_per), :]

        def copy(k, block, to, src=None):
            return pltpu.make_async_remote_copy(
                src_ref=rows(*block) if src is None else src, dst_ref=rows(*block), send_sem=send_sems.at[k],
                recv_sem=recv_sems.at[k], device_id=to, device_id_type=MESH)

        mine = pltpu.make_async_copy(x_ref, rows(*me), local_sem)
        mine.start()
        first = [copy(0, me, sibling, src=x_ref)]
        first += [copy(1 + j, me, (*chip, c), src=x_ref) for j, chip in enumerate(chips)]
        for cp in first:
            cp.start()
        passed = [copy(4 + j, (*chip, c), sibling) for j, chip in enumerate(chips)]
        for j, chip in enumerate(chips):
            copy(1 + j, (*chip, c), me).wait_recv()
            passed[j].start()
        copy(0, sibling, me).wait_recv()
        for j, chip in enumerate(chips):
            copy(4 + j, (*chip, 1 - c), me).wait_recv()
        for cp in first + passed:
            cp.wait_send()
        mine.wait()

    grid = (1,)
    (out,), extra = _plugged_call(
        body, plug, _grid_ends(grid), (x_shard,), name=name, grid=grid, out_shape=[jax.ShapeDtypeStruct((8 * m_per, n), x_shard.dtype)],
        in_specs=[pl.BlockSpec(memory_space=pltpu.VMEM)], out_specs=[pl.BlockSpec(memory_space=pltpu.VMEM)],
        scratch_shapes=[pltpu.SemaphoreType.DMA((7,)), pltpu.SemaphoreType.DMA((7,)), pltpu.SemaphoreType.DMA])
    return out if plug is None else (out, extra)


_HBM = pl.BlockSpec(memory_space=pltpu.HBM)


def _mesh_place():
    x, y, c = lax.axis_index("x"), lax.axis_index("y"), lax.axis_index("c")
    return x, y, c, 2 * x + y, [(1 - x, y), (x, 1 - y), (1 - x, 1 - y)]


def _gather_plug(shards):
    n = len(shards)

    def half(ref, c, lead=None):
        r, cols = ref.shape[-2] // 2, ref.shape[-1] // 2
        if r % 16 == 0:
            rows = pl.ds(pl.multiple_of(c * r, 16), r)
            return ref.at[rows, :] if lead is None else ref.at[lead, rows, :]
        lanes = pl.ds(pl.multiple_of(c * cols, LANES), cols)
        return ref.at[:, lanes] if lead is None else ref.at[lead, :, lanes]

    def copies(ins, outs, send, recv):
        x, y, c, me, chips = _mesh_place()
        ici, fwd, fwd_in = [], [], []
        for i in range(n):
            for j, (px, py) in enumerate(chips):
                q = 2 * px + py
                ici.append((pltpu.make_async_remote_copy(
                    src_ref=half(ins[i], c), dst_ref=half(outs[i], c, me), send_sem=send.at[6 * i + j], recv_sem=recv.at[6 * i + j],
                    device_id=(px, py, c), device_id_type=MESH),
                    pltpu.make_async_remote_copy(
                    src_ref=half(ins[i], c), dst_ref=half(outs[i], c, q), send_sem=send.at[6 * i + j], recv_sem=recv.at[6 * i + j],
                    device_id=(px, py, c), device_id_type=MESH)))
                fwd.append(pltpu.make_async_remote_copy(
                    src_ref=half(outs[i], c, q), dst_ref=half(outs[i], c, q), send_sem=send.at[6 * i + 3 + j],
                    recv_sem=recv.at[6 * i + 3 + j], device_id=(x, y, 1 - c), device_id_type=MESH))
                fwd_in.append(pltpu.make_async_remote_copy(
                    src_ref=half(outs[i], 1 - c, q), dst_ref=half(outs[i], 1 - c, q), send_sem=send.at[6 * i + 3 + j],
                    recv_sem=recv.at[6 * i + 3 + j], device_id=(x, y, 1 - c), device_id_type=MESH))
        return ici, fwd, fwd_in, me

    def start(ins, outs, send, recv, loc):
        ici, _, _, me = copies(ins, outs, send, recv)
        for i in range(n):
            pltpu.make_async_copy(ins[i], outs[i].at[me], loc.at[i]).start()
        for out_cp, _ in ici:
            out_cp.start()

    def finish(ins, outs, send, recv, loc):
        ici, fwd, fwd_in, me = copies(ins, outs, send, recv)
        for (_, in_cp), f in zip(ici, fwd):
            in_cp.wait_recv()
            f.start()
        for f in fwd_in:
            f.wait_recv()
        for (out_cp, _), f in zip(ici, fwd):
            out_cp.wait_send()
            f.wait_send()
        for i in range(n):
            pltpu.make_async_copy(ins[i], outs[i].at[me], loc.at[i]).wait()

    return dict(ins=list(shards), out_shape=[jax.ShapeDtypeStruct((N_CHIPS,) + a.shape, a.dtype) for a in shards],
                scratch=[pltpu.SemaphoreType.DMA((6 * n,)), pltpu.SemaphoreType.DMA((6 * n,)), pltpu.SemaphoreType.DMA((n,))],
                start=start, finish=finish)


def _exchange_plug(pieces):
    n = len(pieces)

    def copies(ins, outs, send, recv):
        x, y, c, me, chips = _mesh_place()
        out_cps, in_cps = [], []
        for i in range(n):
            for j, (px, py) in enumerate(chips):
                q = 2 * px + py
                out_cps.append(pltpu.make_async_remote_copy(src_ref=ins[i].at[q], dst_ref=outs[i].at[me], send_sem=send.at[3 * i + j],
                                                            recv_sem=recv.at[3 * i + j], device_id=(px, py, c), device_id_type=MESH))
                in_cps.append(pltpu.make_async_remote_copy(src_ref=ins[i].at[me], dst_ref=outs[i].at[q], send_sem=send.at[3 * i + j],
                                                           recv_sem=recv.at[3 * i + j], device_id=(px, py, c), device_id_type=MESH))
        return out_cps, in_cps, me

    def start(ins, outs, send, recv, loc):
        out_cps, _, me = copies(ins, outs, send, recv)
        for i in range(n):
            pltpu.make_async_copy(ins[i].at[me], outs[i].at[me], loc.at[i]).start()
        for cp in out_cps:
            cp.start()

    def finish(ins, outs, send, recv, loc):
        out_cps, in_cps, me = copies(ins, outs, send, recv)
        for cp in in_cps:
            cp.wait_recv()
        for cp in out_cps:
            cp.wait_send()
        for i in range(n):
            pltpu.make_async_copy(ins[i].at[me], outs[i].at[me], loc.at[i]).wait()

    return dict(ins=list(pieces), out_shape=[jax.ShapeDtypeStruct(a.shape, a.dtype) for a in pieces],
                scratch=[pltpu.SemaphoreType.DMA((3 * n,)), pltpu.SemaphoreType.DMA((3 * n,)), pltpu.SemaphoreType.DMA((n,))],
                start=start, finish=finish)


def _plugged_call(body, plug, first_last, args, *, name, grid, in_specs, out_specs, out_shape, scratch_shapes=(), aliases=None):
    in_specs, out_specs, out_shape, scratch_shapes = list(in_specs), list(out_specs), list(out_shape), list(scratch_shapes)
    aliases = dict(aliases or {})
    if plug is None:
        return pl.pallas_call(body, name=name, grid=grid, in_specs=in_specs, out_specs=out_specs, out_shape=out_shape,
                              scratch_shapes=scratch_shapes, input_output_aliases=aliases)(*args), []
    n_in, n_out, n_sc = len(in_specs), len(out_specs), len(scratch_shapes)
    p_in, p_out = len(plug["ins"]), len(plug["out_shape"])

    def full(*refs):
        ins, refs = refs[:n_in], refs[n_in:]
        pins, refs = refs[:p_in], refs[p_in:]
        outs, refs = refs[:n_out], refs[n_out:]
        pouts, refs = refs[:p_out], refs[p_out:]
        scr, psems = refs[:n_sc], refs[n_sc:]
        first, last = first_last()

        @pl.when(first)
        def _():
            plug["start"](pins, pouts, *psems)

        body(*ins, *outs, *scr)

        @pl.when(last)
        def _():
            plug["finish"](pins, pouts, *psems)

    res = pl.pallas_call(full, name=name, grid=grid, in_specs=in_specs + [_HBM] * p_in, out_specs=out_specs + [_HBM] * p_out,
                         out_shape=out_shape + plug["out_shape"], scratch_shapes=scratch_shapes + plug["scratch"],
                         input_output_aliases=aliases)(*args, *plug["ins"])
    return res[:n_out], res[n_out:]


def _grid_ends(grid):
    def ends():
        first = last = None
        for ax, n in enumerate(grid):
            i = pl.program_id(ax)
            first = (i == 0) if first is None else first & (i == 0)
            last = (i == n - 1) if last is None else last & (i == n - 1)
        return first, last
    return ends


def _pair_presum(pieces, name):
    n, r, cols = pieces.shape
    by_cols = (cols // 2) % LANES == 0
    half_shape = (n, r, cols // 2) if by_cols else (n, r // 2, cols)

    def body(p_ref, o_ref, mine_ref, land_ref, send_sem, recv_sem, local_sem):
        x, y, c = lax.axis_index("x"), lax.axis_index("y"), lax.axis_index("c")

        def half(which):
            if by_cols:
                return p_ref.at[:, :, pl.ds(pl.multiple_of(which * (cols // 2), LANES), cols // 2)]
            return p_ref.at[:, pl.ds(pl.multiple_of(which * (r // 2), 16), r // 2), :]

        push = pltpu.make_async_remote_copy(src_ref=half(1 - c), dst_ref=land_ref, send_sem=send_sem, recv_sem=recv_sem,
                                            device_id=(x, y, 1 - c), device_id_type=MESH)
        own = pltpu.make_async_copy(half(c), mine_ref, local_sem)
        push.start()
        own.start()
        own.wait()
        push.wait_recv()
        for i in range(n):
            o_ref[i] = (mine_ref[i].astype(F32) + land_ref[i].astype(F32)).astype(BF16)
        push.wait_send()

    return pl.pallas_call(
        body, name=name, out_shape=jax.ShapeDtypeStruct(half_shape, BF16), in_specs=[_HBM],
        out_specs=pl.BlockSpec(memory_space=pltpu.VMEM),
        scratch_shapes=[pltpu.VMEM(half_shape, BF16), pltpu.VMEM(half_shape, BF16), pltpu.SemaphoreType.DMA,
                        pltpu.SemaphoreType.DMA, pltpu.SemaphoreType.DMA],
    )(pieces)


def _sibling_plug(arrs):
    n = len(arrs)

    def copies(ins, outs, send, recv):
        sibling = (lax.axis_index("x"), lax.axis_index("y"), 1 - lax.axis_index("c"))
        return [pltpu.make_async_remote_copy(src_ref=ins[i], dst_ref=outs[i], send_sem=send.at[i], recv_sem=recv.at[i],
                                             device_id=sibling, device_id_type=MESH) for i in range(n)]

    def start(ins, outs, send, recv):
        for cp in copies(ins, outs, send, recv):
            cp.start()

    def finish(ins, outs, send, recv):
        for cp in copies(ins, outs, send, recv):
            cp.wait()

    return dict(ins=list(arrs), out_shape=[jax.ShapeDtypeStruct(a.shape, a.dtype) for a in arrs],
                scratch=[pltpu.SemaphoreType.DMA((n,)), pltpu.SemaphoreType.DMA((n,))], start=start, finish=finish)


_SEGS = ((0, AQ, C_Q), (AQ, AKV, C_K), (AQ + AKV, AKV, C_V), (AQ + 2 * AKV, 3 * DN, C_DQKV), (2304, DN, C_DZ),
         (2816, 2 * DN_HEADS, C_BD), (2824, D_MODEL, C_GA), (3848, D_MODEL, C_GD))
SHARD_ROWS = IN_DIM // N_CHIPS


def _to_padded(w4):
    wt = w4.reshape(IN_DIM, w4.shape[2])
    parts = [wt[o:o + n] for o, n, _ in sorted(_SEGS, key=lambda sg: sg[2])]
    return jnp.concatenate(parts + [jnp.zeros((IN_PAD - IN_DIM, wt.shape[1]), wt.dtype)], axis=0)


def _from_padded(gt):
    return jnp.concatenate([gt[ps:ps + n] for _, n, ps in sorted(_SEGS)], axis=0).reshape(N_CHIPS, SHARD_ROWS, gt.shape[1])


def _lane_vec(a):
    return jnp.zeros((1, LANES), F32).at[0, DN_HEADS:2 * DN_HEADS].set(a)


_ROW_SHARDED = ("w_in", "w_out", "ffn_w_down")
_FFN = ("ffn_w_up", "ffn_w_down")
_LATE_MIXER = ("w_attn_branch", "w_dn_branch", "w_out")
_PRESUM = ("w_in", "ffn_w_up")


def _pieces(k, a):
    if a.ndim == 3:
        return a
    if k in _ROW_SHARDED:
        return a.reshape(N_CHIPS, a.shape[0] // N_CHIPS, a.shape[1]).astype(BF16)
    return jnp.transpose(a.reshape(a.shape[0], N_CHIPS, a.shape[1] // N_CHIPS), (1, 0, 2)).astype(BF16)


def _assemble(k, a):
    if k in _ROW_SHARDED:
        return a.reshape(-1, a.shape[2])
    return jnp.transpose(a, (1, 0, 2)).reshape(a.shape[1], -1)


def _device_step(x2, tgt2, mod, p, bsz, shards=None):
    d = D_MODEL
    on_mesh = shards is not None
    p = dict(p)
    sh1, sc1, g1, sh2, sc2, g2 = [mod[:, i * d:(i + 1) * d].reshape(bsz, 1, d) for i in range(N_MOD)]
    alog_v, dt_v = _lane_vec(p["dn_a_log"]), _lane_vec(p["dn_dt_bias"])
    sinks = p["attn_sinks"].reshape(1, AQ_HEADS)
    if on_mesh:
        u1, got = _pre_fwd(x2, p["norm_mix_pre"], sc1, sh1, "pre1_fwd", plug=_gather_plug(shards["w_in_last"]))
        p["w_in"] = _to_padded(jnp.concatenate(list(shards["w_in_gathered"]) + list(got), axis=1))
    else:
        u1 = _pre_fwd(x2, p["norm_mix_pre"], sc1, sh1, "pre1_fwd")
    if on_mesh:
        proj, got = _mm(u1, p["w_in"], "nt", F32, "mm_proj", _gather_plug(shards["late_mixer"]))
        p.update({k: _assemble(k, a) for k, a in zip(_LATE_MIXER, got)})
    else:
        proj = _mm(u1, p["w_in"], "nt", F32, "mm_proj")
    bias = _bias_build(p["rel_bias"])
    y_attn = _attn_fwd(proj, bias, sinks, bsz)
    qkvn, dn_conv_out = _dn_prep_fwd(proj, p["dn_conv_w"], bsz)
    bg = _bg_fwd(proj, alog_v, dt_v, bsz)
    nc = x2.shape[0] // bsz // DN_CHUNK
    g_rows = jnp.transpose(bg[:, DN_HEADS:2 * DN_HEADS].reshape(bsz, nc, DN_CHUNK, DN_HEADS), (0, 3, 1, 2))
    o, states, got = _dn_chunk_fwd(qkvn, bg, g_rows, bsz, _gather_plug(shards["ffn"][:1]) if on_mesh else None)
    for k, a in zip(_FFN[:1], got):
        p[k] = _assemble(k, a)
    y_dn = _dn_out_fwd(o, proj, p["dn_norm_w"])
    ya = _mm(y_attn, p["w_attn_branch"], "nn", BF16, "mm_ya")
    yd = _mm(y_dn, p["w_dn_branch"], "nn", BF16, "mm_yd")
    merged = _merge_fwd(proj, ya, yd)
    y1 = _mm(merged, p["w_out"], "nn", BF16, "mm_y1")
    h1, u2 = _post_pre_fwd(x2, y1, p["norm_mix_post"], g1, p["norm_ffn_pre"], sc2, sh2, "post1_pre2_fwd")
    if on_mesh:
        up, got = _mm(u2, p["ffn_w_up"], "nn", BF16, "mm_up", _gather_plug(shards["ffn"][1:]))
        p["ffn_w_down"] = _assemble("ffn_w_down", got[0])
    else:
        up = _mm(u2, p["ffn_w_up"], "nn", BF16, "mm_up")
    act, conv_g, conv_v = _ffn_act_fwd(up, p["ffn_conv_w"], bsz)
    y2 = _mm(act, p["ffn_w_down"], "nn", BF16, "mm_y2")
    dh2, dy2, g_ffn_post, dg2, sq = _post_loss_bwd(h1, y2, p["norm_ffn_post"], g2, tgt2, "post2_loss_bwd")
    g = {}
    g["norm_ffn_post"] = g_ffn_post
    dact = _mm(dy2, p["ffn_w_down"], "nt", BF16, "mm_dact")
    g["ffn_w_down"] = _mm(act, dy2, "tn", BF16, "mm_dwdown")
    dupg, dupv, dcwg, dcwv, got_down = _ffn_act_bwd(
        up, p["ffn_conv_w"], (conv_g, conv_v), dact, bsz,
        _exchange_plug([_pieces("ffn_w_down", g["ffn_w_down"])]) if on_mesh else None)
    g["ffn_conv_w"] = jnp.concatenate([dcwg, dcwv], axis=1)
    g["ffn_w_up"] = jnp.concatenate([_mm(u2, dupg, "tn", BF16, "mm_dwup_gate", split=N_CHIPS // 2),
                                     _mm(u2, dupv, "tn", BF16, "mm_dwup_val", split=N_CHIPS // 2)], axis=0)
    du2 = _mm(dupg, p["ffn_w_up"], "nt", F32, "mm_du2_gate", b_kblock=0)
    du2 = _mm(dupv, p["ffn_w_up"], "nt", BF16, "mm_du2_val", b_kblock=1, add=du2)
    dh1, dy1, g["norm_ffn_pre"], dsc2, dsh2, g["norm_mix_post"], dg1 = _pre_post_bwd(
        h1, p["norm_ffn_pre"], sc2, sh2, du2, dh2, y1, p["norm_mix_post"], g1, "pre2_post1_bwd")
    dmerged = _mm(dy1, p["w_out"], "nt", BF16, "mm_dmerged")
    g["w_out"] = _mm(merged, dy1, "tn", BF16, "mm_dwout")
    dproj = lax.empty((x2.shape[0], IN_PAD), BF16)
    dproj, dya = _branch_bwd(dproj, proj, ya, dmerged, C_GA, "merge_bwd_attn")
    dproj, dyd = _branch_bwd(dproj, proj, yd, dmerged, C_GD, "merge_bwd_dn")
    dy_attn = _mm(dya, p["w_attn_branch"], "nt", BF16, "mm_dyattn")
    g["w_attn_branch"] = _mm(y_attn, dya, "tn", BF16, "mm_dwab", split=N_CHIPS)
    dy_dn = _mm(dyd, p["w_dn_branch"], "nt", BF16, "mm_dydn")
    g["w_dn_branch"] = _mm(y_dn, dyd, "tn", BF16, "mm_dwdb", split=N_CHIPS)
    do, dproj, g["dn_norm_w"] = _dn_out_bwd(dproj, o, proj, p["dn_norm_w"], dy_dn)
    def plug_for(names):
        if not on_mesh:
            return None
        return _exchange_plug([_pair_presum(_pieces(k, g[k]), "presum_" + k) if k in _PRESUM else _pieces(k, g[k]) for k in names])

    early = ("w_out", "w_attn_branch", "w_dn_branch")
    dqkvn, dbg4, got_up = _dn_chunk_bwd(qkvn, bg, g_rows, states, do, bsz, plug_for(_FFN[:1]))
    got_ffn = list(got_up) + list(got_down)
    dproj, g["dn_conv_w"] = _dn_prep_bwd(dproj, proj, p["dn_conv_w"], dn_conv_out, dqkvn, bsz)
    dproj, dk, dv, dbias, g["attn_sinks"], got_early = _attn_bwd(dproj, proj, bias, sinks, dy_attn, bsz, plug_for(early))
    dproj, g["dn_a_log"], g["dn_dt_bias"] = _bg_bwd(dproj, proj, alog_v, dt_v, dbg4, dk, dv, bsz)
    g["rel_bias"] = _bias_grad(dbias)
    g["w_in"] = _from_padded(_mm(dproj, u1, "tn", BF16, "mm_dwin"))
    if on_mesh:
        du1, got_in = _mm(dproj, p["w_in"], "nn", BF16, "mm_du1", plug_for(("w_in",)))
        g.update(zip(_FFN + early + ("w_in",), list(got_ffn) + list(got_early) + list(got_in)))
    else:
        du1 = _mm(dproj, p["w_in"], "nn", BF16, "mm_du1")
    dx, g["norm_mix_pre"], dsc1, dsh1 = _pre_bwd(x2, p["norm_mix_pre"], sc1, sh1, du1, dh1, "pre1_bwd")
    dmod = jnp.concatenate([dsh1, dsc1, dg1, dsh2, dsc2, dg2], axis=-1).reshape(bsz, N_MOD * d)
    return sq, dx, dmod, g


_SMALL = (("norm_mix_pre", D_MODEL), ("norm_mix_post", D_MODEL), ("norm_ffn_pre", D_MODEL), ("norm_ffn_post", D_MODEL),
          ("dn_norm_w", DN_HD), ("dn_a_log", LANES), ("dn_dt_bias", LANES), ("attn_sinks", AQ_HEADS * LANES),
          ("rel_bias", AQ_HEADS * LANES), ("dn_conv_w", DN_CONV * 3 * DN), ("ffn_conv_w", FFN_CONV * 2 * D_FF),
          ("loss_sq", D_MODEL))


def _pack_rows(parts, rows):
    flat = jnp.concatenate([a.reshape(-1) for a in parts])
    return jnp.concatenate([flat, jnp.zeros((rows * LANES - flat.shape[0],), F32)]).reshape(rows, LANES)


_W_NAMES = ("ada_w", "ada_b", "norm_mix_pre", "norm_mix_post", "norm_ffn_pre", "norm_ffn_post", "w_in", "dn_conv_w", "dn_a_log",
            "dn_dt_bias", "dn_norm_w", "attn_sinks", "rel_bias", "w_attn_branch", "w_dn_branch", "w_out", "ffn_w_up", "ffn_conv_w",
            "ffn_w_down")
_BIG = ("w_in", "w_attn_branch", "w_dn_branch", "w_out", "ffn_w_up", "ffn_w_down")


def kernel(x, c, *rest):
    nw = len(_W_NAMES)
    w = dict(zip(_W_NAMES, rest[:nw]))
    loss_target = rest[nw]
    m = dict(zip(_W_NAMES, rest[nw + 1:2 * nw + 1]))
    v = dict(zip(_W_NAMES, rest[2 * nw + 1:3 * nw + 1]))
    ix, iy, ic = lax.axis_index("x"), lax.axis_index("y"), lax.axis_index("c")
    chip, dev = 2 * ix + iy, 4 * ix + 2 * iy + ic
    bsz, s, d = x.shape
    t = bsz * s
    n_dev = 8

    front_rows = 64
    front = _pack_rows([c, w["dn_conv_w"], w["ffn_conv_w"]], front_rows)
    w_in_t = jnp.swapaxes(w["w_in"][0], 0, 1).astype(BF16)
    cut1 = SHARD_ROWS // 48 * 16
    cut2 = 2 * cut1
    front_all, (w_in_lo,) = _allgather8(front, "ag_front", _gather_plug([w_in_t[:cut1]]))
    front_all = front_all.reshape(n_dev, front_rows * LANES)
    n_c, n_dc, n_fc = bsz * d, DN_CONV * 3 * DN // N_CHIPS, FFN_CONV * 2 * D_FF // N_CHIPS
    c_all = front_all[:, :n_c].reshape(n_dev * bsz, d)
    per_chip = front_all[0::2]
    dn_conv_full = jnp.transpose(per_chip[:, n_c:n_c + n_dc].reshape(N_CHIPS, DN_CONV, -1), (1, 0, 2)).reshape(DN_CONV, 3 * DN)
    ffn_conv_full = jnp.transpose(per_chip[:, n_c + n_dc:n_c + n_dc + n_fc].reshape(N_CHIPS, FFN_CONV, -1), (1, 0, 2)).reshape(FFN_CONV, 2 * D_FF)

    mod_cols = N_MOD * d // N_CHIPS
    ada_b_loc = lax.dynamic_slice(w["ada_b"], (0, chip * mod_cols), (1, mod_cols))
    mod_part = _mod_fwd(c_all, w["ada_w"][0], ada_b_loc)
    mod_all, (w_in_mid,) = _allgather8(mod_part, "ag_mod", _gather_plug([w_in_t[cut1:cut2]]))
    mod_all = mod_all.reshape(n_dev, n_dev * bsz, mod_cols)[0::2]
    mod = jnp.transpose(lax.dynamic_slice(mod_all, (0, dev * bsz, 0), (N_CHIPS, bsz, mod_cols)), (1, 0, 2)).reshape(bsz, N_MOD * d)

    p = {}
    shards = {"late_mixer": [w[k][0].astype(BF16) for k in _LATE_MIXER], "ffn": [w[k][0].astype(BF16) for k in _FFN],
              "w_in_gathered": [w_in_lo, w_in_mid], "w_in_last": [w_in_t[cut2:]]}
    for k in ("norm_mix_pre", "norm_mix_post", "norm_ffn_pre", "norm_ffn_post", "dn_norm_w", "attn_sinks"):
        p[k] = w[k]
    p["dn_a_log"], p["dn_dt_bias"], p["rel_bias"] = w["dn_a_log"][0], w["dn_dt_bias"][0], w["rel_bias"]
    p["dn_conv_w"], p["ffn_conv_w"] = dn_conv_full, ffn_conv_full

    sq, dx, dmod, g = _device_step(x.reshape(t, d), loss_target.reshape(t, d), mod, p, bsz, shards)
    g["dn_a_log"], g["dn_dt_bias"], g["loss_sq"] = g["dn_a_log"].reshape(-1), g["dn_dt_bias"].reshape(-1), sq
    small_rows = 336
    small = _pack_rows([dmod] + [g[k] for k, _ in _SMALL], small_rows)
    small_all = _allgather8(small, "ag_small").reshape(n_dev, small_rows, LANES)
    n_dm = bsz * N_MOD * d
    dmod_all = small_all.reshape(n_dev, -1)[:, :n_dm].reshape(n_dev * bsz, N_MOD * d)
    tot = _sum_lead(small_all, "sum_small").reshape(-1)
    gs, off = {}, n_dm
    for k, n in _SMALL:
        gs[k] = tot[off:off + n]
        off += n
    loss = 0.5 * jnp.sum(gs["loss_sq"])
    grad = {}
    grad["ada_w"], grad["ada_b"] = _ada_grad(c_all, lax.dynamic_slice(dmod_all, (0, chip * mod_cols), (n_dev * bsz, mod_cols)), dmod_all)
    for k in ("norm_mix_pre", "norm_mix_post", "norm_ffn_pre", "norm_ffn_post", "dn_norm_w"):
        grad[k] = gs[k]
    grad["dn_a_log"] = gs["dn_a_log"][DN_HEADS:2 * DN_HEADS]
    grad["dn_dt_bias"] = gs["dn_dt_bias"][DN_HEADS:2 * DN_HEADS]
    grad["attn_sinks"] = gs["attn_sinks"].reshape(AQ_HEADS, LANES)[:, 0]
    grad["rel_bias"] = gs["rel_bias"].reshape(AQ_HEADS, LANES)[:, :REL_BUCKETS].T
    grad["dn_conv_w"] = lax.dynamic_slice(gs["dn_conv_w"].reshape(DN_CONV, 3 * DN), (0, chip * (3 * DN // N_CHIPS)), (DN_CONV, 3 * DN // N_CHIPS))
    grad["ffn_conv_w"] = lax.dynamic_slice(gs["ffn_conv_w"].reshape(FFN_CONV, 2 * D_FF), (0, chip * (2 * D_FF // N_CHIPS)), (FFN_CONV, 2 * D_FF // N_CHIPS))

    mine = [_sum_lead(g[k], "sum_" + k, rows_apart=(k == "w_in")) for k in _BIG]
    out = {}
    out["ada_w"], theirs = _adamw(w["ada_w"][0], m["ada_w"][0], v["ada_w"][0], [grad["ada_w"]], "adamw_ada_w", _sibling_plug(mine))
    for k, a, b in zip(_BIG, mine, theirs):
        core = ic.reshape(1).astype(jnp.int32)
        if k == "w_in":
            tr = lambda z: jnp.transpose(z, (2, 0, 1))
            out[k] = [jnp.transpose(r, (1, 2, 0)) for r in _adamw_halves(tr(w[k]), tr(m[k]), tr(v[k]), a, b, core, "adamw_" + k)]
        elif k in _PRESUM:
            out[k] = _adamw_halves(w[k][0], m[k][0], v[k][0], a, b, core, "adamw_" + k)
        else:
            out[k] = _adamw(w[k][0], m[k][0], v[k][0], [a, b], "adamw_" + k)
    small_names = [k for k in _W_NAMES if k not in _BIG and k != "ada_w"]
    res_small = _adamw_small([w[k] for k in small_names], [m[k] for k in small_names], [v[k] for k in small_names],
                             [grad[k].reshape(w[k].shape) for k in small_names], "adamw_small")
    out.update(zip(small_names, res_small))
    for k in _BIG + ("ada_w",):
        out[k] = [r.reshape(w[k].shape) for r in out[k]]
    grads, deltas, new_m, new_v = ([out[k][i] for k in _W_NAMES] for i in range(4))
    return (loss, dx.reshape(bsz, s, d), *grads, *deltas, *new_m, *new_v)
```

```python
import math

import numpy as np
import jax
import jax.numpy as jnp
from jax import lax
from jax.experimental import pallas as pl
from jax.experimental.pallas import tpu as pltpu

F32 = jnp.float32
BF16 = jnp.bfloat16
MESH = pl.DeviceIdType.MESH

D_MODEL = 1024
N_MOD = 6
AQ_HEADS, AKV_HEADS, A_HD, WINDOW = 8, 2, 64, 128
REL_BUCKETS, REL_MAX_DIST = 32, 128
DN_HEADS, DN_HD, DN_CONV, DN_CHUNK = 4, 128, 4, 64
D_FF, FFN_CONV = 2816, 3
RMS_EPS, L2_EPS, NEG_INF = 1e-6, 1e-6, -1e30
AQ, AKV, DN = AQ_HEADS * A_HD, AKV_HEADS * A_HD, DN_HEADS * DN_HD
IN_DIM = AQ + 2 * AKV + 3 * DN + DN + 2 * DN_HEADS + 2 * D_MODEL
C_DQKV, C_Q, C_DZ, C_GA, C_GD, C_K, C_V, C_BD = 0, 1536, 2048, 2560, 3584, 4608, 4736, 4864
IN_PAD = 4992
LANES = 128
N_CHIPS = 4

ADAM_LR, ADAM_B1, ADAM_B2, ADAM_EPS, ADAM_WD, ADAM_STEP = 0.001, 0.9, 0.999, 1e-08, 0.01, 10


def _vspec(shape, index_map):
    return pl.BlockSpec(shape, index_map)


MM_VMEM_BUDGET = 40 * 2 ** 20
GRID_STEP_S = 0.35e-6
HBM_BYTES_PER_S = 3.0e12
MXU_FLOPS_PER_S = 9.0e14
MXU_DIM = 256


def _mm_tiles(m, n, k, mode, in_bytes, out_bytes, split=1):
    best = None
    for tm in [t for t in range(LANES, m + 1, LANES) if m % t == 0]:
        for tn in [t for t in range(LANES, n // split + 1, LANES) if (n // split) % t == 0]:
            a, b, o = k * tm * in_bytes, k * tn * in_bytes, tm * tn * out_bytes
            if 2 * (a + b + o) + (a if mode == "tn" else 0) > MM_VMEM_BUDGET:
                continue
            hbm_s = (m * k * in_bytes + (m // tm) * n * k * in_bytes + m * n * out_bytes) / HBM_BYTES_PER_S
            mxu_s = 2 * m * n * k / (MXU_FLOPS_PER_S * min(1.0, tm / MXU_DIM) * min(1.0, tn / MXU_DIM))
            cost = (m // tm) * (n // tn) * GRID_STEP_S + max(hbm_s, mxu_s)
            if best is None or cost < best[0]:
                best = (cost, tm, tn)
    return best[1], best[2]


def _mm(a, b, mode, out_dtype, name, plug=None, split=1, b_kblock=0, add=None):
    if mode == "nn":
        (m, k), n = a.shape, b.shape[1]
        dims = (((1,), (0,)), ((), ()))
    elif mode == "nt":
        (m, k), n = a.shape, b.shape[0]
        dims = (((1,), (1,)), ((), ()))
    else:
        (k, m), n = a.shape, b.shape[1]
        dims = (((0,), (0,)), ((), ()))
    tm, tn = _mm_tiles(m, n, k, mode, a.dtype.itemsize, jnp.dtype(out_dtype).itemsize, split)
    if mode == "tn":
        a_spec = _vspec((k, tm), lambda i, j: (0, i))
    else:
        a_spec = _vspec((tm, k), lambda i, j: (i, 0))
    if mode == "nt":
        b_spec = _vspec((tn, k), lambda i, j: (j, b_kblock))
    else:
        b_spec = _vspec((k, tn), lambda i, j: (0, j))
    in_specs, args = [a_spec, b_spec], (a, b)
    if add is not None:
        in_specs, args = in_specs + [_vspec((tm, tn), lambda i, j: (i, j))], (a, b, add)

    def body(a_ref, b_ref, *rest):
        o_ref = rest[-1]
        acc = lax.dot_general(a_ref[...].astype(BF16), b_ref[...].astype(BF16), dims, preferred_element_type=F32)
        if add is not None:
            acc = acc + rest[0][...]
        o_ref[...] = acc.astype(out_dtype).reshape(o_ref.shape)

    grid = (m // tm, n // tn)
    if split == 1:
        out_spec, out_shape = _vspec((tm, tn), lambda i, j: (i, j)), (m, n)
    else:
        per = n // split // tn
        out_spec, out_shape = _vspec((1, tm, tn), lambda i, j: (j // per, i, j % per)), (split, m, n // split)
    (out,), extra = _plugged_call(body, plug, _grid_ends(grid), args, name=name, grid=grid, in_specs=in_specs,
                                  out_specs=[out_spec], out_shape=[jax.ShapeDtypeStruct(out_shape, out_dtype)])
    return out if plug is None else (out, extra)


def _rms(x, w):
    return (x * lax.rsqrt(jnp.mean(x * x, axis=-1, keepdims=True) + RMS_EPS)) * w


def _pre_f(x, w, sc, sh):
    return _rms(x, w) * (1.0 + sc) + sh


def _post_f(y, w, g):
    return g * _rms(y, w)


def _tok_grid(t, bsz, ts):
    nt = t // bsz // ts
    return nt, (bsz, nt)


def _pre_fwd(x, w, sc, sh, name, ts=512, plug=None):
    t, d = x.shape
    bsz = sc.shape[0]
    nt, grid = _tok_grid(t, bsz, ts)
    row = _vspec((ts, d), lambda b, i: (b * nt + i, 0))
    vec = _vspec((1, d), lambda b, i: (0, 0))
    bvec = _vspec((1, 1, d), lambda b, i: (b, 0, 0))

    def body(x_ref, w_ref, sc_ref, sh_ref, u_ref):
        u_ref[...] = _pre_f(x_ref[...], w_ref[...], sc_ref[0], sh_ref[0]).astype(BF16)

    (u,), extra = _plugged_call(body, plug, _grid_ends(grid), (x, w, sc, sh), name=name, grid=grid, in_specs=[row, vec, bvec, bvec],
                                out_specs=[row], out_shape=[jax.ShapeDtypeStruct((t, d), BF16)])
    return u if plug is None else (u, extra)


def _pre_bwd(x, w, sc, sh, du, dres, name, ts=512):
    t, d = x.shape
    bsz = sc.shape[0]
    nt, grid = _tok_grid(t, bsz, ts)
    row = _vspec((ts, d), lambda b, i: (b * nt + i, 0))
    vec = _vspec((1, d), lambda b, i: (0, 0))
    bvec = _vspec((1, 1, d), lambda b, i: (b, 0, 0))

    def body(x_ref, w_ref, sc_ref, sh_ref, du_ref, dres_ref, dx_ref, dw_ref, dsc_ref, dsh_ref):
        b, i = pl.program_id(0), pl.program_id(1)
        _, vjp = jax.vjp(_pre_f, x_ref[...], w_ref[...], sc_ref[0], sh_ref[0])
        dx, dw, dsc, dsh = vjp(du_ref[...].astype(F32))
        dx_ref[...] = dres_ref[...] + dx

        @pl.when((b == 0) & (i == 0))
        def _():
            dw_ref[...] = jnp.zeros_like(dw_ref)

        @pl.when(i == 0)
        def _():
            dsc_ref[...] = jnp.zeros_like(dsc_ref)
            dsh_ref[...] = jnp.zeros_like(dsh_ref)

        dw_ref[...] += dw
        dsc_ref[0] += dsc
        dsh_ref[0] += dsh

    return pl.pallas_call(
        body, name=name, grid=grid, in_specs=[row, vec, bvec, bvec, row, row], out_specs=[row, vec, bvec, bvec],
        out_shape=[jax.ShapeDtypeStruct((t, d), F32), jax.ShapeDtypeStruct((1, d), F32),
                   jax.ShapeDtypeStruct((bsz, 1, d), F32), jax.ShapeDtypeStruct((bsz, 1, d), F32)],
    )(x, w, sc, sh, du, dres)


def _accumulate(ref, val, first):
    @pl.when(first)
    def _():
        ref[...] = jnp.zeros_like(ref)

    ref[...] += val.reshape(ref.shape)


def _post_pre_fwd(res, y, w_post, g, w_pre, sc, sh, name, ts=512):
    t, d = y.shape
    bsz = g.shape[0]
    nt, grid = _tok_grid(t, bsz, ts)
    row = _vspec((ts, d), lambda b, i: (b * nt + i, 0))
    vec = _vspec((1, d), lambda b, i: (0, 0))
    bvec = _vspec((1, 1, d), lambda b, i: (b, 0, 0))

    def body(res_ref, y_ref, wp_ref, g_ref, w_ref, sc_ref, sh_ref, h_ref, u_ref):
        h = res_ref[...] + _post_f(y_ref[...], wp_ref[...], g_ref[0])
        h_ref[...] = h
        u_ref[...] = _pre_f(h, w_ref[...], sc_ref[0], sh_ref[0]).astype(BF16)

    return pl.pallas_call(body, name=name, grid=grid, in_specs=[row, row, vec, bvec, vec, bvec, bvec], out_specs=[row, row],
                          out_shape=[jax.ShapeDtypeStruct((t, d), F32), jax.ShapeDtypeStruct((t, d), BF16)],
                          )(res, y, w_post, g, w_pre, sc, sh)


def _post_loss_bwd(res, y, w, g, tgt, name, ts=512):
    t, d = y.shape
    bsz = g.shape[0]
    nt, grid = _tok_grid(t, bsz, ts)
    row = _vspec((ts, d), lambda b, i: (b * nt + i, 0))
    vec = _vspec((1, d), lambda b, i: (0, 0))
    bvec = _vspec((1, 1, d), lambda b, i: (b, 0, 0))

    def body(res_ref, y_ref, w_ref, g_ref, tgt_ref, dh_ref, dy_ref, dw_ref, dg_ref, sq_ref):
        b, i = pl.program_id(0), pl.program_id(1)
        part, vjp = jax.vjp(_post_f, y_ref[...], w_ref[...], g_ref[0])
        e = res_ref[...] + part - tgt_ref[...]
        dh = e * (1.0 / d)
        dh_ref[...] = dh
        dy, dw, dg = vjp(dh)
        dy_ref[...] = dy.astype(BF16)
        _accumulate(dw_ref, dw, (b == 0) & (i == 0))
        _accumulate(dg_ref, dg, i == 0)
        _accumulate(sq_ref, jnp.sum(e * e, axis=0, keepdims=True) * (1.0 / d), (b == 0) & (i == 0))

    return pl.pallas_call(
        body, name=name, grid=grid, in_specs=[row, row, vec, bvec, row], out_specs=[row, row, vec, bvec, vec],
        out_shape=[jax.ShapeDtypeStruct((t, d), F32), jax.ShapeDtypeStruct((t, d), BF16), jax.ShapeDtypeStruct((1, d), F32),
                   jax.ShapeDtypeStruct((bsz, 1, d), F32), jax.ShapeDtypeStruct((1, d), F32)],
    )(res, y, w, g, tgt)


def _pre_post_bwd(x, w, sc, sh, du, dres, y, w_post, g, name, ts=512):
    t, d = x.shape
    bsz = sc.shape[0]
    nt, grid = _tok_grid(t, bsz, ts)
    row = _vspec((ts, d), lambda b, i: (b * nt + i, 0))
    vec = _vspec((1, d), lambda b, i: (0, 0))
    bvec = _vspec((1, 1, d), lambda b, i: (b, 0, 0))

    def body(x_ref, w_ref, sc_ref, sh_ref, du_ref, dres_ref, y_ref, wp_ref, g_ref,
             dx_ref, dy_ref, dw_ref, dsc_ref, dsh_ref, dwp_ref, dg_ref):
        b, i = pl.program_id(0), pl.program_id(1)
        _, vjp = jax.vjp(_pre_f, x_ref[...], w_ref[...], sc_ref[0], sh_ref[0])
        dx, dw, dsc, dsh = vjp(du_ref[...].astype(F32))
        dx = dres_ref[...] + dx
        dx_ref[...] = dx
        _, vjp_post = jax.vjp(_post_f, y_ref[...], wp_ref[...], g_ref[0])
        dy, dwp, dg = vjp_post(dx)
        dy_ref[...] = dy.astype(BF16)
        first = (b == 0) & (i == 0)
        _accumulate(dw_ref, dw, first)
        _accumulate(dwp_ref, dwp, first)
        _accumulate(dsc_ref, dsc, i == 0)
        _accumulate(dsh_ref, dsh, i == 0)
        _accumulate(dg_ref, dg, i == 0)

    v1, vb = jax.ShapeDtypeStruct((1, d), F32), jax.ShapeDtypeStruct((bsz, 1, d), F32)
    return pl.pallas_call(
        body, name=name, grid=grid, in_specs=[row, vec, bvec, bvec, row, row, row, vec, bvec],
        out_specs=[row, row, vec, bvec, bvec, vec, bvec],
        out_shape=[jax.ShapeDtypeStruct((t, d), F32), jax.ShapeDtypeStruct((t, d), BF16), v1, vb, vb, v1, vb],
    )(x, w, sc, sh, du, dres, y, w_post, g)


def _merge_f(ga, gd, ya, yd):
    return jax.nn.sigmoid(ga) * ya + jax.nn.sigmoid(gd) * yd


_MW = 512


def _merge_fwd(proj, ya, yd, ts=512):
    t, d = ya.shape
    blk = _vspec((ts, _MW), lambda i, j: (i, j))
    ga = _vspec((ts, _MW), lambda i, j: (i, C_GA // _MW + j))
    gd = _vspec((ts, _MW), lambda i, j: (i, C_GD // _MW + j))

    def body(ga_ref, gd_ref, ya_ref, yd_ref, o_ref):
        o_ref[...] = _merge_f(ga_ref[...], gd_ref[...], ya_ref[...].astype(F32), yd_ref[...].astype(F32)).astype(BF16)

    return pl.pallas_call(body, name="merge_fwd", grid=(t // ts, d // _MW), in_specs=[ga, gd, blk, blk], out_specs=blk,
                          out_shape=jax.ShapeDtypeStruct((t, d), BF16))(proj, proj, ya, yd)


_ANY = pl.BlockSpec(memory_space=pl.ANY)


def _branch_bwd(dproj, proj, y, dm, col0, name, ts=512):
    t, d = y.shape
    blk = _vspec((ts, _MW), lambda i, j: (i, j))
    gate = _vspec((ts, _MW), lambda i, j: (i, col0 // _MW + j))

    def body(buf_ref, g_ref, y_ref, dm_ref, dg_ref, dy_ref):
        del buf_ref
        _, vjp = jax.vjp(lambda g, yy: jax.nn.sigmoid(g) * yy, g_ref[...], y_ref[...].astype(F32))
        dg, dy = vjp(dm_ref[...].astype(F32))
        dg_ref[...] = dg.astype(BF16)
        dy_ref[...] = dy.astype(BF16)

    return pl.pallas_call(body, name=name, grid=(t // ts, d // _MW), in_specs=[_ANY, gate, blk, blk], out_specs=[gate, blk],
                          out_shape=[jax.ShapeDtypeStruct(dproj.shape, BF16), jax.ShapeDtypeStruct((t, d), BF16)],
                          input_output_aliases={0: 0})(dproj, proj, y, dm)


def _shift_down(x, s):
    if s == 0:
        return x
    r = lax.broadcasted_iota(jnp.int32, x.shape, 0)
    return jnp.where(r >= s, pltpu.roll(x, s, 0), 0.0)


def _shift_up(x, s):
    if s == 0:
        return x
    n = x.shape[0]
    r = lax.broadcasted_iota(jnp.int32, x.shape, 0)
    return jnp.where(r < n - s, pltpu.roll(x, n - s, 0), 0.0)


def _conv_fwd(x, w, k):
    out = None
    for j in range(k):
        term = w[j:j + 1, :] * _shift_down(x, k - 1 - j)
        out = term if out is None else out + term
    return out


def _conv_bwd(x, w, dc, k):
    dx = None
    dws = []
    for j in range(k):
        up = _shift_up(dc, k - 1 - j)
        term = w[j:j + 1, :] * up
        dx = term if dx is None else dx + term
        dws.append(jnp.sum(up * x, axis=0, keepdims=True))
    return dx, jnp.concatenate(dws, axis=0)


def _geglu_f(gate, val):
    return jax.nn.gelu(gate, approximate=True) * val


_FW = 256


def _ffn_act_fwd(up, conv_w, bsz):
    t = up.shape[0]
    s = t // bsz
    nj = D_FF // _FW
    xg = _vspec((s, _FW), lambda b, j: (b, j))
    xv = _vspec((s, _FW), lambda b, j: (b, nj + j))
    wg = _vspec((FFN_CONV, _FW), lambda b, j: (0, j))
    wv = _vspec((FFN_CONV, _FW), lambda b, j: (0, nj + j))

    def body(xg_ref, xv_ref, wg_ref, wv_ref, o_ref, cg_ref, cv_ref):
        gate = _conv_fwd(xg_ref[...].astype(F32), wg_ref[...], FFN_CONV)
        val = _conv_fwd(xv_ref[...].astype(F32), wv_ref[...], FFN_CONV)
        o_ref[...] = _geglu_f(gate, val).astype(BF16)
        cg_ref[...] = gate.astype(BF16)
        cv_ref[...] = val.astype(BF16)

    blk = _vspec((s, _FW), lambda b, j: (b, j))
    o = jax.ShapeDtypeStruct((t, D_FF), BF16)
    return pl.pallas_call(body, name="ffn_act_fwd", grid=(bsz, nj), in_specs=[xg, xv, wg, wv], out_specs=[blk] * 3,
                          out_shape=[o] * 3)(up, up, conv_w, conv_w)


def _ffn_act_bwd(up, conv_w, conv_out, dact, bsz, plug=None):
    t = up.shape[0]
    s = t // bsz
    nj = D_FF // _FW
    xg = _vspec((s, _FW), lambda j, b: (b, j))
    xv = _vspec((s, _FW), lambda j, b: (b, nj + j))
    wg = _vspec((FFN_CONV, _FW), lambda j, b: (0, j))
    wv = _vspec((FFN_CONV, _FW), lambda j, b: (0, nj + j))
    da = _vspec((s, _FW), lambda j, b: (b, j))
    dwo = _vspec((FFN_CONV, _FW), lambda j, b: (0, j))

    def body(xg_ref, xv_ref, wg_ref, wv_ref, cg_ref, cv_ref, da_ref, dxg_ref, dxv_ref, dwg_ref, dwv_ref):
        b = pl.program_id(1)
        xg_, xv_, wg_, wv_ = xg_ref[...].astype(F32), xv_ref[...].astype(F32), wg_ref[...], wv_ref[...]
        _, vjp = jax.vjp(_geglu_f, cg_ref[...].astype(F32), cv_ref[...].astype(F32))
        dgate, dval = vjp(da_ref[...].astype(F32))
        dxg, dwg = _conv_bwd(xg_, wg_, dgate, FFN_CONV)
        dxv, dwv = _conv_bwd(xv_, wv_, dval, FFN_CONV)
        dxg_ref[...] = dxg.astype(BF16)
        dxv_ref[...] = dxv.astype(BF16)

        @pl.when(b == 0)
        def _():
            dwg_ref[...] = jnp.zeros_like(dwg_ref)
            dwv_ref[...] = jnp.zeros_like(dwv_ref)

        dwg_ref[...] += dwg
        dwv_ref[...] += dwv

    outs, extra = _plugged_call(
        body, plug, _grid_ends((nj, bsz)), (up, up, conv_w, conv_w, *conv_out, dact), name="ffn_act_bwd", grid=(nj, bsz),
        in_specs=[xg, xv, wg, wv, da, da, da], out_specs=[da, da, dwo, dwo],
        out_shape=[jax.ShapeDtypeStruct((t, D_FF), BF16), jax.ShapeDtypeStruct((t, D_FF), BF16),
                   jax.ShapeDtypeStruct((FFN_CONV, D_FF), F32), jax.ShapeDtypeStruct((FFN_CONV, D_FF), F32)])
    return (*outs, extra)


def _bucket_table():
    qi = np.arange(WINDOW)[:, None]
    kj = np.arange(2 * WINDOW)[None, :]
    dist = WINDOW + qi - kj
    dc = np.maximum(dist, 0)
    max_exact = REL_BUCKETS // 2
    scaled = np.log(np.maximum(dc, 1).astype(np.float32) / np.float32(max_exact)) / np.float32(math.log(REL_MAX_DIST / max_exact))
    large = max_exact + (scaled.astype(np.float32) * np.float32(REL_BUCKETS - max_exact)).astype(np.int32)
    large = np.minimum(large, REL_BUCKETS - 1)
    bucket = np.where(dc < max_exact, dc, large).astype(np.int32)
    in_band = ((dist >= 0) & (dist < WINDOW)).astype(np.int32)
    return bucket, in_band


def _bias_build(rel_bias):
    bucket, _ = _bucket_table()

    def body(rb_ref, idx_ref, o_ref):
        h = pl.program_id(0)
        idx = idx_ref[...]
        acc = jnp.zeros(idx.shape, F32)
        for r in range(REL_BUCKETS):
            acc = jnp.where(idx == r, rb_ref[r, h], acc)
        o_ref[0] = acc

    return pl.pallas_call(
        body, name="bias_build", grid=(AQ_HEADS,),
        in_specs=[pl.BlockSpec(memory_space=pltpu.SMEM), _vspec((WINDOW, 2 * WINDOW), lambda h: (0, 0))],
        out_specs=_vspec((1, WINDOW, 2 * WINDOW), lambda h: (h, 0, 0)),
        out_shape=jax.ShapeDtypeStruct((AQ_HEADS, WINDOW, 2 * WINDOW), F32),
    )(rel_bias, jnp.asarray(bucket))


def _bias_grad(dbias):
    bucket, _ = _bucket_table()

    def body(db_ref, idx_ref, o_ref):
        idx = idx_ref[...]
        db = db_ref[0]
        lane = lax.broadcasted_iota(jnp.int32, (1, LANES), 1)
        acc = jnp.zeros((1, LANES), F32)
        for r in range(REL_BUCKETS):
            s = jnp.sum(jnp.sum(jnp.where(idx == r, db, 0.0), axis=1, keepdims=True), axis=0, keepdims=True)
            acc = jnp.where(lane == r, s, acc)
        o_ref[0] = acc

    return pl.pallas_call(
        body, name="bias_grad", grid=(AQ_HEADS,),
        in_specs=[_vspec((1, WINDOW, 2 * WINDOW), lambda h: (h, 0, 0)), _vspec((WINDOW, 2 * WINDOW), lambda h: (0, 0))],
        out_specs=_vspec((1, 1, LANES), lambda h: (h, 0, 0)),
        out_shape=jax.ShapeDtypeStruct((AQ_HEADS, 1, LANES), F32),
    )(dbias, jnp.asarray(bucket))


def _attn_mask(n):
    qi = lax.broadcasted_iota(jnp.int32, (WINDOW, 2 * WINDOW), 0)
    kj = lax.broadcasted_iota(jnp.int32, (WINDOW, 2 * WINDOW), 1)
    dist = WINDOW + qi - kj
    band = (dist >= 0) & (dist < WINDOW)
    return band & ((kj >= WINDOW) | (n > 0))


def _attn_probs(qk, bias, sink, mask):
    s = jnp.where(mask, qk * (A_HD ** -0.5) + bias, NEG_INF)
    m = jnp.maximum(jnp.max(s, axis=-1, keepdims=True), sink)
    p = jnp.exp(s - m)
    es = jnp.exp(sink - m)
    inv = 1.0 / (jnp.sum(p, axis=-1, keepdims=True) + es)
    return p * inv, es * inv


def _attn_fwd(proj, bias, sinks, bsz):
    t = proj.shape[0]
    s = t // bsz
    nb = s // WINDOW
    grp = AQ_HEADS // AKV_HEADS

    def body(q_ref, k_ref, v_ref, bias_ref, sink_ref, y_ref, kp_ref, vp_ref):
        kp_ref[0:WINDOW, :] = jnp.zeros((WINDOW, LANES), BF16)
        vp_ref[0:WINDOW, :] = jnp.zeros((WINDOW, LANES), BF16)
        kp_ref[WINDOW:, :] = k_ref[...].astype(BF16)
        vp_ref[WINDOW:, :] = v_ref[...].astype(BF16)

        def blk(n, carry):
            r0 = pl.multiple_of(n * WINDOW, WINDOW)
            mask = _attn_mask(n)
            kband = kp_ref[pl.ds(r0, 2 * WINDOW), :]
            vband = vp_ref[pl.ds(r0, 2 * WINDOW), :]
            qb = q_ref[pl.ds(r0, WINDOW), :].astype(BF16)
            heads = range(AQ_HEADS)
            hsl = lambda h: slice(h * A_HD, (h + 1) * A_HD)
            kbs = [kband[:, hsl(kv)] for kv in range(AKV_HEADS)]
            vbs = [vband[:, hsl(kv)] for kv in range(AKV_HEADS)]
            qks = [lax.dot_general(qb[:, hsl(h)], kbs[h // grp], _NT, preferred_element_type=F32) for h in heads]
            probs = [_attn_probs(qks[h], bias_ref[h], sink_ref[0, h], mask)[0] for h in heads]
            outs = [jnp.dot(probs[h].astype(BF16), vbs[h // grp], preferred_element_type=F32) for h in heads]
            y_ref[pl.ds(r0, WINDOW), :] = jnp.concatenate(outs, axis=1).astype(BF16)
            return carry

        lax.fori_loop(0, nb, blk, 0)

    return pl.pallas_call(
        body, name="attn_fwd", grid=(bsz,),
        in_specs=[_vspec((s, AQ), lambda b: (b, C_Q // AQ)), _vspec((s, AKV), lambda b: (b, C_K // AKV)),
                  _vspec((s, AKV), lambda b: (b, C_V // AKV)),
                  _vspec((AQ_HEADS, WINDOW, 2 * WINDOW), lambda b: (0, 0, 0)), pl.BlockSpec(memory_space=pltpu.SMEM)],
        out_specs=_vspec((s, AQ), lambda b: (b, 0)), out_shape=jax.ShapeDtypeStruct((t, AQ), BF16),
        scratch_shapes=[pltpu.VMEM((s + WINDOW, LANES), BF16), pltpu.VMEM((s + WINDOW, LANES), BF16)],
    )(proj, proj, proj, bias, sinks)


def _attn_bwd(dproj, proj, bias, sinks, dy, bsz, plug=None):
    t = proj.shape[0]
    s = t // bsz
    nb = s // WINDOW
    grp = AQ_HEADS // AKV_HEADS
    scale = A_HD ** -0.5

    def body(q_ref, k_ref, v_ref, bias_ref, sink_ref, dy_ref, dq_ref, dk_ref, dv_ref, dbias_ref, dsink_ref,
             kp_ref, vp_ref, dkp_ref, dvp_ref):
        b = pl.program_id(0)
        kp_ref[0:WINDOW, :] = jnp.zeros((WINDOW, LANES), BF16)
        vp_ref[0:WINDOW, :] = jnp.zeros((WINDOW, LANES), BF16)
        kp_ref[WINDOW:, :] = k_ref[...].astype(BF16)
        vp_ref[WINDOW:, :] = v_ref[...].astype(BF16)
        dkp_ref[...] = jnp.zeros_like(dkp_ref)
        dvp_ref[...] = jnp.zeros_like(dvp_ref)

        @pl.when(b == 0)
        def _():
            dbias_ref[...] = jnp.zeros_like(dbias_ref)
            dsink_ref[...] = jnp.zeros_like(dsink_ref)

        def blk(n, carry):
            r0 = pl.multiple_of(n * WINDOW, WINDOW)
            mask = _attn_mask(n)
            kband = kp_ref[pl.ds(r0, 2 * WINDOW), :]
            vband = vp_ref[pl.ds(r0, 2 * WINDOW), :]
            qb = q_ref[pl.ds(r0, WINDOW), :].astype(BF16)
            dyb = dy_ref[pl.ds(r0, WINDOW), :].astype(BF16)
            heads = range(AQ_HEADS)
            hsl = lambda h: slice(h * A_HD, (h + 1) * A_HD)
            kbs = [kband[:, hsl(kv)] for kv in range(AKV_HEADS)]
            vbs = [vband[:, hsl(kv)] for kv in range(AKV_HEADS)]
            qhs = [qb[:, hsl(h)] for h in heads]
            dyhs = [dyb[:, hsl(h)] for h in heads]
            qks = [lax.dot_general(qhs[h], kbs[h // grp], _NT, preferred_element_type=F32) for h in heads]
            dprobs = [lax.dot_general(dyhs[h], vbs[h // grp], _NT, preferred_element_type=F32) for h in heads]
            pbs, dsbs = [], []
            for h in heads:
                probs, psink = _attn_probs(qks[h], bias_ref[h], sink_ref[0, h], mask)
                rowdot = jnp.sum(probs * dprobs[h], axis=-1, keepdims=True)
                ds = probs * (dprobs[h] - rowdot)
                dbias_ref[h] += ds
                dsink_ref[h] += jnp.sum(-psink * rowdot, axis=0, keepdims=True) + jnp.zeros((1, LANES), F32)
                pbs.append(probs.astype(BF16))
                dsbs.append(ds.astype(BF16))
            dvhs = [lax.dot_general(pbs[h], dyhs[h], _TN, preferred_element_type=F32) for h in heads]
            dqs = [jnp.dot(dsbs[h], kbs[h // grp], preferred_element_type=F32) * scale for h in heads]
            dkhs = [lax.dot_general(dsbs[h], qhs[h], _TN, preferred_element_type=F32) * scale for h in heads]
            dks = [sum(dkhs[kv * grp + 1:(kv + 1) * grp], dkhs[kv * grp]) for kv in range(AKV_HEADS)]
            dvs = [sum(dvhs[kv * grp + 1:(kv + 1) * grp], dvhs[kv * grp]) for kv in range(AKV_HEADS)]
            dq_ref[pl.ds(r0, WINDOW), :] = jnp.concatenate(dqs, axis=1).astype(BF16)
            dkp_ref[pl.ds(r0, 2 * WINDOW), :] += jnp.concatenate(dks, axis=1)
            dvp_ref[pl.ds(r0, 2 * WINDOW), :] += jnp.concatenate(dvs, axis=1)
            return carry

        lax.fori_loop(0, nb, blk, 0)
        dk_ref[...] = dkp_ref[WINDOW:, :].astype(BF16)
        dv_ref[...] = dvp_ref[WINDOW:, :].astype(BF16)

    kvs = jax.ShapeDtypeStruct((t, AKV), BF16)
    outs, extra = _plugged_call(
        lambda buf_ref, *refs: body(*refs), plug, _grid_ends((bsz,)), (dproj, proj, proj, proj, bias, sinks, dy),
        name="attn_bwd", grid=(bsz,),
        in_specs=[_ANY, _vspec((s, AQ), lambda b: (b, C_Q // AQ)), _vspec((s, AKV), lambda b: (b, C_K // AKV)),
                  _vspec((s, AKV), lambda b: (b, C_V // AKV)),
                  _vspec((AQ_HEADS, WINDOW, 2 * WINDOW), lambda b: (0, 0, 0)), pl.BlockSpec(memory_space=pltpu.SMEM),
                  _vspec((s, AQ), lambda b: (b, 0))],
        out_specs=[_vspec((s, AQ), lambda b: (b, C_Q // AQ)), _vspec((s, AKV), lambda b: (b, 0)), _vspec((s, AKV), lambda b: (b, 0)),
                   _vspec((AQ_HEADS, WINDOW, 2 * WINDOW), lambda b: (0, 0, 0)), _vspec((AQ_HEADS, 1, LANES), lambda b: (0, 0, 0))],
        out_shape=[jax.ShapeDtypeStruct(dproj.shape, BF16), kvs, kvs,
                   jax.ShapeDtypeStruct((AQ_HEADS, WINDOW, 2 * WINDOW), F32), jax.ShapeDtypeStruct((AQ_HEADS, 1, LANES), F32)],
        scratch_shapes=[pltpu.VMEM((s + WINDOW, LANES), BF16), pltpu.VMEM((s + WINDOW, LANES), BF16),
                        pltpu.VMEM((s + WINDOW, LANES), F32), pltpu.VMEM((s + WINDOW, LANES), F32)],
        aliases={0: 0})
    return (*outs, extra)


def _dn_act_f(c, is_qk):
    a = jax.nn.silu(c)
    outs = []
    for h in range(DN_HEADS):
        ah = a[:, h * DN_HD:(h + 1) * DN_HD]
        nh = ah * lax.rsqrt(jnp.sum(ah * ah, axis=-1, keepdims=True) + L2_EPS)
        outs.append(jnp.where(is_qk, nh, ah))
    return jnp.concatenate(outs, axis=1)


def _dn_prep_fwd(proj, conv_w, bsz):
    t = proj.shape[0]
    s = t // bsz
    blk = _vspec((s, DN), lambda b, j: (b, j))
    wsp = _vspec((DN_CONV, DN), lambda b, j: (0, j))

    def body(x_ref, w_ref, o_ref, c_ref):
        j = pl.program_id(1)
        c = _conv_fwd(x_ref[...], w_ref[...], DN_CONV)
        o_ref[...] = _dn_act_f(c, j < 2)
        c_ref[...] = c.astype(BF16)

    return pl.pallas_call(body, name="dn_prep_fwd", grid=(bsz, 3), in_specs=[blk, wsp], out_specs=[blk, blk],
                          out_shape=[jax.ShapeDtypeStruct((t, 3 * DN), F32), jax.ShapeDtypeStruct((t, 3 * DN), BF16)])(proj, conv_w)


def _dn_prep_bwd(dproj, proj, conv_w, conv_out, dqkvn, bsz):
    t = proj.shape[0]
    s = t // bsz
    blk = _vspec((s, DN), lambda j, b: (b, j))
    wsp = _vspec((DN_CONV, DN), lambda j, b: (0, j))

    def body(buf_ref, x_ref, w_ref, c_ref, d_ref, dx_ref, dw_ref):
        del buf_ref
        j, b = pl.program_id(0), pl.program_id(1)
        x, w = x_ref[...], w_ref[...]
        _, vjp = jax.vjp(lambda cc: _dn_act_f(cc, j < 2), c_ref[...].astype(F32))
        (dc,) = vjp(d_ref[0])
        dx, dw = _conv_bwd(x, w, dc, DN_CONV)
        dx_ref[...] = dx.astype(BF16)

        @pl.when(b == 0)
        def _():
            dw_ref[...] = jnp.zeros_like(dw_ref)

        dw_ref[...] += dw

    return pl.pallas_call(
        body, name="dn_prep_bwd", grid=(3, bsz),
        in_specs=[_ANY, blk, wsp, blk, _vspec((1, s, DN), lambda j, b: (j, b, 0))], out_specs=[blk, wsp],
        out_shape=[jax.ShapeDtypeStruct(dproj.shape, BF16), jax.ShapeDtypeStruct((DN_CONV, 3 * DN), F32)],
        input_output_aliases={0: 0},
    )(dproj, proj, conv_w, conv_out, dqkvn)


def _bg_f(x, alog, dt):
    lane = lax.broadcasted_iota(jnp.int32, x.shape, 1)
    beta = jax.nn.sigmoid(x)
    g = -jnp.exp(alog) * jax.nn.softplus(x + dt)
    return jnp.where(lane < DN_HEADS, beta, jnp.where(lane < 2 * DN_HEADS, g, 0.0))


def _bg_fwd(proj, alog, dt, bsz):
    t = proj.shape[0]
    s = t // bsz
    vec = _vspec((1, LANES), lambda b: (0, 0))

    def body(x_ref, a_ref, d_ref, o_ref):
        o_ref[...] = _bg_f(x_ref[...], a_ref[...], d_ref[...])

    return pl.pallas_call(body, name="bg_fwd", grid=(bsz,), in_specs=[_vspec((s, LANES), lambda b: (b, C_BD // LANES)), vec, vec],
                          out_specs=_vspec((s, LANES), lambda b: (b, 0)), out_shape=jax.ShapeDtypeStruct((t, LANES), F32))(proj, alog, dt)


def _bg_bwd(dproj, proj, alog, dt, dbg4, dk, dv, bsz):
    t = proj.shape[0]
    s = t // bsz
    vec = _vspec((1, LANES), lambda b: (0, 0))
    kv = _vspec((s, AKV), lambda b: (b, 0))
    tail = 3 * LANES

    def body(buf_ref, x_ref, a_ref, d_ref, g4_ref, dk_ref, dv_ref, dx_ref, da_ref, dd_ref):
        del buf_ref
        b = pl.program_id(0)
        lane = lax.broadcasted_iota(jnp.int32, (s, LANES), 1)
        dbg = jnp.zeros((s, LANES), F32)
        for h in range(DN_HEADS):
            gh = g4_ref[:, h * DN_HD:(h + 1) * DN_HD]
            dbg = jnp.where(lane == h, gh[:, 0:1], dbg)
            dbg = jnp.where(lane == DN_HEADS + h, gh[:, 1:2], dbg)
        _, vjp = jax.vjp(_bg_f, x_ref[...], a_ref[...], d_ref[...])
        dx, da, dd = vjp(dbg)
        dx_ref[...] = jnp.concatenate([dk_ref[...], dv_ref[...], dx.astype(BF16)], axis=1)

        @pl.when(b == 0)
        def _():
            da_ref[...] = jnp.zeros_like(da_ref)
            dd_ref[...] = jnp.zeros_like(dd_ref)

        da_ref[...] += da
        dd_ref[...] += dd

    return pl.pallas_call(
        body, name="bg_bwd", grid=(bsz,),
        in_specs=[_ANY, _vspec((s, LANES), lambda b: (b, C_BD // LANES)), vec, vec, _vspec((s, DN), lambda b: (b, 0)), kv, kv],
        out_specs=[_vspec((s, tail), lambda b: (b, C_K // tail)), vec, vec],
        out_shape=[jax.ShapeDtypeStruct(dproj.shape, BF16), jax.ShapeDtypeStruct((1, LANES), F32), jax.ShapeDtypeStruct((1, LANES), F32)],
        input_output_aliases={0: 0},
    )(dproj, proj, alog, dt, dbg4, dk, dv)


def _dn_out_f(o, z, w):
    outs = []
    for h in range(DN_HEADS):
        sl = slice(h * DN_HD, (h + 1) * DN_HD)
        outs.append(_rms(o[:, sl], w) * jax.nn.silu(z[:, sl]))
    return jnp.concatenate(outs, axis=1)


def _dn_out_fwd(o, proj, w, ts=512):
    t = o.shape[0]
    blk = _vspec((ts, DN), lambda i: (i, 0))
    zsp = _vspec((ts, DN), lambda i: (i, C_DZ // DN))
    vec = _vspec((1, DN_HD), lambda i: (0, 0))

    def body(o_ref, z_ref, w_ref, y_ref):
        y_ref[...] = _dn_out_f(o_ref[...], z_ref[...], w_ref[...]).astype(BF16)

    return pl.pallas_call(body, name="dn_out_fwd", grid=(t // ts,), in_specs=[blk, zsp, vec], out_specs=blk,
                          out_shape=jax.ShapeDtypeStruct((t, DN), BF16))(o, proj, w)


def _dn_out_bwd(dproj, o, proj, w, dy, ts=512):
    t = o.shape[0]
    blk = _vspec((ts, DN), lambda i: (i, 0))
    zsp = _vspec((ts, DN), lambda i: (i, C_DZ // DN))
    vec = _vspec((1, DN_HD), lambda i: (0, 0))

    def body(buf_ref, o_ref, z_ref, w_ref, dy_ref, do_ref, dz_ref, dw_ref):
        del buf_ref
        i = pl.program_id(0)
        _, vjp = jax.vjp(_dn_out_f, o_ref[...], z_ref[...], w_ref[...])
        do, dz, dw = vjp(dy_ref[...].astype(F32))
        do_ref[...] = do
        dz_ref[...] = dz.astype(BF16)

        @pl.when(i == 0)
        def _():
            dw_ref[...] = jnp.zeros_like(dw_ref)

        dw_ref[...] += dw

    return pl.pallas_call(
        body, name="dn_out_bwd", grid=(t // ts,), in_specs=[_ANY, blk, zsp, vec, blk], out_specs=[blk, zsp, vec],
        out_shape=[jax.ShapeDtypeStruct((t, DN), F32), jax.ShapeDtypeStruct(dproj.shape, BF16), jax.ShapeDtypeStruct((1, DN_HD), F32)],
        input_output_aliases={0: 1},
    )(dproj, o, proj, w, dy)


_C = DN_CHUNK


def _dot(a, b, dims):
    return lax.dot_general(a.astype(BF16), b.astype(BF16), dims, preferred_element_type=F32)


def _split(a):
    hi = a.astype(BF16)
    return hi, (a - hi.astype(F32)).astype(BF16)


def _dot3(a, b, dims):
    (ah, al), (bh, bl) = (a if isinstance(a, tuple) else _split(a)), (b if isinstance(b, tuple) else _split(b))
    mm = lambda x, y: lax.dot_general(x, y, dims, preferred_element_type=F32)
    return mm(ah, bh) + (mm(ah, bl) + mm(al, bh))


_NN = (((1,), (0,)), ((), ()))
_NT = (((1,), (1,)), ((), ()))
_TN = (((0,), (0,)), ((), ()))


_SUB = 8


def _tri_inverses(ls, lts):
    ri8 = lax.broadcasted_iota(jnp.int32, (_SUB, _C), 0)
    ci8 = lax.broadcasted_iota(jnp.int32, (_SUB, _C), 1)
    nblk = _C // _SUB
    ts = []
    for lt in lts:
        blocks = [jnp.where(ci8 == ri8 + _SUB * b, 1.0, 0.0).astype(F32) for b in range(nblk)]
        for r in range(1, _SUB):
            for b in range(nblk):
                coef = lt[_SUB * b:_SUB * (b + 1), _SUB * b + r:_SUB * b + r + 1]
                row = jnp.sum(coef * blocks[b], axis=0, keepdims=True)
                blocks[b] = jnp.where(ri8 == r, blocks[b] - row, blocks[b])
        ts.append(jnp.concatenate(blocks, axis=0))
    ri = lax.broadcasted_iota(jnp.int32, (_C, _C), 0)
    ci = lax.broadcasted_iota(jnp.int32, (_C, _C), 1)
    s = _SUB
    while s < _C:
        shift = s.bit_length()
        quad = ((ri >> shift) == (ci >> shift)) & ((ri & s) != 0) & ((ci & s) == 0)
        offs = [jnp.where(quad, l, 0.0) for l in ls]
        tsp = [_split(t) for t in ts]
        left = [_dot3(tp, off, _NN) for tp, off in zip(tsp, offs)]
        ts = [t - _dot3(lo, tp, _NN) for t, lo, tp in zip(ts, left, tsp)]
        s *= 2
    return ts


_SEG = 512
_HEADS = tuple(range(DN_HEADS))


def _hsl(hh):
    return slice(hh * DN_HD, (hh + 1) * DN_HD)


def _chunk_specs(bsz, nseg, reverse):
    seg = (lambda i: nseg - 1 - i) if reverse else (lambda i: i)
    ncs = _SEG // _C
    col = lambda off: _vspec((bsz, _SEG, DN), lambda i: (0, seg(i), off))
    return (col, _vspec((bsz, _SEG, LANES), lambda i: (0, seg(i), 0)),
            _vspec((bsz, DN_HEADS, ncs, _C), lambda i: (0, 0, seg(i), 0)),
            _vspec((bsz, DN_HEADS, ncs, DN_HD, DN_HD), lambda i: (0, 0, seg(i), 0, 0)),
            _vspec((bsz, DN_HEADS, ncs, _C, _C), lambda i: (0, 0, seg(i), 0, 0)))


def _chunk_pre(q_ref, k_ref, v_ref, bg_ref, gr_ref, c, bb, hh):
    r0 = pl.multiple_of(c * _C, _C)
    ri = lax.broadcasted_iota(jnp.int32, (_C, _C), 0)
    ci = lax.broadcasted_iota(jnp.int32, (_C, _C), 1)
    q = q_ref[bb, pl.ds(r0, _C), _hsl(hh)] * (DN_HD ** -0.5)
    k = k_ref[bb, pl.ds(r0, _C), _hsl(hh)]
    v = v_ref[bb, pl.ds(r0, _C), _hsl(hh)]
    bgc = bg_ref[bb, pl.ds(r0, _C), :]
    beta = bgc[:, hh:hh + 1]
    g_col = bgc[:, DN_HEADS + hh:DN_HEADS + hh + 1]
    g_row = gr_ref[bb, hh, pl.ds(c, 1), :]
    gc_col = jnp.sum(jnp.where(ri >= ci, g_row, 0.0), axis=1, keepdims=True)
    gc_row = jnp.sum(jnp.where(ri <= ci, g_col, 0.0), axis=0, keepdims=True)
    gc_last = jnp.sum(g_col, axis=0, keepdims=True)
    diff = gc_col - gc_row
    decay = jnp.where(ri >= ci, jnp.exp(jnp.where(ri >= ci, diff, 0.0)), 0.0)
    diff_t = gc_row - gc_col
    decay_t = jnp.where(ri <= ci, jnp.exp(jnp.where(ri <= ci, diff_t, 0.0)), 0.0)
    eg = jnp.exp(gc_col)
    et = jnp.exp(gc_last - gc_col)
    gl = jnp.exp(gc_last)
    kb = k * beta
    vb = v * beta
    return dict(r0=r0, bb=bb, hh=hh, q=q, k=k, v=v, beta=beta, decay=decay, decay_t=decay_t, eg=eg, et=et, gl=gl, kb=kb, vb=vb,
                ri=ri, ci=ci)


def _chunk_solve(ms, tms=None):
    for m in ms:
        m["kk_t"] = _dot(m["k"], m["kb"], _NT)
        m["qk"] = _dot(m["q"], m["k"], _NT)
        m["kk"] = _dot(m["kb"], m["k"], _NT)
        if tms is not None:
            m["qk_t"] = _dot(m["k"], m["q"], _NT)
    if tms is None:
        tms = _tri_inverses([jnp.where(m["ri"] > m["ci"], m["kk"] * m["decay"], 0.0) for m in ms],
                            [jnp.where(m["ri"] < m["ci"], m["kk_t"] * m["decay_t"], 0.0) for m in ms])
    for m, tm in zip(ms, tms):
        m["tm_f32"] = tm
    for m in ms:
        rhs = jnp.concatenate([m["vb"], m["kb"] * m["eg"]], axis=1)
        m["tm"] = _split(m["tm_f32"])
        m["sol"] = _dot3(m["tm"], rhs, _NN)
        m["intra"] = jnp.where(m["ri"] >= m["ci"], m["qk"] * m["decay"], 0.0)


def _dn_chunk_fwd(qkvn, bg, g_rows, bsz, plug=None):
    t = qkvn.shape[0]
    s = t // bsz
    nc, nseg = s // _C, s // _SEG
    pairs = [(bb, hh) for bb in range(bsz) for hh in _HEADS]

    def body(q_ref, k_ref, v_ref, bg_ref, gr_ref, o_ref, st_ref, tm_ref, s_ref):
        @pl.when(pl.program_id(0) == 0)
        def _():
            s_ref[...] = jnp.zeros_like(s_ref)

        def chunk(c, carry):
            ms = [_chunk_pre(q_ref, k_ref, v_ref, bg_ref, gr_ref, c, bb, hh) for bb, hh in pairs]
            _chunk_solve(ms)
            sts = [s_ref[i] for i in range(len(pairs))]
            for m, st in zip(ms, sts):
                st_ref[m["bb"], m["hh"], c] = st
                tm_ref[m["bb"], m["hh"], c] = m["tm_f32"]
            ws = [_dot(m["sol"][:, DN_HD:], st, _NN) for m, st in zip(ms, sts)]
            qs = [_dot(m["q"] * m["eg"], st, _NN) for m, st in zip(ms, sts)]
            v_new = [m["sol"][:, :DN_HD] - a for m, a in zip(ms, ws)]
            iv = [_dot(m["intra"], vn, _NN) for m, vn in zip(ms, v_new)]
            upd = [_dot(m["k"] * m["et"], vn, _TN) for m, vn in zip(ms, v_new)]
            for i, (bb, hh) in enumerate(pairs):
                s_ref[i] = sts[i] * ms[i]["gl"] + upd[i]
                o_ref[bb, pl.ds(ms[i]["r0"], _C), _hsl(hh)] = qs[i] + iv[i]
            return carry

        lax.fori_loop(0, _SEG // _C, chunk, 0)

    col, bgs, grs, sts_spec, tms_spec = _chunk_specs(bsz, nseg, False)
    q3, bg3 = qkvn.reshape(bsz, s, 3 * DN), bg.reshape(bsz, s, LANES)
    (o, states, tms), extra = _plugged_call(
        body, plug, _grid_ends((nseg,)), (q3, q3, q3, bg3, g_rows), name="dn_chunk_fwd", grid=(nseg,),
        in_specs=[col(0), col(1), col(2), bgs, grs], out_specs=[col(0), sts_spec, tms_spec],
        out_shape=[jax.ShapeDtypeStruct((bsz, s, DN), F32), jax.ShapeDtypeStruct((bsz, DN_HEADS, nc, DN_HD, DN_HD), F32),
                   jax.ShapeDtypeStruct((bsz, DN_HEADS, nc, _C, _C), F32)],
        scratch_shapes=[pltpu.VMEM((bsz * DN_HEADS, DN_HD, DN_HD), F32)])
    return o.reshape(t, DN), (states, tms), extra


def _dn_chunk_bwd(qkvn, bg, g_rows, states, do, bsz, plug=None):
    t = qkvn.shape[0]
    s = t // bsz
    nc, nseg = s // _C, s // _SEG
    pairs = [(bb, hh) for bb in range(bsz) for hh in _HEADS]

    def body(q_ref, k_ref, v_ref, bg_ref, gr_ref, st_ref, tm_ref, do_ref, dqkv_ref, dbg_ref, ds_ref):
        @pl.when(pl.program_id(0) == 0)
        def _():
            ds_ref[...] = jnp.zeros_like(ds_ref)

        def chunk(cc, carry):
            c = _SEG // _C - 1 - cc
            ms = [_chunk_pre(q_ref, k_ref, v_ref, bg_ref, gr_ref, c, bb, hh) for bb, hh in pairs]
            _chunk_solve(ms, [tm_ref[bb, hh, c] for bb, hh in pairs])
            ri, ci = ms[0]["ri"], ms[0]["ci"]
            for i, m in enumerate(ms):
                m["st"] = st_ref[m["bb"], m["hh"], c]
                m["ds_out"] = ds_ref[i]
                m["do"] = do_ref[m["bb"], pl.ds(m["r0"], _C), _hsl(m["hh"])]
                m["w"] = m["sol"][:, DN_HD:]
            for m in ms:
                m["v_new"] = m["sol"][:, :DN_HD] - _dot(m["w"], m["st"], _NN)
            for m in ms:
                m["q_dec"], m["k_tail"] = m["q"] * m["eg"], m["k"] * m["et"]
                m["dk_tail"] = _dot(m["v_new"], m["ds_out"], _NT)
                m["dv_new"] = _dot(m["k_tail"], m["ds_out"], _NN) + _dot(m["intra"], m["do"], _TN)
                m["dq_dec"] = _dot(m["do"], m["st"], _NT)
                m["ds_in"] = m["ds_out"] * m["gl"] + _dot(m["q_dec"], m["do"], _TN)
                m["dintra"] = jnp.where(ri >= ci, _dot(m["do"], m["v_new"], _NT), 0.0)
                m["dintra_t"] = jnp.where(ri <= ci, _dot(m["v_new"], m["do"], _NT), 0.0)
            for m in ms:
                m["dw"] = -_dot(m["dv_new"], m["st"], _NT)
                m["ds_in"] = m["ds_in"] - _dot(m["w"], m["dv_new"], _TN)
            for m in ms:
                dsol = jnp.concatenate([m["dv_new"], m["dw"]], axis=1)
                m["drhs"] = _dot3(m["tm"], dsol, _TN)
            for m in ms:
                m["dl"] = jnp.where(ri > ci, -_dot(m["drhs"], m["sol"], _NT), 0.0)
                m["dl_t"] = jnp.where(ri < ci, -_dot(m["sol"], m["drhs"], _NT), 0.0)
            for m in ms:
                m["dkb2"] = _dot(m["dl"] * m["decay"], m["k"], _NN)
                m["dk"] = _dot(m["dl_t"] * m["decay_t"], m["kb"], _NN) + _dot(m["dintra_t"] * m["decay_t"], m["q"], _NN)
                m["dq"] = _dot(m["dintra"] * m["decay"], m["k"], _NN)
            for m in ms:
                _chunk_bwd_finish(m)
            for m in ms:
                ones_ge = jnp.where(ri <= ci, 1.0, 0.0).astype(BF16)
                gh, gl_ = _split(m["dgc"] + jnp.zeros((_C, LANES), F32))
                m["dg_b"] = jnp.dot(ones_ge, gh, preferred_element_type=F32) + jnp.dot(ones_ge, gl_, preferred_element_type=F32)
            lane = lax.broadcasted_iota(jnp.int32, (_C, LANES), 1)
            for i, m in enumerate(ms):
                bb, hh, rows = m["bb"], m["hh"], pl.ds(m["r0"], _C)
                dqkv_ref[0, bb, rows, _hsl(hh)] = m["dq"] * (DN_HD ** -0.5)
                dqkv_ref[1, bb, rows, _hsl(hh)] = m["dk"]
                dqkv_ref[2, bb, rows, _hsl(hh)] = m["dv"]
                dbg_ref[bb, rows, _hsl(hh)] = jnp.where(lane == 0, m["dbeta"], jnp.where(lane == 1, m["dg_b"], 0.0))
                ds_ref[i] = m["ds_in"]
            return carry

        lax.fori_loop(0, _SEG // _C, chunk, 0, unroll=2)

    col, bgs, grs, sts_spec, tms_spec = _chunk_specs(bsz, nseg, True)
    q3, bg3, do3 = qkvn.reshape(bsz, s, 3 * DN), bg.reshape(bsz, s, LANES), do.reshape(bsz, s, DN)
    (dqkv, dbg), extra = _plugged_call(
        body, plug, _grid_ends((nseg,)), (q3, q3, q3, bg3, g_rows, *states, do3), name="dn_chunk_bwd", grid=(nseg,),
        in_specs=[col(0), col(1), col(2), bgs, grs, sts_spec, tms_spec, col(0)],
        out_specs=[_vspec((3, bsz, _SEG, DN), lambda i: (0, 0, nseg - 1 - i, 0)), col(0)],
        out_shape=[jax.ShapeDtypeStruct((3, bsz, s, DN), F32), jax.ShapeDtypeStruct((bsz, s, DN), F32)],
        scratch_shapes=[pltpu.VMEM((bsz * DN_HEADS, DN_HD, DN_HD), F32)])
    return dqkv.reshape(3, t, DN), dbg.reshape(t, DN), extra


def _chunk_bwd_finish(m):
    q, k, v, beta, decay, decay_t = m["q"], m["k"], m["v"], m["beta"], m["decay"], m["decay_t"]
    eg, et, gl, kb, dl, dl_t, dintra, dintra_t = m["eg"], m["et"], m["gl"], m["kb"], m["dl"], m["dl_t"], m["dintra"], m["dintra_t"]
    dq_dec, dk_tail, dq, dk = m["dq_dec"], m["dk_tail"], m["dq"], m["dk"]
    dgl = jnp.sum(jnp.sum(m["ds_out"] * m["st"], axis=1, keepdims=True), axis=0, keepdims=True)
    dvb, dkbeg = m["drhs"][:, :DN_HD], m["drhs"][:, DN_HD:]
    dkb = dkbeg * eg + m["dkb2"]
    deg = jnp.sum(dkbeg * kb, axis=1, keepdims=True)
    em = (dl * m["kk"] + dintra * m["qk"]) * decay
    em_t = (dl_t * m["kk_t"] + dintra_t * m["qk_t"]) * decay_t
    dgc = jnp.sum(em, axis=1, keepdims=True) - jnp.sum(em_t, axis=1, keepdims=True)
    dq = dq + dq_dec * eg
    deg = deg + jnp.sum(dq_dec * q, axis=1, keepdims=True)
    dk = dk + dk_tail * et
    det = jnp.sum(dk_tail * k, axis=1, keepdims=True)
    dgc = dgc + deg * eg - det * et
    dgc_last = jnp.sum(det * et, axis=0, keepdims=True) + dgl * gl
    rcol = lax.broadcasted_iota(jnp.int32, (_C, 1), 0)
    m["dgc"] = dgc + jnp.where(rcol == _C - 1, dgc_last, 0.0)
    m["dq"] = dq
    m["dk"] = dk + dkb * beta
    m["dbeta"] = jnp.sum(dkb * k, axis=1, keepdims=True) + jnp.sum(dvb * v, axis=1, keepdims=True)
    m["dv"] = dvb * beta


def _mod_fwd(c_all, ada_w_loc, ada_b_loc):
    n, cols = c_all.shape[0], ada_w_loc.shape[1]

    def body(c_ref, w_ref, b_ref, o_ref):
        o_ref[...] = _dot(jax.nn.silu(c_ref[...]), w_ref[...], _NN) + b_ref[...]

    return pl.pallas_call(body, name="mod_fwd", out_shape=jax.ShapeDtypeStruct((n, cols), F32))(c_all, ada_w_loc, ada_b_loc)


def _ada_grad(c_all, dmod_loc, dmod_all):
    d, cols = c_all.shape[1], dmod_loc.shape[1]

    def body(c_ref, dl_ref, da_ref, gw_ref, gb_ref):
        gw_ref[...] = _dot(jax.nn.silu(c_ref[...]), dl_ref[...], _TN)
        gb_ref[...] = jnp.sum(da_ref[...], axis=0, keepdims=True)

    return pl.pallas_call(body, name="ada_grad", out_shape=[jax.ShapeDtypeStruct((d, cols), F32),
                                                           jax.ShapeDtypeStruct((1, dmod_all.shape[1]), F32)])(c_all, dmod_loc, dmod_all)


ELEMENTWISE_BLOCK_BYTES = 3 * 2 ** 19


def _row_tile(r, c=1024):
    fits = [tr for tr in range(16, r + 1, 16) if tr * c * 4 <= ELEMENTWISE_BLOCK_BYTES]
    if not fits:
        return r
    whole = [tr for tr in fits if r % tr == 0]
    return whole[-1] if whole else fits[-1]


def _adamw(w, m, v, grads, name, plug=None):
    r, rest = w.shape[0], w.shape[1:]
    c = math.prod(rest)
    tr = _row_tile(r, c)
    blk = _vspec((tr,) + rest, lambda i: (i,) + (0,) * len(rest))
    n = len(grads)

    def body(*refs):
        w_ref, m_ref, v_ref = refs[:3]
        g_ref, d_ref, mo_ref, vo_ref = refs[3 + n:]
        g = refs[3][...]
        for p in refs[4:3 + n]:
            g = g + p[...]
        g_ref[...] = g
        d_ref[...], mo_ref[...], vo_ref[...] = _adamw_math(w_ref[...], m_ref[...], v_ref[...], g)

    o = jax.ShapeDtypeStruct(w.shape, F32)
    grid = (pl.cdiv(r, tr),)
    res, extra = _plugged_call(body, plug, _grid_ends(grid), (w, m, v, *grads), name=name, grid=grid, in_specs=[blk] * (3 + n),
                               out_specs=[blk] * 4, out_shape=[o] * 4)
    return res if plug is None else (res, extra)


def _adamw_halves(w, m, v, own, other, ic, name):
    if w.ndim == 3:
        r, _, c = w.shape
        tr = _row_tile(r, c)
        blk = _vspec((tr, 1, c // 2), lambda i, j: (i, 0, j))
        half = _vspec((tr, 1, c // 2), lambda i, j: (i, 0, 0))
        grid = (pl.cdiv(r, tr), 2)
    else:
        r, c = w.shape
        tr = _row_tile(r // 2, c)
        per = r // 2 // tr
        blk = _vspec((tr, c), lambda i, j: (j * per + i, 0))
        half = _vspec((tr, c), lambda i, j: (i, 0))
        grid = (per, 2)

    def body(ic_ref, w_ref, m_ref, v_ref, own_ref, other_ref, g_ref, d_ref, mo_ref, vo_ref):
        g = jnp.where(pl.program_id(1) == ic_ref[0], own_ref[...], other_ref[...])
        g_ref[...] = g
        d_ref[...], mo_ref[...], vo_ref[...] = _adamw_math(w_ref[...], m_ref[...], v_ref[...], g)

    o = jax.ShapeDtypeStruct(w.shape, F32)
    return pl.pallas_call(body, name=name, grid=grid,
                          in_specs=[pl.BlockSpec(memory_space=pltpu.SMEM), blk, blk, blk, half, half], out_specs=[blk] * 4,
                          out_shape=[o] * 4)(ic, w, m, v, own, other)


def _adamw_math(w, m, v, g):
    m_new = ADAM_B1 * m + (1.0 - ADAM_B1) * g
    v_new = ADAM_B2 * v + (1.0 - ADAM_B2) * jnp.square(g)
    m_hat = m_new / (1.0 - ADAM_B1 ** ADAM_STEP)
    v_hat = v_new / (1.0 - ADAM_B2 ** ADAM_STEP)
    return -ADAM_LR * (m_hat / (jnp.sqrt(v_hat) + ADAM_EPS) + ADAM_WD * w), m_new, v_new


def _adamw_small(ws, ms, vs, gs, name):
    n = len(ws)

    def body(*refs):
        for i in range(n):
            w_ref, m_ref, v_ref, g_ref = (refs[j * n + i] for j in range(4))
            go_ref, d_ref, mo_ref, vo_ref = refs[4 * n + 4 * i:4 * n + 4 * i + 4]
            g = g_ref[...]
            go_ref[...] = g
            d_ref[...], mo_ref[...], vo_ref[...] = _adamw_math(w_ref[...], m_ref[...], v_ref[...], g)

    res = pl.pallas_call(body, name=name, out_shape=[jax.ShapeDtypeStruct(a.shape, F32) for a in ws for _ in range(4)])(
        *ws, *ms, *vs, *gs)
    return [res[4 * i:4 * i + 4] for i in range(n)]


def _sum_lead(x, name, rows_apart=False):
    p, r, c = x.shape
    tr = _row_tile(r, c)
    mid = (1,) if rows_apart else ()

    def body(x_ref, o_ref):
        acc = x_ref[0].astype(F32)
        for i in range(1, p):
            acc = acc + x_ref[i].astype(F32)
        o_ref[...] = acc.reshape(o_ref.shape)

    return pl.pallas_call(body, name=name, grid=(pl.cdiv(r, tr),), in_specs=[_vspec((p, tr, c), lambda i: (0, i, 0))],
                          out_specs=_vspec((tr,) + mid + (c,), lambda i: (i,) + (0,) * (1 + len(mid))),
                          out_shape=jax.ShapeDtypeStruct((r,) + mid + (c,), F32))(x)


def _allgather8(x_shard, name, plug=None):
    m_per, n = x_shard.shape

    def body(x_ref, out_ref, send_sems, recv_sems, local_sem):
        x, y, c = lax.axis_index("x"), lax.axis_index("y"), lax.axis_index("c")
        me, sibling = (x, y, c), (x, y, 1 - c)
        chips = [(1 - x, y), (x, 1 - y), (1 - x, 1 - y)]

        def rows(px, py, pc):
            return out_ref.at[pl.ds((4 * px + 2 * py + pc) * m_per, m_per), :]

        def copy(k, block, to, src=None):
            return pltpu.make_async_remote_copy(
                src_ref=rows(*block) if src is None else src, dst_ref=rows(*block), send_sem=send_sems.at[k],
                recv_sem=recv_sems.at[k], device_id=to, device_id_type=MESH)

        mine = pltpu.make_async_copy(x_ref, rows(*me), local_sem)
        mine.start()
        first = [copy(0, me, sibling, src=x_ref)]
        first += [copy(1 + j, me, (*chip, c), src=x_ref) for j, chip in enumerate(chips)]
        for cp in first:
            cp.start()
        passed = [copy(4 + j, (*chip, c), sibling) for j, chip in enumerate(chips)]
        for j, chip in enumerate(chips):
            copy(1 + j, (*chip, c), me).wait_recv()
            passed[j].start()
        copy(0, sibling, me).wait_recv()
        for j, chip in enumerate(chips):
            copy(4 + j, (*chip, 1 - c), me).wait_recv()
        for cp in first + passed:
            cp.wait_send()
        mine.wait()

    grid = (1,)
    (out,), extra = _plugged_call(
        body, plug, _grid_ends(grid), (x_shard,), name=name, grid=grid, out_shape=[jax.ShapeDtypeStruct((8 * m_per, n), x_shard.dtype)],
        in_specs=[pl.BlockSpec(memory_space=pltpu.VMEM)], out_specs=[pl.BlockSpec(memory_space=pltpu.VMEM)],
        scratch_shapes=[pltpu.SemaphoreType.DMA((7,)), pltpu.SemaphoreType.DMA((7,)), pltpu.SemaphoreType.DMA])
    return out if plug is None else (out, extra)


_HBM = pl.BlockSpec(memory_space=pltpu.HBM)


def _mesh_place():
    x, y, c = lax.axis_index("x"), lax.axis_index("y"), lax.axis_index("c")
    return x, y, c, 2 * x + y, [(1 - x, y), (x, 1 - y), (1 - x, 1 - y)]


def _gather_plug(shards):
    n = len(shards)

    def half(ref, c, lead=None):
        r, cols = ref.shape[-2] // 2, ref.shape[-1] // 2
        if r % 16 == 0:
            rows = pl.ds(pl.multiple_of(c * r, 16), r)
            return ref.at[rows, :] if lead is None else ref.at[lead, rows, :]
        lanes = pl.ds(pl.multiple_of(c * cols, LANES), cols)
        return ref.at[:, lanes] if lead is None else ref.at[lead, :, lanes]

    def copies(ins, outs, send, recv):
        x, y, c, me, chips = _mesh_place()
        ici, fwd, fwd_in = [], [], []
        for i in range(n):
            for j, (px, py) in enumerate(chips):
                q = 2 * px + py
                ici.append((pltpu.make_async_remote_copy(
                    src_ref=half(ins[i], c), dst_ref=half(outs[i], c, me), send_sem=send.at[6 * i + j], recv_sem=recv.at[6 * i + j],
                    device_id=(px, py, c), device_id_type=MESH),
                    pltpu.make_async_remote_copy(
                    src_ref=half(ins[i], c), dst_ref=half(outs[i], c, q), send_sem=send.at[6 * i + j], recv_sem=recv.at[6 * i + j],
                    device_id=(px, py, c), device_id_type=MESH)))
                fwd.append(pltpu.make_async_remote_copy(
                    src_ref=half(outs[i], c, q), dst_ref=half(outs[i], c, q), send_sem=send.at[6 * i + 3 + j],
                    recv_sem=recv.at[6 * i + 3 + j], device_id=(x, y, 1 - c), device_id_type=MESH))
                fwd_in.append(pltpu.make_async_remote_copy(
                    src_ref=half(outs[i], 1 - c, q), dst_ref=half(outs[i], 1 - c, q), send_sem=send.at[6 * i + 3 + j],
                    recv_sem=recv.at[6 * i + 3 + j], device_id=(x, y, 1 - c), device_id_type=MESH))
        return ici, fwd, fwd_in, me

    def start(ins, outs, send, recv, loc):
        ici, _, _, me = copies(ins, outs, send, recv)
        for i in range(n):
            pltpu.make_async_copy(ins[i], outs[i].at[me], loc.at[i]).start()
        for out_cp, _ in ici:
            out_cp.start()

    def finish(ins, outs, send, recv, loc):
        ici, fwd, fwd_in, me = copies(ins, outs, send, recv)
        for (_, in_cp), f in zip(ici, fwd):
            in_cp.wait_recv()
            f.start()
        for f in fwd_in:
            f.wait_recv()
        for (out_cp, _), f in zip(ici, fwd):
            out_cp.wait_send()
            f.wait_send()
        for i in range(n):
            pltpu.make_async_copy(ins[i], outs[i].at[me], loc.at[i]).wait()

    return dict(ins=list(shards), out_shape=[jax.ShapeDtypeStruct((N_CHIPS,) + a.shape, a.dtype) for a in shards],
                scratch=[pltpu.SemaphoreType.DMA((6 * n,)), pltpu.SemaphoreType.DMA((6 * n,)), pltpu.SemaphoreType.DMA((n,))],
                start=start, finish=finish)


def _exchange_plug(pieces):
    n = len(pieces)

    def copies(ins, outs, send, recv):
        x, y, c, me, chips = _mesh_place()
        out_cps, in_cps = [], []
        for i in range(n):
            for j, (px, py) in enumerate(chips):
                q = 2 * px + py
                out_cps.append(pltpu.make_async_remote_copy(src_ref=ins[i].at[q], dst_ref=outs[i].at[me], send_sem=send.at[3 * i + j],
                                                            recv_sem=recv.at[3 * i + j], device_id=(px, py, c), device_id_type=MESH))
                in_cps.append(pltpu.make_async_remote_copy(src_ref=ins[i].at[me], dst_ref=outs[i].at[q], send_sem=send.at[3 * i + j],
                                                           recv_sem=recv.at[3 * i + j], device_id=(px, py, c), device_id_type=MESH))
        return out_cps, in_cps, me

    def start(ins, outs, send, recv, loc):
        out_cps, _, me = copies(ins, outs, send, recv)
        for i in range(n):
            pltpu.make_async_copy(ins[i].at[me], outs[i].at[me], loc.at[i]).start()
        for cp in out_cps:
            cp.start()

    def finish(ins, outs, send, recv, loc):
        out_cps, in_cps, me = copies(ins, outs, send, recv)
        for cp in in_cps:
            cp.wait_recv()
        for cp in out_cps:
            cp.wait_send()
        for i in range(n):
            pltpu.make_async_copy(ins[i].at[me], outs[i].at[me], loc.at[i]).wait()

    return dict(ins=list(pieces), out_shape=[jax.ShapeDtypeStruct(a.shape, a.dtype) for a in pieces],
                scratch=[pltpu.SemaphoreType.DMA((3 * n,)), pltpu.SemaphoreType.DMA((3 * n,)), pltpu.SemaphoreType.DMA((n,))],
                start=start, finish=finish)


def _plugged_call(body, plug, first_last, args, *, name, grid, in_specs, out_specs, out_shape, scratch_shapes=(), aliases=None):
    in_specs, out_specs, out_shape, scratch_shapes = list(in_specs), list(out_specs), list(out_shape), list(scratch_shapes)
    aliases = dict(aliases or {})
    if plug is None:
        return pl.pallas_call(body, name=name, grid=grid, in_specs=in_specs, out_specs=out_specs, out_shape=out_shape,
                              scratch_shapes=scratch_shapes, input_output_aliases=aliases)(*args), []
    n_in, n_out, n_sc = len(in_specs), len(out_specs), len(scratch_shapes)
    p_in, p_out = len(plug["ins"]), len(plug["out_shape"])

    def full(*refs):
        ins, refs = refs[:n_in], refs[n_in:]
        pins, refs = refs[:p_in], refs[p_in:]
        outs, refs = refs[:n_out], refs[n_out:]
        pouts, refs = refs[:p_out], refs[p_out:]
        scr, psems = refs[:n_sc], refs[n_sc:]
        first, last = first_last()

        @pl.when(first)
        def _():
            plug["start"](pins, pouts, *psems)

        body(*ins, *outs, *scr)

        @pl.when(last)
        def _():
            plug["finish"](pins, pouts, *psems)

    res = pl.pallas_call(full, name=name, grid=grid, in_specs=in_specs + [_HBM] * p_in, out_specs=out_specs + [_HBM] * p_out,
                         out_shape=out_shape + plug["out_shape"], scratch_shapes=scratch_shapes + plug["scratch"],
                         input_output_aliases=aliases)(*args, *plug["ins"])
    return res[:n_out], res[n_out:]


def _grid_ends(grid):
    def ends():
        first = last = None
        for ax, n in enumerate(grid):
            i = pl.program_id(ax)
            first = (i == 0) if first is None else first & (i == 0)
            last = (i == n - 1) if last is None else last & (i == n - 1)
        return first, last
    return ends


def _pair_presum(pieces, name):
    n, r, cols = pieces.shape
    by_cols = (cols // 2) % LANES == 0
    half_shape = (n, r, cols // 2) if by_cols else (n, r // 2, cols)

    def body(p_ref, o_ref, mine_ref, land_ref, send_sem, recv_sem, local_sem):
        x, y, c = lax.axis_index("x"), lax.axis_index("y"), lax.axis_index("c")

        def half(which):
            if by_cols:
                return p_ref.at[:, :, pl.ds(pl.multiple_of(which * (cols // 2), LANES), cols // 2)]
            return p_ref.at[:, pl.ds(pl.multiple_of(which * (r // 2), 16), r // 2), :]

        push = pltpu.make_async_remote_copy(src_ref=half(1 - c), dst_ref=land_ref, send_sem=send_sem, recv_sem=recv_sem,
                                            device_id=(x, y, 1 - c), device_id_type=MESH)
        own = pltpu.make_async_copy(half(c), mine_ref, local_sem)
        push.start()
        own.start()
        own.wait()
        push.wait_recv()
        for i in range(n):
            o_ref[i] = (mine_ref[i].astype(F32) + land_ref[i].astype(F32)).astype(BF16)
        push.wait_send()

    return pl.pallas_call(
        body, name=name, out_shape=jax.ShapeDtypeStruct(half_shape, BF16), in_specs=[_HBM],
        out_specs=pl.BlockSpec(memory_space=pltpu.VMEM),
        scratch_shapes=[pltpu.VMEM(half_shape, BF16), pltpu.VMEM(half_shape, BF16), pltpu.SemaphoreType.DMA,
                        pltpu.SemaphoreType.DMA, pltpu.SemaphoreType.DMA],
    )(pieces)


def _sibling_plug(arrs):
    n = len(arrs)

    def copies(ins, outs, send, recv):
        sibling = (lax.axis_index("x"), lax.axis_index("y"), 1 - lax.axis_index("c"))
        return [pltpu.make_async_remote_copy(src_ref=ins[i], dst_ref=outs[i], send_sem=send.at[i], recv_sem=recv.at[i],
                                             device_id=sibling, device_id_type=MESH) for i in range(n)]

    def start(ins, outs, send, recv):
        for cp in copies(ins, outs, send, recv):
            cp.start()

    def finish(ins, outs, send, recv):
        for cp in copies(ins, outs, send, recv):
            cp.wait()

    return dict(ins=list(arrs), out_shape=[jax.ShapeDtypeStruct(a.shape, a.dtype) for a in arrs],
                scratch=[pltpu.SemaphoreType.DMA((n,)), pltpu.SemaphoreType.DMA((n,))], start=start, finish=finish)


_SEGS = ((0, AQ, C_Q), (AQ, AKV, C_K), (AQ + AKV, AKV, C_V), (AQ + 2 * AKV, 3 * DN, C_DQKV), (2304, DN, C_DZ),
         (2816, 2 * DN_HEADS, C_BD), (2824, D_MODEL, C_GA), (3848, D_MODEL, C_GD))
SHARD_ROWS = IN_DIM // N_CHIPS


def _to_padded(w4):
    wt = w4.reshape(IN_DIM, w4.shape[2])
    parts = [wt[o:o + n] for o, n, _ in sorted(_SEGS, key=lambda sg: sg[2])]
    return jnp.concatenate(parts + [jnp.zeros((IN_PAD - IN_DIM, wt.shape[1]), wt.dtype)], axis=0)


def _from_padded(gt):
    return jnp.concatenate([gt[ps:ps + n] for _, n, ps in sorted(_SEGS)], axis=0).reshape(N_CHIPS, SHARD_ROWS, gt.shape[1])


def _lane_vec(a):
    return jnp.zeros((1, LANES), F32).at[0, DN_HEADS:2 * DN_HEADS].set(a)


_ROW_SHARDED = ("w_in", "w_out", "ffn_w_down")
_FFN = ("ffn_w_up", "ffn_w_down")
_LATE_MIXER = ("w_attn_branch", "w_dn_branch", "w_out")
_PRESUM = ("w_in", "ffn_w_up")


def _pieces(k, a):
    if a.ndim == 3:
        return a
    if k in _ROW_SHARDED:
        return a.reshape(N_CHIPS, a.shape[0] // N_CHIPS, a.shape[1]).astype(BF16)
    return jnp.transpose(a.reshape(a.shape[0], N_CHIPS, a.shape[1] // N_CHIPS), (1, 0, 2)).astype(BF16)


def _assemble(k, a):
    if k in _ROW_SHARDED:
        return a.reshape(-1, a.shape[2])
    return jnp.transpose(a, (1, 0, 2)).reshape(a.shape[1], -1)


def _device_step(x2, tgt2, mod, p, bsz, shards=None):
    d = D_MODEL
    on_mesh = shards is not None
    p = dict(p)
    sh1, sc1, g1, sh2, sc2, g2 = [mod[:, i * d:(i + 1) * d].reshape(bsz, 1, d) for i in range(N_MOD)]
    alog_v, dt_v = _lane_vec(p["dn_a_log"]), _lane_vec(p["dn_dt_bias"])
    sinks = p["attn_sinks"].reshape(1, AQ_HEADS)
    if on_mesh:
        u1, got = _pre_fwd(x2, p["norm_mix_pre"], sc1, sh1, "pre1_fwd", plug=_gather_plug(shards["w_in_last"]))
        p["w_in"] = _to_padded(jnp.concatenate(list(shards["w_in_gathered"]) + list(got), axis=1))
    else:
        u1 = _pre_fwd(x2, p["norm_mix_pre"], sc1, sh1, "pre1_fwd")
    if on_mesh:
        proj, got = _mm(u1, p["w_in"], "nt", F32, "mm_proj", _gather_plug(shards["late_mixer"]))
        p.update({k: _assemble(k, a) for k, a in zip(_LATE_MIXER, got)})
    else:
        proj = _mm(u1, p["w_in"], "nt", F32, "mm_proj")
    bias = _bias_build(p["rel_bias"])
    y_attn = _attn_fwd(proj, bias, sinks, bsz)
    qkvn, dn_conv_out = _dn_prep_fwd(proj, p["dn_conv_w"], bsz)
    bg = _bg_fwd(proj, alog_v, dt_v, bsz)
    nc = x2.shape[0] // bsz // DN_CHUNK
    g_rows = jnp.transpose(bg[:, DN_HEADS:2 * DN_HEADS].reshape(bsz, nc, DN_CHUNK, DN_HEADS), (0, 3, 1, 2))
    o, states, got = _dn_chunk_fwd(qkvn, bg, g_rows, bsz, _gather_plug(shards["ffn"][:1]) if on_mesh else None)
    for k, a in zip(_FFN[:1], got):
        p[k] = _assemble(k, a)
    y_dn = _dn_out_fwd(o, proj, p["dn_norm_w"])
    ya = _mm(y_attn, p["w_attn_branch"], "nn", BF16, "mm_ya")
    yd = _mm(y_dn, p["w_dn_branch"], "nn", BF16, "mm_yd")
    merged = _merge_fwd(proj, ya, yd)
    y1 = _mm(merged, p["w_out"], "nn", F32, "mm_y1")
    h1, u2 = _post_pre_fwd(x2, y1, p["norm_mix_post"], g1, p["norm_ffn_pre"], sc2, sh2, "post1_pre2_fwd")
    if on_mesh:
        up, got = _mm(u2, p["ffn_w_up"], "nn", BF16, "mm_up", _gather_plug(shards["ffn"][1:]))
        p["ffn_w_down"] = _assemble("ffn_w_down", got[0])
    else:
        up = _mm(u2, p["ffn_w_up"], "nn", BF16, "mm_up")
    act, conv_g, conv_v = _ffn_act_fwd(up, p["ffn_conv_w"], bsz)
    y2 = _mm(act, p["ffn_w_down"], "nn", F32, "mm_y2")
    dh2, dy2, g_ffn_post, dg2, sq = _post_loss_bwd(h1, y2, p["norm_ffn_post"], g2, tgt2, "post2_loss_bwd")
    g = {}
    g["norm_ffn_post"] = g_ffn_post
    dact = _mm(dy2, p["ffn_w_down"], "nt", BF16, "mm_dact")
    g["ffn_w_down"] = _mm(act, dy2, "tn", BF16, "mm_dwdown")
    dupg, dupv, dcwg, dcwv, got_down = _ffn_act_bwd(
        up, p["ffn_conv_w"], (conv_g, conv_v), dact, bsz,
        _exchange_plug([_pieces("ffn_w_down", g["ffn_w_down"])]) if on_mesh else None)
    g["ffn_conv_w"] = jnp.concatenate([dcwg, dcwv], axis=1)
    g["ffn_w_up"] = jnp.concatenate([_mm(u2, dupg, "tn", BF16, "mm_dwup_gate", split=N_CHIPS // 2),
                                     _mm(u2, dupv, "tn", BF16, "mm_dwup_val", split=N_CHIPS // 2)], axis=0)
    du2 = _mm(dupg, p["ffn_w_up"], "nt", F32, "mm_du2_gate", b_kblock=0)
    du2 = _mm(dupv, p["ffn_w_up"], "nt", BF16, "mm_du2_val", b_kblock=1, add=du2)
    dh1, dy1, g["norm_ffn_pre"], dsc2, dsh2, g["norm_mix_post"], dg1 = _pre_post_bwd(
        h1, p["norm_ffn_pre"], sc2, sh2, du2, dh2, y1, p["norm_mix_post"], g1, "pre2_post1_bwd")
    dmerged = _mm(dy1, p["w_out"], "nt", BF16, "mm_dmerged")
    g["w_out"] = _mm(merged, dy1, "tn", BF16, "mm_dwout")
    dproj = lax.empty((x2.shape[0], IN_PAD), BF16)
    dproj, dya = _branch_bwd(dproj, proj, ya, dmerged, C_GA, "merge_bwd_attn")
    dproj, dyd = _branch_bwd(dproj, proj, yd, dmerged, C_GD, "merge_bwd_dn")
    dy_attn = _mm(dya, p["w_attn_branch"], "nt", BF16, "mm_dyattn")
    g["w_attn_branch"] = _mm(y_attn, dya, "tn", BF16, "mm_dwab", split=N_CHIPS)
    dy_dn = _mm(dyd, p["w_dn_branch"], "nt", BF16, "mm_dydn")
    g["w_dn_branch"] = _mm(y_dn, dyd, "tn", BF16, "mm_dwdb", split=N_CHIPS)
    do, dproj, g["dn_norm_w"] = _dn_out_bwd(dproj, o, proj, p["dn_norm_w"], dy_dn)
    def plug_for(names):
        if not on_mesh:
            return None
        return _exchange_plug([_pair_presum(_pieces(k, g[k]), "presum_" + k) if k in _PRESUM else _pieces(k, g[k]) for k in names])

    early = ("w_out", "w_attn_branch", "w_dn_branch")
    dqkvn, dbg4, got_up = _dn_chunk_bwd(qkvn, bg, g_rows, states, do, bsz, plug_for(_FFN[:1]))
    got_ffn = list(got_up) + list(got_down)
    dproj, g["dn_conv_w"] = _dn_prep_bwd(dproj, proj, p["dn_conv_w"], dn_conv_out, dqkvn, bsz)
    dproj, dk, dv, dbias, g["attn_sinks"], got_early = _attn_bwd(dproj, proj, bias, sinks, dy_attn, bsz, plug_for(early))
    dproj, g["dn_a_log"], g["dn_dt_bias"] = _bg_bwd(dproj, proj, alog_v, dt_v, dbg4, dk, dv, bsz)
    g["rel_bias"] = _bias_grad(dbias)
    g["w_in"] = _from_padded(_mm(dproj, u1, "tn", BF16, "mm_dwin"))
    if on_mesh:
        du1, got_in = _mm(dproj, p["w_in"], "nn", BF16, "mm_du1", plug_for(("w_in",)))
        g.update(zip(_FFN + early + ("w_in",), list(got_ffn) + list(got_early) + list(got_in)))
    else:
        du1 = _mm(dproj, p["w_in"], "nn", BF16, "mm_du1")
    dx, g["norm_mix_pre"], dsc1, dsh1 = _pre_bwd(x2, p["norm_mix_pre"], sc1, sh1, du1, dh1, "pre1_bwd")
    dmod = jnp.concatenate([dsh1, dsc1, dg1, dsh2, dsc2, dg2], axis=-1).reshape(bsz, N_MOD * d)
    return sq, dx, dmod, g


_SMALL = (("norm_mix_pre", D_MODEL), ("norm_mix_post", D_MODEL), ("norm_ffn_pre", D_MODEL), ("norm_ffn_post", D_MODEL),
          ("dn_norm_w", DN_HD), ("dn_a_log", LANES), ("dn_dt_bias", LANES), ("attn_sinks", AQ_HEADS * LANES),
          ("rel_bias", AQ_HEADS * LANES), ("dn_conv_w", DN_CONV * 3 * DN), ("ffn_conv_w", FFN_CONV * 2 * D_FF),
          ("loss_sq", D_MODEL))


def _pack_rows(parts, rows):
    flat = jnp.concatenate([a.reshape(-1) for a in parts])
    return jnp.concatenate([flat, jnp.zeros((rows * LANES - flat.shape[0],), F32)]).reshape(rows, LANES)


_W_NAMES = ("ada_w", "ada_b", "norm_mix_pre", "norm_mix_post", "norm_ffn_pre", "norm_ffn_post", "w_in", "dn_conv_w", "dn_a_log",
            "dn_dt_bias", "dn_norm_w", "attn_sinks", "rel_bias", "w_attn_branch", "w_dn_branch", "w_out", "ffn_w_up", "ffn_conv_w",
            "ffn_w_down")
_BIG = ("w_in", "w_attn_branch", "w_dn_branch", "w_out", "ffn_w_up", "ffn_w_down")


def kernel(x, c, *rest):
    nw = len(_W_NAMES)
    w = dict(zip(_W_NAMES, rest[:nw]))
    loss_target = rest[nw]
    m = dict(zip(_W_NAMES, rest[nw + 1:2 * nw + 1]))
    v = dict(zip(_W_NAMES, rest[2 * nw + 1:3 * nw + 1]))
    ix, iy, ic = lax.axis_index("x"), lax.axis_index("y"), lax.axis_index("c")
    chip, dev = 2 * ix + iy, 4 * ix + 2 * iy + ic
    bsz, s, d = x.shape
    t = bsz * s
    n_dev = 8

    front_rows = 64
    front = _pack_rows([c, w["dn_conv_w"], w["ffn_conv_w"]], front_rows)
    w_in_t = jnp.swapaxes(w["w_in"][0], 0, 1).astype(BF16)
    cut1 = SHARD_ROWS // 48 * 16
    cut2 = 2 * cut1
    front_all, (w_in_lo,) = _allgather8(front, "ag_front", _gather_plug([w_in_t[:cut1]]))
    front_all = front_all.reshape(n_dev, front_rows * LANES)
    n_c, n_dc, n_fc = bsz * d, DN_CONV * 3 * DN // N_CHIPS, FFN_CONV * 2 * D_FF // N_CHIPS
    c_all = front_all[:, :n_c].reshape(n_dev * bsz, d)
    per_chip = front_all[0::2]
    dn_conv_full = jnp.transpose(per_chip[:, n_c:n_c + n_dc].reshape(N_CHIPS, DN_CONV, -1), (1, 0, 2)).reshape(DN_CONV, 3 * DN)
    ffn_conv_full = jnp.transpose(per_chip[:, n_c + n_dc:n_c + n_dc + n_fc].reshape(N_CHIPS, FFN_CONV, -1), (1, 0, 2)).reshape(FFN_CONV, 2 * D_FF)

    mod_cols = N_MOD * d // N_CHIPS
    ada_b_loc = lax.dynamic_slice(w["ada_b"], (0, chip * mod_cols), (1, mod_cols))
    mod_part = _mod_fwd(c_all, w["ada_w"][0], ada_b_loc)
    mod_all, (w_in_mid,) = _allgather8(mod_part, "ag_mod", _gather_plug([w_in_t[cut1:cut2]]))
    mod_all = mod_all.reshape(n_dev, n_dev * bsz, mod_cols)[0::2]
    mod = jnp.transpose(lax.dynamic_slice(mod_all, (0, dev * bsz, 0), (N_CHIPS, bsz, mod_cols)), (1, 0, 2)).reshape(bsz, N_MOD * d)

    p = {}
    shards = {"late_mixer": [w[k][0].astype(BF16) for k in _LATE_MIXER], "ffn": [w[k][0].astype(BF16) for k in _FFN],
              "w_in_gathered": [w_in_lo, w_in_mid], "w_in_last": [w_in_t[cut2:]]}
    for k in ("norm_mix_pre", "norm_mix_post", "norm_ffn_pre", "norm_ffn_post", "dn_norm_w", "attn_sinks"):
        p[k] = w[k]
    p["dn_a_log"], p["dn_dt_bias"], p["rel_bias"] = w["dn_a_log"][0], w["dn_dt_bias"][0], w["rel_bias"]
    p["dn_conv_w"], p["ffn_conv_w"] = dn_conv_full, ffn_conv_full

    sq, dx, dmod, g = _device_step(x.reshape(t, d), loss_target.reshape(t, d), mod, p, bsz, shards)
    g["dn_a_log"], g["dn_dt_bias"], g["loss_sq"] = g["dn_a_log"].reshape(-1), g["dn_dt_bias"].reshape(-1), sq
    small_rows = 336
    small = _pack_rows([dmod] + [g[k] for k, _ in _SMALL], small_rows)
    small_all = _allgather8(small, "ag_small").reshape(n_dev, small_rows, LANES)
    n_dm = bsz * N_MOD * d
    dmod_all = small_all.reshape(n_dev, -1)[:, :n_dm].reshape(n_dev * bsz, N_MOD * d)
    tot = _sum_lead(small_all, "sum_small").reshape(-1)
    gs, off = {}, n_dm
    for k, n in _SMALL:
        gs[k] = tot[off:off + n]
        off += n
    loss = 0.5 * jnp.sum(gs["loss_sq"])
    grad = {}
    grad["ada_w"], grad["ada_b"] = _ada_grad(c_all, lax.dynamic_slice(dmod_all, (0, chip * mod_cols), (n_dev * bsz, mod_cols)), dmod_all)
    for k in ("norm_mix_pre", "norm_mix_post", "norm_ffn_pre", "norm_ffn_post", "dn_norm_w"):
        grad[k] = gs[k]
    grad["dn_a_log"] = gs["dn_a_log"][DN_HEADS:2 * DN_HEADS]
    grad["dn_dt_bias"] = gs["dn_dt_bias"][DN_HEADS:2 * DN_HEADS]
    grad["attn_sinks"] = gs["attn_sinks"].reshape(AQ_HEADS, LANES)[:, 0]
    grad["rel_bias"] = gs["rel_bias"].reshape(AQ_HEADS, LANES)[:, :REL_BUCKETS].T
    grad["dn_conv_w"] = lax.dynamic_slice(gs["dn_conv_w"].reshape(DN_CONV, 3 * DN), (0, chip * (3 * DN // N_CHIPS)), (DN_CONV, 3 * DN // N_CHIPS))
    grad["ffn_conv_w"] = lax.dynamic_slice(gs["ffn_conv_w"].reshape(FFN_CONV, 2 * D_FF), (0, chip * (2 * D_FF // N_CHIPS)), (FFN_CONV, 2 * D_FF // N_CHIPS))

    mine = [_sum_lead(g[k], "sum_" + k, rows_apart=(k == "w_in")) for k in _BIG]
    out = {}
    out["ada_w"], theirs = _adamw(w["ada_w"][0], m["ada_w"][0], v["ada_w"][0], [grad["ada_w"]], "adamw_ada_w", _sibling_plug(mine))
    for k, a, b in zip(_BIG, mine, theirs):
        core = ic.reshape(1).astype(jnp.int32)
        if k == "w_in":
            tr = lambda z: jnp.transpose(z, (2, 0, 1))
            out[k] = [jnp.transpose(r, (1, 2, 0)) for r in _adamw_halves(tr(w[k]), tr(m[k]), tr(v[k]), a, b, core, "adamw_" + k)]
        elif k in _PRESUM:
            out[k] = _adamw_halves(w[k][0], m[k][0], v[k][0], a, b, core, "adamw_" + k)
        else:
            out[k] = _adamw(w[k][0], m[k][0], v[k][0], [a, b], "adamw_" + k)
    small_names = [k for k in _W_NAMES if k not in _BIG and k != "ada_w"]
    res_small = _adamw_small([w[k] for k in small_names], [m[k] for k in small_names], [v[k] for k in small_names],
                             [grad[k].reshape(w[k].shape) for k in small_names], "adamw_small")
    out.update(zip(small_names, res_small))
    for k in _BIG + ("ada_w",):
        out[k] = [r.reshape(w[k].shape) for r in out[k]]
    grads, deltas, new_m, new_v = ([out[k][i] for k in _W_NAMES] for i in range(4))
    return (loss, dx.reshape(bsz, s, d), *grads, *deltas, *new_m, *new_v)
```
